```python
import jax, jax.numpy as jnp
from jax import lax
import numpy as np

D_MODEL = 1024
BATCH = 16
SEQ = 2048
DEPTH = 2

CHUNK = 64
Q_BLOCK = 128
NORM_EPS = 1e-6
MASK_VALUE = -1e30
TINY = 1e-30

MIX_WIDTH = D_MODEL
HGRN_HEADS = 4
HGRN_WIDTH = MIX_WIDTH // 4
HGRN_KEY_DIM = HGRN_WIDTH // HGRN_HEADS
HGRN_VAL_DIM = HGRN_WIDTH // HGRN_HEADS
POOL_WINDOWS = (2, 4, 8, 16)
POOL_GROUPS = len(POOL_WINDOWS)
POOL_WIDTH = MIX_WIDTH // 4
POOL_GROUP_DIM = POOL_WIDTH // POOL_GROUPS
FOX_HEADS = 8
FOX_WIDTH = MIX_WIDTH // 2
FOX_HEAD_DIM = FOX_WIDTH // FOX_HEADS

IN_WIDTHS = (HGRN_WIDTH, HGRN_WIDTH, HGRN_WIDTH, HGRN_WIDTH,
             POOL_WIDTH, POOL_WIDTH,
             FOX_WIDTH, FOX_WIDTH, FOX_WIDTH, FOX_WIDTH, FOX_HEADS)
IN_WIDTH = int(sum(IN_WIDTHS))
IN_SPLIT_POINTS = tuple(int(v) for v in np.cumsum(IN_WIDTHS)[:-1])

kernel_name = "hybrid_hgrn2_pool_fox_stream_encoder"


def _rmsnorm(x, g):
    xf = x.astype(jnp.float32)
    y = xf * lax.rsqrt(jnp.mean(xf * xf, axis=-1, keepdims=True) + NORM_EPS)
    return (y * g.astype(jnp.float32)).astype(x.dtype)


def _hgrn2_chunkwise(q, k, v, log_f):
    B, T, H, dk = q.shape
    dv = v.shape[-1]
    n = T // CHUNK

    def to_chunks(a):
        return a.astype(jnp.float32).reshape(B, n, CHUNK, H, a.shape[-1]).transpose(1, 0, 3, 2, 4)

    qc, kc, vc, gc = to_chunks(q), to_chunks(k), to_chunks(v), to_chunks(log_f)
    pos = jnp.arange(CHUNK)
    causal = (pos[:, None] >= pos[None, :])[:, :, None]
    causal_f = causal.astype(jnp.float32)

    def step(S, inp):
        qi, ki, vi, gi = inp
        b = jnp.cumsum(gi, axis=2)
        diff = b[:, :, :, None, :] - b[:, :, None, :, :]
        decay = jnp.exp(jnp.where(causal, diff, 0.0)) * causal_f
        attn = jnp.einsum('bhtd,bhsd,bhtsd->bhts', qi, ki, decay)
        o = (jnp.einsum('bhts,bhsv->bhtv', attn, vi)
             + jnp.einsum('bhtd,bhdv->bhtv', qi * jnp.exp(b), S))
        b_last = b[:, :, -1, :]
        S = (jnp.exp(b_last)[..., None] * S
             + jnp.einsum('bhsd,bhsv->bhdv', ki * jnp.exp(b_last[:, :, None, :] - b), vi))
        return S, o

    S0 = jnp.zeros((B, H, dk, dv), jnp.float32)
    _, o = lax.scan(step, S0, (qc, kc, vc, gc))
    return o.transpose(1, 0, 3, 2, 4).reshape(B, T, H, dv)


def _multiscale_pool(u):
    B, T, _ = u.shape
    uf = u.astype(jnp.float32).reshape(B, T, POOL_GROUPS, POOL_GROUP_DIM)
    cs = jnp.pad(jnp.cumsum(uf, axis=1), ((0, 0), (1, 0), (0, 0), (0, 0)))
    t = jnp.arange(T, dtype=jnp.float32)
    outs = []
    for g, w in enumerate(POOL_WINDOWS):
        c = cs[:, :, g]
        upper = c[:, 1:]
        lower = jnp.pad(c[:, :T + 1 - w], ((0, 0), (w - 1, 0), (0, 0)))
        count = jnp.minimum(t + 1.0, float(w))[None, :, None]
        outs.append((upper - lower) / count - uf[:, :, g])
    return jnp.stack(outs, axis=2)


def _forgetting_attention(q, k, v, log_f):
    B, T, H, D = q.shape
    c = jnp.cumsum(log_f, axis=1).transpose(0, 2, 1)
    scale = D ** -0.5
    outs = []
    for i in range(T // Q_BLOCK):
        q0, q1 = i * Q_BLOCK, (i + 1) * Q_BLOCK
        s = jnp.einsum('bqhd,bkhd->bhqk', q[:, q0:q1], k[:, :q1]).astype(jnp.float32) * scale
        mask = (q0 + jnp.arange(Q_BLOCK))[:, None] >= jnp.arange(q1)[None, :]
        bias = jnp.where(mask, c[:, :, q0:q1, None] - c[:, :, None, :q1], 0.0)
        p = jax.nn.softmax(jnp.where(mask, s + bias, MASK_VALUE), axis=-1)
        outs.append(jnp.einsum('bhqk,bkhd->bqhd', p.astype(v.dtype), v[:, :q1]))
    return jnp.concatenate(outs, axis=1)


def _fwd_setup_inputs(seed: int = 0) -> dict:
    key = jax.random.key(seed)
    ks = jax.random.split(key, 11)
    f32 = jnp.float32
    x = jax.random.normal(ks[0], (BATCH, SEQ, D_MODEL), f32)
    lower_bounds = jax.random.normal(ks[1], (DEPTH, HGRN_WIDTH), f32)
    pre_norm_g = 1.0 + 0.05 * jax.random.normal(ks[2], (DEPTH, D_MODEL), f32)
    w_in = jax.random.normal(ks[3], (DEPTH, D_MODEL, IN_WIDTH), f32) * D_MODEL ** -0.5
    hgrn_norm_g = 1.0 + 0.05 * jax.random.normal(ks[4], (DEPTH, HGRN_WIDTH), f32)
    fox_f_bias = jax.random.uniform(ks[5], (DEPTH, FOX_HEADS), f32, minval=1.0, maxval=4.0)
    pool_w = jax.random.normal(ks[6], (DEPTH, POOL_GROUPS, POOL_GROUP_DIM, POOL_GROUP_DIM), f32) * POOL_GROUP_DIM ** -0.5
    pool_scale = jax.random.uniform(ks[7], (DEPTH, POOL_WIDTH), f32, minval=0.5, maxval=1.5)
    w_out = jax.random.normal(ks[8], (DEPTH, MIX_WIDTH, D_MODEL), f32) * MIX_WIDTH ** -0.5
    post_norm_g = 1.0 + 0.05 * jax.random.normal(ks[9], (DEPTH, D_MODEL), f32)
    return {"x": x, "lower_bounds": lower_bounds, "pre_norm_g": pre_norm_g, "w_in": w_in,
            "hgrn_norm_g": hgrn_norm_g, "fox_f_bias": fox_f_bias, "pool_w": pool_w,
            "pool_scale": pool_scale, "w_out": w_out, "post_norm_g": post_norm_g}


def _fwd_reference(x, lower_bounds, pre_norm_g, w_in, hgrn_norm_g, fox_f_bias, pool_w, pool_scale, w_out, post_norm_g):
    B, T, _ = x.shape
    p = jax.nn.softmax(lower_bounds.astype(jnp.float32), axis=0)
    lbs = jnp.cumsum(p, axis=0) - p[0]
    for l in range(DEPTH):
        h = _rmsnorm(x, pre_norm_g[l])
        proj = jnp.einsum('btd,de->bte', h, w_in[l])
        q_a, f_a, i_a, g_a, u_b, g_b, q_c, k_c, v_c, g_c, f_c = jnp.split(proj, IN_SPLIT_POINTS, axis=-1)

        lb = lbs[l]
        z = f_a.astype(jnp.float32)
        f_gate = lb + (1.0 - lb) * jax.nn.sigmoid(z)
        log_f_a = jnp.log(jnp.maximum(f_gate, TINY))
        k_a = (1.0 - lb) * jax.nn.sigmoid(-z)
        hshape = (B, T, HGRN_HEADS, HGRN_KEY_DIM)
        o_a = _hgrn2_chunkwise(jax.nn.silu(q_a).reshape(hshape), k_a.reshape(hshape),
                               i_a.reshape(B, T, HGRN_HEADS, HGRN_VAL_DIM), log_f_a.reshape(hshape))
        o_a = _rmsnorm(o_a, hgrn_norm_g[l].reshape(HGRN_HEADS, HGRN_VAL_DIM)).reshape(B, T, HGRN_WIDTH)
        o_a = o_a.astype(x.dtype) * jax.nn.silu(g_a)

        pooled = _multiscale_pool(u_b)
        o_b = jnp.einsum('btgc,gcd->btgd', pooled, pool_w[l].astype(jnp.float32)).reshape(B, T, POOL_WIDTH)
        o_b = (o_b * pool_scale[l].astype(jnp.float32)).astype(x.dtype) * jax.nn.silu(g_b)

        log_f_c = jax.nn.log_sigmoid((f_c + fox_f_bias[l]).astype(jnp.float32))
        fshape = (B, T, FOX_HEADS, FOX_HEAD_DIM)
        o_c = _forgetting_attention(q_c.reshape(fshape), k_c.reshape(fshape), v_c.reshape(fshape), log_f_c)
        o_c = o_c.reshape(B, T, FOX_WIDTH).astype(x.dtype) * jax.nn.silu(g_c)

        mixed = jnp.concatenate([o_a, o_b, o_c], axis=-1)
        y = jnp.einsum('bte,ed->btd', mixed, w_out[l])
        x = x + _rmsnorm(y, post_norm_g[l])
    return x


import jax as _jax
import jax.numpy as _jnp

TWIN_FORMAT = 'train_step'
FWD_PARAMS = ['x', 'lower_bounds', 'pre_norm_g', 'w_in', 'hgrn_norm_g', 'fox_f_bias', 'pool_w', 'pool_scale', 'w_out', 'post_norm_g']
TWIN_WEIGHTS = ['lower_bounds', 'pre_norm_g', 'w_in', 'hgrn_norm_g', 'fox_f_bias', 'pool_w', 'pool_scale', 'w_out', 'post_norm_g']
TWIN_DIFF_INPUT = 'x'
TWIN_INPUTS = ['x', 'lower_bounds', 'pre_norm_g', 'w_in', 'hgrn_norm_g', 'fox_f_bias', 'pool_w', 'pool_scale', 'w_out', 'post_norm_g', 'loss_target', 'm_lower_bounds', 'm_pre_norm_g', 'm_w_in', 'm_hgrn_norm_g', 'm_fox_f_bias', 'm_pool_w', 'm_pool_scale', 'm_w_out', 'm_post_norm_g', 'v_lower_bounds', 'v_pre_norm_g', 'v_w_in', 'v_hgrn_norm_g', 'v_fox_f_bias', 'v_pool_w', 'v_pool_scale', 'v_w_out', 'v_post_norm_g']
TWIN_OUTPUTS = ['loss', 'grad_x', 'grad_lower_bounds', 'grad_pre_norm_g', 'grad_w_in', 'grad_hgrn_norm_g', 'grad_fox_f_bias', 'grad_pool_w', 'grad_pool_scale', 'grad_w_out', 'grad_post_norm_g', 'delta_lower_bounds', 'delta_pre_norm_g', 'delta_w_in', 'delta_hgrn_norm_g', 'delta_fox_f_bias', 'delta_pool_w', 'delta_pool_scale', 'delta_w_out', 'delta_post_norm_g', 'new_m_lower_bounds', 'new_m_pre_norm_g', 'new_m_w_in', 'new_m_hgrn_norm_g', 'new_m_fox_f_bias', 'new_m_pool_w', 'new_m_pool_scale', 'new_m_w_out', 'new_m_post_norm_g', 'new_v_lower_bounds', 'new_v_pre_norm_g', 'new_v_w_in', 'new_v_hgrn_norm_g', 'new_v_fox_f_bias', 'new_v_pool_w', 'new_v_pool_scale', 'new_v_w_out', 'new_v_post_norm_g']
TWIN_LEAF_KINDS = {'loss': 'loss', 'grad_x': 'grad_x', 'grad_lower_bounds': 'grad_w', 'grad_pre_norm_g': 'grad_w', 'grad_w_in': 'grad_w', 'grad_hgrn_norm_g': 'grad_w', 'grad_fox_f_bias': 'grad_w', 'grad_pool_w': 'grad_w', 'grad_pool_scale': 'grad_w', 'grad_w_out': 'grad_w', 'grad_post_norm_g': 'grad_w', 'delta_lower_bounds': 'delta_w', 'delta_pre_norm_g': 'delta_w', 'delta_w_in': 'delta_w', 'delta_hgrn_norm_g': 'delta_w', 'delta_fox_f_bias': 'delta_w', 'delta_pool_w': 'delta_w', 'delta_pool_scale': 'delta_w', 'delta_w_out': 'delta_w', 'delta_post_norm_g': 'delta_w', 'new_m_lower_bounds': 'new_m', 'new_m_pre_norm_g': 'new_m', 'new_m_w_in': 'new_m', 'new_m_hgrn_norm_g': 'new_m', 'new_m_fox_f_bias': 'new_m', 'new_m_pool_w': 'new_m', 'new_m_pool_scale': 'new_m', 'new_m_w_out': 'new_m', 'new_m_post_norm_g': 'new_m', 'new_v_lower_bounds': 'new_v', 'new_v_pre_norm_g': 'new_v', 'new_v_w_in': 'new_v', 'new_v_hgrn_norm_g': 'new_v', 'new_v_fox_f_bias': 'new_v', 'new_v_pool_w': 'new_v', 'new_v_pool_scale': 'new_v', 'new_v_w_out': 'new_v', 'new_v_post_norm_g': 'new_v'}


def _forward(args):
    return _fwd_reference(*[args[k] for k in FWD_PARAMS])


def _output_shape():
    out = _jax.eval_shape(lambda: _forward(_fwd_setup_inputs(0)))
    return out.shape, out.dtype

N_MICROBATCH = 1
ADAM_LR = 0.001
ADAM_B1 = 0.9
ADAM_B2 = 0.999
ADAM_EPS = 1e-08
ADAM_WD = 0.01
ADAM_STEP = 10
PER_EXAMPLE_BATCH_AXIS = {'x': 0, 'loss_target': 0}
SHARED_INPUTS = []
_WEIGHT_DTYPES = {'lower_bounds': _jnp.float32, 'pre_norm_g': _jnp.float32, 'w_in': _jnp.float32, 'hgrn_norm_g': _jnp.float32, 'fox_f_bias': _jnp.float32, 'pool_w': _jnp.float32, 'pool_scale': _jnp.float32, 'w_out': _jnp.float32, 'post_norm_g': _jnp.float32}
MOMENT_SCALE = {'lower_bounds': 5.851648e-02, 'pre_norm_g': 6.189854e-01, 'w_in': 3.307038e-01, 'hgrn_norm_g': 6.931251e-01, 'fox_f_bias': 1.718163e+00, 'pool_w': 5.811076e-01, 'pool_scale': 6.300396e-01, 'w_out': 4.699250e-01, 'post_norm_g': 3.198637e+01}


def _to_microbatches(a, axis):
    t = _jnp.moveaxis(a, axis, 0)
    t = t.reshape((N_MICROBATCH, t.shape[0] // N_MICROBATCH) + t.shape[1:])
    return _jnp.moveaxis(t, 1, axis + 1)


def setup_inputs(seed: int = 0) -> dict:
    inp = _fwd_setup_inputs(seed)
    key = _jax.random.fold_in(_jax.random.key(seed), 7919)
    shape, _ = _output_shape()
    out = dict(inp)
    out["loss_target"] = _jax.random.normal(_jax.random.fold_in(key, 0), shape, _jnp.float32)
    for i, name in enumerate(TWIN_WEIGHTS):
        w = inp[name].astype(_jnp.float32)
        if MOMENT_SCALE is None:
            s = _jnp.sqrt(_jnp.mean(_jnp.square(w)) + 1e-30)
        else:
            s = MOMENT_SCALE[name]
        km, kv = _jax.random.split(_jax.random.fold_in(key, i + 1))
        out[name] = w
        out["m_" + name] = s * _jax.random.normal(km, w.shape, _jnp.float32)
        out["v_" + name] = (s * s) * _jax.random.uniform(kv, w.shape, _jnp.float32, 0.5, 1.5)
    if N_MICROBATCH > 1:
        for name, axis in PER_EXAMPLE_BATCH_AXIS.items():
            out[name] = _to_microbatches(out[name], axis)
    return {'x': out['x'], 'lower_bounds': out['lower_bounds'], 'pre_norm_g': out['pre_norm_g'], 'w_in': out['w_in'], 'hgrn_norm_g': out['hgrn_norm_g'], 'fox_f_bias': out['fox_f_bias'], 'pool_w': out['pool_w'], 'pool_scale': out['pool_scale'], 'w_out': out['w_out'], 'post_norm_g': out['post_norm_g'], 'loss_target': out['loss_target'], 'm_lower_bounds': out['m_lower_bounds'], 'm_pre_norm_g': out['m_pre_norm_g'], 'm_w_in': out['m_w_in'], 'm_hgrn_norm_g': out['m_hgrn_norm_g'], 'm_fox_f_bias': out['m_fox_f_bias'], 'm_pool_w': out['m_pool_w'], 'm_pool_scale': out['m_pool_scale'], 'm_w_out': out['m_w_out'], 'm_post_norm_g': out['m_post_norm_g'], 'v_lower_bounds': out['v_lower_bounds'], 'v_pre_norm_g': out['v_pre_norm_g'], 'v_w_in': out['v_w_in'], 'v_hgrn_norm_g': out['v_hgrn_norm_g'], 'v_fox_f_bias': out['v_fox_f_bias'], 'v_pool_w': out['v_pool_w'], 'v_pool_scale': out['v_pool_scale'], 'v_w_out': out['v_w_out'], 'v_post_norm_g': out['v_post_norm_g']}


def _loss(weights, diff, rest, loss_target):
    with _jax.named_scope("forward"):
        args = {**rest, TWIN_DIFF_INPUT: diff, **{k: w.astype(_WEIGHT_DTYPES[k]) for k, w in weights.items()}}
        y = _forward(args)
    with _jax.named_scope("loss_head"):
        err = _jnp.square(y.astype(_jnp.float32) - loss_target)
        return 0.5 * _jnp.sum(_jnp.mean(err, axis=-1)) if err.ndim else 0.5 * err


def _adamw(w, g, m, v):
    m = ADAM_B1 * m + (1.0 - ADAM_B1) * g
    v = ADAM_B2 * v + (1.0 - ADAM_B2) * _jnp.square(g)
    m_hat = m / (1.0 - ADAM_B1 ** ADAM_STEP)
    v_hat = v / (1.0 - ADAM_B2 ** ADAM_STEP)
    delta = -ADAM_LR * (m_hat / (_jnp.sqrt(v_hat) + ADAM_EPS) + ADAM_WD * w)
    return delta, m, v


def reference(x, lower_bounds, pre_norm_g, w_in, hgrn_norm_g, fox_f_bias, pool_w, pool_scale, w_out, post_norm_g, loss_target, m_lower_bounds, m_pre_norm_g, m_w_in, m_hgrn_norm_g, m_fox_f_bias, m_pool_w, m_pool_scale, m_w_out, m_post_norm_g, v_lower_bounds, v_pre_norm_g, v_w_in, v_hgrn_norm_g, v_fox_f_bias, v_pool_w, v_pool_scale, v_w_out, v_post_norm_g):
    given = dict(x=x, lower_bounds=lower_bounds, pre_norm_g=pre_norm_g, w_in=w_in, hgrn_norm_g=hgrn_norm_g, fox_f_bias=fox_f_bias, pool_w=pool_w, pool_scale=pool_scale, w_out=w_out, post_norm_g=post_norm_g, loss_target=loss_target, m_lower_bounds=m_lower_bounds, m_pre_norm_g=m_pre_norm_g, m_w_in=m_w_in, m_hgrn_norm_g=m_hgrn_norm_g, m_fox_f_bias=m_fox_f_bias, m_pool_w=m_pool_w, m_pool_scale=m_pool_scale, m_w_out=m_w_out, m_post_norm_g=m_post_norm_g, v_lower_bounds=v_lower_bounds, v_pre_norm_g=v_pre_norm_g, v_w_in=v_w_in, v_hgrn_norm_g=v_hgrn_norm_g, v_fox_f_bias=v_fox_f_bias, v_pool_w=v_pool_w, v_pool_scale=v_pool_scale, v_w_out=v_w_out, v_post_norm_g=v_post_norm_g)
    weights = {n: given[n] for n in TWIN_WEIGHTS}
    shared = {n: given[n] for n in SHARED_INPUTS}
    per_example = {n: given[n] for n in ['x']}
    grad_fn = _jax.value_and_grad(_loss, argnums=(0, 1))

    def one_microbatch(ex, loss_target):
        ex = dict(ex)
        diff = ex.pop(TWIN_DIFF_INPUT)
        return grad_fn(weights, diff, {**shared, **ex}, loss_target)

    if N_MICROBATCH == 1:
        loss, (grad_w, grad_x) = one_microbatch(per_example, given["loss_target"])
    else:
        def body(carry, xs):
            loss_sum, grad_sum = carry
            l_k, (gw_k, gx_k) = one_microbatch(xs[0], xs[1])
            with _jax.named_scope("update"):
                return (loss_sum + l_k, _jax.tree.map(_jnp.add, grad_sum, gw_k)), gx_k

        init = (_jnp.zeros((), _jnp.float32), _jax.tree.map(_jnp.zeros_like, weights))
        (loss, grad_w), grad_x = _jax.lax.scan(body, init, (per_example, given["loss_target"]))
    with _jax.named_scope("update"):
        delta_w, new_m, new_v = {}, {}, {}
        for n in TWIN_WEIGHTS:
            delta_w[n], new_m[n], new_v[n] = _adamw(weights[n], grad_w[n], given["m_" + n], given["v_" + n])
    return (loss, grad_x, *[grad_w[n] for n in TWIN_WEIGHTS], *[delta_w[n] for n in TWIN_WEIGHTS],
            *[new_m[n] for n in TWIN_WEIGHTS], *[new_v[n] for n in TWIN_WEIGHTS])
```

```python
import functools

import jax
import jax.numpy as jnp
from jax import lax
from jax.experimental import pallas as pl
from jax.experimental.pallas import tpu as pltpu

F32, BF16 = jnp.float32, jnp.bfloat16
HI = lax.Precision.HIGHEST
MESH = pl.DeviceIdType.MESH
AXES = ("x", "y", "c")
N_DEV = 8

NORM_EPS = 1e-6
MASK_VALUE = -1e30
TINY = 1e-30
CHUNK = 64
HGRN_W, POOL_W, FOX_W = 256, 256, 512
HEAD = 64
FOX_HEADS = 8
POOL_WINDOWS = (2, 4, 8, 16)
POOL_HALO = 16
MAIN_W = 3584
FC_PAD = 128
C_QA, C_FA, C_IA, C_GA, C_UB, C_GB, C_QC, C_KC, C_VC, C_GC = 0, 256, 512, 768, 1024, 1280, 1536, 2048, 2560, 3072
FOX_SCALE = HEAD ** -0.5

ADAM_LR, ADAM_B1, ADAM_B2, ADAM_EPS, ADAM_WD, ADAM_STEP = 0.001, 0.9, 0.999, 1e-08, 0.01, 10

VMEM_LIMIT = 56 * 1024 * 1024


def _pc(fn, name, **kw):
    return pl.pallas_call(fn, name=name, **kw)


def _params(**kw):
    return pltpu.CompilerParams(vmem_limit_bytes=VMEM_LIMIT, **kw)


def _dot(a, b):
    return jnp.dot(a, b, preferred_element_type=F32)


def _dot_nt(a, b):
    return lax.dot_general(a, b, (((1,), (1,)), ((), ())), preferred_element_type=F32)


def _dot_tn(a, b):
    return lax.dot_general(a, b, (((0,), (0,)), ((), ())), preferred_element_type=F32)


def _dot_hi(a, b):
    return jnp.dot(a, b, precision=HI, preferred_element_type=F32)


def _sigmoid(x):
    return 1.0 / (1.0 + jnp.exp(-x))


def _block_ones(n, dtype):
    r = lax.broadcasted_iota(jnp.int32, (n, n), 0) // HEAD
    c = lax.broadcasted_iota(jnp.int32, (n, n), 1) // HEAD
    return (r == c).astype(dtype)


def _sds(shape, dtype):
    return jax.ShapeDtypeStruct(shape, dtype)


def _in_proj_fwd(x, g_pre, w_main, w_fc, name):
    n, d = x.shape
    tm = min(256, n)

    def body(x_ref, g_ref, w_ref, wf_ref, proj_ref, fc_ref, h_ref):
        xv = x_ref[...]
        r = lax.rsqrt(jnp.mean(xv * xv, axis=-1, keepdims=True) + NORM_EPS)
        hb = (xv * r * g_ref[...]).astype(BF16)
        h_ref[...] = hb
        for j in range(0, MAIN_W, 512):
            proj_ref[:, j:j + 512] = _dot(hb, w_ref[:, j:j + 512])
        fc_ref[...] = _dot(hb, wf_ref[...])

    return _pc(
        body, name, grid=(n // tm,),
        in_specs=[pl.BlockSpec((tm, d), lambda i: (i, 0)), pl.BlockSpec((1, d), lambda i: (0, 0)),
                  pl.BlockSpec((d, MAIN_W), lambda i: (0, 0)), pl.BlockSpec((d, FC_PAD), lambda i: (0, 0))],
        out_specs=[pl.BlockSpec((tm, MAIN_W), lambda i: (i, 0)), pl.BlockSpec((tm, FC_PAD), lambda i: (i, 0)),
                   pl.BlockSpec((tm, d), lambda i: (i, 0))],
        out_shape=[_sds((n, MAIN_W), F32), _sds((n, FC_PAD), F32), _sds((n, d), BF16)],
        compiler_params=_params(),
    )(x, g_pre, w_main, w_fc)


def _fox_decay_fwd(fc, bias, nb, name):
    n = fc.shape[0]
    t = n // nb
    tt = min(256, t)
    nt = t // tt

    def body(fc_ref, b_ref, c_ref, carry):
        i = pl.program_id(1)

        @pl.when(i == 0)
        def _():
            carry[...] = jnp.zeros_like(carry)

        xv = fc_ref[...] + b_ref[...]
        lf = jnp.minimum(xv, 0.0) - jnp.log(1.0 + jnp.exp(-jnp.abs(xv)))
        r = lax.broadcasted_iota(jnp.int32, (tt, tt), 0)
        cc = lax.broadcasted_iota(jnp.int32, (tt, tt), 1)
        cs = _dot_hi((r >= cc).astype(F32), lf) + carry[...]
        c_ref[...] = cs
        carry[...] = cs[tt - 1:tt, :]

    return _pc(
        body, name, grid=(nb, nt),
        in_specs=[pl.BlockSpec((tt, FC_PAD), lambda b, i: (b * nt + i, 0)), pl.BlockSpec((1, FC_PAD), lambda b, i: (0, 0))],
        out_specs=pl.BlockSpec((tt, FC_PAD), lambda b, i: (b * nt + i, 0)),
        out_shape=_sds((n, FC_PAD), F32),
        scratch_shapes=[pltpu.VMEM((1, FC_PAD), F32)],
        compiler_params=_params(),
    )(fc, bias)


def _hgrn_gates(q, z, lb):
    sig = _sigmoid(z)
    sn = _sigmoid(-z)
    f = lb + (1.0 - lb) * sig
    g = jnp.log(jnp.maximum(f, TINY))
    k = (1.0 - lb) * sn
    sq = _sigmoid(q)
    return sig, sn, f, g, k, sq


def _hgrn_fwd(proj, lb, ones_b, nb, name):
    n = proj.shape[0]
    t = n // nb
    nc = t // CHUNK
    w = HGRN_W

    def body(q_ref, z_ref, v_ref, lb_ref, ones_ref, o_ref, s0_ref, st_s, b_s, qs_s, k_s, o_s):
        c = pl.program_id(1)

        @pl.when(c == 0)
        def _():
            st_s[...] = jnp.zeros_like(st_s)

        lbv = lb_ref[...]
        q = q_ref[...]
        _, _, _, g, k, sq = _hgrn_gates(q, z_ref[...], lbv)
        qs = q * sq
        r = lax.broadcasted_iota(jnp.int32, (CHUNK, CHUNK), 0)
        cc = lax.broadcasted_iota(jnp.int32, (CHUNK, CHUNK), 1)
        b = _dot_hi((r >= cc).astype(F32), g)
        bl = b[CHUNK - 1:CHUNK, :]
        s0 = st_s[...]
        s0_ref[...] = s0
        o_ref[...] = _dot_nt((qs * jnp.exp(b)).astype(BF16), s0.astype(BF16))
        ktil = k * jnp.exp(bl - b)
        upd = _dot_tn(v_ref[...].astype(BF16), ktil.astype(BF16))
        st_s[...] = s0 * jnp.exp(bl) + upd * _block_ones(w, F32)
        b_s[...] = b
        qs_s[...] = qs
        k_s[...] = k
        srow = lax.broadcasted_iota(jnp.int32, (CHUNK, w), 0)

        def step(i, carry):
            bt = b_s[pl.ds(i, 1), :]
            qt = qs_s[pl.ds(i, 1), :]
            wgt = jnp.where(srow <= i, (qt * k_s[...]) * jnp.exp(bt - b_s[...]), 0.0)
            aexp = _dot(wgt.astype(BF16), ones_ref[...])
            o_s[pl.ds(i, 1), :] = jnp.sum(aexp * v_ref[...], axis=0, keepdims=True)
            return carry

        lax.fori_loop(0, CHUNK, step, 0)
        o_ref[...] += o_s[...]

    def col(j):
        return pl.BlockSpec((CHUNK, w), lambda b, c: (b * nc + c, j))

    return _pc(
        body, name, grid=(nb, nc),
        in_specs=[col(C_QA // w), col(C_FA // w), col(C_IA // w), pl.BlockSpec((1, w), lambda b, c: (0, 0)),
                  pl.BlockSpec((w, w), lambda b, c: (0, 0))],
        out_specs=[pl.BlockSpec((CHUNK, w), lambda b, c: (b * nc + c, 0)),
                   pl.BlockSpec((None, w, w), lambda b, c: (b * nc + c, 0, 0))],
        out_shape=[_sds((n, w), F32), _sds((nb * nc, w, w), F32)],
        scratch_shapes=[pltpu.VMEM((w, w), F32)] + [pltpu.VMEM((CHUNK, w), F32)] * 4,
        compiler_params=_params(),
    )(proj, proj, proj, lb, ones_b)


def _pool_lane_windows():
    lane = lax.broadcasted_iota(jnp.int32, (1, POOL_W), 1) // HEAD
    wl = jnp.zeros((1, POOL_W), F32)
    for gi, win in enumerate(POOL_WINDOWS):
        wl = jnp.where(lane == gi, float(win), wl)
    return lane, wl


def _pool_select(lane, parts):
    out = parts[-1]
    for gi in range(len(parts) - 2, -1, -1):
        out = jnp.where(lane == gi, parts[gi], out)
    return out


def _pool_mix(u, halo, t0, tt):
    lane, wl = _pool_lane_windows()
    ext = jnp.concatenate([halo, u], axis=0)
    sums, cur, shift = [], ext, 1
    for _ in POOL_WINDOWS:
        cur = cur + pltpu.roll(cur, shift, axis=0)
        sums.append(cur[POOL_HALO:, :])
        shift *= 2
    tpos = (t0 + lax.broadcasted_iota(jnp.int32, (tt, POOL_W), 0)).astype(F32)
    cnt = jnp.minimum(tpos + 1.0, wl)
    return _pool_select(lane, sums) / cnt - u, cnt


def _pool_specs(tt, nt, nhb):
    cu, cg = C_UB // POOL_W, C_GB // POOL_W
    return [pl.BlockSpec((tt, POOL_W), lambda b, i: (b * nt + i, cu)),
            pl.BlockSpec((tt, POOL_W), lambda b, i: (b * nt + i, cg)),
            pl.BlockSpec((POOL_HALO, POOL_W), lambda b, i: (jnp.maximum((b * nt + i) * nhb - 1, 0), cu))]


def _pool_fwd(proj, wbd, scale, nb, name):
    n = proj.shape[0]
    t = n // nb
    tt = min(256, t)
    nt = t // tt
    nhb = tt // POOL_HALO

    def body(u_ref, g_ref, h_ref, w_ref, s_ref, o_ref):
        i = pl.program_id(1)
        halo = jnp.where(i == 0, 0.0, h_ref[...])
        pooled, _ = _pool_mix(u_ref[...], halo, i * tt, tt)
        gv = g_ref[...]
        o_ref[...] = _dot(pooled.astype(BF16), w_ref[...]) * s_ref[...] * (gv * _sigmoid(gv))

    return _pc(
        body, name, grid=(nb, nt),
        in_specs=_pool_specs(tt, nt, nhb) + [pl.BlockSpec((POOL_W, POOL_W), lambda b, i: (0, 0)),
                                             pl.BlockSpec((1, POOL_W), lambda b, i: (0, 0))],
        out_specs=pl.BlockSpec((tt, POOL_W), lambda b, i: (b * nt + i, 0)),
        out_shape=_sds((n, POOL_W), F32),
        compiler_params=_params(),
    )(proj, proj, proj, wbd, scale)


def _lane_lo():
    return lax.broadcasted_iota(jnp.int32, (1, 2 * HEAD), 1) < HEAD


def _put_col(tile, hh, colv):
    lane = lax.broadcasted_iota(jnp.int32, tile.shape, 1)
    return jnp.where(lane == hh, colv, tile)


def _fox_fwd(proj, c_col, c_row, nb, name):
    n = proj.shape[0]
    t = n // nb
    tb = min(256, t)
    nq = t // tb
    pw = 2 * HEAD

    def body(q_ref, k_ref, v_ref, cc_ref, cr_ref, o_ref, lse_ref, m_s, l_s, acc_s):
        qi, kj = pl.program_id(1), pl.program_id(2)

        @pl.when(kj == 0)
        def _():
            m_s[...] = jnp.full_like(m_s, -jnp.inf)
            l_s[...] = jnp.zeros_like(l_s)
            acc_s[...] = jnp.zeros_like(acc_s)

        @pl.when(kj <= qi)
        def _():
            rows = qi * tb + lax.broadcasted_iota(jnp.int32, (tb, tb), 0)
            cols = kj * tb + lax.broadcasted_iota(jnp.int32, (tb, tb), 1)
            causal = rows >= cols
            lo = _lane_lo()
            for p in range(FOX_HEADS // 2):
                sl = slice(p * pw, (p + 1) * pw)
                qp = q_ref[:, sl] * FOX_SCALE
                kp = k_ref[:, sl].astype(BF16)
                vp = v_ref[:, sl].astype(BF16)
                for h in range(2):
                    hh = 2 * p + h
                    lm = lo if h == 0 else jnp.logical_not(lo)
                    s = _dot_nt(jnp.where(lm, qp, 0.0).astype(BF16), kp)
                    s = s + (cc_ref[:, hh:hh + 1] - cr_ref[hh:hh + 1, :])
                    s = jnp.where(causal, s, MASK_VALUE)
                    m_prev = m_s[hh]
                    m_new = jnp.maximum(m_prev, jnp.max(s, axis=1, keepdims=True))
                    alpha = jnp.exp(m_prev - m_new)
                    pe = jnp.exp(s - m_new)
                    l_s[hh] = alpha * l_s[hh] + jnp.sum(pe, axis=1, keepdims=True)
                    m_s[hh] = m_new
                    pv = _dot(pe.astype(BF16), vp)
                    acc = acc_s[:, sl]
                    acc_s[:, sl] = jnp.where(lm, alpha * acc + pv, acc)

        @pl.when(kj == qi)
        def _():
            lo = _lane_lo()
            lse = jnp.zeros((tb, FC_PAD), F32)
            for p in range(FOX_HEADS // 2):
                sl = slice(p * pw, (p + 1) * pw)
                linv = jnp.where(lo, 1.0 / l_s[2 * p], 1.0 / l_s[2 * p + 1])
                o_ref[:, sl] = acc_s[:, sl] * linv
                for h in range(2):
                    hh = 2 * p + h
                    lse = _put_col(lse, hh, m_s[hh] + jnp.log(l_s[hh]))
            lse_ref[...] = lse

    def qspec(j):
        return pl.BlockSpec((tb, FOX_W), lambda b, qi, kj: (b * nq + qi, j))

    def kspec(j):
        return pl.BlockSpec((tb, FOX_W), lambda b, qi, kj: (b * nq + jnp.minimum(kj, qi), j))

    return _pc(
        body, name, grid=(nb, nq, nq),
        in_specs=[qspec(C_QC // FOX_W), kspec(C_KC // FOX_W), kspec(C_VC // FOX_W),
                  pl.BlockSpec((tb, FC_PAD), lambda b, qi, kj: (b * nq + qi, 0)),
                  pl.BlockSpec((None, FOX_HEADS, tb), lambda b, qi, kj: (b, 0, jnp.minimum(kj, qi)))],
        out_specs=[pl.BlockSpec((tb, FOX_W), lambda b, qi, kj: (b * nq + qi, 0)),
                   pl.BlockSpec((tb, FC_PAD), lambda b, qi, kj: (b * nq + qi, 0))],
        out_shape=[_sds((n, FOX_W), F32), _sds((n, FC_PAD), F32)],
        scratch_shapes=[pltpu.VMEM((FOX_HEADS, tb, 1), F32), pltpu.VMEM((FOX_HEADS, tb, 1), F32),
                        pltpu.VMEM((tb, FOX_W), F32)],
        compiler_params=_params(),
    )(proj, proj, proj, c_col, c_row)


def _head_mean(x, ones_f):
    return _dot_hi(x, ones_f) * (1.0 / HEAD)


def _merge_fwd(x, proj, o_h, o_b, o_c, gh, w_out, g_post, name):
    n, d = x.shape
    tm = min(256, n)

    def body(x_ref, ga_ref, gc_ref, oh_ref, ob_ref, oc_ref, gh_ref, w_ref, gp_ref, xo_ref, mix_ref, y_ref):
        oh = oh_ref[...]
        ones_f = _block_ones(HGRN_W, F32)
        na = oh * lax.rsqrt(_head_mean(oh * oh, ones_f) + NORM_EPS) * gh_ref[...]
        ga, gc = ga_ref[...], gc_ref[...]
        mixed = jnp.concatenate([na * (ga * _sigmoid(ga)), ob_ref[...], oc_ref[...] * (gc * _sigmoid(gc))], axis=1)
        mb = mixed.astype(BF16)
        mix_ref[...] = mb
        y = _dot(mb, w_ref[...])
        y_ref[...] = y
        xo_ref[...] = x_ref[...] + y * lax.rsqrt(jnp.mean(y * y, axis=-1, keepdims=True) + NORM_EPS) * gp_ref[...]

    def row(wd, j=0):
        return pl.BlockSpec((tm, wd), lambda i: (i, j))

    def full(a, b):
        return pl.BlockSpec((a, b), lambda i: (0, 0))

    return _pc(
        body, name, grid=(n // tm,),
        in_specs=[row(d), row(HGRN_W, C_GA // HGRN_W), row(FOX_W, C_GC // FOX_W), row(HGRN_W), row(POOL_W), row(FOX_W),
                  full(1, HGRN_W), full(d, d), full(1, d)],
        out_specs=[row(d), row(d), row(d)],
        out_shape=[_sds((n, d), F32), _sds((n, d), BF16), _sds((n, d), F32)],
        compiler_params=_params(),
    )(x, proj, proj, o_h, o_b, o_c, gh, w_out, g_post)


def _loss_fwd_bwd(x, target, name):
    n, d = x.shape
    tm = min(512, n)

    def body(x_ref, t_ref, dx_ref, sq_ref):
        @pl.when(pl.program_id(0) == 0)
        def _():
            sq_ref[...] = jnp.zeros_like(sq_ref)

        e = x_ref[...] - t_ref[...]
        dx_ref[...] = e * (1.0 / d)
        sq_ref[...] += jnp.sum(e * e, axis=0, keepdims=True)

    return _pc(
        body, name, grid=(n // tm,),
        in_specs=[pl.BlockSpec((tm, d), lambda i: (i, 0))] * 2,
        out_specs=[pl.BlockSpec((tm, d), lambda i: (i, 0)), pl.BlockSpec((1, d), lambda i: (0, 0))],
        out_shape=[_sds((n, d), F32), _sds((1, d), F32)],
        compiler_params=_params(),
    )(x, target)


def _rms_bwd(dy_scaled, xhat, r):
    return r * (dy_scaled - xhat * jnp.mean(dy_scaled * xhat, axis=-1, keepdims=True))


def _merge_bwd(dxo, y, g_post, w_out_t, name):
    n, d = y.shape
    tm = min(256, n)

    def body(dx_ref, y_ref, gp_ref, wt_ref, dy_ref, dm_ref, dgp_ref):
        @pl.when(pl.program_id(0) == 0)
        def _():
            dgp_ref[...] = jnp.zeros_like(dgp_ref)

        yv, dxv = y_ref[...], dx_ref[...]
        r = lax.rsqrt(jnp.mean(yv * yv, axis=-1, keepdims=True) + NORM_EPS)
        yh = yv * r
        dgp_ref[...] += jnp.sum(dxv * yh, axis=0, keepdims=True)
        dyb = _rms_bwd(dxv * gp_ref[...], yh, r).astype(BF16)
        dy_ref[...] = dyb
        dm_ref[...] = _dot(dyb, wt_ref[...])

    row = pl.BlockSpec((tm, d), lambda i: (i, 0))
    return _pc(
        body, name, grid=(n // tm,),
        in_specs=[row, row, pl.BlockSpec((1, d), lambda i: (0, 0)), pl.BlockSpec((d, d), lambda i: (0, 0))],
        out_specs=[row, row, pl.BlockSpec((1, d), lambda i: (0, 0))],
        out_shape=[_sds((n, d), BF16), _sds((n, d), F32), _sds((1, d), F32)],
        compiler_params=_params(),
    )(dxo, y, g_post, w_out_t)


def _grad_matmul(a, b, name):
    n, ka = a.shape
    kb = b.shape[1]
    ta = min(256, ka)
    tb = min(512, kb)

    def body(a_ref, b_ref, o_ref):
        o_ref[...] = _dot_tn(a_ref[...], b_ref[...])

    return _pc(
        body, name, grid=(ka // ta, kb // tb),
        in_specs=[pl.BlockSpec((n, ta), lambda i, j: (0, i)), pl.BlockSpec((n, tb), lambda i, j: (0, j))],
        out_specs=pl.BlockSpec((ta, tb), lambda i, j: (i, j)),
        out_shape=_sds((ka, kb), F32),
        compiler_params=_params(),
    )(a, b)


def _hgrn_bwd(dmix, proj, o_h, s0, gh, lb, ones_b, nb, name):
    n = proj.shape[0]
    t = n // nb
    nc = t // CHUNK
    w = HGRN_W

    def body(dm_ref, q_ref, z_ref, v_ref, ga_ref, oh_ref, s0_ref, gh_ref, lb_ref, ones_ref,
             dp_ref, dgh_ref, dlb_ref, ds_s, b_s, qs_s, k_s, do_s, dq_s, dk_s, dv_s):
        first = jnp.logical_and(pl.program_id(0) == 0, pl.program_id(1) == 0)

        @pl.when(first)
        def _():
            dgh_ref[...] = jnp.zeros_like(dgh_ref)
            dlb_ref[...] = jnp.zeros_like(dlb_ref)

        @pl.when(pl.program_id(1) == 0)
        def _():
            ds_s[...] = jnp.zeros_like(ds_s)

        ones_f = ones_ref[...].astype(F32)
        lbv, ghv = lb_ref[...], gh_ref[...]
        oh, ga, dm = oh_ref[...], ga_ref[...], dm_ref[...]
        rn = lax.rsqrt(_head_mean(oh * oh, ones_f) + NORM_EPS)
        nh = oh * rn
        sga = _sigmoid(ga)
        silu_ga = ga * sga
        dga = dm * nh * ghv * (sga * (1.0 + ga * (1.0 - sga)))
        dn = dm * silu_ga
        dgh_ref[...] += jnp.sum(dn * nh, axis=0, keepdims=True)
        dn = dn * ghv
        do = rn * (dn - nh * _head_mean(dn * nh, ones_f))
        q, z, v = q_ref[...], z_ref[...], v_ref[...]
        sig, sn, f, g, k, sq = _hgrn_gates(q, z, lbv)
        qs = q * sq
        r = lax.broadcasted_iota(jnp.int32, (CHUNK, CHUNK), 0)
        cc = lax.broadcasted_iota(jnp.int32, (CHUNK, CHUNK), 1)
        b = _dot_hi((r >= cc).astype(F32), g)
        bl = b[CHUNK - 1:CHUNK, :]
        eb, ebl, ekt = jnp.exp(b), jnp.exp(bl), jnp.exp(bl - b)
        qe, ktil = qs * eb, k * ekt
        s0v, ds1 = s0_ref[...], ds_s[...]
        ds1b, dob = ds1.astype(BF16), do.astype(BF16)
        dv_s[...] = _dot_nt(ktil.astype(BF16), ds1b)
        dqe = _dot(dob, s0v.astype(BF16))
        dktil = _dot(v.astype(BF16), ds1b)
        dbl = jnp.sum(dktil * ktil, axis=0, keepdims=True) + ebl * jnp.sum(s0v * ds1, axis=0, keepdims=True)
        ds_s[...] = ds1 * ebl + _dot_tn(dob, qe.astype(BF16)) * _block_ones(w, F32)
        b_s[...] = b
        qs_s[...] = qs
        k_s[...] = k
        do_s[...] = do
        dk_s[...] = jnp.zeros_like(dk_s)
        srow = lax.broadcasted_iota(jnp.int32, (CHUNK, w), 0)

        def step(i, carry):
            bt, qt, dot_ = b_s[pl.ds(i, 1), :], qs_s[pl.ds(i, 1), :], do_s[pl.ds(i, 1), :]
            m = srow <= i
            kk = k_s[...]
            e = jnp.exp(bt - b_s[...])
            vv = v_ref[...]
            aexp = _dot(jnp.where(m, (qt * kk) * e, 0.0).astype(BF16), ones_ref[...])
            gexp = _dot((dot_ * vv).astype(BF16), ones_ref[...])
            ge = jnp.where(m, gexp * e, 0.0)
            dq_s[pl.ds(i, 1), :] = jnp.sum(ge * kk, axis=0, keepdims=True)
            dk_s[...] += ge * qt
            dv_s[...] += aexp * dot_
            return carry

        lax.fori_loop(0, CHUNK, step, 0)
        dqs = dq_s[...] + dqe * eb
        dk = dk_s[...] + dktil * ekt
        db = qs * dqs - k * dk
        dg = _dot_hi((r <= cc).astype(F32), db) + dbl
        dfz = jnp.where(f > TINY, dg / jnp.maximum(f, TINY), 0.0)
        dz = (dfz - dk) * (1.0 - lbv) * sig * sn
        dlb_ref[...] += jnp.sum(dfz * (1.0 - sig) - dk * sn, axis=0, keepdims=True)
        dq = dqs * (sq * (1.0 + q * (1.0 - sq)))
        dp_ref[...] = jnp.concatenate([dq, dz, dv_s[...], dga], axis=1).astype(BF16)

    def rv(b, c):
        return b * nc + (nc - 1 - c)

    def col(j):
        return pl.BlockSpec((CHUNK, w), lambda b, c: (rv(b, c), j))

    def full(a, bb):
        return pl.BlockSpec((a, bb), lambda b, c: (0, 0))

    return _pc(
        body, name, grid=(nb, nc),
        in_specs=[col(0), col(C_QA // w), col(C_FA // w), col(C_IA // w), col(C_GA // w), col(0),
                  pl.BlockSpec((None, w, w), lambda b, c: (rv(b, c), 0, 0)), full(1, w), full(1, w), full(w, w)],
        out_specs=[pl.BlockSpec((CHUNK, 4 * w), lambda b, c: (rv(b, c), 0)), full(1, w), full(1, w)],
        out_shape=[_sds((n, 4 * w), BF16), _sds((1, w), F32), _sds((1, w), F32)],
        scratch_shapes=[pltpu.VMEM((w, w), F32)] + [pltpu.VMEM((CHUNK, w), F32)] * 7,
        compiler_params=_params(),
    )(dmix, proj, proj, proj, proj, o_h, s0, gh, lb, ones_b)


def _pool_bwd(dmix, proj, wbd, wbd_t, scale, nb, name):
    n = proj.shape[0]
    t = n // nb
    tt = min(256, t)
    nt = t // tt
    nhb = tt // POOL_HALO
    cu, cg, cm = C_UB // POOL_W, C_GB // POOL_W, HGRN_W // POOL_W

    def body(u_ref, g_ref, h_ref, dm_ref, gn_ref, dmn_ref, w_ref, wt_ref, s_ref, dp_ref, dw_ref, ds_ref):
        i = pl.program_id(1)
        first = jnp.logical_and(pl.program_id(0) == 0, i == 0)

        @pl.when(first)
        def _():
            dw_ref[...] = jnp.zeros_like(dw_ref)
            ds_ref[...] = jnp.zeros_like(ds_ref)

        sc = s_ref[...]
        halo = jnp.where(i == 0, 0.0, h_ref[...])
        pooled, cnt = _pool_mix(u_ref[...], halo, i * tt, tt)
        pb = pooled.astype(BF16)
        pre = _dot(pb, w_ref[...])
        gv, dm = g_ref[...], dm_ref[...]
        sg = _sigmoid(gv)
        silu = gv * sg
        dgb = dm * pre * sc * (sg * (1.0 + gv * (1.0 - sg)))
        ds_ref[...] += jnp.sum(dm * pre * silu, axis=0, keepdims=True)
        dpre = (dm * sc * silu).astype(BF16)
        dw_ref[...] += _dot_tn(pb, dpre)
        dpool = _dot(dpre, wt_ref[...])
        gn = gn_ref[...]
        dpre_n = (dmn_ref[...] * sc * (gn * _sigmoid(gn))).astype(BF16)
        dpool_n = jnp.where(i == nt - 1, 0.0, _dot(dpre_n, wt_ref[...]))
        lane, wl = _pool_lane_windows()
        tpos_n = ((i + 1) * tt + lax.broadcasted_iota(jnp.int32, (POOL_HALO, POOL_W), 0)).astype(F32)
        ext = jnp.concatenate([dpool / cnt, dpool_n / jnp.minimum(tpos_n + 1.0, wl)], axis=0)
        rows = tt + POOL_HALO
        sums, cur, shift = [], ext, 1
        for _ in POOL_WINDOWS:
            cur = cur + pltpu.roll(cur, rows - shift, axis=0)
            sums.append(cur[:tt, :])
            shift *= 2
        du = _pool_select(lane, sums) - dpool
        dp_ref[...] = jnp.concatenate([du, dgb], axis=1).astype(BF16)

    def nxt(b, i):
        return jnp.minimum((b * nt + i + 1) * nhb, n // POOL_HALO - 1)

    return _pc(
        body, name, grid=(nb, nt),
        in_specs=_pool_specs(tt, nt, nhb) + [
            pl.BlockSpec((tt, POOL_W), lambda b, i: (b * nt + i, cm)),
            pl.BlockSpec((POOL_HALO, POOL_W), lambda b, i: (nxt(b, i), cg)),
            pl.BlockSpec((POOL_HALO, POOL_W), lambda b, i: (nxt(b, i), cm)),
            pl.BlockSpec((POOL_W, POOL_W), lambda b, i: (0, 0)), pl.BlockSpec((POOL_W, POOL_W), lambda b, i: (0, 0)),
            pl.BlockSpec((1, POOL_W), lambda b, i: (0, 0))],
        out_specs=[pl.BlockSpec((tt, 2 * POOL_W), lambda b, i: (b * nt + i, 0)),
                   pl.BlockSpec((POOL_W, POOL_W), lambda b, i: (0, 0)), pl.BlockSpec((1, POOL_W), lambda b, i: (0, 0))],
        out_shape=[_sds((n, 2 * POOL_W), BF16), _sds((POOL_W, POOL_W), F32), _sds((1, POOL_W), F32)],
        compiler_params=_params(),
    )(proj, proj, proj, dmix, proj, dmix, wbd, wbd_t, scale)


def _fox_gate_bwd(dmix, proj, o_c, name):
    n = proj.shape[0]
    tm = min(256, n)

    def body(dm_ref, gc_ref, oc_ref, da_ref, dg_ref, dl_ref):
        dm, gc, oc = dm_ref[...], gc_ref[...], oc_ref[...]
        sg = _sigmoid(gc)
        da = dm * (gc * sg)
        da_ref[...] = da.astype(BF16)
        dg_ref[...] = (dm * oc * (sg * (1.0 + gc * (1.0 - sg)))).astype(BF16)
        r = lax.broadcasted_iota(jnp.int32, (FOX_W, FC_PAD), 0) // HEAD
        c = lax.broadcasted_iota(jnp.int32, (FOX_W, FC_PAD), 1)
        dl_ref[...] = _dot_hi(da * oc, (r == c).astype(F32))

    def row(wd, j=0):
        return pl.BlockSpec((tm, wd), lambda i: (i, j))

    return _pc(
        body, name, grid=(n // tm,),
        in_specs=[row(FOX_W, (HGRN_W + POOL_W) // FOX_W), row(FOX_W, C_GC // FOX_W), row(FOX_W)],
        out_specs=[row(FOX_W), row(FOX_W), row(FC_PAD)],
        out_shape=[_sds((n, FOX_W), BF16), _sds((n, FOX_W), BF16), _sds((n, FC_PAD), F32)],
        compiler_params=_params(),
    )(dmix, proj, o_c)


def _fox_bwd_q(proj, da, c_col, c_row, lse, delta, nb, name):
    n = proj.shape[0]
    t = n // nb
    tb = min(256, t)
    nq = t // tb
    pw = 2 * HEAD

    def body(q_ref, k_ref, v_ref, da_ref, cc_ref, cr_ref, lse_ref, dl_ref, dq_ref, dc_ref, acc_s, dc_s):
        qi, kj = pl.program_id(1), pl.program_id(2)

        @pl.when(kj == 0)
        def _():
            acc_s[...] = jnp.zeros_like(acc_s)
            dc_s[...] = jnp.zeros_like(dc_s)

        @pl.when(kj <= qi)
        def _():
            rows = qi * tb + lax.broadcasted_iota(jnp.int32, (tb, tb), 0)
            cols = kj * tb + lax.broadcasted_iota(jnp.int32, (tb, tb), 1)
            causal = rows >= cols
            lo = _lane_lo()
            dc = dc_s[...]
            for p in range(FOX_HEADS // 2):
                sl = slice(p * pw, (p + 1) * pw)
                qp = q_ref[:, sl] * FOX_SCALE
                kf = k_ref[:, sl]
                kp = kf.astype(BF16)
                vp = v_ref[:, sl].astype(BF16)
                dap = da_ref[:, sl]
                acc = acc_s[:, sl]
                for h in range(2):
                    hh = 2 * p + h
                    lm = lo if h == 0 else jnp.logical_not(lo)
                    s = _dot_nt(jnp.where(lm, qp, 0.0).astype(BF16), kp)
                    s = s + (cc_ref[:, hh:hh + 1] - cr_ref[hh:hh + 1, :])
                    pe = jnp.where(causal, jnp.exp(s - lse_ref[:, hh:hh + 1]), 0.0)
                    dp = _dot_nt(jnp.where(lm, dap, jnp.zeros_like(dap)), vp)
                    ds = pe * (dp - dl_ref[:, hh:hh + 1])
                    acc = acc + _dot(ds.astype(BF16), jnp.where(lm, kf, 0.0).astype(BF16))
                    dc = dc + _put_col(jnp.zeros_like(dc), hh, jnp.sum(ds, axis=1, keepdims=True))
                acc_s[:, sl] = acc
            dc_s[...] = dc

        @pl.when(kj == qi)
        def _():
            dq_ref[...] = (acc_s[...] * FOX_SCALE).astype(BF16)
            dc_ref[...] = dc_s[...]

    def qspec(wd, j=0):
        return pl.BlockSpec((tb, wd), lambda b, qi, kj: (b * nq + qi, j))

    def kspec(j):
        return pl.BlockSpec((tb, FOX_W), lambda b, qi, kj: (b * nq + jnp.minimum(kj, qi), j))

    return _pc(
        body, name, grid=(nb, nq, nq),
        in_specs=[qspec(FOX_W, C_QC // FOX_W), kspec(C_KC // FOX_W), kspec(C_VC // FOX_W), qspec(FOX_W), qspec(FC_PAD),
                  pl.BlockSpec((None, FOX_HEADS, tb), lambda b, qi, kj: (b, 0, jnp.minimum(kj, qi))),
                  qspec(FC_PAD), qspec(FC_PAD)],
        out_specs=[qspec(FOX_W), qspec(FC_PAD)],
        out_shape=[_sds((n, FOX_W), BF16), _sds((n, FC_PAD), F32)],
        scratch_shapes=[pltpu.VMEM((tb, FOX_W), F32), pltpu.VMEM((tb, FC_PAD), F32)],
        compiler_params=_params(),
    )(proj, proj, proj, da, c_col, c_row, lse, delta)


def _fox_bwd_kv(proj, da, c_col, c_row, lse_row, delta_row, nb, name):
    n = proj.shape[0]
    t = n // nb
    tb = min(256, t)
    nq = t // tb
    pw = 2 * HEAD

    def body(q_ref, k_ref, v_ref, da_ref, cc_ref, cr_ref, lse_ref, dl_ref, dk_ref, dv_ref, dc_ref, dk_s, dv_s, dc_s):
        kj, qi = pl.program_id(1), pl.program_id(2)

        @pl.when(qi == 0)
        def _():
            dk_s[...] = jnp.zeros_like(dk_s)
            dv_s[...] = jnp.zeros_like(dv_s)
            dc_s[...] = jnp.zeros_like(dc_s)

        @pl.when(qi >= kj)
        def _():
            keys = kj * tb + lax.broadcasted_iota(jnp.int32, (tb, tb), 0)
            qrys = qi * tb + lax.broadcasted_iota(jnp.int32, (tb, tb), 1)
            causal = qrys >= keys
            lo = _lane_lo()
            dc = dc_s[...]
            for p in range(FOX_HEADS // 2):
                sl = slice(p * pw, (p + 1) * pw)
                qp = q_ref[:, sl] * FOX_SCALE
                kp = k_ref[:, sl].astype(BF16)
                vp = v_ref[:, sl].astype(BF16)
                dap = da_ref[:, sl]
                dk, dv = dk_s[:, sl], dv_s[:, sl]
                for h in range(2):
                    hh = 2 * p + h
                    lm = lo if h == 0 else jnp.logical_not(lo)
                    qm = jnp.where(lm, qp, 0.0).astype(BF16)
                    dam = jnp.where(lm, dap, jnp.zeros_like(dap))
                    s = _dot_nt(kp, qm) + (cr_ref[hh:hh + 1, :] - cc_ref[:, hh:hh + 1])
                    pe = jnp.where(causal, jnp.exp(s - lse_ref[hh:hh + 1, :]), 0.0)
                    dp = _dot_nt(vp, dam)
                    ds = pe * (dp - dl_ref[hh:hh + 1, :])
                    dv = dv + _dot(pe.astype(BF16), dam)
                    dk = dk + _dot(ds.astype(BF16), qm)
                    dc = dc - _put_col(jnp.zeros_like(dc), hh, jnp.sum(ds, axis=1, keepdims=True))
                dk_s[:, sl] = dk
                dv_s[:, sl] = dv
            dc_s[...] = dc

        @pl.when(qi == nq - 1)
        def _():
            dk_ref[...] = dk_s[...].astype(BF16)
            dv_ref[...] = dv_s[...].astype(BF16)
            dc_ref[...] = dc_s[...]

    def kspec(wd, j=0):
        return pl.BlockSpec((tb, wd), lambda b, kj, qi: (b * nq + kj, j))

    def qspec(wd, j=0):
        return pl.BlockSpec((tb, wd), lambda b, kj, qi: (b * nq + jnp.maximum(qi, kj), j))

    def qrow():
        return pl.BlockSpec((None, FOX_HEADS, tb), lambda b, kj, qi: (b, 0, jnp.maximum(qi, kj)))

    return _pc(
        body, name, grid=(nb, nq, nq),
        in_specs=[qspec(FOX_W, C_QC // FOX_W), kspec(FOX_W, C_KC // FOX_W), kspec(FOX_W, C_VC // FOX_W), qspec(FOX_W),
                  kspec(FC_PAD), qrow(), qrow(), qrow()],
        out_specs=[kspec(FOX_W), kspec(FOX_W), kspec(FC_PAD)],
        out_shape=[_sds((n, FOX_W), BF16), _sds((n, FOX_W), BF16), _sds((n, FC_PAD), F32)],
        scratch_shapes=[pltpu.VMEM((tb, FOX_W), F32), pltpu.VMEM((tb, FOX_W), F32), pltpu.VMEM((tb, FC_PAD), F32)],
        compiler_params=_params(),
    )(proj, proj, proj, da, c_col, c_row, lse_row, delta_row)


def _fox_decay_bwd(dc_q, dc_k, fc, bias, nb, name):
    n = fc.shape[0]
    t = n // nb
    tt = min(256, t)
    nt = t // tt

    def body(dcq_ref, dck_ref, fc_ref, b_ref, dfc_ref, db_ref, carry):
        i = pl.program_id(1)
        first = jnp.logical_and(pl.program_id(0) == 0, i == 0)

        @pl.when(first)
        def _():
            db_ref[...] = jnp.zeros_like(db_ref)

        @pl.when(i == 0)
        def _():
            carry[...] = jnp.zeros_like(carry)

        r = lax.broadcasted_iota(jnp.int32, (tt, tt), 0)
        cc = lax.broadcasted_iota(jnp.int32, (tt, tt), 1)
        dlf = _dot_hi((r <= cc).astype(F32), dcq_ref[...] + dck_ref[...]) + carry[...]
        carry[...] = dlf[0:1, :]
        dfc = dlf * _sigmoid(-(fc_ref[...] + b_ref[...]))
        dfc_ref[...] = dfc.astype(BF16)
        db_ref[...] += jnp.sum(dfc, axis=0, keepdims=True)

    def row():
        return pl.BlockSpec((tt, FC_PAD), lambda b, i: (b * nt + (nt - 1 - i), 0))

    return _pc(
        body, name, grid=(nb, nt),
        in_specs=[row(), row(), row(), pl.BlockSpec((1, FC_PAD), lambda b, i: (0, 0))],
        out_specs=[row(), pl.BlockSpec((1, FC_PAD), lambda b, i: (0, 0))],
        out_shape=[_sds((n, FC_PAD), BF16), _sds((1, FC_PAD), F32)],
        scratch_shapes=[pltpu.VMEM((1, FC_PAD), F32)],
        compiler_params=_params(),
    )(dc_q, dc_k, fc, bias)


def _in_proj_bwd(pieces, w_main_t, w_fc_t, x, g_pre, dxo, name):
    n, d = x.shape
    tm = min(256, n)
    widths = [p.shape[1] for p, _ in pieces]
    offs = [o for _, o in pieces]
    np_ = len(pieces)

    def body(*refs):
        p_refs = refs[:np_]
        wt_ref, wf_ref, x_ref, g_ref, dxo_ref, dx_ref, dg_ref = refs[np_:]

        @pl.when(pl.program_id(0) == 0)
        def _():
            dg_ref[...] = jnp.zeros_like(dg_ref)

        dh = _dot(p_refs[-1][...], wf_ref[...])
        for pr, wd, off in zip(p_refs[:-1], widths[:-1], offs[:-1]):
            for j in range(0, wd, 512):
                jw = min(512, wd - j)
                dh = dh + _dot(pr[:, j:j + jw], wt_ref[off + j:off + j + jw, :])
        xv = x_ref[...]
        r = lax.rsqrt(jnp.mean(xv * xv, axis=-1, keepdims=True) + NORM_EPS)
        xh = xv * r
        dg_ref[...] += jnp.sum(dh * xh, axis=0, keepdims=True)
        dx_ref[...] = dxo_ref[...] + _rms_bwd(dh * g_ref[...], xh, r)

    row = pl.BlockSpec((tm, d), lambda i: (i, 0))
    return _pc(
        body, name, grid=(n // tm,),
        in_specs=[pl.BlockSpec((tm, wd), lambda i: (i, 0)) for wd in widths] + [
            pl.BlockSpec((MAIN_W, d), lambda i: (0, 0)), pl.BlockSpec((FC_PAD, d), lambda i: (0, 0)),
            row, pl.BlockSpec((1, d), lambda i: (0, 0)), row],
        out_specs=[row, pl.BlockSpec((1, d), lambda i: (0, 0))],
        out_shape=[_sds((n, d), F32), _sds((1, d), F32)],
        compiler_params=_params(),
    )(*[p for p, _ in pieces], w_main_t, w_fc_t, x, g_pre, dxo)


def _lower_bound_table(lower_bounds, name):
    depth, w = lower_bounds.shape

    def body(lb_ref, o_ref):
        v = lb_ref[...]
        e = jnp.exp(v - jnp.max(v, axis=0, keepdims=True))
        p = e / jnp.sum(e, axis=0, keepdims=True)
        acc = jnp.zeros((1, w), F32)
        for l in range(depth):
            acc = acc + p[l:l + 1, :]
            o_ref[l:l + 1, :] = acc - p[0:1, :]

    return _pc(body, name, out_shape=_sds((depth, w), F32))(lower_bounds)


def _lower_bound_bwd(lower_bounds, dlbs, name):
    depth, w = lower_bounds.shape

    def body(lb_ref, d_ref, o_ref):
        v, dl = lb_ref[...], d_ref[...]
        e = jnp.exp(v - jnp.max(v, axis=0, keepdims=True))
        p = e / jnp.sum(e, axis=0, keepdims=True)
        tot = jnp.sum(dl, axis=0, keepdims=True)
        rows, tail = [], tot
        for l in range(depth):
            rows.append(tail - tot if l == 0 else tail)
            tail = tail - dl[l:l + 1, :]
        dp = jnp.concatenate(rows, axis=0)
        o_ref[...] = p * (dp - jnp.sum(p * dp, axis=0, keepdims=True))

    return _pc(body, name, out_shape=_sds((depth, w), F32))(lower_bounds, dlbs)


def _place():
    x, y, c = lax.axis_index("x"), lax.axis_index("y"), lax.axis_index("c")
    return x, y, c


def _gather_weights(w_in_b, w_out_b):
    arrays = (w_in_b, w_out_b)
    na = len(arrays)

    def body(*refs):
        ins, outs = refs[:na], refs[na:2 * na]
        send_sems, recv_sems, local_sems = refs[2 * na:]
        x, y, c = _place()
        me, sibling = (x, y, c), (x, y, 1 - c)
        chips = [(1 - x, y), (x, 1 - y), (1 - x, 1 - y)]

        def slot(a, px, py, pc):
            return outs[a].at[4 * px + 2 * py + pc]

        def copy(a, k, block, to, own=False):
            return pltpu.make_async_remote_copy(
                src_ref=ins[a] if own else slot(a, *block), dst_ref=slot(a, *block),
                send_sem=send_sems.at[a * 7 + k], recv_sem=recv_sems.at[a * 7 + k],
                device_id=to, device_id_type=MESH)

        mine = [pltpu.make_async_copy(ins[a], slot(a, *me), local_sems.at[a]) for a in range(na)]
        for cp in mine:
            cp.start()
        first = []
        for a in range(na):
            first.append(copy(a, 0, me, sibling, own=True))
            first += [copy(a, 1 + j, me, (*chip, c), own=True) for j, chip in enumerate(chips)]
        for cp in first:
            cp.start()
        passed = []
        for j, chip in enumerate(chips):
            for a in range(na):
                copy(a, 1 + j, (*chip, c), me).wait_recv()
                fw = copy(a, 4 + j, (*chip, c), sibling)
                fw.start()
                passed.append(fw)
        for a in range(na):
            copy(a, 0, sibling, me).wait_recv()
            for j, chip in enumerate(chips):
                copy(a, 4 + j, (*chip, 1 - c), me).wait_recv()
        for cp in first + passed:
            cp.wait_send()
        for cp in mine:
            cp.wait()

    any_spec = pl.BlockSpec(memory_space=pl.ANY)
    return _pc(
        body, "gather_weights",
        in_specs=[any_spec] * na, out_specs=[any_spec] * na,
        out_shape=[_sds((N_DEV,) + a.shape, a.dtype) for a in arrays],
        scratch_shapes=[pltpu.SemaphoreType.DMA((7 * na,)), pltpu.SemaphoreType.DMA((7 * na,)),
                        pltpu.SemaphoreType.DMA((na,))],
    )(*arrays)


def _exchange_grads(blocks_in, blocks_out, small):
    arrays = (blocks_in, blocks_out)
    na = len(arrays) + 1

    def body(*refs):
        ins, outs = refs[:na], refs[na:2 * na]
        send_sems, recv_sems, local_sems = refs[2 * na:]
        x, y, c = _place()
        me = 4 * x + 2 * y + c

        def src(a, j):
            return ins[a] if a == na - 1 else ins[a].at[j]

        local = [pltpu.make_async_copy(src(a, me), outs[a].at[me], local_sems.at[a]) for a in range(na)]
        for cp in local:
            cp.start()
        sends, recvs = [], []
        for k in range(1, N_DEV):
            px = 1 - x if k & 4 else x
            py = 1 - y if k & 2 else y
            pc = 1 - c if k & 1 else c
            peer = 4 * px + 2 * py + pc
            for a in range(na):
                s = (k - 1) * na + a
                sends.append(pltpu.make_async_remote_copy(
                    src_ref=src(a, peer), dst_ref=outs[a].at[me], send_sem=send_sems.at[s], recv_sem=recv_sems.at[s],
                    device_id=(px, py, pc), device_id_type=MESH))
                recvs.append(pltpu.make_async_remote_copy(
                    src_ref=src(a, peer), dst_ref=outs[a].at[peer], send_sem=send_sems.at[s], recv_sem=recv_sems.at[s],
                    device_id=(px, py, pc), device_id_type=MESH))
        for cp in sends:
            cp.start()
        for cp in recvs:
            cp.wait_recv()
        for cp in sends:
            cp.wait_send()
        for cp in local:
            cp.wait()

    any_spec = pl.BlockSpec(memory_space=pl.ANY)
    ns = (N_DEV - 1) * na
    return _pc(
        body, "exchange_grads",
        in_specs=[any_spec] * na, out_specs=[any_spec] * na,
        out_shape=[_sds(a.shape, a.dtype) for a in arrays] + [_sds((N_DEV,) + small.shape, small.dtype)],
        scratch_shapes=[pltpu.SemaphoreType.DMA((ns,)), pltpu.SemaphoreType.DMA((ns,)), pltpu.SemaphoreType.DMA((na,))],
    )(blocks_in, blocks_out, small)


def _sum_adamw(parts, w, m, v, name):
    r, c = w.shape
    tr = 256 if r % 256 == 0 else r

    def body(p_ref, w_ref, m_ref, v_ref, g_ref, d_ref, mo_ref, vo_ref):
        g = p_ref[0]
        for j in range(1, N_DEV):
            g = g + p_ref[j]
        wv = w_ref[...]
        mn = ADAM_B1 * m_ref[...] + (1.0 - ADAM_B1) * g
        vn = ADAM_B2 * v_ref[...] + (1.0 - ADAM_B2) * (g * g)
        m_hat = mn / (1.0 - ADAM_B1 ** ADAM_STEP)
        v_hat = vn / (1.0 - ADAM_B2 ** ADAM_STEP)
        g_ref[...] = g
        d_ref[...] = -ADAM_LR * (m_hat / (jnp.sqrt(v_hat) + ADAM_EPS) + ADAM_WD * wv)
        mo_ref[...] = mn
        vo_ref[...] = vn

    row = pl.BlockSpec((tr, c), lambda i: (i, 0))
    return _pc(
        body, name, grid=(r // tr,),
        in_specs=[pl.BlockSpec((N_DEV, tr, c), lambda i: (0, i, 0)), row, row, row],
        out_specs=[row] * 4,
        out_shape=[_sds((r, c), F32)] * 4,
        compiler_params=_params(),
    )(parts, w, m, v)


SMALL = ("lower_bounds", "pre_norm_g", "hgrn_norm_g", "fox_f_bias", "pool_w", "pool_scale", "post_norm_g")
SMALL_LANES = 128


def _pack_small(tree):
    flat = jnp.concatenate([tree[k].reshape(-1) for k in SMALL])
    rows = -(-flat.shape[0] // (8 * SMALL_LANES)) * 8
    return jnp.pad(flat, (0, rows * SMALL_LANES - flat.shape[0])).reshape(rows, SMALL_LANES)


def _unpack_small(packed, like):
    flat, out, off = packed.reshape(-1), {}, 0
    for k in SMALL:
        size = like[k].size
        out[k] = flat[off:off + size].reshape(like[k].shape)
        off += size
    return out


def _block_diag(pw):
    g = pw.shape[0]
    eye = jnp.eye(g, dtype=pw.dtype)
    return (eye[:, None, :, None] * pw[:, :, None, :]).reshape(g * HEAD, g * HEAD)


def _local_step(x, target, lbs, weights, layer_w):
    nb, t, d = x.shape
    n = nb * t
    depth = len(layer_w)
    ones_b = _block_ones(HGRN_W, BF16)
    xs = [x.reshape(n, d)]
    saved = []
    for l in range(depth):
        w_main, w_fc, _, _, w_out, _ = layer_w[l]
        g_pre = weights["pre_norm_g"][l:l + 1]
        bias = jnp.pad(weights["fox_f_bias"][l:l + 1], ((0, 0), (0, FC_PAD - FOX_HEADS)))
        wbd = _block_diag(weights["pool_w"][l]).astype(BF16)
        proj, fc, h = _in_proj_fwd(xs[l], g_pre, w_main, w_fc, f"in_proj_fwd_{l}")
        c_col = _fox_decay_fwd(fc, bias, nb, f"fox_decay_fwd_{l}")
        c_row = c_col.reshape(nb, t, FC_PAD)[:, :, :FOX_HEADS].transpose(0, 2, 1)
        o_h, s0 = _hgrn_fwd(proj, lbs[l:l + 1], ones_b, nb, f"hgrn_fwd_{l}")
        o_b = _pool_fwd(proj, wbd, weights["pool_scale"][l:l + 1], nb, f"pool_fwd_{l}")
        o_c, lse = _fox_fwd(proj, c_col, c_row, nb, f"fox_fwd_{l}")
        x_next, mixed, y = _merge_fwd(xs[l], proj, o_h, o_b, o_c, weights["hgrn_norm_g"][l:l + 1], w_out,
                                      weights["post_norm_g"][l:l + 1], f"merge_fwd_{l}")
        xs.append(x_next)
        saved.append((proj, fc, h, c_col, c_row, o_h, s0, o_c, lse, mixed, y, bias, wbd))

    dx, sq = _loss_fwd_bwd(xs[depth], target.reshape(n, d), "loss")
    loss = 0.5 * jnp.sum(sq) / d

    grads = {k: [None] * depth for k in ("pre_norm_g", "w_in", "hgrn_norm_g", "fox_f_bias", "pool_w", "pool_scale",
                                         "w_out", "post_norm_g", "lbs")}
    for l in reversed(range(depth)):
        proj, fc, h, c_col, c_row, o_h, s0, o_c, lse, mixed, y, bias, wbd = saved[l]
        _, _, w_main_t, w_fc_t, _, w_out_t = layer_w[l]
        dy, dmix, dgp = _merge_bwd(dx, y, weights["post_norm_g"][l:l + 1], w_out_t, f"merge_bwd_{l}")
        grads["post_norm_g"][l] = dgp[0]
        grads["w_out"][l] = _grad_matmul(mixed, dy, f"w_out_grad_{l}")
        d_a, dgh, dlb = _hgrn_bwd(dmix, proj, o_h, s0, weights["hgrn_norm_g"][l:l + 1], lbs[l:l + 1], ones_b, nb,
                                  f"hgrn_bwd_{l}")
        grads["hgrn_norm_g"][l] = dgh[0]
        grads["lbs"][l] = dlb[0]
        d_b, dwbd, dps = _pool_bwd(dmix, proj, wbd, wbd.T, weights["pool_scale"][l:l + 1], nb, f"pool_bwd_{l}")
        grads["pool_w"][l] = jnp.stack([dwbd[g * HEAD:(g + 1) * HEAD, g * HEAD:(g + 1) * HEAD]
                                        for g in range(len(POOL_WINDOWS))])
        grads["pool_scale"][l] = dps[0]
        da, d_gc, delta = _fox_gate_bwd(dmix, proj, o_c, f"fox_gate_bwd_{l}")

        def rows_of(a):
            return a.reshape(nb, t, FC_PAD)[:, :, :FOX_HEADS].transpose(0, 2, 1)

        d_qc, dc_q = _fox_bwd_q(proj, da, c_col, c_row, lse, delta, nb, f"fox_bwd_q_{l}")
        d_kc, d_vc, dc_k = _fox_bwd_kv(proj, da, c_col, c_row, rows_of(lse), rows_of(delta), nb, f"fox_bwd_kv_{l}")
        d_fc, dbias = _fox_decay_bwd(dc_q, dc_k, fc, bias, nb, f"fox_decay_bwd_{l}")
        grads["fox_f_bias"][l] = dbias[0, :FOX_HEADS]
        pieces = [(d_a, C_QA), (d_b, C_UB), (d_qc, C_QC), (d_kc, C_KC), (d_vc, C_VC), (d_gc, C_GC), (d_fc, None)]
        gw = [_grad_matmul(h, p, f"w_in_grad_{l}_{i}") for i, (p, _) in enumerate(pieces)]
        grads["w_in"][l] = jnp.concatenate(gw[:-1] + [gw[-1][:, :FOX_HEADS]], axis=1)
        dx, dgpre = _in_proj_bwd(pieces, w_main_t, w_fc_t, xs[l], weights["pre_norm_g"][l:l + 1], dx, f"in_proj_bwd_{l}")
        grads["pre_norm_g"][l] = dgpre[0]

    out = {k: jnp.stack(v) for k, v in grads.items()}
    out["lower_bounds"] = _lower_bound_bwd(weights["lower_bounds"], out.pop("lbs"), "lower_bound_bwd")
    return loss, dx.reshape(nb, t, d), out


def kernel(x, lower_bounds, pre_norm_g, w_in, hgrn_norm_g, fox_f_bias, pool_w, pool_scale, w_out, post_norm_g, loss_target, m_lower_bounds, m_pre_norm_g, m_w_in, m_hgrn_norm_g, m_fox_f_bias, m_pool_w, m_pool_scale, m_w_out, m_post_norm_g, v_lower_bounds, v_pre_norm_g, v_w_in, v_hgrn_norm_g, v_fox_f_bias, v_pool_w, v_pool_scale, v_w_out, v_post_norm_g):
    weights = dict(lower_bounds=lower_bounds, pre_norm_g=pre_norm_g, hgrn_norm_g=hgrn_norm_g, fox_f_bias=fox_f_bias,
                   pool_w=pool_w, pool_scale=pool_scale, post_norm_g=post_norm_g)
    mom_m = dict(lower_bounds=m_lower_bounds, pre_norm_g=m_pre_norm_g, hgrn_norm_g=m_hgrn_norm_g, fox_f_bias=m_fox_f_bias,
                 pool_w=m_pool_w, pool_scale=m_pool_scale, post_norm_g=m_post_norm_g)
    mom_v = dict(lower_bounds=v_lower_bounds, pre_norm_g=v_pre_norm_g, hgrn_norm_g=v_hgrn_norm_g, fox_f_bias=v_fox_f_bias,
                 pool_w=v_pool_w, pool_scale=v_pool_scale, post_norm_g=v_post_norm_g)
    depth, d, shard_in = w_in.shape
    shard_out = w_out.shape[1]
    in_w = N_DEV * shard_in

    g_in, g_out = _gather_weights(w_in.astype(BF16), w_out.astype(BF16))
    layer_w = []
    for l in range(depth):
        full_in = g_in[:, l].transpose(1, 0, 2).reshape(d, in_w)
        w_main = full_in[:, :MAIN_W]
        w_fc = jnp.pad(full_in[:, MAIN_W:], ((0, 0), (0, FC_PAD - (in_w - MAIN_W))))
        full_out = g_out[:, l].reshape(N_DEV * shard_out, d)
        layer_w.append((w_main, w_fc, w_main.T, w_fc.T, full_out, full_out.T))

    lbs = _lower_bound_table(lower_bounds, "lower_bound_table")
    loss, dx, grads = _local_step(x, loss_target, lbs, weights, layer_w)
    loss = lax.psum(loss, AXES)

    blocks_in = grads["w_in"].reshape(depth, d, N_DEV, shard_in).transpose(2, 0, 1, 3)
    blocks_out = grads["w_out"].reshape(depth, N_DEV, shard_out, d).transpose(1, 0, 2, 3)
    r_in, r_out, r_small = _exchange_grads(blocks_in, blocks_out, _pack_small(grads))

    res_in = _sum_adamw(r_in.reshape(N_DEV, depth * d, shard_in), w_in.reshape(depth * d, shard_in),
                        m_w_in.reshape(depth * d, shard_in), v_w_in.reshape(depth * d, shard_in), "adamw_w_in")
    res_out = _sum_adamw(r_out.reshape(N_DEV, depth * shard_out, d), w_out.reshape(depth * shard_out, d),
                         m_w_out.reshape(depth * shard_out, d), v_w_out.reshape(depth * shard_out, d), "adamw_w_out")
    res_small = _sum_adamw(r_small, _pack_small(weights), _pack_small(mom_m), _pack_small(mom_v), "adamw_small")

    names = ("lower_bounds", "pre_norm_g", "w_in", "hgrn_norm_g", "fox_f_bias", "pool_w", "pool_scale", "w_out", "post_norm_g")
    outs = [loss, dx]
    for i in range(4):
        small = _unpack_small(res_small[i], weights)
        full = dict(small, w_in=res_in[i].reshape(w_in.shape), w_out=res_out[i].reshape(w_out.shape))
        outs += [full[k] for k in names]
    return tuple(outs)
```

```python
import functools

import jax
import jax.numpy as jnp
from jax import lax
from jax.experimental import pallas as pl
from jax.experimental.pallas import tpu as pltpu

F32, BF16 = jnp.float32, jnp.bfloat16
HI = lax.Precision.HIGHEST
MESH = pl.DeviceIdType.MESH
AXES = ("x", "y", "c")
N_DEV = 8

NORM_EPS = 1e-6
MASK_VALUE = -1e30
TINY = 1e-30
CHUNK = 64
SUB = 16
HGRN_W, POOL_W, FOX_W = 256, 256, 512
HEAD = 64
FOX_HEADS = 8
POOL_WINDOWS = (2, 4, 8, 16)
POOL_HALO = 16
MAIN_W = 3584
FC_PAD = 128
C_QA, C_FA, C_IA, C_GA, C_UB, C_GB, C_QC, C_KC, C_VC, C_GC = 0, 256, 512, 768, 1024, 1280, 1536, 2048, 2560, 3072
FOX_SCALE = HEAD ** -0.5

ADAM_LR, ADAM_B1, ADAM_B2, ADAM_EPS, ADAM_WD, ADAM_STEP = 0.001, 0.9, 0.999, 1e-08, 0.01, 10

VMEM_LIMIT = 56 * 1024 * 1024


def _pc(fn, name, **kw):
    return pl.pallas_call(fn, name=name, **kw)


def _params(**kw):
    return pltpu.CompilerParams(vmem_limit_bytes=VMEM_LIMIT, **kw)


def _dot(a, b):
    return jnp.dot(a, b, preferred_element_type=F32)


def _dot_nt(a, b):
    return lax.dot_general(a, b, (((1,), (1,)), ((), ())), preferred_element_type=F32)


def _dot_tn(a, b):
    return lax.dot_general(a, b, (((0,), (0,)), ((), ())), preferred_element_type=F32)


def _dot_hi(a, b):
    return jnp.dot(a, b, precision=HI, preferred_element_type=F32)


def _sigmoid(x):
    return 1.0 / (1.0 + jnp.exp(-x))


def _block_ones(n, dtype):
    r = lax.broadcasted_iota(jnp.int32, (n, n), 0) // HEAD
    c = lax.broadcasted_iota(jnp.int32, (n, n), 1) // HEAD
    return (r == c).astype(dtype)


def _sds(shape, dtype):
    return jax.ShapeDtypeStruct(shape, dtype)


def _in_proj_fwd(x, g_pre, w_main, w_fc, name):
    n, d = x.shape
    tm = min(256, n)

    def body(x_ref, g_ref, w_ref, wf_ref, proj_ref, fc_ref, h_ref):
        xv = x_ref[...]
        r = lax.rsqrt(jnp.mean(xv * xv, axis=-1, keepdims=True) + NORM_EPS)
        hb = (xv * r * g_ref[...]).astype(BF16)
        h_ref[...] = hb
        for j in range(0, MAIN_W, 512):
            proj_ref[:, j:j + 512] = _dot(hb, w_ref[:, j:j + 512])
        fc_ref[...] = _dot(hb, wf_ref[...])

    return _pc(
        body, name, grid=(n // tm,),
        in_specs=[pl.BlockSpec((tm, d), lambda i: (i, 0)), pl.BlockSpec((1, d), lambda i: (0, 0)),
                  pl.BlockSpec((d, MAIN_W), lambda i: (0, 0)), pl.BlockSpec((d, FC_PAD), lambda i: (0, 0))],
        out_specs=[pl.BlockSpec((tm, MAIN_W), lambda i: (i, 0)), pl.BlockSpec((tm, FC_PAD), lambda i: (i, 0)),
                   pl.BlockSpec((tm, d), lambda i: (i, 0))],
        out_shape=[_sds((n, MAIN_W), F32), _sds((n, FC_PAD), F32), _sds((n, d), BF16)],
        compiler_params=_params(),
    )(x, g_pre, w_main, w_fc)


def _fox_decay_fwd(fc, bias, nb, name):
    n = fc.shape[0]
    t = n // nb
    tt = min(256, t)
    nt = t // tt

    def body(fc_ref, b_ref, c_ref, carry):
        i = pl.program_id(1)

        @pl.when(i == 0)
        def _():
            carry[...] = jnp.zeros_like(carry)

        xv = fc_ref[...] + b_ref[...]
        lf = jnp.minimum(xv, 0.0) - jnp.log(1.0 + jnp.exp(-jnp.abs(xv)))
        r = lax.broadcasted_iota(jnp.int32, (tt, tt), 0)
        cc = lax.broadcasted_iota(jnp.int32, (tt, tt), 1)
        cs = _dot_hi((r >= cc).astype(F32), lf) + carry[...]
        c_ref[...] = cs
        carry[...] = cs[tt - 1:tt, :]

    return _pc(
        body, name, grid=(nb, nt),
        in_specs=[pl.BlockSpec((tt, FC_PAD), lambda b, i: (b * nt + i, 0)), pl.BlockSpec((1, FC_PAD), lambda b, i: (0, 0))],
        out_specs=pl.BlockSpec((tt, FC_PAD), lambda b, i: (b * nt + i, 0)),
        out_shape=_sds((n, FC_PAD), F32),
        scratch_shapes=[pltpu.VMEM((1, FC_PAD), F32)],
        compiler_params=_params(),
    )(fc, bias)


def _hgrn_gates(q, z, lb):
    sig = _sigmoid(z)
    sn = _sigmoid(-z)
    f = lb + (1.0 - lb) * sig
    g = jnp.log(jnp.maximum(f, TINY))
    k = (1.0 - lb) * sn
    sq = _sigmoid(q)
    return sig, sn, f, g, k, sq


def _sub_tri(n, lower):
    r = lax.broadcasted_iota(jnp.int32, (n, n), 0)
    c = lax.broadcasted_iota(jnp.int32, (n, n), 1)
    tri = (r >= c) if lower else (r <= c)
    return jnp.logical_and(r // SUB == c // SUB, tri).astype(F32)


def _hgrn_decays(qs, k, b):
    srow = lax.broadcasted_iota(jnp.int32, (SUB, HGRN_W), 0)
    es, ws = [], []
    for t in range(SUB):
        e = jnp.where(srow <= t, jnp.exp(b[t:t + 1, :] - b), 0.0)
        es.append(e)
        ws.append(e * (qs[t:t + 1, :] * k))
    return srow, es, ws


def _hgrn_state_step(st, k, v, b, bmask):
    bl = b[SUB - 1:SUB, :]
    ktil = k * jnp.exp(bl - b)
    return st * jnp.exp(bl) + _dot_tn(v.astype(BF16), ktil.astype(BF16)) * bmask


def _hgrn_sub_fwd(qs, k, v, b, st, ones_b, bmask):
    srow, _, ws = _hgrn_decays(qs, k, b)
    aexp = _dot(jnp.concatenate(ws, axis=0).astype(BF16), ones_b)
    o = _dot_nt((qs * jnp.exp(b)).astype(BF16), st.astype(BF16))
    for t in range(SUB):
        row = jnp.sum(aexp[t * SUB:(t + 1) * SUB, :] * v, axis=0, keepdims=True)
        o = o + jnp.where(srow == t, row, 0.0)
    return o, _hgrn_state_step(st, k, v, b, bmask)


def _hgrn_tile(t):
    return min(256, t)


def _hgrn_fwd(proj, lb, ones_b, nb, name):
    n = proj.shape[0]
    t = n // nb
    tt = _hgrn_tile(t)
    nt = t // tt
    ncs = tt // CHUNK
    w = HGRN_W

    def body(q_ref, z_ref, v_ref, lb_ref, ones_ref, o_ref, s0_ref, st_s, b_s, qs_s, k_s):
        @pl.when(pl.program_id(1) == 0)
        def _():
            st_s[...] = jnp.zeros_like(st_s)

        q = q_ref[...]
        _, _, _, g, k, sq = _hgrn_gates(q, z_ref[...], lb_ref[...])
        b_s[...] = _dot_hi(_sub_tri(tt, True), g)
        qs_s[...] = q * sq
        k_s[...] = k
        bmask = _block_ones(w, F32)
        ones_b = ones_ref[...]

        def chunk(c, carry):
            st = st_s[...]
            s0_ref[c] = st
            base = pl.multiple_of(c * CHUNK, CHUNK)
            for u in range(CHUNK // SUB):
                rows = pl.ds(base + u * SUB, SUB)
                o, st = _hgrn_sub_fwd(qs_s[rows, :], k_s[rows, :], v_ref[rows, :], b_s[rows, :], st, ones_b, bmask)
                o_ref[rows, :] = o
            st_s[...] = st
            return carry

        lax.fori_loop(0, ncs, chunk, 0)

    def col(j):
        return pl.BlockSpec((tt, w), lambda b, i: (b * nt + i, j))

    return _pc(
        body, name, grid=(nb, nt),
        in_specs=[col(C_QA // w), col(C_FA // w), col(C_IA // w), pl.BlockSpec((1, w), lambda b, i: (0, 0)),
                  pl.BlockSpec((w, w), lambda b, i: (0, 0))],
        out_specs=[pl.BlockSpec((tt, w), lambda b, i: (b * nt + i, 0)),
                   pl.BlockSpec((ncs, w, w), lambda b, i: (b * nt + i, 0, 0))],
        out_shape=[_sds((n, w), F32), _sds((n // CHUNK, w, w), F32)],
        scratch_shapes=[pltpu.VMEM((w, w), F32)] + [pltpu.VMEM((tt, w), F32)] * 3,
        compiler_params=_params(),
    )(proj, proj, proj, lb, ones_b)


def _pool_lane_windows():
    lane = lax.broadcasted_iota(jnp.int32, (1, POOL_W), 1) // HEAD
    wl = jnp.zeros((1, POOL_W), F32)
    for gi, win in enumerate(POOL_WINDOWS):
        wl = jnp.where(lane == gi, float(win), wl)
    return lane, wl


def _pool_select(lane, parts):
    out = parts[-1]
    for gi in range(len(parts) - 2, -1, -1):
        out = jnp.where(lane == gi, parts[gi], out)
    return out


def _pool_mix(u, halo, t0, tt):
    lane, wl = _pool_lane_windows()
    ext = jnp.concatenate([halo, u], axis=0)
    sums, cur, shift = [], ext, 1
    for _ in POOL_WINDOWS:
        cur = cur + pltpu.roll(cur, shift, axis=0)
        sums.append(cur[POOL_HALO:, :])
        shift *= 2
    tpos = (t0 + lax.broadcasted_iota(jnp.int32, (tt, POOL_W), 0)).astype(F32)
    cnt = jnp.minimum(tpos + 1.0, wl)
    return _pool_select(lane, sums) / cnt - u, cnt


def _pool_specs(tt, nt, nhb):
    cu, cg = C_UB // POOL_W, C_GB // POOL_W
    return [pl.BlockSpec((tt, POOL_W), lambda b, i: (b * nt + i, cu)),
            pl.BlockSpec((tt, POOL_W), lambda b, i: (b * nt + i, cg)),
            pl.BlockSpec((POOL_HALO, POOL_W), lambda b, i: (jnp.maximum((b * nt + i) * nhb - 1, 0), cu))]


def _pool_fwd(proj, wbd, scale, nb, name):
    n = proj.shape[0]
    t = n // nb
    tt = min(256, t)
    nt = t // tt
    nhb = tt // POOL_HALO

    def body(u_ref, g_ref, h_ref, w_ref, s_ref, o_ref):
        i = pl.program_id(1)
        halo = jnp.where(i == 0, 0.0, h_ref[...])
        pooled, _ = _pool_mix(u_ref[...], halo, i * tt, tt)
        gv = g_ref[...]
        o_ref[...] = _dot(pooled.astype(BF16), w_ref[...]) * s_ref[...] * (gv * _sigmoid(gv))

    return _pc(
        body, name, grid=(nb, nt),
        in_specs=_pool_specs(tt, nt, nhb) + [pl.BlockSpec((POOL_W, POOL_W), lambda b, i: (0, 0)),
                                             pl.BlockSpec((1, POOL_W), lambda b, i: (0, 0))],
        out_specs=pl.BlockSpec((tt, POOL_W), lambda b, i: (b * nt + i, 0)),
        out_shape=_sds((n, POOL_W), F32),
        compiler_params=_params(),
    )(proj, proj, proj, wbd, scale)


def _lane_lo():
    return lax.broadcasted_iota(jnp.int32, (1, 2 * HEAD), 1) < HEAD


def _put_col(tile, hh, colv):
    lane = lax.broadcasted_iota(jnp.int32, tile.shape, 1)
    return jnp.where(lane == hh, colv, tile)


def _fox_fwd(proj, c_col, c_row, nb, name):
    n = proj.shape[0]
    t = n // nb
    tb = min(256, t)
    nq = t // tb
    pw = 2 * HEAD

    def body(q_ref, k_ref, v_ref, cc_ref, cr_ref, o_ref, lse_ref, m_s, l_s, acc_s):
        qi, kj = pl.program_id(1), pl.program_id(2)

        @pl.when(kj == 0)
        def _():
            m_s[...] = jnp.full_like(m_s, -jnp.inf)
            l_s[...] = jnp.zeros_like(l_s)
            acc_s[...] = jnp.zeros_like(acc_s)

        @pl.when(kj <= qi)
        def _():
            rows = qi * tb + lax.broadcasted_iota(jnp.int32, (tb, tb), 0)
            cols = kj * tb + lax.broadcasted_iota(jnp.int32, (tb, tb), 1)
            causal = rows >= cols
            lo = _lane_lo()
            for p in range(FOX_HEADS // 2):
                sl = slice(p * pw, (p + 1) * pw)
                qp = q_ref[:, sl] * FOX_SCALE
                kp = k_ref[:, sl].astype(BF16)
                vp = v_ref[:, sl].astype(BF16)
                for h in range(2):
                    hh = 2 * p + h
                    lm = lo if h == 0 else jnp.logical_not(lo)
                    s = _dot_nt(jnp.where(lm, qp, 0.0).astype(BF16), kp)
                    s = s + (cc_ref[:, hh:hh + 1] - cr_ref[hh:hh + 1, :])
                    s = jnp.where(causal, s, MASK_VALUE)
                    m_prev = m_s[hh]
                    m_new = jnp.maximum(m_prev, jnp.max(s, axis=1, keepdims=True))
                    alpha = jnp.exp(m_prev - m_new)
                    pe = jnp.exp(s - m_new)
                    l_s[hh] = alpha * l_s[hh] + jnp.sum(pe, axis=1, keepdims=True)
                    m_s[hh] = m_new
                    pv = _dot(pe.astype(BF16), vp)
                    acc = acc_s[:, sl]
                    acc_s[:, sl] = jnp.where(lm, alpha * acc + pv, acc)

        @pl.when(kj == qi)
        def _():
            lo = _lane_lo()
            lse = jnp.zeros((tb, FC_PAD), F32)
            for p in range(FOX_HEADS // 2):
                sl = slice(p * pw, (p + 1) * pw)
                linv = jnp.where(lo, 1.0 / l_s[2 * p], 1.0 / l_s[2 * p + 1])
                o_ref[:, sl] = acc_s[:, sl] * linv
                for h in range(2):
                    hh = 2 * p + h
                    lse = _put_col(lse, hh, m_s[hh] + jnp.log(l_s[hh]))
            lse_ref[...] = lse

    def qspec(j):
        return pl.BlockSpec((tb, FOX_W), lambda b, qi, kj: (b * nq + qi, j))

    def kspec(j):
        return pl.BlockSpec((tb, FOX_W), lambda b, qi, kj: (b * nq + jnp.minimum(kj, qi), j))

    return _pc(
        body, name, grid=(nb, nq, nq),
        in_specs=[qspec(C_QC // FOX_W), kspec(C_KC // FOX_W), kspec(C_VC // FOX_W),
                  pl.BlockSpec((tb, FC_PAD), lambda b, qi, kj: (b * nq + qi, 0)),
                  pl.BlockSpec((None, FOX_HEADS, tb), lambda b, qi, kj: (b, 0, jnp.minimum(kj, qi)))],
        out_specs=[pl.BlockSpec((tb, FOX_W), lambda b, qi, kj: (b * nq + qi, 0)),
                   pl.BlockSpec((tb, FC_PAD), lambda b, qi, kj: (b * nq + qi, 0))],
        out_shape=[_sds((n, FOX_W), F32), _sds((n, FC_PAD), F32)],
        scratch_shapes=[pltpu.VMEM((FOX_HEADS, tb, 1), F32), pltpu.VMEM((FOX_HEADS, tb, 1), F32),
                        pltpu.VMEM((tb, FOX_W), F32)],
        compiler_params=_params(),
    )(proj, proj, proj, c_col, c_row)


def _head_mean(x, ones_f):
    return _dot_hi(x, ones_f) * (1.0 / HEAD)


def _merge_fwd(x, proj, o_h, o_b, o_c, gh, w_out, g_post, name):
    n, d = x.shape
    tm = min(256, n)

    def body(x_ref, ga_ref, gc_ref, oh_ref, ob_ref, oc_ref, gh_ref, w_ref, gp_ref, xo_ref, mix_ref, y_ref):
        oh = oh_ref[...]
        ones_f = _block_ones(HGRN_W, F32)
        na = oh * lax.rsqrt(_head_mean(oh * oh, ones_f) + NORM_EPS) * gh_ref[...]
        ga, gc = ga_ref[...], gc_ref[...]
        mixed = jnp.concatenate([na * (ga * _sigmoid(ga)), ob_ref[...], oc_ref[...] * (gc * _sigmoid(gc))], axis=1)
        mb = mixed.astype(BF16)
        mix_ref[...] = mb
        y = _dot(mb, w_ref[...])
        y_ref[...] = y
        xo_ref[...] = x_ref[...] + y * lax.rsqrt(jnp.mean(y * y, axis=-1, keepdims=True) + NORM_EPS) * gp_ref[...]

    def row(wd, j=0):
        return pl.BlockSpec((tm, wd), lambda i: (i, j))

    def full(a, b):
        return pl.BlockSpec((a, b), lambda i: (0, 0))

    return _pc(
        body, name, grid=(n // tm,),
        in_specs=[row(d), row(HGRN_W, C_GA // HGRN_W), row(FOX_W, C_GC // FOX_W), row(HGRN_W), row(POOL_W), row(FOX_W),
                  full(1, HGRN_W), full(d, d), full(1, d)],
        out_specs=[row(d), row(d), row(d)],
        out_shape=[_sds((n, d), F32), _sds((n, d), BF16), _sds((n, d), F32)],
        compiler_params=_params(),
    )(x, proj, proj, o_h, o_b, o_c, gh, w_out, g_post)


def _loss_fwd_bwd(x, target, name):
    n, d = x.shape
    tm = min(512, n)

    def body(x_ref, t_ref, dx_ref, sq_ref):
        @pl.when(pl.program_id(0) == 0)
        def _():
            sq_ref[...] = jnp.zeros_like(sq_ref)

        e = x_ref[...] - t_ref[...]
        dx_ref[...] = e * (1.0 / d)
        sq_ref[...] += jnp.sum(e * e, axis=0, keepdims=True)

    return _pc(
        body, name, grid=(n // tm,),
        in_specs=[pl.BlockSpec((tm, d), lambda i: (i, 0))] * 2,
        out_specs=[pl.BlockSpec((tm, d), lambda i: (i, 0)), pl.BlockSpec((1, d), lambda i: (0, 0))],
        out_shape=[_sds((n, d), F32), _sds((1, d), F32)],
        compiler_params=_params(),
    )(x, target)


def _rms_bwd(dy_scaled, xhat, r):
    return r * (dy_scaled - xhat * jnp.mean(dy_scaled * xhat, axis=-1, keepdims=True))


def _merge_bwd(dxo, y, g_post, w_out_t, name):
    n, d = y.shape
    tm = min(256, n)

    def body(dx_ref, y_ref, gp_ref, wt_ref, dy_ref, dm_ref, dgp_ref):
        @pl.when(pl.program_id(0) == 0)
        def _():
            dgp_ref[...] = jnp.zeros_like(dgp_ref)

        yv, dxv = y_ref[...], dx_ref[...]
        r = lax.rsqrt(jnp.mean(yv * yv, axis=-1, keepdims=True) + NORM_EPS)
        yh = yv * r
        dgp_ref[...] += jnp.sum(dxv * yh, axis=0, keepdims=True)
        dyb = _rms_bwd(dxv * gp_ref[...], yh, r).astype(BF16)
        dy_ref[...] = dyb
        dm_ref[...] = _dot(dyb, wt_ref[...])

    row = pl.BlockSpec((tm, d), lambda i: (i, 0))
    return _pc(
        body, name, grid=(n // tm,),
        in_specs=[row, row, pl.BlockSpec((1, d), lambda i: (0, 0)), pl.BlockSpec((d, d), lambda i: (0, 0))],
        out_specs=[row, row, pl.BlockSpec((1, d), lambda i: (0, 0))],
        out_shape=[_sds((n, d), BF16), _sds((n, d), F32), _sds((1, d), F32)],
        compiler_params=_params(),
    )(dxo, y, g_post, w_out_t)


def _grad_matmul(a, b, name):
    n, ka = a.shape
    kb = b.shape[1]
    ta = min(256, ka)
    tb = min(512, kb)

    def body(a_ref, b_ref, o_ref):
        o_ref[...] = _dot_tn(a_ref[...], b_ref[...])

    return _pc(
        body, name, grid=(ka // ta, kb // tb),
        in_specs=[pl.BlockSpec((n, ta), lambda i, j: (0, i)), pl.BlockSpec((n, tb), lambda i, j: (0, j))],
        out_specs=pl.BlockSpec((ta, tb), lambda i, j: (i, j)),
        out_shape=_sds((ka, kb), F32),
        compiler_params=_params(),
    )(a, b)


def _hgrn_sub_bwd(qs, k, v, b, do, s0, ds1, ones_b, bmask):
    bl = b[SUB - 1:SUB, :]
    eb, ebl, ekt = jnp.exp(b), jnp.exp(bl), jnp.exp(bl - b)
    qe, ktil = qs * eb, k * ekt
    ds1b, dob = ds1.astype(BF16), do.astype(BF16)
    dv = _dot_nt(ktil.astype(BF16), ds1b)
    dqe = _dot(dob, s0.astype(BF16))
    dktil = _dot(v.astype(BF16), ds1b)
    dbl = jnp.sum(dktil * ktil, axis=0, keepdims=True) + ebl * jnp.sum(s0 * ds1, axis=0, keepdims=True)
    ds0 = ds1 * ebl + _dot_tn(dob, qe.astype(BF16)) * bmask
    srow, es, ws = _hgrn_decays(qs, k, b)
    aexp = _dot(jnp.concatenate(ws, axis=0).astype(BF16), ones_b)
    gexp = _dot(jnp.concatenate([do[t:t + 1, :] * v for t in range(SUB)], axis=0).astype(BF16), ones_b)
    dq = dqe * eb
    dk = dktil * ekt
    for t in range(SUB):
        sl = slice(t * SUB, (t + 1) * SUB)
        ge = gexp[sl, :] * es[t]
        dq = dq + jnp.where(srow == t, jnp.sum(ge * k, axis=0, keepdims=True), 0.0)
        dk = dk + ge * qs[t:t + 1, :]
        dv = dv + aexp[sl, :] * do[t:t + 1, :]
    return dq, dk, dv, dbl, ds0


def _hgrn_bwd(dmix, proj, o_h, s0, gh, lb, ones_b, nb, name):
    n = proj.shape[0]
    t = n // nb
    tt = _hgrn_tile(t)
    nt = t // tt
    ncs = tt // CHUNK
    nsub = CHUNK // SUB
    w = HGRN_W

    def body(dm_ref, q_ref, z_ref, v_ref, ga_ref, oh_ref, s0_ref, gh_ref, lb_ref, ones_ref,
             dp_ref, dgh_ref, dlb_ref, ds_s, ss_s, b_s, qs_s, k_s, do_s, dq_s, dk_s, dv_s, dbl_s):
        first = jnp.logical_and(pl.program_id(0) == 0, pl.program_id(1) == 0)

        @pl.when(first)
        def _():
            dgh_ref[...] = jnp.zeros_like(dgh_ref)
            dlb_ref[...] = jnp.zeros_like(dlb_ref)

        @pl.when(pl.program_id(1) == 0)
        def _():
            ds_s[...] = jnp.zeros_like(ds_s)

        ones_b = ones_ref[...]
        ones_f = ones_b.astype(F32)
        bmask = _block_ones(w, F32)
        lbv, ghv = lb_ref[...], gh_ref[...]
        oh, ga, dm = oh_ref[...], ga_ref[...], dm_ref[...]
        rn = lax.rsqrt(_head_mean(oh * oh, ones_f) + NORM_EPS)
        nh = oh * rn
        sga = _sigmoid(ga)
        dp_ref[:, 3 * w:4 * w] = (dm * nh * ghv * (sga * (1.0 + ga * (1.0 - sga)))).astype(BF16)
        dn = dm * (ga * sga)
        dgh_ref[...] += jnp.sum(dn * nh, axis=0, keepdims=True)
        dn = dn * ghv
        do_s[...] = rn * (dn - nh * _head_mean(dn * nh, ones_f))
        q = q_ref[...]
        sig, sn, f, g, k, sq = _hgrn_gates(q, z_ref[...], lbv)
        qs = q * sq
        b_s[...] = _dot_hi(_sub_tri(tt, True), g)
        qs_s[...] = qs
        k_s[...] = k

        def chunk(cc, carry):
            c = ncs - 1 - cc
            base = pl.multiple_of(c * CHUNK, CHUNK)
            st = s0_ref[c]
            for u in range(nsub):
                ss_s[u] = st
                if u < nsub - 1:
                    rows = pl.ds(base + u * SUB, SUB)
                    st = _hgrn_state_step(st, k_s[rows, :], v_ref[rows, :], b_s[rows, :], bmask)
            ds = ds_s[...]
            for u in reversed(range(nsub)):
                rows = pl.ds(base + u * SUB, SUB)
                dq, dk, dv, dbl, ds = _hgrn_sub_bwd(qs_s[rows, :], k_s[rows, :], v_ref[rows, :], b_s[rows, :],
                                                    do_s[rows, :], ss_s[u], ds, ones_b, bmask)
                dq_s[rows, :] = dq
                dk_s[rows, :] = dk
                dv_s[rows, :] = dv
                dbl_s[rows, :] = jnp.broadcast_to(dbl, (SUB, w))
            ds_s[...] = ds
            return carry

        lax.fori_loop(0, ncs, chunk, 0)
        dqs, dk = dq_s[...], dk_s[...]
        dg = _dot_hi(_sub_tri(tt, False), qs * dqs - k * dk) + dbl_s[...]
        dfz = jnp.where(f > TINY, dg / jnp.maximum(f, TINY), 0.0)
        dlb_ref[...] += jnp.sum(dfz * (1.0 - sig) - dk * sn, axis=0, keepdims=True)
        dp_ref[:, 0:w] = (dqs * (sq * (1.0 + q * (1.0 - sq)))).astype(BF16)
        dp_ref[:, w:2 * w] = ((dfz - dk) * (1.0 - lbv) * sig * sn).astype(BF16)
        dp_ref[:, 2 * w:3 * w] = dv_s[...].astype(BF16)

    def rv(b, i):
        return b * nt + (nt - 1 - i)

    def col(j):
        return pl.BlockSpec((tt, w), lambda b, i: (rv(b, i), j))

    def full(a, bb):
        return pl.BlockSpec((a, bb), lambda b, i: (0, 0))

    return _pc(
        body, name, grid=(nb, nt),
        in_specs=[col(0), col(C_QA // w), col(C_FA // w), col(C_IA // w), col(C_GA // w), col(0),
                  pl.BlockSpec((ncs, w, w), lambda b, i: (rv(b, i), 0, 0)), full(1, w), full(1, w), full(w, w)],
        out_specs=[pl.BlockSpec((tt, 4 * w), lambda b, i: (rv(b, i), 0)), full(1, w), full(1, w)],
        out_shape=[_sds((n, 4 * w), BF16), _sds((1, w), F32), _sds((1, w), F32)],
        scratch_shapes=[pltpu.VMEM((w, w), F32), pltpu.VMEM((nsub, w, w), F32)] + [pltpu.VMEM((tt, w), F32)] * 8,
        compiler_params=_params(),
    )(dmix, proj, proj, proj, proj, o_h, s0, gh, lb, ones_b)


def _pool_bwd(dmix, proj, wbd, wbd_t, scale, nb, name):
    n = proj.shape[0]
    t = n // nb
    tt = min(256, t)
    nt = t // tt
    nhb = tt // POOL_HALO
    cu, cg, cm = C_UB // POOL_W, C_GB // POOL_W, HGRN_W // POOL_W

    def body(u_ref, g_ref, h_ref, dm_ref, gn_ref, dmn_ref, w_ref, wt_ref, s_ref, dp_ref, dw_ref, ds_ref):
        i = pl.program_id(1)
        first = jnp.logical_and(pl.program_id(0) == 0, i == 0)

        @pl.when(first)
        def _():
            dw_ref[...] = jnp.zeros_like(dw_ref)
            ds_ref[...] = jnp.zeros_like(ds_ref)

        sc = s_ref[...]
        halo = jnp.where(i == 0, 0.0, h_ref[...])
        pooled, cnt = _pool_mix(u_ref[...], halo, i * tt, tt)
        pb = pooled.astype(BF16)
        pre = _dot(pb, w_ref[...])
        gv, dm = g_ref[...], dm_ref[...]
        sg = _sigmoid(gv)
        silu = gv * sg
        dgb = dm * pre * sc * (sg * (1.0 + gv * (1.0 - sg)))
        ds_ref[...] += jnp.sum(dm * pre * silu, axis=0, keepdims=True)
        dpre = (dm * sc * silu).astype(BF16)
        dw_ref[...] += _dot_tn(pb, dpre)
        dpool = _dot(dpre, wt_ref[...])
        gn = gn_ref[...]
        dpre_n = (dmn_ref[...] * sc * (gn * _sigmoid(gn))).astype(BF16)
        dpool_n = jnp.where(i == nt - 1, 0.0, _dot(dpre_n, wt_ref[...]))
        lane, wl = _pool_lane_windows()
        tpos_n = ((i + 1) * tt + lax.broadcasted_iota(jnp.int32, (POOL_HALO, POOL_W), 0)).astype(F32)
        ext = jnp.concatenate([dpool / cnt, dpool_n / jnp.minimum(tpos_n + 1.0, wl)], axis=0)
        rows = tt + POOL_HALO
        sums, cur, shift = [], ext, 1
        for _ in POOL_WINDOWS:
            cur = cur + pltpu.roll(cur, rows - shift, axis=0)
            sums.append(cur[:tt, :])
            shift *= 2
        du = _pool_select(lane, sums) - dpool
        dp_ref[...] = jnp.concatenate([du, dgb], axis=1).astype(BF16)

    def nxt(b, i):
        return jnp.minimum((b * nt + i + 1) * nhb, n // POOL_HALO - 1)

    return _pc(
        body, name, grid=(nb, nt),
        in_specs=_pool_specs(tt, nt, nhb) + [
            pl.BlockSpec((tt, POOL_W), lambda b, i: (b * nt + i, cm)),
            pl.BlockSpec((POOL_HALO, POOL_W), lambda b, i: (nxt(b, i), cg)),
            pl.BlockSpec((POOL_HALO, POOL_W), lambda b, i: (nxt(b, i), cm)),
            pl.BlockSpec((POOL_W, POOL_W), lambda b, i: (0, 0)), pl.BlockSpec((POOL_W, POOL_W), lambda b, i: (0, 0)),
            pl.BlockSpec((1, POOL_W), lambda b, i: (0, 0))],
        out_specs=[pl.BlockSpec((tt, 2 * POOL_W), lambda b, i: (b * nt + i, 0)),
                   pl.BlockSpec((POOL_W, POOL_W), lambda b, i: (0, 0)), pl.BlockSpec((1, POOL_W), lambda b, i: (0, 0))],
        out_shape=[_sds((n, 2 * POOL_W), BF16), _sds((POOL_W, POOL_W), F32), _sds((1, POOL_W), F32)],
        compiler_params=_params(),
    )(proj, proj, proj, dmix, proj, dmix, wbd, wbd_t, scale)


def _fox_gate_bwd(dmix, proj, o_c, name):
    n = proj.shape[0]
    tm = min(256, n)

    def body(dm_ref, gc_ref, oc_ref, da_ref, dg_ref, dl_ref):
        dm, gc, oc = dm_ref[...], gc_ref[...], oc_ref[...]
        sg = _sigmoid(gc)
        da = dm * (gc * sg)
        da_ref[...] = da.astype(BF16)
        dg_ref[...] = (dm * oc * (sg * (1.0 + gc * (1.0 - sg)))).astype(BF16)
        r = lax.broadcasted_iota(jnp.int32, (FOX_W, FC_PAD), 0) // HEAD
        c = lax.broadcasted_iota(jnp.int32, (FOX_W, FC_PAD), 1)
        dl_ref[...] = _dot_hi(da * oc, (r == c).astype(F32))

    def row(wd, j=0):
        return pl.BlockSpec((tm, wd), lambda i: (i, j))

    return _pc(
        body, name, grid=(n // tm,),
        in_specs=[row(FOX_W, (HGRN_W + POOL_W) // FOX_W), row(FOX_W, C_GC // FOX_W), row(FOX_W)],
        out_specs=[row(FOX_W), row(FOX_W), row(FC_PAD)],
        out_shape=[_sds((n, FOX_W), BF16), _sds((n, FOX_W), BF16), _sds((n, FC_PAD), F32)],
        compiler_params=_params(),
    )(dmix, proj, o_c)


def _fox_bwd_q(proj, da, c_col, c_row, lse, delta, nb, name):
    n = proj.shape[0]
    t = n // nb
    tb = min(256, t)
    nq = t // tb
    pw = 2 * HEAD

    def body(q_ref, k_ref, v_ref, da_ref, cc_ref, cr_ref, lse_ref, dl_ref, dq_ref, dc_ref, acc_s, dc_s):
        qi, kj = pl.program_id(1), pl.program_id(2)

        @pl.when(kj == 0)
        def _():
            acc_s[...] = jnp.zeros_like(acc_s)
            dc_s[...] = jnp.zeros_like(dc_s)

        @pl.when(kj <= qi)
        def _():
            rows = qi * tb + lax.broadcasted_iota(jnp.int32, (tb, tb), 0)
            cols = kj * tb + lax.broadcasted_iota(jnp.int32, (tb, tb), 1)
            causal = rows >= cols
            lo = _lane_lo()
            dc = dc_s[...]
            for p in range(FOX_HEADS // 2):
                sl = slice(p * pw, (p + 1) * pw)
                qp = q_ref[:, sl] * FOX_SCALE
                kf = k_ref[:, sl]
                kp = kf.astype(BF16)
                vp = v_ref[:, sl].astype(BF16)
                dap = da_ref[:, sl]
                acc = acc_s[:, sl]
                for h in range(2):
                    hh = 2 * p + h
                    lm = lo if h == 0 else jnp.logical_not(lo)
                    s = _dot_nt(jnp.where(lm, qp, 0.0).astype(BF16), kp)
                    s = s + (cc_ref[:, hh:hh + 1] - cr_ref[hh:hh + 1, :])
                    pe = jnp.where(causal, jnp.exp(s - lse_ref[:, hh:hh + 1]), 0.0)
                    dp = _dot_nt(jnp.where(lm, dap, jnp.zeros_like(dap)), vp)
                    ds = pe * (dp - dl_ref[:, hh:hh + 1])
                    acc = acc + _dot(ds.astype(BF16), jnp.where(lm, kf, 0.0).astype(BF16))
                    dc = dc + _put_col(jnp.zeros_like(dc), hh, jnp.sum(ds, axis=1, keepdims=True))
                acc_s[:, sl] = acc
            dc_s[...] = dc

        @pl.when(kj == qi)
        def _():
            dq_ref[...] = (acc_s[...] * FOX_SCALE).astype(BF16)
            dc_ref[...] = dc_s[...]

    def qspec(wd, j=0):
        return pl.BlockSpec((tb, wd), lambda b, qi, kj: (b * nq + qi, j))

    def kspec(j):
        return pl.BlockSpec((tb, FOX_W), lambda b, qi, kj: (b * nq + jnp.minimum(kj, qi), j))

    return _pc(
        body, name, grid=(nb, nq, nq),
        in_specs=[qspec(FOX_W, C_QC // FOX_W), kspec(C_KC // FOX_W), kspec(C_VC // FOX_W), qspec(FOX_W), qspec(FC_PAD),
                  pl.BlockSpec((None, FOX_HEADS, tb), lambda b, qi, kj: (b, 0, jnp.minimum(kj, qi))),
                  qspec(FC_PAD), qspec(FC_PAD)],
        out_specs=[qspec(FOX_W), qspec(FC_PAD)],
        out_shape=[_sds((n, FOX_W), BF16), _sds((n, FC_PAD), F32)],
        scratch_shapes=[pltpu.VMEM((tb, FOX_W), F32), pltpu.VMEM((tb, FC_PAD), F32)],
        compiler_params=_params(),
    )(proj, proj, proj, da, c_col, c_row, lse, delta)


def _fox_bwd_kv(proj, da, c_col, c_row, lse_row, delta_row, nb, name):
    n = proj.shape[0]
    t = n // nb
    tb = min(256, t)
    nq = t // tb
    pw = 2 * HEAD

    def body(q_ref, k_ref, v_ref, da_ref, cc_ref, cr_ref, lse_ref, dl_ref, dk_ref, dv_ref, dc_ref, dk_s, dv_s, dc_s):
        kj, qi = pl.program_id(1), pl.program_id(2)

        @pl.when(qi == 0)
        def _():
            dk_s[...] = jnp.zeros_like(dk_s)
            dv_s[...] = jnp.zeros_like(dv_s)
            dc_s[...] = jnp.zeros_like(dc_s)

        @pl.when(qi >= kj)
        def _():
            keys = kj * tb + lax.broadcasted_iota(jnp.int32, (tb, tb), 0)
            qrys = qi * tb + lax.broadcasted_iota(jnp.int32, (tb, tb), 1)
            causal = qrys >= keys
            lo = _lane_lo()
            dc = dc_s[...]
            for p in range(FOX_HEADS // 2):
                sl = slice(p * pw, (p + 1) * pw)
                qp = q_ref[:, sl] * FOX_SCALE
                kp = k_ref[:, sl].astype(BF16)
                vp = v_ref[:, sl].astype(BF16)
                dap = da_ref[:, sl]
                dk, dv = dk_s[:, sl], dv_s[:, sl]
                for h in range(2):
                    hh = 2 * p + h
                    lm = lo if h == 0 else jnp.logical_not(lo)
                    qm = jnp.where(lm, qp, 0.0).astype(BF16)
                    dam = jnp.where(lm, dap, jnp.zeros_like(dap))
                    s = _dot_nt(kp, qm) + (cr_ref[hh:hh + 1, :] - cc_ref[:, hh:hh + 1])
                    pe = jnp.where(causal, jnp.exp(s - lse_ref[hh:hh + 1, :]), 0.0)
                    dp = _dot_nt(vp, dam)
                    ds = pe * (dp - dl_ref[hh:hh + 1, :])
                    dv = dv + _dot(pe.astype(BF16), dam)
                    dk = dk + _dot(ds.astype(BF16), qm)
                    dc = dc - _put_col(jnp.zeros_like(dc), hh, jnp.sum(ds, axis=1, keepdims=True))
                dk_s[:, sl] = dk
                dv_s[:, sl] = dv
            dc_s[...] = dc

        @pl.when(qi == nq - 1)
        def _():
            dk_ref[...] = dk_s[...].astype(BF16)
            dv_ref[...] = dv_s[...].astype(BF16)
            dc_ref[...] = dc_s[...]

    def kspec(wd, j=0):
        return pl.BlockSpec((tb, wd), lambda b, kj, qi: (b * nq + kj, j))

    def qspec(wd, j=0):
        return pl.BlockSpec((tb, wd), lambda b, kj, qi: (b * nq + jnp.maximum(qi, kj), j))

    def qrow():
        return pl.BlockSpec((None, FOX_HEADS, tb), lambda b, kj, qi: (b, 0, jnp.maximum(qi, kj)))

    return _pc(
        body, name, grid=(nb, nq, nq),
        in_specs=[qspec(FOX_W, C_QC // FOX_W), kspec(FOX_W, C_KC // FOX_W), kspec(FOX_W, C_VC // FOX_W), qspec(FOX_W),
                  kspec(FC_PAD), qrow(), qrow(), qrow()],
        out_specs=[kspec(FOX_W), kspec(FOX_W), kspec(FC_PAD)],
        out_shape=[_sds((n, FOX_W), BF16), _sds((n, FOX_W), BF16), _sds((n, FC_PAD), F32)],
        scratch_shapes=[pltpu.VMEM((tb, FOX_W), F32), pltpu.VMEM((tb, FOX_W), F32), pltpu.VMEM((tb, FC_PAD), F32)],
        compiler_params=_params(),
    )(proj, proj, proj, da, c_col, c_row, lse_row, delta_row)


def _fox_decay_bwd(dc_q, dc_k, fc, bias, nb, name):
    n = fc.shape[0]
    t = n // nb
    tt = min(256, t)
    nt = t // tt

    def body(dcq_ref, dck_ref, fc_ref, b_ref, dfc_ref, db_ref, carry):
        i = pl.program_id(1)
        first = jnp.logical_and(pl.program_id(0) == 0, i == 0)

        @pl.when(first)
        def _():
            db_ref[...] = jnp.zeros_like(db_ref)

        @pl.when(i == 0)
        def _():
            carry[...] = jnp.zeros_like(carry)

        r = lax.broadcasted_iota(jnp.int32, (tt, tt), 0)
        cc = lax.broadcasted_iota(jnp.int32, (tt, tt), 1)
        dlf = _dot_hi((r <= cc).astype(F32), dcq_ref[...] + dck_ref[...]) + carry[...]
        carry[...] = dlf[0:1, :]
        dfc = dlf * _sigmoid(-(fc_ref[...] + b_ref[...]))
        dfc_ref[...] = dfc.astype(BF16)
        db_ref[...] += jnp.sum(dfc, axis=0, keepdims=True)

    def row():
        return pl.BlockSpec((tt, FC_PAD), lambda b, i: (b * nt + (nt - 1 - i), 0))

    return _pc(
        body, name, grid=(nb, nt),
        in_specs=[row(), row(), row(), pl.BlockSpec((1, FC_PAD), lambda b, i: (0, 0))],
        out_specs=[row(), pl.BlockSpec((1, FC_PAD), lambda b, i: (0, 0))],
        out_shape=[_sds((n, FC_PAD), BF16), _sds((1, FC_PAD), F32)],
        scratch_shapes=[pltpu.VMEM((1, FC_PAD), F32)],
        compiler_params=_params(),
    )(dc_q, dc_k, fc, bias)


def _in_proj_bwd(pieces, w_main_t, w_fc_t, x, g_pre, dxo, name):
    n, d = x.shape
    tm = min(256, n)
    widths = [p.shape[1] for p, _ in pieces]
    offs = [o for _, o in pieces]
    np_ = len(pieces)

    def body(*refs):
        p_refs = refs[:np_]
        wt_ref, wf_ref, x_ref, g_ref, dxo_ref, dx_ref, dg_ref = refs[np_:]

        @pl.when(pl.program_id(0) == 0)
        def _():
            dg_ref[...] = jnp.zeros_like(dg_ref)

        dh = _dot(p_refs[-1][...], wf_ref[...])
        for pr, wd, off in zip(p_refs[:-1], widths[:-1], offs[:-1]):
            for j in range(0, wd, 512):
                jw = min(512, wd - j)
                dh = dh + _dot(pr[:, j:j + jw], wt_ref[off + j:off + j + jw, :])
        xv = x_ref[...]
        r = lax.rsqrt(jnp.mean(xv * xv, axis=-1, keepdims=True) + NORM_EPS)
        xh = xv * r
        dg_ref[...] += jnp.sum(dh * xh, axis=0, keepdims=True)
        dx_ref[...] = dxo_ref[...] + _rms_bwd(dh * g_ref[...], xh, r)

    row = pl.BlockSpec((tm, d), lambda i: (i, 0))
    return _pc(
        body, name, grid=(n // tm,),
        in_specs=[pl.BlockSpec((tm, wd), lambda i: (i, 0)) for wd in widths] + [
            pl.BlockSpec((MAIN_W, d), lambda i: (0, 0)), pl.BlockSpec((FC_PAD, d), lambda i: (0, 0)),
            row, pl.BlockSpec((1, d), lambda i: (0, 0)), row],
        out_specs=[row, pl.BlockSpec((1, d), lambda i: (0, 0))],
        out_shape=[_sds((n, d), F32), _sds((1, d), F32)],
        compiler_params=_params(),
    )(*[p for p, _ in pieces], w_main_t, w_fc_t, x, g_pre, dxo)


def _lower_bound_table(lower_bounds, name):
    depth, w = lower_bounds.shape

    def body(lb_ref, o_ref):
        v = lb_ref[...]
        e = jnp.exp(v - jnp.max(v, axis=0, keepdims=True))
        p = e / jnp.sum(e, axis=0, keepdims=True)
        acc = jnp.zeros((1, w), F32)
        for l in range(depth):
            acc = acc + p[l:l + 1, :]
            o_ref[l:l + 1, :] = acc - p[0:1, :]

    return _pc(body, name, out_shape=_sds((depth, w), F32))(lower_bounds)


def _lower_bound_bwd(lower_bounds, dlbs, name):
    depth, w = lower_bounds.shape

    def body(lb_ref, d_ref, o_ref):
        v, dl = lb_ref[...], d_ref[...]
        e = jnp.exp(v - jnp.max(v, axis=0, keepdims=True))
        p = e / jnp.sum(e, axis=0, keepdims=True)
        tot = jnp.sum(dl, axis=0, keepdims=True)
        rows, tail = [], tot
        for l in range(depth):
            rows.append(tail - tot if l == 0 else tail)
            tail = tail - dl[l:l + 1, :]
        dp = jnp.concatenate(rows, axis=0)
        o_ref[...] = p * (dp - jnp.sum(p * dp, axis=0, keepdims=True))

    return _pc(body, name, out_shape=_sds((depth, w), F32))(lower_bounds, dlbs)


def _place():
    x, y, c = lax.axis_index("x"), lax.axis_index("y"), lax.axis_index("c")
    return x, y, c


def _gather_weights(w_in_b, w_out_b):
    arrays = (w_in_b, w_out_b)
    na = len(arrays)

    def body(*refs):
        ins, outs = refs[:na], refs[na:2 * na]
        send_sems, recv_sems, local_sems = refs[2 * na:]
        x, y, c = _place()
        me, sibling = (x, y, c), (x, y, 1 - c)
        chips = [(1 - x, y), (x, 1 - y), (1 - x, 1 - y)]

        def slot(a, px, py, pc):
            return outs[a].at[4 * px + 2 * py + pc]

        def copy(a, k, block, to, own=False):
            return pltpu.make_async_remote_copy(
                src_ref=ins[a] if own else slot(a, *block), dst_ref=slot(a, *block),
                send_sem=send_sems.at[a * 7 + k], recv_sem=recv_sems.at[a * 7 + k],
                device_id=to, device_id_type=MESH)

        mine = [pltpu.make_async_copy(ins[a], slot(a, *me), local_sems.at[a]) for a in range(na)]
        for cp in mine:
            cp.start()
        first = []
        for a in range(na):
            first.append(copy(a, 0, me, sibling, own=True))
            first += [copy(a, 1 + j, me, (*chip, c), own=True) for j, chip in enumerate(chips)]
        for cp in first:
            cp.start()
        passed = []
        for j, chip in enumerate(chips):
            for a in range(na):
                copy(a, 1 + j, (*chip, c), me).wait_recv()
                fw = copy(a, 4 + j, (*chip, c), sibling)
                fw.start()
                passed.append(fw)
        for a in range(na):
            copy(a, 0, sibling, me).wait_recv()
            for j, chip in enumerate(chips):
                copy(a, 4 + j, (*chip, 1 - c), me).wait_recv()
        for cp in first + passed:
            cp.wait_send()
        for cp in mine:
            cp.wait()

    any_spec = pl.BlockSpec(memory_space=pl.ANY)
    return _pc(
        body, "gather_weights",
        in_specs=[any_spec] * na, out_specs=[any_spec] * na,
        out_shape=[_sds((N_DEV,) + a.shape, a.dtype) for a in arrays],
        scratch_shapes=[pltpu.SemaphoreType.DMA((7 * na,)), pltpu.SemaphoreType.DMA((7 * na,)),
                        pltpu.SemaphoreType.DMA((na,))],
    )(*arrays)


def _exchange_grads(blocks_in, blocks_out, small):
    arrays = (blocks_in, blocks_out)
    na = len(arrays) + 1

    def body(*refs):
        ins, outs = refs[:na], refs[na:2 * na]
        send_sems, recv_sems, local_sems = refs[2 * na:]
        x, y, c = _place()
        me = 4 * x + 2 * y + c

        def src(a, j):
            return ins[a] if a == na - 1 else ins[a].at[j]

        local = [pltpu.make_async_copy(src(a, me), outs[a].at[me], local_sems.at[a]) for a in range(na)]
        for cp in local:
            cp.start()
        sends, recvs = [], []
        for k in range(1, N_DEV):
            px = 1 - x if k & 4 else x
            py = 1 - y if k & 2 else y
            pc = 1 - c if k & 1 else c
            peer = 4 * px + 2 * py + pc
            for a in range(na):
                s = (k - 1) * na + a
                sends.append(pltpu.make_async_remote_copy(
                    src_ref=src(a, peer), dst_ref=outs[a].at[me], send_sem=send_sems.at[s], recv_sem=recv_sems.at[s],
                    device_id=(px, py, pc), device_id_type=MESH))
                recvs.append(pltpu.make_async_remote_copy(
                    src_ref=src(a, peer), dst_ref=outs[a].at[peer], send_sem=send_sems.at[s], recv_sem=recv_sems.at[s],
                    device_id=(px, py, pc), device_id_type=MESH))
        for cp in sends:
            cp.start()
        for cp in recvs:
            cp.wait_recv()
        for cp in sends:
            cp.wait_send()
        for cp in local:
            cp.wait()

    any_spec = pl.BlockSpec(memory_space=pl.ANY)
    ns = (N_DEV - 1) * na
    return _pc(
        body, "exchange_grads",
        in_specs=[any_spec] * na, out_specs=[any_spec] * na,
        out_shape=[_sds(a.shape, a.dtype) for a in arrays] + [_sds((N_DEV,) + small.shape, small.dtype)],
        scratch_shapes=[pltpu.SemaphoreType.DMA((ns,)), pltpu.SemaphoreType.DMA((ns,)), pltpu.SemaphoreType.DMA((na,))],
    )(blocks_in, blocks_out, small)


def _sum_adamw(parts, w, m, v, name):
    r, c = w.shape
    tr = 256 if r % 256 == 0 else r

    def body(p_ref, w_ref, m_ref, v_ref, g_ref, d_ref, mo_ref, vo_ref):
        g = p_ref[0]
        for j in range(1, N_DEV):
            g = g + p_ref[j]
        wv = w_ref[...]
        mn = ADAM_B1 * m_ref[...] + (1.0 - ADAM_B1) * g
        vn = ADAM_B2 * v_ref[...] + (1.0 - ADAM_B2) * (g * g)
        m_hat = mn / (1.0 - ADAM_B1 ** ADAM_STEP)
        v_hat = vn / (1.0 - ADAM_B2 ** ADAM_STEP)
        g_ref[...] = g
        d_ref[...] = -ADAM_LR * (m_hat / (jnp.sqrt(v_hat) + ADAM_EPS) + ADAM_WD * wv)
        mo_ref[...] = mn
        vo_ref[...] = vn

    row = pl.BlockSpec((tr, c), lambda i: (i, 0))
    return _pc(
        body, name, grid=(r // tr,),
        in_specs=[pl.BlockSpec((N_DEV, tr, c), lambda i: (0, i, 0)), row, row, row],
        out_specs=[row] * 4,
        out_shape=[_sds((r, c), F32)] * 4,
        compiler_params=_params(),
    )(parts, w, m, v)


SMALL = ("lower_bounds", "pre_norm_g", "hgrn_norm_g", "fox_f_bias", "pool_w", "pool_scale", "post_norm_g")
SMALL_LANES = 128


def _pack_small(tree):
    flat = jnp.concatenate([tree[k].reshape(-1) for k in SMALL])
    rows = -(-flat.shape[0] // (8 * SMALL_LANES)) * 8
    return jnp.pad(flat, (0, rows * SMALL_LANES - flat.shape[0])).reshape(rows, SMALL_LANES)


def _unpack_small(packed, like):
    flat, out, off = packed.reshape(-1), {}, 0
    for k in SMALL:
        size = like[k].size
        out[k] = flat[off:off + size].reshape(like[k].shape)
        off += size
    return out


def _block_diag(pw):
    g = pw.shape[0]
    eye = jnp.eye(g, dtype=pw.dtype)
    return (eye[:, None, :, None] * pw[:, :, None, :]).reshape(g * HEAD, g * HEAD)


def _local_step(x, target, lbs, weights, layer_w):
    nb, t, d = x.shape
    n = nb * t
    depth = len(layer_w)
    ones_b = _block_ones(HGRN_W, BF16)
    xs = [x.reshape(n, d)]
    saved = []
    for l in range(depth):
        w_main, w_fc, _, _, w_out, _ = layer_w[l]
        g_pre = weights["pre_norm_g"][l:l + 1]
        bias = jnp.pad(weights["fox_f_bias"][l:l + 1], ((0, 0), (0, FC_PAD - FOX_HEADS)))
        wbd = _block_diag(weights["pool_w"][l]).astype(BF16)
        proj, fc, h = _in_proj_fwd(xs[l], g_pre, w_main, w_fc, f"in_proj_fwd_{l}")
        c_col = _fox_decay_fwd(fc, bias, nb, f"fox_decay_fwd_{l}")
        c_row = c_col.reshape(nb, t, FC_PAD)[:, :, :FOX_HEADS].transpose(0, 2, 1)
        o_h, s0 = _hgrn_fwd(proj, lbs[l:l + 1], ones_b, nb, f"hgrn_fwd_{l}")
        o_b = _pool_fwd(proj, wbd, weights["pool_scale"][l:l + 1], nb, f"pool_fwd_{l}")
        o_c, lse = _fox_fwd(proj, c_col, c_row, nb, f"fox_fwd_{l}")
        x_next, mixed, y = _merge_fwd(xs[l], proj, o_h, o_b, o_c, weights["hgrn_norm_g"][l:l + 1], w_out,
                                      weights["post_norm_g"][l:l + 1], f"merge_fwd_{l}")
        xs.append(x_next)
        saved.append((proj, fc, h, c_col, c_row, o_h, s0, o_c, lse, mixed, y, bias, wbd))

    dx, sq = _loss_fwd_bwd(xs[depth], target.reshape(n, d), "loss")
    loss = 0.5 * jnp.sum(sq) / d

    grads = {k: [None] * depth for k in ("pre_norm_g", "w_in", "hgrn_norm_g", "fox_f_bias", "pool_w", "pool_scale",
                                         "w_out", "post_norm_g", "lbs")}
    for l in reversed(range(depth)):
        proj, fc, h, c_col, c_row, o_h, s0, o_c, lse, mixed, y, bias, wbd = saved[l]
        _, _, w_main_t, w_fc_t, _, w_out_t = layer_w[l]
        dy, dmix, dgp = _merge_bwd(dx, y, weights["post_norm_g"][l:l + 1], w_out_t, f"merge_bwd_{l}")
        grads["post_norm_g"][l] = dgp[0]
        grads["w_out"][l] = _grad_matmul(mixed, dy, f"w_out_grad_{l}")
        d_a, dgh, dlb = _hgrn_bwd(dmix, proj, o_h, s0, weights["hgrn_norm_g"][l:l + 1], lbs[l:l + 1], ones_b, nb,
                                  f"hgrn_bwd_{l}")
        grads["hgrn_norm_g"][l] = dgh[0]
        grads["lbs"][l] = dlb[0]
        d_b, dwbd, dps = _pool_bwd(dmix, proj, wbd, wbd.T, weights["pool_scale"][l:l + 1], nb, f"pool_bwd_{l}")
        grads["pool_w"][l] = jnp.stack([dwbd[g * HEAD:(g + 1) * HEAD, g * HEAD:(g + 1) * HEAD]
                                        for g in range(len(POOL_WINDOWS))])
        grads["pool_scale"][l] = dps[0]
        da, d_gc, delta = _fox_gate_bwd(dmix, proj, o_c, f"fox_gate_bwd_{l}")

        def rows_of(a):
            return a.reshape(nb, t, FC_PAD)[:, :, :FOX_HEADS].transpose(0, 2, 1)

        d_qc, dc_q = _fox_bwd_q(proj, da, c_col, c_row, lse, delta, nb, f"fox_bwd_q_{l}")
        d_kc, d_vc, dc_k = _fox_bwd_kv(proj, da, c_col, c_row, rows_of(lse), rows_of(delta), nb, f"fox_bwd_kv_{l}")
        d_fc, dbias = _fox_decay_bwd(dc_q, dc_k, fc, bias, nb, f"fox_decay_bwd_{l}")
        grads["fox_f_bias"][l] = dbias[0, :FOX_HEADS]
        pieces = [(d_a, C_QA), (d_b, C_UB), (d_qc, C_QC), (d_kc, C_KC), (d_vc, C_VC), (d_gc, C_GC), (d_fc, None)]
        gw = [_grad_matmul(h, p, f"w_in_grad_{l}_{i}") for i, (p, _) in enumerate(pieces)]
        grads["w_in"][l] = jnp.concatenate(gw[:-1] + [gw[-1][:, :FOX_HEADS]], axis=1)
        dx, dgpre = _in_proj_bwd(pieces, w_main_t, w_fc_t, xs[l], weights["pre_norm_g"][l:l + 1], dx, f"in_proj_bwd_{l}")
        grads["pre_norm_g"][l] = dgpre[0]

    out = {k: jnp.stack(v) for k, v in grads.items()}
    out["lower_bounds"] = _lower_bound_bwd(weights["lower_bounds"], out.pop("lbs"), "lower_bound_bwd")
    return loss, dx.reshape(nb, t, d), out


def kernel(x, lower_bounds, pre_norm_g, w_in, hgrn_norm_g, fox_f_bias, pool_w, pool_scale, w_out, post_norm_g, loss_target, m_lower_bounds, m_pre_norm_g, m_w_in, m_hgrn_norm_g, m_fox_f_bias, m_pool_w, m_pool_scale, m_w_out, m_post_norm_g, v_lower_bounds, v_pre_norm_g, v_w_in, v_hgrn_norm_g, v_fox_f_bias, v_pool_w, v_pool_scale, v_w_out, v_post_norm_g):
    weights = dict(lower_bounds=lower_bounds, pre_norm_g=pre_norm_g, hgrn_norm_g=hgrn_norm_g, fox_f_bias=fox_f_bias,
                   pool_w=pool_w, pool_scale=pool_scale, post_norm_g=post_norm_g)
    mom_m = dict(lower_bounds=m_lower_bounds, pre_norm_g=m_pre_norm_g, hgrn_norm_g=m_hgrn_norm_g, fox_f_bias=m_fox_f_bias,
                 pool_w=m_pool_w, pool_scale=m_pool_scale, post_norm_g=m_post_norm_g)
    mom_v = dict(lower_bounds=v_lower_bounds, pre_norm_g=v_pre_norm_g, hgrn_norm_g=v_hgrn_norm_g, fox_f_bias=v_fox_f_bias,
                 pool_w=v_pool_w, pool_scale=v_pool_scale, post_norm_g=v_post_norm_g)
    depth, d, shard_in = w_in.shape
    shard_out = w_out.shape[1]
    in_w = N_DEV * shard_in

    g_in, g_out = _gather_weights(w_in.astype(BF16), w_out.astype(BF16))
    layer_w = []
    for l in range(depth):
        full_in = g_in[:, l].transpose(1, 0, 2).reshape(d, in_w)
        w_main = full_in[:, :MAIN_W]
        w_fc = jnp.pad(full_in[:, MAIN_W:], ((0, 0), (0, FC_PAD - (in_w - MAIN_W))))
        full_out = g_out[:, l].reshape(N_DEV * shard_out, d)
        layer_w.append((w_main, w_fc, w_main.T, w_fc.T, full_out, full_out.T))

    lbs = _lower_bound_table(lower_bounds, "lower_bound_table")
    loss, dx, grads = _local_step(x, loss_target, lbs, weights, layer_w)
    loss = lax.psum(loss, AXES)

    blocks_in = grads["w_in"].reshape(depth, d, N_DEV, shard_in).transpose(2, 0, 1, 3)
    blocks_out = grads["w_out"].reshape(depth, N_DEV, shard_out, d).transpose(1, 0, 2, 3)
    r_in, r_out, r_small = _exchange_grads(blocks_in, blocks_out, _pack_small(grads))

    res_in = _sum_adamw(r_in.reshape(N_DEV, depth * d, shard_in), w_in.reshape(depth * d, shard_in),
                        m_w_in.reshape(depth * d, shard_in), v_w_in.reshape(depth * d, shard_in), "adamw_w_in")
    res_out = _sum_adamw(r_out.reshape(N_DEV, depth * shard_out, d), w_out.reshape(depth * shard_out, d),
                         m_w_out.reshape(depth * shard_out, d), v_w_out.reshape(depth * shard_out, d), "adamw_w_out")
    res_small = _sum_adamw(r_small, _pack_small(weights), _pack_small(mom_m), _pack_small(mom_v), "adamw_small")

    names = ("lower_bounds", "pre_norm_g", "w_in", "hgrn_norm_g", "fox_f_bias", "pool_w", "pool_scale", "w_out", "post_norm_g")
    outs = [loss, dx]
    for i in range(4):
        small = _unpack_small(res_small[i], weights)
        full = dict(small, w_in=res_in[i].reshape(w_in.shape), w_out=res_out[i].reshape(w_out.shape))
        outs += [full[k] for k in names]
    return tuple(outs)
```

```python
import functools

import jax
import jax.numpy as jnp
from jax import lax
from jax.experimental import pallas as pl
from jax.experimental.pallas import tpu as pltpu

F32, BF16 = jnp.float32, jnp.bfloat16
HI = lax.Precision.HIGHEST
MESH = pl.DeviceIdType.MESH
AXES = ("x", "y", "c")
N_DEV = 8

NORM_EPS = 1e-6
MASK_VALUE = -1e30
TINY = 1e-30
CHUNK = 64
SUB = 16
HGRN_W, POOL_W, FOX_W = 256, 256, 512
HEAD = 64
FOX_HEADS = 8
POOL_WINDOWS = (2, 4, 8, 16)
POOL_HALO = 16
MAIN_W = 3584
FC_PAD = 128
C_QA, C_FA, C_IA, C_GA, C_UB, C_GB, C_QC, C_KC, C_VC, C_GC = 0, 256, 512, 768, 1024, 1280, 1536, 2048, 2560, 3072
FOX_SCALE = HEAD ** -0.5

ADAM_LR, ADAM_B1, ADAM_B2, ADAM_EPS, ADAM_WD, ADAM_STEP = 0.001, 0.9, 0.999, 1e-08, 0.01, 10

VMEM_LIMIT = 56 * 1024 * 1024


def _pc(fn, name, **kw):
    return pl.pallas_call(fn, name=name, **kw)


def _params(**kw):
    return pltpu.CompilerParams(vmem_limit_bytes=VMEM_LIMIT, **kw)


def _dot(a, b):
    return jnp.dot(a, b, preferred_element_type=F32)


def _dot_nt(a, b):
    return lax.dot_general(a, b, (((1,), (1,)), ((), ())), preferred_element_type=F32)


def _dot_tn(a, b):
    return lax.dot_general(a, b, (((0,), (0,)), ((), ())), preferred_element_type=F32)


def _dot_hi(a, b):
    return jnp.dot(a, b, precision=HI, preferred_element_type=F32)


def _sigmoid(x):
    return 1.0 / (1.0 + jnp.exp(-x))


def _block_ones(n, dtype):
    r = lax.broadcasted_iota(jnp.int32, (n, n), 0) // HEAD
    c = lax.broadcasted_iota(jnp.int32, (n, n), 1) // HEAD
    return (r == c).astype(dtype)


def _sds(shape, dtype):
    return jax.ShapeDtypeStruct(shape, dtype)


def _in_proj_fwd(x, g_pre, w_main, w_fc, name):
    n, d = x.shape
    tm = min(256, n)

    def body(x_ref, g_ref, w_ref, wf_ref, proj_ref, fc_ref, h_ref):
        xv = x_ref[...]
        r = lax.rsqrt(jnp.mean(xv * xv, axis=-1, keepdims=True) + NORM_EPS)
        hb = (xv * r * g_ref[...]).astype(BF16)
        h_ref[...] = hb
        for j in range(0, MAIN_W, 512):
            proj_ref[:, j:j + 512] = _dot(hb, w_ref[:, j:j + 512])
        fc_ref[...] = _dot(hb, wf_ref[...])

    return _pc(
        body, name, grid=(n // tm,),
        in_specs=[pl.BlockSpec((tm, d), lambda i: (i, 0)), pl.BlockSpec((1, d), lambda i: (0, 0)),
                  pl.BlockSpec((d, MAIN_W), lambda i: (0, 0)), pl.BlockSpec((d, FC_PAD), lambda i: (0, 0))],
        out_specs=[pl.BlockSpec((tm, MAIN_W), lambda i: (i, 0)), pl.BlockSpec((tm, FC_PAD), lambda i: (i, 0)),
                   pl.BlockSpec((tm, d), lambda i: (i, 0))],
        out_shape=[_sds((n, MAIN_W), F32), _sds((n, FC_PAD), F32), _sds((n, d), BF16)],
        compiler_params=_params(),
    )(x, g_pre, w_main, w_fc)


def _fox_decay_fwd(fc, bias, nb, name):
    n = fc.shape[0]
    t = n // nb
    tt = min(256, t)
    nt = t // tt

    def body(fc_ref, b_ref, c_ref, carry):
        i = pl.program_id(1)

        @pl.when(i == 0)
        def _():
            carry[...] = jnp.zeros_like(carry)

        xv = fc_ref[...] + b_ref[...]
        lf = jnp.minimum(xv, 0.0) - jnp.log(1.0 + jnp.exp(-jnp.abs(xv)))
        r = lax.broadcasted_iota(jnp.int32, (tt, tt), 0)
        cc = lax.broadcasted_iota(jnp.int32, (tt, tt), 1)
        cs = _dot_hi((r >= cc).astype(F32), lf) + carry[...]
        c_ref[...] = cs
        carry[...] = cs[tt - 1:tt, :]

    return _pc(
        body, name, grid=(nb, nt),
        in_specs=[pl.BlockSpec((tt, FC_PAD), lambda b, i: (b * nt + i, 0)), pl.BlockSpec((1, FC_PAD), lambda b, i: (0, 0))],
        out_specs=pl.BlockSpec((tt, FC_PAD), lambda b, i: (b * nt + i, 0)),
        out_shape=_sds((n, FC_PAD), F32),
        scratch_shapes=[pltpu.VMEM((1, FC_PAD), F32)],
        compiler_params=_params(),
    )(fc, bias)


def _hgrn_gates(q, z, lb):
    sig = _sigmoid(z)
    sn = _sigmoid(-z)
    f = lb + (1.0 - lb) * sig
    g = jnp.log(jnp.maximum(f, TINY))
    k = (1.0 - lb) * sn
    sq = _sigmoid(q)
    return sig, sn, f, g, k, sq


def _sub_tri(n, lower):
    r = lax.broadcasted_iota(jnp.int32, (n, n), 0)
    c = lax.broadcasted_iota(jnp.int32, (n, n), 1)
    tri = (r >= c) if lower else (r <= c)
    return jnp.logical_and(r // SUB == c // SUB, tri).astype(F32)


def _hgrn_decays(qs, k, b):
    srow = lax.broadcasted_iota(jnp.int32, (SUB, HGRN_W), 0)
    es, ws = [], []
    for t in range(SUB):
        e = jnp.where(srow <= t, jnp.exp(b[t:t + 1, :] - b), 0.0)
        es.append(e)
        ws.append(e * (qs[t:t + 1, :] * k))
    return srow, es, ws


def _hgrn_state_step(st, k, v, b, bmask):
    bl = b[SUB - 1:SUB, :]
    ktil = k * jnp.exp(bl - b)
    return st * jnp.exp(bl) + _dot_tn(v.astype(BF16), ktil.astype(BF16)) * bmask


def _hgrn_sub_fwd(qs, k, v, b, st, ones_b, bmask):
    srow, _, ws = _hgrn_decays(qs, k, b)
    aexp = _dot(jnp.concatenate(ws, axis=0).astype(BF16), ones_b)
    o = _dot_nt((qs * jnp.exp(b)).astype(BF16), st.astype(BF16))
    for t in range(SUB):
        row = jnp.sum(aexp[t * SUB:(t + 1) * SUB, :] * v, axis=0, keepdims=True)
        o = o + jnp.where(srow == t, row, 0.0)
    return o, _hgrn_state_step(st, k, v, b, bmask)


def _hgrn_tile(t):
    return min(256, t)


def _hgrn_fwd(proj, lb, ones_b, nb, name):
    n = proj.shape[0]
    t = n // nb
    tt = _hgrn_tile(t)
    nt = t // tt
    ncs = tt // CHUNK
    w = HGRN_W

    def body(q_ref, z_ref, v_ref, lb_ref, ones_ref, o_ref, s0_ref, st_s, b_s, qs_s, k_s):
        @pl.when(pl.program_id(1) == 0)
        def _():
            st_s[...] = jnp.zeros_like(st_s)

        q = q_ref[...]
        _, _, _, g, k, sq = _hgrn_gates(q, z_ref[...], lb_ref[...])
        b_s[...] = _dot_hi(_sub_tri(tt, True), g)
        qs_s[...] = q * sq
        k_s[...] = k
        bmask = _block_ones(w, F32)
        ones_b = ones_ref[...]

        def chunk(c, carry):
            st = st_s[...]
            s0_ref[c] = st
            base = pl.multiple_of(c * CHUNK, CHUNK)
            for u in range(CHUNK // SUB):
                rows = pl.ds(base + u * SUB, SUB)
                o, st = _hgrn_sub_fwd(qs_s[rows, :], k_s[rows, :], v_ref[rows, :], b_s[rows, :], st, ones_b, bmask)
                o_ref[rows, :] = o
            st_s[...] = st
            return carry

        lax.fori_loop(0, ncs, chunk, 0)

    def col(j):
        return pl.BlockSpec((tt, w), lambda b, i: (b * nt + i, j))

    return _pc(
        body, name, grid=(nb, nt),
        in_specs=[col(C_QA // w), col(C_FA // w), col(C_IA // w), pl.BlockSpec((1, w), lambda b, i: (0, 0)),
                  pl.BlockSpec((w, w), lambda b, i: (0, 0))],
        out_specs=[pl.BlockSpec((tt, w), lambda b, i: (b * nt + i, 0)),
                   pl.BlockSpec((ncs, w, w), lambda b, i: (b * nt + i, 0, 0))],
        out_shape=[_sds((n, w), F32), _sds((n // CHUNK, w, w), F32)],
        scratch_shapes=[pltpu.VMEM((w, w), F32)] + [pltpu.VMEM((tt, w), F32)] * 3,
        compiler_params=_params(),
    )(proj, proj, proj, lb, ones_b)


def _pool_lane_windows():
    lane = lax.broadcasted_iota(jnp.int32, (1, POOL_W), 1) // HEAD
    wl = jnp.zeros((1, POOL_W), F32)
    for gi, win in enumerate(POOL_WINDOWS):
        wl = jnp.where(lane == gi, float(win), wl)
    return lane, wl


def _pool_select(lane, parts):
    out = parts[-1]
    for gi in range(len(parts) - 2, -1, -1):
        out = jnp.where(lane == gi, parts[gi], out)
    return out


def _pool_mix(u, halo, t0, tt):
    lane, wl = _pool_lane_windows()
    ext = jnp.concatenate([halo, u], axis=0)
    sums, cur, shift = [], ext, 1
    for _ in POOL_WINDOWS:
        cur = cur + pltpu.roll(cur, shift, axis=0)
        sums.append(cur[POOL_HALO:, :])
        shift *= 2
    tpos = (t0 + lax.broadcasted_iota(jnp.int32, (tt, POOL_W), 0)).astype(F32)
    cnt = jnp.minimum(tpos + 1.0, wl)
    return _pool_select(lane, sums) / cnt - u, cnt


def _pool_specs(tt, nt, nhb):
    cu, cg = C_UB // POOL_W, C_GB // POOL_W
    return [pl.BlockSpec((tt, POOL_W), lambda b, i: (b * nt + i, cu)),
            pl.BlockSpec((tt, POOL_W), lambda b, i: (b * nt + i, cg)),
            pl.BlockSpec((POOL_HALO, POOL_W), lambda b, i: (jnp.maximum((b * nt + i) * nhb - 1, 0), cu))]


def _pool_fwd(proj, wbd, scale, nb, name):
    n = proj.shape[0]
    t = n // nb
    tt = min(256, t)
    nt = t // tt
    nhb = tt // POOL_HALO

    def body(u_ref, g_ref, h_ref, w_ref, s_ref, o_ref):
        i = pl.program_id(1)
        halo = jnp.where(i == 0, 0.0, h_ref[...])
        pooled, _ = _pool_mix(u_ref[...], halo, i * tt, tt)
        gv = g_ref[...]
        o_ref[...] = _dot(pooled.astype(BF16), w_ref[...]) * s_ref[...] * (gv * _sigmoid(gv))

    return _pc(
        body, name, grid=(nb, nt),
        in_specs=_pool_specs(tt, nt, nhb) + [pl.BlockSpec((POOL_W, POOL_W), lambda b, i: (0, 0)),
                                             pl.BlockSpec((1, POOL_W), lambda b, i: (0, 0))],
        out_specs=pl.BlockSpec((tt, POOL_W), lambda b, i: (b * nt + i, 0)),
        out_shape=_sds((n, POOL_W), F32),
        compiler_params=_params(),
    )(proj, proj, proj, wbd, scale)


def _lane_lo():
    return lax.broadcasted_iota(jnp.int32, (1, 2 * HEAD), 1) < HEAD


def _put_col(tile, hh, colv):
    lane = lax.broadcasted_iota(jnp.int32, tile.shape, 1)
    return jnp.where(lane == hh, colv, tile)


def _fox_fwd(proj, c_col, c_row, nb, name):
    n = proj.shape[0]
    t = n // nb
    tb = min(256, t)
    nq = t // tb
    pw = 2 * HEAD

    def body(q_ref, k_ref, v_ref, cc_ref, cr_ref, o_ref, lse_ref, m_s, l_s, acc_s):
        qi, kj = pl.program_id(1), pl.program_id(2)

        @pl.when(kj == 0)
        def _():
            m_s[...] = jnp.full_like(m_s, -jnp.inf)
            l_s[...] = jnp.zeros_like(l_s)
            acc_s[...] = jnp.zeros_like(acc_s)

        @pl.when(kj <= qi)
        def _():
            keys = kj * tb + lax.broadcasted_iota(jnp.int32, (tb, tb), 0)
            qrys = qi * tb + lax.broadcasted_iota(jnp.int32, (tb, tb), 1)
            causal = qrys >= keys
            lo = _lane_lo()
            for p in range(FOX_HEADS // 2):
                sl = slice(p * pw, (p + 1) * pw)
                qp = q_ref[:, sl] * FOX_SCALE
                kp = k_ref[:, sl].astype(BF16)
                vt = v_ref[:, sl].T.astype(BF16)
                for h in range(2):
                    hh = 2 * p + h
                    lm = lo if h == 0 else jnp.logical_not(lo)
                    s = _dot_nt(kp, jnp.where(lm, qp, 0.0).astype(BF16))
                    s = s + (cr_ref[hh:hh + 1, :] - cc_ref[:, hh:hh + 1])
                    s = jnp.where(causal, s, MASK_VALUE)
                    m_prev = m_s[hh:hh + 1, :]
                    m_new = jnp.maximum(m_prev, jnp.max(s, axis=0, keepdims=True))
                    alpha = jnp.exp(m_prev - m_new)
                    pe = jnp.exp(s - m_new)
                    l_s[hh:hh + 1, :] = alpha * l_s[hh:hh + 1, :] + jnp.sum(pe, axis=0, keepdims=True)
                    m_s[hh:hh + 1, :] = m_new
                    rows = slice(hh * HEAD, (hh + 1) * HEAD)
                    acc_s[rows, :] = alpha * acc_s[rows, :] + _dot(vt[h * HEAD:(h + 1) * HEAD, :], pe.astype(BF16))

        @pl.when(kj == qi)
        def _():
            for p in range(FOX_HEADS // 2):
                parts = [acc_s[hh * HEAD:(hh + 1) * HEAD, :] * (1.0 / l_s[hh:hh + 1, :]) for hh in (2 * p, 2 * p + 1)]
                o_ref[:, p * pw:(p + 1) * pw] = jnp.concatenate(parts, axis=0).T
            lse_ref[...] = m_s[...] + jnp.log(l_s[...])

    def qspec(j):
        return pl.BlockSpec((tb, FOX_W), lambda b, qi, kj: (b * nq + qi, j))

    def kspec(wd, j):
        return pl.BlockSpec((tb, wd), lambda b, qi, kj: (b * nq + jnp.minimum(kj, qi), j))

    qrow = pl.BlockSpec((None, FOX_HEADS, tb), lambda b, qi, kj: (b, 0, qi))
    return _pc(
        body, name, grid=(nb, nq, nq),
        in_specs=[qspec(C_QC // FOX_W), kspec(FOX_W, C_KC // FOX_W), kspec(FOX_W, C_VC // FOX_W), kspec(FC_PAD, 0), qrow],
        out_specs=[qspec(0), qrow],
        out_shape=[_sds((n, FOX_W), F32), _sds((nb, FOX_HEADS, t), F32)],
        scratch_shapes=[pltpu.VMEM((FOX_HEADS, tb), F32), pltpu.VMEM((FOX_HEADS, tb), F32),
                        pltpu.VMEM((FOX_W, tb), F32)],
        compiler_params=_params(),
    )(proj, proj, proj, c_col, c_row)


def _head_mean(x, ones_f):
    return _dot_hi(x, ones_f) * (1.0 / HEAD)


def _merge_fwd(x, proj, o_h, o_b, o_c, gh, w_out, g_post, name):
    n, d = x.shape
    tm = min(256, n)

    def body(x_ref, ga_ref, gc_ref, oh_ref, ob_ref, oc_ref, gh_ref, w_ref, gp_ref, xo_ref, mix_ref, y_ref):
        oh = oh_ref[...]
        ones_f = _block_ones(HGRN_W, F32)
        na = oh * lax.rsqrt(_head_mean(oh * oh, ones_f) + NORM_EPS) * gh_ref[...]
        ga, gc = ga_ref[...], gc_ref[...]
        mixed = jnp.concatenate([na * (ga * _sigmoid(ga)), ob_ref[...], oc_ref[...] * (gc * _sigmoid(gc))], axis=1)
        mb = mixed.astype(BF16)
        mix_ref[...] = mb
        y = _dot(mb, w_ref[...])
        y_ref[...] = y
        xo_ref[...] = x_ref[...] + y * lax.rsqrt(jnp.mean(y * y, axis=-1, keepdims=True) + NORM_EPS) * gp_ref[...]

    def row(wd, j=0):
        return pl.BlockSpec((tm, wd), lambda i: (i, j))

    def full(a, b):
        return pl.BlockSpec((a, b), lambda i: (0, 0))

    return _pc(
        body, name, grid=(n // tm,),
        in_specs=[row(d), row(HGRN_W, C_GA // HGRN_W), row(FOX_W, C_GC // FOX_W), row(HGRN_W), row(POOL_W), row(FOX_W),
                  full(1, HGRN_W), full(d, d), full(1, d)],
        out_specs=[row(d), row(d), row(d)],
        out_shape=[_sds((n, d), F32), _sds((n, d), BF16), _sds((n, d), F32)],
        compiler_params=_params(),
    )(x, proj, proj, o_h, o_b, o_c, gh, w_out, g_post)


def _loss_fwd_bwd(x, target, name):
    n, d = x.shape
    tm = min(512, n)

    def body(x_ref, t_ref, dx_ref, sq_ref):
        @pl.when(pl.program_id(0) == 0)
        def _():
            sq_ref[...] = jnp.zeros_like(sq_ref)

        e = x_ref[...] - t_ref[...]
        dx_ref[...] = e * (1.0 / d)
        sq_ref[...] += jnp.sum(e * e, axis=0, keepdims=True)

    return _pc(
        body, name, grid=(n // tm,),
        in_specs=[pl.BlockSpec((tm, d), lambda i: (i, 0))] * 2,
        out_specs=[pl.BlockSpec((tm, d), lambda i: (i, 0)), pl.BlockSpec((1, d), lambda i: (0, 0))],
        out_shape=[_sds((n, d), F32), _sds((1, d), F32)],
        compiler_params=_params(),
    )(x, target)


def _rms_bwd(dy_scaled, xhat, r):
    return r * (dy_scaled - xhat * jnp.mean(dy_scaled * xhat, axis=-1, keepdims=True))


def _merge_bwd(dxo, y, g_post, w_out_t, name):
    n, d = y.shape
    tm = min(256, n)

    def body(dx_ref, y_ref, gp_ref, wt_ref, dy_ref, dm_ref, dgp_ref):
        @pl.when(pl.program_id(0) == 0)
        def _():
            dgp_ref[...] = jnp.zeros_like(dgp_ref)

        yv, dxv = y_ref[...], dx_ref[...]
        r = lax.rsqrt(jnp.mean(yv * yv, axis=-1, keepdims=True) + NORM_EPS)
        yh = yv * r
        dgp_ref[...] += jnp.sum(dxv * yh, axis=0, keepdims=True)
        dyb = _rms_bwd(dxv * gp_ref[...], yh, r).astype(BF16)
        dy_ref[...] = dyb
        dm_ref[...] = _dot(dyb, wt_ref[...])

    row = pl.BlockSpec((tm, d), lambda i: (i, 0))
    return _pc(
        body, name, grid=(n // tm,),
        in_specs=[row, row, pl.BlockSpec((1, d), lambda i: (0, 0)), pl.BlockSpec((d, d), lambda i: (0, 0))],
        out_specs=[row, row, pl.BlockSpec((1, d), lambda i: (0, 0))],
        out_shape=[_sds((n, d), BF16), _sds((n, d), F32), _sds((1, d), F32)],
        compiler_params=_params(),
    )(dxo, y, g_post, w_out_t)


def _grad_matmul(a, b, name):
    n, ka = a.shape
    kb = b.shape[1]
    ta = min(256, ka)
    tb = min(512, kb)

    def body(a_ref, b_ref, o_ref):
        o_ref[...] = _dot_tn(a_ref[...], b_ref[...])

    return _pc(
        body, name, grid=(ka // ta, kb // tb),
        in_specs=[pl.BlockSpec((n, ta), lambda i, j: (0, i)), pl.BlockSpec((n, tb), lambda i, j: (0, j))],
        out_specs=pl.BlockSpec((ta, tb), lambda i, j: (i, j)),
        out_shape=_sds((ka, kb), F32),
        compiler_params=_params(),
    )(a, b)


def _hgrn_sub_bwd(qs, k, v, b, do, s0, ds1, ones_b, bmask):
    bl = b[SUB - 1:SUB, :]
    eb, ebl, ekt = jnp.exp(b), jnp.exp(bl), jnp.exp(bl - b)
    qe, ktil = qs * eb, k * ekt
    ds1b, dob = ds1.astype(BF16), do.astype(BF16)
    dv = _dot_nt(ktil.astype(BF16), ds1b)
    dqe = _dot(dob, s0.astype(BF16))
    dktil = _dot(v.astype(BF16), ds1b)
    dbl = jnp.sum(dktil * ktil, axis=0, keepdims=True) + ebl * jnp.sum(s0 * ds1, axis=0, keepdims=True)
    ds0 = ds1 * ebl + _dot_tn(dob, qe.astype(BF16)) * bmask
    srow, es, ws = _hgrn_decays(qs, k, b)
    aexp = _dot(jnp.concatenate(ws, axis=0).astype(BF16), ones_b)
    gexp = _dot(jnp.concatenate([do[t:t + 1, :] * v for t in range(SUB)], axis=0).astype(BF16), ones_b)
    dq = dqe * eb
    dk = dktil * ekt
    for t in range(SUB):
        sl = slice(t * SUB, (t + 1) * SUB)
        ge = gexp[sl, :] * es[t]
        dq = dq + jnp.where(srow == t, jnp.sum(ge * k, axis=0, keepdims=True), 0.0)
        dk = dk + ge * qs[t:t + 1, :]
        dv = dv + aexp[sl, :] * do[t:t + 1, :]
    return dq, dk, dv, dbl, ds0


def _hgrn_bwd(dmix, proj, o_h, s0, gh, lb, ones_b, nb, name):
    n = proj.shape[0]
    t = n // nb
    tt = _hgrn_tile(t)
    nt = t // tt
    ncs = tt // CHUNK
    nsub = CHUNK // SUB
    w = HGRN_W

    def body(dm_ref, q_ref, z_ref, v_ref, ga_ref, oh_ref, s0_ref, gh_ref, lb_ref, ones_ref,
             dp_ref, dgh_ref, dlb_ref, ds_s, ss_s, b_s, qs_s, k_s, do_s, dq_s, dk_s, dv_s, dbl_s):
        first = jnp.logical_and(pl.program_id(0) == 0, pl.program_id(1) == 0)

        @pl.when(first)
        def _():
            dgh_ref[...] = jnp.zeros_like(dgh_ref)
            dlb_ref[...] = jnp.zeros_like(dlb_ref)

        @pl.when(pl.program_id(1) == 0)
        def _():
            ds_s[...] = jnp.zeros_like(ds_s)

        ones_b = ones_ref[...]
        ones_f = ones_b.astype(F32)
        bmask = _block_ones(w, F32)
        lbv, ghv = lb_ref[...], gh_ref[...]
        oh, ga, dm = oh_ref[...], ga_ref[...], dm_ref[...]
        rn = lax.rsqrt(_head_mean(oh * oh, ones_f) + NORM_EPS)
        nh = oh * rn
        sga = _sigmoid(ga)
        dp_ref[:, 3 * w:4 * w] = (dm * nh * ghv * (sga * (1.0 + ga * (1.0 - sga)))).astype(BF16)
        dn = dm * (ga * sga)
        dgh_ref[...] += jnp.sum(dn * nh, axis=0, keepdims=True)
        dn = dn * ghv
        do_s[...] = rn * (dn - nh * _head_mean(dn * nh, ones_f))
        q = q_ref[...]
        sig, sn, f, g, k, sq = _hgrn_gates(q, z_ref[...], lbv)
        qs = q * sq
        b_s[...] = _dot_hi(_sub_tri(tt, True), g)
        qs_s[...] = qs
        k_s[...] = k

        def chunk(cc, carry):
            c = ncs - 1 - cc
            base = pl.multiple_of(c * CHUNK, CHUNK)
            st = s0_ref[c]
            for u in range(nsub):
                ss_s[u] = st
                if u < nsub - 1:
                    rows = pl.ds(base + u * SUB, SUB)
                    st = _hgrn_state_step(st, k_s[rows, :], v_ref[rows, :], b_s[rows, :], bmask)
            ds = ds_s[...]
            for u in reversed(range(nsub)):
                rows = pl.ds(base + u * SUB, SUB)
                dq, dk, dv, dbl, ds = _hgrn_sub_bwd(qs_s[rows, :], k_s[rows, :], v_ref[rows, :], b_s[rows, :],
                                                    do_s[rows, :], ss_s[u], ds, ones_b, bmask)
                dq_s[rows, :] = dq
                dk_s[rows, :] = dk
                dv_s[rows, :] = dv
                dbl_s[rows, :] = jnp.broadcast_to(dbl, (SUB, w))
            ds_s[...] = ds
            return carry

        lax.fori_loop(0, ncs, chunk, 0)
        dqs, dk = dq_s[...], dk_s[...]
        dg = _dot_hi(_sub_tri(tt, False), qs * dqs - k * dk) + dbl_s[...]
        dfz = jnp.where(f > TINY, dg / jnp.maximum(f, TINY), 0.0)
        dlb_ref[...] += jnp.sum(dfz * (1.0 - sig) - dk * sn, axis=0, keepdims=True)
        dp_ref[:, 0:w] = (dqs * (sq * (1.0 + q * (1.0 - sq)))).astype(BF16)
        dp_ref[:, w:2 * w] = ((dfz - dk) * (1.0 - lbv) * sig * sn).astype(BF16)
        dp_ref[:, 2 * w:3 * w] = dv_s[...].astype(BF16)

    def rv(b, i):
        return b * nt + (nt - 1 - i)

    def col(j):
        return pl.BlockSpec((tt, w), lambda b, i: (rv(b, i), j))

    def full(a, bb):
        return pl.BlockSpec((a, bb), lambda b, i: (0, 0))

    return _pc(
        body, name, grid=(nb, nt),
        in_specs=[col(0), col(C_QA // w), col(C_FA // w), col(C_IA // w), col(C_GA // w), col(0),
                  pl.BlockSpec((ncs, w, w), lambda b, i: (rv(b, i), 0, 0)), full(1, w), full(1, w), full(w, w)],
        out_specs=[pl.BlockSpec((tt, 4 * w), lambda b, i: (rv(b, i), 0)), full(1, w), full(1, w)],
        out_shape=[_sds((n, 4 * w), BF16), _sds((1, w), F32), _sds((1, w), F32)],
        scratch_shapes=[pltpu.VMEM((w, w), F32), pltpu.VMEM((nsub, w, w), F32)] + [pltpu.VMEM((tt, w), F32)] * 8,
        compiler_params=_params(),
    )(dmix, proj, proj, proj, proj, o_h, s0, gh, lb, ones_b)


def _pool_bwd(dmix, proj, wbd, wbd_t, scale, nb, name):
    n = proj.shape[0]
    t = n // nb
    tt = min(256, t)
    nt = t // tt
    nhb = tt // POOL_HALO
    cu, cg, cm = C_UB // POOL_W, C_GB // POOL_W, HGRN_W // POOL_W

    def body(u_ref, g_ref, h_ref, dm_ref, gn_ref, dmn_ref, w_ref, wt_ref, s_ref, dp_ref, dw_ref, ds_ref):
        i = pl.program_id(1)
        first = jnp.logical_and(pl.program_id(0) == 0, i == 0)

        @pl.when(first)
        def _():
            dw_ref[...] = jnp.zeros_like(dw_ref)
            ds_ref[...] = jnp.zeros_like(ds_ref)

        sc = s_ref[...]
        halo = jnp.where(i == 0, 0.0, h_ref[...])
        pooled, cnt = _pool_mix(u_ref[...], halo, i * tt, tt)
        pb = pooled.astype(BF16)
        pre = _dot(pb, w_ref[...])
        gv, dm = g_ref[...], dm_ref[...]
        sg = _sigmoid(gv)
        silu = gv * sg
        dgb = dm * pre * sc * (sg * (1.0 + gv * (1.0 - sg)))
        ds_ref[...] += jnp.sum(dm * pre * silu, axis=0, keepdims=True)
        dpre = (dm * sc * silu).astype(BF16)
        dw_ref[...] += _dot_tn(pb, dpre)
        dpool = _dot(dpre, wt_ref[...])
        gn = gn_ref[...]
        dpre_n = (dmn_ref[...] * sc * (gn * _sigmoid(gn))).astype(BF16)
        dpool_n = jnp.where(i == nt - 1, 0.0, _dot(dpre_n, wt_ref[...]))
        lane, wl = _pool_lane_windows()
        tpos_n = ((i + 1) * tt + lax.broadcasted_iota(jnp.int32, (POOL_HALO, POOL_W), 0)).astype(F32)
        ext = jnp.concatenate([dpool / cnt, dpool_n / jnp.minimum(tpos_n + 1.0, wl)], axis=0)
        rows = tt + POOL_HALO
        sums, cur, shift = [], ext, 1
        for _ in POOL_WINDOWS:
            cur = cur + pltpu.roll(cur, rows - shift, axis=0)
            sums.append(cur[:tt, :])
            shift *= 2
        du = _pool_select(lane, sums) - dpool
        dp_ref[...] = jnp.concatenate([du, dgb], axis=1).astype(BF16)

    def nxt(b, i):
        return jnp.minimum((b * nt + i + 1) * nhb, n // POOL_HALO - 1)

    return _pc(
        body, name, grid=(nb, nt),
        in_specs=_pool_specs(tt, nt, nhb) + [
            pl.BlockSpec((tt, POOL_W), lambda b, i: (b * nt + i, cm)),
            pl.BlockSpec((POOL_HALO, POOL_W), lambda b, i: (nxt(b, i), cg)),
            pl.BlockSpec((POOL_HALO, POOL_W), lambda b, i: (nxt(b, i), cm)),
            pl.BlockSpec((POOL_W, POOL_W), lambda b, i: (0, 0)), pl.BlockSpec((POOL_W, POOL_W), lambda b, i: (0, 0)),
            pl.BlockSpec((1, POOL_W), lambda b, i: (0, 0))],
        out_specs=[pl.BlockSpec((tt, 2 * POOL_W), lambda b, i: (b * nt + i, 0)),
                   pl.BlockSpec((POOL_W, POOL_W), lambda b, i: (0, 0)), pl.BlockSpec((1, POOL_W), lambda b, i: (0, 0))],
        out_shape=[_sds((n, 2 * POOL_W), BF16), _sds((POOL_W, POOL_W), F32), _sds((1, POOL_W), F32)],
        compiler_params=_params(),
    )(proj, proj, proj, dmix, proj, dmix, wbd, wbd_t, scale)


def _fox_gate_bwd(dmix, proj, o_c, name):
    n = proj.shape[0]
    tm = min(256, n)

    def body(dm_ref, gc_ref, oc_ref, da_ref, dg_ref, dl_ref):
        dm, gc, oc = dm_ref[...], gc_ref[...], oc_ref[...]
        sg = _sigmoid(gc)
        da = dm * (gc * sg)
        da_ref[...] = da.astype(BF16)
        dg_ref[...] = (dm * oc * (sg * (1.0 + gc * (1.0 - sg)))).astype(BF16)
        r = lax.broadcasted_iota(jnp.int32, (FOX_W, FC_PAD), 0) // HEAD
        c = lax.broadcasted_iota(jnp.int32, (FOX_W, FC_PAD), 1)
        dl_ref[...] = _dot_hi(da * oc, (r == c).astype(F32))

    def row(wd, j=0):
        return pl.BlockSpec((tm, wd), lambda i: (i, j))

    return _pc(
        body, name, grid=(n // tm,),
        in_specs=[row(FOX_W, (HGRN_W + POOL_W) // FOX_W), row(FOX_W, C_GC // FOX_W), row(FOX_W)],
        out_specs=[row(FOX_W), row(FOX_W), row(FC_PAD)],
        out_shape=[_sds((n, FOX_W), BF16), _sds((n, FOX_W), BF16), _sds((n, FC_PAD), F32)],
        compiler_params=_params(),
    )(dmix, proj, o_c)


def _fox_bwd(proj, da, c_col, c_row, lse_row, delta_row, nb, name):
    n = proj.shape[0]
    t = n // nb
    tb = min(256, t)
    nq = t // tb
    pw = 2 * HEAD

    def body(q_ref, k_ref, v_ref, da_ref, cc_ref, cr_ref, lse_ref, dl_ref,
             dq_ref, dk_ref, dv_ref, dck_ref, dcq_ref, dq_s, dk_s, dv_s, dck_s, dcq_s):
        kj, qi = pl.program_id(1), pl.program_id(2)

        @pl.when(jnp.logical_and(kj == 0, qi == 0))
        def _():
            dq_s[...] = jnp.zeros_like(dq_s)
            dcq_s[...] = jnp.zeros_like(dcq_s)

        @pl.when(qi == 0)
        def _():
            dk_s[...] = jnp.zeros_like(dk_s)
            dv_s[...] = jnp.zeros_like(dv_s)
            dck_s[...] = jnp.zeros_like(dck_s)

        @pl.when(qi >= kj)
        def _():
            keys = kj * tb + lax.broadcasted_iota(jnp.int32, (tb, tb), 0)
            qrys = qi * tb + lax.broadcasted_iota(jnp.int32, (tb, tb), 1)
            causal = qrys >= keys
            lo = _lane_lo()
            dck = dck_s[...]
            for p in range(FOX_HEADS // 2):
                sl = slice(p * pw, (p + 1) * pw)
                qp = q_ref[:, sl] * FOX_SCALE
                kf = k_ref[:, sl]
                kp = kf.astype(BF16)
                kt = kf.T.astype(BF16)
                vp = v_ref[:, sl].astype(BF16)
                dap = da_ref[:, sl]
                dk, dv = dk_s[:, sl], dv_s[:, sl]
                for h in range(2):
                    hh = 2 * p + h
                    lm = lo if h == 0 else jnp.logical_not(lo)
                    qm = jnp.where(lm, qp, 0.0).astype(BF16)
                    dam = jnp.where(lm, dap, jnp.zeros_like(dap))
                    s = _dot_nt(kp, qm) + (cr_ref[hh:hh + 1, :] - cc_ref[:, hh:hh + 1])
                    pe = jnp.where(causal, jnp.exp(s - lse_ref[hh:hh + 1, :]), 0.0)
                    dp = _dot_nt(vp, dam)
                    ds = pe * (dp - dl_ref[hh:hh + 1, :])
                    dsb = ds.astype(BF16)
                    dv = dv + _dot(pe.astype(BF16), dam)
                    dk = dk + _dot(dsb, qm)
                    rows = slice(hh * HEAD, (hh + 1) * HEAD)
                    dq_s[qi, rows, :] += _dot(kt[h * HEAD:(h + 1) * HEAD, :], dsb)
                    dck = dck - _put_col(jnp.zeros_like(dck), hh, jnp.sum(ds, axis=1, keepdims=True))
                    dcq_s[qi, hh:hh + 1, :] += jnp.sum(ds, axis=0, keepdims=True)
                dk_s[:, sl] = dk
                dv_s[:, sl] = dv
            dck_s[...] = dck

        @pl.when(qi == nq - 1)
        def _():
            dk_ref[...] = dk_s[...].astype(BF16)
            dv_ref[...] = dv_s[...].astype(BF16)
            dck_ref[...] = dck_s[...]

        @pl.when(jnp.logical_and(kj == nq - 1, qi == nq - 1))
        def _():
            for j in range(nq):
                dq_ref[j * tb:(j + 1) * tb, :] = (dq_s[j].T * FOX_SCALE).astype(BF16)
                dcq_ref[:, j * tb:(j + 1) * tb] = dcq_s[j]

    def kspec(wd, j=0):
        return pl.BlockSpec((tb, wd), lambda b, kj, qi: (b * nq + kj, j))

    def qspec(wd, j=0):
        return pl.BlockSpec((tb, wd), lambda b, kj, qi: (b * nq + jnp.maximum(qi, kj), j))

    def qrow():
        return pl.BlockSpec((None, FOX_HEADS, tb), lambda b, kj, qi: (b, 0, jnp.maximum(qi, kj)))

    return _pc(
        body, name, grid=(nb, nq, nq),
        in_specs=[qspec(FOX_W, C_QC // FOX_W), kspec(FOX_W, C_KC // FOX_W), kspec(FOX_W, C_VC // FOX_W), qspec(FOX_W),
                  kspec(FC_PAD), qrow(), qrow(), qrow()],
        out_specs=[pl.BlockSpec((t, FOX_W), lambda b, kj, qi: (b, 0)), kspec(FOX_W), kspec(FOX_W), kspec(FC_PAD),
                   pl.BlockSpec((None, FOX_HEADS, t), lambda b, kj, qi: (b, 0, 0))],
        out_shape=[_sds((n, FOX_W), BF16), _sds((n, FOX_W), BF16), _sds((n, FOX_W), BF16), _sds((n, FC_PAD), F32),
                   _sds((nb, FOX_HEADS, t), F32)],
        scratch_shapes=[pltpu.VMEM((nq, FOX_W, tb), F32), pltpu.VMEM((tb, FOX_W), F32), pltpu.VMEM((tb, FOX_W), F32),
                        pltpu.VMEM((tb, FC_PAD), F32), pltpu.VMEM((nq, FOX_HEADS, tb), F32)],
        compiler_params=_params(),
    )(proj, proj, proj, da, c_col, c_row, lse_row, delta_row)


def _fox_decay_bwd(dc_q, dc_k, fc, bias, nb, name):
    n = fc.shape[0]
    t = n // nb
    tt = min(256, t)
    nt = t // tt

    def body(dcq_ref, dck_ref, fc_ref, b_ref, dfc_ref, db_ref, carry):
        i = pl.program_id(1)
        first = jnp.logical_and(pl.program_id(0) == 0, i == 0)

        @pl.when(first)
        def _():
            db_ref[...] = jnp.zeros_like(db_ref)

        @pl.when(i == 0)
        def _():
            carry[...] = jnp.zeros_like(carry)

        r = lax.broadcasted_iota(jnp.int32, (tt, tt), 0)
        cc = lax.broadcasted_iota(jnp.int32, (tt, tt), 1)
        dlf = _dot_hi((r <= cc).astype(F32), dcq_ref[...] + dck_ref[...]) + carry[...]
        carry[...] = dlf[0:1, :]
        dfc = dlf * _sigmoid(-(fc_ref[...] + b_ref[...]))
        dfc_ref[...] = dfc.astype(BF16)
        db_ref[...] += jnp.sum(dfc, axis=0, keepdims=True)

    def row():
        return pl.BlockSpec((tt, FC_PAD), lambda b, i: (b * nt + (nt - 1 - i), 0))

    return _pc(
        body, name, grid=(nb, nt),
        in_specs=[row(), row(), row(), pl.BlockSpec((1, FC_PAD), lambda b, i: (0, 0))],
        out_specs=[row(), pl.BlockSpec((1, FC_PAD), lambda b, i: (0, 0))],
        out_shape=[_sds((n, FC_PAD), BF16), _sds((1, FC_PAD), F32)],
        scratch_shapes=[pltpu.VMEM((1, FC_PAD), F32)],
        compiler_params=_params(),
    )(dc_q, dc_k, fc, bias)


def _in_proj_bwd(pieces, w_main_t, w_fc_t, x, g_pre, dxo, name):
    n, d = x.shape
    tm = min(256, n)
    widths = [p.shape[1] for p, _ in pieces]
    offs = [o for _, o in pieces]
    np_ = len(pieces)

    def body(*refs):
        p_refs = refs[:np_]
        wt_ref, wf_ref, x_ref, g_ref, dxo_ref, dx_ref, dg_ref = refs[np_:]

        @pl.when(pl.program_id(0) == 0)
        def _():
            dg_ref[...] = jnp.zeros_like(dg_ref)

        dh = _dot(p_refs[-1][...], wf_ref[...])
        for pr, wd, off in zip(p_refs[:-1], widths[:-1], offs[:-1]):
            for j in range(0, wd, 512):
                jw = min(512, wd - j)
                dh = dh + _dot(pr[:, j:j + jw], wt_ref[off + j:off + j + jw, :])
        xv = x_ref[...]
        r = lax.rsqrt(jnp.mean(xv * xv, axis=-1, keepdims=True) + NORM_EPS)
        xh = xv * r
        dg_ref[...] += jnp.sum(dh * xh, axis=0, keepdims=True)
        dx_ref[...] = dxo_ref[...] + _rms_bwd(dh * g_ref[...], xh, r)

    row = pl.BlockSpec((tm, d), lambda i: (i, 0))
    return _pc(
        body, name, grid=(n // tm,),
        in_specs=[pl.BlockSpec((tm, wd), lambda i: (i, 0)) for wd in widths] + [
            pl.BlockSpec((MAIN_W, d), lambda i: (0, 0)), pl.BlockSpec((FC_PAD, d), lambda i: (0, 0)),
            row, pl.BlockSpec((1, d), lambda i: (0, 0)), row],
        out_specs=[row, pl.BlockSpec((1, d), lambda i: (0, 0))],
        out_shape=[_sds((n, d), F32), _sds((1, d), F32)],
        compiler_params=_params(),
    )(*[p for p, _ in pieces], w_main_t, w_fc_t, x, g_pre, dxo)


def _lower_bound_table(lower_bounds, name):
    depth, w = lower_bounds.shape

    def body(lb_ref, o_ref):
        v = lb_ref[...]
        e = jnp.exp(v - jnp.max(v, axis=0, keepdims=True))
        p = e / jnp.sum(e, axis=0, keepdims=True)
        acc = jnp.zeros((1, w), F32)
        for l in range(depth):
            acc = acc + p[l:l + 1, :]
            o_ref[l:l + 1, :] = acc - p[0:1, :]

    return _pc(body, name, out_shape=_sds((depth, w), F32))(lower_bounds)


def _lower_bound_bwd(lower_bounds, dlbs, name):
    depth, w = lower_bounds.shape

    def body(lb_ref, d_ref, o_ref):
        v, dl = lb_ref[...], d_ref[...]
        e = jnp.exp(v - jnp.max(v, axis=0, keepdims=True))
        p = e / jnp.sum(e, axis=0, keepdims=True)
        tot = jnp.sum(dl, axis=0, keepdims=True)
        rows, tail = [], tot
        for l in range(depth):
            rows.append(tail - tot if l == 0 else tail)
            tail = tail - dl[l:l + 1, :]
        dp = jnp.concatenate(rows, axis=0)
        o_ref[...] = p * (dp - jnp.sum(p * dp, axis=0, keepdims=True))

    return _pc(body, name, out_shape=_sds((depth, w), F32))(lower_bounds, dlbs)


def _place():
    x, y, c = lax.axis_index("x"), lax.axis_index("y"), lax.axis_index("c")
    return x, y, c


def _gather_weights(w_in_b, w_out_b):
    arrays = (w_in_b, w_out_b)
    na = len(arrays)

    def body(*refs):
        ins, outs = refs[:na], refs[na:2 * na]
        send_sems, recv_sems, local_sems = refs[2 * na:]
        x, y, c = _place()
        me, sibling = (x, y, c), (x, y, 1 - c)
        chips = [(1 - x, y), (x, 1 - y), (1 - x, 1 - y)]

        def slot(a, px, py, pc):
            return outs[a].at[4 * px + 2 * py + pc]

        def copy(a, k, block, to, own=False):
            return pltpu.make_async_remote_copy(
                src_ref=ins[a] if own else slot(a, *block), dst_ref=slot(a, *block),
                send_sem=send_sems.at[a * 7 + k], recv_sem=recv_sems.at[a * 7 + k],
                device_id=to, device_id_type=MESH)

        mine = [pltpu.make_async_copy(ins[a], slot(a, *me), local_sems.at[a]) for a in range(na)]
        for cp in mine:
            cp.start()
        first = []
        for a in range(na):
            first.append(copy(a, 0, me, sibling, own=True))
            first += [copy(a, 1 + j, me, (*chip, c), own=True) for j, chip in enumerate(chips)]
        for cp in first:
            cp.start()
        passed = []
        for j, chip in enumerate(chips):
            for a in range(na):
                copy(a, 1 + j, (*chip, c), me).wait_recv()
                fw = copy(a, 4 + j, (*chip, c), sibling)
                fw.start()
                passed.append(fw)
        for a in range(na):
            copy(a, 0, sibling, me).wait_recv()
            for j, chip in enumerate(chips):
                copy(a, 4 + j, (*chip, 1 - c), me).wait_recv()
        for cp in first + passed:
            cp.wait_send()
        for cp in mine:
            cp.wait()

    any_spec = pl.BlockSpec(memory_space=pl.ANY)
    return _pc(
        body, "gather_weights",
        in_specs=[any_spec] * na, out_specs=[any_spec] * na,
        out_shape=[_sds((N_DEV,) + a.shape, a.dtype) for a in arrays],
        scratch_shapes=[pltpu.SemaphoreType.DMA((7 * na,)), pltpu.SemaphoreType.DMA((7 * na,)),
                        pltpu.SemaphoreType.DMA((na,))],
    )(*arrays)


def _exchange_grads(blocks_in, blocks_out, small):
    arrays = (blocks_in, blocks_out)
    na = len(arrays) + 1

    def body(*refs):
        ins, outs = refs[:na], refs[na:2 * na]
        send_sems, recv_sems, local_sems = refs[2 * na:]
        x, y, c = _place()
        me = 4 * x + 2 * y + c

        def src(a, j):
            return ins[a] if a == na - 1 else ins[a].at[j]

        local = [pltpu.make_async_copy(src(a, me), outs[a].at[me], local_sems.at[a]) for a in range(na)]
        for cp in local:
            cp.start()
        sends, recvs = [], []
        for k in range(1, N_DEV):
            px = 1 - x if k & 4 else x
            py = 1 - y if k & 2 else y
            pc = 1 - c if k & 1 else c
            peer = 4 * px + 2 * py + pc
            for a in range(na):
                s = (k - 1) * na + a
                sends.append(pltpu.make_async_remote_copy(
                    src_ref=src(a, peer), dst_ref=outs[a].at[me], send_sem=send_sems.at[s], recv_sem=recv_sems.at[s],
                    device_id=(px, py, pc), device_id_type=MESH))
                recvs.append(pltpu.make_async_remote_copy(
                    src_ref=src(a, peer), dst_ref=outs[a].at[peer], send_sem=send_sems.at[s], recv_sem=recv_sems.at[s],
                    device_id=(px, py, pc), device_id_type=MESH))
        for cp in sends:
            cp.start()
        for cp in recvs:
            cp.wait_recv()
        for cp in sends:
            cp.wait_send()
        for cp in local:
            cp.wait()

    any_spec = pl.BlockSpec(memory_space=pl.ANY)
    ns = (N_DEV - 1) * na
    return _pc(
        body, "exchange_grads",
        in_specs=[any_spec] * na, out_specs=[any_spec] * na,
        out_shape=[_sds(a.shape, a.dtype) for a in arrays] + [_sds((N_DEV,) + small.shape, small.dtype)],
        scratch_shapes=[pltpu.SemaphoreType.DMA((ns,)), pltpu.SemaphoreType.DMA((ns,)), pltpu.SemaphoreType.DMA((na,))],
    )(blocks_in, blocks_out, small)


def _sum_adamw(parts, w, m, v, name):
    r, c = w.shape
    tr = 256 if r % 256 == 0 else r

    def body(p_ref, w_ref, m_ref, v_ref, g_ref, d_ref, mo_ref, vo_ref):
        g = p_ref[0]
        for j in range(1, N_DEV):
            g = g + p_ref[j]
        wv = w_ref[...]
        mn = ADAM_B1 * m_ref[...] + (1.0 - ADAM_B1) * g
        vn = ADAM_B2 * v_ref[...] + (1.0 - ADAM_B2) * (g * g)
        m_hat = mn / (1.0 - ADAM_B1 ** ADAM_STEP)
        v_hat = vn / (1.0 - ADAM_B2 ** ADAM_STEP)
        g_ref[...] = g
        d_ref[...] = -ADAM_LR * (m_hat / (jnp.sqrt(v_hat) + ADAM_EPS) + ADAM_WD * wv)
        mo_ref[...] = mn
        vo_ref[...] = vn

    row = pl.BlockSpec((tr, c), lambda i: (i, 0))
    return _pc(
        body, name, grid=(r // tr,),
        in_specs=[pl.BlockSpec((N_DEV, tr, c), lambda i: (0, i, 0)), row, row, row],
        out_specs=[row] * 4,
        out_shape=[_sds((r, c), F32)] * 4,
        compiler_params=_params(),
    )(parts, w, m, v)


SMALL = ("lower_bounds", "pre_norm_g", "hgrn_norm_g", "fox_f_bias", "pool_w", "pool_scale", "post_norm_g")
SMALL_LANES = 128


def _pack_small(tree):
    flat = jnp.concatenate([tree[k].reshape(-1) for k in SMALL])
    rows = -(-flat.shape[0] // (8 * SMALL_LANES)) * 8
    return jnp.pad(flat, (0, rows * SMALL_LANES - flat.shape[0])).reshape(rows, SMALL_LANES)


def _unpack_small(packed, like):
    flat, out, off = packed.reshape(-1), {}, 0
    for k in SMALL:
        size = like[k].size
        out[k] = flat[off:off + size].reshape(like[k].shape)
        off += size
    return out


def _block_diag(pw):
    g = pw.shape[0]
    eye = jnp.eye(g, dtype=pw.dtype)
    return (eye[:, None, :, None] * pw[:, :, None, :]).reshape(g * HEAD, g * HEAD)


def _local_step(x, target, lbs, weights, layer_w):
    nb, t, d = x.shape
    n = nb * t
    depth = len(layer_w)
    ones_b = _block_ones(HGRN_W, BF16)
    xs = [x.reshape(n, d)]
    saved = []
    for l in range(depth):
        w_main, w_fc, _, _, w_out, _ = layer_w[l]
        g_pre = weights["pre_norm_g"][l:l + 1]
        bias = jnp.pad(weights["fox_f_bias"][l:l + 1], ((0, 0), (0, FC_PAD - FOX_HEADS)))
        wbd = _block_diag(weights["pool_w"][l]).astype(BF16)
        proj, fc, h = _in_proj_fwd(xs[l], g_pre, w_main, w_fc, f"in_proj_fwd_{l}")
        c_col = _fox_decay_fwd(fc, bias, nb, f"fox_decay_fwd_{l}")
        c_row = c_col.reshape(nb, t, FC_PAD)[:, :, :FOX_HEADS].transpose(0, 2, 1)
        o_h, s0 = _hgrn_fwd(proj, lbs[l:l + 1], ones_b, nb, f"hgrn_fwd_{l}")
        o_b = _pool_fwd(proj, wbd, weights["pool_scale"][l:l + 1], nb, f"pool_fwd_{l}")
        o_c, lse = _fox_fwd(proj, c_col, c_row, nb, f"fox_fwd_{l}")
        x_next, mixed, y = _merge_fwd(xs[l], proj, o_h, o_b, o_c, weights["hgrn_norm_g"][l:l + 1], w_out,
                                      weights["post_norm_g"][l:l + 1], f"merge_fwd_{l}")
        xs.append(x_next)
        saved.append((proj, fc, h, c_col, c_row, o_h, s0, o_c, lse, mixed, y, bias, wbd))

    dx, sq = _loss_fwd_bwd(xs[depth], target.reshape(n, d), "loss")
    loss = 0.5 * jnp.sum(sq) / d

    grads = {k: [None] * depth for k in ("pre_norm_g", "w_in", "hgrn_norm_g", "fox_f_bias", "pool_w", "pool_scale",
                                         "w_out", "post_norm_g", "lbs")}
    for l in reversed(range(depth)):
        proj, fc, h, c_col, c_row, o_h, s0, o_c, lse, mixed, y, bias, wbd = saved[l]
        _, _, w_main_t, w_fc_t, _, w_out_t = layer_w[l]
        dy, dmix, dgp = _merge_bwd(dx, y, weights["post_norm_g"][l:l + 1], w_out_t, f"merge_bwd_{l}")
        grads["post_norm_g"][l] = dgp[0]
        grads["w_out"][l] = _grad_matmul(mixed, dy, f"w_out_grad_{l}")
        d_a, dgh, dlb = _hgrn_bwd(dmix, proj, o_h, s0, weights["hgrn_norm_g"][l:l + 1], lbs[l:l + 1], ones_b, nb,
                                  f"hgrn_bwd_{l}")
        grads["hgrn_norm_g"][l] = dgh[0]
        grads["lbs"][l] = dlb[0]
        d_b, dwbd, dps = _pool_bwd(dmix, proj, wbd, wbd.T, weights["pool_scale"][l:l + 1], nb, f"pool_bwd_{l}")
        grads["pool_w"][l] = jnp.stack([dwbd[g * HEAD:(g + 1) * HEAD, g * HEAD:(g + 1) * HEAD]
                                        for g in range(len(POOL_WINDOWS))])
        grads["pool_scale"][l] = dps[0]
        da, d_gc, delta = _fox_gate_bwd(dmix, proj, o_c, f"fox_gate_bwd_{l}")

        def rows_of(a):
            return a.reshape(nb, t, FC_PAD)[:, :, :FOX_HEADS].transpose(0, 2, 1)

        d_qc, d_kc, d_vc, dc_k, dc_q = _fox_bwd(proj, da, c_col, c_row, lse, rows_of(delta), nb, f"fox_bwd_{l}")
        dc_q = jnp.pad(dc_q.transpose(0, 2, 1).reshape(n, FOX_HEADS), ((0, 0), (0, FC_PAD - FOX_HEADS)))
        d_fc, dbias = _fox_decay_bwd(dc_q, dc_k, fc, bias, nb, f"fox_decay_bwd_{l}")
        grads["fox_f_bias"][l] = dbias[0, :FOX_HEADS]
        pieces = [(d_a, C_QA), (d_b, C_UB), (d_qc, C_QC), (d_kc, C_KC), (d_vc, C_VC), (d_gc, C_GC), (d_fc, None)]
        gw = [_grad_matmul(h, p, f"w_in_grad_{l}_{i}") for i, (p, _) in enumerate(pieces)]
        grads["w_in"][l] = jnp.concatenate(gw[:-1] + [gw[-1][:, :FOX_HEADS]], axis=1)
        dx, dgpre = _in_proj_bwd(pieces, w_main_t, w_fc_t, xs[l], weights["pre_norm_g"][l:l + 1], dx, f"in_proj_bwd_{l}")
        grads["pre_norm_g"][l] = dgpre[0]

    out = {k: jnp.stack(v) for k, v in grads.items()}
    out["lower_bounds"] = _lower_bound_bwd(weights["lower_bounds"], out.pop("lbs"), "lower_bound_bwd")
    return loss, dx.reshape(nb, t, d), out


def kernel(x, lower_bounds, pre_norm_g, w_in, hgrn_norm_g, fox_f_bias, pool_w, pool_scale, w_out, post_norm_g, loss_target, m_lower_bounds, m_pre_norm_g, m_w_in, m_hgrn_norm_g, m_fox_f_bias, m_pool_w, m_pool_scale, m_w_out, m_post_norm_g, v_lower_bounds, v_pre_norm_g, v_w_in, v_hgrn_norm_g, v_fox_f_bias, v_pool_w, v_pool_scale, v_w_out, v_post_norm_g):
    weights = dict(lower_bounds=lower_bounds, pre_norm_g=pre_norm_g, hgrn_norm_g=hgrn_norm_g, fox_f_bias=fox_f_bias,
                   pool_w=pool_w, pool_scale=pool_scale, post_norm_g=post_norm_g)
    mom_m = dict(lower_bounds=m_lower_bounds, pre_norm_g=m_pre_norm_g, hgrn_norm_g=m_hgrn_norm_g, fox_f_bias=m_fox_f_bias,
                 pool_w=m_pool_w, pool_scale=m_pool_scale, post_norm_g=m_post_norm_g)
    mom_v = dict(lower_bounds=v_lower_bounds, pre_norm_g=v_pre_norm_g, hgrn_norm_g=v_hgrn_norm_g, fox_f_bias=v_fox_f_bias,
                 pool_w=v_pool_w, pool_scale=v_pool_scale, post_norm_g=v_post_norm_g)
    depth, d, shard_in = w_in.shape
    shard_out = w_out.shape[1]
    in_w = N_DEV * shard_in

    g_in, g_out = _gather_weights(w_in.astype(BF16), w_out.astype(BF16))
    layer_w = []
    for l in range(depth):
        full_in = g_in[:, l].transpose(1, 0, 2).reshape(d, in_w)
        w_main = full_in[:, :MAIN_W]
        w_fc = jnp.pad(full_in[:, MAIN_W:], ((0, 0), (0, FC_PAD - (in_w - MAIN_W))))
        full_out = g_out[:, l].reshape(N_DEV * shard_out, d)
        layer_w.append((w_main, w_fc, w_main.T, w_fc.T, full_out, full_out.T))

    lbs = _lower_bound_table(lower_bounds, "lower_bound_table")
    loss, dx, grads = _local_step(x, loss_target, lbs, weights, layer_w)
    loss = lax.psum(loss, AXES)

    blocks_in = grads["w_in"].reshape(depth, d, N_DEV, shard_in).transpose(2, 0, 1, 3)
    blocks_out = grads["w_out"].reshape(depth, N_DEV, shard_out, d).transpose(1, 0, 2, 3)
    r_in, r_out, r_small = _exchange_grads(blocks_in, blocks_out, _pack_small(grads))

    res_in = _sum_adamw(r_in.reshape(N_DEV, depth * d, shard_in), w_in.reshape(depth * d, shard_in),
                        m_w_in.reshape(depth * d, shard_in), v_w_in.reshape(depth * d, shard_in), "adamw_w_in")
    res_out = _sum_adamw(r_out.reshape(N_DEV, depth * shard_out, d), w_out.reshape(depth * shard_out, d),
                         m_w_out.reshape(depth * shard_out, d), v_w_out.reshape(depth * shard_out, d), "adamw_w_out")
    res_small = _sum_adamw(r_small, _pack_small(weights), _pack_small(mom_m), _pack_small(mom_v), "adamw_small")

    names = ("lower_bounds", "pre_norm_g", "w_in", "hgrn_norm_g", "fox_f_bias", "pool_w", "pool_scale", "w_out", "post_norm_g")
    outs = [loss, dx]
    for i in range(4):
        small = _unpack_small(res_small[i], weights)
        full = dict(small, w_in=res_in[i].reshape(w_in.shape), w_out=res_out[i].reshape(w_out.shape))
        outs += [full[k] for k in names]
    return tuple(outs)
```

```python
import functools

import jax
import jax.numpy as jnp
from jax import lax
from jax.experimental import pallas as pl
from jax.experimental.pallas import tpu as pltpu

F32, BF16 = jnp.float32, jnp.bfloat16
HI = lax.Precision.HIGHEST
MESH = pl.DeviceIdType.MESH
AXES = ("x", "y", "c")
N_DEV = 8

NORM_EPS = 1e-6
MASK_VALUE = -1e30
TINY = 1e-30
CHUNK = 64
SUB = 16
HGRN_W, POOL_W, FOX_W = 256, 256, 512
HEAD = 64
FOX_HEADS = 8
POOL_WINDOWS = (2, 4, 8, 16)
POOL_HALO = 16
MAIN_W = 3584
FC_PAD = 128
C_QA, C_FA, C_IA, C_GA, C_UB, C_GB, C_QC, C_KC, C_VC, C_GC = 0, 256, 512, 768, 1024, 1280, 1536, 2048, 2560, 3072
FOX_SCALE = HEAD ** -0.5

ADAM_LR, ADAM_B1, ADAM_B2, ADAM_EPS, ADAM_WD, ADAM_STEP = 0.001, 0.9, 0.999, 1e-08, 0.01, 10

VMEM_LIMIT = 56 * 1024 * 1024


def _pc(fn, name, **kw):
    return pl.pallas_call(fn, name=name, **kw)


def _params(**kw):
    return pltpu.CompilerParams(vmem_limit_bytes=VMEM_LIMIT, **kw)


class _Rider:
    def __init__(self, inputs, out_shapes, n_sems, n_local, plan):
        self.inputs, self.out_shapes, self.n_sems, self.n_local, self.plan = list(inputs), list(out_shapes), n_sems, n_local, plan

    def start(self, ins, outs, *sems):
        sends, _, locs = self.plan(ins, outs, *sems)
        for cp in locs + sends:
            cp.start()

    def wait(self, ins, outs, *sems):
        sends, recvs, locs = self.plan(ins, outs, *sems)
        for cp in recvs:
            cp.wait_recv()
        for cp in sends:
            cp.wait_send()
        for cp in locs:
            cp.wait()

    def sem_shapes(self):
        return [pltpu.SemaphoreType.DMA((self.n_sems,)), pltpu.SemaphoreType.DMA((self.n_sems,)),
                pltpu.SemaphoreType.DMA((self.n_local,))]


def _call(body, name, args, rider=None, *, grid, in_specs, out_specs, out_shape, scratch_shapes=(), **kw):
    if rider is None:
        res = _pc(body, name, grid=grid, in_specs=in_specs, out_specs=out_specs, out_shape=out_shape,
                  scratch_shapes=list(scratch_shapes), **kw)(*args)
        return res, None
    n_in, n_out, n_scr = len(in_specs), len(out_specs), len(scratch_shapes)
    n_rin, n_rout = len(rider.inputs), len(rider.out_shapes)

    def ridden(*refs):
        ins, refs = refs[:n_in], refs[n_in:]
        rins, refs = refs[:n_rin], refs[n_rin:]
        outs, refs = refs[:n_out], refs[n_out:]
        routs, refs = refs[:n_rout], refs[n_rout:]
        scr, sems = refs[:n_scr], refs[n_scr:]
        first = functools.reduce(jnp.logical_and, [pl.program_id(a) == 0 for a in range(len(grid))])
        last = functools.reduce(jnp.logical_and, [pl.program_id(a) == g - 1 for a, g in enumerate(grid)])

        @pl.when(first)
        def _():
            rider.start(rins, routs, *sems)

        body(*ins, *outs, *scr)

        @pl.when(last)
        def _():
            rider.wait(rins, routs, *sems)

    any_spec = pl.BlockSpec(memory_space=pl.ANY)
    res = _pc(ridden, name, grid=grid, in_specs=list(in_specs) + [any_spec] * n_rin,
              out_specs=list(out_specs) + [any_spec] * n_rout, out_shape=list(out_shape) + rider.out_shapes,
              scratch_shapes=list(scratch_shapes) + rider.sem_shapes(), **kw)(*args, *rider.inputs)
    return res[:n_out], res[n_out:]


def _run_rider(rider, name):
    n_rin = len(rider.inputs)

    def body(*refs):
        ins, outs, sems = refs[:n_rin], refs[n_rin:n_rin + len(rider.out_shapes)], refs[n_rin + len(rider.out_shapes):]
        rider.start(ins, outs, *sems)
        rider.wait(ins, outs, *sems)

    any_spec = pl.BlockSpec(memory_space=pl.ANY)
    return _pc(body, name, in_specs=[any_spec] * n_rin, out_specs=[any_spec] * len(rider.out_shapes),
               out_shape=rider.out_shapes, scratch_shapes=rider.sem_shapes())(*rider.inputs)


def _dot(a, b):
    return jnp.dot(a, b, preferred_element_type=F32)


def _dot_nt(a, b):
    return lax.dot_general(a, b, (((1,), (1,)), ((), ())), preferred_element_type=F32)


def _dot_tn(a, b):
    return lax.dot_general(a, b, (((0,), (0,)), ((), ())), preferred_element_type=F32)


def _dot_hi(a, b):
    return jnp.dot(a, b, precision=HI, preferred_element_type=F32)


def _sigmoid(x):
    return 1.0 / (1.0 + jnp.exp(-x))


def _block_ones(n, dtype):
    r = lax.broadcasted_iota(jnp.int32, (n, n), 0) // HEAD
    c = lax.broadcasted_iota(jnp.int32, (n, n), 1) // HEAD
    return (r == c).astype(dtype)


def _sds(shape, dtype):
    return jax.ShapeDtypeStruct(shape, dtype)


def _in_proj_fwd(x, g_pre, w_main, w_fc, name):
    n, d = x.shape
    tm = min(256, n)

    def body(x_ref, g_ref, w_ref, wf_ref, proj_ref, fc_ref, h_ref):
        xv = x_ref[...]
        r = lax.rsqrt(jnp.mean(xv * xv, axis=-1, keepdims=True) + NORM_EPS)
        hb = (xv * r * g_ref[...]).astype(BF16)
        h_ref[...] = hb
        for j in range(0, MAIN_W, 512):
            proj_ref[:, j:j + 512] = _dot(hb, w_ref[:, j:j + 512])
        fc_ref[...] = _dot(hb, wf_ref[...])

    return _pc(
        body, name, grid=(n // tm,),
        in_specs=[pl.BlockSpec((tm, d), lambda i: (i, 0)), pl.BlockSpec((1, d), lambda i: (0, 0)),
                  pl.BlockSpec((d, MAIN_W), lambda i: (0, 0)), pl.BlockSpec((d, FC_PAD), lambda i: (0, 0))],
        out_specs=[pl.BlockSpec((tm, MAIN_W), lambda i: (i, 0)), pl.BlockSpec((tm, FC_PAD), lambda i: (i, 0)),
                   pl.BlockSpec((tm, d), lambda i: (i, 0))],
        out_shape=[_sds((n, MAIN_W), F32), _sds((n, FC_PAD), F32), _sds((n, d), BF16)],
        compiler_params=_params(),
    )(x, g_pre, w_main, w_fc)


def _fox_decay_fwd(fc, bias, nb, name):
    n = fc.shape[0]
    t = n // nb
    tt = min(256, t)
    nt = t // tt

    def body(fc_ref, b_ref, c_ref, carry):
        i = pl.program_id(1)

        @pl.when(i == 0)
        def _():
            carry[...] = jnp.zeros_like(carry)

        xv = fc_ref[...] + b_ref[...]
        lf = jnp.minimum(xv, 0.0) - jnp.log(1.0 + jnp.exp(-jnp.abs(xv)))
        r = lax.broadcasted_iota(jnp.int32, (tt, tt), 0)
        cc = lax.broadcasted_iota(jnp.int32, (tt, tt), 1)
        cs = _dot_hi((r >= cc).astype(F32), lf) + carry[...]
        c_ref[...] = cs
        carry[...] = cs[tt - 1:tt, :]

    return _pc(
        body, name, grid=(nb, nt),
        in_specs=[pl.BlockSpec((tt, FC_PAD), lambda b, i: (b * nt + i, 0)), pl.BlockSpec((1, FC_PAD), lambda b, i: (0, 0))],
        out_specs=pl.BlockSpec((tt, FC_PAD), lambda b, i: (b * nt + i, 0)),
        out_shape=_sds((n, FC_PAD), F32),
        scratch_shapes=[pltpu.VMEM((1, FC_PAD), F32)],
        compiler_params=_params(),
    )(fc, bias)


def _hgrn_gates(q, z, lb):
    sig = _sigmoid(z)
    sn = _sigmoid(-z)
    f = lb + (1.0 - lb) * sig
    g = jnp.log(jnp.maximum(f, TINY))
    k = (1.0 - lb) * sn
    sq = _sigmoid(q)
    return sig, sn, f, g, k, sq


def _sub_tri(n, lower):
    r = lax.broadcasted_iota(jnp.int32, (n, n), 0)
    c = lax.broadcasted_iota(jnp.int32, (n, n), 1)
    tri = (r >= c) if lower else (r <= c)
    return jnp.logical_and(r // SUB == c // SUB, tri).astype(F32)


def _hgrn_decays(qs, k, b):
    srow = lax.broadcasted_iota(jnp.int32, (SUB, HGRN_W), 0)
    es, ws = [], []
    for t in range(SUB):
        e = jnp.where(srow <= t, jnp.exp(b[t:t + 1, :] - b), 0.0)
        es.append(e)
        ws.append(e * (qs[t:t + 1, :] * k))
    return srow, es, ws


def _hgrn_state_step(st, k, v, b, bmask):
    bl = b[SUB - 1:SUB, :]
    ktil = k * jnp.exp(bl - b)
    return st * jnp.exp(bl) + _dot_tn(v.astype(BF16), ktil.astype(BF16)) * bmask


def _hgrn_sub_fwd(qs, k, v, b, st, ones_b, bmask):
    srow, _, ws = _hgrn_decays(qs, k, b)
    aexp = _dot(jnp.concatenate(ws, axis=0).astype(BF16), ones_b)
    o = _dot_nt((qs * jnp.exp(b)).astype(BF16), st.astype(BF16))
    for t in range(SUB):
        row = jnp.sum(aexp[t * SUB:(t + 1) * SUB, :] * v, axis=0, keepdims=True)
        o = o + jnp.where(srow == t, row, 0.0)
    return o, _hgrn_state_step(st, k, v, b, bmask)


def _hgrn_tile(t):
    return min(256, t)


def _hgrn_fwd(proj, lb, ones_b, nb, name):
    n = proj.shape[0]
    t = n // nb
    tt = _hgrn_tile(t)
    nt = t // tt
    ncs = tt // CHUNK
    w = HGRN_W

    def body(q_ref, z_ref, v_ref, lb_ref, ones_ref, o_ref, s0_ref, st_s, b_s, qs_s, k_s):
        @pl.when(pl.program_id(1) == 0)
        def _():
            st_s[...] = jnp.zeros_like(st_s)

        q = q_ref[...]
        _, _, _, g, k, sq = _hgrn_gates(q, z_ref[...], lb_ref[...])
        b_s[...] = _dot_hi(_sub_tri(tt, True), g)
        qs_s[...] = q * sq
        k_s[...] = k
        bmask = _block_ones(w, F32)
        ones_b = ones_ref[...]

        def chunk(c, carry):
            st = st_s[...]
            s0_ref[c] = st
            base = pl.multiple_of(c * CHUNK, CHUNK)
            for u in range(CHUNK // SUB):
                rows = pl.ds(base + u * SUB, SUB)
                o, st = _hgrn_sub_fwd(qs_s[rows, :], k_s[rows, :], v_ref[rows, :], b_s[rows, :], st, ones_b, bmask)
                o_ref[rows, :] = o
            st_s[...] = st
            return carry

        lax.fori_loop(0, ncs, chunk, 0)

    def col(j):
        return pl.BlockSpec((tt, w), lambda b, i: (b * nt + i, j))

    return _pc(
        body, name, grid=(nb, nt),
        in_specs=[col(C_QA // w), col(C_FA // w), col(C_IA // w), pl.BlockSpec((1, w), lambda b, i: (0, 0)),
                  pl.BlockSpec((w, w), lambda b, i: (0, 0))],
        out_specs=[pl.BlockSpec((tt, w), lambda b, i: (b * nt + i, 0)),
                   pl.BlockSpec((ncs, w, w), lambda b, i: (b * nt + i, 0, 0))],
        out_shape=[_sds((n, w), F32), _sds((n // CHUNK, w, w), F32)],
        scratch_shapes=[pltpu.VMEM((w, w), F32)] + [pltpu.VMEM((tt, w), F32)] * 3,
        compiler_params=_params(),
    )(proj, proj, proj, lb, ones_b)


def _pool_lane_windows():
    lane = lax.broadcasted_iota(jnp.int32, (1, POOL_W), 1) // HEAD
    wl = jnp.zeros((1, POOL_W), F32)
    for gi, win in enumerate(POOL_WINDOWS):
        wl = jnp.where(lane == gi, float(win), wl)
    return lane, wl


def _pool_select(lane, parts):
    out = parts[-1]
    for gi in range(len(parts) - 2, -1, -1):
        out = jnp.where(lane == gi, parts[gi], out)
    return out


def _pool_mix(u, halo, t0, tt):
    lane, wl = _pool_lane_windows()
    ext = jnp.concatenate([halo, u], axis=0)
    sums, cur, shift = [], ext, 1
    for _ in POOL_WINDOWS:
        cur = cur + pltpu.roll(cur, shift, axis=0)
        sums.append(cur[POOL_HALO:, :])
        shift *= 2
    tpos = (t0 + lax.broadcasted_iota(jnp.int32, (tt, POOL_W), 0)).astype(F32)
    cnt = jnp.minimum(tpos + 1.0, wl)
    return _pool_select(lane, sums) / cnt - u, cnt


def _pool_specs(tt, nt, nhb):
    cu, cg = C_UB // POOL_W, C_GB // POOL_W
    return [pl.BlockSpec((tt, POOL_W), lambda b, i: (b * nt + i, cu)),
            pl.BlockSpec((tt, POOL_W), lambda b, i: (b * nt + i, cg)),
            pl.BlockSpec((POOL_HALO, POOL_W), lambda b, i: (jnp.maximum((b * nt + i) * nhb - 1, 0), cu))]


def _pool_fwd(proj, wbd, scale, nb, name):
    n = proj.shape[0]
    t = n // nb
    tt = min(256, t)
    nt = t // tt
    nhb = tt // POOL_HALO

    def body(u_ref, g_ref, h_ref, w_ref, s_ref, o_ref):
        i = pl.program_id(1)
        halo = jnp.where(i == 0, 0.0, h_ref[...])
        pooled, _ = _pool_mix(u_ref[...], halo, i * tt, tt)
        gv = g_ref[...]
        o_ref[...] = _dot(pooled.astype(BF16), w_ref[...]) * s_ref[...] * (gv * _sigmoid(gv))

    return _pc(
        body, name, grid=(nb, nt),
        in_specs=_pool_specs(tt, nt, nhb) + [pl.BlockSpec((POOL_W, POOL_W), lambda b, i: (0, 0)),
                                             pl.BlockSpec((1, POOL_W), lambda b, i: (0, 0))],
        out_specs=pl.BlockSpec((tt, POOL_W), lambda b, i: (b * nt + i, 0)),
        out_shape=_sds((n, POOL_W), F32),
        compiler_params=_params(),
    )(proj, proj, proj, wbd, scale)


def _lane_lo():
    return lax.broadcasted_iota(jnp.int32, (1, 2 * HEAD), 1) < HEAD


def _put_col(tile, hh, colv):
    lane = lax.broadcasted_iota(jnp.int32, tile.shape, 1)
    return jnp.where(lane == hh, colv, tile)


def _fox_fwd(proj, c_col, c_row, nb, name, rider=None):
    n = proj.shape[0]
    t = n // nb
    tb = min(256, t)
    nq = t // tb
    pw = 2 * HEAD

    def body(q_ref, k_ref, v_ref, cc_ref, cr_ref, o_ref, lse_ref, m_s, l_s, acc_s):
        qi, kj = pl.program_id(1), pl.program_id(2)

        @pl.when(kj == 0)
        def _():
            m_s[...] = jnp.full_like(m_s, -jnp.inf)
            l_s[...] = jnp.zeros_like(l_s)
            acc_s[...] = jnp.zeros_like(acc_s)

        @pl.when(kj <= qi)
        def _():
            keys = kj * tb + lax.broadcasted_iota(jnp.int32, (tb, tb), 0)
            qrys = qi * tb + lax.broadcasted_iota(jnp.int32, (tb, tb), 1)
            causal = qrys >= keys
            lo = _lane_lo()
            for p in range(FOX_HEADS // 2):
                sl = slice(p * pw, (p + 1) * pw)
                qp = q_ref[:, sl] * FOX_SCALE
                kp = k_ref[:, sl].astype(BF16)
                vt = v_ref[:, sl].T.astype(BF16)
                for h in range(2):
                    hh = 2 * p + h
                    lm = lo if h == 0 else jnp.logical_not(lo)
                    s = _dot_nt(kp, jnp.where(lm, qp, 0.0).astype(BF16))
                    s = s + (cr_ref[hh:hh + 1, :] - cc_ref[:, hh:hh + 1])
                    s = jnp.where(causal, s, MASK_VALUE)
                    m_prev = m_s[hh:hh + 1, :]
                    m_new = jnp.maximum(m_prev, jnp.max(s, axis=0, keepdims=True))
                    alpha = jnp.exp(m_prev - m_new)
                    pe = jnp.exp(s - m_new)
                    l_s[hh:hh + 1, :] = alpha * l_s[hh:hh + 1, :] + jnp.sum(pe, axis=0, keepdims=True)
                    m_s[hh:hh + 1, :] = m_new
                    rows = slice(hh * HEAD, (hh + 1) * HEAD)
                    acc_s[rows, :] = alpha * acc_s[rows, :] + _dot(vt[h * HEAD:(h + 1) * HEAD, :], pe.astype(BF16))

        @pl.when(kj == qi)
        def _():
            for p in range(FOX_HEADS // 2):
                parts = [acc_s[hh * HEAD:(hh + 1) * HEAD, :] * (1.0 / l_s[hh:hh + 1, :]) for hh in (2 * p, 2 * p + 1)]
                o_ref[:, p * pw:(p + 1) * pw] = jnp.concatenate(parts, axis=0).T
            lse_ref[...] = m_s[...] + jnp.log(l_s[...])

    def qspec(j):
        return pl.BlockSpec((tb, FOX_W), lambda b, qi, kj: (b * nq + qi, j))

    def kspec(wd, j):
        return pl.BlockSpec((tb, wd), lambda b, qi, kj: (b * nq + jnp.minimum(kj, qi), j))

    qrow = pl.BlockSpec((None, FOX_HEADS, tb), lambda b, qi, kj: (b, 0, qi))
    return _call(
        body, name, (proj, proj, proj, c_col, c_row), rider, grid=(nb, nq, nq),
        in_specs=[qspec(C_QC // FOX_W), kspec(FOX_W, C_KC // FOX_W), kspec(FOX_W, C_VC // FOX_W), kspec(FC_PAD, 0), qrow],
        out_specs=[qspec(0), qrow],
        out_shape=[_sds((n, FOX_W), F32), _sds((nb, FOX_HEADS, t), F32)],
        scratch_shapes=[pltpu.VMEM((FOX_HEADS, tb), F32), pltpu.VMEM((FOX_HEADS, tb), F32),
                        pltpu.VMEM((FOX_W, tb), F32)],
        compiler_params=_params(),
    )


def _head_mean(x, ones_f):
    return _dot_hi(x, ones_f) * (1.0 / HEAD)


def _merge_fwd(x, proj, o_h, o_b, o_c, gh, w_out, g_post, name):
    n, d = x.shape
    tm = min(256, n)

    def body(x_ref, ga_ref, gc_ref, oh_ref, ob_ref, oc_ref, gh_ref, w_ref, gp_ref, xo_ref, mix_ref, y_ref):
        oh = oh_ref[...]
        ones_f = _block_ones(HGRN_W, F32)
        na = oh * lax.rsqrt(_head_mean(oh * oh, ones_f) + NORM_EPS) * gh_ref[...]
        ga, gc = ga_ref[...], gc_ref[...]
        mixed = jnp.concatenate([na * (ga * _sigmoid(ga)), ob_ref[...], oc_ref[...] * (gc * _sigmoid(gc))], axis=1)
        mb = mixed.astype(BF16)
        mix_ref[...] = mb
        y = _dot(mb, w_ref[...])
        y_ref[...] = y
        xo_ref[...] = x_ref[...] + y * lax.rsqrt(jnp.mean(y * y, axis=-1, keepdims=True) + NORM_EPS) * gp_ref[...]

    def row(wd, j=0):
        return pl.BlockSpec((tm, wd), lambda i: (i, j))

    def full(a, b):
        return pl.BlockSpec((a, b), lambda i: (0, 0))

    return _pc(
        body, name, grid=(n // tm,),
        in_specs=[row(d), row(HGRN_W, C_GA // HGRN_W), row(FOX_W, C_GC // FOX_W), row(HGRN_W), row(POOL_W), row(FOX_W),
                  full(1, HGRN_W), full(d, d), full(1, d)],
        out_specs=[row(d), row(d), row(d)],
        out_shape=[_sds((n, d), F32), _sds((n, d), BF16), _sds((n, d), F32)],
        compiler_params=_params(),
    )(x, proj, proj, o_h, o_b, o_c, gh, w_out, g_post)


def _loss_fwd_bwd(x, target, name):
    n, d = x.shape
    tm = min(512, n)

    def body(x_ref, t_ref, dx_ref, sq_ref):
        @pl.when(pl.program_id(0) == 0)
        def _():
            sq_ref[...] = jnp.zeros_like(sq_ref)

        e = x_ref[...] - t_ref[...]
        dx_ref[...] = e * (1.0 / d)
        sq_ref[...] += jnp.sum(e * e, axis=0, keepdims=True)

    return _pc(
        body, name, grid=(n // tm,),
        in_specs=[pl.BlockSpec((tm, d), lambda i: (i, 0))] * 2,
        out_specs=[pl.BlockSpec((tm, d), lambda i: (i, 0)), pl.BlockSpec((1, d), lambda i: (0, 0))],
        out_shape=[_sds((n, d), F32), _sds((1, d), F32)],
        compiler_params=_params(),
    )(x, target)


def _rms_bwd(dy_scaled, xhat, r):
    return r * (dy_scaled - xhat * jnp.mean(dy_scaled * xhat, axis=-1, keepdims=True))


def _merge_bwd(dxo, y, g_post, w_out_t, name):
    n, d = y.shape
    tm = min(256, n)

    def body(dx_ref, y_ref, gp_ref, wt_ref, dy_ref, dm_ref, dgp_ref):
        @pl.when(pl.program_id(0) == 0)
        def _():
            dgp_ref[...] = jnp.zeros_like(dgp_ref)

        yv, dxv = y_ref[...], dx_ref[...]
        r = lax.rsqrt(jnp.mean(yv * yv, axis=-1, keepdims=True) + NORM_EPS)
        yh = yv * r
        dgp_ref[...] += jnp.sum(dxv * yh, axis=0, keepdims=True)
        dyb = _rms_bwd(dxv * gp_ref[...], yh, r).astype(BF16)
        dy_ref[...] = dyb
        dm_ref[...] = _dot(dyb, wt_ref[...])

    row = pl.BlockSpec((tm, d), lambda i: (i, 0))
    return _pc(
        body, name, grid=(n // tm,),
        in_specs=[row, row, pl.BlockSpec((1, d), lambda i: (0, 0)), pl.BlockSpec((d, d), lambda i: (0, 0))],
        out_specs=[row, row, pl.BlockSpec((1, d), lambda i: (0, 0))],
        out_shape=[_sds((n, d), BF16), _sds((n, d), F32), _sds((1, d), F32)],
        compiler_params=_params(),
    )(dxo, y, g_post, w_out_t)


def _grad_matmul(a, b, name):
    n, ka = a.shape
    kb = b.shape[1]
    ta = min(256, ka)
    tb = min(512, kb)

    def body(a_ref, b_ref, o_ref):
        o_ref[...] = _dot_tn(a_ref[...], b_ref[...])

    return _pc(
        body, name, grid=(ka // ta, kb // tb),
        in_specs=[pl.BlockSpec((n, ta), lambda i, j: (0, i)), pl.BlockSpec((n, tb), lambda i, j: (0, j))],
        out_specs=pl.BlockSpec((ta, tb), lambda i, j: (i, j)),
        out_shape=_sds((ka, kb), F32),
        compiler_params=_params(),
    )(a, b)


def _hgrn_sub_bwd(qs, k, v, b, do, s0, ds1, ones_b, bmask):
    bl = b[SUB - 1:SUB, :]
    eb, ebl, ekt = jnp.exp(b), jnp.exp(bl), jnp.exp(bl - b)
    qe, ktil = qs * eb, k * ekt
    ds1b, dob = ds1.astype(BF16), do.astype(BF16)
    dv = _dot_nt(ktil.astype(BF16), ds1b)
    dqe = _dot(dob, s0.astype(BF16))
    dktil = _dot(v.astype(BF16), ds1b)
    dbl = jnp.sum(dktil * ktil, axis=0, keepdims=True) + ebl * jnp.sum(s0 * ds1, axis=0, keepdims=True)
    ds0 = ds1 * ebl + _dot_tn(dob, qe.astype(BF16)) * bmask
    srow, es, ws = _hgrn_decays(qs, k, b)
    aexp = _dot(jnp.concatenate(ws, axis=0).astype(BF16), ones_b)
    gexp = _dot(jnp.concatenate([do[t:t + 1, :] * v for t in range(SUB)], axis=0).astype(BF16), ones_b)
    dq = dqe * eb
    dk = dktil * ekt
    for t in range(SUB):
        sl = slice(t * SUB, (t + 1) * SUB)
        ge = gexp[sl, :] * es[t]
        dq = dq + jnp.where(srow == t, jnp.sum(ge * k, axis=0, keepdims=True), 0.0)
        dk = dk + ge * qs[t:t + 1, :]
        dv = dv + aexp[sl, :] * do[t:t + 1, :]
    return dq, dk, dv, dbl, ds0


def _hgrn_bwd(dmix, proj, o_h, s0, gh, lb, ones_b, nb, name):
    n = proj.shape[0]
    t = n // nb
    tt = _hgrn_tile(t)
    nt = t // tt
    ncs = tt // CHUNK
    nsub = CHUNK // SUB
    w = HGRN_W

    def body(dm_ref, q_ref, z_ref, v_ref, ga_ref, oh_ref, s0_ref, gh_ref, lb_ref, ones_ref,
             dp_ref, dgh_ref, dlb_ref, ds_s, ss_s, b_s, qs_s, k_s, do_s, dq_s, dk_s, dv_s, dbl_s):
        first = jnp.logical_and(pl.program_id(0) == 0, pl.program_id(1) == 0)

        @pl.when(first)
        def _():
            dgh_ref[...] = jnp.zeros_like(dgh_ref)
            dlb_ref[...] = jnp.zeros_like(dlb_ref)

        @pl.when(pl.program_id(1) == 0)
        def _():
            ds_s[...] = jnp.zeros_like(ds_s)

        ones_b = ones_ref[...]
        ones_f = ones_b.astype(F32)
        bmask = _block_ones(w, F32)
        lbv, ghv = lb_ref[...], gh_ref[...]
        oh, ga, dm = oh_ref[...], ga_ref[...], dm_ref[...]
        rn = lax.rsqrt(_head_mean(oh * oh, ones_f) + NORM_EPS)
        nh = oh * rn
        sga = _sigmoid(ga)
        dp_ref[:, 3 * w:4 * w] = (dm * nh * ghv * (sga * (1.0 + ga * (1.0 - sga)))).astype(BF16)
        dn = dm * (ga * sga)
        dgh_ref[...] += jnp.sum(dn * nh, axis=0, keepdims=True)
        dn = dn * ghv
        do_s[...] = rn * (dn - nh * _head_mean(dn * nh, ones_f))
        q = q_ref[...]
        sig, sn, f, g, k, sq = _hgrn_gates(q, z_ref[...], lbv)
        qs = q * sq
        b_s[...] = _dot_hi(_sub_tri(tt, True), g)
        qs_s[...] = qs
        k_s[...] = k

        def chunk(cc, carry):
            c = ncs - 1 - cc
            base = pl.multiple_of(c * CHUNK, CHUNK)
            st = s0_ref[c]
            for u in range(nsub):
                ss_s[u] = st
                if u < nsub - 1:
                    rows = pl.ds(base + u * SUB, SUB)
                    st = _hgrn_state_step(st, k_s[rows, :], v_ref[rows, :], b_s[rows, :], bmask)
            ds = ds_s[...]
            for u in reversed(range(nsub)):
                rows = pl.ds(base + u * SUB, SUB)
                dq, dk, dv, dbl, ds = _hgrn_sub_bwd(qs_s[rows, :], k_s[rows, :], v_ref[rows, :], b_s[rows, :],
                                                    do_s[rows, :], ss_s[u], ds, ones_b, bmask)
                dq_s[rows, :] = dq
                dk_s[rows, :] = dk
                dv_s[rows, :] = dv
                dbl_s[rows, :] = jnp.broadcast_to(dbl, (SUB, w))
            ds_s[...] = ds
            return carry

        lax.fori_loop(0, ncs, chunk, 0)
        dqs, dk = dq_s[...], dk_s[...]
        dg = _dot_hi(_sub_tri(tt, False), qs * dqs - k * dk) + dbl_s[...]
        dfz = jnp.where(f > TINY, dg / jnp.maximum(f, TINY), 0.0)
        dlb_ref[...] += jnp.sum(dfz * (1.0 - sig) - dk * sn, axis=0, keepdims=True)
        dp_ref[:, 0:w] = (dqs * (sq * (1.0 + q * (1.0 - sq)))).astype(BF16)
        dp_ref[:, w:2 * w] = ((dfz - dk) * (1.0 - lbv) * sig * sn).astype(BF16)
        dp_ref[:, 2 * w:3 * w] = dv_s[...].astype(BF16)

    def rv(b, i):
        return b * nt + (nt - 1 - i)

    def col(j):
        return pl.BlockSpec((tt, w), lambda b, i: (rv(b, i), j))

    def full(a, bb):
        return pl.BlockSpec((a, bb), lambda b, i: (0, 0))

    return _pc(
        body, name, grid=(nb, nt),
        in_specs=[col(0), col(C_QA // w), col(C_FA // w), col(C_IA // w), col(C_GA // w), col(0),
                  pl.BlockSpec((ncs, w, w), lambda b, i: (rv(b, i), 0, 0)), full(1, w), full(1, w), full(w, w)],
        out_specs=[pl.BlockSpec((tt, 4 * w), lambda b, i: (rv(b, i), 0)), full(1, w), full(1, w)],
        out_shape=[_sds((n, 4 * w), BF16), _sds((1, w), F32), _sds((1, w), F32)],
        scratch_shapes=[pltpu.VMEM((w, w), F32), pltpu.VMEM((nsub, w, w), F32)] + [pltpu.VMEM((tt, w), F32)] * 8,
        compiler_params=_params(),
    )(dmix, proj, proj, proj, proj, o_h, s0, gh, lb, ones_b)


def _pool_bwd(dmix, proj, wbd, wbd_t, scale, nb, name):
    n = proj.shape[0]
    t = n // nb
    tt = min(256, t)
    nt = t // tt
    nhb = tt // POOL_HALO
    cu, cg, cm = C_UB // POOL_W, C_GB // POOL_W, HGRN_W // POOL_W

    def body(u_ref, g_ref, h_ref, dm_ref, gn_ref, dmn_ref, w_ref, wt_ref, s_ref, dp_ref, dw_ref, ds_ref):
        i = pl.program_id(1)
        first = jnp.logical_and(pl.program_id(0) == 0, i == 0)

        @pl.when(first)
        def _():
            dw_ref[...] = jnp.zeros_like(dw_ref)
            ds_ref[...] = jnp.zeros_like(ds_ref)

        sc = s_ref[...]
        halo = jnp.where(i == 0, 0.0, h_ref[...])
        pooled, cnt = _pool_mix(u_ref[...], halo, i * tt, tt)
        pb = pooled.astype(BF16)
        pre = _dot(pb, w_ref[...])
        gv, dm = g_ref[...], dm_ref[...]
        sg = _sigmoid(gv)
        silu = gv * sg
        dgb = dm * pre * sc * (sg * (1.0 + gv * (1.0 - sg)))
        ds_ref[...] += jnp.sum(dm * pre * silu, axis=0, keepdims=True)
        dpre = (dm * sc * silu).astype(BF16)
        dw_ref[...] += _dot_tn(pb, dpre)
        dpool = _dot(dpre, wt_ref[...])
        gn = gn_ref[...]
        dpre_n = (dmn_ref[...] * sc * (gn * _sigmoid(gn))).astype(BF16)
        dpool_n = jnp.where(i == nt - 1, 0.0, _dot(dpre_n, wt_ref[...]))
        lane, wl = _pool_lane_windows()
        tpos_n = ((i + 1) * tt + lax.broadcasted_iota(jnp.int32, (POOL_HALO, POOL_W), 0)).astype(F32)
        ext = jnp.concatenate([dpool / cnt, dpool_n / jnp.minimum(tpos_n + 1.0, wl)], axis=0)
        rows = tt + POOL_HALO
        sums, cur, shift = [], ext, 1
        for _ in POOL_WINDOWS:
            cur = cur + pltpu.roll(cur, rows - shift, axis=0)
            sums.append(cur[:tt, :])
            shift *= 2
        du = _pool_select(lane, sums) - dpool
        dp_ref[...] = jnp.concatenate([du, dgb], axis=1).astype(BF16)

    def nxt(b, i):
        return jnp.minimum((b * nt + i + 1) * nhb, n // POOL_HALO - 1)

    return _pc(
        body, name, grid=(nb, nt),
        in_specs=_pool_specs(tt, nt, nhb) + [
            pl.BlockSpec((tt, POOL_W), lambda b, i: (b * nt + i, cm)),
            pl.BlockSpec((POOL_HALO, POOL_W), lambda b, i: (nxt(b, i), cg)),
            pl.BlockSpec((POOL_HALO, POOL_W), lambda b, i: (nxt(b, i), cm)),
            pl.BlockSpec((POOL_W, POOL_W), lambda b, i: (0, 0)), pl.BlockSpec((POOL_W, POOL_W), lambda b, i: (0, 0)),
            pl.BlockSpec((1, POOL_W), lambda b, i: (0, 0))],
        out_specs=[pl.BlockSpec((tt, 2 * POOL_W), lambda b, i: (b * nt + i, 0)),
                   pl.BlockSpec((POOL_W, POOL_W), lambda b, i: (0, 0)), pl.BlockSpec((1, POOL_W), lambda b, i: (0, 0))],
        out_shape=[_sds((n, 2 * POOL_W), BF16), _sds((POOL_W, POOL_W), F32), _sds((1, POOL_W), F32)],
        compiler_params=_params(),
    )(proj, proj, proj, dmix, proj, dmix, wbd, wbd_t, scale)


def _fox_gate_bwd(dmix, proj, o_c, name):
    n = proj.shape[0]
    tm = min(256, n)

    def body(dm_ref, gc_ref, oc_ref, da_ref, dg_ref, dl_ref):
        dm, gc, oc = dm_ref[...], gc_ref[...], oc_ref[...]
        sg = _sigmoid(gc)
        da = dm * (gc * sg)
        da_ref[...] = da.astype(BF16)
        dg_ref[...] = (dm * oc * (sg * (1.0 + gc * (1.0 - sg)))).astype(BF16)
        r = lax.broadcasted_iota(jnp.int32, (FOX_W, FC_PAD), 0) // HEAD
        c = lax.broadcasted_iota(jnp.int32, (FOX_W, FC_PAD), 1)
        dl_ref[...] = _dot_hi(da * oc, (r == c).astype(F32))

    def row(wd, j=0):
        return pl.BlockSpec((tm, wd), lambda i: (i, j))

    return _pc(
        body, name, grid=(n // tm,),
        in_specs=[row(FOX_W, (HGRN_W + POOL_W) // FOX_W), row(FOX_W, C_GC // FOX_W), row(FOX_W)],
        out_specs=[row(FOX_W), row(FOX_W), row(FC_PAD)],
        out_shape=[_sds((n, FOX_W), BF16), _sds((n, FOX_W), BF16), _sds((n, FC_PAD), F32)],
        compiler_params=_params(),
    )(dmix, proj, o_c)


def _fox_bwd(proj, da, c_col, c_row, lse_row, delta_row, nb, name, rider=None):
    n = proj.shape[0]
    t = n // nb
    tb = min(256, t)
    nq = t // tb
    pw = 2 * HEAD

    def body(q_ref, k_ref, v_ref, da_ref, cc_ref, cr_ref, lse_ref, dl_ref,
             dq_ref, dk_ref, dv_ref, dck_ref, dcq_ref, dq_s, dk_s, dv_s, dck_s, dcq_s):
        kj, qi = pl.program_id(1), pl.program_id(2)

        @pl.when(jnp.logical_and(kj == 0, qi == 0))
        def _():
            dq_s[...] = jnp.zeros_like(dq_s)
            dcq_s[...] = jnp.zeros_like(dcq_s)

        @pl.when(qi == 0)
        def _():
            dk_s[...] = jnp.zeros_like(dk_s)
            dv_s[...] = jnp.zeros_like(dv_s)
            dck_s[...] = jnp.zeros_like(dck_s)

        @pl.when(qi >= kj)
        def _():
            keys = kj * tb + lax.broadcasted_iota(jnp.int32, (tb, tb), 0)
            qrys = qi * tb + lax.broadcasted_iota(jnp.int32, (tb, tb), 1)
            causal = qrys >= keys
            lo = _lane_lo()
            dck = dck_s[...]
            for p in range(FOX_HEADS // 2):
                sl = slice(p * pw, (p + 1) * pw)
                qp = q_ref[:, sl] * FOX_SCALE
                kf = k_ref[:, sl]
                kp = kf.astype(BF16)
                kt = kf.T.astype(BF16)
                vp = v_ref[:, sl].astype(BF16)
                dap = da_ref[:, sl]
                dk, dv = dk_s[:, sl], dv_s[:, sl]
                for h in range(2):
                    hh = 2 * p + h
                    lm = lo if h == 0 else jnp.logical_not(lo)
                    qm = jnp.where(lm, qp, 0.0).astype(BF16)
                    dam = jnp.where(lm, dap, jnp.zeros_like(dap))
                    s = _dot_nt(kp, qm) + (cr_ref[hh:hh + 1, :] - cc_ref[:, hh:hh + 1])
                    pe = jnp.where(causal, jnp.exp(s - lse_ref[hh:hh + 1, :]), 0.0)
                    dp = _dot_nt(vp, dam)
                    ds = pe * (dp - dl_ref[hh:hh + 1, :])
                    dsb = ds.astype(BF16)
                    dv = dv + _dot(pe.astype(BF16), dam)
                    dk = dk + _dot(dsb, qm)
                    rows = slice(hh * HEAD, (hh + 1) * HEAD)
                    dq_s[qi, rows, :] += _dot(kt[h * HEAD:(h + 1) * HEAD, :], dsb)
                    dck = dck - _put_col(jnp.zeros_like(dck), hh, jnp.sum(ds, axis=1, keepdims=True))
                    dcq_s[qi, hh:hh + 1, :] += jnp.sum(ds, axis=0, keepdims=True)
                dk_s[:, sl] = dk
                dv_s[:, sl] = dv
            dck_s[...] = dck

        @pl.when(qi == nq - 1)
        def _():
            dk_ref[...] = dk_s[...].astype(BF16)
            dv_ref[...] = dv_s[...].astype(BF16)
            dck_ref[...] = dck_s[...]

        @pl.when(jnp.logical_and(kj == nq - 1, qi == nq - 1))
        def _():
            for j in range(nq):
                dq_ref[j * tb:(j + 1) * tb, :] = (dq_s[j].T * FOX_SCALE).astype(BF16)
                dcq_ref[:, j * tb:(j + 1) * tb] = dcq_s[j]

    def kspec(wd, j=0):
        return pl.BlockSpec((tb, wd), lambda b, kj, qi: (b * nq + kj, j))

    def qspec(wd, j=0):
        return pl.BlockSpec((tb, wd), lambda b, kj, qi: (b * nq + jnp.maximum(qi, kj), j))

    def qrow():
        return pl.BlockSpec((None, FOX_HEADS, tb), lambda b, kj, qi: (b, 0, jnp.maximum(qi, kj)))

    return _call(
        body, name, (proj, proj, proj, da, c_col, c_row, lse_row, delta_row), rider, grid=(nb, nq, nq),
        in_specs=[qspec(FOX_W, C_QC // FOX_W), kspec(FOX_W, C_KC // FOX_W), kspec(FOX_W, C_VC // FOX_W), qspec(FOX_W),
                  kspec(FC_PAD), qrow(), qrow(), qrow()],
        out_specs=[pl.BlockSpec((t, FOX_W), lambda b, kj, qi: (b, 0)), kspec(FOX_W), kspec(FOX_W), kspec(FC_PAD),
                   pl.BlockSpec((None, FOX_HEADS, t), lambda b, kj, qi: (b, 0, 0))],
        out_shape=[_sds((n, FOX_W), BF16), _sds((n, FOX_W), BF16), _sds((n, FOX_W), BF16), _sds((n, FC_PAD), F32),
                   _sds((nb, FOX_HEADS, t), F32)],
        scratch_shapes=[pltpu.VMEM((nq, FOX_W, tb), F32), pltpu.VMEM((tb, FOX_W), F32), pltpu.VMEM((tb, FOX_W), F32),
                        pltpu.VMEM((tb, FC_PAD), F32), pltpu.VMEM((nq, FOX_HEADS, tb), F32)],
        compiler_params=_params(),
    )


def _fox_decay_bwd(dc_q, dc_k, fc, bias, nb, name):
    n = fc.shape[0]
    t = n // nb
    tt = min(256, t)
    nt = t // tt

    def body(dcq_ref, dck_ref, fc_ref, b_ref, dfc_ref, db_ref, carry):
        i = pl.program_id(1)
        first = jnp.logical_and(pl.program_id(0) == 0, i == 0)

        @pl.when(first)
        def _():
            db_ref[...] = jnp.zeros_like(db_ref)

        @pl.when(i == 0)
        def _():
            carry[...] = jnp.zeros_like(carry)

        r = lax.broadcasted_iota(jnp.int32, (tt, tt), 0)
        cc = lax.broadcasted_iota(jnp.int32, (tt, tt), 1)
        dlf = _dot_hi((r <= cc).astype(F32), dcq_ref[...] + dck_ref[...]) + carry[...]
        carry[...] = dlf[0:1, :]
        dfc = dlf * _sigmoid(-(fc_ref[...] + b_ref[...]))
        dfc_ref[...] = dfc.astype(BF16)
        db_ref[...] += jnp.sum(dfc, axis=0, keepdims=True)

    def row():
        return pl.BlockSpec((tt, FC_PAD), lambda b, i: (b * nt + (nt - 1 - i), 0))

    return _pc(
        body, name, grid=(nb, nt),
        in_specs=[row(), row(), row(), pl.BlockSpec((1, FC_PAD), lambda b, i: (0, 0))],
        out_specs=[row(), pl.BlockSpec((1, FC_PAD), lambda b, i: (0, 0))],
        out_shape=[_sds((n, FC_PAD), BF16), _sds((1, FC_PAD), F32)],
        scratch_shapes=[pltpu.VMEM((1, FC_PAD), F32)],
        compiler_params=_params(),
    )(dc_q, dc_k, fc, bias)


def _in_proj_bwd(pieces, w_main_t, w_fc_t, x, g_pre, dxo, name, rider=None):
    n, d = x.shape
    tm = min(256, n)
    widths = [p.shape[1] for p, _ in pieces]
    offs = [o for _, o in pieces]
    np_ = len(pieces)

    def body(*refs):
        p_refs = refs[:np_]
        wt_ref, wf_ref, x_ref, g_ref, dxo_ref, dx_ref, dg_ref = refs[np_:]

        @pl.when(pl.program_id(0) == 0)
        def _():
            dg_ref[...] = jnp.zeros_like(dg_ref)

        dh = _dot(p_refs[-1][...], wf_ref[...])
        for pr, wd, off in zip(p_refs[:-1], widths[:-1], offs[:-1]):
            for j in range(0, wd, 512):
                jw = min(512, wd - j)
                dh = dh + _dot(pr[:, j:j + jw], wt_ref[off + j:off + j + jw, :])
        xv = x_ref[...]
        r = lax.rsqrt(jnp.mean(xv * xv, axis=-1, keepdims=True) + NORM_EPS)
        xh = xv * r
        dg_ref[...] += jnp.sum(dh * xh, axis=0, keepdims=True)
        dx_ref[...] = dxo_ref[...] + _rms_bwd(dh * g_ref[...], xh, r)

    row = pl.BlockSpec((tm, d), lambda i: (i, 0))
    return _call(
        body, name, (*[p for p, _ in pieces], w_main_t, w_fc_t, x, g_pre, dxo), rider, grid=(n // tm,),
        in_specs=[pl.BlockSpec((tm, wd), lambda i: (i, 0)) for wd in widths] + [
            pl.BlockSpec((MAIN_W, d), lambda i: (0, 0)), pl.BlockSpec((FC_PAD, d), lambda i: (0, 0)),
            row, pl.BlockSpec((1, d), lambda i: (0, 0)), row],
        out_specs=[row, pl.BlockSpec((1, d), lambda i: (0, 0))],
        out_shape=[_sds((n, d), F32), _sds((1, d), F32)],
        compiler_params=_params(),
    )


def _lower_bound_table(lower_bounds, name):
    depth, w = lower_bounds.shape

    def body(lb_ref, o_ref):
        v = lb_ref[...]
        e = jnp.exp(v - jnp.max(v, axis=0, keepdims=True))
        p = e / jnp.sum(e, axis=0, keepdims=True)
        acc = jnp.zeros((1, w), F32)
        for l in range(depth):
            acc = acc + p[l:l + 1, :]
            o_ref[l:l + 1, :] = acc - p[0:1, :]

    return _pc(body, name, out_shape=_sds((depth, w), F32))(lower_bounds)


def _lower_bound_bwd(lower_bounds, dlbs, name):
    depth, w = lower_bounds.shape

    def body(lb_ref, d_ref, o_ref):
        v, dl = lb_ref[...], d_ref[...]
        e = jnp.exp(v - jnp.max(v, axis=0, keepdims=True))
        p = e / jnp.sum(e, axis=0, keepdims=True)
        tot = jnp.sum(dl, axis=0, keepdims=True)
        rows, tail = [], tot
        for l in range(depth):
            rows.append(tail - tot if l == 0 else tail)
            tail = tail - dl[l:l + 1, :]
        dp = jnp.concatenate(rows, axis=0)
        o_ref[...] = p * (dp - jnp.sum(p * dp, axis=0, keepdims=True))

    return _pc(body, name, out_shape=_sds((depth, w), F32))(lower_bounds, dlbs)


def _place():
    x, y, c = lax.axis_index("x"), lax.axis_index("y"), lax.axis_index("c")
    return x, y, c


def _gather_weights(w_in_b, w_out_b):
    arrays = (w_in_b, w_out_b)
    na = len(arrays)

    def body(*refs):
        ins, outs = refs[:na], refs[na:2 * na]
        send_sems, recv_sems, local_sems = refs[2 * na:]
        x, y, c = _place()
        me, sibling = (x, y, c), (x, y, 1 - c)
        chips = [(1 - x, y), (x, 1 - y), (1 - x, 1 - y)]

        def slot(a, px, py, pc):
            return outs[a].at[4 * px + 2 * py + pc]

        def copy(a, k, block, to, own=False):
            return pltpu.make_async_remote_copy(
                src_ref=ins[a] if own else slot(a, *block), dst_ref=slot(a, *block),
                send_sem=send_sems.at[a * 7 + k], recv_sem=recv_sems.at[a * 7 + k],
                device_id=to, device_id_type=MESH)

        mine = [pltpu.make_async_copy(ins[a], slot(a, *me), local_sems.at[a]) for a in range(na)]
        for cp in mine:
            cp.start()
        first = []
        for a in range(na):
            first.append(copy(a, 0, me, sibling, own=True))
            first += [copy(a, 1 + j, me, (*chip, c), own=True) for j, chip in enumerate(chips)]
        for cp in first:
            cp.start()
        passed = []
        for j, chip in enumerate(chips):
            for a in range(na):
                copy(a, 1 + j, (*chip, c), me).wait_recv()
                fw = copy(a, 4 + j, (*chip, c), sibling)
                fw.start()
                passed.append(fw)
        for a in range(na):
            copy(a, 0, sibling, me).wait_recv()
            for j, chip in enumerate(chips):
                copy(a, 4 + j, (*chip, 1 - c), me).wait_recv()
        for cp in first + passed:
            cp.wait_send()
        for cp in mine:
            cp.wait()

    any_spec = pl.BlockSpec(memory_space=pl.ANY)
    return _pc(
        body, "gather_weights",
        in_specs=[any_spec] * na, out_specs=[any_spec] * na,
        out_shape=[_sds((N_DEV,) + a.shape, a.dtype) for a in arrays],
        scratch_shapes=[pltpu.SemaphoreType.DMA((7 * na,)), pltpu.SemaphoreType.DMA((7 * na,)),
                        pltpu.SemaphoreType.DMA((na,))],
    )(*arrays)


def _peer(k):
    x, y, c = _place()
    return (1 - x if k & 4 else x, 1 - y if k & 2 else y, 1 - c if k & 1 else c)


def _remote(src, dst, sems, s, to):
    return pltpu.make_async_remote_copy(src_ref=src, dst_ref=dst, send_sem=sems[0].at[s], recv_sem=sems[1].at[s],
                                        device_id=to, device_id_type=MESH)


def _gather_rider(shards):
    na = len(shards)

    def plan(ins, outs, *sems):
        x, y, c = _place()
        me = 4 * x + 2 * y + c
        locs = [pltpu.make_async_copy(ins[a], outs[a].at[me], sems[2].at[a]) for a in range(na)]
        sends, recvs = [], []
        for k in range(1, N_DEV):
            px, py, pc = _peer(k)
            for a in range(na):
                s = (k - 1) * na + a
                sends.append(_remote(ins[a], outs[a].at[me], sems, s, (px, py, pc)))
                recvs.append(_remote(ins[a], outs[a].at[4 * px + 2 * py + pc], sems, s, (px, py, pc)))
        return sends, recvs, locs

    return _Rider(shards, [_sds((N_DEV,) + a.shape, a.dtype) for a in shards], (N_DEV - 1) * na, na, plan)


def _swap_rider(halves):
    na = len(halves)

    def plan(ins, outs, *sems):
        x, y, c = _place()
        cps = [_remote(ins[a].at[1 - c], outs[a], sems, a, (x, y, 1 - c)) for a in range(na)]
        return cps, cps, []

    return _Rider(halves, [_sds(a.shape[1:], a.dtype) for a in halves], na, 1, plan)


def _chip_exchange_rider(parts, small=None):
    na = len(parts)
    n_chip = N_DEV // 2

    def plan(ins, outs, *sems):
        x, y, c = _place()
        chip = 2 * x + y
        locs = [pltpu.make_async_copy(ins[a].at[chip], outs[a].at[chip], sems[2].at[a]) for a in range(na)]
        sends, recvs = [], []
        for k in range(1, n_chip):
            px, py, _ = _peer(2 * k)
            for a in range(na):
                s = (k - 1) * na + a
                sends.append(_remote(ins[a].at[2 * px + py], outs[a].at[chip], sems, s, (px, py, c)))
                recvs.append(_remote(ins[a].at[2 * px + py], outs[a].at[2 * px + py], sems, s, (px, py, c)))
        if small is not None:
            me = 2 * chip + c
            locs.append(pltpu.make_async_copy(ins[na], outs[na].at[me], sems[2].at[na]))
            for k in range(1, N_DEV):
                px, py, pc = _peer(k)
                s = (n_chip - 1) * na + k - 1
                sends.append(_remote(ins[na], outs[na].at[me], sems, s, (px, py, pc)))
                recvs.append(_remote(ins[na], outs[na].at[4 * px + 2 * py + pc], sems, s, (px, py, pc)))
        return sends, recvs, locs

    extra = [] if small is None else [small]
    shapes = [_sds(a.shape, a.dtype) for a in parts] + [_sds((N_DEV,) + s.shape, s.dtype) for s in extra]
    n_sems = (n_chip - 1) * na + (N_DEV - 1) * len(extra)
    return _Rider(list(parts) + extra, shapes, n_sems, na + len(extra), plan)


def _pair_add(halves, other, core, name):
    _, nch, r, c = halves.shape
    tr = 256 if r % 256 == 0 else r

    def body(h_ref, o_ref, c_ref, p_ref):
        mine = jnp.where(c_ref[...] == 0, h_ref[0].astype(F32), h_ref[1].astype(F32))
        p_ref[...] = (mine + o_ref[...].astype(F32)).astype(BF16)

    blk = pl.BlockSpec((None, tr, c), lambda j, i: (j, i, 0))
    return _pc(
        body, name, grid=(nch, r // tr),
        in_specs=[pl.BlockSpec((2, None, tr, c), lambda j, i: (0, j, i, 0)), blk, pl.BlockSpec((1, 1), lambda j, i: (0, 0))],
        out_specs=blk, out_shape=_sds((nch, r, c), BF16),
        compiler_params=_params(),
    )(halves, other, core)


def _sum_adamw(parts, w, m, v, name):
    nl, r, c = w.shape
    tr = 256 if r % 256 == 0 else r

    def body(*refs):
        p_refs = refs[:nl]
        w_ref, m_ref, v_ref, g_ref, d_ref, mo_ref, vo_ref = refs[nl:]
        for l in range(nl):
            @pl.when(pl.program_id(0) == l)
            def _(p_ref=p_refs[l]):
                g = p_ref[0].astype(F32)
                for j in range(1, p_ref.shape[0]):
                    g = g + p_ref[j].astype(F32)
                mn = ADAM_B1 * m_ref[...] + (1.0 - ADAM_B1) * g
                vn = ADAM_B2 * v_ref[...] + (1.0 - ADAM_B2) * (g * g)
                m_hat = mn / (1.0 - ADAM_B1 ** ADAM_STEP)
                v_hat = vn / (1.0 - ADAM_B2 ** ADAM_STEP)
                g_ref[...] = g
                d_ref[...] = -ADAM_LR * (m_hat / (jnp.sqrt(v_hat) + ADAM_EPS) + ADAM_WD * w_ref[...])
                mo_ref[...] = mn
                vo_ref[...] = vn

    def part_spec(l, k):
        return pl.BlockSpec((k, tr, c), lambda li, i: (0, jnp.where(li == l, i, 0), 0))

    row = pl.BlockSpec((None, tr, c), lambda li, i: (li, i, 0))
    return _pc(
        body, name, grid=(nl, r // tr),
        in_specs=[part_spec(l, p.shape[0]) for l, p in enumerate(parts)] + [row, row, row],
        out_specs=[row] * 4,
        out_shape=[_sds((nl, r, c), F32)] * 4,
        compiler_params=_params(),
    )(*parts, w, m, v)


SMALL = ("lower_bounds", "pre_norm_g", "hgrn_norm_g", "fox_f_bias", "pool_w", "pool_scale", "post_norm_g")
SMALL_LANES = 128


def _pack_small(tree):
    flat = jnp.concatenate([tree[k].reshape(-1) for k in SMALL])
    rows = -(-flat.shape[0] // (8 * SMALL_LANES)) * 8
    return jnp.pad(flat, (0, rows * SMALL_LANES - flat.shape[0])).reshape(rows, SMALL_LANES)


def _unpack_small(packed, like):
    flat, out, off = packed.reshape(-1), {}, 0
    for k in SMALL:
        size = like[k].size
        out[k] = flat[off:off + size].reshape(like[k].shape)
        off += size
    return out


def _block_diag(pw):
    g = pw.shape[0]
    eye = jnp.eye(g, dtype=pw.dtype)
    return (eye[:, None, :, None] * pw[:, :, None, :]).reshape(g * HEAD, g * HEAD)


def _layer_weights(g_in, g_out):
    d = g_in.shape[1]
    in_w = N_DEV * g_in.shape[2]
    full_in = g_in.transpose(1, 0, 2).reshape(d, in_w)
    w_main = full_in[:, :MAIN_W]
    w_fc = jnp.pad(full_in[:, MAIN_W:], ((0, 0), (0, FC_PAD - (in_w - MAIN_W))))
    full_out = g_out.reshape(N_DEV * g_out.shape[1], d)
    return w_main, w_fc, w_main.T, w_fc.T, full_out, full_out.T


def _layer_fwd(l, x, lbs, weights, lw, nb, rider=None):
    n = x.shape[0]
    t = n // nb
    w_main, w_fc, _, _, w_out, _ = lw
    bias = jnp.pad(weights["fox_f_bias"][l:l + 1], ((0, 0), (0, FC_PAD - FOX_HEADS)))
    wbd = _block_diag(weights["pool_w"][l]).astype(BF16)
    proj, fc, h = _in_proj_fwd(x, weights["pre_norm_g"][l:l + 1], w_main, w_fc, f"in_proj_fwd_{l}")
    c_col = _fox_decay_fwd(fc, bias, nb, f"fox_decay_fwd_{l}")
    c_row = c_col.reshape(nb, t, FC_PAD)[:, :, :FOX_HEADS].transpose(0, 2, 1)
    o_h, s0 = _hgrn_fwd(proj, lbs[l:l + 1], _block_ones(HGRN_W, BF16), nb, f"hgrn_fwd_{l}")
    o_b = _pool_fwd(proj, wbd, weights["pool_scale"][l:l + 1], nb, f"pool_fwd_{l}")
    (o_c, lse), rode = _fox_fwd(proj, c_col, c_row, nb, f"fox_fwd_{l}", rider)
    x_next, mixed, y = _merge_fwd(x, proj, o_h, o_b, o_c, weights["hgrn_norm_g"][l:l + 1], w_out,
                                  weights["post_norm_g"][l:l + 1], f"merge_fwd_{l}")
    return x_next, (x, proj, fc, h, c_col, c_row, o_h, s0, o_c, lse, mixed, y, bias, wbd), rode


def _layer_bwd(l, dx, saved, lbs, weights, lw, nb, rider=None):
    _, proj, fc, h, c_col, c_row, o_h, s0, o_c, lse, mixed, y, bias, wbd = saved
    n = proj.shape[0]
    t = n // nb
    w_out_t = lw[5]
    g = {}
    dy, dmix, dgp = _merge_bwd(dx, y, weights["post_norm_g"][l:l + 1], w_out_t, f"merge_bwd_{l}")
    g["post_norm_g"] = dgp[0]
    g["w_out"] = _grad_matmul(mixed, dy, f"w_out_grad_{l}")
    d_a, dgh, dlb = _hgrn_bwd(dmix, proj, o_h, s0, weights["hgrn_norm_g"][l:l + 1], lbs[l:l + 1],
                              _block_ones(HGRN_W, BF16), nb, f"hgrn_bwd_{l}")
    g["hgrn_norm_g"], g["lbs"] = dgh[0], dlb[0]
    d_b, dwbd, dps = _pool_bwd(dmix, proj, wbd, wbd.T, weights["pool_scale"][l:l + 1], nb, f"pool_bwd_{l}")
    g["pool_w"] = jnp.stack([dwbd[j * HEAD:(j + 1) * HEAD, j * HEAD:(j + 1) * HEAD] for j in range(len(POOL_WINDOWS))])
    g["pool_scale"] = dps[0]
    da, d_gc, delta = _fox_gate_bwd(dmix, proj, o_c, f"fox_gate_bwd_{l}")
    delta_row = delta.reshape(nb, t, FC_PAD)[:, :, :FOX_HEADS].transpose(0, 2, 1)
    (d_qc, d_kc, d_vc, dc_k, dc_q), rode = _fox_bwd(proj, da, c_col, c_row, lse, delta_row, nb, f"fox_bwd_{l}", rider)
    dc_q = jnp.pad(dc_q.transpose(0, 2, 1).reshape(n, FOX_HEADS), ((0, 0), (0, FC_PAD - FOX_HEADS)))
    d_fc, dbias = _fox_decay_bwd(dc_q, dc_k, fc, bias, nb, f"fox_decay_bwd_{l}")
    g["fox_f_bias"] = dbias[0, :FOX_HEADS]
    pieces = [(d_a, C_QA), (d_b, C_UB), (d_qc, C_QC), (d_kc, C_KC), (d_vc, C_VC), (d_gc, C_GC), (d_fc, None)]
    gw = [_grad_matmul(h, p, f"w_in_grad_{l}_{i}") for i, (p, _) in enumerate(pieces)]
    g["w_in"] = jnp.concatenate(gw[:-1] + [gw[-1][:, :FOX_HEADS]], axis=1)
    return g, pieces, rode


def _layer_bwd_input(l, dx, pieces, saved, weights, lw, rider=None):
    (dxi, dgpre), rode = _in_proj_bwd(pieces, lw[2], lw[3], saved[0], weights["pre_norm_g"][l:l + 1], dx,
                                      f"in_proj_bwd_{l}", rider)
    return dxi, dgpre[0], rode


def _grad_halves(g_in, g_out):
    d = g_in.shape[0]
    n_chip = N_DEV // 2
    h_in = g_in.reshape(d, n_chip, 2, g_in.shape[1] // N_DEV).transpose(2, 1, 0, 3)
    h_out = g_out.reshape(n_chip, 2, g_out.shape[0] // N_DEV, d).transpose(1, 0, 2, 3)
    return h_in.astype(BF16), h_out.astype(BF16)


def kernel(x, lower_bounds, pre_norm_g, w_in, hgrn_norm_g, fox_f_bias, pool_w, pool_scale, w_out, post_norm_g, loss_target, m_lower_bounds, m_pre_norm_g, m_w_in, m_hgrn_norm_g, m_fox_f_bias, m_pool_w, m_pool_scale, m_w_out, m_post_norm_g, v_lower_bounds, v_pre_norm_g, v_w_in, v_hgrn_norm_g, v_fox_f_bias, v_pool_w, v_pool_scale, v_w_out, v_post_norm_g):
    weights = dict(lower_bounds=lower_bounds, pre_norm_g=pre_norm_g, hgrn_norm_g=hgrn_norm_g, fox_f_bias=fox_f_bias,
                   pool_w=pool_w, pool_scale=pool_scale, post_norm_g=post_norm_g)
    mom_m = dict(lower_bounds=m_lower_bounds, pre_norm_g=m_pre_norm_g, hgrn_norm_g=m_hgrn_norm_g, fox_f_bias=m_fox_f_bias,
                 pool_w=m_pool_w, pool_scale=m_pool_scale, post_norm_g=m_post_norm_g)
    mom_v = dict(lower_bounds=v_lower_bounds, pre_norm_g=v_pre_norm_g, hgrn_norm_g=v_hgrn_norm_g, fox_f_bias=v_fox_f_bias,
                 pool_w=v_pool_w, pool_scale=v_pool_scale, post_norm_g=v_post_norm_g)
    depth = w_in.shape[0]
    nb, t, d = x.shape
    n = nb * t
    core = lax.axis_index("c").astype(jnp.int32).reshape(1, 1)
    shards = [(w_in[l].astype(BF16), w_out[l].astype(BF16)) for l in range(depth)]
    lbs = _lower_bound_table(lower_bounds, "lower_bound_table")

    lw = [_layer_weights(*_gather_weights(*shards[0]))]
    xl, saved = x.reshape(n, d), []
    for l in range(depth):
        rider = _gather_rider(shards[l + 1]) if l + 1 < depth else None
        xl, sv, rode = _layer_fwd(l, xl, lbs, weights, lw[l], nb, rider)
        saved.append(sv)
        if rode is not None:
            lw.append(_layer_weights(*rode))
    dx, sq = _loss_fwd_bwd(xl, loss_target.reshape(n, d), "loss")
    loss = lax.psum(0.5 * jnp.sum(sq) / d, AXES)

    grads, recv, pending = [None] * depth, [None] * depth, None
    for l in reversed(range(depth)):
        rider = None if pending is None else _chip_exchange_rider(pending)
        g, pieces, rode = _layer_bwd(l, dx, saved[l], lbs, weights, lw[l], nb, rider)
        if rode is not None:
            recv[l + 1] = rode
        halves = _grad_halves(g["w_in"], g["w_out"])
        other = _run_rider(_swap_rider(halves), f"grad_swap_{l}")
        pending = [_pair_add(hv, ot, core, f"grad_pair_add_{l}_{i}") for i, (hv, ot) in enumerate(zip(halves, other))]
        rider = _chip_exchange_rider(pending) if l == 0 else None
        dx, g["pre_norm_g"], rode = _layer_bwd_input(l, dx, pieces, saved[l], weights, lw[l], rider)
        if rode is not None:
            recv[l] = rode
        grads[l] = g
    small = {k: jnp.stack([grads[l][k] for l in range(depth)]) for k in SMALL if k != "lower_bounds"}
    small["lower_bounds"] = _lower_bound_bwd(lower_bounds, jnp.stack([grads[l]["lbs"] for l in range(depth)]),
                                             "lower_bound_bwd")
    (r_small,) = _run_rider(_gather_rider([_pack_small(small)]), "small_grads_gather")

    res_in = _sum_adamw([recv[l][0] for l in range(depth)], w_in, m_w_in, v_w_in, "adamw_w_in")
    res_out = _sum_adamw([recv[l][1] for l in range(depth)], w_out, m_w_out, v_w_out, "adamw_w_out")
    res_small = _sum_adamw([r_small], _pack_small(weights)[None], _pack_small(mom_m)[None], _pack_small(mom_v)[None],
                           "adamw_small")

    names = ("lower_bounds", "pre_norm_g", "w_in", "hgrn_norm_g", "fox_f_bias", "pool_w", "pool_scale", "w_out", "post_norm_g")
    outs = [loss, dx.reshape(nb, t, d)]
    for i in range(4):
        full = dict(_unpack_small(res_small[i][0], weights), w_in=res_in[i], w_out=res_out[i])
        outs += [full[k] for k in names]
    return tuple(outs)
```

```python
import functools

import jax
import jax.numpy as jnp
from jax import lax
from jax.experimental import pallas as pl
from jax.experimental.pallas import tpu as pltpu

F32, BF16 = jnp.float32, jnp.bfloat16
HI = lax.Precision.HIGHEST
MESH = pl.DeviceIdType.MESH
AXES = ("x", "y", "c")
N_DEV = 8

NORM_EPS = 1e-6
MASK_VALUE = -1e30
TINY = 1e-30
CHUNK = 64
SUB = 16
HGRN_W, POOL_W, FOX_W = 256, 256, 512
HEAD = 64
FOX_HEADS = 8
POOL_WINDOWS = (2, 4, 8, 16)
POOL_HALO = 16
MAIN_W = 3584
FC_PAD = 128
C_QA, C_FA, C_IA, C_GA, C_UB, C_GB, C_QC, C_KC, C_VC, C_GC = 0, 256, 512, 768, 1024, 1280, 1536, 2048, 2560, 3072
FOX_SCALE = HEAD ** -0.5

ADAM_LR, ADAM_B1, ADAM_B2, ADAM_EPS, ADAM_WD, ADAM_STEP = 0.001, 0.9, 0.999, 1e-08, 0.01, 10

VMEM_LIMIT = 56 * 1024 * 1024


def _pc(fn, name, **kw):
    return pl.pallas_call(fn, name=name, **kw)


def _params(**kw):
    return pltpu.CompilerParams(vmem_limit_bytes=VMEM_LIMIT, **kw)


class _Rider:
    def __init__(self, inputs, out_shapes, n_sems, n_local, plan):
        self.inputs, self.out_shapes, self.n_sems, self.n_local, self.plan = list(inputs), list(out_shapes), n_sems, n_local, plan

    def start(self, ins, outs, *sems):
        sends, _, locs = self.plan(ins, outs, *sems)
        for cp in locs + sends:
            cp.start()

    def wait(self, ins, outs, *sems):
        sends, recvs, locs = self.plan(ins, outs, *sems)
        for cp in recvs:
            cp.wait_recv()
        for cp in sends:
            cp.wait_send()
        for cp in locs:
            cp.wait()

    def sem_shapes(self):
        return [pltpu.SemaphoreType.DMA((self.n_sems,)), pltpu.SemaphoreType.DMA((self.n_sems,)),
                pltpu.SemaphoreType.DMA((self.n_local,))]


def _call(body, name, args, rider=None, *, grid, in_specs, out_specs, out_shape, scratch_shapes=(), **kw):
    if rider is None:
        res = _pc(body, name, grid=grid, in_specs=in_specs, out_specs=out_specs, out_shape=out_shape,
                  scratch_shapes=list(scratch_shapes), **kw)(*args)
        return res, None
    n_in, n_out, n_scr = len(in_specs), len(out_specs), len(scratch_shapes)
    n_rin, n_rout = len(rider.inputs), len(rider.out_shapes)

    def ridden(*refs):
        ins, refs = refs[:n_in], refs[n_in:]
        rins, refs = refs[:n_rin], refs[n_rin:]
        outs, refs = refs[:n_out], refs[n_out:]
        routs, refs = refs[:n_rout], refs[n_rout:]
        scr, sems = refs[:n_scr], refs[n_scr:]
        first = functools.reduce(jnp.logical_and, [pl.program_id(a) == 0 for a in range(len(grid))])
        last = functools.reduce(jnp.logical_and, [pl.program_id(a) == g - 1 for a, g in enumerate(grid)])

        @pl.when(first)
        def _():
            rider.start(rins, routs, *sems)

        body(*ins, *outs, *scr)

        @pl.when(last)
        def _():
            rider.wait(rins, routs, *sems)

    any_spec = pl.BlockSpec(memory_space=pl.ANY)
    res = _pc(ridden, name, grid=grid, in_specs=list(in_specs) + [any_spec] * n_rin,
              out_specs=list(out_specs) + [any_spec] * n_rout, out_shape=list(out_shape) + rider.out_shapes,
              scratch_shapes=list(scratch_shapes) + rider.sem_shapes(), **kw)(*args, *rider.inputs)
    return res[:n_out], res[n_out:]


def _run_rider(rider, name):
    n_rin = len(rider.inputs)

    def body(*refs):
        ins, outs, sems = refs[:n_rin], refs[n_rin:n_rin + len(rider.out_shapes)], refs[n_rin + len(rider.out_shapes):]
        rider.start(ins, outs, *sems)
        rider.wait(ins, outs, *sems)

    any_spec = pl.BlockSpec(memory_space=pl.ANY)
    return _pc(body, name, in_specs=[any_spec] * n_rin, out_specs=[any_spec] * len(rider.out_shapes),
               out_shape=rider.out_shapes, scratch_shapes=rider.sem_shapes())(*rider.inputs)


def _dot(a, b):
    return jnp.dot(a, b, preferred_element_type=F32)


def _dot_nt(a, b):
    return lax.dot_general(a, b, (((1,), (1,)), ((), ())), preferred_element_type=F32)


def _dot_tn(a, b):
    return lax.dot_general(a, b, (((0,), (0,)), ((), ())), preferred_element_type=F32)


def _dot_hi(a, b):
    return jnp.dot(a, b, precision=HI, preferred_element_type=F32)


def _sigmoid(x):
    return 1.0 / (1.0 + jnp.exp(-x))


def _block_ones(n, dtype):
    r = lax.broadcasted_iota(jnp.int32, (n, n), 0) // HEAD
    c = lax.broadcasted_iota(jnp.int32, (n, n), 1) // HEAD
    return (r == c).astype(dtype)


def _sds(shape, dtype):
    return jax.ShapeDtypeStruct(shape, dtype)


def _in_proj_fwd(x, g_pre, w_main, w_fc, name):
    n, d = x.shape
    tm = min(256, n)

    def body(x_ref, g_ref, w_ref, wf_ref, proj_ref, fc_ref, ht_ref):
        xv = x_ref[...]
        r = lax.rsqrt(jnp.mean(xv * xv, axis=-1, keepdims=True) + NORM_EPS)
        hf = xv * r * g_ref[...]
        hb = hf.astype(BF16)
        ht_ref[...] = hf.T.astype(BF16)
        for j in range(0, MAIN_W, 512):
            proj_ref[:, j:j + 512] = _dot(hb, w_ref[:, j:j + 512])
        fc_ref[...] = _dot(hb, wf_ref[...])

    return _pc(
        body, name, grid=(n // tm,),
        in_specs=[pl.BlockSpec((tm, d), lambda i: (i, 0)), pl.BlockSpec((1, d), lambda i: (0, 0)),
                  pl.BlockSpec((d, MAIN_W), lambda i: (0, 0)), pl.BlockSpec((d, FC_PAD), lambda i: (0, 0))],
        out_specs=[pl.BlockSpec((tm, MAIN_W), lambda i: (i, 0)), pl.BlockSpec((tm, FC_PAD), lambda i: (i, 0)),
                   pl.BlockSpec((d, tm), lambda i: (0, i))],
        out_shape=[_sds((n, MAIN_W), F32), _sds((n, FC_PAD), F32), _sds((d, n), BF16)],
        compiler_params=_params(),
    )(x, g_pre, w_main, w_fc)


def _fox_decay_fwd(fc, bias, nb, name):
    n = fc.shape[0]
    t = n // nb
    tt = min(256, t)
    nt = t // tt

    def body(fc_ref, b_ref, c_ref, carry):
        i = pl.program_id(1)

        @pl.when(i == 0)
        def _():
            carry[...] = jnp.zeros_like(carry)

        xv = fc_ref[...] + b_ref[...]
        lf = jnp.minimum(xv, 0.0) - jnp.log(1.0 + jnp.exp(-jnp.abs(xv)))
        r = lax.broadcasted_iota(jnp.int32, (tt, tt), 0)
        cc = lax.broadcasted_iota(jnp.int32, (tt, tt), 1)
        cs = _dot_hi((r >= cc).astype(F32), lf) + carry[...]
        c_ref[...] = cs
        carry[...] = cs[tt - 1:tt, :]

    return _pc(
        body, name, grid=(nb, nt),
        in_specs=[pl.BlockSpec((tt, FC_PAD), lambda b, i: (b * nt + i, 0)), pl.BlockSpec((1, FC_PAD), lambda b, i: (0, 0))],
        out_specs=pl.BlockSpec((tt, FC_PAD), lambda b, i: (b * nt + i, 0)),
        out_shape=_sds((n, FC_PAD), F32),
        scratch_shapes=[pltpu.VMEM((1, FC_PAD), F32)],
        compiler_params=_params(),
    )(fc, bias)


def _hgrn_gates(q, z, lb):
    sig = _sigmoid(z)
    sn = _sigmoid(-z)
    f = lb + (1.0 - lb) * sig
    g = jnp.log(jnp.maximum(f, TINY))
    k = (1.0 - lb) * sn
    sq = _sigmoid(q)
    return sig, sn, f, g, k, sq


def _sub_tri(n, lower):
    r = lax.broadcasted_iota(jnp.int32, (n, n), 0)
    c = lax.broadcasted_iota(jnp.int32, (n, n), 1)
    tri = (r >= c) if lower else (r <= c)
    return jnp.logical_and(r // SUB == c // SUB, tri).astype(F32)


def _hgrn_decays(qs, k, b):
    srow = lax.broadcasted_iota(jnp.int32, (SUB, HGRN_W), 0)
    es, ws = [], []
    for t in range(SUB):
        e = jnp.where(srow <= t, jnp.exp(b[t:t + 1, :] - b), 0.0)
        es.append(e)
        ws.append(e * (qs[t:t + 1, :] * k))
    return srow, es, ws


def _hgrn_state_step(st, k, v, b, bmask):
    bl = b[SUB - 1:SUB, :]
    ktil = k * jnp.exp(bl - b)
    return st * jnp.exp(bl) + _dot_tn(v.astype(BF16), ktil.astype(BF16)) * bmask


def _hgrn_sub_fwd(qs, k, v, b, st, ones_b, bmask):
    srow, _, ws = _hgrn_decays(qs, k, b)
    aexp = _dot(jnp.concatenate(ws, axis=0).astype(BF16), ones_b)
    o = _dot_nt((qs * jnp.exp(b)).astype(BF16), st.astype(BF16))
    for t in range(SUB):
        row = jnp.sum(aexp[t * SUB:(t + 1) * SUB, :] * v, axis=0, keepdims=True)
        o = o + jnp.where(srow == t, row, 0.0)
    return o, _hgrn_state_step(st, k, v, b, bmask)


def _hgrn_tile(t):
    return min(256, t)


def _hgrn_fwd(proj, lb, ones_b, nb, name):
    n = proj.shape[0]
    t = n // nb
    tt = _hgrn_tile(t)
    nt = t // tt
    ncs = tt // CHUNK
    w = HGRN_W

    def body(q_ref, z_ref, v_ref, lb_ref, ones_ref, o_ref, s0_ref, st_s, b_s, qs_s, k_s):
        @pl.when(pl.program_id(1) == 0)
        def _():
            st_s[...] = jnp.zeros_like(st_s)

        q = q_ref[...]
        _, _, _, g, k, sq = _hgrn_gates(q, z_ref[...], lb_ref[...])
        b_s[...] = _dot_hi(_sub_tri(tt, True), g)
        qs_s[...] = q * sq
        k_s[...] = k
        bmask = _block_ones(w, F32)
        ones_b = ones_ref[...]

        def chunk(c, carry):
            st = st_s[...]
            s0_ref[c] = st
            base = pl.multiple_of(c * CHUNK, CHUNK)
            for u in range(CHUNK // SUB):
                rows = pl.ds(base + u * SUB, SUB)
                o, st = _hgrn_sub_fwd(qs_s[rows, :], k_s[rows, :], v_ref[rows, :], b_s[rows, :], st, ones_b, bmask)
                o_ref[rows, :] = o
            st_s[...] = st
            return carry

        lax.fori_loop(0, ncs, chunk, 0)

    def col(j):
        return pl.BlockSpec((tt, w), lambda b, i: (b * nt + i, j))

    return _pc(
        body, name, grid=(nb, nt),
        in_specs=[col(C_QA // w), col(C_FA // w), col(C_IA // w), pl.BlockSpec((1, w), lambda b, i: (0, 0)),
                  pl.BlockSpec((w, w), lambda b, i: (0, 0))],
        out_specs=[pl.BlockSpec((tt, w), lambda b, i: (b * nt + i, 0)),
                   pl.BlockSpec((ncs, w, w), lambda b, i: (b * nt + i, 0, 0))],
        out_shape=[_sds((n, w), F32), _sds((n // CHUNK, w, w), F32)],
        scratch_shapes=[pltpu.VMEM((w, w), F32)] + [pltpu.VMEM((tt, w), F32)] * 3,
        compiler_params=_params(),
    )(proj, proj, proj, lb, ones_b)


def _pool_lane_windows():
    lane = lax.broadcasted_iota(jnp.int32, (1, POOL_W), 1) // HEAD
    wl = jnp.zeros((1, POOL_W), F32)
    for gi, win in enumerate(POOL_WINDOWS):
        wl = jnp.where(lane == gi, float(win), wl)
    return lane, wl


def _pool_select(lane, parts):
    out = parts[-1]
    for gi in range(len(parts) - 2, -1, -1):
        out = jnp.where(lane == gi, parts[gi], out)
    return out


def _pool_mix(u, halo, t0, tt):
    lane, wl = _pool_lane_windows()
    ext = jnp.concatenate([halo, u], axis=0)
    sums, cur, shift = [], ext, 1
    for _ in POOL_WINDOWS:
        cur = cur + pltpu.roll(cur, shift, axis=0)
        sums.append(cur[POOL_HALO:, :])
        shift *= 2
    tpos = (t0 + lax.broadcasted_iota(jnp.int32, (tt, POOL_W), 0)).astype(F32)
    cnt = jnp.minimum(tpos + 1.0, wl)
    return _pool_select(lane, sums) / cnt - u, cnt


def _pool_specs(tt, nt, nhb):
    cu, cg = C_UB // POOL_W, C_GB // POOL_W
    return [pl.BlockSpec((tt, POOL_W), lambda b, i: (b * nt + i, cu)),
            pl.BlockSpec((tt, POOL_W), lambda b, i: (b * nt + i, cg)),
            pl.BlockSpec((POOL_HALO, POOL_W), lambda b, i: (jnp.maximum((b * nt + i) * nhb - 1, 0), cu))]


def _pool_fwd(proj, wbd, scale, nb, name):
    n = proj.shape[0]
    t = n // nb
    tt = min(256, t)
    nt = t // tt
    nhb = tt // POOL_HALO

    def body(u_ref, g_ref, h_ref, w_ref, s_ref, o_ref):
        i = pl.program_id(1)
        halo = jnp.where(i == 0, 0.0, h_ref[...])
        pooled, _ = _pool_mix(u_ref[...], halo, i * tt, tt)
        gv = g_ref[...]
        o_ref[...] = _dot(pooled.astype(BF16), w_ref[...]) * s_ref[...] * (gv * _sigmoid(gv))

    return _pc(
        body, name, grid=(nb, nt),
        in_specs=_pool_specs(tt, nt, nhb) + [pl.BlockSpec((POOL_W, POOL_W), lambda b, i: (0, 0)),
                                             pl.BlockSpec((1, POOL_W), lambda b, i: (0, 0))],
        out_specs=pl.BlockSpec((tt, POOL_W), lambda b, i: (b * nt + i, 0)),
        out_shape=_sds((n, POOL_W), F32),
        compiler_params=_params(),
    )(proj, proj, proj, wbd, scale)


def _rows_reduce(x, op, final):
    while x.shape[0] > 8 and x.shape[0] % 16 == 0:
        half = x.shape[0] // 2
        x = op(x[:half], x[half:])
    return final(x, axis=0, keepdims=True)


def _lane_lo():
    return lax.broadcasted_iota(jnp.int32, (1, 2 * HEAD), 1) < HEAD


def _put_col(tile, hh, colv):
    lane = lax.broadcasted_iota(jnp.int32, tile.shape, 1)
    return jnp.where(lane == hh, colv, tile)


def _fox_fwd(proj, c_col, c_row, nb, name, rider=None):
    n = proj.shape[0]
    t = n // nb
    tb = min(256, t)
    nq = t // tb
    pw = 2 * HEAD

    def body(q_ref, k_ref, v_ref, cc_ref, cr_ref, o_ref, lse_ref, m_s, l_s, acc_s):
        qi, kj = pl.program_id(1), pl.program_id(2)

        @pl.when(kj == 0)
        def _():
            m_s[...] = jnp.full_like(m_s, -jnp.inf)
            l_s[...] = jnp.zeros_like(l_s)
            acc_s[...] = jnp.zeros_like(acc_s)

        @pl.when(kj <= qi)
        def _():
            keys = kj * tb + lax.broadcasted_iota(jnp.int32, (tb, tb), 0)
            qrys = qi * tb + lax.broadcasted_iota(jnp.int32, (tb, tb), 1)
            causal = qrys >= keys
            lo = _lane_lo()
            scores, vts = [], []
            for p in range(FOX_HEADS // 2):
                sl = slice(p * pw, (p + 1) * pw)
                qp = q_ref[:, sl] * FOX_SCALE
                kp = k_ref[:, sl].astype(BF16)
                vts.append(v_ref[:, sl].T.astype(BF16))
                for h in range(2):
                    lm = lo if h == 0 else jnp.logical_not(lo)
                    scores.append(_dot_nt(kp, jnp.where(lm, qp, 0.0).astype(BF16)))
            m_old, l_old = m_s[...], l_s[...]
            m_rows, l_rows = [], []
            for hh in range(FOX_HEADS):
                s = scores[hh] + (cr_ref[hh:hh + 1, :] - cc_ref[:, hh:hh + 1])
                s = jnp.where(causal, s, MASK_VALUE)
                m_prev = m_old[hh:hh + 1, :]
                m_new = jnp.maximum(m_prev, _rows_reduce(s, jnp.maximum, jnp.max))
                alpha = jnp.exp(m_prev - m_new)
                pe = jnp.exp(s - m_new)
                l_rows.append(alpha * l_old[hh:hh + 1, :] + _rows_reduce(pe, jnp.add, jnp.sum))
                m_rows.append(m_new)
                rows = slice(hh * HEAD, (hh + 1) * HEAD)
                vt = vts[hh // 2][(hh % 2) * HEAD:(hh % 2 + 1) * HEAD, :]
                acc_s[rows, :] = alpha * acc_s[rows, :] + _dot(vt, pe.astype(BF16))
            m_s[...] = jnp.concatenate(m_rows, axis=0)
            l_s[...] = jnp.concatenate(l_rows, axis=0)

        @pl.when(kj == qi)
        def _():
            for p in range(FOX_HEADS // 2):
                parts = [acc_s[hh * HEAD:(hh + 1) * HEAD, :] * (1.0 / l_s[hh:hh + 1, :]) for hh in (2 * p, 2 * p + 1)]
                o_ref[:, p * pw:(p + 1) * pw] = jnp.concatenate(parts, axis=0).T
            lse_ref[...] = m_s[...] + jnp.log(l_s[...])

    def qspec(j):
        return pl.BlockSpec((tb, FOX_W), lambda b, qi, kj: (b * nq + qi, j))

    def kspec(wd, j):
        return pl.BlockSpec((tb, wd), lambda b, qi, kj: (b * nq + jnp.minimum(kj, qi), j))

    qrow = pl.BlockSpec((None, FOX_HEADS, tb), lambda b, qi, kj: (b, 0, qi))
    return _call(
        body, name, (proj, proj, proj, c_col, c_row), rider, grid=(nb, nq, nq),
        in_specs=[qspec(C_QC // FOX_W), kspec(FOX_W, C_KC // FOX_W), kspec(FOX_W, C_VC // FOX_W), kspec(FC_PAD, 0), qrow],
        out_specs=[qspec(0), qrow],
        out_shape=[_sds((n, FOX_W), F32), _sds((nb, FOX_HEADS, t), F32)],
        scratch_shapes=[pltpu.VMEM((FOX_HEADS, tb), F32), pltpu.VMEM((FOX_HEADS, tb), F32),
                        pltpu.VMEM((FOX_W, tb), F32)],
        compiler_params=_params(),
    )


def _head_mean(x, ones_f):
    return _dot_hi(x, ones_f) * (1.0 / HEAD)


def _merge_fwd(x, proj, o_h, o_b, o_c, gh, w_out, g_post, name):
    n, d = x.shape
    tm = min(256, n)

    def body(x_ref, ga_ref, gc_ref, oh_ref, ob_ref, oc_ref, gh_ref, w_ref, gp_ref, xo_ref, mixt_ref, y_ref):
        oh = oh_ref[...]
        ones_f = _block_ones(HGRN_W, F32)
        na = oh * lax.rsqrt(_head_mean(oh * oh, ones_f) + NORM_EPS) * gh_ref[...]
        ga, gc = ga_ref[...], gc_ref[...]
        mixed = jnp.concatenate([na * (ga * _sigmoid(ga)), ob_ref[...], oc_ref[...] * (gc * _sigmoid(gc))], axis=1)
        mixt_ref[...] = mixed.T.astype(BF16)
        y = _dot(mixed.astype(BF16), w_ref[...])
        y_ref[...] = y
        xo_ref[...] = x_ref[...] + y * lax.rsqrt(jnp.mean(y * y, axis=-1, keepdims=True) + NORM_EPS) * gp_ref[...]

    def row(wd, j=0):
        return pl.BlockSpec((tm, wd), lambda i: (i, j))

    def full(a, b):
        return pl.BlockSpec((a, b), lambda i: (0, 0))

    return _pc(
        body, name, grid=(n // tm,),
        in_specs=[row(d), row(HGRN_W, C_GA // HGRN_W), row(FOX_W, C_GC // FOX_W), row(HGRN_W), row(POOL_W), row(FOX_W),
                  full(1, HGRN_W), full(d, d), full(1, d)],
        out_specs=[row(d), pl.BlockSpec((d, tm), lambda i: (0, i)), row(d)],
        out_shape=[_sds((n, d), F32), _sds((d, n), BF16), _sds((n, d), F32)],
        compiler_params=_params(),
    )(x, proj, proj, o_h, o_b, o_c, gh, w_out, g_post)


def _loss_fwd_bwd(x, target, name):
    n, d = x.shape
    tm = min(512, n)

    def body(x_ref, t_ref, dx_ref, sq_ref):
        @pl.when(pl.program_id(0) == 0)
        def _():
            sq_ref[...] = jnp.zeros_like(sq_ref)

        e = x_ref[...] - t_ref[...]
        dx_ref[...] = e * (1.0 / d)
        sq_ref[...] += jnp.sum(e * e, axis=0, keepdims=True)

    return _pc(
        body, name, grid=(n // tm,),
        in_specs=[pl.BlockSpec((tm, d), lambda i: (i, 0))] * 2,
        out_specs=[pl.BlockSpec((tm, d), lambda i: (i, 0)), pl.BlockSpec((1, d), lambda i: (0, 0))],
        out_shape=[_sds((n, d), F32), _sds((1, d), F32)],
        compiler_params=_params(),
    )(x, target)


def _rms_bwd(dy_scaled, xhat, r):
    return r * (dy_scaled - xhat * jnp.mean(dy_scaled * xhat, axis=-1, keepdims=True))


def _merge_bwd(dxo, y, g_post, w_out_t, name):
    n, d = y.shape
    tm = min(256, n)

    def body(dx_ref, y_ref, gp_ref, wt_ref, dy_ref, dm_ref, dgp_ref):
        @pl.when(pl.program_id(0) == 0)
        def _():
            dgp_ref[...] = jnp.zeros_like(dgp_ref)

        yv, dxv = y_ref[...], dx_ref[...]
        r = lax.rsqrt(jnp.mean(yv * yv, axis=-1, keepdims=True) + NORM_EPS)
        yh = yv * r
        dgp_ref[...] += jnp.sum(dxv * yh, axis=0, keepdims=True)
        dyb = _rms_bwd(dxv * gp_ref[...], yh, r).astype(BF16)
        dy_ref[...] = dyb
        dm_ref[...] = _dot(dyb, wt_ref[...])

    row = pl.BlockSpec((tm, d), lambda i: (i, 0))
    return _pc(
        body, name, grid=(n // tm,),
        in_specs=[row, row, pl.BlockSpec((1, d), lambda i: (0, 0)), pl.BlockSpec((d, d), lambda i: (0, 0))],
        out_specs=[row, row, pl.BlockSpec((1, d), lambda i: (0, 0))],
        out_shape=[_sds((n, d), BF16), _sds((n, d), F32), _sds((1, d), F32)],
        compiler_params=_params(),
    )(dxo, y, g_post, w_out_t)


def _w_out_grad(mixt, dy, name):
    d, n = mixt.shape
    rows = d // N_DEV

    def body(a_ref, b_ref, o_ref):
        o_ref[...] = _dot(a_ref[...], b_ref[...]).astype(BF16)

    return _pc(
        body, name, grid=(N_DEV,),
        in_specs=[pl.BlockSpec((rows, n), lambda j: (j, 0)), pl.BlockSpec((n, d), lambda j: (0, 0))],
        out_specs=pl.BlockSpec((None, None, rows, d), lambda j: (j % 2, j // 2, 0, 0)),
        out_shape=_sds((2, N_DEV // 2, rows, d), BF16),
        compiler_params=_params(),
    )(mixt, dy)


def _w_in_grad(ht, pieces, name):
    d, n = ht.shape
    ta, tk = min(512, d), min(512, n)
    nk = n // tk
    arrays = [p for p, _ in pieces]
    widths = [p.shape[1] for p in arrays]
    offs = [sum(widths[:i]) for i in range(len(widths))]
    in_w = MAIN_W + FOX_HEADS
    shard = in_w // N_DEV

    def body(*refs):
        a_ref, p_refs = refs[0], refs[1:1 + len(arrays)]
        o_ref, acc = refs[1 + len(arrays):]
        k = pl.program_id(1)

        @pl.when(k == 0)
        def _():
            acc[...] = jnp.zeros_like(acc)

        a = a_ref[...]
        for pr, off, wd in zip(p_refs, offs, widths):
            for j in range(0, wd, 512):
                jw = min(512, wd - j)
                acc[:, off + j:off + j + jw] += _dot(a, pr[:, j:j + jw])

        @pl.when(k == nk - 1)
        def _():
            for j in range(N_DEV):
                o_ref[j % 2, j // 2] = acc[:, j * shard:(j + 1) * shard].astype(BF16)

    return _pc(
        body, name, grid=(d // ta, nk),
        in_specs=[pl.BlockSpec((ta, tk), lambda i, k: (i, k))] + [pl.BlockSpec((tk, wd), lambda i, k: (k, 0)) for wd in widths],
        out_specs=pl.BlockSpec((2, N_DEV // 2, ta, shard), lambda i, k: (0, 0, i, 0)),
        out_shape=_sds((2, N_DEV // 2, d, shard), BF16),
        scratch_shapes=[pltpu.VMEM((ta, sum(widths)), F32)],
        compiler_params=_params(),
    )(ht, *arrays)


def _hgrn_sub_bwd(qs, k, v, b, do, s0, ds1, ones_b, bmask):
    bl = b[SUB - 1:SUB, :]
    eb, ebl, ekt = jnp.exp(b), jnp.exp(bl), jnp.exp(bl - b)
    qe, ktil = qs * eb, k * ekt
    ds1b, dob = ds1.astype(BF16), do.astype(BF16)
    dv = _dot_nt(ktil.astype(BF16), ds1b)
    dqe = _dot(dob, s0.astype(BF16))
    dktil = _dot(v.astype(BF16), ds1b)
    dbl = jnp.sum(dktil * ktil, axis=0, keepdims=True) + ebl * jnp.sum(s0 * ds1, axis=0, keepdims=True)
    ds0 = ds1 * ebl + _dot_tn(dob, qe.astype(BF16)) * bmask
    srow, es, ws = _hgrn_decays(qs, k, b)
    aexp = _dot(jnp.concatenate(ws, axis=0).astype(BF16), ones_b)
    gexp = _dot(jnp.concatenate([do[t:t + 1, :] * v for t in range(SUB)], axis=0).astype(BF16), ones_b)
    dq = dqe * eb
    dk = dktil * ekt
    for t in range(SUB):
        sl = slice(t * SUB, (t + 1) * SUB)
        ge = gexp[sl, :] * es[t]
        dq = dq + jnp.where(srow == t, jnp.sum(ge * k, axis=0, keepdims=True), 0.0)
        dk = dk + ge * qs[t:t + 1, :]
        dv = dv + aexp[sl, :] * do[t:t + 1, :]
    return dq, dk, dv, dbl, ds0


def _hgrn_bwd(dmix, proj, o_h, s0, gh, lb, ones_b, nb, name):
    n = proj.shape[0]
    t = n // nb
    tt = _hgrn_tile(t)
    nt = t // tt
    ncs = tt // CHUNK
    nsub = CHUNK // SUB
    w = HGRN_W

    def body(dm_ref, q_ref, z_ref, v_ref, ga_ref, oh_ref, s0_ref, gh_ref, lb_ref, ones_ref,
             dp_ref, dgh_ref, dlb_ref, ds_s, ss_s, b_s, qs_s, k_s, do_s, dq_s, dk_s, dv_s, dbl_s):
        first = jnp.logical_and(pl.program_id(0) == 0, pl.program_id(1) == 0)

        @pl.when(first)
        def _():
            dgh_ref[...] = jnp.zeros_like(dgh_ref)
            dlb_ref[...] = jnp.zeros_like(dlb_ref)

        @pl.when(pl.program_id(1) == 0)
        def _():
            ds_s[...] = jnp.zeros_like(ds_s)

        ones_b = ones_ref[...]
        ones_f = ones_b.astype(F32)
        bmask = _block_ones(w, F32)
        lbv, ghv = lb_ref[...], gh_ref[...]
        oh, ga, dm = oh_ref[...], ga_ref[...], dm_ref[...]
        rn = lax.rsqrt(_head_mean(oh * oh, ones_f) + NORM_EPS)
        nh = oh * rn
        sga = _sigmoid(ga)
        dp_ref[:, 3 * w:4 * w] = (dm * nh * ghv * (sga * (1.0 + ga * (1.0 - sga)))).astype(BF16)
        dn = dm * (ga * sga)
        dgh_ref[...] += jnp.sum(dn * nh, axis=0, keepdims=True)
        dn = dn * ghv
        do_s[...] = rn * (dn - nh * _head_mean(dn * nh, ones_f))
        q = q_ref[...]
        sig, sn, f, g, k, sq = _hgrn_gates(q, z_ref[...], lbv)
        qs = q * sq
        b_s[...] = _dot_hi(_sub_tri(tt, True), g)
        qs_s[...] = qs
        k_s[...] = k

        def chunk(cc, carry):
            c = ncs - 1 - cc
            base = pl.multiple_of(c * CHUNK, CHUNK)
            st = s0_ref[c]
            for u in range(nsub):
                ss_s[u] = st
                if u < nsub - 1:
                    rows = pl.ds(base + u * SUB, SUB)
                    st = _hgrn_state_step(st, k_s[rows, :], v_ref[rows, :], b_s[rows, :], bmask)
            ds = ds_s[...]
            for u in reversed(range(nsub)):
                rows = pl.ds(base + u * SUB, SUB)
                dq, dk, dv, dbl, ds = _hgrn_sub_bwd(qs_s[rows, :], k_s[rows, :], v_ref[rows, :], b_s[rows, :],
                                                    do_s[rows, :], ss_s[u], ds, ones_b, bmask)
                dq_s[rows, :] = dq
                dk_s[rows, :] = dk
                dv_s[rows, :] = dv
                dbl_s[rows, :] = jnp.broadcast_to(dbl, (SUB, w))
            ds_s[...] = ds
            return carry

        lax.fori_loop(0, ncs, chunk, 0)
        dqs, dk = dq_s[...], dk_s[...]
        dg = _dot_hi(_sub_tri(tt, False), qs * dqs - k * dk) + dbl_s[...]
        dfz = jnp.where(f > TINY, dg / jnp.maximum(f, TINY), 0.0)
        dlb_ref[...] += jnp.sum(dfz * (1.0 - sig) - dk * sn, axis=0, keepdims=True)
        dp_ref[:, 0:w] = (dqs * (sq * (1.0 + q * (1.0 - sq)))).astype(BF16)
        dp_ref[:, w:2 * w] = ((dfz - dk) * (1.0 - lbv) * sig * sn).astype(BF16)
        dp_ref[:, 2 * w:3 * w] = dv_s[...].astype(BF16)

    def rv(b, i):
        return b * nt + (nt - 1 - i)

    def col(j):
        return pl.BlockSpec((tt, w), lambda b, i: (rv(b, i), j))

    def full(a, bb):
        return pl.BlockSpec((a, bb), lambda b, i: (0, 0))

    return _pc(
        body, name, grid=(nb, nt),
        in_specs=[col(0), col(C_QA // w), col(C_FA // w), col(C_IA // w), col(C_GA // w), col(0),
                  pl.BlockSpec((ncs, w, w), lambda b, i: (rv(b, i), 0, 0)), full(1, w), full(1, w), full(w, w)],
        out_specs=[pl.BlockSpec((tt, 4 * w), lambda b, i: (rv(b, i), 0)), full(1, w), full(1, w)],
        out_shape=[_sds((n, 4 * w), BF16), _sds((1, w), F32), _sds((1, w), F32)],
        scratch_shapes=[pltpu.VMEM((w, w), F32), pltpu.VMEM((nsub, w, w), F32)] + [pltpu.VMEM((tt, w), F32)] * 8,
        compiler_params=_params(),
    )(dmix, proj, proj, proj, proj, o_h, s0, gh, lb, ones_b)


def _pool_bwd(dmix, proj, wbd, wbd_t, scale, nb, name):
    n = proj.shape[0]
    t = n // nb
    tt = min(256, t)
    nt = t // tt
    nhb = tt // POOL_HALO
    cu, cg, cm = C_UB // POOL_W, C_GB // POOL_W, HGRN_W // POOL_W

    def body(u_ref, g_ref, h_ref, dm_ref, gn_ref, dmn_ref, w_ref, wt_ref, s_ref, dp_ref, dw_ref, ds_ref):
        i = pl.program_id(1)
        first = jnp.logical_and(pl.program_id(0) == 0, i == 0)

        @pl.when(first)
        def _():
            dw_ref[...] = jnp.zeros_like(dw_ref)
            ds_ref[...] = jnp.zeros_like(ds_ref)

        sc = s_ref[...]
        halo = jnp.where(i == 0, 0.0, h_ref[...])
        pooled, cnt = _pool_mix(u_ref[...], halo, i * tt, tt)
        pb = pooled.astype(BF16)
        pre = _dot(pb, w_ref[...])
        gv, dm = g_ref[...], dm_ref[...]
        sg = _sigmoid(gv)
        silu = gv * sg
        dgb = dm * pre * sc * (sg * (1.0 + gv * (1.0 - sg)))
        ds_ref[...] += jnp.sum(dm * pre * silu, axis=0, keepdims=True)
        dpre = (dm * sc * silu).astype(BF16)
        dw_ref[...] += _dot_tn(pb, dpre)
        dpool = _dot(dpre, wt_ref[...])
        gn = gn_ref[...]
        dpre_n = (dmn_ref[...] * sc * (gn * _sigmoid(gn))).astype(BF16)
        dpool_n = jnp.where(i == nt - 1, 0.0, _dot(dpre_n, wt_ref[...]))
        lane, wl = _pool_lane_windows()
        tpos_n = ((i + 1) * tt + lax.broadcasted_iota(jnp.int32, (POOL_HALO, POOL_W), 0)).astype(F32)
        ext = jnp.concatenate([dpool / cnt, dpool_n / jnp.minimum(tpos_n + 1.0, wl)], axis=0)
        rows = tt + POOL_HALO
        sums, cur, shift = [], ext, 1
        for _ in POOL_WINDOWS:
            cur = cur + pltpu.roll(cur, rows - shift, axis=0)
            sums.append(cur[:tt, :])
            shift *= 2
        du = _pool_select(lane, sums) - dpool
        dp_ref[...] = jnp.concatenate([du, dgb], axis=1).astype(BF16)

    def nxt(b, i):
        return jnp.minimum((b * nt + i + 1) * nhb, n // POOL_HALO - 1)

    return _pc(
        body, name, grid=(nb, nt),
        in_specs=_pool_specs(tt, nt, nhb) + [
            pl.BlockSpec((tt, POOL_W), lambda b, i: (b * nt + i, cm)),
            pl.BlockSpec((POOL_HALO, POOL_W), lambda b, i: (nxt(b, i), cg)),
            pl.BlockSpec((POOL_HALO, POOL_W), lambda b, i: (nxt(b, i), cm)),
            pl.BlockSpec((POOL_W, POOL_W), lambda b, i: (0, 0)), pl.BlockSpec((POOL_W, POOL_W), lambda b, i: (0, 0)),
            pl.BlockSpec((1, POOL_W), lambda b, i: (0, 0))],
        out_specs=[pl.BlockSpec((tt, 2 * POOL_W), lambda b, i: (b * nt + i, 0)),
                   pl.BlockSpec((POOL_W, POOL_W), lambda b, i: (0, 0)), pl.BlockSpec((1, POOL_W), lambda b, i: (0, 0))],
        out_shape=[_sds((n, 2 * POOL_W), BF16), _sds((POOL_W, POOL_W), F32), _sds((1, POOL_W), F32)],
        compiler_params=_params(),
    )(proj, proj, proj, dmix, proj, dmix, wbd, wbd_t, scale)


def _fox_gate_bwd(dmix, proj, o_c, name):
    n = proj.shape[0]
    tm = min(256, n)

    def body(dm_ref, gc_ref, oc_ref, da_ref, dg_ref, dl_ref):
        dm, gc, oc = dm_ref[...], gc_ref[...], oc_ref[...]
        sg = _sigmoid(gc)
        da = dm * (gc * sg)
        da_ref[...] = da.astype(BF16)
        dg_ref[...] = (dm * oc * (sg * (1.0 + gc * (1.0 - sg)))).astype(BF16)
        r = lax.broadcasted_iota(jnp.int32, (FOX_W, FC_PAD), 0) // HEAD
        c = lax.broadcasted_iota(jnp.int32, (FOX_W, FC_PAD), 1)
        dl_ref[...] = _dot_hi(da * oc, (r == c).astype(F32))

    def row(wd, j=0):
        return pl.BlockSpec((tm, wd), lambda i: (i, j))

    return _pc(
        body, name, grid=(n // tm,),
        in_specs=[row(FOX_W, (HGRN_W + POOL_W) // FOX_W), row(FOX_W, C_GC // FOX_W), row(FOX_W)],
        out_specs=[row(FOX_W), row(FOX_W), row(FC_PAD)],
        out_shape=[_sds((n, FOX_W), BF16), _sds((n, FOX_W), BF16), _sds((n, FC_PAD), F32)],
        compiler_params=_params(),
    )(dmix, proj, o_c)


def _fox_bwd(proj, da, c_col, c_row, lse_row, delta_row, nb, name, rider=None):
    n = proj.shape[0]
    t = n // nb
    tb = min(256, t)
    nq = t // tb
    pw = 2 * HEAD

    def body(q_ref, k_ref, v_ref, da_ref, cc_ref, cr_ref, lse_ref, dl_ref,
             dq_ref, dk_ref, dv_ref, dck_ref, dcq_ref, dq_s, dk_s, dv_s, dck_s, dcq_s):
        kj, qi = pl.program_id(1), pl.program_id(2)

        @pl.when(jnp.logical_and(kj == 0, qi == 0))
        def _():
            dq_s[...] = jnp.zeros_like(dq_s)
            dcq_s[...] = jnp.zeros_like(dcq_s)

        @pl.when(qi == 0)
        def _():
            dk_s[...] = jnp.zeros_like(dk_s)
            dv_s[...] = jnp.zeros_like(dv_s)
            dck_s[...] = jnp.zeros_like(dck_s)

        @pl.when(qi >= kj)
        def _():
            keys = kj * tb + lax.broadcasted_iota(jnp.int32, (tb, tb), 0)
            qrys = qi * tb + lax.broadcasted_iota(jnp.int32, (tb, tb), 1)
            causal = qrys >= keys
            lo = _lane_lo()
            dck = dck_s[...]
            for p in range(FOX_HEADS // 2):
                sl = slice(p * pw, (p + 1) * pw)
                qp = q_ref[:, sl] * FOX_SCALE
                kf = k_ref[:, sl]
                kp = kf.astype(BF16)
                kt = kf.T.astype(BF16)
                vp = v_ref[:, sl].astype(BF16)
                dap = da_ref[:, sl]
                dk, dv = dk_s[:, sl], dv_s[:, sl]
                for h in range(2):
                    hh = 2 * p + h
                    lm = lo if h == 0 else jnp.logical_not(lo)
                    qm = jnp.where(lm, qp, 0.0).astype(BF16)
                    dam = jnp.where(lm, dap, jnp.zeros_like(dap))
                    s = _dot_nt(kp, qm) + (cr_ref[hh:hh + 1, :] - cc_ref[:, hh:hh + 1])
                    pe = jnp.where(causal, jnp.exp(s - lse_ref[hh:hh + 1, :]), 0.0)
                    dp = _dot_nt(vp, dam)
                    ds = pe * (dp - dl_ref[hh:hh + 1, :])
                    dsb = ds.astype(BF16)
                    dv = dv + _dot(pe.astype(BF16), dam)
                    dk = dk + _dot(dsb, qm)
                    rows = slice(hh * HEAD, (hh + 1) * HEAD)
                    dq_s[qi, rows, :] += _dot(kt[h * HEAD:(h + 1) * HEAD, :], dsb)
                    dck = dck - _put_col(jnp.zeros_like(dck), hh, jnp.sum(ds, axis=1, keepdims=True))
                    dcq_s[qi, hh:hh + 1, :] += _rows_reduce(ds, jnp.add, jnp.sum)
                dk_s[:, sl] = dk
                dv_s[:, sl] = dv
            dck_s[...] = dck

        @pl.when(qi == nq - 1)
        def _():
            dk_ref[...] = dk_s[...].astype(BF16)
            dv_ref[...] = dv_s[...].astype(BF16)
            dck_ref[...] = dck_s[...]

        @pl.when(jnp.logical_and(kj == nq - 1, qi == nq - 1))
        def _():
            for j in range(nq):
                dq_ref[j * tb:(j + 1) * tb, :] = (dq_s[j].T * FOX_SCALE).astype(BF16)
                dcq_ref[:, j * tb:(j + 1) * tb] = dcq_s[j]

    def kspec(wd, j=0):
        return pl.BlockSpec((tb, wd), lambda b, kj, qi: (b * nq + kj, j))

    def qspec(wd, j=0):
        return pl.BlockSpec((tb, wd), lambda b, kj, qi: (b * nq + jnp.maximum(qi, kj), j))

    def qrow():
        return pl.BlockSpec((None, FOX_HEADS, tb), lambda b, kj, qi: (b, 0, jnp.maximum(qi, kj)))

    return _call(
        body, name, (proj, proj, proj, da, c_col, c_row, lse_row, delta_row), rider, grid=(nb, nq, nq),
        in_specs=[qspec(FOX_W, C_QC // FOX_W), kspec(FOX_W, C_KC // FOX_W), kspec(FOX_W, C_VC // FOX_W), qspec(FOX_W),
                  kspec(FC_PAD), qrow(), qrow(), qrow()],
        out_specs=[pl.BlockSpec((t, FOX_W), lambda b, kj, qi: (b, 0)), kspec(FOX_W), kspec(FOX_W), kspec(FC_PAD),
                   pl.BlockSpec((None, FOX_HEADS, t), lambda b, kj, qi: (b, 0, 0))],
        out_shape=[_sds((n, FOX_W), BF16), _sds((n, FOX_W), BF16), _sds((n, FOX_W), BF16), _sds((n, FC_PAD), F32),
                   _sds((nb, FOX_HEADS, t), F32)],
        scratch_shapes=[pltpu.VMEM((nq, FOX_W, tb), F32), pltpu.VMEM((tb, FOX_W), F32), pltpu.VMEM((tb, FOX_W), F32),
                        pltpu.VMEM((tb, FC_PAD), F32), pltpu.VMEM((nq, FOX_HEADS, tb), F32)],
        compiler_params=_params(),
    )


def _fox_decay_bwd(dc_q, dc_k, fc, bias, nb, name):
    n = fc.shape[0]
    t = n // nb
    tt = min(256, t)
    nt = t // tt

    def body(dcq_ref, dck_ref, fc_ref, b_ref, dfc_ref, db_ref, carry):
        i = pl.program_id(1)
        first = jnp.logical_and(pl.program_id(0) == 0, i == 0)

        @pl.when(first)
        def _():
            db_ref[...] = jnp.zeros_like(db_ref)

        @pl.when(i == 0)
        def _():
            carry[...] = jnp.zeros_like(carry)

        r = lax.broadcasted_iota(jnp.int32, (tt, tt), 0)
        cc = lax.broadcasted_iota(jnp.int32, (tt, tt), 1)
        dlf = _dot_hi((r <= cc).astype(F32), dcq_ref[...] + dck_ref[...]) + carry[...]
        carry[...] = dlf[0:1, :]
        dfc = dlf * _sigmoid(-(fc_ref[...] + b_ref[...]))
        dfc_ref[...] = dfc.astype(BF16)
        db_ref[...] += jnp.sum(dfc, axis=0, keepdims=True)

    def row():
        return pl.BlockSpec((tt, FC_PAD), lambda b, i: (b * nt + (nt - 1 - i), 0))

    return _pc(
        body, name, grid=(nb, nt),
        in_specs=[row(), row(), row(), pl.BlockSpec((1, FC_PAD), lambda b, i: (0, 0))],
        out_specs=[row(), pl.BlockSpec((1, FC_PAD), lambda b, i: (0, 0))],
        out_shape=[_sds((n, FC_PAD), BF16), _sds((1, FC_PAD), F32)],
        scratch_shapes=[pltpu.VMEM((1, FC_PAD), F32)],
        compiler_params=_params(),
    )(dc_q, dc_k, fc, bias)


def _in_proj_bwd(pieces, w_main_t, w_fc_t, x, g_pre, dxo, name, rider=None):
    n, d = x.shape
    tm = min(256, n)
    widths = [p.shape[1] for p, _ in pieces]
    offs = [o for _, o in pieces]
    np_ = len(pieces)

    def body(*refs):
        p_refs = refs[:np_]
        wt_ref, wf_ref, x_ref, g_ref, dxo_ref, dx_ref, dg_ref = refs[np_:]

        @pl.when(pl.program_id(0) == 0)
        def _():
            dg_ref[...] = jnp.zeros_like(dg_ref)

        dh = _dot(p_refs[-1][...], wf_ref[...])
        for pr, wd, off in zip(p_refs[:-1], widths[:-1], offs[:-1]):
            for j in range(0, wd, 512):
                jw = min(512, wd - j)
                dh = dh + _dot(pr[:, j:j + jw], wt_ref[off + j:off + j + jw, :])
        xv = x_ref[...]
        r = lax.rsqrt(jnp.mean(xv * xv, axis=-1, keepdims=True) + NORM_EPS)
        xh = xv * r
        dg_ref[...] += jnp.sum(dh * xh, axis=0, keepdims=True)
        dx_ref[...] = dxo_ref[...] + _rms_bwd(dh * g_ref[...], xh, r)

    row = pl.BlockSpec((tm, d), lambda i: (i, 0))
    return _call(
        body, name, (*[p for p, _ in pieces], w_main_t, w_fc_t, x, g_pre, dxo), rider, grid=(n // tm,),
        in_specs=[pl.BlockSpec((tm, wd), lambda i: (i, 0)) for wd in widths] + [
            pl.BlockSpec((MAIN_W, d), lambda i: (0, 0)), pl.BlockSpec((FC_PAD, d), lambda i: (0, 0)),
            row, pl.BlockSpec((1, d), lambda i: (0, 0)), row],
        out_specs=[row, pl.BlockSpec((1, d), lambda i: (0, 0))],
        out_shape=[_sds((n, d), F32), _sds((1, d), F32)],
        compiler_params=_params(),
    )


def _lower_bound_table(lower_bounds, name):
    depth, w = lower_bounds.shape

    def body(lb_ref, o_ref):
        v = lb_ref[...]
        e = jnp.exp(v - jnp.max(v, axis=0, keepdims=True))
        p = e / jnp.sum(e, axis=0, keepdims=True)
        acc = jnp.zeros((1, w), F32)
        for l in range(depth):
            acc = acc + p[l:l + 1, :]
            o_ref[l:l + 1, :] = acc - p[0:1, :]

    return _pc(body, name, out_shape=_sds((depth, w), F32))(lower_bounds)


def _lower_bound_bwd(lower_bounds, dlbs, name):
    depth, w = lower_bounds.shape

    def body(lb_ref, d_ref, o_ref):
        v, dl = lb_ref[...], d_ref[...]
        e = jnp.exp(v - jnp.max(v, axis=0, keepdims=True))
        p = e / jnp.sum(e, axis=0, keepdims=True)
        tot = jnp.sum(dl, axis=0, keepdims=True)
        rows, tail = [], tot
        for l in range(depth):
            rows.append(tail - tot if l == 0 else tail)
            tail = tail - dl[l:l + 1, :]
        dp = jnp.concatenate(rows, axis=0)
        o_ref[...] = p * (dp - jnp.sum(p * dp, axis=0, keepdims=True))

    return _pc(body, name, out_shape=_sds((depth, w), F32))(lower_bounds, dlbs)


def _place():
    x, y, c = lax.axis_index("x"), lax.axis_index("y"), lax.axis_index("c")
    return x, y, c


def _gather_weights(w_in_b, w_out_b):
    arrays = (w_in_b, w_out_b)
    na = len(arrays)

    def body(*refs):
        ins, outs = refs[:na], refs[na:2 * na]
        send_sems, recv_sems, local_sems = refs[2 * na:]
        x, y, c = _place()
        me, sibling = (x, y, c), (x, y, 1 - c)
        chips = [(1 - x, y), (x, 1 - y), (1 - x, 1 - y)]

        def slot(a, px, py, pc):
            return outs[a].at[4 * px + 2 * py + pc]

        def copy(a, k, block, to, own=False):
            return pltpu.make_async_remote_copy(
                src_ref=ins[a] if own else slot(a, *block), dst_ref=slot(a, *block),
                send_sem=send_sems.at[a * 7 + k], recv_sem=recv_sems.at[a * 7 + k],
                device_id=to, device_id_type=MESH)

        mine = [pltpu.make_async_copy(ins[a], slot(a, *me), local_sems.at[a]) for a in range(na)]
        for cp in mine:
            cp.start()
        first = []
        for a in range(na):
            first.append(copy(a, 0, me, sibling, own=True))
            first += [copy(a, 1 + j, me, (*chip, c), own=True) for j, chip in enumerate(chips)]
        for cp in first:
            cp.start()
        passed = []
        for j, chip in enumerate(chips):
            for a in range(na):
                copy(a, 1 + j, (*chip, c), me).wait_recv()
                fw = copy(a, 4 + j, (*chip, c), sibling)
                fw.start()
                passed.append(fw)
        for a in range(na):
            copy(a, 0, sibling, me).wait_recv()
            for j, chip in enumerate(chips):
                copy(a, 4 + j, (*chip, 1 - c), me).wait_recv()
        for cp in first + passed:
            cp.wait_send()
        for cp in mine:
            cp.wait()

    any_spec = pl.BlockSpec(memory_space=pl.ANY)
    return _pc(
        body, "gather_weights",
        in_specs=[any_spec] * na, out_specs=[any_spec] * na,
        out_shape=[_sds((N_DEV,) + a.shape, a.dtype) for a in arrays],
        scratch_shapes=[pltpu.SemaphoreType.DMA((7 * na,)), pltpu.SemaphoreType.DMA((7 * na,)),
                        pltpu.SemaphoreType.DMA((na,))],
    )(*arrays)


def _peer(k):
    x, y, c = _place()
    return (1 - x if k & 4 else x, 1 - y if k & 2 else y, 1 - c if k & 1 else c)


def _remote(src, dst, sems, s, to):
    return pltpu.make_async_remote_copy(src_ref=src, dst_ref=dst, send_sem=sems[0].at[s], recv_sem=sems[1].at[s],
                                        device_id=to, device_id_type=MESH)


def _gather_rider(shards):
    na = len(shards)

    def plan(ins, outs, *sems):
        x, y, c = _place()
        me = 4 * x + 2 * y + c
        locs = [pltpu.make_async_copy(ins[a], outs[a].at[me], sems[2].at[a]) for a in range(na)]
        sends, recvs = [], []
        for k in range(1, N_DEV):
            px, py, pc = _peer(k)
            for a in range(na):
                s = (k - 1) * na + a
                sends.append(_remote(ins[a], outs[a].at[me], sems, s, (px, py, pc)))
                recvs.append(_remote(ins[a], outs[a].at[4 * px + 2 * py + pc], sems, s, (px, py, pc)))
        return sends, recvs, locs

    return _Rider(shards, [_sds((N_DEV,) + a.shape, a.dtype) for a in shards], (N_DEV - 1) * na, na, plan)


def _swap_rider(halves):
    na = len(halves)

    def plan(ins, outs, *sems):
        x, y, c = _place()
        cps = [_remote(ins[a].at[1 - c], outs[a], sems, a, (x, y, 1 - c)) for a in range(na)]
        return cps, cps, []

    return _Rider(halves, [_sds(a.shape[1:], a.dtype) for a in halves], na, 1, plan)


def _chip_exchange_rider(parts, small=None):
    na = len(parts)
    n_chip = N_DEV // 2

    def plan(ins, outs, *sems):
        x, y, c = _place()
        chip = 2 * x + y
        locs = [pltpu.make_async_copy(ins[a].at[chip], outs[a].at[chip], sems[2].at[a]) for a in range(na)]
        sends, recvs = [], []
        for k in range(1, n_chip):
            px, py, _ = _peer(2 * k)
            for a in range(na):
                s = (k - 1) * na + a
                sends.append(_remote(ins[a].at[2 * px + py], outs[a].at[chip], sems, s, (px, py, c)))
                recvs.append(_remote(ins[a].at[2 * px + py], outs[a].at[2 * px + py], sems, s, (px, py, c)))
        if small is not None:
            me = 2 * chip + c
            locs.append(pltpu.make_async_copy(ins[na], outs[na].at[me], sems[2].at[na]))
            for k in range(1, N_DEV):
                px, py, pc = _peer(k)
                s = (n_chip - 1) * na + k - 1
                sends.append(_remote(ins[na], outs[na].at[me], sems, s, (px, py, pc)))
                recvs.append(_remote(ins[na], outs[na].at[4 * px + 2 * py + pc], sems, s, (px, py, pc)))
        return sends, recvs, locs

    extra = [] if small is None else [small]
    shapes = [_sds(a.shape, a.dtype) for a in parts] + [_sds((N_DEV,) + s.shape, s.dtype) for s in extra]
    n_sems = (n_chip - 1) * na + (N_DEV - 1) * len(extra)
    return _Rider(list(parts) + extra, shapes, n_sems, na + len(extra), plan)


def _pair_add(halves, other, core, name):
    _, nch, r, c = halves.shape
    tr = 256 if r % 256 == 0 else r

    def body(h_ref, o_ref, c_ref, p_ref):
        mine = jnp.where(c_ref[...] == 0, h_ref[0].astype(F32), h_ref[1].astype(F32))
        p_ref[...] = (mine + o_ref[...].astype(F32)).astype(BF16)

    blk = pl.BlockSpec((None, tr, c), lambda j, i: (j, i, 0))
    return _pc(
        body, name, grid=(nch, r // tr),
        in_specs=[pl.BlockSpec((2, None, tr, c), lambda j, i: (0, j, i, 0)), blk, pl.BlockSpec((1, 1), lambda j, i: (0, 0))],
        out_specs=blk, out_shape=_sds((nch, r, c), BF16),
        compiler_params=_params(),
    )(halves, other, core)


def _sum_adamw(parts, w, m, v, name):
    nl, r, c = w.shape
    tr = 256 if r % 256 == 0 else r

    def body(*refs):
        p_refs = refs[:nl]
        w_ref, m_ref, v_ref, g_ref, d_ref, mo_ref, vo_ref = refs[nl:]
        for l in range(nl):
            @pl.when(pl.program_id(0) == l)
            def _(p_ref=p_refs[l]):
                g = p_ref[0].astype(F32)
                for j in range(1, p_ref.shape[0]):
                    g = g + p_ref[j].astype(F32)
                mn = ADAM_B1 * m_ref[...] + (1.0 - ADAM_B1) * g
                vn = ADAM_B2 * v_ref[...] + (1.0 - ADAM_B2) * (g * g)
                m_hat = mn / (1.0 - ADAM_B1 ** ADAM_STEP)
                v_hat = vn / (1.0 - ADAM_B2 ** ADAM_STEP)
                g_ref[...] = g
                d_ref[...] = -ADAM_LR * (m_hat / (jnp.sqrt(v_hat) + ADAM_EPS) + ADAM_WD * w_ref[...])
                mo_ref[...] = mn
                vo_ref[...] = vn

    def part_spec(l, k):
        return pl.BlockSpec((k, tr, c), lambda li, i: (0, jnp.where(li == l, i, 0), 0))

    row = pl.BlockSpec((None, tr, c), lambda li, i: (li, i, 0))
    return _pc(
        body, name, grid=(nl, r // tr),
        in_specs=[part_spec(l, p.shape[0]) for l, p in enumerate(parts)] + [row, row, row],
        out_specs=[row] * 4,
        out_shape=[_sds((nl, r, c), F32)] * 4,
        compiler_params=_params(),
    )(*parts, w, m, v)


SMALL = ("lower_bounds", "pre_norm_g", "hgrn_norm_g", "fox_f_bias", "pool_w", "pool_scale", "post_norm_g")
SMALL_LANES = 128


def _pack_small(tree):
    flat = jnp.concatenate([tree[k].reshape(-1) for k in SMALL])
    rows = -(-flat.shape[0] // (8 * SMALL_LANES)) * 8
    return jnp.pad(flat, (0, rows * SMALL_LANES - flat.shape[0])).reshape(rows, SMALL_LANES)


def _unpack_small(packed, like):
    flat, out, off = packed.reshape(-1), {}, 0
    for k in SMALL:
        size = like[k].size
        out[k] = flat[off:off + size].reshape(like[k].shape)
        off += size
    return out


def _block_diag(pw):
    g = pw.shape[0]
    eye = jnp.eye(g, dtype=pw.dtype)
    return (eye[:, None, :, None] * pw[:, :, None, :]).reshape(g * HEAD, g * HEAD)


def _layer_weights(g_in, g_out):
    d = g_in.shape[1]
    in_w = N_DEV * g_in.shape[2]
    full_in = g_in.transpose(1, 0, 2).reshape(d, in_w)
    w_main = full_in[:, :MAIN_W]
    w_fc = jnp.pad(full_in[:, MAIN_W:], ((0, 0), (0, FC_PAD - (in_w - MAIN_W))))
    full_out = g_out.reshape(N_DEV * g_out.shape[1], d)
    return w_main, w_fc, w_main.T, w_fc.T, full_out, full_out.T


def _layer_fwd(l, x, lbs, weights, lw, nb, rider=None):
    n = x.shape[0]
    t = n // nb
    w_main, w_fc, _, _, w_out, _ = lw
    bias = jnp.pad(weights["fox_f_bias"][l:l + 1], ((0, 0), (0, FC_PAD - FOX_HEADS)))
    wbd = _block_diag(weights["pool_w"][l]).astype(BF16)
    proj, fc, ht = _in_proj_fwd(x, weights["pre_norm_g"][l:l + 1], w_main, w_fc, f"in_proj_fwd_{l}")
    c_col = _fox_decay_fwd(fc, bias, nb, f"fox_decay_fwd_{l}")
    c_row = c_col.reshape(nb, t, FC_PAD)[:, :, :FOX_HEADS].transpose(0, 2, 1)
    o_h, s0 = _hgrn_fwd(proj, lbs[l:l + 1], _block_ones(HGRN_W, BF16), nb, f"hgrn_fwd_{l}")
    o_b = _pool_fwd(proj, wbd, weights["pool_scale"][l:l + 1], nb, f"pool_fwd_{l}")
    (o_c, lse), rode = _fox_fwd(proj, c_col, c_row, nb, f"fox_fwd_{l}", rider)
    x_next, mixt, y = _merge_fwd(x, proj, o_h, o_b, o_c, weights["hgrn_norm_g"][l:l + 1], w_out,
                                 weights["post_norm_g"][l:l + 1], f"merge_fwd_{l}")
    return x_next, (x, proj, fc, ht, c_col, c_row, o_h, s0, o_c, lse, mixt, y, bias, wbd), rode


def _layer_bwd(l, dx, saved, lbs, weights, lw, nb, rider=None):
    _, proj, fc, ht, c_col, c_row, o_h, s0, o_c, lse, mixt, y, bias, wbd = saved
    n = proj.shape[0]
    t = n // nb
    w_out_t = lw[5]
    g = {}
    dy, dmix, dgp = _merge_bwd(dx, y, weights["post_norm_g"][l:l + 1], w_out_t, f"merge_bwd_{l}")
    g["post_norm_g"] = dgp[0]
    g["w_out"] = _w_out_grad(mixt, dy, f"w_out_grad_{l}")
    d_a, dgh, dlb = _hgrn_bwd(dmix, proj, o_h, s0, weights["hgrn_norm_g"][l:l + 1], lbs[l:l + 1],
                              _block_ones(HGRN_W, BF16), nb, f"hgrn_bwd_{l}")
    g["hgrn_norm_g"], g["lbs"] = dgh[0], dlb[0]
    d_b, dwbd, dps = _pool_bwd(dmix, proj, wbd, wbd.T, weights["pool_scale"][l:l + 1], nb, f"pool_bwd_{l}")
    g["pool_w"] = jnp.stack([dwbd[j * HEAD:(j + 1) * HEAD, j * HEAD:(j + 1) * HEAD] for j in range(len(POOL_WINDOWS))])
    g["pool_scale"] = dps[0]
    da, d_gc, delta = _fox_gate_bwd(dmix, proj, o_c, f"fox_gate_bwd_{l}")
    delta_row = delta.reshape(nb, t, FC_PAD)[:, :, :FOX_HEADS].transpose(0, 2, 1)
    (d_qc, d_kc, d_vc, dc_k, dc_q), rode = _fox_bwd(proj, da, c_col, c_row, lse, delta_row, nb, f"fox_bwd_{l}", rider)
    dc_q = jnp.pad(dc_q.transpose(0, 2, 1).reshape(n, FOX_HEADS), ((0, 0), (0, FC_PAD - FOX_HEADS)))
    d_fc, dbias = _fox_decay_bwd(dc_q, dc_k, fc, bias, nb, f"fox_decay_bwd_{l}")
    g["fox_f_bias"] = dbias[0, :FOX_HEADS]
    pieces = [(d_a, C_QA), (d_b, C_UB), (d_qc, C_QC), (d_kc, C_KC), (d_vc, C_VC), (d_gc, C_GC), (d_fc, None)]
    g["w_in"] = _w_in_grad(ht, pieces, f"w_in_grad_{l}")
    return g, pieces, rode


def _layer_bwd_input(l, dx, pieces, saved, weights, lw, rider=None):
    (dxi, dgpre), rode = _in_proj_bwd(pieces, lw[2], lw[3], saved[0], weights["pre_norm_g"][l:l + 1], dx,
                                      f"in_proj_bwd_{l}", rider)
    return dxi, dgpre[0], rode


def kernel(x, lower_bounds, pre_norm_g, w_in, hgrn_norm_g, fox_f_bias, pool_w, pool_scale, w_out, post_norm_g, loss_target, m_lower_bounds, m_pre_norm_g, m_w_in, m_hgrn_norm_g, m_fox_f_bias, m_pool_w, m_pool_scale, m_w_out, m_post_norm_g, v_lower_bounds, v_pre_norm_g, v_w_in, v_hgrn_norm_g, v_fox_f_bias, v_pool_w, v_pool_scale, v_w_out, v_post_norm_g):
    weights = dict(lower_bounds=lower_bounds, pre_norm_g=pre_norm_g, hgrn_norm_g=hgrn_norm_g, fox_f_bias=fox_f_bias,
                   pool_w=pool_w, pool_scale=pool_scale, post_norm_g=post_norm_g)
    mom_m = dict(lower_bounds=m_lower_bounds, pre_norm_g=m_pre_norm_g, hgrn_norm_g=m_hgrn_norm_g, fox_f_bias=m_fox_f_bias,
                 pool_w=m_pool_w, pool_scale=m_pool_scale, post_norm_g=m_post_norm_g)
    mom_v = dict(lower_bounds=v_lower_bounds, pre_norm_g=v_pre_norm_g, hgrn_norm_g=v_hgrn_norm_g, fox_f_bias=v_fox_f_bias,
                 pool_w=v_pool_w, pool_scale=v_pool_scale, post_norm_g=v_post_norm_g)
    depth = w_in.shape[0]
    nb, t, d = x.shape
    n = nb * t
    core = lax.axis_index("c").astype(jnp.int32).reshape(1, 1)
    shards = [(w_in[l].astype(BF16), w_out[l].astype(BF16)) for l in range(depth)]
    lbs = _lower_bound_table(lower_bounds, "lower_bound_table")

    lw = [_layer_weights(*_gather_weights(*shards[0]))]
    xl, saved = x.reshape(n, d), []
    for l in range(depth):
        rider = _gather_rider(shards[l + 1]) if l + 1 < depth else None
        xl, sv, rode = _layer_fwd(l, xl, lbs, weights, lw[l], nb, rider)
        saved.append(sv)
        if rode is not None:
            lw.append(_layer_weights(*rode))
    dx, sq = _loss_fwd_bwd(xl, loss_target.reshape(n, d), "loss")
    loss = lax.psum(0.5 * jnp.sum(sq) / d, AXES)

    grads, recv, pending = [None] * depth, [None] * depth, None
    for l in reversed(range(depth)):
        rider = None if pending is None else _chip_exchange_rider(pending)
        g, pieces, rode = _layer_bwd(l, dx, saved[l], lbs, weights, lw[l], nb, rider)
        if rode is not None:
            recv[l + 1] = rode
        halves = (g["w_in"], g["w_out"])
        other = _run_rider(_swap_rider(halves), f"grad_swap_{l}")
        pending = [_pair_add(hv, ot, core, f"grad_pair_add_{l}_{i}") for i, (hv, ot) in enumerate(zip(halves, other))]
        rider = _chip_exchange_rider(pending) if l == 0 else None
        dx, g["pre_norm_g"], rode = _layer_bwd_input(l, dx, pieces, saved[l], weights, lw[l], rider)
        if rode is not None:
            recv[l] = rode
        grads[l] = g
    small = {k: jnp.stack([grads[l][k] for l in range(depth)]) for k in SMALL if k != "lower_bounds"}
    small["lower_bounds"] = _lower_bound_bwd(lower_bounds, jnp.stack([grads[l]["lbs"] for l in range(depth)]),
                                             "lower_bound_bwd")
    (r_small,) = _run_rider(_gather_rider([_pack_small(small)]), "small_grads_gather")

    res_in = _sum_adamw([recv[l][0] for l in range(depth)], w_in, m_w_in, v_w_in, "adamw_w_in")
    res_out = _sum_adamw([recv[l][1] for l in range(depth)], w_out, m_w_out, v_w_out, "adamw_w_out")
    res_small = _sum_adamw([r_small], _pack_small(weights)[None], _pack_small(mom_m)[None], _pack_small(mom_v)[None],
                           "adamw_small")

    names = ("lower_bounds", "pre_norm_g", "w_in", "hgrn_norm_g", "fox_f_bias", "pool_w", "pool_scale", "w_out", "post_norm_g")
    outs = [loss, dx.reshape(nb, t, d)]
    for i in range(4):
        full = dict(_unpack_small(res_small[i][0], weights), w_in=res_in[i], w_out=res_out[i])
        outs += [full[k] for k in names]
    return tuple(outs)
```

```python
import functools

import jax
import jax.numpy as jnp
from jax import lax
from jax.experimental import pallas as pl
from jax.experimental.pallas import tpu as pltpu

F32, BF16 = jnp.float32, jnp.bfloat16
HI = lax.Precision.HIGHEST
MESH = pl.DeviceIdType.MESH
AXES = ("x", "y", "c")
N_DEV = 8

NORM_EPS = 1e-6
MASK_VALUE = -1e30
TINY = 1e-30
CHUNK = 64
SUB = 16
HGRN_W, POOL_W, FOX_W = 256, 256, 512
HEAD = 64
FOX_HEADS = 8
POOL_WINDOWS = (2, 4, 8, 16)
POOL_HALO = 16
MAIN_W = 3584
FC_PAD = 128
C_QA, C_FA, C_IA, C_GA, C_UB, C_GB, C_QC, C_KC, C_VC, C_GC = 0, 256, 512, 768, 1024, 1280, 1536, 2048, 2560, 3072
FOX_SCALE = HEAD ** -0.5

ADAM_LR, ADAM_B1, ADAM_B2, ADAM_EPS, ADAM_WD, ADAM_STEP = 0.001, 0.9, 0.999, 1e-08, 0.01, 10

VMEM_LIMIT = 56 * 1024 * 1024


def _pc(fn, name, **kw):
    return pl.pallas_call(fn, name=name, **kw)


def _params(**kw):
    return pltpu.CompilerParams(vmem_limit_bytes=VMEM_LIMIT, **kw)


class _Rider:
    def __init__(self, inputs, out_shapes, n_sems, n_local, plan):
        self.inputs, self.out_shapes, self.n_sems, self.n_local, self.plan = list(inputs), list(out_shapes), n_sems, n_local, plan

    def start(self, ins, outs, *sems):
        sends, _, locs = self.plan(ins, outs, *sems)
        for cp in locs + sends:
            cp.start()

    def wait(self, ins, outs, *sems):
        sends, recvs, locs = self.plan(ins, outs, *sems)
        for cp in recvs:
            cp.wait_recv()
        for cp in sends:
            cp.wait_send()
        for cp in locs:
            cp.wait()

    def sem_shapes(self):
        return [pltpu.SemaphoreType.DMA((self.n_sems,)), pltpu.SemaphoreType.DMA((self.n_sems,)),
                pltpu.SemaphoreType.DMA((self.n_local,))]


def _call(body, name, args, rider=None, *, grid, in_specs, out_specs, out_shape, scratch_shapes=(), **kw):
    if rider is None:
        res = _pc(body, name, grid=grid, in_specs=in_specs, out_specs=out_specs, out_shape=out_shape,
                  scratch_shapes=list(scratch_shapes), **kw)(*args)
        return res, None
    n_in, n_out, n_scr = len(in_specs), len(out_specs), len(scratch_shapes)
    n_rin, n_rout = len(rider.inputs), len(rider.out_shapes)

    def ridden(*refs):
        ins, refs = refs[:n_in], refs[n_in:]
        rins, refs = refs[:n_rin], refs[n_rin:]
        outs, refs = refs[:n_out], refs[n_out:]
        routs, refs = refs[:n_rout], refs[n_rout:]
        scr, sems = refs[:n_scr], refs[n_scr:]
        first = functools.reduce(jnp.logical_and, [pl.program_id(a) == 0 for a in range(len(grid))])
        last = functools.reduce(jnp.logical_and, [pl.program_id(a) == g - 1 for a, g in enumerate(grid)])

        @pl.when(first)
        def _():
            rider.start(rins, routs, *sems)

        body(*ins, *outs, *scr)

        @pl.when(last)
        def _():
            rider.wait(rins, routs, *sems)

    any_spec = pl.BlockSpec(memory_space=pl.ANY)
    res = _pc(ridden, name, grid=grid, in_specs=list(in_specs) + [any_spec] * n_rin,
              out_specs=list(out_specs) + [any_spec] * n_rout, out_shape=list(out_shape) + rider.out_shapes,
              scratch_shapes=list(scratch_shapes) + rider.sem_shapes(), **kw)(*args, *rider.inputs)
    return res[:n_out], res[n_out:]


def _run_rider(rider, name):
    n_rin = len(rider.inputs)

    def body(*refs):
        ins, outs, sems = refs[:n_rin], refs[n_rin:n_rin + len(rider.out_shapes)], refs[n_rin + len(rider.out_shapes):]
        rider.start(ins, outs, *sems)
        rider.wait(ins, outs, *sems)

    any_spec = pl.BlockSpec(memory_space=pl.ANY)
    return _pc(body, name, in_specs=[any_spec] * n_rin, out_specs=[any_spec] * len(rider.out_shapes),
               out_shape=rider.out_shapes, scratch_shapes=rider.sem_shapes())(*rider.inputs)


def _dot(a, b):
    return jnp.dot(a, b, preferred_element_type=F32)


def _dot_nt(a, b):
    return lax.dot_general(a, b, (((1,), (1,)), ((), ())), preferred_element_type=F32)


def _dot_tn(a, b):
    return lax.dot_general(a, b, (((0,), (0,)), ((), ())), preferred_element_type=F32)


def _dot_hi(a, b):
    return jnp.dot(a, b, precision=HI, preferred_element_type=F32)


def _sigmoid(x):
    return 1.0 / (1.0 + jnp.exp(-x))


def _block_ones(n, dtype):
    r = lax.broadcasted_iota(jnp.int32, (n, n), 0) // HEAD
    c = lax.broadcasted_iota(jnp.int32, (n, n), 1) // HEAD
    return (r == c).astype(dtype)


def _sds(shape, dtype):
    return jax.ShapeDtypeStruct(shape, dtype)


def _in_proj_fwd(x, g_pre, w_main, w_fc, name):
    n, d = x.shape
    tm = min(256, n)

    def body(x_ref, g_ref, w_ref, wf_ref, proj_ref, fc_ref, ht_ref):
        xv = x_ref[...]
        r = lax.rsqrt(jnp.mean(xv * xv, axis=-1, keepdims=True) + NORM_EPS)
        hf = xv * r * g_ref[...]
        hb = hf.astype(BF16)
        ht_ref[...] = hf.T.astype(BF16)
        for j in range(0, MAIN_W, 512):
            proj_ref[:, j:j + 512] = _dot(hb, w_ref[:, j:j + 512])
        fc_ref[...] = _dot(hb, wf_ref[...])

    return _pc(
        body, name, grid=(n // tm,),
        in_specs=[pl.BlockSpec((tm, d), lambda i: (i, 0)), pl.BlockSpec((1, d), lambda i: (0, 0)),
                  pl.BlockSpec((d, MAIN_W), lambda i: (0, 0)), pl.BlockSpec((d, FC_PAD), lambda i: (0, 0))],
        out_specs=[pl.BlockSpec((tm, MAIN_W), lambda i: (i, 0)), pl.BlockSpec((tm, FC_PAD), lambda i: (i, 0)),
                   pl.BlockSpec((d, tm), lambda i: (0, i))],
        out_shape=[_sds((n, MAIN_W), F32), _sds((n, FC_PAD), F32), _sds((d, n), BF16)],
        compiler_params=_params(),
    )(x, g_pre, w_main, w_fc)


def _fox_decay_fwd(fc, bias, nb, name):
    n = fc.shape[0]
    t = n // nb
    tt = min(256, t)
    nt = t // tt

    def body(fc_ref, b_ref, c_ref, carry):
        i = pl.program_id(1)

        @pl.when(i == 0)
        def _():
            carry[...] = jnp.zeros_like(carry)

        xv = fc_ref[...] + b_ref[...]
        lf = jnp.minimum(xv, 0.0) - jnp.log(1.0 + jnp.exp(-jnp.abs(xv)))
        r = lax.broadcasted_iota(jnp.int32, (tt, tt), 0)
        cc = lax.broadcasted_iota(jnp.int32, (tt, tt), 1)
        cs = _dot_hi((r >= cc).astype(F32), lf) + carry[...]
        c_ref[...] = cs
        carry[...] = cs[tt - 1:tt, :]

    return _pc(
        body, name, grid=(nb, nt),
        in_specs=[pl.BlockSpec((tt, FC_PAD), lambda b, i: (b * nt + i, 0)), pl.BlockSpec((1, FC_PAD), lambda b, i: (0, 0))],
        out_specs=pl.BlockSpec((tt, FC_PAD), lambda b, i: (b * nt + i, 0)),
        out_shape=_sds((n, FC_PAD), F32),
        scratch_shapes=[pltpu.VMEM((1, FC_PAD), F32)],
        compiler_params=_params(),
    )(fc, bias)


def _hgrn_gates(q, z, lb):
    sig = _sigmoid(z)
    sn = _sigmoid(-z)
    f = lb + (1.0 - lb) * sig
    g = jnp.log(jnp.maximum(f, TINY))
    k = (1.0 - lb) * sn
    sq = _sigmoid(q)
    return sig, sn, f, g, k, sq


def _sub_tri(n, lower):
    r = lax.broadcasted_iota(jnp.int32, (n, n), 0)
    c = lax.broadcasted_iota(jnp.int32, (n, n), 1)
    tri = (r >= c) if lower else (r <= c)
    return jnp.logical_and(r // SUB == c // SUB, tri).astype(F32)


def _hgrn_decays(qs, k, b):
    srow = lax.broadcasted_iota(jnp.int32, (SUB, HGRN_W), 0)
    es, ws = [], []
    for t in range(SUB):
        e = jnp.where(srow <= t, jnp.exp(b[t:t + 1, :] - b), 0.0)
        es.append(e)
        ws.append(e * (qs[t:t + 1, :] * k))
    return srow, es, ws


def _hgrn_state_step(st, k, v, b, bmask):
    bl = b[SUB - 1:SUB, :]
    ktil = k * jnp.exp(bl - b)
    return st * jnp.exp(bl) + _dot_tn(v.astype(BF16), ktil.astype(BF16)) * bmask


def _hgrn_sub_fwd(qs, k, v, b, st, ones_b, bmask):
    srow, _, ws = _hgrn_decays(qs, k, b)
    aexp = _dot(jnp.concatenate(ws, axis=0).astype(BF16), ones_b)
    o = _dot_nt((qs * jnp.exp(b)).astype(BF16), st.astype(BF16))
    for t in range(SUB):
        row = jnp.sum(aexp[t * SUB:(t + 1) * SUB, :] * v, axis=0, keepdims=True)
        o = o + jnp.where(srow == t, row, 0.0)
    return o, _hgrn_state_step(st, k, v, b, bmask)


def _hgrn_tile(t):
    return min(256, t)


def _hgrn_fwd(proj, lb, ones_b, nb, name):
    n = proj.shape[0]
    t = n // nb
    tt = _hgrn_tile(t)
    nt = t // tt
    ncs = tt // CHUNK
    w = HGRN_W

    def body(q_ref, z_ref, v_ref, lb_ref, ones_ref, o_ref, s0_ref, st_s, b_s, qs_s, k_s):
        @pl.when(pl.program_id(1) == 0)
        def _():
            st_s[...] = jnp.zeros_like(st_s)

        q = q_ref[...]
        _, _, _, g, k, sq = _hgrn_gates(q, z_ref[...], lb_ref[...])
        b_s[...] = _dot_hi(_sub_tri(tt, True), g)
        qs_s[...] = q * sq
        k_s[...] = k
        bmask = _block_ones(w, F32)
        ones_b = ones_ref[...]

        def chunk(c, carry):
            st = st_s[...]
            s0_ref[c] = st
            base = pl.multiple_of(c * CHUNK, CHUNK)
            for u in range(CHUNK // SUB):
                rows = pl.ds(base + u * SUB, SUB)
                o, st = _hgrn_sub_fwd(qs_s[rows, :], k_s[rows, :], v_ref[rows, :], b_s[rows, :], st, ones_b, bmask)
                o_ref[rows, :] = o
            st_s[...] = st
            return carry

        lax.fori_loop(0, ncs, chunk, 0)

    def col(j):
        return pl.BlockSpec((tt, w), lambda b, i: (b * nt + i, j))

    return _pc(
        body, name, grid=(nb, nt),
        in_specs=[col(C_QA // w), col(C_FA // w), col(C_IA // w), pl.BlockSpec((1, w), lambda b, i: (0, 0)),
                  pl.BlockSpec((w, w), lambda b, i: (0, 0))],
        out_specs=[pl.BlockSpec((tt, w), lambda b, i: (b * nt + i, 0)),
                   pl.BlockSpec((ncs, w, w), lambda b, i: (b * nt + i, 0, 0))],
        out_shape=[_sds((n, w), F32), _sds((n // CHUNK, w, w), F32)],
        scratch_shapes=[pltpu.VMEM((w, w), F32)] + [pltpu.VMEM((tt, w), F32)] * 3,
        compiler_params=_params(),
    )(proj, proj, proj, lb, ones_b)


def _pool_lane_windows():
    lane = lax.broadcasted_iota(jnp.int32, (1, POOL_W), 1) // HEAD
    wl = jnp.zeros((1, POOL_W), F32)
    for gi, win in enumerate(POOL_WINDOWS):
        wl = jnp.where(lane == gi, float(win), wl)
    return lane, wl


def _pool_select(lane, parts):
    out = parts[-1]
    for gi in range(len(parts) - 2, -1, -1):
        out = jnp.where(lane == gi, parts[gi], out)
    return out


def _pool_mix(u, halo, t0, tt):
    lane, wl = _pool_lane_windows()
    ext = jnp.concatenate([halo, u], axis=0)
    sums, cur, shift = [], ext, 1
    for _ in POOL_WINDOWS:
        cur = cur + pltpu.roll(cur, shift, axis=0)
        sums.append(cur[POOL_HALO:, :])
        shift *= 2
    tpos = (t0 + lax.broadcasted_iota(jnp.int32, (tt, POOL_W), 0)).astype(F32)
    cnt = jnp.minimum(tpos + 1.0, wl)
    return _pool_select(lane, sums) / cnt - u, cnt


def _pool_specs(tt, nt, nhb):
    cu, cg = C_UB // POOL_W, C_GB // POOL_W
    return [pl.BlockSpec((tt, POOL_W), lambda b, i: (b * nt + i, cu)),
            pl.BlockSpec((tt, POOL_W), lambda b, i: (b * nt + i, cg)),
            pl.BlockSpec((POOL_HALO, POOL_W), lambda b, i: (jnp.maximum((b * nt + i) * nhb - 1, 0), cu))]


def _pool_fwd(proj, wbd, scale, nb, name):
    n = proj.shape[0]
    t = n // nb
    tt = min(256, t)
    nt = t // tt
    nhb = tt // POOL_HALO

    def body(u_ref, g_ref, h_ref, w_ref, s_ref, o_ref):
        i = pl.program_id(1)
        halo = jnp.where(i == 0, 0.0, h_ref[...])
        pooled, _ = _pool_mix(u_ref[...], halo, i * tt, tt)
        gv = g_ref[...]
        o_ref[...] = _dot(pooled.astype(BF16), w_ref[...]) * s_ref[...] * (gv * _sigmoid(gv))

    return _pc(
        body, name, grid=(nb, nt),
        in_specs=_pool_specs(tt, nt, nhb) + [pl.BlockSpec((POOL_W, POOL_W), lambda b, i: (0, 0)),
                                             pl.BlockSpec((1, POOL_W), lambda b, i: (0, 0))],
        out_specs=pl.BlockSpec((tt, POOL_W), lambda b, i: (b * nt + i, 0)),
        out_shape=_sds((n, POOL_W), F32),
        compiler_params=_params(),
    )(proj, proj, proj, wbd, scale)


def _rows_reduce(x, op, final):
    while x.shape[0] > 8 and x.shape[0] % 16 == 0:
        half = x.shape[0] // 2
        x = op(x[:half], x[half:])
    return final(x, axis=0, keepdims=True)


def _lane_lo():
    return lax.broadcasted_iota(jnp.int32, (1, 2 * HEAD), 1) < HEAD


def _put_col(tile, hh, colv):
    lane = lax.broadcasted_iota(jnp.int32, tile.shape, 1)
    return jnp.where(lane == hh, colv, tile)


def _fox_fwd(proj, c_col, c_row, nb, name, rider=None):
    n = proj.shape[0]
    t = n // nb
    tb = min(256, t)
    nq = t // tb
    pw = 2 * HEAD

    def body(q_ref, k_ref, v_ref, cc_ref, cr_ref, o_ref, lse_ref, m_s, acc_s, cq_s):
        qi, kj = pl.program_id(1), pl.program_id(2)

        @pl.when(kj == 0)
        def _():
            m_s[...] = jnp.full_like(m_s, -jnp.inf)
            acc_s[...] = jnp.zeros_like(acc_s)
            for hh in range(FOX_HEADS):
                cq_s[hh] = jnp.broadcast_to(cc_ref[:, hh:hh + 1], (tb, pw))

        def block(masked):
            lo = _lane_lo()
            if masked:
                causal = lax.broadcasted_iota(jnp.int32, (tb, tb), 0) >= lax.broadcasted_iota(jnp.int32, (tb, tb), 1)
            for p in range(FOX_HEADS // 2):
                sl = slice(p * pw, (p + 1) * pw)
                qp = q_ref[:, sl] * FOX_SCALE
                kp = k_ref[:, sl].astype(BF16)
                vf = v_ref[:, sl]
                for h in range(2):
                    hh = 2 * p + h
                    lm = lo if h == 0 else jnp.logical_not(lo)
                    s = _dot_nt(jnp.where(lm, qp, 0.0).astype(BF16), kp)
                    s = s + (jnp.tile(cq_s[hh], (1, tb // pw)) - cr_ref[hh:hh + 1, :])
                    if masked:
                        s = jnp.where(causal, s, MASK_VALUE)
                    m_prev = m_s[hh]
                    m_new = jnp.maximum(m_prev, jnp.max(s, axis=1, keepdims=True))
                    alpha = jnp.exp(m_prev - m_new)
                    pe = jnp.exp(s - jnp.tile(m_new, (1, tb // pw)))
                    m_s[hh] = m_new
                    acc_s[hh] = alpha * acc_s[hh] + _dot(pe.astype(BF16), jnp.where(lm, vf, 1.0).astype(BF16))

        @pl.when(kj < qi)
        def _():
            block(False)

        @pl.when(kj == qi)
        def _():
            block(True)
            lo = _lane_lo()
            lse = jnp.zeros((tb, FC_PAD), F32)
            for p in range(FOX_HEADS // 2):
                halves = []
                for h in range(2):
                    hh = 2 * p + h
                    lm = lo if h == 0 else jnp.logical_not(lo)
                    acc = acc_s[hh]
                    swapped = pltpu.roll(acc, HEAD, axis=1)
                    halves.append(acc / swapped)
                    lse = _put_col(lse, hh, m_s[hh] + jnp.log(jnp.where(lm, swapped, acc)))
                o_ref[:, p * pw:(p + 1) * pw] = jnp.where(lo, halves[0], halves[1])
            lse_ref[...] = lse

    def qspec(wd, j):
        return pl.BlockSpec((tb, wd), lambda b, qi, kj: (b * nq + qi, j))

    def kspec(j):
        return pl.BlockSpec((tb, FOX_W), lambda b, qi, kj: (b * nq + jnp.minimum(kj, qi), j))

    return _call(
        body, name, (proj, proj, proj, c_col, c_row), rider, grid=(nb, nq, nq),
        in_specs=[qspec(FOX_W, C_QC // FOX_W), kspec(C_KC // FOX_W), kspec(C_VC // FOX_W), qspec(FC_PAD, 0),
                  pl.BlockSpec((None, FOX_HEADS, tb), lambda b, qi, kj: (b, 0, jnp.minimum(kj, qi)))],
        out_specs=[qspec(FOX_W, 0), qspec(FC_PAD, 0)],
        out_shape=[_sds((n, FOX_W), F32), _sds((n, FC_PAD), F32)],
        scratch_shapes=[pltpu.VMEM((FOX_HEADS, tb, pw), F32), pltpu.VMEM((FOX_HEADS, tb, pw), F32),
                        pltpu.VMEM((FOX_HEADS, tb, pw), F32)],
        compiler_params=_params(),
    )


def _head_mean(x, ones_f):
    return _dot_hi(x, ones_f) * (1.0 / HEAD)


def _merge_fwd(x, proj, o_h, o_b, o_c, gh, w_out, g_post, name):
    n, d = x.shape
    tm = min(256, n)

    def body(x_ref, ga_ref, gc_ref, oh_ref, ob_ref, oc_ref, gh_ref, w_ref, gp_ref, xo_ref, mixt_ref, y_ref):
        oh = oh_ref[...]
        ones_f = _block_ones(HGRN_W, F32)
        na = oh * lax.rsqrt(_head_mean(oh * oh, ones_f) + NORM_EPS) * gh_ref[...]
        ga, gc = ga_ref[...], gc_ref[...]
        mixed = jnp.concatenate([na * (ga * _sigmoid(ga)), ob_ref[...], oc_ref[...] * (gc * _sigmoid(gc))], axis=1)
        mixt_ref[...] = mixed.T.astype(BF16)
        y = _dot(mixed.astype(BF16), w_ref[...])
        y_ref[...] = y
        xo_ref[...] = x_ref[...] + y * lax.rsqrt(jnp.mean(y * y, axis=-1, keepdims=True) + NORM_EPS) * gp_ref[...]

    def row(wd, j=0):
        return pl.BlockSpec((tm, wd), lambda i: (i, j))

    def full(a, b):
        return pl.BlockSpec((a, b), lambda i: (0, 0))

    return _pc(
        body, name, grid=(n // tm,),
        in_specs=[row(d), row(HGRN_W, C_GA // HGRN_W), row(FOX_W, C_GC // FOX_W), row(HGRN_W), row(POOL_W), row(FOX_W),
                  full(1, HGRN_W), full(d, d), full(1, d)],
        out_specs=[row(d), pl.BlockSpec((d, tm), lambda i: (0, i)), row(d)],
        out_shape=[_sds((n, d), F32), _sds((d, n), BF16), _sds((n, d), F32)],
        compiler_params=_params(),
    )(x, proj, proj, o_h, o_b, o_c, gh, w_out, g_post)


def _loss_fwd_bwd(x, target, name):
    n, d = x.shape
    tm = min(512, n)

    def body(x_ref, t_ref, dx_ref, sq_ref):
        @pl.when(pl.program_id(0) == 0)
        def _():
            sq_ref[...] = jnp.zeros_like(sq_ref)

        e = x_ref[...] - t_ref[...]
        dx_ref[...] = e * (1.0 / d)
        sq_ref[...] += jnp.sum(e * e, axis=0, keepdims=True)

    return _pc(
        body, name, grid=(n // tm,),
        in_specs=[pl.BlockSpec((tm, d), lambda i: (i, 0))] * 2,
        out_specs=[pl.BlockSpec((tm, d), lambda i: (i, 0)), pl.BlockSpec((1, d), lambda i: (0, 0))],
        out_shape=[_sds((n, d), F32), _sds((1, d), F32)],
        compiler_params=_params(),
    )(x, target)


def _rms_bwd(dy_scaled, xhat, r):
    return r * (dy_scaled - xhat * jnp.mean(dy_scaled * xhat, axis=-1, keepdims=True))


def _merge_bwd(dxo, y, g_post, w_out_t, name):
    n, d = y.shape
    tm = min(256, n)

    def body(dx_ref, y_ref, gp_ref, wt_ref, dy_ref, dm_ref, dgp_ref):
        @pl.when(pl.program_id(0) == 0)
        def _():
            dgp_ref[...] = jnp.zeros_like(dgp_ref)

        yv, dxv = y_ref[...], dx_ref[...]
        r = lax.rsqrt(jnp.mean(yv * yv, axis=-1, keepdims=True) + NORM_EPS)
        yh = yv * r
        dgp_ref[...] += jnp.sum(dxv * yh, axis=0, keepdims=True)
        dyb = _rms_bwd(dxv * gp_ref[...], yh, r).astype(BF16)
        dy_ref[...] = dyb
        dm_ref[...] = _dot(dyb, wt_ref[...])

    row = pl.BlockSpec((tm, d), lambda i: (i, 0))
    return _pc(
        body, name, grid=(n // tm,),
        in_specs=[row, row, pl.BlockSpec((1, d), lambda i: (0, 0)), pl.BlockSpec((d, d), lambda i: (0, 0))],
        out_specs=[row, row, pl.BlockSpec((1, d), lambda i: (0, 0))],
        out_shape=[_sds((n, d), BF16), _sds((n, d), F32), _sds((1, d), F32)],
        compiler_params=_params(),
    )(dxo, y, g_post, w_out_t)


def _w_out_grad(mixt, dy, name):
    d, n = mixt.shape
    rows = d // N_DEV

    def body(a_ref, b_ref, o_ref):
        o_ref[...] = _dot(a_ref[...], b_ref[...]).astype(BF16)

    return _pc(
        body, name, grid=(N_DEV,),
        in_specs=[pl.BlockSpec((rows, n), lambda j: (j, 0)), pl.BlockSpec((n, d), lambda j: (0, 0))],
        out_specs=pl.BlockSpec((None, None, rows, d), lambda j: (j % 2, j // 2, 0, 0)),
        out_shape=_sds((2, N_DEV // 2, rows, d), BF16),
        compiler_params=_params(),
    )(mixt, dy)


def _w_in_grad(ht, pieces, name):
    d, n = ht.shape
    ta, tk = min(512, d), min(512, n)
    nk = n // tk
    arrays = [p for p, _ in pieces]
    widths = [p.shape[1] for p in arrays]
    offs = [sum(widths[:i]) for i in range(len(widths))]
    in_w = MAIN_W + FOX_HEADS
    shard = in_w // N_DEV

    def body(*refs):
        a_ref, p_refs = refs[0], refs[1:1 + len(arrays)]
        o_ref, acc = refs[1 + len(arrays):]
        k = pl.program_id(1)

        @pl.when(k == 0)
        def _():
            acc[...] = jnp.zeros_like(acc)

        a = a_ref[...]
        for pr, off, wd in zip(p_refs, offs, widths):
            for j in range(0, wd, 512):
                jw = min(512, wd - j)
                acc[:, off + j:off + j + jw] += _dot(a, pr[:, j:j + jw])

        @pl.when(k == nk - 1)
        def _():
            for j in range(N_DEV):
                o_ref[j % 2, j // 2] = acc[:, j * shard:(j + 1) * shard].astype(BF16)

    return _pc(
        body, name, grid=(d // ta, nk),
        in_specs=[pl.BlockSpec((ta, tk), lambda i, k: (i, k))] + [pl.BlockSpec((tk, wd), lambda i, k: (k, 0)) for wd in widths],
        out_specs=pl.BlockSpec((2, N_DEV // 2, ta, shard), lambda i, k: (0, 0, i, 0)),
        out_shape=_sds((2, N_DEV // 2, d, shard), BF16),
        scratch_shapes=[pltpu.VMEM((ta, sum(widths)), F32)],
        compiler_params=_params(),
    )(ht, *arrays)


def _hgrn_sub_bwd(qs, k, v, b, do, s0, ds1, ones_b, bmask):
    bl = b[SUB - 1:SUB, :]
    eb, ebl, ekt = jnp.exp(b), jnp.exp(bl), jnp.exp(bl - b)
    qe, ktil = qs * eb, k * ekt
    ds1b, dob = ds1.astype(BF16), do.astype(BF16)
    dv = _dot_nt(ktil.astype(BF16), ds1b)
    dqe = _dot(dob, s0.astype(BF16))
    dktil = _dot(v.astype(BF16), ds1b)
    dbl = jnp.sum(dktil * ktil, axis=0, keepdims=True) + ebl * jnp.sum(s0 * ds1, axis=0, keepdims=True)
    ds0 = ds1 * ebl + _dot_tn(dob, qe.astype(BF16)) * bmask
    srow, es, ws = _hgrn_decays(qs, k, b)
    aexp = _dot(jnp.concatenate(ws, axis=0).astype(BF16), ones_b)
    gexp = _dot(jnp.concatenate([do[t:t + 1, :] * v for t in range(SUB)], axis=0).astype(BF16), ones_b)
    dq = dqe * eb
    dk = dktil * ekt
    for t in range(SUB):
        sl = slice(t * SUB, (t + 1) * SUB)
        ge = gexp[sl, :] * es[t]
        dq = dq + jnp.where(srow == t, jnp.sum(ge * k, axis=0, keepdims=True), 0.0)
        dk = dk + ge * qs[t:t + 1, :]
        dv = dv + aexp[sl, :] * do[t:t + 1, :]
    return dq, dk, dv, dbl, ds0


def _hgrn_bwd(dmix, proj, o_h, s0, gh, lb, ones_b, nb, name):
    n = proj.shape[0]
    t = n // nb
    tt = _hgrn_tile(t)
    nt = t // tt
    ncs = tt // CHUNK
    nsub = CHUNK // SUB
    w = HGRN_W

    def body(dm_ref, q_ref, z_ref, v_ref, ga_ref, oh_ref, s0_ref, gh_ref, lb_ref, ones_ref,
             dp_ref, dgh_ref, dlb_ref, ds_s, ss_s, b_s, qs_s, k_s, do_s, dq_s, dk_s, dv_s, dbl_s):
        first = jnp.logical_and(pl.program_id(0) == 0, pl.program_id(1) == 0)

        @pl.when(first)
        def _():
            dgh_ref[...] = jnp.zeros_like(dgh_ref)
            dlb_ref[...] = jnp.zeros_like(dlb_ref)

        @pl.when(pl.program_id(1) == 0)
        def _():
            ds_s[...] = jnp.zeros_like(ds_s)

        ones_b = ones_ref[...]
        ones_f = ones_b.astype(F32)
        bmask = _block_ones(w, F32)
        lbv, ghv = lb_ref[...], gh_ref[...]
        oh, ga, dm = oh_ref[...], ga_ref[...], dm_ref[...]
        rn = lax.rsqrt(_head_mean(oh * oh, ones_f) + NORM_EPS)
        nh = oh * rn
        sga = _sigmoid(ga)
        dp_ref[:, 3 * w:4 * w] = (dm * nh * ghv * (sga * (1.0 + ga * (1.0 - sga)))).astype(BF16)
        dn = dm * (ga * sga)
        dgh_ref[...] += jnp.sum(dn * nh, axis=0, keepdims=True)
        dn = dn * ghv
        do_s[...] = rn * (dn - nh * _head_mean(dn * nh, ones_f))
        q = q_ref[...]
        sig, sn, f, g, k, sq = _hgrn_gates(q, z_ref[...], lbv)
        qs = q * sq
        b_s[...] = _dot_hi(_sub_tri(tt, True), g)
        qs_s[...] = qs
        k_s[...] = k

        def chunk(cc, carry):
            c = ncs - 1 - cc
            base = pl.multiple_of(c * CHUNK, CHUNK)
            st = s0_ref[c]
            for u in range(nsub):
                ss_s[u] = st
                if u < nsub - 1:
                    rows = pl.ds(base + u * SUB, SUB)
                    st = _hgrn_state_step(st, k_s[rows, :], v_ref[rows, :], b_s[rows, :], bmask)
            ds = ds_s[...]
            for u in reversed(range(nsub)):
                rows = pl.ds(base + u * SUB, SUB)
                dq, dk, dv, dbl, ds = _hgrn_sub_bwd(qs_s[rows, :], k_s[rows, :], v_ref[rows, :], b_s[rows, :],
                                                    do_s[rows, :], ss_s[u], ds, ones_b, bmask)
                dq_s[rows, :] = dq
                dk_s[rows, :] = dk
                dv_s[rows, :] = dv
                dbl_s[rows, :] = jnp.broadcast_to(dbl, (SUB, w))
            ds_s[...] = ds
            return carry

        lax.fori_loop(0, ncs, chunk, 0)
        dqs, dk = dq_s[...], dk_s[...]
        dg = _dot_hi(_sub_tri(tt, False), qs * dqs - k * dk) + dbl_s[...]
        dfz = jnp.where(f > TINY, dg / jnp.maximum(f, TINY), 0.0)
        dlb_ref[...] += jnp.sum(dfz * (1.0 - sig) - dk * sn, axis=0, keepdims=True)
        dp_ref[:, 0:w] = (dqs * (sq * (1.0 + q * (1.0 - sq)))).astype(BF16)
        dp_ref[:, w:2 * w] = ((dfz - dk) * (1.0 - lbv) * sig * sn).astype(BF16)
        dp_ref[:, 2 * w:3 * w] = dv_s[...].astype(BF16)

    def rv(b, i):
        return b * nt + (nt - 1 - i)

    def col(j):
        return pl.BlockSpec((tt, w), lambda b, i: (rv(b, i), j))

    def full(a, bb):
        return pl.BlockSpec((a, bb), lambda b, i: (0, 0))

    return _pc(
        body, name, grid=(nb, nt),
        in_specs=[col(0), col(C_QA // w), col(C_FA // w), col(C_IA // w), col(C_GA // w), col(0),
                  pl.BlockSpec((ncs, w, w), lambda b, i: (rv(b, i), 0, 0)), full(1, w), full(1, w), full(w, w)],
        out_specs=[pl.BlockSpec((tt, 4 * w), lambda b, i: (rv(b, i), 0)), full(1, w), full(1, w)],
        out_shape=[_sds((n, 4 * w), BF16), _sds((1, w), F32), _sds((1, w), F32)],
        scratch_shapes=[pltpu.VMEM((w, w), F32), pltpu.VMEM((nsub, w, w), F32)] + [pltpu.VMEM((tt, w), F32)] * 8,
        compiler_params=_params(),
    )(dmix, proj, proj, proj, proj, o_h, s0, gh, lb, ones_b)


def _pool_bwd(dmix, proj, wbd, wbd_t, scale, nb, name):
    n = proj.shape[0]
    t = n // nb
    tt = min(256, t)
    nt = t // tt
    nhb = tt // POOL_HALO
    cu, cg, cm = C_UB // POOL_W, C_GB // POOL_W, HGRN_W // POOL_W

    def body(u_ref, g_ref, h_ref, dm_ref, gn_ref, dmn_ref, w_ref, wt_ref, s_ref, dp_ref, dw_ref, ds_ref):
        i = pl.program_id(1)
        first = jnp.logical_and(pl.program_id(0) == 0, i == 0)

        @pl.when(first)
        def _():
            dw_ref[...] = jnp.zeros_like(dw_ref)
            ds_ref[...] = jnp.zeros_like(ds_ref)

        sc = s_ref[...]
        halo = jnp.where(i == 0, 0.0, h_ref[...])
        pooled, cnt = _pool_mix(u_ref[...], halo, i * tt, tt)
        pb = pooled.astype(BF16)
        pre = _dot(pb, w_ref[...])
        gv, dm = g_ref[...], dm_ref[...]
        sg = _sigmoid(gv)
        silu = gv * sg
        dgb = dm * pre * sc * (sg * (1.0 + gv * (1.0 - sg)))
        ds_ref[...] += jnp.sum(dm * pre * silu, axis=0, keepdims=True)
        dpre = (dm * sc * silu).astype(BF16)
        dw_ref[...] += _dot_tn(pb, dpre)
        dpool = _dot(dpre, wt_ref[...])
        gn = gn_ref[...]
        dpre_n = (dmn_ref[...] * sc * (gn * _sigmoid(gn))).astype(BF16)
        dpool_n = jnp.where(i == nt - 1, 0.0, _dot(dpre_n, wt_ref[...]))
        lane, wl = _pool_lane_windows()
        tpos_n = ((i + 1) * tt + lax.broadcasted_iota(jnp.int32, (POOL_HALO, POOL_W), 0)).astype(F32)
        ext = jnp.concatenate([dpool / cnt, dpool_n / jnp.minimum(tpos_n + 1.0, wl)], axis=0)
        rows = tt + POOL_HALO
        sums, cur, shift = [], ext, 1
        for _ in POOL_WINDOWS:
            cur = cur + pltpu.roll(cur, rows - shift, axis=0)
            sums.append(cur[:tt, :])
            shift *= 2
        du = _pool_select(lane, sums) - dpool
        dp_ref[...] = jnp.concatenate([du, dgb], axis=1).astype(BF16)

    def nxt(b, i):
        return jnp.minimum((b * nt + i + 1) * nhb, n // POOL_HALO - 1)

    return _pc(
        body, name, grid=(nb, nt),
        in_specs=_pool_specs(tt, nt, nhb) + [
            pl.BlockSpec((tt, POOL_W), lambda b, i: (b * nt + i, cm)),
            pl.BlockSpec((POOL_HALO, POOL_W), lambda b, i: (nxt(b, i), cg)),
            pl.BlockSpec((POOL_HALO, POOL_W), lambda b, i: (nxt(b, i), cm)),
            pl.BlockSpec((POOL_W, POOL_W), lambda b, i: (0, 0)), pl.BlockSpec((POOL_W, POOL_W), lambda b, i: (0, 0)),
            pl.BlockSpec((1, POOL_W), lambda b, i: (0, 0))],
        out_specs=[pl.BlockSpec((tt, 2 * POOL_W), lambda b, i: (b * nt + i, 0)),
                   pl.BlockSpec((POOL_W, POOL_W), lambda b, i: (0, 0)), pl.BlockSpec((1, POOL_W), lambda b, i: (0, 0))],
        out_shape=[_sds((n, 2 * POOL_W), BF16), _sds((POOL_W, POOL_W), F32), _sds((1, POOL_W), F32)],
        compiler_params=_params(),
    )(proj, proj, proj, dmix, proj, dmix, wbd, wbd_t, scale)


def _fox_gate_bwd(dmix, proj, o_c, name):
    n = proj.shape[0]
    tm = min(256, n)

    def body(dm_ref, gc_ref, oc_ref, da_ref, dg_ref, dl_ref):
        dm, gc, oc = dm_ref[...], gc_ref[...], oc_ref[...]
        sg = _sigmoid(gc)
        da = dm * (gc * sg)
        da_ref[...] = da.astype(BF16)
        dg_ref[...] = (dm * oc * (sg * (1.0 + gc * (1.0 - sg)))).astype(BF16)
        r = lax.broadcasted_iota(jnp.int32, (FOX_W, FC_PAD), 0) // HEAD
        c = lax.broadcasted_iota(jnp.int32, (FOX_W, FC_PAD), 1)
        dl_ref[...] = _dot_hi(da * oc, (r == c).astype(F32))

    def row(wd, j=0):
        return pl.BlockSpec((tm, wd), lambda i: (i, j))

    return _pc(
        body, name, grid=(n // tm,),
        in_specs=[row(FOX_W, (HGRN_W + POOL_W) // FOX_W), row(FOX_W, C_GC // FOX_W), row(FOX_W)],
        out_specs=[row(FOX_W), row(FOX_W), row(FC_PAD)],
        out_shape=[_sds((n, FOX_W), BF16), _sds((n, FOX_W), BF16), _sds((n, FC_PAD), F32)],
        compiler_params=_params(),
    )(dmix, proj, o_c)


def _fox_bwd(proj, da, c_col, c_row, lse_row, delta_row, nb, name, rider=None):
    n = proj.shape[0]
    t = n // nb
    tb = min(256, t)
    nq = t // tb
    pw = 2 * HEAD

    def body(q_ref, k_ref, v_ref, da_ref, cc_ref, cr_ref, lse_ref, dl_ref,
             dq_ref, dk_ref, dv_ref, dck_ref, dcq_ref, dq_s, dk_s, dv_s, dck_s, dcq_s):
        kj, qi = pl.program_id(1), pl.program_id(2)

        @pl.when(jnp.logical_and(kj == 0, qi == 0))
        def _():
            dq_s[...] = jnp.zeros_like(dq_s)
            dcq_s[...] = jnp.zeros_like(dcq_s)

        @pl.when(qi == 0)
        def _():
            dk_s[...] = jnp.zeros_like(dk_s)
            dv_s[...] = jnp.zeros_like(dv_s)
            dck_s[...] = jnp.zeros_like(dck_s)

        @pl.when(qi >= kj)
        def _():
            keys = kj * tb + lax.broadcasted_iota(jnp.int32, (tb, tb), 0)
            qrys = qi * tb + lax.broadcasted_iota(jnp.int32, (tb, tb), 1)
            causal = qrys >= keys
            lo = _lane_lo()
            dck = dck_s[...]
            for p in range(FOX_HEADS // 2):
                sl = slice(p * pw, (p + 1) * pw)
                qp = q_ref[:, sl] * FOX_SCALE
                kf = k_ref[:, sl]
                kp = kf.astype(BF16)
                kt = kf.T.astype(BF16)
                vp = v_ref[:, sl].astype(BF16)
                dap = da_ref[:, sl]
                dk, dv = dk_s[:, sl], dv_s[:, sl]
                for h in range(2):
                    hh = 2 * p + h
                    lm = lo if h == 0 else jnp.logical_not(lo)
                    qm = jnp.where(lm, qp, 0.0).astype(BF16)
                    dam = jnp.where(lm, dap, jnp.zeros_like(dap))
                    s = _dot_nt(kp, qm) + (cr_ref[hh:hh + 1, :] - cc_ref[:, hh:hh + 1])
                    pe = jnp.where(causal, jnp.exp(s - lse_ref[hh:hh + 1, :]), 0.0)
                    dp = _dot_nt(vp, dam)
                    ds = pe * (dp - dl_ref[hh:hh + 1, :])
                    dsb = ds.astype(BF16)
                    dv = dv + _dot(pe.astype(BF16), dam)
                    dk = dk + _dot(dsb, qm)
                    rows = slice(hh * HEAD, (hh + 1) * HEAD)
                    dq_s[qi, rows, :] += _dot(kt[h * HEAD:(h + 1) * HEAD, :], dsb)
                    dck = dck - _put_col(jnp.zeros_like(dck), hh, jnp.sum(ds, axis=1, keepdims=True))
                    dcq_s[qi, hh:hh + 1, :] += _rows_reduce(ds, jnp.add, jnp.sum)
                dk_s[:, sl] = dk
                dv_s[:, sl] = dv
            dck_s[...] = dck

        @pl.when(qi == nq - 1)
        def _():
            dk_ref[...] = dk_s[...].astype(BF16)
            dv_ref[...] = dv_s[...].astype(BF16)
            dck_ref[...] = dck_s[...]

        @pl.when(jnp.logical_and(kj == nq - 1, qi == nq - 1))
        def _():
            for j in range(nq):
                dq_ref[j * tb:(j + 1) * tb, :] = (dq_s[j].T * FOX_SCALE).astype(BF16)
                dcq_ref[:, j * tb:(j + 1) * tb] = dcq_s[j]

    def kspec(wd, j=0):
        return pl.BlockSpec((tb, wd), lambda b, kj, qi: (b * nq + kj, j))

    def qspec(wd, j=0):
        return pl.BlockSpec((tb, wd), lambda b, kj, qi: (b * nq + jnp.maximum(qi, kj), j))

    def qrow():
        return pl.BlockSpec((None, FOX_HEADS, tb), lambda b, kj, qi: (b, 0, jnp.maximum(qi, kj)))

    return _call(
        body, name, (proj, proj, proj, da, c_col, c_row, lse_row, delta_row), rider, grid=(nb, nq, nq),
        in_specs=[qspec(FOX_W, C_QC // FOX_W), kspec(FOX_W, C_KC // FOX_W), kspec(FOX_W, C_VC // FOX_W), qspec(FOX_W),
                  kspec(FC_PAD), qrow(), qrow(), qrow()],
        out_specs=[pl.BlockSpec((t, FOX_W), lambda b, kj, qi: (b, 0)), kspec(FOX_W), kspec(FOX_W), kspec(FC_PAD),
                   pl.BlockSpec((None, FOX_HEADS, t), lambda b, kj, qi: (b, 0, 0))],
        out_shape=[_sds((n, FOX_W), BF16), _sds((n, FOX_W), BF16), _sds((n, FOX_W), BF16), _sds((n, FC_PAD), F32),
                   _sds((nb, FOX_HEADS, t), F32)],
        scratch_shapes=[pltpu.VMEM((nq, FOX_W, tb), F32), pltpu.VMEM((tb, FOX_W), F32), pltpu.VMEM((tb, FOX_W), F32),
                        pltpu.VMEM((tb, FC_PAD), F32), pltpu.VMEM((nq, FOX_HEADS, tb), F32)],
        compiler_params=_params(),
    )


def _fox_decay_bwd(dc_q, dc_k, fc, bias, nb, name):
    n = fc.shape[0]
    t = n // nb
    tt = min(256, t)
    nt = t // tt

    def body(dcq_ref, dck_ref, fc_ref, b_ref, dfc_ref, db_ref, carry):
        i = pl.program_id(1)
        first = jnp.logical_and(pl.program_id(0) == 0, i == 0)

        @pl.when(first)
        def _():
            db_ref[...] = jnp.zeros_like(db_ref)

        @pl.when(i == 0)
        def _():
            carry[...] = jnp.zeros_like(carry)

        r = lax.broadcasted_iota(jnp.int32, (tt, tt), 0)
        cc = lax.broadcasted_iota(jnp.int32, (tt, tt), 1)
        dlf = _dot_hi((r <= cc).astype(F32), dcq_ref[...] + dck_ref[...]) + carry[...]
        carry[...] = dlf[0:1, :]
        dfc = dlf * _sigmoid(-(fc_ref[...] + b_ref[...]))
        dfc_ref[...] = dfc.astype(BF16)
        db_ref[...] += jnp.sum(dfc, axis=0, keepdims=True)

    def row():
        return pl.BlockSpec((tt, FC_PAD), lambda b, i: (b * nt + (nt - 1 - i), 0))

    return _pc(
        body, name, grid=(nb, nt),
        in_specs=[row(), row(), row(), pl.BlockSpec((1, FC_PAD), lambda b, i: (0, 0))],
        out_specs=[row(), pl.BlockSpec((1, FC_PAD), lambda b, i: (0, 0))],
        out_shape=[_sds((n, FC_PAD), BF16), _sds((1, FC_PAD), F32)],
        scratch_shapes=[pltpu.VMEM((1, FC_PAD), F32)],
        compiler_params=_params(),
    )(dc_q, dc_k, fc, bias)


def _in_proj_bwd(pieces, w_main_t, w_fc_t, x, g_pre, dxo, name, rider=None):
    n, d = x.shape
    tm = min(256, n)
    widths = [p.shape[1] for p, _ in pieces]
    offs = [o for _, o in pieces]
    np_ = len(pieces)

    def body(*refs):
        p_refs = refs[:np_]
        wt_ref, wf_ref, x_ref, g_ref, dxo_ref, dx_ref, dg_ref = refs[np_:]

        @pl.when(pl.program_id(0) == 0)
        def _():
            dg_ref[...] = jnp.zeros_like(dg_ref)

        dh = _dot(p_refs[-1][...], wf_ref[...])
        for pr, wd, off in zip(p_refs[:-1], widths[:-1], offs[:-1]):
            for j in range(0, wd, 512):
                jw = min(512, wd - j)
                dh = dh + _dot(pr[:, j:j + jw], wt_ref[off + j:off + j + jw, :])
        xv = x_ref[...]
        r = lax.rsqrt(jnp.mean(xv * xv, axis=-1, keepdims=True) + NORM_EPS)
        xh = xv * r
        dg_ref[...] += jnp.sum(dh * xh, axis=0, keepdims=True)
        dx_ref[...] = dxo_ref[...] + _rms_bwd(dh * g_ref[...], xh, r)

    row = pl.BlockSpec((tm, d), lambda i: (i, 0))
    return _call(
        body, name, (*[p for p, _ in pieces], w_main_t, w_fc_t, x, g_pre, dxo), rider, grid=(n // tm,),
        in_specs=[pl.BlockSpec((tm, wd), lambda i: (i, 0)) for wd in widths] + [
            pl.BlockSpec((MAIN_W, d), lambda i: (0, 0)), pl.BlockSpec((FC_PAD, d), lambda i: (0, 0)),
            row, pl.BlockSpec((1, d), lambda i: (0, 0)), row],
        out_specs=[row, pl.BlockSpec((1, d), lambda i: (0, 0))],
        out_shape=[_sds((n, d), F32), _sds((1, d), F32)],
        compiler_params=_params(),
    )


def _lower_bound_table(lower_bounds, name):
    depth, w = lower_bounds.shape

    def body(lb_ref, o_ref):
        v = lb_ref[...]
        e = jnp.exp(v - jnp.max(v, axis=0, keepdims=True))
        p = e / jnp.sum(e, axis=0, keepdims=True)
        acc = jnp.zeros((1, w), F32)
        for l in range(depth):
            acc = acc + p[l:l + 1, :]
            o_ref[l:l + 1, :] = acc - p[0:1, :]

    return _pc(body, name, out_shape=_sds((depth, w), F32))(lower_bounds)


def _lower_bound_bwd(lower_bounds, dlbs, name):
    depth, w = lower_bounds.shape

    def body(lb_ref, d_ref, o_ref):
        v, dl = lb_ref[...], d_ref[...]
        e = jnp.exp(v - jnp.max(v, axis=0, keepdims=True))
        p = e / jnp.sum(e, axis=0, keepdims=True)
        tot = jnp.sum(dl, axis=0, keepdims=True)
        rows, tail = [], tot
        for l in range(depth):
            rows.append(tail - tot if l == 0 else tail)
            tail = tail - dl[l:l + 1, :]
        dp = jnp.concatenate(rows, axis=0)
        o_ref[...] = p * (dp - jnp.sum(p * dp, axis=0, keepdims=True))

    return _pc(body, name, out_shape=_sds((depth, w), F32))(lower_bounds, dlbs)


def _place():
    x, y, c = lax.axis_index("x"), lax.axis_index("y"), lax.axis_index("c")
    return x, y, c


def _gather_weights(w_in_b, w_out_b):
    arrays = (w_in_b, w_out_b)
    na = len(arrays)

    def body(*refs):
        ins, outs = refs[:na], refs[na:2 * na]
        send_sems, recv_sems, local_sems = refs[2 * na:]
        x, y, c = _place()
        me, sibling = (x, y, c), (x, y, 1 - c)
        chips = [(1 - x, y), (x, 1 - y), (1 - x, 1 - y)]

        def slot(a, px, py, pc):
            return outs[a].at[4 * px + 2 * py + pc]

        def copy(a, k, block, to, own=False):
            return pltpu.make_async_remote_copy(
                src_ref=ins[a] if own else slot(a, *block), dst_ref=slot(a, *block),
                send_sem=send_sems.at[a * 7 + k], recv_sem=recv_sems.at[a * 7 + k],
                device_id=to, device_id_type=MESH)

        mine = [pltpu.make_async_copy(ins[a], slot(a, *me), local_sems.at[a]) for a in range(na)]
        for cp in mine:
            cp.start()
        first = []
        for a in range(na):
            first.append(copy(a, 0, me, sibling, own=True))
            first += [copy(a, 1 + j, me, (*chip, c), own=True) for j, chip in enumerate(chips)]
        for cp in first:
            cp.start()
        passed = []
        for j, chip in enumerate(chips):
            for a in range(na):
                copy(a, 1 + j, (*chip, c), me).wait_recv()
                fw = copy(a, 4 + j, (*chip, c), sibling)
                fw.start()
                passed.append(fw)
        for a in range(na):
            copy(a, 0, sibling, me).wait_recv()
            for j, chip in enumerate(chips):
                copy(a, 4 + j, (*chip, 1 - c), me).wait_recv()
        for cp in first + passed:
            cp.wait_send()
        for cp in mine:
            cp.wait()

    any_spec = pl.BlockSpec(memory_space=pl.ANY)
    return _pc(
        body, "gather_weights",
        in_specs=[any_spec] * na, out_specs=[any_spec] * na,
        out_shape=[_sds((N_DEV,) + a.shape, a.dtype) for a in arrays],
        scratch_shapes=[pltpu.SemaphoreType.DMA((7 * na,)), pltpu.SemaphoreType.DMA((7 * na,)),
                        pltpu.SemaphoreType.DMA((na,))],
    )(*arrays)


def _peer(k):
    x, y, c = _place()
    return (1 - x if k & 4 else x, 1 - y if k & 2 else y, 1 - c if k & 1 else c)


def _remote(src, dst, sems, s, to):
    return pltpu.make_async_remote_copy(src_ref=src, dst_ref=dst, send_sem=sems[0].at[s], recv_sem=sems[1].at[s],
                                        device_id=to, device_id_type=MESH)


def _gather_rider(shards):
    na = len(shards)

    def plan(ins, outs, *sems):
        x, y, c = _place()
        me = 4 * x + 2 * y + c
        locs = [pltpu.make_async_copy(ins[a], outs[a].at[me], sems[2].at[a]) for a in range(na)]
        sends, recvs = [], []
        for k in range(1, N_DEV):
            px, py, pc = _peer(k)
            for a in range(na):
                s = (k - 1) * na + a
                sends.append(_remote(ins[a], outs[a].at[me], sems, s, (px, py, pc)))
                recvs.append(_remote(ins[a], outs[a].at[4 * px + 2 * py + pc], sems, s, (px, py, pc)))
        return sends, recvs, locs

    return _Rider(shards, [_sds((N_DEV,) + a.shape, a.dtype) for a in shards], (N_DEV - 1) * na, na, plan)


def _swap_rider(halves):
    na = len(halves)

    def plan(ins, outs, *sems):
        x, y, c = _place()
        cps = [_remote(ins[a].at[1 - c], outs[a], sems, a, (x, y, 1 - c)) for a in range(na)]
        return cps, cps, []

    return _Rider(halves, [_sds(a.shape[1:], a.dtype) for a in halves], na, 1, plan)


def _chip_exchange_rider(parts, small=None):
    na = len(parts)
    n_chip = N_DEV // 2

    def plan(ins, outs, *sems):
        x, y, c = _place()
        chip = 2 * x + y
        locs = [pltpu.make_async_copy(ins[a].at[chip], outs[a].at[chip], sems[2].at[a]) for a in range(na)]
        sends, recvs = [], []
        for k in range(1, n_chip):
            px, py, _ = _peer(2 * k)
            for a in range(na):
                s = (k - 1) * na + a
                sends.append(_remote(ins[a].at[2 * px + py], outs[a].at[chip], sems, s, (px, py, c)))
                recvs.append(_remote(ins[a].at[2 * px + py], outs[a].at[2 * px + py], sems, s, (px, py, c)))
        if small is not None:
            me = 2 * chip + c
            locs.append(pltpu.make_async_copy(ins[na], outs[na].at[me], sems[2].at[na]))
            for k in range(1, N_DEV):
                px, py, pc = _peer(k)
                s = (n_chip - 1) * na + k - 1
                sends.append(_remote(ins[na], outs[na].at[me], sems, s, (px, py, pc)))
                recvs.append(_remote(ins[na], outs[na].at[4 * px + 2 * py + pc], sems, s, (px, py, pc)))
        return sends, recvs, locs

    extra = [] if small is None else [small]
    shapes = [_sds(a.shape, a.dtype) for a in parts] + [_sds((N_DEV,) + s.shape, s.dtype) for s in extra]
    n_sems = (n_chip - 1) * na + (N_DEV - 1) * len(extra)
    return _Rider(list(parts) + extra, shapes, n_sems, na + len(extra), plan)


def _pair_add(halves, other, core, name):
    _, nch, r, c = halves.shape
    tr = 256 if r % 256 == 0 else r

    def body(h_ref, o_ref, c_ref, p_ref):
        mine = jnp.where(c_ref[...] == 0, h_ref[0].astype(F32), h_ref[1].astype(F32))
        p_ref[...] = (mine + o_ref[...].astype(F32)).astype(BF16)

    blk = pl.BlockSpec((None, tr, c), lambda j, i: (j, i, 0))
    return _pc(
        body, name, grid=(nch, r // tr),
        in_specs=[pl.BlockSpec((2, None, tr, c), lambda j, i: (0, j, i, 0)), blk, pl.BlockSpec((1, 1), lambda j, i: (0, 0))],
        out_specs=blk, out_shape=_sds((nch, r, c), BF16),
        compiler_params=_params(),
    )(halves, other, core)


def _sum_adamw(parts, w, m, v, name):
    nl, r, c = w.shape
    tr = 256 if r % 256 == 0 else r

    def body(*refs):
        p_refs = refs[:nl]
        w_ref, m_ref, v_ref, g_ref, d_ref, mo_ref, vo_ref = refs[nl:]
        for l in range(nl):
            @pl.when(pl.program_id(0) == l)
            def _(p_ref=p_refs[l]):
                g = p_ref[0].astype(F32)
                for j in range(1, p_ref.shape[0]):
                    g = g + p_ref[j].astype(F32)
                mn = ADAM_B1 * m_ref[...] + (1.0 - ADAM_B1) * g
                vn = ADAM_B2 * v_ref[...] + (1.0 - ADAM_B2) * (g * g)
                m_hat = mn / (1.0 - ADAM_B1 ** ADAM_STEP)
                v_hat = vn / (1.0 - ADAM_B2 ** ADAM_STEP)
                g_ref[...] = g
                d_ref[...] = -ADAM_LR * (m_hat / (jnp.sqrt(v_hat) + ADAM_EPS) + ADAM_WD * w_ref[...])
                mo_ref[...] = mn
                vo_ref[...] = vn

    def part_spec(l, k):
        return pl.BlockSpec((k, tr, c), lambda li, i: (0, jnp.where(li == l, i, 0), 0))

    row = pl.BlockSpec((None, tr, c), lambda li, i: (li, i, 0))
    return _pc(
        body, name, grid=(nl, r // tr),
        in_specs=[part_spec(l, p.shape[0]) for l, p in enumerate(parts)] + [row, row, row],
        out_specs=[row] * 4,
        out_shape=[_sds((nl, r, c), F32)] * 4,
        compiler_params=_params(),
    )(*parts, w, m, v)


SMALL = ("lower_bounds", "pre_norm_g", "hgrn_norm_g", "fox_f_bias", "pool_w", "pool_scale", "post_norm_g")
SMALL_LANES = 128


def _pack_small(tree):
    flat = jnp.concatenate([tree[k].reshape(-1) for k in SMALL])
    rows = -(-flat.shape[0] // (8 * SMALL_LANES)) * 8
    return jnp.pad(flat, (0, rows * SMALL_LANES - flat.shape[0])).reshape(rows, SMALL_LANES)


def _unpack_small(packed, like):
    flat, out, off = packed.reshape(-1), {}, 0
    for k in SMALL:
        size = like[k].size
        out[k] = flat[off:off + size].reshape(like[k].shape)
        off += size
    return out


def _block_diag(pw):
    g = pw.shape[0]
    eye = jnp.eye(g, dtype=pw.dtype)
    return (eye[:, None, :, None] * pw[:, :, None, :]).reshape(g * HEAD, g * HEAD)


def _layer_weights(g_in, g_out):
    d = g_in.shape[1]
    in_w = N_DEV * g_in.shape[2]
    full_in = g_in.transpose(1, 0, 2).reshape(d, in_w)
    w_main = full_in[:, :MAIN_W]
    w_fc = jnp.pad(full_in[:, MAIN_W:], ((0, 0), (0, FC_PAD - (in_w - MAIN_W))))
    full_out = g_out.reshape(N_DEV * g_out.shape[1], d)
    return w_main, w_fc, w_main.T, w_fc.T, full_out, full_out.T


def _layer_fwd(l, x, lbs, weights, lw, nb, rider=None):
    n = x.shape[0]
    t = n // nb
    w_main, w_fc, _, _, w_out, _ = lw
    bias = jnp.pad(weights["fox_f_bias"][l:l + 1], ((0, 0), (0, FC_PAD - FOX_HEADS)))
    wbd = _block_diag(weights["pool_w"][l]).astype(BF16)
    proj, fc, ht = _in_proj_fwd(x, weights["pre_norm_g"][l:l + 1], w_main, w_fc, f"in_proj_fwd_{l}")
    c_col = _fox_decay_fwd(fc, bias, nb, f"fox_decay_fwd_{l}")
    c_row = c_col.reshape(nb, t, FC_PAD)[:, :, :FOX_HEADS].transpose(0, 2, 1)
    o_h, s0 = _hgrn_fwd(proj, lbs[l:l + 1], _block_ones(HGRN_W, BF16), nb, f"hgrn_fwd_{l}")
    o_b = _pool_fwd(proj, wbd, weights["pool_scale"][l:l + 1], nb, f"pool_fwd_{l}")
    (o_c, lse), rode = _fox_fwd(proj, c_col, c_row, nb, f"fox_fwd_{l}", rider)
    x_next, mixt, y = _merge_fwd(x, proj, o_h, o_b, o_c, weights["hgrn_norm_g"][l:l + 1], w_out,
                                 weights["post_norm_g"][l:l + 1], f"merge_fwd_{l}")
    return x_next, (x, proj, fc, ht, c_col, c_row, o_h, s0, o_c, lse, mixt, y, bias, wbd), rode


def _layer_bwd(l, dx, saved, lbs, weights, lw, nb, rider=None):
    _, proj, fc, ht, c_col, c_row, o_h, s0, o_c, lse, mixt, y, bias, wbd = saved
    n = proj.shape[0]
    t = n // nb
    w_out_t = lw[5]
    g = {}
    dy, dmix, dgp = _merge_bwd(dx, y, weights["post_norm_g"][l:l + 1], w_out_t, f"merge_bwd_{l}")
    g["post_norm_g"] = dgp[0]
    g["w_out"] = _w_out_grad(mixt, dy, f"w_out_grad_{l}")
    d_a, dgh, dlb = _hgrn_bwd(dmix, proj, o_h, s0, weights["hgrn_norm_g"][l:l + 1], lbs[l:l + 1],
                              _block_ones(HGRN_W, BF16), nb, f"hgrn_bwd_{l}")
    g["hgrn_norm_g"], g["lbs"] = dgh[0], dlb[0]
    d_b, dwbd, dps = _pool_bwd(dmix, proj, wbd, wbd.T, weights["pool_scale"][l:l + 1], nb, f"pool_bwd_{l}")
    g["pool_w"] = jnp.stack([dwbd[j * HEAD:(j + 1) * HEAD, j * HEAD:(j + 1) * HEAD] for j in range(len(POOL_WINDOWS))])
    g["pool_scale"] = dps[0]
    da, d_gc, delta = _fox_gate_bwd(dmix, proj, o_c, f"fox_gate_bwd_{l}")
    delta_row, lse_row = [a.reshape(nb, t, FC_PAD)[:, :, :FOX_HEADS].transpose(0, 2, 1) for a in (delta, lse)]
    (d_qc, d_kc, d_vc, dc_k, dc_q), rode = _fox_bwd(proj, da, c_col, c_row, lse_row, delta_row, nb, f"fox_bwd_{l}", rider)
    dc_q = jnp.pad(dc_q.transpose(0, 2, 1).reshape(n, FOX_HEADS), ((0, 0), (0, FC_PAD - FOX_HEADS)))
    d_fc, dbias = _fox_decay_bwd(dc_q, dc_k, fc, bias, nb, f"fox_decay_bwd_{l}")
    g["fox_f_bias"] = dbias[0, :FOX_HEADS]
    pieces = [(d_a, C_QA), (d_b, C_UB), (d_qc, C_QC), (d_kc, C_KC), (d_vc, C_VC), (d_gc, C_GC), (d_fc, None)]
    g["w_in"] = _w_in_grad(ht, pieces, f"w_in_grad_{l}")
    return g, pieces, rode


def _layer_bwd_input(l, dx, pieces, saved, weights, lw, rider=None):
    (dxi, dgpre), rode = _in_proj_bwd(pieces, lw[2], lw[3], saved[0], weights["pre_norm_g"][l:l + 1], dx,
                                      f"in_proj_bwd_{l}", rider)
    return dxi, dgpre[0], rode


def kernel(x, lower_bounds, pre_norm_g, w_in, hgrn_norm_g, fox_f_bias, pool_w, pool_scale, w_out, post_norm_g, loss_target, m_lower_bounds, m_pre_norm_g, m_w_in, m_hgrn_norm_g, m_fox_f_bias, m_pool_w, m_pool_scale, m_w_out, m_post_norm_g, v_lower_bounds, v_pre_norm_g, v_w_in, v_hgrn_norm_g, v_fox_f_bias, v_pool_w, v_pool_scale, v_w_out, v_post_norm_g):
    weights = dict(lower_bounds=lower_bounds, pre_norm_g=pre_norm_g, hgrn_norm_g=hgrn_norm_g, fox_f_bias=fox_f_bias,
                   pool_w=pool_w, pool_scale=pool_scale, post_norm_g=post_norm_g)
    mom_m = dict(lower_bounds=m_lower_bounds, pre_norm_g=m_pre_norm_g, hgrn_norm_g=m_hgrn_norm_g, fox_f_bias=m_fox_f_bias,
                 pool_w=m_pool_w, pool_scale=m_pool_scale, post_norm_g=m_post_norm_g)
    mom_v = dict(lower_bounds=v_lower_bounds, pre_norm_g=v_pre_norm_g, hgrn_norm_g=v_hgrn_norm_g, fox_f_bias=v_fox_f_bias,
                 pool_w=v_pool_w, pool_scale=v_pool_scale, post_norm_g=v_post_norm_g)
    depth = w_in.shape[0]
    nb, t, d = x.shape
    n = nb * t
    core = lax.axis_index("c").astype(jnp.int32).reshape(1, 1)
    shards = [(w_in[l].astype(BF16), w_out[l].astype(BF16)) for l in range(depth)]
    lbs = _lower_bound_table(lower_bounds, "lower_bound_table")

    lw = [_layer_weights(*_gather_weights(*shards[0]))]
    xl, saved = x.reshape(n, d), []
    for l in range(depth):
        rider = _gather_rider(shards[l + 1]) if l + 1 < depth else None
        xl, sv, rode = _layer_fwd(l, xl, lbs, weights, lw[l], nb, rider)
        saved.append(sv)
        if rode is not None:
            lw.append(_layer_weights(*rode))
    dx, sq = _loss_fwd_bwd(xl, loss_target.reshape(n, d), "loss")
    loss = lax.psum(0.5 * jnp.sum(sq) / d, AXES)

    grads, recv, pending = [None] * depth, [None] * depth, None
    for l in reversed(range(depth)):
        rider = None if pending is None else _chip_exchange_rider(pending)
        g, pieces, rode = _layer_bwd(l, dx, saved[l], lbs, weights, lw[l], nb, rider)
        if rode is not None:
            recv[l + 1] = rode
        halves = (g["w_in"], g["w_out"])
        other = _run_rider(_swap_rider(halves), f"grad_swap_{l}")
        pending = [_pair_add(hv, ot, core, f"grad_pair_add_{l}_{i}") for i, (hv, ot) in enumerate(zip(halves, other))]
        rider = _chip_exchange_rider(pending) if l == 0 else None
        dx, g["pre_norm_g"], rode = _layer_bwd_input(l, dx, pieces, saved[l], weights, lw[l], rider)
        if rode is not None:
            recv[l] = rode
        grads[l] = g
    small = {k: jnp.stack([grads[l][k] for l in range(depth)]) for k in SMALL if k != "lower_bounds"}
    small["lower_bounds"] = _lower_bound_bwd(lower_bounds, jnp.stack([grads[l]["lbs"] for l in range(depth)]),
                                             "lower_bound_bwd")
    (r_small,) = _run_rider(_gather_rider([_pack_small(small)]), "small_grads_gather")

    res_in = _sum_adamw([recv[l][0] for l in range(depth)], w_in, m_w_in, v_w_in, "adamw_w_in")
    res_out = _sum_adamw([recv[l][1] for l in range(depth)], w_out, m_w_out, v_w_out, "adamw_w_out")
    res_small = _sum_adamw([r_small], _pack_small(weights)[None], _pack_small(mom_m)[None], _pack_small(mom_v)[None],
                           "adamw_small")

    names = ("lower_bounds", "pre_norm_g", "w_in", "hgrn_norm_g", "fox_f_bias", "pool_w", "pool_scale", "w_out", "post_norm_g")
    outs = [loss, dx.reshape(nb, t, d)]
    for i in range(4):
        full = dict(_unpack_small(res_small[i][0], weights), w_in=res_in[i], w_out=res_out[i])
        outs += [full[k] for k in names]
    return tuple(outs)
```

```python
import functools

import jax
import jax.numpy as jnp
from jax import lax
from jax.experimental import pallas as pl
from jax.experimental.pallas import tpu as pltpu

F32, BF16 = jnp.float32, jnp.bfloat16
HI = lax.Precision.HIGHEST
MESH = pl.DeviceIdType.MESH
AXES = ("x", "y", "c")
N_DEV = 8

NORM_EPS = 1e-6
MASK_VALUE = -1e30
TINY = 1e-30
CHUNK = 64
SUB = 16
HGRN_W, POOL_W, FOX_W = 256, 256, 512
HEAD = 64
FOX_HEADS = 8
POOL_WINDOWS = (2, 4, 8, 16)
POOL_HALO = 16
MAIN_W = 3584
FC_PAD = 128
C_QA, C_FA, C_IA, C_GA, C_UB, C_GB, C_QC, C_KC, C_VC, C_GC = 0, 256, 512, 768, 1024, 1280, 1536, 2048, 2560, 3072
FOX_SCALE = HEAD ** -0.5

ADAM_LR, ADAM_B1, ADAM_B2, ADAM_EPS, ADAM_WD, ADAM_STEP = 0.001, 0.9, 0.999, 1e-08, 0.01, 10

VMEM_LIMIT = 56 * 1024 * 1024


def _pc(fn, name, **kw):
    return pl.pallas_call(fn, name=name, **kw)


def _params(**kw):
    return pltpu.CompilerParams(vmem_limit_bytes=VMEM_LIMIT, **kw)


class _Rider:
    def __init__(self, inputs, out_shapes, n_sems, n_local, plan):
        self.inputs, self.out_shapes, self.n_sems, self.n_local, self.plan = list(inputs), list(out_shapes), n_sems, n_local, plan

    def start(self, ins, outs, *sems):
        sends, _, locs = self.plan(ins, outs, *sems)
        for cp in locs + sends:
            cp.start()

    def wait(self, ins, outs, *sems):
        sends, recvs, locs = self.plan(ins, outs, *sems)
        for cp in recvs:
            cp.wait_recv()
        for cp in sends:
            cp.wait_send()
        for cp in locs:
            cp.wait()

    def sem_shapes(self):
        return [pltpu.SemaphoreType.DMA((self.n_sems,)), pltpu.SemaphoreType.DMA((self.n_sems,)),
                pltpu.SemaphoreType.DMA((self.n_local,))]


def _call(body, name, args, rider=None, *, grid, in_specs, out_specs, out_shape, scratch_shapes=(), **kw):
    if rider is None:
        res = _pc(body, name, grid=grid, in_specs=in_specs, out_specs=out_specs, out_shape=out_shape,
                  scratch_shapes=list(scratch_shapes), **kw)(*args)
        return res, None
    n_in, n_out, n_scr = len(in_specs), len(out_specs), len(scratch_shapes)
    n_rin, n_rout = len(rider.inputs), len(rider.out_shapes)

    def ridden(*refs):
        ins, refs = refs[:n_in], refs[n_in:]
        rins, refs = refs[:n_rin], refs[n_rin:]
        outs, refs = refs[:n_out], refs[n_out:]
        routs, refs = refs[:n_rout], refs[n_rout:]
        scr, sems = refs[:n_scr], refs[n_scr:]
        first = functools.reduce(jnp.logical_and, [pl.program_id(a) == 0 for a in range(len(grid))])
        last = functools.reduce(jnp.logical_and, [pl.program_id(a) == g - 1 for a, g in enumerate(grid)])

        @pl.when(first)
        def _():
            rider.start(rins, routs, *sems)

        body(*ins, *outs, *scr)

        @pl.when(last)
        def _():
            rider.wait(rins, routs, *sems)

    any_spec = pl.BlockSpec(memory_space=pl.ANY)
    res = _pc(ridden, name, grid=grid, in_specs=list(in_specs) + [any_spec] * n_rin,
              out_specs=list(out_specs) + [any_spec] * n_rout, out_shape=list(out_shape) + rider.out_shapes,
              scratch_shapes=list(scratch_shapes) + rider.sem_shapes(), **kw)(*args, *rider.inputs)
    return res[:n_out], res[n_out:]


def _run_rider(rider, name):
    n_rin = len(rider.inputs)

    def body(*refs):
        ins, outs, sems = refs[:n_rin], refs[n_rin:n_rin + len(rider.out_shapes)], refs[n_rin + len(rider.out_shapes):]
        rider.start(ins, outs, *sems)
        rider.wait(ins, outs, *sems)

    any_spec = pl.BlockSpec(memory_space=pl.ANY)
    return _pc(body, name, in_specs=[any_spec] * n_rin, out_specs=[any_spec] * len(rider.out_shapes),
               out_shape=rider.out_shapes, scratch_shapes=rider.sem_shapes())(*rider.inputs)


def _dot(a, b):
    return jnp.dot(a, b, preferred_element_type=F32)


def _dot_nt(a, b):
    return lax.dot_general(a, b, (((1,), (1,)), ((), ())), preferred_element_type=F32)


def _dot_tn(a, b):
    return lax.dot_general(a, b, (((0,), (0,)), ((), ())), preferred_element_type=F32)


def _dot_hi(a, b):
    return jnp.dot(a, b, precision=HI, preferred_element_type=F32)


def _sigmoid(x):
    return 1.0 / (1.0 + jnp.exp(-x))


def _block_ones(n, dtype):
    r = lax.broadcasted_iota(jnp.int32, (n, n), 0) // HEAD
    c = lax.broadcasted_iota(jnp.int32, (n, n), 1) // HEAD
    return (r == c).astype(dtype)


def _sds(shape, dtype):
    return jax.ShapeDtypeStruct(shape, dtype)


def _in_proj_fwd(x, g_pre, w_main, w_fc, name):
    n, d = x.shape
    tm = min(256, n)

    def body(x_ref, g_ref, w_ref, wf_ref, proj_ref, fc_ref, ht_ref):
        xv = x_ref[...]
        r = lax.rsqrt(jnp.mean(xv * xv, axis=-1, keepdims=True) + NORM_EPS)
        hf = xv * r * g_ref[...]
        hb = hf.astype(BF16)
        ht_ref[...] = hf.T.astype(BF16)
        for j in range(0, MAIN_W, 512):
            proj_ref[:, j:j + 512] = _dot(hb, w_ref[:, j:j + 512])
        fc_ref[...] = _dot(hb, wf_ref[...])

    return _pc(
        body, name, grid=(n // tm,),
        in_specs=[pl.BlockSpec((tm, d), lambda i: (i, 0)), pl.BlockSpec((1, d), lambda i: (0, 0)),
                  pl.BlockSpec((d, MAIN_W), lambda i: (0, 0)), pl.BlockSpec((d, FC_PAD), lambda i: (0, 0))],
        out_specs=[pl.BlockSpec((tm, MAIN_W), lambda i: (i, 0)), pl.BlockSpec((tm, FC_PAD), lambda i: (i, 0)),
                   pl.BlockSpec((d, tm), lambda i: (0, i))],
        out_shape=[_sds((n, MAIN_W), F32), _sds((n, FC_PAD), F32), _sds((d, n), BF16)],
        compiler_params=_params(),
    )(x, g_pre, w_main, w_fc)


def _fox_decay_fwd(fc, bias, nb, name):
    n = fc.shape[0]
    t = n // nb
    tt = min(256, t)
    nt = t // tt

    def body(fc_ref, b_ref, c_ref, carry):
        i = pl.program_id(1)

        @pl.when(i == 0)
        def _():
            carry[...] = jnp.zeros_like(carry)

        xv = fc_ref[...] + b_ref[...]
        lf = jnp.minimum(xv, 0.0) - jnp.log(1.0 + jnp.exp(-jnp.abs(xv)))
        r = lax.broadcasted_iota(jnp.int32, (tt, tt), 0)
        cc = lax.broadcasted_iota(jnp.int32, (tt, tt), 1)
        cs = _dot_hi((r >= cc).astype(F32), lf) + carry[...]
        c_ref[...] = cs
        carry[...] = cs[tt - 1:tt, :]

    return _pc(
        body, name, grid=(nb, nt),
        in_specs=[pl.BlockSpec((tt, FC_PAD), lambda b, i: (b * nt + i, 0)), pl.BlockSpec((1, FC_PAD), lambda b, i: (0, 0))],
        out_specs=pl.BlockSpec((tt, FC_PAD), lambda b, i: (b * nt + i, 0)),
        out_shape=_sds((n, FC_PAD), F32),
        scratch_shapes=[pltpu.VMEM((1, FC_PAD), F32)],
        compiler_params=_params(),
    )(fc, bias)


def _hgrn_gates(q, z, lb):
    sig = _sigmoid(z)
    sn = _sigmoid(-z)
    f = lb + (1.0 - lb) * sig
    g = jnp.log(jnp.maximum(f, TINY))
    k = (1.0 - lb) * sn
    sq = _sigmoid(q)
    return sig, sn, f, g, k, sq


def _sub_tri(n, lower):
    r = lax.broadcasted_iota(jnp.int32, (n, n), 0)
    c = lax.broadcasted_iota(jnp.int32, (n, n), 1)
    tri = (r >= c) if lower else (r <= c)
    return jnp.logical_and(r // SUB == c // SUB, tri).astype(F32)


def _hgrn_decays(qs, k, b):
    srow = lax.broadcasted_iota(jnp.int32, (SUB, HGRN_W), 0)
    es, ws = [], []
    for t in range(SUB):
        e = jnp.where(srow <= t, jnp.exp(b[t:t + 1, :] - b), 0.0)
        es.append(e)
        ws.append(e * (qs[t:t + 1, :] * k))
    return srow, es, ws


def _hgrn_state_step(st, k, v, b, bmask):
    bl = b[SUB - 1:SUB, :]
    ktil = k * jnp.exp(bl - b)
    return st * jnp.exp(bl) + _dot_tn(v.astype(BF16), ktil.astype(BF16)) * bmask


def _hgrn_sub_fwd(qs, k, v, b, st, ones_b, bmask):
    srow, _, ws = _hgrn_decays(qs, k, b)
    aexp = _dot(jnp.concatenate(ws, axis=0).astype(BF16), ones_b)
    o = _dot_nt((qs * jnp.exp(b)).astype(BF16), st.astype(BF16))
    for t in range(SUB):
        row = jnp.sum(aexp[t * SUB:(t + 1) * SUB, :] * v, axis=0, keepdims=True)
        o = o + jnp.where(srow == t, row, 0.0)
    return o, _hgrn_state_step(st, k, v, b, bmask)


def _hgrn_tile(t):
    return min(256, t)


def _hgrn_fwd(proj, lb, ones_b, nb, name):
    n = proj.shape[0]
    t = n // nb
    tt = _hgrn_tile(t)
    nt = t // tt
    ncs = tt // CHUNK
    w = HGRN_W

    def body(q_ref, z_ref, v_ref, lb_ref, ones_ref, o_ref, s0_ref, st_s, b_s, qs_s, k_s):
        @pl.when(pl.program_id(1) == 0)
        def _():
            st_s[...] = jnp.zeros_like(st_s)

        q = q_ref[...]
        _, _, _, g, k, sq = _hgrn_gates(q, z_ref[...], lb_ref[...])
        b_s[...] = _dot_hi(_sub_tri(tt, True), g)
        qs_s[...] = q * sq
        k_s[...] = k
        bmask = _block_ones(w, F32)
        ones_b = ones_ref[...]

        def chunk(c, carry):
            st = st_s[...]
            s0_ref[c] = st
            base = pl.multiple_of(c * CHUNK, CHUNK)
            for u in range(CHUNK // SUB):
                rows = pl.ds(base + u * SUB, SUB)
                o, st = _hgrn_sub_fwd(qs_s[rows, :], k_s[rows, :], v_ref[rows, :], b_s[rows, :], st, ones_b, bmask)
                o_ref[rows, :] = o
            st_s[...] = st
            return carry

        lax.fori_loop(0, ncs, chunk, 0)

    def col(j):
        return pl.BlockSpec((tt, w), lambda b, i: (b * nt + i, j))

    return _pc(
        body, name, grid=(nb, nt),
        in_specs=[col(C_QA // w), col(C_FA // w), col(C_IA // w), pl.BlockSpec((1, w), lambda b, i: (0, 0)),
                  pl.BlockSpec((w, w), lambda b, i: (0, 0))],
        out_specs=[pl.BlockSpec((tt, w), lambda b, i: (b * nt + i, 0)),
                   pl.BlockSpec((ncs, w, w), lambda b, i: (b * nt + i, 0, 0))],
        out_shape=[_sds((n, w), F32), _sds((n // CHUNK, w, w), F32)],
        scratch_shapes=[pltpu.VMEM((w, w), F32)] + [pltpu.VMEM((tt, w), F32)] * 3,
        compiler_params=_params(),
    )(proj, proj, proj, lb, ones_b)


def _pool_lane_windows():
    lane = lax.broadcasted_iota(jnp.int32, (1, POOL_W), 1) // HEAD
    wl = jnp.zeros((1, POOL_W), F32)
    for gi, win in enumerate(POOL_WINDOWS):
        wl = jnp.where(lane == gi, float(win), wl)
    return lane, wl


def _pool_select(lane, parts):
    out = parts[-1]
    for gi in range(len(parts) - 2, -1, -1):
        out = jnp.where(lane == gi, parts[gi], out)
    return out


def _pool_mix(u, halo, t0, tt):
    lane, wl = _pool_lane_windows()
    ext = jnp.concatenate([halo, u], axis=0)
    sums, cur, shift = [], ext, 1
    for _ in POOL_WINDOWS:
        cur = cur + pltpu.roll(cur, shift, axis=0)
        sums.append(cur[POOL_HALO:, :])
        shift *= 2
    tpos = (t0 + lax.broadcasted_iota(jnp.int32, (tt, POOL_W), 0)).astype(F32)
    cnt = jnp.minimum(tpos + 1.0, wl)
    return _pool_select(lane, sums) / cnt - u, cnt


def _pool_specs(tt, nt, nhb):
    cu, cg = C_UB // POOL_W, C_GB // POOL_W
    return [pl.BlockSpec((tt, POOL_W), lambda b, i: (b * nt + i, cu)),
            pl.BlockSpec((tt, POOL_W), lambda b, i: (b * nt + i, cg)),
            pl.BlockSpec((POOL_HALO, POOL_W), lambda b, i: (jnp.maximum((b * nt + i) * nhb - 1, 0), cu))]


def _pool_fwd(proj, wbd, scale, nb, name):
    n = proj.shape[0]
    t = n // nb
    tt = min(256, t)
    nt = t // tt
    nhb = tt // POOL_HALO

    def body(u_ref, g_ref, h_ref, w_ref, s_ref, o_ref):
        i = pl.program_id(1)
        halo = jnp.where(i == 0, 0.0, h_ref[...])
        pooled, _ = _pool_mix(u_ref[...], halo, i * tt, tt)
        gv = g_ref[...]
        o_ref[...] = _dot(pooled.astype(BF16), w_ref[...]) * s_ref[...] * (gv * _sigmoid(gv))

    return _pc(
        body, name, grid=(nb, nt),
        in_specs=_pool_specs(tt, nt, nhb) + [pl.BlockSpec((POOL_W, POOL_W), lambda b, i: (0, 0)),
                                             pl.BlockSpec((1, POOL_W), lambda b, i: (0, 0))],
        out_specs=pl.BlockSpec((tt, POOL_W), lambda b, i: (b * nt + i, 0)),
        out_shape=_sds((n, POOL_W), F32),
        compiler_params=_params(),
    )(proj, proj, proj, wbd, scale)


def _rows_reduce(x, op, final):
    while x.shape[0] > 8 and x.shape[0] % 16 == 0:
        half = x.shape[0] // 2
        x = op(x[:half], x[half:])
    return final(x, axis=0, keepdims=True)


def _lane_lo():
    return lax.broadcasted_iota(jnp.int32, (1, 2 * HEAD), 1) < HEAD


def _put_col(tile, hh, colv):
    lane = lax.broadcasted_iota(jnp.int32, tile.shape, 1)
    return jnp.where(lane == hh, colv, tile)


def _fox_fwd(proj, c_col, c_row, nb, name, rider=None):
    n = proj.shape[0]
    t = n // nb
    tb = min(256, t)
    nq = t // tb
    pw = 2 * HEAD

    def body(q_ref, k_ref, v_ref, cc_ref, cr_ref, o_ref, lse_ref, m_s, acc_s, cq_s):
        qi, kj = pl.program_id(1), pl.program_id(2)

        @pl.when(kj == 0)
        def _():
            m_s[...] = jnp.full_like(m_s, -jnp.inf)
            acc_s[...] = jnp.zeros_like(acc_s)
            for hh in range(FOX_HEADS):
                cq_s[hh] = jnp.broadcast_to(cc_ref[:, hh:hh + 1], (tb, pw))

        def block(masked):
            lo = _lane_lo()
            if masked:
                causal = lax.broadcasted_iota(jnp.int32, (tb, tb), 0) >= lax.broadcasted_iota(jnp.int32, (tb, tb), 1)
            for p in range(FOX_HEADS // 2):
                sl = slice(p * pw, (p + 1) * pw)
                qp = q_ref[:, sl] * FOX_SCALE
                kp = k_ref[:, sl].astype(BF16)
                vf = v_ref[:, sl]
                for h in range(2):
                    hh = 2 * p + h
                    lm = lo if h == 0 else jnp.logical_not(lo)
                    s = _dot_nt(jnp.where(lm, qp, 0.0).astype(BF16), kp)
                    s = s + (jnp.tile(cq_s[hh], (1, tb // pw)) - cr_ref[hh:hh + 1, :])
                    if masked:
                        s = jnp.where(causal, s, MASK_VALUE)
                    m_prev = m_s[hh]
                    m_new = jnp.maximum(m_prev, jnp.max(s, axis=1, keepdims=True))
                    alpha = jnp.exp(m_prev - m_new)
                    pe = jnp.exp(s - jnp.tile(m_new, (1, tb // pw)))
                    m_s[hh] = m_new
                    acc_s[hh] = alpha * acc_s[hh] + _dot(pe.astype(BF16), jnp.where(lm, vf, 1.0).astype(BF16))

        @pl.when(kj < qi)
        def _():
            block(False)

        @pl.when(kj == qi)
        def _():
            block(True)
            lo = _lane_lo()
            lse = jnp.zeros((tb, FC_PAD), F32)
            for p in range(FOX_HEADS // 2):
                halves = []
                for h in range(2):
                    hh = 2 * p + h
                    lm = lo if h == 0 else jnp.logical_not(lo)
                    acc = acc_s[hh]
                    swapped = pltpu.roll(acc, HEAD, axis=1)
                    halves.append(acc / swapped)
                    lse = _put_col(lse, hh, m_s[hh] + jnp.log(jnp.where(lm, swapped, acc)))
                o_ref[:, p * pw:(p + 1) * pw] = jnp.where(lo, halves[0], halves[1])
            lse_ref[...] = lse

    def qspec(wd, j):
        return pl.BlockSpec((tb, wd), lambda b, qi, kj: (b * nq + qi, j))

    def kspec(j):
        return pl.BlockSpec((tb, FOX_W), lambda b, qi, kj: (b * nq + jnp.minimum(kj, qi), j))

    return _call(
        body, name, (proj, proj, proj, c_col, c_row), rider, grid=(nb, nq, nq),
        in_specs=[qspec(FOX_W, C_QC // FOX_W), kspec(C_KC // FOX_W), kspec(C_VC // FOX_W), qspec(FC_PAD, 0),
                  pl.BlockSpec((None, FOX_HEADS, tb), lambda b, qi, kj: (b, 0, jnp.minimum(kj, qi)))],
        out_specs=[qspec(FOX_W, 0), qspec(FC_PAD, 0)],
        out_shape=[_sds((n, FOX_W), F32), _sds((n, FC_PAD), F32)],
        scratch_shapes=[pltpu.VMEM((FOX_HEADS, tb, pw), F32), pltpu.VMEM((FOX_HEADS, tb, pw), F32),
                        pltpu.VMEM((FOX_HEADS, tb, pw), F32)],
        compiler_params=_params(),
    )


def _head_mean(x, ones_f):
    return _dot_hi(x, ones_f) * (1.0 / HEAD)


def _merge_fwd(x, proj, o_h, o_b, o_c, gh, w_out, g_post, name):
    n, d = x.shape
    tm = min(256, n)

    def body(x_ref, ga_ref, gc_ref, oh_ref, ob_ref, oc_ref, gh_ref, w_ref, gp_ref, xo_ref, mixt_ref, y_ref):
        oh = oh_ref[...]
        ones_f = _block_ones(HGRN_W, F32)
        na = oh * lax.rsqrt(_head_mean(oh * oh, ones_f) + NORM_EPS) * gh_ref[...]
        ga, gc = ga_ref[...], gc_ref[...]
        mixed = jnp.concatenate([na * (ga * _sigmoid(ga)), ob_ref[...], oc_ref[...] * (gc * _sigmoid(gc))], axis=1)
        mixt_ref[...] = mixed.T.astype(BF16)
        y = _dot(mixed.astype(BF16), w_ref[...])
        y_ref[...] = y
        xo_ref[...] = x_ref[...] + y * lax.rsqrt(jnp.mean(y * y, axis=-1, keepdims=True) + NORM_EPS) * gp_ref[...]

    def row(wd, j=0):
        return pl.BlockSpec((tm, wd), lambda i: (i, j))

    def full(a, b):
        return pl.BlockSpec((a, b), lambda i: (0, 0))

    return _pc(
        body, name, grid=(n // tm,),
        in_specs=[row(d), row(HGRN_W, C_GA // HGRN_W), row(FOX_W, C_GC // FOX_W), row(HGRN_W), row(POOL_W), row(FOX_W),
                  full(1, HGRN_W), full(d, d), full(1, d)],
        out_specs=[row(d), pl.BlockSpec((d, tm), lambda i: (0, i)), row(d)],
        out_shape=[_sds((n, d), F32), _sds((d, n), BF16), _sds((n, d), F32)],
        compiler_params=_params(),
    )(x, proj, proj, o_h, o_b, o_c, gh, w_out, g_post)


def _loss_fwd_bwd(x, target, name):
    n, d = x.shape
    tm = min(512, n)

    def body(x_ref, t_ref, dx_ref, sq_ref):
        @pl.when(pl.program_id(0) == 0)
        def _():
            sq_ref[...] = jnp.zeros_like(sq_ref)

        e = x_ref[...] - t_ref[...]
        dx_ref[...] = e * (1.0 / d)
        sq_ref[...] += jnp.sum(e * e, axis=0, keepdims=True)

    return _pc(
        body, name, grid=(n // tm,),
        in_specs=[pl.BlockSpec((tm, d), lambda i: (i, 0))] * 2,
        out_specs=[pl.BlockSpec((tm, d), lambda i: (i, 0)), pl.BlockSpec((1, d), lambda i: (0, 0))],
        out_shape=[_sds((n, d), F32), _sds((1, d), F32)],
        compiler_params=_params(),
    )(x, target)


def _rms_bwd(dy_scaled, xhat, r):
    return r * (dy_scaled - xhat * jnp.mean(dy_scaled * xhat, axis=-1, keepdims=True))


def _merge_bwd(dxo, y, g_post, w_out_t, name):
    n, d = y.shape
    tm = min(256, n)

    def body(dx_ref, y_ref, gp_ref, wt_ref, dy_ref, dm_ref, dgp_ref):
        @pl.when(pl.program_id(0) == 0)
        def _():
            dgp_ref[...] = jnp.zeros_like(dgp_ref)

        yv, dxv = y_ref[...], dx_ref[...]
        r = lax.rsqrt(jnp.mean(yv * yv, axis=-1, keepdims=True) + NORM_EPS)
        yh = yv * r
        dgp_ref[...] += jnp.sum(dxv * yh, axis=0, keepdims=True)
        dyb = _rms_bwd(dxv * gp_ref[...], yh, r).astype(BF16)
        dy_ref[...] = dyb
        dm_ref[...] = _dot(dyb, wt_ref[...])

    row = pl.BlockSpec((tm, d), lambda i: (i, 0))
    return _pc(
        body, name, grid=(n // tm,),
        in_specs=[row, row, pl.BlockSpec((1, d), lambda i: (0, 0)), pl.BlockSpec((d, d), lambda i: (0, 0))],
        out_specs=[row, row, pl.BlockSpec((1, d), lambda i: (0, 0))],
        out_shape=[_sds((n, d), BF16), _sds((n, d), F32), _sds((1, d), F32)],
        compiler_params=_params(),
    )(dxo, y, g_post, w_out_t)


def _w_out_grad(mixt, dy, name):
    d, n = mixt.shape
    rows = d // N_DEV

    def body(a_ref, b_ref, o_ref):
        o_ref[...] = _dot(a_ref[...], b_ref[...]).astype(BF16)

    return _pc(
        body, name, grid=(N_DEV,),
        in_specs=[pl.BlockSpec((rows, n), lambda j: (j, 0)), pl.BlockSpec((n, d), lambda j: (0, 0))],
        out_specs=pl.BlockSpec((None, None, rows, d), lambda j: (j % 2, j // 2, 0, 0)),
        out_shape=_sds((2, N_DEV // 2, rows, d), BF16),
        compiler_params=_params(),
    )(mixt, dy)


def _w_in_grad(ht, pieces, name):
    d, n = ht.shape
    ta, tk = min(512, d), min(512, n)
    nk = n // tk
    arrays = [p for p, _ in pieces]
    widths = [p.shape[1] for p in arrays]
    offs = [sum(widths[:i]) for i in range(len(widths))]
    in_w = MAIN_W + FOX_HEADS
    shard = in_w // N_DEV

    def body(*refs):
        a_ref, p_refs = refs[0], refs[1:1 + len(arrays)]
        o_ref, acc = refs[1 + len(arrays):]
        k = pl.program_id(1)

        @pl.when(k == 0)
        def _():
            acc[...] = jnp.zeros_like(acc)

        a = a_ref[...]
        for pr, off, wd in zip(p_refs, offs, widths):
            for j in range(0, wd, 512):
                jw = min(512, wd - j)
                acc[:, off + j:off + j + jw] += _dot(a, pr[:, j:j + jw])

        @pl.when(k == nk - 1)
        def _():
            for j in range(N_DEV):
                o_ref[j % 2, j // 2] = acc[:, j * shard:(j + 1) * shard].astype(BF16)

    return _pc(
        body, name, grid=(d // ta, nk),
        in_specs=[pl.BlockSpec((ta, tk), lambda i, k: (i, k))] + [pl.BlockSpec((tk, wd), lambda i, k: (k, 0)) for wd in widths],
        out_specs=pl.BlockSpec((2, N_DEV // 2, ta, shard), lambda i, k: (0, 0, i, 0)),
        out_shape=_sds((2, N_DEV // 2, d, shard), BF16),
        scratch_shapes=[pltpu.VMEM((ta, sum(widths)), F32)],
        compiler_params=_params(),
    )(ht, *arrays)


def _hgrn_sub_bwd(qs, k, v, b, do, s0, ds1, ones_b, bmask):
    bl = b[SUB - 1:SUB, :]
    eb, ebl, ekt = jnp.exp(b), jnp.exp(bl), jnp.exp(bl - b)
    qe, ktil = qs * eb, k * ekt
    ds1b, dob = ds1.astype(BF16), do.astype(BF16)
    dv = _dot_nt(ktil.astype(BF16), ds1b)
    dqe = _dot(dob, s0.astype(BF16))
    dktil = _dot(v.astype(BF16), ds1b)
    dbl = jnp.sum(dktil * ktil, axis=0, keepdims=True) + ebl * jnp.sum(s0 * ds1, axis=0, keepdims=True)
    ds0 = ds1 * ebl + _dot_tn(dob, qe.astype(BF16)) * bmask
    srow, es, ws = _hgrn_decays(qs, k, b)
    aexp = _dot(jnp.concatenate(ws, axis=0).astype(BF16), ones_b)
    gexp = _dot(jnp.concatenate([do[t:t + 1, :] * v for t in range(SUB)], axis=0).astype(BF16), ones_b)
    dq = dqe * eb
    dk = dktil * ekt
    for t in range(SUB):
        sl = slice(t * SUB, (t + 1) * SUB)
        ge = gexp[sl, :] * es[t]
        dq = dq + jnp.where(srow == t, jnp.sum(ge * k, axis=0, keepdims=True), 0.0)
        dk = dk + ge * qs[t:t + 1, :]
        dv = dv + aexp[sl, :] * do[t:t + 1, :]
    return dq, dk, dv, dbl, ds0


def _hgrn_bwd(dmix, proj, o_h, s0, gh, lb, ones_b, nb, name):
    n = proj.shape[0]
    t = n // nb
    tt = _hgrn_tile(t)
    nt = t // tt
    ncs = tt // CHUNK
    nsub = CHUNK // SUB
    w = HGRN_W

    def body(dm_ref, q_ref, z_ref, v_ref, ga_ref, oh_ref, s0_ref, gh_ref, lb_ref, ones_ref,
             dp_ref, dgh_ref, dlb_ref, ds_s, ss_s, b_s, qs_s, k_s, do_s, dq_s, dk_s, dv_s, dbl_s):
        first = jnp.logical_and(pl.program_id(0) == 0, pl.program_id(1) == 0)

        @pl.when(first)
        def _():
            dgh_ref[...] = jnp.zeros_like(dgh_ref)
            dlb_ref[...] = jnp.zeros_like(dlb_ref)

        @pl.when(pl.program_id(1) == 0)
        def _():
            ds_s[...] = jnp.zeros_like(ds_s)

        ones_b = ones_ref[...]
        ones_f = ones_b.astype(F32)
        bmask = _block_ones(w, F32)
        lbv, ghv = lb_ref[...], gh_ref[...]
        oh, ga, dm = oh_ref[...], ga_ref[...], dm_ref[...]
        rn = lax.rsqrt(_head_mean(oh * oh, ones_f) + NORM_EPS)
        nh = oh * rn
        sga = _sigmoid(ga)
        dp_ref[:, 3 * w:4 * w] = (dm * nh * ghv * (sga * (1.0 + ga * (1.0 - sga)))).astype(BF16)
        dn = dm * (ga * sga)
        dgh_ref[...] += jnp.sum(dn * nh, axis=0, keepdims=True)
        dn = dn * ghv
        do_s[...] = rn * (dn - nh * _head_mean(dn * nh, ones_f))
        q = q_ref[...]
        sig, sn, f, g, k, sq = _hgrn_gates(q, z_ref[...], lbv)
        qs = q * sq
        b_s[...] = _dot_hi(_sub_tri(tt, True), g)
        qs_s[...] = qs
        k_s[...] = k

        def chunk(cc, carry):
            c = ncs - 1 - cc
            base = pl.multiple_of(c * CHUNK, CHUNK)
            st = s0_ref[c]
            for u in range(nsub):
                ss_s[u] = st
                if u < nsub - 1:
                    rows = pl.ds(base + u * SUB, SUB)
                    st = _hgrn_state_step(st, k_s[rows, :], v_ref[rows, :], b_s[rows, :], bmask)
            ds = ds_s[...]
            for u in reversed(range(nsub)):
                rows = pl.ds(base + u * SUB, SUB)
                dq, dk, dv, dbl, ds = _hgrn_sub_bwd(qs_s[rows, :], k_s[rows, :], v_ref[rows, :], b_s[rows, :],
                                                    do_s[rows, :], ss_s[u], ds, ones_b, bmask)
                dq_s[rows, :] = dq
                dk_s[rows, :] = dk
                dv_s[rows, :] = dv
                dbl_s[rows, :] = jnp.broadcast_to(dbl, (SUB, w))
            ds_s[...] = ds
            return carry

        lax.fori_loop(0, ncs, chunk, 0)
        dqs, dk = dq_s[...], dk_s[...]
        dg = _dot_hi(_sub_tri(tt, False), qs * dqs - k * dk) + dbl_s[...]
        dfz = jnp.where(f > TINY, dg / jnp.maximum(f, TINY), 0.0)
        dlb_ref[...] += jnp.sum(dfz * (1.0 - sig) - dk * sn, axis=0, keepdims=True)
        dp_ref[:, 0:w] = (dqs * (sq * (1.0 + q * (1.0 - sq)))).astype(BF16)
        dp_ref[:, w:2 * w] = ((dfz - dk) * (1.0 - lbv) * sig * sn).astype(BF16)
        dp_ref[:, 2 * w:3 * w] = dv_s[...].astype(BF16)

    def rv(b, i):
        return b * nt + (nt - 1 - i)

    def col(j):
        return pl.BlockSpec((tt, w), lambda b, i: (rv(b, i), j))

    def full(a, bb):
        return pl.BlockSpec((a, bb), lambda b, i: (0, 0))

    return _pc(
        body, name, grid=(nb, nt),
        in_specs=[col(0), col(C_QA // w), col(C_FA // w), col(C_IA // w), col(C_GA // w), col(0),
                  pl.BlockSpec((ncs, w, w), lambda b, i: (rv(b, i), 0, 0)), full(1, w), full(1, w), full(w, w)],
        out_specs=[pl.BlockSpec((tt, 4 * w), lambda b, i: (rv(b, i), 0)), full(1, w), full(1, w)],
        out_shape=[_sds((n, 4 * w), BF16), _sds((1, w), F32), _sds((1, w), F32)],
        scratch_shapes=[pltpu.VMEM((w, w), F32), pltpu.VMEM((nsub, w, w), F32)] + [pltpu.VMEM((tt, w), F32)] * 8,
        compiler_params=_params(),
    )(dmix, proj, proj, proj, proj, o_h, s0, gh, lb, ones_b)


def _pool_bwd(dmix, proj, wbd, wbd_t, scale, nb, name):
    n = proj.shape[0]
    t = n // nb
    tt = min(256, t)
    nt = t // tt
    nhb = tt // POOL_HALO
    cu, cg, cm = C_UB // POOL_W, C_GB // POOL_W, HGRN_W // POOL_W

    def body(u_ref, g_ref, h_ref, dm_ref, gn_ref, dmn_ref, w_ref, wt_ref, s_ref, dp_ref, dw_ref, ds_ref):
        i = pl.program_id(1)
        first = jnp.logical_and(pl.program_id(0) == 0, i == 0)

        @pl.when(first)
        def _():
            dw_ref[...] = jnp.zeros_like(dw_ref)
            ds_ref[...] = jnp.zeros_like(ds_ref)

        sc = s_ref[...]
        halo = jnp.where(i == 0, 0.0, h_ref[...])
        pooled, cnt = _pool_mix(u_ref[...], halo, i * tt, tt)
        pb = pooled.astype(BF16)
        pre = _dot(pb, w_ref[...])
        gv, dm = g_ref[...], dm_ref[...]
        sg = _sigmoid(gv)
        silu = gv * sg
        dgb = dm * pre * sc * (sg * (1.0 + gv * (1.0 - sg)))
        ds_ref[...] += jnp.sum(dm * pre * silu, axis=0, keepdims=True)
        dpre = (dm * sc * silu).astype(BF16)
        dw_ref[...] += _dot_tn(pb, dpre)
        dpool = _dot(dpre, wt_ref[...])
        gn = gn_ref[...]
        dpre_n = (dmn_ref[...] * sc * (gn * _sigmoid(gn))).astype(BF16)
        dpool_n = jnp.where(i == nt - 1, 0.0, _dot(dpre_n, wt_ref[...]))
        lane, wl = _pool_lane_windows()
        tpos_n = ((i + 1) * tt + lax.broadcasted_iota(jnp.int32, (POOL_HALO, POOL_W), 0)).astype(F32)
        ext = jnp.concatenate([dpool / cnt, dpool_n / jnp.minimum(tpos_n + 1.0, wl)], axis=0)
        rows = tt + POOL_HALO
        sums, cur, shift = [], ext, 1
        for _ in POOL_WINDOWS:
            cur = cur + pltpu.roll(cur, rows - shift, axis=0)
            sums.append(cur[:tt, :])
            shift *= 2
        du = _pool_select(lane, sums) - dpool
        dp_ref[...] = jnp.concatenate([du, dgb], axis=1).astype(BF16)

    def nxt(b, i):
        return jnp.minimum((b * nt + i + 1) * nhb, n // POOL_HALO - 1)

    return _pc(
        body, name, grid=(nb, nt),
        in_specs=_pool_specs(tt, nt, nhb) + [
            pl.BlockSpec((tt, POOL_W), lambda b, i: (b * nt + i, cm)),
            pl.BlockSpec((POOL_HALO, POOL_W), lambda b, i: (nxt(b, i), cg)),
            pl.BlockSpec((POOL_HALO, POOL_W), lambda b, i: (nxt(b, i), cm)),
            pl.BlockSpec((POOL_W, POOL_W), lambda b, i: (0, 0)), pl.BlockSpec((POOL_W, POOL_W), lambda b, i: (0, 0)),
            pl.BlockSpec((1, POOL_W), lambda b, i: (0, 0))],
        out_specs=[pl.BlockSpec((tt, 2 * POOL_W), lambda b, i: (b * nt + i, 0)),
                   pl.BlockSpec((POOL_W, POOL_W), lambda b, i: (0, 0)), pl.BlockSpec((1, POOL_W), lambda b, i: (0, 0))],
        out_shape=[_sds((n, 2 * POOL_W), BF16), _sds((POOL_W, POOL_W), F32), _sds((1, POOL_W), F32)],
        compiler_params=_params(),
    )(proj, proj, proj, dmix, proj, dmix, wbd, wbd_t, scale)


def _fox_gate_bwd(dmix, proj, o_c, name):
    n = proj.shape[0]
    tm = min(256, n)

    def body(dm_ref, gc_ref, oc_ref, da_ref, dg_ref, dl_ref):
        dm, gc, oc = dm_ref[...], gc_ref[...], oc_ref[...]
        sg = _sigmoid(gc)
        da = dm * (gc * sg)
        da_ref[...] = da.astype(BF16)
        dg_ref[...] = (dm * oc * (sg * (1.0 + gc * (1.0 - sg)))).astype(BF16)
        r = lax.broadcasted_iota(jnp.int32, (FOX_W, FC_PAD), 0) // HEAD
        c = lax.broadcasted_iota(jnp.int32, (FOX_W, FC_PAD), 1)
        dl_ref[...] = _dot_hi(da * oc, (r == c).astype(F32))

    def row(wd, j=0):
        return pl.BlockSpec((tm, wd), lambda i: (i, j))

    return _pc(
        body, name, grid=(n // tm,),
        in_specs=[row(FOX_W, (HGRN_W + POOL_W) // FOX_W), row(FOX_W, C_GC // FOX_W), row(FOX_W)],
        out_specs=[row(FOX_W), row(FOX_W), row(FC_PAD)],
        out_shape=[_sds((n, FOX_W), BF16), _sds((n, FOX_W), BF16), _sds((n, FC_PAD), F32)],
        compiler_params=_params(),
    )(dmix, proj, o_c)


def _fox_bwd(proj, da, c_col, c_row, lse_row, delta_row, nb, name, rider=None):
    n = proj.shape[0]
    t = n // nb
    tb = min(256, t)
    nq = t // tb
    pw = 2 * HEAD

    def body(q_ref, k_ref, v_ref, da_ref, cc_ref, cr_ref, lse_ref, dl_ref,
             dq_ref, dk_ref, dv_ref, dck_ref, dcq_ref, dq_s, dk_s, dv_s, dck_s, dcq_s):
        kj, qi = pl.program_id(1), pl.program_id(2)

        @pl.when(jnp.logical_and(kj == 0, qi == 0))
        def _():
            dq_s[...] = jnp.zeros_like(dq_s)
            dcq_s[...] = jnp.zeros_like(dcq_s)

        @pl.when(qi == 0)
        def _():
            dk_s[...] = jnp.zeros_like(dk_s)
            dv_s[...] = jnp.zeros_like(dv_s)
            dck_s[...] = jnp.zeros_like(dck_s)

        def block(masked):
            lo = _lane_lo()
            if masked:
                causal = lax.broadcasted_iota(jnp.int32, (tb, tb), 1) >= lax.broadcasted_iota(jnp.int32, (tb, tb), 0)
            dck = dck_s[...]
            for p in range(FOX_HEADS // 2):
                sl = slice(p * pw, (p + 1) * pw)
                qp = q_ref[:, sl] * FOX_SCALE
                kf = k_ref[:, sl]
                kp = kf.astype(BF16)
                kt = kf.T.astype(BF16)
                vp = v_ref[:, sl].astype(BF16)
                dap = da_ref[:, sl]
                dk, dv = dk_s[:, sl], dv_s[:, sl]
                for h in range(2):
                    hh = 2 * p + h
                    lm = lo if h == 0 else jnp.logical_not(lo)
                    qm = jnp.where(lm, qp, 0.0).astype(BF16)
                    dam = jnp.where(lm, dap, jnp.zeros_like(dap))
                    s = _dot_nt(kp, qm) + (cr_ref[hh:hh + 1, :] - cc_ref[:, hh:hh + 1])
                    pe = jnp.exp(s - lse_ref[hh:hh + 1, :])
                    if masked:
                        pe = jnp.where(causal, pe, 0.0)
                    dp = _dot_nt(vp, dam)
                    ds = pe * (dp - dl_ref[hh:hh + 1, :])
                    dsb = ds.astype(BF16)
                    dv = dv + _dot(pe.astype(BF16), dam)
                    dk = dk + _dot(dsb, qm)
                    rows = slice(hh * HEAD, (hh + 1) * HEAD)
                    dq_s[qi, rows, :] += _dot(kt[h * HEAD:(h + 1) * HEAD, :], dsb)
                    dck = dck - _put_col(jnp.zeros_like(dck), hh, jnp.sum(ds, axis=1, keepdims=True))
                    dcq_s[qi, hh:hh + 1, :] += _rows_reduce(ds, jnp.add, jnp.sum)
                dk_s[:, sl] = dk
                dv_s[:, sl] = dv
            dck_s[...] = dck

        @pl.when(qi > kj)
        def _():
            block(False)

        @pl.when(qi == kj)
        def _():
            block(True)

        @pl.when(qi == nq - 1)
        def _():
            dk_ref[...] = dk_s[...].astype(BF16)
            dv_ref[...] = dv_s[...].astype(BF16)
            dck_ref[...] = dck_s[...]

        @pl.when(jnp.logical_and(kj == nq - 1, qi == nq - 1))
        def _():
            for j in range(nq):
                dq_ref[j * tb:(j + 1) * tb, :] = (dq_s[j].T * FOX_SCALE).astype(BF16)
                dcq_ref[:, j * tb:(j + 1) * tb] = dcq_s[j]

    def kspec(wd, j=0):
        return pl.BlockSpec((tb, wd), lambda b, kj, qi: (b * nq + kj, j))

    def qspec(wd, j=0):
        return pl.BlockSpec((tb, wd), lambda b, kj, qi: (b * nq + jnp.maximum(qi, kj), j))

    def qrow():
        return pl.BlockSpec((None, FOX_HEADS, tb), lambda b, kj, qi: (b, 0, jnp.maximum(qi, kj)))

    return _call(
        body, name, (proj, proj, proj, da, c_col, c_row, lse_row, delta_row), rider, grid=(nb, nq, nq),
        in_specs=[qspec(FOX_W, C_QC // FOX_W), kspec(FOX_W, C_KC // FOX_W), kspec(FOX_W, C_VC // FOX_W), qspec(FOX_W),
                  kspec(FC_PAD), qrow(), qrow(), qrow()],
        out_specs=[pl.BlockSpec((t, FOX_W), lambda b, kj, qi: (b, 0)), kspec(FOX_W), kspec(FOX_W), kspec(FC_PAD),
                   pl.BlockSpec((None, FOX_HEADS, t), lambda b, kj, qi: (b, 0, 0))],
        out_shape=[_sds((n, FOX_W), BF16), _sds((n, FOX_W), BF16), _sds((n, FOX_W), BF16), _sds((n, FC_PAD), F32),
                   _sds((nb, FOX_HEADS, t), F32)],
        scratch_shapes=[pltpu.VMEM((nq, FOX_W, tb), F32), pltpu.VMEM((tb, FOX_W), F32), pltpu.VMEM((tb, FOX_W), F32),
                        pltpu.VMEM((tb, FC_PAD), F32), pltpu.VMEM((nq, FOX_HEADS, tb), F32)],
        compiler_params=_params(),
    )


def _fox_decay_bwd(dc_q, dc_k, fc, bias, nb, name):
    n = fc.shape[0]
    t = n // nb
    tt = min(256, t)
    nt = t // tt

    def body(dcq_ref, dck_ref, fc_ref, b_ref, dfc_ref, db_ref, carry):
        i = pl.program_id(1)
        first = jnp.logical_and(pl.program_id(0) == 0, i == 0)

        @pl.when(first)
        def _():
            db_ref[...] = jnp.zeros_like(db_ref)

        @pl.when(i == 0)
        def _():
            carry[...] = jnp.zeros_like(carry)

        r = lax.broadcasted_iota(jnp.int32, (tt, tt), 0)
        cc = lax.broadcasted_iota(jnp.int32, (tt, tt), 1)
        dlf = _dot_hi((r <= cc).astype(F32), dcq_ref[...] + dck_ref[...]) + carry[...]
        carry[...] = dlf[0:1, :]
        dfc = dlf * _sigmoid(-(fc_ref[...] + b_ref[...]))
        dfc_ref[...] = dfc.astype(BF16)
        db_ref[...] += jnp.sum(dfc, axis=0, keepdims=True)

    def row():
        return pl.BlockSpec((tt, FC_PAD), lambda b, i: (b * nt + (nt - 1 - i), 0))

    return _pc(
        body, name, grid=(nb, nt),
        in_specs=[row(), row(), row(), pl.BlockSpec((1, FC_PAD), lambda b, i: (0, 0))],
        out_specs=[row(), pl.BlockSpec((1, FC_PAD), lambda b, i: (0, 0))],
        out_shape=[_sds((n, FC_PAD), BF16), _sds((1, FC_PAD), F32)],
        scratch_shapes=[pltpu.VMEM((1, FC_PAD), F32)],
        compiler_params=_params(),
    )(dc_q, dc_k, fc, bias)


def _in_proj_bwd(pieces, w_main_t, w_fc_t, x, g_pre, dxo, name, rider=None):
    n, d = x.shape
    tm = min(256, n)
    widths = [p.shape[1] for p, _ in pieces]
    offs = [o for _, o in pieces]
    np_ = len(pieces)

    def body(*refs):
        p_refs = refs[:np_]
        wt_ref, wf_ref, x_ref, g_ref, dxo_ref, dx_ref, dg_ref = refs[np_:]

        @pl.when(pl.program_id(0) == 0)
        def _():
            dg_ref[...] = jnp.zeros_like(dg_ref)

        dh = _dot(p_refs[-1][...], wf_ref[...])
        for pr, wd, off in zip(p_refs[:-1], widths[:-1], offs[:-1]):
            for j in range(0, wd, 512):
                jw = min(512, wd - j)
                dh = dh + _dot(pr[:, j:j + jw], wt_ref[off + j:off + j + jw, :])
        xv = x_ref[...]
        r = lax.rsqrt(jnp.mean(xv * xv, axis=-1, keepdims=True) + NORM_EPS)
        xh = xv * r
        dg_ref[...] += jnp.sum(dh * xh, axis=0, keepdims=True)
        dx_ref[...] = dxo_ref[...] + _rms_bwd(dh * g_ref[...], xh, r)

    row = pl.BlockSpec((tm, d), lambda i: (i, 0))
    return _call(
        body, name, (*[p for p, _ in pieces], w_main_t, w_fc_t, x, g_pre, dxo), rider, grid=(n // tm,),
        in_specs=[pl.BlockSpec((tm, wd), lambda i: (i, 0)) for wd in widths] + [
            pl.BlockSpec((MAIN_W, d), lambda i: (0, 0)), pl.BlockSpec((FC_PAD, d), lambda i: (0, 0)),
            row, pl.BlockSpec((1, d), lambda i: (0, 0)), row],
        out_specs=[row, pl.BlockSpec((1, d), lambda i: (0, 0))],
        out_shape=[_sds((n, d), F32), _sds((1, d), F32)],
        compiler_params=_params(),
    )


def _lower_bound_table(lower_bounds, name):
    depth, w = lower_bounds.shape

    def body(lb_ref, o_ref):
        v = lb_ref[...]
        e = jnp.exp(v - jnp.max(v, axis=0, keepdims=True))
        p = e / jnp.sum(e, axis=0, keepdims=True)
        acc = jnp.zeros((1, w), F32)
        for l in range(depth):
            acc = acc + p[l:l + 1, :]
            o_ref[l:l + 1, :] = acc - p[0:1, :]

    return _pc(body, name, out_shape=_sds((depth, w), F32))(lower_bounds)


def _lower_bound_bwd(lower_bounds, dlbs, name):
    depth, w = lower_bounds.shape

    def body(lb_ref, d_ref, o_ref):
        v, dl = lb_ref[...], d_ref[...]
        e = jnp.exp(v - jnp.max(v, axis=0, keepdims=True))
        p = e / jnp.sum(e, axis=0, keepdims=True)
        tot = jnp.sum(dl, axis=0, keepdims=True)
        rows, tail = [], tot
        for l in range(depth):
            rows.append(tail - tot if l == 0 else tail)
            tail = tail - dl[l:l + 1, :]
        dp = jnp.concatenate(rows, axis=0)
        o_ref[...] = p * (dp - jnp.sum(p * dp, axis=0, keepdims=True))

    return _pc(body, name, out_shape=_sds((depth, w), F32))(lower_bounds, dlbs)


def _place():
    x, y, c = lax.axis_index("x"), lax.axis_index("y"), lax.axis_index("c")
    return x, y, c


def _gather_weights(w_in_b, w_out_b):
    arrays = (w_in_b, w_out_b)
    na = len(arrays)

    def body(*refs):
        ins, outs = refs[:na], refs[na:2 * na]
        send_sems, recv_sems, local_sems = refs[2 * na:]
        x, y, c = _place()
        me, sibling = (x, y, c), (x, y, 1 - c)
        chips = [(1 - x, y), (x, 1 - y), (1 - x, 1 - y)]

        def slot(a, px, py, pc):
            return outs[a].at[4 * px + 2 * py + pc]

        def copy(a, k, block, to, own=False):
            return pltpu.make_async_remote_copy(
                src_ref=ins[a] if own else slot(a, *block), dst_ref=slot(a, *block),
                send_sem=send_sems.at[a * 7 + k], recv_sem=recv_sems.at[a * 7 + k],
                device_id=to, device_id_type=MESH)

        mine = [pltpu.make_async_copy(ins[a], slot(a, *me), local_sems.at[a]) for a in range(na)]
        for cp in mine:
            cp.start()
        first = []
        for a in range(na):
            first.append(copy(a, 0, me, sibling, own=True))
            first += [copy(a, 1 + j, me, (*chip, c), own=True) for j, chip in enumerate(chips)]
        for cp in first:
            cp.start()
        passed = []
        for j, chip in enumerate(chips):
            for a in range(na):
                copy(a, 1 + j, (*chip, c), me).wait_recv()
                fw = copy(a, 4 + j, (*chip, c), sibling)
                fw.start()
                passed.append(fw)
        for a in range(na):
            copy(a, 0, sibling, me).wait_recv()
            for j, chip in enumerate(chips):
                copy(a, 4 + j, (*chip, 1 - c), me).wait_recv()
        for cp in first + passed:
            cp.wait_send()
        for cp in mine:
            cp.wait()

    any_spec = pl.BlockSpec(memory_space=pl.ANY)
    return _pc(
        body, "gather_weights",
        in_specs=[any_spec] * na, out_specs=[any_spec] * na,
        out_shape=[_sds((N_DEV,) + a.shape, a.dtype) for a in arrays],
        scratch_shapes=[pltpu.SemaphoreType.DMA((7 * na,)), pltpu.SemaphoreType.DMA((7 * na,)),
                        pltpu.SemaphoreType.DMA((na,))],
    )(*arrays)


def _peer(k):
    x, y, c = _place()
    return (1 - x if k & 4 else x, 1 - y if k & 2 else y, 1 - c if k & 1 else c)


def _remote(src, dst, sems, s, to):
    return pltpu.make_async_remote_copy(src_ref=src, dst_ref=dst, send_sem=sems[0].at[s], recv_sem=sems[1].at[s],
                                        device_id=to, device_id_type=MESH)


def _gather_rider(shards):
    na = len(shards)

    def plan(ins, outs, *sems):
        x, y, c = _place()
        me = 4 * x + 2 * y + c
        locs = [pltpu.make_async_copy(ins[a], outs[a].at[me], sems[2].at[a]) for a in range(na)]
        sends, recvs = [], []
        for k in range(1, N_DEV):
            px, py, pc = _peer(k)
            for a in range(na):
                s = (k - 1) * na + a
                sends.append(_remote(ins[a], outs[a].at[me], sems, s, (px, py, pc)))
                recvs.append(_remote(ins[a], outs[a].at[4 * px + 2 * py + pc], sems, s, (px, py, pc)))
        return sends, recvs, locs

    return _Rider(shards, [_sds((N_DEV,) + a.shape, a.dtype) for a in shards], (N_DEV - 1) * na, na, plan)


def _swap_rider(halves):
    na = len(halves)

    def plan(ins, outs, *sems):
        x, y, c = _place()
        cps = [_remote(ins[a].at[1 - c], outs[a], sems, a, (x, y, 1 - c)) for a in range(na)]
        return cps, cps, []

    return _Rider(halves, [_sds(a.shape[1:], a.dtype) for a in halves], na, 1, plan)


def _chip_exchange_rider(parts, small=None):
    na = len(parts)
    n_chip = N_DEV // 2

    def plan(ins, outs, *sems):
        x, y, c = _place()
        chip = 2 * x + y
        locs = [pltpu.make_async_copy(ins[a].at[chip], outs[a].at[chip], sems[2].at[a]) for a in range(na)]
        sends, recvs = [], []
        for k in range(1, n_chip):
            px, py, _ = _peer(2 * k)
            for a in range(na):
                s = (k - 1) * na + a
                sends.append(_remote(ins[a].at[2 * px + py], outs[a].at[chip], sems, s, (px, py, c)))
                recvs.append(_remote(ins[a].at[2 * px + py], outs[a].at[2 * px + py], sems, s, (px, py, c)))
        if small is not None:
            me = 2 * chip + c
            locs.append(pltpu.make_async_copy(ins[na], outs[na].at[me], sems[2].at[na]))
            for k in range(1, N_DEV):
                px, py, pc = _peer(k)
                s = (n_chip - 1) * na + k - 1
                sends.append(_remote(ins[na], outs[na].at[me], sems, s, (px, py, pc)))
                recvs.append(_remote(ins[na], outs[na].at[4 * px + 2 * py + pc], sems, s, (px, py, pc)))
        return sends, recvs, locs

    extra = [] if small is None else [small]
    shapes = [_sds(a.shape, a.dtype) for a in parts] + [_sds((N_DEV,) + s.shape, s.dtype) for s in extra]
    n_sems = (n_chip - 1) * na + (N_DEV - 1) * len(extra)
    return _Rider(list(parts) + extra, shapes, n_sems, na + len(extra), plan)


def _pair_add(halves, other, core, name):
    _, nch, r, c = halves.shape
    tr = 256 if r % 256 == 0 else r

    def body(h_ref, o_ref, c_ref, p_ref):
        mine = jnp.where(c_ref[...] == 0, h_ref[0].astype(F32), h_ref[1].astype(F32))
        p_ref[...] = (mine + o_ref[...].astype(F32)).astype(BF16)

    blk = pl.BlockSpec((None, tr, c), lambda j, i: (j, i, 0))
    return _pc(
        body, name, grid=(nch, r // tr),
        in_specs=[pl.BlockSpec((2, None, tr, c), lambda j, i: (0, j, i, 0)), blk, pl.BlockSpec((1, 1), lambda j, i: (0, 0))],
        out_specs=blk, out_shape=_sds((nch, r, c), BF16),
        compiler_params=_params(),
    )(halves, other, core)


def _sum_adamw(parts, w, m, v, name, rider=None):
    nl, r, c = w.shape
    tr = 256 if r % 256 == 0 else r

    def body(*refs):
        p_refs = refs[:nl]
        w_ref, m_ref, v_ref, g_ref, d_ref, mo_ref, vo_ref = refs[nl:]
        for l in range(nl):
            @pl.when(pl.program_id(0) == l)
            def _(p_ref=p_refs[l]):
                g = p_ref[0].astype(F32)
                for j in range(1, p_ref.shape[0]):
                    g = g + p_ref[j].astype(F32)
                mn = ADAM_B1 * m_ref[...] + (1.0 - ADAM_B1) * g
                vn = ADAM_B2 * v_ref[...] + (1.0 - ADAM_B2) * (g * g)
                m_hat = mn / (1.0 - ADAM_B1 ** ADAM_STEP)
                v_hat = vn / (1.0 - ADAM_B2 ** ADAM_STEP)
                g_ref[...] = g
                d_ref[...] = -ADAM_LR * (m_hat / (jnp.sqrt(v_hat) + ADAM_EPS) + ADAM_WD * w_ref[...])
                mo_ref[...] = mn
                vo_ref[...] = vn

    def part_spec(l, k):
        return pl.BlockSpec((k, tr, c), lambda li, i: (0, jnp.where(li == l, i, 0), 0))

    row = pl.BlockSpec((None, tr, c), lambda li, i: (li, i, 0))
    return _call(
        body, name, (*parts, w, m, v), rider, grid=(nl, r // tr),
        in_specs=[part_spec(l, p.shape[0]) for l, p in enumerate(parts)] + [row, row, row],
        out_specs=[row] * 4,
        out_shape=[_sds((nl, r, c), F32)] * 4,
        compiler_params=_params(),
    )


SMALL = ("lower_bounds", "pre_norm_g", "hgrn_norm_g", "fox_f_bias", "pool_w", "pool_scale", "post_norm_g")
SMALL_LANES = 128


def _pack_small(tree):
    flat = jnp.concatenate([tree[k].reshape(-1) for k in SMALL])
    rows = -(-flat.shape[0] // (8 * SMALL_LANES)) * 8
    return jnp.pad(flat, (0, rows * SMALL_LANES - flat.shape[0])).reshape(rows, SMALL_LANES)


def _unpack_small(packed, like):
    flat, out, off = packed.reshape(-1), {}, 0
    for k in SMALL:
        size = like[k].size
        out[k] = flat[off:off + size].reshape(like[k].shape)
        off += size
    return out


def _block_diag(pw):
    g = pw.shape[0]
    eye = jnp.eye(g, dtype=pw.dtype)
    return (eye[:, None, :, None] * pw[:, :, None, :]).reshape(g * HEAD, g * HEAD)


def _assemble_w_in(g_in, name):
    _, d, shard = g_in.shape
    tr = min(256, d)
    wide = MAIN_W + FC_PAD

    def body(g_ref, wm_ref, wf_ref, wmt_ref, wft_ref, row_s):
        row_s[:, MAIN_W:] = jnp.zeros((tr, FC_PAD), F32)
        for j in range(N_DEV):
            row_s[:, j * shard:(j + 1) * shard] = g_ref[j].astype(F32)
        wm_ref[...] = row_s[:, :MAIN_W].astype(BF16)
        wf_ref[...] = row_s[:, MAIN_W:].astype(BF16)
        for j in range(0, MAIN_W, 512):
            wmt_ref[j:j + 512, :] = row_s[:, j:j + 512].T.astype(BF16)
        wft_ref[...] = row_s[:, MAIN_W:].T.astype(BF16)

    return _pc(
        body, name, grid=(d // tr,),
        in_specs=[pl.BlockSpec((N_DEV, tr, shard), lambda i: (0, i, 0))],
        out_specs=[pl.BlockSpec((tr, MAIN_W), lambda i: (i, 0)), pl.BlockSpec((tr, FC_PAD), lambda i: (i, 0)),
                   pl.BlockSpec((MAIN_W, tr), lambda i: (0, i)), pl.BlockSpec((FC_PAD, tr), lambda i: (0, i))],
        out_shape=[_sds((d, MAIN_W), BF16), _sds((d, FC_PAD), BF16), _sds((MAIN_W, d), BF16), _sds((FC_PAD, d), BF16)],
        scratch_shapes=[pltpu.VMEM((tr, wide), F32)],
        compiler_params=_params(),
    )(g_in)


def _layer_weights(l, g_in, g_out):
    d = g_in.shape[1]
    w_main, w_fc, w_main_t, w_fc_t = _assemble_w_in(g_in, f"assemble_w_in_{l}")
    full_out = g_out.reshape(N_DEV * g_out.shape[1], d)
    return w_main, w_fc, w_main_t, w_fc_t, full_out, full_out.T


def _layer_fwd(l, x, lbs, weights, lw, nb, rider=None):
    n = x.shape[0]
    t = n // nb
    w_main, w_fc, _, _, w_out, _ = lw
    bias = jnp.pad(weights["fox_f_bias"][l:l + 1], ((0, 0), (0, FC_PAD - FOX_HEADS)))
    wbd = _block_diag(weights["pool_w"][l]).astype(BF16)
    proj, fc, ht = _in_proj_fwd(x, weights["pre_norm_g"][l:l + 1], w_main, w_fc, f"in_proj_fwd_{l}")
    c_col = _fox_decay_fwd(fc, bias, nb, f"fox_decay_fwd_{l}")
    c_row = c_col.reshape(nb, t, FC_PAD)[:, :, :FOX_HEADS].transpose(0, 2, 1)
    o_h, s0 = _hgrn_fwd(proj, lbs[l:l + 1], _block_ones(HGRN_W, BF16), nb, f"hgrn_fwd_{l}")
    o_b = _pool_fwd(proj, wbd, weights["pool_scale"][l:l + 1], nb, f"pool_fwd_{l}")
    (o_c, lse), rode = _fox_fwd(proj, c_col, c_row, nb, f"fox_fwd_{l}", rider)
    x_next, mixt, y = _merge_fwd(x, proj, o_h, o_b, o_c, weights["hgrn_norm_g"][l:l + 1], w_out,
                                 weights["post_norm_g"][l:l + 1], f"merge_fwd_{l}")
    return x_next, (x, proj, fc, ht, c_col, c_row, o_h, s0, o_c, lse, mixt, y, bias, wbd), rode


def _layer_bwd(l, dx, saved, lbs, weights, lw, nb, rider=None):
    _, proj, fc, ht, c_col, c_row, o_h, s0, o_c, lse, mixt, y, bias, wbd = saved
    n = proj.shape[0]
    t = n // nb
    w_out_t = lw[5]
    g = {}
    dy, dmix, dgp = _merge_bwd(dx, y, weights["post_norm_g"][l:l + 1], w_out_t, f"merge_bwd_{l}")
    g["post_norm_g"] = dgp[0]
    g["w_out"] = _w_out_grad(mixt, dy, f"w_out_grad_{l}")
    d_a, dgh, dlb = _hgrn_bwd(dmix, proj, o_h, s0, weights["hgrn_norm_g"][l:l + 1], lbs[l:l + 1],
                              _block_ones(HGRN_W, BF16), nb, f"hgrn_bwd_{l}")
    g["hgrn_norm_g"], g["lbs"] = dgh[0], dlb[0]
    d_b, dwbd, dps = _pool_bwd(dmix, proj, wbd, wbd.T, weights["pool_scale"][l:l + 1], nb, f"pool_bwd_{l}")
    g["pool_w"] = jnp.stack([dwbd[j * HEAD:(j + 1) * HEAD, j * HEAD:(j + 1) * HEAD] for j in range(len(POOL_WINDOWS))])
    g["pool_scale"] = dps[0]
    da, d_gc, delta = _fox_gate_bwd(dmix, proj, o_c, f"fox_gate_bwd_{l}")
    delta_row, lse_row = [a.reshape(nb, t, FC_PAD)[:, :, :FOX_HEADS].transpose(0, 2, 1) for a in (delta, lse)]
    (d_qc, d_kc, d_vc, dc_k, dc_q), rode = _fox_bwd(proj, da, c_col, c_row, lse_row, delta_row, nb, f"fox_bwd_{l}", rider)
    dc_q = jnp.pad(dc_q.transpose(0, 2, 1).reshape(n, FOX_HEADS), ((0, 0), (0, FC_PAD - FOX_HEADS)))
    d_fc, dbias = _fox_decay_bwd(dc_q, dc_k, fc, bias, nb, f"fox_decay_bwd_{l}")
    g["fox_f_bias"] = dbias[0, :FOX_HEADS]
    pieces = [(d_a, C_QA), (d_b, C_UB), (d_qc, C_QC), (d_kc, C_KC), (d_vc, C_VC), (d_gc, C_GC), (d_fc, None)]
    g["w_in"] = _w_in_grad(ht, pieces, f"w_in_grad_{l}")
    return g, pieces, rode


def _layer_bwd_input(l, dx, pieces, saved, weights, lw, rider=None):
    (dxi, dgpre), rode = _in_proj_bwd(pieces, lw[2], lw[3], saved[0], weights["pre_norm_g"][l:l + 1], dx,
                                      f"in_proj_bwd_{l}", rider)
    return dxi, dgpre[0], rode


def kernel(x, lower_bounds, pre_norm_g, w_in, hgrn_norm_g, fox_f_bias, pool_w, pool_scale, w_out, post_norm_g, loss_target, m_lower_bounds, m_pre_norm_g, m_w_in, m_hgrn_norm_g, m_fox_f_bias, m_pool_w, m_pool_scale, m_w_out, m_post_norm_g, v_lower_bounds, v_pre_norm_g, v_w_in, v_hgrn_norm_g, v_fox_f_bias, v_pool_w, v_pool_scale, v_w_out, v_post_norm_g):
    weights = dict(lower_bounds=lower_bounds, pre_norm_g=pre_norm_g, hgrn_norm_g=hgrn_norm_g, fox_f_bias=fox_f_bias,
                   pool_w=pool_w, pool_scale=pool_scale, post_norm_g=post_norm_g)
    mom_m = dict(lower_bounds=m_lower_bounds, pre_norm_g=m_pre_norm_g, hgrn_norm_g=m_hgrn_norm_g, fox_f_bias=m_fox_f_bias,
                 pool_w=m_pool_w, pool_scale=m_pool_scale, post_norm_g=m_post_norm_g)
    mom_v = dict(lower_bounds=v_lower_bounds, pre_norm_g=v_pre_norm_g, hgrn_norm_g=v_hgrn_norm_g, fox_f_bias=v_fox_f_bias,
                 pool_w=v_pool_w, pool_scale=v_pool_scale, post_norm_g=v_post_norm_g)
    depth = w_in.shape[0]
    nb, t, d = x.shape
    n = nb * t
    core = lax.axis_index("c").astype(jnp.int32).reshape(1, 1)
    shards = [(w_in[l].astype(BF16), w_out[l].astype(BF16)) for l in range(depth)]
    lbs = _lower_bound_table(lower_bounds, "lower_bound_table")

    lw = [_layer_weights(0, *_gather_weights(*shards[0]))]
    xl, saved = x.reshape(n, d), []
    for l in range(depth):
        rider = _gather_rider(shards[l + 1]) if l + 1 < depth else None
        xl, sv, rode = _layer_fwd(l, xl, lbs, weights, lw[l], nb, rider)
        saved.append(sv)
        if rode is not None:
            lw.append(_layer_weights(l + 1, *rode))
    dx, sq = _loss_fwd_bwd(xl, loss_target.reshape(n, d), "loss")
    loss = lax.psum(0.5 * jnp.sum(sq) / d, AXES)

    grads, recv, pending = [None] * depth, [None] * depth, None
    for l in reversed(range(depth)):
        rider = None if pending is None else _chip_exchange_rider(pending)
        g, pieces, rode = _layer_bwd(l, dx, saved[l], lbs, weights, lw[l], nb, rider)
        if rode is not None:
            recv[l + 1] = rode
        halves = (g["w_in"], g["w_out"])
        if l > 0:
            dx, g["pre_norm_g"], other = _layer_bwd_input(l, dx, pieces, saved[l], weights, lw[l], _swap_rider(halves))
        else:
            other = _run_rider(_swap_rider(halves), f"grad_swap_{l}")
        pending = [_pair_add(hv, ot, core, f"grad_pair_add_{l}_{i}") for i, (hv, ot) in enumerate(zip(halves, other))]
        if l == 0:
            dx, g["pre_norm_g"], recv[l] = _layer_bwd_input(l, dx, pieces, saved[l], weights, lw[l],
                                                            _chip_exchange_rider(pending))
        grads[l] = g
    small = {k: jnp.stack([grads[l][k] for l in range(depth)]) for k in SMALL if k != "lower_bounds"}
    small["lower_bounds"] = _lower_bound_bwd(lower_bounds, jnp.stack([grads[l]["lbs"] for l in range(depth)]),
                                             "lower_bound_bwd")
    res_in, (r_small,) = _sum_adamw([recv[l][0] for l in range(depth)], w_in, m_w_in, v_w_in, "adamw_w_in",
                                    _gather_rider([_pack_small(small)]))
    res_out, _ = _sum_adamw([recv[l][1] for l in range(depth)], w_out, m_w_out, v_w_out, "adamw_w_out")
    res_small, _ = _sum_adamw([r_small], _pack_small(weights)[None], _pack_small(mom_m)[None], _pack_small(mom_v)[None],
                              "adamw_small")

    names = ("lower_bounds", "pre_norm_g", "w_in", "hgrn_norm_g", "fox_f_bias", "pool_w", "pool_scale", "w_out", "post_norm_g")
    outs = [loss, dx.reshape(nb, t, d)]
    for i in range(4):
        full = dict(_unpack_small(res_small[i][0], weights), w_in=res_in[i], w_out=res_out[i])
        outs += [full[k] for k in names]
    return tuple(outs)
```

```python
import functools

import jax
import jax.numpy as jnp
from jax import lax
from jax.experimental import pallas as pl
from jax.experimental.pallas import tpu as pltpu

F32, BF16 = jnp.float32, jnp.bfloat16
HI = lax.Precision.HIGHEST
MESH = pl.DeviceIdType.MESH
AXES = ("x", "y", "c")
N_DEV = 8

NORM_EPS = 1e-6
MASK_VALUE = -1e30
TINY = 1e-30
CHUNK = 64
SUB = 16
HGRN_W, POOL_W, FOX_W = 256, 256, 512
HEAD = 64
FOX_HEADS = 8
POOL_WINDOWS = (2, 4, 8, 16)
POOL_HALO = 16
MAIN_W = 3584
FC_PAD = 128
C_QA, C_FA, C_IA, C_GA, C_UB, C_GB, C_QC, C_KC, C_VC, C_GC = 0, 256, 512, 768, 1024, 1280, 1536, 2048, 2560, 3072
FOX_SCALE = HEAD ** -0.5

ADAM_LR, ADAM_B1, ADAM_B2, ADAM_EPS, ADAM_WD, ADAM_STEP = 0.001, 0.9, 0.999, 1e-08, 0.01, 10

VMEM_LIMIT = 56 * 1024 * 1024


def _pc(fn, name, **kw):
    return pl.pallas_call(fn, name=name, **kw)


def _params(**kw):
    return pltpu.CompilerParams(vmem_limit_bytes=VMEM_LIMIT, **kw)


class _Rider:
    def __init__(self, inputs, out_shapes, n_sems, n_local, plan):
        self.inputs, self.out_shapes, self.n_sems, self.n_local, self.plan = list(inputs), list(out_shapes), n_sems, n_local, plan

    def start(self, ins, outs, *sems):
        sends, _, locs = self.plan(ins, outs, *sems)
        for cp in locs + sends:
            cp.start()

    def wait(self, ins, outs, *sems):
        sends, recvs, locs = self.plan(ins, outs, *sems)
        for cp in recvs:
            cp.wait_recv()
        for cp in sends:
            cp.wait_send()
        for cp in locs:
            cp.wait()

    def sem_shapes(self):
        return [pltpu.SemaphoreType.DMA((self.n_sems,)), pltpu.SemaphoreType.DMA((self.n_sems,)),
                pltpu.SemaphoreType.DMA((self.n_local,))]


def _call(body, name, args, rider=None, *, grid, in_specs, out_specs, out_shape, scratch_shapes=(), **kw):
    if rider is None:
        res = _pc(body, name, grid=grid, in_specs=in_specs, out_specs=out_specs, out_shape=out_shape,
                  scratch_shapes=list(scratch_shapes), **kw)(*args)
        return res, None
    n_in, n_out, n_scr = len(in_specs), len(out_specs), len(scratch_shapes)
    n_rin, n_rout = len(rider.inputs), len(rider.out_shapes)

    def ridden(*refs):
        ins, refs = refs[:n_in], refs[n_in:]
        rins, refs = refs[:n_rin], refs[n_rin:]
        outs, refs = refs[:n_out], refs[n_out:]
        routs, refs = refs[:n_rout], refs[n_rout:]
        scr, sems = refs[:n_scr], refs[n_scr:]
        first = functools.reduce(jnp.logical_and, [pl.program_id(a) == 0 for a in range(len(grid))])
        last = functools.reduce(jnp.logical_and, [pl.program_id(a) == g - 1 for a, g in enumerate(grid)])

        @pl.when(first)
        def _():
            rider.start(rins, routs, *sems)

        body(*ins, *outs, *scr)

        @pl.when(last)
        def _():
            rider.wait(rins, routs, *sems)

    any_spec = pl.BlockSpec(memory_space=pl.ANY)
    res = _pc(ridden, name, grid=grid, in_specs=list(in_specs) + [any_spec] * n_rin,
              out_specs=list(out_specs) + [any_spec] * n_rout, out_shape=list(out_shape) + rider.out_shapes,
              scratch_shapes=list(scratch_shapes) + rider.sem_shapes(), **kw)(*args, *rider.inputs)
    return res[:n_out], res[n_out:]


def _run_rider(rider, name):
    n_rin = len(rider.inputs)

    def body(*refs):
        ins, outs, sems = refs[:n_rin], refs[n_rin:n_rin + len(rider.out_shapes)], refs[n_rin + len(rider.out_shapes):]
        rider.start(ins, outs, *sems)
        rider.wait(ins, outs, *sems)

    any_spec = pl.BlockSpec(memory_space=pl.ANY)
    return _pc(body, name, in_specs=[any_spec] * n_rin, out_specs=[any_spec] * len(rider.out_shapes),
               out_shape=rider.out_shapes, scratch_shapes=rider.sem_shapes())(*rider.inputs)


def _dot(a, b):
    return jnp.dot(a, b, preferred_element_type=F32)


def _dot_nt(a, b):
    return lax.dot_general(a, b, (((1,), (1,)), ((), ())), preferred_element_type=F32)


def _dot_tn(a, b):
    return lax.dot_general(a, b, (((0,), (0,)), ((), ())), preferred_element_type=F32)


def _dot_hi(a, b):
    return jnp.dot(a, b, precision=HI, preferred_element_type=F32)


def _sigmoid(x):
    return 1.0 / (1.0 + jnp.exp(-x))


def _block_ones(n, dtype):
    r = lax.broadcasted_iota(jnp.int32, (n, n), 0) // HEAD
    c = lax.broadcasted_iota(jnp.int32, (n, n), 1) // HEAD
    return (r == c).astype(dtype)


def _sds(shape, dtype):
    return jax.ShapeDtypeStruct(shape, dtype)


def _in_proj_fwd(x, g_pre, w_main, w_fc, name):
    n, d = x.shape
    tm = min(512, n)

    def body(x_ref, g_ref, w_ref, wf_ref, proj_ref, fc_ref, ht_ref):
        xv = x_ref[...]
        r = lax.rsqrt(jnp.mean(xv * xv, axis=-1, keepdims=True) + NORM_EPS)
        hf = xv * r * g_ref[...]
        hb = hf.astype(BF16)
        ht_ref[...] = hf.T.astype(BF16)
        for j in range(0, MAIN_W, 512):
            proj_ref[:, j:j + 512] = _dot(hb, w_ref[:, j:j + 512])
        fc_ref[...] = _dot(hb, wf_ref[...])

    return _pc(
        body, name, grid=(n // tm,),
        in_specs=[pl.BlockSpec((tm, d), lambda i: (i, 0)), pl.BlockSpec((1, d), lambda i: (0, 0)),
                  pl.BlockSpec((d, MAIN_W), lambda i: (0, 0)), pl.BlockSpec((d, FC_PAD), lambda i: (0, 0))],
        out_specs=[pl.BlockSpec((tm, MAIN_W), lambda i: (i, 0)), pl.BlockSpec((tm, FC_PAD), lambda i: (i, 0)),
                   pl.BlockSpec((d, tm), lambda i: (0, i))],
        out_shape=[_sds((n, MAIN_W), F32), _sds((n, FC_PAD), F32), _sds((d, n), BF16)],
        compiler_params=_params(),
    )(x, g_pre, w_main, w_fc)


def _fox_decay_fwd(fc, bias, nb, name):
    n = fc.shape[0]
    t = n // nb
    tt = min(256, t)
    nt = t // tt

    def body(fc_ref, b_ref, c_ref, carry):
        i = pl.program_id(1)

        @pl.when(i == 0)
        def _():
            carry[...] = jnp.zeros_like(carry)

        xv = fc_ref[...] + b_ref[...]
        lf = jnp.minimum(xv, 0.0) - jnp.log(1.0 + jnp.exp(-jnp.abs(xv)))
        r = lax.broadcasted_iota(jnp.int32, (tt, tt), 0)
        cc = lax.broadcasted_iota(jnp.int32, (tt, tt), 1)
        cs = _dot_hi((r >= cc).astype(F32), lf) + carry[...]
        c_ref[...] = cs
        carry[...] = cs[tt - 1:tt, :]

    return _pc(
        body, name, grid=(nb, nt),
        in_specs=[pl.BlockSpec((tt, FC_PAD), lambda b, i: (b * nt + i, 0)), pl.BlockSpec((1, FC_PAD), lambda b, i: (0, 0))],
        out_specs=pl.BlockSpec((tt, FC_PAD), lambda b, i: (b * nt + i, 0)),
        out_shape=_sds((n, FC_PAD), F32),
        scratch_shapes=[pltpu.VMEM((1, FC_PAD), F32)],
        compiler_params=_params(),
    )(fc, bias)


def _hgrn_gates(q, z, lb):
    sig = _sigmoid(z)
    sn = _sigmoid(-z)
    f = lb + (1.0 - lb) * sig
    g = jnp.log(jnp.maximum(f, TINY))
    k = (1.0 - lb) * sn
    sq = _sigmoid(q)
    return sig, sn, f, g, k, sq


def _sub_tri(n, lower):
    r = lax.broadcasted_iota(jnp.int32, (n, n), 0)
    c = lax.broadcasted_iota(jnp.int32, (n, n), 1)
    tri = (r >= c) if lower else (r <= c)
    return jnp.logical_and(r // SUB == c // SUB, tri).astype(F32)


def _hgrn_decays(qs, k, b):
    srow = lax.broadcasted_iota(jnp.int32, (SUB, HGRN_W), 0)
    es, ws = [], []
    for t in range(SUB):
        e = jnp.where(srow <= t, jnp.exp(b[t:t + 1, :] - b), 0.0)
        es.append(e)
        ws.append(e * (qs[t:t + 1, :] * k))
    return srow, es, ws


def _hgrn_state_step(st, k, v, b, bmask):
    bl = b[SUB - 1:SUB, :]
    ktil = k * jnp.exp(bl - b)
    return st * jnp.exp(bl) + _dot_tn(v.astype(BF16), ktil.astype(BF16)) * bmask


def _hgrn_sub_fwd(qs, k, v, b, st, ones_b, bmask):
    srow, _, ws = _hgrn_decays(qs, k, b)
    aexp = _dot(jnp.concatenate(ws, axis=0).astype(BF16), ones_b)
    o = _dot_nt((qs * jnp.exp(b)).astype(BF16), st.astype(BF16))
    for t in range(SUB):
        row = jnp.sum(aexp[t * SUB:(t + 1) * SUB, :] * v, axis=0, keepdims=True)
        o = o + jnp.where(srow == t, row, 0.0)
    return o, _hgrn_state_step(st, k, v, b, bmask)


def _hgrn_tile(t):
    return min(256, t)


def _hgrn_fwd(proj, lb, ones_b, nb, name):
    n = proj.shape[0]
    t = n // nb
    tt = _hgrn_tile(t)
    nt = t // tt
    ncs = tt // CHUNK
    w = HGRN_W

    def body(q_ref, z_ref, v_ref, lb_ref, ones_ref, o_ref, s0_ref, st_s, b_s, qs_s, k_s):
        @pl.when(pl.program_id(1) == 0)
        def _():
            st_s[...] = jnp.zeros_like(st_s)

        q = q_ref[...]
        _, _, _, g, k, sq = _hgrn_gates(q, z_ref[...], lb_ref[...])
        b_s[...] = _dot_hi(_sub_tri(tt, True), g)
        qs_s[...] = q * sq
        k_s[...] = k
        bmask = _block_ones(w, F32)
        ones_b = ones_ref[...]

        def chunk(c, carry):
            st = st_s[...]
            s0_ref[c] = st
            base = pl.multiple_of(c * CHUNK, CHUNK)
            for u in range(CHUNK // SUB):
                rows = pl.ds(base + u * SUB, SUB)
                o, st = _hgrn_sub_fwd(qs_s[rows, :], k_s[rows, :], v_ref[rows, :], b_s[rows, :], st, ones_b, bmask)
                o_ref[rows, :] = o
            st_s[...] = st
            return carry

        lax.fori_loop(0, ncs, chunk, 0)

    def col(j):
        return pl.BlockSpec((tt, w), lambda b, i: (b * nt + i, j))

    return _pc(
        body, name, grid=(nb, nt),
        in_specs=[col(C_QA // w), col(C_FA // w), col(C_IA // w), pl.BlockSpec((1, w), lambda b, i: (0, 0)),
                  pl.BlockSpec((w, w), lambda b, i: (0, 0))],
        out_specs=[pl.BlockSpec((tt, w), lambda b, i: (b * nt + i, 0)),
                   pl.BlockSpec((ncs, w, w), lambda b, i: (b * nt + i, 0, 0))],
        out_shape=[_sds((n, w), F32), _sds((n // CHUNK, w, w), F32)],
        scratch_shapes=[pltpu.VMEM((w, w), F32)] + [pltpu.VMEM((tt, w), F32)] * 3,
        compiler_params=_params(),
    )(proj, proj, proj, lb, ones_b)


def _pool_lane_windows():
    lane = lax.broadcasted_iota(jnp.int32, (1, POOL_W), 1) // HEAD
    wl = jnp.zeros((1, POOL_W), F32)
    for gi, win in enumerate(POOL_WINDOWS):
        wl = jnp.where(lane == gi, float(win), wl)
    return lane, wl


def _pool_select(lane, parts):
    out = parts[-1]
    for gi in range(len(parts) - 2, -1, -1):
        out = jnp.where(lane == gi, parts[gi], out)
    return out


def _pool_mix(u, halo, t0, tt):
    lane, wl = _pool_lane_windows()
    ext = jnp.concatenate([halo, u], axis=0)
    sums, cur, shift = [], ext, 1
    for _ in POOL_WINDOWS:
        cur = cur + pltpu.roll(cur, shift, axis=0)
        sums.append(cur[POOL_HALO:, :])
        shift *= 2
    tpos = (t0 + lax.broadcasted_iota(jnp.int32, (tt, POOL_W), 0)).astype(F32)
    cnt = jnp.minimum(tpos + 1.0, wl)
    return _pool_select(lane, sums) / cnt - u, cnt


def _pool_specs(tt, nt, nhb):
    cu, cg = C_UB // POOL_W, C_GB // POOL_W
    return [pl.BlockSpec((tt, POOL_W), lambda b, i: (b * nt + i, cu)),
            pl.BlockSpec((tt, POOL_W), lambda b, i: (b * nt + i, cg)),
            pl.BlockSpec((POOL_HALO, POOL_W), lambda b, i: (jnp.maximum((b * nt + i) * nhb - 1, 0), cu))]


def _pool_fwd(proj, wbd, scale, nb, name):
    n = proj.shape[0]
    t = n // nb
    tt = min(256, t)
    nt = t // tt
    nhb = tt // POOL_HALO

    def body(u_ref, g_ref, h_ref, w_ref, s_ref, o_ref):
        i = pl.program_id(1)
        halo = jnp.where(i == 0, 0.0, h_ref[...])
        pooled, _ = _pool_mix(u_ref[...], halo, i * tt, tt)
        gv = g_ref[...]
        o_ref[...] = _dot(pooled.astype(BF16), w_ref[...]) * s_ref[...] * (gv * _sigmoid(gv))

    return _pc(
        body, name, grid=(nb, nt),
        in_specs=_pool_specs(tt, nt, nhb) + [pl.BlockSpec((POOL_W, POOL_W), lambda b, i: (0, 0)),
                                             pl.BlockSpec((1, POOL_W), lambda b, i: (0, 0))],
        out_specs=pl.BlockSpec((tt, POOL_W), lambda b, i: (b * nt + i, 0)),
        out_shape=_sds((n, POOL_W), F32),
        compiler_params=_params(),
    )(proj, proj, proj, wbd, scale)


def _rows_reduce(x, op, final):
    while x.shape[0] > 8 and x.shape[0] % 16 == 0:
        half = x.shape[0] // 2
        x = op(x[:half], x[half:])
    return final(x, axis=0, keepdims=True)


def _lane_lo():
    return lax.broadcasted_iota(jnp.int32, (1, 2 * HEAD), 1) < HEAD


def _put_col(tile, hh, colv):
    lane = lax.broadcasted_iota(jnp.int32, tile.shape, 1)
    return jnp.where(lane == hh, colv, tile)


def _fox_fwd(proj, c_col, c_row, nb, name, rider=None):
    n = proj.shape[0]
    t = n // nb
    tb = min(256, t)
    nq = t // tb
    pw = 2 * HEAD

    def body(q_ref, k_ref, v_ref, cc_ref, cr_ref, o_ref, lse_ref, m_s, acc_s, cq_s):
        qi, kj = pl.program_id(1), pl.program_id(2)

        @pl.when(kj == 0)
        def _():
            m_s[...] = jnp.full_like(m_s, -jnp.inf)
            acc_s[...] = jnp.zeros_like(acc_s)
            for hh in range(FOX_HEADS):
                cq_s[hh] = jnp.broadcast_to(cc_ref[:, hh:hh + 1], (tb, pw))

        def block(masked):
            lo = _lane_lo()
            if masked:
                causal = lax.broadcasted_iota(jnp.int32, (tb, tb), 0) >= lax.broadcasted_iota(jnp.int32, (tb, tb), 1)
            for p in range(FOX_HEADS // 2):
                sl = slice(p * pw, (p + 1) * pw)
                qp = q_ref[:, sl] * FOX_SCALE
                kp = k_ref[:, sl].astype(BF16)
                vf = v_ref[:, sl]
                for h in range(2):
                    hh = 2 * p + h
                    lm = lo if h == 0 else jnp.logical_not(lo)
                    s = _dot_nt(jnp.where(lm, qp, 0.0).astype(BF16), kp)
                    s = s + (jnp.tile(cq_s[hh], (1, tb // pw)) - cr_ref[hh:hh + 1, :])
                    if masked:
                        s = jnp.where(causal, s, MASK_VALUE)
                    m_prev = m_s[hh]
                    m_new = jnp.maximum(m_prev, jnp.max(s, axis=1, keepdims=True))
                    alpha = jnp.exp(m_prev - m_new)
                    pe = jnp.exp(s - jnp.tile(m_new, (1, tb // pw)))
                    m_s[hh] = m_new
                    acc_s[hh] = alpha * acc_s[hh] + _dot(pe.astype(BF16), jnp.where(lm, vf, 1.0).astype(BF16))

        @pl.when(kj < qi)
        def _():
            block(False)

        @pl.when(kj == qi)
        def _():
            block(True)
            lo = _lane_lo()
            lse = jnp.zeros((tb, FC_PAD), F32)
            for p in range(FOX_HEADS // 2):
                halves = []
                for h in range(2):
                    hh = 2 * p + h
                    lm = lo if h == 0 else jnp.logical_not(lo)
                    acc = acc_s[hh]
                    swapped = pltpu.roll(acc, HEAD, axis=1)
                    halves.append(acc / swapped)
                    lse = _put_col(lse, hh, m_s[hh] + jnp.log(jnp.where(lm, swapped, acc)))
                o_ref[:, p * pw:(p + 1) * pw] = jnp.where(lo, halves[0], halves[1])
            lse_ref[...] = lse

    def qspec(wd, j):
        return pl.BlockSpec((tb, wd), lambda b, qi, kj: (b * nq + qi, j))

    def kspec(j):
        return pl.BlockSpec((tb, FOX_W), lambda b, qi, kj: (b * nq + jnp.minimum(kj, qi), j))

    return _call(
        body, name, (proj, proj, proj, c_col, c_row), rider, grid=(nb, nq, nq),
        in_specs=[qspec(FOX_W, C_QC // FOX_W), kspec(C_KC // FOX_W), kspec(C_VC // FOX_W), qspec(FC_PAD, 0),
                  pl.BlockSpec((None, FOX_HEADS, tb), lambda b, qi, kj: (b, 0, jnp.minimum(kj, qi)))],
        out_specs=[qspec(FOX_W, 0), qspec(FC_PAD, 0)],
        out_shape=[_sds((n, FOX_W), F32), _sds((n, FC_PAD), F32)],
        scratch_shapes=[pltpu.VMEM((FOX_HEADS, tb, pw), F32), pltpu.VMEM((FOX_HEADS, tb, pw), F32),
                        pltpu.VMEM((FOX_HEADS, tb, pw), F32)],
        compiler_params=_params(),
    )


def _head_mean(x, ones_f):
    return _dot_hi(x, ones_f) * (1.0 / HEAD)


def _merge_fwd(x, proj, o_h, o_b, o_c, gh, w_out, g_post, name):
    n, d = x.shape
    tm = min(512, n)

    def body(x_ref, ga_ref, gc_ref, oh_ref, ob_ref, oc_ref, gh_ref, w_ref, gp_ref, xo_ref, mixt_ref, y_ref):
        oh = oh_ref[...]
        ones_f = _block_ones(HGRN_W, F32)
        na = oh * lax.rsqrt(_head_mean(oh * oh, ones_f) + NORM_EPS) * gh_ref[...]
        ga, gc = ga_ref[...], gc_ref[...]
        mixed = jnp.concatenate([na * (ga * _sigmoid(ga)), ob_ref[...], oc_ref[...] * (gc * _sigmoid(gc))], axis=1)
        mixt_ref[...] = mixed.T.astype(BF16)
        y = _dot(mixed.astype(BF16), w_ref[...])
        y_ref[...] = y
        xo_ref[...] = x_ref[...] + y * lax.rsqrt(jnp.mean(y * y, axis=-1, keepdims=True) + NORM_EPS) * gp_ref[...]

    def row(wd, j=0):
        return pl.BlockSpec((tm, wd), lambda i: (i, j))

    def full(a, b):
        return pl.BlockSpec((a, b), lambda i: (0, 0))

    return _pc(
        body, name, grid=(n // tm,),
        in_specs=[row(d), row(HGRN_W, C_GA // HGRN_W), row(FOX_W, C_GC // FOX_W), row(HGRN_W), row(POOL_W), row(FOX_W),
                  full(1, HGRN_W), full(d, d), full(1, d)],
        out_specs=[row(d), pl.BlockSpec((d, tm), lambda i: (0, i)), row(d)],
        out_shape=[_sds((n, d), F32), _sds((d, n), BF16), _sds((n, d), F32)],
        compiler_params=_params(),
    )(x, proj, proj, o_h, o_b, o_c, gh, w_out, g_post)


def _loss_fwd_bwd(x, target, name):
    n, d = x.shape
    tm = min(512, n)

    def body(x_ref, t_ref, dx_ref, sq_ref):
        @pl.when(pl.program_id(0) == 0)
        def _():
            sq_ref[...] = jnp.zeros_like(sq_ref)

        e = x_ref[...] - t_ref[...]
        dx_ref[...] = e * (1.0 / d)
        sq_ref[...] += jnp.sum(e * e, axis=0, keepdims=True)

    return _pc(
        body, name, grid=(n // tm,),
        in_specs=[pl.BlockSpec((tm, d), lambda i: (i, 0))] * 2,
        out_specs=[pl.BlockSpec((tm, d), lambda i: (i, 0)), pl.BlockSpec((1, d), lambda i: (0, 0))],
        out_shape=[_sds((n, d), F32), _sds((1, d), F32)],
        compiler_params=_params(),
    )(x, target)


def _rms_bwd(dy_scaled, xhat, r):
    return r * (dy_scaled - xhat * jnp.mean(dy_scaled * xhat, axis=-1, keepdims=True))


def _merge_bwd(dxo, y, g_post, w_out_t, name):
    n, d = y.shape
    tm = min(512, n)

    def body(dx_ref, y_ref, gp_ref, wt_ref, dy_ref, dm_ref, dgp_ref):
        @pl.when(pl.program_id(0) == 0)
        def _():
            dgp_ref[...] = jnp.zeros_like(dgp_ref)

        yv, dxv = y_ref[...], dx_ref[...]
        r = lax.rsqrt(jnp.mean(yv * yv, axis=-1, keepdims=True) + NORM_EPS)
        yh = yv * r
        dgp_ref[...] += jnp.sum(dxv * yh, axis=0, keepdims=True)
        dyb = _rms_bwd(dxv * gp_ref[...], yh, r).astype(BF16)
        dy_ref[...] = dyb
        dm_ref[...] = _dot(dyb, wt_ref[...])

    row = pl.BlockSpec((tm, d), lambda i: (i, 0))
    return _pc(
        body, name, grid=(n // tm,),
        in_specs=[row, row, pl.BlockSpec((1, d), lambda i: (0, 0)), pl.BlockSpec((d, d), lambda i: (0, 0))],
        out_specs=[row, row, pl.BlockSpec((1, d), lambda i: (0, 0))],
        out_shape=[_sds((n, d), BF16), _sds((n, d), F32), _sds((1, d), F32)],
        compiler_params=_params(),
    )(dxo, y, g_post, w_out_t)


def _w_out_grad(mixt, dy, name):
    d, n = mixt.shape
    rows = d // N_DEV

    def body(a_ref, b_ref, o_ref):
        o_ref[...] = _dot(a_ref[...], b_ref[...]).astype(BF16)

    return _pc(
        body, name, grid=(N_DEV,),
        in_specs=[pl.BlockSpec((rows, n), lambda j: (j, 0)), pl.BlockSpec((n, d), lambda j: (0, 0))],
        out_specs=pl.BlockSpec((None, None, rows, d), lambda j: (j % 2, j // 2, 0, 0)),
        out_shape=_sds((2, N_DEV // 2, rows, d), BF16),
        compiler_params=_params(),
    )(mixt, dy)


def _w_in_grad(ht, pieces, name):
    d, n = ht.shape
    ta, tk = min(512, d), min(512, n)
    nk = n // tk
    arrays = [p for p, _ in pieces]
    widths = [p.shape[1] for p in arrays]
    offs = [sum(widths[:i]) for i in range(len(widths))]
    in_w = MAIN_W + FOX_HEADS
    shard = in_w // N_DEV

    def body(*refs):
        a_ref, p_refs = refs[0], refs[1:1 + len(arrays)]
        o_ref, acc = refs[1 + len(arrays):]
        k = pl.program_id(1)

        @pl.when(k == 0)
        def _():
            acc[...] = jnp.zeros_like(acc)

        a = a_ref[...]
        for pr, off, wd in zip(p_refs, offs, widths):
            for j in range(0, wd, 512):
                jw = min(512, wd - j)
                acc[:, off + j:off + j + jw] += _dot(a, pr[:, j:j + jw])

        @pl.when(k == nk - 1)
        def _():
            for j in range(N_DEV):
                o_ref[j % 2, j // 2] = acc[:, j * shard:(j + 1) * shard].astype(BF16)

    return _pc(
        body, name, grid=(d // ta, nk),
        in_specs=[pl.BlockSpec((ta, tk), lambda i, k: (i, k))] + [pl.BlockSpec((tk, wd), lambda i, k: (k, 0)) for wd in widths],
        out_specs=pl.BlockSpec((2, N_DEV // 2, ta, shard), lambda i, k: (0, 0, i, 0)),
        out_shape=_sds((2, N_DEV // 2, d, shard), BF16),
        scratch_shapes=[pltpu.VMEM((ta, sum(widths)), F32)],
        compiler_params=_params(),
    )(ht, *arrays)


def _hgrn_sub_bwd(qs, k, v, b, do, s0, ds1, ones_b, bmask):
    bl = b[SUB - 1:SUB, :]
    eb, ebl, ekt = jnp.exp(b), jnp.exp(bl), jnp.exp(bl - b)
    qe, ktil = qs * eb, k * ekt
    ds1b, dob = ds1.astype(BF16), do.astype(BF16)
    dv = _dot_nt(ktil.astype(BF16), ds1b)
    dqe = _dot(dob, s0.astype(BF16))
    dktil = _dot(v.astype(BF16), ds1b)
    dbl = jnp.sum(dktil * ktil, axis=0, keepdims=True) + ebl * jnp.sum(s0 * ds1, axis=0, keepdims=True)
    ds0 = ds1 * ebl + _dot_tn(dob, qe.astype(BF16)) * bmask
    srow, es, ws = _hgrn_decays(qs, k, b)
    aexp = _dot(jnp.concatenate(ws, axis=0).astype(BF16), ones_b)
    gexp = _dot(jnp.concatenate([do[t:t + 1, :] * v for t in range(SUB)], axis=0).astype(BF16), ones_b)
    dq = dqe * eb
    dk = dktil * ekt
    for t in range(SUB):
        sl = slice(t * SUB, (t + 1) * SUB)
        ge = gexp[sl, :] * es[t]
        dq = dq + jnp.where(srow == t, jnp.sum(ge * k, axis=0, keepdims=True), 0.0)
        dk = dk + ge * qs[t:t + 1, :]
        dv = dv + aexp[sl, :] * do[t:t + 1, :]
    return dq, dk, dv, dbl, ds0


def _hgrn_bwd(dmix, proj, o_h, s0, gh, lb, ones_b, nb, name):
    n = proj.shape[0]
    t = n // nb
    tt = _hgrn_tile(t)
    nt = t // tt
    ncs = tt // CHUNK
    nsub = CHUNK // SUB
    w = HGRN_W

    def body(dm_ref, q_ref, z_ref, v_ref, ga_ref, oh_ref, s0_ref, gh_ref, lb_ref, ones_ref,
             dp_ref, dgh_ref, dlb_ref, ds_s, ss_s, b_s, qs_s, k_s, do_s, dq_s, dk_s, dv_s, dbl_s):
        first = jnp.logical_and(pl.program_id(0) == 0, pl.program_id(1) == 0)

        @pl.when(first)
        def _():
            dgh_ref[...] = jnp.zeros_like(dgh_ref)
            dlb_ref[...] = jnp.zeros_like(dlb_ref)

        @pl.when(pl.program_id(1) == 0)
        def _():
            ds_s[...] = jnp.zeros_like(ds_s)

        ones_b = ones_ref[...]
        ones_f = ones_b.astype(F32)
        bmask = _block_ones(w, F32)
        lbv, ghv = lb_ref[...], gh_ref[...]
        oh, ga, dm = oh_ref[...], ga_ref[...], dm_ref[...]
        rn = lax.rsqrt(_head_mean(oh * oh, ones_f) + NORM_EPS)
        nh = oh * rn
        sga = _sigmoid(ga)
        dp_ref[:, 3 * w:4 * w] = (dm * nh * ghv * (sga * (1.0 + ga * (1.0 - sga)))).astype(BF16)
        dn = dm * (ga * sga)
        dgh_ref[...] += jnp.sum(dn * nh, axis=0, keepdims=True)
        dn = dn * ghv
        do_s[...] = rn * (dn - nh * _head_mean(dn * nh, ones_f))
        q = q_ref[...]
        sig, sn, f, g, k, sq = _hgrn_gates(q, z_ref[...], lbv)
        qs = q * sq
        b_s[...] = _dot_hi(_sub_tri(tt, True), g)
        qs_s[...] = qs
        k_s[...] = k

        def chunk(cc, carry):
            c = ncs - 1 - cc
            base = pl.multiple_of(c * CHUNK, CHUNK)
            st = s0_ref[c]
            for u in range(nsub):
                ss_s[u] = st
                if u < nsub - 1:
                    rows = pl.ds(base + u * SUB, SUB)
                    st = _hgrn_state_step(st, k_s[rows, :], v_ref[rows, :], b_s[rows, :], bmask)
            ds = ds_s[...]
            for u in reversed(range(nsub)):
                rows = pl.ds(base + u * SUB, SUB)
                dq, dk, dv, dbl, ds = _hgrn_sub_bwd(qs_s[rows, :], k_s[rows, :], v_ref[rows, :], b_s[rows, :],
                                                    do_s[rows, :], ss_s[u], ds, ones_b, bmask)
                dq_s[rows, :] = dq
                dk_s[rows, :] = dk
                dv_s[rows, :] = dv
                dbl_s[rows, :] = jnp.broadcast_to(dbl, (SUB, w))
            ds_s[...] = ds
            return carry

        lax.fori_loop(0, ncs, chunk, 0)
        dqs, dk = dq_s[...], dk_s[...]
        dg = _dot_hi(_sub_tri(tt, False), qs * dqs - k * dk) + dbl_s[...]
        dfz = jnp.where(f > TINY, dg / jnp.maximum(f, TINY), 0.0)
        dlb_ref[...] += jnp.sum(dfz * (1.0 - sig) - dk * sn, axis=0, keepdims=True)
        dp_ref[:, 0:w] = (dqs * (sq * (1.0 + q * (1.0 - sq)))).astype(BF16)
        dp_ref[:, w:2 * w] = ((dfz - dk) * (1.0 - lbv) * sig * sn).astype(BF16)
        dp_ref[:, 2 * w:3 * w] = dv_s[...].astype(BF16)

    def rv(b, i):
        return b * nt + (nt - 1 - i)

    def col(j):
        return pl.BlockSpec((tt, w), lambda b, i: (rv(b, i), j))

    def full(a, bb):
        return pl.BlockSpec((a, bb), lambda b, i: (0, 0))

    return _pc(
        body, name, grid=(nb, nt),
        in_specs=[col(0), col(C_QA // w), col(C_FA // w), col(C_IA // w), col(C_GA // w), col(0),
                  pl.BlockSpec((ncs, w, w), lambda b, i: (rv(b, i), 0, 0)), full(1, w), full(1, w), full(w, w)],
        out_specs=[pl.BlockSpec((tt, 4 * w), lambda b, i: (rv(b, i), 0)), full(1, w), full(1, w)],
        out_shape=[_sds((n, 4 * w), BF16), _sds((1, w), F32), _sds((1, w), F32)],
        scratch_shapes=[pltpu.VMEM((w, w), F32), pltpu.VMEM((nsub, w, w), F32)] + [pltpu.VMEM((tt, w), F32)] * 8,
        compiler_params=_params(),
    )(dmix, proj, proj, proj, proj, o_h, s0, gh, lb, ones_b)


def _pool_bwd(dmix, proj, wbd, wbd_t, scale, nb, name):
    n = proj.shape[0]
    t = n // nb
    tt = min(256, t)
    nt = t // tt
    nhb = tt // POOL_HALO
    cu, cg, cm = C_UB // POOL_W, C_GB // POOL_W, HGRN_W // POOL_W

    def body(u_ref, g_ref, h_ref, dm_ref, gn_ref, dmn_ref, w_ref, wt_ref, s_ref, dp_ref, dw_ref, ds_ref):
        i = pl.program_id(1)
        first = jnp.logical_and(pl.program_id(0) == 0, i == 0)

        @pl.when(first)
        def _():
            dw_ref[...] = jnp.zeros_like(dw_ref)
            ds_ref[...] = jnp.zeros_like(ds_ref)

        sc = s_ref[...]
        halo = jnp.where(i == 0, 0.0, h_ref[...])
        pooled, cnt = _pool_mix(u_ref[...], halo, i * tt, tt)
        pb = pooled.astype(BF16)
        pre = _dot(pb, w_ref[...])
        gv, dm = g_ref[...], dm_ref[...]
        sg = _sigmoid(gv)
        silu = gv * sg
        dgb = dm * pre * sc * (sg * (1.0 + gv * (1.0 - sg)))
        ds_ref[...] += jnp.sum(dm * pre * silu, axis=0, keepdims=True)
        dpre = (dm * sc * silu).astype(BF16)
        dw_ref[...] += _dot_tn(pb, dpre)
        dpool = _dot(dpre, wt_ref[...])
        gn = gn_ref[...]
        dpre_n = (dmn_ref[...] * sc * (gn * _sigmoid(gn))).astype(BF16)
        dpool_n = jnp.where(i == nt - 1, 0.0, _dot(dpre_n, wt_ref[...]))
        lane, wl = _pool_lane_windows()
        tpos_n = ((i + 1) * tt + lax.broadcasted_iota(jnp.int32, (POOL_HALO, POOL_W), 0)).astype(F32)
        ext = jnp.concatenate([dpool / cnt, dpool_n / jnp.minimum(tpos_n + 1.0, wl)], axis=0)
        rows = tt + POOL_HALO
        sums, cur, shift = [], ext, 1
        for _ in POOL_WINDOWS:
            cur = cur + pltpu.roll(cur, rows - shift, axis=0)
            sums.append(cur[:tt, :])
            shift *= 2
        du = _pool_select(lane, sums) - dpool
        dp_ref[...] = jnp.concatenate([du, dgb], axis=1).astype(BF16)

    def nxt(b, i):
        return jnp.minimum((b * nt + i + 1) * nhb, n // POOL_HALO - 1)

    return _pc(
        body, name, grid=(nb, nt),
        in_specs=_pool_specs(tt, nt, nhb) + [
            pl.BlockSpec((tt, POOL_W), lambda b, i: (b * nt + i, cm)),
            pl.BlockSpec((POOL_HALO, POOL_W), lambda b, i: (nxt(b, i), cg)),
            pl.BlockSpec((POOL_HALO, POOL_W), lambda b, i: (nxt(b, i), cm)),
            pl.BlockSpec((POOL_W, POOL_W), lambda b, i: (0, 0)), pl.BlockSpec((POOL_W, POOL_W), lambda b, i: (0, 0)),
            pl.BlockSpec((1, POOL_W), lambda b, i: (0, 0))],
        out_specs=[pl.BlockSpec((tt, 2 * POOL_W), lambda b, i: (b * nt + i, 0)),
                   pl.BlockSpec((POOL_W, POOL_W), lambda b, i: (0, 0)), pl.BlockSpec((1, POOL_W), lambda b, i: (0, 0))],
        out_shape=[_sds((n, 2 * POOL_W), BF16), _sds((POOL_W, POOL_W), F32), _sds((1, POOL_W), F32)],
        compiler_params=_params(),
    )(proj, proj, proj, dmix, proj, dmix, wbd, wbd_t, scale)


def _fox_gate_bwd(dmix, proj, o_c, name):
    n = proj.shape[0]
    tm = min(512, n)

    def body(dm_ref, gc_ref, oc_ref, da_ref, dg_ref, dl_ref):
        dm, gc, oc = dm_ref[...], gc_ref[...], oc_ref[...]
        sg = _sigmoid(gc)
        da = dm * (gc * sg)
        da_ref[...] = da.astype(BF16)
        dg_ref[...] = (dm * oc * (sg * (1.0 + gc * (1.0 - sg)))).astype(BF16)
        r = lax.broadcasted_iota(jnp.int32, (FOX_W, FC_PAD), 0) // HEAD
        c = lax.broadcasted_iota(jnp.int32, (FOX_W, FC_PAD), 1)
        dl_ref[...] = _dot_hi(da * oc, (r == c).astype(F32))

    def row(wd, j=0):
        return pl.BlockSpec((tm, wd), lambda i: (i, j))

    return _pc(
        body, name, grid=(n // tm,),
        in_specs=[row(FOX_W, (HGRN_W + POOL_W) // FOX_W), row(FOX_W, C_GC // FOX_W), row(FOX_W)],
        out_specs=[row(FOX_W), row(FOX_W), row(FC_PAD)],
        out_shape=[_sds((n, FOX_W), BF16), _sds((n, FOX_W), BF16), _sds((n, FC_PAD), F32)],
        compiler_params=_params(),
    )(dmix, proj, o_c)


def _fox_bwd(proj, da, c_col, c_row, lse_row, delta_row, nb, name, rider=None):
    n = proj.shape[0]
    t = n // nb
    tb = min(256, t)
    nq = t // tb
    pw = 2 * HEAD

    def body(q_ref, k_ref, v_ref, da_ref, cc_ref, cr_ref, lse_ref, dl_ref,
             dq_ref, dk_ref, dv_ref, dck_ref, dcq_ref, dq_s, dk_s, dv_s, dck_s, dcq_s):
        kj, qi = pl.program_id(1), pl.program_id(2)

        @pl.when(jnp.logical_and(kj == 0, qi == 0))
        def _():
            dq_s[...] = jnp.zeros_like(dq_s)
            dcq_s[...] = jnp.zeros_like(dcq_s)

        @pl.when(qi == 0)
        def _():
            dk_s[...] = jnp.zeros_like(dk_s)
            dv_s[...] = jnp.zeros_like(dv_s)
            dck_s[...] = jnp.zeros_like(dck_s)

        def block(masked):
            lo = _lane_lo()
            if masked:
                causal = lax.broadcasted_iota(jnp.int32, (tb, tb), 1) >= lax.broadcasted_iota(jnp.int32, (tb, tb), 0)
            dck = dck_s[...]
            for p in range(FOX_HEADS // 2):
                sl = slice(p * pw, (p + 1) * pw)
                qp = q_ref[:, sl] * FOX_SCALE
                kf = k_ref[:, sl]
                kp = kf.astype(BF16)
                kt = kf.T.astype(BF16)
                vp = v_ref[:, sl].astype(BF16)
                dap = da_ref[:, sl]
                dk, dv = dk_s[:, sl], dv_s[:, sl]
                for h in range(2):
                    hh = 2 * p + h
                    lm = lo if h == 0 else jnp.logical_not(lo)
                    qm = jnp.where(lm, qp, 0.0).astype(BF16)
                    dam = jnp.where(lm, dap, jnp.zeros_like(dap))
                    s = _dot_nt(kp, qm) + (cr_ref[hh:hh + 1, :] - cc_ref[:, hh:hh + 1])
                    pe = jnp.exp(s - lse_ref[hh:hh + 1, :])
                    if masked:
                        pe = jnp.where(causal, pe, 0.0)
                    dp = _dot_nt(vp, dam)
                    ds = pe * (dp - dl_ref[hh:hh + 1, :])
                    dsb = ds.astype(BF16)
                    dv = dv + _dot(pe.astype(BF16), dam)
                    dk = dk + _dot(dsb, qm)
                    rows = slice(hh * HEAD, (hh + 1) * HEAD)
                    dq_s[qi, rows, :] += _dot(kt[h * HEAD:(h + 1) * HEAD, :], dsb)
                    dck = dck - _put_col(jnp.zeros_like(dck), hh, jnp.sum(ds, axis=1, keepdims=True))
                    dcq_s[qi, hh:hh + 1, :] += _rows_reduce(ds, jnp.add, jnp.sum)
                dk_s[:, sl] = dk
                dv_s[:, sl] = dv
            dck_s[...] = dck

        @pl.when(qi > kj)
        def _():
            block(False)

        @pl.when(qi == kj)
        def _():
            block(True)

        @pl.when(qi == nq - 1)
        def _():
            dk_ref[...] = dk_s[...].astype(BF16)
            dv_ref[...] = dv_s[...].astype(BF16)
            dck_ref[...] = dck_s[...]

        @pl.when(jnp.logical_and(kj == nq - 1, qi == nq - 1))
        def _():
            for j in range(nq):
                dq_ref[j * tb:(j + 1) * tb, :] = (dq_s[j].T * FOX_SCALE).astype(BF16)
                dcq_ref[:, j * tb:(j + 1) * tb] = dcq_s[j]

    def kspec(wd, j=0):
        return pl.BlockSpec((tb, wd), lambda b, kj, qi: (b * nq + kj, j))

    def qspec(wd, j=0):
        return pl.BlockSpec((tb, wd), lambda b, kj, qi: (b * nq + jnp.maximum(qi, kj), j))

    def qrow():
        return pl.BlockSpec((None, FOX_HEADS, tb), lambda b, kj, qi: (b, 0, jnp.maximum(qi, kj)))

    return _call(
        body, name, (proj, proj, proj, da, c_col, c_row, lse_row, delta_row), rider, grid=(nb, nq, nq),
        in_specs=[qspec(FOX_W, C_QC // FOX_W), kspec(FOX_W, C_KC // FOX_W), kspec(FOX_W, C_VC // FOX_W), qspec(FOX_W),
                  kspec(FC_PAD), qrow(), qrow(), qrow()],
        out_specs=[pl.BlockSpec((t, FOX_W), lambda b, kj, qi: (b, 0)), kspec(FOX_W), kspec(FOX_W), kspec(FC_PAD),
                   pl.BlockSpec((None, FOX_HEADS, t), lambda b, kj, qi: (b, 0, 0))],
        out_shape=[_sds((n, FOX_W), BF16), _sds((n, FOX_W), BF16), _sds((n, FOX_W), BF16), _sds((n, FC_PAD), F32),
                   _sds((nb, FOX_HEADS, t), F32)],
        scratch_shapes=[pltpu.VMEM((nq, FOX_W, tb), F32), pltpu.VMEM((tb, FOX_W), F32), pltpu.VMEM((tb, FOX_W), F32),
                        pltpu.VMEM((tb, FC_PAD), F32), pltpu.VMEM((nq, FOX_HEADS, tb), F32)],
        compiler_params=_params(),
    )


def _fox_decay_bwd(dc_q, dc_k, fc, bias, nb, name):
    n = fc.shape[0]
    t = n // nb
    tt = min(256, t)
    nt = t // tt

    def body(dcq_ref, dck_ref, fc_ref, b_ref, dfc_ref, db_ref, carry):
        i = pl.program_id(1)
        first = jnp.logical_and(pl.program_id(0) == 0, i == 0)

        @pl.when(first)
        def _():
            db_ref[...] = jnp.zeros_like(db_ref)

        @pl.when(i == 0)
        def _():
            carry[...] = jnp.zeros_like(carry)

        r = lax.broadcasted_iota(jnp.int32, (tt, tt), 0)
        cc = lax.broadcasted_iota(jnp.int32, (tt, tt), 1)
        dlf = _dot_hi((r <= cc).astype(F32), dcq_ref[...] + dck_ref[...]) + carry[...]
        carry[...] = dlf[0:1, :]
        dfc = dlf * _sigmoid(-(fc_ref[...] + b_ref[...]))
        dfc_ref[...] = dfc.astype(BF16)
        db_ref[...] += jnp.sum(dfc, axis=0, keepdims=True)

    def row():
        return pl.BlockSpec((tt, FC_PAD), lambda b, i: (b * nt + (nt - 1 - i), 0))

    return _pc(
        body, name, grid=(nb, nt),
        in_specs=[row(), row(), row(), pl.BlockSpec((1, FC_PAD), lambda b, i: (0, 0))],
        out_specs=[row(), pl.BlockSpec((1, FC_PAD), lambda b, i: (0, 0))],
        out_shape=[_sds((n, FC_PAD), BF16), _sds((1, FC_PAD), F32)],
        scratch_shapes=[pltpu.VMEM((1, FC_PAD), F32)],
        compiler_params=_params(),
    )(dc_q, dc_k, fc, bias)


def _in_proj_bwd(pieces, w_main_t, w_fc_t, x, g_pre, dxo, name, rider=None):
    n, d = x.shape
    tm = min(512, n)
    widths = [p.shape[1] for p, _ in pieces]
    offs = [o for _, o in pieces]
    np_ = len(pieces)

    def body(*refs):
        p_refs = refs[:np_]
        wt_ref, wf_ref, x_ref, g_ref, dxo_ref, dx_ref, dg_ref = refs[np_:]

        @pl.when(pl.program_id(0) == 0)
        def _():
            dg_ref[...] = jnp.zeros_like(dg_ref)

        dh = _dot(p_refs[-1][...], wf_ref[...])
        for pr, wd, off in zip(p_refs[:-1], widths[:-1], offs[:-1]):
            for j in range(0, wd, 512):
                jw = min(512, wd - j)
                dh = dh + _dot(pr[:, j:j + jw], wt_ref[off + j:off + j + jw, :])
        xv = x_ref[...]
        r = lax.rsqrt(jnp.mean(xv * xv, axis=-1, keepdims=True) + NORM_EPS)
        xh = xv * r
        dg_ref[...] += jnp.sum(dh * xh, axis=0, keepdims=True)
        dx_ref[...] = dxo_ref[...] + _rms_bwd(dh * g_ref[...], xh, r)

    row = pl.BlockSpec((tm, d), lambda i: (i, 0))
    return _call(
        body, name, (*[p for p, _ in pieces], w_main_t, w_fc_t, x, g_pre, dxo), rider, grid=(n // tm,),
        in_specs=[pl.BlockSpec((tm, wd), lambda i: (i, 0)) for wd in widths] + [
            pl.BlockSpec((MAIN_W, d), lambda i: (0, 0)), pl.BlockSpec((FC_PAD, d), lambda i: (0, 0)),
            row, pl.BlockSpec((1, d), lambda i: (0, 0)), row],
        out_specs=[row, pl.BlockSpec((1, d), lambda i: (0, 0))],
        out_shape=[_sds((n, d), F32), _sds((1, d), F32)],
        compiler_params=_params(),
    )


def _lower_bound_table(lower_bounds, name):
    depth, w = lower_bounds.shape

    def body(lb_ref, o_ref):
        v = lb_ref[...]
        e = jnp.exp(v - jnp.max(v, axis=0, keepdims=True))
        p = e / jnp.sum(e, axis=0, keepdims=True)
        acc = jnp.zeros((1, w), F32)
        for l in range(depth):
            acc = acc + p[l:l + 1, :]
            o_ref[l:l + 1, :] = acc - p[0:1, :]

    return _pc(body, name, out_shape=_sds((depth, w), F32))(lower_bounds)


def _lower_bound_bwd(lower_bounds, dlbs, name):
    depth, w = lower_bounds.shape

    def body(lb_ref, d_ref, o_ref):
        v, dl = lb_ref[...], d_ref[...]
        e = jnp.exp(v - jnp.max(v, axis=0, keepdims=True))
        p = e / jnp.sum(e, axis=0, keepdims=True)
        tot = jnp.sum(dl, axis=0, keepdims=True)
        rows, tail = [], tot
        for l in range(depth):
            rows.append(tail - tot if l == 0 else tail)
            tail = tail - dl[l:l + 1, :]
        dp = jnp.concatenate(rows, axis=0)
        o_ref[...] = p * (dp - jnp.sum(p * dp, axis=0, keepdims=True))

    return _pc(body, name, out_shape=_sds((depth, w), F32))(lower_bounds, dlbs)


def _place():
    x, y, c = lax.axis_index("x"), lax.axis_index("y"), lax.axis_index("c")
    return x, y, c


def _gather_weights(w_in_b, w_out_b):
    arrays = (w_in_b, w_out_b)
    na = len(arrays)

    def body(*refs):
        ins, outs = refs[:na], refs[na:2 * na]
        send_sems, recv_sems, local_sems = refs[2 * na:]
        x, y, c = _place()
        me, sibling = (x, y, c), (x, y, 1 - c)
        chips = [(1 - x, y), (x, 1 - y), (1 - x, 1 - y)]

        def slot(a, px, py, pc):
            return outs[a].at[4 * px + 2 * py + pc]

        def copy(a, k, block, to, own=False):
            return pltpu.make_async_remote_copy(
                src_ref=ins[a] if own else slot(a, *block), dst_ref=slot(a, *block),
                send_sem=send_sems.at[a * 7 + k], recv_sem=recv_sems.at[a * 7 + k],
                device_id=to, device_id_type=MESH)

        mine = [pltpu.make_async_copy(ins[a], slot(a, *me), local_sems.at[a]) for a in range(na)]
        for cp in mine:
            cp.start()
        first = []
        for a in range(na):
            first.append(copy(a, 0, me, sibling, own=True))
            first += [copy(a, 1 + j, me, (*chip, c), own=True) for j, chip in enumerate(chips)]
        for cp in first:
            cp.start()
        passed = []
        for j, chip in enumerate(chips):
            for a in range(na):
                copy(a, 1 + j, (*chip, c), me).wait_recv()
                fw = copy(a, 4 + j, (*chip, c), sibling)
                fw.start()
                passed.append(fw)
        for a in range(na):
            copy(a, 0, sibling, me).wait_recv()
            for j, chip in enumerate(chips):
                copy(a, 4 + j, (*chip, 1 - c), me).wait_recv()
        for cp in first + passed:
            cp.wait_send()
        for cp in mine:
            cp.wait()

    any_spec = pl.BlockSpec(memory_space=pl.ANY)
    return _pc(
        body, "gather_weights",
        in_specs=[any_spec] * na, out_specs=[any_spec] * na,
        out_shape=[_sds((N_DEV,) + a.shape, a.dtype) for a in arrays],
        scratch_shapes=[pltpu.SemaphoreType.DMA((7 * na,)), pltpu.SemaphoreType.DMA((7 * na,)),
                        pltpu.SemaphoreType.DMA((na,))],
    )(*arrays)


def _peer(k):
    x, y, c = _place()
    return (1 - x if k & 4 else x, 1 - y if k & 2 else y, 1 - c if k & 1 else c)


def _remote(src, dst, sems, s, to):
    return pltpu.make_async_remote_copy(src_ref=src, dst_ref=dst, send_sem=sems[0].at[s], recv_sem=sems[1].at[s],
                                        device_id=to, device_id_type=MESH)


def _gather_rider(shards):
    na = len(shards)

    def plan(ins, outs, *sems):
        x, y, c = _place()
        me = 4 * x + 2 * y + c
        locs = [pltpu.make_async_copy(ins[a], outs[a].at[me], sems[2].at[a]) for a in range(na)]
        sends, recvs = [], []
        for k in range(1, N_DEV):
            px, py, pc = _peer(k)
            for a in range(na):
                s = (k - 1) * na + a
                sends.append(_remote(ins[a], outs[a].at[me], sems, s, (px, py, pc)))
                recvs.append(_remote(ins[a], outs[a].at[4 * px + 2 * py + pc], sems, s, (px, py, pc)))
        return sends, recvs, locs

    return _Rider(shards, [_sds((N_DEV,) + a.shape, a.dtype) for a in shards], (N_DEV - 1) * na, na, plan)


def _swap_rider(halves):
    na = len(halves)

    def plan(ins, outs, *sems):
        x, y, c = _place()
        cps = [_remote(ins[a].at[1 - c], outs[a], sems, a, (x, y, 1 - c)) for a in range(na)]
        return cps, cps, []

    return _Rider(halves, [_sds(a.shape[1:], a.dtype) for a in halves], na, 1, plan)


def _chip_exchange_rider(parts, small=None):
    na = len(parts)
    n_chip = N_DEV // 2

    def plan(ins, outs, *sems):
        x, y, c = _place()
        chip = 2 * x + y
        locs = [pltpu.make_async_copy(ins[a].at[chip], outs[a].at[chip], sems[2].at[a]) for a in range(na)]
        sends, recvs = [], []
        for k in range(1, n_chip):
            px, py, _ = _peer(2 * k)
            for a in range(na):
                s = (k - 1) * na + a
                sends.append(_remote(ins[a].at[2 * px + py], outs[a].at[chip], sems, s, (px, py, c)))
                recvs.append(_remote(ins[a].at[2 * px + py], outs[a].at[2 * px + py], sems, s, (px, py, c)))
        if small is not None:
            me = 2 * chip + c
            locs.append(pltpu.make_async_copy(ins[na], outs[na].at[me], sems[2].at[na]))
            for k in range(1, N_DEV):
                px, py, pc = _peer(k)
                s = (n_chip - 1) * na + k - 1
                sends.append(_remote(ins[na], outs[na].at[me], sems, s, (px, py, pc)))
                recvs.append(_remote(ins[na], outs[na].at[4 * px + 2 * py + pc], sems, s, (px, py, pc)))
        return sends, recvs, locs

    extra = [] if small is None else [small]
    shapes = [_sds(a.shape, a.dtype) for a in parts] + [_sds((N_DEV,) + s.shape, s.dtype) for s in extra]
    n_sems = (n_chip - 1) * na + (N_DEV - 1) * len(extra)
    return _Rider(list(parts) + extra, shapes, n_sems, na + len(extra), plan)


def _pair_add(halves, other, core, name):
    _, nch, r, c = halves.shape
    tr = 256 if r % 256 == 0 else r

    def body(h_ref, o_ref, c_ref, p_ref):
        mine = jnp.where(c_ref[...] == 0, h_ref[0].astype(F32), h_ref[1].astype(F32))
        p_ref[...] = (mine + o_ref[...].astype(F32)).astype(BF16)

    blk = pl.BlockSpec((None, tr, c), lambda j, i: (j, i, 0))
    return _pc(
        body, name, grid=(nch, r // tr),
        in_specs=[pl.BlockSpec((2, None, tr, c), lambda j, i: (0, j, i, 0)), blk, pl.BlockSpec((1, 1), lambda j, i: (0, 0))],
        out_specs=blk, out_shape=_sds((nch, r, c), BF16),
        compiler_params=_params(),
    )(halves, other, core)


def _sum_adamw(parts, w, m, v, name, rider=None):
    nl, r, c = w.shape
    tr = 256 if r % 256 == 0 else r

    def body(*refs):
        p_refs = refs[:nl]
        w_ref, m_ref, v_ref, g_ref, d_ref, mo_ref, vo_ref = refs[nl:]
        for l in range(nl):
            @pl.when(pl.program_id(0) == l)
            def _(p_ref=p_refs[l]):
                g = p_ref[0].astype(F32)
                for j in range(1, p_ref.shape[0]):
                    g = g + p_ref[j].astype(F32)
                mn = ADAM_B1 * m_ref[...] + (1.0 - ADAM_B1) * g
                vn = ADAM_B2 * v_ref[...] + (1.0 - ADAM_B2) * (g * g)
                m_hat = mn / (1.0 - ADAM_B1 ** ADAM_STEP)
                v_hat = vn / (1.0 - ADAM_B2 ** ADAM_STEP)
                g_ref[...] = g
                d_ref[...] = -ADAM_LR * (m_hat / (jnp.sqrt(v_hat) + ADAM_EPS) + ADAM_WD * w_ref[...])
                mo_ref[...] = mn
                vo_ref[...] = vn

    def part_spec(l, k):
        return pl.BlockSpec((k, tr, c), lambda li, i: (0, jnp.where(li == l, i, 0), 0))

    row = pl.BlockSpec((None, tr, c), lambda li, i: (li, i, 0))
    return _call(
        body, name, (*parts, w, m, v), rider, grid=(nl, r // tr),
        in_specs=[part_spec(l, p.shape[0]) for l, p in enumerate(parts)] + [row, row, row],
        out_specs=[row] * 4,
        out_shape=[_sds((nl, r, c), F32)] * 4,
        compiler_params=_params(),
    )


SMALL = ("lower_bounds", "pre_norm_g", "hgrn_norm_g", "fox_f_bias", "pool_w", "pool_scale", "post_norm_g")
SMALL_LANES = 128


def _small_size(tree):
    return sum(tree[k].size for k in SMALL)


def _pack_small(tree, extra=None):
    flat = jnp.concatenate([tree[k].reshape(-1) for k in SMALL] + ([] if extra is None else [extra.reshape(1)]))
    rows = -(-(_small_size(tree) + 1) // (8 * SMALL_LANES)) * 8
    return jnp.pad(flat, (0, rows * SMALL_LANES - flat.shape[0])).reshape(rows, SMALL_LANES)


def _unpack_small(packed, like):
    flat, out, off = packed.reshape(-1), {}, 0
    for k in SMALL:
        size = like[k].size
        out[k] = flat[off:off + size].reshape(like[k].shape)
        off += size
    return out


def _block_diag(pw):
    g = pw.shape[0]
    eye = jnp.eye(g, dtype=pw.dtype)
    return (eye[:, None, :, None] * pw[:, :, None, :]).reshape(g * HEAD, g * HEAD)


def _assemble_w_in(g_in, name):
    _, d, shard = g_in.shape
    tr = min(256, d)
    wide = MAIN_W + FC_PAD

    def body(g_ref, wm_ref, wf_ref, wmt_ref, wft_ref, row_s):
        row_s[:, MAIN_W:] = jnp.zeros((tr, FC_PAD), F32)
        for j in range(N_DEV):
            row_s[:, j * shard:(j + 1) * shard] = g_ref[j].astype(F32)
        wm_ref[...] = row_s[:, :MAIN_W].astype(BF16)
        wf_ref[...] = row_s[:, MAIN_W:].astype(BF16)
        for j in range(0, MAIN_W, 512):
            wmt_ref[j:j + 512, :] = row_s[:, j:j + 512].T.astype(BF16)
        wft_ref[...] = row_s[:, MAIN_W:].T.astype(BF16)

    return _pc(
        body, name, grid=(d // tr,),
        in_specs=[pl.BlockSpec((N_DEV, tr, shard), lambda i: (0, i, 0))],
        out_specs=[pl.BlockSpec((tr, MAIN_W), lambda i: (i, 0)), pl.BlockSpec((tr, FC_PAD), lambda i: (i, 0)),
                   pl.BlockSpec((MAIN_W, tr), lambda i: (0, i)), pl.BlockSpec((FC_PAD, tr), lambda i: (0, i))],
        out_shape=[_sds((d, MAIN_W), BF16), _sds((d, FC_PAD), BF16), _sds((MAIN_W, d), BF16), _sds((FC_PAD, d), BF16)],
        scratch_shapes=[pltpu.VMEM((tr, wide), F32)],
        compiler_params=_params(),
    )(g_in)


def _layer_weights(l, g_in, g_out):
    d = g_in.shape[1]
    w_main, w_fc, w_main_t, w_fc_t = _assemble_w_in(g_in, f"assemble_w_in_{l}")
    full_out = g_out.reshape(N_DEV * g_out.shape[1], d)
    return w_main, w_fc, w_main_t, w_fc_t, full_out, full_out.T


def _layer_fwd(l, x, lbs, weights, lw, nb, rider=None):
    n = x.shape[0]
    t = n // nb
    w_main, w_fc, _, _, w_out, _ = lw
    bias = jnp.pad(weights["fox_f_bias"][l:l + 1], ((0, 0), (0, FC_PAD - FOX_HEADS)))
    wbd = _block_diag(weights["pool_w"][l]).astype(BF16)
    proj, fc, ht = _in_proj_fwd(x, weights["pre_norm_g"][l:l + 1], w_main, w_fc, f"in_proj_fwd_{l}")
    c_col = _fox_decay_fwd(fc, bias, nb, f"fox_decay_fwd_{l}")
    c_row = c_col.reshape(nb, t, FC_PAD)[:, :, :FOX_HEADS].transpose(0, 2, 1)
    o_h, s0 = _hgrn_fwd(proj, lbs[l:l + 1], _block_ones(HGRN_W, BF16), nb, f"hgrn_fwd_{l}")
    o_b = _pool_fwd(proj, wbd, weights["pool_scale"][l:l + 1], nb, f"pool_fwd_{l}")
    (o_c, lse), rode = _fox_fwd(proj, c_col, c_row, nb, f"fox_fwd_{l}", rider)
    x_next, mixt, y = _merge_fwd(x, proj, o_h, o_b, o_c, weights["hgrn_norm_g"][l:l + 1], w_out,
                                 weights["post_norm_g"][l:l + 1], f"merge_fwd_{l}")
    return x_next, (x, proj, fc, ht, c_col, c_row, o_h, s0, o_c, lse, mixt, y, bias, wbd), rode


def _layer_bwd(l, dx, saved, lbs, weights, lw, nb, rider=None):
    _, proj, fc, ht, c_col, c_row, o_h, s0, o_c, lse, mixt, y, bias, wbd = saved
    n = proj.shape[0]
    t = n // nb
    w_out_t = lw[5]
    g = {}
    dy, dmix, dgp = _merge_bwd(dx, y, weights["post_norm_g"][l:l + 1], w_out_t, f"merge_bwd_{l}")
    g["post_norm_g"] = dgp[0]
    g["w_out"] = _w_out_grad(mixt, dy, f"w_out_grad_{l}")
    d_a, dgh, dlb = _hgrn_bwd(dmix, proj, o_h, s0, weights["hgrn_norm_g"][l:l + 1], lbs[l:l + 1],
                              _block_ones(HGRN_W, BF16), nb, f"hgrn_bwd_{l}")
    g["hgrn_norm_g"], g["lbs"] = dgh[0], dlb[0]
    d_b, dwbd, dps = _pool_bwd(dmix, proj, wbd, wbd.T, weights["pool_scale"][l:l + 1], nb, f"pool_bwd_{l}")
    g["pool_w"] = jnp.stack([dwbd[j * HEAD:(j + 1) * HEAD, j * HEAD:(j + 1) * HEAD] for j in range(len(POOL_WINDOWS))])
    g["pool_scale"] = dps[0]
    da, d_gc, delta = _fox_gate_bwd(dmix, proj, o_c, f"fox_gate_bwd_{l}")
    delta_row, lse_row = [a.reshape(nb, t, FC_PAD)[:, :, :FOX_HEADS].transpose(0, 2, 1) for a in (delta, lse)]
    (d_qc, d_kc, d_vc, dc_k, dc_q), rode = _fox_bwd(proj, da, c_col, c_row, lse_row, delta_row, nb, f"fox_bwd_{l}", rider)
    dc_q = jnp.pad(dc_q.transpose(0, 2, 1).reshape(n, FOX_HEADS), ((0, 0), (0, FC_PAD - FOX_HEADS)))
    d_fc, dbias = _fox_decay_bwd(dc_q, dc_k, fc, bias, nb, f"fox_decay_bwd_{l}")
    g["fox_f_bias"] = dbias[0, :FOX_HEADS]
    pieces = [(d_a, C_QA), (d_b, C_UB), (d_qc, C_QC), (d_kc, C_KC), (d_vc, C_VC), (d_gc, C_GC), (d_fc, None)]
    g["w_in"] = _w_in_grad(ht, pieces, f"w_in_grad_{l}")
    return g, pieces, rode


def _layer_bwd_input(l, dx, pieces, saved, weights, lw, rider=None):
    (dxi, dgpre), rode = _in_proj_bwd(pieces, lw[2], lw[3], saved[0], weights["pre_norm_g"][l:l + 1], dx,
                                      f"in_proj_bwd_{l}", rider)
    return dxi, dgpre[0], rode


def kernel(x, lower_bounds, pre_norm_g, w_in, hgrn_norm_g, fox_f_bias, pool_w, pool_scale, w_out, post_norm_g, loss_target, m_lower_bounds, m_pre_norm_g, m_w_in, m_hgrn_norm_g, m_fox_f_bias, m_pool_w, m_pool_scale, m_w_out, m_post_norm_g, v_lower_bounds, v_pre_norm_g, v_w_in, v_hgrn_norm_g, v_fox_f_bias, v_pool_w, v_pool_scale, v_w_out, v_post_norm_g):
    weights = dict(lower_bounds=lower_bounds, pre_norm_g=pre_norm_g, hgrn_norm_g=hgrn_norm_g, fox_f_bias=fox_f_bias,
                   pool_w=pool_w, pool_scale=pool_scale, post_norm_g=post_norm_g)
    mom_m = dict(lower_bounds=m_lower_bounds, pre_norm_g=m_pre_norm_g, hgrn_norm_g=m_hgrn_norm_g, fox_f_bias=m_fox_f_bias,
                 pool_w=m_pool_w, pool_scale=m_pool_scale, post_norm_g=m_post_norm_g)
    mom_v = dict(lower_bounds=v_lower_bounds, pre_norm_g=v_pre_norm_g, hgrn_norm_g=v_hgrn_norm_g, fox_f_bias=v_fox_f_bias,
                 pool_w=v_pool_w, pool_scale=v_pool_scale, post_norm_g=v_post_norm_g)
    depth = w_in.shape[0]
    nb, t, d = x.shape
    n = nb * t
    core = lax.axis_index("c").astype(jnp.int32).reshape(1, 1)
    shards = [(w_in[l].astype(BF16), w_out[l].astype(BF16)) for l in range(depth)]
    lbs = _lower_bound_table(lower_bounds, "lower_bound_table")

    lw = [_layer_weights(0, *_gather_weights(*shards[0]))]
    xl, saved = x.reshape(n, d), []
    for l in range(depth):
        rider = _gather_rider(shards[l + 1]) if l + 1 < depth else None
        xl, sv, rode = _layer_fwd(l, xl, lbs, weights, lw[l], nb, rider)
        saved.append(sv)
        if rode is not None:
            lw.append(_layer_weights(l + 1, *rode))
    dx, sq = _loss_fwd_bwd(xl, loss_target.reshape(n, d), "loss")
    loss_here = 0.5 * jnp.sum(sq) / d

    grads, recv, pending = [None] * depth, [None] * depth, None
    for l in reversed(range(depth)):
        rider = None if pending is None else _chip_exchange_rider(pending)
        g, pieces, rode = _layer_bwd(l, dx, saved[l], lbs, weights, lw[l], nb, rider)
        if rode is not None:
            recv[l + 1] = rode
        halves = (g["w_in"], g["w_out"])
        if l > 0:
            dx, g["pre_norm_g"], other = _layer_bwd_input(l, dx, pieces, saved[l], weights, lw[l], _swap_rider(halves))
        else:
            other = _run_rider(_swap_rider(halves), f"grad_swap_{l}")
        pending = [_pair_add(hv, ot, core, f"grad_pair_add_{l}_{i}") for i, (hv, ot) in enumerate(zip(halves, other))]
        if l == 0:
            dx, g["pre_norm_g"], recv[l] = _layer_bwd_input(l, dx, pieces, saved[l], weights, lw[l],
                                                            _chip_exchange_rider(pending))
        grads[l] = g
    small = {k: jnp.stack([grads[l][k] for l in range(depth)]) for k in SMALL if k != "lower_bounds"}
    small["lower_bounds"] = _lower_bound_bwd(lower_bounds, jnp.stack([grads[l]["lbs"] for l in range(depth)]),
                                             "lower_bound_bwd")
    (r_small,) = _run_rider(_gather_rider([_pack_small(small, loss_here)]), "small_grads_gather")

    res_in, _ = _sum_adamw([recv[l][0] for l in range(depth)], w_in, m_w_in, v_w_in, "adamw_w_in")
    res_out, _ = _sum_adamw([recv[l][1] for l in range(depth)], w_out, m_w_out, v_w_out, "adamw_w_out")
    res_small, _ = _sum_adamw([r_small], _pack_small(weights)[None], _pack_small(mom_m)[None], _pack_small(mom_v)[None],
                              "adamw_small")
    loss = res_small[0][0].reshape(-1)[_small_size(weights)]

    names = ("lower_bounds", "pre_norm_g", "w_in", "hgrn_norm_g", "fox_f_bias", "pool_w", "pool_scale", "w_out", "post_norm_g")
    outs = [loss, dx.reshape(nb, t, d)]
    for i in range(4):
        full = dict(_unpack_small(res_small[i][0], weights), w_in=res_in[i], w_out=res_out[i])
        outs += [full[k] for k in names]
    return tuple(outs)
```

```python
import functools

import jax
import jax.numpy as jnp
from jax import lax
from jax.experimental import pallas as pl
from jax.experimental.pallas import tpu as pltpu

F32, BF16 = jnp.float32, jnp.bfloat16
HI = lax.Precision.HIGHEST
MESH = pl.DeviceIdType.MESH
AXES = ("x", "y", "c")
N_DEV = 8

NORM_EPS = 1e-6
MASK_VALUE = -1e30
TINY = 1e-30
CHUNK = 64
SUB = 16
HGRN_W, POOL_W, FOX_W = 256, 256, 512
HEAD = 64
FOX_HEADS = 8
POOL_WINDOWS = (2, 4, 8, 16)
POOL_HALO = 16
MAIN_W = 3584
FC_PAD = 128
C_QA, C_FA, C_IA, C_GA, C_UB, C_GB, C_QC, C_KC, C_VC, C_GC = 0, 256, 512, 768, 1024, 1280, 1536, 2048, 2560, 3072
FOX_SCALE = HEAD ** -0.5

ADAM_LR, ADAM_B1, ADAM_B2, ADAM_EPS, ADAM_WD, ADAM_STEP = 0.001, 0.9, 0.999, 1e-08, 0.01, 10

VMEM_LIMIT = 56 * 1024 * 1024


def _pc(fn, name, **kw):
    return pl.pallas_call(fn, name=name, **kw)


def _params(**kw):
    return pltpu.CompilerParams(vmem_limit_bytes=VMEM_LIMIT, **kw)


class _Rider:
    def __init__(self, inputs, out_shapes, n_sems, n_local, plan):
        self.inputs, self.out_shapes, self.n_sems, self.n_local, self.plan = list(inputs), list(out_shapes), n_sems, n_local, plan

    def start(self, ins, outs, *sems):
        sends, _, locs = self.plan(ins, outs, *sems)
        for cp in locs + sends:
            cp.start()

    def wait(self, ins, outs, *sems):
        sends, recvs, locs = self.plan(ins, outs, *sems)
        for cp in recvs:
            cp.wait_recv()
        for cp in sends:
            cp.wait_send()
        for cp in locs:
            cp.wait()

    def sem_shapes(self):
        return [pltpu.SemaphoreType.DMA((self.n_sems,)), pltpu.SemaphoreType.DMA((self.n_sems,)),
                pltpu.SemaphoreType.DMA((self.n_local,))]


def _call(body, name, args, rider=None, *, grid, in_specs, out_specs, out_shape, scratch_shapes=(), **kw):
    if rider is None:
        res = _pc(body, name, grid=grid, in_specs=in_specs, out_specs=out_specs, out_shape=out_shape,
                  scratch_shapes=list(scratch_shapes), **kw)(*args)
        return res, None
    n_in, n_out, n_scr = len(in_specs), len(out_specs), len(scratch_shapes)
    n_rin, n_rout = len(rider.inputs), len(rider.out_shapes)

    def ridden(*refs):
        ins, refs = refs[:n_in], refs[n_in:]
        rins, refs = refs[:n_rin], refs[n_rin:]
        outs, refs = refs[:n_out], refs[n_out:]
        routs, refs = refs[:n_rout], refs[n_rout:]
        scr, sems = refs[:n_scr], refs[n_scr:]
        first = functools.reduce(jnp.logical_and, [pl.program_id(a) == 0 for a in range(len(grid))])
        last = functools.reduce(jnp.logical_and, [pl.program_id(a) == g - 1 for a, g in enumerate(grid)])

        @pl.when(first)
        def _():
            rider.start(rins, routs, *sems)

        body(*ins, *outs, *scr)

        @pl.when(last)
        def _():
            rider.wait(rins, routs, *sems)

    any_spec = pl.BlockSpec(memory_space=pl.ANY)
    res = _pc(ridden, name, grid=grid, in_specs=list(in_specs) + [any_spec] * n_rin,
              out_specs=list(out_specs) + [any_spec] * n_rout, out_shape=list(out_shape) + rider.out_shapes,
              scratch_shapes=list(scratch_shapes) + rider.sem_shapes(), **kw)(*args, *rider.inputs)
    return res[:n_out], res[n_out:]


def _run_rider(rider, name):
    n_rin = len(rider.inputs)

    def body(*refs):
        ins, outs, sems = refs[:n_rin], refs[n_rin:n_rin + len(rider.out_shapes)], refs[n_rin + len(rider.out_shapes):]
        rider.start(ins, outs, *sems)
        rider.wait(ins, outs, *sems)

    any_spec = pl.BlockSpec(memory_space=pl.ANY)
    return _pc(body, name, in_specs=[any_spec] * n_rin, out_specs=[any_spec] * len(rider.out_shapes),
               out_shape=rider.out_shapes, scratch_shapes=rider.sem_shapes())(*rider.inputs)


def _dot(a, b):
    return jnp.dot(a, b, preferred_element_type=F32)


def _dot_nt(a, b):
    return lax.dot_general(a, b, (((1,), (1,)), ((), ())), preferred_element_type=F32)


def _dot_tn(a, b):
    return lax.dot_general(a, b, (((0,), (0,)), ((), ())), preferred_element_type=F32)


def _dot_hi(a, b):
    return jnp.dot(a, b, precision=HI, preferred_element_type=F32)


def _split2(x):
    hi = x.astype(BF16)
    return hi, (x - hi.astype(F32)).astype(BF16)


def _sel_dot(sel, x):
    hi, lo = _split2(x)
    sb = sel.astype(BF16)
    return _dot(sb, hi) + _dot(sb, lo)


def _dot_sel(x, sel):
    hi, lo = _split2(x)
    sb = sel.astype(BF16)
    return _dot(hi, sb) + _dot(lo, sb)


def _sigmoid(x):
    return 1.0 / (1.0 + jnp.exp(-x))


def _block_ones(n, dtype):
    r = lax.broadcasted_iota(jnp.int32, (n, n), 0) // HEAD
    c = lax.broadcasted_iota(jnp.int32, (n, n), 1) // HEAD
    return (r == c).astype(dtype)


def _sds(shape, dtype):
    return jax.ShapeDtypeStruct(shape, dtype)


def _in_proj_fwd(x, g_pre, w_main, w_fc, name):
    n, d = x.shape
    tm = min(512, n)

    def body(x_ref, g_ref, w_ref, wf_ref, proj_ref, fc_ref, ht_ref):
        xv = x_ref[...]
        r = lax.rsqrt(jnp.mean(xv * xv, axis=-1, keepdims=True) + NORM_EPS)
        hf = xv * r * g_ref[...]
        hb = hf.astype(BF16)
        ht_ref[...] = hf.T.astype(BF16)
        for j in range(0, MAIN_W, 512):
            proj_ref[:, j:j + 512] = _dot(hb, w_ref[:, j:j + 512])
        fc_ref[...] = _dot(hb, wf_ref[...])

    return _pc(
        body, name, grid=(n // tm,),
        in_specs=[pl.BlockSpec((tm, d), lambda i: (i, 0)), pl.BlockSpec((1, d), lambda i: (0, 0)),
                  pl.BlockSpec((d, MAIN_W), lambda i: (0, 0)), pl.BlockSpec((d, FC_PAD), lambda i: (0, 0))],
        out_specs=[pl.BlockSpec((tm, MAIN_W), lambda i: (i, 0)), pl.BlockSpec((tm, FC_PAD), lambda i: (i, 0)),
                   pl.BlockSpec((d, tm), lambda i: (0, i))],
        out_shape=[_sds((n, MAIN_W), F32), _sds((n, FC_PAD), F32), _sds((d, n), BF16)],
        compiler_params=_params(),
    )(x, g_pre, w_main, w_fc)


def _fox_decay_fwd(fc, bias, nb, name):
    n = fc.shape[0]
    t = n // nb
    tt = min(256, t)
    nt = t // tt

    def body(fc_ref, b_ref, c_ref, carry):
        i = pl.program_id(1)

        @pl.when(i == 0)
        def _():
            carry[...] = jnp.zeros_like(carry)

        xv = fc_ref[...] + b_ref[...]
        lf = jnp.minimum(xv, 0.0) - jnp.log(1.0 + jnp.exp(-jnp.abs(xv)))
        r = lax.broadcasted_iota(jnp.int32, (tt, tt), 0)
        cc = lax.broadcasted_iota(jnp.int32, (tt, tt), 1)
        cs = _dot_hi((r >= cc).astype(F32), lf) + carry[...]
        c_ref[...] = cs
        carry[...] = cs[tt - 1:tt, :]

    return _pc(
        body, name, grid=(nb, nt),
        in_specs=[pl.BlockSpec((tt, FC_PAD), lambda b, i: (b * nt + i, 0)), pl.BlockSpec((1, FC_PAD), lambda b, i: (0, 0))],
        out_specs=pl.BlockSpec((tt, FC_PAD), lambda b, i: (b * nt + i, 0)),
        out_shape=_sds((n, FC_PAD), F32),
        scratch_shapes=[pltpu.VMEM((1, FC_PAD), F32)],
        compiler_params=_params(),
    )(fc, bias)


def _hgrn_gates(q, z, lb):
    sig = _sigmoid(z)
    sn = _sigmoid(-z)
    f = lb + (1.0 - lb) * sig
    g = jnp.log(jnp.maximum(f, TINY))
    k = (1.0 - lb) * sn
    sq = _sigmoid(q)
    return sig, sn, f, g, k, sq


def _sub_tri(n, lower):
    r = lax.broadcasted_iota(jnp.int32, (n, n), 0)
    c = lax.broadcasted_iota(jnp.int32, (n, n), 1)
    tri = (r >= c) if lower else (r <= c)
    return jnp.logical_and(r // SUB == c // SUB, tri).astype(F32)


def _hgrn_decays(qs, k, b):
    srow = lax.broadcasted_iota(jnp.int32, (SUB, HGRN_W), 0)
    es, ws = [], []
    for t in range(SUB):
        e = jnp.where(srow <= t, jnp.exp(b[t:t + 1, :] - b), 0.0)
        es.append(e)
        ws.append(e * (qs[t:t + 1, :] * k))
    return srow, es, ws


def _hgrn_state_step(st, k, v, b, bmask):
    bl = b[SUB - 1:SUB, :]
    ktil = k * jnp.exp(bl - b)
    return st * jnp.exp(bl) + _dot_tn(v.astype(BF16), ktil.astype(BF16)) * bmask


def _hgrn_sub_fwd(qs, k, v, b, st, ones_b, bmask):
    srow, _, ws = _hgrn_decays(qs, k, b)
    aexp = _dot(jnp.concatenate(ws, axis=0).astype(BF16), ones_b)
    o = _dot_nt((qs * jnp.exp(b)).astype(BF16), st.astype(BF16))
    for t in range(SUB):
        row = jnp.sum(aexp[t * SUB:(t + 1) * SUB, :] * v, axis=0, keepdims=True)
        o = o + jnp.where(srow == t, row, 0.0)
    return o, _hgrn_state_step(st, k, v, b, bmask)


def _hgrn_tile(t):
    return min(256, t)


def _hgrn_fwd(proj, lb, ones_b, nb, name):
    n = proj.shape[0]
    t = n // nb
    tt = _hgrn_tile(t)
    nt = t // tt
    ncs = tt // CHUNK
    w = HGRN_W

    def body(q_ref, z_ref, v_ref, lb_ref, ones_ref, o_ref, s0_ref, st_s, b_s, qs_s, k_s):
        @pl.when(pl.program_id(1) == 0)
        def _():
            st_s[...] = jnp.zeros_like(st_s)

        q = q_ref[...]
        _, _, _, g, k, sq = _hgrn_gates(q, z_ref[...], lb_ref[...])
        b_s[...] = _sel_dot(_sub_tri(tt, True), g)
        qs_s[...] = q * sq
        k_s[...] = k
        bmask = _block_ones(w, F32)
        ones_b = ones_ref[...]

        def chunk(c, carry):
            st = st_s[...]
            s0_ref[c] = st
            base = pl.multiple_of(c * CHUNK, CHUNK)
            for u in range(CHUNK // SUB):
                rows = pl.ds(base + u * SUB, SUB)
                o, st = _hgrn_sub_fwd(qs_s[rows, :], k_s[rows, :], v_ref[rows, :], b_s[rows, :], st, ones_b, bmask)
                o_ref[rows, :] = o
            st_s[...] = st
            return carry

        lax.fori_loop(0, ncs, chunk, 0)

    def col(j):
        return pl.BlockSpec((tt, w), lambda b, i: (b * nt + i, j))

    return _pc(
        body, name, grid=(nb, nt),
        in_specs=[col(C_QA // w), col(C_FA // w), col(C_IA // w), pl.BlockSpec((1, w), lambda b, i: (0, 0)),
                  pl.BlockSpec((w, w), lambda b, i: (0, 0))],
        out_specs=[pl.BlockSpec((tt, w), lambda b, i: (b * nt + i, 0)),
                   pl.BlockSpec((ncs, w, w), lambda b, i: (b * nt + i, 0, 0))],
        out_shape=[_sds((n, w), F32), _sds((n // CHUNK, w, w), F32)],
        scratch_shapes=[pltpu.VMEM((w, w), F32)] + [pltpu.VMEM((tt, w), F32)] * 3,
        compiler_params=_params(),
    )(proj, proj, proj, lb, ones_b)


def _pool_lane_windows():
    lane = lax.broadcasted_iota(jnp.int32, (1, POOL_W), 1) // HEAD
    wl = jnp.zeros((1, POOL_W), F32)
    for gi, win in enumerate(POOL_WINDOWS):
        wl = jnp.where(lane == gi, float(win), wl)
    return lane, wl


def _pool_select(lane, parts):
    out = parts[-1]
    for gi in range(len(parts) - 2, -1, -1):
        out = jnp.where(lane == gi, parts[gi], out)
    return out


def _pool_mix(u, halo, t0, tt):
    lane, wl = _pool_lane_windows()
    ext = jnp.concatenate([halo, u], axis=0)
    sums, cur, shift = [], ext, 1
    for _ in POOL_WINDOWS:
        cur = cur + pltpu.roll(cur, shift, axis=0)
        sums.append(cur[POOL_HALO:, :])
        shift *= 2
    tpos = (t0 + lax.broadcasted_iota(jnp.int32, (tt, POOL_W), 0)).astype(F32)
    cnt = jnp.minimum(tpos + 1.0, wl)
    return _pool_select(lane, sums) / cnt - u, cnt


def _pool_specs(tt, nt, nhb):
    cu, cg = C_UB // POOL_W, C_GB // POOL_W
    return [pl.BlockSpec((tt, POOL_W), lambda b, i: (b * nt + i, cu)),
            pl.BlockSpec((tt, POOL_W), lambda b, i: (b * nt + i, cg)),
            pl.BlockSpec((POOL_HALO, POOL_W), lambda b, i: (jnp.maximum((b * nt + i) * nhb - 1, 0), cu))]


def _pool_fwd(proj, wbd, scale, nb, name):
    n = proj.shape[0]
    t = n // nb
    tt = min(256, t)
    nt = t // tt
    nhb = tt // POOL_HALO

    def body(u_ref, g_ref, h_ref, w_ref, s_ref, o_ref):
        i = pl.program_id(1)
        halo = jnp.where(i == 0, 0.0, h_ref[...])
        pooled, _ = _pool_mix(u_ref[...], halo, i * tt, tt)
        gv = g_ref[...]
        o_ref[...] = _dot(pooled.astype(BF16), w_ref[...]) * s_ref[...] * (gv * _sigmoid(gv))

    return _pc(
        body, name, grid=(nb, nt),
        in_specs=_pool_specs(tt, nt, nhb) + [pl.BlockSpec((POOL_W, POOL_W), lambda b, i: (0, 0)),
                                             pl.BlockSpec((1, POOL_W), lambda b, i: (0, 0))],
        out_specs=pl.BlockSpec((tt, POOL_W), lambda b, i: (b * nt + i, 0)),
        out_shape=_sds((n, POOL_W), F32),
        compiler_params=_params(),
    )(proj, proj, proj, wbd, scale)


def _rows_reduce(x, op, final):
    while x.shape[0] > 8 and x.shape[0] % 16 == 0:
        half = x.shape[0] // 2
        x = op(x[:half], x[half:])
    return final(x, axis=0, keepdims=True)


def _lane_lo():
    return lax.broadcasted_iota(jnp.int32, (1, 2 * HEAD), 1) < HEAD


def _put_col(tile, hh, colv):
    lane = lax.broadcasted_iota(jnp.int32, tile.shape, 1)
    return jnp.where(lane == hh, colv, tile)


def _fox_fwd(proj, c_col, c_row, nb, name, rider=None):
    n = proj.shape[0]
    t = n // nb
    tb = min(256, t)
    nq = t // tb
    pw = 2 * HEAD

    def body(q_ref, k_ref, v_ref, cc_ref, cr_ref, o_ref, lse_ref, m_s, acc_s, cq_s):
        qi, kj = pl.program_id(1), pl.program_id(2)

        @pl.when(kj == 0)
        def _():
            m_s[...] = jnp.full_like(m_s, -jnp.inf)
            acc_s[...] = jnp.zeros_like(acc_s)
            for hh in range(FOX_HEADS):
                cq_s[hh] = jnp.broadcast_to(cc_ref[:, hh:hh + 1], (tb, pw))

        def block(masked):
            lo = _lane_lo()
            if masked:
                causal = lax.broadcasted_iota(jnp.int32, (tb, tb), 0) >= lax.broadcasted_iota(jnp.int32, (tb, tb), 1)
            for p in range(FOX_HEADS // 2):
                sl = slice(p * pw, (p + 1) * pw)
                qp = q_ref[:, sl] * FOX_SCALE
                kp = k_ref[:, sl].astype(BF16)
                vf = v_ref[:, sl]
                for h in range(2):
                    hh = 2 * p + h
                    lm = lo if h == 0 else jnp.logical_not(lo)
                    s = _dot_nt(jnp.where(lm, qp, 0.0).astype(BF16), kp)
                    s = s + (jnp.tile(cq_s[hh], (1, tb // pw)) - cr_ref[hh:hh + 1, :])
                    if masked:
                        s = jnp.where(causal, s, MASK_VALUE)
                    m_prev = m_s[hh]
                    m_new = jnp.maximum(m_prev, jnp.max(s, axis=1, keepdims=True))
                    alpha = jnp.exp(m_prev - m_new)
                    pe = jnp.exp(s - jnp.tile(m_new, (1, tb // pw)))
                    m_s[hh] = m_new
                    acc_s[hh] = alpha * acc_s[hh] + _dot(pe.astype(BF16), jnp.where(lm, vf, 1.0).astype(BF16))

        @pl.when(kj < qi)
        def _():
            block(False)

        @pl.when(kj == qi)
        def _():
            block(True)
            lo = _lane_lo()
            lse = jnp.zeros((tb, FC_PAD), F32)
            for p in range(FOX_HEADS // 2):
                halves = []
                for h in range(2):
                    hh = 2 * p + h
                    lm = lo if h == 0 else jnp.logical_not(lo)
                    acc = acc_s[hh]
                    swapped = pltpu.roll(acc, HEAD, axis=1)
                    halves.append(acc / swapped)
                    lse = _put_col(lse, hh, m_s[hh] + jnp.log(jnp.where(lm, swapped, acc)))
                o_ref[:, p * pw:(p + 1) * pw] = jnp.where(lo, halves[0], halves[1])
            lse_ref[...] = lse

    def qspec(wd, j):
        return pl.BlockSpec((tb, wd), lambda b, qi, kj: (b * nq + qi, j))

    def kspec(j):
        return pl.BlockSpec((tb, FOX_W), lambda b, qi, kj: (b * nq + jnp.minimum(kj, qi), j))

    return _call(
        body, name, (proj, proj, proj, c_col, c_row), rider, grid=(nb, nq, nq),
        in_specs=[qspec(FOX_W, C_QC // FOX_W), kspec(C_KC // FOX_W), kspec(C_VC // FOX_W), qspec(FC_PAD, 0),
                  pl.BlockSpec((None, FOX_HEADS, tb), lambda b, qi, kj: (b, 0, jnp.minimum(kj, qi)))],
        out_specs=[qspec(FOX_W, 0), qspec(FC_PAD, 0)],
        out_shape=[_sds((n, FOX_W), F32), _sds((n, FC_PAD), F32)],
        scratch_shapes=[pltpu.VMEM((FOX_HEADS, tb, pw), F32), pltpu.VMEM((FOX_HEADS, tb, pw), F32),
                        pltpu.VMEM((FOX_HEADS, tb, pw), F32)],
        compiler_params=_params(),
    )


def _head_mean(x, ones_f):
    return _dot_sel(x, ones_f) * (1.0 / HEAD)


def _merge_fwd(x, proj, o_h, o_b, o_c, gh, w_out, g_post, name, target=None):
    n, d = x.shape
    tm = min(512, n)

    def body(*refs):
        x_ref, ga_ref, gc_ref, oh_ref, ob_ref, oc_ref, gh_ref, w_ref, gp_ref = refs[:9]
        if target is None:
            xo_ref, mixt_ref, y_ref = refs[9:]
        else:
            t_ref, dx_ref, sq_ref, mixt_ref, y_ref = refs[9:]
        oh = oh_ref[...]
        ones_f = _block_ones(HGRN_W, F32)
        na = oh * lax.rsqrt(_head_mean(oh * oh, ones_f) + NORM_EPS) * gh_ref[...]
        ga, gc = ga_ref[...], gc_ref[...]
        mixed = jnp.concatenate([na * (ga * _sigmoid(ga)), ob_ref[...], oc_ref[...] * (gc * _sigmoid(gc))], axis=1)
        mixt_ref[...] = mixed.T.astype(BF16)
        y = _dot(mixed.astype(BF16), w_ref[...])
        y_ref[...] = y
        xn = x_ref[...] + y * lax.rsqrt(jnp.mean(y * y, axis=-1, keepdims=True) + NORM_EPS) * gp_ref[...]
        if target is None:
            xo_ref[...] = xn
        else:
            @pl.when(pl.program_id(0) == 0)
            def _():
                sq_ref[...] = jnp.zeros_like(sq_ref)

            e = xn - t_ref[...]
            dx_ref[...] = e * (1.0 / d)
            sq_ref[...] += jnp.sum(e * e, axis=0, keepdims=True)

    def row(wd, j=0):
        return pl.BlockSpec((tm, wd), lambda i: (i, j))

    def full(a, b):
        return pl.BlockSpec((a, b), lambda i: (0, 0))

    head = [] if target is None else [target]
    res = _pc(
        body, name, grid=(n // tm,),
        in_specs=[row(d), row(HGRN_W, C_GA // HGRN_W), row(FOX_W, C_GC // FOX_W), row(HGRN_W), row(POOL_W), row(FOX_W),
                  full(1, HGRN_W), full(d, d), full(1, d)] + [row(d)] * len(head),
        out_specs=[row(d)] + [full(1, d)] * len(head) + [pl.BlockSpec((d, tm), lambda i: (0, i)), row(d)],
        out_shape=[_sds((n, d), F32)] + [_sds((1, d), F32)] * len(head) + [_sds((d, n), BF16), _sds((n, d), F32)],
        compiler_params=_params(),
    )(x, proj, proj, o_h, o_b, o_c, gh, w_out, g_post, *head)
    return (res[0], res[1], res[2]) if target is None else ((res[0], res[1]), res[2], res[3])


def _rms_bwd(dy_scaled, xhat, r):
    return r * (dy_scaled - xhat * jnp.mean(dy_scaled * xhat, axis=-1, keepdims=True))


def _merge_bwd(dxo, y, g_post, w_out_t, name):
    n, d = y.shape
    tm = min(512, n)

    def body(dx_ref, y_ref, gp_ref, wt_ref, dy_ref, dm_ref, dgp_ref):
        @pl.when(pl.program_id(0) == 0)
        def _():
            dgp_ref[...] = jnp.zeros_like(dgp_ref)

        yv, dxv = y_ref[...], dx_ref[...]
        r = lax.rsqrt(jnp.mean(yv * yv, axis=-1, keepdims=True) + NORM_EPS)
        yh = yv * r
        dgp_ref[...] += jnp.sum(dxv * yh, axis=0, keepdims=True)
        dyb = _rms_bwd(dxv * gp_ref[...], yh, r).astype(BF16)
        dy_ref[...] = dyb
        dm_ref[...] = _dot(dyb, wt_ref[...])

    row = pl.BlockSpec((tm, d), lambda i: (i, 0))
    return _pc(
        body, name, grid=(n // tm,),
        in_specs=[row, row, pl.BlockSpec((1, d), lambda i: (0, 0)), pl.BlockSpec((d, d), lambda i: (0, 0))],
        out_specs=[row, row, pl.BlockSpec((1, d), lambda i: (0, 0))],
        out_shape=[_sds((n, d), BF16), _sds((n, d), F32), _sds((1, d), F32)],
        compiler_params=_params(),
    )(dxo, y, g_post, w_out_t)


def _w_out_grad(mixt, dy, name):
    d, n = mixt.shape
    rows = d // N_DEV

    def body(a_ref, b_ref, o_ref):
        o_ref[...] = _dot(a_ref[...], b_ref[...]).astype(BF16)

    return _pc(
        body, name, grid=(N_DEV,),
        in_specs=[pl.BlockSpec((rows, n), lambda j: (j, 0)), pl.BlockSpec((n, d), lambda j: (0, 0))],
        out_specs=pl.BlockSpec((None, None, rows, d), lambda j: (j % 2, j // 2, 0, 0)),
        out_shape=_sds((2, N_DEV // 2, rows, d), BF16),
        compiler_params=_params(),
    )(mixt, dy)


def _w_in_grad(ht, pieces, name):
    d, n = ht.shape
    ta, tk = min(512, d), min(512, n)
    nk = n // tk
    arrays = [p for p, _ in pieces]
    widths = [p.shape[1] for p in arrays]
    offs = [sum(widths[:i]) for i in range(len(widths))]
    in_w = MAIN_W + FOX_HEADS
    shard = in_w // N_DEV

    def body(*refs):
        a_ref, p_refs = refs[0], refs[1:1 + len(arrays)]
        o_ref, acc = refs[1 + len(arrays):]
        k = pl.program_id(1)

        @pl.when(k == 0)
        def _():
            acc[...] = jnp.zeros_like(acc)

        a = a_ref[...]
        for pr, off, wd in zip(p_refs, offs, widths):
            for j in range(0, wd, 512):
                jw = min(512, wd - j)
                acc[:, off + j:off + j + jw] += _dot(a, pr[:, j:j + jw])

        @pl.when(k == nk - 1)
        def _():
            for j in range(N_DEV):
                o_ref[j % 2, j // 2] = acc[:, j * shard:(j + 1) * shard].astype(BF16)

    return _pc(
        body, name, grid=(d // ta, nk),
        in_specs=[pl.BlockSpec((ta, tk), lambda i, k: (i, k))] + [pl.BlockSpec((tk, wd), lambda i, k: (k, 0)) for wd in widths],
        out_specs=pl.BlockSpec((2, N_DEV // 2, ta, shard), lambda i, k: (0, 0, i, 0)),
        out_shape=_sds((2, N_DEV // 2, d, shard), BF16),
        scratch_shapes=[pltpu.VMEM((ta, sum(widths)), F32)],
        compiler_params=_params(),
    )(ht, *arrays)


def _hgrn_sub_bwd(qs, k, v, b, do, s0, ds1, ones_b, bmask):
    bl = b[SUB - 1:SUB, :]
    eb, ebl, ekt = jnp.exp(b), jnp.exp(bl), jnp.exp(bl - b)
    qe, ktil = qs * eb, k * ekt
    ds1b, dob = ds1.astype(BF16), do.astype(BF16)
    dv = _dot_nt(ktil.astype(BF16), ds1b)
    dqe = _dot(dob, s0.astype(BF16))
    dktil = _dot(v.astype(BF16), ds1b)
    dbl = jnp.sum(dktil * ktil, axis=0, keepdims=True) + ebl * jnp.sum(s0 * ds1, axis=0, keepdims=True)
    ds0 = ds1 * ebl + _dot_tn(dob, qe.astype(BF16)) * bmask
    srow, es, ws = _hgrn_decays(qs, k, b)
    aexp = _dot(jnp.concatenate(ws, axis=0).astype(BF16), ones_b)
    gexp = _dot(jnp.concatenate([do[t:t + 1, :] * v for t in range(SUB)], axis=0).astype(BF16), ones_b)
    dq = dqe * eb
    dk = dktil * ekt
    for t in range(SUB):
        sl = slice(t * SUB, (t + 1) * SUB)
        ge = gexp[sl, :] * es[t]
        dq = dq + jnp.where(srow == t, jnp.sum(ge * k, axis=0, keepdims=True), 0.0)
        dk = dk + ge * qs[t:t + 1, :]
        dv = dv + aexp[sl, :] * do[t:t + 1, :]
    return dq, dk, dv, dbl, ds0


def _hgrn_bwd(dmix, proj, o_h, s0, gh, lb, ones_b, nb, name):
    n = proj.shape[0]
    t = n // nb
    tt = _hgrn_tile(t)
    nt = t // tt
    ncs = tt // CHUNK
    nsub = CHUNK // SUB
    w = HGRN_W

    def body(dm_ref, q_ref, z_ref, v_ref, ga_ref, oh_ref, s0_ref, gh_ref, lb_ref, ones_ref,
             dp_ref, dgh_ref, dlb_ref, ds_s, ss_s, b_s, qs_s, k_s, do_s, dq_s, dk_s, dv_s, dbl_s):
        first = jnp.logical_and(pl.program_id(0) == 0, pl.program_id(1) == 0)

        @pl.when(first)
        def _():
            dgh_ref[...] = jnp.zeros_like(dgh_ref)
            dlb_ref[...] = jnp.zeros_like(dlb_ref)

        @pl.when(pl.program_id(1) == 0)
        def _():
            ds_s[...] = jnp.zeros_like(ds_s)

        ones_b = ones_ref[...]
        ones_f = ones_b.astype(F32)
        bmask = _block_ones(w, F32)
        lbv, ghv = lb_ref[...], gh_ref[...]
        oh, ga, dm = oh_ref[...], ga_ref[...], dm_ref[...]
        rn = lax.rsqrt(_head_mean(oh * oh, ones_f) + NORM_EPS)
        nh = oh * rn
        sga = _sigmoid(ga)
        dp_ref[:, 3 * w:4 * w] = (dm * nh * ghv * (sga * (1.0 + ga * (1.0 - sga)))).astype(BF16)
        dn = dm * (ga * sga)
        dgh_ref[...] += jnp.sum(dn * nh, axis=0, keepdims=True)
        dn = dn * ghv
        do_s[...] = rn * (dn - nh * _head_mean(dn * nh, ones_f))
        q = q_ref[...]
        sig, sn, f, g, k, sq = _hgrn_gates(q, z_ref[...], lbv)
        qs = q * sq
        b_s[...] = _sel_dot(_sub_tri(tt, True), g)
        qs_s[...] = qs
        k_s[...] = k

        def chunk(cc, carry):
            c = ncs - 1 - cc
            base = pl.multiple_of(c * CHUNK, CHUNK)
            st = s0_ref[c]
            for u in range(nsub):
                ss_s[u] = st
                if u < nsub - 1:
                    rows = pl.ds(base + u * SUB, SUB)
                    st = _hgrn_state_step(st, k_s[rows, :], v_ref[rows, :], b_s[rows, :], bmask)
            ds = ds_s[...]
            for u in reversed(range(nsub)):
                rows = pl.ds(base + u * SUB, SUB)
                dq, dk, dv, dbl, ds = _hgrn_sub_bwd(qs_s[rows, :], k_s[rows, :], v_ref[rows, :], b_s[rows, :],
                                                    do_s[rows, :], ss_s[u], ds, ones_b, bmask)
                dq_s[rows, :] = dq
                dk_s[rows, :] = dk
                dv_s[rows, :] = dv
                dbl_s[rows, :] = jnp.broadcast_to(dbl, (SUB, w))
            ds_s[...] = ds
            return carry

        lax.fori_loop(0, ncs, chunk, 0)
        dqs, dk = dq_s[...], dk_s[...]
        dg = _sel_dot(_sub_tri(tt, False), qs * dqs - k * dk) + dbl_s[...]
        dfz = jnp.where(f > TINY, dg / jnp.maximum(f, TINY), 0.0)
        dlb_ref[...] += jnp.sum(dfz * (1.0 - sig) - dk * sn, axis=0, keepdims=True)
        dp_ref[:, 0:w] = (dqs * (sq * (1.0 + q * (1.0 - sq)))).astype(BF16)
        dp_ref[:, w:2 * w] = ((dfz - dk) * (1.0 - lbv) * sig * sn).astype(BF16)
        dp_ref[:, 2 * w:3 * w] = dv_s[...].astype(BF16)

    def rv(b, i):
        return b * nt + (nt - 1 - i)

    def col(j):
        return pl.BlockSpec((tt, w), lambda b, i: (rv(b, i), j))

    def full(a, bb):
        return pl.BlockSpec((a, bb), lambda b, i: (0, 0))

    return _pc(
        body, name, grid=(nb, nt),
        in_specs=[col(0), col(C_QA // w), col(C_FA // w), col(C_IA // w), col(C_GA // w), col(0),
                  pl.BlockSpec((ncs, w, w), lambda b, i: (rv(b, i), 0, 0)), full(1, w), full(1, w), full(w, w)],
        out_specs=[pl.BlockSpec((tt, 4 * w), lambda b, i: (rv(b, i), 0)), full(1, w), full(1, w)],
        out_shape=[_sds((n, 4 * w), BF16), _sds((1, w), F32), _sds((1, w), F32)],
        scratch_shapes=[pltpu.VMEM((w, w), F32), pltpu.VMEM((nsub, w, w), F32)] + [pltpu.VMEM((tt, w), F32)] * 8,
        compiler_params=_params(),
    )(dmix, proj, proj, proj, proj, o_h, s0, gh, lb, ones_b)


def _pool_bwd(dmix, proj, wbd, wbd_t, scale, nb, name):
    n = proj.shape[0]
    t = n // nb
    tt = min(256, t)
    nt = t // tt
    nhb = tt // POOL_HALO
    cu, cg, cm = C_UB // POOL_W, C_GB // POOL_W, HGRN_W // POOL_W

    def body(u_ref, g_ref, h_ref, dm_ref, gn_ref, dmn_ref, w_ref, wt_ref, s_ref, dp_ref, dw_ref, ds_ref):
        i = pl.program_id(1)
        first = jnp.logical_and(pl.program_id(0) == 0, i == 0)

        @pl.when(first)
        def _():
            dw_ref[...] = jnp.zeros_like(dw_ref)
            ds_ref[...] = jnp.zeros_like(ds_ref)

        sc = s_ref[...]
        halo = jnp.where(i == 0, 0.0, h_ref[...])
        pooled, cnt = _pool_mix(u_ref[...], halo, i * tt, tt)
        pb = pooled.astype(BF16)
        pre = _dot(pb, w_ref[...])
        gv, dm = g_ref[...], dm_ref[...]
        sg = _sigmoid(gv)
        silu = gv * sg
        dgb = dm * pre * sc * (sg * (1.0 + gv * (1.0 - sg)))
        ds_ref[...] += jnp.sum(dm * pre * silu, axis=0, keepdims=True)
        dpre = (dm * sc * silu).astype(BF16)
        dw_ref[...] += _dot_tn(pb, dpre)
        dpool = _dot(dpre, wt_ref[...])
        gn = gn_ref[...]
        dpre_n = (dmn_ref[...] * sc * (gn * _sigmoid(gn))).astype(BF16)
        dpool_n = jnp.where(i == nt - 1, 0.0, _dot(dpre_n, wt_ref[...]))
        lane, wl = _pool_lane_windows()
        tpos_n = ((i + 1) * tt + lax.broadcasted_iota(jnp.int32, (POOL_HALO, POOL_W), 0)).astype(F32)
        ext = jnp.concatenate([dpool / cnt, dpool_n / jnp.minimum(tpos_n + 1.0, wl)], axis=0)
        rows = tt + POOL_HALO
        sums, cur, shift = [], ext, 1
        for _ in POOL_WINDOWS:
            cur = cur + pltpu.roll(cur, rows - shift, axis=0)
            sums.append(cur[:tt, :])
            shift *= 2
        du = _pool_select(lane, sums) - dpool
        dp_ref[...] = jnp.concatenate([du, dgb], axis=1).astype(BF16)

    def nxt(b, i):
        return jnp.minimum((b * nt + i + 1) * nhb, n // POOL_HALO - 1)

    return _pc(
        body, name, grid=(nb, nt),
        in_specs=_pool_specs(tt, nt, nhb) + [
            pl.BlockSpec((tt, POOL_W), lambda b, i: (b * nt + i, cm)),
            pl.BlockSpec((POOL_HALO, POOL_W), lambda b, i: (nxt(b, i), cg)),
            pl.BlockSpec((POOL_HALO, POOL_W), lambda b, i: (nxt(b, i), cm)),
            pl.BlockSpec((POOL_W, POOL_W), lambda b, i: (0, 0)), pl.BlockSpec((POOL_W, POOL_W), lambda b, i: (0, 0)),
            pl.BlockSpec((1, POOL_W), lambda b, i: (0, 0))],
        out_specs=[pl.BlockSpec((tt, 2 * POOL_W), lambda b, i: (b * nt + i, 0)),
                   pl.BlockSpec((POOL_W, POOL_W), lambda b, i: (0, 0)), pl.BlockSpec((1, POOL_W), lambda b, i: (0, 0))],
        out_shape=[_sds((n, 2 * POOL_W), BF16), _sds((POOL_W, POOL_W), F32), _sds((1, POOL_W), F32)],
        compiler_params=_params(),
    )(proj, proj, proj, dmix, proj, dmix, wbd, wbd_t, scale)


def _fox_gate_bwd(dmix, proj, o_c, name):
    n = proj.shape[0]
    tm = min(512, n)

    def body(dm_ref, gc_ref, oc_ref, da_ref, dg_ref, dl_ref):
        dm, gc, oc = dm_ref[...], gc_ref[...], oc_ref[...]
        sg = _sigmoid(gc)
        da = dm * (gc * sg)
        da_ref[...] = da.astype(BF16)
        dg_ref[...] = (dm * oc * (sg * (1.0 + gc * (1.0 - sg)))).astype(BF16)
        r = lax.broadcasted_iota(jnp.int32, (FOX_W, FC_PAD), 0) // HEAD
        c = lax.broadcasted_iota(jnp.int32, (FOX_W, FC_PAD), 1)
        dl_ref[...] = _dot_sel(da * oc, (r == c).astype(F32))

    def row(wd, j=0):
        return pl.BlockSpec((tm, wd), lambda i: (i, j))

    return _pc(
        body, name, grid=(n // tm,),
        in_specs=[row(FOX_W, (HGRN_W + POOL_W) // FOX_W), row(FOX_W, C_GC // FOX_W), row(FOX_W)],
        out_specs=[row(FOX_W), row(FOX_W), row(FC_PAD)],
        out_shape=[_sds((n, FOX_W), BF16), _sds((n, FOX_W), BF16), _sds((n, FC_PAD), F32)],
        compiler_params=_params(),
    )(dmix, proj, o_c)


def _fox_bwd(proj, da, c_col, c_row, lse_row, delta_row, nb, name, rider=None):
    n = proj.shape[0]
    t = n // nb
    tb = min(256, t)
    nq = t // tb
    pw = 2 * HEAD

    def body(q_ref, k_ref, v_ref, da_ref, cc_ref, cr_ref, lse_ref, dl_ref,
             dq_ref, dk_ref, dv_ref, dck_ref, dcq_ref, dq_s, dk_s, dv_s, dck_s, dcq_s):
        kj, qi = pl.program_id(1), pl.program_id(2)

        @pl.when(jnp.logical_and(kj == 0, qi == 0))
        def _():
            dq_s[...] = jnp.zeros_like(dq_s)
            dcq_s[...] = jnp.zeros_like(dcq_s)

        @pl.when(qi == 0)
        def _():
            dk_s[...] = jnp.zeros_like(dk_s)
            dv_s[...] = jnp.zeros_like(dv_s)
            dck_s[...] = jnp.zeros_like(dck_s)

        def block(masked):
            lo = _lane_lo()
            if masked:
                causal = lax.broadcasted_iota(jnp.int32, (tb, tb), 1) >= lax.broadcasted_iota(jnp.int32, (tb, tb), 0)
            dck = dck_s[...]
            for p in range(FOX_HEADS // 2):
                sl = slice(p * pw, (p + 1) * pw)
                qp = q_ref[:, sl] * FOX_SCALE
                kf = k_ref[:, sl]
                kp = kf.astype(BF16)
                kt = kf.T.astype(BF16)
                vp = v_ref[:, sl].astype(BF16)
                dap = da_ref[:, sl]
                dk, dv = dk_s[:, sl], dv_s[:, sl]
                for h in range(2):
                    hh = 2 * p + h
                    lm = lo if h == 0 else jnp.logical_not(lo)
                    qm = jnp.where(lm, qp, 0.0).astype(BF16)
                    dam = jnp.where(lm, dap, jnp.zeros_like(dap))
                    s = _dot_nt(kp, qm) + (cr_ref[hh:hh + 1, :] - cc_ref[:, hh:hh + 1])
                    pe = jnp.exp(s - lse_ref[hh:hh + 1, :])
                    if masked:
                        pe = jnp.where(causal, pe, 0.0)
                    dp = _dot_nt(vp, dam)
                    ds = pe * (dp - dl_ref[hh:hh + 1, :])
                    dsb = ds.astype(BF16)
                    dv = dv + _dot(pe.astype(BF16), dam)
                    dk = dk + _dot(dsb, qm)
                    rows = slice(hh * HEAD, (hh + 1) * HEAD)
                    dq_s[qi, rows, :] += _dot(kt[h * HEAD:(h + 1) * HEAD, :], dsb)
                    dck = dck - _put_col(jnp.zeros_like(dck), hh, jnp.sum(ds, axis=1, keepdims=True))
                    dcq_s[qi, hh:hh + 1, :] += _rows_reduce(ds, jnp.add, jnp.sum)
                dk_s[:, sl] = dk
                dv_s[:, sl] = dv
            dck_s[...] = dck

        @pl.when(qi > kj)
        def _():
            block(False)

        @pl.when(qi == kj)
        def _():
            block(True)

        @pl.when(qi == nq - 1)
        def _():
            dk_ref[...] = dk_s[...].astype(BF16)
            dv_ref[...] = dv_s[...].astype(BF16)
            dck_ref[...] = dck_s[...]

        @pl.when(jnp.logical_and(kj == nq - 1, qi == nq - 1))
        def _():
            for j in range(nq):
                dq_ref[j * tb:(j + 1) * tb, :] = (dq_s[j].T * FOX_SCALE).astype(BF16)
                dcq_ref[:, j * tb:(j + 1) * tb] = dcq_s[j]

    def kspec(wd, j=0):
        return pl.BlockSpec((tb, wd), lambda b, kj, qi: (b * nq + kj, j))

    def qspec(wd, j=0):
        return pl.BlockSpec((tb, wd), lambda b, kj, qi: (b * nq + jnp.maximum(qi, kj), j))

    def qrow():
        return pl.BlockSpec((None, FOX_HEADS, tb), lambda b, kj, qi: (b, 0, jnp.maximum(qi, kj)))

    return _call(
        body, name, (proj, proj, proj, da, c_col, c_row, lse_row, delta_row), rider, grid=(nb, nq, nq),
        in_specs=[qspec(FOX_W, C_QC // FOX_W), kspec(FOX_W, C_KC // FOX_W), kspec(FOX_W, C_VC // FOX_W), qspec(FOX_W),
                  kspec(FC_PAD), qrow(), qrow(), qrow()],
        out_specs=[pl.BlockSpec((t, FOX_W), lambda b, kj, qi: (b, 0)), kspec(FOX_W), kspec(FOX_W), kspec(FC_PAD),
                   pl.BlockSpec((None, FOX_HEADS, t), lambda b, kj, qi: (b, 0, 0))],
        out_shape=[_sds((n, FOX_W), BF16), _sds((n, FOX_W), BF16), _sds((n, FOX_W), BF16), _sds((n, FC_PAD), F32),
                   _sds((nb, FOX_HEADS, t), F32)],
        scratch_shapes=[pltpu.VMEM((nq, FOX_W, tb), F32), pltpu.VMEM((tb, FOX_W), F32), pltpu.VMEM((tb, FOX_W), F32),
                        pltpu.VMEM((tb, FC_PAD), F32), pltpu.VMEM((nq, FOX_HEADS, tb), F32)],
        compiler_params=_params(),
    )


def _fox_decay_bwd(dc_q, dc_k, fc, bias, nb, name):
    n = fc.shape[0]
    t = n // nb
    tt = min(256, t)
    nt = t // tt

    def body(dcq_ref, dck_ref, fc_ref, b_ref, dfc_ref, db_ref, carry):
        i = pl.program_id(1)
        first = jnp.logical_and(pl.program_id(0) == 0, i == 0)

        @pl.when(first)
        def _():
            db_ref[...] = jnp.zeros_like(db_ref)

        @pl.when(i == 0)
        def _():
            carry[...] = jnp.zeros_like(carry)

        r = lax.broadcasted_iota(jnp.int32, (tt, tt), 0)
        cc = lax.broadcasted_iota(jnp.int32, (tt, tt), 1)
        dlf = _dot_hi((r <= cc).astype(F32), dcq_ref[...] + dck_ref[...]) + carry[...]
        carry[...] = dlf[0:1, :]
        dfc = dlf * _sigmoid(-(fc_ref[...] + b_ref[...]))
        dfc_ref[...] = dfc.astype(BF16)
        db_ref[...] += jnp.sum(dfc, axis=0, keepdims=True)

    def row():
        return pl.BlockSpec((tt, FC_PAD), lambda b, i: (b * nt + (nt - 1 - i), 0))

    return _pc(
        body, name, grid=(nb, nt),
        in_specs=[row(), row(), row(), pl.BlockSpec((1, FC_PAD), lambda b, i: (0, 0))],
        out_specs=[row(), pl.BlockSpec((1, FC_PAD), lambda b, i: (0, 0))],
        out_shape=[_sds((n, FC_PAD), BF16), _sds((1, FC_PAD), F32)],
        scratch_shapes=[pltpu.VMEM((1, FC_PAD), F32)],
        compiler_params=_params(),
    )(dc_q, dc_k, fc, bias)


def _in_proj_bwd(pieces, w_main_t, w_fc_t, x, g_pre, dxo, name, rider=None):
    n, d = x.shape
    tm = min(512, n)
    widths = [p.shape[1] for p, _ in pieces]
    offs = [o for _, o in pieces]
    np_ = len(pieces)

    def body(*refs):
        p_refs = refs[:np_]
        wt_ref, wf_ref, x_ref, g_ref, dxo_ref, dx_ref, dg_ref = refs[np_:]

        @pl.when(pl.program_id(0) == 0)
        def _():
            dg_ref[...] = jnp.zeros_like(dg_ref)

        dh = _dot(p_refs[-1][...], wf_ref[...])
        for pr, wd, off in zip(p_refs[:-1], widths[:-1], offs[:-1]):
            for j in range(0, wd, 512):
                jw = min(512, wd - j)
                dh = dh + _dot(pr[:, j:j + jw], wt_ref[off + j:off + j + jw, :])
        xv = x_ref[...]
        r = lax.rsqrt(jnp.mean(xv * xv, axis=-1, keepdims=True) + NORM_EPS)
        xh = xv * r
        dg_ref[...] += jnp.sum(dh * xh, axis=0, keepdims=True)
        dx_ref[...] = dxo_ref[...] + _rms_bwd(dh * g_ref[...], xh, r)

    row = pl.BlockSpec((tm, d), lambda i: (i, 0))
    return _call(
        body, name, (*[p for p, _ in pieces], w_main_t, w_fc_t, x, g_pre, dxo), rider, grid=(n // tm,),
        in_specs=[pl.BlockSpec((tm, wd), lambda i: (i, 0)) for wd in widths] + [
            pl.BlockSpec((MAIN_W, d), lambda i: (0, 0)), pl.BlockSpec((FC_PAD, d), lambda i: (0, 0)),
            row, pl.BlockSpec((1, d), lambda i: (0, 0)), row],
        out_specs=[row, pl.BlockSpec((1, d), lambda i: (0, 0))],
        out_shape=[_sds((n, d), F32), _sds((1, d), F32)],
        compiler_params=_params(),
    )


def _lower_bound_table(lower_bounds, name):
    depth, w = lower_bounds.shape

    def body(lb_ref, o_ref):
        v = lb_ref[...]
        e = jnp.exp(v - jnp.max(v, axis=0, keepdims=True))
        p = e / jnp.sum(e, axis=0, keepdims=True)
        acc = jnp.zeros((1, w), F32)
        for l in range(depth):
            acc = acc + p[l:l + 1, :]
            o_ref[l:l + 1, :] = acc - p[0:1, :]

    return _pc(body, name, out_shape=_sds((depth, w), F32))(lower_bounds)


def _lower_bound_bwd(lower_bounds, dlbs, name):
    depth, w = lower_bounds.shape

    def body(lb_ref, d_ref, o_ref):
        v, dl = lb_ref[...], d_ref[...]
        e = jnp.exp(v - jnp.max(v, axis=0, keepdims=True))
        p = e / jnp.sum(e, axis=0, keepdims=True)
        tot = jnp.sum(dl, axis=0, keepdims=True)
        rows, tail = [], tot
        for l in range(depth):
            rows.append(tail - tot if l == 0 else tail)
            tail = tail - dl[l:l + 1, :]
        dp = jnp.concatenate(rows, axis=0)
        o_ref[...] = p * (dp - jnp.sum(p * dp, axis=0, keepdims=True))

    return _pc(body, name, out_shape=_sds((depth, w), F32))(lower_bounds, dlbs)


def _place():
    x, y, c = lax.axis_index("x"), lax.axis_index("y"), lax.axis_index("c")
    return x, y, c


def _gather_weights(w_in_b, w_out_b):
    arrays = (w_in_b, w_out_b)
    na = len(arrays)

    def body(*refs):
        ins, outs = refs[:na], refs[na:2 * na]
        send_sems, recv_sems, local_sems = refs[2 * na:]
        x, y, c = _place()
        me, sibling = (x, y, c), (x, y, 1 - c)
        chips = [(1 - x, y), (x, 1 - y), (1 - x, 1 - y)]

        def slot(a, px, py, pc):
            return outs[a].at[4 * px + 2 * py + pc]

        def copy(a, k, block, to, own=False):
            return pltpu.make_async_remote_copy(
                src_ref=ins[a] if own else slot(a, *block), dst_ref=slot(a, *block),
                send_sem=send_sems.at[a * 7 + k], recv_sem=recv_sems.at[a * 7 + k],
                device_id=to, device_id_type=MESH)

        mine = [pltpu.make_async_copy(ins[a], slot(a, *me), local_sems.at[a]) for a in range(na)]
        for cp in mine:
            cp.start()
        first = []
        for a in range(na):
            first.append(copy(a, 0, me, sibling, own=True))
            first += [copy(a, 1 + j, me, (*chip, c), own=True) for j, chip in enumerate(chips)]
        for cp in first:
            cp.start()
        passed = []
        for j, chip in enumerate(chips):
            for a in range(na):
                copy(a, 1 + j, (*chip, c), me).wait_recv()
                fw = copy(a, 4 + j, (*chip, c), sibling)
                fw.start()
                passed.append(fw)
        for a in range(na):
            copy(a, 0, sibling, me).wait_recv()
            for j, chip in enumerate(chips):
                copy(a, 4 + j, (*chip, 1 - c), me).wait_recv()
        for cp in first + passed:
            cp.wait_send()
        for cp in mine:
            cp.wait()

    any_spec = pl.BlockSpec(memory_space=pl.ANY)
    return _pc(
        body, "gather_weights",
        in_specs=[any_spec] * na, out_specs=[any_spec] * na,
        out_shape=[_sds((N_DEV,) + a.shape, a.dtype) for a in arrays],
        scratch_shapes=[pltpu.SemaphoreType.DMA((7 * na,)), pltpu.SemaphoreType.DMA((7 * na,)),
                        pltpu.SemaphoreType.DMA((na,))],
    )(*arrays)


def _peer(k):
    x, y, c = _place()
    return (1 - x if k & 4 else x, 1 - y if k & 2 else y, 1 - c if k & 1 else c)


def _remote(src, dst, sems, s, to):
    return pltpu.make_async_remote_copy(src_ref=src, dst_ref=dst, send_sem=sems[0].at[s], recv_sem=sems[1].at[s],
                                        device_id=to, device_id_type=MESH)


def _gather_rider(shards):
    na = len(shards)

    def plan(ins, outs, *sems):
        x, y, c = _place()
        me = 4 * x + 2 * y + c
        locs = [pltpu.make_async_copy(ins[a], outs[a].at[me], sems[2].at[a]) for a in range(na)]
        sends, recvs = [], []
        for k in range(1, N_DEV):
            px, py, pc = _peer(k)
            for a in range(na):
                s = (k - 1) * na + a
                sends.append(_remote(ins[a], outs[a].at[me], sems, s, (px, py, pc)))
                recvs.append(_remote(ins[a], outs[a].at[4 * px + 2 * py + pc], sems, s, (px, py, pc)))
        return sends, recvs, locs

    return _Rider(shards, [_sds((N_DEV,) + a.shape, a.dtype) for a in shards], (N_DEV - 1) * na, na, plan)


def _direct_exchange_rider(blocks):
    na = len(blocks)

    def plan(ins, outs, *sems):
        x, y, c = _place()
        me = 4 * x + 2 * y + c
        locs = [pltpu.make_async_copy(ins[a].at[c, 2 * x + y], outs[a].at[me], sems[2].at[a]) for a in range(na)]
        sends, recvs = [], []
        for k in range(1, N_DEV):
            px, py, pc = _peer(k)
            for a in range(na):
                s = (k - 1) * na + a
                sends.append(_remote(ins[a].at[pc, 2 * px + py], outs[a].at[me], sems, s, (px, py, pc)))
                recvs.append(_remote(ins[a].at[pc, 2 * px + py], outs[a].at[4 * px + 2 * py + pc], sems, s, (px, py, pc)))
        return sends, recvs, locs

    return _Rider(blocks, [_sds((N_DEV,) + a.shape[2:], a.dtype) for a in blocks], (N_DEV - 1) * na, na, plan)


def _swap_rider(halves):
    na = len(halves)

    def plan(ins, outs, *sems):
        x, y, c = _place()
        cps = [_remote(ins[a].at[1 - c], outs[a], sems, a, (x, y, 1 - c)) for a in range(na)]
        return cps, cps, []

    return _Rider(halves, [_sds(a.shape[1:], a.dtype) for a in halves], na, 1, plan)


def _chip_exchange_rider(parts, small=None):
    na = len(parts)
    n_chip = N_DEV // 2

    def plan(ins, outs, *sems):
        x, y, c = _place()
        chip = 2 * x + y
        locs = [pltpu.make_async_copy(ins[a].at[chip], outs[a].at[chip], sems[2].at[a]) for a in range(na)]
        sends, recvs = [], []
        for k in range(1, n_chip):
            px, py, _ = _peer(2 * k)
            for a in range(na):
                s = (k - 1) * na + a
                sends.append(_remote(ins[a].at[2 * px + py], outs[a].at[chip], sems, s, (px, py, c)))
                recvs.append(_remote(ins[a].at[2 * px + py], outs[a].at[2 * px + py], sems, s, (px, py, c)))
        if small is not None:
            me = 2 * chip + c
            locs.append(pltpu.make_async_copy(ins[na], outs[na].at[me], sems[2].at[na]))
            for k in range(1, N_DEV):
                px, py, pc = _peer(k)
                s = (n_chip - 1) * na + k - 1
                sends.append(_remote(ins[na], outs[na].at[me], sems, s, (px, py, pc)))
                recvs.append(_remote(ins[na], outs[na].at[4 * px + 2 * py + pc], sems, s, (px, py, pc)))
        return sends, recvs, locs

    extra = [] if small is None else [small]
    shapes = [_sds(a.shape, a.dtype) for a in parts] + [_sds((N_DEV,) + s.shape, s.dtype) for s in extra]
    n_sems = (n_chip - 1) * na + (N_DEV - 1) * len(extra)
    return _Rider(list(parts) + extra, shapes, n_sems, na + len(extra), plan)


def _pair_add(halves, other, core, name):
    _, nch, r, c = halves.shape
    tr = 256 if r % 256 == 0 else r

    def body(h_ref, o_ref, c_ref, p_ref):
        mine = jnp.where(c_ref[...] == 0, h_ref[0].astype(F32), h_ref[1].astype(F32))
        p_ref[...] = (mine + o_ref[...].astype(F32)).astype(BF16)

    blk = pl.BlockSpec((None, tr, c), lambda j, i: (j, i, 0))
    return _pc(
        body, name, grid=(nch, r // tr),
        in_specs=[pl.BlockSpec((2, None, tr, c), lambda j, i: (0, j, i, 0)), blk, pl.BlockSpec((1, 1), lambda j, i: (0, 0))],
        out_specs=blk, out_shape=_sds((nch, r, c), BF16),
        compiler_params=_params(),
    )(halves, other, core)


def _sum_adamw(parts, w, m, v, name, rider=None):
    nl, r, c = w.shape
    tr = 256 if r % 256 == 0 else r

    def body(*refs):
        p_refs = refs[:nl]
        w_ref, m_ref, v_ref, g_ref, d_ref, mo_ref, vo_ref = refs[nl:]
        for l in range(nl):
            @pl.when(pl.program_id(0) == l)
            def _(p_ref=p_refs[l]):
                g = p_ref[0].astype(F32)
                for j in range(1, p_ref.shape[0]):
                    g = g + p_ref[j].astype(F32)
                mn = ADAM_B1 * m_ref[...] + (1.0 - ADAM_B1) * g
                vn = ADAM_B2 * v_ref[...] + (1.0 - ADAM_B2) * (g * g)
                m_hat = mn / (1.0 - ADAM_B1 ** ADAM_STEP)
                v_hat = vn / (1.0 - ADAM_B2 ** ADAM_STEP)
                g_ref[...] = g
                d_ref[...] = -ADAM_LR * (m_hat / (jnp.sqrt(v_hat) + ADAM_EPS) + ADAM_WD * w_ref[...])
                mo_ref[...] = mn
                vo_ref[...] = vn

    def part_spec(l, k):
        return pl.BlockSpec((k, tr, c), lambda li, i: (0, jnp.where(li == l, i, 0), 0))

    row = pl.BlockSpec((None, tr, c), lambda li, i: (li, i, 0))
    return _call(
        body, name, (*parts, w, m, v), rider, grid=(nl, r // tr),
        in_specs=[part_spec(l, p.shape[0]) for l, p in enumerate(parts)] + [row, row, row],
        out_specs=[row] * 4,
        out_shape=[_sds((nl, r, c), F32)] * 4,
        compiler_params=_params(),
    )


SMALL = ("lower_bounds", "pre_norm_g", "hgrn_norm_g", "fox_f_bias", "pool_w", "pool_scale", "post_norm_g")
SMALL_LANES = 128


def _small_size(tree):
    return sum(tree[k].size for k in SMALL)


def _pack_small(tree, extra=None):
    flat = jnp.concatenate([tree[k].reshape(-1) for k in SMALL] + ([] if extra is None else [extra.reshape(1)]))
    rows = -(-(_small_size(tree) + 1) // (8 * SMALL_LANES)) * 8
    return jnp.pad(flat, (0, rows * SMALL_LANES - flat.shape[0])).reshape(rows, SMALL_LANES)


def _unpack_small(packed, like):
    flat, out, off = packed.reshape(-1), {}, 0
    for k in SMALL:
        size = like[k].size
        out[k] = flat[off:off + size].reshape(like[k].shape)
        off += size
    return out


def _block_diag(pw):
    g = pw.shape[0]
    eye = jnp.eye(g, dtype=pw.dtype)
    return (eye[:, None, :, None] * pw[:, :, None, :]).reshape(g * HEAD, g * HEAD)


def _assemble_w_in(g_in, name):
    _, d, shard = g_in.shape
    tr = min(256, d)
    wide = MAIN_W + FC_PAD

    def body(g_ref, wm_ref, wf_ref, wmt_ref, wft_ref, row_s):
        row_s[:, MAIN_W:] = jnp.zeros((tr, FC_PAD), F32)
        for j in range(N_DEV):
            row_s[:, j * shard:(j + 1) * shard] = g_ref[j].astype(F32)
        wm_ref[...] = row_s[:, :MAIN_W].astype(BF16)
        wf_ref[...] = row_s[:, MAIN_W:].astype(BF16)
        for j in range(0, MAIN_W, 512):
            wmt_ref[j:j + 512, :] = row_s[:, j:j + 512].T.astype(BF16)
        wft_ref[...] = row_s[:, MAIN_W:].T.astype(BF16)

    return _pc(
        body, name, grid=(d // tr,),
        in_specs=[pl.BlockSpec((N_DEV, tr, shard), lambda i: (0, i, 0))],
        out_specs=[pl.BlockSpec((tr, MAIN_W), lambda i: (i, 0)), pl.BlockSpec((tr, FC_PAD), lambda i: (i, 0)),
                   pl.BlockSpec((MAIN_W, tr), lambda i: (0, i)), pl.BlockSpec((FC_PAD, tr), lambda i: (0, i))],
        out_shape=[_sds((d, MAIN_W), BF16), _sds((d, FC_PAD), BF16), _sds((MAIN_W, d), BF16), _sds((FC_PAD, d), BF16)],
        scratch_shapes=[pltpu.VMEM((tr, wide), F32)],
        compiler_params=_params(),
    )(g_in)


def _layer_weights(l, g_in, g_out):
    d = g_in.shape[1]
    w_main, w_fc, w_main_t, w_fc_t = _assemble_w_in(g_in, f"assemble_w_in_{l}")
    full_out = g_out.reshape(N_DEV * g_out.shape[1], d)
    return w_main, w_fc, w_main_t, w_fc_t, full_out, full_out.T


def _layer_fwd(l, x, lbs, weights, lw, nb, rider=None, target=None):
    n = x.shape[0]
    t = n // nb
    w_main, w_fc, _, _, w_out, _ = lw
    bias = jnp.pad(weights["fox_f_bias"][l:l + 1], ((0, 0), (0, FC_PAD - FOX_HEADS)))
    wbd = _block_diag(weights["pool_w"][l]).astype(BF16)
    proj, fc, ht = _in_proj_fwd(x, weights["pre_norm_g"][l:l + 1], w_main, w_fc, f"in_proj_fwd_{l}")
    c_col = _fox_decay_fwd(fc, bias, nb, f"fox_decay_fwd_{l}")
    c_row = c_col.reshape(nb, t, FC_PAD)[:, :, :FOX_HEADS].transpose(0, 2, 1)
    o_h, s0 = _hgrn_fwd(proj, lbs[l:l + 1], _block_ones(HGRN_W, BF16), nb, f"hgrn_fwd_{l}")
    o_b = _pool_fwd(proj, wbd, weights["pool_scale"][l:l + 1], nb, f"pool_fwd_{l}")
    (o_c, lse), rode = _fox_fwd(proj, c_col, c_row, nb, f"fox_fwd_{l}", rider)
    x_next, mixt, y = _merge_fwd(x, proj, o_h, o_b, o_c, weights["hgrn_norm_g"][l:l + 1], w_out,
                                 weights["post_norm_g"][l:l + 1], f"merge_fwd_{l}", target)
    return x_next, (x, proj, fc, ht, c_col, c_row, o_h, s0, o_c, lse, mixt, y, bias, wbd), rode


def _layer_bwd(l, dx, saved, lbs, weights, lw, nb, rider=None):
    _, proj, fc, ht, c_col, c_row, o_h, s0, o_c, lse, mixt, y, bias, wbd = saved
    n = proj.shape[0]
    t = n // nb
    w_out_t = lw[5]
    g = {}
    dy, dmix, dgp = _merge_bwd(dx, y, weights["post_norm_g"][l:l + 1], w_out_t, f"merge_bwd_{l}")
    g["post_norm_g"] = dgp[0]
    g["w_out"] = _w_out_grad(mixt, dy, f"w_out_grad_{l}")
    d_a, dgh, dlb = _hgrn_bwd(dmix, proj, o_h, s0, weights["hgrn_norm_g"][l:l + 1], lbs[l:l + 1],
                              _block_ones(HGRN_W, BF16), nb, f"hgrn_bwd_{l}")
    g["hgrn_norm_g"], g["lbs"] = dgh[0], dlb[0]
    d_b, dwbd, dps = _pool_bwd(dmix, proj, wbd, wbd.T, weights["pool_scale"][l:l + 1], nb, f"pool_bwd_{l}")
    g["pool_w"] = jnp.stack([dwbd[j * HEAD:(j + 1) * HEAD, j * HEAD:(j + 1) * HEAD] for j in range(len(POOL_WINDOWS))])
    g["pool_scale"] = dps[0]
    da, d_gc, delta = _fox_gate_bwd(dmix, proj, o_c, f"fox_gate_bwd_{l}")
    delta_row, lse_row = [a.reshape(nb, t, FC_PAD)[:, :, :FOX_HEADS].transpose(0, 2, 1) for a in (delta, lse)]
    (d_qc, d_kc, d_vc, dc_k, dc_q), rode = _fox_bwd(proj, da, c_col, c_row, lse_row, delta_row, nb, f"fox_bwd_{l}", rider)
    dc_q = jnp.pad(dc_q.transpose(0, 2, 1).reshape(n, FOX_HEADS), ((0, 0), (0, FC_PAD - FOX_HEADS)))
    d_fc, dbias = _fox_decay_bwd(dc_q, dc_k, fc, bias, nb, f"fox_decay_bwd_{l}")
    g["fox_f_bias"] = dbias[0, :FOX_HEADS]
    pieces = [(d_a, C_QA), (d_b, C_UB), (d_qc, C_QC), (d_kc, C_KC), (d_vc, C_VC), (d_gc, C_GC), (d_fc, None)]
    g["w_in"] = _w_in_grad(ht, pieces, f"w_in_grad_{l}")
    return g, pieces, rode


def _layer_bwd_input(l, dx, pieces, saved, weights, lw, rider=None):
    (dxi, dgpre), rode = _in_proj_bwd(pieces, lw[2], lw[3], saved[0], weights["pre_norm_g"][l:l + 1], dx,
                                      f"in_proj_bwd_{l}", rider)
    return dxi, dgpre[0], rode


def kernel(x, lower_bounds, pre_norm_g, w_in, hgrn_norm_g, fox_f_bias, pool_w, pool_scale, w_out, post_norm_g, loss_target, m_lower_bounds, m_pre_norm_g, m_w_in, m_hgrn_norm_g, m_fox_f_bias, m_pool_w, m_pool_scale, m_w_out, m_post_norm_g, v_lower_bounds, v_pre_norm_g, v_w_in, v_hgrn_norm_g, v_fox_f_bias, v_pool_w, v_pool_scale, v_w_out, v_post_norm_g):
    weights = dict(lower_bounds=lower_bounds, pre_norm_g=pre_norm_g, hgrn_norm_g=hgrn_norm_g, fox_f_bias=fox_f_bias,
                   pool_w=pool_w, pool_scale=pool_scale, post_norm_g=post_norm_g)
    mom_m = dict(lower_bounds=m_lower_bounds, pre_norm_g=m_pre_norm_g, hgrn_norm_g=m_hgrn_norm_g, fox_f_bias=m_fox_f_bias,
                 pool_w=m_pool_w, pool_scale=m_pool_scale, post_norm_g=m_post_norm_g)
    mom_v = dict(lower_bounds=v_lower_bounds, pre_norm_g=v_pre_norm_g, hgrn_norm_g=v_hgrn_norm_g, fox_f_bias=v_fox_f_bias,
                 pool_w=v_pool_w, pool_scale=v_pool_scale, post_norm_g=v_post_norm_g)
    depth = w_in.shape[0]
    nb, t, d = x.shape
    n = nb * t
    core = lax.axis_index("c").astype(jnp.int32).reshape(1, 1)
    shards = [(w_in[l].astype(BF16), w_out[l].astype(BF16)) for l in range(depth)]
    lbs = _lower_bound_table(lower_bounds, "lower_bound_table")

    lw = [_layer_weights(0, *_gather_weights(*shards[0]))]
    xl, saved = x.reshape(n, d), []
    for l in range(depth):
        last = l + 1 == depth
        rider = None if last else _gather_rider(shards[l + 1])
        xl, sv, rode = _layer_fwd(l, xl, lbs, weights, lw[l], nb, rider, loss_target.reshape(n, d) if last else None)
        saved.append(sv)
        if rode is not None:
            lw.append(_layer_weights(l + 1, *rode))
    dx, sq = xl
    loss_here = 0.5 * jnp.sum(sq) / d

    grads, recv, pending = [None] * depth, [None] * depth, None
    for l in reversed(range(depth)):
        g, pieces, rode = _layer_bwd(l, dx, saved[l], lbs, weights, lw[l], nb, pending)
        if rode is not None:
            recv[l + 1] = rode
        blocks = (g["w_in"], g["w_out"])
        if l > 0:
            pending = _direct_exchange_rider(blocks)
            dx, g["pre_norm_g"], _ = _layer_bwd_input(l, dx, pieces, saved[l], weights, lw[l])
        else:
            other = _run_rider(_swap_rider(blocks), "grad_swap")
            summed = [_pair_add(hv, ot, core, f"grad_pair_add_{i}") for i, (hv, ot) in enumerate(zip(blocks, other))]
            dx, g["pre_norm_g"], recv[l] = _layer_bwd_input(l, dx, pieces, saved[l], weights, lw[l],
                                                            _chip_exchange_rider(summed))
        grads[l] = g
    small = {k: jnp.stack([grads[l][k] for l in range(depth)]) for k in SMALL if k != "lower_bounds"}
    small["lower_bounds"] = _lower_bound_bwd(lower_bounds, jnp.stack([grads[l]["lbs"] for l in range(depth)]),
                                             "lower_bound_bwd")
    (r_small,) = _run_rider(_gather_rider([_pack_small(small, loss_here)]), "small_grads_gather")

    res_in, _ = _sum_adamw([recv[l][0] for l in range(depth)], w_in, m_w_in, v_w_in, "adamw_w_in")
    res_out, _ = _sum_adamw([recv[l][1] for l in range(depth)], w_out, m_w_out, v_w_out, "adamw_w_out")
    res_small, _ = _sum_adamw([r_small], _pack_small(weights)[None], _pack_small(mom_m)[None], _pack_small(mom_v)[None],
                              "adamw_small")
    loss = res_small[0][0].reshape(-1)[_small_size(weights)]

    names = ("lower_bounds", "pre_norm_g", "w_in", "hgrn_norm_g", "fox_f_bias", "pool_w", "pool_scale", "w_out", "post_norm_g")
    outs = [loss, dx.reshape(nb, t, d)]
    for i in range(4):
        full = dict(_unpack_small(res_small[i][0], weights), w_in=res_in[i], w_out=res_out[i])
        outs += [full[k] for k in names]
    return tuple(outs)
```

```python
import functools

import jax
import jax.numpy as jnp
from jax import lax
from jax.experimental import pallas as pl
from jax.experimental.pallas import tpu as pltpu

F32, BF16 = jnp.float32, jnp.bfloat16
HI = lax.Precision.HIGHEST
MESH = pl.DeviceIdType.MESH
AXES = ("x", "y", "c")
N_DEV = 8

NORM_EPS = 1e-6
MASK_VALUE = -1e30
TINY = 1e-30
CHUNK = 64
SUB = 16
HGRN_W, POOL_W, FOX_W = 256, 256, 512
HEAD = 64
FOX_HEADS = 8
POOL_WINDOWS = (2, 4, 8, 16)
POOL_HALO = 16
MAIN_W = 3584
FC_PAD = 128
C_QA, C_FA, C_IA, C_GA, C_UB, C_GB, C_QC, C_KC, C_VC, C_GC = 0, 256, 512, 768, 1024, 1280, 1536, 2048, 2560, 3072
FOX_SCALE = HEAD ** -0.5

ADAM_LR, ADAM_B1, ADAM_B2, ADAM_EPS, ADAM_WD, ADAM_STEP = 0.001, 0.9, 0.999, 1e-08, 0.01, 10

VMEM_LIMIT = 56 * 1024 * 1024


def _pc(fn, name, **kw):
    return pl.pallas_call(fn, name=name, **kw)


def _params(**kw):
    return pltpu.CompilerParams(vmem_limit_bytes=VMEM_LIMIT, **kw)


class _Rider:
    def __init__(self, inputs, out_shapes, n_sems, n_local, plan):
        self.inputs, self.out_shapes, self.n_sems, self.n_local, self.plan = list(inputs), list(out_shapes), n_sems, n_local, plan

    def start(self, ins, outs, *sems):
        sends, _, locs = self.plan(ins, outs, *sems)
        for cp in locs + sends:
            cp.start()

    def wait(self, ins, outs, *sems):
        sends, recvs, locs = self.plan(ins, outs, *sems)
        for cp in recvs:
            cp.wait_recv()
        for cp in sends:
            cp.wait_send()
        for cp in locs:
            cp.wait()

    def sem_shapes(self):
        return [pltpu.SemaphoreType.DMA((self.n_sems,)), pltpu.SemaphoreType.DMA((self.n_sems,)),
                pltpu.SemaphoreType.DMA((self.n_local,))]


def _call(body, name, args, rider=None, *, grid, in_specs, out_specs, out_shape, scratch_shapes=(), **kw):
    if rider is None:
        res = _pc(body, name, grid=grid, in_specs=in_specs, out_specs=out_specs, out_shape=out_shape,
                  scratch_shapes=list(scratch_shapes), **kw)(*args)
        return res, None
    n_in, n_out, n_scr = len(in_specs), len(out_specs), len(scratch_shapes)
    n_rin, n_rout = len(rider.inputs), len(rider.out_shapes)

    def ridden(*refs):
        ins, refs = refs[:n_in], refs[n_in:]
        rins, refs = refs[:n_rin], refs[n_rin:]
        outs, refs = refs[:n_out], refs[n_out:]
        routs, refs = refs[:n_rout], refs[n_rout:]
        scr, sems = refs[:n_scr], refs[n_scr:]
        first = functools.reduce(jnp.logical_and, [pl.program_id(a) == 0 for a in range(len(grid))])
        last = functools.reduce(jnp.logical_and, [pl.program_id(a) == g - 1 for a, g in enumerate(grid)])

        @pl.when(first)
        def _():
            rider.start(rins, routs, *sems)

        body(*ins, *outs, *scr)

        @pl.when(last)
        def _():
            rider.wait(rins, routs, *sems)

    any_spec = pl.BlockSpec(memory_space=pl.ANY)
    res = _pc(ridden, name, grid=grid, in_specs=list(in_specs) + [any_spec] * n_rin,
              out_specs=list(out_specs) + [any_spec] * n_rout, out_shape=list(out_shape) + rider.out_shapes,
              scratch_shapes=list(scratch_shapes) + rider.sem_shapes(), **kw)(*args, *rider.inputs)
    return res[:n_out], res[n_out:]


def _run_rider(rider, name):
    n_rin = len(rider.inputs)

    def body(*refs):
        ins, outs, sems = refs[:n_rin], refs[n_rin:n_rin + len(rider.out_shapes)], refs[n_rin + len(rider.out_shapes):]
        rider.start(ins, outs, *sems)
        rider.wait(ins, outs, *sems)

    any_spec = pl.BlockSpec(memory_space=pl.ANY)
    return _pc(body, name, in_specs=[any_spec] * n_rin, out_specs=[any_spec] * len(rider.out_shapes),
               out_shape=rider.out_shapes, scratch_shapes=rider.sem_shapes())(*rider.inputs)


def _dot(a, b):
    return jnp.dot(a, b, preferred_element_type=F32)


def _dot_nt(a, b):
    return lax.dot_general(a, b, (((1,), (1,)), ((), ())), preferred_element_type=F32)


def _dot_tn(a, b):
    return lax.dot_general(a, b, (((0,), (0,)), ((), ())), preferred_element_type=F32)


def _dot_hi(a, b):
    return jnp.dot(a, b, precision=HI, preferred_element_type=F32)


def _split2(x):
    hi = x.astype(BF16)
    return hi, (x - hi.astype(F32)).astype(BF16)


def _sel_dot(sel, x):
    hi, lo = _split2(x)
    sb = sel.astype(BF16)
    return _dot(sb, hi) + _dot(sb, lo)


def _dot_sel(x, sel):
    hi, lo = _split2(x)
    sb = sel.astype(BF16)
    return _dot(hi, sb) + _dot(lo, sb)


def _sigmoid(x):
    return 1.0 / (1.0 + jnp.exp(-x))


def _block_ones(n, dtype):
    r = lax.broadcasted_iota(jnp.int32, (n, n), 0) // HEAD
    c = lax.broadcasted_iota(jnp.int32, (n, n), 1) // HEAD
    return (r == c).astype(dtype)


def _sds(shape, dtype):
    return jax.ShapeDtypeStruct(shape, dtype)


def _in_proj_fwd(x, g_pre, w_main, w_fc, name):
    n, d = x.shape
    tm = min(512, n)

    def body(x_ref, g_ref, w_ref, wf_ref, proj_ref, fc_ref, ht_ref):
        xv = x_ref[...]
        r = lax.rsqrt(jnp.mean(xv * xv, axis=-1, keepdims=True) + NORM_EPS)
        hf = xv * r * g_ref[...]
        hb = hf.astype(BF16)
        ht_ref[...] = hf.T.astype(BF16)
        for j in range(0, MAIN_W, 512):
            proj_ref[:, j:j + 512] = _dot(hb, w_ref[:, j:j + 512])
        fc_ref[...] = _dot(hb, wf_ref[...])

    return _pc(
        body, name, grid=(n // tm,),
        in_specs=[pl.BlockSpec((tm, d), lambda i: (i, 0)), pl.BlockSpec((1, d), lambda i: (0, 0)),
                  pl.BlockSpec((d, MAIN_W), lambda i: (0, 0)), pl.BlockSpec((d, FC_PAD), lambda i: (0, 0))],
        out_specs=[pl.BlockSpec((tm, MAIN_W), lambda i: (i, 0)), pl.BlockSpec((tm, FC_PAD), lambda i: (i, 0)),
                   pl.BlockSpec((d, tm), lambda i: (0, i))],
        out_shape=[_sds((n, MAIN_W), F32), _sds((n, FC_PAD), F32), _sds((d, n), BF16)],
        compiler_params=_params(),
    )(x, g_pre, w_main, w_fc)


def _fox_decay_fwd(fc, bias, nb, name):
    n = fc.shape[0]
    t = n // nb
    tt = min(256, t)
    nt = t // tt

    def body(fc_ref, b_ref, c_ref, carry):
        i = pl.program_id(1)

        @pl.when(i == 0)
        def _():
            carry[...] = jnp.zeros_like(carry)

        xv = fc_ref[...] + b_ref[...]
        lf = jnp.minimum(xv, 0.0) - jnp.log(1.0 + jnp.exp(-jnp.abs(xv)))
        r = lax.broadcasted_iota(jnp.int32, (tt, tt), 0)
        cc = lax.broadcasted_iota(jnp.int32, (tt, tt), 1)
        cs = _dot_hi((r >= cc).astype(F32), lf) + carry[...]
        c_ref[...] = cs
        carry[...] = cs[tt - 1:tt, :]

    return _pc(
        body, name, grid=(nb, nt),
        in_specs=[pl.BlockSpec((tt, FC_PAD), lambda b, i: (b * nt + i, 0)), pl.BlockSpec((1, FC_PAD), lambda b, i: (0, 0))],
        out_specs=pl.BlockSpec((tt, FC_PAD), lambda b, i: (b * nt + i, 0)),
        out_shape=_sds((n, FC_PAD), F32),
        scratch_shapes=[pltpu.VMEM((1, FC_PAD), F32)],
        compiler_params=_params(),
    )(fc, bias)


def _hgrn_gates(q, z, lb):
    sig = _sigmoid(z)
    sn = _sigmoid(-z)
    f = lb + (1.0 - lb) * sig
    g = jnp.log(jnp.maximum(f, TINY))
    k = (1.0 - lb) * sn
    sq = _sigmoid(q)
    return sig, sn, f, g, k, sq


def _sub_tri(n, lower):
    r = lax.broadcasted_iota(jnp.int32, (n, n), 0)
    c = lax.broadcasted_iota(jnp.int32, (n, n), 1)
    tri = (r >= c) if lower else (r <= c)
    return jnp.logical_and(r // SUB == c // SUB, tri).astype(F32)


def _hgrn_decays(qs, k, b):
    srow = lax.broadcasted_iota(jnp.int32, (SUB, HGRN_W), 0)
    es, ws = [], []
    for t in range(SUB):
        e = jnp.where(srow <= t, jnp.exp(b[t:t + 1, :] - b), 0.0)
        es.append(e)
        ws.append(e * (qs[t:t + 1, :] * k))
    return srow, es, ws


def _hgrn_state_step(st, k, v, b, bmask):
    bl = b[SUB - 1:SUB, :]
    ktil = k * jnp.exp(bl - b)
    return st * jnp.exp(bl) + _dot_tn(v.astype(BF16), ktil.astype(BF16)) * bmask


def _hgrn_sub_fwd(qs, k, v, b, st, ones_b, bmask):
    srow, _, ws = _hgrn_decays(qs, k, b)
    aexp = _dot(jnp.concatenate(ws, axis=0).astype(BF16), ones_b)
    o = _dot_nt((qs * jnp.exp(b)).astype(BF16), st.astype(BF16))
    for t in range(SUB):
        row = jnp.sum(aexp[t * SUB:(t + 1) * SUB, :] * v, axis=0, keepdims=True)
        o = o + jnp.where(srow == t, row, 0.0)
    return o, _hgrn_state_step(st, k, v, b, bmask)


def _hgrn_tile(t):
    return min(256, t)


def _hgrn_fwd(proj, lb, ones_b, nb, name):
    n = proj.shape[0]
    t = n // nb
    tt = _hgrn_tile(t)
    nt = t // tt
    ncs = tt // CHUNK
    w = HGRN_W

    def body(q_ref, z_ref, v_ref, lb_ref, ones_ref, o_ref, s0_ref, st_s, b_s, qs_s, k_s):
        @pl.when(pl.program_id(1) == 0)
        def _():
            st_s[...] = jnp.zeros_like(st_s)

        q = q_ref[...]
        _, _, _, g, k, sq = _hgrn_gates(q, z_ref[...], lb_ref[...])
        b_s[...] = _sel_dot(_sub_tri(tt, True), g)
        qs_s[...] = q * sq
        k_s[...] = k
        bmask = _block_ones(w, F32)
        ones_b = ones_ref[...]

        def chunk(c, carry):
            st = st_s[...]
            s0_ref[c] = st
            base = pl.multiple_of(c * CHUNK, CHUNK)
            for u in range(CHUNK // SUB):
                rows = pl.ds(base + u * SUB, SUB)
                o, st = _hgrn_sub_fwd(qs_s[rows, :], k_s[rows, :], v_ref[rows, :], b_s[rows, :], st, ones_b, bmask)
                o_ref[rows, :] = o
            st_s[...] = st
            return carry

        lax.fori_loop(0, ncs, chunk, 0)

    def col(j):
        return pl.BlockSpec((tt, w), lambda b, i: (b * nt + i, j))

    return _pc(
        body, name, grid=(nb, nt),
        in_specs=[col(C_QA // w), col(C_FA // w), col(C_IA // w), pl.BlockSpec((1, w), lambda b, i: (0, 0)),
                  pl.BlockSpec((w, w), lambda b, i: (0, 0))],
        out_specs=[pl.BlockSpec((tt, w), lambda b, i: (b * nt + i, 0)),
                   pl.BlockSpec((ncs, w, w), lambda b, i: (b * nt + i, 0, 0))],
        out_shape=[_sds((n, w), F32), _sds((n // CHUNK, w, w), F32)],
        scratch_shapes=[pltpu.VMEM((w, w), F32)] + [pltpu.VMEM((tt, w), F32)] * 3,
        compiler_params=_params(),
    )(proj, proj, proj, lb, ones_b)


def _pool_lane_windows():
    lane = lax.broadcasted_iota(jnp.int32, (1, POOL_W), 1) // HEAD
    wl = jnp.zeros((1, POOL_W), F32)
    for gi, win in enumerate(POOL_WINDOWS):
        wl = jnp.where(lane == gi, float(win), wl)
    return lane, wl


def _pool_select(lane, parts):
    out = parts[-1]
    for gi in range(len(parts) - 2, -1, -1):
        out = jnp.where(lane == gi, parts[gi], out)
    return out


def _pool_mix(u, halo, t0, tt):
    lane, wl = _pool_lane_windows()
    ext = jnp.concatenate([halo, u], axis=0)
    sums, cur, shift = [], ext, 1
    for _ in POOL_WINDOWS:
        cur = cur + pltpu.roll(cur, shift, axis=0)
        sums.append(cur[POOL_HALO:, :])
        shift *= 2
    tpos = (t0 + lax.broadcasted_iota(jnp.int32, (tt, POOL_W), 0)).astype(F32)
    cnt = jnp.minimum(tpos + 1.0, wl)
    return _pool_select(lane, sums) / cnt - u, cnt


def _pool_specs(tt, nt, nhb):
    cu, cg = C_UB // POOL_W, C_GB // POOL_W
    return [pl.BlockSpec((tt, POOL_W), lambda b, i: (b * nt + i, cu)),
            pl.BlockSpec((tt, POOL_W), lambda b, i: (b * nt + i, cg)),
            pl.BlockSpec((POOL_HALO, POOL_W), lambda b, i: (jnp.maximum((b * nt + i) * nhb - 1, 0), cu))]


def _pool_fwd(proj, wbd, scale, nb, name):
    n = proj.shape[0]
    t = n // nb
    tt = min(256, t)
    nt = t // tt
    nhb = tt // POOL_HALO

    def body(u_ref, g_ref, h_ref, w_ref, s_ref, o_ref):
        i = pl.program_id(1)
        halo = jnp.where(i == 0, 0.0, h_ref[...])
        pooled, _ = _pool_mix(u_ref[...], halo, i * tt, tt)
        gv = g_ref[...]
        o_ref[...] = _dot(pooled.astype(BF16), w_ref[...]) * s_ref[...] * (gv * _sigmoid(gv))

    return _pc(
        body, name, grid=(nb, nt),
        in_specs=_pool_specs(tt, nt, nhb) + [pl.BlockSpec((POOL_W, POOL_W), lambda b, i: (0, 0)),
                                             pl.BlockSpec((1, POOL_W), lambda b, i: (0, 0))],
        out_specs=pl.BlockSpec((tt, POOL_W), lambda b, i: (b * nt + i, 0)),
        out_shape=_sds((n, POOL_W), F32),
        compiler_params=_params(),
    )(proj, proj, proj, wbd, scale)


def _rows_reduce(x, op, final):
    while x.shape[0] > 8 and x.shape[0] % 16 == 0:
        half = x.shape[0] // 2
        x = op(x[:half], x[half:])
    return final(x, axis=0, keepdims=True)


def _tri_pair(step, n):
    a = sum([(step >= r * (r + 1) // 2).astype(jnp.int32) for r in range(1, n)], jnp.int32(0))
    return a, step - a * (a + 1) // 2


def _lane_lo():
    return lax.broadcasted_iota(jnp.int32, (1, 2 * HEAD), 1) < HEAD


def _put_col(tile, hh, colv):
    lane = lax.broadcasted_iota(jnp.int32, tile.shape, 1)
    return jnp.where(lane == hh, colv, tile)


def _fox_fwd(proj, c_col, c_row, nb, name, rider=None):
    n = proj.shape[0]
    t = n // nb
    tb = min(256, t)
    nq = t // tb
    pw = 2 * HEAD

    def body(q_ref, k_ref, v_ref, cc_ref, cr_ref, o_ref, lse_ref, m_s, acc_s, cq_s):
        qi, kj = _tri_pair(pl.program_id(1), nq)

        @pl.when(kj == 0)
        def _():
            m_s[...] = jnp.full_like(m_s, -jnp.inf)
            acc_s[...] = jnp.zeros_like(acc_s)
            for hh in range(FOX_HEADS):
                cq_s[hh] = jnp.broadcast_to(cc_ref[:, hh:hh + 1], (tb, pw))

        def block(masked):
            lo = _lane_lo()
            if masked:
                causal = lax.broadcasted_iota(jnp.int32, (tb, tb), 0) >= lax.broadcasted_iota(jnp.int32, (tb, tb), 1)
            for p in range(FOX_HEADS // 2):
                sl = slice(p * pw, (p + 1) * pw)
                qp = q_ref[:, sl] * FOX_SCALE
                kp = k_ref[:, sl].astype(BF16)
                vf = v_ref[:, sl]
                for h in range(2):
                    hh = 2 * p + h
                    lm = lo if h == 0 else jnp.logical_not(lo)
                    s = _dot_nt(jnp.where(lm, qp, 0.0).astype(BF16), kp)
                    s = s + (jnp.tile(cq_s[hh], (1, tb // pw)) - cr_ref[hh:hh + 1, :])
                    if masked:
                        s = jnp.where(causal, s, MASK_VALUE)
                    m_prev = m_s[hh]
                    m_new = jnp.maximum(m_prev, jnp.max(s, axis=1, keepdims=True))
                    alpha = jnp.exp(m_prev - m_new)
                    pe = jnp.exp(s - jnp.tile(m_new, (1, tb // pw)))
                    m_s[hh] = m_new
                    acc_s[hh] = alpha * acc_s[hh] + _dot(pe.astype(BF16), jnp.where(lm, vf, 1.0).astype(BF16))

        @pl.when(kj < qi)
        def _():
            block(False)

        @pl.when(kj == qi)
        def _():
            block(True)
            lo = _lane_lo()
            lse = jnp.zeros((tb, FC_PAD), F32)
            for p in range(FOX_HEADS // 2):
                halves = []
                for h in range(2):
                    hh = 2 * p + h
                    lm = lo if h == 0 else jnp.logical_not(lo)
                    acc = acc_s[hh]
                    swapped = pltpu.roll(acc, HEAD, axis=1)
                    halves.append(acc / swapped)
                    lse = _put_col(lse, hh, m_s[hh] + jnp.log(jnp.where(lm, swapped, acc)))
                o_ref[:, p * pw:(p + 1) * pw] = jnp.where(lo, halves[0], halves[1])
            lse_ref[...] = lse

    def qspec(wd, j):
        return pl.BlockSpec((tb, wd), lambda b, st: (b * nq + _tri_pair(st, nq)[0], j))

    def kspec(j):
        return pl.BlockSpec((tb, FOX_W), lambda b, st: (b * nq + _tri_pair(st, nq)[1], j))

    return _call(
        body, name, (proj, proj, proj, c_col, c_row), rider, grid=(nb, nq * (nq + 1) // 2),
        in_specs=[qspec(FOX_W, C_QC // FOX_W), kspec(C_KC // FOX_W), kspec(C_VC // FOX_W), qspec(FC_PAD, 0),
                  pl.BlockSpec((None, FOX_HEADS, tb), lambda b, st: (b, 0, _tri_pair(st, nq)[1]))],
        out_specs=[qspec(FOX_W, 0), qspec(FC_PAD, 0)],
        out_shape=[_sds((n, FOX_W), F32), _sds((n, FC_PAD), F32)],
        scratch_shapes=[pltpu.VMEM((FOX_HEADS, tb, pw), F32), pltpu.VMEM((FOX_HEADS, tb, pw), F32),
                        pltpu.VMEM((FOX_HEADS, tb, pw), F32)],
        compiler_params=_params(),
    )


def _head_mean(x, ones_f):
    return _dot_sel(x, ones_f) * (1.0 / HEAD)


def _merge_fwd(x, proj, o_h, o_b, o_c, gh, w_out, g_post, name, target=None):
    n, d = x.shape
    tm = min(512, n)

    def body(*refs):
        x_ref, ga_ref, gc_ref, oh_ref, ob_ref, oc_ref, gh_ref, w_ref, gp_ref = refs[:9]
        if target is None:
            xo_ref, mixt_ref, y_ref = refs[9:]
        else:
            t_ref, dx_ref, sq_ref, mixt_ref, y_ref = refs[9:]
        oh = oh_ref[...]
        ones_f = _block_ones(HGRN_W, F32)
        na = oh * lax.rsqrt(_head_mean(oh * oh, ones_f) + NORM_EPS) * gh_ref[...]
        ga, gc = ga_ref[...], gc_ref[...]
        mixed = jnp.concatenate([na * (ga * _sigmoid(ga)), ob_ref[...], oc_ref[...] * (gc * _sigmoid(gc))], axis=1)
        mixt_ref[...] = mixed.T.astype(BF16)
        y = _dot(mixed.astype(BF16), w_ref[...])
        y_ref[...] = y
        xn = x_ref[...] + y * lax.rsqrt(jnp.mean(y * y, axis=-1, keepdims=True) + NORM_EPS) * gp_ref[...]
        if target is None:
            xo_ref[...] = xn
        else:
            @pl.when(pl.program_id(0) == 0)
            def _():
                sq_ref[...] = jnp.zeros_like(sq_ref)

            e = xn - t_ref[...]
            dx_ref[...] = e * (1.0 / d)
            sq_ref[...] += jnp.sum(e * e, axis=0, keepdims=True)

    def row(wd, j=0):
        return pl.BlockSpec((tm, wd), lambda i: (i, j))

    def full(a, b):
        return pl.BlockSpec((a, b), lambda i: (0, 0))

    head = [] if target is None else [target]
    res = _pc(
        body, name, grid=(n // tm,),
        in_specs=[row(d), row(HGRN_W, C_GA // HGRN_W), row(FOX_W, C_GC // FOX_W), row(HGRN_W), row(POOL_W), row(FOX_W),
                  full(1, HGRN_W), full(d, d), full(1, d)] + [row(d)] * len(head),
        out_specs=[row(d)] + [full(1, d)] * len(head) + [pl.BlockSpec((d, tm), lambda i: (0, i)), row(d)],
        out_shape=[_sds((n, d), F32)] + [_sds((1, d), F32)] * len(head) + [_sds((d, n), BF16), _sds((n, d), F32)],
        compiler_params=_params(),
    )(x, proj, proj, o_h, o_b, o_c, gh, w_out, g_post, *head)
    return (res[0], res[1], res[2]) if target is None else ((res[0], res[1]), res[2], res[3])


def _rms_bwd(dy_scaled, xhat, r):
    return r * (dy_scaled - xhat * jnp.mean(dy_scaled * xhat, axis=-1, keepdims=True))


def _merge_bwd(dxo, y, g_post, w_out_t, name):
    n, d = y.shape
    tm = min(512, n)

    def body(dx_ref, y_ref, gp_ref, wt_ref, dy_ref, dm_ref, dgp_ref):
        @pl.when(pl.program_id(0) == 0)
        def _():
            dgp_ref[...] = jnp.zeros_like(dgp_ref)

        yv, dxv = y_ref[...], dx_ref[...]
        r = lax.rsqrt(jnp.mean(yv * yv, axis=-1, keepdims=True) + NORM_EPS)
        yh = yv * r
        dgp_ref[...] += jnp.sum(dxv * yh, axis=0, keepdims=True)
        dyb = _rms_bwd(dxv * gp_ref[...], yh, r).astype(BF16)
        dy_ref[...] = dyb
        dm_ref[...] = _dot(dyb, wt_ref[...])

    row = pl.BlockSpec((tm, d), lambda i: (i, 0))
    return _pc(
        body, name, grid=(n // tm,),
        in_specs=[row, row, pl.BlockSpec((1, d), lambda i: (0, 0)), pl.BlockSpec((d, d), lambda i: (0, 0))],
        out_specs=[row, row, pl.BlockSpec((1, d), lambda i: (0, 0))],
        out_shape=[_sds((n, d), BF16), _sds((n, d), F32), _sds((1, d), F32)],
        compiler_params=_params(),
    )(dxo, y, g_post, w_out_t)


def _w_out_grad(mixt, dy, name):
    d, n = mixt.shape
    rows = d // N_DEV

    def body(a_ref, b_ref, o_ref):
        o_ref[...] = _dot(a_ref[...], b_ref[...]).astype(BF16)

    return _pc(
        body, name, grid=(N_DEV,),
        in_specs=[pl.BlockSpec((rows, n), lambda j: (j, 0)), pl.BlockSpec((n, d), lambda j: (0, 0))],
        out_specs=pl.BlockSpec((None, None, rows, d), lambda j: (j % 2, j // 2, 0, 0)),
        out_shape=_sds((2, N_DEV // 2, rows, d), BF16),
        compiler_params=_params(),
    )(mixt, dy)


def _w_in_grad(ht, pieces, name):
    d, n = ht.shape
    ta, tk = min(512, d), min(512, n)
    nk = n // tk
    arrays = [p for p, _ in pieces]
    widths = [p.shape[1] for p in arrays]
    offs = [sum(widths[:i]) for i in range(len(widths))]
    in_w = MAIN_W + FOX_HEADS
    shard = in_w // N_DEV

    def body(*refs):
        a_ref, p_refs = refs[0], refs[1:1 + len(arrays)]
        o_ref, acc = refs[1 + len(arrays):]
        k = pl.program_id(1)

        @pl.when(k == 0)
        def _():
            acc[...] = jnp.zeros_like(acc)

        a = a_ref[...]
        for pr, off, wd in zip(p_refs, offs, widths):
            for j in range(0, wd, 512):
                jw = min(512, wd - j)
                acc[:, off + j:off + j + jw] += _dot(a, pr[:, j:j + jw])

        @pl.when(k == nk - 1)
        def _():
            for j in range(N_DEV):
                o_ref[j % 2, j // 2] = acc[:, j * shard:(j + 1) * shard].astype(BF16)

    return _pc(
        body, name, grid=(d // ta, nk),
        in_specs=[pl.BlockSpec((ta, tk), lambda i, k: (i, k))] + [pl.BlockSpec((tk, wd), lambda i, k: (k, 0)) for wd in widths],
        out_specs=pl.BlockSpec((2, N_DEV // 2, ta, shard), lambda i, k: (0, 0, i, 0)),
        out_shape=_sds((2, N_DEV // 2, d, shard), BF16),
        scratch_shapes=[pltpu.VMEM((ta, sum(widths)), F32)],
        compiler_params=_params(),
    )(ht, *arrays)


def _hgrn_sub_bwd(qs, k, v, b, do, s0, ds1, ones_b, bmask):
    bl = b[SUB - 1:SUB, :]
    eb, ebl, ekt = jnp.exp(b), jnp.exp(bl), jnp.exp(bl - b)
    qe, ktil = qs * eb, k * ekt
    ds1b, dob = ds1.astype(BF16), do.astype(BF16)
    dv = _dot_nt(ktil.astype(BF16), ds1b)
    dqe = _dot(dob, s0.astype(BF16))
    dktil = _dot(v.astype(BF16), ds1b)
    dbl = jnp.sum(dktil * ktil, axis=0, keepdims=True) + ebl * jnp.sum(s0 * ds1, axis=0, keepdims=True)
    ds0 = ds1 * ebl + _dot_tn(dob, qe.astype(BF16)) * bmask
    srow, es, ws = _hgrn_decays(qs, k, b)
    aexp = _dot(jnp.concatenate(ws, axis=0).astype(BF16), ones_b)
    gexp = _dot(jnp.concatenate([do[t:t + 1, :] * v for t in range(SUB)], axis=0).astype(BF16), ones_b)
    dq = dqe * eb
    dk = dktil * ekt
    for t in range(SUB):
        sl = slice(t * SUB, (t + 1) * SUB)
        ge = gexp[sl, :] * es[t]
        dq = dq + jnp.where(srow == t, jnp.sum(ge * k, axis=0, keepdims=True), 0.0)
        dk = dk + ge * qs[t:t + 1, :]
        dv = dv + aexp[sl, :] * do[t:t + 1, :]
    return dq, dk, dv, dbl, ds0


def _hgrn_bwd(dmix, proj, o_h, s0, gh, lb, ones_b, nb, name):
    n = proj.shape[0]
    t = n // nb
    tt = _hgrn_tile(t)
    nt = t // tt
    ncs = tt // CHUNK
    nsub = CHUNK // SUB
    w = HGRN_W

    def body(dm_ref, q_ref, z_ref, v_ref, ga_ref, oh_ref, s0_ref, gh_ref, lb_ref, ones_ref,
             dp_ref, dgh_ref, dlb_ref, ds_s, ss_s, b_s, qs_s, k_s, do_s, dq_s, dk_s, dv_s, dbl_s):
        first = jnp.logical_and(pl.program_id(0) == 0, pl.program_id(1) == 0)

        @pl.when(first)
        def _():
            dgh_ref[...] = jnp.zeros_like(dgh_ref)
            dlb_ref[...] = jnp.zeros_like(dlb_ref)

        @pl.when(pl.program_id(1) == 0)
        def _():
            ds_s[...] = jnp.zeros_like(ds_s)

        ones_b = ones_ref[...]
        ones_f = ones_b.astype(F32)
        bmask = _block_ones(w, F32)
        lbv, ghv = lb_ref[...], gh_ref[...]
        oh, ga, dm = oh_ref[...], ga_ref[...], dm_ref[...]
        rn = lax.rsqrt(_head_mean(oh * oh, ones_f) + NORM_EPS)
        nh = oh * rn
        sga = _sigmoid(ga)
        dp_ref[:, 3 * w:4 * w] = (dm * nh * ghv * (sga * (1.0 + ga * (1.0 - sga)))).astype(BF16)
        dn = dm * (ga * sga)
        dgh_ref[...] += jnp.sum(dn * nh, axis=0, keepdims=True)
        dn = dn * ghv
        do_s[...] = rn * (dn - nh * _head_mean(dn * nh, ones_f))
        q = q_ref[...]
        sig, sn, f, g, k, sq = _hgrn_gates(q, z_ref[...], lbv)
        qs = q * sq
        b_s[...] = _sel_dot(_sub_tri(tt, True), g)
        qs_s[...] = qs
        k_s[...] = k

        def chunk(cc, carry):
            c = ncs - 1 - cc
            base = pl.multiple_of(c * CHUNK, CHUNK)
            st = s0_ref[c]
            for u in range(nsub):
                ss_s[u] = st
                if u < nsub - 1:
                    rows = pl.ds(base + u * SUB, SUB)
                    st = _hgrn_state_step(st, k_s[rows, :], v_ref[rows, :], b_s[rows, :], bmask)
            ds = ds_s[...]
            for u in reversed(range(nsub)):
                rows = pl.ds(base + u * SUB, SUB)
                dq, dk, dv, dbl, ds = _hgrn_sub_bwd(qs_s[rows, :], k_s[rows, :], v_ref[rows, :], b_s[rows, :],
                                                    do_s[rows, :], ss_s[u], ds, ones_b, bmask)
                dq_s[rows, :] = dq
                dk_s[rows, :] = dk
                dv_s[rows, :] = dv
                dbl_s[rows, :] = jnp.broadcast_to(dbl, (SUB, w))
            ds_s[...] = ds
            return carry

        lax.fori_loop(0, ncs, chunk, 0)
        dqs, dk = dq_s[...], dk_s[...]
        dg = _sel_dot(_sub_tri(tt, False), qs * dqs - k * dk) + dbl_s[...]
        dfz = jnp.where(f > TINY, dg / jnp.maximum(f, TINY), 0.0)
        dlb_ref[...] += jnp.sum(dfz * (1.0 - sig) - dk * sn, axis=0, keepdims=True)
        dp_ref[:, 0:w] = (dqs * (sq * (1.0 + q * (1.0 - sq)))).astype(BF16)
        dp_ref[:, w:2 * w] = ((dfz - dk) * (1.0 - lbv) * sig * sn).astype(BF16)
        dp_ref[:, 2 * w:3 * w] = dv_s[...].astype(BF16)

    def rv(b, i):
        return b * nt + (nt - 1 - i)

    def col(j):
        return pl.BlockSpec((tt, w), lambda b, i: (rv(b, i), j))

    def full(a, bb):
        return pl.BlockSpec((a, bb), lambda b, i: (0, 0))

    return _pc(
        body, name, grid=(nb, nt),
        in_specs=[col(0), col(C_QA // w), col(C_FA // w), col(C_IA // w), col(C_GA // w), col(0),
                  pl.BlockSpec((ncs, w, w), lambda b, i: (rv(b, i), 0, 0)), full(1, w), full(1, w), full(w, w)],
        out_specs=[pl.BlockSpec((tt, 4 * w), lambda b, i: (rv(b, i), 0)), full(1, w), full(1, w)],
        out_shape=[_sds((n, 4 * w), BF16), _sds((1, w), F32), _sds((1, w), F32)],
        scratch_shapes=[pltpu.VMEM((w, w), F32), pltpu.VMEM((nsub, w, w), F32)] + [pltpu.VMEM((tt, w), F32)] * 8,
        compiler_params=_params(),
    )(dmix, proj, proj, proj, proj, o_h, s0, gh, lb, ones_b)


def _pool_bwd(dmix, proj, wbd, wbd_t, scale, nb, name):
    n = proj.shape[0]
    t = n // nb
    tt = min(256, t)
    nt = t // tt
    nhb = tt // POOL_HALO
    cu, cg, cm = C_UB // POOL_W, C_GB // POOL_W, HGRN_W // POOL_W

    def body(u_ref, g_ref, h_ref, dm_ref, gn_ref, dmn_ref, w_ref, wt_ref, s_ref, dp_ref, dw_ref, ds_ref):
        i = pl.program_id(1)
        first = jnp.logical_and(pl.program_id(0) == 0, i == 0)

        @pl.when(first)
        def _():
            dw_ref[...] = jnp.zeros_like(dw_ref)
            ds_ref[...] = jnp.zeros_like(ds_ref)

        sc = s_ref[...]
        halo = jnp.where(i == 0, 0.0, h_ref[...])
        pooled, cnt = _pool_mix(u_ref[...], halo, i * tt, tt)
        pb = pooled.astype(BF16)
        pre = _dot(pb, w_ref[...])
        gv, dm = g_ref[...], dm_ref[...]
        sg = _sigmoid(gv)
        silu = gv * sg
        dgb = dm * pre * sc * (sg * (1.0 + gv * (1.0 - sg)))
        ds_ref[...] += jnp.sum(dm * pre * silu, axis=0, keepdims=True)
        dpre = (dm * sc * silu).astype(BF16)
        dw_ref[...] += _dot_tn(pb, dpre)
        dpool = _dot(dpre, wt_ref[...])
        gn = gn_ref[...]
        dpre_n = (dmn_ref[...] * sc * (gn * _sigmoid(gn))).astype(BF16)
        dpool_n = jnp.where(i == nt - 1, 0.0, _dot(dpre_n, wt_ref[...]))
        lane, wl = _pool_lane_windows()
        tpos_n = ((i + 1) * tt + lax.broadcasted_iota(jnp.int32, (POOL_HALO, POOL_W), 0)).astype(F32)
        ext = jnp.concatenate([dpool / cnt, dpool_n / jnp.minimum(tpos_n + 1.0, wl)], axis=0)
        rows = tt + POOL_HALO
        sums, cur, shift = [], ext, 1
        for _ in POOL_WINDOWS:
            cur = cur + pltpu.roll(cur, rows - shift, axis=0)
            sums.append(cur[:tt, :])
            shift *= 2
        du = _pool_select(lane, sums) - dpool
        dp_ref[...] = jnp.concatenate([du, dgb], axis=1).astype(BF16)

    def nxt(b, i):
        return jnp.minimum((b * nt + i + 1) * nhb, n // POOL_HALO - 1)

    return _pc(
        body, name, grid=(nb, nt),
        in_specs=_pool_specs(tt, nt, nhb) + [
            pl.BlockSpec((tt, POOL_W), lambda b, i: (b * nt + i, cm)),
            pl.BlockSpec((POOL_HALO, POOL_W), lambda b, i: (nxt(b, i), cg)),
            pl.BlockSpec((POOL_HALO, POOL_W), lambda b, i: (nxt(b, i), cm)),
            pl.BlockSpec((POOL_W, POOL_W), lambda b, i: (0, 0)), pl.BlockSpec((POOL_W, POOL_W), lambda b, i: (0, 0)),
            pl.BlockSpec((1, POOL_W), lambda b, i: (0, 0))],
        out_specs=[pl.BlockSpec((tt, 2 * POOL_W), lambda b, i: (b * nt + i, 0)),
                   pl.BlockSpec((POOL_W, POOL_W), lambda b, i: (0, 0)), pl.BlockSpec((1, POOL_W), lambda b, i: (0, 0))],
        out_shape=[_sds((n, 2 * POOL_W), BF16), _sds((POOL_W, POOL_W), F32), _sds((1, POOL_W), F32)],
        compiler_params=_params(),
    )(proj, proj, proj, dmix, proj, dmix, wbd, wbd_t, scale)


def _fox_gate_bwd(dmix, proj, o_c, name):
    n = proj.shape[0]
    tm = min(512, n)

    def body(dm_ref, gc_ref, oc_ref, da_ref, dg_ref, dl_ref):
        dm, gc, oc = dm_ref[...], gc_ref[...], oc_ref[...]
        sg = _sigmoid(gc)
        da = dm * (gc * sg)
        da_ref[...] = da.astype(BF16)
        dg_ref[...] = (dm * oc * (sg * (1.0 + gc * (1.0 - sg)))).astype(BF16)
        r = lax.broadcasted_iota(jnp.int32, (FOX_W, FC_PAD), 0) // HEAD
        c = lax.broadcasted_iota(jnp.int32, (FOX_W, FC_PAD), 1)
        dl_ref[...] = _dot_sel(da * oc, (r == c).astype(F32))

    def row(wd, j=0):
        return pl.BlockSpec((tm, wd), lambda i: (i, j))

    return _pc(
        body, name, grid=(n // tm,),
        in_specs=[row(FOX_W, (HGRN_W + POOL_W) // FOX_W), row(FOX_W, C_GC // FOX_W), row(FOX_W)],
        out_specs=[row(FOX_W), row(FOX_W), row(FC_PAD)],
        out_shape=[_sds((n, FOX_W), BF16), _sds((n, FOX_W), BF16), _sds((n, FC_PAD), F32)],
        compiler_params=_params(),
    )(dmix, proj, o_c)


def _fox_bwd(proj, da, c_col, c_row, lse_row, delta_row, nb, name, rider=None):
    n = proj.shape[0]
    t = n // nb
    tb = min(256, t)
    nq = t // tb
    pw = 2 * HEAD

    def body(q_ref, k_ref, v_ref, da_ref, cc_ref, cr_ref, lse_ref, dl_ref,
             dq_ref, dk_ref, dv_ref, dck_ref, dcq_ref, dq_s, dk_s, dv_s, dck_s, dcq_s):
        step = pl.program_id(1)
        kj, qi = pairs(step)

        @pl.when(step == 0)
        def _():
            dq_s[...] = jnp.zeros_like(dq_s)
            dcq_s[...] = jnp.zeros_like(dcq_s)

        @pl.when(qi == nq - 1)
        def _():
            dk_s[...] = jnp.zeros_like(dk_s)
            dv_s[...] = jnp.zeros_like(dv_s)
            dck_s[...] = jnp.zeros_like(dck_s)

        def block(masked):
            lo = _lane_lo()
            if masked:
                causal = lax.broadcasted_iota(jnp.int32, (tb, tb), 1) >= lax.broadcasted_iota(jnp.int32, (tb, tb), 0)
            dck = dck_s[...]
            for p in range(FOX_HEADS // 2):
                sl = slice(p * pw, (p + 1) * pw)
                qp = q_ref[:, sl] * FOX_SCALE
                kf = k_ref[:, sl]
                kp = kf.astype(BF16)
                kt = kf.T.astype(BF16)
                vp = v_ref[:, sl].astype(BF16)
                dap = da_ref[:, sl]
                dk, dv = dk_s[:, sl], dv_s[:, sl]
                for h in range(2):
                    hh = 2 * p + h
                    lm = lo if h == 0 else jnp.logical_not(lo)
                    qm = jnp.where(lm, qp, 0.0).astype(BF16)
                    dam = jnp.where(lm, dap, jnp.zeros_like(dap))
                    s = _dot_nt(kp, qm) + (cr_ref[hh:hh + 1, :] - cc_ref[:, hh:hh + 1])
                    pe = jnp.exp(s - lse_ref[hh:hh + 1, :])
                    if masked:
                        pe = jnp.where(causal, pe, 0.0)
                    dp = _dot_nt(vp, dam)
                    ds = pe * (dp - dl_ref[hh:hh + 1, :])
                    dsb = ds.astype(BF16)
                    dv = dv + _dot(pe.astype(BF16), dam)
                    dk = dk + _dot(dsb, qm)
                    rows = slice(hh * HEAD, (hh + 1) * HEAD)
                    dq_s[qi, rows, :] += _dot(kt[h * HEAD:(h + 1) * HEAD, :], dsb)
                    dck = dck - _put_col(jnp.zeros_like(dck), hh, jnp.sum(ds, axis=1, keepdims=True))
                    dcq_s[qi, hh:hh + 1, :] += _rows_reduce(ds, jnp.add, jnp.sum)
                dk_s[:, sl] = dk
                dv_s[:, sl] = dv
            dck_s[...] = dck

        @pl.when(qi > kj)
        def _():
            block(False)

        @pl.when(qi == kj)
        def _():
            block(True)

        @pl.when(qi == kj)
        def _():
            dk_ref[...] = dk_s[...].astype(BF16)
            dv_ref[...] = dv_s[...].astype(BF16)
            dck_ref[...] = dck_s[...]

        @pl.when(step == nq * (nq + 1) // 2 - 1)
        def _():
            for j in range(nq):
                dq_ref[j * tb:(j + 1) * tb, :] = (dq_s[j].T * FOX_SCALE).astype(BF16)
                dcq_ref[:, j * tb:(j + 1) * tb] = dcq_s[j]

    def pairs(step):
        a, b = _tri_pair(step, nq)
        return nq - 1 - a, nq - 1 - b

    def kspec(wd, j=0):
        return pl.BlockSpec((tb, wd), lambda b, st: (b * nq + pairs(st)[0], j))

    def qspec(wd, j=0):
        return pl.BlockSpec((tb, wd), lambda b, st: (b * nq + pairs(st)[1], j))

    def qrow():
        return pl.BlockSpec((None, FOX_HEADS, tb), lambda b, st: (b, 0, pairs(st)[1]))

    return _call(
        body, name, (proj, proj, proj, da, c_col, c_row, lse_row, delta_row), rider, grid=(nb, nq * (nq + 1) // 2),
        in_specs=[qspec(FOX_W, C_QC // FOX_W), kspec(FOX_W, C_KC // FOX_W), kspec(FOX_W, C_VC // FOX_W), qspec(FOX_W),
                  kspec(FC_PAD), qrow(), qrow(), qrow()],
        out_specs=[pl.BlockSpec((t, FOX_W), lambda b, st: (b, 0)), kspec(FOX_W), kspec(FOX_W), kspec(FC_PAD),
                   pl.BlockSpec((None, FOX_HEADS, t), lambda b, st: (b, 0, 0))],
        out_shape=[_sds((n, FOX_W), BF16), _sds((n, FOX_W), BF16), _sds((n, FOX_W), BF16), _sds((n, FC_PAD), F32),
                   _sds((nb, FOX_HEADS, t), F32)],
        scratch_shapes=[pltpu.VMEM((nq, FOX_W, tb), F32), pltpu.VMEM((tb, FOX_W), F32), pltpu.VMEM((tb, FOX_W), F32),
                        pltpu.VMEM((tb, FC_PAD), F32), pltpu.VMEM((nq, FOX_HEADS, tb), F32)],
        compiler_params=_params(),
    )


def _fox_decay_bwd(dc_q, dc_k, fc, bias, nb, name):
    n = fc.shape[0]
    t = n // nb
    tt = min(256, t)
    nt = t // tt

    def body(dcq_ref, dck_ref, fc_ref, b_ref, dfc_ref, db_ref, carry):
        i = pl.program_id(1)
        first = jnp.logical_and(pl.program_id(0) == 0, i == 0)

        @pl.when(first)
        def _():
            db_ref[...] = jnp.zeros_like(db_ref)

        @pl.when(i == 0)
        def _():
            carry[...] = jnp.zeros_like(carry)

        r = lax.broadcasted_iota(jnp.int32, (tt, tt), 0)
        cc = lax.broadcasted_iota(jnp.int32, (tt, tt), 1)
        dlf = _dot_hi((r <= cc).astype(F32), dcq_ref[...] + dck_ref[...]) + carry[...]
        carry[...] = dlf[0:1, :]
        dfc = dlf * _sigmoid(-(fc_ref[...] + b_ref[...]))
        dfc_ref[...] = dfc.astype(BF16)
        db_ref[...] += jnp.sum(dfc, axis=0, keepdims=True)

    def row():
        return pl.BlockSpec((tt, FC_PAD), lambda b, i: (b * nt + (nt - 1 - i), 0))

    return _pc(
        body, name, grid=(nb, nt),
        in_specs=[row(), row(), row(), pl.BlockSpec((1, FC_PAD), lambda b, i: (0, 0))],
        out_specs=[row(), pl.BlockSpec((1, FC_PAD), lambda b, i: (0, 0))],
        out_shape=[_sds((n, FC_PAD), BF16), _sds((1, FC_PAD), F32)],
        scratch_shapes=[pltpu.VMEM((1, FC_PAD), F32)],
        compiler_params=_params(),
    )(dc_q, dc_k, fc, bias)


def _in_proj_bwd(pieces, w_main_t, w_fc_t, x, g_pre, dxo, name, rider=None):
    n, d = x.shape
    tm = min(512, n)
    widths = [p.shape[1] for p, _ in pieces]
    offs = [o for _, o in pieces]
    np_ = len(pieces)

    def body(*refs):
        p_refs = refs[:np_]
        wt_ref, wf_ref, x_ref, g_ref, dxo_ref, dx_ref, dg_ref = refs[np_:]

        @pl.when(pl.program_id(0) == 0)
        def _():
            dg_ref[...] = jnp.zeros_like(dg_ref)

        dh = _dot(p_refs[-1][...], wf_ref[...])
        for pr, wd, off in zip(p_refs[:-1], widths[:-1], offs[:-1]):
            for j in range(0, wd, 512):
                jw = min(512, wd - j)
                dh = dh + _dot(pr[:, j:j + jw], wt_ref[off + j:off + j + jw, :])
        xv = x_ref[...]
        r = lax.rsqrt(jnp.mean(xv * xv, axis=-1, keepdims=True) + NORM_EPS)
        xh = xv * r
        dg_ref[...] += jnp.sum(dh * xh, axis=0, keepdims=True)
        dx_ref[...] = dxo_ref[...] + _rms_bwd(dh * g_ref[...], xh, r)

    row = pl.BlockSpec((tm, d), lambda i: (i, 0))
    return _call(
        body, name, (*[p for p, _ in pieces], w_main_t, w_fc_t, x, g_pre, dxo), rider, grid=(n // tm,),
        in_specs=[pl.BlockSpec((tm, wd), lambda i: (i, 0)) for wd in widths] + [
            pl.BlockSpec((MAIN_W, d), lambda i: (0, 0)), pl.BlockSpec((FC_PAD, d), lambda i: (0, 0)),
            row, pl.BlockSpec((1, d), lambda i: (0, 0)), row],
        out_specs=[row, pl.BlockSpec((1, d), lambda i: (0, 0))],
        out_shape=[_sds((n, d), F32), _sds((1, d), F32)],
        compiler_params=_params(),
    )


def _lower_bound_table(lower_bounds, name):
    depth, w = lower_bounds.shape

    def body(lb_ref, o_ref):
        v = lb_ref[...]
        e = jnp.exp(v - jnp.max(v, axis=0, keepdims=True))
        p = e / jnp.sum(e, axis=0, keepdims=True)
        acc = jnp.zeros((1, w), F32)
        for l in range(depth):
            acc = acc + p[l:l + 1, :]
            o_ref[l:l + 1, :] = acc - p[0:1, :]

    return _pc(body, name, out_shape=_sds((depth, w), F32))(lower_bounds)


def _lower_bound_bwd(lower_bounds, dlbs, name):
    depth, w = lower_bounds.shape

    def body(lb_ref, d_ref, o_ref):
        v, dl = lb_ref[...], d_ref[...]
        e = jnp.exp(v - jnp.max(v, axis=0, keepdims=True))
        p = e / jnp.sum(e, axis=0, keepdims=True)
        tot = jnp.sum(dl, axis=0, keepdims=True)
        rows, tail = [], tot
        for l in range(depth):
            rows.append(tail - tot if l == 0 else tail)
            tail = tail - dl[l:l + 1, :]
        dp = jnp.concatenate(rows, axis=0)
        o_ref[...] = p * (dp - jnp.sum(p * dp, axis=0, keepdims=True))

    return _pc(body, name, out_shape=_sds((depth, w), F32))(lower_bounds, dlbs)


def _place():
    x, y, c = lax.axis_index("x"), lax.axis_index("y"), lax.axis_index("c")
    return x, y, c


def _gather_weights(w_in_b, w_out_b):
    arrays = (w_in_b, w_out_b)
    na = len(arrays)

    def body(*refs):
        ins, outs = refs[:na], refs[na:2 * na]
        send_sems, recv_sems, local_sems = refs[2 * na:]
        x, y, c = _place()
        me, sibling = (x, y, c), (x, y, 1 - c)
        chips = [(1 - x, y), (x, 1 - y), (1 - x, 1 - y)]

        def slot(a, px, py, pc):
            return outs[a].at[4 * px + 2 * py + pc]

        def copy(a, k, block, to, own=False):
            return pltpu.make_async_remote_copy(
                src_ref=ins[a] if own else slot(a, *block), dst_ref=slot(a, *block),
                send_sem=send_sems.at[a * 7 + k], recv_sem=recv_sems.at[a * 7 + k],
                device_id=to, device_id_type=MESH)

        mine = [pltpu.make_async_copy(ins[a], slot(a, *me), local_sems.at[a]) for a in range(na)]
        for cp in mine:
            cp.start()
        first = []
        for a in range(na):
            first.append(copy(a, 0, me, sibling, own=True))
            first += [copy(a, 1 + j, me, (*chip, c), own=True) for j, chip in enumerate(chips)]
        for cp in first:
            cp.start()
        passed = []
        for j, chip in enumerate(chips):
            for a in range(na):
                copy(a, 1 + j, (*chip, c), me).wait_recv()
                fw = copy(a, 4 + j, (*chip, c), sibling)
                fw.start()
                passed.append(fw)
        for a in range(na):
            copy(a, 0, sibling, me).wait_recv()
            for j, chip in enumerate(chips):
                copy(a, 4 + j, (*chip, 1 - c), me).wait_recv()
        for cp in first + passed:
            cp.wait_send()
        for cp in mine:
            cp.wait()

    any_spec = pl.BlockSpec(memory_space=pl.ANY)
    return _pc(
        body, "gather_weights",
        in_specs=[any_spec] * na, out_specs=[any_spec] * na,
        out_shape=[_sds((N_DEV,) + a.shape, a.dtype) for a in arrays],
        scratch_shapes=[pltpu.SemaphoreType.DMA((7 * na,)), pltpu.SemaphoreType.DMA((7 * na,)),
                        pltpu.SemaphoreType.DMA((na,))],
    )(*arrays)


def _peer(k):
    x, y, c = _place()
    return (1 - x if k & 4 else x, 1 - y if k & 2 else y, 1 - c if k & 1 else c)


def _remote(src, dst, sems, s, to):
    return pltpu.make_async_remote_copy(src_ref=src, dst_ref=dst, send_sem=sems[0].at[s], recv_sem=sems[1].at[s],
                                        device_id=to, device_id_type=MESH)


def _gather_rider(shards):
    na = len(shards)

    def plan(ins, outs, *sems):
        x, y, c = _place()
        me = 4 * x + 2 * y + c
        locs = [pltpu.make_async_copy(ins[a], outs[a].at[me], sems[2].at[a]) for a in range(na)]
        sends, recvs = [], []
        for k in range(1, N_DEV):
            px, py, pc = _peer(k)
            for a in range(na):
                s = (k - 1) * na + a
                sends.append(_remote(ins[a], outs[a].at[me], sems, s, (px, py, pc)))
                recvs.append(_remote(ins[a], outs[a].at[4 * px + 2 * py + pc], sems, s, (px, py, pc)))
        return sends, recvs, locs

    return _Rider(shards, [_sds((N_DEV,) + a.shape, a.dtype) for a in shards], (N_DEV - 1) * na, na, plan)


def _direct_exchange_rider(blocks):
    na = len(blocks)

    def plan(ins, outs, *sems):
        x, y, c = _place()
        me = 4 * x + 2 * y + c
        locs = [pltpu.make_async_copy(ins[a].at[c, 2 * x + y], outs[a].at[me], sems[2].at[a]) for a in range(na)]
        sends, recvs = [], []
        for k in range(1, N_DEV):
            px, py, pc = _peer(k)
            for a in range(na):
                s = (k - 1) * na + a
                sends.append(_remote(ins[a].at[pc, 2 * px + py], outs[a].at[me], sems, s, (px, py, pc)))
                recvs.append(_remote(ins[a].at[pc, 2 * px + py], outs[a].at[4 * px + 2 * py + pc], sems, s, (px, py, pc)))
        return sends, recvs, locs

    return _Rider(blocks, [_sds((N_DEV,) + a.shape[2:], a.dtype) for a in blocks], (N_DEV - 1) * na, na, plan)


def _swap_rider(halves):
    na = len(halves)

    def plan(ins, outs, *sems):
        x, y, c = _place()
        cps = [_remote(ins[a].at[1 - c], outs[a], sems, a, (x, y, 1 - c)) for a in range(na)]
        return cps, cps, []

    return _Rider(halves, [_sds(a.shape[1:], a.dtype) for a in halves], na, 1, plan)


def _chip_exchange_rider(parts, small=None):
    na = len(parts)
    n_chip = N_DEV // 2

    def plan(ins, outs, *sems):
        x, y, c = _place()
        chip = 2 * x + y
        locs = [pltpu.make_async_copy(ins[a].at[chip], outs[a].at[chip], sems[2].at[a]) for a in range(na)]
        sends, recvs = [], []
        for k in range(1, n_chip):
            px, py, _ = _peer(2 * k)
            for a in range(na):
                s = (k - 1) * na + a
                sends.append(_remote(ins[a].at[2 * px + py], outs[a].at[chip], sems, s, (px, py, c)))
                recvs.append(_remote(ins[a].at[2 * px + py], outs[a].at[2 * px + py], sems, s, (px, py, c)))
        if small is not None:
            me = 2 * chip + c
            locs.append(pltpu.make_async_copy(ins[na], outs[na].at[me], sems[2].at[na]))
            for k in range(1, N_DEV):
                px, py, pc = _peer(k)
                s = (n_chip - 1) * na + k - 1
                sends.append(_remote(ins[na], outs[na].at[me], sems, s, (px, py, pc)))
                recvs.append(_remote(ins[na], outs[na].at[4 * px + 2 * py + pc], sems, s, (px, py, pc)))
        return sends, recvs, locs

    extra = [] if small is None else [small]
    shapes = [_sds(a.shape, a.dtype) for a in parts] + [_sds((N_DEV,) + s.shape, s.dtype) for s in extra]
    n_sems = (n_chip - 1) * na + (N_DEV - 1) * len(extra)
    return _Rider(list(parts) + extra, shapes, n_sems, na + len(extra), plan)


def _pair_add(halves, other, core, name):
    _, nch, r, c = halves.shape
    tr = 256 if r % 256 == 0 else r

    def body(h_ref, o_ref, c_ref, p_ref):
        mine = jnp.where(c_ref[...] == 0, h_ref[0].astype(F32), h_ref[1].astype(F32))
        p_ref[...] = (mine + o_ref[...].astype(F32)).astype(BF16)

    blk = pl.BlockSpec((None, tr, c), lambda j, i: (j, i, 0))
    return _pc(
        body, name, grid=(nch, r // tr),
        in_specs=[pl.BlockSpec((2, None, tr, c), lambda j, i: (0, j, i, 0)), blk, pl.BlockSpec((1, 1), lambda j, i: (0, 0))],
        out_specs=blk, out_shape=_sds((nch, r, c), BF16),
        compiler_params=_params(),
    )(halves, other, core)


def _sum_adamw(parts, w, m, v, name, rider=None):
    nl, r, c = w.shape
    tr = 256 if r % 256 == 0 else r

    def body(*refs):
        p_refs = refs[:nl]
        w_ref, m_ref, v_ref, g_ref, d_ref, mo_ref, vo_ref = refs[nl:]
        for l in range(nl):
            @pl.when(pl.program_id(0) == l)
            def _(p_ref=p_refs[l]):
                g = p_ref[0].astype(F32)
                for j in range(1, p_ref.shape[0]):
                    g = g + p_ref[j].astype(F32)
                mn = ADAM_B1 * m_ref[...] + (1.0 - ADAM_B1) * g
                vn = ADAM_B2 * v_ref[...] + (1.0 - ADAM_B2) * (g * g)
                m_hat = mn / (1.0 - ADAM_B1 ** ADAM_STEP)
                v_hat = vn / (1.0 - ADAM_B2 ** ADAM_STEP)
                g_ref[...] = g
                d_ref[...] = -ADAM_LR * (m_hat / (jnp.sqrt(v_hat) + ADAM_EPS) + ADAM_WD * w_ref[...])
                mo_ref[...] = mn
                vo_ref[...] = vn

    def part_spec(l, k):
        return pl.BlockSpec((k, tr, c), lambda li, i: (0, jnp.where(li == l, i, 0), 0))

    row = pl.BlockSpec((None, tr, c), lambda li, i: (li, i, 0))
    return _call(
        body, name, (*parts, w, m, v), rider, grid=(nl, r // tr),
        in_specs=[part_spec(l, p.shape[0]) for l, p in enumerate(parts)] + [row, row, row],
        out_specs=[row] * 4,
        out_shape=[_sds((nl, r, c), F32)] * 4,
        compiler_params=_params(),
    )


SMALL = ("lower_bounds", "pre_norm_g", "hgrn_norm_g", "fox_f_bias", "pool_w", "pool_scale", "post_norm_g")
SMALL_LANES = 128


def _small_size(tree):
    return sum(tree[k].size for k in SMALL)


def _pack_small(tree, extra=None):
    flat = jnp.concatenate([tree[k].reshape(-1) for k in SMALL] + ([] if extra is None else [extra.reshape(1)]))
    rows = -(-(_small_size(tree) + 1) // (8 * SMALL_LANES)) * 8
    return jnp.pad(flat, (0, rows * SMALL_LANES - flat.shape[0])).reshape(rows, SMALL_LANES)


def _unpack_small(packed, like):
    flat, out, off = packed.reshape(-1), {}, 0
    for k in SMALL:
        size = like[k].size
        out[k] = flat[off:off + size].reshape(like[k].shape)
        off += size
    return out


def _block_diag(pw):
    g = pw.shape[0]
    eye = jnp.eye(g, dtype=pw.dtype)
    return (eye[:, None, :, None] * pw[:, :, None, :]).reshape(g * HEAD, g * HEAD)


def _assemble_w_in(g_in, name):
    _, d, shard = g_in.shape
    tr = min(256, d)
    wide = MAIN_W + FC_PAD

    def body(g_ref, wm_ref, wf_ref, wmt_ref, wft_ref, row_s):
        row_s[:, MAIN_W:] = jnp.zeros((tr, FC_PAD), F32)
        for j in range(N_DEV):
            row_s[:, j * shard:(j + 1) * shard] = g_ref[j].astype(F32)
        wm_ref[...] = row_s[:, :MAIN_W].astype(BF16)
        wf_ref[...] = row_s[:, MAIN_W:].astype(BF16)
        for j in range(0, MAIN_W, 512):
            wmt_ref[j:j + 512, :] = row_s[:, j:j + 512].T.astype(BF16)
        wft_ref[...] = row_s[:, MAIN_W:].T.astype(BF16)

    return _pc(
        body, name, grid=(d // tr,),
        in_specs=[pl.BlockSpec((N_DEV, tr, shard), lambda i: (0, i, 0))],
        out_specs=[pl.BlockSpec((tr, MAIN_W), lambda i: (i, 0)), pl.BlockSpec((tr, FC_PAD), lambda i: (i, 0)),
                   pl.BlockSpec((MAIN_W, tr), lambda i: (0, i)), pl.BlockSpec((FC_PAD, tr), lambda i: (0, i))],
        out_shape=[_sds((d, MAIN_W), BF16), _sds((d, FC_PAD), BF16), _sds((MAIN_W, d), BF16), _sds((FC_PAD, d), BF16)],
        scratch_shapes=[pltpu.VMEM((tr, wide), F32)],
        compiler_params=_params(),
    )(g_in)


def _layer_weights(l, g_in, g_out):
    d = g_in.shape[1]
    w_main, w_fc, w_main_t, w_fc_t = _assemble_w_in(g_in, f"assemble_w_in_{l}")
    full_out = g_out.reshape(N_DEV * g_out.shape[1], d)
    return w_main, w_fc, w_main_t, w_fc_t, full_out, full_out.T


def _layer_fwd(l, x, lbs, weights, lw, nb, rider=None, target=None):
    n = x.shape[0]
    t = n // nb
    w_main, w_fc, _, _, w_out, _ = lw
    bias = jnp.pad(weights["fox_f_bias"][l:l + 1], ((0, 0), (0, FC_PAD - FOX_HEADS)))
    wbd = _block_diag(weights["pool_w"][l]).astype(BF16)
    proj, fc, ht = _in_proj_fwd(x, weights["pre_norm_g"][l:l + 1], w_main, w_fc, f"in_proj_fwd_{l}")
    c_col = _fox_decay_fwd(fc, bias, nb, f"fox_decay_fwd_{l}")
    c_row = c_col.reshape(nb, t, FC_PAD)[:, :, :FOX_HEADS].transpose(0, 2, 1)
    o_h, s0 = _hgrn_fwd(proj, lbs[l:l + 1], _block_ones(HGRN_W, BF16), nb, f"hgrn_fwd_{l}")
    o_b = _pool_fwd(proj, wbd, weights["pool_scale"][l:l + 1], nb, f"pool_fwd_{l}")
    (o_c, lse), rode = _fox_fwd(proj, c_col, c_row, nb, f"fox_fwd_{l}", rider)
    x_next, mixt, y = _merge_fwd(x, proj, o_h, o_b, o_c, weights["hgrn_norm_g"][l:l + 1], w_out,
                                 weights["post_norm_g"][l:l + 1], f"merge_fwd_{l}", target)
    return x_next, (x, proj, fc, ht, c_col, c_row, o_h, s0, o_c, lse, mixt, y, bias, wbd), rode


def _layer_bwd(l, dx, saved, lbs, weights, lw, nb, rider=None):
    _, proj, fc, ht, c_col, c_row, o_h, s0, o_c, lse, mixt, y, bias, wbd = saved
    n = proj.shape[0]
    t = n // nb
    w_out_t = lw[5]
    g = {}
    dy, dmix, dgp = _merge_bwd(dx, y, weights["post_norm_g"][l:l + 1], w_out_t, f"merge_bwd_{l}")
    g["post_norm_g"] = dgp[0]
    g["w_out"] = _w_out_grad(mixt, dy, f"w_out_grad_{l}")
    d_a, dgh, dlb = _hgrn_bwd(dmix, proj, o_h, s0, weights["hgrn_norm_g"][l:l + 1], lbs[l:l + 1],
                              _block_ones(HGRN_W, BF16), nb, f"hgrn_bwd_{l}")
    g["hgrn_norm_g"], g["lbs"] = dgh[0], dlb[0]
    d_b, dwbd, dps = _pool_bwd(dmix, proj, wbd, wbd.T, weights["pool_scale"][l:l + 1], nb, f"pool_bwd_{l}")
    g["pool_w"] = jnp.stack([dwbd[j * HEAD:(j + 1) * HEAD, j * HEAD:(j + 1) * HEAD] for j in range(len(POOL_WINDOWS))])
    g["pool_scale"] = dps[0]
    da, d_gc, delta = _fox_gate_bwd(dmix, proj, o_c, f"fox_gate_bwd_{l}")
    delta_row, lse_row = [a.reshape(nb, t, FC_PAD)[:, :, :FOX_HEADS].transpose(0, 2, 1) for a in (delta, lse)]
    (d_qc, d_kc, d_vc, dc_k, dc_q), rode = _fox_bwd(proj, da, c_col, c_row, lse_row, delta_row, nb, f"fox_bwd_{l}", rider)
    dc_q = jnp.pad(dc_q.transpose(0, 2, 1).reshape(n, FOX_HEADS), ((0, 0), (0, FC_PAD - FOX_HEADS)))
    d_fc, dbias = _fox_decay_bwd(dc_q, dc_k, fc, bias, nb, f"fox_decay_bwd_{l}")
    g["fox_f_bias"] = dbias[0, :FOX_HEADS]
    pieces = [(d_a, C_QA), (d_b, C_UB), (d_qc, C_QC), (d_kc, C_KC), (d_vc, C_VC), (d_gc, C_GC), (d_fc, None)]
    g["w_in"] = _w_in_grad(ht, pieces, f"w_in_grad_{l}")
    return g, pieces, rode


def _layer_bwd_input(l, dx, pieces, saved, weights, lw, rider=None):
    (dxi, dgpre), rode = _in_proj_bwd(pieces, lw[2], lw[3], saved[0], weights["pre_norm_g"][l:l + 1], dx,
                                      f"in_proj_bwd_{l}", rider)
    return dxi, dgpre[0], rode


def kernel(x, lower_bounds, pre_norm_g, w_in, hgrn_norm_g, fox_f_bias, pool_w, pool_scale, w_out, post_norm_g, loss_target, m_lower_bounds, m_pre_norm_g, m_w_in, m_hgrn_norm_g, m_fox_f_bias, m_pool_w, m_pool_scale, m_w_out, m_post_norm_g, v_lower_bounds, v_pre_norm_g, v_w_in, v_hgrn_norm_g, v_fox_f_bias, v_pool_w, v_pool_scale, v_w_out, v_post_norm_g):
    weights = dict(lower_bounds=lower_bounds, pre_norm_g=pre_norm_g, hgrn_norm_g=hgrn_norm_g, fox_f_bias=fox_f_bias,
                   pool_w=pool_w, pool_scale=pool_scale, post_norm_g=post_norm_g)
    mom_m = dict(lower_bounds=m_lower_bounds, pre_norm_g=m_pre_norm_g, hgrn_norm_g=m_hgrn_norm_g, fox_f_bias=m_fox_f_bias,
                 pool_w=m_pool_w, pool_scale=m_pool_scale, post_norm_g=m_post_norm_g)
    mom_v = dict(lower_bounds=v_lower_bounds, pre_norm_g=v_pre_norm_g, hgrn_norm_g=v_hgrn_norm_g, fox_f_bias=v_fox_f_bias,
                 pool_w=v_pool_w, pool_scale=v_pool_scale, post_norm_g=v_post_norm_g)
    depth = w_in.shape[0]
    nb, t, d = x.shape
    n = nb * t
    core = lax.axis_index("c").astype(jnp.int32).reshape(1, 1)
    shards = [(w_in[l].astype(BF16), w_out[l].astype(BF16)) for l in range(depth)]
    lbs = _lower_bound_table(lower_bounds, "lower_bound_table")

    lw = [_layer_weights(0, *_gather_weights(*shards[0]))]
    xl, saved = x.reshape(n, d), []
    for l in range(depth):
        last = l + 1 == depth
        rider = None if last else _gather_rider(shards[l + 1])
        xl, sv, rode = _layer_fwd(l, xl, lbs, weights, lw[l], nb, rider, loss_target.reshape(n, d) if last else None)
        saved.append(sv)
        if rode is not None:
            lw.append(_layer_weights(l + 1, *rode))
    dx, sq = xl
    loss_here = 0.5 * jnp.sum(sq) / d

    grads, recv, pending = [None] * depth, [None] * depth, None
    for l in reversed(range(depth)):
        g, pieces, rode = _layer_bwd(l, dx, saved[l], lbs, weights, lw[l], nb, pending)
        if rode is not None:
            recv[l + 1] = rode
        blocks = (g["w_in"], g["w_out"])
        if l > 0:
            pending = _direct_exchange_rider(blocks)
            dx, g["pre_norm_g"], _ = _layer_bwd_input(l, dx, pieces, saved[l], weights, lw[l])
        else:
            other = _run_rider(_swap_rider(blocks), "grad_swap")
            summed = [_pair_add(hv, ot, core, f"grad_pair_add_{i}") for i, (hv, ot) in enumerate(zip(blocks, other))]
            dx, g["pre_norm_g"], recv[l] = _layer_bwd_input(l, dx, pieces, saved[l], weights, lw[l],
                                                            _chip_exchange_rider(summed))
        grads[l] = g
    small = {k: jnp.stack([grads[l][k] for l in range(depth)]) for k in SMALL if k != "lower_bounds"}
    small["lower_bounds"] = _lower_bound_bwd(lower_bounds, jnp.stack([grads[l]["lbs"] for l in range(depth)]),
                                             "lower_bound_bwd")
    (r_small,) = _run_rider(_gather_rider([_pack_small(small, loss_here)]), "small_grads_gather")

    res_in, _ = _sum_adamw([recv[l][0] for l in range(depth)], w_in, m_w_in, v_w_in, "adamw_w_in")
    res_out, _ = _sum_adamw([recv[l][1] for l in range(depth)], w_out, m_w_out, v_w_out, "adamw_w_out")
    res_small, _ = _sum_adamw([r_small], _pack_small(weights)[None], _pack_small(mom_m)[None], _pack_small(mom_v)[None],
                              "adamw_small")
    loss = res_small[0][0].reshape(-1)[_small_size(weights)]

    names = ("lower_bounds", "pre_norm_g", "w_in", "hgrn_norm_g", "fox_f_bias", "pool_w", "pool_scale", "w_out", "post_norm_g")
    outs = [loss, dx.reshape(nb, t, d)]
    for i in range(4):
        full = dict(_unpack_small(res_small[i][0], weights), w_in=res_in[i], w_out=res_out[i])
        outs += [full[k] for k in names]
    return tuple(outs)
```

```python
import functools

import jax
import jax.numpy as jnp
from jax import lax
from jax.experimental import pallas as pl
from jax.experimental.pallas import tpu as pltpu

F32, BF16 = jnp.float32, jnp.bfloat16
HI = lax.Precision.HIGHEST
MESH = pl.DeviceIdType.MESH
AXES = ("x", "y", "c")
N_DEV = 8

NORM_EPS = 1e-6
MASK_VALUE = -1e30
TINY = 1e-30
CHUNK = 64
SUB = 16
HGRN_W, POOL_W, FOX_W = 256, 256, 512
HEAD = 64
FOX_HEADS = 8
POOL_WINDOWS = (2, 4, 8, 16)
POOL_HALO = 16
MAIN_W = 3584
FC_PAD = 128
C_QA, C_FA, C_IA, C_GA, C_UB, C_GB, C_QC, C_KC, C_VC, C_GC = 0, 256, 512, 768, 1024, 1280, 1536, 2048, 2560, 3072
FOX_SCALE = HEAD ** -0.5

ADAM_LR, ADAM_B1, ADAM_B2, ADAM_EPS, ADAM_WD, ADAM_STEP = 0.001, 0.9, 0.999, 1e-08, 0.01, 10

VMEM_LIMIT = 56 * 1024 * 1024


def _pc(fn, name, **kw):
    return pl.pallas_call(fn, name=name, **kw)


def _params(**kw):
    return pltpu.CompilerParams(vmem_limit_bytes=VMEM_LIMIT, **kw)


class _Rider:
    def __init__(self, inputs, out_shapes, n_sems, n_local, plan):
        self.inputs, self.out_shapes, self.n_sems, self.n_local, self.plan = list(inputs), list(out_shapes), n_sems, n_local, plan

    def start(self, ins, outs, *sems):
        sends, _, locs = self.plan(ins, outs, *sems)
        for cp in locs + sends:
            cp.start()

    def wait(self, ins, outs, *sems):
        sends, recvs, locs = self.plan(ins, outs, *sems)
        for cp in recvs:
            cp.wait_recv()
        for cp in sends:
            cp.wait_send()
        for cp in locs:
            cp.wait()

    def sem_shapes(self):
        return [pltpu.SemaphoreType.DMA((self.n_sems,)), pltpu.SemaphoreType.DMA((self.n_sems,)),
                pltpu.SemaphoreType.DMA((self.n_local,))]


def _call(body, name, args, rider=None, *, grid, in_specs, out_specs, out_shape, scratch_shapes=(), **kw):
    if rider is None:
        res = _pc(body, name, grid=grid, in_specs=in_specs, out_specs=out_specs, out_shape=out_shape,
                  scratch_shapes=list(scratch_shapes), **kw)(*args)
        return res, None
    n_in, n_out, n_scr = len(in_specs), len(out_specs), len(scratch_shapes)
    n_rin, n_rout = len(rider.inputs), len(rider.out_shapes)

    def ridden(*refs):
        ins, refs = refs[:n_in], refs[n_in:]
        rins, refs = refs[:n_rin], refs[n_rin:]
        outs, refs = refs[:n_out], refs[n_out:]
        routs, refs = refs[:n_rout], refs[n_rout:]
        scr, sems = refs[:n_scr], refs[n_scr:]
        first = functools.reduce(jnp.logical_and, [pl.program_id(a) == 0 for a in range(len(grid))])
        last = functools.reduce(jnp.logical_and, [pl.program_id(a) == g - 1 for a, g in enumerate(grid)])

        @pl.when(first)
        def _():
            rider.start(rins, routs, *sems)

        body(*ins, *outs, *scr)

        @pl.when(last)
        def _():
            rider.wait(rins, routs, *sems)

    any_spec = pl.BlockSpec(memory_space=pl.ANY)
    res = _pc(ridden, name, grid=grid, in_specs=list(in_specs) + [any_spec] * n_rin,
              out_specs=list(out_specs) + [any_spec] * n_rout, out_shape=list(out_shape) + rider.out_shapes,
              scratch_shapes=list(scratch_shapes) + rider.sem_shapes(), **kw)(*args, *rider.inputs)
    return res[:n_out], res[n_out:]


def _run_rider(rider, name):
    n_rin = len(rider.inputs)

    def body(*refs):
        ins, outs, sems = refs[:n_rin], refs[n_rin:n_rin + len(rider.out_shapes)], refs[n_rin + len(rider.out_shapes):]
        rider.start(ins, outs, *sems)
        rider.wait(ins, outs, *sems)

    any_spec = pl.BlockSpec(memory_space=pl.ANY)
    return _pc(body, name, in_specs=[any_spec] * n_rin, out_specs=[any_spec] * len(rider.out_shapes),
               out_shape=rider.out_shapes, scratch_shapes=rider.sem_shapes())(*rider.inputs)


def _dot(a, b):
    return jnp.dot(a, b, preferred_element_type=F32)


def _dot_nt(a, b):
    return lax.dot_general(a, b, (((1,), (1,)), ((), ())), preferred_element_type=F32)


def _dot_tn(a, b):
    return lax.dot_general(a, b, (((0,), (0,)), ((), ())), preferred_element_type=F32)


def _dot_hi(a, b):
    return jnp.dot(a, b, precision=HI, preferred_element_type=F32)


def _split2(x):
    hi = x.astype(BF16)
    return hi, (x - hi.astype(F32)).astype(BF16)


def _sel_dot(sel, x):
    hi, lo = _split2(x)
    sb = sel.astype(BF16)
    return _dot(sb, hi) + _dot(sb, lo)


def _dot_sel(x, sel):
    hi, lo = _split2(x)
    sb = sel.astype(BF16)
    return _dot(hi, sb) + _dot(lo, sb)


def _sigmoid(x):
    return 1.0 / (1.0 + jnp.exp(-x))


def _block_ones(n, dtype):
    r = lax.broadcasted_iota(jnp.int32, (n, n), 0) // HEAD
    c = lax.broadcasted_iota(jnp.int32, (n, n), 1) // HEAD
    return (r == c).astype(dtype)


def _sds(shape, dtype):
    return jax.ShapeDtypeStruct(shape, dtype)


def _in_proj_fwd(x, g_pre, w_main, w_fc, name):
    n, d = x.shape
    tm = min(512, n)

    def body(x_ref, g_ref, w_ref, wf_ref, proj_ref, fc_ref, ht_ref):
        xv = x_ref[...]
        r = lax.rsqrt(jnp.mean(xv * xv, axis=-1, keepdims=True) + NORM_EPS)
        hf = xv * r * g_ref[...]
        hb = hf.astype(BF16)
        ht_ref[...] = hf.T.astype(BF16)
        for j in range(0, MAIN_W, 512):
            proj_ref[:, j:j + 512] = _dot(hb, w_ref[:, j:j + 512])
        fc_ref[...] = _dot(hb, wf_ref[...])

    return _pc(
        body, name, grid=(n // tm,),
        in_specs=[pl.BlockSpec((tm, d), lambda i: (i, 0)), pl.BlockSpec((1, d), lambda i: (0, 0)),
                  pl.BlockSpec((d, MAIN_W), lambda i: (0, 0)), pl.BlockSpec((d, FC_PAD), lambda i: (0, 0))],
        out_specs=[pl.BlockSpec((tm, MAIN_W), lambda i: (i, 0)), pl.BlockSpec((tm, FC_PAD), lambda i: (i, 0)),
                   pl.BlockSpec((d, tm), lambda i: (0, i))],
        out_shape=[_sds((n, MAIN_W), F32), _sds((n, FC_PAD), F32), _sds((d, n), BF16)],
        compiler_params=_params(),
    )(x, g_pre, w_main, w_fc)


def _fox_decay_fwd(fc, bias, nb, name):
    n = fc.shape[0]
    t = n // nb
    tt = min(256, t)
    nt = t // tt

    def body(fc_ref, b_ref, c_ref, carry):
        i = pl.program_id(1)

        @pl.when(i == 0)
        def _():
            carry[...] = jnp.zeros_like(carry)

        xv = fc_ref[...] + b_ref[...]
        lf = jnp.minimum(xv, 0.0) - jnp.log(1.0 + jnp.exp(-jnp.abs(xv)))
        r = lax.broadcasted_iota(jnp.int32, (tt, tt), 0)
        cc = lax.broadcasted_iota(jnp.int32, (tt, tt), 1)
        cs = _dot_hi((r >= cc).astype(F32), lf) + carry[...]
        c_ref[...] = cs
        carry[...] = cs[tt - 1:tt, :]

    return _pc(
        body, name, grid=(nb, nt),
        in_specs=[pl.BlockSpec((tt, FC_PAD), lambda b, i: (b * nt + i, 0)), pl.BlockSpec((1, FC_PAD), lambda b, i: (0, 0))],
        out_specs=pl.BlockSpec((tt, FC_PAD), lambda b, i: (b * nt + i, 0)),
        out_shape=_sds((n, FC_PAD), F32),
        scratch_shapes=[pltpu.VMEM((1, FC_PAD), F32)],
        compiler_params=_params(),
    )(fc, bias)


def _hgrn_gates(q, z, lb):
    sig = _sigmoid(z)
    sn = _sigmoid(-z)
    f = lb + (1.0 - lb) * sig
    g = jnp.log(jnp.maximum(f, TINY))
    k = (1.0 - lb) * sn
    sq = _sigmoid(q)
    return sig, sn, f, g, k, sq


def _sub_tri(n, lower):
    r = lax.broadcasted_iota(jnp.int32, (n, n), 0)
    c = lax.broadcasted_iota(jnp.int32, (n, n), 1)
    tri = (r >= c) if lower else (r <= c)
    return jnp.logical_and(r // SUB == c // SUB, tri).astype(F32)


def _live_rows(t):
    return 8 * (t // 8 + 1)


def _pad_rows(x):
    return x if x.shape[0] == SUB else jnp.concatenate([x, jnp.zeros((SUB - x.shape[0], x.shape[1]), x.dtype)], axis=0)


def _hgrn_decays(qs, k, b):
    srow = lax.broadcasted_iota(jnp.int32, (SUB, HGRN_W), 0)
    es, ws = [], []
    for t in range(SUB):
        r = _live_rows(t)
        e = jnp.where(srow[:r] <= t, jnp.exp(b[t:t + 1, :] - b[:r]), 0.0)
        es.append(e)
        ws.append(_pad_rows(e * (qs[t:t + 1, :] * k[:r])))
    return srow, es, ws


def _hgrn_state_step(st, k, v, b, bmask):
    bl = b[SUB - 1:SUB, :]
    ktil = k * jnp.exp(bl - b)
    return st * jnp.exp(bl) + _dot_tn(v.astype(BF16), ktil.astype(BF16)) * bmask


def _hgrn_sub_fwd(qs, k, v, b, st, ones_b, bmask, put_row):
    _, _, ws = _hgrn_decays(qs, k, b)
    aexp = _dot(jnp.concatenate(ws, axis=0).astype(BF16), ones_b)
    o = _dot_nt((qs * jnp.exp(b)).astype(BF16), st.astype(BF16))
    for t in range(SUB):
        r = _live_rows(t)
        put_row(t, o[t:t + 1, :] + jnp.sum(aexp[t * SUB:t * SUB + r, :] * v[:r], axis=0, keepdims=True))
    return _hgrn_state_step(st, k, v, b, bmask)


def _hgrn_tile(t):
    return min(256, t)


def _hgrn_fwd(proj, lb, ones_b, nb, name):
    n = proj.shape[0]
    t = n // nb
    tt = _hgrn_tile(t)
    nt = t // tt
    ncs = tt // CHUNK
    w = HGRN_W

    def body(q_ref, z_ref, v_ref, lb_ref, ones_ref, o_ref, s0_ref, st_s, b_s, qs_s, k_s):
        @pl.when(pl.program_id(1) == 0)
        def _():
            st_s[...] = jnp.zeros_like(st_s)

        q = q_ref[...]
        _, _, _, g, k, sq = _hgrn_gates(q, z_ref[...], lb_ref[...])
        b_s[...] = _sel_dot(_sub_tri(tt, True), g)
        qs_s[...] = q * sq
        k_s[...] = k
        bmask = _block_ones(w, F32)
        ones_b = ones_ref[...]

        def chunk(c, carry):
            st = st_s[...]
            s0_ref[c] = st
            base = pl.multiple_of(c * CHUNK, CHUNK)
            for u in range(CHUNK // SUB):
                rows = pl.ds(base + u * SUB, SUB)

                def put_row(i, row, r0=base + u * SUB):
                    o_ref[pl.ds(r0 + i, 1), :] = row

                st = _hgrn_sub_fwd(qs_s[rows, :], k_s[rows, :], v_ref[rows, :], b_s[rows, :], st, ones_b, bmask, put_row)
            st_s[...] = st
            return carry

        lax.fori_loop(0, ncs, chunk, 0)

    def col(j):
        return pl.BlockSpec((tt, w), lambda b, i: (b * nt + i, j))

    return _pc(
        body, name, grid=(nb, nt),
        in_specs=[col(C_QA // w), col(C_FA // w), col(C_IA // w), pl.BlockSpec((1, w), lambda b, i: (0, 0)),
                  pl.BlockSpec((w, w), lambda b, i: (0, 0))],
        out_specs=[pl.BlockSpec((tt, w), lambda b, i: (b * nt + i, 0)),
                   pl.BlockSpec((ncs, w, w), lambda b, i: (b * nt + i, 0, 0))],
        out_shape=[_sds((n, w), F32), _sds((n // CHUNK, w, w), F32)],
        scratch_shapes=[pltpu.VMEM((w, w), F32)] + [pltpu.VMEM((tt, w), F32)] * 3,
        compiler_params=_params(),
    )(proj, proj, proj, lb, ones_b)


def _pool_lane_windows():
    lane = lax.broadcasted_iota(jnp.int32, (1, POOL_W), 1) // HEAD
    wl = jnp.zeros((1, POOL_W), F32)
    for gi, win in enumerate(POOL_WINDOWS):
        wl = jnp.where(lane == gi, float(win), wl)
    return lane, wl


def _pool_select(lane, parts):
    out = parts[-1]
    for gi in range(len(parts) - 2, -1, -1):
        out = jnp.where(lane == gi, parts[gi], out)
    return out


def _pool_mix(u, halo, t0, tt):
    lane, wl = _pool_lane_windows()
    ext = jnp.concatenate([halo, u], axis=0)
    sums, cur, shift = [], ext, 1
    for _ in POOL_WINDOWS:
        cur = cur + pltpu.roll(cur, shift, axis=0)
        sums.append(cur[POOL_HALO:, :])
        shift *= 2
    tpos = (t0 + lax.broadcasted_iota(jnp.int32, (tt, POOL_W), 0)).astype(F32)
    cnt = jnp.minimum(tpos + 1.0, wl)
    return _pool_select(lane, sums) / cnt - u, cnt


def _pool_specs(tt, nt, nhb):
    cu, cg = C_UB // POOL_W, C_GB // POOL_W
    return [pl.BlockSpec((tt, POOL_W), lambda b, i: (b * nt + i, cu)),
            pl.BlockSpec((tt, POOL_W), lambda b, i: (b * nt + i, cg)),
            pl.BlockSpec((POOL_HALO, POOL_W), lambda b, i: (jnp.maximum((b * nt + i) * nhb - 1, 0), cu))]


def _pool_fwd(proj, wbd, scale, nb, name):
    n = proj.shape[0]
    t = n // nb
    tt = min(256, t)
    nt = t // tt
    nhb = tt // POOL_HALO

    def body(u_ref, g_ref, h_ref, w_ref, s_ref, o_ref):
        i = pl.program_id(1)
        halo = jnp.where(i == 0, 0.0, h_ref[...])
        pooled, _ = _pool_mix(u_ref[...], halo, i * tt, tt)
        gv = g_ref[...]
        o_ref[...] = _dot(pooled.astype(BF16), w_ref[...]) * s_ref[...] * (gv * _sigmoid(gv))

    return _pc(
        body, name, grid=(nb, nt),
        in_specs=_pool_specs(tt, nt, nhb) + [pl.BlockSpec((POOL_W, POOL_W), lambda b, i: (0, 0)),
                                             pl.BlockSpec((1, POOL_W), lambda b, i: (0, 0))],
        out_specs=pl.BlockSpec((tt, POOL_W), lambda b, i: (b * nt + i, 0)),
        out_shape=_sds((n, POOL_W), F32),
        compiler_params=_params(),
    )(proj, proj, proj, wbd, scale)


def _rows_reduce(x, op, final):
    while x.shape[0] > 8 and x.shape[0] % 16 == 0:
        half = x.shape[0] // 2
        x = op(x[:half], x[half:])
    return final(x, axis=0, keepdims=True)


def _tri_pair(step, n):
    a = sum([(step >= r * (r + 1) // 2).astype(jnp.int32) for r in range(1, n)], jnp.int32(0))
    return a, step - a * (a + 1) // 2


def _lane_lo():
    return lax.broadcasted_iota(jnp.int32, (1, 2 * HEAD), 1) < HEAD


def _put_col(tile, hh, colv):
    lane = lax.broadcasted_iota(jnp.int32, tile.shape, 1)
    return jnp.where(lane == hh, colv, tile)


def _fox_fwd(proj, c_col, c_row, nb, name, rider=None):
    n = proj.shape[0]
    t = n // nb
    tb = min(256, t)
    nq = t // tb
    pw = 2 * HEAD

    def body(q_ref, k_ref, v_ref, cc_ref, cr_ref, o_ref, lse_ref, m_s, acc_s, cq_s):
        qi, kj = _tri_pair(pl.program_id(1), nq)

        @pl.when(kj == 0)
        def _():
            m_s[...] = jnp.full_like(m_s, -jnp.inf)
            acc_s[...] = jnp.zeros_like(acc_s)
            for hh in range(FOX_HEADS):
                cq_s[hh] = jnp.broadcast_to(cc_ref[:, hh:hh + 1], (tb, pw))

        def block(masked):
            lo = _lane_lo()
            if masked:
                causal = lax.broadcasted_iota(jnp.int32, (tb, tb), 0) >= lax.broadcasted_iota(jnp.int32, (tb, tb), 1)
            for p in range(FOX_HEADS // 2):
                sl = slice(p * pw, (p + 1) * pw)
                qp = q_ref[:, sl] * FOX_SCALE
                kp = k_ref[:, sl].astype(BF16)
                vf = v_ref[:, sl]
                for h in range(2):
                    hh = 2 * p + h
                    lm = lo if h == 0 else jnp.logical_not(lo)
                    s = _dot_nt(jnp.where(lm, qp, 0.0).astype(BF16), kp)
                    s = s + (jnp.tile(cq_s[hh], (1, tb // pw)) - cr_ref[hh:hh + 1, :])
                    if masked:
                        s = jnp.where(causal, s, MASK_VALUE)
                    m_prev = m_s[hh]
                    m_new = jnp.maximum(m_prev, jnp.max(s, axis=1, keepdims=True))
                    alpha = jnp.exp(m_prev - m_new)
                    pe = jnp.exp(s - jnp.tile(m_new, (1, tb // pw)))
                    m_s[hh] = m_new
                    acc_s[hh] = alpha * acc_s[hh] + _dot(pe.astype(BF16), jnp.where(lm, vf, 1.0).astype(BF16))

        @pl.when(kj < qi)
        def _():
            block(False)

        @pl.when(kj == qi)
        def _():
            block(True)
            lo = _lane_lo()
            lse = jnp.zeros((tb, FC_PAD), F32)
            for p in range(FOX_HEADS // 2):
                halves = []
                for h in range(2):
                    hh = 2 * p + h
                    lm = lo if h == 0 else jnp.logical_not(lo)
                    acc = acc_s[hh]
                    swapped = pltpu.roll(acc, HEAD, axis=1)
                    halves.append(acc / swapped)
                    lse = _put_col(lse, hh, m_s[hh] + jnp.log(jnp.where(lm, swapped, acc)))
                o_ref[:, p * pw:(p + 1) * pw] = jnp.where(lo, halves[0], halves[1])
            lse_ref[...] = lse

    def qspec(wd, j):
        return pl.BlockSpec((tb, wd), lambda b, st: (b * nq + _tri_pair(st, nq)[0], j))

    def kspec(j):
        return pl.BlockSpec((tb, FOX_W), lambda b, st: (b * nq + _tri_pair(st, nq)[1], j))

    return _call(
        body, name, (proj, proj, proj, c_col, c_row), rider, grid=(nb, nq * (nq + 1) // 2),
        in_specs=[qspec(FOX_W, C_QC // FOX_W), kspec(C_KC // FOX_W), kspec(C_VC // FOX_W), qspec(FC_PAD, 0),
                  pl.BlockSpec((None, FOX_HEADS, tb), lambda b, st: (b, 0, _tri_pair(st, nq)[1]))],
        out_specs=[qspec(FOX_W, 0), qspec(FC_PAD, 0)],
        out_shape=[_sds((n, FOX_W), F32), _sds((n, FC_PAD), F32)],
        scratch_shapes=[pltpu.VMEM((FOX_HEADS, tb, pw), F32), pltpu.VMEM((FOX_HEADS, tb, pw), F32),
                        pltpu.VMEM((FOX_HEADS, tb, pw), F32)],
        compiler_params=_params(),
    )


def _head_mean(x, ones_f):
    return _dot_sel(x, ones_f) * (1.0 / HEAD)


def _merge_fwd(x, proj, o_h, o_b, o_c, gh, w_out, g_post, name, target=None):
    n, d = x.shape
    tm = min(512, n)

    def body(*refs):
        x_ref, ga_ref, gc_ref, oh_ref, ob_ref, oc_ref, gh_ref, w_ref, gp_ref = refs[:9]
        if target is None:
            xo_ref, mixt_ref, y_ref = refs[9:]
        else:
            t_ref, dx_ref, sq_ref, mixt_ref, y_ref = refs[9:]
        oh = oh_ref[...]
        ones_f = _block_ones(HGRN_W, F32)
        na = oh * lax.rsqrt(_head_mean(oh * oh, ones_f) + NORM_EPS) * gh_ref[...]
        ga, gc = ga_ref[...], gc_ref[...]
        mixed = jnp.concatenate([na * (ga * _sigmoid(ga)), ob_ref[...], oc_ref[...] * (gc * _sigmoid(gc))], axis=1)
        mixt_ref[...] = mixed.T.astype(BF16)
        y = _dot(mixed.astype(BF16), w_ref[...])
        y_ref[...] = y
        xn = x_ref[...] + y * lax.rsqrt(jnp.mean(y * y, axis=-1, keepdims=True) + NORM_EPS) * gp_ref[...]
        if target is None:
            xo_ref[...] = xn
        else:
            @pl.when(pl.program_id(0) == 0)
            def _():
                sq_ref[...] = jnp.zeros_like(sq_ref)

            e = xn - t_ref[...]
            dx_ref[...] = e * (1.0 / d)
            sq_ref[...] += jnp.sum(e * e, axis=0, keepdims=True)

    def row(wd, j=0):
        return pl.BlockSpec((tm, wd), lambda i: (i, j))

    def full(a, b):
        return pl.BlockSpec((a, b), lambda i: (0, 0))

    head = [] if target is None else [target]
    res = _pc(
        body, name, grid=(n // tm,),
        in_specs=[row(d), row(HGRN_W, C_GA // HGRN_W), row(FOX_W, C_GC // FOX_W), row(HGRN_W), row(POOL_W), row(FOX_W),
                  full(1, HGRN_W), full(d, d), full(1, d)] + [row(d)] * len(head),
        out_specs=[row(d)] + [full(1, d)] * len(head) + [pl.BlockSpec((d, tm), lambda i: (0, i)), row(d)],
        out_shape=[_sds((n, d), F32)] + [_sds((1, d), F32)] * len(head) + [_sds((d, n), BF16), _sds((n, d), F32)],
        compiler_params=_params(),
    )(x, proj, proj, o_h, o_b, o_c, gh, w_out, g_post, *head)
    return (res[0], res[1], res[2]) if target is None else ((res[0], res[1]), res[2], res[3])


def _rms_bwd(dy_scaled, xhat, r):
    return r * (dy_scaled - xhat * jnp.mean(dy_scaled * xhat, axis=-1, keepdims=True))


def _merge_bwd(dxo, y, g_post, w_out_t, name):
    n, d = y.shape
    tm = min(512, n)

    def body(dx_ref, y_ref, gp_ref, wt_ref, dy_ref, dm_ref, dgp_ref):
        @pl.when(pl.program_id(0) == 0)
        def _():
            dgp_ref[...] = jnp.zeros_like(dgp_ref)

        yv, dxv = y_ref[...], dx_ref[...]
        r = lax.rsqrt(jnp.mean(yv * yv, axis=-1, keepdims=True) + NORM_EPS)
        yh = yv * r
        dgp_ref[...] += jnp.sum(dxv * yh, axis=0, keepdims=True)
        dyb = _rms_bwd(dxv * gp_ref[...], yh, r).astype(BF16)
        dy_ref[...] = dyb
        dm_ref[...] = _dot(dyb, wt_ref[...])

    row = pl.BlockSpec((tm, d), lambda i: (i, 0))
    return _pc(
        body, name, grid=(n // tm,),
        in_specs=[row, row, pl.BlockSpec((1, d), lambda i: (0, 0)), pl.BlockSpec((d, d), lambda i: (0, 0))],
        out_specs=[row, row, pl.BlockSpec((1, d), lambda i: (0, 0))],
        out_shape=[_sds((n, d), BF16), _sds((n, d), F32), _sds((1, d), F32)],
        compiler_params=_params(),
    )(dxo, y, g_post, w_out_t)


def _w_out_grad(mixt, dy, name):
    d, n = mixt.shape
    rows = d // N_DEV

    def body(a_ref, b_ref, o_ref):
        o_ref[...] = _dot(a_ref[...], b_ref[...]).astype(BF16)

    return _pc(
        body, name, grid=(N_DEV,),
        in_specs=[pl.BlockSpec((rows, n), lambda j: (j, 0)), pl.BlockSpec((n, d), lambda j: (0, 0))],
        out_specs=pl.BlockSpec((None, None, rows, d), lambda j: (j % 2, j // 2, 0, 0)),
        out_shape=_sds((2, N_DEV // 2, rows, d), BF16),
        compiler_params=_params(),
    )(mixt, dy)


def _w_in_grad(ht, pieces, name):
    d, n = ht.shape
    ta, tk = min(512, d), min(512, n)
    nk = n // tk
    arrays = [p for p, _ in pieces]
    widths = [p.shape[1] for p in arrays]
    offs = [sum(widths[:i]) for i in range(len(widths))]
    in_w = MAIN_W + FOX_HEADS
    shard = in_w // N_DEV

    def body(*refs):
        a_ref, p_refs = refs[0], refs[1:1 + len(arrays)]
        o_ref, acc = refs[1 + len(arrays):]
        k = pl.program_id(1)

        @pl.when(k == 0)
        def _():
            acc[...] = jnp.zeros_like(acc)

        a = a_ref[...]
        for pr, off, wd in zip(p_refs, offs, widths):
            for j in range(0, wd, 512):
                jw = min(512, wd - j)
                acc[:, off + j:off + j + jw] += _dot(a, pr[:, j:j + jw])

        @pl.when(k == nk - 1)
        def _():
            for j in range(N_DEV):
                o_ref[j % 2, j // 2] = acc[:, j * shard:(j + 1) * shard].astype(BF16)

    return _pc(
        body, name, grid=(d // ta, nk),
        in_specs=[pl.BlockSpec((ta, tk), lambda i, k: (i, k))] + [pl.BlockSpec((tk, wd), lambda i, k: (k, 0)) for wd in widths],
        out_specs=pl.BlockSpec((2, N_DEV // 2, ta, shard), lambda i, k: (0, 0, i, 0)),
        out_shape=_sds((2, N_DEV // 2, d, shard), BF16),
        scratch_shapes=[pltpu.VMEM((ta, sum(widths)), F32)],
        compiler_params=_params(),
    )(ht, *arrays)


def _hgrn_sub_bwd(qs, k, v, b, do, s0, ds1, ones_b, bmask, put_dq_row):
    bl = b[SUB - 1:SUB, :]
    eb, ebl, ekt = jnp.exp(b), jnp.exp(bl), jnp.exp(bl - b)
    qe, ktil = qs * eb, k * ekt
    ds1b, dob = ds1.astype(BF16), do.astype(BF16)
    dv = _dot_nt(ktil.astype(BF16), ds1b)
    dqe = _dot(dob, s0.astype(BF16))
    dktil = _dot(v.astype(BF16), ds1b)
    dbl = jnp.sum(dktil * ktil, axis=0, keepdims=True) + ebl * jnp.sum(s0 * ds1, axis=0, keepdims=True)
    ds0 = ds1 * ebl + _dot_tn(dob, qe.astype(BF16)) * bmask
    srow, es, ws = _hgrn_decays(qs, k, b)
    aexp = _dot(jnp.concatenate(ws, axis=0).astype(BF16), ones_b)
    gs = [_pad_rows(do[t:t + 1, :] * v[:_live_rows(t)]) for t in range(SUB)]
    gexp = _dot(jnp.concatenate(gs, axis=0).astype(BF16), ones_b)
    dq = dqe * eb
    dk = dktil * ekt
    dks = [dk[j:j + 8] for j in range(0, SUB, 8)]
    dvs = [dv[j:j + 8] for j in range(0, SUB, 8)]
    for t in range(SUB):
        r = _live_rows(t)
        ge = gexp[t * SUB:t * SUB + r, :] * es[t]
        put_dq_row(t, dq[t:t + 1, :] + jnp.sum(ge * k[:r], axis=0, keepdims=True))
        for j in range(r // 8):
            dks[j] = dks[j] + ge[8 * j:8 * j + 8] * qs[t:t + 1, :]
            dvs[j] = dvs[j] + aexp[t * SUB + 8 * j:t * SUB + 8 * j + 8, :] * do[t:t + 1, :]
    return jnp.concatenate(dks, axis=0), jnp.concatenate(dvs, axis=0), dbl, ds0


def _hgrn_bwd(dmix, proj, o_h, s0, gh, lb, ones_b, nb, name):
    n = proj.shape[0]
    t = n // nb
    tt = _hgrn_tile(t)
    nt = t // tt
    ncs = tt // CHUNK
    nsub = CHUNK // SUB
    w = HGRN_W

    def body(dm_ref, q_ref, z_ref, v_ref, ga_ref, oh_ref, s0_ref, gh_ref, lb_ref, ones_ref,
             dp_ref, dgh_ref, dlb_ref, ds_s, ss_s, b_s, qs_s, k_s, do_s, dq_s, dk_s, dv_s, dbl_s):
        first = jnp.logical_and(pl.program_id(0) == 0, pl.program_id(1) == 0)

        @pl.when(first)
        def _():
            dgh_ref[...] = jnp.zeros_like(dgh_ref)
            dlb_ref[...] = jnp.zeros_like(dlb_ref)

        @pl.when(pl.program_id(1) == 0)
        def _():
            ds_s[...] = jnp.zeros_like(ds_s)

        ones_b = ones_ref[...]
        ones_f = ones_b.astype(F32)
        bmask = _block_ones(w, F32)
        lbv, ghv = lb_ref[...], gh_ref[...]
        oh, ga, dm = oh_ref[...], ga_ref[...], dm_ref[...]
        rn = lax.rsqrt(_head_mean(oh * oh, ones_f) + NORM_EPS)
        nh = oh * rn
        sga = _sigmoid(ga)
        dp_ref[:, 3 * w:4 * w] = (dm * nh * ghv * (sga * (1.0 + ga * (1.0 - sga)))).astype(BF16)
        dn = dm * (ga * sga)
        dgh_ref[...] += jnp.sum(dn * nh, axis=0, keepdims=True)
        dn = dn * ghv
        do_s[...] = rn * (dn - nh * _head_mean(dn * nh, ones_f))
        q = q_ref[...]
        sig, sn, f, g, k, sq = _hgrn_gates(q, z_ref[...], lbv)
        qs = q * sq
        b_s[...] = _sel_dot(_sub_tri(tt, True), g)
        qs_s[...] = qs
        k_s[...] = k

        def chunk(cc, carry):
            c = ncs - 1 - cc
            base = pl.multiple_of(c * CHUNK, CHUNK)
            st = s0_ref[c]
            for u in range(nsub):
                ss_s[u] = st
                if u < nsub - 1:
                    rows = pl.ds(base + u * SUB, SUB)
                    st = _hgrn_state_step(st, k_s[rows, :], v_ref[rows, :], b_s[rows, :], bmask)
            ds = ds_s[...]
            for u in reversed(range(nsub)):
                rows = pl.ds(base + u * SUB, SUB)

                def put_dq_row(i, row, r0=base + u * SUB):
                    dq_s[pl.ds(r0 + i, 1), :] = row

                dk, dv, dbl, ds = _hgrn_sub_bwd(qs_s[rows, :], k_s[rows, :], v_ref[rows, :], b_s[rows, :],
                                                do_s[rows, :], ss_s[u], ds, ones_b, bmask, put_dq_row)
                dk_s[rows, :] = dk
                dv_s[rows, :] = dv
                dbl_s[rows, :] = jnp.broadcast_to(dbl, (SUB, w))
            ds_s[...] = ds
            return carry

        lax.fori_loop(0, ncs, chunk, 0)
        dqs, dk = dq_s[...], dk_s[...]
        dg = _sel_dot(_sub_tri(tt, False), qs * dqs - k * dk) + dbl_s[...]
        dfz = jnp.where(f > TINY, dg / jnp.maximum(f, TINY), 0.0)
        dlb_ref[...] += jnp.sum(dfz * (1.0 - sig) - dk * sn, axis=0, keepdims=True)
        dp_ref[:, 0:w] = (dqs * (sq * (1.0 + q * (1.0 - sq)))).astype(BF16)
        dp_ref[:, w:2 * w] = ((dfz - dk) * (1.0 - lbv) * sig * sn).astype(BF16)
        dp_ref[:, 2 * w:3 * w] = dv_s[...].astype(BF16)

    def rv(b, i):
        return b * nt + (nt - 1 - i)

    def col(j):
        return pl.BlockSpec((tt, w), lambda b, i: (rv(b, i), j))

    def full(a, bb):
        return pl.BlockSpec((a, bb), lambda b, i: (0, 0))

    return _pc(
        body, name, grid=(nb, nt),
        in_specs=[col(0), col(C_QA // w), col(C_FA // w), col(C_IA // w), col(C_GA // w), col(0),
                  pl.BlockSpec((ncs, w, w), lambda b, i: (rv(b, i), 0, 0)), full(1, w), full(1, w), full(w, w)],
        out_specs=[pl.BlockSpec((tt, 4 * w), lambda b, i: (rv(b, i), 0)), full(1, w), full(1, w)],
        out_shape=[_sds((n, 4 * w), BF16), _sds((1, w), F32), _sds((1, w), F32)],
        scratch_shapes=[pltpu.VMEM((w, w), F32), pltpu.VMEM((nsub, w, w), F32)] + [pltpu.VMEM((tt, w), F32)] * 8,
        compiler_params=_params(),
    )(dmix, proj, proj, proj, proj, o_h, s0, gh, lb, ones_b)


def _pool_bwd(dmix, proj, wbd, wbd_t, scale, nb, name):
    n = proj.shape[0]
    t = n // nb
    tt = min(256, t)
    nt = t // tt
    nhb = tt // POOL_HALO
    cu, cg, cm = C_UB // POOL_W, C_GB // POOL_W, HGRN_W // POOL_W

    def body(u_ref, g_ref, h_ref, dm_ref, gn_ref, dmn_ref, w_ref, wt_ref, s_ref, dp_ref, dw_ref, ds_ref):
        i = pl.program_id(1)
        first = jnp.logical_and(pl.program_id(0) == 0, i == 0)

        @pl.when(first)
        def _():
            dw_ref[...] = jnp.zeros_like(dw_ref)
            ds_ref[...] = jnp.zeros_like(ds_ref)

        sc = s_ref[...]
        halo = jnp.where(i == 0, 0.0, h_ref[...])
        pooled, cnt = _pool_mix(u_ref[...], halo, i * tt, tt)
        pb = pooled.astype(BF16)
        pre = _dot(pb, w_ref[...])
        gv, dm = g_ref[...], dm_ref[...]
        sg = _sigmoid(gv)
        silu = gv * sg
        dgb = dm * pre * sc * (sg * (1.0 + gv * (1.0 - sg)))
        ds_ref[...] += jnp.sum(dm * pre * silu, axis=0, keepdims=True)
        dpre = (dm * sc * silu).astype(BF16)
        dw_ref[...] += _dot_tn(pb, dpre)
        dpool = _dot(dpre, wt_ref[...])
        gn = gn_ref[...]
        dpre_n = (dmn_ref[...] * sc * (gn * _sigmoid(gn))).astype(BF16)
        dpool_n = jnp.where(i == nt - 1, 0.0, _dot(dpre_n, wt_ref[...]))
        lane, wl = _pool_lane_windows()
        tpos_n = ((i + 1) * tt + lax.broadcasted_iota(jnp.int32, (POOL_HALO, POOL_W), 0)).astype(F32)
        ext = jnp.concatenate([dpool / cnt, dpool_n / jnp.minimum(tpos_n + 1.0, wl)], axis=0)
        rows = tt + POOL_HALO
        sums, cur, shift = [], ext, 1
        for _ in POOL_WINDOWS:
            cur = cur + pltpu.roll(cur, rows - shift, axis=0)
            sums.append(cur[:tt, :])
            shift *= 2
        du = _pool_select(lane, sums) - dpool
        dp_ref[...] = jnp.concatenate([du, dgb], axis=1).astype(BF16)

    def nxt(b, i):
        return jnp.minimum((b * nt + i + 1) * nhb, n // POOL_HALO - 1)

    return _pc(
        body, name, grid=(nb, nt),
        in_specs=_pool_specs(tt, nt, nhb) + [
            pl.BlockSpec((tt, POOL_W), lambda b, i: (b * nt + i, cm)),
            pl.BlockSpec((POOL_HALO, POOL_W), lambda b, i: (nxt(b, i), cg)),
            pl.BlockSpec((POOL_HALO, POOL_W), lambda b, i: (nxt(b, i), cm)),
            pl.BlockSpec((POOL_W, POOL_W), lambda b, i: (0, 0)), pl.BlockSpec((POOL_W, POOL_W), lambda b, i: (0, 0)),
            pl.BlockSpec((1, POOL_W), lambda b, i: (0, 0))],
        out_specs=[pl.BlockSpec((tt, 2 * POOL_W), lambda b, i: (b * nt + i, 0)),
                   pl.BlockSpec((POOL_W, POOL_W), lambda b, i: (0, 0)), pl.BlockSpec((1, POOL_W), lambda b, i: (0, 0))],
        out_shape=[_sds((n, 2 * POOL_W), BF16), _sds((POOL_W, POOL_W), F32), _sds((1, POOL_W), F32)],
        compiler_params=_params(),
    )(proj, proj, proj, dmix, proj, dmix, wbd, wbd_t, scale)


def _fox_gate_bwd(dmix, proj, o_c, name):
    n = proj.shape[0]
    tm = min(512, n)

    def body(dm_ref, gc_ref, oc_ref, da_ref, dg_ref, dl_ref):
        dm, gc, oc = dm_ref[...], gc_ref[...], oc_ref[...]
        sg = _sigmoid(gc)
        da = dm * (gc * sg)
        da_ref[...] = da.astype(BF16)
        dg_ref[...] = (dm * oc * (sg * (1.0 + gc * (1.0 - sg)))).astype(BF16)
        r = lax.broadcasted_iota(jnp.int32, (FOX_W, FC_PAD), 0) // HEAD
        c = lax.broadcasted_iota(jnp.int32, (FOX_W, FC_PAD), 1)
        dl_ref[...] = _dot_sel(da * oc, (r == c).astype(F32))

    def row(wd, j=0):
        return pl.BlockSpec((tm, wd), lambda i: (i, j))

    return _pc(
        body, name, grid=(n // tm,),
        in_specs=[row(FOX_W, (HGRN_W + POOL_W) // FOX_W), row(FOX_W, C_GC // FOX_W), row(FOX_W)],
        out_specs=[row(FOX_W), row(FOX_W), row(FC_PAD)],
        out_shape=[_sds((n, FOX_W), BF16), _sds((n, FOX_W), BF16), _sds((n, FC_PAD), F32)],
        compiler_params=_params(),
    )(dmix, proj, o_c)


def _fox_bwd(proj, da, c_col, c_row, lse_row, delta_row, nb, name, rider=None):
    n = proj.shape[0]
    t = n // nb
    tb = min(256, t)
    nq = t // tb
    pw = 2 * HEAD

    def body(q_ref, k_ref, v_ref, da_ref, cc_ref, cr_ref, lse_ref, dl_ref,
             dq_ref, dk_ref, dv_ref, dck_ref, dcq_ref, dq_s, dk_s, dv_s, dck_s, dcq_s):
        step = pl.program_id(1)
        kj, qi = pairs(step)

        @pl.when(step == 0)
        def _():
            dq_s[...] = jnp.zeros_like(dq_s)
            dcq_s[...] = jnp.zeros_like(dcq_s)

        @pl.when(qi == nq - 1)
        def _():
            dk_s[...] = jnp.zeros_like(dk_s)
            dv_s[...] = jnp.zeros_like(dv_s)
            dck_s[...] = jnp.zeros_like(dck_s)

        def block(masked):
            lo = _lane_lo()
            if masked:
                causal = lax.broadcasted_iota(jnp.int32, (tb, tb), 1) >= lax.broadcasted_iota(jnp.int32, (tb, tb), 0)
            dck = dck_s[...]
            for p in range(FOX_HEADS // 2):
                sl = slice(p * pw, (p + 1) * pw)
                qp = q_ref[:, sl] * FOX_SCALE
                kf = k_ref[:, sl]
                kp = kf.astype(BF16)
                kt = kf.T.astype(BF16)
                vp = v_ref[:, sl].astype(BF16)
                dap = da_ref[:, sl]
                dk, dv = dk_s[:, sl], dv_s[:, sl]
                for h in range(2):
                    hh = 2 * p + h
                    lm = lo if h == 0 else jnp.logical_not(lo)
                    qm = jnp.where(lm, qp, 0.0).astype(BF16)
                    dam = jnp.where(lm, dap, jnp.zeros_like(dap))
                    s = _dot_nt(kp, qm) + (cr_ref[hh:hh + 1, :] - cc_ref[:, hh:hh + 1])
                    pe = jnp.exp(s - lse_ref[hh:hh + 1, :])
                    if masked:
                        pe = jnp.where(causal, pe, 0.0)
                    dp = _dot_nt(vp, dam)
                    ds = pe * (dp - dl_ref[hh:hh + 1, :])
                    dsb = ds.astype(BF16)
                    dv = dv + _dot(pe.astype(BF16), dam)
                    dk = dk + _dot(dsb, qm)
                    rows = slice(hh * HEAD, (hh + 1) * HEAD)
                    dq_s[qi, rows, :] += _dot(kt[h * HEAD:(h + 1) * HEAD, :], dsb)
                    dck = dck - _put_col(jnp.zeros_like(dck), hh, jnp.sum(ds, axis=1, keepdims=True))
                    dcq_s[qi, hh:hh + 1, :] += _rows_reduce(ds, jnp.add, jnp.sum)
                dk_s[:, sl] = dk
                dv_s[:, sl] = dv
            dck_s[...] = dck

        @pl.when(qi > kj)
        def _():
            block(False)

        @pl.when(qi == kj)
        def _():
            block(True)

        @pl.when(qi == kj)
        def _():
            dk_ref[...] = dk_s[...].astype(BF16)
            dv_ref[...] = dv_s[...].astype(BF16)
            dck_ref[...] = dck_s[...]

        @pl.when(step == nq * (nq + 1) // 2 - 1)
        def _():
            for j in range(nq):
                dq_ref[j * tb:(j + 1) * tb, :] = (dq_s[j].T * FOX_SCALE).astype(BF16)
                dcq_ref[:, j * tb:(j + 1) * tb] = dcq_s[j]

    def pairs(step):
        a, b = _tri_pair(step, nq)
        return nq - 1 - a, nq - 1 - b

    def kspec(wd, j=0):
        return pl.BlockSpec((tb, wd), lambda b, st: (b * nq + pairs(st)[0], j))

    def qspec(wd, j=0):
        return pl.BlockSpec((tb, wd), lambda b, st: (b * nq + pairs(st)[1], j))

    def qrow():
        return pl.BlockSpec((None, FOX_HEADS, tb), lambda b, st: (b, 0, pairs(st)[1]))

    return _call(
        body, name, (proj, proj, proj, da, c_col, c_row, lse_row, delta_row), rider, grid=(nb, nq * (nq + 1) // 2),
        in_specs=[qspec(FOX_W, C_QC // FOX_W), kspec(FOX_W, C_KC // FOX_W), kspec(FOX_W, C_VC // FOX_W), qspec(FOX_W),
                  kspec(FC_PAD), qrow(), qrow(), qrow()],
        out_specs=[pl.BlockSpec((t, FOX_W), lambda b, st: (b, 0)), kspec(FOX_W), kspec(FOX_W), kspec(FC_PAD),
                   pl.BlockSpec((None, FOX_HEADS, t), lambda b, st: (b, 0, 0))],
        out_shape=[_sds((n, FOX_W), BF16), _sds((n, FOX_W), BF16), _sds((n, FOX_W), BF16), _sds((n, FC_PAD), F32),
                   _sds((nb, FOX_HEADS, t), F32)],
        scratch_shapes=[pltpu.VMEM((nq, FOX_W, tb), F32), pltpu.VMEM((tb, FOX_W), F32), pltpu.VMEM((tb, FOX_W), F32),
                        pltpu.VMEM((tb, FC_PAD), F32), pltpu.VMEM((nq, FOX_HEADS, tb), F32)],
        compiler_params=_params(),
    )


def _fox_decay_bwd(dc_q, dc_k, fc, bias, nb, name):
    n = fc.shape[0]
    t = n // nb
    tt = min(256, t)
    nt = t // tt

    def body(dcq_ref, dck_ref, fc_ref, b_ref, dfc_ref, db_ref, carry):
        i = pl.program_id(1)
        first = jnp.logical_and(pl.program_id(0) == 0, i == 0)

        @pl.when(first)
        def _():
            db_ref[...] = jnp.zeros_like(db_ref)

        @pl.when(i == 0)
        def _():
            carry[...] = jnp.zeros_like(carry)

        r = lax.broadcasted_iota(jnp.int32, (tt, tt), 0)
        cc = lax.broadcasted_iota(jnp.int32, (tt, tt), 1)
        dlf = _dot_hi((r <= cc).astype(F32), dcq_ref[...] + dck_ref[...]) + carry[...]
        carry[...] = dlf[0:1, :]
        dfc = dlf * _sigmoid(-(fc_ref[...] + b_ref[...]))
        dfc_ref[...] = dfc.astype(BF16)
        db_ref[...] += jnp.sum(dfc, axis=0, keepdims=True)

    def row():
        return pl.BlockSpec((tt, FC_PAD), lambda b, i: (b * nt + (nt - 1 - i), 0))

    return _pc(
        body, name, grid=(nb, nt),
        in_specs=[row(), row(), row(), pl.BlockSpec((1, FC_PAD), lambda b, i: (0, 0))],
        out_specs=[row(), pl.BlockSpec((1, FC_PAD), lambda b, i: (0, 0))],
        out_shape=[_sds((n, FC_PAD), BF16), _sds((1, FC_PAD), F32)],
        scratch_shapes=[pltpu.VMEM((1, FC_PAD), F32)],
        compiler_params=_params(),
    )(dc_q, dc_k, fc, bias)


def _in_proj_bwd(pieces, w_main_t, w_fc_t, x, g_pre, dxo, name, rider=None):
    n, d = x.shape
    tm = min(512, n)
    widths = [p.shape[1] for p, _ in pieces]
    offs = [o for _, o in pieces]
    np_ = len(pieces)

    def body(*refs):
        p_refs = refs[:np_]
        wt_ref, wf_ref, x_ref, g_ref, dxo_ref, dx_ref, dg_ref = refs[np_:]

        @pl.when(pl.program_id(0) == 0)
        def _():
            dg_ref[...] = jnp.zeros_like(dg_ref)

        dh = _dot(p_refs[-1][...], wf_ref[...])
        for pr, wd, off in zip(p_refs[:-1], widths[:-1], offs[:-1]):
            for j in range(0, wd, 512):
                jw = min(512, wd - j)
                dh = dh + _dot(pr[:, j:j + jw], wt_ref[off + j:off + j + jw, :])
        xv = x_ref[...]
        r = lax.rsqrt(jnp.mean(xv * xv, axis=-1, keepdims=True) + NORM_EPS)
        xh = xv * r
        dg_ref[...] += jnp.sum(dh * xh, axis=0, keepdims=True)
        dx_ref[...] = dxo_ref[...] + _rms_bwd(dh * g_ref[...], xh, r)

    row = pl.BlockSpec((tm, d), lambda i: (i, 0))
    return _call(
        body, name, (*[p for p, _ in pieces], w_main_t, w_fc_t, x, g_pre, dxo), rider, grid=(n // tm,),
        in_specs=[pl.BlockSpec((tm, wd), lambda i: (i, 0)) for wd in widths] + [
            pl.BlockSpec((MAIN_W, d), lambda i: (0, 0)), pl.BlockSpec((FC_PAD, d), lambda i: (0, 0)),
            row, pl.BlockSpec((1, d), lambda i: (0, 0)), row],
        out_specs=[row, pl.BlockSpec((1, d), lambda i: (0, 0))],
        out_shape=[_sds((n, d), F32), _sds((1, d), F32)],
        compiler_params=_params(),
    )


def _lower_bound_table(lower_bounds, name):
    depth, w = lower_bounds.shape

    def body(lb_ref, o_ref):
        v = lb_ref[...]
        e = jnp.exp(v - jnp.max(v, axis=0, keepdims=True))
        p = e / jnp.sum(e, axis=0, keepdims=True)
        acc = jnp.zeros((1, w), F32)
        for l in range(depth):
            acc = acc + p[l:l + 1, :]
            o_ref[l:l + 1, :] = acc - p[0:1, :]

    return _pc(body, name, out_shape=_sds((depth, w), F32))(lower_bounds)


def _lower_bound_bwd(lower_bounds, dlbs, name):
    depth, w = lower_bounds.shape

    def body(lb_ref, d_ref, o_ref):
        v, dl = lb_ref[...], d_ref[...]
        e = jnp.exp(v - jnp.max(v, axis=0, keepdims=True))
        p = e / jnp.sum(e, axis=0, keepdims=True)
        tot = jnp.sum(dl, axis=0, keepdims=True)
        rows, tail = [], tot
        for l in range(depth):
            rows.append(tail - tot if l == 0 else tail)
            tail = tail - dl[l:l + 1, :]
        dp = jnp.concatenate(rows, axis=0)
        o_ref[...] = p * (dp - jnp.sum(p * dp, axis=0, keepdims=True))

    return _pc(body, name, out_shape=_sds((depth, w), F32))(lower_bounds, dlbs)


def _place():
    x, y, c = lax.axis_index("x"), lax.axis_index("y"), lax.axis_index("c")
    return x, y, c


def _gather_weights(w_in_b, w_out_b):
    arrays = (w_in_b, w_out_b)
    na = len(arrays)

    def body(*refs):
        ins, outs = refs[:na], refs[na:2 * na]
        send_sems, recv_sems, local_sems = refs[2 * na:]
        x, y, c = _place()
        me, sibling = (x, y, c), (x, y, 1 - c)
        chips = [(1 - x, y), (x, 1 - y), (1 - x, 1 - y)]

        def slot(a, px, py, pc):
            return outs[a].at[4 * px + 2 * py + pc]

        def copy(a, k, block, to, own=False):
            return pltpu.make_async_remote_copy(
                src_ref=ins[a] if own else slot(a, *block), dst_ref=slot(a, *block),
                send_sem=send_sems.at[a * 7 + k], recv_sem=recv_sems.at[a * 7 + k],
                device_id=to, device_id_type=MESH)

        mine = [pltpu.make_async_copy(ins[a], slot(a, *me), local_sems.at[a]) for a in range(na)]
        for cp in mine:
            cp.start()
        first = []
        for a in range(na):
            first.append(copy(a, 0, me, sibling, own=True))
            first += [copy(a, 1 + j, me, (*chip, c), own=True) for j, chip in enumerate(chips)]
        for cp in first:
            cp.start()
        passed = []
        for j, chip in enumerate(chips):
            for a in range(na):
                copy(a, 1 + j, (*chip, c), me).wait_recv()
                fw = copy(a, 4 + j, (*chip, c), sibling)
                fw.start()
                passed.append(fw)
        for a in range(na):
            copy(a, 0, sibling, me).wait_recv()
            for j, chip in enumerate(chips):
                copy(a, 4 + j, (*chip, 1 - c), me).wait_recv()
        for cp in first + passed:
            cp.wait_send()
        for cp in mine:
            cp.wait()

    any_spec = pl.BlockSpec(memory_space=pl.ANY)
    return _pc(
        body, "gather_weights",
        in_specs=[any_spec] * na, out_specs=[any_spec] * na,
        out_shape=[_sds((N_DEV,) + a.shape, a.dtype) for a in arrays],
        scratch_shapes=[pltpu.SemaphoreType.DMA((7 * na,)), pltpu.SemaphoreType.DMA((7 * na,)),
                        pltpu.SemaphoreType.DMA((na,))],
    )(*arrays)


def _peer(k):
    x, y, c = _place()
    return (1 - x if k & 4 else x, 1 - y if k & 2 else y, 1 - c if k & 1 else c)


def _remote(src, dst, sems, s, to):
    return pltpu.make_async_remote_copy(src_ref=src, dst_ref=dst, send_sem=sems[0].at[s], recv_sem=sems[1].at[s],
                                        device_id=to, device_id_type=MESH)


def _gather_rider(shards):
    na = len(shards)

    def plan(ins, outs, *sems):
        x, y, c = _place()
        me = 4 * x + 2 * y + c
        locs = [pltpu.make_async_copy(ins[a], outs[a].at[me], sems[2].at[a]) for a in range(na)]
        sends, recvs = [], []
        for k in range(1, N_DEV):
            px, py, pc = _peer(k)
            for a in range(na):
                s = (k - 1) * na + a
                sends.append(_remote(ins[a], outs[a].at[me], sems, s, (px, py, pc)))
                recvs.append(_remote(ins[a], outs[a].at[4 * px + 2 * py + pc], sems, s, (px, py, pc)))
        return sends, recvs, locs

    return _Rider(shards, [_sds((N_DEV,) + a.shape, a.dtype) for a in shards], (N_DEV - 1) * na, na, plan)


def _direct_exchange_rider(blocks):
    na = len(blocks)

    def plan(ins, outs, *sems):
        x, y, c = _place()
        me = 4 * x + 2 * y + c
        locs = [pltpu.make_async_copy(ins[a].at[c, 2 * x + y], outs[a].at[me], sems[2].at[a]) for a in range(na)]
        sends, recvs = [], []
        for k in range(1, N_DEV):
            px, py, pc = _peer(k)
            for a in range(na):
                s = (k - 1) * na + a
                sends.append(_remote(ins[a].at[pc, 2 * px + py], outs[a].at[me], sems, s, (px, py, pc)))
                recvs.append(_remote(ins[a].at[pc, 2 * px + py], outs[a].at[4 * px + 2 * py + pc], sems, s, (px, py, pc)))
        return sends, recvs, locs

    return _Rider(blocks, [_sds((N_DEV,) + a.shape[2:], a.dtype) for a in blocks], (N_DEV - 1) * na, na, plan)


def _swap_rider(halves):
    na = len(halves)

    def plan(ins, outs, *sems):
        x, y, c = _place()
        cps = [_remote(ins[a].at[1 - c], outs[a], sems, a, (x, y, 1 - c)) for a in range(na)]
        return cps, cps, []

    return _Rider(halves, [_sds(a.shape[1:], a.dtype) for a in halves], na, 1, plan)


def _chip_exchange_rider(parts, small=None):
    na = len(parts)
    n_chip = N_DEV // 2

    def plan(ins, outs, *sems):
        x, y, c = _place()
        chip = 2 * x + y
        locs = [pltpu.make_async_copy(ins[a].at[chip], outs[a].at[chip], sems[2].at[a]) for a in range(na)]
        sends, recvs = [], []
        for k in range(1, n_chip):
            px, py, _ = _peer(2 * k)
            for a in range(na):
                s = (k - 1) * na + a
                sends.append(_remote(ins[a].at[2 * px + py], outs[a].at[chip], sems, s, (px, py, c)))
                recvs.append(_remote(ins[a].at[2 * px + py], outs[a].at[2 * px + py], sems, s, (px, py, c)))
        if small is not None:
            me = 2 * chip + c
            locs.append(pltpu.make_async_copy(ins[na], outs[na].at[me], sems[2].at[na]))
            for k in range(1, N_DEV):
                px, py, pc = _peer(k)
                s = (n_chip - 1) * na + k - 1
                sends.append(_remote(ins[na], outs[na].at[me], sems, s, (px, py, pc)))
                recvs.append(_remote(ins[na], outs[na].at[4 * px + 2 * py + pc], sems, s, (px, py, pc)))
        return sends, recvs, locs

    extra = [] if small is None else [small]
    shapes = [_sds(a.shape, a.dtype) for a in parts] + [_sds((N_DEV,) + s.shape, s.dtype) for s in extra]
    n_sems = (n_chip - 1) * na + (N_DEV - 1) * len(extra)
    return _Rider(list(parts) + extra, shapes, n_sems, na + len(extra), plan)


def _pair_add(halves, other, core, name):
    _, nch, r, c = halves.shape
    tr = 256 if r % 256 == 0 else r

    def body(h_ref, o_ref, c_ref, p_ref):
        mine = jnp.where(c_ref[...] == 0, h_ref[0].astype(F32), h_ref[1].astype(F32))
        p_ref[...] = (mine + o_ref[...].astype(F32)).astype(BF16)

    blk = pl.BlockSpec((None, tr, c), lambda j, i: (j, i, 0))
    return _pc(
        body, name, grid=(nch, r // tr),
        in_specs=[pl.BlockSpec((2, None, tr, c), lambda j, i: (0, j, i, 0)), blk, pl.BlockSpec((1, 1), lambda j, i: (0, 0))],
        out_specs=blk, out_shape=_sds((nch, r, c), BF16),
        compiler_params=_params(),
    )(halves, other, core)


def _sum_adamw(parts, w, m, v, name, rider=None):
    nl, r, c = w.shape
    tr = 256 if r % 256 == 0 else r

    def body(*refs):
        p_refs = refs[:nl]
        w_ref, m_ref, v_ref, g_ref, d_ref, mo_ref, vo_ref = refs[nl:]
        for l in range(nl):
            @pl.when(pl.program_id(0) == l)
            def _(p_ref=p_refs[l]):
                g = p_ref[0].astype(F32)
                for j in range(1, p_ref.shape[0]):
                    g = g + p_ref[j].astype(F32)
                mn = ADAM_B1 * m_ref[...] + (1.0 - ADAM_B1) * g
                vn = ADAM_B2 * v_ref[...] + (1.0 - ADAM_B2) * (g * g)
                m_hat = mn / (1.0 - ADAM_B1 ** ADAM_STEP)
                v_hat = vn / (1.0 - ADAM_B2 ** ADAM_STEP)
                g_ref[...] = g
                d_ref[...] = -ADAM_LR * (m_hat / (jnp.sqrt(v_hat) + ADAM_EPS) + ADAM_WD * w_ref[...])
                mo_ref[...] = mn
                vo_ref[...] = vn

    def part_spec(l, k):
        return pl.BlockSpec((k, tr, c), lambda li, i: (0, jnp.where(li == l, i, 0), 0))

    row = pl.BlockSpec((None, tr, c), lambda li, i: (li, i, 0))
    return _call(
        body, name, (*parts, w, m, v), rider, grid=(nl, r // tr),
        in_specs=[part_spec(l, p.shape[0]) for l, p in enumerate(parts)] + [row, row, row],
        out_specs=[row] * 4,
        out_shape=[_sds((nl, r, c), F32)] * 4,
        compiler_params=_params(),
    )


SMALL = ("lower_bounds", "pre_norm_g", "hgrn_norm_g", "fox_f_bias", "pool_w", "pool_scale", "post_norm_g")
SMALL_LANES = 128


def _small_size(tree):
    return sum(tree[k].size for k in SMALL)


def _pack_small(tree, extra=None):
    flat = jnp.concatenate([tree[k].reshape(-1) for k in SMALL] + ([] if extra is None else [extra.reshape(1)]))
    rows = -(-(_small_size(tree) + 1) // (8 * SMALL_LANES)) * 8
    return jnp.pad(flat, (0, rows * SMALL_LANES - flat.shape[0])).reshape(rows, SMALL_LANES)


def _unpack_small(packed, like):
    flat, out, off = packed.reshape(-1), {}, 0
    for k in SMALL:
        size = like[k].size
        out[k] = flat[off:off + size].reshape(like[k].shape)
        off += size
    return out


def _block_diag(pw):
    g = pw.shape[0]
    eye = jnp.eye(g, dtype=pw.dtype)
    return (eye[:, None, :, None] * pw[:, :, None, :]).reshape(g * HEAD, g * HEAD)


def _assemble_w_in(g_in, name):
    _, d, shard = g_in.shape
    tr = min(256, d)
    wide = MAIN_W + FC_PAD

    def body(g_ref, wm_ref, wf_ref, wmt_ref, wft_ref, row_s):
        row_s[:, MAIN_W:] = jnp.zeros((tr, FC_PAD), F32)
        for j in range(N_DEV):
            row_s[:, j * shard:(j + 1) * shard] = g_ref[j].astype(F32)
        wm_ref[...] = row_s[:, :MAIN_W].astype(BF16)
        wf_ref[...] = row_s[:, MAIN_W:].astype(BF16)
        for j in range(0, MAIN_W, 512):
            wmt_ref[j:j + 512, :] = row_s[:, j:j + 512].T.astype(BF16)
        wft_ref[...] = row_s[:, MAIN_W:].T.astype(BF16)

    return _pc(
        body, name, grid=(d // tr,),
        in_specs=[pl.BlockSpec((N_DEV, tr, shard), lambda i: (0, i, 0))],
        out_specs=[pl.BlockSpec((tr, MAIN_W), lambda i: (i, 0)), pl.BlockSpec((tr, FC_PAD), lambda i: (i, 0)),
                   pl.BlockSpec((MAIN_W, tr), lambda i: (0, i)), pl.BlockSpec((FC_PAD, tr), lambda i: (0, i))],
        out_shape=[_sds((d, MAIN_W), BF16), _sds((d, FC_PAD), BF16), _sds((MAIN_W, d), BF16), _sds((FC_PAD, d), BF16)],
        scratch_shapes=[pltpu.VMEM((tr, wide), F32)],
        compiler_params=_params(),
    )(g_in)


def _layer_weights(l, g_in, g_out):
    d = g_in.shape[1]
    w_main, w_fc, w_main_t, w_fc_t = _assemble_w_in(g_in, f"assemble_w_in_{l}")
    full_out = g_out.reshape(N_DEV * g_out.shape[1], d)
    return w_main, w_fc, w_main_t, w_fc_t, full_out, full_out.T


def _layer_fwd(l, x, lbs, weights, lw, nb, rider=None, target=None):
    n = x.shape[0]
    t = n // nb
    w_main, w_fc, _, _, w_out, _ = lw
    bias = jnp.pad(weights["fox_f_bias"][l:l + 1], ((0, 0), (0, FC_PAD - FOX_HEADS)))
    wbd = _block_diag(weights["pool_w"][l]).astype(BF16)
    proj, fc, ht = _in_proj_fwd(x, weights["pre_norm_g"][l:l + 1], w_main, w_fc, f"in_proj_fwd_{l}")
    c_col = _fox_decay_fwd(fc, bias, nb, f"fox_decay_fwd_{l}")
    c_row = c_col.reshape(nb, t, FC_PAD)[:, :, :FOX_HEADS].transpose(0, 2, 1)
    o_h, s0 = _hgrn_fwd(proj, lbs[l:l + 1], _block_ones(HGRN_W, BF16), nb, f"hgrn_fwd_{l}")
    o_b = _pool_fwd(proj, wbd, weights["pool_scale"][l:l + 1], nb, f"pool_fwd_{l}")
    (o_c, lse), rode = _fox_fwd(proj, c_col, c_row, nb, f"fox_fwd_{l}", rider)
    x_next, mixt, y = _merge_fwd(x, proj, o_h, o_b, o_c, weights["hgrn_norm_g"][l:l + 1], w_out,
                                 weights["post_norm_g"][l:l + 1], f"merge_fwd_{l}", target)
    return x_next, (x, proj, fc, ht, c_col, c_row, o_h, s0, o_c, lse, mixt, y, bias, wbd), rode


def _layer_bwd(l, dx, saved, lbs, weights, lw, nb, rider=None):
    _, proj, fc, ht, c_col, c_row, o_h, s0, o_c, lse, mixt, y, bias, wbd = saved
    n = proj.shape[0]
    t = n // nb
    w_out_t = lw[5]
    g = {}
    dy, dmix, dgp = _merge_bwd(dx, y, weights["post_norm_g"][l:l + 1], w_out_t, f"merge_bwd_{l}")
    g["post_norm_g"] = dgp[0]
    g["w_out"] = _w_out_grad(mixt, dy, f"w_out_grad_{l}")
    d_a, dgh, dlb = _hgrn_bwd(dmix, proj, o_h, s0, weights["hgrn_norm_g"][l:l + 1], lbs[l:l + 1],
                              _block_ones(HGRN_W, BF16), nb, f"hgrn_bwd_{l}")
    g["hgrn_norm_g"], g["lbs"] = dgh[0], dlb[0]
    d_b, dwbd, dps = _pool_bwd(dmix, proj, wbd, wbd.T, weights["pool_scale"][l:l + 1], nb, f"pool_bwd_{l}")
    g["pool_w"] = jnp.stack([dwbd[j * HEAD:(j + 1) * HEAD, j * HEAD:(j + 1) * HEAD] for j in range(len(POOL_WINDOWS))])
    g["pool_scale"] = dps[0]
    da, d_gc, delta = _fox_gate_bwd(dmix, proj, o_c, f"fox_gate_bwd_{l}")
    delta_row, lse_row = [a.reshape(nb, t, FC_PAD)[:, :, :FOX_HEADS].transpose(0, 2, 1) for a in (delta, lse)]
    (d_qc, d_kc, d_vc, dc_k, dc_q), rode = _fox_bwd(proj, da, c_col, c_row, lse_row, delta_row, nb, f"fox_bwd_{l}", rider)
    dc_q = jnp.pad(dc_q.transpose(0, 2, 1).reshape(n, FOX_HEADS), ((0, 0), (0, FC_PAD - FOX_HEADS)))
    d_fc, dbias = _fox_decay_bwd(dc_q, dc_k, fc, bias, nb, f"fox_decay_bwd_{l}")
    g["fox_f_bias"] = dbias[0, :FOX_HEADS]
    pieces = [(d_a, C_QA), (d_b, C_UB), (d_qc, C_QC), (d_kc, C_KC), (d_vc, C_VC), (d_gc, C_GC), (d_fc, None)]
    g["w_in"] = _w_in_grad(ht, pieces, f"w_in_grad_{l}")
    return g, pieces, rode


def _layer_bwd_input(l, dx, pieces, saved, weights, lw, rider=None):
    (dxi, dgpre), rode = _in_proj_bwd(pieces, lw[2], lw[3], saved[0], weights["pre_norm_g"][l:l + 1], dx,
                                      f"in_proj_bwd_{l}", rider)
    return dxi, dgpre[0], rode


def kernel(x, lower_bounds, pre_norm_g, w_in, hgrn_norm_g, fox_f_bias, pool_w, pool_scale, w_out, post_norm_g, loss_target, m_lower_bounds, m_pre_norm_g, m_w_in, m_hgrn_norm_g, m_fox_f_bias, m_pool_w, m_pool_scale, m_w_out, m_post_norm_g, v_lower_bounds, v_pre_norm_g, v_w_in, v_hgrn_norm_g, v_fox_f_bias, v_pool_w, v_pool_scale, v_w_out, v_post_norm_g):
    weights = dict(lower_bounds=lower_bounds, pre_norm_g=pre_norm_g, hgrn_norm_g=hgrn_norm_g, fox_f_bias=fox_f_bias,
                   pool_w=pool_w, pool_scale=pool_scale, post_norm_g=post_norm_g)
    mom_m = dict(lower_bounds=m_lower_bounds, pre_norm_g=m_pre_norm_g, hgrn_norm_g=m_hgrn_norm_g, fox_f_bias=m_fox_f_bias,
                 pool_w=m_pool_w, pool_scale=m_pool_scale, post_norm_g=m_post_norm_g)
    mom_v = dict(lower_bounds=v_lower_bounds, pre_norm_g=v_pre_norm_g, hgrn_norm_g=v_hgrn_norm_g, fox_f_bias=v_fox_f_bias,
                 pool_w=v_pool_w, pool_scale=v_pool_scale, post_norm_g=v_post_norm_g)
    depth = w_in.shape[0]
    nb, t, d = x.shape
    n = nb * t
    core = lax.axis_index("c").astype(jnp.int32).reshape(1, 1)
    shards = [(w_in[l].astype(BF16), w_out[l].astype(BF16)) for l in range(depth)]
    lbs = _lower_bound_table(lower_bounds, "lower_bound_table")

    lw = [_layer_weights(0, *_gather_weights(*shards[0]))]
    xl, saved = x.reshape(n, d), []
    for l in range(depth):
        last = l + 1 == depth
        rider = None if last else _gather_rider(shards[l + 1])
        xl, sv, rode = _layer_fwd(l, xl, lbs, weights, lw[l], nb, rider, loss_target.reshape(n, d) if last else None)
        saved.append(sv)
        if rode is not None:
            lw.append(_layer_weights(l + 1, *rode))
    dx, sq = xl
    loss_here = 0.5 * jnp.sum(sq) / d

    grads, recv, pending = [None] * depth, [None] * depth, None
    for l in reversed(range(depth)):
        g, pieces, rode = _layer_bwd(l, dx, saved[l], lbs, weights, lw[l], nb, pending)
        if rode is not None:
            recv[l + 1] = rode
        blocks = (g["w_in"], g["w_out"])
        if l > 0:
            pending = _direct_exchange_rider(blocks)
            dx, g["pre_norm_g"], _ = _layer_bwd_input(l, dx, pieces, saved[l], weights, lw[l])
        else:
            other = _run_rider(_swap_rider(blocks), "grad_swap")
            summed = [_pair_add(hv, ot, core, f"grad_pair_add_{i}") for i, (hv, ot) in enumerate(zip(blocks, other))]
            dx, g["pre_norm_g"], recv[l] = _layer_bwd_input(l, dx, pieces, saved[l], weights, lw[l],
                                                            _chip_exchange_rider(summed))
        grads[l] = g
    small = {k: jnp.stack([grads[l][k] for l in range(depth)]) for k in SMALL if k != "lower_bounds"}
    small["lower_bounds"] = _lower_bound_bwd(lower_bounds, jnp.stack([grads[l]["lbs"] for l in range(depth)]),
                                             "lower_bound_bwd")
    (r_small,) = _run_rider(_gather_rider([_pack_small(small, loss_here)]), "small_grads_gather")

    res_in, _ = _sum_adamw([recv[l][0] for l in range(depth)], w_in, m_w_in, v_w_in, "adamw_w_in")
    res_out, _ = _sum_adamw([recv[l][1] for l in range(depth)], w_out, m_w_out, v_w_out, "adamw_w_out")
    res_small, _ = _sum_adamw([r_small], _pack_small(weights)[None], _pack_small(mom_m)[None], _pack_small(mom_v)[None],
                              "adamw_small")
    loss = res_small[0][0].reshape(-1)[_small_size(weights)]

    names = ("lower_bounds", "pre_norm_g", "w_in", "hgrn_norm_g", "fox_f_bias", "pool_w", "pool_scale", "w_out", "post_norm_g")
    outs = [loss, dx.reshape(nb, t, d)]
    for i in range(4):
        full = dict(_unpack_small(res_small[i][0], weights), w_in=res_in[i], w_out=res_out[i])
        outs += [full[k] for k in names]
    return tuple(outs)
```

```python
import functools

import jax
import jax.numpy as jnp
from jax import lax
from jax.experimental import pallas as pl
from jax.experimental.pallas import tpu as pltpu

F32, BF16 = jnp.float32, jnp.bfloat16
HI = lax.Precision.HIGHEST
MESH = pl.DeviceIdType.MESH
AXES = ("x", "y", "c")
N_DEV = 8

NORM_EPS = 1e-6
MASK_VALUE = -1e30
TINY = 1e-30
CHUNK = 64
SUB = 16
HGRN_W, POOL_W, FOX_W = 256, 256, 512
HEAD = 64
FOX_HEADS = 8
POOL_WINDOWS = (2, 4, 8, 16)
POOL_HALO = 16
MAIN_W = 3584
FC_PAD = 128
C_QA, C_FA, C_IA, C_GA, C_UB, C_GB, C_QC, C_KC, C_VC, C_GC = 0, 256, 512, 768, 1024, 1280, 1536, 2048, 2560, 3072
FOX_SCALE = HEAD ** -0.5

ADAM_LR, ADAM_B1, ADAM_B2, ADAM_EPS, ADAM_WD, ADAM_STEP = 0.001, 0.9, 0.999, 1e-08, 0.01, 10

VMEM_LIMIT = 56 * 1024 * 1024


def _pc(fn, name, **kw):
    return pl.pallas_call(fn, name=name, **kw)


def _params(**kw):
    return pltpu.CompilerParams(vmem_limit_bytes=VMEM_LIMIT, **kw)


class _Rider:
    def __init__(self, inputs, out_shapes, n_sems, n_local, plan):
        self.inputs, self.out_shapes, self.n_sems, self.n_local, self.plan = list(inputs), list(out_shapes), n_sems, n_local, plan

    def start(self, ins, outs, *sems):
        sends, _, locs = self.plan(ins, outs, *sems)
        for cp in locs + sends:
            cp.start()

    def wait(self, ins, outs, *sems):
        sends, recvs, locs = self.plan(ins, outs, *sems)
        for cp in recvs:
            cp.wait_recv()
        for cp in sends:
            cp.wait_send()
        for cp in locs:
            cp.wait()

    def sem_shapes(self):
        return [pltpu.SemaphoreType.DMA((self.n_sems,)), pltpu.SemaphoreType.DMA((self.n_sems,)),
                pltpu.SemaphoreType.DMA((self.n_local,))]


def _call(body, name, args, rider=None, *, grid, in_specs, out_specs, out_shape, scratch_shapes=(), **kw):
    if rider is None:
        res = _pc(body, name, grid=grid, in_specs=in_specs, out_specs=out_specs, out_shape=out_shape,
                  scratch_shapes=list(scratch_shapes), **kw)(*args)
        return res, None
    n_in, n_out, n_scr = len(in_specs), len(out_specs), len(scratch_shapes)
    n_rin, n_rout = len(rider.inputs), len(rider.out_shapes)

    def ridden(*refs):
        ins, refs = refs[:n_in], refs[n_in:]
        rins, refs = refs[:n_rin], refs[n_rin:]
        outs, refs = refs[:n_out], refs[n_out:]
        routs, refs = refs[:n_rout], refs[n_rout:]
        scr, sems = refs[:n_scr], refs[n_scr:]
        first = functools.reduce(jnp.logical_and, [pl.program_id(a) == 0 for a in range(len(grid))])
        last = functools.reduce(jnp.logical_and, [pl.program_id(a) == g - 1 for a, g in enumerate(grid)])

        @pl.when(first)
        def _():
            rider.start(rins, routs, *sems)

        body(*ins, *outs, *scr)

        @pl.when(last)
        def _():
            rider.wait(rins, routs, *sems)

    any_spec = pl.BlockSpec(memory_space=pl.ANY)
    res = _pc(ridden, name, grid=grid, in_specs=list(in_specs) + [any_spec] * n_rin,
              out_specs=list(out_specs) + [any_spec] * n_rout, out_shape=list(out_shape) + rider.out_shapes,
              scratch_shapes=list(scratch_shapes) + rider.sem_shapes(), **kw)(*args, *rider.inputs)
    return res[:n_out], res[n_out:]


def _run_rider(rider, name):
    n_rin = len(rider.inputs)

    def body(*refs):
        ins, outs, sems = refs[:n_rin], refs[n_rin:n_rin + len(rider.out_shapes)], refs[n_rin + len(rider.out_shapes):]
        rider.start(ins, outs, *sems)
        rider.wait(ins, outs, *sems)

    any_spec = pl.BlockSpec(memory_space=pl.ANY)
    return _pc(body, name, in_specs=[any_spec] * n_rin, out_specs=[any_spec] * len(rider.out_shapes),
               out_shape=rider.out_shapes, scratch_shapes=rider.sem_shapes())(*rider.inputs)


def _dot(a, b):
    return jnp.dot(a, b, preferred_element_type=F32)


def _dot_nt(a, b):
    return lax.dot_general(a, b, (((1,), (1,)), ((), ())), preferred_element_type=F32)


def _dot_tn(a, b):
    return lax.dot_general(a, b, (((0,), (0,)), ((), ())), preferred_element_type=F32)


def _dot_hi(a, b):
    return jnp.dot(a, b, precision=HI, preferred_element_type=F32)


def _split2(x):
    hi = x.astype(BF16)
    return hi, (x - hi.astype(F32)).astype(BF16)


def _sel_dot(sel, x):
    hi, lo = _split2(x)
    sb = sel.astype(BF16)
    return _dot(sb, hi) + _dot(sb, lo)


def _dot_sel(x, sel):
    hi, lo = _split2(x)
    sb = sel.astype(BF16)
    return _dot(hi, sb) + _dot(lo, sb)


def _sigmoid(x):
    return 1.0 / (1.0 + jnp.exp(-x))


def _block_ones(n, dtype):
    r = lax.broadcasted_iota(jnp.int32, (n, n), 0) // HEAD
    c = lax.broadcasted_iota(jnp.int32, (n, n), 1) // HEAD
    return (r == c).astype(dtype)


def _sds(shape, dtype):
    return jax.ShapeDtypeStruct(shape, dtype)


def _in_proj_fwd(x, g_pre, w_main, w_fc, name):
    n, d = x.shape
    tm = min(512, n)

    def body(x_ref, g_ref, w_ref, wf_ref, proj_ref, fc_ref, ht_ref):
        xv = x_ref[...]
        r = lax.rsqrt(jnp.mean(xv * xv, axis=-1, keepdims=True) + NORM_EPS)
        hf = xv * r * g_ref[...]
        hb = hf.astype(BF16)
        ht_ref[...] = hf.T.astype(BF16)
        for j in range(0, MAIN_W, 512):
            proj_ref[:, j:j + 512] = _dot(hb, w_ref[:, j:j + 512])
        fc_ref[...] = _dot(hb, wf_ref[...])

    return _pc(
        body, name, grid=(n // tm,),
        in_specs=[pl.BlockSpec((tm, d), lambda i: (i, 0)), pl.BlockSpec((1, d), lambda i: (0, 0)),
                  pl.BlockSpec((d, MAIN_W), lambda i: (0, 0)), pl.BlockSpec((d, FC_PAD), lambda i: (0, 0))],
        out_specs=[pl.BlockSpec((tm, MAIN_W), lambda i: (i, 0)), pl.BlockSpec((tm, FC_PAD), lambda i: (i, 0)),
                   pl.BlockSpec((d, tm), lambda i: (0, i))],
        out_shape=[_sds((n, MAIN_W), F32), _sds((n, FC_PAD), F32), _sds((d, n), BF16)],
        compiler_params=_params(),
    )(x, g_pre, w_main, w_fc)


def _fox_decay_fwd(fc, bias, nb, name):
    n = fc.shape[0]
    t = n // nb
    tt = min(256, t)
    nt = t // tt

    def body(fc_ref, b_ref, c_ref, carry):
        i = pl.program_id(1)

        @pl.when(i == 0)
        def _():
            carry[...] = jnp.zeros_like(carry)

        xv = fc_ref[...] + b_ref[...]
        lf = jnp.minimum(xv, 0.0) - jnp.log(1.0 + jnp.exp(-jnp.abs(xv)))
        r = lax.broadcasted_iota(jnp.int32, (tt, tt), 0)
        cc = lax.broadcasted_iota(jnp.int32, (tt, tt), 1)
        cs = _dot_hi((r >= cc).astype(F32), lf) + carry[...]
        c_ref[...] = cs
        carry[...] = cs[tt - 1:tt, :]

    return _pc(
        body, name, grid=(nb, nt),
        in_specs=[pl.BlockSpec((tt, FC_PAD), lambda b, i: (b * nt + i, 0)), pl.BlockSpec((1, FC_PAD), lambda b, i: (0, 0))],
        out_specs=pl.BlockSpec((tt, FC_PAD), lambda b, i: (b * nt + i, 0)),
        out_shape=_sds((n, FC_PAD), F32),
        scratch_shapes=[pltpu.VMEM((1, FC_PAD), F32)],
        compiler_params=_params(),
    )(fc, bias)


def _hgrn_gates(q, z, lb):
    sig = _sigmoid(z)
    sn = _sigmoid(-z)
    f = lb + (1.0 - lb) * sig
    g = jnp.log(jnp.maximum(f, TINY))
    k = (1.0 - lb) * sn
    sq = _sigmoid(q)
    return sig, sn, f, g, k, sq


def _sub_tri(n, lower):
    r = lax.broadcasted_iota(jnp.int32, (n, n), 0)
    c = lax.broadcasted_iota(jnp.int32, (n, n), 1)
    tri = (r >= c) if lower else (r <= c)
    return jnp.logical_and(r // SUB == c // SUB, tri).astype(F32)


def _live_rows(t):
    return 8 * (t // 8 + 1)


def _pad_rows(x):
    return x if x.shape[0] == SUB else jnp.concatenate([x, jnp.zeros((SUB - x.shape[0], x.shape[1]), x.dtype)], axis=0)


def _hgrn_decays(qs, k, b):
    srow = lax.broadcasted_iota(jnp.int32, (SUB, HGRN_W), 0)
    es, ws = [], []
    for t in range(SUB):
        r = _live_rows(t)
        e = jnp.where(srow[:r] <= t, jnp.exp(b[t:t + 1, :] - b[:r]), 0.0)
        es.append(e)
        ws.append(_pad_rows(e * (qs[t:t + 1, :] * k[:r])))
    return srow, es, ws


def _hgrn_state_step(st, k, v, b, bmask):
    bl = b[SUB - 1:SUB, :]
    ktil = k * jnp.exp(bl - b)
    return st * jnp.exp(bl) + _dot_tn(v.astype(BF16), ktil.astype(BF16)) * bmask


def _hgrn_tile(t):
    return min(256, t)


def _hgrn_fwd(proj, lb, ones_b, nb, name):
    n = proj.shape[0]
    t = n // nb
    tt = _hgrn_tile(t)
    nt = t // tt
    ncs = tt // CHUNK
    w = HGRN_W

    def body(q_ref, z_ref, v_ref, lb_ref, ones_ref, o_ref, s0_ref, st_s, b_s, qs_s, k_s):
        @pl.when(pl.program_id(1) == 0)
        def _():
            st_s[...] = jnp.zeros_like(st_s)

        q = q_ref[...]
        _, _, _, g, k, sq = _hgrn_gates(q, z_ref[...], lb_ref[...])
        b_s[...] = _sel_dot(_sub_tri(tt, True), g)
        qs_s[...] = q * sq
        k_s[...] = k
        bmask = _block_ones(w, F32)
        ones_b = ones_ref[...]

        def chunk(c, carry):
            st = st_s[...]
            s0_ref[c] = st
            base = pl.multiple_of(c * CHUNK, CHUNK)
            tiles = []
            for u in range(CHUNK // SUB):
                rows = pl.ds(base + u * SUB, SUB)
                tiles.append((qs_s[rows, :], k_s[rows, :], v_ref[rows, :], b_s[rows, :]))
            aexps = []
            for qs, k, v, b in tiles:
                _, _, ws = _hgrn_decays(qs, k, b)
                aexps.append(_dot(jnp.concatenate(ws, axis=0).astype(BF16), ones_b))
            inters = []
            for qs, k, v, b in tiles:
                inters.append(_dot_nt((qs * jnp.exp(b)).astype(BF16), st.astype(BF16)))
                st = _hgrn_state_step(st, k, v, b, bmask)
            st_s[...] = st
            for u, ((qs, k, v, b), aexp, o) in enumerate(zip(tiles, aexps, inters)):
                for t in range(SUB):
                    r = _live_rows(t)
                    row = o[t:t + 1, :] + jnp.sum(aexp[t * SUB:t * SUB + r, :] * v[:r], axis=0, keepdims=True)
                    o_ref[pl.ds(base + u * SUB + t, 1), :] = row
            return carry

        lax.fori_loop(0, ncs, chunk, 0)

    def col(j):
        return pl.BlockSpec((tt, w), lambda b, i: (b * nt + i, j))

    return _pc(
        body, name, grid=(nb, nt),
        in_specs=[col(C_QA // w), col(C_FA // w), col(C_IA // w), pl.BlockSpec((1, w), lambda b, i: (0, 0)),
                  pl.BlockSpec((w, w), lambda b, i: (0, 0))],
        out_specs=[pl.BlockSpec((tt, w), lambda b, i: (b * nt + i, 0)),
                   pl.BlockSpec((ncs, w, w), lambda b, i: (b * nt + i, 0, 0))],
        out_shape=[_sds((n, w), F32), _sds((n // CHUNK, w, w), F32)],
        scratch_shapes=[pltpu.VMEM((w, w), F32)] + [pltpu.VMEM((tt, w), F32)] * 3,
        compiler_params=_params(),
    )(proj, proj, proj, lb, ones_b)


def _pool_lane_windows():
    lane = lax.broadcasted_iota(jnp.int32, (1, POOL_W), 1) // HEAD
    wl = jnp.zeros((1, POOL_W), F32)
    for gi, win in enumerate(POOL_WINDOWS):
        wl = jnp.where(lane == gi, float(win), wl)
    return lane, wl


def _pool_select(lane, parts):
    out = parts[-1]
    for gi in range(len(parts) - 2, -1, -1):
        out = jnp.where(lane == gi, parts[gi], out)
    return out


def _pool_mix(u, halo, t0, tt):
    lane, wl = _pool_lane_windows()
    ext = jnp.concatenate([halo, u], axis=0)
    sums, cur, shift = [], ext, 1
    for _ in POOL_WINDOWS:
        cur = cur + pltpu.roll(cur, shift, axis=0)
        sums.append(cur[POOL_HALO:, :])
        shift *= 2
    tpos = (t0 + lax.broadcasted_iota(jnp.int32, (tt, POOL_W), 0)).astype(F32)
    cnt = jnp.minimum(tpos + 1.0, wl)
    return _pool_select(lane, sums) / cnt - u, cnt


def _pool_specs(tt, nt, nhb):
    cu, cg = C_UB // POOL_W, C_GB // POOL_W
    return [pl.BlockSpec((tt, POOL_W), lambda b, i: (b * nt + i, cu)),
            pl.BlockSpec((tt, POOL_W), lambda b, i: (b * nt + i, cg)),
            pl.BlockSpec((POOL_HALO, POOL_W), lambda b, i: (jnp.maximum((b * nt + i) * nhb - 1, 0), cu))]


def _pool_fwd(proj, wbd, scale, nb, name):
    n = proj.shape[0]
    t = n // nb
    tt = min(256, t)
    nt = t // tt
    nhb = tt // POOL_HALO

    def body(u_ref, g_ref, h_ref, w_ref, s_ref, o_ref):
        i = pl.program_id(1)
        halo = jnp.where(i == 0, 0.0, h_ref[...])
        pooled, _ = _pool_mix(u_ref[...], halo, i * tt, tt)
        gv = g_ref[...]
        o_ref[...] = _dot(pooled.astype(BF16), w_ref[...]) * s_ref[...] * (gv * _sigmoid(gv))

    return _pc(
        body, name, grid=(nb, nt),
        in_specs=_pool_specs(tt, nt, nhb) + [pl.BlockSpec((POOL_W, POOL_W), lambda b, i: (0, 0)),
                                             pl.BlockSpec((1, POOL_W), lambda b, i: (0, 0))],
        out_specs=pl.BlockSpec((tt, POOL_W), lambda b, i: (b * nt + i, 0)),
        out_shape=_sds((n, POOL_W), F32),
        compiler_params=_params(),
    )(proj, proj, proj, wbd, scale)


def _rows_reduce(x, op, final):
    while x.shape[0] > 8 and x.shape[0] % 16 == 0:
        half = x.shape[0] // 2
        x = op(x[:half], x[half:])
    return final(x, axis=0, keepdims=True)


def _tri_pair(step, n):
    a = sum([(step >= r * (r + 1) // 2).astype(jnp.int32) for r in range(1, n)], jnp.int32(0))
    return a, step - a * (a + 1) // 2


def _lane_lo():
    return lax.broadcasted_iota(jnp.int32, (1, 2 * HEAD), 1) < HEAD


def _put_col(tile, hh, colv):
    lane = lax.broadcasted_iota(jnp.int32, tile.shape, 1)
    return jnp.where(lane == hh, colv, tile)


def _fox_fwd(proj, c_col, c_row, nb, name, rider=None):
    n = proj.shape[0]
    t = n // nb
    tb = min(256, t)
    nq = t // tb
    pw = 2 * HEAD

    def body(q_ref, k_ref, v_ref, cc_ref, cr_ref, o_ref, lse_ref, m_s, acc_s, cq_s):
        qi, kj = _tri_pair(pl.program_id(1), nq)

        @pl.when(kj == 0)
        def _():
            m_s[...] = jnp.full_like(m_s, -jnp.inf)
            acc_s[...] = jnp.zeros_like(acc_s)
            for hh in range(FOX_HEADS):
                cq_s[hh] = jnp.broadcast_to(cc_ref[:, hh:hh + 1], (tb, pw))

        def block(masked):
            lo = _lane_lo()
            if masked:
                causal = lax.broadcasted_iota(jnp.int32, (tb, tb), 0) >= lax.broadcasted_iota(jnp.int32, (tb, tb), 1)
            for p in range(FOX_HEADS // 2):
                sl = slice(p * pw, (p + 1) * pw)
                qp = q_ref[:, sl] * FOX_SCALE
                kp = k_ref[:, sl].astype(BF16)
                vf = v_ref[:, sl]
                for h in range(2):
                    hh = 2 * p + h
                    lm = lo if h == 0 else jnp.logical_not(lo)
                    s = _dot_nt(jnp.where(lm, qp, 0.0).astype(BF16), kp)
                    s = s + (jnp.tile(cq_s[hh], (1, tb // pw)) - cr_ref[hh:hh + 1, :])
                    if masked:
                        s = jnp.where(causal, s, MASK_VALUE)
                    m_prev = m_s[hh]
                    m_new = jnp.maximum(m_prev, jnp.max(s, axis=1, keepdims=True))
                    alpha = jnp.exp(m_prev - m_new)
                    pe = jnp.exp(s - jnp.tile(m_new, (1, tb // pw)))
                    m_s[hh] = m_new
                    acc_s[hh] = alpha * acc_s[hh] + _dot(pe.astype(BF16), jnp.where(lm, vf, 1.0).astype(BF16))

        @pl.when(kj < qi)
        def _():
            block(False)

        @pl.when(kj == qi)
        def _():
            block(True)
            lo = _lane_lo()
            lse = jnp.zeros((tb, FC_PAD), F32)
            for p in range(FOX_HEADS // 2):
                halves = []
                for h in range(2):
                    hh = 2 * p + h
                    lm = lo if h == 0 else jnp.logical_not(lo)
                    acc = acc_s[hh]
                    swapped = pltpu.roll(acc, HEAD, axis=1)
                    halves.append(acc / swapped)
                    lse = _put_col(lse, hh, m_s[hh] + jnp.log(jnp.where(lm, swapped, acc)))
                o_ref[:, p * pw:(p + 1) * pw] = jnp.where(lo, halves[0], halves[1])
            lse_ref[...] = lse

    def qspec(wd, j):
        return pl.BlockSpec((tb, wd), lambda b, st: (b * nq + _tri_pair(st, nq)[0], j))

    def kspec(j):
        return pl.BlockSpec((tb, FOX_W), lambda b, st: (b * nq + _tri_pair(st, nq)[1], j))

    return _call(
        body, name, (proj, proj, proj, c_col, c_row), rider, grid=(nb, nq * (nq + 1) // 2),
        in_specs=[qspec(FOX_W, C_QC // FOX_W), kspec(C_KC // FOX_W), kspec(C_VC // FOX_W), qspec(FC_PAD, 0),
                  pl.BlockSpec((None, FOX_HEADS, tb), lambda b, st: (b, 0, _tri_pair(st, nq)[1]))],
        out_specs=[qspec(FOX_W, 0), qspec(FC_PAD, 0)],
        out_shape=[_sds((n, FOX_W), F32), _sds((n, FC_PAD), F32)],
        scratch_shapes=[pltpu.VMEM((FOX_HEADS, tb, pw), F32), pltpu.VMEM((FOX_HEADS, tb, pw), F32),
                        pltpu.VMEM((FOX_HEADS, tb, pw), F32)],
        compiler_params=_params(),
    )


def _head_mean(x, ones_f):
    return _dot_sel(x, ones_f) * (1.0 / HEAD)


def _merge_fwd(x, proj, o_h, o_b, o_c, gh, w_out, g_post, name, target=None):
    n, d = x.shape
    tm = min(512, n)

    def body(*refs):
        x_ref, ga_ref, gc_ref, oh_ref, ob_ref, oc_ref, gh_ref, w_ref, gp_ref = refs[:9]
        if target is None:
            xo_ref, mixt_ref, y_ref = refs[9:]
        else:
            t_ref, dx_ref, sq_ref, mixt_ref, y_ref = refs[9:]
        oh = oh_ref[...]
        ones_f = _block_ones(HGRN_W, F32)
        na = oh * lax.rsqrt(_head_mean(oh * oh, ones_f) + NORM_EPS) * gh_ref[...]
        ga, gc = ga_ref[...], gc_ref[...]
        mixed = jnp.concatenate([na * (ga * _sigmoid(ga)), ob_ref[...], oc_ref[...] * (gc * _sigmoid(gc))], axis=1)
        mixt_ref[...] = mixed.T.astype(BF16)
        y = _dot(mixed.astype(BF16), w_ref[...])
        y_ref[...] = y
        xn = x_ref[...] + y * lax.rsqrt(jnp.mean(y * y, axis=-1, keepdims=True) + NORM_EPS) * gp_ref[...]
        if target is None:
            xo_ref[...] = xn
        else:
            @pl.when(pl.program_id(0) == 0)
            def _():
                sq_ref[...] = jnp.zeros_like(sq_ref)

            e = xn - t_ref[...]
            dx_ref[...] = e * (1.0 / d)
            sq_ref[...] += jnp.sum(e * e, axis=0, keepdims=True)

    def row(wd, j=0):
        return pl.BlockSpec((tm, wd), lambda i: (i, j))

    def full(a, b):
        return pl.BlockSpec((a, b), lambda i: (0, 0))

    head = [] if target is None else [target]
    res = _pc(
        body, name, grid=(n // tm,),
        in_specs=[row(d), row(HGRN_W, C_GA // HGRN_W), row(FOX_W, C_GC // FOX_W), row(HGRN_W), row(POOL_W), row(FOX_W),
                  full(1, HGRN_W), full(d, d), full(1, d)] + [row(d)] * len(head),
        out_specs=[row(d)] + [full(1, d)] * len(head) + [pl.BlockSpec((d, tm), lambda i: (0, i)), row(d)],
        out_shape=[_sds((n, d), F32)] + [_sds((1, d), F32)] * len(head) + [_sds((d, n), BF16), _sds((n, d), F32)],
        compiler_params=_params(),
    )(x, proj, proj, o_h, o_b, o_c, gh, w_out, g_post, *head)
    return (res[0], res[1], res[2]) if target is None else ((res[0], res[1]), res[2], res[3])


def _rms_bwd(dy_scaled, xhat, r):
    return r * (dy_scaled - xhat * jnp.mean(dy_scaled * xhat, axis=-1, keepdims=True))


def _merge_bwd(dxo, y, g_post, w_out_t, name):
    n, d = y.shape
    tm = min(512, n)

    def body(dx_ref, y_ref, gp_ref, wt_ref, dy_ref, dm_ref, dgp_ref):
        @pl.when(pl.program_id(0) == 0)
        def _():
            dgp_ref[...] = jnp.zeros_like(dgp_ref)

        yv, dxv = y_ref[...], dx_ref[...]
        r = lax.rsqrt(jnp.mean(yv * yv, axis=-1, keepdims=True) + NORM_EPS)
        yh = yv * r
        dgp_ref[...] += jnp.sum(dxv * yh, axis=0, keepdims=True)
        dyb = _rms_bwd(dxv * gp_ref[...], yh, r).astype(BF16)
        dy_ref[...] = dyb
        dm_ref[...] = _dot(dyb, wt_ref[...])

    row = pl.BlockSpec((tm, d), lambda i: (i, 0))
    return _pc(
        body, name, grid=(n // tm,),
        in_specs=[row, row, pl.BlockSpec((1, d), lambda i: (0, 0)), pl.BlockSpec((d, d), lambda i: (0, 0))],
        out_specs=[row, row, pl.BlockSpec((1, d), lambda i: (0, 0))],
        out_shape=[_sds((n, d), BF16), _sds((n, d), F32), _sds((1, d), F32)],
        compiler_params=_params(),
    )(dxo, y, g_post, w_out_t)


def _w_out_grad(mixt, dy, name):
    d, n = mixt.shape
    rows = d // N_DEV

    def body(a_ref, b_ref, o_ref):
        o_ref[...] = _dot(a_ref[...], b_ref[...]).astype(BF16)

    return _pc(
        body, name, grid=(N_DEV,),
        in_specs=[pl.BlockSpec((rows, n), lambda j: (j, 0)), pl.BlockSpec((n, d), lambda j: (0, 0))],
        out_specs=pl.BlockSpec((None, None, rows, d), lambda j: (j % 2, j // 2, 0, 0)),
        out_shape=_sds((2, N_DEV // 2, rows, d), BF16),
        compiler_params=_params(),
    )(mixt, dy)


def _w_in_grad(ht, pieces, name):
    d, n = ht.shape
    ta, tk = min(512, d), min(512, n)
    nk = n // tk
    arrays = [p for p, _ in pieces]
    widths = [p.shape[1] for p in arrays]
    offs = [sum(widths[:i]) for i in range(len(widths))]
    in_w = MAIN_W + FOX_HEADS
    shard = in_w // N_DEV

    def body(*refs):
        a_ref, p_refs = refs[0], refs[1:1 + len(arrays)]
        o_ref, acc = refs[1 + len(arrays):]
        k = pl.program_id(1)

        @pl.when(k == 0)
        def _():
            acc[...] = jnp.zeros_like(acc)

        a = a_ref[...]
        for pr, off, wd in zip(p_refs, offs, widths):
            for j in range(0, wd, 512):
                jw = min(512, wd - j)
                acc[:, off + j:off + j + jw] += _dot(a, pr[:, j:j + jw])

        @pl.when(k == nk - 1)
        def _():
            for j in range(N_DEV):
                o_ref[j % 2, j // 2] = acc[:, j * shard:(j + 1) * shard].astype(BF16)

    return _pc(
        body, name, grid=(d // ta, nk),
        in_specs=[pl.BlockSpec((ta, tk), lambda i, k: (i, k))] + [pl.BlockSpec((tk, wd), lambda i, k: (k, 0)) for wd in widths],
        out_specs=pl.BlockSpec((2, N_DEV // 2, ta, shard), lambda i, k: (0, 0, i, 0)),
        out_shape=_sds((2, N_DEV // 2, d, shard), BF16),
        scratch_shapes=[pltpu.VMEM((ta, sum(widths)), F32)],
        compiler_params=_params(),
    )(ht, *arrays)


def _hgrn_state_bwd(qs, k, v, b, do, s0, ds1, bmask):
    bl = b[SUB - 1:SUB, :]
    eb, ebl, ekt = jnp.exp(b), jnp.exp(bl), jnp.exp(bl - b)
    qe, ktil = qs * eb, k * ekt
    ds1b, dob = ds1.astype(BF16), do.astype(BF16)
    dv = _dot_nt(ktil.astype(BF16), ds1b)
    dqe = _dot(dob, s0.astype(BF16))
    dktil = _dot(v.astype(BF16), ds1b)
    dbl = jnp.sum(dktil * ktil, axis=0, keepdims=True) + ebl * jnp.sum(s0 * ds1, axis=0, keepdims=True)
    ds0 = ds1 * ebl + _dot_tn(dob, qe.astype(BF16)) * bmask
    return dqe * eb, dktil * ekt, dv, dbl, ds0


def _hgrn_intra_bwd(qs, k, v, do, es, aexp, gexp, dq, dk, dv, put_dq_row):
    dks = [dk[j:j + 8] for j in range(0, SUB, 8)]
    dvs = [dv[j:j + 8] for j in range(0, SUB, 8)]
    for t in range(SUB):
        r = _live_rows(t)
        ge = gexp[t * SUB:t * SUB + r, :] * es[t]
        put_dq_row(t, dq[t:t + 1, :] + jnp.sum(ge * k[:r], axis=0, keepdims=True))
        for j in range(r // 8):
            dks[j] = dks[j] + ge[8 * j:8 * j + 8] * qs[t:t + 1, :]
            dvs[j] = dvs[j] + aexp[t * SUB + 8 * j:t * SUB + 8 * j + 8, :] * do[t:t + 1, :]
    return jnp.concatenate(dks, axis=0), jnp.concatenate(dvs, axis=0)


def _hgrn_bwd(dmix, proj, o_h, s0, gh, lb, ones_b, nb, name):
    n = proj.shape[0]
    t = n // nb
    tt = _hgrn_tile(t)
    nt = t // tt
    ncs = tt // CHUNK
    nsub = CHUNK // SUB
    w = HGRN_W

    def body(dm_ref, q_ref, z_ref, v_ref, ga_ref, oh_ref, s0_ref, gh_ref, lb_ref, ones_ref,
             dp_ref, dgh_ref, dlb_ref, ds_s, ss_s, b_s, qs_s, k_s, do_s, dq_s, dk_s, dv_s, dbl_s):
        first = jnp.logical_and(pl.program_id(0) == 0, pl.program_id(1) == 0)

        @pl.when(first)
        def _():
            dgh_ref[...] = jnp.zeros_like(dgh_ref)
            dlb_ref[...] = jnp.zeros_like(dlb_ref)

        @pl.when(pl.program_id(1) == 0)
        def _():
            ds_s[...] = jnp.zeros_like(ds_s)

        ones_b = ones_ref[...]
        ones_f = ones_b.astype(F32)
        bmask = _block_ones(w, F32)
        lbv, ghv = lb_ref[...], gh_ref[...]
        oh, ga, dm = oh_ref[...], ga_ref[...], dm_ref[...]
        rn = lax.rsqrt(_head_mean(oh * oh, ones_f) + NORM_EPS)
        nh = oh * rn
        sga = _sigmoid(ga)
        dp_ref[:, 3 * w:4 * w] = (dm * nh * ghv * (sga * (1.0 + ga * (1.0 - sga)))).astype(BF16)
        dn = dm * (ga * sga)
        dgh_ref[...] += jnp.sum(dn * nh, axis=0, keepdims=True)
        dn = dn * ghv
        do_s[...] = rn * (dn - nh * _head_mean(dn * nh, ones_f))
        q = q_ref[...]
        sig, sn, f, g, k, sq = _hgrn_gates(q, z_ref[...], lbv)
        qs = q * sq
        b_s[...] = _sel_dot(_sub_tri(tt, True), g)
        qs_s[...] = qs
        k_s[...] = k

        def chunk(cc, carry):
            c = ncs - 1 - cc
            base = pl.multiple_of(c * CHUNK, CHUNK)
            tiles = []
            for u in range(nsub):
                rows = pl.ds(base + u * SUB, SUB)
                tiles.append((qs_s[rows, :], k_s[rows, :], v_ref[rows, :], b_s[rows, :], do_s[rows, :]))
            st = s0_ref[c]
            for u, (qs, k, v, b, do) in enumerate(tiles):
                ss_s[u] = st
                if u < nsub - 1:
                    st = _hgrn_state_step(st, k, v, b, bmask)
            ds = ds_s[...]
            for u in reversed(range(nsub)):
                qs, k, v, b, do = tiles[u]
                _, es, ws = _hgrn_decays(qs, k, b)
                gs = [_pad_rows(do[t:t + 1, :] * v[:_live_rows(t)]) for t in range(SUB)]
                aexp = _dot(jnp.concatenate(ws, axis=0).astype(BF16), ones_b)
                gexp = _dot(jnp.concatenate(gs, axis=0).astype(BF16), ones_b)
                dq, dk, dv, dbl, ds = _hgrn_state_bwd(qs, k, v, b, do, ss_s[u], ds, bmask)

                def put_dq_row(i, row, r0=base + u * SUB):
                    dq_s[pl.ds(r0 + i, 1), :] = row

                dk, dv = _hgrn_intra_bwd(qs, k, v, do, es, aexp, gexp, dq, dk, dv, put_dq_row)
                dk_s[pl.ds(base + u * SUB, SUB), :] = dk
                dv_s[pl.ds(base + u * SUB, SUB), :] = dv
                dbl_s[pl.ds(base + u * SUB, SUB), :] = jnp.broadcast_to(dbl, (SUB, w))
            ds_s[...] = ds
            return carry

        lax.fori_loop(0, ncs, chunk, 0)
        dqs, dk = dq_s[...], dk_s[...]
        dg = _sel_dot(_sub_tri(tt, False), qs * dqs - k * dk) + dbl_s[...]
        dfz = jnp.where(f > TINY, dg / jnp.maximum(f, TINY), 0.0)
        dlb_ref[...] += jnp.sum(dfz * (1.0 - sig) - dk * sn, axis=0, keepdims=True)
        dp_ref[:, 0:w] = (dqs * (sq * (1.0 + q * (1.0 - sq)))).astype(BF16)
        dp_ref[:, w:2 * w] = ((dfz - dk) * (1.0 - lbv) * sig * sn).astype(BF16)
        dp_ref[:, 2 * w:3 * w] = dv_s[...].astype(BF16)

    def rv(b, i):
        return b * nt + (nt - 1 - i)

    def col(j):
        return pl.BlockSpec((tt, w), lambda b, i: (rv(b, i), j))

    def full(a, bb):
        return pl.BlockSpec((a, bb), lambda b, i: (0, 0))

    return _pc(
        body, name, grid=(nb, nt),
        in_specs=[col(0), col(C_QA // w), col(C_FA // w), col(C_IA // w), col(C_GA // w), col(0),
                  pl.BlockSpec((ncs, w, w), lambda b, i: (rv(b, i), 0, 0)), full(1, w), full(1, w), full(w, w)],
        out_specs=[pl.BlockSpec((tt, 4 * w), lambda b, i: (rv(b, i), 0)), full(1, w), full(1, w)],
        out_shape=[_sds((n, 4 * w), BF16), _sds((1, w), F32), _sds((1, w), F32)],
        scratch_shapes=[pltpu.VMEM((w, w), F32), pltpu.VMEM((nsub, w, w), F32)] + [pltpu.VMEM((tt, w), F32)] * 8,
        compiler_params=_params(),
    )(dmix, proj, proj, proj, proj, o_h, s0, gh, lb, ones_b)


def _pool_bwd(dmix, proj, wbd, wbd_t, scale, nb, name):
    n = proj.shape[0]
    t = n // nb
    tt = min(256, t)
    nt = t // tt
    nhb = tt // POOL_HALO
    cu, cg, cm = C_UB // POOL_W, C_GB // POOL_W, HGRN_W // POOL_W

    def body(u_ref, g_ref, h_ref, dm_ref, gn_ref, dmn_ref, w_ref, wt_ref, s_ref, dp_ref, dw_ref, ds_ref):
        i = pl.program_id(1)
        first = jnp.logical_and(pl.program_id(0) == 0, i == 0)

        @pl.when(first)
        def _():
            dw_ref[...] = jnp.zeros_like(dw_ref)
            ds_ref[...] = jnp.zeros_like(ds_ref)

        sc = s_ref[...]
        halo = jnp.where(i == 0, 0.0, h_ref[...])
        pooled, cnt = _pool_mix(u_ref[...], halo, i * tt, tt)
        pb = pooled.astype(BF16)
        pre = _dot(pb, w_ref[...])
        gv, dm = g_ref[...], dm_ref[...]
        sg = _sigmoid(gv)
        silu = gv * sg
        dgb = dm * pre * sc * (sg * (1.0 + gv * (1.0 - sg)))
        ds_ref[...] += jnp.sum(dm * pre * silu, axis=0, keepdims=True)
        dpre = (dm * sc * silu).astype(BF16)
        dw_ref[...] += _dot_tn(pb, dpre)
        dpool = _dot(dpre, wt_ref[...])
        gn = gn_ref[...]
        dpre_n = (dmn_ref[...] * sc * (gn * _sigmoid(gn))).astype(BF16)
        dpool_n = jnp.where(i == nt - 1, 0.0, _dot(dpre_n, wt_ref[...]))
        lane, wl = _pool_lane_windows()
        tpos_n = ((i + 1) * tt + lax.broadcasted_iota(jnp.int32, (POOL_HALO, POOL_W), 0)).astype(F32)
        ext = jnp.concatenate([dpool / cnt, dpool_n / jnp.minimum(tpos_n + 1.0, wl)], axis=0)
        rows = tt + POOL_HALO
        sums, cur, shift = [], ext, 1
        for _ in POOL_WINDOWS:
            cur = cur + pltpu.roll(cur, rows - shift, axis=0)
            sums.append(cur[:tt, :])
            shift *= 2
        du = _pool_select(lane, sums) - dpool
        dp_ref[...] = jnp.concatenate([du, dgb], axis=1).astype(BF16)

    def nxt(b, i):
        return jnp.minimum((b * nt + i + 1) * nhb, n // POOL_HALO - 1)

    return _pc(
        body, name, grid=(nb, nt),
        in_specs=_pool_specs(tt, nt, nhb) + [
            pl.BlockSpec((tt, POOL_W), lambda b, i: (b * nt + i, cm)),
            pl.BlockSpec((POOL_HALO, POOL_W), lambda b, i: (nxt(b, i), cg)),
            pl.BlockSpec((POOL_HALO, POOL_W), lambda b, i: (nxt(b, i), cm)),
            pl.BlockSpec((POOL_W, POOL_W), lambda b, i: (0, 0)), pl.BlockSpec((POOL_W, POOL_W), lambda b, i: (0, 0)),
            pl.BlockSpec((1, POOL_W), lambda b, i: (0, 0))],
        out_specs=[pl.BlockSpec((tt, 2 * POOL_W), lambda b, i: (b * nt + i, 0)),
                   pl.BlockSpec((POOL_W, POOL_W), lambda b, i: (0, 0)), pl.BlockSpec((1, POOL_W), lambda b, i: (0, 0))],
        out_shape=[_sds((n, 2 * POOL_W), BF16), _sds((POOL_W, POOL_W), F32), _sds((1, POOL_W), F32)],
        compiler_params=_params(),
    )(proj, proj, proj, dmix, proj, dmix, wbd, wbd_t, scale)


def _fox_gate_bwd(dmix, proj, o_c, name):
    n = proj.shape[0]
    tm = min(512, n)

    def body(dm_ref, gc_ref, oc_ref, da_ref, dg_ref, dl_ref):
        dm, gc, oc = dm_ref[...], gc_ref[...], oc_ref[...]
        sg = _sigmoid(gc)
        da = dm * (gc * sg)
        da_ref[...] = da.astype(BF16)
        dg_ref[...] = (dm * oc * (sg * (1.0 + gc * (1.0 - sg)))).astype(BF16)
        r = lax.broadcasted_iota(jnp.int32, (FOX_W, FC_PAD), 0) // HEAD
        c = lax.broadcasted_iota(jnp.int32, (FOX_W, FC_PAD), 1)
        dl_ref[...] = _dot_sel(da * oc, (r == c).astype(F32))

    def row(wd, j=0):
        return pl.BlockSpec((tm, wd), lambda i: (i, j))

    return _pc(
        body, name, grid=(n // tm,),
        in_specs=[row(FOX_W, (HGRN_W + POOL_W) // FOX_W), row(FOX_W, C_GC // FOX_W), row(FOX_W)],
        out_specs=[row(FOX_W), row(FOX_W), row(FC_PAD)],
        out_shape=[_sds((n, FOX_W), BF16), _sds((n, FOX_W), BF16), _sds((n, FC_PAD), F32)],
        compiler_params=_params(),
    )(dmix, proj, o_c)


def _fox_bwd(proj, da, c_col, c_row, lse_row, delta_row, nb, name, rider=None):
    n = proj.shape[0]
    t = n // nb
    tb = min(256, t)
    nq = t // tb
    pw = 2 * HEAD

    def body(q_ref, k_ref, v_ref, da_ref, cc_ref, cr_ref, lse_ref, dl_ref,
             dq_ref, dk_ref, dv_ref, dck_ref, dcq_ref, dq_s, dk_s, dv_s, dck_s, dcq_s):
        step = pl.program_id(1)
        kj, qi = pairs(step)

        @pl.when(step == 0)
        def _():
            dq_s[...] = jnp.zeros_like(dq_s)
            dcq_s[...] = jnp.zeros_like(dcq_s)

        @pl.when(qi == nq - 1)
        def _():
            dk_s[...] = jnp.zeros_like(dk_s)
            dv_s[...] = jnp.zeros_like(dv_s)
            dck_s[...] = jnp.zeros_like(dck_s)

        def block(masked):
            lo = _lane_lo()
            if masked:
                causal = lax.broadcasted_iota(jnp.int32, (tb, tb), 1) >= lax.broadcasted_iota(jnp.int32, (tb, tb), 0)
            dck = dck_s[...]
            for p in range(FOX_HEADS // 2):
                sl = slice(p * pw, (p + 1) * pw)
                qp = q_ref[:, sl] * FOX_SCALE
                kf = k_ref[:, sl]
                kp = kf.astype(BF16)
                kt = kf.T.astype(BF16)
                vp = v_ref[:, sl].astype(BF16)
                dap = da_ref[:, sl]
                dk, dv = dk_s[:, sl], dv_s[:, sl]
                for h in range(2):
                    hh = 2 * p + h
                    lm = lo if h == 0 else jnp.logical_not(lo)
                    qm = jnp.where(lm, qp, 0.0).astype(BF16)
                    dam = jnp.where(lm, dap, jnp.zeros_like(dap))
                    s = _dot_nt(kp, qm) + (cr_ref[hh:hh + 1, :] - cc_ref[:, hh:hh + 1])
                    pe = jnp.exp(s - lse_ref[hh:hh + 1, :])
                    if masked:
                        pe = jnp.where(causal, pe, 0.0)
                    dp = _dot_nt(vp, dam)
                    ds = pe * (dp - dl_ref[hh:hh + 1, :])
                    dsb = ds.astype(BF16)
                    dv = dv + _dot(pe.astype(BF16), dam)
                    dk = dk + _dot(dsb, qm)
                    rows = slice(hh * HEAD, (hh + 1) * HEAD)
                    dq_s[qi, rows, :] += _dot(kt[h * HEAD:(h + 1) * HEAD, :], dsb)
                    dck = dck - _put_col(jnp.zeros_like(dck), hh, jnp.sum(ds, axis=1, keepdims=True))
                    dcq_s[qi, hh:hh + 1, :] += _rows_reduce(ds, jnp.add, jnp.sum)
                dk_s[:, sl] = dk
                dv_s[:, sl] = dv
            dck_s[...] = dck

        @pl.when(qi > kj)
        def _():
            block(False)

        @pl.when(qi == kj)
        def _():
            block(True)

        @pl.when(qi == kj)
        def _():
            dk_ref[...] = dk_s[...].astype(BF16)
            dv_ref[...] = dv_s[...].astype(BF16)
            dck_ref[...] = dck_s[...]

        @pl.when(step == nq * (nq + 1) // 2 - 1)
        def _():
            for j in range(nq):
                dq_ref[j * tb:(j + 1) * tb, :] = (dq_s[j].T * FOX_SCALE).astype(BF16)
                dcq_ref[:, j * tb:(j + 1) * tb] = dcq_s[j]

    def pairs(step):
        a, b = _tri_pair(step, nq)
        return nq - 1 - a, nq - 1 - b

    def kspec(wd, j=0):
        return pl.BlockSpec((tb, wd), lambda b, st: (b * nq + pairs(st)[0], j))

    def qspec(wd, j=0):
        return pl.BlockSpec((tb, wd), lambda b, st: (b * nq + pairs(st)[1], j))

    def qrow():
        return pl.BlockSpec((None, FOX_HEADS, tb), lambda b, st: (b, 0, pairs(st)[1]))

    return _call(
        body, name, (proj, proj, proj, da, c_col, c_row, lse_row, delta_row), rider, grid=(nb, nq * (nq + 1) // 2),
        in_specs=[qspec(FOX_W, C_QC // FOX_W), kspec(FOX_W, C_KC // FOX_W), kspec(FOX_W, C_VC // FOX_W), qspec(FOX_W),
                  kspec(FC_PAD), qrow(), qrow(), qrow()],
        out_specs=[pl.BlockSpec((t, FOX_W), lambda b, st: (b, 0)), kspec(FOX_W), kspec(FOX_W), kspec(FC_PAD),
                   pl.BlockSpec((None, FOX_HEADS, t), lambda b, st: (b, 0, 0))],
        out_shape=[_sds((n, FOX_W), BF16), _sds((n, FOX_W), BF16), _sds((n, FOX_W), BF16), _sds((n, FC_PAD), F32),
                   _sds((nb, FOX_HEADS, t), F32)],
        scratch_shapes=[pltpu.VMEM((nq, FOX_W, tb), F32), pltpu.VMEM((tb, FOX_W), F32), pltpu.VMEM((tb, FOX_W), F32),
                        pltpu.VMEM((tb, FC_PAD), F32), pltpu.VMEM((nq, FOX_HEADS, tb), F32)],
        compiler_params=_params(),
    )


def _fox_decay_bwd(dc_q, dc_k, fc, bias, nb, name):
    n = fc.shape[0]
    t = n // nb
    tt = min(256, t)
    nt = t // tt

    def body(dcq_ref, dck_ref, fc_ref, b_ref, dfc_ref, db_ref, carry):
        i = pl.program_id(1)
        first = jnp.logical_and(pl.program_id(0) == 0, i == 0)

        @pl.when(first)
        def _():
            db_ref[...] = jnp.zeros_like(db_ref)

        @pl.when(i == 0)
        def _():
            carry[...] = jnp.zeros_like(carry)

        r = lax.broadcasted_iota(jnp.int32, (tt, tt), 0)
        cc = lax.broadcasted_iota(jnp.int32, (tt, tt), 1)
        dlf = _dot_hi((r <= cc).astype(F32), dcq_ref[...] + dck_ref[...]) + carry[...]
        carry[...] = dlf[0:1, :]
        dfc = dlf * _sigmoid(-(fc_ref[...] + b_ref[...]))
        dfc_ref[...] = dfc.astype(BF16)
        db_ref[...] += jnp.sum(dfc, axis=0, keepdims=True)

    def row():
        return pl.BlockSpec((tt, FC_PAD), lambda b, i: (b * nt + (nt - 1 - i), 0))

    return _pc(
        body, name, grid=(nb, nt),
        in_specs=[row(), row(), row(), pl.BlockSpec((1, FC_PAD), lambda b, i: (0, 0))],
        out_specs=[row(), pl.BlockSpec((1, FC_PAD), lambda b, i: (0, 0))],
        out_shape=[_sds((n, FC_PAD), BF16), _sds((1, FC_PAD), F32)],
        scratch_shapes=[pltpu.VMEM((1, FC_PAD), F32)],
        compiler_params=_params(),
    )(dc_q, dc_k, fc, bias)


def _in_proj_bwd(pieces, w_main_t, w_fc_t, x, g_pre, dxo, name, rider=None):
    n, d = x.shape
    tm = min(512, n)
    widths = [p.shape[1] for p, _ in pieces]
    offs = [o for _, o in pieces]
    np_ = len(pieces)

    def body(*refs):
        p_refs = refs[:np_]
        wt_ref, wf_ref, x_ref, g_ref, dxo_ref, dx_ref, dg_ref = refs[np_:]

        @pl.when(pl.program_id(0) == 0)
        def _():
            dg_ref[...] = jnp.zeros_like(dg_ref)

        dh = _dot(p_refs[-1][...], wf_ref[...])
        for pr, wd, off in zip(p_refs[:-1], widths[:-1], offs[:-1]):
            for j in range(0, wd, 512):
                jw = min(512, wd - j)
                dh = dh + _dot(pr[:, j:j + jw], wt_ref[off + j:off + j + jw, :])
        xv = x_ref[...]
        r = lax.rsqrt(jnp.mean(xv * xv, axis=-1, keepdims=True) + NORM_EPS)
        xh = xv * r
        dg_ref[...] += jnp.sum(dh * xh, axis=0, keepdims=True)
        dx_ref[...] = dxo_ref[...] + _rms_bwd(dh * g_ref[...], xh, r)

    row = pl.BlockSpec((tm, d), lambda i: (i, 0))
    return _call(
        body, name, (*[p for p, _ in pieces], w_main_t, w_fc_t, x, g_pre, dxo), rider, grid=(n // tm,),
        in_specs=[pl.BlockSpec((tm, wd), lambda i: (i, 0)) for wd in widths] + [
            pl.BlockSpec((MAIN_W, d), lambda i: (0, 0)), pl.BlockSpec((FC_PAD, d), lambda i: (0, 0)),
            row, pl.BlockSpec((1, d), lambda i: (0, 0)), row],
        out_specs=[row, pl.BlockSpec((1, d), lambda i: (0, 0))],
        out_shape=[_sds((n, d), F32), _sds((1, d), F32)],
        compiler_params=_params(),
    )


def _lower_bound_table(lower_bounds, name):
    depth, w = lower_bounds.shape

    def body(lb_ref, o_ref):
        v = lb_ref[...]
        e = jnp.exp(v - jnp.max(v, axis=0, keepdims=True))
        p = e / jnp.sum(e, axis=0, keepdims=True)
        acc = jnp.zeros((1, w), F32)
        for l in range(depth):
            acc = acc + p[l:l + 1, :]
            o_ref[l:l + 1, :] = acc - p[0:1, :]

    return _pc(body, name, out_shape=_sds((depth, w), F32))(lower_bounds)


def _lower_bound_bwd(lower_bounds, dlbs, name):
    depth, w = lower_bounds.shape

    def body(lb_ref, d_ref, o_ref):
        v, dl = lb_ref[...], d_ref[...]
        e = jnp.exp(v - jnp.max(v, axis=0, keepdims=True))
        p = e / jnp.sum(e, axis=0, keepdims=True)
        tot = jnp.sum(dl, axis=0, keepdims=True)
        rows, tail = [], tot
        for l in range(depth):
            rows.append(tail - tot if l == 0 else tail)
            tail = tail - dl[l:l + 1, :]
        dp = jnp.concatenate(rows, axis=0)
        o_ref[...] = p * (dp - jnp.sum(p * dp, axis=0, keepdims=True))

    return _pc(body, name, out_shape=_sds((depth, w), F32))(lower_bounds, dlbs)


def _place():
    x, y, c = lax.axis_index("x"), lax.axis_index("y"), lax.axis_index("c")
    return x, y, c


def _gather_weights(w_in_b, w_out_b):
    arrays = (w_in_b, w_out_b)
    na = len(arrays)

    def body(*refs):
        ins, outs = refs[:na], refs[na:2 * na]
        send_sems, recv_sems, local_sems = refs[2 * na:]
        x, y, c = _place()
        me, sibling = (x, y, c), (x, y, 1 - c)
        chips = [(1 - x, y), (x, 1 - y), (1 - x, 1 - y)]

        def slot(a, px, py, pc):
            return outs[a].at[4 * px + 2 * py + pc]

        def copy(a, k, block, to, own=False):
            return pltpu.make_async_remote_copy(
                src_ref=ins[a] if own else slot(a, *block), dst_ref=slot(a, *block),
                send_sem=send_sems.at[a * 7 + k], recv_sem=recv_sems.at[a * 7 + k],
                device_id=to, device_id_type=MESH)

        mine = [pltpu.make_async_copy(ins[a], slot(a, *me), local_sems.at[a]) for a in range(na)]
        for cp in mine:
            cp.start()
        first = []
        for a in range(na):
            first.append(copy(a, 0, me, sibling, own=True))
            first += [copy(a, 1 + j, me, (*chip, c), own=True) for j, chip in enumerate(chips)]
        for cp in first:
            cp.start()
        passed = []
        for j, chip in enumerate(chips):
            for a in range(na):
                copy(a, 1 + j, (*chip, c), me).wait_recv()
                fw = copy(a, 4 + j, (*chip, c), sibling)
                fw.start()
                passed.append(fw)
        for a in range(na):
            copy(a, 0, sibling, me).wait_recv()
            for j, chip in enumerate(chips):
                copy(a, 4 + j, (*chip, 1 - c), me).wait_recv()
        for cp in first + passed:
            cp.wait_send()
        for cp in mine:
            cp.wait()

    any_spec = pl.BlockSpec(memory_space=pl.ANY)
    return _pc(
        body, "gather_weights",
        in_specs=[any_spec] * na, out_specs=[any_spec] * na,
        out_shape=[_sds((N_DEV,) + a.shape, a.dtype) for a in arrays],
        scratch_shapes=[pltpu.SemaphoreType.DMA((7 * na,)), pltpu.SemaphoreType.DMA((7 * na,)),
                        pltpu.SemaphoreType.DMA((na,))],
    )(*arrays)


def _peer(k):
    x, y, c = _place()
    return (1 - x if k & 4 else x, 1 - y if k & 2 else y, 1 - c if k & 1 else c)


def _remote(src, dst, sems, s, to):
    return pltpu.make_async_remote_copy(src_ref=src, dst_ref=dst, send_sem=sems[0].at[s], recv_sem=sems[1].at[s],
                                        device_id=to, device_id_type=MESH)


def _gather_rider(shards):
    na = len(shards)

    def plan(ins, outs, *sems):
        x, y, c = _place()
        me = 4 * x + 2 * y + c
        locs = [pltpu.make_async_copy(ins[a], outs[a].at[me], sems[2].at[a]) for a in range(na)]
        sends, recvs = [], []
        for k in range(1, N_DEV):
            px, py, pc = _peer(k)
            for a in range(na):
                s = (k - 1) * na + a
                sends.append(_remote(ins[a], outs[a].at[me], sems, s, (px, py, pc)))
                recvs.append(_remote(ins[a], outs[a].at[4 * px + 2 * py + pc], sems, s, (px, py, pc)))
        return sends, recvs, locs

    return _Rider(shards, [_sds((N_DEV,) + a.shape, a.dtype) for a in shards], (N_DEV - 1) * na, na, plan)


def _direct_exchange_rider(blocks):
    na = len(blocks)

    def plan(ins, outs, *sems):
        x, y, c = _place()
        me = 4 * x + 2 * y + c
        locs = [pltpu.make_async_copy(ins[a].at[c, 2 * x + y], outs[a].at[me], sems[2].at[a]) for a in range(na)]
        sends, recvs = [], []
        for k in range(1, N_DEV):
            px, py, pc = _peer(k)
            for a in range(na):
                s = (k - 1) * na + a
                sends.append(_remote(ins[a].at[pc, 2 * px + py], outs[a].at[me], sems, s, (px, py, pc)))
                recvs.append(_remote(ins[a].at[pc, 2 * px + py], outs[a].at[4 * px + 2 * py + pc], sems, s, (px, py, pc)))
        return sends, recvs, locs

    return _Rider(blocks, [_sds((N_DEV,) + a.shape[2:], a.dtype) for a in blocks], (N_DEV - 1) * na, na, plan)


def _swap_rider(halves):
    na = len(halves)

    def plan(ins, outs, *sems):
        x, y, c = _place()
        cps = [_remote(ins[a].at[1 - c], outs[a], sems, a, (x, y, 1 - c)) for a in range(na)]
        return cps, cps, []

    return _Rider(halves, [_sds(a.shape[1:], a.dtype) for a in halves], na, 1, plan)


def _chip_exchange_rider(parts, small=None):
    na = len(parts)
    n_chip = N_DEV // 2

    def plan(ins, outs, *sems):
        x, y, c = _place()
        chip = 2 * x + y
        locs = [pltpu.make_async_copy(ins[a].at[chip], outs[a].at[chip], sems[2].at[a]) for a in range(na)]
        sends, recvs = [], []
        for k in range(1, n_chip):
            px, py, _ = _peer(2 * k)
            for a in range(na):
                s = (k - 1) * na + a
                sends.append(_remote(ins[a].at[2 * px + py], outs[a].at[chip], sems, s, (px, py, c)))
                recvs.append(_remote(ins[a].at[2 * px + py], outs[a].at[2 * px + py], sems, s, (px, py, c)))
        if small is not None:
            me = 2 * chip + c
            locs.append(pltpu.make_async_copy(ins[na], outs[na].at[me], sems[2].at[na]))
            for k in range(1, N_DEV):
                px, py, pc = _peer(k)
                s = (n_chip - 1) * na + k - 1
                sends.append(_remote(ins[na], outs[na].at[me], sems, s, (px, py, pc)))
                recvs.append(_remote(ins[na], outs[na].at[4 * px + 2 * py + pc], sems, s, (px, py, pc)))
        return sends, recvs, locs

    extra = [] if small is None else [small]
    shapes = [_sds(a.shape, a.dtype) for a in parts] + [_sds((N_DEV,) + s.shape, s.dtype) for s in extra]
    n_sems = (n_chip - 1) * na + (N_DEV - 1) * len(extra)
    return _Rider(list(parts) + extra, shapes, n_sems, na + len(extra), plan)


def _pair_add(halves, other, core, name):
    _, nch, r, c = halves.shape
    tr = 256 if r % 256 == 0 else r

    def body(h_ref, o_ref, c_ref, p_ref):
        mine = jnp.where(c_ref[...] == 0, h_ref[0].astype(F32), h_ref[1].astype(F32))
        p_ref[...] = (mine + o_ref[...].astype(F32)).astype(BF16)

    blk = pl.BlockSpec((None, tr, c), lambda j, i: (j, i, 0))
    return _pc(
        body, name, grid=(nch, r // tr),
        in_specs=[pl.BlockSpec((2, None, tr, c), lambda j, i: (0, j, i, 0)), blk, pl.BlockSpec((1, 1), lambda j, i: (0, 0))],
        out_specs=blk, out_shape=_sds((nch, r, c), BF16),
        compiler_params=_params(),
    )(halves, other, core)


def _sum_adamw(parts, w, m, v, name, rider=None):
    nl, r, c = w.shape
    tr = 256 if r % 256 == 0 else r

    def body(*refs):
        p_refs = refs[:nl]
        w_ref, m_ref, v_ref, g_ref, d_ref, mo_ref, vo_ref = refs[nl:]
        for l in range(nl):
            @pl.when(pl.program_id(0) == l)
            def _(p_ref=p_refs[l]):
                g = p_ref[0].astype(F32)
                for j in range(1, p_ref.shape[0]):
                    g = g + p_ref[j].astype(F32)
                mn = ADAM_B1 * m_ref[...] + (1.0 - ADAM_B1) * g
                vn = ADAM_B2 * v_ref[...] + (1.0 - ADAM_B2) * (g * g)
                m_hat = mn / (1.0 - ADAM_B1 ** ADAM_STEP)
                v_hat = vn / (1.0 - ADAM_B2 ** ADAM_STEP)
                g_ref[...] = g
                d_ref[...] = -ADAM_LR * (m_hat / (jnp.sqrt(v_hat) + ADAM_EPS) + ADAM_WD * w_ref[...])
                mo_ref[...] = mn
                vo_ref[...] = vn

    def part_spec(l, k):
        return pl.BlockSpec((k, tr, c), lambda li, i: (0, jnp.where(li == l, i, 0), 0))

    row = pl.BlockSpec((None, tr, c), lambda li, i: (li, i, 0))
    return _call(
        body, name, (*parts, w, m, v), rider, grid=(nl, r // tr),
        in_specs=[part_spec(l, p.shape[0]) for l, p in enumerate(parts)] + [row, row, row],
        out_specs=[row] * 4,
        out_shape=[_sds((nl, r, c), F32)] * 4,
        compiler_params=_params(),
    )


SMALL = ("lower_bounds", "pre_norm_g", "hgrn_norm_g", "fox_f_bias", "pool_w", "pool_scale", "post_norm_g")
SMALL_LANES = 128


def _small_size(tree):
    return sum(tree[k].size for k in SMALL)


def _pack_small(tree, extra=None):
    flat = jnp.concatenate([tree[k].reshape(-1) for k in SMALL] + ([] if extra is None else [extra.reshape(1)]))
    rows = -(-(_small_size(tree) + 1) // (8 * SMALL_LANES)) * 8
    return jnp.pad(flat, (0, rows * SMALL_LANES - flat.shape[0])).reshape(rows, SMALL_LANES)


def _unpack_small(packed, like):
    flat, out, off = packed.reshape(-1), {}, 0
    for k in SMALL:
        size = like[k].size
        out[k] = flat[off:off + size].reshape(like[k].shape)
        off += size
    return out


def _block_diag(pw):
    g = pw.shape[0]
    eye = jnp.eye(g, dtype=pw.dtype)
    return (eye[:, None, :, None] * pw[:, :, None, :]).reshape(g * HEAD, g * HEAD)


def _assemble_w_in(g_in, name):
    _, d, shard = g_in.shape
    tr = min(256, d)
    wide = MAIN_W + FC_PAD

    def body(g_ref, wm_ref, wf_ref, wmt_ref, wft_ref, row_s):
        row_s[:, MAIN_W:] = jnp.zeros((tr, FC_PAD), F32)
        for j in range(N_DEV):
            row_s[:, j * shard:(j + 1) * shard] = g_ref[j].astype(F32)
        wm_ref[...] = row_s[:, :MAIN_W].astype(BF16)
        wf_ref[...] = row_s[:, MAIN_W:].astype(BF16)
        for j in range(0, MAIN_W, 512):
            wmt_ref[j:j + 512, :] = row_s[:, j:j + 512].T.astype(BF16)
        wft_ref[...] = row_s[:, MAIN_W:].T.astype(BF16)

    return _pc(
        body, name, grid=(d // tr,),
        in_specs=[pl.BlockSpec((N_DEV, tr, shard), lambda i: (0, i, 0))],
        out_specs=[pl.BlockSpec((tr, MAIN_W), lambda i: (i, 0)), pl.BlockSpec((tr, FC_PAD), lambda i: (i, 0)),
                   pl.BlockSpec((MAIN_W, tr), lambda i: (0, i)), pl.BlockSpec((FC_PAD, tr), lambda i: (0, i))],
        out_shape=[_sds((d, MAIN_W), BF16), _sds((d, FC_PAD), BF16), _sds((MAIN_W, d), BF16), _sds((FC_PAD, d), BF16)],
        scratch_shapes=[pltpu.VMEM((tr, wide), F32)],
        compiler_params=_params(),
    )(g_in)


def _layer_weights(l, g_in, g_out):
    d = g_in.shape[1]
    w_main, w_fc, w_main_t, w_fc_t = _assemble_w_in(g_in, f"assemble_w_in_{l}")
    full_out = g_out.reshape(N_DEV * g_out.shape[1], d)
    return w_main, w_fc, w_main_t, w_fc_t, full_out, full_out.T


def _layer_fwd(l, x, lbs, weights, lw, nb, rider=None, target=None):
    n = x.shape[0]
    t = n // nb
    w_main, w_fc, _, _, w_out, _ = lw
    bias = jnp.pad(weights["fox_f_bias"][l:l + 1], ((0, 0), (0, FC_PAD - FOX_HEADS)))
    wbd = _block_diag(weights["pool_w"][l]).astype(BF16)
    proj, fc, ht = _in_proj_fwd(x, weights["pre_norm_g"][l:l + 1], w_main, w_fc, f"in_proj_fwd_{l}")
    c_col = _fox_decay_fwd(fc, bias, nb, f"fox_decay_fwd_{l}")
    c_row = c_col.reshape(nb, t, FC_PAD)[:, :, :FOX_HEADS].transpose(0, 2, 1)
    o_h, s0 = _hgrn_fwd(proj, lbs[l:l + 1], _block_ones(HGRN_W, BF16), nb, f"hgrn_fwd_{l}")
    o_b = _pool_fwd(proj, wbd, weights["pool_scale"][l:l + 1], nb, f"pool_fwd_{l}")
    (o_c, lse), rode = _fox_fwd(proj, c_col, c_row, nb, f"fox_fwd_{l}", rider)
    x_next, mixt, y = _merge_fwd(x, proj, o_h, o_b, o_c, weights["hgrn_norm_g"][l:l + 1], w_out,
                                 weights["post_norm_g"][l:l + 1], f"merge_fwd_{l}", target)
    return x_next, (x, proj, fc, ht, c_col, c_row, o_h, s0, o_c, lse, mixt, y, bias, wbd), rode


def _layer_bwd(l, dx, saved, lbs, weights, lw, nb, rider=None):
    _, proj, fc, ht, c_col, c_row, o_h, s0, o_c, lse, mixt, y, bias, wbd = saved
    n = proj.shape[0]
    t = n // nb
    w_out_t = lw[5]
    g = {}
    dy, dmix, dgp = _merge_bwd(dx, y, weights["post_norm_g"][l:l + 1], w_out_t, f"merge_bwd_{l}")
    g["post_norm_g"] = dgp[0]
    g["w_out"] = _w_out_grad(mixt, dy, f"w_out_grad_{l}")
    d_a, dgh, dlb = _hgrn_bwd(dmix, proj, o_h, s0, weights["hgrn_norm_g"][l:l + 1], lbs[l:l + 1],
                              _block_ones(HGRN_W, BF16), nb, f"hgrn_bwd_{l}")
    g["hgrn_norm_g"], g["lbs"] = dgh[0], dlb[0]
    d_b, dwbd, dps = _pool_bwd(dmix, proj, wbd, wbd.T, weights["pool_scale"][l:l + 1], nb, f"pool_bwd_{l}")
    g["pool_w"] = jnp.stack([dwbd[j * HEAD:(j + 1) * HEAD, j * HEAD:(j + 1) * HEAD] for j in range(len(POOL_WINDOWS))])
    g["pool_scale"] = dps[0]
    da, d_gc, delta = _fox_gate_bwd(dmix, proj, o_c, f"fox_gate_bwd_{l}")
    delta_row, lse_row = [a.reshape(nb, t, FC_PAD)[:, :, :FOX_HEADS].transpose(0, 2, 1) for a in (delta, lse)]
    (d_qc, d_kc, d_vc, dc_k, dc_q), rode = _fox_bwd(proj, da, c_col, c_row, lse_row, delta_row, nb, f"fox_bwd_{l}", rider)
    dc_q = jnp.pad(dc_q.transpose(0, 2, 1).reshape(n, FOX_HEADS), ((0, 0), (0, FC_PAD - FOX_HEADS)))
    d_fc, dbias = _fox_decay_bwd(dc_q, dc_k, fc, bias, nb, f"fox_decay_bwd_{l}")
    g["fox_f_bias"] = dbias[0, :FOX_HEADS]
    pieces = [(d_a, C_QA), (d_b, C_UB), (d_qc, C_QC), (d_kc, C_KC), (d_vc, C_VC), (d_gc, C_GC), (d_fc, None)]
    g["w_in"] = _w_in_grad(ht, pieces, f"w_in_grad_{l}")
    return g, pieces, rode


def _layer_bwd_input(l, dx, pieces, saved, weights, lw, rider=None):
    (dxi, dgpre), rode = _in_proj_bwd(pieces, lw[2], lw[3], saved[0], weights["pre_norm_g"][l:l + 1], dx,
                                      f"in_proj_bwd_{l}", rider)
    return dxi, dgpre[0], rode


def kernel(x, lower_bounds, pre_norm_g, w_in, hgrn_norm_g, fox_f_bias, pool_w, pool_scale, w_out, post_norm_g, loss_target, m_lower_bounds, m_pre_norm_g, m_w_in, m_hgrn_norm_g, m_fox_f_bias, m_pool_w, m_pool_scale, m_w_out, m_post_norm_g, v_lower_bounds, v_pre_norm_g, v_w_in, v_hgrn_norm_g, v_fox_f_bias, v_pool_w, v_pool_scale, v_w_out, v_post_norm_g):
    weights = dict(lower_bounds=lower_bounds, pre_norm_g=pre_norm_g, hgrn_norm_g=hgrn_norm_g, fox_f_bias=fox_f_bias,
                   pool_w=pool_w, pool_scale=pool_scale, post_norm_g=post_norm_g)
    mom_m = dict(lower_bounds=m_lower_bounds, pre_norm_g=m_pre_norm_g, hgrn_norm_g=m_hgrn_norm_g, fox_f_bias=m_fox_f_bias,
                 pool_w=m_pool_w, pool_scale=m_pool_scale, post_norm_g=m_post_norm_g)
    mom_v = dict(lower_bounds=v_lower_bounds, pre_norm_g=v_pre_norm_g, hgrn_norm_g=v_hgrn_norm_g, fox_f_bias=v_fox_f_bias,
                 pool_w=v_pool_w, pool_scale=v_pool_scale, post_norm_g=v_post_norm_g)
    depth = w_in.shape[0]
    nb, t, d = x.shape
    n = nb * t
    core = lax.axis_index("c").astype(jnp.int32).reshape(1, 1)
    shards = [(w_in[l].astype(BF16), w_out[l].astype(BF16)) for l in range(depth)]
    lbs = _lower_bound_table(lower_bounds, "lower_bound_table")

    lw = [_layer_weights(0, *_gather_weights(*shards[0]))]
    xl, saved = x.reshape(n, d), []
    for l in range(depth):
        last = l + 1 == depth
        rider = None if last else _gather_rider(shards[l + 1])
        xl, sv, rode = _layer_fwd(l, xl, lbs, weights, lw[l], nb, rider, loss_target.reshape(n, d) if last else None)
        saved.append(sv)
        if rode is not None:
            lw.append(_layer_weights(l + 1, *rode))
    dx, sq = xl
    loss_here = 0.5 * jnp.sum(sq) / d

    grads, recv, pending = [None] * depth, [None] * depth, None
    for l in reversed(range(depth)):
        g, pieces, rode = _layer_bwd(l, dx, saved[l], lbs, weights, lw[l], nb, pending)
        if rode is not None:
            recv[l + 1] = rode
        blocks = (g["w_in"], g["w_out"])
        if l > 0:
            pending = _direct_exchange_rider(blocks)
            dx, g["pre_norm_g"], _ = _layer_bwd_input(l, dx, pieces, saved[l], weights, lw[l])
        else:
            other = _run_rider(_swap_rider(blocks), "grad_swap")
            summed = [_pair_add(hv, ot, core, f"grad_pair_add_{i}") for i, (hv, ot) in enumerate(zip(blocks, other))]
            dx, g["pre_norm_g"], recv[l] = _layer_bwd_input(l, dx, pieces, saved[l], weights, lw[l],
                                                            _chip_exchange_rider(summed))
        grads[l] = g
    small = {k: jnp.stack([grads[l][k] for l in range(depth)]) for k in SMALL if k != "lower_bounds"}
    small["lower_bounds"] = _lower_bound_bwd(lower_bounds, jnp.stack([grads[l]["lbs"] for l in range(depth)]),
                                             "lower_bound_bwd")
    (r_small,) = _run_rider(_gather_rider([_pack_small(small, loss_here)]), "small_grads_gather")

    res_in, _ = _sum_adamw([recv[l][0] for l in range(depth)], w_in, m_w_in, v_w_in, "adamw_w_in")
    res_out, _ = _sum_adamw([recv[l][1] for l in range(depth)], w_out, m_w_out, v_w_out, "adamw_w_out")
    res_small, _ = _sum_adamw([r_small], _pack_small(weights)[None], _pack_small(mom_m)[None], _pack_small(mom_v)[None],
                              "adamw_small")
    loss = res_small[0][0].reshape(-1)[_small_size(weights)]

    names = ("lower_bounds", "pre_norm_g", "w_in", "hgrn_norm_g", "fox_f_bias", "pool_w", "pool_scale", "w_out", "post_norm_g")
    outs = [loss, dx.reshape(nb, t, d)]
    for i in range(4):
        full = dict(_unpack_small(res_small[i][0], weights), w_in=res_in[i], w_out=res_out[i])
        outs += [full[k] for k in names]
    return tuple(outs)
```

```python
import functools

import jax
import jax.numpy as jnp
from jax import lax
from jax.experimental import pallas as pl
from jax.experimental.pallas import tpu as pltpu

F32, BF16 = jnp.float32, jnp.bfloat16
HI = lax.Precision.HIGHEST
MESH = pl.DeviceIdType.MESH
AXES = ("x", "y", "c")
N_DEV = 8

NORM_EPS = 1e-6
MASK_VALUE = -1e30
TINY = 1e-30
CHUNK = 64
SUB = 16
HGRN_W, POOL_W, FOX_W = 256, 256, 512
HEAD = 64
FOX_HEADS = 8
POOL_WINDOWS = (2, 4, 8, 16)
POOL_HALO = 16
MAIN_W = 3584
FC_PAD = 128
C_QA, C_FA, C_IA, C_GA, C_UB, C_GB, C_QC, C_KC, C_VC, C_GC = 0, 256, 512, 768, 1024, 1280, 1536, 2048, 2560, 3072
FOX_SCALE = HEAD ** -0.5

ADAM_LR, ADAM_B1, ADAM_B2, ADAM_EPS, ADAM_WD, ADAM_STEP = 0.001, 0.9, 0.999, 1e-08, 0.01, 10

VMEM_LIMIT = 56 * 1024 * 1024


def _pc(fn, name, **kw):
    return pl.pallas_call(fn, name=name, **kw)


def _params(**kw):
    return pltpu.CompilerParams(vmem_limit_bytes=VMEM_LIMIT, **kw)


class _Rider:
    def __init__(self, inputs, out_shapes, n_sems, n_local, plan):
        self.inputs, self.out_shapes, self.n_sems, self.n_local, self.plan = list(inputs), list(out_shapes), n_sems, n_local, plan

    def start(self, ins, outs, *sems):
        sends, _, locs = self.plan(ins, outs, *sems)
        for cp in locs + sends:
            cp.start()

    def wait(self, ins, outs, *sems):
        sends, recvs, locs = self.plan(ins, outs, *sems)
        for cp in recvs:
            cp.wait_recv()
        for cp in sends:
            cp.wait_send()
        for cp in locs:
            cp.wait()

    def sem_shapes(self):
        return [pltpu.SemaphoreType.DMA((self.n_sems,)), pltpu.SemaphoreType.DMA((self.n_sems,)),
                pltpu.SemaphoreType.DMA((self.n_local,))]


def _call(body, name, args, rider=None, *, grid, in_specs, out_specs, out_shape, scratch_shapes=(), **kw):
    if rider is None:
        res = _pc(body, name, grid=grid, in_specs=in_specs, out_specs=out_specs, out_shape=out_shape,
                  scratch_shapes=list(scratch_shapes), **kw)(*args)
        return res, None
    n_in, n_out, n_scr = len(in_specs), len(out_specs), len(scratch_shapes)
    n_rin, n_rout = len(rider.inputs), len(rider.out_shapes)

    def ridden(*refs):
        ins, refs = refs[:n_in], refs[n_in:]
        rins, refs = refs[:n_rin], refs[n_rin:]
        outs, refs = refs[:n_out], refs[n_out:]
        routs, refs = refs[:n_rout], refs[n_rout:]
        scr, sems = refs[:n_scr], refs[n_scr:]
        first = functools.reduce(jnp.logical_and, [pl.program_id(a) == 0 for a in range(len(grid))])
        last = functools.reduce(jnp.logical_and, [pl.program_id(a) == g - 1 for a, g in enumerate(grid)])

        @pl.when(first)
        def _():
            rider.start(rins, routs, *sems)

        body(*ins, *outs, *scr)

        @pl.when(last)
        def _():
            rider.wait(rins, routs, *sems)

    any_spec = pl.BlockSpec(memory_space=pl.ANY)
    res = _pc(ridden, name, grid=grid, in_specs=list(in_specs) + [any_spec] * n_rin,
              out_specs=list(out_specs) + [any_spec] * n_rout, out_shape=list(out_shape) + rider.out_shapes,
              scratch_shapes=list(scratch_shapes) + rider.sem_shapes(), **kw)(*args, *rider.inputs)
    return res[:n_out], res[n_out:]


def _run_rider(rider, name):
    n_rin = len(rider.inputs)

    def body(*refs):
        ins, outs, sems = refs[:n_rin], refs[n_rin:n_rin + len(rider.out_shapes)], refs[n_rin + len(rider.out_shapes):]
        rider.start(ins, outs, *sems)
        rider.wait(ins, outs, *sems)

    any_spec = pl.BlockSpec(memory_space=pl.ANY)
    return _pc(body, name, in_specs=[any_spec] * n_rin, out_specs=[any_spec] * len(rider.out_shapes),
               out_shape=rider.out_shapes, scratch_shapes=rider.sem_shapes())(*rider.inputs)


def _dot(a, b):
    return jnp.dot(a, b, preferred_element_type=F32)


def _dot_nt(a, b):
    return lax.dot_general(a, b, (((1,), (1,)), ((), ())), preferred_element_type=F32)


def _dot_tn(a, b):
    return lax.dot_general(a, b, (((0,), (0,)), ((), ())), preferred_element_type=F32)


def _dot_hi(a, b):
    return jnp.dot(a, b, precision=HI, preferred_element_type=F32)


def _split2(x):
    hi = x.astype(BF16)
    return hi, (x - hi.astype(F32)).astype(BF16)


def _sel_dot(sel, x):
    hi, lo = _split2(x)
    sb = sel.astype(BF16)
    return _dot(sb, hi) + _dot(sb, lo)


def _dot_sel(x, sel):
    hi, lo = _split2(x)
    sb = sel.astype(BF16)
    return _dot(hi, sb) + _dot(lo, sb)


def _sigmoid(x):
    return 1.0 / (1.0 + jnp.exp(-x))


def _block_ones(n, dtype):
    r = lax.broadcasted_iota(jnp.int32, (n, n), 0) // HEAD
    c = lax.broadcasted_iota(jnp.int32, (n, n), 1) // HEAD
    return (r == c).astype(dtype)


def _sds(shape, dtype):
    return jax.ShapeDtypeStruct(shape, dtype)


def _in_proj_fwd(x, g_pre, w_main, w_fc, bias, nb, name):
    n, d = x.shape
    tm = min(512, n // nb)
    per_example = n // nb // tm

    def body(x_ref, g_ref, w_ref, wf_ref, b_ref, proj_ref, fc_ref, ht_ref, c_ref, carry):
        @pl.when(pl.program_id(0) % per_example == 0)
        def _():
            carry[...] = jnp.zeros_like(carry)

        xv = x_ref[...]
        r = lax.rsqrt(jnp.mean(xv * xv, axis=-1, keepdims=True) + NORM_EPS)
        hf = xv * r * g_ref[...]
        hb = hf.astype(BF16)
        ht_ref[...] = hf.T.astype(BF16)
        for j in range(0, MAIN_W, 512):
            proj_ref[:, j:j + 512] = _dot(hb, w_ref[:, j:j + 512])
        fc = _dot(hb, wf_ref[...])
        fc_ref[...] = fc
        fb = fc + b_ref[...]
        lf = jnp.minimum(fb, 0.0) - jnp.log(1.0 + jnp.exp(-jnp.abs(fb)))
        rows = lax.broadcasted_iota(jnp.int32, (tm, tm), 0)
        cols = lax.broadcasted_iota(jnp.int32, (tm, tm), 1)
        cs = _dot_hi((rows >= cols).astype(F32), lf) + carry[...]
        c_ref[...] = cs
        carry[...] = cs[tm - 1:tm, :]

    def row(wd):
        return pl.BlockSpec((tm, wd), lambda i: (i, 0))

    def full(a, b):
        return pl.BlockSpec((a, b), lambda i: (0, 0))

    return _pc(
        body, name, grid=(n // tm,),
        in_specs=[row(d), full(1, d), full(d, MAIN_W), full(d, FC_PAD), full(1, FC_PAD)],
        out_specs=[row(MAIN_W), row(FC_PAD), pl.BlockSpec((d, tm), lambda i: (0, i)), row(FC_PAD)],
        out_shape=[_sds((n, MAIN_W), F32), _sds((n, FC_PAD), F32), _sds((d, n), BF16), _sds((n, FC_PAD), F32)],
        scratch_shapes=[pltpu.VMEM((1, FC_PAD), F32)],
        compiler_params=_params(),
    )(x, g_pre, w_main, w_fc, bias)


def _hgrn_gates(q, z, lb):
    sig = _sigmoid(z)
    sn = _sigmoid(-z)
    f = lb + (1.0 - lb) * sig
    g = jnp.log(jnp.maximum(f, TINY))
    k = (1.0 - lb) * sn
    sq = _sigmoid(q)
    return sig, sn, f, g, k, sq


def _sub_tri(n, lower):
    r = lax.broadcasted_iota(jnp.int32, (n, n), 0)
    c = lax.broadcasted_iota(jnp.int32, (n, n), 1)
    tri = (r >= c) if lower else (r <= c)
    return jnp.logical_and(r // SUB == c // SUB, tri).astype(F32)


def _live_rows(t):
    return 8 * (t // 8 + 1)


def _pad_rows(x):
    return x if x.shape[0] == SUB else jnp.concatenate([x, jnp.zeros((SUB - x.shape[0], x.shape[1]), x.dtype)], axis=0)


def _hgrn_decays(qs, k, b):
    srow = lax.broadcasted_iota(jnp.int32, (SUB, HGRN_W), 0)
    es, ws = [], []
    for t in range(SUB):
        r = _live_rows(t)
        e = jnp.where(srow[:r] <= t, jnp.exp(b[t:t + 1, :] - b[:r]), 0.0)
        es.append(e)
        ws.append(_pad_rows(e * (qs[t:t + 1, :] * k[:r])))
    return srow, es, ws


def _hgrn_state_step(st, k, v, b, bmask):
    bl = b[SUB - 1:SUB, :]
    ktil = k * jnp.exp(bl - b)
    return st * jnp.exp(bl) + _dot_tn(v.astype(BF16), ktil.astype(BF16)) * bmask


def _hgrn_tile(t):
    return min(256, t)


def _hgrn_fwd(proj, lb, ones_b, nb, name):
    n = proj.shape[0]
    t = n // nb
    tt = _hgrn_tile(t)
    nt = t // tt
    ncs = tt // CHUNK
    w = HGRN_W

    def body(q_ref, z_ref, v_ref, lb_ref, ones_ref, o_ref, s0_ref, st_s, b_s, qs_s, k_s):
        @pl.when(pl.program_id(1) == 0)
        def _():
            st_s[...] = jnp.zeros_like(st_s)

        q = q_ref[...]
        _, _, _, g, k, sq = _hgrn_gates(q, z_ref[...], lb_ref[...])
        b_s[...] = _sel_dot(_sub_tri(tt, True), g)
        qs_s[...] = q * sq
        k_s[...] = k
        bmask = _block_ones(w, F32)
        ones_b = ones_ref[...]

        def chunk(c, carry):
            st = st_s[...]
            s0_ref[c] = st
            base = pl.multiple_of(c * CHUNK, CHUNK)
            tiles = []
            for u in range(CHUNK // SUB):
                rows = pl.ds(base + u * SUB, SUB)
                tiles.append((qs_s[rows, :], k_s[rows, :], v_ref[rows, :], b_s[rows, :]))
            aexps = []
            for qs, k, v, b in tiles:
                _, _, ws = _hgrn_decays(qs, k, b)
                aexps.append(_dot(jnp.concatenate(ws, axis=0).astype(BF16), ones_b))
            inters = []
            for qs, k, v, b in tiles:
                inters.append(_dot_nt((qs * jnp.exp(b)).astype(BF16), st.astype(BF16)))
                st = _hgrn_state_step(st, k, v, b, bmask)
            st_s[...] = st
            for u, ((qs, k, v, b), aexp, o) in enumerate(zip(tiles, aexps, inters)):
                for t in range(SUB):
                    r = _live_rows(t)
                    row = o[t:t + 1, :] + jnp.sum(aexp[t * SUB:t * SUB + r, :] * v[:r], axis=0, keepdims=True)
                    o_ref[pl.ds(base + u * SUB + t, 1), :] = row
            return carry

        lax.fori_loop(0, ncs, chunk, 0)

    def col(j):
        return pl.BlockSpec((tt, w), lambda b, i: (b * nt + i, j))

    return _pc(
        body, name, grid=(nb, nt),
        in_specs=[col(C_QA // w), col(C_FA // w), col(C_IA // w), pl.BlockSpec((1, w), lambda b, i: (0, 0)),
                  pl.BlockSpec((w, w), lambda b, i: (0, 0))],
        out_specs=[pl.BlockSpec((tt, w), lambda b, i: (b * nt + i, 0)),
                   pl.BlockSpec((ncs, w, w), lambda b, i: (b * nt + i, 0, 0))],
        out_shape=[_sds((n, w), F32), _sds((n // CHUNK, w, w), F32)],
        scratch_shapes=[pltpu.VMEM((w, w), F32)] + [pltpu.VMEM((tt, w), F32)] * 3,
        compiler_params=_params(),
    )(proj, proj, proj, lb, ones_b)


def _pool_lane_windows():
    lane = lax.broadcasted_iota(jnp.int32, (1, POOL_W), 1) // HEAD
    wl = jnp.zeros((1, POOL_W), F32)
    for gi, win in enumerate(POOL_WINDOWS):
        wl = jnp.where(lane == gi, float(win), wl)
    return lane, wl


def _pool_select(lane, parts):
    out = parts[-1]
    for gi in range(len(parts) - 2, -1, -1):
        out = jnp.where(lane == gi, parts[gi], out)
    return out


def _pool_mix(u, halo, t0, tt):
    lane, wl = _pool_lane_windows()
    ext = jnp.concatenate([halo, u], axis=0)
    sums, cur, shift = [], ext, 1
    for _ in POOL_WINDOWS:
        cur = cur + pltpu.roll(cur, shift, axis=0)
        sums.append(cur[POOL_HALO:, :])
        shift *= 2
    tpos = (t0 + lax.broadcasted_iota(jnp.int32, (tt, POOL_W), 0)).astype(F32)
    cnt = jnp.minimum(tpos + 1.0, wl)
    return _pool_select(lane, sums) / cnt - u, cnt


def _pool_specs(tt, nt, nhb):
    cu, cg = C_UB // POOL_W, C_GB // POOL_W
    return [pl.BlockSpec((tt, POOL_W), lambda b, i: (b * nt + i, cu)),
            pl.BlockSpec((tt, POOL_W), lambda b, i: (b * nt + i, cg)),
            pl.BlockSpec((POOL_HALO, POOL_W), lambda b, i: (jnp.maximum((b * nt + i) * nhb - 1, 0), cu))]


def _pool_fwd(proj, wbd, scale, nb, name):
    n = proj.shape[0]
    t = n // nb
    tt = min(256, t)
    nt = t // tt
    nhb = tt // POOL_HALO

    def body(u_ref, g_ref, h_ref, w_ref, s_ref, o_ref):
        i = pl.program_id(1)
        halo = jnp.where(i == 0, 0.0, h_ref[...])
        pooled, _ = _pool_mix(u_ref[...], halo, i * tt, tt)
        gv = g_ref[...]
        o_ref[...] = _dot(pooled.astype(BF16), w_ref[...]) * s_ref[...] * (gv * _sigmoid(gv))

    return _pc(
        body, name, grid=(nb, nt),
        in_specs=_pool_specs(tt, nt, nhb) + [pl.BlockSpec((POOL_W, POOL_W), lambda b, i: (0, 0)),
                                             pl.BlockSpec((1, POOL_W), lambda b, i: (0, 0))],
        out_specs=pl.BlockSpec((tt, POOL_W), lambda b, i: (b * nt + i, 0)),
        out_shape=_sds((n, POOL_W), F32),
        compiler_params=_params(),
    )(proj, proj, proj, wbd, scale)


def _rows_reduce(x, op, final):
    while x.shape[0] > 8 and x.shape[0] % 16 == 0:
        half = x.shape[0] // 2
        x = op(x[:half], x[half:])
    return final(x, axis=0, keepdims=True)


def _tri_pair(step, n):
    a = sum([(step >= r * (r + 1) // 2).astype(jnp.int32) for r in range(1, n)], jnp.int32(0))
    return a, step - a * (a + 1) // 2


def _lane_lo():
    return lax.broadcasted_iota(jnp.int32, (1, 2 * HEAD), 1) < HEAD


def _put_col(tile, hh, colv):
    lane = lax.broadcasted_iota(jnp.int32, tile.shape, 1)
    return jnp.where(lane == hh, colv, tile)


def _fox_fwd(proj, c_col, c_row, nb, name, rider=None):
    n = proj.shape[0]
    t = n // nb
    tb = min(256, t)
    nq = t // tb
    pw = 2 * HEAD

    def body(q_ref, k_ref, v_ref, cc_ref, cr_ref, o_ref, lse_ref, m_s, acc_s, cq_s):
        qi, kj = _tri_pair(pl.program_id(1), nq)

        @pl.when(kj == 0)
        def _():
            m_s[...] = jnp.full_like(m_s, -jnp.inf)
            acc_s[...] = jnp.zeros_like(acc_s)
            for hh in range(FOX_HEADS):
                cq_s[hh] = jnp.broadcast_to(cc_ref[:, hh:hh + 1], (tb, pw))

        def block(masked):
            lo = _lane_lo()
            if masked:
                causal = lax.broadcasted_iota(jnp.int32, (tb, tb), 0) >= lax.broadcasted_iota(jnp.int32, (tb, tb), 1)
            def lanes(hh):
                return lo if hh % 2 == 0 else jnp.logical_not(lo)

            def scores(hh):
                sl = slice((hh // 2) * pw, (hh // 2 + 1) * pw)
                return _dot_nt(jnp.where(lanes(hh), q_ref[:, sl] * FOX_SCALE, 0.0).astype(BF16), k_ref[:, sl].astype(BF16))

            ahead = scores(0)
            for hh in range(FOX_HEADS):
                s = ahead
                if hh + 1 < FOX_HEADS:
                    ahead = scores(hh + 1)
                s = s + (jnp.tile(cq_s[hh], (1, tb // pw)) - cr_ref[hh:hh + 1, :])
                if masked:
                    s = jnp.where(causal, s, MASK_VALUE)
                m_prev = m_s[hh]
                m_new = jnp.maximum(m_prev, jnp.max(s, axis=1, keepdims=True))
                alpha = jnp.exp(m_prev - m_new)
                pe = jnp.exp(s - jnp.tile(m_new, (1, tb // pw)))
                m_s[hh] = m_new
                vf = v_ref[:, (hh // 2) * pw:(hh // 2 + 1) * pw]
                acc_s[hh] = alpha * acc_s[hh] + _dot(pe.astype(BF16), jnp.where(lanes(hh), vf, 1.0).astype(BF16))

        @pl.when(kj < qi)
        def _():
            block(False)

        @pl.when(kj == qi)
        def _():
            block(True)
            lo = _lane_lo()
            lse = jnp.zeros((tb, FC_PAD), F32)
            for p in range(FOX_HEADS // 2):
                halves = []
                for h in range(2):
                    hh = 2 * p + h
                    lm = lo if h == 0 else jnp.logical_not(lo)
                    acc = acc_s[hh]
                    swapped = pltpu.roll(acc, HEAD, axis=1)
                    halves.append(acc / swapped)
                    lse = _put_col(lse, hh, m_s[hh] + jnp.log(jnp.where(lm, swapped, acc)))
                o_ref[:, p * pw:(p + 1) * pw] = jnp.where(lo, halves[0], halves[1])
            lse_ref[...] = lse

    def qspec(wd, j):
        return pl.BlockSpec((tb, wd), lambda b, st: (b * nq + _tri_pair(st, nq)[0], j))

    def kspec(j):
        return pl.BlockSpec((tb, FOX_W), lambda b, st: (b * nq + _tri_pair(st, nq)[1], j))

    return _call(
        body, name, (proj, proj, proj, c_col, c_row), rider, grid=(nb, nq * (nq + 1) // 2),
        in_specs=[qspec(FOX_W, C_QC // FOX_W), kspec(C_KC // FOX_W), kspec(C_VC // FOX_W), qspec(FC_PAD, 0),
                  pl.BlockSpec((None, FOX_HEADS, tb), lambda b, st: (b, 0, _tri_pair(st, nq)[1]))],
        out_specs=[qspec(FOX_W, 0), qspec(FC_PAD, 0)],
        out_shape=[_sds((n, FOX_W), F32), _sds((n, FC_PAD), F32)],
        scratch_shapes=[pltpu.VMEM((FOX_HEADS, tb, pw), F32), pltpu.VMEM((FOX_HEADS, tb, pw), F32),
                        pltpu.VMEM((FOX_HEADS, tb, pw), F32)],
        compiler_params=_params(),
    )


def _head_mean(x, ones_f):
    return _dot_sel(x, ones_f) * (1.0 / HEAD)


def _merge_fwd(x, proj, o_h, o_b, o_c, gh, w_out, g_post, name, target=None):
    n, d = x.shape
    tm = min(512, n)

    def body(*refs):
        x_ref, ga_ref, gc_ref, oh_ref, ob_ref, oc_ref, gh_ref, w_ref, gp_ref = refs[:9]
        if target is None:
            xo_ref, mixt_ref, y_ref = refs[9:]
        else:
            t_ref, dx_ref, sq_ref, mixt_ref, y_ref = refs[9:]
        oh = oh_ref[...]
        ones_f = _block_ones(HGRN_W, F32)
        na = oh * lax.rsqrt(_head_mean(oh * oh, ones_f) + NORM_EPS) * gh_ref[...]
        ga, gc = ga_ref[...], gc_ref[...]
        mixed = jnp.concatenate([na * (ga * _sigmoid(ga)), ob_ref[...], oc_ref[...] * (gc * _sigmoid(gc))], axis=1)
        mixt_ref[...] = mixed.T.astype(BF16)
        y = _dot(mixed.astype(BF16), w_ref[...])
        y_ref[...] = y
        xn = x_ref[...] + y * lax.rsqrt(jnp.mean(y * y, axis=-1, keepdims=True) + NORM_EPS) * gp_ref[...]
        if target is None:
            xo_ref[...] = xn
        else:
            @pl.when(pl.program_id(0) == 0)
            def _():
                sq_ref[...] = jnp.zeros_like(sq_ref)

            e = xn - t_ref[...]
            dx_ref[...] = e * (1.0 / d)
            sq_ref[...] += jnp.sum(e * e, axis=0, keepdims=True)

    def row(wd, j=0):
        return pl.BlockSpec((tm, wd), lambda i: (i, j))

    def full(a, b):
        return pl.BlockSpec((a, b), lambda i: (0, 0))

    head = [] if target is None else [target]
    res = _pc(
        body, name, grid=(n // tm,),
        in_specs=[row(d), row(HGRN_W, C_GA // HGRN_W), row(FOX_W, C_GC // FOX_W), row(HGRN_W), row(POOL_W), row(FOX_W),
                  full(1, HGRN_W), full(d, d), full(1, d)] + [row(d)] * len(head),
        out_specs=[row(d)] + [full(1, d)] * len(head) + [pl.BlockSpec((d, tm), lambda i: (0, i)), row(d)],
        out_shape=[_sds((n, d), F32)] + [_sds((1, d), F32)] * len(head) + [_sds((d, n), BF16), _sds((n, d), F32)],
        compiler_params=_params(),
    )(x, proj, proj, o_h, o_b, o_c, gh, w_out, g_post, *head)
    return (res[0], res[1], res[2]) if target is None else ((res[0], res[1]), res[2], res[3])


def _rms_bwd(dy_scaled, xhat, r):
    return r * (dy_scaled - xhat * jnp.mean(dy_scaled * xhat, axis=-1, keepdims=True))


def _merge_bwd(dxo, y, g_post, w_out_t, proj, o_c, name):
    n, d = y.shape
    tm = min(512, n)
    wab = HGRN_W + POOL_W

    def body(dx_ref, y_ref, gp_ref, wt_ref, gc_ref, oc_ref, dy_ref, dm_ref, dgp_ref, da_ref, dg_ref, dl_ref):
        @pl.when(pl.program_id(0) == 0)
        def _():
            dgp_ref[...] = jnp.zeros_like(dgp_ref)

        yv, dxv = y_ref[...], dx_ref[...]
        r = lax.rsqrt(jnp.mean(yv * yv, axis=-1, keepdims=True) + NORM_EPS)
        yh = yv * r
        dgp_ref[...] += jnp.sum(dxv * yh, axis=0, keepdims=True)
        dyb = _rms_bwd(dxv * gp_ref[...], yh, r).astype(BF16)
        dy_ref[...] = dyb
        dm_ref[...] = _dot(dyb, wt_ref[:, :wab])
        dmc = _dot(dyb, wt_ref[:, wab:])
        gc, oc = gc_ref[...], oc_ref[...]
        sg = _sigmoid(gc)
        da = dmc * (gc * sg)
        da_ref[...] = da.astype(BF16)
        dg_ref[...] = (dmc * oc * (sg * (1.0 + gc * (1.0 - sg)))).astype(BF16)
        rr = lax.broadcasted_iota(jnp.int32, (FOX_W, FC_PAD), 0) // HEAD
        cc = lax.broadcasted_iota(jnp.int32, (FOX_W, FC_PAD), 1)
        dl_ref[...] = _dot_sel(da * oc, (rr == cc).astype(F32))

    def row(wd, j=0):
        return pl.BlockSpec((tm, wd), lambda i: (i, j))

    def full(a, b):
        return pl.BlockSpec((a, b), lambda i: (0, 0))

    return _pc(
        body, name, grid=(n // tm,),
        in_specs=[row(d), row(d), full(1, d), full(d, d), row(FOX_W, C_GC // FOX_W), row(FOX_W)],
        out_specs=[row(d), row(wab), full(1, d), row(FOX_W), row(FOX_W), row(FC_PAD)],
        out_shape=[_sds((n, d), BF16), _sds((n, wab), F32), _sds((1, d), F32), _sds((n, FOX_W), BF16),
                   _sds((n, FOX_W), BF16), _sds((n, FC_PAD), F32)],
        compiler_params=_params(),
    )(dxo, y, g_post, w_out_t, proj, o_c)


def _w_out_grad(mixt, dy, name):
    d, n = mixt.shape
    rows = d // N_DEV

    def body(a_ref, b_ref, o_ref):
        o_ref[...] = _dot(a_ref[...], b_ref[...]).astype(BF16)

    return _pc(
        body, name, grid=(N_DEV,),
        in_specs=[pl.BlockSpec((rows, n), lambda j: (j, 0)), pl.BlockSpec((n, d), lambda j: (0, 0))],
        out_specs=pl.BlockSpec((None, None, rows, d), lambda j: (j % 2, j // 2, 0, 0)),
        out_shape=_sds((2, N_DEV // 2, rows, d), BF16),
        compiler_params=_params(),
    )(mixt, dy)


def _w_in_grad(ht, pieces, name):
    d, n = ht.shape
    ta, tk = min(512, d), min(512, n)
    nk = n // tk
    arrays = [p for p, _ in pieces]
    widths = [p.shape[1] for p in arrays]
    offs = [sum(widths[:i]) for i in range(len(widths))]
    in_w = MAIN_W + FOX_HEADS
    shard = in_w // N_DEV

    def body(*refs):
        a_ref, p_refs = refs[0], refs[1:1 + len(arrays)]
        o_ref, acc = refs[1 + len(arrays):]
        k = pl.program_id(1)

        @pl.when(k == 0)
        def _():
            acc[...] = jnp.zeros_like(acc)

        a = a_ref[...]
        for pr, off, wd in zip(p_refs, offs, widths):
            for j in range(0, wd, 512):
                jw = min(512, wd - j)
                acc[:, off + j:off + j + jw] += _dot(a, pr[:, j:j + jw])

        @pl.when(k == nk - 1)
        def _():
            for j in range(N_DEV):
                o_ref[j % 2, j // 2] = acc[:, j * shard:(j + 1) * shard].astype(BF16)

    return _pc(
        body, name, grid=(d // ta, nk),
        in_specs=[pl.BlockSpec((ta, tk), lambda i, k: (i, k))] + [pl.BlockSpec((tk, wd), lambda i, k: (k, 0)) for wd in widths],
        out_specs=pl.BlockSpec((2, N_DEV // 2, ta, shard), lambda i, k: (0, 0, i, 0)),
        out_shape=_sds((2, N_DEV // 2, d, shard), BF16),
        scratch_shapes=[pltpu.VMEM((ta, sum(widths)), F32)],
        compiler_params=_params(),
    )(ht, *arrays)


def _hgrn_state_bwd(qs, k, v, b, do, s0, ds1, bmask):
    bl = b[SUB - 1:SUB, :]
    eb, ebl, ekt = jnp.exp(b), jnp.exp(bl), jnp.exp(bl - b)
    qe, ktil = qs * eb, k * ekt
    ds1b, dob = ds1.astype(BF16), do.astype(BF16)
    dv = _dot_nt(ktil.astype(BF16), ds1b)
    dqe = _dot(dob, s0.astype(BF16))
    dktil = _dot(v.astype(BF16), ds1b)
    dbl = jnp.sum(dktil * ktil, axis=0, keepdims=True) + ebl * jnp.sum(s0 * ds1, axis=0, keepdims=True)
    ds0 = ds1 * ebl + _dot_tn(dob, qe.astype(BF16)) * bmask
    return dqe * eb, dktil * ekt, dv, dbl, ds0


def _hgrn_intra_bwd(qs, k, v, do, es, aexp, gexp, dq, dk, dv, put_dq_row):
    dks = [dk[j:j + 8] for j in range(0, SUB, 8)]
    dvs = [dv[j:j + 8] for j in range(0, SUB, 8)]
    for t in range(SUB):
        r = _live_rows(t)
        ge = gexp[t * SUB:t * SUB + r, :] * es[t]
        put_dq_row(t, dq[t:t + 1, :] + jnp.sum(ge * k[:r], axis=0, keepdims=True))
        for j in range(r // 8):
            dks[j] = dks[j] + ge[8 * j:8 * j + 8] * qs[t:t + 1, :]
            dvs[j] = dvs[j] + aexp[t * SUB + 8 * j:t * SUB + 8 * j + 8, :] * do[t:t + 1, :]
    return jnp.concatenate(dks, axis=0), jnp.concatenate(dvs, axis=0)


def _hgrn_bwd(dmix, proj, o_h, s0, gh, lb, ones_b, nb, name):
    n = proj.shape[0]
    t = n // nb
    tt = _hgrn_tile(t)
    nt = t // tt
    ncs = tt // CHUNK
    nsub = CHUNK // SUB
    w = HGRN_W

    def body(dm_ref, q_ref, z_ref, v_ref, ga_ref, oh_ref, s0_ref, gh_ref, lb_ref, ones_ref,
             dp_ref, dgh_ref, dlb_ref, ds_s, ss_s, b_s, qs_s, k_s, do_s, dq_s, dk_s, dv_s, dbl_s):
        first = jnp.logical_and(pl.program_id(0) == 0, pl.program_id(1) == 0)

        @pl.when(first)
        def _():
            dgh_ref[...] = jnp.zeros_like(dgh_ref)
            dlb_ref[...] = jnp.zeros_like(dlb_ref)

        @pl.when(pl.program_id(1) == 0)
        def _():
            ds_s[...] = jnp.zeros_like(ds_s)

        ones_b = ones_ref[...]
        ones_f = ones_b.astype(F32)
        bmask = _block_ones(w, F32)
        lbv, ghv = lb_ref[...], gh_ref[...]
        oh, ga, dm = oh_ref[...], ga_ref[...], dm_ref[...]
        rn = lax.rsqrt(_head_mean(oh * oh, ones_f) + NORM_EPS)
        nh = oh * rn
        sga = _sigmoid(ga)
        dp_ref[:, 3 * w:4 * w] = (dm * nh * ghv * (sga * (1.0 + ga * (1.0 - sga)))).astype(BF16)
        dn = dm * (ga * sga)
        dgh_ref[...] += jnp.sum(dn * nh, axis=0, keepdims=True)
        dn = dn * ghv
        do_s[...] = rn * (dn - nh * _head_mean(dn * nh, ones_f))
        q = q_ref[...]
        sig, sn, f, g, k, sq = _hgrn_gates(q, z_ref[...], lbv)
        qs = q * sq
        b_s[...] = _sel_dot(_sub_tri(tt, True), g)
        qs_s[...] = qs
        k_s[...] = k

        def chunk(cc, carry):
            c = ncs - 1 - cc
            base = pl.multiple_of(c * CHUNK, CHUNK)
            tiles = []
            for u in range(nsub):
                rows = pl.ds(base + u * SUB, SUB)
                tiles.append((qs_s[rows, :], k_s[rows, :], v_ref[rows, :], b_s[rows, :], do_s[rows, :]))
            st = s0_ref[c]
            for u, (qs, k, v, b, do) in enumerate(tiles):
                ss_s[u] = st
                if u < nsub - 1:
                    st = _hgrn_state_step(st, k, v, b, bmask)
            ds = ds_s[...]
            for u in reversed(range(nsub)):
                qs, k, v, b, do = tiles[u]
                _, es, ws = _hgrn_decays(qs, k, b)
                gs = [_pad_rows(do[t:t + 1, :] * v[:_live_rows(t)]) for t in range(SUB)]
                aexp = _dot(jnp.concatenate(ws, axis=0).astype(BF16), ones_b)
                gexp = _dot(jnp.concatenate(gs, axis=0).astype(BF16), ones_b)
                dq, dk, dv, dbl, ds = _hgrn_state_bwd(qs, k, v, b, do, ss_s[u], ds, bmask)

                def put_dq_row(i, row, r0=base + u * SUB):
                    dq_s[pl.ds(r0 + i, 1), :] = row

                dk, dv = _hgrn_intra_bwd(qs, k, v, do, es, aexp, gexp, dq, dk, dv, put_dq_row)
                dk_s[pl.ds(base + u * SUB, SUB), :] = dk
                dv_s[pl.ds(base + u * SUB, SUB), :] = dv
                dbl_s[pl.ds(base + u * SUB, SUB), :] = jnp.broadcast_to(dbl, (SUB, w))
            ds_s[...] = ds
            return carry

        lax.fori_loop(0, ncs, chunk, 0)
        dqs, dk = dq_s[...], dk_s[...]
        dg = _sel_dot(_sub_tri(tt, False), qs * dqs - k * dk) + dbl_s[...]
        dfz = jnp.where(f > TINY, dg / jnp.maximum(f, TINY), 0.0)
        dlb_ref[...] += jnp.sum(dfz * (1.0 - sig) - dk * sn, axis=0, keepdims=True)
        dp_ref[:, 0:w] = (dqs * (sq * (1.0 + q * (1.0 - sq)))).astype(BF16)
        dp_ref[:, w:2 * w] = ((dfz - dk) * (1.0 - lbv) * sig * sn).astype(BF16)
        dp_ref[:, 2 * w:3 * w] = dv_s[...].astype(BF16)

    def rv(b, i):
        return b * nt + (nt - 1 - i)

    def col(j):
        return pl.BlockSpec((tt, w), lambda b, i: (rv(b, i), j))

    def full(a, bb):
        return pl.BlockSpec((a, bb), lambda b, i: (0, 0))

    return _pc(
        body, name, grid=(nb, nt),
        in_specs=[col(0), col(C_QA // w), col(C_FA // w), col(C_IA // w), col(C_GA // w), col(0),
                  pl.BlockSpec((ncs, w, w), lambda b, i: (rv(b, i), 0, 0)), full(1, w), full(1, w), full(w, w)],
        out_specs=[pl.BlockSpec((tt, 4 * w), lambda b, i: (rv(b, i), 0)), full(1, w), full(1, w)],
        out_shape=[_sds((n, 4 * w), BF16), _sds((1, w), F32), _sds((1, w), F32)],
        scratch_shapes=[pltpu.VMEM((w, w), F32), pltpu.VMEM((nsub, w, w), F32)] + [pltpu.VMEM((tt, w), F32)] * 8,
        compiler_params=_params(),
    )(dmix, proj, proj, proj, proj, o_h, s0, gh, lb, ones_b)


def _pool_bwd(dmix, proj, wbd, wbd_t, scale, nb, name):
    n = proj.shape[0]
    t = n // nb
    tt = min(256, t)
    nt = t // tt
    nhb = tt // POOL_HALO
    cu, cg, cm = C_UB // POOL_W, C_GB // POOL_W, HGRN_W // POOL_W

    def body(u_ref, g_ref, h_ref, dm_ref, gn_ref, dmn_ref, w_ref, wt_ref, s_ref, dp_ref, dw_ref, ds_ref):
        i = pl.program_id(1)
        first = jnp.logical_and(pl.program_id(0) == 0, i == 0)

        @pl.when(first)
        def _():
            dw_ref[...] = jnp.zeros_like(dw_ref)
            ds_ref[...] = jnp.zeros_like(ds_ref)

        sc = s_ref[...]
        halo = jnp.where(i == 0, 0.0, h_ref[...])
        pooled, cnt = _pool_mix(u_ref[...], halo, i * tt, tt)
        pb = pooled.astype(BF16)
        pre = _dot(pb, w_ref[...])
        gv, dm = g_ref[...], dm_ref[...]
        sg = _sigmoid(gv)
        silu = gv * sg
        dgb = dm * pre * sc * (sg * (1.0 + gv * (1.0 - sg)))
        ds_ref[...] += jnp.sum(dm * pre * silu, axis=0, keepdims=True)
        dpre = (dm * sc * silu).astype(BF16)
        dw_ref[...] += _dot_tn(pb, dpre)
        dpool = _dot(dpre, wt_ref[...])
        gn = gn_ref[...]
        dpre_n = (dmn_ref[...] * sc * (gn * _sigmoid(gn))).astype(BF16)
        dpool_n = jnp.where(i == nt - 1, 0.0, _dot(dpre_n, wt_ref[...]))
        lane, wl = _pool_lane_windows()
        tpos_n = ((i + 1) * tt + lax.broadcasted_iota(jnp.int32, (POOL_HALO, POOL_W), 0)).astype(F32)
        ext = jnp.concatenate([dpool / cnt, dpool_n / jnp.minimum(tpos_n + 1.0, wl)], axis=0)
        rows = tt + POOL_HALO
        sums, cur, shift = [], ext, 1
        for _ in POOL_WINDOWS:
            cur = cur + pltpu.roll(cur, rows - shift, axis=0)
            sums.append(cur[:tt, :])
            shift *= 2
        du = _pool_select(lane, sums) - dpool
        dp_ref[...] = jnp.concatenate([du, dgb], axis=1).astype(BF16)

    def nxt(b, i):
        return jnp.minimum((b * nt + i + 1) * nhb, n // POOL_HALO - 1)

    return _pc(
        body, name, grid=(nb, nt),
        in_specs=_pool_specs(tt, nt, nhb) + [
            pl.BlockSpec((tt, POOL_W), lambda b, i: (b * nt + i, cm)),
            pl.BlockSpec((POOL_HALO, POOL_W), lambda b, i: (nxt(b, i), cg)),
            pl.BlockSpec((POOL_HALO, POOL_W), lambda b, i: (nxt(b, i), cm)),
            pl.BlockSpec((POOL_W, POOL_W), lambda b, i: (0, 0)), pl.BlockSpec((POOL_W, POOL_W), lambda b, i: (0, 0)),
            pl.BlockSpec((1, POOL_W), lambda b, i: (0, 0))],
        out_specs=[pl.BlockSpec((tt, 2 * POOL_W), lambda b, i: (b * nt + i, 0)),
                   pl.BlockSpec((POOL_W, POOL_W), lambda b, i: (0, 0)), pl.BlockSpec((1, POOL_W), lambda b, i: (0, 0))],
        out_shape=[_sds((n, 2 * POOL_W), BF16), _sds((POOL_W, POOL_W), F32), _sds((1, POOL_W), F32)],
        compiler_params=_params(),
    )(proj, proj, proj, dmix, proj, dmix, wbd, wbd_t, scale)


def _fox_bwd(proj, da, c_col, c_row, lse_row, delta_row, nb, name, rider=None):
    n = proj.shape[0]
    t = n // nb
    tb = min(256, t)
    nq = t // tb
    pw = 2 * HEAD

    def body(q_ref, k_ref, v_ref, da_ref, cc_ref, cr_ref, lse_ref, dl_ref,
             dq_ref, dk_ref, dv_ref, dck_ref, dcq_ref, dq_s, dk_s, dv_s, dck_s, dcq_s):
        step = pl.program_id(1)
        kj, qi = pairs(step)

        @pl.when(step == 0)
        def _():
            dq_s[...] = jnp.zeros_like(dq_s)
            dcq_s[...] = jnp.zeros_like(dcq_s)

        @pl.when(qi == nq - 1)
        def _():
            dk_s[...] = jnp.zeros_like(dk_s)
            dv_s[...] = jnp.zeros_like(dv_s)
            dck_s[...] = jnp.zeros_like(dck_s)

        def block(masked):
            lo = _lane_lo()
            if masked:
                causal = lax.broadcasted_iota(jnp.int32, (tb, tb), 1) >= lax.broadcasted_iota(jnp.int32, (tb, tb), 0)
            dck = dck_s[...]
            for p in range(FOX_HEADS // 2):
                sl = slice(p * pw, (p + 1) * pw)
                qp = q_ref[:, sl] * FOX_SCALE
                kf = k_ref[:, sl]
                kp = kf.astype(BF16)
                kt = kf.T.astype(BF16)
                vp = v_ref[:, sl].astype(BF16)
                dap = da_ref[:, sl]
                dk, dv = dk_s[:, sl], dv_s[:, sl]
                for h in range(2):
                    hh = 2 * p + h
                    lm = lo if h == 0 else jnp.logical_not(lo)
                    qm = jnp.where(lm, qp, 0.0).astype(BF16)
                    dam = jnp.where(lm, dap, jnp.zeros_like(dap))
                    s = _dot_nt(kp, qm) + (cr_ref[hh:hh + 1, :] - cc_ref[:, hh:hh + 1])
                    pe = jnp.exp(s - lse_ref[hh:hh + 1, :])
                    if masked:
                        pe = jnp.where(causal, pe, 0.0)
                    dp = _dot_nt(vp, dam)
                    ds = pe * (dp - dl_ref[hh:hh + 1, :])
                    dsb = ds.astype(BF16)
                    dv = dv + _dot(pe.astype(BF16), dam)
                    dk = dk + _dot(dsb, qm)
                    rows = slice(hh * HEAD, (hh + 1) * HEAD)
                    dq_s[qi, rows, :] += _dot(kt[h * HEAD:(h + 1) * HEAD, :], dsb)
                    dck = dck - _put_col(jnp.zeros_like(dck), hh, jnp.sum(ds, axis=1, keepdims=True))
                    dcq_s[qi, hh:hh + 1, :] += _rows_reduce(ds, jnp.add, jnp.sum)
                dk_s[:, sl] = dk
                dv_s[:, sl] = dv
            dck_s[...] = dck

        @pl.when(qi > kj)
        def _():
            block(False)

        @pl.when(qi == kj)
        def _():
            block(True)

        @pl.when(qi == kj)
        def _():
            dk_ref[...] = dk_s[...].astype(BF16)
            dv_ref[...] = dv_s[...].astype(BF16)
            dck_ref[...] = dck_s[...]

        @pl.when(step == nq * (nq + 1) // 2 - 1)
        def _():
            for j in range(nq):
                dq_ref[j * tb:(j + 1) * tb, :] = (dq_s[j].T * FOX_SCALE).astype(BF16)
                dcq_ref[:, j * tb:(j + 1) * tb] = dcq_s[j]

    def pairs(step):
        a, b = _tri_pair(step, nq)
        return nq - 1 - a, nq - 1 - b

    def kspec(wd, j=0):
        return pl.BlockSpec((tb, wd), lambda b, st: (b * nq + pairs(st)[0], j))

    def qspec(wd, j=0):
        return pl.BlockSpec((tb, wd), lambda b, st: (b * nq + pairs(st)[1], j))

    def qrow():
        return pl.BlockSpec((None, FOX_HEADS, tb), lambda b, st: (b, 0, pairs(st)[1]))

    return _call(
        body, name, (proj, proj, proj, da, c_col, c_row, lse_row, delta_row), rider, grid=(nb, nq * (nq + 1) // 2),
        in_specs=[qspec(FOX_W, C_QC // FOX_W), kspec(FOX_W, C_KC // FOX_W), kspec(FOX_W, C_VC // FOX_W), qspec(FOX_W),
                  kspec(FC_PAD), qrow(), qrow(), qrow()],
        out_specs=[pl.BlockSpec((t, FOX_W), lambda b, st: (b, 0)), kspec(FOX_W), kspec(FOX_W), kspec(FC_PAD),
                   pl.BlockSpec((None, FOX_HEADS, t), lambda b, st: (b, 0, 0))],
        out_shape=[_sds((n, FOX_W), BF16), _sds((n, FOX_W), BF16), _sds((n, FOX_W), BF16), _sds((n, FC_PAD), F32),
                   _sds((nb, FOX_HEADS, t), F32)],
        scratch_shapes=[pltpu.VMEM((nq, FOX_W, tb), F32), pltpu.VMEM((tb, FOX_W), F32), pltpu.VMEM((tb, FOX_W), F32),
                        pltpu.VMEM((tb, FC_PAD), F32), pltpu.VMEM((nq, FOX_HEADS, tb), F32)],
        compiler_params=_params(),
    )


def _fox_decay_bwd(dc_q, dc_k, fc, bias, nb, name):
    n = fc.shape[0]
    t = n // nb
    tt = min(256, t)
    nt = t // tt

    def body(dcq_ref, dck_ref, fc_ref, b_ref, dfc_ref, db_ref, carry):
        i = pl.program_id(1)
        first = jnp.logical_and(pl.program_id(0) == 0, i == 0)

        @pl.when(first)
        def _():
            db_ref[...] = jnp.zeros_like(db_ref)

        @pl.when(i == 0)
        def _():
            carry[...] = jnp.zeros_like(carry)

        r = lax.broadcasted_iota(jnp.int32, (tt, tt), 0)
        cc = lax.broadcasted_iota(jnp.int32, (tt, tt), 1)
        dlf = _dot_hi((r <= cc).astype(F32), dcq_ref[...] + dck_ref[...]) + carry[...]
        carry[...] = dlf[0:1, :]
        dfc = dlf * _sigmoid(-(fc_ref[...] + b_ref[...]))
        dfc_ref[...] = dfc.astype(BF16)
        db_ref[...] += jnp.sum(dfc, axis=0, keepdims=True)

    def row():
        return pl.BlockSpec((tt, FC_PAD), lambda b, i: (b * nt + (nt - 1 - i), 0))

    return _pc(
        body, name, grid=(nb, nt),
        in_specs=[row(), row(), row(), pl.BlockSpec((1, FC_PAD), lambda b, i: (0, 0))],
        out_specs=[row(), pl.BlockSpec((1, FC_PAD), lambda b, i: (0, 0))],
        out_shape=[_sds((n, FC_PAD), BF16), _sds((1, FC_PAD), F32)],
        scratch_shapes=[pltpu.VMEM((1, FC_PAD), F32)],
        compiler_params=_params(),
    )(dc_q, dc_k, fc, bias)


def _in_proj_bwd(pieces, w_main_t, w_fc_t, x, g_pre, dxo, name, rider=None):
    n, d = x.shape
    tm = min(512, n)
    widths = [p.shape[1] for p, _ in pieces]
    offs = [o for _, o in pieces]
    np_ = len(pieces)

    def body(*refs):
        p_refs = refs[:np_]
        wt_ref, wf_ref, x_ref, g_ref, dxo_ref, dx_ref, dg_ref = refs[np_:]

        @pl.when(pl.program_id(0) == 0)
        def _():
            dg_ref[...] = jnp.zeros_like(dg_ref)

        dh = _dot(p_refs[-1][...], wf_ref[...])
        for pr, wd, off in zip(p_refs[:-1], widths[:-1], offs[:-1]):
            for j in range(0, wd, 512):
                jw = min(512, wd - j)
                dh = dh + _dot(pr[:, j:j + jw], wt_ref[off + j:off + j + jw, :])
        xv = x_ref[...]
        r = lax.rsqrt(jnp.mean(xv * xv, axis=-1, keepdims=True) + NORM_EPS)
        xh = xv * r
        dg_ref[...] += jnp.sum(dh * xh, axis=0, keepdims=True)
        dx_ref[...] = dxo_ref[...] + _rms_bwd(dh * g_ref[...], xh, r)

    row = pl.BlockSpec((tm, d), lambda i: (i, 0))
    return _call(
        body, name, (*[p for p, _ in pieces], w_main_t, w_fc_t, x, g_pre, dxo), rider, grid=(n // tm,),
        in_specs=[pl.BlockSpec((tm, wd), lambda i: (i, 0)) for wd in widths] + [
            pl.BlockSpec((MAIN_W, d), lambda i: (0, 0)), pl.BlockSpec((FC_PAD, d), lambda i: (0, 0)),
            row, pl.BlockSpec((1, d), lambda i: (0, 0)), row],
        out_specs=[row, pl.BlockSpec((1, d), lambda i: (0, 0))],
        out_shape=[_sds((n, d), F32), _sds((1, d), F32)],
        compiler_params=_params(),
    )


def _lower_bound_table(lower_bounds, name):
    depth, w = lower_bounds.shape

    def body(lb_ref, o_ref):
        v = lb_ref[...]
        e = jnp.exp(v - jnp.max(v, axis=0, keepdims=True))
        p = e / jnp.sum(e, axis=0, keepdims=True)
        acc = jnp.zeros((1, w), F32)
        for l in range(depth):
            acc = acc + p[l:l + 1, :]
            o_ref[l:l + 1, :] = acc - p[0:1, :]

    return _pc(body, name, out_shape=_sds((depth, w), F32))(lower_bounds)


def _lower_bound_bwd(lower_bounds, dlbs, name):
    depth, w = lower_bounds.shape

    def body(lb_ref, d_ref, o_ref):
        v, dl = lb_ref[...], d_ref[...]
        e = jnp.exp(v - jnp.max(v, axis=0, keepdims=True))
        p = e / jnp.sum(e, axis=0, keepdims=True)
        tot = jnp.sum(dl, axis=0, keepdims=True)
        rows, tail = [], tot
        for l in range(depth):
            rows.append(tail - tot if l == 0 else tail)
            tail = tail - dl[l:l + 1, :]
        dp = jnp.concatenate(rows, axis=0)
        o_ref[...] = p * (dp - jnp.sum(p * dp, axis=0, keepdims=True))

    return _pc(body, name, out_shape=_sds((depth, w), F32))(lower_bounds, dlbs)


def _place():
    x, y, c = lax.axis_index("x"), lax.axis_index("y"), lax.axis_index("c")
    return x, y, c


def _gather_weights(*arrays):
    na = len(arrays)

    def body(*refs):
        ins, outs = refs[:na], refs[na:2 * na]
        send_sems, recv_sems, local_sems = refs[2 * na:]
        x, y, c = _place()
        me, sibling = (x, y, c), (x, y, 1 - c)
        chips = [(1 - x, y), (x, 1 - y), (1 - x, 1 - y)]

        def slot(a, px, py, pc):
            return outs[a].at[4 * px + 2 * py + pc]

        def copy(a, k, block, to, own=False):
            return pltpu.make_async_remote_copy(
                src_ref=ins[a] if own else slot(a, *block), dst_ref=slot(a, *block),
                send_sem=send_sems.at[a * 7 + k], recv_sem=recv_sems.at[a * 7 + k],
                device_id=to, device_id_type=MESH)

        mine = [pltpu.make_async_copy(ins[a], slot(a, *me), local_sems.at[a]) for a in range(na)]
        for cp in mine:
            cp.start()
        first = []
        for a in range(na):
            first.append(copy(a, 0, me, sibling, own=True))
            first += [copy(a, 1 + j, me, (*chip, c), own=True) for j, chip in enumerate(chips)]
        for cp in first:
            cp.start()
        passed = []
        for j, chip in enumerate(chips):
            for a in range(na):
                copy(a, 1 + j, (*chip, c), me).wait_recv()
                fw = copy(a, 4 + j, (*chip, c), sibling)
                fw.start()
                passed.append(fw)
        for a in range(na):
            copy(a, 0, sibling, me).wait_recv()
            for j, chip in enumerate(chips):
                copy(a, 4 + j, (*chip, 1 - c), me).wait_recv()
        for cp in first + passed:
            cp.wait_send()
        for cp in mine:
            cp.wait()

    any_spec = pl.BlockSpec(memory_space=pl.ANY)
    return _pc(
        body, "gather_weights",
        in_specs=[any_spec] * na, out_specs=[any_spec] * na,
        out_shape=[_sds((N_DEV,) + a.shape, a.dtype) for a in arrays],
        scratch_shapes=[pltpu.SemaphoreType.DMA((7 * na,)), pltpu.SemaphoreType.DMA((7 * na,)),
                        pltpu.SemaphoreType.DMA((na,))],
    )(*arrays)


def _peer(k):
    x, y, c = _place()
    return (1 - x if k & 4 else x, 1 - y if k & 2 else y, 1 - c if k & 1 else c)


def _remote(src, dst, sems, s, to):
    return pltpu.make_async_remote_copy(src_ref=src, dst_ref=dst, send_sem=sems[0].at[s], recv_sem=sems[1].at[s],
                                        device_id=to, device_id_type=MESH)


def _gather_rider(shards):
    na = len(shards)

    def plan(ins, outs, *sems):
        x, y, c = _place()
        me = 4 * x + 2 * y + c
        locs = [pltpu.make_async_copy(ins[a], outs[a].at[me], sems[2].at[a]) for a in range(na)]
        sends, recvs = [], []
        for k in range(1, N_DEV):
            px, py, pc = _peer(k)
            for a in range(na):
                s = (k - 1) * na + a
                sends.append(_remote(ins[a], outs[a].at[me], sems, s, (px, py, pc)))
                recvs.append(_remote(ins[a], outs[a].at[4 * px + 2 * py + pc], sems, s, (px, py, pc)))
        return sends, recvs, locs

    return _Rider(shards, [_sds((N_DEV,) + a.shape, a.dtype) for a in shards], (N_DEV - 1) * na, na, plan)


def _direct_exchange_rider(blocks):
    na = len(blocks)

    def plan(ins, outs, *sems):
        x, y, c = _place()
        me = 4 * x + 2 * y + c
        locs = [pltpu.make_async_copy(ins[a].at[c, 2 * x + y], outs[a].at[me], sems[2].at[a]) for a in range(na)]
        sends, recvs = [], []
        for k in range(1, N_DEV):
            px, py, pc = _peer(k)
            for a in range(na):
                s = (k - 1) * na + a
                sends.append(_remote(ins[a].at[pc, 2 * px + py], outs[a].at[me], sems, s, (px, py, pc)))
                recvs.append(_remote(ins[a].at[pc, 2 * px + py], outs[a].at[4 * px + 2 * py + pc], sems, s, (px, py, pc)))
        return sends, recvs, locs

    return _Rider(blocks, [_sds((N_DEV,) + a.shape[2:], a.dtype) for a in blocks], (N_DEV - 1) * na, na, plan)


def _swap_rider(halves):
    na = len(halves)

    def plan(ins, outs, *sems):
        x, y, c = _place()
        cps = [_remote(ins[a].at[1 - c], outs[a], sems, a, (x, y, 1 - c)) for a in range(na)]
        return cps, cps, []

    return _Rider(halves, [_sds(a.shape[1:], a.dtype) for a in halves], na, 1, plan)


def _chip_exchange_rider(parts, small=None):
    na = len(parts)
    n_chip = N_DEV // 2

    def plan(ins, outs, *sems):
        x, y, c = _place()
        chip = 2 * x + y
        locs = [pltpu.make_async_copy(ins[a].at[chip], outs[a].at[chip], sems[2].at[a]) for a in range(na)]
        sends, recvs = [], []
        for k in range(1, n_chip):
            px, py, _ = _peer(2 * k)
            for a in range(na):
                s = (k - 1) * na + a
                sends.append(_remote(ins[a].at[2 * px + py], outs[a].at[chip], sems, s, (px, py, c)))
                recvs.append(_remote(ins[a].at[2 * px + py], outs[a].at[2 * px + py], sems, s, (px, py, c)))
        if small is not None:
            me = 2 * chip + c
            locs.append(pltpu.make_async_copy(ins[na], outs[na].at[me], sems[2].at[na]))
            for k in range(1, N_DEV):
                px, py, pc = _peer(k)
                s = (n_chip - 1) * na + k - 1
                sends.append(_remote(ins[na], outs[na].at[me], sems, s, (px, py, pc)))
                recvs.append(_remote(ins[na], outs[na].at[4 * px + 2 * py + pc], sems, s, (px, py, pc)))
        return sends, recvs, locs

    extra = [] if small is None else [small]
    shapes = [_sds(a.shape, a.dtype) for a in parts] + [_sds((N_DEV,) + s.shape, s.dtype) for s in extra]
    n_sems = (n_chip - 1) * na + (N_DEV - 1) * len(extra)
    return _Rider(list(parts) + extra, shapes, n_sems, na + len(extra), plan)


def _pair_add(halves, other, core, name):
    _, nch, r, c = halves.shape
    tr = 256 if r % 256 == 0 else r

    def body(h_ref, o_ref, c_ref, p_ref):
        mine = jnp.where(c_ref[...] == 0, h_ref[0].astype(F32), h_ref[1].astype(F32))
        p_ref[...] = (mine + o_ref[...].astype(F32)).astype(BF16)

    blk = pl.BlockSpec((None, tr, c), lambda j, i: (j, i, 0))
    return _pc(
        body, name, grid=(nch, r // tr),
        in_specs=[pl.BlockSpec((2, None, tr, c), lambda j, i: (0, j, i, 0)), blk, pl.BlockSpec((1, 1), lambda j, i: (0, 0))],
        out_specs=blk, out_shape=_sds((nch, r, c), BF16),
        compiler_params=_params(),
    )(halves, other, core)


def _sum_adamw(parts, w, m, v, name, rider=None):
    nl, r, c = w.shape
    tr = 256 if r % 256 == 0 else r

    def body(*refs):
        p_refs = refs[:nl]
        w_ref, m_ref, v_ref, g_ref, d_ref, mo_ref, vo_ref = refs[nl:]
        for l in range(nl):
            @pl.when(pl.program_id(0) == l)
            def _(p_ref=p_refs[l]):
                g = p_ref[0].astype(F32)
                for j in range(1, p_ref.shape[0]):
                    g = g + p_ref[j].astype(F32)
                mn = ADAM_B1 * m_ref[...] + (1.0 - ADAM_B1) * g
                vn = ADAM_B2 * v_ref[...] + (1.0 - ADAM_B2) * (g * g)
                m_hat = mn / (1.0 - ADAM_B1 ** ADAM_STEP)
                v_hat = vn / (1.0 - ADAM_B2 ** ADAM_STEP)
                g_ref[...] = g
                d_ref[...] = -ADAM_LR * (m_hat / (jnp.sqrt(v_hat) + ADAM_EPS) + ADAM_WD * w_ref[...])
                mo_ref[...] = mn
                vo_ref[...] = vn

    def part_spec(l, k):
        return pl.BlockSpec((k, tr, c), lambda li, i: (0, jnp.where(li == l, i, 0), 0))

    row = pl.BlockSpec((None, tr, c), lambda li, i: (li, i, 0))
    return _call(
        body, name, (*parts, w, m, v), rider, grid=(nl, r // tr),
        in_specs=[part_spec(l, p.shape[0]) for l, p in enumerate(parts)] + [row, row, row],
        out_specs=[row] * 4,
        out_shape=[_sds((nl, r, c), F32)] * 4,
        compiler_params=_params(),
    )


SMALL = ("lower_bounds", "pre_norm_g", "hgrn_norm_g", "fox_f_bias", "pool_w", "pool_scale", "post_norm_g")
SMALL_LANES = 128


def _small_size(tree):
    return sum(tree[k].size for k in SMALL)


def _pack_small(tree, extra=None):
    flat = jnp.concatenate([tree[k].reshape(-1) for k in SMALL] + ([] if extra is None else [extra.reshape(1)]))
    rows = -(-(_small_size(tree) + 1) // (8 * SMALL_LANES)) * 8
    return jnp.pad(flat, (0, rows * SMALL_LANES - flat.shape[0])).reshape(rows, SMALL_LANES)


def _unpack_small(packed, like):
    flat, out, off = packed.reshape(-1), {}, 0
    for k in SMALL:
        size = like[k].size
        out[k] = flat[off:off + size].reshape(like[k].shape)
        off += size
    return out


def _block_diag(pw):
    g = pw.shape[0]
    eye = jnp.eye(g, dtype=pw.dtype)
    return (eye[:, None, :, None] * pw[:, :, None, :]).reshape(g * HEAD, g * HEAD)


def _assemble_w_in(g_in, name):
    _, d, shard = g_in.shape
    tr = min(256, d)
    wide = MAIN_W + FC_PAD

    def body(g_ref, wm_ref, wf_ref, wmt_ref, wft_ref, row_s):
        row_s[:, MAIN_W:] = jnp.zeros((tr, FC_PAD), F32)
        for j in range(N_DEV):
            row_s[:, j * shard:(j + 1) * shard] = g_ref[j].astype(F32)
        wm_ref[...] = row_s[:, :MAIN_W].astype(BF16)
        wf_ref[...] = row_s[:, MAIN_W:].astype(BF16)
        for j in range(0, MAIN_W, 512):
            wmt_ref[j:j + 512, :] = row_s[:, j:j + 512].T.astype(BF16)
        wft_ref[...] = row_s[:, MAIN_W:].T.astype(BF16)

    return _pc(
        body, name, grid=(d // tr,),
        in_specs=[pl.BlockSpec((N_DEV, tr, shard), lambda i: (0, i, 0))],
        out_specs=[pl.BlockSpec((tr, MAIN_W), lambda i: (i, 0)), pl.BlockSpec((tr, FC_PAD), lambda i: (i, 0)),
                   pl.BlockSpec((MAIN_W, tr), lambda i: (0, i)), pl.BlockSpec((FC_PAD, tr), lambda i: (0, i))],
        out_shape=[_sds((d, MAIN_W), BF16), _sds((d, FC_PAD), BF16), _sds((MAIN_W, d), BF16), _sds((FC_PAD, d), BF16)],
        scratch_shapes=[pltpu.VMEM((tr, wide), F32)],
        compiler_params=_params(),
    )(g_in)


def _w_out_parts(g_out):
    full_out = g_out.reshape(N_DEV * g_out.shape[1], g_out.shape[2])
    return full_out, full_out.T


def _layer_fwd(l, x, lbs, weights, lw, nb, rider=None, target=None):
    n = x.shape[0]
    t = n // nb
    w_main, w_fc, _, _, w_out, _ = lw
    bias = jnp.pad(weights["fox_f_bias"][l:l + 1], ((0, 0), (0, FC_PAD - FOX_HEADS)))
    wbd = _block_diag(weights["pool_w"][l]).astype(BF16)
    proj, fc, ht, c_col = _in_proj_fwd(x, weights["pre_norm_g"][l:l + 1], w_main, w_fc, bias, nb, f"in_proj_fwd_{l}")
    c_row = c_col.reshape(nb, t, FC_PAD)[:, :, :FOX_HEADS].transpose(0, 2, 1)
    o_h, s0 = _hgrn_fwd(proj, lbs[l:l + 1], _block_ones(HGRN_W, BF16), nb, f"hgrn_fwd_{l}")
    o_b = _pool_fwd(proj, wbd, weights["pool_scale"][l:l + 1], nb, f"pool_fwd_{l}")
    (o_c, lse), rode = _fox_fwd(proj, c_col, c_row, nb, f"fox_fwd_{l}", rider)
    if w_out is None:
        lw = tuple(lw[:4]) + _w_out_parts(rode[0])
        w_out, rode = lw[4], rode[1:]
    x_next, mixt, y = _merge_fwd(x, proj, o_h, o_b, o_c, weights["hgrn_norm_g"][l:l + 1], w_out,
                                 weights["post_norm_g"][l:l + 1], f"merge_fwd_{l}", target)
    return x_next, (x, proj, fc, ht, c_col, c_row, o_h, s0, o_c, lse, mixt, y, bias, wbd), lw, rode


def _layer_bwd(l, dx, saved, lbs, weights, lw, nb, rider=None):
    _, proj, fc, ht, c_col, c_row, o_h, s0, o_c, lse, mixt, y, bias, wbd = saved
    n = proj.shape[0]
    t = n // nb
    w_out_t = lw[5]
    g = {}
    dy, dmix, dgp, da, d_gc, delta = _merge_bwd(dx, y, weights["post_norm_g"][l:l + 1], w_out_t, proj, o_c,
                                                f"merge_bwd_{l}")
    g["post_norm_g"] = dgp[0]
    g["w_out"] = _w_out_grad(mixt, dy, f"w_out_grad_{l}")
    d_a, dgh, dlb = _hgrn_bwd(dmix, proj, o_h, s0, weights["hgrn_norm_g"][l:l + 1], lbs[l:l + 1],
                              _block_ones(HGRN_W, BF16), nb, f"hgrn_bwd_{l}")
    g["hgrn_norm_g"], g["lbs"] = dgh[0], dlb[0]
    d_b, dwbd, dps = _pool_bwd(dmix, proj, wbd, wbd.T, weights["pool_scale"][l:l + 1], nb, f"pool_bwd_{l}")
    g["pool_w"] = jnp.stack([dwbd[j * HEAD:(j + 1) * HEAD, j * HEAD:(j + 1) * HEAD] for j in range(len(POOL_WINDOWS))])
    g["pool_scale"] = dps[0]
    delta_row, lse_row = [a.reshape(nb, t, FC_PAD)[:, :, :FOX_HEADS].transpose(0, 2, 1) for a in (delta, lse)]
    (d_qc, d_kc, d_vc, dc_k, dc_q), rode = _fox_bwd(proj, da, c_col, c_row, lse_row, delta_row, nb, f"fox_bwd_{l}", rider)
    dc_q = jnp.pad(dc_q.transpose(0, 2, 1).reshape(n, FOX_HEADS), ((0, 0), (0, FC_PAD - FOX_HEADS)))
    d_fc, dbias = _fox_decay_bwd(dc_q, dc_k, fc, bias, nb, f"fox_decay_bwd_{l}")
    g["fox_f_bias"] = dbias[0, :FOX_HEADS]
    pieces = [(d_a, C_QA), (d_b, C_UB), (d_qc, C_QC), (d_kc, C_KC), (d_vc, C_VC), (d_gc, C_GC), (d_fc, None)]
    g["w_in"] = _w_in_grad(ht, pieces, f"w_in_grad_{l}")
    return g, pieces, rode


def _layer_bwd_input(l, dx, pieces, saved, weights, lw, rider=None):
    (dxi, dgpre), rode = _in_proj_bwd(pieces, lw[2], lw[3], saved[0], weights["pre_norm_g"][l:l + 1], dx,
                                      f"in_proj_bwd_{l}", rider)
    return dxi, dgpre[0], rode


def kernel(x, lower_bounds, pre_norm_g, w_in, hgrn_norm_g, fox_f_bias, pool_w, pool_scale, w_out, post_norm_g, loss_target, m_lower_bounds, m_pre_norm_g, m_w_in, m_hgrn_norm_g, m_fox_f_bias, m_pool_w, m_pool_scale, m_w_out, m_post_norm_g, v_lower_bounds, v_pre_norm_g, v_w_in, v_hgrn_norm_g, v_fox_f_bias, v_pool_w, v_pool_scale, v_w_out, v_post_norm_g):
    weights = dict(lower_bounds=lower_bounds, pre_norm_g=pre_norm_g, hgrn_norm_g=hgrn_norm_g, fox_f_bias=fox_f_bias,
                   pool_w=pool_w, pool_scale=pool_scale, post_norm_g=post_norm_g)
    mom_m = dict(lower_bounds=m_lower_bounds, pre_norm_g=m_pre_norm_g, hgrn_norm_g=m_hgrn_norm_g, fox_f_bias=m_fox_f_bias,
                 pool_w=m_pool_w, pool_scale=m_pool_scale, post_norm_g=m_post_norm_g)
    mom_v = dict(lower_bounds=v_lower_bounds, pre_norm_g=v_pre_norm_g, hgrn_norm_g=v_hgrn_norm_g, fox_f_bias=v_fox_f_bias,
                 pool_w=v_pool_w, pool_scale=v_pool_scale, post_norm_g=v_post_norm_g)
    depth = w_in.shape[0]
    nb, t, d = x.shape
    n = nb * t
    core = lax.axis_index("c").astype(jnp.int32).reshape(1, 1)
    shards = [(w_in[l].astype(BF16), w_out[l].astype(BF16)) for l in range(depth)]
    lbs = _lower_bound_table(lower_bounds, "lower_bound_table")

    (g_in,) = _gather_weights(shards[0][0])
    coming = tuple(_assemble_w_in(g_in, "assemble_w_in_0")) + (None, None)
    xl, saved, lw = x.reshape(n, d), [], []
    for l in range(depth):
        last = l + 1 == depth
        riding = ([shards[l][1]] if coming[4] is None else []) + ([] if last else list(shards[l + 1]))
        xl, sv, lw_l, rode = _layer_fwd(l, xl, lbs, weights, coming, nb, _gather_rider(riding) if riding else None,
                                        loss_target.reshape(n, d) if last else None)
        saved.append(sv)
        lw.append(lw_l)
        if not last:
            coming = tuple(_assemble_w_in(rode[0], f"assemble_w_in_{l + 1}")) + _w_out_parts(rode[1])
    dx, sq = xl
    loss_here = 0.5 * jnp.sum(sq) / d

    grads, recv, pending = [None] * depth, [None] * depth, None
    for l in reversed(range(depth)):
        g, pieces, rode = _layer_bwd(l, dx, saved[l], lbs, weights, lw[l], nb, pending)
        if rode is not None:
            recv[l + 1] = rode
        blocks = (g["w_in"], g["w_out"])
        if l > 0:
            pending = _direct_exchange_rider(blocks)
            dx, g["pre_norm_g"], _ = _layer_bwd_input(l, dx, pieces, saved[l], weights, lw[l])
        else:
            other = _run_rider(_swap_rider(blocks), "grad_swap")
            summed = [_pair_add(hv, ot, core, f"grad_pair_add_{i}") for i, (hv, ot) in enumerate(zip(blocks, other))]
            dx, g["pre_norm_g"], recv[l] = _layer_bwd_input(l, dx, pieces, saved[l], weights, lw[l],
                                                            _chip_exchange_rider(summed))
        grads[l] = g
    small = {k: jnp.stack([grads[l][k] for l in range(depth)]) for k in SMALL if k != "lower_bounds"}
    small["lower_bounds"] = _lower_bound_bwd(lower_bounds, jnp.stack([grads[l]["lbs"] for l in range(depth)]),
                                             "lower_bound_bwd")
    (r_small,) = _run_rider(_gather_rider([_pack_small(small, loss_here)]), "small_grads_gather")

    res_in, _ = _sum_adamw([recv[l][0] for l in range(depth)], w_in, m_w_in, v_w_in, "adamw_w_in")
    res_out, _ = _sum_adamw([recv[l][1] for l in range(depth)], w_out, m_w_out, v_w_out, "adamw_w_out")
    res_small, _ = _sum_adamw([r_small], _pack_small(weights)[None], _pack_small(mom_m)[None], _pack_small(mom_v)[None],
                              "adamw_small")
    loss = res_small[0][0].reshape(-1)[_small_size(weights)]

    names = ("lower_bounds", "pre_norm_g", "w_in", "hgrn_norm_g", "fox_f_bias", "pool_w", "pool_scale", "w_out", "post_norm_g")
    outs = [loss, dx.reshape(nb, t, d)]
    for i in range(4):
        full = dict(_unpack_small(res_small[i][0], weights), w_in=res_in[i], w_out=res_out[i])
        outs += [full[k] for k in names]
    return tuple(outs)
```

```python
import functools

import jax
import jax.numpy as jnp
from jax import lax
from jax.experimental import pallas as pl
from jax.experimental.pallas import tpu as pltpu

F32, BF16 = jnp.float32, jnp.bfloat16
HI = lax.Precision.HIGHEST
MESH = pl.DeviceIdType.MESH
AXES = ("x", "y", "c")
N_DEV = 8

NORM_EPS = 1e-6
MASK_VALUE = -1e30
TINY = 1e-30
CHUNK = 64
SUB = 16
HGRN_W, POOL_W, FOX_W = 256, 256, 512
HEAD = 64
FOX_HEADS = 8
POOL_WINDOWS = (2, 4, 8, 16)
POOL_HALO = 16
MAIN_W = 3584
FC_PAD = 128
C_QA, C_FA, C_IA, C_GA, C_UB, C_GB, C_QC, C_KC, C_VC, C_GC = 0, 256, 512, 768, 1024, 1280, 1536, 2048, 2560, 3072
FOX_SCALE = HEAD ** -0.5

ADAM_LR, ADAM_B1, ADAM_B2, ADAM_EPS, ADAM_WD, ADAM_STEP = 0.001, 0.9, 0.999, 1e-08, 0.01, 10

VMEM_LIMIT = 56 * 1024 * 1024


def _pc(fn, name, **kw):
    return pl.pallas_call(fn, name=name, **kw)


def _params(**kw):
    return pltpu.CompilerParams(vmem_limit_bytes=VMEM_LIMIT, **kw)


class _Rider:
    def __init__(self, inputs, out_shapes, n_sems, n_local, plan):
        self.inputs, self.out_shapes, self.n_sems, self.n_local, self.plan = list(inputs), list(out_shapes), n_sems, n_local, plan

    def start(self, ins, outs, *sems):
        sends, _, locs = self.plan(ins, outs, *sems)
        for cp in locs + sends:
            cp.start()

    def wait(self, ins, outs, *sems):
        sends, recvs, locs = self.plan(ins, outs, *sems)
        for cp in recvs:
            cp.wait_recv()
        for cp in sends:
            cp.wait_send()
        for cp in locs:
            cp.wait()

    def sem_shapes(self):
        return [pltpu.SemaphoreType.DMA((self.n_sems,)), pltpu.SemaphoreType.DMA((self.n_sems,)),
                pltpu.SemaphoreType.DMA((self.n_local,))]


def _call(body, name, args, rider=None, *, grid, in_specs, out_specs, out_shape, scratch_shapes=(), **kw):
    if rider is None:
        res = _pc(body, name, grid=grid, in_specs=in_specs, out_specs=out_specs, out_shape=out_shape,
                  scratch_shapes=list(scratch_shapes), **kw)(*args)
        return res, None
    n_in, n_out, n_scr = len(in_specs), len(out_specs), len(scratch_shapes)
    n_rin, n_rout = len(rider.inputs), len(rider.out_shapes)

    def ridden(*refs):
        ins, refs = refs[:n_in], refs[n_in:]
        rins, refs = refs[:n_rin], refs[n_rin:]
        outs, refs = refs[:n_out], refs[n_out:]
        routs, refs = refs[:n_rout], refs[n_rout:]
        scr, sems = refs[:n_scr], refs[n_scr:]
        first = functools.reduce(jnp.logical_and, [pl.program_id(a) == 0 for a in range(len(grid))])
        last = functools.reduce(jnp.logical_and, [pl.program_id(a) == g - 1 for a, g in enumerate(grid)])

        @pl.when(first)
        def _():
            rider.start(rins, routs, *sems)

        body(*ins, *outs, *scr)

        @pl.when(last)
        def _():
            rider.wait(rins, routs, *sems)

    any_spec = pl.BlockSpec(memory_space=pl.ANY)
    res = _pc(ridden, name, grid=grid, in_specs=list(in_specs) + [any_spec] * n_rin,
              out_specs=list(out_specs) + [any_spec] * n_rout, out_shape=list(out_shape) + rider.out_shapes,
              scratch_shapes=list(scratch_shapes) + rider.sem_shapes(), **kw)(*args, *rider.inputs)
    return res[:n_out], res[n_out:]


def _run_rider(rider, name):
    n_rin = len(rider.inputs)

    def body(*refs):
        ins, outs, sems = refs[:n_rin], refs[n_rin:n_rin + len(rider.out_shapes)], refs[n_rin + len(rider.out_shapes):]
        rider.start(ins, outs, *sems)
        rider.wait(ins, outs, *sems)

    any_spec = pl.BlockSpec(memory_space=pl.ANY)
    return _pc(body, name, in_specs=[any_spec] * n_rin, out_specs=[any_spec] * len(rider.out_shapes),
               out_shape=rider.out_shapes, scratch_shapes=rider.sem_shapes())(*rider.inputs)


def _dot(a, b):
    return jnp.dot(a, b, preferred_element_type=F32)


def _dot_nt(a, b):
    return lax.dot_general(a, b, (((1,), (1,)), ((), ())), preferred_element_type=F32)


def _dot_tn(a, b):
    return lax.dot_general(a, b, (((0,), (0,)), ((), ())), preferred_element_type=F32)


def _dot_hi(a, b):
    return jnp.dot(a, b, precision=HI, preferred_element_type=F32)


def _split2(x):
    hi = x.astype(BF16)
    return hi, (x - hi.astype(F32)).astype(BF16)


def _sel_dot(sel, x):
    hi, lo = _split2(x)
    sb = sel.astype(BF16)
    return _dot(sb, hi) + _dot(sb, lo)


def _dot_sel(x, sel):
    hi, lo = _split2(x)
    sb = sel.astype(BF16)
    return _dot(hi, sb) + _dot(lo, sb)


def _sigmoid(x):
    return 1.0 / (1.0 + jnp.exp(-x))


def _block_ones(n, dtype):
    r = lax.broadcasted_iota(jnp.int32, (n, n), 0) // HEAD
    c = lax.broadcasted_iota(jnp.int32, (n, n), 1) // HEAD
    return (r == c).astype(dtype)


def _sds(shape, dtype):
    return jax.ShapeDtypeStruct(shape, dtype)


def _in_proj_fwd(x, g_pre, w_main, w_fc, name):
    n, d = x.shape
    tm = min(512, n)

    def body(x_ref, g_ref, w_ref, wf_ref, proj_ref, fc_ref, ht_ref):
        xv = x_ref[...]
        r = lax.rsqrt(jnp.mean(xv * xv, axis=-1, keepdims=True) + NORM_EPS)
        hf = xv * r * g_ref[...]
        hb = hf.astype(BF16)
        ht_ref[...] = hf.T.astype(BF16)
        for j in range(0, MAIN_W, 512):
            proj_ref[:, j:j + 512] = _dot(hb, w_ref[:, j:j + 512])
        fc_ref[...] = _dot(hb, wf_ref[...])

    return _pc(
        body, name, grid=(n // tm,),
        in_specs=[pl.BlockSpec((tm, d), lambda i: (i, 0)), pl.BlockSpec((1, d), lambda i: (0, 0)),
                  pl.BlockSpec((d, MAIN_W), lambda i: (0, 0)), pl.BlockSpec((d, FC_PAD), lambda i: (0, 0))],
        out_specs=[pl.BlockSpec((tm, MAIN_W), lambda i: (i, 0)), pl.BlockSpec((tm, FC_PAD), lambda i: (i, 0)),
                   pl.BlockSpec((d, tm), lambda i: (0, i))],
        out_shape=[_sds((n, MAIN_W), F32), _sds((n, FC_PAD), F32), _sds((d, n), BF16)],
        compiler_params=_params(),
    )(x, g_pre, w_main, w_fc)


def _fox_decay_fwd(fc, bias, nb, name):
    n = fc.shape[0]
    t = n // nb
    tt = min(256, t)
    nt = t // tt

    def body(fc_ref, b_ref, c_ref, carry):
        i = pl.program_id(1)

        @pl.when(i == 0)
        def _():
            carry[...] = jnp.zeros_like(carry)

        xv = fc_ref[...] + b_ref[...]
        lf = jnp.minimum(xv, 0.0) - jnp.log(1.0 + jnp.exp(-jnp.abs(xv)))
        r = lax.broadcasted_iota(jnp.int32, (tt, tt), 0)
        cc = lax.broadcasted_iota(jnp.int32, (tt, tt), 1)
        cs = _dot_hi((r >= cc).astype(F32), lf) + carry[...]
        c_ref[...] = cs
        carry[...] = cs[tt - 1:tt, :]

    return _pc(
        body, name, grid=(nb, nt),
        in_specs=[pl.BlockSpec((tt, FC_PAD), lambda b, i: (b * nt + i, 0)), pl.BlockSpec((1, FC_PAD), lambda b, i: (0, 0))],
        out_specs=pl.BlockSpec((tt, FC_PAD), lambda b, i: (b * nt + i, 0)),
        out_shape=_sds((n, FC_PAD), F32),
        scratch_shapes=[pltpu.VMEM((1, FC_PAD), F32)],
        compiler_params=_params(),
    )(fc, bias)


def _hgrn_gates(q, z, lb):
    sig = _sigmoid(z)
    sn = _sigmoid(-z)
    f = lb + (1.0 - lb) * sig
    g = jnp.log(jnp.maximum(f, TINY))
    k = (1.0 - lb) * sn
    sq = _sigmoid(q)
    return sig, sn, f, g, k, sq


def _sub_tri(n, lower):
    r = lax.broadcasted_iota(jnp.int32, (n, n), 0)
    c = lax.broadcasted_iota(jnp.int32, (n, n), 1)
    tri = (r >= c) if lower else (r <= c)
    return jnp.logical_and(r // SUB == c // SUB, tri).astype(F32)


def _live_rows(t):
    return 8 * (t // 8 + 1)


def _pad_rows(x):
    return x if x.shape[0] == SUB else jnp.concatenate([x, jnp.zeros((SUB - x.shape[0], x.shape[1]), x.dtype)], axis=0)


def _hgrn_decays(qs, k, b):
    srow = lax.broadcasted_iota(jnp.int32, (SUB, HGRN_W), 0)
    es, ws = [], []
    for t in range(SUB):
        r = _live_rows(t)
        e = jnp.where(srow[:r] <= t, jnp.exp(b[t:t + 1, :] - b[:r]), 0.0)
        es.append(e)
        ws.append(_pad_rows(e * (qs[t:t + 1, :] * k[:r])))
    return srow, es, ws


def _hgrn_state_step(st, k, v, b, bmask):
    bl = b[SUB - 1:SUB, :]
    ktil = k * jnp.exp(bl - b)
    return st * jnp.exp(bl) + _dot_tn(v.astype(BF16), ktil.astype(BF16)) * bmask


def _hgrn_tile(t):
    return min(256, t)


def _hgrn_fwd(proj, lb, ones_b, nb, name, rider=None):
    n = proj.shape[0]
    t = n // nb
    tt = _hgrn_tile(t)
    nt = t // tt
    ncs = tt // CHUNK
    w = HGRN_W

    def body(q_ref, z_ref, v_ref, lb_ref, ones_ref, o_ref, s0_ref, st_s, b_s, qs_s, k_s):
        @pl.when(pl.program_id(1) == 0)
        def _():
            st_s[...] = jnp.zeros_like(st_s)

        q = q_ref[...]
        _, _, _, g, k, sq = _hgrn_gates(q, z_ref[...], lb_ref[...])
        b_s[...] = _sel_dot(_sub_tri(tt, True), g)
        qs_s[...] = q * sq
        k_s[...] = k
        bmask = _block_ones(w, F32)
        ones_b = ones_ref[...]

        def chunk(c, carry):
            st = st_s[...]
            s0_ref[c] = st
            base = pl.multiple_of(c * CHUNK, CHUNK)
            tiles = []
            for u in range(CHUNK // SUB):
                rows = pl.ds(base + u * SUB, SUB)
                tiles.append((qs_s[rows, :], k_s[rows, :], v_ref[rows, :], b_s[rows, :]))
            aexps = []
            for qs, k, v, b in tiles:
                _, _, ws = _hgrn_decays(qs, k, b)
                aexps.append(_dot(jnp.concatenate(ws, axis=0).astype(BF16), ones_b))
            inters = []
            for qs, k, v, b in tiles:
                inters.append(_dot_nt((qs * jnp.exp(b)).astype(BF16), st.astype(BF16)))
                st = _hgrn_state_step(st, k, v, b, bmask)
            st_s[...] = st
            for u, ((qs, k, v, b), aexp, o) in enumerate(zip(tiles, aexps, inters)):
                for t in range(SUB):
                    r = _live_rows(t)
                    row = o[t:t + 1, :] + jnp.sum(aexp[t * SUB:t * SUB + r, :] * v[:r], axis=0, keepdims=True)
                    o_ref[pl.ds(base + u * SUB + t, 1), :] = row
            return carry

        lax.fori_loop(0, ncs, chunk, 0)

    def col(j):
        return pl.BlockSpec((tt, w), lambda b, i: (b * nt + i, j))

    return _call(
        body, name, (proj, proj, proj, lb, ones_b), rider, grid=(nb, nt),
        in_specs=[col(C_QA // w), col(C_FA // w), col(C_IA // w), pl.BlockSpec((1, w), lambda b, i: (0, 0)),
                  pl.BlockSpec((w, w), lambda b, i: (0, 0))],
        out_specs=[pl.BlockSpec((tt, w), lambda b, i: (b * nt + i, 0)),
                   pl.BlockSpec((ncs, w, w), lambda b, i: (b * nt + i, 0, 0))],
        out_shape=[_sds((n, w), F32), _sds((n // CHUNK, w, w), F32)],
        scratch_shapes=[pltpu.VMEM((w, w), F32)] + [pltpu.VMEM((tt, w), F32)] * 3,
        compiler_params=_params(),
    )


def _pool_lane_windows():
    lane = lax.broadcasted_iota(jnp.int32, (1, POOL_W), 1) // HEAD
    wl = jnp.zeros((1, POOL_W), F32)
    for gi, win in enumerate(POOL_WINDOWS):
        wl = jnp.where(lane == gi, float(win), wl)
    return lane, wl


def _pool_select(lane, parts):
    out = parts[-1]
    for gi in range(len(parts) - 2, -1, -1):
        out = jnp.where(lane == gi, parts[gi], out)
    return out


def _pool_mix(u, halo, t0, tt):
    lane, wl = _pool_lane_windows()
    ext = jnp.concatenate([halo, u], axis=0)
    sums, cur, shift = [], ext, 1
    for _ in POOL_WINDOWS:
        cur = cur + pltpu.roll(cur, shift, axis=0)
        sums.append(cur[POOL_HALO:, :])
        shift *= 2
    tpos = (t0 + lax.broadcasted_iota(jnp.int32, (tt, POOL_W), 0)).astype(F32)
    cnt = jnp.minimum(tpos + 1.0, wl)
    return _pool_select(lane, sums) / cnt - u, cnt


def _pool_specs(tt, nt, nhb):
    cu, cg = C_UB // POOL_W, C_GB // POOL_W
    return [pl.BlockSpec((tt, POOL_W), lambda b, i: (b * nt + i, cu)),
            pl.BlockSpec((tt, POOL_W), lambda b, i: (b * nt + i, cg)),
            pl.BlockSpec((POOL_HALO, POOL_W), lambda b, i: (jnp.maximum((b * nt + i) * nhb - 1, 0), cu))]


def _pool_fwd(proj, wbd, scale, nb, name):
    n = proj.shape[0]
    t = n // nb
    tt = min(256, t)
    nt = t // tt
    nhb = tt // POOL_HALO

    def body(u_ref, g_ref, h_ref, w_ref, s_ref, o_ref):
        i = pl.program_id(1)
        halo = jnp.where(i == 0, 0.0, h_ref[...])
        pooled, _ = _pool_mix(u_ref[...], halo, i * tt, tt)
        gv = g_ref[...]
        o_ref[...] = _dot(pooled.astype(BF16), w_ref[...]) * s_ref[...] * (gv * _sigmoid(gv))

    return _pc(
        body, name, grid=(nb, nt),
        in_specs=_pool_specs(tt, nt, nhb) + [pl.BlockSpec((POOL_W, POOL_W), lambda b, i: (0, 0)),
                                             pl.BlockSpec((1, POOL_W), lambda b, i: (0, 0))],
        out_specs=pl.BlockSpec((tt, POOL_W), lambda b, i: (b * nt + i, 0)),
        out_shape=_sds((n, POOL_W), F32),
        compiler_params=_params(),
    )(proj, proj, proj, wbd, scale)


def _rows_reduce(x, op, final):
    while x.shape[0] > 8 and x.shape[0] % 16 == 0:
        half = x.shape[0] // 2
        x = op(x[:half], x[half:])
    return final(x, axis=0, keepdims=True)


def _tri_pair(step, n):
    a = sum([(step >= r * (r + 1) // 2).astype(jnp.int32) for r in range(1, n)], jnp.int32(0))
    return a, step - a * (a + 1) // 2


def _lane_lo():
    return lax.broadcasted_iota(jnp.int32, (1, 2 * HEAD), 1) < HEAD


def _put_col(tile, hh, colv):
    lane = lax.broadcasted_iota(jnp.int32, tile.shape, 1)
    return jnp.where(lane == hh, colv, tile)


def _fox_fwd(proj, c_col, c_row, nb, name, rider=None):
    n = proj.shape[0]
    t = n // nb
    tb = min(256, t)
    nq = t // tb
    pw = 2 * HEAD

    def body(q_ref, k_ref, v_ref, cc_ref, cr_ref, o_ref, lse_ref, m_s, acc_s, cq_s):
        qi, kj = _tri_pair(pl.program_id(1), nq)

        @pl.when(kj == 0)
        def _():
            m_s[...] = jnp.full_like(m_s, -jnp.inf)
            acc_s[...] = jnp.zeros_like(acc_s)
            for hh in range(FOX_HEADS):
                cq_s[hh] = jnp.broadcast_to(cc_ref[:, hh:hh + 1], (tb, pw))

        def block(masked):
            lo = _lane_lo()
            if masked:
                causal = lax.broadcasted_iota(jnp.int32, (tb, tb), 0) >= lax.broadcasted_iota(jnp.int32, (tb, tb), 1)
            def lanes(hh):
                return lo if hh % 2 == 0 else jnp.logical_not(lo)

            def scores(hh):
                sl = slice((hh // 2) * pw, (hh // 2 + 1) * pw)
                return _dot_nt(jnp.where(lanes(hh), q_ref[:, sl] * FOX_SCALE, 0.0).astype(BF16), k_ref[:, sl].astype(BF16))

            ahead = scores(0)
            for hh in range(FOX_HEADS):
                s = ahead
                if hh + 1 < FOX_HEADS:
                    ahead = scores(hh + 1)
                s = s + (jnp.tile(cq_s[hh], (1, tb // pw)) - cr_ref[hh:hh + 1, :])
                if masked:
                    s = jnp.where(causal, s, MASK_VALUE)
                m_prev = m_s[hh]
                m_new = jnp.maximum(m_prev, jnp.max(s, axis=1, keepdims=True))
                alpha = jnp.exp(m_prev - m_new)
                pe = jnp.exp(s - jnp.tile(m_new, (1, tb // pw)))
                m_s[hh] = m_new
                vf = v_ref[:, (hh // 2) * pw:(hh // 2 + 1) * pw]
                acc_s[hh] = alpha * acc_s[hh] + _dot(pe.astype(BF16), jnp.where(lanes(hh), vf, 1.0).astype(BF16))

        @pl.when(kj < qi)
        def _():
            block(False)

        @pl.when(kj == qi)
        def _():
            block(True)
            lo = _lane_lo()
            lse = jnp.zeros((tb, FC_PAD), F32)
            for p in range(FOX_HEADS // 2):
                halves = []
                for h in range(2):
                    hh = 2 * p + h
                    lm = lo if h == 0 else jnp.logical_not(lo)
                    acc = acc_s[hh]
                    swapped = pltpu.roll(acc, HEAD, axis=1)
                    halves.append(acc / swapped)
                    lse = _put_col(lse, hh, m_s[hh] + jnp.log(jnp.where(lm, swapped, acc)))
                o_ref[:, p * pw:(p + 1) * pw] = jnp.where(lo, halves[0], halves[1])
            lse_ref[...] = lse

    def qspec(wd, j):
        return pl.BlockSpec((tb, wd), lambda b, st: (b * nq + _tri_pair(st, nq)[0], j))

    def kspec(j):
        return pl.BlockSpec((tb, FOX_W), lambda b, st: (b * nq + _tri_pair(st, nq)[1], j))

    return _call(
        body, name, (proj, proj, proj, c_col, c_row), rider, grid=(nb, nq * (nq + 1) // 2),
        in_specs=[qspec(FOX_W, C_QC // FOX_W), kspec(C_KC // FOX_W), kspec(C_VC // FOX_W), qspec(FC_PAD, 0),
                  pl.BlockSpec((None, FOX_HEADS, tb), lambda b, st: (b, 0, _tri_pair(st, nq)[1]))],
        out_specs=[qspec(FOX_W, 0), qspec(FC_PAD, 0)],
        out_shape=[_sds((n, FOX_W), F32), _sds((n, FC_PAD), F32)],
        scratch_shapes=[pltpu.VMEM((FOX_HEADS, tb, pw), F32), pltpu.VMEM((FOX_HEADS, tb, pw), F32),
                        pltpu.VMEM((FOX_HEADS, tb, pw), F32)],
        compiler_params=_params(),
    )


def _head_mean(x, ones_f):
    return _dot_sel(x, ones_f) * (1.0 / HEAD)


def _merge_fwd(x, proj, o_h, o_b, o_c, gh, w_out, g_post, name, target=None):
    n, d = x.shape
    tm = min(512, n)

    def body(*refs):
        x_ref, ga_ref, gc_ref, oh_ref, ob_ref, oc_ref, gh_ref, w_ref, gp_ref = refs[:9]
        if target is None:
            xo_ref, mixt_ref, y_ref = refs[9:]
        else:
            t_ref, dx_ref, sq_ref, mixt_ref, y_ref = refs[9:]
        oh = oh_ref[...]
        ones_f = _block_ones(HGRN_W, F32)
        na = oh * lax.rsqrt(_head_mean(oh * oh, ones_f) + NORM_EPS) * gh_ref[...]
        ga, gc = ga_ref[...], gc_ref[...]
        mixed = jnp.concatenate([na * (ga * _sigmoid(ga)), ob_ref[...], oc_ref[...] * (gc * _sigmoid(gc))], axis=1)
        mixt_ref[...] = mixed.T.astype(BF16)
        y = _dot(mixed.astype(BF16), w_ref[...])
        y_ref[...] = y
        xn = x_ref[...] + y * lax.rsqrt(jnp.mean(y * y, axis=-1, keepdims=True) + NORM_EPS) * gp_ref[...]
        if target is None:
            xo_ref[...] = xn
        else:
            @pl.when(pl.program_id(0) == 0)
            def _():
                sq_ref[...] = jnp.zeros_like(sq_ref)

            e = xn - t_ref[...]
            dx_ref[...] = e * (1.0 / d)
            sq_ref[...] += jnp.sum(e * e, axis=0, keepdims=True)

    def row(wd, j=0):
        return pl.BlockSpec((tm, wd), lambda i: (i, j))

    def full(a, b):
        return pl.BlockSpec((a, b), lambda i: (0, 0))

    head = [] if target is None else [target]
    res = _pc(
        body, name, grid=(n // tm,),
        in_specs=[row(d), row(HGRN_W, C_GA // HGRN_W), row(FOX_W, C_GC // FOX_W), row(HGRN_W), row(POOL_W), row(FOX_W),
                  full(1, HGRN_W), full(d, d), full(1, d)] + [row(d)] * len(head),
        out_specs=[row(d)] + [full(1, d)] * len(head) + [pl.BlockSpec((d, tm), lambda i: (0, i)), row(d)],
        out_shape=[_sds((n, d), F32)] + [_sds((1, d), F32)] * len(head) + [_sds((d, n), BF16), _sds((n, d), F32)],
        compiler_params=_params(),
    )(x, proj, proj, o_h, o_b, o_c, gh, w_out, g_post, *head)
    return (res[0], res[1], res[2]) if target is None else ((res[0], res[1]), res[2], res[3])


def _rms_bwd(dy_scaled, xhat, r):
    return r * (dy_scaled - xhat * jnp.mean(dy_scaled * xhat, axis=-1, keepdims=True))


def _merge_bwd(dxo, y, g_post, w_out_t, proj, o_c, name):
    n, d = y.shape
    tm = min(512, n)
    wab = HGRN_W + POOL_W

    def body(dx_ref, y_ref, gp_ref, wt_ref, gc_ref, oc_ref, dy_ref, dm_ref, dgp_ref, da_ref, dg_ref, dl_ref):
        @pl.when(pl.program_id(0) == 0)
        def _():
            dgp_ref[...] = jnp.zeros_like(dgp_ref)

        yv, dxv = y_ref[...], dx_ref[...]
        r = lax.rsqrt(jnp.mean(yv * yv, axis=-1, keepdims=True) + NORM_EPS)
        yh = yv * r
        dgp_ref[...] += jnp.sum(dxv * yh, axis=0, keepdims=True)
        dyb = _rms_bwd(dxv * gp_ref[...], yh, r).astype(BF16)
        dy_ref[...] = dyb
        dm_ref[...] = _dot(dyb, wt_ref[:, :wab])
        dmc = _dot(dyb, wt_ref[:, wab:])
        gc, oc = gc_ref[...], oc_ref[...]
        sg = _sigmoid(gc)
        da = dmc * (gc * sg)
        da_ref[...] = da.astype(BF16)
        dg_ref[...] = (dmc * oc * (sg * (1.0 + gc * (1.0 - sg)))).astype(BF16)
        rr = lax.broadcasted_iota(jnp.int32, (FOX_W, FC_PAD), 0) // HEAD
        cc = lax.broadcasted_iota(jnp.int32, (FOX_W, FC_PAD), 1)
        dl_ref[...] = _dot_sel(da * oc, (rr == cc).astype(F32))

    def row(wd, j=0):
        return pl.BlockSpec((tm, wd), lambda i: (i, j))

    def full(a, b):
        return pl.BlockSpec((a, b), lambda i: (0, 0))

    return _pc(
        body, name, grid=(n // tm,),
        in_specs=[row(d), row(d), full(1, d), full(d, d), row(FOX_W, C_GC // FOX_W), row(FOX_W)],
        out_specs=[row(d), row(wab), full(1, d), row(FOX_W), row(FOX_W), row(FC_PAD)],
        out_shape=[_sds((n, d), BF16), _sds((n, wab), F32), _sds((1, d), F32), _sds((n, FOX_W), BF16),
                   _sds((n, FOX_W), BF16), _sds((n, FC_PAD), F32)],
        compiler_params=_params(),
    )(dxo, y, g_post, w_out_t, proj, o_c)


def _w_out_grad(mixt, dy, name):
    d, n = mixt.shape
    rows = d // N_DEV

    def body(a_ref, b_ref, o_ref):
        o_ref[...] = _dot(a_ref[...], b_ref[...]).astype(BF16)

    return _pc(
        body, name, grid=(N_DEV,),
        in_specs=[pl.BlockSpec((rows, n), lambda j: (j, 0)), pl.BlockSpec((n, d), lambda j: (0, 0))],
        out_specs=pl.BlockSpec((None, None, rows, d), lambda j: (j % 2, j // 2, 0, 0)),
        out_shape=_sds((2, N_DEV // 2, rows, d), BF16),
        compiler_params=_params(),
    )(mixt, dy)


def _w_in_grad(ht, pieces, name):
    d, n = ht.shape
    ta, tk = min(512, d), min(512, n)
    nk = n // tk
    arrays = [p for p, _ in pieces]
    widths = [p.shape[1] for p in arrays]
    offs = [sum(widths[:i]) for i in range(len(widths))]
    in_w = MAIN_W + FOX_HEADS
    shard = in_w // N_DEV

    def body(*refs):
        a_ref, p_refs = refs[0], refs[1:1 + len(arrays)]
        o_ref, acc = refs[1 + len(arrays):]
        k = pl.program_id(1)

        @pl.when(k == 0)
        def _():
            acc[...] = jnp.zeros_like(acc)

        a = a_ref[...]
        for pr, off, wd in zip(p_refs, offs, widths):
            for j in range(0, wd, 512):
                jw = min(512, wd - j)
                acc[:, off + j:off + j + jw] += _dot(a, pr[:, j:j + jw])

        @pl.when(k == nk - 1)
        def _():
            for j in range(N_DEV):
                o_ref[j % 2, j // 2] = acc[:, j * shard:(j + 1) * shard].astype(BF16)

    return _pc(
        body, name, grid=(d // ta, nk),
        in_specs=[pl.BlockSpec((ta, tk), lambda i, k: (i, k))] + [pl.BlockSpec((tk, wd), lambda i, k: (k, 0)) for wd in widths],
        out_specs=pl.BlockSpec((2, N_DEV // 2, ta, shard), lambda i, k: (0, 0, i, 0)),
        out_shape=_sds((2, N_DEV // 2, d, shard), BF16),
        scratch_shapes=[pltpu.VMEM((ta, sum(widths)), F32)],
        compiler_params=_params(),
    )(ht, *arrays)


def _hgrn_state_bwd(qs, k, v, b, do, s0, ds1, bmask):
    bl = b[SUB - 1:SUB, :]
    eb, ebl, ekt = jnp.exp(b), jnp.exp(bl), jnp.exp(bl - b)
    qe, ktil = qs * eb, k * ekt
    ds1b, dob = ds1.astype(BF16), do.astype(BF16)
    dv = _dot_nt(ktil.astype(BF16), ds1b)
    dqe = _dot(dob, s0.astype(BF16))
    dktil = _dot(v.astype(BF16), ds1b)
    dbl = jnp.sum(dktil * ktil, axis=0, keepdims=True) + ebl * jnp.sum(s0 * ds1, axis=0, keepdims=True)
    ds0 = ds1 * ebl + _dot_tn(dob, qe.astype(BF16)) * bmask
    return dqe * eb, dktil * ekt, dv, dbl, ds0


def _hgrn_intra_bwd(qs, k, v, do, es, aexp, gexp, dq, dk, dv, put_dq_row):
    dks = [dk[j:j + 8] for j in range(0, SUB, 8)]
    dvs = [dv[j:j + 8] for j in range(0, SUB, 8)]
    for t in range(SUB):
        r = _live_rows(t)
        ge = gexp[t * SUB:t * SUB + r, :] * es[t]
        put_dq_row(t, dq[t:t + 1, :] + jnp.sum(ge * k[:r], axis=0, keepdims=True))
        for j in range(r // 8):
            dks[j] = dks[j] + ge[8 * j:8 * j + 8] * qs[t:t + 1, :]
            dvs[j] = dvs[j] + aexp[t * SUB + 8 * j:t * SUB + 8 * j + 8, :] * do[t:t + 1, :]
    return jnp.concatenate(dks, axis=0), jnp.concatenate(dvs, axis=0)


def _hgrn_bwd(dmix, proj, o_h, s0, gh, lb, ones_b, nb, name):
    n = proj.shape[0]
    t = n // nb
    tt = _hgrn_tile(t)
    nt = t // tt
    ncs = tt // CHUNK
    nsub = CHUNK // SUB
    w = HGRN_W

    def body(dm_ref, q_ref, z_ref, v_ref, ga_ref, oh_ref, s0_ref, gh_ref, lb_ref, ones_ref,
             dp_ref, dgh_ref, dlb_ref, ds_s, ss_s, b_s, qs_s, k_s, do_s, dq_s, dk_s, dv_s, dbl_s):
        first = jnp.logical_and(pl.program_id(0) == 0, pl.program_id(1) == 0)

        @pl.when(first)
        def _():
            dgh_ref[...] = jnp.zeros_like(dgh_ref)
            dlb_ref[...] = jnp.zeros_like(dlb_ref)

        @pl.when(pl.program_id(1) == 0)
        def _():
            ds_s[...] = jnp.zeros_like(ds_s)

        ones_b = ones_ref[...]
        ones_f = ones_b.astype(F32)
        bmask = _block_ones(w, F32)
        lbv, ghv = lb_ref[...], gh_ref[...]
        oh, ga, dm = oh_ref[...], ga_ref[...], dm_ref[...]
        rn = lax.rsqrt(_head_mean(oh * oh, ones_f) + NORM_EPS)
        nh = oh * rn
        sga = _sigmoid(ga)
        dp_ref[:, 3 * w:4 * w] = (dm * nh * ghv * (sga * (1.0 + ga * (1.0 - sga)))).astype(BF16)
        dn = dm * (ga * sga)
        dgh_ref[...] += jnp.sum(dn * nh, axis=0, keepdims=True)
        dn = dn * ghv
        do_s[...] = rn * (dn - nh * _head_mean(dn * nh, ones_f))
        q = q_ref[...]
        sig, sn, f, g, k, sq = _hgrn_gates(q, z_ref[...], lbv)
        qs = q * sq
        b_s[...] = _sel_dot(_sub_tri(tt, True), g)
        qs_s[...] = qs
        k_s[...] = k

        def chunk(cc, carry):
            c = ncs - 1 - cc
            base = pl.multiple_of(c * CHUNK, CHUNK)
            tiles = []
            for u in range(nsub):
                rows = pl.ds(base + u * SUB, SUB)
                tiles.append((qs_s[rows, :], k_s[rows, :], v_ref[rows, :], b_s[rows, :], do_s[rows, :]))
            st = s0_ref[c]
            for u, (qs, k, v, b, do) in enumerate(tiles):
                ss_s[u] = st
                if u < nsub - 1:
                    st = _hgrn_state_step(st, k, v, b, bmask)
            ds = ds_s[...]
            for u in reversed(range(nsub)):
                qs, k, v, b, do = tiles[u]
                _, es, ws = _hgrn_decays(qs, k, b)
                gs = [_pad_rows(do[t:t + 1, :] * v[:_live_rows(t)]) for t in range(SUB)]
                aexp = _dot(jnp.concatenate(ws, axis=0).astype(BF16), ones_b)
                gexp = _dot(jnp.concatenate(gs, axis=0).astype(BF16), ones_b)
                dq, dk, dv, dbl, ds = _hgrn_state_bwd(qs, k, v, b, do, ss_s[u], ds, bmask)

                def put_dq_row(i, row, r0=base + u * SUB):
                    dq_s[pl.ds(r0 + i, 1), :] = row

                dk, dv = _hgrn_intra_bwd(qs, k, v, do, es, aexp, gexp, dq, dk, dv, put_dq_row)
                dk_s[pl.ds(base + u * SUB, SUB), :] = dk
                dv_s[pl.ds(base + u * SUB, SUB), :] = dv
                dbl_s[pl.ds(base + u * SUB, SUB), :] = jnp.broadcast_to(dbl, (SUB, w))
            ds_s[...] = ds
            return carry

        lax.fori_loop(0, ncs, chunk, 0)
        dqs, dk = dq_s[...], dk_s[...]
        dg = _sel_dot(_sub_tri(tt, False), qs * dqs - k * dk) + dbl_s[...]
        dfz = jnp.where(f > TINY, dg / jnp.maximum(f, TINY), 0.0)
        dlb_ref[...] += jnp.sum(dfz * (1.0 - sig) - dk * sn, axis=0, keepdims=True)
        dp_ref[:, 0:w] = (dqs * (sq * (1.0 + q * (1.0 - sq)))).astype(BF16)
        dp_ref[:, w:2 * w] = ((dfz - dk) * (1.0 - lbv) * sig * sn).astype(BF16)
        dp_ref[:, 2 * w:3 * w] = dv_s[...].astype(BF16)

    def rv(b, i):
        return b * nt + (nt - 1 - i)

    def col(j):
        return pl.BlockSpec((tt, w), lambda b, i: (rv(b, i), j))

    def full(a, bb):
        return pl.BlockSpec((a, bb), lambda b, i: (0, 0))

    return _pc(
        body, name, grid=(nb, nt),
        in_specs=[col(0), col(C_QA // w), col(C_FA // w), col(C_IA // w), col(C_GA // w), col(0),
                  pl.BlockSpec((ncs, w, w), lambda b, i: (rv(b, i), 0, 0)), full(1, w), full(1, w), full(w, w)],
        out_specs=[pl.BlockSpec((tt, 4 * w), lambda b, i: (rv(b, i), 0)), full(1, w), full(1, w)],
        out_shape=[_sds((n, 4 * w), BF16), _sds((1, w), F32), _sds((1, w), F32)],
        scratch_shapes=[pltpu.VMEM((w, w), F32), pltpu.VMEM((nsub, w, w), F32)] + [pltpu.VMEM((tt, w), F32)] * 8,
        compiler_params=_params(),
    )(dmix, proj, proj, proj, proj, o_h, s0, gh, lb, ones_b)


def _pool_bwd(dmix, proj, wbd, wbd_t, scale, nb, name):
    n = proj.shape[0]
    t = n // nb
    tt = min(256, t)
    nt = t // tt
    nhb = tt // POOL_HALO
    cu, cg, cm = C_UB // POOL_W, C_GB // POOL_W, HGRN_W // POOL_W

    def body(u_ref, g_ref, h_ref, dm_ref, gn_ref, dmn_ref, w_ref, wt_ref, s_ref, dp_ref, dw_ref, ds_ref):
        i = pl.program_id(1)
        first = jnp.logical_and(pl.program_id(0) == 0, i == 0)

        @pl.when(first)
        def _():
            dw_ref[...] = jnp.zeros_like(dw_ref)
            ds_ref[...] = jnp.zeros_like(ds_ref)

        sc = s_ref[...]
        halo = jnp.where(i == 0, 0.0, h_ref[...])
        pooled, cnt = _pool_mix(u_ref[...], halo, i * tt, tt)
        pb = pooled.astype(BF16)
        pre = _dot(pb, w_ref[...])
        gv, dm = g_ref[...], dm_ref[...]
        sg = _sigmoid(gv)
        silu = gv * sg
        dgb = dm * pre * sc * (sg * (1.0 + gv * (1.0 - sg)))
        ds_ref[...] += jnp.sum(dm * pre * silu, axis=0, keepdims=True)
        dpre = (dm * sc * silu).astype(BF16)
        dw_ref[...] += _dot_tn(pb, dpre)
        dpool = _dot(dpre, wt_ref[...])
        gn = gn_ref[...]
        dpre_n = (dmn_ref[...] * sc * (gn * _sigmoid(gn))).astype(BF16)
        dpool_n = jnp.where(i == nt - 1, 0.0, _dot(dpre_n, wt_ref[...]))
        lane, wl = _pool_lane_windows()
        tpos_n = ((i + 1) * tt + lax.broadcasted_iota(jnp.int32, (POOL_HALO, POOL_W), 0)).astype(F32)
        ext = jnp.concatenate([dpool / cnt, dpool_n / jnp.minimum(tpos_n + 1.0, wl)], axis=0)
        rows = tt + POOL_HALO
        sums, cur, shift = [], ext, 1
        for _ in POOL_WINDOWS:
            cur = cur + pltpu.roll(cur, rows - shift, axis=0)
            sums.append(cur[:tt, :])
            shift *= 2
        du = _pool_select(lane, sums) - dpool
        dp_ref[...] = jnp.concatenate([du, dgb], axis=1).astype(BF16)

    def nxt(b, i):
        return jnp.minimum((b * nt + i + 1) * nhb, n // POOL_HALO - 1)

    return _pc(
        body, name, grid=(nb, nt),
        in_specs=_pool_specs(tt, nt, nhb) + [
            pl.BlockSpec((tt, POOL_W), lambda b, i: (b * nt + i, cm)),
            pl.BlockSpec((POOL_HALO, POOL_W), lambda b, i: (nxt(b, i), cg)),
            pl.BlockSpec((POOL_HALO, POOL_W), lambda b, i: (nxt(b, i), cm)),
            pl.BlockSpec((POOL_W, POOL_W), lambda b, i: (0, 0)), pl.BlockSpec((POOL_W, POOL_W), lambda b, i: (0, 0)),
            pl.BlockSpec((1, POOL_W), lambda b, i: (0, 0))],
        out_specs=[pl.BlockSpec((tt, 2 * POOL_W), lambda b, i: (b * nt + i, 0)),
                   pl.BlockSpec((POOL_W, POOL_W), lambda b, i: (0, 0)), pl.BlockSpec((1, POOL_W), lambda b, i: (0, 0))],
        out_shape=[_sds((n, 2 * POOL_W), BF16), _sds((POOL_W, POOL_W), F32), _sds((1, POOL_W), F32)],
        compiler_params=_params(),
    )(proj, proj, proj, dmix, proj, dmix, wbd, wbd_t, scale)


def _fox_bwd(proj, da, c_col, c_row, lse_row, delta_row, nb, name, rider=None):
    n = proj.shape[0]
    t = n // nb
    tb = min(256, t)
    nq = t // tb
    pw = 2 * HEAD

    def body(q_ref, k_ref, v_ref, da_ref, cc_ref, cr_ref, lse_ref, dl_ref,
             dq_ref, dk_ref, dv_ref, dck_ref, dcq_ref, dq_s, dk_s, dv_s, dck_s, dcq_s):
        step = pl.program_id(1)
        kj, qi = pairs(step)

        @pl.when(step == 0)
        def _():
            dq_s[...] = jnp.zeros_like(dq_s)
            dcq_s[...] = jnp.zeros_like(dcq_s)

        @pl.when(qi == nq - 1)
        def _():
            dk_s[...] = jnp.zeros_like(dk_s)
            dv_s[...] = jnp.zeros_like(dv_s)
            dck_s[...] = jnp.zeros_like(dck_s)

        def block(masked):
            lo = _lane_lo()
            if masked:
                causal = lax.broadcasted_iota(jnp.int32, (tb, tb), 1) >= lax.broadcasted_iota(jnp.int32, (tb, tb), 0)
            dck = dck_s[...]
            for p in range(FOX_HEADS // 2):
                sl = slice(p * pw, (p + 1) * pw)
                qp = q_ref[:, sl] * FOX_SCALE
                kf = k_ref[:, sl]
                kp = kf.astype(BF16)
                kt = kf.T.astype(BF16)
                vp = v_ref[:, sl].astype(BF16)
                dap = da_ref[:, sl]
                dk, dv = dk_s[:, sl], dv_s[:, sl]
                for h in range(2):
                    hh = 2 * p + h
                    lm = lo if h == 0 else jnp.logical_not(lo)
                    qm = jnp.where(lm, qp, 0.0).astype(BF16)
                    dam = jnp.where(lm, dap, jnp.zeros_like(dap))
                    s = _dot_nt(kp, qm) + (cr_ref[hh:hh + 1, :] - cc_ref[:, hh:hh + 1])
                    pe = jnp.exp(s - lse_ref[hh:hh + 1, :])
                    if masked:
                        pe = jnp.where(causal, pe, 0.0)
                    dp = _dot_nt(vp, dam)
                    ds = pe * (dp - dl_ref[hh:hh + 1, :])
                    dsb = ds.astype(BF16)
                    dv = dv + _dot(pe.astype(BF16), dam)
                    dk = dk + _dot(dsb, qm)
                    rows = slice(hh * HEAD, (hh + 1) * HEAD)
                    dq_s[qi, rows, :] += _dot(kt[h * HEAD:(h + 1) * HEAD, :], dsb)
                    dck = dck - _put_col(jnp.zeros_like(dck), hh, jnp.sum(ds, axis=1, keepdims=True))
                    dcq_s[qi, hh:hh + 1, :] += _rows_reduce(ds, jnp.add, jnp.sum)
                dk_s[:, sl] = dk
                dv_s[:, sl] = dv
            dck_s[...] = dck

        @pl.when(qi > kj)
        def _():
            block(False)

        @pl.when(qi == kj)
        def _():
            block(True)

        @pl.when(qi == kj)
        def _():
            dk_ref[...] = dk_s[...].astype(BF16)
            dv_ref[...] = dv_s[...].astype(BF16)
            dck_ref[...] = dck_s[...]

        @pl.when(step == nq * (nq + 1) // 2 - 1)
        def _():
            for j in range(nq):
                dq_ref[j * tb:(j + 1) * tb, :] = (dq_s[j].T * FOX_SCALE).astype(BF16)
                dcq_ref[:, j * tb:(j + 1) * tb] = dcq_s[j]

    def pairs(step):
        a, b = _tri_pair(step, nq)
        return nq - 1 - a, nq - 1 - b

    def kspec(wd, j=0):
        return pl.BlockSpec((tb, wd), lambda b, st: (b * nq + pairs(st)[0], j))

    def qspec(wd, j=0):
        return pl.BlockSpec((tb, wd), lambda b, st: (b * nq + pairs(st)[1], j))

    def qrow():
        return pl.BlockSpec((None, FOX_HEADS, tb), lambda b, st: (b, 0, pairs(st)[1]))

    return _call(
        body, name, (proj, proj, proj, da, c_col, c_row, lse_row, delta_row), rider, grid=(nb, nq * (nq + 1) // 2),
        in_specs=[qspec(FOX_W, C_QC // FOX_W), kspec(FOX_W, C_KC // FOX_W), kspec(FOX_W, C_VC // FOX_W), qspec(FOX_W),
                  kspec(FC_PAD), qrow(), qrow(), qrow()],
        out_specs=[pl.BlockSpec((t, FOX_W), lambda b, st: (b, 0)), kspec(FOX_W), kspec(FOX_W), kspec(FC_PAD),
                   pl.BlockSpec((None, FOX_HEADS, t), lambda b, st: (b, 0, 0))],
        out_shape=[_sds((n, FOX_W), BF16), _sds((n, FOX_W), BF16), _sds((n, FOX_W), BF16), _sds((n, FC_PAD), F32),
                   _sds((nb, FOX_HEADS, t), F32)],
        scratch_shapes=[pltpu.VMEM((nq, FOX_W, tb), F32), pltpu.VMEM((tb, FOX_W), F32), pltpu.VMEM((tb, FOX_W), F32),
                        pltpu.VMEM((tb, FC_PAD), F32), pltpu.VMEM((nq, FOX_HEADS, tb), F32)],
        compiler_params=_params(),
    )


def _fox_decay_bwd(dc_q, dc_k, fc, bias, nb, name):
    n = fc.shape[0]
    t = n // nb
    tt = min(256, t)
    nt = t // tt

    def body(dcq_ref, dck_ref, fc_ref, b_ref, dfc_ref, db_ref, carry):
        i = pl.program_id(1)
        first = jnp.logical_and(pl.program_id(0) == 0, i == 0)

        @pl.when(first)
        def _():
            db_ref[...] = jnp.zeros_like(db_ref)

        @pl.when(i == 0)
        def _():
            carry[...] = jnp.zeros_like(carry)

        r = lax.broadcasted_iota(jnp.int32, (tt, tt), 0)
        cc = lax.broadcasted_iota(jnp.int32, (tt, tt), 1)
        dlf = _dot_hi((r <= cc).astype(F32), dcq_ref[...] + dck_ref[...]) + carry[...]
        carry[...] = dlf[0:1, :]
        dfc = dlf * _sigmoid(-(fc_ref[...] + b_ref[...]))
        dfc_ref[...] = dfc.astype(BF16)
        db_ref[...] += jnp.sum(dfc, axis=0, keepdims=True)

    def row():
        return pl.BlockSpec((tt, FC_PAD), lambda b, i: (b * nt + (nt - 1 - i), 0))

    return _pc(
        body, name, grid=(nb, nt),
        in_specs=[row(), row(), row(), pl.BlockSpec((1, FC_PAD), lambda b, i: (0, 0))],
        out_specs=[row(), pl.BlockSpec((1, FC_PAD), lambda b, i: (0, 0))],
        out_shape=[_sds((n, FC_PAD), BF16), _sds((1, FC_PAD), F32)],
        scratch_shapes=[pltpu.VMEM((1, FC_PAD), F32)],
        compiler_params=_params(),
    )(dc_q, dc_k, fc, bias)


def _in_proj_bwd(pieces, w_main_t, w_fc_t, x, g_pre, dxo, name, rider=None):
    n, d = x.shape
    tm = min(512, n)
    widths = [p.shape[1] for p, _ in pieces]
    offs = [o for _, o in pieces]
    np_ = len(pieces)

    def body(*refs):
        p_refs = refs[:np_]
        wt_ref, wf_ref, x_ref, g_ref, dxo_ref, dx_ref, dg_ref = refs[np_:]

        @pl.when(pl.program_id(0) == 0)
        def _():
            dg_ref[...] = jnp.zeros_like(dg_ref)

        dh = _dot(p_refs[-1][...], wf_ref[...])
        for pr, wd, off in zip(p_refs[:-1], widths[:-1], offs[:-1]):
            for j in range(0, wd, 512):
                jw = min(512, wd - j)
                dh = dh + _dot(pr[:, j:j + jw], wt_ref[off + j:off + j + jw, :])
        xv = x_ref[...]
        r = lax.rsqrt(jnp.mean(xv * xv, axis=-1, keepdims=True) + NORM_EPS)
        xh = xv * r
        dg_ref[...] += jnp.sum(dh * xh, axis=0, keepdims=True)
        dx_ref[...] = dxo_ref[...] + _rms_bwd(dh * g_ref[...], xh, r)

    row = pl.BlockSpec((tm, d), lambda i: (i, 0))
    return _call(
        body, name, (*[p for p, _ in pieces], w_main_t, w_fc_t, x, g_pre, dxo), rider, grid=(n // tm,),
        in_specs=[pl.BlockSpec((tm, wd), lambda i: (i, 0)) for wd in widths] + [
            pl.BlockSpec((MAIN_W, d), lambda i: (0, 0)), pl.BlockSpec((FC_PAD, d), lambda i: (0, 0)),
            row, pl.BlockSpec((1, d), lambda i: (0, 0)), row],
        out_specs=[row, pl.BlockSpec((1, d), lambda i: (0, 0))],
        out_shape=[_sds((n, d), F32), _sds((1, d), F32)],
        compiler_params=_params(),
    )


def _lower_bound_table(lower_bounds, name):
    depth, w = lower_bounds.shape

    def body(lb_ref, o_ref):
        v = lb_ref[...]
        e = jnp.exp(v - jnp.max(v, axis=0, keepdims=True))
        p = e / jnp.sum(e, axis=0, keepdims=True)
        acc = jnp.zeros((1, w), F32)
        for l in range(depth):
            acc = acc + p[l:l + 1, :]
            o_ref[l:l + 1, :] = acc - p[0:1, :]

    return _pc(body, name, out_shape=_sds((depth, w), F32))(lower_bounds)


def _lower_bound_bwd(lower_bounds, dlbs, name):
    depth, w = lower_bounds.shape

    def body(lb_ref, d_ref, o_ref):
        v, dl = lb_ref[...], d_ref[...]
        e = jnp.exp(v - jnp.max(v, axis=0, keepdims=True))
        p = e / jnp.sum(e, axis=0, keepdims=True)
        tot = jnp.sum(dl, axis=0, keepdims=True)
        rows, tail = [], tot
        for l in range(depth):
            rows.append(tail - tot if l == 0 else tail)
            tail = tail - dl[l:l + 1, :]
        dp = jnp.concatenate(rows, axis=0)
        o_ref[...] = p * (dp - jnp.sum(p * dp, axis=0, keepdims=True))

    return _pc(body, name, out_shape=_sds((depth, w), F32))(lower_bounds, dlbs)


def _place():
    x, y, c = lax.axis_index("x"), lax.axis_index("y"), lax.axis_index("c")
    return x, y, c


def _gather_weights(*arrays):
    na = len(arrays)

    def body(*refs):
        ins, outs = refs[:na], refs[na:2 * na]
        send_sems, recv_sems, local_sems = refs[2 * na:]
        x, y, c = _place()
        me, sibling = (x, y, c), (x, y, 1 - c)
        chips = [(1 - x, y), (x, 1 - y), (1 - x, 1 - y)]

        def slot(a, px, py, pc):
            return outs[a].at[4 * px + 2 * py + pc]

        def copy(a, k, block, to, own=False):
            return pltpu.make_async_remote_copy(
                src_ref=ins[a] if own else slot(a, *block), dst_ref=slot(a, *block),
                send_sem=send_sems.at[a * 7 + k], recv_sem=recv_sems.at[a * 7 + k],
                device_id=to, device_id_type=MESH)

        mine = [pltpu.make_async_copy(ins[a], slot(a, *me), local_sems.at[a]) for a in range(na)]
        for cp in mine:
            cp.start()
        first = []
        for a in range(na):
            first.append(copy(a, 0, me, sibling, own=True))
            first += [copy(a, 1 + j, me, (*chip, c), own=True) for j, chip in enumerate(chips)]
        for cp in first:
            cp.start()
        passed = []
        for j, chip in enumerate(chips):
            for a in range(na):
                copy(a, 1 + j, (*chip, c), me).wait_recv()
                fw = copy(a, 4 + j, (*chip, c), sibling)
                fw.start()
                passed.append(fw)
        for a in range(na):
            copy(a, 0, sibling, me).wait_recv()
            for j, chip in enumerate(chips):
                copy(a, 4 + j, (*chip, 1 - c), me).wait_recv()
        for cp in first + passed:
            cp.wait_send()
        for cp in mine:
            cp.wait()

    any_spec = pl.BlockSpec(memory_space=pl.ANY)
    return _pc(
        body, "gather_weights",
        in_specs=[any_spec] * na, out_specs=[any_spec] * na,
        out_shape=[_sds((N_DEV,) + a.shape, a.dtype) for a in arrays],
        scratch_shapes=[pltpu.SemaphoreType.DMA((7 * na,)), pltpu.SemaphoreType.DMA((7 * na,)),
                        pltpu.SemaphoreType.DMA((na,))],
    )(*arrays)


def _peer(k):
    x, y, c = _place()
    return (1 - x if k & 4 else x, 1 - y if k & 2 else y, 1 - c if k & 1 else c)


def _remote(src, dst, sems, s, to):
    return pltpu.make_async_remote_copy(src_ref=src, dst_ref=dst, send_sem=sems[0].at[s], recv_sem=sems[1].at[s],
                                        device_id=to, device_id_type=MESH)


def _gather_rider(shards):
    na = len(shards)

    def plan(ins, outs, *sems):
        x, y, c = _place()
        me = 4 * x + 2 * y + c
        locs = [pltpu.make_async_copy(ins[a], outs[a].at[me], sems[2].at[a]) for a in range(na)]
        sends, recvs = [], []
        for k in range(1, N_DEV):
            px, py, pc = _peer(k)
            for a in range(na):
                s = (k - 1) * na + a
                sends.append(_remote(ins[a], outs[a].at[me], sems, s, (px, py, pc)))
                recvs.append(_remote(ins[a], outs[a].at[4 * px + 2 * py + pc], sems, s, (px, py, pc)))
        return sends, recvs, locs

    return _Rider(shards, [_sds((N_DEV,) + a.shape, a.dtype) for a in shards], (N_DEV - 1) * na, na, plan)


def _direct_exchange_rider(blocks):
    na = len(blocks)

    def plan(ins, outs, *sems):
        x, y, c = _place()
        me = 4 * x + 2 * y + c
        locs = [pltpu.make_async_copy(ins[a].at[c, 2 * x + y], outs[a].at[me], sems[2].at[a]) for a in range(na)]
        sends, recvs = [], []
        for k in range(1, N_DEV):
            px, py, pc = _peer(k)
            for a in range(na):
                s = (k - 1) * na + a
                sends.append(_remote(ins[a].at[pc, 2 * px + py], outs[a].at[me], sems, s, (px, py, pc)))
                recvs.append(_remote(ins[a].at[pc, 2 * px + py], outs[a].at[4 * px + 2 * py + pc], sems, s, (px, py, pc)))
        return sends, recvs, locs

    return _Rider(blocks, [_sds((N_DEV,) + a.shape[2:], a.dtype) for a in blocks], (N_DEV - 1) * na, na, plan)


def _swap_rider(halves):
    na = len(halves)

    def plan(ins, outs, *sems):
        x, y, c = _place()
        cps = [_remote(ins[a].at[1 - c], outs[a], sems, a, (x, y, 1 - c)) for a in range(na)]
        return cps, cps, []

    return _Rider(halves, [_sds(a.shape[1:], a.dtype) for a in halves], na, 1, plan)


def _chip_exchange_rider(parts, small=None):
    na = len(parts)
    n_chip = N_DEV // 2

    def plan(ins, outs, *sems):
        x, y, c = _place()
        chip = 2 * x + y
        locs = [pltpu.make_async_copy(ins[a].at[chip], outs[a].at[chip], sems[2].at[a]) for a in range(na)]
        sends, recvs = [], []
        for k in range(1, n_chip):
            px, py, _ = _peer(2 * k)
            for a in range(na):
                s = (k - 1) * na + a
                sends.append(_remote(ins[a].at[2 * px + py], outs[a].at[chip], sems, s, (px, py, c)))
                recvs.append(_remote(ins[a].at[2 * px + py], outs[a].at[2 * px + py], sems, s, (px, py, c)))
        if small is not None:
            me = 2 * chip + c
            locs.append(pltpu.make_async_copy(ins[na], outs[na].at[me], sems[2].at[na]))
            for k in range(1, N_DEV):
                px, py, pc = _peer(k)
                s = (n_chip - 1) * na + k - 1
                sends.append(_remote(ins[na], outs[na].at[me], sems, s, (px, py, pc)))
                recvs.append(_remote(ins[na], outs[na].at[4 * px + 2 * py + pc], sems, s, (px, py, pc)))
        return sends, recvs, locs

    extra = [] if small is None else [small]
    shapes = [_sds(a.shape, a.dtype) for a in parts] + [_sds((N_DEV,) + s.shape, s.dtype) for s in extra]
    n_sems = (n_chip - 1) * na + (N_DEV - 1) * len(extra)
    return _Rider(list(parts) + extra, shapes, n_sems, na + len(extra), plan)


def _pair_add(halves, other, core, name):
    _, nch, r, c = halves.shape
    tr = 256 if r % 256 == 0 else r

    def body(h_ref, o_ref, c_ref, p_ref):
        mine = jnp.where(c_ref[...] == 0, h_ref[0].astype(F32), h_ref[1].astype(F32))
        p_ref[...] = (mine + o_ref[...].astype(F32)).astype(BF16)

    blk = pl.BlockSpec((None, tr, c), lambda j, i: (j, i, 0))
    return _pc(
        body, name, grid=(nch, r // tr),
        in_specs=[pl.BlockSpec((2, None, tr, c), lambda j, i: (0, j, i, 0)), blk, pl.BlockSpec((1, 1), lambda j, i: (0, 0))],
        out_specs=blk, out_shape=_sds((nch, r, c), BF16),
        compiler_params=_params(),
    )(halves, other, core)


def _sum_adamw(parts, w, m, v, name, rider=None):
    nl, r, c = w.shape
    tr = 256 if r % 256 == 0 else r

    def body(*refs):
        p_refs = refs[:nl]
        w_ref, m_ref, v_ref, g_ref, d_ref, mo_ref, vo_ref = refs[nl:]
        for l in range(nl):
            @pl.when(pl.program_id(0) == l)
            def _(p_ref=p_refs[l]):
                g = p_ref[0].astype(F32)
                for j in range(1, p_ref.shape[0]):
                    g = g + p_ref[j].astype(F32)
                mn = ADAM_B1 * m_ref[...] + (1.0 - ADAM_B1) * g
                vn = ADAM_B2 * v_ref[...] + (1.0 - ADAM_B2) * (g * g)
                m_hat = mn / (1.0 - ADAM_B1 ** ADAM_STEP)
                v_hat = vn / (1.0 - ADAM_B2 ** ADAM_STEP)
                g_ref[...] = g
                d_ref[...] = -ADAM_LR * (m_hat / (jnp.sqrt(v_hat) + ADAM_EPS) + ADAM_WD * w_ref[...])
                mo_ref[...] = mn
                vo_ref[...] = vn

    def part_spec(l, k):
        return pl.BlockSpec((k, tr, c), lambda li, i: (0, jnp.where(li == l, i, 0), 0))

    row = pl.BlockSpec((None, tr, c), lambda li, i: (li, i, 0))
    return _call(
        body, name, (*parts, w, m, v), rider, grid=(nl, r // tr),
        in_specs=[part_spec(l, p.shape[0]) for l, p in enumerate(parts)] + [row, row, row],
        out_specs=[row] * 4,
        out_shape=[_sds((nl, r, c), F32)] * 4,
        compiler_params=_params(),
    )


SMALL = ("lower_bounds", "pre_norm_g", "hgrn_norm_g", "fox_f_bias", "pool_w", "pool_scale", "post_norm_g")
SMALL_LANES = 128


def _small_size(tree):
    return sum(tree[k].size for k in SMALL)


def _pack_small(tree, extra=None):
    flat = jnp.concatenate([tree[k].reshape(-1) for k in SMALL] + ([] if extra is None else [extra.reshape(1)]))
    rows = -(-(_small_size(tree) + 1) // (8 * SMALL_LANES)) * 8
    return jnp.pad(flat, (0, rows * SMALL_LANES - flat.shape[0])).reshape(rows, SMALL_LANES)


def _unpack_small(packed, like):
    flat, out, off = packed.reshape(-1), {}, 0
    for k in SMALL:
        size = like[k].size
        out[k] = flat[off:off + size].reshape(like[k].shape)
        off += size
    return out


def _block_diag(pw):
    g = pw.shape[0]
    eye = jnp.eye(g, dtype=pw.dtype)
    return (eye[:, None, :, None] * pw[:, :, None, :]).reshape(g * HEAD, g * HEAD)


def _assemble_w_in(g_in, name):
    _, d, shard = g_in.shape
    tr = min(256, d)
    wide = MAIN_W + FC_PAD

    def body(g_ref, wm_ref, wf_ref, wmt_ref, wft_ref, row_s):
        row_s[:, MAIN_W:] = jnp.zeros((tr, FC_PAD), F32)
        for j in range(N_DEV):
            row_s[:, j * shard:(j + 1) * shard] = g_ref[j].astype(F32)
        wm_ref[...] = row_s[:, :MAIN_W].astype(BF16)
        wf_ref[...] = row_s[:, MAIN_W:].astype(BF16)
        for j in range(0, MAIN_W, 512):
            wmt_ref[j:j + 512, :] = row_s[:, j:j + 512].T.astype(BF16)
        wft_ref[...] = row_s[:, MAIN_W:].T.astype(BF16)

    return _pc(
        body, name, grid=(d // tr,),
        in_specs=[pl.BlockSpec((N_DEV, tr, shard), lambda i: (0, i, 0))],
        out_specs=[pl.BlockSpec((tr, MAIN_W), lambda i: (i, 0)), pl.BlockSpec((tr, FC_PAD), lambda i: (i, 0)),
                   pl.BlockSpec((MAIN_W, tr), lambda i: (0, i)), pl.BlockSpec((FC_PAD, tr), lambda i: (0, i))],
        out_shape=[_sds((d, MAIN_W), BF16), _sds((d, FC_PAD), BF16), _sds((MAIN_W, d), BF16), _sds((FC_PAD, d), BF16)],
        scratch_shapes=[pltpu.VMEM((tr, wide), F32)],
        compiler_params=_params(),
    )(g_in)


def _w_out_parts(g_out):
    full_out = g_out.reshape(N_DEV * g_out.shape[1], g_out.shape[2])
    return full_out, full_out.T


def _layer_fwd(l, x, lbs, weights, lw, nb, rider_h=None, rider_c=None, target=None):
    n = x.shape[0]
    t = n // nb
    w_main, w_fc, _, _, w_out, _ = lw
    bias = jnp.pad(weights["fox_f_bias"][l:l + 1], ((0, 0), (0, FC_PAD - FOX_HEADS)))
    wbd = _block_diag(weights["pool_w"][l]).astype(BF16)
    proj, fc, ht = _in_proj_fwd(x, weights["pre_norm_g"][l:l + 1], w_main, w_fc, f"in_proj_fwd_{l}")
    c_col = _fox_decay_fwd(fc, bias, nb, f"fox_decay_fwd_{l}")
    c_row = c_col.reshape(nb, t, FC_PAD)[:, :, :FOX_HEADS].transpose(0, 2, 1)
    (o_h, s0), rode_h = _hgrn_fwd(proj, lbs[l:l + 1], _block_ones(HGRN_W, BF16), nb, f"hgrn_fwd_{l}", rider_h)
    o_b = _pool_fwd(proj, wbd, weights["pool_scale"][l:l + 1], nb, f"pool_fwd_{l}")
    (o_c, lse), rode_c = _fox_fwd(proj, c_col, c_row, nb, f"fox_fwd_{l}", rider_c)
    if w_out is None:
        lw = tuple(lw[:4]) + _w_out_parts(rode_h[0])
        w_out, rode_h = lw[4], rode_h[1:]
    x_next, mixt, y = _merge_fwd(x, proj, o_h, o_b, o_c, weights["hgrn_norm_g"][l:l + 1], w_out,
                                 weights["post_norm_g"][l:l + 1], f"merge_fwd_{l}", target)
    return x_next, (x, proj, fc, ht, c_col, c_row, o_h, s0, o_c, lse, mixt, y, bias, wbd), lw, (rode_h, rode_c)


def _layer_bwd(l, dx, saved, lbs, weights, lw, nb, rider=None):
    _, proj, fc, ht, c_col, c_row, o_h, s0, o_c, lse, mixt, y, bias, wbd = saved
    n = proj.shape[0]
    t = n // nb
    w_out_t = lw[5]
    g = {}
    dy, dmix, dgp, da, d_gc, delta = _merge_bwd(dx, y, weights["post_norm_g"][l:l + 1], w_out_t, proj, o_c,
                                                f"merge_bwd_{l}")
    g["post_norm_g"] = dgp[0]
    g["w_out"] = _w_out_grad(mixt, dy, f"w_out_grad_{l}")
    d_a, dgh, dlb = _hgrn_bwd(dmix, proj, o_h, s0, weights["hgrn_norm_g"][l:l + 1], lbs[l:l + 1],
                              _block_ones(HGRN_W, BF16), nb, f"hgrn_bwd_{l}")
    g["hgrn_norm_g"], g["lbs"] = dgh[0], dlb[0]
    d_b, dwbd, dps = _pool_bwd(dmix, proj, wbd, wbd.T, weights["pool_scale"][l:l + 1], nb, f"pool_bwd_{l}")
    g["pool_w"] = jnp.stack([dwbd[j * HEAD:(j + 1) * HEAD, j * HEAD:(j + 1) * HEAD] for j in range(len(POOL_WINDOWS))])
    g["pool_scale"] = dps[0]
    delta_row, lse_row = [a.reshape(nb, t, FC_PAD)[:, :, :FOX_HEADS].transpose(0, 2, 1) for a in (delta, lse)]
    (d_qc, d_kc, d_vc, dc_k, dc_q), rode = _fox_bwd(proj, da, c_col, c_row, lse_row, delta_row, nb, f"fox_bwd_{l}", rider)
    dc_q = jnp.pad(dc_q.transpose(0, 2, 1).reshape(n, FOX_HEADS), ((0, 0), (0, FC_PAD - FOX_HEADS)))
    d_fc, dbias = _fox_decay_bwd(dc_q, dc_k, fc, bias, nb, f"fox_decay_bwd_{l}")
    g["fox_f_bias"] = dbias[0, :FOX_HEADS]
    pieces = [(d_a, C_QA), (d_b, C_UB), (d_qc, C_QC), (d_kc, C_KC), (d_vc, C_VC), (d_gc, C_GC), (d_fc, None)]
    g["w_in"] = _w_in_grad(ht, pieces, f"w_in_grad_{l}")
    return g, pieces, rode


def _layer_bwd_input(l, dx, pieces, saved, weights, lw, rider=None):
    (dxi, dgpre), rode = _in_proj_bwd(pieces, lw[2], lw[3], saved[0], weights["pre_norm_g"][l:l + 1], dx,
                                      f"in_proj_bwd_{l}", rider)
    return dxi, dgpre[0], rode


def kernel(x, lower_bounds, pre_norm_g, w_in, hgrn_norm_g, fox_f_bias, pool_w, pool_scale, w_out, post_norm_g, loss_target, m_lower_bounds, m_pre_norm_g, m_w_in, m_hgrn_norm_g, m_fox_f_bias, m_pool_w, m_pool_scale, m_w_out, m_post_norm_g, v_lower_bounds, v_pre_norm_g, v_w_in, v_hgrn_norm_g, v_fox_f_bias, v_pool_w, v_pool_scale, v_w_out, v_post_norm_g):
    weights = dict(lower_bounds=lower_bounds, pre_norm_g=pre_norm_g, hgrn_norm_g=hgrn_norm_g, fox_f_bias=fox_f_bias,
                   pool_w=pool_w, pool_scale=pool_scale, post_norm_g=post_norm_g)
    mom_m = dict(lower_bounds=m_lower_bounds, pre_norm_g=m_pre_norm_g, hgrn_norm_g=m_hgrn_norm_g, fox_f_bias=m_fox_f_bias,
                 pool_w=m_pool_w, pool_scale=m_pool_scale, post_norm_g=m_post_norm_g)
    mom_v = dict(lower_bounds=v_lower_bounds, pre_norm_g=v_pre_norm_g, hgrn_norm_g=v_hgrn_norm_g, fox_f_bias=v_fox_f_bias,
                 pool_w=v_pool_w, pool_scale=v_pool_scale, post_norm_g=v_post_norm_g)
    depth = w_in.shape[0]
    nb, t, d = x.shape
    n = nb * t
    core = lax.axis_index("c").astype(jnp.int32).reshape(1, 1)
    shards = [(w_in[l].astype(BF16), w_out[l].astype(BF16)) for l in range(depth)]
    lbs = _lower_bound_table(lower_bounds, "lower_bound_table")

    (g_in,) = _gather_weights(shards[0][0])
    coming = tuple(_assemble_w_in(g_in, "assemble_w_in_0")) + (None, None)
    xl, saved, lw = x.reshape(n, d), [], []
    for l in range(depth):
        last = l + 1 == depth
        ride_h = ([shards[l][1]] if coming[4] is None else []) + ([] if last else [shards[l + 1][1]])
        xl, sv, lw_l, (rode_h, rode_c) = _layer_fwd(
            l, xl, lbs, weights, coming, nb, _gather_rider(ride_h) if ride_h else None,
            None if last else _gather_rider([shards[l + 1][0]]), loss_target.reshape(n, d) if last else None)
        saved.append(sv)
        lw.append(lw_l)
        if not last:
            coming = tuple(_assemble_w_in(rode_c[0], f"assemble_w_in_{l + 1}")) + _w_out_parts(rode_h[0])
    dx, sq = xl
    loss_here = 0.5 * jnp.sum(sq) / d

    grads, recv, pending = [None] * depth, [None] * depth, None
    for l in reversed(range(depth)):
        g, pieces, rode = _layer_bwd(l, dx, saved[l], lbs, weights, lw[l], nb, pending)
        if rode is not None:
            recv[l + 1] = rode
        blocks = (g["w_in"], g["w_out"])
        if l > 0:
            pending = _direct_exchange_rider(blocks)
            dx, g["pre_norm_g"], _ = _layer_bwd_input(l, dx, pieces, saved[l], weights, lw[l])
        else:
            other = _run_rider(_swap_rider(blocks), "grad_swap")
            summed = [_pair_add(hv, ot, core, f"grad_pair_add_{i}") for i, (hv, ot) in enumerate(zip(blocks, other))]
            dx, g["pre_norm_g"], recv[l] = _layer_bwd_input(l, dx, pieces, saved[l], weights, lw[l],
                                                            _chip_exchange_rider(summed))
        grads[l] = g
    small = {k: jnp.stack([grads[l][k] for l in range(depth)]) for k in SMALL if k != "lower_bounds"}
    small["lower_bounds"] = _lower_bound_bwd(lower_bounds, jnp.stack([grads[l]["lbs"] for l in range(depth)]),
                                             "lower_bound_bwd")
    (r_small,) = _run_rider(_gather_rider([_pack_small(small, loss_here)]), "small_grads_gather")

    res_in, _ = _sum_adamw([recv[l][0] for l in range(depth)], w_in, m_w_in, v_w_in, "adamw_w_in")
    res_out, _ = _sum_adamw([recv[l][1] for l in range(depth)], w_out, m_w_out, v_w_out, "adamw_w_out")
    res_small, _ = _sum_adamw([r_small], _pack_small(weights)[None], _pack_small(mom_m)[None], _pack_small(mom_v)[None],
                              "adamw_small")
    loss = res_small[0][0].reshape(-1)[_small_size(weights)]

    names = ("lower_bounds", "pre_norm_g", "w_in", "hgrn_norm_g", "fox_f_bias", "pool_w", "pool_scale", "w_out", "post_norm_g")
    outs = [loss, dx.reshape(nb, t, d)]
    for i in range(4):
        full = dict(_unpack_small(res_small[i][0], weights), w_in=res_in[i], w_out=res_out[i])
        outs += [full[k] for k in names]
    return tuple(outs)
```

```python
import functools

import jax
import jax.numpy as jnp
from jax import lax
from jax.experimental import pallas as pl
from jax.experimental.pallas import tpu as pltpu

F32, BF16 = jnp.float32, jnp.bfloat16
HI = lax.Precision.HIGHEST
MESH = pl.DeviceIdType.MESH
AXES = ("x", "y", "c")
N_DEV = 8

NORM_EPS = 1e-6
MASK_VALUE = -1e30
TINY = 1e-30
CHUNK = 64
SUB = 16
HGRN_W, POOL_W, FOX_W = 256, 256, 512
HEAD = 64
FOX_HEADS = 8
POOL_WINDOWS = (2, 4, 8, 16)
POOL_HALO = 16
MAIN_W = 3584
FC_PAD = 128
C_QA, C_FA, C_IA, C_GA, C_UB, C_GB, C_QC, C_KC, C_VC, C_GC = 0, 256, 512, 768, 1024, 1280, 1536, 2048, 2560, 3072
FOX_SCALE = HEAD ** -0.5

ADAM_LR, ADAM_B1, ADAM_B2, ADAM_EPS, ADAM_WD, ADAM_STEP = 0.001, 0.9, 0.999, 1e-08, 0.01, 10

VMEM_LIMIT = 56 * 1024 * 1024


def _pc(fn, name, **kw):
    return pl.pallas_call(fn, name=name, **kw)


def _params(**kw):
    return pltpu.CompilerParams(vmem_limit_bytes=VMEM_LIMIT, **kw)


class _Rider:
    def __init__(self, inputs, out_shapes, n_sems, n_local, plan):
        self.inputs, self.out_shapes, self.n_sems, self.n_local, self.plan = list(inputs), list(out_shapes), n_sems, n_local, plan

    def start(self, ins, outs, *sems):
        sends, _, locs = self.plan(ins, outs, *sems)
        for cp in locs + sends:
            cp.start()

    def wait(self, ins, outs, *sems):
        sends, recvs, locs = self.plan(ins, outs, *sems)
        for cp in recvs:
            cp.wait_recv()
        for cp in sends:
            cp.wait_send()
        for cp in locs:
            cp.wait()

    def sem_shapes(self):
        return [pltpu.SemaphoreType.DMA((self.n_sems,)), pltpu.SemaphoreType.DMA((self.n_sems,)),
                pltpu.SemaphoreType.DMA((self.n_local,))]


def _call(body, name, args, rider=None, *, grid, in_specs, out_specs, out_shape, scratch_shapes=(), **kw):
    if rider is None:
        res = _pc(body, name, grid=grid, in_specs=in_specs, out_specs=out_specs, out_shape=out_shape,
                  scratch_shapes=list(scratch_shapes), **kw)(*args)
        return res, None
    n_in, n_out, n_scr = len(in_specs), len(out_specs), len(scratch_shapes)
    n_rin, n_rout = len(rider.inputs), len(rider.out_shapes)

    def ridden(*refs):
        ins, refs = refs[:n_in], refs[n_in:]
        rins, refs = refs[:n_rin], refs[n_rin:]
        outs, refs = refs[:n_out], refs[n_out:]
        routs, refs = refs[:n_rout], refs[n_rout:]
        scr, sems = refs[:n_scr], refs[n_scr:]
        first = functools.reduce(jnp.logical_and, [pl.program_id(a) == 0 for a in range(len(grid))])
        last = functools.reduce(jnp.logical_and, [pl.program_id(a) == g - 1 for a, g in enumerate(grid)])

        @pl.when(first)
        def _():
            rider.start(rins, routs, *sems)

        body(*ins, *outs, *scr)

        @pl.when(last)
        def _():
            rider.wait(rins, routs, *sems)

    any_spec = pl.BlockSpec(memory_space=pl.ANY)
    res = _pc(ridden, name, grid=grid, in_specs=list(in_specs) + [any_spec] * n_rin,
              out_specs=list(out_specs) + [any_spec] * n_rout, out_shape=list(out_shape) + rider.out_shapes,
              scratch_shapes=list(scratch_shapes) + rider.sem_shapes(), **kw)(*args, *rider.inputs)
    return res[:n_out], res[n_out:]


def _run_rider(rider, name):
    n_rin = len(rider.inputs)

    def body(*refs):
        ins, outs, sems = refs[:n_rin], refs[n_rin:n_rin + len(rider.out_shapes)], refs[n_rin + len(rider.out_shapes):]
        rider.start(ins, outs, *sems)
        rider.wait(ins, outs, *sems)

    any_spec = pl.BlockSpec(memory_space=pl.ANY)
    return _pc(body, name, in_specs=[any_spec] * n_rin, out_specs=[any_spec] * len(rider.out_shapes),
               out_shape=rider.out_shapes, scratch_shapes=rider.sem_shapes())(*rider.inputs)


def _dot(a, b):
    return jnp.dot(a, b, preferred_element_type=F32)


def _dot_nt(a, b):
    return lax.dot_general(a, b, (((1,), (1,)), ((), ())), preferred_element_type=F32)


def _dot_tn(a, b):
    return lax.dot_general(a, b, (((0,), (0,)), ((), ())), preferred_element_type=F32)


def _dot_hi(a, b):
    return jnp.dot(a, b, precision=HI, preferred_element_type=F32)


def _split2(x):
    hi = x.astype(BF16)
    return hi, (x - hi.astype(F32)).astype(BF16)


def _sel_dot(sel, x):
    hi, lo = _split2(x)
    sb = sel.astype(BF16)
    return _dot(sb, hi) + _dot(sb, lo)


def _dot_sel(x, sel):
    hi, lo = _split2(x)
    sb = sel.astype(BF16)
    return _dot(hi, sb) + _dot(lo, sb)


def _sigmoid(x):
    return 1.0 / (1.0 + jnp.exp(-x))


def _block_ones(n, dtype):
    r = lax.broadcasted_iota(jnp.int32, (n, n), 0) // HEAD
    c = lax.broadcasted_iota(jnp.int32, (n, n), 1) // HEAD
    return (r == c).astype(dtype)


def _sds(shape, dtype):
    return jax.ShapeDtypeStruct(shape, dtype)


def _in_proj_fwd(x, g_pre, w_main, w_fc, name):
    n, d = x.shape
    tm = min(512, n)

    def body(x_ref, g_ref, w_ref, wf_ref, proj_ref, fc_ref, ht_ref, qt_ref, kt_ref):
        xv = x_ref[...]
        r = lax.rsqrt(jnp.mean(xv * xv, axis=-1, keepdims=True) + NORM_EPS)
        hf = xv * r * g_ref[...]
        hb = hf.astype(BF16)
        ht_ref[...] = hf.T.astype(BF16)
        for j in range(0, MAIN_W, FOX_W):
            res = _dot(hb, w_ref[:, j:j + FOX_W])
            proj_ref[:, j:j + FOX_W] = res
            if j == C_QC:
                qt_ref[...] = (res * FOX_SCALE).T.astype(BF16)
            if j == C_KC:
                kt_ref[...] = res.T.astype(BF16)
        fc_ref[...] = _dot(hb, wf_ref[...])

    def cols(rows):
        return pl.BlockSpec((rows, tm), lambda i: (0, i))

    return _pc(
        body, name, grid=(n // tm,),
        in_specs=[pl.BlockSpec((tm, d), lambda i: (i, 0)), pl.BlockSpec((1, d), lambda i: (0, 0)),
                  pl.BlockSpec((d, MAIN_W), lambda i: (0, 0)), pl.BlockSpec((d, FC_PAD), lambda i: (0, 0))],
        out_specs=[pl.BlockSpec((tm, MAIN_W), lambda i: (i, 0)), pl.BlockSpec((tm, FC_PAD), lambda i: (i, 0)),
                   cols(d), cols(FOX_W), cols(FOX_W)],
        out_shape=[_sds((n, MAIN_W), F32), _sds((n, FC_PAD), F32), _sds((d, n), BF16), _sds((FOX_W, n), BF16),
                   _sds((FOX_W, n), BF16)],
        compiler_params=_params(),
    )(x, g_pre, w_main, w_fc)


def _fox_decay_fwd(fc, bias, nb, name):
    n = fc.shape[0]
    t = n // nb
    tt = min(256, t)
    nt = t // tt

    def body(fc_ref, b_ref, c_ref, carry):
        i = pl.program_id(1)

        @pl.when(i == 0)
        def _():
            carry[...] = jnp.zeros_like(carry)

        xv = fc_ref[...] + b_ref[...]
        lf = jnp.minimum(xv, 0.0) - jnp.log(1.0 + jnp.exp(-jnp.abs(xv)))
        r = lax.broadcasted_iota(jnp.int32, (tt, tt), 0)
        cc = lax.broadcasted_iota(jnp.int32, (tt, tt), 1)
        cs = _dot_hi((r >= cc).astype(F32), lf) + carry[...]
        c_ref[...] = cs
        carry[...] = cs[tt - 1:tt, :]

    return _pc(
        body, name, grid=(nb, nt),
        in_specs=[pl.BlockSpec((tt, FC_PAD), lambda b, i: (b * nt + i, 0)), pl.BlockSpec((1, FC_PAD), lambda b, i: (0, 0))],
        out_specs=pl.BlockSpec((tt, FC_PAD), lambda b, i: (b * nt + i, 0)),
        out_shape=_sds((n, FC_PAD), F32),
        scratch_shapes=[pltpu.VMEM((1, FC_PAD), F32)],
        compiler_params=_params(),
    )(fc, bias)


def _hgrn_gates(q, z, lb):
    sig = _sigmoid(z)
    sn = _sigmoid(-z)
    f = lb + (1.0 - lb) * sig
    g = jnp.log(jnp.maximum(f, TINY))
    k = (1.0 - lb) * sn
    sq = _sigmoid(q)
    return sig, sn, f, g, k, sq


def _sub_tri(n, lower):
    r = lax.broadcasted_iota(jnp.int32, (n, n), 0)
    c = lax.broadcasted_iota(jnp.int32, (n, n), 1)
    tri = (r >= c) if lower else (r <= c)
    return jnp.logical_and(r // SUB == c // SUB, tri).astype(F32)


def _live_rows(t):
    return 8 * (t // 8 + 1)


def _pad_rows(x):
    return x if x.shape[0] == SUB else jnp.concatenate([x, jnp.zeros((SUB - x.shape[0], x.shape[1]), x.dtype)], axis=0)


def _hgrn_decays(qs, k, b):
    srow = lax.broadcasted_iota(jnp.int32, (SUB, HGRN_W), 0)
    es, ws = [], []
    for t in range(SUB):
        r = _live_rows(t)
        e = jnp.where(srow[:r] <= t, jnp.exp(b[t:t + 1, :] - b[:r]), 0.0)
        es.append(e)
        ws.append(_pad_rows(e * (qs[t:t + 1, :] * k[:r])))
    return srow, es, ws


def _hgrn_state_step(st, k, v, b, bmask):
    bl = b[SUB - 1:SUB, :]
    ktil = k * jnp.exp(bl - b)
    return st * jnp.exp(bl) + _dot_tn(v.astype(BF16), ktil.astype(BF16)) * bmask


def _hgrn_tile(t):
    return min(256, t)


def _hgrn_fwd(proj, lb, ones_b, nb, name, rider=None):
    n = proj.shape[0]
    t = n // nb
    tt = _hgrn_tile(t)
    nt = t // tt
    ncs = tt // CHUNK
    w = HGRN_W

    def body(q_ref, z_ref, v_ref, lb_ref, ones_ref, o_ref, s0_ref, st_s, b_s, qs_s, k_s):
        @pl.when(pl.program_id(1) == 0)
        def _():
            st_s[...] = jnp.zeros_like(st_s)

        q = q_ref[...]
        _, _, _, g, k, sq = _hgrn_gates(q, z_ref[...], lb_ref[...])
        b_s[...] = _sel_dot(_sub_tri(tt, True), g)
        qs_s[...] = q * sq
        k_s[...] = k
        bmask = _block_ones(w, F32)
        ones_b = ones_ref[...]

        def chunk(c, carry):
            st = st_s[...]
            s0_ref[c] = st
            base = pl.multiple_of(c * CHUNK, CHUNK)
            tiles = []
            for u in range(CHUNK // SUB):
                rows = pl.ds(base + u * SUB, SUB)
                tiles.append((qs_s[rows, :], k_s[rows, :], v_ref[rows, :], b_s[rows, :]))
            aexps = []
            for qs, k, v, b in tiles:
                _, _, ws = _hgrn_decays(qs, k, b)
                aexps.append(_dot(jnp.concatenate(ws, axis=0).astype(BF16), ones_b))
            inters = []
            for qs, k, v, b in tiles:
                inters.append(_dot_nt((qs * jnp.exp(b)).astype(BF16), st.astype(BF16)))
                st = _hgrn_state_step(st, k, v, b, bmask)
            st_s[...] = st
            for u, ((qs, k, v, b), aexp, o) in enumerate(zip(tiles, aexps, inters)):
                for t in range(SUB):
                    r = _live_rows(t)
                    row = o[t:t + 1, :] + jnp.sum(aexp[t * SUB:t * SUB + r, :] * v[:r], axis=0, keepdims=True)
                    o_ref[pl.ds(base + u * SUB + t, 1), :] = row
            return carry

        lax.fori_loop(0, ncs, chunk, 0)

    def col(j):
        return pl.BlockSpec((tt, w), lambda b, i: (b * nt + i, j))

    return _call(
        body, name, (proj, proj, proj, lb, ones_b), rider, grid=(nb, nt),
        in_specs=[col(C_QA // w), col(C_FA // w), col(C_IA // w), pl.BlockSpec((1, w), lambda b, i: (0, 0)),
                  pl.BlockSpec((w, w), lambda b, i: (0, 0))],
        out_specs=[pl.BlockSpec((tt, w), lambda b, i: (b * nt + i, 0)),
                   pl.BlockSpec((ncs, w, w), lambda b, i: (b * nt + i, 0, 0))],
        out_shape=[_sds((n, w), F32), _sds((n // CHUNK, w, w), F32)],
        scratch_shapes=[pltpu.VMEM((w, w), F32)] + [pltpu.VMEM((tt, w), F32)] * 3,
        compiler_params=_params(),
    )


def _pool_lane_windows():
    lane = lax.broadcasted_iota(jnp.int32, (1, POOL_W), 1) // HEAD
    wl = jnp.zeros((1, POOL_W), F32)
    for gi, win in enumerate(POOL_WINDOWS):
        wl = jnp.where(lane == gi, float(win), wl)
    return lane, wl


def _pool_select(lane, parts):
    out = parts[-1]
    for gi in range(len(parts) - 2, -1, -1):
        out = jnp.where(lane == gi, parts[gi], out)
    return out


def _pool_mix(u, halo, t0, tt):
    lane, wl = _pool_lane_windows()
    ext = jnp.concatenate([halo, u], axis=0)
    sums, cur, shift = [], ext, 1
    for _ in POOL_WINDOWS:
        cur = cur + pltpu.roll(cur, shift, axis=0)
        sums.append(cur[POOL_HALO:, :])
        shift *= 2
    tpos = (t0 + lax.broadcasted_iota(jnp.int32, (tt, POOL_W), 0)).astype(F32)
    cnt = jnp.minimum(tpos + 1.0, wl)
    return _pool_select(lane, sums) / cnt - u, cnt


def _pool_specs(tt, nt, nhb):
    cu, cg = C_UB // POOL_W, C_GB // POOL_W
    return [pl.BlockSpec((tt, POOL_W), lambda b, i: (b * nt + i, cu)),
            pl.BlockSpec((tt, POOL_W), lambda b, i: (b * nt + i, cg)),
            pl.BlockSpec((POOL_HALO, POOL_W), lambda b, i: (jnp.maximum((b * nt + i) * nhb - 1, 0), cu))]


def _pool_fwd(proj, wbd, scale, nb, name):
    n = proj.shape[0]
    t = n // nb
    tt = min(256, t)
    nt = t // tt
    nhb = tt // POOL_HALO

    def body(u_ref, g_ref, h_ref, w_ref, s_ref, o_ref):
        i = pl.program_id(1)
        halo = jnp.where(i == 0, 0.0, h_ref[...])
        pooled, _ = _pool_mix(u_ref[...], halo, i * tt, tt)
        gv = g_ref[...]
        o_ref[...] = _dot(pooled.astype(BF16), w_ref[...]) * s_ref[...] * (gv * _sigmoid(gv))

    return _pc(
        body, name, grid=(nb, nt),
        in_specs=_pool_specs(tt, nt, nhb) + [pl.BlockSpec((POOL_W, POOL_W), lambda b, i: (0, 0)),
                                             pl.BlockSpec((1, POOL_W), lambda b, i: (0, 0))],
        out_specs=pl.BlockSpec((tt, POOL_W), lambda b, i: (b * nt + i, 0)),
        out_shape=_sds((n, POOL_W), F32),
        compiler_params=_params(),
    )(proj, proj, proj, wbd, scale)


def _rows_reduce(x, op, final):
    while x.shape[0] > 8 and x.shape[0] % 16 == 0:
        half = x.shape[0] // 2
        x = op(x[:half], x[half:])
    return final(x, axis=0, keepdims=True)


def _tri_pair(step, n):
    a = sum([(step >= r * (r + 1) // 2).astype(jnp.int32) for r in range(1, n)], jnp.int32(0))
    return a, step - a * (a + 1) // 2


def _lane_lo():
    return lax.broadcasted_iota(jnp.int32, (1, 2 * HEAD), 1) < HEAD


def _put_col(tile, hh, colv):
    lane = lax.broadcasted_iota(jnp.int32, tile.shape, 1)
    return jnp.where(lane == hh, colv, tile)


def _fox_fwd(proj, kt, c_col, c_row, nb, name, rider=None):
    n = proj.shape[0]
    t = n // nb
    tb = min(256, t)
    nq = t // tb
    pw = 2 * HEAD

    def body(q_ref, kt_ref, v_ref, cc_ref, cr_ref, o_ref, lse_ref, m_s, acc_s, cq_s):
        qi, kj = _tri_pair(pl.program_id(1), nq)

        @pl.when(kj == 0)
        def _():
            m_s[...] = jnp.full_like(m_s, -jnp.inf)
            acc_s[...] = jnp.zeros_like(acc_s)
            for hh in range(FOX_HEADS):
                cq_s[hh] = jnp.broadcast_to(cc_ref[:, hh:hh + 1], (tb, pw))

        def block(masked):
            lo = _lane_lo()
            if masked:
                causal = lax.broadcasted_iota(jnp.int32, (tb, tb), 0) >= lax.broadcasted_iota(jnp.int32, (tb, tb), 1)
            def lanes(hh):
                return lo if hh % 2 == 0 else jnp.logical_not(lo)

            def scores(hh):
                sl = slice((hh // 2) * pw, (hh // 2 + 1) * pw)
                return _dot(jnp.where(lanes(hh), q_ref[:, sl] * FOX_SCALE, 0.0).astype(BF16), kt_ref[sl, :])

            ahead = scores(0)
            for hh in range(FOX_HEADS):
                s = ahead
                if hh + 1 < FOX_HEADS:
                    ahead = scores(hh + 1)
                s = s + (jnp.tile(cq_s[hh], (1, tb // pw)) - cr_ref[hh:hh + 1, :])
                if masked:
                    s = jnp.where(causal, s, MASK_VALUE)
                m_prev = m_s[hh]
                m_new = jnp.maximum(m_prev, jnp.max(s, axis=1, keepdims=True))
                alpha = jnp.exp(m_prev - m_new)
                pe = jnp.exp(s - jnp.tile(m_new, (1, tb // pw)))
                m_s[hh] = m_new
                vf = v_ref[:, (hh // 2) * pw:(hh // 2 + 1) * pw]
                acc_s[hh] = alpha * acc_s[hh] + _dot(pe.astype(BF16), jnp.where(lanes(hh), vf, 1.0).astype(BF16))

        @pl.when(kj < qi)
        def _():
            block(False)

        @pl.when(kj == qi)
        def _():
            block(True)
            lo = _lane_lo()
            lse = jnp.zeros((tb, FC_PAD), F32)
            for p in range(FOX_HEADS // 2):
                halves = []
                for h in range(2):
                    hh = 2 * p + h
                    lm = lo if h == 0 else jnp.logical_not(lo)
                    acc = acc_s[hh]
                    swapped = pltpu.roll(acc, HEAD, axis=1)
                    halves.append(acc / swapped)
                    lse = _put_col(lse, hh, m_s[hh] + jnp.log(jnp.where(lm, swapped, acc)))
                o_ref[:, p * pw:(p + 1) * pw] = jnp.where(lo, halves[0], halves[1])
            lse_ref[...] = lse

    def qspec(wd, j):
        return pl.BlockSpec((tb, wd), lambda b, st: (b * nq + _tri_pair(st, nq)[0], j))

    def kspec(j):
        return pl.BlockSpec((tb, FOX_W), lambda b, st: (b * nq + _tri_pair(st, nq)[1], j))

    return _call(
        body, name, (proj, kt, proj, c_col, c_row), rider, grid=(nb, nq * (nq + 1) // 2),
        in_specs=[qspec(FOX_W, C_QC // FOX_W), pl.BlockSpec((FOX_W, tb), lambda b, st: (0, b * nq + _tri_pair(st, nq)[1])),
                  kspec(C_VC // FOX_W), qspec(FC_PAD, 0),
                  pl.BlockSpec((None, FOX_HEADS, tb), lambda b, st: (b, 0, _tri_pair(st, nq)[1]))],
        out_specs=[qspec(FOX_W, 0), qspec(FC_PAD, 0)],
        out_shape=[_sds((n, FOX_W), F32), _sds((n, FC_PAD), F32)],
        scratch_shapes=[pltpu.VMEM((FOX_HEADS, tb, pw), F32), pltpu.VMEM((FOX_HEADS, tb, pw), F32),
                        pltpu.VMEM((FOX_HEADS, tb, pw), F32)],
        compiler_params=_params(),
    )


def _head_mean(x, ones_f):
    return _dot_sel(x, ones_f) * (1.0 / HEAD)


def _merge_fwd(x, proj, o_h, o_b, o_c, gh, w_out, g_post, name, target=None):
    n, d = x.shape
    tm = min(512, n)

    def body(*refs):
        x_ref, ga_ref, gc_ref, oh_ref, ob_ref, oc_ref, gh_ref, w_ref, gp_ref = refs[:9]
        if target is None:
            xo_ref, mixt_ref, y_ref = refs[9:]
        else:
            t_ref, dx_ref, sq_ref, mixt_ref, y_ref = refs[9:]
        oh = oh_ref[...]
        ones_f = _block_ones(HGRN_W, F32)
        na = oh * lax.rsqrt(_head_mean(oh * oh, ones_f) + NORM_EPS) * gh_ref[...]
        ga, gc = ga_ref[...], gc_ref[...]
        mixed = jnp.concatenate([na * (ga * _sigmoid(ga)), ob_ref[...], oc_ref[...] * (gc * _sigmoid(gc))], axis=1)
        mixt_ref[...] = mixed.T.astype(BF16)
        y = _dot(mixed.astype(BF16), w_ref[...])
        y_ref[...] = y
        xn = x_ref[...] + y * lax.rsqrt(jnp.mean(y * y, axis=-1, keepdims=True) + NORM_EPS) * gp_ref[...]
        if target is None:
            xo_ref[...] = xn
        else:
            @pl.when(pl.program_id(0) == 0)
            def _():
                sq_ref[...] = jnp.zeros_like(sq_ref)

            e = xn - t_ref[...]
            dx_ref[...] = e * (1.0 / d)
            sq_ref[...] += jnp.sum(e * e, axis=0, keepdims=True)

    def row(wd, j=0):
        return pl.BlockSpec((tm, wd), lambda i: (i, j))

    def full(a, b):
        return pl.BlockSpec((a, b), lambda i: (0, 0))

    head = [] if target is None else [target]
    res = _pc(
        body, name, grid=(n // tm,),
        in_specs=[row(d), row(HGRN_W, C_GA // HGRN_W), row(FOX_W, C_GC // FOX_W), row(HGRN_W), row(POOL_W), row(FOX_W),
                  full(1, HGRN_W), full(d, d), full(1, d)] + [row(d)] * len(head),
        out_specs=[row(d)] + [full(1, d)] * len(head) + [pl.BlockSpec((d, tm), lambda i: (0, i)), row(d)],
        out_shape=[_sds((n, d), F32)] + [_sds((1, d), F32)] * len(head) + [_sds((d, n), BF16), _sds((n, d), F32)],
        compiler_params=_params(),
    )(x, proj, proj, o_h, o_b, o_c, gh, w_out, g_post, *head)
    return (res[0], res[1], res[2]) if target is None else ((res[0], res[1]), res[2], res[3])


def _rms_bwd(dy_scaled, xhat, r):
    return r * (dy_scaled - xhat * jnp.mean(dy_scaled * xhat, axis=-1, keepdims=True))


def _merge_bwd(dxo, y, g_post, w_out_t, proj, o_c, name):
    n, d = y.shape
    tm = min(512, n)
    wab = HGRN_W + POOL_W

    def body(dx_ref, y_ref, gp_ref, wt_ref, gc_ref, oc_ref, dy_ref, dm_ref, dgp_ref, da_ref, dat_ref, dg_ref, dl_ref):
        @pl.when(pl.program_id(0) == 0)
        def _():
            dgp_ref[...] = jnp.zeros_like(dgp_ref)

        yv, dxv = y_ref[...], dx_ref[...]
        r = lax.rsqrt(jnp.mean(yv * yv, axis=-1, keepdims=True) + NORM_EPS)
        yh = yv * r
        dgp_ref[...] += jnp.sum(dxv * yh, axis=0, keepdims=True)
        dyb = _rms_bwd(dxv * gp_ref[...], yh, r).astype(BF16)
        dy_ref[...] = dyb
        dm_ref[...] = _dot(dyb, wt_ref[:, :wab])
        dmc = _dot(dyb, wt_ref[:, wab:])
        gc, oc = gc_ref[...], oc_ref[...]
        sg = _sigmoid(gc)
        da = dmc * (gc * sg)
        da_ref[...] = da.astype(BF16)
        dat_ref[...] = da.T.astype(BF16)
        dg_ref[...] = (dmc * oc * (sg * (1.0 + gc * (1.0 - sg)))).astype(BF16)
        rr = lax.broadcasted_iota(jnp.int32, (FOX_W, FC_PAD), 0) // HEAD
        cc = lax.broadcasted_iota(jnp.int32, (FOX_W, FC_PAD), 1)
        dl_ref[...] = _dot_sel(da * oc, (rr == cc).astype(F32))

    def row(wd, j=0):
        return pl.BlockSpec((tm, wd), lambda i: (i, j))

    def full(a, b):
        return pl.BlockSpec((a, b), lambda i: (0, 0))

    return _pc(
        body, name, grid=(n // tm,),
        in_specs=[row(d), row(d), full(1, d), full(d, d), row(FOX_W, C_GC // FOX_W), row(FOX_W)],
        out_specs=[row(d), row(wab), full(1, d), row(FOX_W), pl.BlockSpec((FOX_W, tm), lambda i: (0, i)), row(FOX_W),
                   row(FC_PAD)],
        out_shape=[_sds((n, d), BF16), _sds((n, wab), F32), _sds((1, d), F32), _sds((n, FOX_W), BF16),
                   _sds((FOX_W, n), BF16), _sds((n, FOX_W), BF16), _sds((n, FC_PAD), F32)],
        compiler_params=_params(),
    )(dxo, y, g_post, w_out_t, proj, o_c)


def _w_out_grad(mixt, dy, name):
    d, n = mixt.shape
    rows = d // N_DEV

    def body(a_ref, b_ref, o_ref):
        o_ref[...] = _dot(a_ref[...], b_ref[...]).astype(BF16)

    return _pc(
        body, name, grid=(N_DEV,),
        in_specs=[pl.BlockSpec((rows, n), lambda j: (j, 0)), pl.BlockSpec((n, d), lambda j: (0, 0))],
        out_specs=pl.BlockSpec((None, None, rows, d), lambda j: (j % 2, j // 2, 0, 0)),
        out_shape=_sds((2, N_DEV // 2, rows, d), BF16),
        compiler_params=_params(),
    )(mixt, dy)


def _w_in_grad(ht, pieces, name):
    d, n = ht.shape
    ta, tk = min(512, d), min(512, n)
    nk = n // tk
    arrays = [p for p, _ in pieces]
    widths = [p.shape[1] for p in arrays]
    offs = [sum(widths[:i]) for i in range(len(widths))]
    in_w = MAIN_W + FOX_HEADS
    shard = in_w // N_DEV

    def body(*refs):
        a_ref, p_refs = refs[0], refs[1:1 + len(arrays)]
        o_ref, acc = refs[1 + len(arrays):]
        k = pl.program_id(1)

        @pl.when(k == 0)
        def _():
            acc[...] = jnp.zeros_like(acc)

        a = a_ref[...]
        for pr, off, wd in zip(p_refs, offs, widths):
            for j in range(0, wd, 512):
                jw = min(512, wd - j)
                acc[:, off + j:off + j + jw] += _dot(a, pr[:, j:j + jw])

        @pl.when(k == nk - 1)
        def _():
            for j in range(N_DEV):
                o_ref[j % 2, j // 2] = acc[:, j * shard:(j + 1) * shard].astype(BF16)

    return _pc(
        body, name, grid=(d // ta, nk),
        in_specs=[pl.BlockSpec((ta, tk), lambda i, k: (i, k))] + [pl.BlockSpec((tk, wd), lambda i, k: (k, 0)) for wd in widths],
        out_specs=pl.BlockSpec((2, N_DEV // 2, ta, shard), lambda i, k: (0, 0, i, 0)),
        out_shape=_sds((2, N_DEV // 2, d, shard), BF16),
        scratch_shapes=[pltpu.VMEM((ta, sum(widths)), F32)],
        compiler_params=_params(),
    )(ht, *arrays)


def _hgrn_state_bwd(qs, k, v, b, do, s0, ds1, bmask):
    bl = b[SUB - 1:SUB, :]
    eb, ebl, ekt = jnp.exp(b), jnp.exp(bl), jnp.exp(bl - b)
    qe, ktil = qs * eb, k * ekt
    ds1b, dob = ds1.astype(BF16), do.astype(BF16)
    dv = _dot_nt(ktil.astype(BF16), ds1b)
    dqe = _dot(dob, s0.astype(BF16))
    dktil = _dot(v.astype(BF16), ds1b)
    dbl = jnp.sum(dktil * ktil, axis=0, keepdims=True) + ebl * jnp.sum(s0 * ds1, axis=0, keepdims=True)
    ds0 = ds1 * ebl + _dot_tn(dob, qe.astype(BF16)) * bmask
    return dqe * eb, dktil * ekt, dv, dbl, ds0


def _hgrn_intra_bwd(qs, k, v, do, es, aexp, gexp, dq, dk, dv, put_dq_row):
    dks = [dk[j:j + 8] for j in range(0, SUB, 8)]
    dvs = [dv[j:j + 8] for j in range(0, SUB, 8)]
    for t in range(SUB):
        r = _live_rows(t)
        ge = gexp[t * SUB:t * SUB + r, :] * es[t]
        put_dq_row(t, dq[t:t + 1, :] + jnp.sum(ge * k[:r], axis=0, keepdims=True))
        for j in range(r // 8):
            dks[j] = dks[j] + ge[8 * j:8 * j + 8] * qs[t:t + 1, :]
            dvs[j] = dvs[j] + aexp[t * SUB + 8 * j:t * SUB + 8 * j + 8, :] * do[t:t + 1, :]
    return jnp.concatenate(dks, axis=0), jnp.concatenate(dvs, axis=0)


def _hgrn_bwd(dmix, proj, o_h, s0, gh, lb, ones_b, nb, name):
    n = proj.shape[0]
    t = n // nb
    tt = _hgrn_tile(t)
    nt = t // tt
    ncs = tt // CHUNK
    nsub = CHUNK // SUB
    w = HGRN_W

    def body(dm_ref, q_ref, z_ref, v_ref, ga_ref, oh_ref, s0_ref, gh_ref, lb_ref, ones_ref,
             dp_ref, dgh_ref, dlb_ref, ds_s, ss_s, b_s, qs_s, k_s, do_s, dq_s, dk_s, dv_s, dbl_s):
        first = jnp.logical_and(pl.program_id(0) == 0, pl.program_id(1) == 0)

        @pl.when(first)
        def _():
            dgh_ref[...] = jnp.zeros_like(dgh_ref)
            dlb_ref[...] = jnp.zeros_like(dlb_ref)

        @pl.when(pl.program_id(1) == 0)
        def _():
            ds_s[...] = jnp.zeros_like(ds_s)

        ones_b = ones_ref[...]
        ones_f = ones_b.astype(F32)
        bmask = _block_ones(w, F32)
        lbv, ghv = lb_ref[...], gh_ref[...]
        oh, ga, dm = oh_ref[...], ga_ref[...], dm_ref[...]
        rn = lax.rsqrt(_head_mean(oh * oh, ones_f) + NORM_EPS)
        nh = oh * rn
        sga = _sigmoid(ga)
        dp_ref[:, 3 * w:4 * w] = (dm * nh * ghv * (sga * (1.0 + ga * (1.0 - sga)))).astype(BF16)
        dn = dm * (ga * sga)
        dgh_ref[...] += jnp.sum(dn * nh, axis=0, keepdims=True)
        dn = dn * ghv
        do_s[...] = rn * (dn - nh * _head_mean(dn * nh, ones_f))
        q = q_ref[...]
        sig, sn, f, g, k, sq = _hgrn_gates(q, z_ref[...], lbv)
        qs = q * sq
        b_s[...] = _sel_dot(_sub_tri(tt, True), g)
        qs_s[...] = qs
        k_s[...] = k

        def chunk(cc, carry):
            c = ncs - 1 - cc
            base = pl.multiple_of(c * CHUNK, CHUNK)
            tiles = []
            for u in range(nsub):
                rows = pl.ds(base + u * SUB, SUB)
                tiles.append((qs_s[rows, :], k_s[rows, :], v_ref[rows, :], b_s[rows, :], do_s[rows, :]))
            st = s0_ref[c]
            for u, (qs, k, v, b, do) in enumerate(tiles):
                ss_s[u] = st
                if u < nsub - 1:
                    st = _hgrn_state_step(st, k, v, b, bmask)
            ds = ds_s[...]
            for u in reversed(range(nsub)):
                qs, k, v, b, do = tiles[u]
                _, es, ws = _hgrn_decays(qs, k, b)
                gs = [_pad_rows(do[t:t + 1, :] * v[:_live_rows(t)]) for t in range(SUB)]
                aexp = _dot(jnp.concatenate(ws, axis=0).astype(BF16), ones_b)
                gexp = _dot(jnp.concatenate(gs, axis=0).astype(BF16), ones_b)
                dq, dk, dv, dbl, ds = _hgrn_state_bwd(qs, k, v, b, do, ss_s[u], ds, bmask)

                def put_dq_row(i, row, r0=base + u * SUB):
                    dq_s[pl.ds(r0 + i, 1), :] = row

                dk, dv = _hgrn_intra_bwd(qs, k, v, do, es, aexp, gexp, dq, dk, dv, put_dq_row)
                dk_s[pl.ds(base + u * SUB, SUB), :] = dk
                dv_s[pl.ds(base + u * SUB, SUB), :] = dv
                dbl_s[pl.ds(base + u * SUB, SUB), :] = jnp.broadcast_to(dbl, (SUB, w))
            ds_s[...] = ds
            return carry

        lax.fori_loop(0, ncs, chunk, 0)
        dqs, dk = dq_s[...], dk_s[...]
        dg = _sel_dot(_sub_tri(tt, False), qs * dqs - k * dk) + dbl_s[...]
        dfz = jnp.where(f > TINY, dg / jnp.maximum(f, TINY), 0.0)
        dlb_ref[...] += jnp.sum(dfz * (1.0 - sig) - dk * sn, axis=0, keepdims=True)
        dp_ref[:, 0:w] = (dqs * (sq * (1.0 + q * (1.0 - sq)))).astype(BF16)
        dp_ref[:, w:2 * w] = ((dfz - dk) * (1.0 - lbv) * sig * sn).astype(BF16)
        dp_ref[:, 2 * w:3 * w] = dv_s[...].astype(BF16)

    def rv(b, i):
        return b * nt + (nt - 1 - i)

    def col(j):
        return pl.BlockSpec((tt, w), lambda b, i: (rv(b, i), j))

    def full(a, bb):
        return pl.BlockSpec((a, bb), lambda b, i: (0, 0))

    return _pc(
        body, name, grid=(nb, nt),
        in_specs=[col(0), col(C_QA // w), col(C_FA // w), col(C_IA // w), col(C_GA // w), col(0),
                  pl.BlockSpec((ncs, w, w), lambda b, i: (rv(b, i), 0, 0)), full(1, w), full(1, w), full(w, w)],
        out_specs=[pl.BlockSpec((tt, 4 * w), lambda b, i: (rv(b, i), 0)), full(1, w), full(1, w)],
        out_shape=[_sds((n, 4 * w), BF16), _sds((1, w), F32), _sds((1, w), F32)],
        scratch_shapes=[pltpu.VMEM((w, w), F32), pltpu.VMEM((nsub, w, w), F32)] + [pltpu.VMEM((tt, w), F32)] * 8,
        compiler_params=_params(),
    )(dmix, proj, proj, proj, proj, o_h, s0, gh, lb, ones_b)


def _pool_bwd(dmix, proj, wbd, wbd_t, scale, nb, name):
    n = proj.shape[0]
    t = n // nb
    tt = min(256, t)
    nt = t // tt
    nhb = tt // POOL_HALO
    cu, cg, cm = C_UB // POOL_W, C_GB // POOL_W, HGRN_W // POOL_W

    def body(u_ref, g_ref, h_ref, dm_ref, gn_ref, dmn_ref, w_ref, wt_ref, s_ref, dp_ref, dw_ref, ds_ref):
        i = pl.program_id(1)
        first = jnp.logical_and(pl.program_id(0) == 0, i == 0)

        @pl.when(first)
        def _():
            dw_ref[...] = jnp.zeros_like(dw_ref)
            ds_ref[...] = jnp.zeros_like(ds_ref)

        sc = s_ref[...]
        halo = jnp.where(i == 0, 0.0, h_ref[...])
        pooled, cnt = _pool_mix(u_ref[...], halo, i * tt, tt)
        pb = pooled.astype(BF16)
        pre = _dot(pb, w_ref[...])
        gv, dm = g_ref[...], dm_ref[...]
        sg = _sigmoid(gv)
        silu = gv * sg
        dgb = dm * pre * sc * (sg * (1.0 + gv * (1.0 - sg)))
        ds_ref[...] += jnp.sum(dm * pre * silu, axis=0, keepdims=True)
        dpre = (dm * sc * silu).astype(BF16)
        dw_ref[...] += _dot_tn(pb, dpre)
        dpool = _dot(dpre, wt_ref[...])
        gn = gn_ref[...]
        dpre_n = (dmn_ref[...] * sc * (gn * _sigmoid(gn))).astype(BF16)
        dpool_n = jnp.where(i == nt - 1, 0.0, _dot(dpre_n, wt_ref[...]))
        lane, wl = _pool_lane_windows()
        tpos_n = ((i + 1) * tt + lax.broadcasted_iota(jnp.int32, (POOL_HALO, POOL_W), 0)).astype(F32)
        ext = jnp.concatenate([dpool / cnt, dpool_n / jnp.minimum(tpos_n + 1.0, wl)], axis=0)
        rows = tt + POOL_HALO
        sums, cur, shift = [], ext, 1
        for _ in POOL_WINDOWS:
            cur = cur + pltpu.roll(cur, rows - shift, axis=0)
            sums.append(cur[:tt, :])
            shift *= 2
        du = _pool_select(lane, sums) - dpool
        dp_ref[...] = jnp.concatenate([du, dgb], axis=1).astype(BF16)

    def nxt(b, i):
        return jnp.minimum((b * nt + i + 1) * nhb, n // POOL_HALO - 1)

    return _pc(
        body, name, grid=(nb, nt),
        in_specs=_pool_specs(tt, nt, nhb) + [
            pl.BlockSpec((tt, POOL_W), lambda b, i: (b * nt + i, cm)),
            pl.BlockSpec((POOL_HALO, POOL_W), lambda b, i: (nxt(b, i), cg)),
            pl.BlockSpec((POOL_HALO, POOL_W), lambda b, i: (nxt(b, i), cm)),
            pl.BlockSpec((POOL_W, POOL_W), lambda b, i: (0, 0)), pl.BlockSpec((POOL_W, POOL_W), lambda b, i: (0, 0)),
            pl.BlockSpec((1, POOL_W), lambda b, i: (0, 0))],
        out_specs=[pl.BlockSpec((tt, 2 * POOL_W), lambda b, i: (b * nt + i, 0)),
                   pl.BlockSpec((POOL_W, POOL_W), lambda b, i: (0, 0)), pl.BlockSpec((1, POOL_W), lambda b, i: (0, 0))],
        out_shape=[_sds((n, 2 * POOL_W), BF16), _sds((POOL_W, POOL_W), F32), _sds((1, POOL_W), F32)],
        compiler_params=_params(),
    )(proj, proj, proj, dmix, proj, dmix, wbd, wbd_t, scale)


def _fox_bwd(proj, qt, kt, da, dat, c_col, c_row, lse_row, delta_row, nb, name, rider=None):
    n = proj.shape[0]
    t = n // nb
    tb = min(256, t)
    nq = t // tb
    pw = 2 * HEAD

    def body(q_ref, k_ref, v_ref, da_ref, qt_ref, kt_ref, dat_ref, cc_ref, cr_ref, lse_ref, dl_ref,
             dq_ref, dk_ref, dv_ref, dck_ref, dcq_ref, dq_s, dk_s, dv_s, dck_s, dcq_s):
        step = pl.program_id(1)
        kj, qi = pairs(step)

        @pl.when(step == 0)
        def _():
            dq_s[...] = jnp.zeros_like(dq_s)
            dcq_s[...] = jnp.zeros_like(dcq_s)

        @pl.when(qi == nq - 1)
        def _():
            dk_s[...] = jnp.zeros_like(dk_s)
            dv_s[...] = jnp.zeros_like(dv_s)
            dck_s[...] = jnp.zeros_like(dck_s)

        def block(masked):
            lo = _lane_lo()
            if masked:
                causal = lax.broadcasted_iota(jnp.int32, (tb, tb), 1) >= lax.broadcasted_iota(jnp.int32, (tb, tb), 0)
            dck = dck_s[...]
            for p in range(FOX_HEADS // 2):
                sl = slice(p * pw, (p + 1) * pw)
                qp = q_ref[:, sl] * FOX_SCALE
                kp = k_ref[:, sl].astype(BF16)
                vp = v_ref[:, sl].astype(BF16)
                dap = da_ref[:, sl]
                dk, dv = dk_s[:, sl], dv_s[:, sl]
                for h in range(2):
                    hh = 2 * p + h
                    lm = lo if h == 0 else jnp.logical_not(lo)
                    rows = slice(hh * HEAD, (hh + 1) * HEAD)
                    none = jnp.zeros((HEAD, tb), BF16)
                    qm = jnp.where(lm, qp, 0.0).astype(BF16)
                    dam = jnp.where(lm, dap, jnp.zeros_like(dap))
                    qtm = jnp.concatenate([qt_ref[rows, :], none] if h == 0 else [none, qt_ref[rows, :]], axis=0)
                    datm = jnp.concatenate([dat_ref[rows, :], none] if h == 0 else [none, dat_ref[rows, :]], axis=0)
                    s = _dot(kp, qtm) + (cr_ref[hh:hh + 1, :] - cc_ref[:, hh:hh + 1])
                    pe = jnp.exp(s - lse_ref[hh:hh + 1, :])
                    if masked:
                        pe = jnp.where(causal, pe, 0.0)
                    dp = _dot(vp, datm)
                    ds = pe * (dp - dl_ref[hh:hh + 1, :])
                    dsb = ds.astype(BF16)
                    dv = dv + _dot(pe.astype(BF16), dam)
                    dk = dk + _dot(dsb, qm)
                    dq_s[qi, rows, :] += _dot(kt_ref[rows, :], dsb)
                    dck = dck - _put_col(jnp.zeros_like(dck), hh, jnp.sum(ds, axis=1, keepdims=True))
                    dcq_s[qi, hh:hh + 1, :] += _rows_reduce(ds, jnp.add, jnp.sum)
                dk_s[:, sl] = dk
                dv_s[:, sl] = dv
            dck_s[...] = dck

        @pl.when(qi > kj)
        def _():
            block(False)

        @pl.when(qi == kj)
        def _():
            block(True)

        @pl.when(qi == kj)
        def _():
            dk_ref[...] = dk_s[...].astype(BF16)
            dv_ref[...] = dv_s[...].astype(BF16)
            dck_ref[...] = dck_s[...]

        @pl.when(step == nq * (nq + 1) // 2 - 1)
        def _():
            for j in range(nq):
                dq_ref[j * tb:(j + 1) * tb, :] = (dq_s[j].T * FOX_SCALE).astype(BF16)
                dcq_ref[:, j * tb:(j + 1) * tb] = dcq_s[j]

    def pairs(step):
        a, b = _tri_pair(step, nq)
        return nq - 1 - a, nq - 1 - b

    def kspec(wd, j=0):
        return pl.BlockSpec((tb, wd), lambda b, st: (b * nq + pairs(st)[0], j))

    def qspec(wd, j=0):
        return pl.BlockSpec((tb, wd), lambda b, st: (b * nq + pairs(st)[1], j))

    def qrow():
        return pl.BlockSpec((None, FOX_HEADS, tb), lambda b, st: (b, 0, pairs(st)[1]))

    def tspec(which):
        return pl.BlockSpec((FOX_W, tb), lambda b, st: (0, b * nq + pairs(st)[which]))

    return _call(
        body, name, (proj, proj, proj, da, qt, kt, dat, c_col, c_row, lse_row, delta_row), rider,
        grid=(nb, nq * (nq + 1) // 2),
        in_specs=[qspec(FOX_W, C_QC // FOX_W), kspec(FOX_W, C_KC // FOX_W), kspec(FOX_W, C_VC // FOX_W), qspec(FOX_W),
                  tspec(1), tspec(0), tspec(1), kspec(FC_PAD), qrow(), qrow(), qrow()],
        out_specs=[pl.BlockSpec((t, FOX_W), lambda b, st: (b, 0)), kspec(FOX_W), kspec(FOX_W), kspec(FC_PAD),
                   pl.BlockSpec((None, FOX_HEADS, t), lambda b, st: (b, 0, 0))],
        out_shape=[_sds((n, FOX_W), BF16), _sds((n, FOX_W), BF16), _sds((n, FOX_W), BF16), _sds((n, FC_PAD), F32),
                   _sds((nb, FOX_HEADS, t), F32)],
        scratch_shapes=[pltpu.VMEM((nq, FOX_W, tb), F32), pltpu.VMEM((tb, FOX_W), F32), pltpu.VMEM((tb, FOX_W), F32),
                        pltpu.VMEM((tb, FC_PAD), F32), pltpu.VMEM((nq, FOX_HEADS, tb), F32)],
        compiler_params=_params(),
    )


def _fox_decay_bwd(dc_q, dc_k, fc, bias, nb, name):
    n = fc.shape[0]
    t = n // nb
    tt = min(256, t)
    nt = t // tt

    def body(dcq_ref, dck_ref, fc_ref, b_ref, dfc_ref, db_ref, carry):
        i = pl.program_id(1)
        first = jnp.logical_and(pl.program_id(0) == 0, i == 0)

        @pl.when(first)
        def _():
            db_ref[...] = jnp.zeros_like(db_ref)

        @pl.when(i == 0)
        def _():
            carry[...] = jnp.zeros_like(carry)

        r = lax.broadcasted_iota(jnp.int32, (tt, tt), 0)
        cc = lax.broadcasted_iota(jnp.int32, (tt, tt), 1)
        dlf = _dot_hi((r <= cc).astype(F32), dcq_ref[...] + dck_ref[...]) + carry[...]
        carry[...] = dlf[0:1, :]
        dfc = dlf * _sigmoid(-(fc_ref[...] + b_ref[...]))
        dfc_ref[...] = dfc.astype(BF16)
        db_ref[...] += jnp.sum(dfc, axis=0, keepdims=True)

    def row():
        return pl.BlockSpec((tt, FC_PAD), lambda b, i: (b * nt + (nt - 1 - i), 0))

    return _pc(
        body, name, grid=(nb, nt),
        in_specs=[row(), row(), row(), pl.BlockSpec((1, FC_PAD), lambda b, i: (0, 0))],
        out_specs=[row(), pl.BlockSpec((1, FC_PAD), lambda b, i: (0, 0))],
        out_shape=[_sds((n, FC_PAD), BF16), _sds((1, FC_PAD), F32)],
        scratch_shapes=[pltpu.VMEM((1, FC_PAD), F32)],
        compiler_params=_params(),
    )(dc_q, dc_k, fc, bias)


def _in_proj_bwd(pieces, w_main_t, w_fc_t, x, g_pre, dxo, name, rider=None):
    n, d = x.shape
    tm = min(512, n)
    widths = [p.shape[1] for p, _ in pieces]
    offs = [o for _, o in pieces]
    np_ = len(pieces)

    def body(*refs):
        p_refs = refs[:np_]
        wt_ref, wf_ref, x_ref, g_ref, dxo_ref, dx_ref, dg_ref = refs[np_:]

        @pl.when(pl.program_id(0) == 0)
        def _():
            dg_ref[...] = jnp.zeros_like(dg_ref)

        dh = _dot(p_refs[-1][...], wf_ref[...])
        for pr, wd, off in zip(p_refs[:-1], widths[:-1], offs[:-1]):
            for j in range(0, wd, 512):
                jw = min(512, wd - j)
                dh = dh + _dot(pr[:, j:j + jw], wt_ref[off + j:off + j + jw, :])
        xv = x_ref[...]
        r = lax.rsqrt(jnp.mean(xv * xv, axis=-1, keepdims=True) + NORM_EPS)
        xh = xv * r
        dg_ref[...] += jnp.sum(dh * xh, axis=0, keepdims=True)
        dx_ref[...] = dxo_ref[...] + _rms_bwd(dh * g_ref[...], xh, r)

    row = pl.BlockSpec((tm, d), lambda i: (i, 0))
    return _call(
        body, name, (*[p for p, _ in pieces], w_main_t, w_fc_t, x, g_pre, dxo), rider, grid=(n // tm,),
        in_specs=[pl.BlockSpec((tm, wd), lambda i: (i, 0)) for wd in widths] + [
            pl.BlockSpec((MAIN_W, d), lambda i: (0, 0)), pl.BlockSpec((FC_PAD, d), lambda i: (0, 0)),
            row, pl.BlockSpec((1, d), lambda i: (0, 0)), row],
        out_specs=[row, pl.BlockSpec((1, d), lambda i: (0, 0))],
        out_shape=[_sds((n, d), F32), _sds((1, d), F32)],
        compiler_params=_params(),
    )


def _lower_bound_table(lower_bounds, name):
    depth, w = lower_bounds.shape

    def body(lb_ref, o_ref):
        v = lb_ref[...]
        e = jnp.exp(v - jnp.max(v, axis=0, keepdims=True))
        p = e / jnp.sum(e, axis=0, keepdims=True)
        acc = jnp.zeros((1, w), F32)
        for l in range(depth):
            acc = acc + p[l:l + 1, :]
            o_ref[l:l + 1, :] = acc - p[0:1, :]

    return _pc(body, name, out_shape=_sds((depth, w), F32))(lower_bounds)


def _lower_bound_bwd(lower_bounds, dlbs, name):
    depth, w = lower_bounds.shape

    def body(lb_ref, d_ref, o_ref):
        v, dl = lb_ref[...], d_ref[...]
        e = jnp.exp(v - jnp.max(v, axis=0, keepdims=True))
        p = e / jnp.sum(e, axis=0, keepdims=True)
        tot = jnp.sum(dl, axis=0, keepdims=True)
        rows, tail = [], tot
        for l in range(depth):
            rows.append(tail - tot if l == 0 else tail)
            tail = tail - dl[l:l + 1, :]
        dp = jnp.concatenate(rows, axis=0)
        o_ref[...] = p * (dp - jnp.sum(p * dp, axis=0, keepdims=True))

    return _pc(body, name, out_shape=_sds((depth, w), F32))(lower_bounds, dlbs)


def _place():
    x, y, c = lax.axis_index("x"), lax.axis_index("y"), lax.axis_index("c")
    return x, y, c


def _gather_weights(*arrays):
    na = len(arrays)

    def body(*refs):
        ins, outs = refs[:na], refs[na:2 * na]
        send_sems, recv_sems, local_sems = refs[2 * na:]
        x, y, c = _place()
        me, sibling = (x, y, c), (x, y, 1 - c)
        chips = [(1 - x, y), (x, 1 - y), (1 - x, 1 - y)]

        def slot(a, px, py, pc):
            return outs[a].at[4 * px + 2 * py + pc]

        def copy(a, k, block, to, own=False):
            return pltpu.make_async_remote_copy(
                src_ref=ins[a] if own else slot(a, *block), dst_ref=slot(a, *block),
                send_sem=send_sems.at[a * 7 + k], recv_sem=recv_sems.at[a * 7 + k],
                device_id=to, device_id_type=MESH)

        mine = [pltpu.make_async_copy(ins[a], slot(a, *me), local_sems.at[a]) for a in range(na)]
        for cp in mine:
            cp.start()
        first = []
        for a in range(na):
            first.append(copy(a, 0, me, sibling, own=True))
            first += [copy(a, 1 + j, me, (*chip, c), own=True) for j, chip in enumerate(chips)]
        for cp in first:
            cp.start()
        passed = []
        for j, chip in enumerate(chips):
            for a in range(na):
                copy(a, 1 + j, (*chip, c), me).wait_recv()
                fw = copy(a, 4 + j, (*chip, c), sibling)
                fw.start()
                passed.append(fw)
        for a in range(na):
            copy(a, 0, sibling, me).wait_recv()
            for j, chip in enumerate(chips):
                copy(a, 4 + j, (*chip, 1 - c), me).wait_recv()
        for cp in first + passed:
            cp.wait_send()
        for cp in mine:
            cp.wait()

    any_spec = pl.BlockSpec(memory_space=pl.ANY)
    return _pc(
        body, "gather_weights",
        in_specs=[any_spec] * na, out_specs=[any_spec] * na,
        out_shape=[_sds((N_DEV,) + a.shape, a.dtype) for a in arrays],
        scratch_shapes=[pltpu.SemaphoreType.DMA((7 * na,)), pltpu.SemaphoreType.DMA((7 * na,)),
                        pltpu.SemaphoreType.DMA((na,))],
    )(*arrays)


def _peer(k):
    x, y, c = _place()
    return (1 - x if k & 4 else x, 1 - y if k & 2 else y, 1 - c if k & 1 else c)


def _remote(src, dst, sems, s, to):
    return pltpu.make_async_remote_copy(src_ref=src, dst_ref=dst, send_sem=sems[0].at[s], recv_sem=sems[1].at[s],
                                        device_id=to, device_id_type=MESH)


def _gather_rider(shards):
    na = len(shards)

    def plan(ins, outs, *sems):
        x, y, c = _place()
        me = 4 * x + 2 * y + c
        locs = [pltpu.make_async_copy(ins[a], outs[a].at[me], sems[2].at[a]) for a in range(na)]
        sends, recvs = [], []
        for k in range(1, N_DEV):
            px, py, pc = _peer(k)
            for a in range(na):
                s = (k - 1) * na + a
                sends.append(_remote(ins[a], outs[a].at[me], sems, s, (px, py, pc)))
                recvs.append(_remote(ins[a], outs[a].at[4 * px + 2 * py + pc], sems, s, (px, py, pc)))
        return sends, recvs, locs

    return _Rider(shards, [_sds((N_DEV,) + a.shape, a.dtype) for a in shards], (N_DEV - 1) * na, na, plan)


def _direct_exchange_rider(blocks):
    na = len(blocks)

    def plan(ins, outs, *sems):
        x, y, c = _place()
        me = 4 * x + 2 * y + c
        locs = [pltpu.make_async_copy(ins[a].at[c, 2 * x + y], outs[a].at[me], sems[2].at[a]) for a in range(na)]
        sends, recvs = [], []
        for k in range(1, N_DEV):
            px, py, pc = _peer(k)
            for a in range(na):
                s = (k - 1) * na + a
                sends.append(_remote(ins[a].at[pc, 2 * px + py], outs[a].at[me], sems, s, (px, py, pc)))
                recvs.append(_remote(ins[a].at[pc, 2 * px + py], outs[a].at[4 * px + 2 * py + pc], sems, s, (px, py, pc)))
        return sends, recvs, locs

    return _Rider(blocks, [_sds((N_DEV,) + a.shape[2:], a.dtype) for a in blocks], (N_DEV - 1) * na, na, plan)


def _swap_rider(halves):
    na = len(halves)

    def plan(ins, outs, *sems):
        x, y, c = _place()
        cps = [_remote(ins[a].at[1 - c], outs[a], sems, a, (x, y, 1 - c)) for a in range(na)]
        return cps, cps, []

    return _Rider(halves, [_sds(a.shape[1:], a.dtype) for a in halves], na, 1, plan)


def _chip_exchange_rider(parts, small=None):
    na = len(parts)
    n_chip = N_DEV // 2

    def plan(ins, outs, *sems):
        x, y, c = _place()
        chip = 2 * x + y
        locs = [pltpu.make_async_copy(ins[a].at[chip], outs[a].at[chip], sems[2].at[a]) for a in range(na)]
        sends, recvs = [], []
        for k in range(1, n_chip):
            px, py, _ = _peer(2 * k)
            for a in range(na):
                s = (k - 1) * na + a
                sends.append(_remote(ins[a].at[2 * px + py], outs[a].at[chip], sems, s, (px, py, c)))
                recvs.append(_remote(ins[a].at[2 * px + py], outs[a].at[2 * px + py], sems, s, (px, py, c)))
        if small is not None:
            me = 2 * chip + c
            locs.append(pltpu.make_async_copy(ins[na], outs[na].at[me], sems[2].at[na]))
            for k in range(1, N_DEV):
                px, py, pc = _peer(k)
                s = (n_chip - 1) * na + k - 1
                sends.append(_remote(ins[na], outs[na].at[me], sems, s, (px, py, pc)))
                recvs.append(_remote(ins[na], outs[na].at[4 * px + 2 * py + pc], sems, s, (px, py, pc)))
        return sends, recvs, locs

    extra = [] if small is None else [small]
    shapes = [_sds(a.shape, a.dtype) for a in parts] + [_sds((N_DEV,) + s.shape, s.dtype) for s in extra]
    n_sems = (n_chip - 1) * na + (N_DEV - 1) * len(extra)
    return _Rider(list(parts) + extra, shapes, n_sems, na + len(extra), plan)


def _pair_add(halves, other, core, name):
    _, nch, r, c = halves.shape
    tr = 256 if r % 256 == 0 else r

    def body(h_ref, o_ref, c_ref, p_ref):
        mine = jnp.where(c_ref[...] == 0, h_ref[0].astype(F32), h_ref[1].astype(F32))
        p_ref[...] = (mine + o_ref[...].astype(F32)).astype(BF16)

    blk = pl.BlockSpec((None, tr, c), lambda j, i: (j, i, 0))
    return _pc(
        body, name, grid=(nch, r // tr),
        in_specs=[pl.BlockSpec((2, None, tr, c), lambda j, i: (0, j, i, 0)), blk, pl.BlockSpec((1, 1), lambda j, i: (0, 0))],
        out_specs=blk, out_shape=_sds((nch, r, c), BF16),
        compiler_params=_params(),
    )(halves, other, core)


def _sum_adamw(parts, w, m, v, name, rider=None):
    nl, r, c = w.shape
    tr = 256 if r % 256 == 0 else r

    def body(*refs):
        p_refs = refs[:nl]
        w_ref, m_ref, v_ref, g_ref, d_ref, mo_ref, vo_ref = refs[nl:]
        for l in range(nl):
            @pl.when(pl.program_id(0) == l)
            def _(p_ref=p_refs[l]):
                g = p_ref[0].astype(F32)
                for j in range(1, p_ref.shape[0]):
                    g = g + p_ref[j].astype(F32)
                mn = ADAM_B1 * m_ref[...] + (1.0 - ADAM_B1) * g
                vn = ADAM_B2 * v_ref[...] + (1.0 - ADAM_B2) * (g * g)
                m_hat = mn / (1.0 - ADAM_B1 ** ADAM_STEP)
                v_hat = vn / (1.0 - ADAM_B2 ** ADAM_STEP)
                g_ref[...] = g
                d_ref[...] = -ADAM_LR * (m_hat / (jnp.sqrt(v_hat) + ADAM_EPS) + ADAM_WD * w_ref[...])
                mo_ref[...] = mn
                vo_ref[...] = vn

    def part_spec(l, k):
        return pl.BlockSpec((k, tr, c), lambda li, i: (0, jnp.where(li == l, i, 0), 0))

    row = pl.BlockSpec((None, tr, c), lambda li, i: (li, i, 0))
    return _call(
        body, name, (*parts, w, m, v), rider, grid=(nl, r // tr),
        in_specs=[part_spec(l, p.shape[0]) for l, p in enumerate(parts)] + [row, row, row],
        out_specs=[row] * 4,
        out_shape=[_sds((nl, r, c), F32)] * 4,
        compiler_params=_params(),
    )


SMALL = ("lower_bounds", "pre_norm_g", "hgrn_norm_g", "fox_f_bias", "pool_w", "pool_scale", "post_norm_g")
SMALL_LANES = 128


def _small_size(tree):
    return sum(tree[k].size for k in SMALL)


def _pack_small(tree, extra=None):
    flat = jnp.concatenate([tree[k].reshape(-1) for k in SMALL] + ([] if extra is None else [extra.reshape(1)]))
    rows = -(-(_small_size(tree) + 1) // (8 * SMALL_LANES)) * 8
    return jnp.pad(flat, (0, rows * SMALL_LANES - flat.shape[0])).reshape(rows, SMALL_LANES)


def _unpack_small(packed, like):
    flat, out, off = packed.reshape(-1), {}, 0
    for k in SMALL:
        size = like[k].size
        out[k] = flat[off:off + size].reshape(like[k].shape)
        off += size
    return out


def _block_diag(pw):
    g = pw.shape[0]
    eye = jnp.eye(g, dtype=pw.dtype)
    return (eye[:, None, :, None] * pw[:, :, None, :]).reshape(g * HEAD, g * HEAD)


def _assemble_w_in(g_in, name):
    _, d, shard = g_in.shape
    tr = min(256, d)
    wide = MAIN_W + FC_PAD

    def body(g_ref, wm_ref, wf_ref, wmt_ref, wft_ref, row_s):
        row_s[:, MAIN_W:] = jnp.zeros((tr, FC_PAD), F32)
        for j in range(N_DEV):
            row_s[:, j * shard:(j + 1) * shard] = g_ref[j].astype(F32)
        wm_ref[...] = row_s[:, :MAIN_W].astype(BF16)
        wf_ref[...] = row_s[:, MAIN_W:].astype(BF16)
        for j in range(0, MAIN_W, 512):
            wmt_ref[j:j + 512, :] = row_s[:, j:j + 512].T.astype(BF16)
        wft_ref[...] = row_s[:, MAIN_W:].T.astype(BF16)

    return _pc(
        body, name, grid=(d // tr,),
        in_specs=[pl.BlockSpec((N_DEV, tr, shard), lambda i: (0, i, 0))],
        out_specs=[pl.BlockSpec((tr, MAIN_W), lambda i: (i, 0)), pl.BlockSpec((tr, FC_PAD), lambda i: (i, 0)),
                   pl.BlockSpec((MAIN_W, tr), lambda i: (0, i)), pl.BlockSpec((FC_PAD, tr), lambda i: (0, i))],
        out_shape=[_sds((d, MAIN_W), BF16), _sds((d, FC_PAD), BF16), _sds((MAIN_W, d), BF16), _sds((FC_PAD, d), BF16)],
        scratch_shapes=[pltpu.VMEM((tr, wide), F32)],
        compiler_params=_params(),
    )(g_in)


def _w_out_parts(g_out):
    full_out = g_out.reshape(N_DEV * g_out.shape[1], g_out.shape[2])
    return full_out, full_out.T


def _layer_fwd(l, x, lbs, weights, lw, nb, rider_h=None, rider_c=None, target=None):
    n = x.shape[0]
    t = n // nb
    w_main, w_fc, _, _, w_out, _ = lw
    bias = jnp.pad(weights["fox_f_bias"][l:l + 1], ((0, 0), (0, FC_PAD - FOX_HEADS)))
    wbd = _block_diag(weights["pool_w"][l]).astype(BF16)
    proj, fc, ht, qt, kt = _in_proj_fwd(x, weights["pre_norm_g"][l:l + 1], w_main, w_fc, f"in_proj_fwd_{l}")
    c_col = _fox_decay_fwd(fc, bias, nb, f"fox_decay_fwd_{l}")
    c_row = c_col.reshape(nb, t, FC_PAD)[:, :, :FOX_HEADS].transpose(0, 2, 1)
    (o_h, s0), rode_h = _hgrn_fwd(proj, lbs[l:l + 1], _block_ones(HGRN_W, BF16), nb, f"hgrn_fwd_{l}", rider_h)
    o_b = _pool_fwd(proj, wbd, weights["pool_scale"][l:l + 1], nb, f"pool_fwd_{l}")
    (o_c, lse), rode_c = _fox_fwd(proj, kt, c_col, c_row, nb, f"fox_fwd_{l}", rider_c)
    if w_out is None:
        lw = tuple(lw[:4]) + _w_out_parts(rode_h[0])
        w_out, rode_h = lw[4], rode_h[1:]
    x_next, mixt, y = _merge_fwd(x, proj, o_h, o_b, o_c, weights["hgrn_norm_g"][l:l + 1], w_out,
                                 weights["post_norm_g"][l:l + 1], f"merge_fwd_{l}", target)
    return x_next, (x, proj, fc, ht, qt, kt, c_col, c_row, o_h, s0, o_c, lse, mixt, y, bias, wbd), lw, (rode_h, rode_c)


def _layer_bwd(l, dx, saved, lbs, weights, lw, nb, rider=None):
    _, proj, fc, ht, qt, kt, c_col, c_row, o_h, s0, o_c, lse, mixt, y, bias, wbd = saved
    n = proj.shape[0]
    t = n // nb
    w_out_t = lw[5]
    g = {}
    dy, dmix, dgp, da, dat, d_gc, delta = _merge_bwd(dx, y, weights["post_norm_g"][l:l + 1], w_out_t, proj, o_c,
                                                f"merge_bwd_{l}")
    g["post_norm_g"] = dgp[0]
    g["w_out"] = _w_out_grad(mixt, dy, f"w_out_grad_{l}")
    d_a, dgh, dlb = _hgrn_bwd(dmix, proj, o_h, s0, weights["hgrn_norm_g"][l:l + 1], lbs[l:l + 1],
                              _block_ones(HGRN_W, BF16), nb, f"hgrn_bwd_{l}")
    g["hgrn_norm_g"], g["lbs"] = dgh[0], dlb[0]
    d_b, dwbd, dps = _pool_bwd(dmix, proj, wbd, wbd.T, weights["pool_scale"][l:l + 1], nb, f"pool_bwd_{l}")
    g["pool_w"] = jnp.stack([dwbd[j * HEAD:(j + 1) * HEAD, j * HEAD:(j + 1) * HEAD] for j in range(len(POOL_WINDOWS))])
    g["pool_scale"] = dps[0]
    delta_row, lse_row = [a.reshape(nb, t, FC_PAD)[:, :, :FOX_HEADS].transpose(0, 2, 1) for a in (delta, lse)]
    (d_qc, d_kc, d_vc, dc_k, dc_q), rode = _fox_bwd(proj, qt, kt, da, dat, c_col, c_row, lse_row, delta_row, nb,
                                                    f"fox_bwd_{l}", rider)
    dc_q = jnp.pad(dc_q.transpose(0, 2, 1).reshape(n, FOX_HEADS), ((0, 0), (0, FC_PAD - FOX_HEADS)))
    d_fc, dbias = _fox_decay_bwd(dc_q, dc_k, fc, bias, nb, f"fox_decay_bwd_{l}")
    g["fox_f_bias"] = dbias[0, :FOX_HEADS]
    pieces = [(d_a, C_QA), (d_b, C_UB), (d_qc, C_QC), (d_kc, C_KC), (d_vc, C_VC), (d_gc, C_GC), (d_fc, None)]
    g["w_in"] = _w_in_grad(ht, pieces, f"w_in_grad_{l}")
    return g, pieces, rode


def _layer_bwd_input(l, dx, pieces, saved, weights, lw, rider=None):
    (dxi, dgpre), rode = _in_proj_bwd(pieces, lw[2], lw[3], saved[0], weights["pre_norm_g"][l:l + 1], dx,
                                      f"in_proj_bwd_{l}", rider)
    return dxi, dgpre[0], rode


def kernel(x, lower_bounds, pre_norm_g, w_in, hgrn_norm_g, fox_f_bias, pool_w, pool_scale, w_out, post_norm_g, loss_target, m_lower_bounds, m_pre_norm_g, m_w_in, m_hgrn_norm_g, m_fox_f_bias, m_pool_w, m_pool_scale, m_w_out, m_post_norm_g, v_lower_bounds, v_pre_norm_g, v_w_in, v_hgrn_norm_g, v_fox_f_bias, v_pool_w, v_pool_scale, v_w_out, v_post_norm_g):
    weights = dict(lower_bounds=lower_bounds, pre_norm_g=pre_norm_g, hgrn_norm_g=hgrn_norm_g, fox_f_bias=fox_f_bias,
                   pool_w=pool_w, pool_scale=pool_scale, post_norm_g=post_norm_g)
    mom_m = dict(lower_bounds=m_lower_bounds, pre_norm_g=m_pre_norm_g, hgrn_norm_g=m_hgrn_norm_g, fox_f_bias=m_fox_f_bias,
                 pool_w=m_pool_w, pool_scale=m_pool_scale, post_norm_g=m_post_norm_g)
    mom_v = dict(lower_bounds=v_lower_bounds, pre_norm_g=v_pre_norm_g, hgrn_norm_g=v_hgrn_norm_g, fox_f_bias=v_fox_f_bias,
                 pool_w=v_pool_w, pool_scale=v_pool_scale, post_norm_g=v_post_norm_g)
    depth = w_in.shape[0]
    nb, t, d = x.shape
    n = nb * t
    core = lax.axis_index("c").astype(jnp.int32).reshape(1, 1)
    shards = [(w_in[l].astype(BF16), w_out[l].astype(BF16)) for l in range(depth)]
    lbs = _lower_bound_table(lower_bounds, "lower_bound_table")

    (g_in,) = _gather_weights(shards[0][0])
    coming = tuple(_assemble_w_in(g_in, "assemble_w_in_0")) + (None, None)
    xl, saved, lw = x.reshape(n, d), [], []
    for l in range(depth):
        last = l + 1 == depth
        ride_h = ([shards[l][1]] if coming[4] is None else []) + ([] if last else [shards[l + 1][1]])
        xl, sv, lw_l, (rode_h, rode_c) = _layer_fwd(
            l, xl, lbs, weights, coming, nb, _gather_rider(ride_h) if ride_h else None,
            None if last else _gather_rider([shards[l + 1][0]]), loss_target.reshape(n, d) if last else None)
        saved.append(sv)
        lw.append(lw_l)
        if not last:
            coming = tuple(_assemble_w_in(rode_c[0], f"assemble_w_in_{l + 1}")) + _w_out_parts(rode_h[0])
    dx, sq = xl
    loss_here = 0.5 * jnp.sum(sq) / d

    grads, recv, pending = [None] * depth, [None] * depth, None
    for l in reversed(range(depth)):
        g, pieces, rode = _layer_bwd(l, dx, saved[l], lbs, weights, lw[l], nb, pending)
        if rode is not None:
            recv[l + 1] = rode
        blocks = (g["w_in"], g["w_out"])
        if l > 0:
            pending = _direct_exchange_rider(blocks)
            dx, g["pre_norm_g"], _ = _layer_bwd_input(l, dx, pieces, saved[l], weights, lw[l])
        else:
            other = _run_rider(_swap_rider(blocks), "grad_swap")
            summed = [_pair_add(hv, ot, core, f"grad_pair_add_{i}") for i, (hv, ot) in enumerate(zip(blocks, other))]
            dx, g["pre_norm_g"], recv[l] = _layer_bwd_input(l, dx, pieces, saved[l], weights, lw[l],
                                                            _chip_exchange_rider(summed))
        grads[l] = g
    small = {k: jnp.stack([grads[l][k] for l in range(depth)]) for k in SMALL if k != "lower_bounds"}
    small["lower_bounds"] = _lower_bound_bwd(lower_bounds, jnp.stack([grads[l]["lbs"] for l in range(depth)]),
                                             "lower_bound_bwd")
    (r_small,) = _run_rider(_gather_rider([_pack_small(small, loss_here)]), "small_grads_gather")

    res_in, _ = _sum_adamw([recv[l][0] for l in range(depth)], w_in, m_w_in, v_w_in, "adamw_w_in")
    res_out, _ = _sum_adamw([recv[l][1] for l in range(depth)], w_out, m_w_out, v_w_out, "adamw_w_out")
    res_small, _ = _sum_adamw([r_small], _pack_small(weights)[None], _pack_small(mom_m)[None], _pack_small(mom_v)[None],
                              "adamw_small")
    loss = res_small[0][0].reshape(-1)[_small_size(weights)]

    names = ("lower_bounds", "pre_norm_g", "w_in", "hgrn_norm_g", "fox_f_bias", "pool_w", "pool_scale", "w_out", "post_norm_g")
    outs = [loss, dx.reshape(nb, t, d)]
    for i in range(4):
        full = dict(_unpack_small(res_small[i][0], weights), w_in=res_in[i], w_out=res_out[i])
        outs += [full[k] for k in names]
    return tuple(outs)
```

```python
import functools

import jax
import jax.numpy as jnp
from jax import lax
from jax.experimental import pallas as pl
from jax.experimental.pallas import tpu as pltpu

F32, BF16 = jnp.float32, jnp.bfloat16
HI = lax.Precision.HIGHEST
MESH = pl.DeviceIdType.MESH
N_DEV = 8

NORM_EPS = 1e-6
MASK_VALUE = -1e30
TINY = 1e-30
CHUNK = 64
SUB = 16
HGRN_W, POOL_W, FOX_W = 256, 256, 512
HEAD = 64
FOX_HEADS = 8
POOL_WINDOWS = (2, 4, 8, 16)
POOL_HALO = 16
MAIN_W = 3584
FC_PAD = 128
C_QA, C_FA, C_IA, C_GA, C_UB, C_GB, C_QC, C_KC, C_VC, C_GC = 0, 256, 512, 768, 1024, 1280, 1536, 2048, 2560, 3072
FOX_SCALE = HEAD ** -0.5

ADAM_LR, ADAM_B1, ADAM_B2, ADAM_EPS, ADAM_WD, ADAM_STEP = 0.001, 0.9, 0.999, 1e-08, 0.01, 10

VMEM_LIMIT = 56 * 1024 * 1024


def _pc(fn, name, **kw):
    return pl.pallas_call(fn, name=name, **kw)


def _params(**kw):
    return pltpu.CompilerParams(vmem_limit_bytes=VMEM_LIMIT, **kw)


class _Rider:
    def __init__(self, inputs, out_shapes, n_sems, n_local, plan):
        self.inputs, self.out_shapes, self.n_sems, self.n_local, self.plan = list(inputs), list(out_shapes), n_sems, n_local, plan

    def start(self, ins, outs, *sems):
        sends, _, locs = self.plan(ins, outs, *sems)
        for cp in locs + sends:
            cp.start()

    def wait(self, ins, outs, *sems):
        sends, recvs, locs = self.plan(ins, outs, *sems)
        for cp in recvs:
            cp.wait_recv()
        for cp in sends:
            cp.wait_send()
        for cp in locs:
            cp.wait()

    def sem_shapes(self):
        return [pltpu.SemaphoreType.DMA((self.n_sems,)), pltpu.SemaphoreType.DMA((self.n_sems,)),
                pltpu.SemaphoreType.DMA((self.n_local,))]


def _call(body, name, args, rider=None, *, grid, in_specs, out_specs, out_shape, scratch_shapes=(), **kw):
    if rider is None:
        res = _pc(body, name, grid=grid, in_specs=in_specs, out_specs=out_specs, out_shape=out_shape,
                  scratch_shapes=list(scratch_shapes), **kw)(*args)
        return res, None
    n_in, n_out, n_scr = len(in_specs), len(out_specs), len(scratch_shapes)
    n_rin, n_rout = len(rider.inputs), len(rider.out_shapes)

    def ridden(*refs):
        ins, refs = refs[:n_in], refs[n_in:]
        rins, refs = refs[:n_rin], refs[n_rin:]
        outs, refs = refs[:n_out], refs[n_out:]
        routs, refs = refs[:n_rout], refs[n_rout:]
        scr, sems = refs[:n_scr], refs[n_scr:]
        first = functools.reduce(jnp.logical_and, [pl.program_id(a) == 0 for a in range(len(grid))])
        last = functools.reduce(jnp.logical_and, [pl.program_id(a) == g - 1 for a, g in enumerate(grid)])

        @pl.when(first)
        def _():
            rider.start(rins, routs, *sems)

        body(*ins, *outs, *scr)

        @pl.when(last)
        def _():
            rider.wait(rins, routs, *sems)

    any_spec = pl.BlockSpec(memory_space=pl.ANY)
    res = _pc(ridden, name, grid=grid, in_specs=list(in_specs) + [any_spec] * n_rin,
              out_specs=list(out_specs) + [any_spec] * n_rout, out_shape=list(out_shape) + rider.out_shapes,
              scratch_shapes=list(scratch_shapes) + rider.sem_shapes(), **kw)(*args, *rider.inputs)
    return res[:n_out], res[n_out:]


def _run_rider(rider, name):
    n_rin = len(rider.inputs)

    def body(*refs):
        ins, outs, sems = refs[:n_rin], refs[n_rin:n_rin + len(rider.out_shapes)], refs[n_rin + len(rider.out_shapes):]
        rider.start(ins, outs, *sems)
        rider.wait(ins, outs, *sems)

    any_spec = pl.BlockSpec(memory_space=pl.ANY)
    return _pc(body, name, in_specs=[any_spec] * n_rin, out_specs=[any_spec] * len(rider.out_shapes),
               out_shape=rider.out_shapes, scratch_shapes=rider.sem_shapes())(*rider.inputs)


def _dot(a, b):
    return jnp.dot(a, b, preferred_element_type=F32)


def _dot_nt(a, b):
    return lax.dot_general(a, b, (((1,), (1,)), ((), ())), preferred_element_type=F32)


def _dot_tn(a, b):
    return lax.dot_general(a, b, (((0,), (0,)), ((), ())), preferred_element_type=F32)


def _dot_hi(a, b):
    return jnp.dot(a, b, precision=HI, preferred_element_type=F32)


def _split2(x):
    hi = x.astype(BF16)
    return hi, (x - hi.astype(F32)).astype(BF16)


def _sel_dot(sel, x):
    hi, lo = _split2(x)
    sb = sel.astype(BF16)
    return _dot(sb, hi) + _dot(sb, lo)


def _dot_sel(x, sel):
    hi, lo = _split2(x)
    sb = sel.astype(BF16)
    return _dot(hi, sb) + _dot(lo, sb)


def _sigmoid(x):
    return 1.0 / (1.0 + jnp.exp(-x))


def _block_ones(n, dtype):
    r = lax.broadcasted_iota(jnp.int32, (n, n), 0) // HEAD
    c = lax.broadcasted_iota(jnp.int32, (n, n), 1) // HEAD
    return (r == c).astype(dtype)


def _sds(shape, dtype):
    return jax.ShapeDtypeStruct(shape, dtype)


def _in_proj_fwd(x, g_pre, w_main, w_fc, name):
    n, d = x.shape
    tm = min(512, n)

    def body(x_ref, g_ref, w_ref, wf_ref, proj_ref, fc_ref, ht_ref, qt_ref, kt_ref):
        xv = x_ref[...]
        r = lax.rsqrt(jnp.mean(xv * xv, axis=-1, keepdims=True) + NORM_EPS)
        hf = xv * r * g_ref[...]
        hb = hf.astype(BF16)
        ht_ref[...] = hf.T.astype(BF16)
        for j in range(0, MAIN_W, FOX_W):
            res = _dot(hb, w_ref[:, j:j + FOX_W])
            proj_ref[:, j:j + FOX_W] = res
            if j == C_QC:
                qt_ref[...] = (res * FOX_SCALE).T.astype(BF16)
            if j == C_KC:
                kt_ref[...] = res.T.astype(BF16)
        fc_ref[...] = _dot(hb, wf_ref[...])

    def cols(rows):
        return pl.BlockSpec((rows, tm), lambda i: (0, i))

    return _pc(
        body, name, grid=(n // tm,),
        in_specs=[pl.BlockSpec((tm, d), lambda i: (i, 0)), pl.BlockSpec((1, d), lambda i: (0, 0)),
                  pl.BlockSpec((d, MAIN_W), lambda i: (0, 0)), pl.BlockSpec((d, FC_PAD), lambda i: (0, 0))],
        out_specs=[pl.BlockSpec((tm, MAIN_W), lambda i: (i, 0)), pl.BlockSpec((tm, FC_PAD), lambda i: (i, 0)),
                   cols(d), cols(FOX_W), cols(FOX_W)],
        out_shape=[_sds((n, MAIN_W), F32), _sds((n, FC_PAD), F32), _sds((d, n), BF16), _sds((FOX_W, n), BF16),
                   _sds((FOX_W, n), BF16)],
        compiler_params=_params(),
    )(x, g_pre, w_main, w_fc)


def _fox_decay_fwd(fc, bias, nb, name):
    n = fc.shape[0]
    t = n // nb
    tt = min(256, t)
    nt = t // tt

    def body(fc_ref, b_ref, c_ref, carry):
        i = pl.program_id(1)

        @pl.when(i == 0)
        def _():
            carry[...] = jnp.zeros_like(carry)

        xv = fc_ref[...] + b_ref[...]
        lf = jnp.minimum(xv, 0.0) - jnp.log(1.0 + jnp.exp(-jnp.abs(xv)))
        r = lax.broadcasted_iota(jnp.int32, (tt, tt), 0)
        cc = lax.broadcasted_iota(jnp.int32, (tt, tt), 1)
        cs = _dot_hi((r >= cc).astype(F32), lf) + carry[...]
        c_ref[...] = cs
        carry[...] = cs[tt - 1:tt, :]

    return _pc(
        body, name, grid=(nb, nt),
        in_specs=[pl.BlockSpec((tt, FC_PAD), lambda b, i: (b * nt + i, 0)), pl.BlockSpec((1, FC_PAD), lambda b, i: (0, 0))],
        out_specs=pl.BlockSpec((tt, FC_PAD), lambda b, i: (b * nt + i, 0)),
        out_shape=_sds((n, FC_PAD), F32),
        scratch_shapes=[pltpu.VMEM((1, FC_PAD), F32)],
        compiler_params=_params(),
    )(fc, bias)


def _hgrn_gates(q, z, lb):
    sig = _sigmoid(z)
    sn = _sigmoid(-z)
    f = lb + (1.0 - lb) * sig
    g = jnp.log(jnp.maximum(f, TINY))
    k = (1.0 - lb) * sn
    sq = _sigmoid(q)
    return sig, sn, f, g, k, sq


def _sub_tri(n, lower):
    r = lax.broadcasted_iota(jnp.int32, (n, n), 0)
    c = lax.broadcasted_iota(jnp.int32, (n, n), 1)
    tri = (r >= c) if lower else (r <= c)
    return jnp.logical_and(r // SUB == c // SUB, tri).astype(F32)


def _live_rows(t):
    return 8 * (t // 8 + 1)


def _pad_rows(x):
    return x if x.shape[0] == SUB else jnp.concatenate([x, jnp.zeros((SUB - x.shape[0], x.shape[1]), x.dtype)], axis=0)


def _hgrn_decays(qs, k, b):
    srow = lax.broadcasted_iota(jnp.int32, (SUB, HGRN_W), 0)
    es, ws = [], []
    for t in range(SUB):
        r = _live_rows(t)
        e = jnp.where(srow[:r] <= t, jnp.exp(b[t:t + 1, :] - b[:r]), 0.0)
        es.append(e)
        ws.append(_pad_rows(e * (qs[t:t + 1, :] * k[:r])))
    return srow, es, ws


def _hgrn_state_step(st, k, v, b, bmask):
    bl = b[SUB - 1:SUB, :]
    ktil = k * jnp.exp(bl - b)
    return st * jnp.exp(bl) + _dot_tn(v.astype(BF16), ktil.astype(BF16)) * bmask


def _hgrn_tile(t):
    return min(256, t)


def _hgrn_fwd(proj, lb, ones_b, nb, name, rider=None):
    n = proj.shape[0]
    t = n // nb
    tt = _hgrn_tile(t)
    nt = t // tt
    ncs = tt // CHUNK
    w = HGRN_W

    def body(q_ref, z_ref, v_ref, lb_ref, ones_ref, o_ref, s0_ref, st_s, b_s, qs_s, k_s):
        @pl.when(pl.program_id(1) == 0)
        def _():
            st_s[...] = jnp.zeros_like(st_s)

        q = q_ref[...]
        _, _, _, g, k, sq = _hgrn_gates(q, z_ref[...], lb_ref[...])
        b_s[...] = _sel_dot(_sub_tri(tt, True), g)
        qs_s[...] = q * sq
        k_s[...] = k
        bmask = _block_ones(w, F32)
        ones_b = ones_ref[...]

        def chunk(c, carry):
            st = st_s[...]
            s0_ref[c] = st
            base = pl.multiple_of(c * CHUNK, CHUNK)
            tiles = []
            for u in range(CHUNK // SUB):
                rows = pl.ds(base + u * SUB, SUB)
                tiles.append((qs_s[rows, :], k_s[rows, :], v_ref[rows, :], b_s[rows, :]))
            aexps = []
            for qs, k, v, b in tiles:
                _, _, ws = _hgrn_decays(qs, k, b)
                aexps.append(_dot(jnp.concatenate(ws, axis=0).astype(BF16), ones_b))
            inters = []
            for qs, k, v, b in tiles:
                inters.append(_dot_nt((qs * jnp.exp(b)).astype(BF16), st.astype(BF16)))
                st = _hgrn_state_step(st, k, v, b, bmask)
            st_s[...] = st
            for u, ((qs, k, v, b), aexp, o) in enumerate(zip(tiles, aexps, inters)):
                for t in range(SUB):
                    r = _live_rows(t)
                    row = o[t:t + 1, :] + jnp.sum(aexp[t * SUB:t * SUB + r, :] * v[:r], axis=0, keepdims=True)
                    o_ref[pl.ds(base + u * SUB + t, 1), :] = row
            return carry

        lax.fori_loop(0, ncs, chunk, 0)

    def col(j):
        return pl.BlockSpec((tt, w), lambda b, i: (b * nt + i, j))

    return _call(
        body, name, (proj, proj, proj, lb, ones_b), rider, grid=(nb, nt),
        in_specs=[col(C_QA // w), col(C_FA // w), col(C_IA // w), pl.BlockSpec((1, w), lambda b, i: (0, 0)),
                  pl.BlockSpec((w, w), lambda b, i: (0, 0))],
        out_specs=[pl.BlockSpec((tt, w), lambda b, i: (b * nt + i, 0)),
                   pl.BlockSpec((ncs, w, w), lambda b, i: (b * nt + i, 0, 0))],
        out_shape=[_sds((n, w), F32), _sds((n // CHUNK, w, w), F32)],
        scratch_shapes=[pltpu.VMEM((w, w), F32)] + [pltpu.VMEM((tt, w), F32)] * 3,
        compiler_params=_params(),
    )


def _pool_lane_windows():
    lane = lax.broadcasted_iota(jnp.int32, (1, POOL_W), 1) // HEAD
    wl = jnp.zeros((1, POOL_W), F32)
    for gi, win in enumerate(POOL_WINDOWS):
        wl = jnp.where(lane == gi, float(win), wl)
    return lane, wl


def _pool_select(lane, parts):
    out = parts[-1]
    for gi in range(len(parts) - 2, -1, -1):
        out = jnp.where(lane == gi, parts[gi], out)
    return out


def _pool_mix(u, halo, t0, tt):
    lane, wl = _pool_lane_windows()
    ext = jnp.concatenate([halo, u], axis=0)
    sums, cur, shift = [], ext, 1
    for _ in POOL_WINDOWS:
        cur = cur + pltpu.roll(cur, shift, axis=0)
        sums.append(cur[POOL_HALO:, :])
        shift *= 2
    tpos = (t0 + lax.broadcasted_iota(jnp.int32, (tt, POOL_W), 0)).astype(F32)
    cnt = jnp.minimum(tpos + 1.0, wl)
    return _pool_select(lane, sums) / cnt - u, cnt


def _pool_specs(tt, nt, nhb):
    cu, cg = C_UB // POOL_W, C_GB // POOL_W
    return [pl.BlockSpec((tt, POOL_W), lambda b, i: (b * nt + i, cu)),
            pl.BlockSpec((tt, POOL_W), lambda b, i: (b * nt + i, cg)),
            pl.BlockSpec((POOL_HALO, POOL_W), lambda b, i: (jnp.maximum((b * nt + i) * nhb - 1, 0), cu))]


def _pool_fwd(proj, wbd, scale, nb, name):
    n = proj.shape[0]
    t = n // nb
    tt = min(256, t)
    nt = t // tt
    nhb = tt // POOL_HALO

    def body(u_ref, g_ref, h_ref, w_ref, s_ref, o_ref):
        i = pl.program_id(1)
        halo = jnp.where(i == 0, 0.0, h_ref[...])
        pooled, _ = _pool_mix(u_ref[...], halo, i * tt, tt)
        gv = g_ref[...]
        o_ref[...] = _dot(pooled.astype(BF16), w_ref[...]) * s_ref[...] * (gv * _sigmoid(gv))

    return _pc(
        body, name, grid=(nb, nt),
        in_specs=_pool_specs(tt, nt, nhb) + [pl.BlockSpec((POOL_W, POOL_W), lambda b, i: (0, 0)),
                                             pl.BlockSpec((1, POOL_W), lambda b, i: (0, 0))],
        out_specs=pl.BlockSpec((tt, POOL_W), lambda b, i: (b * nt + i, 0)),
        out_shape=_sds((n, POOL_W), F32),
        compiler_params=_params(),
    )(proj, proj, proj, wbd, scale)


def _rows_reduce(x, op, final):
    while x.shape[0] > 8 and x.shape[0] % 16 == 0:
        half = x.shape[0] // 2
        x = op(x[:half], x[half:])
    return final(x, axis=0, keepdims=True)


def _tri_pair(step, n):
    a = sum([(step >= r * (r + 1) // 2).astype(jnp.int32) for r in range(1, n)], jnp.int32(0))
    return a, step - a * (a + 1) // 2


def _lane_lo():
    return lax.broadcasted_iota(jnp.int32, (1, 2 * HEAD), 1) < HEAD


def _put_col(tile, hh, colv):
    lane = lax.broadcasted_iota(jnp.int32, tile.shape, 1)
    return jnp.where(lane == hh, colv, tile)


def _fox_fwd(proj, kt, c_col, c_row, nb, name, rider=None):
    n = proj.shape[0]
    t = n // nb
    tb = min(256, t)
    nq = t // tb
    pw = 2 * HEAD

    def body(q_ref, kt_ref, v_ref, cc_ref, cr_ref, o_ref, lse_ref, m_s, acc_s, cq_s):
        qi, kj = _tri_pair(pl.program_id(1), nq)

        @pl.when(kj == 0)
        def _():
            m_s[...] = jnp.full_like(m_s, -jnp.inf)
            acc_s[...] = jnp.zeros_like(acc_s)
            for hh in range(FOX_HEADS):
                cq_s[hh] = jnp.broadcast_to(cc_ref[:, hh:hh + 1], (tb, pw))

        def block(masked):
            lo = _lane_lo()
            if masked:
                causal = lax.broadcasted_iota(jnp.int32, (tb, tb), 0) >= lax.broadcasted_iota(jnp.int32, (tb, tb), 1)
            def lanes(hh):
                return lo if hh % 2 == 0 else jnp.logical_not(lo)

            def scores(hh):
                sl = slice((hh // 2) * pw, (hh // 2 + 1) * pw)
                return _dot(jnp.where(lanes(hh), q_ref[:, sl] * FOX_SCALE, 0.0).astype(BF16), kt_ref[sl, :])

            ahead = scores(0)
            for hh in range(FOX_HEADS):
                s = ahead
                if hh + 1 < FOX_HEADS:
                    ahead = scores(hh + 1)
                s = s + (jnp.tile(cq_s[hh], (1, tb // pw)) - cr_ref[hh:hh + 1, :])
                if masked:
                    s = jnp.where(causal, s, MASK_VALUE)
                m_prev = m_s[hh]
                m_new = jnp.maximum(m_prev, jnp.max(s, axis=1, keepdims=True))
                alpha = jnp.exp(m_prev - m_new)
                pe = jnp.exp(s - jnp.tile(m_new, (1, tb // pw)))
                m_s[hh] = m_new
                vf = v_ref[:, (hh // 2) * pw:(hh // 2 + 1) * pw]
                acc_s[hh] = alpha * acc_s[hh] + _dot(pe.astype(BF16), jnp.where(lanes(hh), vf, 1.0).astype(BF16))

        @pl.when(kj < qi)
        def _():
            block(False)

        @pl.when(kj == qi)
        def _():
            block(True)
            lo = _lane_lo()
            m_all, l_all = jnp.zeros((tb, FC_PAD), F32), jnp.ones((tb, FC_PAD), F32)
            for p in range(FOX_HEADS // 2):
                halves = []
                for h in range(2):
                    hh = 2 * p + h
                    lm = lo if h == 0 else jnp.logical_not(lo)
                    acc = acc_s[hh]
                    swapped = pltpu.roll(acc, HEAD, axis=1)
                    halves.append(acc / swapped)
                    m_all = _put_col(m_all, hh, m_s[hh])
                    l_all = _put_col(l_all, hh, jnp.where(lm, swapped, acc))
                o_ref[:, p * pw:(p + 1) * pw] = jnp.where(lo, halves[0], halves[1])
            lse_ref[...] = m_all + jnp.log(l_all)

    def qspec(wd, j):
        return pl.BlockSpec((tb, wd), lambda b, st: (b * nq + _tri_pair(st, nq)[0], j))

    def kspec(j):
        return pl.BlockSpec((tb, FOX_W), lambda b, st: (b * nq + _tri_pair(st, nq)[1], j))

    return _call(
        body, name, (proj, kt, proj, c_col, c_row), rider, grid=(nb, nq * (nq + 1) // 2),
        in_specs=[qspec(FOX_W, C_QC // FOX_W), pl.BlockSpec((FOX_W, tb), lambda b, st: (0, b * nq + _tri_pair(st, nq)[1])),
                  kspec(C_VC // FOX_W), qspec(FC_PAD, 0),
                  pl.BlockSpec((None, FOX_HEADS, tb), lambda b, st: (b, 0, _tri_pair(st, nq)[1]))],
        out_specs=[qspec(FOX_W, 0), qspec(FC_PAD, 0)],
        out_shape=[_sds((n, FOX_W), F32), _sds((n, FC_PAD), F32)],
        scratch_shapes=[pltpu.VMEM((FOX_HEADS, tb, pw), F32), pltpu.VMEM((FOX_HEADS, tb, pw), F32),
                        pltpu.VMEM((FOX_HEADS, tb, pw), F32)],
        compiler_params=_params(),
    )


def _head_mean(x, ones_f):
    return _dot_sel(x, ones_f) * (1.0 / HEAD)


def _merge_fwd(x, proj, o_h, o_b, o_c, gh, w_out, g_post, name, target=None):
    n, d = x.shape
    tm = min(512, n)

    def body(*refs):
        x_ref, ga_ref, gc_ref, oh_ref, ob_ref, oc_ref, gh_ref, w_ref, gp_ref = refs[:9]
        if target is None:
            xo_ref, mixt_ref, y_ref = refs[9:]
        else:
            t_ref, dx_ref, sq_ref, mixt_ref, y_ref = refs[9:]
        oh = oh_ref[...]
        ones_f = _block_ones(HGRN_W, F32)
        na = oh * lax.rsqrt(_head_mean(oh * oh, ones_f) + NORM_EPS) * gh_ref[...]
        ga, gc = ga_ref[...], gc_ref[...]
        mixed = jnp.concatenate([na * (ga * _sigmoid(ga)), ob_ref[...], oc_ref[...] * (gc * _sigmoid(gc))], axis=1)
        mixt_ref[...] = mixed.T.astype(BF16)
        y = _dot(mixed.astype(BF16), w_ref[...])
        y_ref[...] = y
        xn = x_ref[...] + y * lax.rsqrt(jnp.mean(y * y, axis=-1, keepdims=True) + NORM_EPS) * gp_ref[...]
        if target is None:
            xo_ref[...] = xn
        else:
            @pl.when(pl.program_id(0) == 0)
            def _():
                sq_ref[...] = jnp.zeros_like(sq_ref)

            e = xn - t_ref[...]
            dx_ref[...] = e * (1.0 / d)
            sq_ref[...] += jnp.sum(e * e, axis=0, keepdims=True)

    def row(wd, j=0):
        return pl.BlockSpec((tm, wd), lambda i: (i, j))

    def full(a, b):
        return pl.BlockSpec((a, b), lambda i: (0, 0))

    head = [] if target is None else [target]
    res = _pc(
        body, name, grid=(n // tm,),
        in_specs=[row(d), row(HGRN_W, C_GA // HGRN_W), row(FOX_W, C_GC // FOX_W), row(HGRN_W), row(POOL_W), row(FOX_W),
                  full(1, HGRN_W), full(d, d), full(1, d)] + [row(d)] * len(head),
        out_specs=[row(d)] + [full(1, d)] * len(head) + [pl.BlockSpec((d, tm), lambda i: (0, i)), row(d)],
        out_shape=[_sds((n, d), F32)] + [_sds((1, d), F32)] * len(head) + [_sds((d, n), BF16), _sds((n, d), F32)],
        compiler_params=_params(),
    )(x, proj, proj, o_h, o_b, o_c, gh, w_out, g_post, *head)
    return (res[0], res[1], res[2]) if target is None else ((res[0], res[1]), res[2], res[3])


def _rms_bwd(dy_scaled, xhat, r):
    return r * (dy_scaled - xhat * jnp.mean(dy_scaled * xhat, axis=-1, keepdims=True))


def _merge_bwd(dxo, y, g_post, w_out_t, proj, o_c, name):
    n, d = y.shape
    tm = min(512, n)
    wab = HGRN_W + POOL_W

    def body(dx_ref, y_ref, gp_ref, wt_ref, gc_ref, oc_ref, dy_ref, dm_ref, dgp_ref, da_ref, dat_ref, dg_ref, dl_ref):
        @pl.when(pl.program_id(0) == 0)
        def _():
            dgp_ref[...] = jnp.zeros_like(dgp_ref)

        yv, dxv = y_ref[...], dx_ref[...]
        r = lax.rsqrt(jnp.mean(yv * yv, axis=-1, keepdims=True) + NORM_EPS)
        yh = yv * r
        dgp_ref[...] += jnp.sum(dxv * yh, axis=0, keepdims=True)
        dyb = _rms_bwd(dxv * gp_ref[...], yh, r).astype(BF16)
        dy_ref[...] = dyb
        dm_ref[...] = _dot(dyb, wt_ref[:, :wab])
        dmc = _dot(dyb, wt_ref[:, wab:])
        gc, oc = gc_ref[...], oc_ref[...]
        sg = _sigmoid(gc)
        da = dmc * (gc * sg)
        da_ref[...] = da.astype(BF16)
        dat_ref[...] = da.T.astype(BF16)
        dg_ref[...] = (dmc * oc * (sg * (1.0 + gc * (1.0 - sg)))).astype(BF16)
        rr = lax.broadcasted_iota(jnp.int32, (FOX_W, FC_PAD), 0) // HEAD
        cc = lax.broadcasted_iota(jnp.int32, (FOX_W, FC_PAD), 1)
        dl_ref[...] = _dot_sel(da * oc, (rr == cc).astype(F32))

    def row(wd, j=0):
        return pl.BlockSpec((tm, wd), lambda i: (i, j))

    def full(a, b):
        return pl.BlockSpec((a, b), lambda i: (0, 0))

    return _pc(
        body, name, grid=(n // tm,),
        in_specs=[row(d), row(d), full(1, d), full(d, d), row(FOX_W, C_GC // FOX_W), row(FOX_W)],
        out_specs=[row(d), row(wab), full(1, d), row(FOX_W), pl.BlockSpec((FOX_W, tm), lambda i: (0, i)), row(FOX_W),
                   row(FC_PAD)],
        out_shape=[_sds((n, d), BF16), _sds((n, wab), F32), _sds((1, d), F32), _sds((n, FOX_W), BF16),
                   _sds((FOX_W, n), BF16), _sds((n, FOX_W), BF16), _sds((n, FC_PAD), F32)],
        compiler_params=_params(),
    )(dxo, y, g_post, w_out_t, proj, o_c)


def _w_out_grad(mixt, dy, name):
    d, n = mixt.shape
    rows = d // N_DEV

    def body(a_ref, b_ref, o_ref):
        o_ref[...] = _dot(a_ref[...], b_ref[...]).astype(BF16)

    return _pc(
        body, name, grid=(N_DEV,),
        in_specs=[pl.BlockSpec((rows, n), lambda j: (j, 0)), pl.BlockSpec((n, d), lambda j: (0, 0))],
        out_specs=pl.BlockSpec((None, None, rows, d), lambda j: (j % 2, j // 2, 0, 0)),
        out_shape=_sds((2, N_DEV // 2, rows, d), BF16),
        compiler_params=_params(),
    )(mixt, dy)


def _w_in_grad(ht, pieces, name):
    d, n = ht.shape
    ta, tk = min(512, d), min(512, n)
    nk = n // tk
    arrays = [p for p, _ in pieces]
    widths = [p.shape[1] for p in arrays]
    offs = [sum(widths[:i]) for i in range(len(widths))]
    in_w = MAIN_W + FOX_HEADS
    shard = in_w // N_DEV

    def body(*refs):
        a_ref, p_refs = refs[0], refs[1:1 + len(arrays)]
        o_ref, acc = refs[1 + len(arrays):]
        k = pl.program_id(1)

        @pl.when(k == 0)
        def _():
            acc[...] = jnp.zeros_like(acc)

        a = a_ref[...]
        for pr, off, wd in zip(p_refs, offs, widths):
            for j in range(0, wd, 512):
                jw = min(512, wd - j)
                acc[:, off + j:off + j + jw] += _dot(a, pr[:, j:j + jw])

        @pl.when(k == nk - 1)
        def _():
            for j in range(N_DEV):
                o_ref[j % 2, j // 2] = acc[:, j * shard:(j + 1) * shard].astype(BF16)

    return _pc(
        body, name, grid=(d // ta, nk),
        in_specs=[pl.BlockSpec((ta, tk), lambda i, k: (i, k))] + [pl.BlockSpec((tk, wd), lambda i, k: (k, 0)) for wd in widths],
        out_specs=pl.BlockSpec((2, N_DEV // 2, ta, shard), lambda i, k: (0, 0, i, 0)),
        out_shape=_sds((2, N_DEV // 2, d, shard), BF16),
        scratch_shapes=[pltpu.VMEM((ta, sum(widths)), F32)],
        compiler_params=_params(),
    )(ht, *arrays)


def _hgrn_state_bwd(qs, k, v, b, do, s0, ds1, bmask):
    bl = b[SUB - 1:SUB, :]
    eb, ebl, ekt = jnp.exp(b), jnp.exp(bl), jnp.exp(bl - b)
    qe, ktil = qs * eb, k * ekt
    ds1b, dob = ds1.astype(BF16), do.astype(BF16)
    dv = _dot_nt(ktil.astype(BF16), ds1b)
    dqe = _dot(dob, s0.astype(BF16))
    dktil = _dot(v.astype(BF16), ds1b)
    dbl = jnp.sum(dktil * ktil, axis=0, keepdims=True) + ebl * jnp.sum(s0 * ds1, axis=0, keepdims=True)
    ds0 = ds1 * ebl + _dot_tn(dob, qe.astype(BF16)) * bmask
    return dqe * eb, dktil * ekt, dv, dbl, ds0


def _hgrn_intra_bwd(qs, k, v, do, es, aexp, gexp, dq, dk, dv, put_dq_row):
    dks = [dk[j:j + 8] for j in range(0, SUB, 8)]
    dvs = [dv[j:j + 8] for j in range(0, SUB, 8)]
    for t in range(SUB):
        r = _live_rows(t)
        ge = gexp[t * SUB:t * SUB + r, :] * es[t]
        put_dq_row(t, dq[t:t + 1, :] + jnp.sum(ge * k[:r], axis=0, keepdims=True))
        for j in range(r // 8):
            dks[j] = dks[j] + ge[8 * j:8 * j + 8] * qs[t:t + 1, :]
            dvs[j] = dvs[j] + aexp[t * SUB + 8 * j:t * SUB + 8 * j + 8, :] * do[t:t + 1, :]
    return jnp.concatenate(dks, axis=0), jnp.concatenate(dvs, axis=0)


def _hgrn_bwd(dmix, proj, o_h, s0, gh, lb, ones_b, nb, name):
    n = proj.shape[0]
    t = n // nb
    tt = _hgrn_tile(t)
    nt = t // tt
    ncs = tt // CHUNK
    nsub = CHUNK // SUB
    w = HGRN_W

    def body(dm_ref, q_ref, z_ref, v_ref, ga_ref, oh_ref, s0_ref, gh_ref, lb_ref, ones_ref,
             dp_ref, dgh_ref, dlb_ref, ds_s, ss_s, b_s, qs_s, k_s, do_s, dq_s, dk_s, dv_s, dbl_s):
        first = jnp.logical_and(pl.program_id(0) == 0, pl.program_id(1) == 0)

        @pl.when(first)
        def _():
            dgh_ref[...] = jnp.zeros_like(dgh_ref)
            dlb_ref[...] = jnp.zeros_like(dlb_ref)

        @pl.when(pl.program_id(1) == 0)
        def _():
            ds_s[...] = jnp.zeros_like(ds_s)

        ones_b = ones_ref[...]
        ones_f = ones_b.astype(F32)
        bmask = _block_ones(w, F32)
        lbv, ghv = lb_ref[...], gh_ref[...]
        oh, ga, dm = oh_ref[...], ga_ref[...], dm_ref[...]
        rn = lax.rsqrt(_head_mean(oh * oh, ones_f) + NORM_EPS)
        nh = oh * rn
        sga = _sigmoid(ga)
        dp_ref[:, 3 * w:4 * w] = (dm * nh * ghv * (sga * (1.0 + ga * (1.0 - sga)))).astype(BF16)
        dn = dm * (ga * sga)
        dgh_ref[...] += jnp.sum(dn * nh, axis=0, keepdims=True)
        dn = dn * ghv
        do_s[...] = rn * (dn - nh * _head_mean(dn * nh, ones_f))
        q = q_ref[...]
        sig, sn, f, g, k, sq = _hgrn_gates(q, z_ref[...], lbv)
        qs = q * sq
        b_s[...] = _sel_dot(_sub_tri(tt, True), g)
        qs_s[...] = qs
        k_s[...] = k

        def chunk(cc, carry):
            c = ncs - 1 - cc
            base = pl.multiple_of(c * CHUNK, CHUNK)
            tiles = []
            for u in range(nsub):
                rows = pl.ds(base + u * SUB, SUB)
                tiles.append((qs_s[rows, :], k_s[rows, :], v_ref[rows, :], b_s[rows, :], do_s[rows, :]))
            st = s0_ref[c]
            for u, (qs, k, v, b, do) in enumerate(tiles):
                ss_s[u] = st
                if u < nsub - 1:
                    st = _hgrn_state_step(st, k, v, b, bmask)
            ds = ds_s[...]
            for u in reversed(range(nsub)):
                qs, k, v, b, do = tiles[u]
                _, es, ws = _hgrn_decays(qs, k, b)
                gs = [_pad_rows(do[t:t + 1, :] * v[:_live_rows(t)]) for t in range(SUB)]
                aexp = _dot(jnp.concatenate(ws, axis=0).astype(BF16), ones_b)
                gexp = _dot(jnp.concatenate(gs, axis=0).astype(BF16), ones_b)
                dq, dk, dv, dbl, ds = _hgrn_state_bwd(qs, k, v, b, do, ss_s[u], ds, bmask)

                def put_dq_row(i, row, r0=base + u * SUB):
                    dq_s[pl.ds(r0 + i, 1), :] = row

                dk, dv = _hgrn_intra_bwd(qs, k, v, do, es, aexp, gexp, dq, dk, dv, put_dq_row)
                dk_s[pl.ds(base + u * SUB, SUB), :] = dk
                dv_s[pl.ds(base + u * SUB, SUB), :] = dv
                dbl_s[pl.ds(base + u * SUB, SUB), :] = jnp.broadcast_to(dbl, (SUB, w))
            ds_s[...] = ds
            return carry

        lax.fori_loop(0, ncs, chunk, 0)
        dqs, dk = dq_s[...], dk_s[...]
        dg = _sel_dot(_sub_tri(tt, False), qs * dqs - k * dk) + dbl_s[...]
        dfz = jnp.where(f > TINY, dg / jnp.maximum(f, TINY), 0.0)
        dlb_ref[...] += jnp.sum(dfz * (1.0 - sig) - dk * sn, axis=0, keepdims=True)
        dp_ref[:, 0:w] = (dqs * (sq * (1.0 + q * (1.0 - sq)))).astype(BF16)
        dp_ref[:, w:2 * w] = ((dfz - dk) * (1.0 - lbv) * sig * sn).astype(BF16)
        dp_ref[:, 2 * w:3 * w] = dv_s[...].astype(BF16)

    def rv(b, i):
        return b * nt + (nt - 1 - i)

    def col(j):
        return pl.BlockSpec((tt, w), lambda b, i: (rv(b, i), j))

    def full(a, bb):
        return pl.BlockSpec((a, bb), lambda b, i: (0, 0))

    return _pc(
        body, name, grid=(nb, nt),
        in_specs=[col(0), col(C_QA // w), col(C_FA // w), col(C_IA // w), col(C_GA // w), col(0),
                  pl.BlockSpec((ncs, w, w), lambda b, i: (rv(b, i), 0, 0)), full(1, w), full(1, w), full(w, w)],
        out_specs=[pl.BlockSpec((tt, 4 * w), lambda b, i: (rv(b, i), 0)), full(1, w), full(1, w)],
        out_shape=[_sds((n, 4 * w), BF16), _sds((1, w), F32), _sds((1, w), F32)],
        scratch_shapes=[pltpu.VMEM((w, w), F32), pltpu.VMEM((nsub, w, w), F32)] + [pltpu.VMEM((tt, w), F32)] * 8,
        compiler_params=_params(),
    )(dmix, proj, proj, proj, proj, o_h, s0, gh, lb, ones_b)


def _pool_bwd(dmix, proj, wbd, wbd_t, scale, nb, name):
    n = proj.shape[0]
    t = n // nb
    tt = min(256, t)
    nt = t // tt
    nhb = tt // POOL_HALO
    cu, cg, cm = C_UB // POOL_W, C_GB // POOL_W, HGRN_W // POOL_W

    def body(u_ref, g_ref, h_ref, dm_ref, gn_ref, dmn_ref, w_ref, wt_ref, s_ref, dp_ref, dw_ref, ds_ref):
        i = pl.program_id(1)
        first = jnp.logical_and(pl.program_id(0) == 0, i == 0)

        @pl.when(first)
        def _():
            dw_ref[...] = jnp.zeros_like(dw_ref)
            ds_ref[...] = jnp.zeros_like(ds_ref)

        sc = s_ref[...]
        halo = jnp.where(i == 0, 0.0, h_ref[...])
        pooled, cnt = _pool_mix(u_ref[...], halo, i * tt, tt)
        pb = pooled.astype(BF16)
        pre = _dot(pb, w_ref[...])
        gv, dm = g_ref[...], dm_ref[...]
        sg = _sigmoid(gv)
        silu = gv * sg
        dgb = dm * pre * sc * (sg * (1.0 + gv * (1.0 - sg)))
        ds_ref[...] += jnp.sum(dm * pre * silu, axis=0, keepdims=True)
        dpre = (dm * sc * silu).astype(BF16)
        dw_ref[...] += _dot_tn(pb, dpre)
        dpool = _dot(dpre, wt_ref[...])
        gn = gn_ref[...]
        dpre_n = (dmn_ref[...] * sc * (gn * _sigmoid(gn))).astype(BF16)
        dpool_n = jnp.where(i == nt - 1, 0.0, _dot(dpre_n, wt_ref[...]))
        lane, wl = _pool_lane_windows()
        tpos_n = ((i + 1) * tt + lax.broadcasted_iota(jnp.int32, (POOL_HALO, POOL_W), 0)).astype(F32)
        ext = jnp.concatenate([dpool / cnt, dpool_n / jnp.minimum(tpos_n + 1.0, wl)], axis=0)
        rows = tt + POOL_HALO
        sums, cur, shift = [], ext, 1
        for _ in POOL_WINDOWS:
            cur = cur + pltpu.roll(cur, rows - shift, axis=0)
            sums.append(cur[:tt, :])
            shift *= 2
        du = _pool_select(lane, sums) - dpool
        dp_ref[...] = jnp.concatenate([du, dgb], axis=1).astype(BF16)

    def nxt(b, i):
        return jnp.minimum((b * nt + i + 1) * nhb, n // POOL_HALO - 1)

    return _pc(
        body, name, grid=(nb, nt),
        in_specs=_pool_specs(tt, nt, nhb) + [
            pl.BlockSpec((tt, POOL_W), lambda b, i: (b * nt + i, cm)),
            pl.BlockSpec((POOL_HALO, POOL_W), lambda b, i: (nxt(b, i), cg)),
            pl.BlockSpec((POOL_HALO, POOL_W), lambda b, i: (nxt(b, i), cm)),
            pl.BlockSpec((POOL_W, POOL_W), lambda b, i: (0, 0)), pl.BlockSpec((POOL_W, POOL_W), lambda b, i: (0, 0)),
            pl.BlockSpec((1, POOL_W), lambda b, i: (0, 0))],
        out_specs=[pl.BlockSpec((tt, 2 * POOL_W), lambda b, i: (b * nt + i, 0)),
                   pl.BlockSpec((POOL_W, POOL_W), lambda b, i: (0, 0)), pl.BlockSpec((1, POOL_W), lambda b, i: (0, 0))],
        out_shape=[_sds((n, 2 * POOL_W), BF16), _sds((POOL_W, POOL_W), F32), _sds((1, POOL_W), F32)],
        compiler_params=_params(),
    )(proj, proj, proj, dmix, proj, dmix, wbd, wbd_t, scale)


def _fox_bwd(proj, qt, kt, da, dat, c_col, c_row, lse_row, delta_row, nb, name, rider=None):
    n = proj.shape[0]
    t = n // nb
    tb = min(256, t)
    nq = t // tb
    pw = 2 * HEAD

    def body(q_ref, k_ref, v_ref, da_ref, qt_ref, kt_ref, dat_ref, cc_ref, cr_ref, lse_ref, dl_ref,
             dq_ref, dk_ref, dv_ref, dck_ref, dcq_ref, dq_s, dk_s, dv_s, dck_s, dcq_s):
        step = pl.program_id(1)
        kj, qi = pairs(step)

        @pl.when(step == 0)
        def _():
            dq_s[...] = jnp.zeros_like(dq_s)
            dcq_s[...] = jnp.zeros_like(dcq_s)

        @pl.when(qi == nq - 1)
        def _():
            dk_s[...] = jnp.zeros_like(dk_s)
            dv_s[...] = jnp.zeros_like(dv_s)
            dck_s[...] = jnp.zeros_like(dck_s)

        def block(masked):
            lo = _lane_lo()
            if masked:
                causal = lax.broadcasted_iota(jnp.int32, (tb, tb), 1) >= lax.broadcasted_iota(jnp.int32, (tb, tb), 0)
            dck = dck_s[...]
            for p in range(FOX_HEADS // 2):
                sl = slice(p * pw, (p + 1) * pw)
                qp = q_ref[:, sl] * FOX_SCALE
                kp = k_ref[:, sl].astype(BF16)
                vp = v_ref[:, sl].astype(BF16)
                dap = da_ref[:, sl]
                dk, dv = dk_s[:, sl], dv_s[:, sl]
                for h in range(2):
                    hh = 2 * p + h
                    lm = lo if h == 0 else jnp.logical_not(lo)
                    rows = slice(hh * HEAD, (hh + 1) * HEAD)
                    none = jnp.zeros((HEAD, tb), BF16)
                    qm = jnp.where(lm, qp, 0.0).astype(BF16)
                    dam = jnp.where(lm, dap, jnp.zeros_like(dap))
                    qtm = jnp.concatenate([qt_ref[rows, :], none] if h == 0 else [none, qt_ref[rows, :]], axis=0)
                    datm = jnp.concatenate([dat_ref[rows, :], none] if h == 0 else [none, dat_ref[rows, :]], axis=0)
                    s = _dot(kp, qtm) + (cr_ref[hh:hh + 1, :] - cc_ref[:, hh:hh + 1])
                    pe = jnp.exp(s - lse_ref[hh:hh + 1, :])
                    if masked:
                        pe = jnp.where(causal, pe, 0.0)
                    dp = _dot(vp, datm)
                    ds = pe * (dp - dl_ref[hh:hh + 1, :])
                    dsb = ds.astype(BF16)
                    dv = dv + _dot(pe.astype(BF16), dam)
                    dk = dk + _dot(dsb, qm)
                    dq_s[qi, rows, :] += _dot(kt_ref[rows, :], dsb)
                    dck = dck - _put_col(jnp.zeros_like(dck), hh, jnp.sum(ds, axis=1, keepdims=True))
                    dcq_s[qi, hh:hh + 1, :] += _rows_reduce(ds, jnp.add, jnp.sum)
                dk_s[:, sl] = dk
                dv_s[:, sl] = dv
            dck_s[...] = dck

        @pl.when(qi > kj)
        def _():
            block(False)

        @pl.when(qi == kj)
        def _():
            block(True)

        @pl.when(qi == kj)
        def _():
            dk_ref[...] = dk_s[...].astype(BF16)
            dv_ref[...] = dv_s[...].astype(BF16)
            dck_ref[...] = dck_s[...]

        @pl.when(step == nq * (nq + 1) // 2 - 1)
        def _():
            for j in range(nq):
                dq_ref[j * tb:(j + 1) * tb, :] = (dq_s[j].T * FOX_SCALE).astype(BF16)
                dcq_ref[:, j * tb:(j + 1) * tb] = dcq_s[j]

    def pairs(step):
        a, b = _tri_pair(step, nq)
        return nq - 1 - a, nq - 1 - b

    def kspec(wd, j=0):
        return pl.BlockSpec((tb, wd), lambda b, st: (b * nq + pairs(st)[0], j))

    def qspec(wd, j=0):
        return pl.BlockSpec((tb, wd), lambda b, st: (b * nq + pairs(st)[1], j))

    def qrow():
        return pl.BlockSpec((None, FOX_HEADS, tb), lambda b, st: (b, 0, pairs(st)[1]))

    def tspec(which):
        return pl.BlockSpec((FOX_W, tb), lambda b, st: (0, b * nq + pairs(st)[which]))

    return _call(
        body, name, (proj, proj, proj, da, qt, kt, dat, c_col, c_row, lse_row, delta_row), rider,
        grid=(nb, nq * (nq + 1) // 2),
        in_specs=[qspec(FOX_W, C_QC // FOX_W), kspec(FOX_W, C_KC // FOX_W), kspec(FOX_W, C_VC // FOX_W), qspec(FOX_W),
                  tspec(1), tspec(0), tspec(1), kspec(FC_PAD), qrow(), qrow(), qrow()],
        out_specs=[pl.BlockSpec((t, FOX_W), lambda b, st: (b, 0)), kspec(FOX_W), kspec(FOX_W), kspec(FC_PAD),
                   pl.BlockSpec((None, FOX_HEADS, t), lambda b, st: (b, 0, 0))],
        out_shape=[_sds((n, FOX_W), BF16), _sds((n, FOX_W), BF16), _sds((n, FOX_W), BF16), _sds((n, FC_PAD), F32),
                   _sds((nb, FOX_HEADS, t), F32)],
        scratch_shapes=[pltpu.VMEM((nq, FOX_W, tb), F32), pltpu.VMEM((tb, FOX_W), F32), pltpu.VMEM((tb, FOX_W), F32),
                        pltpu.VMEM((tb, FC_PAD), F32), pltpu.VMEM((nq, FOX_HEADS, tb), F32)],
        compiler_params=_params(),
    )


def _fox_decay_bwd(dc_q, dc_k, fc, bias, nb, name):
    n = fc.shape[0]
    t = n // nb
    tt = min(256, t)
    nt = t // tt

    def body(dcq_ref, dck_ref, fc_ref, b_ref, dfc_ref, db_ref, carry):
        i = pl.program_id(1)
        first = jnp.logical_and(pl.program_id(0) == 0, i == 0)

        @pl.when(first)
        def _():
            db_ref[...] = jnp.zeros_like(db_ref)

        @pl.when(i == 0)
        def _():
            carry[...] = jnp.zeros_like(carry)

        r = lax.broadcasted_iota(jnp.int32, (tt, tt), 0)
        cc = lax.broadcasted_iota(jnp.int32, (tt, tt), 1)
        dlf = _dot_hi((r <= cc).astype(F32), dcq_ref[...] + dck_ref[...]) + carry[...]
        carry[...] = dlf[0:1, :]
        dfc = dlf * _sigmoid(-(fc_ref[...] + b_ref[...]))
        dfc_ref[...] = dfc.astype(BF16)
        db_ref[...] += jnp.sum(dfc, axis=0, keepdims=True)

    def row():
        return pl.BlockSpec((tt, FC_PAD), lambda b, i: (b * nt + (nt - 1 - i), 0))

    return _pc(
        body, name, grid=(nb, nt),
        in_specs=[row(), row(), row(), pl.BlockSpec((1, FC_PAD), lambda b, i: (0, 0))],
        out_specs=[row(), pl.BlockSpec((1, FC_PAD), lambda b, i: (0, 0))],
        out_shape=[_sds((n, FC_PAD), BF16), _sds((1, FC_PAD), F32)],
        scratch_shapes=[pltpu.VMEM((1, FC_PAD), F32)],
        compiler_params=_params(),
    )(dc_q, dc_k, fc, bias)


def _in_proj_bwd(pieces, w_main_t, w_fc_t, x, g_pre, dxo, name, rider=None):
    n, d = x.shape
    tm = min(512, n)
    widths = [p.shape[1] for p, _ in pieces]
    offs = [o for _, o in pieces]
    np_ = len(pieces)

    def body(*refs):
        p_refs = refs[:np_]
        wt_ref, wf_ref, x_ref, g_ref, dxo_ref, dx_ref, dg_ref = refs[np_:]

        @pl.when(pl.program_id(0) == 0)
        def _():
            dg_ref[...] = jnp.zeros_like(dg_ref)

        dh = _dot(p_refs[-1][...], wf_ref[...])
        for pr, wd, off in zip(p_refs[:-1], widths[:-1], offs[:-1]):
            for j in range(0, wd, 512):
                jw = min(512, wd - j)
                dh = dh + _dot(pr[:, j:j + jw], wt_ref[off + j:off + j + jw, :])
        xv = x_ref[...]
        r = lax.rsqrt(jnp.mean(xv * xv, axis=-1, keepdims=True) + NORM_EPS)
        xh = xv * r
        dg_ref[...] += jnp.sum(dh * xh, axis=0, keepdims=True)
        dx_ref[...] = dxo_ref[...] + _rms_bwd(dh * g_ref[...], xh, r)

    row = pl.BlockSpec((tm, d), lambda i: (i, 0))
    return _call(
        body, name, (*[p for p, _ in pieces], w_main_t, w_fc_t, x, g_pre, dxo), rider, grid=(n // tm,),
        in_specs=[pl.BlockSpec((tm, wd), lambda i: (i, 0)) for wd in widths] + [
            pl.BlockSpec((MAIN_W, d), lambda i: (0, 0)), pl.BlockSpec((FC_PAD, d), lambda i: (0, 0)),
            row, pl.BlockSpec((1, d), lambda i: (0, 0)), row],
        out_specs=[row, pl.BlockSpec((1, d), lambda i: (0, 0))],
        out_shape=[_sds((n, d), F32), _sds((1, d), F32)],
        compiler_params=_params(),
    )


def _lower_bound_table(lower_bounds, name):
    depth, w = lower_bounds.shape

    def body(lb_ref, o_ref):
        v = lb_ref[...]
        e = jnp.exp(v - jnp.max(v, axis=0, keepdims=True))
        p = e / jnp.sum(e, axis=0, keepdims=True)
        acc = jnp.zeros((1, w), F32)
        for l in range(depth):
            acc = acc + p[l:l + 1, :]
            o_ref[l:l + 1, :] = acc - p[0:1, :]

    return _pc(body, name, out_shape=_sds((depth, w), F32))(lower_bounds)


def _lower_bound_bwd(lower_bounds, dlbs, name):
    depth, w = lower_bounds.shape

    def body(lb_ref, d_ref, o_ref):
        v, dl = lb_ref[...], d_ref[...]
        e = jnp.exp(v - jnp.max(v, axis=0, keepdims=True))
        p = e / jnp.sum(e, axis=0, keepdims=True)
        tot = jnp.sum(dl, axis=0, keepdims=True)
        rows, tail = [], tot
        for l in range(depth):
            rows.append(tail - tot if l == 0 else tail)
            tail = tail - dl[l:l + 1, :]
        dp = jnp.concatenate(rows, axis=0)
        o_ref[...] = p * (dp - jnp.sum(p * dp, axis=0, keepdims=True))

    return _pc(body, name, out_shape=_sds((depth, w), F32))(lower_bounds, dlbs)


def _place():
    x, y, c = lax.axis_index("x"), lax.axis_index("y"), lax.axis_index("c")
    return x, y, c


def _gather_weights(*arrays):
    na = len(arrays)

    def body(*refs):
        ins, outs = refs[:na], refs[na:2 * na]
        send_sems, recv_sems, local_sems = refs[2 * na:]
        x, y, c = _place()
        me, sibling = (x, y, c), (x, y, 1 - c)
        chips = [(1 - x, y), (x, 1 - y), (1 - x, 1 - y)]

        def slot(a, px, py, pc):
            return outs[a].at[4 * px + 2 * py + pc]

        def copy(a, k, block, to, own=False):
            return pltpu.make_async_remote_copy(
                src_ref=ins[a] if own else slot(a, *block), dst_ref=slot(a, *block),
                send_sem=send_sems.at[a * 7 + k], recv_sem=recv_sems.at[a * 7 + k],
                device_id=to, device_id_type=MESH)

        mine = [pltpu.make_async_copy(ins[a], slot(a, *me), local_sems.at[a]) for a in range(na)]
        for cp in mine:
            cp.start()
        first = []
        for a in range(na):
            first.append(copy(a, 0, me, sibling, own=True))
            first += [copy(a, 1 + j, me, (*chip, c), own=True) for j, chip in enumerate(chips)]
        for cp in first:
            cp.start()
        passed = []
        for j, chip in enumerate(chips):
            for a in range(na):
                copy(a, 1 + j, (*chip, c), me).wait_recv()
                fw = copy(a, 4 + j, (*chip, c), sibling)
                fw.start()
                passed.append(fw)
        for a in range(na):
            copy(a, 0, sibling, me).wait_recv()
            for j, chip in enumerate(chips):
                copy(a, 4 + j, (*chip, 1 - c), me).wait_recv()
        for cp in first + passed:
            cp.wait_send()
        for cp in mine:
            cp.wait()

    any_spec = pl.BlockSpec(memory_space=pl.ANY)
    return _pc(
        body, "gather_weights",
        in_specs=[any_spec] * na, out_specs=[any_spec] * na,
        out_shape=[_sds((N_DEV,) + a.shape, a.dtype) for a in arrays],
        scratch_shapes=[pltpu.SemaphoreType.DMA((7 * na,)), pltpu.SemaphoreType.DMA((7 * na,)),
                        pltpu.SemaphoreType.DMA((na,))],
    )(*arrays)


def _peer(k):
    x, y, c = _place()
    return (1 - x if k & 4 else x, 1 - y if k & 2 else y, 1 - c if k & 1 else c)


def _remote(src, dst, sems, s, to):
    return pltpu.make_async_remote_copy(src_ref=src, dst_ref=dst, send_sem=sems[0].at[s], recv_sem=sems[1].at[s],
                                        device_id=to, device_id_type=MESH)


def _gather_rider(shards):
    na = len(shards)

    def plan(ins, outs, *sems):
        x, y, c = _place()
        me = 4 * x + 2 * y + c
        locs = [pltpu.make_async_copy(ins[a], outs[a].at[me], sems[2].at[a]) for a in range(na)]
        sends, recvs = [], []
        for k in range(1, N_DEV):
            px, py, pc = _peer(k)
            for a in range(na):
                s = (k - 1) * na + a
                sends.append(_remote(ins[a], outs[a].at[me], sems, s, (px, py, pc)))
                recvs.append(_remote(ins[a], outs[a].at[4 * px + 2 * py + pc], sems, s, (px, py, pc)))
        return sends, recvs, locs

    return _Rider(shards, [_sds((N_DEV,) + a.shape, a.dtype) for a in shards], (N_DEV - 1) * na, na, plan)


def _direct_exchange_rider(blocks):
    na = len(blocks)

    def plan(ins, outs, *sems):
        x, y, c = _place()
        me = 4 * x + 2 * y + c
        locs = [pltpu.make_async_copy(ins[a].at[c, 2 * x + y], outs[a].at[me], sems[2].at[a]) for a in range(na)]
        sends, recvs = [], []
        for k in range(1, N_DEV):
            px, py, pc = _peer(k)
            for a in range(na):
                s = (k - 1) * na + a
                sends.append(_remote(ins[a].at[pc, 2 * px + py], outs[a].at[me], sems, s, (px, py, pc)))
                recvs.append(_remote(ins[a].at[pc, 2 * px + py], outs[a].at[4 * px + 2 * py + pc], sems, s, (px, py, pc)))
        return sends, recvs, locs

    return _Rider(blocks, [_sds((N_DEV,) + a.shape[2:], a.dtype) for a in blocks], (N_DEV - 1) * na, na, plan)


def _swap_rider(halves):
    na = len(halves)

    def plan(ins, outs, *sems):
        x, y, c = _place()
        cps = [_remote(ins[a].at[1 - c], outs[a], sems, a, (x, y, 1 - c)) for a in range(na)]
        return cps, cps, []

    return _Rider(halves, [_sds(a.shape[1:], a.dtype) for a in halves], na, 1, plan)


def _chip_exchange_rider(parts, small=None):
    na = len(parts)
    n_chip = N_DEV // 2

    def plan(ins, outs, *sems):
        x, y, c = _place()
        chip = 2 * x + y
        locs = [pltpu.make_async_copy(ins[a].at[chip], outs[a].at[chip], sems[2].at[a]) for a in range(na)]
        sends, recvs = [], []
        for k in range(1, n_chip):
            px, py, _ = _peer(2 * k)
            for a in range(na):
                s = (k - 1) * na + a
                sends.append(_remote(ins[a].at[2 * px + py], outs[a].at[chip], sems, s, (px, py, c)))
                recvs.append(_remote(ins[a].at[2 * px + py], outs[a].at[2 * px + py], sems, s, (px, py, c)))
        if small is not None:
            me = 2 * chip + c
            locs.append(pltpu.make_async_copy(ins[na], outs[na].at[me], sems[2].at[na]))
            for k in range(1, N_DEV):
                px, py, pc = _peer(k)
                s = (n_chip - 1) * na + k - 1
                sends.append(_remote(ins[na], outs[na].at[me], sems, s, (px, py, pc)))
                recvs.append(_remote(ins[na], outs[na].at[4 * px + 2 * py + pc], sems, s, (px, py, pc)))
        return sends, recvs, locs

    extra = [] if small is None else [small]
    shapes = [_sds(a.shape, a.dtype) for a in parts] + [_sds((N_DEV,) + s.shape, s.dtype) for s in extra]
    n_sems = (n_chip - 1) * na + (N_DEV - 1) * len(extra)
    return _Rider(list(parts) + extra, shapes, n_sems, na + len(extra), plan)


def _pair_add(halves, other, core, name):
    _, nch, r, c = halves.shape

    def body(c_ref, h_ref, o_ref, p_ref):
        p_ref[...] = (h_ref[...].astype(F32) + o_ref[...].astype(F32)).astype(BF16)

    blk = pl.BlockSpec((None, r, c), lambda j, c_ref: (j, 0, 0))
    return _pc(
        body, name,
        grid_spec=pltpu.PrefetchScalarGridSpec(
            num_scalar_prefetch=1, grid=(nch,),
            in_specs=[pl.BlockSpec((None, None, r, c), lambda j, c_ref: (c_ref[0], j, 0, 0)), blk], out_specs=blk),
        out_shape=_sds((nch, r, c), BF16),
        compiler_params=_params(),
    )(core, halves, other)


def _sum_adamw(parts, w, m, v, name, rider=None):
    nl, r, c = w.shape
    tr = 256 if r % 256 == 0 else r

    def body(*refs):
        p_refs = refs[:nl]
        w_ref, m_ref, v_ref, g_ref, d_ref, mo_ref, vo_ref = refs[nl:]
        for l in range(nl):
            @pl.when(pl.program_id(0) == l)
            def _(p_ref=p_refs[l]):
                g = p_ref[0].astype(F32)
                for j in range(1, p_ref.shape[0]):
                    g = g + p_ref[j].astype(F32)
                mn = ADAM_B1 * m_ref[...] + (1.0 - ADAM_B1) * g
                vn = ADAM_B2 * v_ref[...] + (1.0 - ADAM_B2) * (g * g)
                m_hat = mn / (1.0 - ADAM_B1 ** ADAM_STEP)
                v_hat = vn / (1.0 - ADAM_B2 ** ADAM_STEP)
                g_ref[...] = g
                d_ref[...] = -ADAM_LR * (m_hat / (jnp.sqrt(v_hat) + ADAM_EPS) + ADAM_WD * w_ref[...])
                mo_ref[...] = mn
                vo_ref[...] = vn

    def part_spec(l, k):
        return pl.BlockSpec((k, tr, c), lambda li, i: (0, jnp.where(li == l, i, 0), 0))

    row = pl.BlockSpec((None, tr, c), lambda li, i: (li, i, 0))
    return _call(
        body, name, (*parts, w, m, v), rider, grid=(nl, r // tr),
        in_specs=[part_spec(l, p.shape[0]) for l, p in enumerate(parts)] + [row, row, row],
        out_specs=[row] * 4,
        out_shape=[_sds((nl, r, c), F32)] * 4,
        compiler_params=_params(),
    )


SMALL = ("lower_bounds", "pre_norm_g", "hgrn_norm_g", "fox_f_bias", "pool_w", "pool_scale", "post_norm_g")
SMALL_LANES = 128


def _small_size(tree):
    return sum(tree[k].size for k in SMALL)


def _pack_small(tree, extra=None):
    flat = jnp.concatenate([tree[k].reshape(-1) for k in SMALL] + ([] if extra is None else [extra.reshape(1)]))
    rows = -(-(_small_size(tree) + 1) // (8 * SMALL_LANES)) * 8
    return jnp.pad(flat, (0, rows * SMALL_LANES - flat.shape[0])).reshape(rows, SMALL_LANES)


def _unpack_small(packed, like):
    flat, out, off = packed.reshape(-1), {}, 0
    for k in SMALL:
        size = like[k].size
        out[k] = flat[off:off + size].reshape(like[k].shape)
        off += size
    return out


def _block_diag(pw):
    g = pw.shape[0]
    eye = jnp.eye(g, dtype=pw.dtype)
    return (eye[:, None, :, None] * pw[:, :, None, :]).reshape(g * HEAD, g * HEAD)


def _assemble_w_in(g_in, name):
    _, d, shard = g_in.shape
    tr = min(256, d)
    wide = MAIN_W + FC_PAD

    def body(g_ref, wm_ref, wf_ref, wmt_ref, wft_ref, row_s):
        row_s[:, MAIN_W:] = jnp.zeros((tr, FC_PAD), F32)
        for j in range(N_DEV):
            row_s[:, j * shard:(j + 1) * shard] = g_ref[j].astype(F32)
        wm_ref[...] = row_s[:, :MAIN_W].astype(BF16)
        wf_ref[...] = row_s[:, MAIN_W:].astype(BF16)
        for j in range(0, MAIN_W, 512):
            wmt_ref[j:j + 512, :] = row_s[:, j:j + 512].T.astype(BF16)
        wft_ref[...] = row_s[:, MAIN_W:].T.astype(BF16)

    return _pc(
        body, name, grid=(d // tr,),
        in_specs=[pl.BlockSpec((N_DEV, tr, shard), lambda i: (0, i, 0))],
        out_specs=[pl.BlockSpec((tr, MAIN_W), lambda i: (i, 0)), pl.BlockSpec((tr, FC_PAD), lambda i: (i, 0)),
                   pl.BlockSpec((MAIN_W, tr), lambda i: (0, i)), pl.BlockSpec((FC_PAD, tr), lambda i: (0, i))],
        out_shape=[_sds((d, MAIN_W), BF16), _sds((d, FC_PAD), BF16), _sds((MAIN_W, d), BF16), _sds((FC_PAD, d), BF16)],
        scratch_shapes=[pltpu.VMEM((tr, wide), F32)],
        compiler_params=_params(),
    )(g_in)


def _w_out_parts(g_out):
    full_out = g_out.reshape(N_DEV * g_out.shape[1], g_out.shape[2])
    return full_out, full_out.T


def _layer_fwd(l, x, lbs, weights, lw, nb, rider_h=None, rider_c=None, target=None):
    n = x.shape[0]
    t = n // nb
    w_main, w_fc, _, _, w_out, _ = lw
    bias = jnp.pad(weights["fox_f_bias"][l:l + 1], ((0, 0), (0, FC_PAD - FOX_HEADS)))
    wbd = _block_diag(weights["pool_w"][l]).astype(BF16)
    proj, fc, ht, qt, kt = _in_proj_fwd(x, weights["pre_norm_g"][l:l + 1], w_main, w_fc, f"in_proj_fwd_{l}")
    c_col = _fox_decay_fwd(fc, bias, nb, f"fox_decay_fwd_{l}")
    c_row = c_col.reshape(nb, t, FC_PAD)[:, :, :FOX_HEADS].transpose(0, 2, 1)
    (o_h, s0), rode_h = _hgrn_fwd(proj, lbs[l:l + 1], _block_ones(HGRN_W, BF16), nb, f"hgrn_fwd_{l}", rider_h)
    o_b = _pool_fwd(proj, wbd, weights["pool_scale"][l:l + 1], nb, f"pool_fwd_{l}")
    (o_c, lse), rode_c = _fox_fwd(proj, kt, c_col, c_row, nb, f"fox_fwd_{l}", rider_c)
    if w_out is None:
        lw = tuple(lw[:4]) + _w_out_parts(rode_h[0])
        w_out, rode_h = lw[4], rode_h[1:]
    x_next, mixt, y = _merge_fwd(x, proj, o_h, o_b, o_c, weights["hgrn_norm_g"][l:l + 1], w_out,
                                 weights["post_norm_g"][l:l + 1], f"merge_fwd_{l}", target)
    return x_next, (x, proj, fc, ht, qt, kt, c_col, c_row, o_h, s0, o_c, lse, mixt, y, bias, wbd), lw, (rode_h, rode_c)


def _layer_bwd(l, dx, saved, lbs, weights, lw, nb, rider=None):
    _, proj, fc, ht, qt, kt, c_col, c_row, o_h, s0, o_c, lse, mixt, y, bias, wbd = saved
    n = proj.shape[0]
    t = n // nb
    w_out_t = lw[5]
    g = {}
    dy, dmix, dgp, da, dat, d_gc, delta = _merge_bwd(dx, y, weights["post_norm_g"][l:l + 1], w_out_t, proj, o_c,
                                                f"merge_bwd_{l}")
    g["post_norm_g"] = dgp[0]
    g["w_out"] = _w_out_grad(mixt, dy, f"w_out_grad_{l}")
    d_a, dgh, dlb = _hgrn_bwd(dmix, proj, o_h, s0, weights["hgrn_norm_g"][l:l + 1], lbs[l:l + 1],
                              _block_ones(HGRN_W, BF16), nb, f"hgrn_bwd_{l}")
    g["hgrn_norm_g"], g["lbs"] = dgh[0], dlb[0]
    d_b, dwbd, dps = _pool_bwd(dmix, proj, wbd, wbd.T, weights["pool_scale"][l:l + 1], nb, f"pool_bwd_{l}")
    g["pool_w"] = jnp.stack([dwbd[j * HEAD:(j + 1) * HEAD, j * HEAD:(j + 1) * HEAD] for j in range(len(POOL_WINDOWS))])
    g["pool_scale"] = dps[0]
    delta_row, lse_row = [a.reshape(nb, t, FC_PAD)[:, :, :FOX_HEADS].transpose(0, 2, 1) for a in (delta, lse)]
    (d_qc, d_kc, d_vc, dc_k, dc_q), rode = _fox_bwd(proj, qt, kt, da, dat, c_col, c_row, lse_row, delta_row, nb,
                                                    f"fox_bwd_{l}", rider)
    dc_q = jnp.pad(dc_q.transpose(0, 2, 1).reshape(n, FOX_HEADS), ((0, 0), (0, FC_PAD - FOX_HEADS)))
    d_fc, dbias = _fox_decay_bwd(dc_q, dc_k, fc, bias, nb, f"fox_decay_bwd_{l}")
    g["fox_f_bias"] = dbias[0, :FOX_HEADS]
    pieces = [(d_a, C_QA), (d_b, C_UB), (d_qc, C_QC), (d_kc, C_KC), (d_vc, C_VC), (d_gc, C_GC), (d_fc, None)]
    g["w_in"] = _w_in_grad(ht, pieces, f"w_in_grad_{l}")
    return g, pieces, rode


def _layer_bwd_input(l, dx, pieces, saved, weights, lw, rider=None):
    (dxi, dgpre), rode = _in_proj_bwd(pieces, lw[2], lw[3], saved[0], weights["pre_norm_g"][l:l + 1], dx,
                                      f"in_proj_bwd_{l}", rider)
    return dxi, dgpre[0], rode


def kernel(x, lower_bounds, pre_norm_g, w_in, hgrn_norm_g, fox_f_bias, pool_w, pool_scale, w_out, post_norm_g, loss_target, m_lower_bounds, m_pre_norm_g, m_w_in, m_hgrn_norm_g, m_fox_f_bias, m_pool_w, m_pool_scale, m_w_out, m_post_norm_g, v_lower_bounds, v_pre_norm_g, v_w_in, v_hgrn_norm_g, v_fox_f_bias, v_pool_w, v_pool_scale, v_w_out, v_post_norm_g):
    weights = dict(lower_bounds=lower_bounds, pre_norm_g=pre_norm_g, hgrn_norm_g=hgrn_norm_g, fox_f_bias=fox_f_bias,
                   pool_w=pool_w, pool_scale=pool_scale, post_norm_g=post_norm_g)
    mom_m = dict(lower_bounds=m_lower_bounds, pre_norm_g=m_pre_norm_g, hgrn_norm_g=m_hgrn_norm_g, fox_f_bias=m_fox_f_bias,
                 pool_w=m_pool_w, pool_scale=m_pool_scale, post_norm_g=m_post_norm_g)
    mom_v = dict(lower_bounds=v_lower_bounds, pre_norm_g=v_pre_norm_g, hgrn_norm_g=v_hgrn_norm_g, fox_f_bias=v_fox_f_bias,
                 pool_w=v_pool_w, pool_scale=v_pool_scale, post_norm_g=v_post_norm_g)
    depth = w_in.shape[0]
    nb, t, d = x.shape
    n = nb * t
    core = lax.axis_index("c").astype(jnp.int32).reshape(1)
    shards = [(w_in[l].astype(BF16), w_out[l].astype(BF16)) for l in range(depth)]
    lbs = _lower_bound_table(lower_bounds, "lower_bound_table")

    (g_in,) = _gather_weights(shards[0][0])
    coming = tuple(_assemble_w_in(g_in, "assemble_w_in_0")) + (None, None)
    xl, saved, lw = x.reshape(n, d), [], []
    for l in range(depth):
        last = l + 1 == depth
        ride_h = ([shards[l][1]] if coming[4] is None else []) + ([] if last else [shards[l + 1][1]])
        xl, sv, lw_l, (rode_h, rode_c) = _layer_fwd(
            l, xl, lbs, weights, coming, nb, _gather_rider(ride_h) if ride_h else None,
            None if last else _gather_rider([shards[l + 1][0]]), loss_target.reshape(n, d) if last else None)
        saved.append(sv)
        lw.append(lw_l)
        if not last:
            coming = tuple(_assemble_w_in(rode_c[0], f"assemble_w_in_{l + 1}")) + _w_out_parts(rode_h[0])
    dx, sq = xl
    loss_here = 0.5 * jnp.sum(sq) / d

    grads, recv, pending = [None] * depth, [None] * depth, None
    for l in reversed(range(depth)):
        g, pieces, rode = _layer_bwd(l, dx, saved[l], lbs, weights, lw[l], nb, pending)
        if rode is not None:
            recv[l + 1] = rode
        blocks = (g["w_in"], g["w_out"])
        if l > 0:
            pending = _direct_exchange_rider(blocks)
            dx, g["pre_norm_g"], _ = _layer_bwd_input(l, dx, pieces, saved[l], weights, lw[l])
        else:
            other = _run_rider(_swap_rider(blocks), "grad_swap")
            summed = [_pair_add(hv, ot, core, f"grad_pair_add_{i}") for i, (hv, ot) in enumerate(zip(blocks, other))]
            dx, g["pre_norm_g"], recv[l] = _layer_bwd_input(l, dx, pieces, saved[l], weights, lw[l],
                                                            _chip_exchange_rider(summed))
        grads[l] = g
    small = {k: jnp.stack([grads[l][k] for l in range(depth)]) for k in SMALL if k != "lower_bounds"}
    small["lower_bounds"] = _lower_bound_bwd(lower_bounds, jnp.stack([grads[l]["lbs"] for l in range(depth)]),
                                             "lower_bound_bwd")
    (r_small,) = _run_rider(_gather_rider([_pack_small(small, loss_here)]), "small_grads_gather")

    res_in, _ = _sum_adamw([recv[l][0] for l in range(depth)], w_in, m_w_in, v_w_in, "adamw_w_in")
    res_out, _ = _sum_adamw([recv[l][1] for l in range(depth)], w_out, m_w_out, v_w_out, "adamw_w_out")
    res_small, _ = _sum_adamw([r_small], _pack_small(weights)[None], _pack_small(mom_m)[None], _pack_small(mom_v)[None],
                              "adamw_small")
    loss = res_small[0][0].reshape(-1)[_small_size(weights)]

    names = ("lower_bounds", "pre_norm_g", "w_in", "hgrn_norm_g", "fox_f_bias", "pool_w", "pool_scale", "w_out", "post_norm_g")
    outs = [loss, dx.reshape(nb, t, d)]
    for i in range(4):
        full = dict(_unpack_small(res_small[i][0], weights), w_in=res_in[i], w_out=res_out[i])
        outs += [full[k] for k in names]
    return tuple(outs)
```

```python
import functools

import jax
import jax.numpy as jnp
from jax import lax
from jax.experimental import pallas as pl
from jax.experimental.pallas import tpu as pltpu

F32, BF16 = jnp.float32, jnp.bfloat16
HI = lax.Precision.HIGHEST
MESH = pl.DeviceIdType.MESH
N_DEV = 8

NORM_EPS = 1e-6
MASK_VALUE = -1e30
TINY = 1e-30
CHUNK = 64
SUB = 16
HGRN_W, POOL_W, FOX_W = 256, 256, 512
HEAD = 64
FOX_HEADS = 8
POOL_WINDOWS = (2, 4, 8, 16)
POOL_HALO = 16
MAIN_W = 3584
FC_PAD = 128
C_QA, C_FA, C_IA, C_GA, C_UB, C_GB, C_QC, C_KC, C_VC, C_GC = 0, 256, 512, 768, 1024, 1280, 1536, 2048, 2560, 3072
FOX_SCALE = HEAD ** -0.5

ADAM_LR, ADAM_B1, ADAM_B2, ADAM_EPS, ADAM_WD, ADAM_STEP = 0.001, 0.9, 0.999, 1e-08, 0.01, 10

VMEM_LIMIT = 56 * 1024 * 1024


def _pc(fn, name, **kw):
    return pl.pallas_call(fn, name=name, **kw)


def _params(**kw):
    return pltpu.CompilerParams(vmem_limit_bytes=VMEM_LIMIT, **kw)


class _Rider:
    def __init__(self, inputs, out_shapes, n_sems, n_local, plan):
        self.inputs, self.out_shapes, self.n_sems, self.n_local, self.plan = list(inputs), list(out_shapes), n_sems, n_local, plan

    def start(self, ins, outs, *sems):
        sends, _, locs = self.plan(ins, outs, *sems)
        for cp in locs + sends:
            cp.start()

    def wait(self, ins, outs, *sems):
        sends, recvs, locs = self.plan(ins, outs, *sems)
        for cp in recvs:
            cp.wait_recv()
        for cp in sends:
            cp.wait_send()
        for cp in locs:
            cp.wait()

    def sem_shapes(self):
        return [pltpu.SemaphoreType.DMA((self.n_sems,)), pltpu.SemaphoreType.DMA((self.n_sems,)),
                pltpu.SemaphoreType.DMA((self.n_local,))]


def _call(body, name, args, rider=None, *, grid, in_specs, out_specs, out_shape, scratch_shapes=(), **kw):
    if rider is None:
        res = _pc(body, name, grid=grid, in_specs=in_specs, out_specs=out_specs, out_shape=out_shape,
                  scratch_shapes=list(scratch_shapes), **kw)(*args)
        return res, None
    n_in, n_out, n_scr = len(in_specs), len(out_specs), len(scratch_shapes)
    n_rin, n_rout = len(rider.inputs), len(rider.out_shapes)

    def ridden(*refs):
        ins, refs = refs[:n_in], refs[n_in:]
        rins, refs = refs[:n_rin], refs[n_rin:]
        outs, refs = refs[:n_out], refs[n_out:]
        routs, refs = refs[:n_rout], refs[n_rout:]
        scr, sems = refs[:n_scr], refs[n_scr:]
        first = functools.reduce(jnp.logical_and, [pl.program_id(a) == 0 for a in range(len(grid))])
        last = functools.reduce(jnp.logical_and, [pl.program_id(a) == g - 1 for a, g in enumerate(grid)])

        @pl.when(first)
        def _():
            rider.start(rins, routs, *sems)

        body(*ins, *outs, *scr)

        @pl.when(last)
        def _():
            rider.wait(rins, routs, *sems)

    any_spec = pl.BlockSpec(memory_space=pl.ANY)
    res = _pc(ridden, name, grid=grid, in_specs=list(in_specs) + [any_spec] * n_rin,
              out_specs=list(out_specs) + [any_spec] * n_rout, out_shape=list(out_shape) + rider.out_shapes,
              scratch_shapes=list(scratch_shapes) + rider.sem_shapes(), **kw)(*args, *rider.inputs)
    return res[:n_out], res[n_out:]


def _run_rider(rider, name):
    n_rin = len(rider.inputs)

    def body(*refs):
        ins, outs, sems = refs[:n_rin], refs[n_rin:n_rin + len(rider.out_shapes)], refs[n_rin + len(rider.out_shapes):]
        rider.start(ins, outs, *sems)
        rider.wait(ins, outs, *sems)

    any_spec = pl.BlockSpec(memory_space=pl.ANY)
    return _pc(body, name, in_specs=[any_spec] * n_rin, out_specs=[any_spec] * len(rider.out_shapes),
               out_shape=rider.out_shapes, scratch_shapes=rider.sem_shapes())(*rider.inputs)


def _dot(a, b):
    return jnp.dot(a, b, preferred_element_type=F32)


def _dot_nt(a, b):
    return lax.dot_general(a, b, (((1,), (1,)), ((), ())), preferred_element_type=F32)


def _dot_tn(a, b):
    return lax.dot_general(a, b, (((0,), (0,)), ((), ())), preferred_element_type=F32)


def _dot_hi(a, b):
    return jnp.dot(a, b, precision=HI, preferred_element_type=F32)


def _split2(x):
    hi = x.astype(BF16)
    return hi, (x - hi.astype(F32)).astype(BF16)


def _sel_dot(sel, x):
    hi, lo = _split2(x)
    sb = sel.astype(BF16)
    return _dot(sb, hi) + _dot(sb, lo)


def _dot_sel(x, sel):
    hi, lo = _split2(x)
    sb = sel.astype(BF16)
    return _dot(hi, sb) + _dot(lo, sb)


def _sigmoid(x):
    return 1.0 / (1.0 + jnp.exp(-x))


def _block_ones(n, dtype):
    r = lax.broadcasted_iota(jnp.int32, (n, n), 0) // HEAD
    c = lax.broadcasted_iota(jnp.int32, (n, n), 1) // HEAD
    return (r == c).astype(dtype)


def _sds(shape, dtype):
    return jax.ShapeDtypeStruct(shape, dtype)


def _in_proj_fwd(x, g_pre, w_main, w_fc, name):
    n, d = x.shape
    tm = min(512, n)

    def body(x_ref, g_ref, w_ref, wf_ref, proj_ref, fc_ref, ht_ref, qt_ref, kt_ref):
        xv = x_ref[...]
        r = lax.rsqrt(jnp.mean(xv * xv, axis=-1, keepdims=True) + NORM_EPS)
        hf = xv * r * g_ref[...]
        hb = hf.astype(BF16)
        ht_ref[...] = hf.T.astype(BF16)
        for j in range(0, MAIN_W, FOX_W):
            res = _dot(hb, w_ref[:, j:j + FOX_W])
            proj_ref[:, j:j + FOX_W] = res
            if j == C_QC:
                qt_ref[...] = (res * FOX_SCALE).T.astype(BF16)
            if j == C_KC:
                kt_ref[...] = res.T.astype(BF16)
        fc_ref[...] = _dot(hb, wf_ref[...])

    def cols(rows):
        return pl.BlockSpec((rows, tm), lambda i: (0, i))

    return _pc(
        body, name, grid=(n // tm,),
        in_specs=[pl.BlockSpec((tm, d), lambda i: (i, 0)), pl.BlockSpec((1, d), lambda i: (0, 0)),
                  pl.BlockSpec((d, MAIN_W), lambda i: (0, 0)), pl.BlockSpec((d, FC_PAD), lambda i: (0, 0))],
        out_specs=[pl.BlockSpec((tm, MAIN_W), lambda i: (i, 0)), pl.BlockSpec((tm, FC_PAD), lambda i: (i, 0)),
                   cols(d), cols(FOX_W), cols(FOX_W)],
        out_shape=[_sds((n, MAIN_W), F32), _sds((n, FC_PAD), F32), _sds((d, n), BF16), _sds((FOX_W, n), BF16),
                   _sds((FOX_W, n), BF16)],
        compiler_params=_params(),
    )(x, g_pre, w_main, w_fc)


def _fox_decay_fwd(fc, bias, nb, name):
    n = fc.shape[0]
    t = n // nb
    tt = min(256, t)
    nt = t // tt

    def body(fc_ref, b_ref, c_ref, carry):
        i = pl.program_id(1)

        @pl.when(i == 0)
        def _():
            carry[...] = jnp.zeros_like(carry)

        xv = fc_ref[...] + b_ref[...]
        lf = jnp.minimum(xv, 0.0) - jnp.log(1.0 + jnp.exp(-jnp.abs(xv)))
        r = lax.broadcasted_iota(jnp.int32, (tt, tt), 0)
        cc = lax.broadcasted_iota(jnp.int32, (tt, tt), 1)
        cs = _dot_hi((r >= cc).astype(F32), lf) + carry[...]
        c_ref[...] = cs
        carry[...] = cs[tt - 1:tt, :]

    return _pc(
        body, name, grid=(nb, nt),
        in_specs=[pl.BlockSpec((tt, FC_PAD), lambda b, i: (b * nt + i, 0)), pl.BlockSpec((1, FC_PAD), lambda b, i: (0, 0))],
        out_specs=pl.BlockSpec((tt, FC_PAD), lambda b, i: (b * nt + i, 0)),
        out_shape=_sds((n, FC_PAD), F32),
        scratch_shapes=[pltpu.VMEM((1, FC_PAD), F32)],
        compiler_params=_params(),
    )(fc, bias)


def _hgrn_gates(q, z, lb):
    sig = _sigmoid(z)
    sn = _sigmoid(-z)
    f = lb + (1.0 - lb) * sig
    g = jnp.log(jnp.maximum(f, TINY))
    k = (1.0 - lb) * sn
    sq = _sigmoid(q)
    return sig, sn, f, g, k, sq


def _sub_tri(n, lower):
    r = lax.broadcasted_iota(jnp.int32, (n, n), 0)
    c = lax.broadcasted_iota(jnp.int32, (n, n), 1)
    tri = (r >= c) if lower else (r <= c)
    return jnp.logical_and(r // SUB == c // SUB, tri).astype(F32)


def _live_rows(t):
    return 8 * (t // 8 + 1)


def _pad_rows(x):
    return x if x.shape[0] == SUB else jnp.concatenate([x, jnp.zeros((SUB - x.shape[0], x.shape[1]), x.dtype)], axis=0)


def _hgrn_decays(qs, k, b):
    srow = lax.broadcasted_iota(jnp.int32, (SUB, HGRN_W), 0)
    es, ws = [], []
    for t in range(SUB):
        r = _live_rows(t)
        e = jnp.where(srow[:r] <= t, jnp.exp(b[t:t + 1, :] - b[:r]), 0.0)
        es.append(e)
        ws.append(_pad_rows(e * (qs[t:t + 1, :] * k[:r])))
    return srow, es, ws


def _hgrn_state_step(st, k, v, b, bmask):
    bl = b[SUB - 1:SUB, :]
    ktil = k * jnp.exp(bl - b)
    return st * jnp.exp(bl) + _dot_tn(v.astype(BF16), ktil.astype(BF16)) * bmask


def _hgrn_tile(t):
    return min(256, t)


def _hgrn_fwd(proj, lb, ones_b, nb, name, rider=None):
    n = proj.shape[0]
    t = n // nb
    tt = _hgrn_tile(t)
    nt = t // tt
    ncs = tt // CHUNK
    w = HGRN_W

    def body(q_ref, z_ref, v_ref, lb_ref, ones_ref, o_ref, s0_ref, st_s, b_s, qs_s, k_s):
        @pl.when(pl.program_id(1) == 0)
        def _():
            st_s[...] = jnp.zeros_like(st_s)

        q = q_ref[...]
        _, _, _, g, k, sq = _hgrn_gates(q, z_ref[...], lb_ref[...])
        b_s[...] = _sel_dot(_sub_tri(tt, True), g)
        qs_s[...] = q * sq
        k_s[...] = k
        bmask = _block_ones(w, F32)
        ones_b = ones_ref[...]

        def chunk(c, carry):
            st = st_s[...]
            s0_ref[c] = st
            base = pl.multiple_of(c * CHUNK, CHUNK)
            tiles = []
            for u in range(CHUNK // SUB):
                rows = pl.ds(base + u * SUB, SUB)
                tiles.append((qs_s[rows, :], k_s[rows, :], v_ref[rows, :], b_s[rows, :]))
            aexps = []
            for qs, k, v, b in tiles:
                _, _, ws = _hgrn_decays(qs, k, b)
                aexps.append(_dot(jnp.concatenate(ws, axis=0).astype(BF16), ones_b))
            inters = []
            for qs, k, v, b in tiles:
                inters.append(_dot_nt((qs * jnp.exp(b)).astype(BF16), st.astype(BF16)))
                st = _hgrn_state_step(st, k, v, b, bmask)
            st_s[...] = st
            for u, ((qs, k, v, b), aexp, o) in enumerate(zip(tiles, aexps, inters)):
                for t in range(SUB):
                    r = _live_rows(t)
                    row = o[t:t + 1, :] + jnp.sum(aexp[t * SUB:t * SUB + r, :] * v[:r], axis=0, keepdims=True)
                    o_ref[pl.ds(base + u * SUB + t, 1), :] = row
            return carry

        lax.fori_loop(0, ncs, chunk, 0)

    def col(j):
        return pl.BlockSpec((tt, w), lambda b, i: (b * nt + i, j))

    return _call(
        body, name, (proj, proj, proj, lb, ones_b), rider, grid=(nb, nt),
        in_specs=[col(C_QA // w), col(C_FA // w), col(C_IA // w), pl.BlockSpec((1, w), lambda b, i: (0, 0)),
                  pl.BlockSpec((w, w), lambda b, i: (0, 0))],
        out_specs=[pl.BlockSpec((tt, w), lambda b, i: (b * nt + i, 0)),
                   pl.BlockSpec((ncs, w, w), lambda b, i: (b * nt + i, 0, 0))],
        out_shape=[_sds((n, w), F32), _sds((n // CHUNK, w, w), F32)],
        scratch_shapes=[pltpu.VMEM((w, w), F32)] + [pltpu.VMEM((tt, w), F32)] * 3,
        compiler_params=_params(),
    )


def _pool_lane_windows():
    lane = lax.broadcasted_iota(jnp.int32, (1, POOL_W), 1) // HEAD
    wl = jnp.zeros((1, POOL_W), F32)
    for gi, win in enumerate(POOL_WINDOWS):
        wl = jnp.where(lane == gi, float(win), wl)
    return lane, wl


def _pool_select(lane, parts):
    out = parts[-1]
    for gi in range(len(parts) - 2, -1, -1):
        out = jnp.where(lane == gi, parts[gi], out)
    return out


def _pool_mix(u, halo, t0, tt):
    lane, wl = _pool_lane_windows()
    ext = jnp.concatenate([halo, u], axis=0)
    sums, cur, shift = [], ext, 1
    for _ in POOL_WINDOWS:
        cur = cur + pltpu.roll(cur, shift, axis=0)
        sums.append(cur[POOL_HALO:, :])
        shift *= 2
    tpos = (t0 + lax.broadcasted_iota(jnp.int32, (tt, POOL_W), 0)).astype(F32)
    cnt = jnp.minimum(tpos + 1.0, wl)
    return _pool_select(lane, sums) / cnt - u, cnt


def _pool_specs(tt, nt, nhb):
    cu, cg = C_UB // POOL_W, C_GB // POOL_W
    return [pl.BlockSpec((tt, POOL_W), lambda b, i: (b * nt + i, cu)),
            pl.BlockSpec((tt, POOL_W), lambda b, i: (b * nt + i, cg)),
            pl.BlockSpec((POOL_HALO, POOL_W), lambda b, i: (jnp.maximum((b * nt + i) * nhb - 1, 0), cu))]


def _pool_fwd(proj, wbd, scale, nb, name):
    n = proj.shape[0]
    t = n // nb
    tt = min(512, t)
    nt = t // tt
    nhb = tt // POOL_HALO

    def body(u_ref, g_ref, h_ref, w_ref, s_ref, o_ref):
        i = pl.program_id(1)
        halo = jnp.where(i == 0, 0.0, h_ref[...])
        pooled, _ = _pool_mix(u_ref[...], halo, i * tt, tt)
        gv = g_ref[...]
        o_ref[...] = _dot(pooled.astype(BF16), w_ref[...]) * s_ref[...] * (gv * _sigmoid(gv))

    return _pc(
        body, name, grid=(nb, nt),
        in_specs=_pool_specs(tt, nt, nhb) + [pl.BlockSpec((POOL_W, POOL_W), lambda b, i: (0, 0)),
                                             pl.BlockSpec((1, POOL_W), lambda b, i: (0, 0))],
        out_specs=pl.BlockSpec((tt, POOL_W), lambda b, i: (b * nt + i, 0)),
        out_shape=_sds((n, POOL_W), F32),
        compiler_params=_params(),
    )(proj, proj, proj, wbd, scale)


def _rows_reduce(x, op, final):
    while x.shape[0] > 8 and x.shape[0] % 16 == 0:
        half = x.shape[0] // 2
        x = op(x[:half], x[half:])
    return final(x, axis=0, keepdims=True)


def _tri_pair(step, n):
    a = sum([(step >= r * (r + 1) // 2).astype(jnp.int32) for r in range(1, n)], jnp.int32(0))
    return a, step - a * (a + 1) // 2


def _lane_lo():
    return lax.broadcasted_iota(jnp.int32, (1, 2 * HEAD), 1) < HEAD


def _put_col(tile, hh, colv):
    lane = lax.broadcasted_iota(jnp.int32, tile.shape, 1)
    return jnp.where(lane == hh, colv, tile)


def _fox_fwd(proj, kt, c_col, c_row, nb, name, rider=None):
    n = proj.shape[0]
    t = n // nb
    tb = min(256, t)
    nq = t // tb
    pw = 2 * HEAD

    def body(q_ref, kt_ref, v_ref, cc_ref, cr_ref, o_ref, lse_ref, m_s, acc_s, cq_s):
        qi, kj = _tri_pair(pl.program_id(1), nq)

        @pl.when(kj == 0)
        def _():
            m_s[...] = jnp.full_like(m_s, -jnp.inf)
            acc_s[...] = jnp.zeros_like(acc_s)
            for hh in range(FOX_HEADS):
                cq_s[hh] = jnp.broadcast_to(cc_ref[:, hh:hh + 1], (tb, pw))

        def block(masked):
            lo = _lane_lo()
            if masked:
                causal = lax.broadcasted_iota(jnp.int32, (tb, tb), 0) >= lax.broadcasted_iota(jnp.int32, (tb, tb), 1)
            def lanes(hh):
                return lo if hh % 2 == 0 else jnp.logical_not(lo)

            def scores(hh):
                sl = slice((hh // 2) * pw, (hh // 2 + 1) * pw)
                return _dot(jnp.where(lanes(hh), q_ref[:, sl] * FOX_SCALE, 0.0).astype(BF16), kt_ref[sl, :])

            ahead = scores(0)
            for hh in range(FOX_HEADS):
                s = ahead
                if hh + 1 < FOX_HEADS:
                    ahead = scores(hh + 1)
                s = s + (jnp.tile(cq_s[hh], (1, tb // pw)) - cr_ref[hh:hh + 1, :])
                if masked:
                    s = jnp.where(causal, s, MASK_VALUE)
                m_prev = m_s[hh]
                m_new = jnp.maximum(m_prev, jnp.max(s, axis=1, keepdims=True))
                alpha = jnp.exp(m_prev - m_new)
                pe = jnp.exp(s - jnp.tile(m_new, (1, tb // pw)))
                m_s[hh] = m_new
                vf = v_ref[:, (hh // 2) * pw:(hh // 2 + 1) * pw]
                acc_s[hh] = alpha * acc_s[hh] + _dot(pe.astype(BF16), jnp.where(lanes(hh), vf, 1.0).astype(BF16))

        @pl.when(kj < qi)
        def _():
            block(False)

        @pl.when(kj == qi)
        def _():
            block(True)
            lo = _lane_lo()
            m_all, l_all = jnp.zeros((tb, FC_PAD), F32), jnp.ones((tb, FC_PAD), F32)
            for p in range(FOX_HEADS // 2):
                a0, a1 = acc_s[2 * p], acc_s[2 * p + 1]
                both = pltpu.roll(jnp.where(lo, a1, a0), HEAD, axis=1)
                o_ref[:, p * pw:(p + 1) * pw] = jnp.where(lo, a0, a1) / both
                m_all = _put_col(_put_col(m_all, 2 * p, m_s[2 * p]), 2 * p + 1, m_s[2 * p + 1])
                l_all = _put_col(_put_col(l_all, 2 * p, both), 2 * p + 1, a1)
            lse_ref[...] = m_all + jnp.log(l_all)

    def qspec(wd, j):
        return pl.BlockSpec((tb, wd), lambda b, st: (b * nq + _tri_pair(st, nq)[0], j))

    def kspec(j):
        return pl.BlockSpec((tb, FOX_W), lambda b, st: (b * nq + _tri_pair(st, nq)[1], j))

    return _call(
        body, name, (proj, kt, proj, c_col, c_row), rider, grid=(nb, nq * (nq + 1) // 2),
        in_specs=[qspec(FOX_W, C_QC // FOX_W), pl.BlockSpec((FOX_W, tb), lambda b, st: (0, b * nq + _tri_pair(st, nq)[1])),
                  kspec(C_VC // FOX_W), qspec(FC_PAD, 0),
                  pl.BlockSpec((None, FOX_HEADS, tb), lambda b, st: (b, 0, _tri_pair(st, nq)[1]))],
        out_specs=[qspec(FOX_W, 0), qspec(FC_PAD, 0)],
        out_shape=[_sds((n, FOX_W), F32), _sds((n, FC_PAD), F32)],
        scratch_shapes=[pltpu.VMEM((FOX_HEADS, tb, pw), F32), pltpu.VMEM((FOX_HEADS, tb, pw), F32),
                        pltpu.VMEM((FOX_HEADS, tb, pw), F32)],
        compiler_params=_params(),
    )


def _head_mean(x, ones_f):
    return _dot_sel(x, ones_f) * (1.0 / HEAD)


def _merge_fwd(x, proj, o_h, o_b, o_c, gh, w_out, g_post, name, target=None):
    n, d = x.shape
    tm = min(512, n)

    def body(*refs):
        x_ref, ga_ref, gc_ref, oh_ref, ob_ref, oc_ref, gh_ref, w_ref, gp_ref = refs[:9]
        if target is None:
            xo_ref, mixt_ref, y_ref = refs[9:]
        else:
            t_ref, dx_ref, sq_ref, mixt_ref, y_ref = refs[9:]
        oh = oh_ref[...]
        ones_f = _block_ones(HGRN_W, F32)
        na = oh * lax.rsqrt(_head_mean(oh * oh, ones_f) + NORM_EPS) * gh_ref[...]
        ga, gc = ga_ref[...], gc_ref[...]
        mixed = jnp.concatenate([na * (ga * _sigmoid(ga)), ob_ref[...], oc_ref[...] * (gc * _sigmoid(gc))], axis=1)
        mixt_ref[...] = mixed.T.astype(BF16)
        y = _dot(mixed.astype(BF16), w_ref[...])
        y_ref[...] = y
        xn = x_ref[...] + y * lax.rsqrt(jnp.mean(y * y, axis=-1, keepdims=True) + NORM_EPS) * gp_ref[...]
        if target is None:
            xo_ref[...] = xn
        else:
            @pl.when(pl.program_id(0) == 0)
            def _():
                sq_ref[...] = jnp.zeros_like(sq_ref)

            e = xn - t_ref[...]
            dx_ref[...] = e * (1.0 / d)
            sq_ref[...] += jnp.sum(e * e, axis=0, keepdims=True)

    def row(wd, j=0):
        return pl.BlockSpec((tm, wd), lambda i: (i, j))

    def full(a, b):
        return pl.BlockSpec((a, b), lambda i: (0, 0))

    head = [] if target is None else [target]
    res = _pc(
        body, name, grid=(n // tm,),
        in_specs=[row(d), row(HGRN_W, C_GA // HGRN_W), row(FOX_W, C_GC // FOX_W), row(HGRN_W), row(POOL_W), row(FOX_W),
                  full(1, HGRN_W), full(d, d), full(1, d)] + [row(d)] * len(head),
        out_specs=[row(d)] + [full(1, d)] * len(head) + [pl.BlockSpec((d, tm), lambda i: (0, i)), row(d)],
        out_shape=[_sds((n, d), F32)] + [_sds((1, d), F32)] * len(head) + [_sds((d, n), BF16), _sds((n, d), F32)],
        compiler_params=_params(),
    )(x, proj, proj, o_h, o_b, o_c, gh, w_out, g_post, *head)
    return (res[0], res[1], res[2]) if target is None else ((res[0], res[1]), res[2], res[3])


def _rms_bwd(dy_scaled, xhat, r):
    return r * (dy_scaled - xhat * jnp.mean(dy_scaled * xhat, axis=-1, keepdims=True))


def _merge_bwd(dxo, y, g_post, w_out_t, proj, o_c, name):
    n, d = y.shape
    tm = min(512, n)
    wab = HGRN_W + POOL_W

    def body(dx_ref, y_ref, gp_ref, wt_ref, gc_ref, oc_ref, dy_ref, dm_ref, dgp_ref, da_ref, dat_ref, dg_ref, dl_ref):
        @pl.when(pl.program_id(0) == 0)
        def _():
            dgp_ref[...] = jnp.zeros_like(dgp_ref)

        yv, dxv = y_ref[...], dx_ref[...]
        r = lax.rsqrt(jnp.mean(yv * yv, axis=-1, keepdims=True) + NORM_EPS)
        yh = yv * r
        dgp_ref[...] += jnp.sum(dxv * yh, axis=0, keepdims=True)
        dyb = _rms_bwd(dxv * gp_ref[...], yh, r).astype(BF16)
        dy_ref[...] = dyb
        dm_ref[...] = _dot(dyb, wt_ref[:, :wab])
        dmc = _dot(dyb, wt_ref[:, wab:])
        gc, oc = gc_ref[...], oc_ref[...]
        sg = _sigmoid(gc)
        da = dmc * (gc * sg)
        da_ref[...] = da.astype(BF16)
        dat_ref[...] = da.T.astype(BF16)
        dg_ref[...] = (dmc * oc * (sg * (1.0 + gc * (1.0 - sg)))).astype(BF16)
        rr = lax.broadcasted_iota(jnp.int32, (FOX_W, FC_PAD), 0) // HEAD
        cc = lax.broadcasted_iota(jnp.int32, (FOX_W, FC_PAD), 1)
        dl_ref[...] = _dot_sel(da * oc, (rr == cc).astype(F32))

    def row(wd, j=0):
        return pl.BlockSpec((tm, wd), lambda i: (i, j))

    def full(a, b):
        return pl.BlockSpec((a, b), lambda i: (0, 0))

    return _pc(
        body, name, grid=(n // tm,),
        in_specs=[row(d), row(d), full(1, d), full(d, d), row(FOX_W, C_GC // FOX_W), row(FOX_W)],
        out_specs=[row(d), row(wab), full(1, d), row(FOX_W), pl.BlockSpec((FOX_W, tm), lambda i: (0, i)), row(FOX_W),
                   row(FC_PAD)],
        out_shape=[_sds((n, d), BF16), _sds((n, wab), F32), _sds((1, d), F32), _sds((n, FOX_W), BF16),
                   _sds((FOX_W, n), BF16), _sds((n, FOX_W), BF16), _sds((n, FC_PAD), F32)],
        compiler_params=_params(),
    )(dxo, y, g_post, w_out_t, proj, o_c)


def _w_out_grad(mixt, dy, name):
    d, n = mixt.shape
    rows = d // N_DEV

    def body(a_ref, b_ref, o_ref):
        o_ref[...] = _dot(a_ref[...], b_ref[...]).astype(BF16)

    return _pc(
        body, name, grid=(N_DEV,),
        in_specs=[pl.BlockSpec((rows, n), lambda j: (j, 0)), pl.BlockSpec((n, d), lambda j: (0, 0))],
        out_specs=pl.BlockSpec((None, None, rows, d), lambda j: (j % 2, j // 2, 0, 0)),
        out_shape=_sds((2, N_DEV // 2, rows, d), BF16),
        compiler_params=_params(),
    )(mixt, dy)


def _w_in_grad(ht, pieces, name):
    d, n = ht.shape
    ta, tk = min(512, d), min(512, n)
    nk = n // tk
    arrays = [p for p, _ in pieces]
    widths = [p.shape[1] for p in arrays]
    offs = [sum(widths[:i]) for i in range(len(widths))]
    in_w = MAIN_W + FOX_HEADS
    shard = in_w // N_DEV

    def body(*refs):
        a_ref, p_refs = refs[0], refs[1:1 + len(arrays)]
        o_ref, acc = refs[1 + len(arrays):]
        k = pl.program_id(1)

        @pl.when(k == 0)
        def _():
            acc[...] = jnp.zeros_like(acc)

        a = a_ref[...]
        for pr, off, wd in zip(p_refs, offs, widths):
            for j in range(0, wd, 512):
                jw = min(512, wd - j)
                acc[:, off + j:off + j + jw] += _dot(a, pr[:, j:j + jw])

        @pl.when(k == nk - 1)
        def _():
            for j in range(N_DEV):
                o_ref[j % 2, j // 2] = acc[:, j * shard:(j + 1) * shard].astype(BF16)

    return _pc(
        body, name, grid=(d // ta, nk),
        in_specs=[pl.BlockSpec((ta, tk), lambda i, k: (i, k))] + [pl.BlockSpec((tk, wd), lambda i, k: (k, 0)) for wd in widths],
        out_specs=pl.BlockSpec((2, N_DEV // 2, ta, shard), lambda i, k: (0, 0, i, 0)),
        out_shape=_sds((2, N_DEV // 2, d, shard), BF16),
        scratch_shapes=[pltpu.VMEM((ta, sum(widths)), F32)],
        compiler_params=_params(),
    )(ht, *arrays)


def _hgrn_state_bwd(qs, k, v, b, do, s0, ds1, bmask):
    bl = b[SUB - 1:SUB, :]
    eb, ebl, ekt = jnp.exp(b), jnp.exp(bl), jnp.exp(bl - b)
    qe, ktil = qs * eb, k * ekt
    ds1b, dob = ds1.astype(BF16), do.astype(BF16)
    dv = _dot_nt(ktil.astype(BF16), ds1b)
    dqe = _dot(dob, s0.astype(BF16))
    dktil = _dot(v.astype(BF16), ds1b)
    dbl = jnp.sum(dktil * ktil, axis=0, keepdims=True) + ebl * jnp.sum(s0 * ds1, axis=0, keepdims=True)
    ds0 = ds1 * ebl + _dot_tn(dob, qe.astype(BF16)) * bmask
    return dqe * eb, dktil * ekt, dv, dbl, ds0


def _hgrn_intra_bwd(qs, k, v, do, es, aexp, gexp, dq, dk, dv, put_dq_row):
    dks = [dk[j:j + 8] for j in range(0, SUB, 8)]
    dvs = [dv[j:j + 8] for j in range(0, SUB, 8)]
    for t in range(SUB):
        r = _live_rows(t)
        ge = gexp[t * SUB:t * SUB + r, :] * es[t]
        put_dq_row(t, dq[t:t + 1, :] + jnp.sum(ge * k[:r], axis=0, keepdims=True))
        for j in range(r // 8):
            dks[j] = dks[j] + ge[8 * j:8 * j + 8] * qs[t:t + 1, :]
            dvs[j] = dvs[j] + aexp[t * SUB + 8 * j:t * SUB + 8 * j + 8, :] * do[t:t + 1, :]
    return jnp.concatenate(dks, axis=0), jnp.concatenate(dvs, axis=0)


def _hgrn_bwd(dmix, proj, o_h, s0, gh, lb, ones_b, nb, name):
    n = proj.shape[0]
    t = n // nb
    tt = _hgrn_tile(t)
    nt = t // tt
    ncs = tt // CHUNK
    nsub = CHUNK // SUB
    w = HGRN_W

    def body(dm_ref, q_ref, z_ref, v_ref, ga_ref, oh_ref, s0_ref, gh_ref, lb_ref, ones_ref,
             dp_ref, dgh_ref, dlb_ref, ds_s, ss_s, b_s, qs_s, k_s, do_s, dq_s, dk_s, dv_s, dbl_s):
        first = jnp.logical_and(pl.program_id(0) == 0, pl.program_id(1) == 0)

        @pl.when(first)
        def _():
            dgh_ref[...] = jnp.zeros_like(dgh_ref)
            dlb_ref[...] = jnp.zeros_like(dlb_ref)

        @pl.when(pl.program_id(1) == 0)
        def _():
            ds_s[...] = jnp.zeros_like(ds_s)

        ones_b = ones_ref[...]
        ones_f = ones_b.astype(F32)
        bmask = _block_ones(w, F32)
        lbv, ghv = lb_ref[...], gh_ref[...]
        oh, ga, dm = oh_ref[...], ga_ref[...], dm_ref[...]
        rn = lax.rsqrt(_head_mean(oh * oh, ones_f) + NORM_EPS)
        nh = oh * rn
        sga = _sigmoid(ga)
        dp_ref[:, 3 * w:4 * w] = (dm * nh * ghv * (sga * (1.0 + ga * (1.0 - sga)))).astype(BF16)
        dn = dm * (ga * sga)
        dgh_ref[...] += jnp.sum(dn * nh, axis=0, keepdims=True)
        dn = dn * ghv
        do_s[...] = rn * (dn - nh * _head_mean(dn * nh, ones_f))
        q = q_ref[...]
        sig, sn, f, g, k, sq = _hgrn_gates(q, z_ref[...], lbv)
        qs = q * sq
        b_s[...] = _sel_dot(_sub_tri(tt, True), g)
        qs_s[...] = qs
        k_s[...] = k

        def chunk(cc, carry):
            c = ncs - 1 - cc
            base = pl.multiple_of(c * CHUNK, CHUNK)
            tiles = []
            for u in range(nsub):
                rows = pl.ds(base + u * SUB, SUB)
                tiles.append((qs_s[rows, :], k_s[rows, :], v_ref[rows, :], b_s[rows, :], do_s[rows, :]))
            st = s0_ref[c]
            for u, (qs, k, v, b, do) in enumerate(tiles):
                ss_s[u] = st
                if u < nsub - 1:
                    st = _hgrn_state_step(st, k, v, b, bmask)
            ds = ds_s[...]
            for u in reversed(range(nsub)):
                qs, k, v, b, do = tiles[u]
                _, es, ws = _hgrn_decays(qs, k, b)
                gs = [_pad_rows(do[t:t + 1, :] * v[:_live_rows(t)]) for t in range(SUB)]
                aexp = _dot(jnp.concatenate(ws, axis=0).astype(BF16), ones_b)
                gexp = _dot(jnp.concatenate(gs, axis=0).astype(BF16), ones_b)
                dq, dk, dv, dbl, ds = _hgrn_state_bwd(qs, k, v, b, do, ss_s[u], ds, bmask)

                def put_dq_row(i, row, r0=base + u * SUB):
                    dq_s[pl.ds(r0 + i, 1), :] = row

                dk, dv = _hgrn_intra_bwd(qs, k, v, do, es, aexp, gexp, dq, dk, dv, put_dq_row)
                dk_s[pl.ds(base + u * SUB, SUB), :] = dk
                dv_s[pl.ds(base + u * SUB, SUB), :] = dv
                dbl_s[pl.ds(base + u * SUB, SUB), :] = jnp.broadcast_to(dbl, (SUB, w))
            ds_s[...] = ds
            return carry

        lax.fori_loop(0, ncs, chunk, 0)
        dqs, dk = dq_s[...], dk_s[...]
        dg = _sel_dot(_sub_tri(tt, False), qs * dqs - k * dk) + dbl_s[...]
        dfz = jnp.where(f > TINY, dg / jnp.maximum(f, TINY), 0.0)
        dlb_ref[...] += jnp.sum(dfz * (1.0 - sig) - dk * sn, axis=0, keepdims=True)
        dp_ref[:, 0:w] = (dqs * (sq * (1.0 + q * (1.0 - sq)))).astype(BF16)
        dp_ref[:, w:2 * w] = ((dfz - dk) * (1.0 - lbv) * sig * sn).astype(BF16)
        dp_ref[:, 2 * w:3 * w] = dv_s[...].astype(BF16)

    def rv(b, i):
        return b * nt + (nt - 1 - i)

    def col(j):
        return pl.BlockSpec((tt, w), lambda b, i: (rv(b, i), j))

    def full(a, bb):
        return pl.BlockSpec((a, bb), lambda b, i: (0, 0))

    return _pc(
        body, name, grid=(nb, nt),
        in_specs=[col(0), col(C_QA // w), col(C_FA // w), col(C_IA // w), col(C_GA // w), col(0),
                  pl.BlockSpec((ncs, w, w), lambda b, i: (rv(b, i), 0, 0)), full(1, w), full(1, w), full(w, w)],
        out_specs=[pl.BlockSpec((tt, 4 * w), lambda b, i: (rv(b, i), 0)), full(1, w), full(1, w)],
        out_shape=[_sds((n, 4 * w), BF16), _sds((1, w), F32), _sds((1, w), F32)],
        scratch_shapes=[pltpu.VMEM((w, w), F32), pltpu.VMEM((nsub, w, w), F32)] + [pltpu.VMEM((tt, w), F32)] * 8,
        compiler_params=_params(),
    )(dmix, proj, proj, proj, proj, o_h, s0, gh, lb, ones_b)


def _pool_bwd(dmix, proj, wbd, wbd_t, scale, nb, name):
    n = proj.shape[0]
    t = n // nb
    tt = min(512, t)
    nt = t // tt
    nhb = tt // POOL_HALO
    cu, cg, cm = C_UB // POOL_W, C_GB // POOL_W, HGRN_W // POOL_W

    def body(u_ref, g_ref, h_ref, dm_ref, gn_ref, dmn_ref, w_ref, wt_ref, s_ref, dp_ref, dw_ref, ds_ref):
        i = pl.program_id(1)
        first = jnp.logical_and(pl.program_id(0) == 0, i == 0)

        @pl.when(first)
        def _():
            dw_ref[...] = jnp.zeros_like(dw_ref)
            ds_ref[...] = jnp.zeros_like(ds_ref)

        sc = s_ref[...]
        halo = jnp.where(i == 0, 0.0, h_ref[...])
        pooled, cnt = _pool_mix(u_ref[...], halo, i * tt, tt)
        pb = pooled.astype(BF16)
        pre = _dot(pb, w_ref[...])
        gv, dm = g_ref[...], dm_ref[...]
        sg = _sigmoid(gv)
        silu = gv * sg
        dgb = dm * pre * sc * (sg * (1.0 + gv * (1.0 - sg)))
        ds_ref[...] += jnp.sum(dm * pre * silu, axis=0, keepdims=True)
        dpre = (dm * sc * silu).astype(BF16)
        dw_ref[...] += _dot_tn(pb, dpre)
        dpool = _dot(dpre, wt_ref[...])
        gn = gn_ref[...]
        dpre_n = (dmn_ref[...] * sc * (gn * _sigmoid(gn))).astype(BF16)
        dpool_n = jnp.where(i == nt - 1, 0.0, _dot(dpre_n, wt_ref[...]))
        lane, wl = _pool_lane_windows()
        tpos_n = ((i + 1) * tt + lax.broadcasted_iota(jnp.int32, (POOL_HALO, POOL_W), 0)).astype(F32)
        ext = jnp.concatenate([dpool / cnt, dpool_n / jnp.minimum(tpos_n + 1.0, wl)], axis=0)
        rows = tt + POOL_HALO
        sums, cur, shift = [], ext, 1
        for _ in POOL_WINDOWS:
            cur = cur + pltpu.roll(cur, rows - shift, axis=0)
            sums.append(cur[:tt, :])
            shift *= 2
        du = _pool_select(lane, sums) - dpool
        dp_ref[...] = jnp.concatenate([du, dgb], axis=1).astype(BF16)

    def nxt(b, i):
        return jnp.minimum((b * nt + i + 1) * nhb, n // POOL_HALO - 1)

    return _pc(
        body, name, grid=(nb, nt),
        in_specs=_pool_specs(tt, nt, nhb) + [
            pl.BlockSpec((tt, POOL_W), lambda b, i: (b * nt + i, cm)),
            pl.BlockSpec((POOL_HALO, POOL_W), lambda b, i: (nxt(b, i), cg)),
            pl.BlockSpec((POOL_HALO, POOL_W), lambda b, i: (nxt(b, i), cm)),
            pl.BlockSpec((POOL_W, POOL_W), lambda b, i: (0, 0)), pl.BlockSpec((POOL_W, POOL_W), lambda b, i: (0, 0)),
            pl.BlockSpec((1, POOL_W), lambda b, i: (0, 0))],
        out_specs=[pl.BlockSpec((tt, 2 * POOL_W), lambda b, i: (b * nt + i, 0)),
                   pl.BlockSpec((POOL_W, POOL_W), lambda b, i: (0, 0)), pl.BlockSpec((1, POOL_W), lambda b, i: (0, 0))],
        out_shape=[_sds((n, 2 * POOL_W), BF16), _sds((POOL_W, POOL_W), F32), _sds((1, POOL_W), F32)],
        compiler_params=_params(),
    )(proj, proj, proj, dmix, proj, dmix, wbd, wbd_t, scale)


def _fox_bwd(proj, qt, kt, da, dat, c_col, c_row, lse_row, delta_row, nb, name, rider=None):
    n = proj.shape[0]
    t = n // nb
    tb = min(256, t)
    nq = t // tb
    pw = 2 * HEAD

    def body(q_ref, k_ref, v_ref, da_ref, qt_ref, kt_ref, dat_ref, cc_ref, cr_ref, lse_ref, dl_ref,
             dq_ref, dk_ref, dv_ref, dck_ref, dcq_ref, dq_s, dk_s, dv_s, dck_s, dcq_s):
        step = pl.program_id(1)
        kj, qi = pairs(step)

        @pl.when(step == 0)
        def _():
            dq_s[...] = jnp.zeros_like(dq_s)
            dcq_s[...] = jnp.zeros_like(dcq_s)

        @pl.when(qi == nq - 1)
        def _():
            dk_s[...] = jnp.zeros_like(dk_s)
            dv_s[...] = jnp.zeros_like(dv_s)
            dck_s[...] = jnp.zeros_like(dck_s)

        def block(masked):
            lo = _lane_lo()
            if masked:
                causal = lax.broadcasted_iota(jnp.int32, (tb, tb), 1) >= lax.broadcasted_iota(jnp.int32, (tb, tb), 0)
            dck = dck_s[...]
            for p in range(FOX_HEADS // 2):
                sl = slice(p * pw, (p + 1) * pw)
                qp = q_ref[:, sl] * FOX_SCALE
                kp = k_ref[:, sl].astype(BF16)
                vp = v_ref[:, sl].astype(BF16)
                dap = da_ref[:, sl]
                dk, dv = dk_s[:, sl], dv_s[:, sl]
                for h in range(2):
                    hh = 2 * p + h
                    lm = lo if h == 0 else jnp.logical_not(lo)
                    rows = slice(hh * HEAD, (hh + 1) * HEAD)
                    none = jnp.zeros((HEAD, tb), BF16)
                    qm = jnp.where(lm, qp, 0.0).astype(BF16)
                    dam = jnp.where(lm, dap, jnp.zeros_like(dap))
                    qtm = jnp.concatenate([qt_ref[rows, :], none] if h == 0 else [none, qt_ref[rows, :]], axis=0)
                    datm = jnp.concatenate([dat_ref[rows, :], none] if h == 0 else [none, dat_ref[rows, :]], axis=0)
                    s = _dot(kp, qtm) + (cr_ref[hh:hh + 1, :] - cc_ref[:, hh:hh + 1])
                    pe = jnp.exp(s - lse_ref[hh:hh + 1, :])
                    if masked:
                        pe = jnp.where(causal, pe, 0.0)
                    dp = _dot(vp, datm)
                    ds = pe * (dp - dl_ref[hh:hh + 1, :])
                    dsb = ds.astype(BF16)
                    dv = dv + _dot(pe.astype(BF16), dam)
                    dk = dk + _dot(dsb, qm)
                    dq_s[qi, rows, :] += _dot(kt_ref[rows, :], dsb)
                    dck = dck - _put_col(jnp.zeros_like(dck), hh, jnp.sum(ds, axis=1, keepdims=True))
                    dcq_s[qi, hh:hh + 1, :] += _rows_reduce(ds, jnp.add, jnp.sum)
                dk_s[:, sl] = dk
                dv_s[:, sl] = dv
            dck_s[...] = dck

        @pl.when(qi > kj)
        def _():
            block(False)

        @pl.when(qi == kj)
        def _():
            block(True)

        @pl.when(qi == kj)
        def _():
            dk_ref[...] = dk_s[...].astype(BF16)
            dv_ref[...] = dv_s[...].astype(BF16)
            dck_ref[...] = dck_s[...]

        @pl.when(step == nq * (nq + 1) // 2 - 1)
        def _():
            for j in range(nq):
                dq_ref[j * tb:(j + 1) * tb, :] = (dq_s[j].T * FOX_SCALE).astype(BF16)
                dcq_ref[:, j * tb:(j + 1) * tb] = dcq_s[j]

    def pairs(step):
        a, b = _tri_pair(step, nq)
        return nq - 1 - a, nq - 1 - b

    def kspec(wd, j=0):
        return pl.BlockSpec((tb, wd), lambda b, st: (b * nq + pairs(st)[0], j))

    def qspec(wd, j=0):
        return pl.BlockSpec((tb, wd), lambda b, st: (b * nq + pairs(st)[1], j))

    def qrow():
        return pl.BlockSpec((None, FOX_HEADS, tb), lambda b, st: (b, 0, pairs(st)[1]))

    def tspec(which):
        return pl.BlockSpec((FOX_W, tb), lambda b, st: (0, b * nq + pairs(st)[which]))

    return _call(
        body, name, (proj, proj, proj, da, qt, kt, dat, c_col, c_row, lse_row, delta_row), rider,
        grid=(nb, nq * (nq + 1) // 2),
        in_specs=[qspec(FOX_W, C_QC // FOX_W), kspec(FOX_W, C_KC // FOX_W), kspec(FOX_W, C_VC // FOX_W), qspec(FOX_W),
                  tspec(1), tspec(0), tspec(1), kspec(FC_PAD), qrow(), qrow(), qrow()],
        out_specs=[pl.BlockSpec((t, FOX_W), lambda b, st: (b, 0)), kspec(FOX_W), kspec(FOX_W), kspec(FC_PAD),
                   pl.BlockSpec((None, FOX_HEADS, t), lambda b, st: (b, 0, 0))],
        out_shape=[_sds((n, FOX_W), BF16), _sds((n, FOX_W), BF16), _sds((n, FOX_W), BF16), _sds((n, FC_PAD), F32),
                   _sds((nb, FOX_HEADS, t), F32)],
        scratch_shapes=[pltpu.VMEM((nq, FOX_W, tb), F32), pltpu.VMEM((tb, FOX_W), F32), pltpu.VMEM((tb, FOX_W), F32),
                        pltpu.VMEM((tb, FC_PAD), F32), pltpu.VMEM((nq, FOX_HEADS, tb), F32)],
        compiler_params=_params(),
    )


def _fox_decay_bwd(dc_q, dc_k, fc, bias, nb, name):
    n = fc.shape[0]
    t = n // nb
    tt = min(256, t)
    nt = t // tt

    def body(dcq_ref, dck_ref, fc_ref, b_ref, dfc_ref, db_ref, carry):
        i = pl.program_id(1)
        first = jnp.logical_and(pl.program_id(0) == 0, i == 0)

        @pl.when(first)
        def _():
            db_ref[...] = jnp.zeros_like(db_ref)

        @pl.when(i == 0)
        def _():
            carry[...] = jnp.zeros_like(carry)

        r = lax.broadcasted_iota(jnp.int32, (tt, tt), 0)
        cc = lax.broadcasted_iota(jnp.int32, (tt, tt), 1)
        dlf = _dot_hi((r <= cc).astype(F32), dcq_ref[...] + dck_ref[...]) + carry[...]
        carry[...] = dlf[0:1, :]
        dfc = dlf * _sigmoid(-(fc_ref[...] + b_ref[...]))
        dfc_ref[...] = dfc.astype(BF16)
        db_ref[...] += jnp.sum(dfc, axis=0, keepdims=True)

    def row():
        return pl.BlockSpec((tt, FC_PAD), lambda b, i: (b * nt + (nt - 1 - i), 0))

    return _pc(
        body, name, grid=(nb, nt),
        in_specs=[row(), row(), row(), pl.BlockSpec((1, FC_PAD), lambda b, i: (0, 0))],
        out_specs=[row(), pl.BlockSpec((1, FC_PAD), lambda b, i: (0, 0))],
        out_shape=[_sds((n, FC_PAD), BF16), _sds((1, FC_PAD), F32)],
        scratch_shapes=[pltpu.VMEM((1, FC_PAD), F32)],
        compiler_params=_params(),
    )(dc_q, dc_k, fc, bias)


def _in_proj_bwd(pieces, w_main_t, w_fc_t, x, g_pre, dxo, name, rider=None):
    n, d = x.shape
    tm = min(512, n)
    widths = [p.shape[1] for p, _ in pieces]
    offs = [o for _, o in pieces]
    np_ = len(pieces)

    def body(*refs):
        p_refs = refs[:np_]
        wt_ref, wf_ref, x_ref, g_ref, dxo_ref, dx_ref, dg_ref = refs[np_:]

        @pl.when(pl.program_id(0) == 0)
        def _():
            dg_ref[...] = jnp.zeros_like(dg_ref)

        dh = _dot(p_refs[-1][...], wf_ref[...])
        for pr, wd, off in zip(p_refs[:-1], widths[:-1], offs[:-1]):
            for j in range(0, wd, 512):
                jw = min(512, wd - j)
                dh = dh + _dot(pr[:, j:j + jw], wt_ref[off + j:off + j + jw, :])
        xv = x_ref[...]
        r = lax.rsqrt(jnp.mean(xv * xv, axis=-1, keepdims=True) + NORM_EPS)
        xh = xv * r
        dg_ref[...] += jnp.sum(dh * xh, axis=0, keepdims=True)
        dx_ref[...] = dxo_ref[...] + _rms_bwd(dh * g_ref[...], xh, r)

    row = pl.BlockSpec((tm, d), lambda i: (i, 0))
    return _call(
        body, name, (*[p for p, _ in pieces], w_main_t, w_fc_t, x, g_pre, dxo), rider, grid=(n // tm,),
        in_specs=[pl.BlockSpec((tm, wd), lambda i: (i, 0)) for wd in widths] + [
            pl.BlockSpec((MAIN_W, d), lambda i: (0, 0)), pl.BlockSpec((FC_PAD, d), lambda i: (0, 0)),
            row, pl.BlockSpec((1, d), lambda i: (0, 0)), row],
        out_specs=[row, pl.BlockSpec((1, d), lambda i: (0, 0))],
        out_shape=[_sds((n, d), F32), _sds((1, d), F32)],
        compiler_params=_params(),
    )


def _lower_bound_table(lower_bounds, name):
    depth, w = lower_bounds.shape

    def body(lb_ref, o_ref):
        v = lb_ref[...]
        e = jnp.exp(v - jnp.max(v, axis=0, keepdims=True))
        p = e / jnp.sum(e, axis=0, keepdims=True)
        acc = jnp.zeros((1, w), F32)
        for l in range(depth):
            acc = acc + p[l:l + 1, :]
            o_ref[l:l + 1, :] = acc - p[0:1, :]

    return _pc(body, name, out_shape=_sds((depth, w), F32))(lower_bounds)


def _lower_bound_bwd(lower_bounds, dlbs, name):
    depth, w = lower_bounds.shape

    def body(lb_ref, d_ref, o_ref):
        v, dl = lb_ref[...], d_ref[...]
        e = jnp.exp(v - jnp.max(v, axis=0, keepdims=True))
        p = e / jnp.sum(e, axis=0, keepdims=True)
        tot = jnp.sum(dl, axis=0, keepdims=True)
        rows, tail = [], tot
        for l in range(depth):
            rows.append(tail - tot if l == 0 else tail)
            tail = tail - dl[l:l + 1, :]
        dp = jnp.concatenate(rows, axis=0)
        o_ref[...] = p * (dp - jnp.sum(p * dp, axis=0, keepdims=True))

    return _pc(body, name, out_shape=_sds((depth, w), F32))(lower_bounds, dlbs)


def _place():
    x, y, c = lax.axis_index("x"), lax.axis_index("y"), lax.axis_index("c")
    return x, y, c


def _gather_weights(*arrays):
    na = len(arrays)

    def body(*refs):
        ins, outs = refs[:na], refs[na:2 * na]
        send_sems, recv_sems, local_sems = refs[2 * na:]
        x, y, c = _place()
        me, sibling = (x, y, c), (x, y, 1 - c)
        chips = [(1 - x, y), (x, 1 - y), (1 - x, 1 - y)]

        def slot(a, px, py, pc):
            return outs[a].at[4 * px + 2 * py + pc]

        def copy(a, k, block, to, own=False):
            return pltpu.make_async_remote_copy(
                src_ref=ins[a] if own else slot(a, *block), dst_ref=slot(a, *block),
                send_sem=send_sems.at[a * 7 + k], recv_sem=recv_sems.at[a * 7 + k],
                device_id=to, device_id_type=MESH)

        mine = [pltpu.make_async_copy(ins[a], slot(a, *me), local_sems.at[a]) for a in range(na)]
        for cp in mine:
            cp.start()
        first = []
        for a in range(na):
            first.append(copy(a, 0, me, sibling, own=True))
            first += [copy(a, 1 + j, me, (*chip, c), own=True) for j, chip in enumerate(chips)]
        for cp in first:
            cp.start()
        passed = []
        for j, chip in enumerate(chips):
            for a in range(na):
                copy(a, 1 + j, (*chip, c), me).wait_recv()
                fw = copy(a, 4 + j, (*chip, c), sibling)
                fw.start()
                passed.append(fw)
        for a in range(na):
            copy(a, 0, sibling, me).wait_recv()
            for j, chip in enumerate(chips):
                copy(a, 4 + j, (*chip, 1 - c), me).wait_recv()
        for cp in first + passed:
            cp.wait_send()
        for cp in mine:
            cp.wait()

    any_spec = pl.BlockSpec(memory_space=pl.ANY)
    return _pc(
        body, "gather_weights",
        in_specs=[any_spec] * na, out_specs=[any_spec] * na,
        out_shape=[_sds((N_DEV,) + a.shape, a.dtype) for a in arrays],
        scratch_shapes=[pltpu.SemaphoreType.DMA((7 * na,)), pltpu.SemaphoreType.DMA((7 * na,)),
                        pltpu.SemaphoreType.DMA((na,))],
    )(*arrays)


def _peer(k):
    x, y, c = _place()
    return (1 - x if k & 4 else x, 1 - y if k & 2 else y, 1 - c if k & 1 else c)


def _remote(src, dst, sems, s, to):
    return pltpu.make_async_remote_copy(src_ref=src, dst_ref=dst, send_sem=sems[0].at[s], recv_sem=sems[1].at[s],
                                        device_id=to, device_id_type=MESH)


def _gather_rider(shards):
    na = len(shards)

    def plan(ins, outs, *sems):
        x, y, c = _place()
        me = 4 * x + 2 * y + c
        locs = [pltpu.make_async_copy(ins[a], outs[a].at[me], sems[2].at[a]) for a in range(na)]
        sends, recvs = [], []
        for k in range(1, N_DEV):
            px, py, pc = _peer(k)
            for a in range(na):
                s = (k - 1) * na + a
                sends.append(_remote(ins[a], outs[a].at[me], sems, s, (px, py, pc)))
                recvs.append(_remote(ins[a], outs[a].at[4 * px + 2 * py + pc], sems, s, (px, py, pc)))
        return sends, recvs, locs

    return _Rider(shards, [_sds((N_DEV,) + a.shape, a.dtype) for a in shards], (N_DEV - 1) * na, na, plan)


def _direct_exchange_rider(blocks):
    na = len(blocks)

    def plan(ins, outs, *sems):
        x, y, c = _place()
        me = 4 * x + 2 * y + c
        locs = [pltpu.make_async_copy(ins[a].at[c, 2 * x + y], outs[a].at[me], sems[2].at[a]) for a in range(na)]
        sends, recvs = [], []
        for k in range(1, N_DEV):
            px, py, pc = _peer(k)
            for a in range(na):
                s = (k - 1) * na + a
                sends.append(_remote(ins[a].at[pc, 2 * px + py], outs[a].at[me], sems, s, (px, py, pc)))
                recvs.append(_remote(ins[a].at[pc, 2 * px + py], outs[a].at[4 * px + 2 * py + pc], sems, s, (px, py, pc)))
        return sends, recvs, locs

    return _Rider(blocks, [_sds((N_DEV,) + a.shape[2:], a.dtype) for a in blocks], (N_DEV - 1) * na, na, plan)


def _swap_rider(halves):
    na = len(halves)

    def plan(ins, outs, *sems):
        x, y, c = _place()
        cps = [_remote(ins[a].at[1 - c], outs[a], sems, a, (x, y, 1 - c)) for a in range(na)]
        return cps, cps, []

    return _Rider(halves, [_sds(a.shape[1:], a.dtype) for a in halves], na, 1, plan)


def _chip_exchange_rider(parts, small=None):
    na = len(parts)
    n_chip = N_DEV // 2

    def plan(ins, outs, *sems):
        x, y, c = _place()
        chip = 2 * x + y
        locs = [pltpu.make_async_copy(ins[a].at[chip], outs[a].at[chip], sems[2].at[a]) for a in range(na)]
        sends, recvs = [], []
        for k in range(1, n_chip):
            px, py, _ = _peer(2 * k)
            for a in range(na):
                s = (k - 1) * na + a
                sends.append(_remote(ins[a].at[2 * px + py], outs[a].at[chip], sems, s, (px, py, c)))
                recvs.append(_remote(ins[a].at[2 * px + py], outs[a].at[2 * px + py], sems, s, (px, py, c)))
        if small is not None:
            me = 2 * chip + c
            locs.append(pltpu.make_async_copy(ins[na], outs[na].at[me], sems[2].at[na]))
            for k in range(1, N_DEV):
                px, py, pc = _peer(k)
                s = (n_chip - 1) * na + k - 1
                sends.append(_remote(ins[na], outs[na].at[me], sems, s, (px, py, pc)))
                recvs.append(_remote(ins[na], outs[na].at[4 * px + 2 * py + pc], sems, s, (px, py, pc)))
        return sends, recvs, locs

    extra = [] if small is None else [small]
    shapes = [_sds(a.shape, a.dtype) for a in parts] + [_sds((N_DEV,) + s.shape, s.dtype) for s in extra]
    n_sems = (n_chip - 1) * na + (N_DEV - 1) * len(extra)
    return _Rider(list(parts) + extra, shapes, n_sems, na + len(extra), plan)


def _pair_add(halves, other, core, name):
    _, nch, r, c = halves.shape

    def body(c_ref, h_ref, o_ref, p_ref):
        p_ref[...] = (h_ref[...].astype(F32) + o_ref[...].astype(F32)).astype(BF16)

    blk = pl.BlockSpec((None, r, c), lambda j, c_ref: (j, 0, 0))
    return _pc(
        body, name,
        grid_spec=pltpu.PrefetchScalarGridSpec(
            num_scalar_prefetch=1, grid=(nch,),
            in_specs=[pl.BlockSpec((None, None, r, c), lambda j, c_ref: (c_ref[0], j, 0, 0)), blk], out_specs=blk),
        out_shape=_sds((nch, r, c), BF16),
        compiler_params=_params(),
    )(core, halves, other)


def _sum_adamw(parts, w, m, v, name, rider=None):
    nl, r, c = w.shape
    tr = 256 if r % 256 == 0 else r

    def body(*refs):
        p_refs = refs[:nl]
        w_ref, m_ref, v_ref, g_ref, d_ref, mo_ref, vo_ref = refs[nl:]
        for l in range(nl):
            @pl.when(pl.program_id(0) == l)
            def _(p_ref=p_refs[l]):
                g = p_ref[0].astype(F32)
                for j in range(1, p_ref.shape[0]):
                    g = g + p_ref[j].astype(F32)
                mn = ADAM_B1 * m_ref[...] + (1.0 - ADAM_B1) * g
                vn = ADAM_B2 * v_ref[...] + (1.0 - ADAM_B2) * (g * g)
                m_hat = mn / (1.0 - ADAM_B1 ** ADAM_STEP)
                v_hat = vn / (1.0 - ADAM_B2 ** ADAM_STEP)
                g_ref[...] = g
                d_ref[...] = -ADAM_LR * (m_hat / (jnp.sqrt(v_hat) + ADAM_EPS) + ADAM_WD * w_ref[...])
                mo_ref[...] = mn
                vo_ref[...] = vn

    def part_spec(l, k):
        return pl.BlockSpec((k, tr, c), lambda li, i: (0, jnp.where(li == l, i, 0), 0))

    row = pl.BlockSpec((None, tr, c), lambda li, i: (li, i, 0))
    return _call(
        body, name, (*parts, w, m, v), rider, grid=(nl, r // tr),
        in_specs=[part_spec(l, p.shape[0]) for l, p in enumerate(parts)] + [row, row, row],
        out_specs=[row] * 4,
        out_shape=[_sds((nl, r, c), F32)] * 4,
        compiler_params=_params(),
    )


SMALL = ("lower_bounds", "pre_norm_g", "hgrn_norm_g", "fox_f_bias", "pool_w", "pool_scale", "post_norm_g")
SMALL_LANES = 128


def _small_size(tree):
    return sum(tree[k].size for k in SMALL)


def _pack_small(tree, extra=None):
    flat = jnp.concatenate([tree[k].reshape(-1) for k in SMALL] + ([] if extra is None else [extra.reshape(1)]))
    rows = -(-(_small_size(tree) + 1) // (8 * SMALL_LANES)) * 8
    return jnp.pad(flat, (0, rows * SMALL_LANES - flat.shape[0])).reshape(rows, SMALL_LANES)


def _unpack_small(packed, like):
    flat, out, off = packed.reshape(-1), {}, 0
    for k in SMALL:
        size = like[k].size
        out[k] = flat[off:off + size].reshape(like[k].shape)
        off += size
    return out


def _block_diag(pw):
    g = pw.shape[0]
    eye = jnp.eye(g, dtype=pw.dtype)
    return (eye[:, None, :, None] * pw[:, :, None, :]).reshape(g * HEAD, g * HEAD)


def _assemble_w_in(g_in, name):
    _, d, shard = g_in.shape
    tr = min(256, d)
    wide = MAIN_W + FC_PAD

    def body(g_ref, wm_ref, wf_ref, wmt_ref, wft_ref, row_s):
        row_s[:, MAIN_W:] = jnp.zeros((tr, FC_PAD), F32)
        for j in range(N_DEV):
            row_s[:, j * shard:(j + 1) * shard] = g_ref[j].astype(F32)
        wm_ref[...] = row_s[:, :MAIN_W].astype(BF16)
        wf_ref[...] = row_s[:, MAIN_W:].astype(BF16)
        for j in range(0, MAIN_W, 512):
            wmt_ref[j:j + 512, :] = row_s[:, j:j + 512].T.astype(BF16)
        wft_ref[...] = row_s[:, MAIN_W:].T.astype(BF16)

    return _pc(
        body, name, grid=(d // tr,),
        in_specs=[pl.BlockSpec((N_DEV, tr, shard), lambda i: (0, i, 0))],
        out_specs=[pl.BlockSpec((tr, MAIN_W), lambda i: (i, 0)), pl.BlockSpec((tr, FC_PAD), lambda i: (i, 0)),
                   pl.BlockSpec((MAIN_W, tr), lambda i: (0, i)), pl.BlockSpec((FC_PAD, tr), lambda i: (0, i))],
        out_shape=[_sds((d, MAIN_W), BF16), _sds((d, FC_PAD), BF16), _sds((MAIN_W, d), BF16), _sds((FC_PAD, d), BF16)],
        scratch_shapes=[pltpu.VMEM((tr, wide), F32)],
        compiler_params=_params(),
    )(g_in)


def _w_out_parts(g_out):
    full_out = g_out.reshape(N_DEV * g_out.shape[1], g_out.shape[2])
    return full_out, full_out.T


def _layer_fwd(l, x, lbs, weights, lw, nb, rider_h=None, rider_c=None, target=None):
    n = x.shape[0]
    t = n // nb
    w_main, w_fc, _, _, w_out, _ = lw
    bias = jnp.pad(weights["fox_f_bias"][l:l + 1], ((0, 0), (0, FC_PAD - FOX_HEADS)))
    wbd = _block_diag(weights["pool_w"][l]).astype(BF16)
    proj, fc, ht, qt, kt = _in_proj_fwd(x, weights["pre_norm_g"][l:l + 1], w_main, w_fc, f"in_proj_fwd_{l}")
    c_col = _fox_decay_fwd(fc, bias, nb, f"fox_decay_fwd_{l}")
    c_row = c_col.reshape(nb, t, FC_PAD)[:, :, :FOX_HEADS].transpose(0, 2, 1)
    (o_h, s0), rode_h = _hgrn_fwd(proj, lbs[l:l + 1], _block_ones(HGRN_W, BF16), nb, f"hgrn_fwd_{l}", rider_h)
    o_b = _pool_fwd(proj, wbd, weights["pool_scale"][l:l + 1], nb, f"pool_fwd_{l}")
    (o_c, lse), rode_c = _fox_fwd(proj, kt, c_col, c_row, nb, f"fox_fwd_{l}", rider_c)
    if w_out is None:
        lw = tuple(lw[:4]) + _w_out_parts(rode_h[0])
        w_out, rode_h = lw[4], rode_h[1:]
    x_next, mixt, y = _merge_fwd(x, proj, o_h, o_b, o_c, weights["hgrn_norm_g"][l:l + 1], w_out,
                                 weights["post_norm_g"][l:l + 1], f"merge_fwd_{l}", target)
    return x_next, (x, proj, fc, ht, qt, kt, c_col, c_row, o_h, s0, o_c, lse, mixt, y, bias, wbd), lw, (rode_h, rode_c)


def _layer_bwd(l, dx, saved, lbs, weights, lw, nb, rider=None):
    _, proj, fc, ht, qt, kt, c_col, c_row, o_h, s0, o_c, lse, mixt, y, bias, wbd = saved
    n = proj.shape[0]
    t = n // nb
    w_out_t = lw[5]
    g = {}
    dy, dmix, dgp, da, dat, d_gc, delta = _merge_bwd(dx, y, weights["post_norm_g"][l:l + 1], w_out_t, proj, o_c,
                                                f"merge_bwd_{l}")
    g["post_norm_g"] = dgp[0]
    g["w_out"] = _w_out_grad(mixt, dy, f"w_out_grad_{l}")
    d_a, dgh, dlb = _hgrn_bwd(dmix, proj, o_h, s0, weights["hgrn_norm_g"][l:l + 1], lbs[l:l + 1],
                              _block_ones(HGRN_W, BF16), nb, f"hgrn_bwd_{l}")
    g["hgrn_norm_g"], g["lbs"] = dgh[0], dlb[0]
    d_b, dwbd, dps = _pool_bwd(dmix, proj, wbd, wbd.T, weights["pool_scale"][l:l + 1], nb, f"pool_bwd_{l}")
    g["pool_w"] = jnp.stack([dwbd[j * HEAD:(j + 1) * HEAD, j * HEAD:(j + 1) * HEAD] for j in range(len(POOL_WINDOWS))])
    g["pool_scale"] = dps[0]
    delta_row, lse_row = [a.reshape(nb, t, FC_PAD)[:, :, :FOX_HEADS].transpose(0, 2, 1) for a in (delta, lse)]
    (d_qc, d_kc, d_vc, dc_k, dc_q), rode = _fox_bwd(proj, qt, kt, da, dat, c_col, c_row, lse_row, delta_row, nb,
                                                    f"fox_bwd_{l}", rider)
    dc_q = jnp.pad(dc_q.transpose(0, 2, 1).reshape(n, FOX_HEADS), ((0, 0), (0, FC_PAD - FOX_HEADS)))
    d_fc, dbias = _fox_decay_bwd(dc_q, dc_k, fc, bias, nb, f"fox_decay_bwd_{l}")
    g["fox_f_bias"] = dbias[0, :FOX_HEADS]
    pieces = [(d_a, C_QA), (d_b, C_UB), (d_qc, C_QC), (d_kc, C_KC), (d_vc, C_VC), (d_gc, C_GC), (d_fc, None)]
    g["w_in"] = _w_in_grad(ht, pieces, f"w_in_grad_{l}")
    return g, pieces, rode


def _layer_bwd_input(l, dx, pieces, saved, weights, lw, rider=None):
    (dxi, dgpre), rode = _in_proj_bwd(pieces, lw[2], lw[3], saved[0], weights["pre_norm_g"][l:l + 1], dx,
                                      f"in_proj_bwd_{l}", rider)
    return dxi, dgpre[0], rode


def kernel(x, lower_bounds, pre_norm_g, w_in, hgrn_norm_g, fox_f_bias, pool_w, pool_scale, w_out, post_norm_g, loss_target, m_lower_bounds, m_pre_norm_g, m_w_in, m_hgrn_norm_g, m_fox_f_bias, m_pool_w, m_pool_scale, m_w_out, m_post_norm_g, v_lower_bounds, v_pre_norm_g, v_w_in, v_hgrn_norm_g, v_fox_f_bias, v_pool_w, v_pool_scale, v_w_out, v_post_norm_g):
    weights = dict(lower_bounds=lower_bounds, pre_norm_g=pre_norm_g, hgrn_norm_g=hgrn_norm_g, fox_f_bias=fox_f_bias,
                   pool_w=pool_w, pool_scale=pool_scale, post_norm_g=post_norm_g)
    mom_m = dict(lower_bounds=m_lower_bounds, pre_norm_g=m_pre_norm_g, hgrn_norm_g=m_hgrn_norm_g, fox_f_bias=m_fox_f_bias,
                 pool_w=m_pool_w, pool_scale=m_pool_scale, post_norm_g=m_post_norm_g)
    mom_v = dict(lower_bounds=v_lower_bounds, pre_norm_g=v_pre_norm_g, hgrn_norm_g=v_hgrn_norm_g, fox_f_bias=v_fox_f_bias,
                 pool_w=v_pool_w, pool_scale=v_pool_scale, post_norm_g=v_post_norm_g)
    depth = w_in.shape[0]
    nb, t, d = x.shape
    n = nb * t
    core = lax.axis_index("c").astype(jnp.int32).reshape(1)
    shards = [(w_in[l].astype(BF16), w_out[l].astype(BF16)) for l in range(depth)]
    lbs = _lower_bound_table(lower_bounds, "lower_bound_table")

    (g_in,) = _gather_weights(shards[0][0])
    coming = tuple(_assemble_w_in(g_in, "assemble_w_in_0")) + (None, None)
    xl, saved, lw = x.reshape(n, d), [], []
    for l in range(depth):
        last = l + 1 == depth
        ride_h = ([shards[l][1]] if coming[4] is None else []) + ([] if last else [shards[l + 1][1]])
        xl, sv, lw_l, (rode_h, rode_c) = _layer_fwd(
            l, xl, lbs, weights, coming, nb, _gather_rider(ride_h) if ride_h else None,
            None if last else _gather_rider([shards[l + 1][0]]), loss_target.reshape(n, d) if last else None)
        saved.append(sv)
        lw.append(lw_l)
        if not last:
            coming = tuple(_assemble_w_in(rode_c[0], f"assemble_w_in_{l + 1}")) + _w_out_parts(rode_h[0])
    dx, sq = xl
    loss_here = 0.5 * jnp.sum(sq) / d

    grads, recv, pending = [None] * depth, [None] * depth, None
    for l in reversed(range(depth)):
        g, pieces, rode = _layer_bwd(l, dx, saved[l], lbs, weights, lw[l], nb, pending)
        if rode is not None:
            recv[l + 1] = rode
        blocks = (g["w_in"], g["w_out"])
        if l > 0:
            pending = _direct_exchange_rider(blocks)
            dx, g["pre_norm_g"], _ = _layer_bwd_input(l, dx, pieces, saved[l], weights, lw[l])
        else:
            other = _run_rider(_swap_rider(blocks), "grad_swap")
            summed = [_pair_add(hv, ot, core, f"grad_pair_add_{i}") for i, (hv, ot) in enumerate(zip(blocks, other))]
            dx, g["pre_norm_g"], recv[l] = _layer_bwd_input(l, dx, pieces, saved[l], weights, lw[l],
                                                            _chip_exchange_rider(summed))
        grads[l] = g
    small = {k: jnp.stack([grads[l][k] for l in range(depth)]) for k in SMALL if k != "lower_bounds"}
    small["lower_bounds"] = _lower_bound_bwd(lower_bounds, jnp.stack([grads[l]["lbs"] for l in range(depth)]),
                                             "lower_bound_bwd")
    (r_small,) = _run_rider(_gather_rider([_pack_small(small, loss_here)]), "small_grads_gather")

    res_in, _ = _sum_adamw([recv[l][0] for l in range(depth)], w_in, m_w_in, v_w_in, "adamw_w_in")
    res_out, _ = _sum_adamw([recv[l][1] for l in range(depth)], w_out, m_w_out, v_w_out, "adamw_w_out")
    res_small, _ = _sum_adamw([r_small], _pack_small(weights)[None], _pack_small(mom_m)[None], _pack_small(mom_v)[None],
                              "adamw_small")
    loss = res_small[0][0].reshape(-1)[_small_size(weights)]

    names = ("lower_bounds", "pre_norm_g", "w_in", "hgrn_norm_g", "fox_f_bias", "pool_w", "pool_scale", "w_out", "post_norm_g")
    outs = [loss, dx.reshape(nb, t, d)]
    for i in range(4):
        full = dict(_unpack_small(res_small[i][0], weights), w_in=res_in[i], w_out=res_out[i])
        outs += [full[k] for k in names]
    return tuple(outs)
```

```python
import functools

import jax
import jax.numpy as jnp
from jax import lax
from jax.experimental import pallas as pl
from jax.experimental.pallas import tpu as pltpu

F32, BF16 = jnp.float32, jnp.bfloat16
HI = lax.Precision.HIGHEST
MESH = pl.DeviceIdType.MESH
N_DEV = 8

NORM_EPS = 1e-6
MASK_VALUE = -1e30
TINY = 1e-30
CHUNK = 64
SUB = 16
HGRN_W, POOL_W, FOX_W = 256, 256, 512
HEAD = 64
FOX_HEADS = 8
POOL_WINDOWS = (2, 4, 8, 16)
POOL_HALO = 16
MAIN_W = 3584
FC_PAD = 128
C_QA, C_FA, C_IA, C_GA, C_UB, C_GB, C_QC, C_KC, C_VC, C_GC = 0, 256, 512, 768, 1024, 1280, 1536, 2048, 2560, 3072
FOX_SCALE = HEAD ** -0.5

ADAM_LR, ADAM_B1, ADAM_B2, ADAM_EPS, ADAM_WD, ADAM_STEP = 0.001, 0.9, 0.999, 1e-08, 0.01, 10

VMEM_LIMIT = 56 * 1024 * 1024


def _pc(fn, name, **kw):
    return pl.pallas_call(fn, name=name, **kw)


def _params(**kw):
    return pltpu.CompilerParams(vmem_limit_bytes=VMEM_LIMIT, **kw)


class _Rider:
    def __init__(self, inputs, out_shapes, n_sems, n_local, plan):
        self.inputs, self.out_shapes, self.n_sems, self.n_local, self.plan = list(inputs), list(out_shapes), n_sems, n_local, plan

    def start(self, ins, outs, *sems):
        sends, _, locs = self.plan(ins, outs, *sems)
        for cp in locs + sends:
            cp.start()

    def wait(self, ins, outs, *sems):
        sends, recvs, locs = self.plan(ins, outs, *sems)
        for cp in recvs:
            cp.wait_recv()
        for cp in sends:
            cp.wait_send()
        for cp in locs:
            cp.wait()

    def sem_shapes(self):
        return [pltpu.SemaphoreType.DMA((self.n_sems,)), pltpu.SemaphoreType.DMA((self.n_sems,)),
                pltpu.SemaphoreType.DMA((self.n_local,))]


def _call(body, name, args, rider=None, *, grid, in_specs, out_specs, out_shape, scratch_shapes=(), **kw):
    if rider is None:
        res = _pc(body, name, grid=grid, in_specs=in_specs, out_specs=out_specs, out_shape=out_shape,
                  scratch_shapes=list(scratch_shapes), **kw)(*args)
        return res, None
    n_in, n_out, n_scr = len(in_specs), len(out_specs), len(scratch_shapes)
    n_rin, n_rout = len(rider.inputs), len(rider.out_shapes)

    def ridden(*refs):
        ins, refs = refs[:n_in], refs[n_in:]
        rins, refs = refs[:n_rin], refs[n_rin:]
        outs, refs = refs[:n_out], refs[n_out:]
        routs, refs = refs[:n_rout], refs[n_rout:]
        scr, sems = refs[:n_scr], refs[n_scr:]
        first = functools.reduce(jnp.logical_and, [pl.program_id(a) == 0 for a in range(len(grid))])
        last = functools.reduce(jnp.logical_and, [pl.program_id(a) == g - 1 for a, g in enumerate(grid)])

        @pl.when(first)
        def _():
            rider.start(rins, routs, *sems)

        body(*ins, *outs, *scr)

        @pl.when(last)
        def _():
            rider.wait(rins, routs, *sems)

    any_spec = pl.BlockSpec(memory_space=pl.ANY)
    res = _pc(ridden, name, grid=grid, in_specs=list(in_specs) + [any_spec] * n_rin,
              out_specs=list(out_specs) + [any_spec] * n_rout, out_shape=list(out_shape) + rider.out_shapes,
              scratch_shapes=list(scratch_shapes) + rider.sem_shapes(), **kw)(*args, *rider.inputs)
    return res[:n_out], res[n_out:]


def _run_rider(rider, name):
    n_rin = len(rider.inputs)

    def body(*refs):
        ins, outs, sems = refs[:n_rin], refs[n_rin:n_rin + len(rider.out_shapes)], refs[n_rin + len(rider.out_shapes):]
        rider.start(ins, outs, *sems)
        rider.wait(ins, outs, *sems)

    any_spec = pl.BlockSpec(memory_space=pl.ANY)
    return _pc(body, name, in_specs=[any_spec] * n_rin, out_specs=[any_spec] * len(rider.out_shapes),
               out_shape=rider.out_shapes, scratch_shapes=rider.sem_shapes())(*rider.inputs)


def _dot(a, b):
    return jnp.dot(a, b, preferred_element_type=F32)


def _dot_nt(a, b):
    return lax.dot_general(a, b, (((1,), (1,)), ((), ())), preferred_element_type=F32)


def _dot_tn(a, b):
    return lax.dot_general(a, b, (((0,), (0,)), ((), ())), preferred_element_type=F32)


def _dot_hi(a, b):
    return jnp.dot(a, b, precision=HI, preferred_element_type=F32)


def _split2(x):
    hi = x.astype(BF16)
    return hi, (x - hi.astype(F32)).astype(BF16)


def _sel_dot(sel, x):
    hi, lo = _split2(x)
    sb = sel.astype(BF16)
    return _dot(sb, hi) + _dot(sb, lo)


def _dot_sel(x, sel):
    hi, lo = _split2(x)
    sb = sel.astype(BF16)
    return _dot(hi, sb) + _dot(lo, sb)


def _sigmoid(x):
    return 1.0 / (1.0 + jnp.exp(-x))


def _block_ones(n, dtype):
    r = lax.broadcasted_iota(jnp.int32, (n, n), 0) // HEAD
    c = lax.broadcasted_iota(jnp.int32, (n, n), 1) // HEAD
    return (r == c).astype(dtype)


def _sds(shape, dtype):
    return jax.ShapeDtypeStruct(shape, dtype)


def _in_proj_fwd(x, g_pre, w_main, w_fc, name):
    n, d = x.shape
    tm = min(512, n)

    def body(x_ref, g_ref, w_ref, wf_ref, proj_ref, fc_ref, ht_ref, qt_ref, kt_ref):
        xv = x_ref[...]
        r = lax.rsqrt(jnp.mean(xv * xv, axis=-1, keepdims=True) + NORM_EPS)
        hf = xv * r * g_ref[...]
        hb = hf.astype(BF16)
        ht_ref[...] = hf.T.astype(BF16)
        for j in range(0, MAIN_W, FOX_W):
            res = _dot(hb, w_ref[:, j:j + FOX_W])
            proj_ref[:, j:j + FOX_W] = res
            if j == C_QC:
                qt_ref[...] = (res * FOX_SCALE).T.astype(BF16)
            if j == C_KC:
                kt_ref[...] = res.T.astype(BF16)
        fc_ref[...] = _dot(hb, wf_ref[...])

    def cols(rows):
        return pl.BlockSpec((rows, tm), lambda i: (0, i))

    return _pc(
        body, name, grid=(n // tm,),
        in_specs=[pl.BlockSpec((tm, d), lambda i: (i, 0)), pl.BlockSpec((1, d), lambda i: (0, 0)),
                  pl.BlockSpec((d, MAIN_W), lambda i: (0, 0)), pl.BlockSpec((d, FC_PAD), lambda i: (0, 0))],
        out_specs=[pl.BlockSpec((tm, MAIN_W), lambda i: (i, 0)), pl.BlockSpec((tm, FC_PAD), lambda i: (i, 0)),
                   cols(d), cols(FOX_W), cols(FOX_W)],
        out_shape=[_sds((n, MAIN_W), F32), _sds((n, FC_PAD), F32), _sds((d, n), BF16), _sds((FOX_W, n), BF16),
                   _sds((FOX_W, n), BF16)],
        compiler_params=_params(),
    )(x, g_pre, w_main, w_fc)


def _fox_decay_fwd(fc, bias, nb, name):
    n = fc.shape[0]
    t = n // nb
    tt = min(256, t)
    nt = t // tt

    def body(fc_ref, b_ref, c_ref, cr_ref, carry):
        i = pl.program_id(1)

        @pl.when(i == 0)
        def _():
            carry[...] = jnp.zeros_like(carry)

        xv = fc_ref[...] + b_ref[...]
        lf = jnp.minimum(xv, 0.0) - jnp.log(1.0 + jnp.exp(-jnp.abs(xv)))
        r = lax.broadcasted_iota(jnp.int32, (tt, tt), 0)
        cc = lax.broadcasted_iota(jnp.int32, (tt, tt), 1)
        cs = _dot_hi((r >= cc).astype(F32), lf) + carry[...]
        c_ref[...] = cs
        cr_ref[...] = cs.T[:FOX_HEADS, :]
        carry[...] = cs[tt - 1:tt, :]

    return _pc(
        body, name, grid=(nb, nt),
        in_specs=[pl.BlockSpec((tt, FC_PAD), lambda b, i: (b * nt + i, 0)), pl.BlockSpec((1, FC_PAD), lambda b, i: (0, 0))],
        out_specs=[pl.BlockSpec((tt, FC_PAD), lambda b, i: (b * nt + i, 0)),
                   pl.BlockSpec((None, FOX_HEADS, tt), lambda b, i: (b, 0, i))],
        out_shape=[_sds((n, FC_PAD), F32), _sds((nb, FOX_HEADS, t), F32)],
        scratch_shapes=[pltpu.VMEM((1, FC_PAD), F32)],
        compiler_params=_params(),
    )(fc, bias)


def _hgrn_gates(q, z, lb):
    sig = _sigmoid(z)
    sn = _sigmoid(-z)
    f = lb + (1.0 - lb) * sig
    g = jnp.log(jnp.maximum(f, TINY))
    k = (1.0 - lb) * sn
    sq = _sigmoid(q)
    return sig, sn, f, g, k, sq


def _sub_tri(n, lower):
    r = lax.broadcasted_iota(jnp.int32, (n, n), 0)
    c = lax.broadcasted_iota(jnp.int32, (n, n), 1)
    tri = (r >= c) if lower else (r <= c)
    return jnp.logical_and(r // SUB == c // SUB, tri).astype(F32)


def _live_rows(t):
    return 8 * (t // 8 + 1)


def _pad_rows(x):
    return x if x.shape[0] == SUB else jnp.concatenate([x, jnp.zeros((SUB - x.shape[0], x.shape[1]), x.dtype)], axis=0)


def _hgrn_decays(qs, k, b):
    srow = lax.broadcasted_iota(jnp.int32, (SUB, HGRN_W), 0)
    es, ws = [], []
    for t in range(SUB):
        r = _live_rows(t)
        e = jnp.where(srow[:r] <= t, jnp.exp(b[t:t + 1, :] - b[:r]), 0.0)
        es.append(e)
        ws.append(_pad_rows(e * (qs[t:t + 1, :] * k[:r])))
    return srow, es, ws


def _hgrn_state_step(st, k, v, b, bmask):
    bl = b[SUB - 1:SUB, :]
    ktil = k * jnp.exp(bl - b)
    return st * jnp.exp(bl) + _dot_tn(v.astype(BF16), ktil.astype(BF16)) * bmask


def _hgrn_tile(t):
    return min(256, t)


def _hgrn_fwd(proj, lb, ones_b, nb, name, rider=None):
    n = proj.shape[0]
    t = n // nb
    tt = _hgrn_tile(t)
    nt = t // tt
    ncs = tt // CHUNK
    w = HGRN_W

    def body(q_ref, z_ref, v_ref, lb_ref, ones_ref, o_ref, s0_ref, st_s, b_s, qs_s, k_s):
        @pl.when(pl.program_id(1) == 0)
        def _():
            st_s[...] = jnp.zeros_like(st_s)

        q = q_ref[...]
        _, _, _, g, k, sq = _hgrn_gates(q, z_ref[...], lb_ref[...])
        b_s[...] = _sel_dot(_sub_tri(tt, True), g)
        qs_s[...] = q * sq
        k_s[...] = k
        bmask = _block_ones(w, F32)
        ones_b = ones_ref[...]

        def chunk(c, carry):
            st = st_s[...]
            s0_ref[c] = st
            base = pl.multiple_of(c * CHUNK, CHUNK)
            tiles = []
            for u in range(CHUNK // SUB):
                rows = pl.ds(base + u * SUB, SUB)
                tiles.append((qs_s[rows, :], k_s[rows, :], v_ref[rows, :], b_s[rows, :]))
            aexps = []
            for qs, k, v, b in tiles:
                _, _, ws = _hgrn_decays(qs, k, b)
                aexps.append(_dot(jnp.concatenate(ws, axis=0).astype(BF16), ones_b))
            inters = []
            for qs, k, v, b in tiles:
                inters.append(_dot_nt((qs * jnp.exp(b)).astype(BF16), st.astype(BF16)))
                st = _hgrn_state_step(st, k, v, b, bmask)
            st_s[...] = st
            for u, ((qs, k, v, b), aexp, o) in enumerate(zip(tiles, aexps, inters)):
                for t in range(SUB):
                    r = _live_rows(t)
                    row = o[t:t + 1, :] + jnp.sum(aexp[t * SUB:t * SUB + r, :] * v[:r], axis=0, keepdims=True)
                    o_ref[pl.ds(base + u * SUB + t, 1), :] = row
            return carry

        lax.fori_loop(0, ncs, chunk, 0)

    def col(j):
        return pl.BlockSpec((tt, w), lambda b, i: (b * nt + i, j))

    return _call(
        body, name, (proj, proj, proj, lb, ones_b), rider, grid=(nb, nt),
        in_specs=[col(C_QA // w), col(C_FA // w), col(C_IA // w), pl.BlockSpec((1, w), lambda b, i: (0, 0)),
                  pl.BlockSpec((w, w), lambda b, i: (0, 0))],
        out_specs=[pl.BlockSpec((tt, w), lambda b, i: (b * nt + i, 0)),
                   pl.BlockSpec((ncs, w, w), lambda b, i: (b * nt + i, 0, 0))],
        out_shape=[_sds((n, w), F32), _sds((n // CHUNK, w, w), F32)],
        scratch_shapes=[pltpu.VMEM((w, w), F32)] + [pltpu.VMEM((tt, w), F32)] * 3,
        compiler_params=_params(),
    )


def _pool_lane_windows():
    lane = lax.broadcasted_iota(jnp.int32, (1, POOL_W), 1) // HEAD
    wl = jnp.zeros((1, POOL_W), F32)
    for gi, win in enumerate(POOL_WINDOWS):
        wl = jnp.where(lane == gi, float(win), wl)
    return lane, wl


def _pool_select(lane, parts):
    out = parts[-1]
    for gi in range(len(parts) - 2, -1, -1):
        out = jnp.where(lane == gi, parts[gi], out)
    return out


def _pool_mix(u, halo, t0, tt):
    lane, wl = _pool_lane_windows()
    ext = jnp.concatenate([halo, u], axis=0)
    sums, cur, shift = [], ext, 1
    for _ in POOL_WINDOWS:
        cur = cur + pltpu.roll(cur, shift, axis=0)
        sums.append(cur[POOL_HALO:, :])
        shift *= 2
    tpos = (t0 + lax.broadcasted_iota(jnp.int32, (tt, POOL_W), 0)).astype(F32)
    cnt = jnp.minimum(tpos + 1.0, wl)
    return _pool_select(lane, sums) / cnt - u, cnt


def _pool_specs(tt, nt, nhb):
    cu, cg = C_UB // POOL_W, C_GB // POOL_W
    return [pl.BlockSpec((tt, POOL_W), lambda b, i: (b * nt + i, cu)),
            pl.BlockSpec((tt, POOL_W), lambda b, i: (b * nt + i, cg)),
            pl.BlockSpec((POOL_HALO, POOL_W), lambda b, i: (jnp.maximum((b * nt + i) * nhb - 1, 0), cu))]


def _pool_fwd(proj, wbd, scale, nb, name):
    n = proj.shape[0]
    t = n // nb
    tt = min(512, t)
    nt = t // tt
    nhb = tt // POOL_HALO

    def body(u_ref, g_ref, h_ref, w_ref, s_ref, o_ref):
        i = pl.program_id(1)
        halo = jnp.where(i == 0, 0.0, h_ref[...])
        pooled, _ = _pool_mix(u_ref[...], halo, i * tt, tt)
        gv = g_ref[...]
        o_ref[...] = _dot(pooled.astype(BF16), w_ref[...]) * s_ref[...] * (gv * _sigmoid(gv))

    return _pc(
        body, name, grid=(nb, nt),
        in_specs=_pool_specs(tt, nt, nhb) + [pl.BlockSpec((POOL_W, POOL_W), lambda b, i: (0, 0)),
                                             pl.BlockSpec((1, POOL_W), lambda b, i: (0, 0))],
        out_specs=pl.BlockSpec((tt, POOL_W), lambda b, i: (b * nt + i, 0)),
        out_shape=_sds((n, POOL_W), F32),
        compiler_params=_params(),
    )(proj, proj, proj, wbd, scale)


def _rows_reduce(x, op, final):
    while x.shape[0] > 8 and x.shape[0] % 16 == 0:
        half = x.shape[0] // 2
        x = op(x[:half], x[half:])
    return final(x, axis=0, keepdims=True)


def _tri_pair(step, n):
    a = sum([(step >= r * (r + 1) // 2).astype(jnp.int32) for r in range(1, n)], jnp.int32(0))
    return a, step - a * (a + 1) // 2


def _lane_lo():
    return lax.broadcasted_iota(jnp.int32, (1, 2 * HEAD), 1) < HEAD


def _put_col(tile, hh, colv):
    lane = lax.broadcasted_iota(jnp.int32, tile.shape, 1)
    return jnp.where(lane == hh, colv, tile)


def _fox_fwd(proj, kt, c_col, c_row, nb, name, rider=None):
    n = proj.shape[0]
    t = n // nb
    tb = min(256, t)
    nq = t // tb
    pw = 2 * HEAD

    def body(q_ref, kt_ref, v_ref, cc_ref, cr_ref, o_ref, lse_ref, m_s, acc_s, cq_s):
        qi, kj = _tri_pair(pl.program_id(1), nq)

        @pl.when(kj == 0)
        def _():
            m_s[...] = jnp.full_like(m_s, -jnp.inf)
            acc_s[...] = jnp.zeros_like(acc_s)
            for hh in range(FOX_HEADS):
                cq_s[hh] = jnp.broadcast_to(cc_ref[:, hh:hh + 1], (tb, pw))

        def block(masked):
            lo = _lane_lo()
            if masked:
                causal = lax.broadcasted_iota(jnp.int32, (tb, tb), 0) >= lax.broadcasted_iota(jnp.int32, (tb, tb), 1)
            def lanes(hh):
                return lo if hh % 2 == 0 else jnp.logical_not(lo)

            def scores(hh):
                sl = slice((hh // 2) * pw, (hh // 2 + 1) * pw)
                return _dot(jnp.where(lanes(hh), q_ref[:, sl] * FOX_SCALE, 0.0).astype(BF16), kt_ref[sl, :])

            ahead = scores(0)
            for hh in range(FOX_HEADS):
                s = ahead
                if hh + 1 < FOX_HEADS:
                    ahead = scores(hh + 1)
                s = s + (jnp.tile(cq_s[hh], (1, tb // pw)) - cr_ref[hh:hh + 1, :])
                if masked:
                    s = jnp.where(causal, s, MASK_VALUE)
                m_prev = m_s[hh]
                m_new = jnp.maximum(m_prev, jnp.max(s, axis=1, keepdims=True))
                alpha = jnp.exp(m_prev - m_new)
                pe = jnp.exp(s - jnp.tile(m_new, (1, tb // pw)))
                m_s[hh] = m_new
                vf = v_ref[:, (hh // 2) * pw:(hh // 2 + 1) * pw]
                acc_s[hh] = alpha * acc_s[hh] + _dot(pe.astype(BF16), jnp.where(lanes(hh), vf, 1.0).astype(BF16))

        @pl.when(kj < qi)
        def _():
            block(False)

        @pl.when(kj == qi)
        def _():
            block(True)
            lo = _lane_lo()
            m_all, l_all = jnp.zeros((tb, FC_PAD), F32), jnp.ones((tb, FC_PAD), F32)
            for p in range(FOX_HEADS // 2):
                a0, a1 = acc_s[2 * p], acc_s[2 * p + 1]
                both = pltpu.roll(jnp.where(lo, a1, a0), HEAD, axis=1)
                o_ref[:, p * pw:(p + 1) * pw] = jnp.where(lo, a0, a1) / both
                m_all = _put_col(_put_col(m_all, 2 * p, m_s[2 * p]), 2 * p + 1, m_s[2 * p + 1])
                l_all = _put_col(_put_col(l_all, 2 * p, both), 2 * p + 1, a1)
            lse_ref[...] = (m_all + jnp.log(l_all)).T[:FOX_HEADS, :]

    def qspec(wd, j):
        return pl.BlockSpec((tb, wd), lambda b, st: (b * nq + _tri_pair(st, nq)[0], j))

    def kspec(j):
        return pl.BlockSpec((tb, FOX_W), lambda b, st: (b * nq + _tri_pair(st, nq)[1], j))

    return _call(
        body, name, (proj, kt, proj, c_col, c_row), rider, grid=(nb, nq * (nq + 1) // 2),
        in_specs=[qspec(FOX_W, C_QC // FOX_W), pl.BlockSpec((FOX_W, tb), lambda b, st: (0, b * nq + _tri_pair(st, nq)[1])),
                  kspec(C_VC // FOX_W), qspec(FC_PAD, 0),
                  pl.BlockSpec((None, FOX_HEADS, tb), lambda b, st: (b, 0, _tri_pair(st, nq)[1]))],
        out_specs=[qspec(FOX_W, 0), pl.BlockSpec((None, FOX_HEADS, tb), lambda b, st: (b, 0, _tri_pair(st, nq)[0]))],
        out_shape=[_sds((n, FOX_W), F32), _sds((nb, FOX_HEADS, t), F32)],
        scratch_shapes=[pltpu.VMEM((FOX_HEADS, tb, pw), F32), pltpu.VMEM((FOX_HEADS, tb, pw), F32),
                        pltpu.VMEM((FOX_HEADS, tb, pw), F32)],
        compiler_params=_params(),
    )


def _head_mean(x, ones_f):
    return _dot_sel(x, ones_f) * (1.0 / HEAD)


def _merge_fwd(x, proj, o_h, o_b, o_c, gh, w_out, g_post, name, target=None):
    n, d = x.shape
    tm = min(512, n)

    def body(*refs):
        x_ref, ga_ref, gc_ref, oh_ref, ob_ref, oc_ref, gh_ref, w_ref, gp_ref = refs[:9]
        if target is None:
            xo_ref, mixt_ref, y_ref = refs[9:]
        else:
            t_ref, dx_ref, sq_ref, mixt_ref, y_ref = refs[9:]
        oh = oh_ref[...]
        ones_f = _block_ones(HGRN_W, F32)
        na = oh * lax.rsqrt(_head_mean(oh * oh, ones_f) + NORM_EPS) * gh_ref[...]
        ga, gc = ga_ref[...], gc_ref[...]
        mixed = jnp.concatenate([na * (ga * _sigmoid(ga)), ob_ref[...], oc_ref[...] * (gc * _sigmoid(gc))], axis=1)
        mixt_ref[...] = mixed.T.astype(BF16)
        y = _dot(mixed.astype(BF16), w_ref[...])
        y_ref[...] = y
        xn = x_ref[...] + y * lax.rsqrt(jnp.mean(y * y, axis=-1, keepdims=True) + NORM_EPS) * gp_ref[...]
        if target is None:
            xo_ref[...] = xn
        else:
            @pl.when(pl.program_id(0) == 0)
            def _():
                sq_ref[...] = jnp.zeros_like(sq_ref)

            e = xn - t_ref[...]
            dx_ref[...] = e * (1.0 / d)
            sq_ref[...] += jnp.sum(e * e, axis=0, keepdims=True)

    def row(wd, j=0):
        return pl.BlockSpec((tm, wd), lambda i: (i, j))

    def full(a, b):
        return pl.BlockSpec((a, b), lambda i: (0, 0))

    head = [] if target is None else [target]
    res = _pc(
        body, name, grid=(n // tm,),
        in_specs=[row(d), row(HGRN_W, C_GA // HGRN_W), row(FOX_W, C_GC // FOX_W), row(HGRN_W), row(POOL_W), row(FOX_W),
                  full(1, HGRN_W), full(d, d), full(1, d)] + [row(d)] * len(head),
        out_specs=[row(d)] + [full(1, d)] * len(head) + [pl.BlockSpec((d, tm), lambda i: (0, i)), row(d)],
        out_shape=[_sds((n, d), F32)] + [_sds((1, d), F32)] * len(head) + [_sds((d, n), BF16), _sds((n, d), F32)],
        compiler_params=_params(),
    )(x, proj, proj, o_h, o_b, o_c, gh, w_out, g_post, *head)
    return (res[0], res[1], res[2]) if target is None else ((res[0], res[1]), res[2], res[3])


def _rms_bwd(dy_scaled, xhat, r):
    return r * (dy_scaled - xhat * jnp.mean(dy_scaled * xhat, axis=-1, keepdims=True))


def _merge_bwd(dxo, y, g_post, w_out_t, proj, o_c, nb, name):
    n, d = y.shape
    t = n // nb
    tm = min(512, t)
    nt = t // tm
    wab = HGRN_W + POOL_W

    def body(dx_ref, y_ref, gp_ref, wt_ref, gc_ref, oc_ref, dy_ref, dm_ref, dgp_ref, da_ref, dat_ref, dg_ref, dl_ref):
        @pl.when(pl.program_id(0) == 0)
        def _():
            dgp_ref[...] = jnp.zeros_like(dgp_ref)

        yv, dxv = y_ref[...], dx_ref[...]
        r = lax.rsqrt(jnp.mean(yv * yv, axis=-1, keepdims=True) + NORM_EPS)
        yh = yv * r
        dgp_ref[...] += jnp.sum(dxv * yh, axis=0, keepdims=True)
        dyb = _rms_bwd(dxv * gp_ref[...], yh, r).astype(BF16)
        dy_ref[...] = dyb
        dm_ref[...] = _dot(dyb, wt_ref[:, :wab])
        dmc = _dot(dyb, wt_ref[:, wab:])
        gc, oc = gc_ref[...], oc_ref[...]
        sg = _sigmoid(gc)
        da = dmc * (gc * sg)
        da_ref[...] = da.astype(BF16)
        dat_ref[...] = da.T.astype(BF16)
        dg_ref[...] = (dmc * oc * (sg * (1.0 + gc * (1.0 - sg)))).astype(BF16)
        rr = lax.broadcasted_iota(jnp.int32, (FOX_W, FC_PAD), 0) // HEAD
        cc = lax.broadcasted_iota(jnp.int32, (FOX_W, FC_PAD), 1)
        dl_ref[...] = _dot_sel(da * oc, (rr == cc).astype(F32)).T[:FOX_HEADS, :]

    def row(wd, j=0):
        return pl.BlockSpec((tm, wd), lambda i: (i, j))

    def full(a, b):
        return pl.BlockSpec((a, b), lambda i: (0, 0))

    return _pc(
        body, name, grid=(n // tm,),
        in_specs=[row(d), row(d), full(1, d), full(d, d), row(FOX_W, C_GC // FOX_W), row(FOX_W)],
        out_specs=[row(d), row(wab), full(1, d), row(FOX_W), pl.BlockSpec((FOX_W, tm), lambda i: (0, i)), row(FOX_W),
                   pl.BlockSpec((None, FOX_HEADS, tm), lambda i: (i // nt, 0, i % nt))],
        out_shape=[_sds((n, d), BF16), _sds((n, wab), F32), _sds((1, d), F32), _sds((n, FOX_W), BF16),
                   _sds((FOX_W, n), BF16), _sds((n, FOX_W), BF16), _sds((nb, FOX_HEADS, t), F32)],
        compiler_params=_params(),
    )(dxo, y, g_post, w_out_t, proj, o_c)


def _w_out_grad(mixt, dy, name):
    d, n = mixt.shape
    rows = d // N_DEV

    def body(a_ref, b_ref, o_ref):
        o_ref[...] = _dot(a_ref[...], b_ref[...]).astype(BF16)

    return _pc(
        body, name, grid=(N_DEV,),
        in_specs=[pl.BlockSpec((rows, n), lambda j: (j, 0)), pl.BlockSpec((n, d), lambda j: (0, 0))],
        out_specs=pl.BlockSpec((None, None, rows, d), lambda j: (j % 2, j // 2, 0, 0)),
        out_shape=_sds((2, N_DEV // 2, rows, d), BF16),
        compiler_params=_params(),
    )(mixt, dy)


def _w_in_grad(ht, pieces, name):
    d, n = ht.shape
    ta, tk = min(512, d), min(512, n)
    nk = n // tk
    arrays = [p for p, _ in pieces]
    widths = [p.shape[1] for p in arrays]
    offs = [sum(widths[:i]) for i in range(len(widths))]
    in_w = MAIN_W + FOX_HEADS
    shard = in_w // N_DEV

    def body(*refs):
        a_ref, p_refs = refs[0], refs[1:1 + len(arrays)]
        o_ref, acc = refs[1 + len(arrays):]
        k = pl.program_id(1)

        @pl.when(k == 0)
        def _():
            acc[...] = jnp.zeros_like(acc)

        a = a_ref[...]
        for pr, off, wd in zip(p_refs, offs, widths):
            for j in range(0, wd, 512):
                jw = min(512, wd - j)
                acc[:, off + j:off + j + jw] += _dot(a, pr[:, j:j + jw])

        @pl.when(k == nk - 1)
        def _():
            for j in range(N_DEV):
                o_ref[j % 2, j // 2] = acc[:, j * shard:(j + 1) * shard].astype(BF16)

    return _pc(
        body, name, grid=(d // ta, nk),
        in_specs=[pl.BlockSpec((ta, tk), lambda i, k: (i, k))] + [pl.BlockSpec((tk, wd), lambda i, k: (k, 0)) for wd in widths],
        out_specs=pl.BlockSpec((2, N_DEV // 2, ta, shard), lambda i, k: (0, 0, i, 0)),
        out_shape=_sds((2, N_DEV // 2, d, shard), BF16),
        scratch_shapes=[pltpu.VMEM((ta, sum(widths)), F32)],
        compiler_params=_params(),
    )(ht, *arrays)


def _hgrn_state_bwd(qs, k, v, b, do, s0, ds1, bmask):
    bl = b[SUB - 1:SUB, :]
    eb, ebl, ekt = jnp.exp(b), jnp.exp(bl), jnp.exp(bl - b)
    qe, ktil = qs * eb, k * ekt
    ds1b, dob = ds1.astype(BF16), do.astype(BF16)
    dv = _dot_nt(ktil.astype(BF16), ds1b)
    dqe = _dot(dob, s0.astype(BF16))
    dktil = _dot(v.astype(BF16), ds1b)
    dbl = jnp.sum(dktil * ktil, axis=0, keepdims=True) + ebl * jnp.sum(s0 * ds1, axis=0, keepdims=True)
    ds0 = ds1 * ebl + _dot_tn(dob, qe.astype(BF16)) * bmask
    return dqe * eb, dktil * ekt, dv, dbl, ds0


def _hgrn_intra_bwd(qs, k, v, do, es, aexp, gexp, dq, dk, dv, put_dq_row):
    dks = [dk[j:j + 8] for j in range(0, SUB, 8)]
    dvs = [dv[j:j + 8] for j in range(0, SUB, 8)]
    for t in range(SUB):
        r = _live_rows(t)
        ge = gexp[t * SUB:t * SUB + r, :] * es[t]
        put_dq_row(t, dq[t:t + 1, :] + jnp.sum(ge * k[:r], axis=0, keepdims=True))
        for j in range(r // 8):
            dks[j] = dks[j] + ge[8 * j:8 * j + 8] * qs[t:t + 1, :]
            dvs[j] = dvs[j] + aexp[t * SUB + 8 * j:t * SUB + 8 * j + 8, :] * do[t:t + 1, :]
    return jnp.concatenate(dks, axis=0), jnp.concatenate(dvs, axis=0)


def _hgrn_bwd(dmix, proj, o_h, s0, gh, lb, ones_b, nb, name):
    n = proj.shape[0]
    t = n // nb
    tt = _hgrn_tile(t)
    nt = t // tt
    ncs = tt // CHUNK
    nsub = CHUNK // SUB
    w = HGRN_W

    def body(dm_ref, q_ref, z_ref, v_ref, ga_ref, oh_ref, s0_ref, gh_ref, lb_ref, ones_ref,
             dp_ref, dgh_ref, dlb_ref, ds_s, ss_s, b_s, qs_s, k_s, do_s, dq_s, dk_s, dv_s, dbl_s):
        first = jnp.logical_and(pl.program_id(0) == 0, pl.program_id(1) == 0)

        @pl.when(first)
        def _():
            dgh_ref[...] = jnp.zeros_like(dgh_ref)
            dlb_ref[...] = jnp.zeros_like(dlb_ref)

        @pl.when(pl.program_id(1) == 0)
        def _():
            ds_s[...] = jnp.zeros_like(ds_s)

        ones_b = ones_ref[...]
        ones_f = ones_b.astype(F32)
        bmask = _block_ones(w, F32)
        lbv, ghv = lb_ref[...], gh_ref[...]
        oh, ga, dm = oh_ref[...], ga_ref[...], dm_ref[...]
        rn = lax.rsqrt(_head_mean(oh * oh, ones_f) + NORM_EPS)
        nh = oh * rn
        sga = _sigmoid(ga)
        dp_ref[:, 3 * w:4 * w] = (dm * nh * ghv * (sga * (1.0 + ga * (1.0 - sga)))).astype(BF16)
        dn = dm * (ga * sga)
        dgh_ref[...] += jnp.sum(dn * nh, axis=0, keepdims=True)
        dn = dn * ghv
        do_s[...] = rn * (dn - nh * _head_mean(dn * nh, ones_f))
        q = q_ref[...]
        sig, sn, f, g, k, sq = _hgrn_gates(q, z_ref[...], lbv)
        qs = q * sq
        b_s[...] = _sel_dot(_sub_tri(tt, True), g)
        qs_s[...] = qs
        k_s[...] = k

        def chunk(cc, carry):
            c = ncs - 1 - cc
            base = pl.multiple_of(c * CHUNK, CHUNK)
            tiles = []
            for u in range(nsub):
                rows = pl.ds(base + u * SUB, SUB)
                tiles.append((qs_s[rows, :], k_s[rows, :], v_ref[rows, :], b_s[rows, :], do_s[rows, :]))
            st = s0_ref[c]
            for u, (qs, k, v, b, do) in enumerate(tiles):
                ss_s[u] = st
                if u < nsub - 1:
                    st = _hgrn_state_step(st, k, v, b, bmask)
            ds = ds_s[...]
            for u in reversed(range(nsub)):
                qs, k, v, b, do = tiles[u]
                _, es, ws = _hgrn_decays(qs, k, b)
                gs = [_pad_rows(do[t:t + 1, :] * v[:_live_rows(t)]) for t in range(SUB)]
                aexp = _dot(jnp.concatenate(ws, axis=0).astype(BF16), ones_b)
                gexp = _dot(jnp.concatenate(gs, axis=0).astype(BF16), ones_b)
                dq, dk, dv, dbl, ds = _hgrn_state_bwd(qs, k, v, b, do, ss_s[u], ds, bmask)

                def put_dq_row(i, row, r0=base + u * SUB):
                    dq_s[pl.ds(r0 + i, 1), :] = row

                dk, dv = _hgrn_intra_bwd(qs, k, v, do, es, aexp, gexp, dq, dk, dv, put_dq_row)
                dk_s[pl.ds(base + u * SUB, SUB), :] = dk
                dv_s[pl.ds(base + u * SUB, SUB), :] = dv
                dbl_s[pl.ds(base + u * SUB, SUB), :] = jnp.broadcast_to(dbl, (SUB, w))
            ds_s[...] = ds
            return carry

        lax.fori_loop(0, ncs, chunk, 0)
        dqs, dk = dq_s[...], dk_s[...]
        dg = _sel_dot(_sub_tri(tt, False), qs * dqs - k * dk) + dbl_s[...]
        dfz = jnp.where(f > TINY, dg / jnp.maximum(f, TINY), 0.0)
        dlb_ref[...] += jnp.sum(dfz * (1.0 - sig) - dk * sn, axis=0, keepdims=True)
        dp_ref[:, 0:w] = (dqs * (sq * (1.0 + q * (1.0 - sq)))).astype(BF16)
        dp_ref[:, w:2 * w] = ((dfz - dk) * (1.0 - lbv) * sig * sn).astype(BF16)
        dp_ref[:, 2 * w:3 * w] = dv_s[...].astype(BF16)

    def rv(b, i):
        return b * nt + (nt - 1 - i)

    def col(j):
        return pl.BlockSpec((tt, w), lambda b, i: (rv(b, i), j))

    def full(a, bb):
        return pl.BlockSpec((a, bb), lambda b, i: (0, 0))

    return _pc(
        body, name, grid=(nb, nt),
        in_specs=[col(0), col(C_QA // w), col(C_FA // w), col(C_IA // w), col(C_GA // w), col(0),
                  pl.BlockSpec((ncs, w, w), lambda b, i: (rv(b, i), 0, 0)), full(1, w), full(1, w), full(w, w)],
        out_specs=[pl.BlockSpec((tt, 4 * w), lambda b, i: (rv(b, i), 0)), full(1, w), full(1, w)],
        out_shape=[_sds((n, 4 * w), BF16), _sds((1, w), F32), _sds((1, w), F32)],
        scratch_shapes=[pltpu.VMEM((w, w), F32), pltpu.VMEM((nsub, w, w), F32)] + [pltpu.VMEM((tt, w), F32)] * 8,
        compiler_params=_params(),
    )(dmix, proj, proj, proj, proj, o_h, s0, gh, lb, ones_b)


def _pool_bwd(dmix, proj, wbd, wbd_t, scale, nb, name):
    n = proj.shape[0]
    t = n // nb
    tt = min(512, t)
    nt = t // tt
    nhb = tt // POOL_HALO
    cu, cg, cm = C_UB // POOL_W, C_GB // POOL_W, HGRN_W // POOL_W

    def body(u_ref, g_ref, h_ref, dm_ref, gn_ref, dmn_ref, w_ref, wt_ref, s_ref, dp_ref, dw_ref, ds_ref):
        i = pl.program_id(1)
        first = jnp.logical_and(pl.program_id(0) == 0, i == 0)

        @pl.when(first)
        def _():
            dw_ref[...] = jnp.zeros_like(dw_ref)
            ds_ref[...] = jnp.zeros_like(ds_ref)

        sc = s_ref[...]
        halo = jnp.where(i == 0, 0.0, h_ref[...])
        pooled, cnt = _pool_mix(u_ref[...], halo, i * tt, tt)
        pb = pooled.astype(BF16)
        pre = _dot(pb, w_ref[...])
        gv, dm = g_ref[...], dm_ref[...]
        sg = _sigmoid(gv)
        silu = gv * sg
        dgb = dm * pre * sc * (sg * (1.0 + gv * (1.0 - sg)))
        ds_ref[...] += jnp.sum(dm * pre * silu, axis=0, keepdims=True)
        dpre = (dm * sc * silu).astype(BF16)
        dw_ref[...] += _dot_tn(pb, dpre)
        dpool = _dot(dpre, wt_ref[...])
        gn = gn_ref[...]
        dpre_n = (dmn_ref[...] * sc * (gn * _sigmoid(gn))).astype(BF16)
        dpool_n = jnp.where(i == nt - 1, 0.0, _dot(dpre_n, wt_ref[...]))
        lane, wl = _pool_lane_windows()
        tpos_n = ((i + 1) * tt + lax.broadcasted_iota(jnp.int32, (POOL_HALO, POOL_W), 0)).astype(F32)
        ext = jnp.concatenate([dpool / cnt, dpool_n / jnp.minimum(tpos_n + 1.0, wl)], axis=0)
        rows = tt + POOL_HALO
        sums, cur, shift = [], ext, 1
        for _ in POOL_WINDOWS:
            cur = cur + pltpu.roll(cur, rows - shift, axis=0)
            sums.append(cur[:tt, :])
            shift *= 2
        du = _pool_select(lane, sums) - dpool
        dp_ref[...] = jnp.concatenate([du, dgb], axis=1).astype(BF16)

    def nxt(b, i):
        return jnp.minimum((b * nt + i + 1) * nhb, n // POOL_HALO - 1)

    return _pc(
        body, name, grid=(nb, nt),
        in_specs=_pool_specs(tt, nt, nhb) + [
            pl.BlockSpec((tt, POOL_W), lambda b, i: (b * nt + i, cm)),
            pl.BlockSpec((POOL_HALO, POOL_W), lambda b, i: (nxt(b, i), cg)),
            pl.BlockSpec((POOL_HALO, POOL_W), lambda b, i: (nxt(b, i), cm)),
            pl.BlockSpec((POOL_W, POOL_W), lambda b, i: (0, 0)), pl.BlockSpec((POOL_W, POOL_W), lambda b, i: (0, 0)),
            pl.BlockSpec((1, POOL_W), lambda b, i: (0, 0))],
        out_specs=[pl.BlockSpec((tt, 2 * POOL_W), lambda b, i: (b * nt + i, 0)),
                   pl.BlockSpec((POOL_W, POOL_W), lambda b, i: (0, 0)), pl.BlockSpec((1, POOL_W), lambda b, i: (0, 0))],
        out_shape=[_sds((n, 2 * POOL_W), BF16), _sds((POOL_W, POOL_W), F32), _sds((1, POOL_W), F32)],
        compiler_params=_params(),
    )(proj, proj, proj, dmix, proj, dmix, wbd, wbd_t, scale)


def _fox_bwd(proj, qt, kt, da, dat, c_col, c_row, lse_row, delta_row, nb, name, rider=None):
    n = proj.shape[0]
    t = n // nb
    tb = min(256, t)
    nq = t // tb
    pw = 2 * HEAD

    def body(q_ref, k_ref, v_ref, da_ref, qt_ref, kt_ref, dat_ref, cc_ref, cr_ref, lse_ref, dl_ref,
             dq_ref, dk_ref, dv_ref, dck_ref, dcq_ref, dq_s, dk_s, dv_s, dck_s, dcq_s):
        step = pl.program_id(1)
        kj, qi = pairs(step)

        @pl.when(step == 0)
        def _():
            dq_s[...] = jnp.zeros_like(dq_s)
            dcq_s[...] = jnp.zeros_like(dcq_s)

        @pl.when(qi == nq - 1)
        def _():
            dk_s[...] = jnp.zeros_like(dk_s)
            dv_s[...] = jnp.zeros_like(dv_s)
            dck_s[...] = jnp.zeros_like(dck_s)

        def block(masked):
            lo = _lane_lo()
            if masked:
                causal = lax.broadcasted_iota(jnp.int32, (tb, tb), 1) >= lax.broadcasted_iota(jnp.int32, (tb, tb), 0)
            dck = dck_s[...]
            for p in range(FOX_HEADS // 2):
                sl = slice(p * pw, (p + 1) * pw)
                qp = q_ref[:, sl] * FOX_SCALE
                kp = k_ref[:, sl].astype(BF16)
                vp = v_ref[:, sl].astype(BF16)
                dap = da_ref[:, sl]
                dk, dv = dk_s[:, sl], dv_s[:, sl]
                for h in range(2):
                    hh = 2 * p + h
                    lm = lo if h == 0 else jnp.logical_not(lo)
                    rows = slice(hh * HEAD, (hh + 1) * HEAD)
                    none = jnp.zeros((HEAD, tb), BF16)
                    qm = jnp.where(lm, qp, 0.0).astype(BF16)
                    dam = jnp.where(lm, dap, jnp.zeros_like(dap))
                    qtm = jnp.concatenate([qt_ref[rows, :], none] if h == 0 else [none, qt_ref[rows, :]], axis=0)
                    datm = jnp.concatenate([dat_ref[rows, :], none] if h == 0 else [none, dat_ref[rows, :]], axis=0)
                    s = _dot(kp, qtm) + (cr_ref[hh:hh + 1, :] - cc_ref[:, hh:hh + 1])
                    pe = jnp.exp(s - lse_ref[hh:hh + 1, :])
                    if masked:
                        pe = jnp.where(causal, pe, 0.0)
                    dp = _dot(vp, datm)
                    ds = pe * (dp - dl_ref[hh:hh + 1, :])
                    dsb = ds.astype(BF16)
                    dv = dv + _dot(pe.astype(BF16), dam)
                    dk = dk + _dot(dsb, qm)
                    dq_s[qi, rows, :] += _dot(kt_ref[rows, :], dsb)
                    dck = dck - _put_col(jnp.zeros_like(dck), hh, jnp.sum(ds, axis=1, keepdims=True))
                    dcq_s[qi, hh:hh + 1, :] += _rows_reduce(ds, jnp.add, jnp.sum)
                dk_s[:, sl] = dk
                dv_s[:, sl] = dv
            dck_s[...] = dck

        @pl.when(qi > kj)
        def _():
            block(False)

        @pl.when(qi == kj)
        def _():
            block(True)

        @pl.when(qi == kj)
        def _():
            dk_ref[...] = dk_s[...].astype(BF16)
            dv_ref[...] = dv_s[...].astype(BF16)
            dck_ref[...] = dck_s[...]

        @pl.when(step == nq * (nq + 1) // 2 - 1)
        def _():
            for j in range(nq):
                dq_ref[j * tb:(j + 1) * tb, :] = (dq_s[j].T * FOX_SCALE).astype(BF16)
                dcq_ref[:, j * tb:(j + 1) * tb] = dcq_s[j]

    def pairs(step):
        a, b = _tri_pair(step, nq)
        return nq - 1 - a, nq - 1 - b

    def kspec(wd, j=0):
        return pl.BlockSpec((tb, wd), lambda b, st: (b * nq + pairs(st)[0], j))

    def qspec(wd, j=0):
        return pl.BlockSpec((tb, wd), lambda b, st: (b * nq + pairs(st)[1], j))

    def qrow():
        return pl.BlockSpec((None, FOX_HEADS, tb), lambda b, st: (b, 0, pairs(st)[1]))

    def tspec(which):
        return pl.BlockSpec((FOX_W, tb), lambda b, st: (0, b * nq + pairs(st)[which]))

    return _call(
        body, name, (proj, proj, proj, da, qt, kt, dat, c_col, c_row, lse_row, delta_row), rider,
        grid=(nb, nq * (nq + 1) // 2),
        in_specs=[qspec(FOX_W, C_QC // FOX_W), kspec(FOX_W, C_KC // FOX_W), kspec(FOX_W, C_VC // FOX_W), qspec(FOX_W),
                  tspec(1), tspec(0), tspec(1), kspec(FC_PAD), qrow(), qrow(), qrow()],
        out_specs=[pl.BlockSpec((t, FOX_W), lambda b, st: (b, 0)), kspec(FOX_W), kspec(FOX_W), kspec(FC_PAD),
                   pl.BlockSpec((None, FOX_HEADS, t), lambda b, st: (b, 0, 0))],
        out_shape=[_sds((n, FOX_W), BF16), _sds((n, FOX_W), BF16), _sds((n, FOX_W), BF16), _sds((n, FC_PAD), F32),
                   _sds((nb, FOX_HEADS, t), F32)],
        scratch_shapes=[pltpu.VMEM((nq, FOX_W, tb), F32), pltpu.VMEM((tb, FOX_W), F32), pltpu.VMEM((tb, FOX_W), F32),
                        pltpu.VMEM((tb, FC_PAD), F32), pltpu.VMEM((nq, FOX_HEADS, tb), F32)],
        compiler_params=_params(),
    )


def _fox_decay_bwd(dc_q, dc_k, fc, bias, nb, name):
    n = fc.shape[0]
    t = n // nb
    tt = min(256, t)
    nt = t // tt

    def body(dcq_ref, dck_ref, fc_ref, b_ref, dfc_ref, db_ref, carry):
        i = pl.program_id(1)
        first = jnp.logical_and(pl.program_id(0) == 0, i == 0)

        @pl.when(first)
        def _():
            db_ref[...] = jnp.zeros_like(db_ref)

        @pl.when(i == 0)
        def _():
            carry[...] = jnp.zeros_like(carry)

        r = lax.broadcasted_iota(jnp.int32, (tt, tt), 0)
        cc = lax.broadcasted_iota(jnp.int32, (tt, tt), 1)
        dcq = jnp.concatenate([dcq_ref[...], jnp.zeros((FC_PAD - FOX_HEADS, tt), F32)], axis=0).T
        dlf = _dot_hi((r <= cc).astype(F32), dcq + dck_ref[...]) + carry[...]
        carry[...] = dlf[0:1, :]
        dfc = dlf * _sigmoid(-(fc_ref[...] + b_ref[...]))
        dfc_ref[...] = dfc.astype(BF16)
        db_ref[...] += jnp.sum(dfc, axis=0, keepdims=True)

    def row():
        return pl.BlockSpec((tt, FC_PAD), lambda b, i: (b * nt + (nt - 1 - i), 0))

    return _pc(
        body, name, grid=(nb, nt),
        in_specs=[pl.BlockSpec((None, FOX_HEADS, tt), lambda b, i: (b, 0, nt - 1 - i)), row(), row(),
                  pl.BlockSpec((1, FC_PAD), lambda b, i: (0, 0))],
        out_specs=[row(), pl.BlockSpec((1, FC_PAD), lambda b, i: (0, 0))],
        out_shape=[_sds((n, FC_PAD), BF16), _sds((1, FC_PAD), F32)],
        scratch_shapes=[pltpu.VMEM((1, FC_PAD), F32)],
        compiler_params=_params(),
    )(dc_q, dc_k, fc, bias)


def _in_proj_bwd(pieces, w_main_t, w_fc_t, x, g_pre, dxo, name, rider=None):
    n, d = x.shape
    tm = min(512, n)
    widths = [p.shape[1] for p, _ in pieces]
    offs = [o for _, o in pieces]
    np_ = len(pieces)

    def body(*refs):
        p_refs = refs[:np_]
        wt_ref, wf_ref, x_ref, g_ref, dxo_ref, dx_ref, dg_ref = refs[np_:]

        @pl.when(pl.program_id(0) == 0)
        def _():
            dg_ref[...] = jnp.zeros_like(dg_ref)

        dh = _dot(p_refs[-1][...], wf_ref[...])
        for pr, wd, off in zip(p_refs[:-1], widths[:-1], offs[:-1]):
            for j in range(0, wd, 512):
                jw = min(512, wd - j)
                dh = dh + _dot(pr[:, j:j + jw], wt_ref[off + j:off + j + jw, :])
        xv = x_ref[...]
        r = lax.rsqrt(jnp.mean(xv * xv, axis=-1, keepdims=True) + NORM_EPS)
        xh = xv * r
        dg_ref[...] += jnp.sum(dh * xh, axis=0, keepdims=True)
        dx_ref[...] = dxo_ref[...] + _rms_bwd(dh * g_ref[...], xh, r)

    row = pl.BlockSpec((tm, d), lambda i: (i, 0))
    return _call(
        body, name, (*[p for p, _ in pieces], w_main_t, w_fc_t, x, g_pre, dxo), rider, grid=(n // tm,),
        in_specs=[pl.BlockSpec((tm, wd), lambda i: (i, 0)) for wd in widths] + [
            pl.BlockSpec((MAIN_W, d), lambda i: (0, 0)), pl.BlockSpec((FC_PAD, d), lambda i: (0, 0)),
            row, pl.BlockSpec((1, d), lambda i: (0, 0)), row],
        out_specs=[row, pl.BlockSpec((1, d), lambda i: (0, 0))],
        out_shape=[_sds((n, d), F32), _sds((1, d), F32)],
        compiler_params=_params(),
    )


def _lower_bound_table(lower_bounds, name):
    depth, w = lower_bounds.shape

    def body(lb_ref, o_ref):
        v = lb_ref[...]
        e = jnp.exp(v - jnp.max(v, axis=0, keepdims=True))
        p = e / jnp.sum(e, axis=0, keepdims=True)
        acc = jnp.zeros((1, w), F32)
        for l in range(depth):
            acc = acc + p[l:l + 1, :]
            o_ref[l:l + 1, :] = acc - p[0:1, :]

    return _pc(body, name, out_shape=_sds((depth, w), F32))(lower_bounds)


def _lower_bound_bwd(lower_bounds, dlbs, name):
    depth, w = lower_bounds.shape

    def body(lb_ref, d_ref, o_ref):
        v, dl = lb_ref[...], d_ref[...]
        e = jnp.exp(v - jnp.max(v, axis=0, keepdims=True))
        p = e / jnp.sum(e, axis=0, keepdims=True)
        tot = jnp.sum(dl, axis=0, keepdims=True)
        rows, tail = [], tot
        for l in range(depth):
            rows.append(tail - tot if l == 0 else tail)
            tail = tail - dl[l:l + 1, :]
        dp = jnp.concatenate(rows, axis=0)
        o_ref[...] = p * (dp - jnp.sum(p * dp, axis=0, keepdims=True))

    return _pc(body, name, out_shape=_sds((depth, w), F32))(lower_bounds, dlbs)


def _place():
    x, y, c = lax.axis_index("x"), lax.axis_index("y"), lax.axis_index("c")
    return x, y, c


def _gather_weights(*arrays):
    na = len(arrays)

    def body(*refs):
        ins, outs = refs[:na], refs[na:2 * na]
        send_sems, recv_sems, local_sems = refs[2 * na:]
        x, y, c = _place()
        me, sibling = (x, y, c), (x, y, 1 - c)
        chips = [(1 - x, y), (x, 1 - y), (1 - x, 1 - y)]

        def slot(a, px, py, pc):
            return outs[a].at[4 * px + 2 * py + pc]

        def copy(a, k, block, to, own=False):
            return pltpu.make_async_remote_copy(
                src_ref=ins[a] if own else slot(a, *block), dst_ref=slot(a, *block),
                send_sem=send_sems.at[a * 7 + k], recv_sem=recv_sems.at[a * 7 + k],
                device_id=to, device_id_type=MESH)

        mine = [pltpu.make_async_copy(ins[a], slot(a, *me), local_sems.at[a]) for a in range(na)]
        for cp in mine:
            cp.start()
        first = []
        for a in range(na):
            first.append(copy(a, 0, me, sibling, own=True))
            first += [copy(a, 1 + j, me, (*chip, c), own=True) for j, chip in enumerate(chips)]
        for cp in first:
            cp.start()
        passed = []
        for j, chip in enumerate(chips):
            for a in range(na):
                copy(a, 1 + j, (*chip, c), me).wait_recv()
                fw = copy(a, 4 + j, (*chip, c), sibling)
                fw.start()
                passed.append(fw)
        for a in range(na):
            copy(a, 0, sibling, me).wait_recv()
            for j, chip in enumerate(chips):
                copy(a, 4 + j, (*chip, 1 - c), me).wait_recv()
        for cp in first + passed:
            cp.wait_send()
        for cp in mine:
            cp.wait()

    any_spec = pl.BlockSpec(memory_space=pl.ANY)
    return _pc(
        body, "gather_weights",
        in_specs=[any_spec] * na, out_specs=[any_spec] * na,
        out_shape=[_sds((N_DEV,) + a.shape, a.dtype) for a in arrays],
        scratch_shapes=[pltpu.SemaphoreType.DMA((7 * na,)), pltpu.SemaphoreType.DMA((7 * na,)),
                        pltpu.SemaphoreType.DMA((na,))],
    )(*arrays)


def _peer(k):
    x, y, c = _place()
    return (1 - x if k & 4 else x, 1 - y if k & 2 else y, 1 - c if k & 1 else c)


def _remote(src, dst, sems, s, to):
    return pltpu.make_async_remote_copy(src_ref=src, dst_ref=dst, send_sem=sems[0].at[s], recv_sem=sems[1].at[s],
                                        device_id=to, device_id_type=MESH)


def _gather_rider(shards):
    na = len(shards)

    def plan(ins, outs, *sems):
        x, y, c = _place()
        me = 4 * x + 2 * y + c
        locs = [pltpu.make_async_copy(ins[a], outs[a].at[me], sems[2].at[a]) for a in range(na)]
        sends, recvs = [], []
        for k in range(1, N_DEV):
            px, py, pc = _peer(k)
            for a in range(na):
                s = (k - 1) * na + a
                sends.append(_remote(ins[a], outs[a].at[me], sems, s, (px, py, pc)))
                recvs.append(_remote(ins[a], outs[a].at[4 * px + 2 * py + pc], sems, s, (px, py, pc)))
        return sends, recvs, locs

    return _Rider(shards, [_sds((N_DEV,) + a.shape, a.dtype) for a in shards], (N_DEV - 1) * na, na, plan)


def _direct_exchange_rider(blocks):
    na = len(blocks)

    def plan(ins, outs, *sems):
        x, y, c = _place()
        me = 4 * x + 2 * y + c
        locs = [pltpu.make_async_copy(ins[a].at[c, 2 * x + y], outs[a].at[me], sems[2].at[a]) for a in range(na)]
        sends, recvs = [], []
        for k in range(1, N_DEV):
            px, py, pc = _peer(k)
            for a in range(na):
                s = (k - 1) * na + a
                sends.append(_remote(ins[a].at[pc, 2 * px + py], outs[a].at[me], sems, s, (px, py, pc)))
                recvs.append(_remote(ins[a].at[pc, 2 * px + py], outs[a].at[4 * px + 2 * py + pc], sems, s, (px, py, pc)))
        return sends, recvs, locs

    return _Rider(blocks, [_sds((N_DEV,) + a.shape[2:], a.dtype) for a in blocks], (N_DEV - 1) * na, na, plan)


def _swap_rider(halves):
    na = len(halves)

    def plan(ins, outs, *sems):
        x, y, c = _place()
        cps = [_remote(ins[a].at[1 - c], outs[a], sems, a, (x, y, 1 - c)) for a in range(na)]
        return cps, cps, []

    return _Rider(halves, [_sds(a.shape[1:], a.dtype) for a in halves], na, 1, plan)


def _chip_exchange_rider(parts, small=None):
    na = len(parts)
    n_chip = N_DEV // 2

    def plan(ins, outs, *sems):
        x, y, c = _place()
        chip = 2 * x + y
        locs = [pltpu.make_async_copy(ins[a].at[chip], outs[a].at[chip], sems[2].at[a]) for a in range(na)]
        sends, recvs = [], []
        for k in range(1, n_chip):
            px, py, _ = _peer(2 * k)
            for a in range(na):
                s = (k - 1) * na + a
                sends.append(_remote(ins[a].at[2 * px + py], outs[a].at[chip], sems, s, (px, py, c)))
                recvs.append(_remote(ins[a].at[2 * px + py], outs[a].at[2 * px + py], sems, s, (px, py, c)))
        if small is not None:
            me = 2 * chip + c
            locs.append(pltpu.make_async_copy(ins[na], outs[na].at[me], sems[2].at[na]))
            for k in range(1, N_DEV):
                px, py, pc = _peer(k)
                s = (n_chip - 1) * na + k - 1
                sends.append(_remote(ins[na], outs[na].at[me], sems, s, (px, py, pc)))
                recvs.append(_remote(ins[na], outs[na].at[4 * px + 2 * py + pc], sems, s, (px, py, pc)))
        return sends, recvs, locs

    extra = [] if small is None else [small]
    shapes = [_sds(a.shape, a.dtype) for a in parts] + [_sds((N_DEV,) + s.shape, s.dtype) for s in extra]
    n_sems = (n_chip - 1) * na + (N_DEV - 1) * len(extra)
    return _Rider(list(parts) + extra, shapes, n_sems, na + len(extra), plan)


def _pair_add(halves, other, core, name):
    _, nch, r, c = halves.shape

    def body(c_ref, h_ref, o_ref, p_ref):
        p_ref[...] = (h_ref[...].astype(F32) + o_ref[...].astype(F32)).astype(BF16)

    blk = pl.BlockSpec((None, r, c), lambda j, c_ref: (j, 0, 0))
    return _pc(
        body, name,
        grid_spec=pltpu.PrefetchScalarGridSpec(
            num_scalar_prefetch=1, grid=(nch,),
            in_specs=[pl.BlockSpec((None, None, r, c), lambda j, c_ref: (c_ref[0], j, 0, 0)), blk], out_specs=blk),
        out_shape=_sds((nch, r, c), BF16),
        compiler_params=_params(),
    )(core, halves, other)


def _sum_adamw(parts, w, m, v, name, rider=None):
    nl, r, c = w.shape
    tr = 256 if r % 256 == 0 else r

    def body(*refs):
        p_refs = refs[:nl]
        w_ref, m_ref, v_ref, g_ref, d_ref, mo_ref, vo_ref = refs[nl:]
        for l in range(nl):
            @pl.when(pl.program_id(0) == l)
            def _(p_ref=p_refs[l]):
                g = p_ref[0].astype(F32)
                for j in range(1, p_ref.shape[0]):
                    g = g + p_ref[j].astype(F32)
                mn = ADAM_B1 * m_ref[...] + (1.0 - ADAM_B1) * g
                vn = ADAM_B2 * v_ref[...] + (1.0 - ADAM_B2) * (g * g)
                m_hat = mn / (1.0 - ADAM_B1 ** ADAM_STEP)
                v_hat = vn / (1.0 - ADAM_B2 ** ADAM_STEP)
                g_ref[...] = g
                d_ref[...] = -ADAM_LR * (m_hat / (jnp.sqrt(v_hat) + ADAM_EPS) + ADAM_WD * w_ref[...])
                mo_ref[...] = mn
                vo_ref[...] = vn

    def part_spec(l, k):
        return pl.BlockSpec((k, tr, c), lambda li, i: (0, jnp.where(li == l, i, 0), 0))

    row = pl.BlockSpec((None, tr, c), lambda li, i: (li, i, 0))
    return _call(
        body, name, (*parts, w, m, v), rider, grid=(nl, r // tr),
        in_specs=[part_spec(l, p.shape[0]) for l, p in enumerate(parts)] + [row, row, row],
        out_specs=[row] * 4,
        out_shape=[_sds((nl, r, c), F32)] * 4,
        compiler_params=_params(),
    )


SMALL = ("lower_bounds", "pre_norm_g", "hgrn_norm_g", "pool_w", "pool_scale", "post_norm_g", "fox_f_bias")
SMALL_LANES = 128


def _small_size(tree):
    return sum(tree[k].size for k in SMALL)


def _pack_small(tree, extra=None):
    flat = jnp.concatenate([tree[k].reshape(-1) for k in SMALL] + ([] if extra is None else [extra.reshape(1)]))
    rows = -(-(_small_size(tree) + 1) // (8 * SMALL_LANES)) * 8
    return jnp.pad(flat, (0, rows * SMALL_LANES - flat.shape[0])).reshape(rows, SMALL_LANES)


def _unpack_small(packed, like):
    out, off = {}, 0
    for k in SMALL:
        size = like[k].size
        assert off % SMALL_LANES == 0
        rows = packed[off // SMALL_LANES:-(-(off + size) // SMALL_LANES)]
        out[k] = rows.reshape(-1)[:size].reshape(like[k].shape)
        off += size
    return out


def _block_diag(pw):
    g = pw.shape[0]
    eye = jnp.eye(g, dtype=pw.dtype)
    return (eye[:, None, :, None] * pw[:, :, None, :]).reshape(g * HEAD, g * HEAD)


def _assemble_w_in(g_in, name):
    _, d, shard = g_in.shape
    tr = min(256, d)
    wide = MAIN_W + FC_PAD

    def body(g_ref, wm_ref, wf_ref, wmt_ref, wft_ref, row_s):
        row_s[:, MAIN_W:] = jnp.zeros((tr, FC_PAD), F32)
        for j in range(N_DEV):
            row_s[:, j * shard:(j + 1) * shard] = g_ref[j].astype(F32)
        wm_ref[...] = row_s[:, :MAIN_W].astype(BF16)
        wf_ref[...] = row_s[:, MAIN_W:].astype(BF16)
        for j in range(0, MAIN_W, 512):
            wmt_ref[j:j + 512, :] = row_s[:, j:j + 512].T.astype(BF16)
        wft_ref[...] = row_s[:, MAIN_W:].T.astype(BF16)

    return _pc(
        body, name, grid=(d // tr,),
        in_specs=[pl.BlockSpec((N_DEV, tr, shard), lambda i: (0, i, 0))],
        out_specs=[pl.BlockSpec((tr, MAIN_W), lambda i: (i, 0)), pl.BlockSpec((tr, FC_PAD), lambda i: (i, 0)),
                   pl.BlockSpec((MAIN_W, tr), lambda i: (0, i)), pl.BlockSpec((FC_PAD, tr), lambda i: (0, i))],
        out_shape=[_sds((d, MAIN_W), BF16), _sds((d, FC_PAD), BF16), _sds((MAIN_W, d), BF16), _sds((FC_PAD, d), BF16)],
        scratch_shapes=[pltpu.VMEM((tr, wide), F32)],
        compiler_params=_params(),
    )(g_in)


def _w_out_parts(g_out):
    full_out = g_out.reshape(N_DEV * g_out.shape[1], g_out.shape[2])
    return full_out, full_out.T


def _layer_fwd(l, x, lbs, weights, lw, nb, rider_h=None, rider_c=None, target=None):
    n = x.shape[0]
    t = n // nb
    w_main, w_fc, _, _, w_out, _ = lw
    bias = jnp.pad(weights["fox_f_bias"][l:l + 1], ((0, 0), (0, FC_PAD - FOX_HEADS)))
    wbd = _block_diag(weights["pool_w"][l]).astype(BF16)
    proj, fc, ht, qt, kt = _in_proj_fwd(x, weights["pre_norm_g"][l:l + 1], w_main, w_fc, f"in_proj_fwd_{l}")
    c_col, c_row = _fox_decay_fwd(fc, bias, nb, f"fox_decay_fwd_{l}")
    (o_h, s0), rode_h = _hgrn_fwd(proj, lbs[l:l + 1], _block_ones(HGRN_W, BF16), nb, f"hgrn_fwd_{l}", rider_h)
    o_b = _pool_fwd(proj, wbd, weights["pool_scale"][l:l + 1], nb, f"pool_fwd_{l}")
    (o_c, lse), rode_c = _fox_fwd(proj, kt, c_col, c_row, nb, f"fox_fwd_{l}", rider_c)
    if w_out is None:
        lw = tuple(lw[:4]) + _w_out_parts(rode_h[0])
        w_out, rode_h = lw[4], rode_h[1:]
    x_next, mixt, y = _merge_fwd(x, proj, o_h, o_b, o_c, weights["hgrn_norm_g"][l:l + 1], w_out,
                                 weights["post_norm_g"][l:l + 1], f"merge_fwd_{l}", target)
    return x_next, (x, proj, fc, ht, qt, kt, c_col, c_row, o_h, s0, o_c, lse, mixt, y, bias, wbd), lw, (rode_h, rode_c)


def _layer_bwd(l, dx, saved, lbs, weights, lw, nb, rider=None):
    _, proj, fc, ht, qt, kt, c_col, c_row, o_h, s0, o_c, lse, mixt, y, bias, wbd = saved
    n = proj.shape[0]
    t = n // nb
    w_out_t = lw[5]
    g = {}
    dy, dmix, dgp, da, dat, d_gc, delta = _merge_bwd(dx, y, weights["post_norm_g"][l:l + 1], w_out_t, proj, o_c,
                                                nb, f"merge_bwd_{l}")
    g["post_norm_g"] = dgp[0]
    g["w_out"] = _w_out_grad(mixt, dy, f"w_out_grad_{l}")
    d_a, dgh, dlb = _hgrn_bwd(dmix, proj, o_h, s0, weights["hgrn_norm_g"][l:l + 1], lbs[l:l + 1],
                              _block_ones(HGRN_W, BF16), nb, f"hgrn_bwd_{l}")
    g["hgrn_norm_g"], g["lbs"] = dgh[0], dlb[0]
    d_b, dwbd, dps = _pool_bwd(dmix, proj, wbd, wbd.T, weights["pool_scale"][l:l + 1], nb, f"pool_bwd_{l}")
    g["pool_w"] = jnp.stack([dwbd[j * HEAD:(j + 1) * HEAD, j * HEAD:(j + 1) * HEAD] for j in range(len(POOL_WINDOWS))])
    g["pool_scale"] = dps[0]
    (d_qc, d_kc, d_vc, dc_k, dc_q), rode = _fox_bwd(proj, qt, kt, da, dat, c_col, c_row, lse, delta, nb,
                                                    f"fox_bwd_{l}", rider)
    d_fc, dbias = _fox_decay_bwd(dc_q, dc_k, fc, bias, nb, f"fox_decay_bwd_{l}")
    g["fox_f_bias"] = dbias[0, :FOX_HEADS]
    pieces = [(d_a, C_QA), (d_b, C_UB), (d_qc, C_QC), (d_kc, C_KC), (d_vc, C_VC), (d_gc, C_GC), (d_fc, None)]
    g["w_in"] = _w_in_grad(ht, pieces, f"w_in_grad_{l}")
    return g, pieces, rode


def _layer_bwd_input(l, dx, pieces, saved, weights, lw, rider=None):
    (dxi, dgpre), rode = _in_proj_bwd(pieces, lw[2], lw[3], saved[0], weights["pre_norm_g"][l:l + 1], dx,
                                      f"in_proj_bwd_{l}", rider)
    return dxi, dgpre[0], rode


def kernel(x, lower_bounds, pre_norm_g, w_in, hgrn_norm_g, fox_f_bias, pool_w, pool_scale, w_out, post_norm_g, loss_target, m_lower_bounds, m_pre_norm_g, m_w_in, m_hgrn_norm_g, m_fox_f_bias, m_pool_w, m_pool_scale, m_w_out, m_post_norm_g, v_lower_bounds, v_pre_norm_g, v_w_in, v_hgrn_norm_g, v_fox_f_bias, v_pool_w, v_pool_scale, v_w_out, v_post_norm_g):
    weights = dict(lower_bounds=lower_bounds, pre_norm_g=pre_norm_g, hgrn_norm_g=hgrn_norm_g, fox_f_bias=fox_f_bias,
                   pool_w=pool_w, pool_scale=pool_scale, post_norm_g=post_norm_g)
    mom_m = dict(lower_bounds=m_lower_bounds, pre_norm_g=m_pre_norm_g, hgrn_norm_g=m_hgrn_norm_g, fox_f_bias=m_fox_f_bias,
                 pool_w=m_pool_w, pool_scale=m_pool_scale, post_norm_g=m_post_norm_g)
    mom_v = dict(lower_bounds=v_lower_bounds, pre_norm_g=v_pre_norm_g, hgrn_norm_g=v_hgrn_norm_g, fox_f_bias=v_fox_f_bias,
                 pool_w=v_pool_w, pool_scale=v_pool_scale, post_norm_g=v_post_norm_g)
    depth = w_in.shape[0]
    nb, t, d = x.shape
    n = nb * t
    core = lax.axis_index("c").astype(jnp.int32).reshape(1)
    shards = [(w_in[l].astype(BF16), w_out[l].astype(BF16)) for l in range(depth)]
    lbs = _lower_bound_table(lower_bounds, "lower_bound_table")

    (g_in,) = _gather_weights(shards[0][0])
    coming = tuple(_assemble_w_in(g_in, "assemble_w_in_0")) + (None, None)
    xl, saved, lw = x.reshape(n, d), [], []
    for l in range(depth):
        last = l + 1 == depth
        ride_h = ([shards[l][1]] if coming[4] is None else []) + ([] if last else [shards[l + 1][1]])
        xl, sv, lw_l, (rode_h, rode_c) = _layer_fwd(
            l, xl, lbs, weights, coming, nb, _gather_rider(ride_h) if ride_h else None,
            None if last else _gather_rider([shards[l + 1][0]]), loss_target.reshape(n, d) if last else None)
        saved.append(sv)
        lw.append(lw_l)
        if not last:
            coming = tuple(_assemble_w_in(rode_c[0], f"assemble_w_in_{l + 1}")) + _w_out_parts(rode_h[0])
    dx, sq = xl
    loss_here = 0.5 * jnp.sum(sq) / d

    grads, recv, pending = [None] * depth, [None] * depth, None
    for l in reversed(range(depth)):
        g, pieces, rode = _layer_bwd(l, dx, saved[l], lbs, weights, lw[l], nb, pending)
        if rode is not None:
            recv[l + 1] = rode
        blocks = (g["w_in"], g["w_out"])
        if l > 0:
            pending = _direct_exchange_rider(blocks)
            dx, g["pre_norm_g"], _ = _layer_bwd_input(l, dx, pieces, saved[l], weights, lw[l])
        else:
            other = _run_rider(_swap_rider(blocks), "grad_swap")
            summed = [_pair_add(hv, ot, core, f"grad_pair_add_{i}") for i, (hv, ot) in enumerate(zip(blocks, other))]
            dx, g["pre_norm_g"], recv[l] = _layer_bwd_input(l, dx, pieces, saved[l], weights, lw[l],
                                                            _chip_exchange_rider(summed))
        grads[l] = g
    small = {k: jnp.stack([grads[l][k] for l in range(depth)]) for k in SMALL if k != "lower_bounds"}
    small["lower_bounds"] = _lower_bound_bwd(lower_bounds, jnp.stack([grads[l]["lbs"] for l in range(depth)]),
                                             "lower_bound_bwd")
    (r_small,) = _run_rider(_gather_rider([_pack_small(small, loss_here)]), "small_grads_gather")

    res_in, _ = _sum_adamw([recv[l][0] for l in range(depth)], w_in, m_w_in, v_w_in, "adamw_w_in")
    res_out, _ = _sum_adamw([recv[l][1] for l in range(depth)], w_out, m_w_out, v_w_out, "adamw_w_out")
    res_small, _ = _sum_adamw([r_small], _pack_small(weights)[None], _pack_small(mom_m)[None], _pack_small(mom_v)[None],
                              "adamw_small")
    loss = res_small[0][0][_small_size(weights) // SMALL_LANES, _small_size(weights) % SMALL_LANES]

    names = ("lower_bounds", "pre_norm_g", "w_in", "hgrn_norm_g", "fox_f_bias", "pool_w", "pool_scale", "w_out", "post_norm_g")
    outs = [loss, dx.reshape(nb, t, d)]
    for i in range(4):
        full = dict(_unpack_small(res_small[i][0], weights), w_in=res_in[i], w_out=res_out[i])
        outs += [full[k] for k in names]
    return tuple(outs)
```

```python
import functools

import jax
import jax.numpy as jnp
from jax import lax
from jax.experimental import pallas as pl
from jax.experimental.pallas import tpu as pltpu

F32, BF16 = jnp.float32, jnp.bfloat16
HI = lax.Precision.HIGHEST
MESH = pl.DeviceIdType.MESH
N_DEV = 8

NORM_EPS = 1e-6
MASK_VALUE = -1e30
TINY = 1e-30
CHUNK = 64
SUB = 16
HGRN_W, POOL_W, FOX_W = 256, 256, 512
HEAD = 64
FOX_HEADS = 8
POOL_WINDOWS = (2, 4, 8, 16)
POOL_HALO = 16
MAIN_W = 3584
FC_PAD = 128
C_QA, C_FA, C_IA, C_GA, C_UB, C_GB, C_QC, C_KC, C_VC, C_GC = 0, 256, 512, 768, 1024, 1280, 1536, 2048, 2560, 3072
FOX_SCALE = HEAD ** -0.5

ADAM_LR, ADAM_B1, ADAM_B2, ADAM_EPS, ADAM_WD, ADAM_STEP = 0.001, 0.9, 0.999, 1e-08, 0.01, 10

VMEM_LIMIT = 56 * 1024 * 1024


def _pc(fn, name, **kw):
    return pl.pallas_call(fn, name=name, **kw)


def _params(**kw):
    return pltpu.CompilerParams(vmem_limit_bytes=VMEM_LIMIT, **kw)


class _Rider:
    def __init__(self, inputs, out_shapes, n_sems, n_local, plan):
        self.inputs, self.out_shapes, self.n_sems, self.n_local, self.plan = list(inputs), list(out_shapes), n_sems, n_local, plan

    def start(self, ins, outs, *sems):
        sends, _, locs = self.plan(ins, outs, *sems)
        for cp in locs + sends:
            cp.start()

    def wait(self, ins, outs, *sems):
        sends, recvs, locs = self.plan(ins, outs, *sems)
        for cp in recvs:
            cp.wait_recv()
        for cp in sends:
            cp.wait_send()
        for cp in locs:
            cp.wait()

    def sem_shapes(self):
        return [pltpu.SemaphoreType.DMA((self.n_sems,)), pltpu.SemaphoreType.DMA((self.n_sems,)),
                pltpu.SemaphoreType.DMA((self.n_local,))]


def _call(body, name, args, rider=None, *, grid, in_specs, out_specs, out_shape, scratch_shapes=(), **kw):
    if rider is None:
        res = _pc(body, name, grid=grid, in_specs=in_specs, out_specs=out_specs, out_shape=out_shape,
                  scratch_shapes=list(scratch_shapes), **kw)(*args)
        return res, None
    n_in, n_out, n_scr = len(in_specs), len(out_specs), len(scratch_shapes)
    n_rin, n_rout = len(rider.inputs), len(rider.out_shapes)

    def ridden(*refs):
        ins, refs = refs[:n_in], refs[n_in:]
        rins, refs = refs[:n_rin], refs[n_rin:]
        outs, refs = refs[:n_out], refs[n_out:]
        routs, refs = refs[:n_rout], refs[n_rout:]
        scr, sems = refs[:n_scr], refs[n_scr:]
        first = functools.reduce(jnp.logical_and, [pl.program_id(a) == 0 for a in range(len(grid))])
        last = functools.reduce(jnp.logical_and, [pl.program_id(a) == g - 1 for a, g in enumerate(grid)])

        @pl.when(first)
        def _():
            rider.start(rins, routs, *sems)

        body(*ins, *outs, *scr)

        @pl.when(last)
        def _():
            rider.wait(rins, routs, *sems)

    any_spec = pl.BlockSpec(memory_space=pl.ANY)
    res = _pc(ridden, name, grid=grid, in_specs=list(in_specs) + [any_spec] * n_rin,
              out_specs=list(out_specs) + [any_spec] * n_rout, out_shape=list(out_shape) + rider.out_shapes,
              scratch_shapes=list(scratch_shapes) + rider.sem_shapes(), **kw)(*args, *rider.inputs)
    return res[:n_out], res[n_out:]


def _run_rider(rider, name):
    n_rin = len(rider.inputs)

    def body(*refs):
        ins, outs, sems = refs[:n_rin], refs[n_rin:n_rin + len(rider.out_shapes)], refs[n_rin + len(rider.out_shapes):]
        rider.start(ins, outs, *sems)
        rider.wait(ins, outs, *sems)

    any_spec = pl.BlockSpec(memory_space=pl.ANY)
    return _pc(body, name, in_specs=[any_spec] * n_rin, out_specs=[any_spec] * len(rider.out_shapes),
               out_shape=rider.out_shapes, scratch_shapes=rider.sem_shapes())(*rider.inputs)


def _dot(a, b):
    return jnp.dot(a, b, preferred_element_type=F32)


def _dot_nt(a, b):
    return lax.dot_general(a, b, (((1,), (1,)), ((), ())), preferred_element_type=F32)


def _dot_tn(a, b):
    return lax.dot_general(a, b, (((0,), (0,)), ((), ())), preferred_element_type=F32)


def _dot_hi(a, b):
    return jnp.dot(a, b, precision=HI, preferred_element_type=F32)


def _split2(x):
    hi = x.astype(BF16)
    return hi, (x - hi.astype(F32)).astype(BF16)


def _sel_dot(sel, x):
    hi, lo = _split2(x)
    sb = sel.astype(BF16)
    return _dot(sb, hi) + _dot(sb, lo)


def _dot_sel(x, sel):
    hi, lo = _split2(x)
    sb = sel.astype(BF16)
    return _dot(hi, sb) + _dot(lo, sb)


def _sigmoid(x):
    return 1.0 / (1.0 + jnp.exp(-x))


def _block_ones(n, dtype):
    r = lax.broadcasted_iota(jnp.int32, (n, n), 0) // HEAD
    c = lax.broadcasted_iota(jnp.int32, (n, n), 1) // HEAD
    return (r == c).astype(dtype)


def _sds(shape, dtype):
    return jax.ShapeDtypeStruct(shape, dtype)


def _in_proj_fwd(x, g_pre, w_main, w_fc, name):
    n, d = x.shape
    tm = min(512, n)

    def body(x_ref, g_ref, w_ref, wf_ref, proj_ref, fc_ref, ht_ref, qt_ref, kt_ref):
        xv = x_ref[...]
        r = lax.rsqrt(jnp.mean(xv * xv, axis=-1, keepdims=True) + NORM_EPS)
        hf = xv * r * g_ref[...]
        hb = hf.astype(BF16)
        ht_ref[...] = hf.T.astype(BF16)
        for j in range(0, MAIN_W, FOX_W):
            res = _dot(hb, w_ref[:, j:j + FOX_W])
            proj_ref[:, j:j + FOX_W] = res
            if j == C_QC:
                qt_ref[...] = (res * FOX_SCALE).T.astype(BF16)
            if j == C_KC:
                kt_ref[...] = res.T.astype(BF16)
        fc_ref[...] = _dot(hb, wf_ref[...])

    def cols(rows):
        return pl.BlockSpec((rows, tm), lambda i: (0, i))

    return _pc(
        body, name, grid=(n // tm,),
        in_specs=[pl.BlockSpec((tm, d), lambda i: (i, 0)), pl.BlockSpec((1, d), lambda i: (0, 0)),
                  pl.BlockSpec((d, MAIN_W), lambda i: (0, 0)), pl.BlockSpec((d, FC_PAD), lambda i: (0, 0))],
        out_specs=[pl.BlockSpec((tm, MAIN_W), lambda i: (i, 0)), pl.BlockSpec((tm, FC_PAD), lambda i: (i, 0)),
                   cols(d), cols(FOX_W), cols(FOX_W)],
        out_shape=[_sds((n, MAIN_W), F32), _sds((n, FC_PAD), F32), _sds((d, n), BF16), _sds((FOX_W, n), BF16),
                   _sds((FOX_W, n), BF16)],
        compiler_params=_params(),
    )(x, g_pre, w_main, w_fc)


def _fox_decay_fwd(fc, bias, nb, name):
    n = fc.shape[0]
    t = n // nb
    tt = min(256, t)
    nt = t // tt

    def body(fc_ref, b_ref, c_ref, cr_ref, carry):
        i = pl.program_id(1)

        @pl.when(i == 0)
        def _():
            carry[...] = jnp.zeros_like(carry)

        xv = fc_ref[...] + b_ref[...]
        lf = jnp.minimum(xv, 0.0) - jnp.log(1.0 + jnp.exp(-jnp.abs(xv)))
        r = lax.broadcasted_iota(jnp.int32, (tt, tt), 0)
        cc = lax.broadcasted_iota(jnp.int32, (tt, tt), 1)
        cs = _dot_hi((r >= cc).astype(F32), lf) + carry[...]
        c_ref[...] = cs
        cr_ref[...] = cs.T[:FOX_HEADS, :]
        carry[...] = cs[tt - 1:tt, :]

    return _pc(
        body, name, grid=(nb, nt),
        in_specs=[pl.BlockSpec((tt, FC_PAD), lambda b, i: (b * nt + i, 0)), pl.BlockSpec((1, FC_PAD), lambda b, i: (0, 0))],
        out_specs=[pl.BlockSpec((tt, FC_PAD), lambda b, i: (b * nt + i, 0)),
                   pl.BlockSpec((None, FOX_HEADS, tt), lambda b, i: (b, 0, i))],
        out_shape=[_sds((n, FC_PAD), F32), _sds((nb, FOX_HEADS, t), F32)],
        scratch_shapes=[pltpu.VMEM((1, FC_PAD), F32)],
        compiler_params=_params(),
    )(fc, bias)


def _hgrn_gates(q, z, lb):
    sig = _sigmoid(z)
    sn = _sigmoid(-z)
    f = lb + (1.0 - lb) * sig
    g = jnp.log(jnp.maximum(f, TINY))
    k = (1.0 - lb) * sn
    sq = _sigmoid(q)
    return sig, sn, f, g, k, sq


def _sub_tri(n, lower):
    r = lax.broadcasted_iota(jnp.int32, (n, n), 0)
    c = lax.broadcasted_iota(jnp.int32, (n, n), 1)
    tri = (r >= c) if lower else (r <= c)
    return jnp.logical_and(r // SUB == c // SUB, tri).astype(F32)


def _live_rows(t):
    return 8 * (t // 8 + 1)


def _pad_rows(x):
    return x if x.shape[0] == SUB else jnp.concatenate([x, jnp.zeros((SUB - x.shape[0], x.shape[1]), x.dtype)], axis=0)


def _hgrn_decays(qs, k, b):
    srow = lax.broadcasted_iota(jnp.int32, (SUB, HGRN_W), 0)
    es, ws = [], []
    for t in range(SUB):
        r = _live_rows(t)
        e = jnp.where(srow[:r] <= t, jnp.exp(b[t:t + 1, :] - b[:r]), 0.0)
        es.append(e)
        ws.append(_pad_rows(e * (qs[t:t + 1, :] * k[:r])))
    return srow, es, ws


def _hgrn_state_step(st, k, v, b, bmask):
    bl = b[SUB - 1:SUB, :]
    ktil = k * jnp.exp(bl - b)
    return st * jnp.exp(bl) + _dot_tn(v.astype(BF16), ktil.astype(BF16)) * bmask


def _hgrn_tile(t):
    return min(256, t)


def _hgrn_fwd(proj, lb, ones_b, nb, name, rider=None):
    n = proj.shape[0]
    t = n // nb
    tt = _hgrn_tile(t)
    nt = t // tt
    ncs = tt // CHUNK
    w = HGRN_W

    def body(q_ref, z_ref, v_ref, lb_ref, ones_ref, o_ref, s0_ref, st_s, b_s, qs_s, k_s):
        @pl.when(pl.program_id(1) == 0)
        def _():
            st_s[...] = jnp.zeros_like(st_s)

        q = q_ref[...]
        _, _, _, g, k, sq = _hgrn_gates(q, z_ref[...], lb_ref[...])
        b_s[...] = _sel_dot(_sub_tri(tt, True), g)
        qs_s[...] = q * sq
        k_s[...] = k
        bmask = _block_ones(w, F32)
        ones_b = ones_ref[...]

        def chunk(c, carry):
            st = st_s[...]
            s0_ref[c] = st
            base = pl.multiple_of(c * CHUNK, CHUNK)
            tiles = []
            for u in range(CHUNK // SUB):
                rows = pl.ds(base + u * SUB, SUB)
                tiles.append((qs_s[rows, :], k_s[rows, :], v_ref[rows, :], b_s[rows, :]))
            aexps = []
            for qs, k, v, b in tiles:
                _, _, ws = _hgrn_decays(qs, k, b)
                aexps.append(_dot(jnp.concatenate(ws, axis=0).astype(BF16), ones_b))
            inters = []
            for qs, k, v, b in tiles:
                inters.append(_dot_nt((qs * jnp.exp(b)).astype(BF16), st.astype(BF16)))
                st = _hgrn_state_step(st, k, v, b, bmask)
            st_s[...] = st
            for u, ((qs, k, v, b), aexp, o) in enumerate(zip(tiles, aexps, inters)):
                for t in range(SUB):
                    r = _live_rows(t)
                    row = o[t:t + 1, :] + jnp.sum(aexp[t * SUB:t * SUB + r, :] * v[:r], axis=0, keepdims=True)
                    o_ref[pl.ds(base + u * SUB + t, 1), :] = row
            return carry

        lax.fori_loop(0, ncs, chunk, 0)

    def col(j):
        return pl.BlockSpec((tt, w), lambda b, i: (b * nt + i, j))

    return _call(
        body, name, (proj, proj, proj, lb, ones_b), rider, grid=(nb, nt),
        in_specs=[col(C_QA // w), col(C_FA // w), col(C_IA // w), pl.BlockSpec((1, w), lambda b, i: (0, 0)),
                  pl.BlockSpec((w, w), lambda b, i: (0, 0))],
        out_specs=[pl.BlockSpec((tt, w), lambda b, i: (b * nt + i, 0)),
                   pl.BlockSpec((ncs, w, w), lambda b, i: (b * nt + i, 0, 0))],
        out_shape=[_sds((n, w), F32), _sds((n // CHUNK, w, w), F32)],
        scratch_shapes=[pltpu.VMEM((w, w), F32)] + [pltpu.VMEM((tt, w), F32)] * 3,
        compiler_params=_params(),
    )


def _pool_lane_windows():
    lane = lax.broadcasted_iota(jnp.int32, (1, POOL_W), 1) // HEAD
    wl = jnp.zeros((1, POOL_W), F32)
    for gi, win in enumerate(POOL_WINDOWS):
        wl = jnp.where(lane == gi, float(win), wl)
    return lane, wl


def _pool_select(lane, parts):
    out = parts[-1]
    for gi in range(len(parts) - 2, -1, -1):
        out = jnp.where(lane == gi, parts[gi], out)
    return out


def _pool_mix(u, halo, t0, tt):
    lane, wl = _pool_lane_windows()
    ext = jnp.concatenate([halo, u], axis=0)
    sums, cur, shift = [], ext, 1
    for _ in POOL_WINDOWS:
        cur = cur + pltpu.roll(cur, shift, axis=0)
        sums.append(cur[POOL_HALO:, :])
        shift *= 2
    tpos = (t0 + lax.broadcasted_iota(jnp.int32, (tt, POOL_W), 0)).astype(F32)
    cnt = jnp.minimum(tpos + 1.0, wl)
    return _pool_select(lane, sums) / cnt - u, cnt


def _pool_specs(tt, nt, nhb):
    cu, cg = C_UB // POOL_W, C_GB // POOL_W
    return [pl.BlockSpec((tt, POOL_W), lambda b, i: (b * nt + i, cu)),
            pl.BlockSpec((tt, POOL_W), lambda b, i: (b * nt + i, cg)),
            pl.BlockSpec((POOL_HALO, POOL_W), lambda b, i: (jnp.maximum((b * nt + i) * nhb - 1, 0), cu))]


def _pool_fwd(proj, wbd, scale, nb, name):
    n = proj.shape[0]
    t = n // nb
    tt = min(512, t)
    nt = t // tt
    nhb = tt // POOL_HALO

    def body(u_ref, g_ref, h_ref, w_ref, s_ref, o_ref):
        i = pl.program_id(1)
        halo = jnp.where(i == 0, 0.0, h_ref[...])
        pooled, _ = _pool_mix(u_ref[...], halo, i * tt, tt)
        gv = g_ref[...]
        o_ref[...] = _dot(pooled.astype(BF16), w_ref[...]) * s_ref[...] * (gv * _sigmoid(gv))

    return _pc(
        body, name, grid=(nb, nt),
        in_specs=_pool_specs(tt, nt, nhb) + [pl.BlockSpec((POOL_W, POOL_W), lambda b, i: (0, 0)),
                                             pl.BlockSpec((1, POOL_W), lambda b, i: (0, 0))],
        out_specs=pl.BlockSpec((tt, POOL_W), lambda b, i: (b * nt + i, 0)),
        out_shape=_sds((n, POOL_W), F32),
        compiler_params=_params(),
    )(proj, proj, proj, wbd, scale)


def _rows_reduce(x, op, final):
    while x.shape[0] > 8 and x.shape[0] % 16 == 0:
        half = x.shape[0] // 2
        x = op(x[:half], x[half:])
    return final(x, axis=0, keepdims=True)


def _tri_pair(step, n):
    a = sum([(step >= r * (r + 1) // 2).astype(jnp.int32) for r in range(1, n)], jnp.int32(0))
    return a, step - a * (a + 1) // 2


def _lane_lo():
    return lax.broadcasted_iota(jnp.int32, (1, 2 * HEAD), 1) < HEAD


def _put_col(tile, hh, colv):
    lane = lax.broadcasted_iota(jnp.int32, tile.shape, 1)
    return jnp.where(lane == hh, colv, tile)


def _fox_fwd(proj, kt, c_col, c_row, nb, name, rider=None):
    n = proj.shape[0]
    t = n // nb
    tb = min(256, t)
    nq = t // tb
    pw = 2 * HEAD

    def body(q_ref, kt_ref, v_ref, cc_ref, cr_ref, o_ref, lse_ref, m_s, acc_s, cq_s):
        qi, kj = _tri_pair(pl.program_id(1), nq)

        @pl.when(kj == 0)
        def _():
            m_s[...] = jnp.full_like(m_s, -jnp.inf)
            acc_s[...] = jnp.zeros_like(acc_s)
            for hh in range(FOX_HEADS):
                cq_s[hh] = jnp.broadcast_to(cc_ref[:, hh:hh + 1], (tb, pw))

        def block(masked):
            lo = _lane_lo()
            if masked:
                causal = lax.broadcasted_iota(jnp.int32, (tb, tb), 0) >= lax.broadcasted_iota(jnp.int32, (tb, tb), 1)
            def lanes(hh):
                return lo if hh % 2 == 0 else jnp.logical_not(lo)

            def scores(hh):
                sl = slice((hh // 2) * pw, (hh // 2 + 1) * pw)
                return _dot(jnp.where(lanes(hh), q_ref[:, sl] * FOX_SCALE, 0.0).astype(BF16), kt_ref[sl, :])

            ahead = scores(0)
            for hh in range(FOX_HEADS):
                s = ahead
                if hh + 1 < FOX_HEADS:
                    ahead = scores(hh + 1)
                s = s + (jnp.tile(cq_s[hh], (1, tb // pw)) - cr_ref[hh:hh + 1, :])
                if masked:
                    s = jnp.where(causal, s, MASK_VALUE)
                m_prev = m_s[hh]
                m_new = jnp.maximum(m_prev, jnp.max(s, axis=1, keepdims=True))
                alpha = jnp.exp(m_prev - m_new)
                pe = jnp.exp(s - jnp.tile(m_new, (1, tb // pw)))
                m_s[hh] = m_new
                vf = v_ref[:, (hh // 2) * pw:(hh // 2 + 1) * pw]
                acc_s[hh] = alpha * acc_s[hh] + _dot(pe.astype(BF16), jnp.where(lanes(hh), vf, 1.0).astype(BF16))

        @pl.when(kj < qi)
        def _():
            block(False)

        @pl.when(kj == qi)
        def _():
            block(True)
            lo = _lane_lo()
            m_all, l_all = jnp.zeros((tb, FC_PAD), F32), jnp.ones((tb, FC_PAD), F32)
            for p in range(FOX_HEADS // 2):
                a0, a1 = acc_s[2 * p], acc_s[2 * p + 1]
                both = pltpu.roll(jnp.where(lo, a1, a0), HEAD, axis=1)
                o_ref[:, p * pw:(p + 1) * pw] = jnp.where(lo, a0, a1) / both
                m_all = _put_col(_put_col(m_all, 2 * p, m_s[2 * p]), 2 * p + 1, m_s[2 * p + 1])
                l_all = _put_col(_put_col(l_all, 2 * p, both), 2 * p + 1, a1)
            lse_ref[...] = (m_all + jnp.log(l_all)).T[:FOX_HEADS, :]

    def qspec(wd, j):
        return pl.BlockSpec((tb, wd), lambda b, st: (b * nq + _tri_pair(st, nq)[0], j))

    def kspec(j):
        return pl.BlockSpec((tb, FOX_W), lambda b, st: (b * nq + _tri_pair(st, nq)[1], j))

    return _call(
        body, name, (proj, kt, proj, c_col, c_row), rider, grid=(nb, nq * (nq + 1) // 2),
        in_specs=[qspec(FOX_W, C_QC // FOX_W), pl.BlockSpec((FOX_W, tb), lambda b, st: (0, b * nq + _tri_pair(st, nq)[1])),
                  kspec(C_VC // FOX_W), qspec(FC_PAD, 0),
                  pl.BlockSpec((None, FOX_HEADS, tb), lambda b, st: (b, 0, _tri_pair(st, nq)[1]))],
        out_specs=[qspec(FOX_W, 0), pl.BlockSpec((None, FOX_HEADS, tb), lambda b, st: (b, 0, _tri_pair(st, nq)[0]))],
        out_shape=[_sds((n, FOX_W), F32), _sds((nb, FOX_HEADS, t), F32)],
        scratch_shapes=[pltpu.VMEM((FOX_HEADS, tb, pw), F32), pltpu.VMEM((FOX_HEADS, tb, pw), F32),
                        pltpu.VMEM((FOX_HEADS, tb, pw), F32)],
        compiler_params=_params(),
    )


def _head_mean(x, ones_f):
    return _dot_sel(x, ones_f) * (1.0 / HEAD)


def _merge_fwd(x, proj, o_h, o_b, o_c, gh, w_out, g_post, name, target=None):
    n, d = x.shape
    tm = min(512, n)

    def body(*refs):
        x_ref, ga_ref, gc_ref, oh_ref, ob_ref, oc_ref, gh_ref, w_ref, gp_ref = refs[:9]
        if target is None:
            xo_ref, mixt_ref, y_ref = refs[9:]
        else:
            t_ref, dx_ref, sq_ref, mixt_ref, y_ref = refs[9:]
        oh = oh_ref[...]
        ones_f = _block_ones(HGRN_W, F32)
        na = oh * lax.rsqrt(_head_mean(oh * oh, ones_f) + NORM_EPS) * gh_ref[...]
        ga, gc = ga_ref[...], gc_ref[...]
        mixed = jnp.concatenate([na * (ga * _sigmoid(ga)), ob_ref[...], oc_ref[...] * (gc * _sigmoid(gc))], axis=1)
        mixt_ref[...] = mixed.T.astype(BF16)
        y = _dot(mixed.astype(BF16), w_ref[...])
        y_ref[...] = y
        xn = x_ref[...] + y * lax.rsqrt(jnp.mean(y * y, axis=-1, keepdims=True) + NORM_EPS) * gp_ref[...]
        if target is None:
            xo_ref[...] = xn
        else:
            @pl.when(pl.program_id(0) == 0)
            def _():
                sq_ref[...] = jnp.zeros_like(sq_ref)

            e = xn - t_ref[...]
            dx_ref[...] = e * (1.0 / d)
            sq_ref[...] += jnp.sum(e * e, axis=0, keepdims=True)

    def row(wd, j=0):
        return pl.BlockSpec((tm, wd), lambda i: (i, j))

    def full(a, b):
        return pl.BlockSpec((a, b), lambda i: (0, 0))

    head = [] if target is None else [target]
    res = _pc(
        body, name, grid=(n // tm,),
        in_specs=[row(d), row(HGRN_W, C_GA // HGRN_W), row(FOX_W, C_GC // FOX_W), row(HGRN_W), row(POOL_W), row(FOX_W),
                  full(1, HGRN_W), full(d, d), full(1, d)] + [row(d)] * len(head),
        out_specs=[row(d)] + [full(1, d)] * len(head) + [pl.BlockSpec((d, tm), lambda i: (0, i)), row(d)],
        out_shape=[_sds((n, d), F32)] + [_sds((1, d), F32)] * len(head) + [_sds((d, n), BF16), _sds((n, d), F32)],
        compiler_params=_params(),
    )(x, proj, proj, o_h, o_b, o_c, gh, w_out, g_post, *head)
    return (res[0], res[1], res[2]) if target is None else ((res[0], res[1]), res[2], res[3])


def _rms_bwd(dy_scaled, xhat, r):
    return r * (dy_scaled - xhat * jnp.mean(dy_scaled * xhat, axis=-1, keepdims=True))


def _merge_bwd(dxo, y, g_post, w_out_t, proj, o_c, nb, name):
    n, d = y.shape
    t = n // nb
    tm = min(512, t)
    nt = t // tm
    wab = HGRN_W + POOL_W

    def body(dx_ref, y_ref, gp_ref, wt_ref, gc_ref, oc_ref, dy_ref, dm_ref, dgp_ref, da_ref, dat_ref, dg_ref, dl_ref):
        @pl.when(pl.program_id(0) == 0)
        def _():
            dgp_ref[...] = jnp.zeros_like(dgp_ref)

        yv, dxv = y_ref[...], dx_ref[...]
        r = lax.rsqrt(jnp.mean(yv * yv, axis=-1, keepdims=True) + NORM_EPS)
        yh = yv * r
        dgp_ref[...] += jnp.sum(dxv * yh, axis=0, keepdims=True)
        dyb = _rms_bwd(dxv * gp_ref[...], yh, r).astype(BF16)
        dy_ref[...] = dyb
        dm_ref[...] = _dot(dyb, wt_ref[:, :wab])
        dmc = _dot(dyb, wt_ref[:, wab:])
        gc, oc = gc_ref[...], oc_ref[...]
        sg = _sigmoid(gc)
        da = dmc * (gc * sg)
        da_ref[...] = da.astype(BF16)
        dat_ref[...] = da.T.astype(BF16)
        dg_ref[...] = (dmc * oc * (sg * (1.0 + gc * (1.0 - sg)))).astype(BF16)
        rr = lax.broadcasted_iota(jnp.int32, (FOX_W, FC_PAD), 0) // HEAD
        cc = lax.broadcasted_iota(jnp.int32, (FOX_W, FC_PAD), 1)
        dl_ref[...] = _dot_sel(da * oc, (rr == cc).astype(F32)).T[:FOX_HEADS, :]

    def row(wd, j=0):
        return pl.BlockSpec((tm, wd), lambda i: (i, j))

    def full(a, b):
        return pl.BlockSpec((a, b), lambda i: (0, 0))

    return _pc(
        body, name, grid=(n // tm,),
        in_specs=[row(d), row(d), full(1, d), full(d, d), row(FOX_W, C_GC // FOX_W), row(FOX_W)],
        out_specs=[row(d), row(wab), full(1, d), row(FOX_W), pl.BlockSpec((FOX_W, tm), lambda i: (0, i)), row(FOX_W),
                   pl.BlockSpec((None, FOX_HEADS, tm), lambda i: (i // nt, 0, i % nt))],
        out_shape=[_sds((n, d), BF16), _sds((n, wab), F32), _sds((1, d), F32), _sds((n, FOX_W), BF16),
                   _sds((FOX_W, n), BF16), _sds((n, FOX_W), BF16), _sds((nb, FOX_HEADS, t), F32)],
        compiler_params=_params(),
    )(dxo, y, g_post, w_out_t, proj, o_c)


def _w_out_grad(mixt, dy, name):
    d, n = mixt.shape
    rows = d // N_DEV

    def body(a_ref, b_ref, o_ref):
        o_ref[...] = _dot(a_ref[...], b_ref[...]).astype(BF16)

    return _pc(
        body, name, grid=(N_DEV,),
        in_specs=[pl.BlockSpec((rows, n), lambda j: (j, 0)), pl.BlockSpec((n, d), lambda j: (0, 0))],
        out_specs=pl.BlockSpec((None, None, rows, d), lambda j: (j % 2, j // 2, 0, 0)),
        out_shape=_sds((2, N_DEV // 2, rows, d), BF16),
        compiler_params=_params(),
    )(mixt, dy)


def _w_in_grad(ht, pieces, name):
    d, n = ht.shape
    ta, tk = min(1024, d), min(512, n)
    nk = n // tk
    arrays = [p for p, _ in pieces]
    widths = [p.shape[1] for p in arrays]
    offs = [sum(widths[:i]) for i in range(len(widths))]
    in_w = MAIN_W + FOX_HEADS
    shard = in_w // N_DEV

    def body(*refs):
        a_ref, p_refs = refs[0], refs[1:1 + len(arrays)]
        o_ref, acc = refs[1 + len(arrays):]
        k = pl.program_id(1)

        @pl.when(k == 0)
        def _():
            acc[...] = jnp.zeros_like(acc)

        a = a_ref[...]
        for pr, off, wd in zip(p_refs, offs, widths):
            for j in range(0, wd, 512):
                jw = min(512, wd - j)
                acc[:, off + j:off + j + jw] += _dot(a, pr[:, j:j + jw])

        @pl.when(k == nk - 1)
        def _():
            for j in range(N_DEV):
                o_ref[j % 2, j // 2] = acc[:, j * shard:(j + 1) * shard].astype(BF16)

    return _pc(
        body, name, grid=(d // ta, nk),
        in_specs=[pl.BlockSpec((ta, tk), lambda i, k: (i, k))] + [pl.BlockSpec((tk, wd), lambda i, k: (k, 0)) for wd in widths],
        out_specs=pl.BlockSpec((2, N_DEV // 2, ta, shard), lambda i, k: (0, 0, i, 0)),
        out_shape=_sds((2, N_DEV // 2, d, shard), BF16),
        scratch_shapes=[pltpu.VMEM((ta, sum(widths)), F32)],
        compiler_params=_params(),
    )(ht, *arrays)


def _hgrn_state_bwd(qs, k, v, b, do, s0, ds1, bmask):
    bl = b[SUB - 1:SUB, :]
    eb, ebl, ekt = jnp.exp(b), jnp.exp(bl), jnp.exp(bl - b)
    qe, ktil = qs * eb, k * ekt
    ds1b, dob = ds1.astype(BF16), do.astype(BF16)
    dv = _dot_nt(ktil.astype(BF16), ds1b)
    dqe = _dot(dob, s0.astype(BF16))
    dktil = _dot(v.astype(BF16), ds1b)
    dbl = jnp.sum(dktil * ktil, axis=0, keepdims=True) + ebl * jnp.sum(s0 * ds1, axis=0, keepdims=True)
    ds0 = ds1 * ebl + _dot_tn(dob, qe.astype(BF16)) * bmask
    return dqe * eb, dktil * ekt, dv, dbl, ds0


def _hgrn_intra_bwd(qs, k, v, do, es, aexp, gexp, dq, dk, dv, put_dq_row):
    dks = [dk[j:j + 8] for j in range(0, SUB, 8)]
    dvs = [dv[j:j + 8] for j in range(0, SUB, 8)]
    for t in range(SUB):
        r = _live_rows(t)
        ge = gexp[t * SUB:t * SUB + r, :] * es[t]
        put_dq_row(t, dq[t:t + 1, :] + jnp.sum(ge * k[:r], axis=0, keepdims=True))
        for j in range(r // 8):
            dks[j] = dks[j] + ge[8 * j:8 * j + 8] * qs[t:t + 1, :]
            dvs[j] = dvs[j] + aexp[t * SUB + 8 * j:t * SUB + 8 * j + 8, :] * do[t:t + 1, :]
    return jnp.concatenate(dks, axis=0), jnp.concatenate(dvs, axis=0)


def _hgrn_bwd(dmix, proj, o_h, s0, gh, lb, ones_b, nb, name):
    n = proj.shape[0]
    t = n // nb
    tt = _hgrn_tile(t)
    nt = t // tt
    ncs = tt // CHUNK
    nsub = CHUNK // SUB
    w = HGRN_W

    def body(dm_ref, q_ref, z_ref, v_ref, ga_ref, oh_ref, s0_ref, gh_ref, lb_ref, ones_ref,
             dp_ref, dgh_ref, dlb_ref, ds_s, ss_s, b_s, qs_s, k_s, do_s, dq_s, dk_s, dv_s, dbl_s):
        first = jnp.logical_and(pl.program_id(0) == 0, pl.program_id(1) == 0)

        @pl.when(first)
        def _():
            dgh_ref[...] = jnp.zeros_like(dgh_ref)
            dlb_ref[...] = jnp.zeros_like(dlb_ref)

        @pl.when(pl.program_id(1) == 0)
        def _():
            ds_s[...] = jnp.zeros_like(ds_s)

        ones_b = ones_ref[...]
        ones_f = ones_b.astype(F32)
        bmask = _block_ones(w, F32)
        lbv, ghv = lb_ref[...], gh_ref[...]
        oh, ga, dm = oh_ref[...], ga_ref[...], dm_ref[...]
        rn = lax.rsqrt(_head_mean(oh * oh, ones_f) + NORM_EPS)
        nh = oh * rn
        sga = _sigmoid(ga)
        dp_ref[:, 3 * w:4 * w] = (dm * nh * ghv * (sga * (1.0 + ga * (1.0 - sga)))).astype(BF16)
        dn = dm * (ga * sga)
        dgh_ref[...] += jnp.sum(dn * nh, axis=0, keepdims=True)
        dn = dn * ghv
        do_s[...] = rn * (dn - nh * _head_mean(dn * nh, ones_f))
        q = q_ref[...]
        sig, sn, f, g, k, sq = _hgrn_gates(q, z_ref[...], lbv)
        qs = q * sq
        b_s[...] = _sel_dot(_sub_tri(tt, True), g)
        qs_s[...] = qs
        k_s[...] = k

        def chunk(cc, carry):
            c = ncs - 1 - cc
            base = pl.multiple_of(c * CHUNK, CHUNK)
            tiles = []
            for u in range(nsub):
                rows = pl.ds(base + u * SUB, SUB)
                tiles.append((qs_s[rows, :], k_s[rows, :], v_ref[rows, :], b_s[rows, :], do_s[rows, :]))
            st = s0_ref[c]
            for u, (qs, k, v, b, do) in enumerate(tiles):
                ss_s[u] = st
                if u < nsub - 1:
                    st = _hgrn_state_step(st, k, v, b, bmask)
            ds = ds_s[...]
            for u in reversed(range(nsub)):
                qs, k, v, b, do = tiles[u]
                _, es, ws = _hgrn_decays(qs, k, b)
                gs = [_pad_rows(do[t:t + 1, :] * v[:_live_rows(t)]) for t in range(SUB)]
                aexp = _dot(jnp.concatenate(ws, axis=0).astype(BF16), ones_b)
                gexp = _dot(jnp.concatenate(gs, axis=0).astype(BF16), ones_b)
                dq, dk, dv, dbl, ds = _hgrn_state_bwd(qs, k, v, b, do, ss_s[u], ds, bmask)

                def put_dq_row(i, row, r0=base + u * SUB):
                    dq_s[pl.ds(r0 + i, 1), :] = row

                dk, dv = _hgrn_intra_bwd(qs, k, v, do, es, aexp, gexp, dq, dk, dv, put_dq_row)
                dk_s[pl.ds(base + u * SUB, SUB), :] = dk
                dv_s[pl.ds(base + u * SUB, SUB), :] = dv
                dbl_s[pl.ds(base + u * SUB, SUB), :] = jnp.broadcast_to(dbl, (SUB, w))
            ds_s[...] = ds
            return carry

        lax.fori_loop(0, ncs, chunk, 0)
        dqs, dk = dq_s[...], dk_s[...]
        dg = _sel_dot(_sub_tri(tt, False), qs * dqs - k * dk) + dbl_s[...]
        dfz = jnp.where(f > TINY, dg / jnp.maximum(f, TINY), 0.0)
        dlb_ref[...] += jnp.sum(dfz * (1.0 - sig) - dk * sn, axis=0, keepdims=True)
        dp_ref[:, 0:w] = (dqs * (sq * (1.0 + q * (1.0 - sq)))).astype(BF16)
        dp_ref[:, w:2 * w] = ((dfz - dk) * (1.0 - lbv) * sig * sn).astype(BF16)
        dp_ref[:, 2 * w:3 * w] = dv_s[...].astype(BF16)

    def rv(b, i):
        return b * nt + (nt - 1 - i)

    def col(j):
        return pl.BlockSpec((tt, w), lambda b, i: (rv(b, i), j))

    def full(a, bb):
        return pl.BlockSpec((a, bb), lambda b, i: (0, 0))

    return _pc(
        body, name, grid=(nb, nt),
        in_specs=[col(0), col(C_QA // w), col(C_FA // w), col(C_IA // w), col(C_GA // w), col(0),
                  pl.BlockSpec((ncs, w, w), lambda b, i: (rv(b, i), 0, 0)), full(1, w), full(1, w), full(w, w)],
        out_specs=[pl.BlockSpec((tt, 4 * w), lambda b, i: (rv(b, i), 0)), full(1, w), full(1, w)],
        out_shape=[_sds((n, 4 * w), BF16), _sds((1, w), F32), _sds((1, w), F32)],
        scratch_shapes=[pltpu.VMEM((w, w), F32), pltpu.VMEM((nsub, w, w), F32)] + [pltpu.VMEM((tt, w), F32)] * 8,
        compiler_params=_params(),
    )(dmix, proj, proj, proj, proj, o_h, s0, gh, lb, ones_b)


def _pool_bwd(dmix, proj, wbd, wbd_t, scale, nb, name):
    n = proj.shape[0]
    t = n // nb
    tt = min(512, t)
    nt = t // tt
    nhb = tt // POOL_HALO
    cu, cg, cm = C_UB // POOL_W, C_GB // POOL_W, HGRN_W // POOL_W

    def body(u_ref, g_ref, h_ref, dm_ref, gn_ref, dmn_ref, w_ref, wt_ref, s_ref, dp_ref, dw_ref, ds_ref):
        i = pl.program_id(1)
        first = jnp.logical_and(pl.program_id(0) == 0, i == 0)

        @pl.when(first)
        def _():
            dw_ref[...] = jnp.zeros_like(dw_ref)
            ds_ref[...] = jnp.zeros_like(ds_ref)

        sc = s_ref[...]
        halo = jnp.where(i == 0, 0.0, h_ref[...])
        pooled, cnt = _pool_mix(u_ref[...], halo, i * tt, tt)
        pb = pooled.astype(BF16)
        pre = _dot(pb, w_ref[...])
        gv, dm = g_ref[...], dm_ref[...]
        sg = _sigmoid(gv)
        silu = gv * sg
        dgb = dm * pre * sc * (sg * (1.0 + gv * (1.0 - sg)))
        ds_ref[...] += jnp.sum(dm * pre * silu, axis=0, keepdims=True)
        dpre = (dm * sc * silu).astype(BF16)
        dw_ref[...] += _dot_tn(pb, dpre)
        dpool = _dot(dpre, wt_ref[...])
        gn = gn_ref[...]
        dpre_n = (dmn_ref[...] * sc * (gn * _sigmoid(gn))).astype(BF16)
        dpool_n = jnp.where(i == nt - 1, 0.0, _dot(dpre_n, wt_ref[...]))
        lane, wl = _pool_lane_windows()
        tpos_n = ((i + 1) * tt + lax.broadcasted_iota(jnp.int32, (POOL_HALO, POOL_W), 0)).astype(F32)
        ext = jnp.concatenate([dpool / cnt, dpool_n / jnp.minimum(tpos_n + 1.0, wl)], axis=0)
        rows = tt + POOL_HALO
        sums, cur, shift = [], ext, 1
        for _ in POOL_WINDOWS:
            cur = cur + pltpu.roll(cur, rows - shift, axis=0)
            sums.append(cur[:tt, :])
            shift *= 2
        du = _pool_select(lane, sums) - dpool
        dp_ref[...] = jnp.concatenate([du, dgb], axis=1).astype(BF16)

    def nxt(b, i):
        return jnp.minimum((b * nt + i + 1) * nhb, n // POOL_HALO - 1)

    return _pc(
        body, name, grid=(nb, nt),
        in_specs=_pool_specs(tt, nt, nhb) + [
            pl.BlockSpec((tt, POOL_W), lambda b, i: (b * nt + i, cm)),
            pl.BlockSpec((POOL_HALO, POOL_W), lambda b, i: (nxt(b, i), cg)),
            pl.BlockSpec((POOL_HALO, POOL_W), lambda b, i: (nxt(b, i), cm)),
            pl.BlockSpec((POOL_W, POOL_W), lambda b, i: (0, 0)), pl.BlockSpec((POOL_W, POOL_W), lambda b, i: (0, 0)),
            pl.BlockSpec((1, POOL_W), lambda b, i: (0, 0))],
        out_specs=[pl.BlockSpec((tt, 2 * POOL_W), lambda b, i: (b * nt + i, 0)),
                   pl.BlockSpec((POOL_W, POOL_W), lambda b, i: (0, 0)), pl.BlockSpec((1, POOL_W), lambda b, i: (0, 0))],
        out_shape=[_sds((n, 2 * POOL_W), BF16), _sds((POOL_W, POOL_W), F32), _sds((1, POOL_W), F32)],
        compiler_params=_params(),
    )(proj, proj, proj, dmix, proj, dmix, wbd, wbd_t, scale)


def _fox_bwd(proj, qt, kt, da, dat, c_col, c_row, lse_row, delta_row, nb, name, rider=None):
    n = proj.shape[0]
    t = n // nb
    tb = min(256, t)
    nq = t // tb
    pw = 2 * HEAD

    def body(q_ref, k_ref, v_ref, da_ref, qt_ref, kt_ref, dat_ref, cc_ref, cr_ref, lse_ref, dl_ref,
             dq_ref, dk_ref, dv_ref, dck_ref, dcq_ref, dq_s, dk_s, dv_s, dck_s, dcq_s):
        step = pl.program_id(1)
        kj, qi = pairs(step)

        @pl.when(step == 0)
        def _():
            dq_s[...] = jnp.zeros_like(dq_s)
            dcq_s[...] = jnp.zeros_like(dcq_s)

        @pl.when(qi == nq - 1)
        def _():
            dk_s[...] = jnp.zeros_like(dk_s)
            dv_s[...] = jnp.zeros_like(dv_s)
            dck_s[...] = jnp.zeros_like(dck_s)

        def block(masked):
            lo = _lane_lo()
            if masked:
                causal = lax.broadcasted_iota(jnp.int32, (tb, tb), 1) >= lax.broadcasted_iota(jnp.int32, (tb, tb), 0)
            dck = dck_s[...]
            for p in range(FOX_HEADS // 2):
                sl = slice(p * pw, (p + 1) * pw)
                qp = q_ref[:, sl] * FOX_SCALE
                kp = k_ref[:, sl].astype(BF16)
                vp = v_ref[:, sl].astype(BF16)
                dap = da_ref[:, sl]
                dk, dv = dk_s[:, sl], dv_s[:, sl]
                for h in range(2):
                    hh = 2 * p + h
                    lm = lo if h == 0 else jnp.logical_not(lo)
                    rows = slice(hh * HEAD, (hh + 1) * HEAD)
                    none = jnp.zeros((HEAD, tb), BF16)
                    qm = jnp.where(lm, qp, 0.0).astype(BF16)
                    dam = jnp.where(lm, dap, jnp.zeros_like(dap))
                    qtm = jnp.concatenate([qt_ref[rows, :], none] if h == 0 else [none, qt_ref[rows, :]], axis=0)
                    datm = jnp.concatenate([dat_ref[rows, :], none] if h == 0 else [none, dat_ref[rows, :]], axis=0)
                    s = _dot(kp, qtm) + (cr_ref[hh:hh + 1, :] - cc_ref[:, hh:hh + 1])
                    pe = jnp.exp(s - lse_ref[hh:hh + 1, :])
                    if masked:
                        pe = jnp.where(causal, pe, 0.0)
                    dp = _dot(vp, datm)
                    ds = pe * (dp - dl_ref[hh:hh + 1, :])
                    dsb = ds.astype(BF16)
                    dv = dv + _dot(pe.astype(BF16), dam)
                    dk = dk + _dot(dsb, qm)
                    dq_s[qi, rows, :] += _dot(kt_ref[rows, :], dsb)
                    dck = dck - _put_col(jnp.zeros_like(dck), hh, jnp.sum(ds, axis=1, keepdims=True))
                    dcq_s[qi, hh:hh + 1, :] += _rows_reduce(ds, jnp.add, jnp.sum)
                dk_s[:, sl] = dk
                dv_s[:, sl] = dv
            dck_s[...] = dck

        @pl.when(qi > kj)
        def _():
            block(False)

        @pl.when(qi == kj)
        def _():
            block(True)

        @pl.when(qi == kj)
        def _():
            dk_ref[...] = dk_s[...].astype(BF16)
            dv_ref[...] = dv_s[...].astype(BF16)
            dck_ref[...] = dck_s[...]

        @pl.when(step == nq * (nq + 1) // 2 - 1)
        def _():
            for j in range(nq):
                dq_ref[j * tb:(j + 1) * tb, :] = (dq_s[j].T * FOX_SCALE).astype(BF16)
                dcq_ref[:, j * tb:(j + 1) * tb] = dcq_s[j]

    def pairs(step):
        a, b = _tri_pair(step, nq)
        return nq - 1 - a, nq - 1 - b

    def kspec(wd, j=0):
        return pl.BlockSpec((tb, wd), lambda b, st: (b * nq + pairs(st)[0], j))

    def qspec(wd, j=0):
        return pl.BlockSpec((tb, wd), lambda b, st: (b * nq + pairs(st)[1], j))

    def qrow():
        return pl.BlockSpec((None, FOX_HEADS, tb), lambda b, st: (b, 0, pairs(st)[1]))

    def tspec(which):
        return pl.BlockSpec((FOX_W, tb), lambda b, st: (0, b * nq + pairs(st)[which]))

    return _call(
        body, name, (proj, proj, proj, da, qt, kt, dat, c_col, c_row, lse_row, delta_row), rider,
        grid=(nb, nq * (nq + 1) // 2),
        in_specs=[qspec(FOX_W, C_QC // FOX_W), kspec(FOX_W, C_KC // FOX_W), kspec(FOX_W, C_VC // FOX_W), qspec(FOX_W),
                  tspec(1), tspec(0), tspec(1), kspec(FC_PAD), qrow(), qrow(), qrow()],
        out_specs=[pl.BlockSpec((t, FOX_W), lambda b, st: (b, 0)), kspec(FOX_W), kspec(FOX_W), kspec(FC_PAD),
                   pl.BlockSpec((None, FOX_HEADS, t), lambda b, st: (b, 0, 0))],
        out_shape=[_sds((n, FOX_W), BF16), _sds((n, FOX_W), BF16), _sds((n, FOX_W), BF16), _sds((n, FC_PAD), F32),
                   _sds((nb, FOX_HEADS, t), F32)],
        scratch_shapes=[pltpu.VMEM((nq, FOX_W, tb), F32), pltpu.VMEM((tb, FOX_W), F32), pltpu.VMEM((tb, FOX_W), F32),
                        pltpu.VMEM((tb, FC_PAD), F32), pltpu.VMEM((nq, FOX_HEADS, tb), F32)],
        compiler_params=_params(),
    )


def _fox_decay_bwd(dc_q, dc_k, fc, bias, nb, name):
    n = fc.shape[0]
    t = n // nb
    tt = min(256, t)
    nt = t // tt

    def body(dcq_ref, dck_ref, fc_ref, b_ref, dfc_ref, db_ref, carry):
        i = pl.program_id(1)
        first = jnp.logical_and(pl.program_id(0) == 0, i == 0)

        @pl.when(first)
        def _():
            db_ref[...] = jnp.zeros_like(db_ref)

        @pl.when(i == 0)
        def _():
            carry[...] = jnp.zeros_like(carry)

        r = lax.broadcasted_iota(jnp.int32, (tt, tt), 0)
        cc = lax.broadcasted_iota(jnp.int32, (tt, tt), 1)
        dcq = jnp.concatenate([dcq_ref[...], jnp.zeros((FC_PAD - FOX_HEADS, tt), F32)], axis=0).T
        dlf = _dot_hi((r <= cc).astype(F32), dcq + dck_ref[...]) + carry[...]
        carry[...] = dlf[0:1, :]
        dfc = dlf * _sigmoid(-(fc_ref[...] + b_ref[...]))
        dfc_ref[...] = dfc.astype(BF16)
        db_ref[...] += jnp.sum(dfc, axis=0, keepdims=True)

    def row():
        return pl.BlockSpec((tt, FC_PAD), lambda b, i: (b * nt + (nt - 1 - i), 0))

    return _pc(
        body, name, grid=(nb, nt),
        in_specs=[pl.BlockSpec((None, FOX_HEADS, tt), lambda b, i: (b, 0, nt - 1 - i)), row(), row(),
                  pl.BlockSpec((1, FC_PAD), lambda b, i: (0, 0))],
        out_specs=[row(), pl.BlockSpec((1, FC_PAD), lambda b, i: (0, 0))],
        out_shape=[_sds((n, FC_PAD), BF16), _sds((1, FC_PAD), F32)],
        scratch_shapes=[pltpu.VMEM((1, FC_PAD), F32)],
        compiler_params=_params(),
    )(dc_q, dc_k, fc, bias)


def _in_proj_bwd(pieces, w_main_t, w_fc_t, x, g_pre, dxo, name, rider=None):
    n, d = x.shape
    tm = min(512, n)
    widths = [p.shape[1] for p, _ in pieces]
    offs = [o for _, o in pieces]
    np_ = len(pieces)

    def body(*refs):
        p_refs = refs[:np_]
        wt_ref, wf_ref, x_ref, g_ref, dxo_ref, dx_ref, dg_ref = refs[np_:]

        @pl.when(pl.program_id(0) == 0)
        def _():
            dg_ref[...] = jnp.zeros_like(dg_ref)

        dh = _dot(p_refs[-1][...], wf_ref[...])
        for pr, wd, off in zip(p_refs[:-1], widths[:-1], offs[:-1]):
            for j in range(0, wd, 512):
                jw = min(512, wd - j)
                dh = dh + _dot(pr[:, j:j + jw], wt_ref[off + j:off + j + jw, :])
        xv = x_ref[...]
        r = lax.rsqrt(jnp.mean(xv * xv, axis=-1, keepdims=True) + NORM_EPS)
        xh = xv * r
        dg_ref[...] += jnp.sum(dh * xh, axis=0, keepdims=True)
        dx_ref[...] = dxo_ref[...] + _rms_bwd(dh * g_ref[...], xh, r)

    row = pl.BlockSpec((tm, d), lambda i: (i, 0))
    return _call(
        body, name, (*[p for p, _ in pieces], w_main_t, w_fc_t, x, g_pre, dxo), rider, grid=(n // tm,),
        in_specs=[pl.BlockSpec((tm, wd), lambda i: (i, 0)) for wd in widths] + [
            pl.BlockSpec((MAIN_W, d), lambda i: (0, 0)), pl.BlockSpec((FC_PAD, d), lambda i: (0, 0)),
            row, pl.BlockSpec((1, d), lambda i: (0, 0)), row],
        out_specs=[row, pl.BlockSpec((1, d), lambda i: (0, 0))],
        out_shape=[_sds((n, d), F32), _sds((1, d), F32)],
        compiler_params=_params(),
    )


def _lower_bound_table(lower_bounds, name):
    depth, w = lower_bounds.shape

    def body(lb_ref, o_ref):
        v = lb_ref[...]
        e = jnp.exp(v - jnp.max(v, axis=0, keepdims=True))
        p = e / jnp.sum(e, axis=0, keepdims=True)
        acc = jnp.zeros((1, w), F32)
        for l in range(depth):
            acc = acc + p[l:l + 1, :]
            o_ref[l:l + 1, :] = acc - p[0:1, :]

    return _pc(body, name, out_shape=_sds((depth, w), F32))(lower_bounds)


def _lower_bound_bwd(lower_bounds, dlbs, name):
    depth, w = lower_bounds.shape

    def body(lb_ref, d_ref, o_ref):
        v, dl = lb_ref[...], d_ref[...]
        e = jnp.exp(v - jnp.max(v, axis=0, keepdims=True))
        p = e / jnp.sum(e, axis=0, keepdims=True)
        tot = jnp.sum(dl, axis=0, keepdims=True)
        rows, tail = [], tot
        for l in range(depth):
            rows.append(tail - tot if l == 0 else tail)
            tail = tail - dl[l:l + 1, :]
        dp = jnp.concatenate(rows, axis=0)
        o_ref[...] = p * (dp - jnp.sum(p * dp, axis=0, keepdims=True))

    return _pc(body, name, out_shape=_sds((depth, w), F32))(lower_bounds, dlbs)


def _place():
    x, y, c = lax.axis_index("x"), lax.axis_index("y"), lax.axis_index("c")
    return x, y, c


def _gather_weights(*arrays):
    na = len(arrays)

    def body(*refs):
        ins, outs = refs[:na], refs[na:2 * na]
        send_sems, recv_sems, local_sems = refs[2 * na:]
        x, y, c = _place()
        me, sibling = (x, y, c), (x, y, 1 - c)
        chips = [(1 - x, y), (x, 1 - y), (1 - x, 1 - y)]

        def slot(a, px, py, pc):
            return outs[a].at[4 * px + 2 * py + pc]

        def copy(a, k, block, to, own=False):
            return pltpu.make_async_remote_copy(
                src_ref=ins[a] if own else slot(a, *block), dst_ref=slot(a, *block),
                send_sem=send_sems.at[a * 7 + k], recv_sem=recv_sems.at[a * 7 + k],
                device_id=to, device_id_type=MESH)

        mine = [pltpu.make_async_copy(ins[a], slot(a, *me), local_sems.at[a]) for a in range(na)]
        for cp in mine:
            cp.start()
        first = []
        for a in range(na):
            first.append(copy(a, 0, me, sibling, own=True))
            first += [copy(a, 1 + j, me, (*chip, c), own=True) for j, chip in enumerate(chips)]
        for cp in first:
            cp.start()
        passed = []
        for j, chip in enumerate(chips):
            for a in range(na):
                copy(a, 1 + j, (*chip, c), me).wait_recv()
                fw = copy(a, 4 + j, (*chip, c), sibling)
                fw.start()
                passed.append(fw)
        for a in range(na):
            copy(a, 0, sibling, me).wait_recv()
            for j, chip in enumerate(chips):
                copy(a, 4 + j, (*chip, 1 - c), me).wait_recv()
        for cp in first + passed:
            cp.wait_send()
        for cp in mine:
            cp.wait()

    any_spec = pl.BlockSpec(memory_space=pl.ANY)
    return _pc(
        body, "gather_weights",
        in_specs=[any_spec] * na, out_specs=[any_spec] * na,
        out_shape=[_sds((N_DEV,) + a.shape, a.dtype) for a in arrays],
        scratch_shapes=[pltpu.SemaphoreType.DMA((7 * na,)), pltpu.SemaphoreType.DMA((7 * na,)),
                        pltpu.SemaphoreType.DMA((na,))],
    )(*arrays)


def _peer(k):
    x, y, c = _place()
    return (1 - x if k & 4 else x, 1 - y if k & 2 else y, 1 - c if k & 1 else c)


def _remote(src, dst, sems, s, to):
    return pltpu.make_async_remote_copy(src_ref=src, dst_ref=dst, send_sem=sems[0].at[s], recv_sem=sems[1].at[s],
                                        device_id=to, device_id_type=MESH)


def _gather_rider(shards):
    na = len(shards)

    def plan(ins, outs, *sems):
        x, y, c = _place()
        me = 4 * x + 2 * y + c
        locs = [pltpu.make_async_copy(ins[a], outs[a].at[me], sems[2].at[a]) for a in range(na)]
        sends, recvs = [], []
        for k in range(1, N_DEV):
            px, py, pc = _peer(k)
            for a in range(na):
                s = (k - 1) * na + a
                sends.append(_remote(ins[a], outs[a].at[me], sems, s, (px, py, pc)))
                recvs.append(_remote(ins[a], outs[a].at[4 * px + 2 * py + pc], sems, s, (px, py, pc)))
        return sends, recvs, locs

    return _Rider(shards, [_sds((N_DEV,) + a.shape, a.dtype) for a in shards], (N_DEV - 1) * na, na, plan)


def _direct_exchange_rider(blocks):
    na = len(blocks)

    def plan(ins, outs, *sems):
        x, y, c = _place()
        me = 4 * x + 2 * y + c
        locs = [pltpu.make_async_copy(ins[a].at[c, 2 * x + y], outs[a].at[me], sems[2].at[a]) for a in range(na)]
        sends, recvs = [], []
        for k in range(1, N_DEV):
            px, py, pc = _peer(k)
            for a in range(na):
                s = (k - 1) * na + a
                sends.append(_remote(ins[a].at[pc, 2 * px + py], outs[a].at[me], sems, s, (px, py, pc)))
                recvs.append(_remote(ins[a].at[pc, 2 * px + py], outs[a].at[4 * px + 2 * py + pc], sems, s, (px, py, pc)))
        return sends, recvs, locs

    return _Rider(blocks, [_sds((N_DEV,) + a.shape[2:], a.dtype) for a in blocks], (N_DEV - 1) * na, na, plan)


def _swap_rider(halves):
    na = len(halves)

    def plan(ins, outs, *sems):
        x, y, c = _place()
        cps = [_remote(ins[a].at[1 - c], outs[a], sems, a, (x, y, 1 - c)) for a in range(na)]
        return cps, cps, []

    return _Rider(halves, [_sds(a.shape[1:], a.dtype) for a in halves], na, 1, plan)


def _chip_exchange_rider(parts, small=None):
    na = len(parts)
    n_chip = N_DEV // 2

    def plan(ins, outs, *sems):
        x, y, c = _place()
        chip = 2 * x + y
        locs = [pltpu.make_async_copy(ins[a].at[chip], outs[a].at[chip], sems[2].at[a]) for a in range(na)]
        sends, recvs = [], []
        for k in range(1, n_chip):
            px, py, _ = _peer(2 * k)
            for a in range(na):
                s = (k - 1) * na + a
                sends.append(_remote(ins[a].at[2 * px + py], outs[a].at[chip], sems, s, (px, py, c)))
                recvs.append(_remote(ins[a].at[2 * px + py], outs[a].at[2 * px + py], sems, s, (px, py, c)))
        if small is not None:
            me = 2 * chip + c
            locs.append(pltpu.make_async_copy(ins[na], outs[na].at[me], sems[2].at[na]))
            for k in range(1, N_DEV):
                px, py, pc = _peer(k)
                s = (n_chip - 1) * na + k - 1
                sends.append(_remote(ins[na], outs[na].at[me], sems, s, (px, py, pc)))
                recvs.append(_remote(ins[na], outs[na].at[4 * px + 2 * py + pc], sems, s, (px, py, pc)))
        return sends, recvs, locs

    extra = [] if small is None else [small]
    shapes = [_sds(a.shape, a.dtype) for a in parts] + [_sds((N_DEV,) + s.shape, s.dtype) for s in extra]
    n_sems = (n_chip - 1) * na + (N_DEV - 1) * len(extra)
    return _Rider(list(parts) + extra, shapes, n_sems, na + len(extra), plan)


def _pair_add(halves, other, core, name):
    _, nch, r, c = halves.shape

    def body(c_ref, h_ref, o_ref, p_ref):
        p_ref[...] = (h_ref[...].astype(F32) + o_ref[...].astype(F32)).astype(BF16)

    blk = pl.BlockSpec((None, r, c), lambda j, c_ref: (j, 0, 0))
    return _pc(
        body, name,
        grid_spec=pltpu.PrefetchScalarGridSpec(
            num_scalar_prefetch=1, grid=(nch,),
            in_specs=[pl.BlockSpec((None, None, r, c), lambda j, c_ref: (c_ref[0], j, 0, 0)), blk], out_specs=blk),
        out_shape=_sds((nch, r, c), BF16),
        compiler_params=_params(),
    )(core, halves, other)


def _sum_adamw(parts, w, m, v, name, rider=None):
    nl, r, c = w.shape
    tr = 256 if r % 256 == 0 else r

    def body(*refs):
        p_refs = refs[:nl]
        w_ref, m_ref, v_ref, g_ref, d_ref, mo_ref, vo_ref = refs[nl:]
        for l in range(nl):
            @pl.when(pl.program_id(0) == l)
            def _(p_ref=p_refs[l]):
                g = p_ref[0].astype(F32)
                for j in range(1, p_ref.shape[0]):
                    g = g + p_ref[j].astype(F32)
                mn = ADAM_B1 * m_ref[...] + (1.0 - ADAM_B1) * g
                vn = ADAM_B2 * v_ref[...] + (1.0 - ADAM_B2) * (g * g)
                m_hat = mn / (1.0 - ADAM_B1 ** ADAM_STEP)
                v_hat = vn / (1.0 - ADAM_B2 ** ADAM_STEP)
                g_ref[...] = g
                d_ref[...] = -ADAM_LR * (m_hat / (jnp.sqrt(v_hat) + ADAM_EPS) + ADAM_WD * w_ref[...])
                mo_ref[...] = mn
                vo_ref[...] = vn

    def part_spec(l, k):
        return pl.BlockSpec((k, tr, c), lambda li, i: (0, jnp.where(li == l, i, 0), 0))

    row = pl.BlockSpec((None, tr, c), lambda li, i: (li, i, 0))
    return _call(
        body, name, (*parts, w, m, v), rider, grid=(nl, r // tr),
        in_specs=[part_spec(l, p.shape[0]) for l, p in enumerate(parts)] + [row, row, row],
        out_specs=[row] * 4,
        out_shape=[_sds((nl, r, c), F32)] * 4,
        compiler_params=_params(),
    )


SMALL = ("lower_bounds", "pre_norm_g", "hgrn_norm_g", "pool_w", "pool_scale", "post_norm_g", "fox_f_bias")
SMALL_LANES = 128


def _small_size(tree):
    return sum(tree[k].size for k in SMALL)


def _pack_small(tree, extra=None):
    flat = jnp.concatenate([tree[k].reshape(-1) for k in SMALL] + ([] if extra is None else [extra.reshape(1)]))
    rows = -(-(_small_size(tree) + 1) // (8 * SMALL_LANES)) * 8
    return jnp.pad(flat, (0, rows * SMALL_LANES - flat.shape[0])).reshape(rows, SMALL_LANES)


def _unpack_small(packed, like):
    out, off = {}, 0
    for k in SMALL:
        size = like[k].size
        assert off % SMALL_LANES == 0
        rows = packed[off // SMALL_LANES:-(-(off + size) // SMALL_LANES)]
        out[k] = rows.reshape(-1)[:size].reshape(like[k].shape)
        off += size
    return out


def _block_diag(pw):
    g = pw.shape[0]
    eye = jnp.eye(g, dtype=pw.dtype)
    return (eye[:, None, :, None] * pw[:, :, None, :]).reshape(g * HEAD, g * HEAD)


def _assemble_w_in(g_in, name):
    _, d, shard = g_in.shape
    tr = min(256, d)
    wide = MAIN_W + FC_PAD

    def body(g_ref, wm_ref, wf_ref, wmt_ref, wft_ref, row_s):
        row_s[:, MAIN_W:] = jnp.zeros((tr, FC_PAD), F32)
        for j in range(N_DEV):
            row_s[:, j * shard:(j + 1) * shard] = g_ref[j].astype(F32)
        wm_ref[...] = row_s[:, :MAIN_W].astype(BF16)
        wf_ref[...] = row_s[:, MAIN_W:].astype(BF16)
        for j in range(0, MAIN_W, 512):
            wmt_ref[j:j + 512, :] = row_s[:, j:j + 512].T.astype(BF16)
        wft_ref[...] = row_s[:, MAIN_W:].T.astype(BF16)

    return _pc(
        body, name, grid=(d // tr,),
        in_specs=[pl.BlockSpec((N_DEV, tr, shard), lambda i: (0, i, 0))],
        out_specs=[pl.BlockSpec((tr, MAIN_W), lambda i: (i, 0)), pl.BlockSpec((tr, FC_PAD), lambda i: (i, 0)),
                   pl.BlockSpec((MAIN_W, tr), lambda i: (0, i)), pl.BlockSpec((FC_PAD, tr), lambda i: (0, i))],
        out_shape=[_sds((d, MAIN_W), BF16), _sds((d, FC_PAD), BF16), _sds((MAIN_W, d), BF16), _sds((FC_PAD, d), BF16)],
        scratch_shapes=[pltpu.VMEM((tr, wide), F32)],
        compiler_params=_params(),
    )(g_in)


def _w_out_parts(g_out):
    full_out = g_out.reshape(N_DEV * g_out.shape[1], g_out.shape[2])
    return full_out, full_out.T


def _layer_fwd(l, x, lbs, weights, lw, nb, rider_h=None, rider_c=None, target=None):
    n = x.shape[0]
    t = n // nb
    w_main, w_fc, _, _, w_out, _ = lw
    bias = jnp.pad(weights["fox_f_bias"][l:l + 1], ((0, 0), (0, FC_PAD - FOX_HEADS)))
    wbd = _block_diag(weights["pool_w"][l]).astype(BF16)
    proj, fc, ht, qt, kt = _in_proj_fwd(x, weights["pre_norm_g"][l:l + 1], w_main, w_fc, f"in_proj_fwd_{l}")
    c_col, c_row = _fox_decay_fwd(fc, bias, nb, f"fox_decay_fwd_{l}")
    (o_h, s0), rode_h = _hgrn_fwd(proj, lbs[l:l + 1], _block_ones(HGRN_W, BF16), nb, f"hgrn_fwd_{l}", rider_h)
    o_b = _pool_fwd(proj, wbd, weights["pool_scale"][l:l + 1], nb, f"pool_fwd_{l}")
    (o_c, lse), rode_c = _fox_fwd(proj, kt, c_col, c_row, nb, f"fox_fwd_{l}", rider_c)
    if w_out is None:
        lw = tuple(lw[:4]) + _w_out_parts(rode_h[0])
        w_out, rode_h = lw[4], rode_h[1:]
    x_next, mixt, y = _merge_fwd(x, proj, o_h, o_b, o_c, weights["hgrn_norm_g"][l:l + 1], w_out,
                                 weights["post_norm_g"][l:l + 1], f"merge_fwd_{l}", target)
    return x_next, (x, proj, fc, ht, qt, kt, c_col, c_row, o_h, s0, o_c, lse, mixt, y, bias, wbd), lw, (rode_h, rode_c)


def _layer_bwd(l, dx, saved, lbs, weights, lw, nb, rider=None):
    _, proj, fc, ht, qt, kt, c_col, c_row, o_h, s0, o_c, lse, mixt, y, bias, wbd = saved
    n = proj.shape[0]
    t = n // nb
    w_out_t = lw[5]
    g = {}
    dy, dmix, dgp, da, dat, d_gc, delta = _merge_bwd(dx, y, weights["post_norm_g"][l:l + 1], w_out_t, proj, o_c,
                                                nb, f"merge_bwd_{l}")
    g["post_norm_g"] = dgp[0]
    g["w_out"] = _w_out_grad(mixt, dy, f"w_out_grad_{l}")
    d_a, dgh, dlb = _hgrn_bwd(dmix, proj, o_h, s0, weights["hgrn_norm_g"][l:l + 1], lbs[l:l + 1],
                              _block_ones(HGRN_W, BF16), nb, f"hgrn_bwd_{l}")
    g["hgrn_norm_g"], g["lbs"] = dgh[0], dlb[0]
    d_b, dwbd, dps = _pool_bwd(dmix, proj, wbd, wbd.T, weights["pool_scale"][l:l + 1], nb, f"pool_bwd_{l}")
    g["pool_w"] = jnp.stack([dwbd[j * HEAD:(j + 1) * HEAD, j * HEAD:(j + 1) * HEAD] for j in range(len(POOL_WINDOWS))])
    g["pool_scale"] = dps[0]
    (d_qc, d_kc, d_vc, dc_k, dc_q), rode = _fox_bwd(proj, qt, kt, da, dat, c_col, c_row, lse, delta, nb,
                                                    f"fox_bwd_{l}", rider)
    d_fc, dbias = _fox_decay_bwd(dc_q, dc_k, fc, bias, nb, f"fox_decay_bwd_{l}")
    g["fox_f_bias"] = dbias[0, :FOX_HEADS]
    pieces = [(d_a, C_QA), (d_b, C_UB), (d_qc, C_QC), (d_kc, C_KC), (d_vc, C_VC), (d_gc, C_GC), (d_fc, None)]
    g["w_in"] = _w_in_grad(ht, pieces, f"w_in_grad_{l}")
    return g, pieces, rode


def _layer_bwd_input(l, dx, pieces, saved, weights, lw, rider=None):
    (dxi, dgpre), rode = _in_proj_bwd(pieces, lw[2], lw[3], saved[0], weights["pre_norm_g"][l:l + 1], dx,
                                      f"in_proj_bwd_{l}", rider)
    return dxi, dgpre[0], rode


def kernel(x, lower_bounds, pre_norm_g, w_in, hgrn_norm_g, fox_f_bias, pool_w, pool_scale, w_out, post_norm_g, loss_target, m_lower_bounds, m_pre_norm_g, m_w_in, m_hgrn_norm_g, m_fox_f_bias, m_pool_w, m_pool_scale, m_w_out, m_post_norm_g, v_lower_bounds, v_pre_norm_g, v_w_in, v_hgrn_norm_g, v_fox_f_bias, v_pool_w, v_pool_scale, v_w_out, v_post_norm_g):
    weights = dict(lower_bounds=lower_bounds, pre_norm_g=pre_norm_g, hgrn_norm_g=hgrn_norm_g, fox_f_bias=fox_f_bias,
                   pool_w=pool_w, pool_scale=pool_scale, post_norm_g=post_norm_g)
    mom_m = dict(lower_bounds=m_lower_bounds, pre_norm_g=m_pre_norm_g, hgrn_norm_g=m_hgrn_norm_g, fox_f_bias=m_fox_f_bias,
                 pool_w=m_pool_w, pool_scale=m_pool_scale, post_norm_g=m_post_norm_g)
    mom_v = dict(lower_bounds=v_lower_bounds, pre_norm_g=v_pre_norm_g, hgrn_norm_g=v_hgrn_norm_g, fox_f_bias=v_fox_f_bias,
                 pool_w=v_pool_w, pool_scale=v_pool_scale, post_norm_g=v_post_norm_g)
    depth = w_in.shape[0]
    nb, t, d = x.shape
    n = nb * t
    core = lax.axis_index("c").astype(jnp.int32).reshape(1)
    shards = [(w_in[l].astype(BF16), w_out[l].astype(BF16)) for l in range(depth)]
    lbs = _lower_bound_table(lower_bounds, "lower_bound_table")

    (g_in,) = _gather_weights(shards[0][0])
    coming = tuple(_assemble_w_in(g_in, "assemble_w_in_0")) + (None, None)
    xl, saved, lw = x.reshape(n, d), [], []
    for l in range(depth):
        last = l + 1 == depth
        ride_h = ([shards[l][1]] if coming[4] is None else []) + ([] if last else [shards[l + 1][1]])
        xl, sv, lw_l, (rode_h, rode_c) = _layer_fwd(
            l, xl, lbs, weights, coming, nb, _gather_rider(ride_h) if ride_h else None,
            None if last else _gather_rider([shards[l + 1][0]]), loss_target.reshape(n, d) if last else None)
        saved.append(sv)
        lw.append(lw_l)
        if not last:
            coming = tuple(_assemble_w_in(rode_c[0], f"assemble_w_in_{l + 1}")) + _w_out_parts(rode_h[0])
    dx, sq = xl
    loss_here = 0.5 * jnp.sum(sq) / d

    grads, recv, pending = [None] * depth, [None] * depth, None
    for l in reversed(range(depth)):
        g, pieces, rode = _layer_bwd(l, dx, saved[l], lbs, weights, lw[l], nb, pending)
        if rode is not None:
            recv[l + 1] = rode
        blocks = (g["w_in"], g["w_out"])
        if l > 0:
            pending = _direct_exchange_rider(blocks)
            dx, g["pre_norm_g"], _ = _layer_bwd_input(l, dx, pieces, saved[l], weights, lw[l])
        else:
            other = _run_rider(_swap_rider(blocks), "grad_swap")
            summed = [_pair_add(hv, ot, core, f"grad_pair_add_{i}") for i, (hv, ot) in enumerate(zip(blocks, other))]
            dx, g["pre_norm_g"], recv[l] = _layer_bwd_input(l, dx, pieces, saved[l], weights, lw[l],
                                                            _chip_exchange_rider(summed))
        grads[l] = g
    small = {k: jnp.stack([grads[l][k] for l in range(depth)]) for k in SMALL if k != "lower_bounds"}
    small["lower_bounds"] = _lower_bound_bwd(lower_bounds, jnp.stack([grads[l]["lbs"] for l in range(depth)]),
                                             "lower_bound_bwd")
    (r_small,) = _run_rider(_gather_rider([_pack_small(small, loss_here)]), "small_grads_gather")

    res_in, _ = _sum_adamw([recv[l][0] for l in range(depth)], w_in, m_w_in, v_w_in, "adamw_w_in")
    res_out, _ = _sum_adamw([recv[l][1] for l in range(depth)], w_out, m_w_out, v_w_out, "adamw_w_out")
    res_small, _ = _sum_adamw([r_small], _pack_small(weights)[None], _pack_small(mom_m)[None], _pack_small(mom_v)[None],
                              "adamw_small")
    loss = res_small[0][0][_small_size(weights) // SMALL_LANES, _small_size(weights) % SMALL_LANES]

    names = ("lower_bounds", "pre_norm_g", "w_in", "hgrn_norm_g", "fox_f_bias", "pool_w", "pool_scale", "w_out", "post_norm_g")
    outs = [loss, dx.reshape(nb, t, d)]
    for i in range(4):
        full = dict(_unpack_small(res_small[i][0], weights), w_in=res_in[i], w_out=res_out[i])
        outs += [full[k] for k in names]
    return tuple(outs)
```

```python
import functools

import jax
import jax.numpy as jnp
from jax import lax
from jax.experimental import pallas as pl
from jax.experimental.pallas import tpu as pltpu

F32, BF16 = jnp.float32, jnp.bfloat16
HI = lax.Precision.HIGHEST
MESH = pl.DeviceIdType.MESH
N_DEV = 8

NORM_EPS = 1e-6
MASK_VALUE = -1e30
TINY = 1e-30
CHUNK = 64
SUB = 16
HGRN_W, POOL_W, FOX_W = 256, 256, 512
HEAD = 64
FOX_HEADS = 8
POOL_WINDOWS = (2, 4, 8, 16)
POOL_HALO = 16
MAIN_W = 3584
FC_PAD = 128
C_QA, C_FA, C_IA, C_GA, C_UB, C_GB, C_QC, C_KC, C_VC, C_GC = 0, 256, 512, 768, 1024, 1280, 1536, 2048, 2560, 3072
FOX_SCALE = HEAD ** -0.5

ADAM_LR, ADAM_B1, ADAM_B2, ADAM_EPS, ADAM_WD, ADAM_STEP = 0.001, 0.9, 0.999, 1e-08, 0.01, 10

VMEM_LIMIT = 56 * 1024 * 1024


def _pc(fn, name, **kw):
    return pl.pallas_call(fn, name=name, **kw)


def _params(**kw):
    return pltpu.CompilerParams(vmem_limit_bytes=VMEM_LIMIT, **kw)


class _Rider:
    def __init__(self, inputs, out_shapes, n_sems, n_local, plan):
        self.inputs, self.out_shapes, self.n_sems, self.n_local, self.plan = list(inputs), list(out_shapes), n_sems, n_local, plan

    def start(self, ins, outs, *sems):
        sends, _, locs = self.plan(ins, outs, *sems)
        for cp in locs + sends:
            cp.start()

    def wait(self, ins, outs, *sems):
        sends, recvs, locs = self.plan(ins, outs, *sems)
        for cp in recvs:
            cp.wait_recv()
        for cp in sends:
            cp.wait_send()
        for cp in locs:
            cp.wait()

    def sem_shapes(self):
        return [pltpu.SemaphoreType.DMA((self.n_sems,)), pltpu.SemaphoreType.DMA((self.n_sems,)),
                pltpu.SemaphoreType.DMA((self.n_local,))]


def _call(body, name, args, rider=None, *, grid, in_specs, out_specs, out_shape, scratch_shapes=(), **kw):
    if rider is None:
        res = _pc(body, name, grid=grid, in_specs=in_specs, out_specs=out_specs, out_shape=out_shape,
                  scratch_shapes=list(scratch_shapes), **kw)(*args)
        return res, None
    n_in, n_out, n_scr = len(in_specs), len(out_specs), len(scratch_shapes)
    n_rin, n_rout = len(rider.inputs), len(rider.out_shapes)

    def ridden(*refs):
        ins, refs = refs[:n_in], refs[n_in:]
        rins, refs = refs[:n_rin], refs[n_rin:]
        outs, refs = refs[:n_out], refs[n_out:]
        routs, refs = refs[:n_rout], refs[n_rout:]
        scr, sems = refs[:n_scr], refs[n_scr:]
        first = functools.reduce(jnp.logical_and, [pl.program_id(a) == 0 for a in range(len(grid))])
        last = functools.reduce(jnp.logical_and, [pl.program_id(a) == g - 1 for a, g in enumerate(grid)])

        @pl.when(first)
        def _():
            rider.start(rins, routs, *sems)

        body(*ins, *outs, *scr)

        @pl.when(last)
        def _():
            rider.wait(rins, routs, *sems)

    any_spec = pl.BlockSpec(memory_space=pl.ANY)
    res = _pc(ridden, name, grid=grid, in_specs=list(in_specs) + [any_spec] * n_rin,
              out_specs=list(out_specs) + [any_spec] * n_rout, out_shape=list(out_shape) + rider.out_shapes,
              scratch_shapes=list(scratch_shapes) + rider.sem_shapes(), **kw)(*args, *rider.inputs)
    return res[:n_out], res[n_out:]


def _run_rider(rider, name):
    n_rin = len(rider.inputs)

    def body(*refs):
        ins, outs, sems = refs[:n_rin], refs[n_rin:n_rin + len(rider.out_shapes)], refs[n_rin + len(rider.out_shapes):]
        rider.start(ins, outs, *sems)
        rider.wait(ins, outs, *sems)

    any_spec = pl.BlockSpec(memory_space=pl.ANY)
    return _pc(body, name, in_specs=[any_spec] * n_rin, out_specs=[any_spec] * len(rider.out_shapes),
               out_shape=rider.out_shapes, scratch_shapes=rider.sem_shapes())(*rider.inputs)


def _dot(a, b):
    return jnp.dot(a, b, preferred_element_type=F32)


def _dot_nt(a, b):
    return lax.dot_general(a, b, (((1,), (1,)), ((), ())), preferred_element_type=F32)


def _dot_tn(a, b):
    return lax.dot_general(a, b, (((0,), (0,)), ((), ())), preferred_element_type=F32)


def _dot_hi(a, b):
    return jnp.dot(a, b, precision=HI, preferred_element_type=F32)


def _split2(x):
    hi = x.astype(BF16)
    return hi, (x - hi.astype(F32)).astype(BF16)


def _sel_dot(sel, x):
    hi, lo = _split2(x)
    sb = sel.astype(BF16)
    return _dot(sb, hi) + _dot(sb, lo)


def _dot_sel(x, sel):
    hi, lo = _split2(x)
    sb = sel.astype(BF16)
    return _dot(hi, sb) + _dot(lo, sb)


def _sigmoid(x):
    return 1.0 / (1.0 + jnp.exp(-x))


def _block_ones(n, dtype):
    r = lax.broadcasted_iota(jnp.int32, (n, n), 0) // HEAD
    c = lax.broadcasted_iota(jnp.int32, (n, n), 1) // HEAD
    return (r == c).astype(dtype)


def _sds(shape, dtype):
    return jax.ShapeDtypeStruct(shape, dtype)


def _in_proj_fwd(x, g_pre, w_main, w_fc, name):
    n, d = x.shape
    tm = min(512, n)

    def body(x_ref, g_ref, w_ref, wf_ref, proj_ref, fc_ref, ht_ref, qt_ref, kt_ref):
        xv = x_ref[...]
        r = lax.rsqrt(jnp.mean(xv * xv, axis=-1, keepdims=True) + NORM_EPS)
        hf = xv * r * g_ref[...]
        hb = hf.astype(BF16)
        ht_ref[...] = hf.T.astype(BF16)
        for j in range(0, MAIN_W, FOX_W):
            res = _dot(hb, w_ref[:, j:j + FOX_W])
            proj_ref[:, j:j + FOX_W] = res
            if j == C_QC:
                qt_ref[...] = (res * FOX_SCALE).T.astype(BF16)
            if j == C_KC:
                kt_ref[...] = res.T.astype(BF16)
        fc_ref[...] = _dot(hb, wf_ref[...])

    def cols(rows):
        return pl.BlockSpec((rows, tm), lambda i: (0, i))

    return _pc(
        body, name, grid=(n // tm,),
        in_specs=[pl.BlockSpec((tm, d), lambda i: (i, 0)), pl.BlockSpec((1, d), lambda i: (0, 0)),
                  pl.BlockSpec((d, MAIN_W), lambda i: (0, 0)), pl.BlockSpec((d, FC_PAD), lambda i: (0, 0))],
        out_specs=[pl.BlockSpec((tm, MAIN_W), lambda i: (i, 0)), pl.BlockSpec((tm, FC_PAD), lambda i: (i, 0)),
                   cols(d), cols(FOX_W), cols(FOX_W)],
        out_shape=[_sds((n, MAIN_W), F32), _sds((n, FC_PAD), F32), _sds((d, n), BF16), _sds((FOX_W, n), BF16),
                   _sds((FOX_W, n), BF16)],
        compiler_params=_params(),
    )(x, g_pre, w_main, w_fc)


def _fox_decay_fwd(fc, bias, nb, name):
    n = fc.shape[0]
    t = n // nb
    tt = min(256, t)
    nt = t // tt

    def body(fc_ref, b_ref, c_ref, cr_ref, carry):
        i = pl.program_id(1)

        @pl.when(i == 0)
        def _():
            carry[...] = jnp.zeros_like(carry)

        xv = fc_ref[...] + b_ref[...]
        lf = jnp.minimum(xv, 0.0) - jnp.log(1.0 + jnp.exp(-jnp.abs(xv)))
        r = lax.broadcasted_iota(jnp.int32, (tt, tt), 0)
        cc = lax.broadcasted_iota(jnp.int32, (tt, tt), 1)
        cs = _dot_hi((r >= cc).astype(F32), lf) + carry[...]
        c_ref[...] = cs
        cr_ref[...] = cs.T[:FOX_HEADS, :]
        carry[...] = cs[tt - 1:tt, :]

    return _pc(
        body, name, grid=(nb, nt),
        in_specs=[pl.BlockSpec((tt, FC_PAD), lambda b, i: (b * nt + i, 0)), pl.BlockSpec((1, FC_PAD), lambda b, i: (0, 0))],
        out_specs=[pl.BlockSpec((tt, FC_PAD), lambda b, i: (b * nt + i, 0)),
                   pl.BlockSpec((None, FOX_HEADS, tt), lambda b, i: (b, 0, i))],
        out_shape=[_sds((n, FC_PAD), F32), _sds((nb, FOX_HEADS, t), F32)],
        scratch_shapes=[pltpu.VMEM((1, FC_PAD), F32)],
        compiler_params=_params(),
    )(fc, bias)


def _hgrn_gates(q, z, lb):
    sig = _sigmoid(z)
    sn = _sigmoid(-z)
    f = lb + (1.0 - lb) * sig
    g = jnp.log(jnp.maximum(f, TINY))
    k = (1.0 - lb) * sn
    sq = _sigmoid(q)
    return sig, sn, f, g, k, sq


def _sub_tri(n, lower):
    r = lax.broadcasted_iota(jnp.int32, (n, n), 0)
    c = lax.broadcasted_iota(jnp.int32, (n, n), 1)
    tri = (r >= c) if lower else (r <= c)
    return jnp.logical_and(r // SUB == c // SUB, tri).astype(F32)


def _live_rows(t):
    return 8 * (t // 8 + 1)


def _pad_rows(x):
    return x if x.shape[0] == SUB else jnp.concatenate([x, jnp.zeros((SUB - x.shape[0], x.shape[1]), x.dtype)], axis=0)


def _hgrn_decays(qs, k, b):
    srow = lax.broadcasted_iota(jnp.int32, (SUB, HGRN_W), 0)
    es, ws = [], []
    for t in range(SUB):
        r = _live_rows(t)
        e = jnp.where(srow[:r] <= t, jnp.exp(b[t:t + 1, :] - b[:r]), 0.0)
        es.append(e)
        ws.append(_pad_rows(e * (qs[t:t + 1, :] * k[:r])))
    return srow, es, ws


def _hgrn_state_step(st, k, v, b, bmask):
    bl = b[SUB - 1:SUB, :]
    ktil = k * jnp.exp(bl - b)
    return st * jnp.exp(bl) + _dot_tn(v.astype(BF16), ktil.astype(BF16)) * bmask


def _hgrn_tile(t):
    return min(256, t)


def _hgrn_fwd(proj, lb, ones_b, nb, name, rider=None):
    n = proj.shape[0]
    t = n // nb
    tt = _hgrn_tile(t)
    nt = t // tt
    ncs = tt // CHUNK
    w = HGRN_W

    def body(q_ref, z_ref, v_ref, lb_ref, ones_ref, o_ref, s0_ref, st_s, b_s, qs_s, k_s):
        @pl.when(pl.program_id(1) == 0)
        def _():
            st_s[...] = jnp.zeros_like(st_s)

        q = q_ref[...]
        _, _, _, g, k, sq = _hgrn_gates(q, z_ref[...], lb_ref[...])
        b_s[...] = _sel_dot(_sub_tri(tt, True), g)
        qs_s[...] = q * sq
        k_s[...] = k
        bmask = _block_ones(w, F32)
        ones_b = ones_ref[...]

        def chunk(c, carry):
            st = st_s[...]
            s0_ref[c] = st
            base = pl.multiple_of(c * CHUNK, CHUNK)
            tiles = []
            for u in range(CHUNK // SUB):
                rows = pl.ds(base + u * SUB, SUB)
                tiles.append((qs_s[rows, :], k_s[rows, :], v_ref[rows, :], b_s[rows, :]))
            aexps = []
            for qs, k, v, b in tiles:
                _, _, ws = _hgrn_decays(qs, k, b)
                aexps.append(_dot(jnp.concatenate(ws, axis=0).astype(BF16), ones_b))
            inters = []
            for qs, k, v, b in tiles:
                inters.append(_dot_nt((qs * jnp.exp(b)).astype(BF16), st.astype(BF16)))
                st = _hgrn_state_step(st, k, v, b, bmask)
            st_s[...] = st
            for u, ((qs, k, v, b), aexp, o) in enumerate(zip(tiles, aexps, inters)):
                for t in range(SUB):
                    r = _live_rows(t)
                    row = o[t:t + 1, :] + jnp.sum(aexp[t * SUB:t * SUB + r, :] * v[:r], axis=0, keepdims=True)
                    o_ref[pl.ds(base + u * SUB + t, 1), :] = row
            return carry

        lax.fori_loop(0, ncs, chunk, 0)

    def col(j):
        return pl.BlockSpec((tt, w), lambda b, i: (b * nt + i, j))

    return _call(
        body, name, (proj, proj, proj, lb, ones_b), rider, grid=(nb, nt),
        in_specs=[col(C_QA // w), col(C_FA // w), col(C_IA // w), pl.BlockSpec((1, w), lambda b, i: (0, 0)),
                  pl.BlockSpec((w, w), lambda b, i: (0, 0))],
        out_specs=[pl.BlockSpec((tt, w), lambda b, i: (b * nt + i, 0)),
                   pl.BlockSpec((ncs, w, w), lambda b, i: (b * nt + i, 0, 0))],
        out_shape=[_sds((n, w), F32), _sds((n // CHUNK, w, w), F32)],
        scratch_shapes=[pltpu.VMEM((w, w), F32)] + [pltpu.VMEM((tt, w), F32)] * 3,
        compiler_params=_params(),
    )


def _pool_lane_windows():
    lane = lax.broadcasted_iota(jnp.int32, (1, POOL_W), 1) // HEAD
    wl = jnp.zeros((1, POOL_W), F32)
    for gi, win in enumerate(POOL_WINDOWS):
        wl = jnp.where(lane == gi, float(win), wl)
    return lane, wl


def _pool_select(lane, parts):
    out = parts[-1]
    for gi in range(len(parts) - 2, -1, -1):
        out = jnp.where(lane == gi, parts[gi], out)
    return out


def _pool_mix(u, halo, t0, tt):
    lane, wl = _pool_lane_windows()
    ext = jnp.concatenate([halo, u], axis=0)
    sums, cur, shift = [], ext, 1
    for _ in POOL_WINDOWS:
        cur = cur + pltpu.roll(cur, shift, axis=0)
        sums.append(cur[POOL_HALO:, :])
        shift *= 2
    tpos = (t0 + lax.broadcasted_iota(jnp.int32, (tt, POOL_W), 0)).astype(F32)
    cnt = jnp.minimum(tpos + 1.0, wl)
    return _pool_select(lane, sums) / cnt - u, cnt


def _pool_specs(tt, nt, nhb):
    cu, cg = C_UB // POOL_W, C_GB // POOL_W
    return [pl.BlockSpec((tt, POOL_W), lambda b, i: (b * nt + i, cu)),
            pl.BlockSpec((tt, POOL_W), lambda b, i: (b * nt + i, cg)),
            pl.BlockSpec((POOL_HALO, POOL_W), lambda b, i: (jnp.maximum((b * nt + i) * nhb - 1, 0), cu))]


def _pool_fwd(proj, wbd, scale, nb, name):
    n = proj.shape[0]
    t = n // nb
    tt = min(512, t)
    nt = t // tt
    nhb = tt // POOL_HALO

    def body(u_ref, g_ref, h_ref, w_ref, s_ref, o_ref):
        i = pl.program_id(1)
        halo = jnp.where(i == 0, 0.0, h_ref[...])
        pooled, _ = _pool_mix(u_ref[...], halo, i * tt, tt)
        gv = g_ref[...]
        o_ref[...] = _dot(pooled.astype(BF16), w_ref[...]) * s_ref[...] * (gv * _sigmoid(gv))

    return _pc(
        body, name, grid=(nb, nt),
        in_specs=_pool_specs(tt, nt, nhb) + [pl.BlockSpec((POOL_W, POOL_W), lambda b, i: (0, 0)),
                                             pl.BlockSpec((1, POOL_W), lambda b, i: (0, 0))],
        out_specs=pl.BlockSpec((tt, POOL_W), lambda b, i: (b * nt + i, 0)),
        out_shape=_sds((n, POOL_W), F32),
        compiler_params=_params(),
    )(proj, proj, proj, wbd, scale)


def _rows_reduce(x, op, final):
    while x.shape[0] > 8 and x.shape[0] % 16 == 0:
        half = x.shape[0] // 2
        x = op(x[:half], x[half:])
    return final(x, axis=0, keepdims=True)


def _tri_pair(step, n, group=1):
    counts = [a // group + 1 for a in range(n)]
    firsts = [sum(counts[:a]) for a in range(1, n)]
    a = sum([(step >= f).astype(jnp.int32) for f in firsts], jnp.int32(0))
    first = sum([jnp.where(step >= f, c, 0) for f, c in zip(firsts, counts)], jnp.int32(0))
    return a, step - first


def _tri_steps(n, group=1):
    return sum(a // group + 1 for a in range(n))


def _lane_lo():
    return lax.broadcasted_iota(jnp.int32, (1, 2 * HEAD), 1) < HEAD


def _put_col(tile, hh, colv):
    lane = lax.broadcasted_iota(jnp.int32, tile.shape, 1)
    return jnp.where(lane == hh, colv, tile)


def _fox_fwd(proj, kt, c_col, c_row, nb, name, rider=None):
    n = proj.shape[0]
    t = n // nb
    tb = min(256, t)
    nq = t // tb
    pw = 2 * HEAD
    kw = 2 if nq % 2 == 0 else 1
    nk = nq // kw

    def body(q_ref, kt_ref, v_ref, cc_ref, cr_ref, o_ref, lse_ref, m_s, acc_s, cq_s):
        qi, kk = _tri_pair(pl.program_id(1), nq, kw)

        @pl.when(kk == 0)
        def _():
            m_s[...] = jnp.full_like(m_s, -jnp.inf)
            acc_s[...] = jnp.zeros_like(acc_s)
            for hh in range(FOX_HEADS):
                cq_s[hh] = jnp.broadcast_to(cc_ref[:, hh:hh + 1], (tb, pw))

        def block(masked, sub):
            lo = _lane_lo()
            keys = slice(sub * tb, (sub + 1) * tb)
            if masked:
                causal = lax.broadcasted_iota(jnp.int32, (tb, tb), 0) >= lax.broadcasted_iota(jnp.int32, (tb, tb), 1)
            def lanes(hh):
                return lo if hh % 2 == 0 else jnp.logical_not(lo)

            def scores(hh):
                sl = slice((hh // 2) * pw, (hh // 2 + 1) * pw)
                return _dot(jnp.where(lanes(hh), q_ref[:, sl] * FOX_SCALE, 0.0).astype(BF16), kt_ref[sl, keys])

            ahead = scores(0)
            for hh in range(FOX_HEADS):
                s = ahead
                if hh + 1 < FOX_HEADS:
                    ahead = scores(hh + 1)
                s = s + (jnp.tile(cq_s[hh], (1, tb // pw)) - cr_ref[hh:hh + 1, keys])
                if masked:
                    s = jnp.where(causal, s, MASK_VALUE)
                m_prev = m_s[hh]
                m_new = jnp.maximum(m_prev, jnp.max(s, axis=1, keepdims=True))
                alpha = jnp.exp(m_prev - m_new)
                pe = jnp.exp(s - jnp.tile(m_new, (1, tb // pw)))
                m_s[hh] = m_new
                vf = v_ref[keys, (hh // 2) * pw:(hh // 2 + 1) * pw]
                acc_s[hh] = alpha * acc_s[hh] + _dot(pe.astype(BF16), jnp.where(lanes(hh), vf, 1.0).astype(BF16))

        def finish():
            lo = _lane_lo()
            m_all, l_all = jnp.zeros((tb, FC_PAD), F32), jnp.ones((tb, FC_PAD), F32)
            for p in range(FOX_HEADS // 2):
                a0, a1 = acc_s[2 * p], acc_s[2 * p + 1]
                both = pltpu.roll(jnp.where(lo, a1, a0), HEAD, axis=1)
                o_ref[:, p * pw:(p + 1) * pw] = jnp.where(lo, a0, a1) / both
                m_all = _put_col(_put_col(m_all, 2 * p, m_s[2 * p]), 2 * p + 1, m_s[2 * p + 1])
                l_all = _put_col(_put_col(l_all, 2 * p, both), 2 * p + 1, a1)
            lse_ref[...] = (m_all + jnp.log(l_all)).T[:FOX_HEADS, :]

        for sub in range(kw):
            @pl.when(kk * kw + sub < qi)
            def _(sub=sub):
                block(False, sub)

            @pl.when(kk * kw + sub == qi)
            def _(sub=sub):
                block(True, sub)
                finish()

    def qspec(wd, j):
        return pl.BlockSpec((tb, wd), lambda b, st: (b * nq + _tri_pair(st, nq, kw)[0], j))

    def kspec(j):
        return pl.BlockSpec((kw * tb, FOX_W), lambda b, st: (b * nk + _tri_pair(st, nq, kw)[1], j))

    return _call(
        body, name, (proj, kt, proj, c_col, c_row), rider, grid=(nb, _tri_steps(nq, kw)),
        in_specs=[qspec(FOX_W, C_QC // FOX_W),
                  pl.BlockSpec((FOX_W, kw * tb), lambda b, st: (0, b * nk + _tri_pair(st, nq, kw)[1])),
                  kspec(C_VC // FOX_W), qspec(FC_PAD, 0),
                  pl.BlockSpec((None, FOX_HEADS, kw * tb), lambda b, st: (b, 0, _tri_pair(st, nq, kw)[1]))],
        out_specs=[qspec(FOX_W, 0),
                   pl.BlockSpec((None, FOX_HEADS, tb), lambda b, st: (b, 0, _tri_pair(st, nq, kw)[0]))],
        out_shape=[_sds((n, FOX_W), F32), _sds((nb, FOX_HEADS, t), F32)],
        scratch_shapes=[pltpu.VMEM((FOX_HEADS, tb, pw), F32), pltpu.VMEM((FOX_HEADS, tb, pw), F32),
                        pltpu.VMEM((FOX_HEADS, tb, pw), F32)],
        compiler_params=_params(),
    )


def _head_mean(x, ones_f):
    return _dot_sel(x, ones_f) * (1.0 / HEAD)


def _merge_fwd(x, proj, o_h, o_b, o_c, gh, w_out, g_post, name, target=None):
    n, d = x.shape
    tm = min(512, n)

    def body(*refs):
        x_ref, ga_ref, gc_ref, oh_ref, ob_ref, oc_ref, gh_ref, w_ref, gp_ref = refs[:9]
        if target is None:
            xo_ref, mixt_ref, y_ref = refs[9:]
        else:
            t_ref, dx_ref, sq_ref, mixt_ref, y_ref = refs[9:]
        oh = oh_ref[...]
        ones_f = _block_ones(HGRN_W, F32)
        na = oh * lax.rsqrt(_head_mean(oh * oh, ones_f) + NORM_EPS) * gh_ref[...]
        ga, gc = ga_ref[...], gc_ref[...]
        mixed = jnp.concatenate([na * (ga * _sigmoid(ga)), ob_ref[...], oc_ref[...] * (gc * _sigmoid(gc))], axis=1)
        mixt_ref[...] = mixed.T.astype(BF16)
        y = _dot(mixed.astype(BF16), w_ref[...])
        y_ref[...] = y
        xn = x_ref[...] + y * lax.rsqrt(jnp.mean(y * y, axis=-1, keepdims=True) + NORM_EPS) * gp_ref[...]
        if target is None:
            xo_ref[...] = xn
        else:
            @pl.when(pl.program_id(0) == 0)
            def _():
                sq_ref[...] = jnp.zeros_like(sq_ref)

            e = xn - t_ref[...]
            dx_ref[...] = e * (1.0 / d)
            sq_ref[...] += jnp.sum(e * e, axis=0, keepdims=True)

    def row(wd, j=0):
        return pl.BlockSpec((tm, wd), lambda i: (i, j))

    def full(a, b):
        return pl.BlockSpec((a, b), lambda i: (0, 0))

    head = [] if target is None else [target]
    res = _pc(
        body, name, grid=(n // tm,),
        in_specs=[row(d), row(HGRN_W, C_GA // HGRN_W), row(FOX_W, C_GC // FOX_W), row(HGRN_W), row(POOL_W), row(FOX_W),
                  full(1, HGRN_W), full(d, d), full(1, d)] + [row(d)] * len(head),
        out_specs=[row(d)] + [full(1, d)] * len(head) + [pl.BlockSpec((d, tm), lambda i: (0, i)), row(d)],
        out_shape=[_sds((n, d), F32)] + [_sds((1, d), F32)] * len(head) + [_sds((d, n), BF16), _sds((n, d), F32)],
        compiler_params=_params(),
    )(x, proj, proj, o_h, o_b, o_c, gh, w_out, g_post, *head)
    return (res[0], res[1], res[2]) if target is None else ((res[0], res[1]), res[2], res[3])


def _rms_bwd(dy_scaled, xhat, r):
    return r * (dy_scaled - xhat * jnp.mean(dy_scaled * xhat, axis=-1, keepdims=True))


def _merge_bwd(dxo, y, g_post, w_out_t, proj, o_c, nb, name):
    n, d = y.shape
    t = n // nb
    tm = min(512, t)
    nt = t // tm
    wab = HGRN_W + POOL_W

    def body(dx_ref, y_ref, gp_ref, wt_ref, gc_ref, oc_ref, dy_ref, dm_ref, dgp_ref, da_ref, dat_ref, dg_ref, dl_ref):
        @pl.when(pl.program_id(0) == 0)
        def _():
            dgp_ref[...] = jnp.zeros_like(dgp_ref)

        yv, dxv = y_ref[...], dx_ref[...]
        r = lax.rsqrt(jnp.mean(yv * yv, axis=-1, keepdims=True) + NORM_EPS)
        yh = yv * r
        dgp_ref[...] += jnp.sum(dxv * yh, axis=0, keepdims=True)
        dyb = _rms_bwd(dxv * gp_ref[...], yh, r).astype(BF16)
        dy_ref[...] = dyb
        dm_ref[...] = _dot(dyb, wt_ref[:, :wab])
        dmc = _dot(dyb, wt_ref[:, wab:])
        gc, oc = gc_ref[...], oc_ref[...]
        sg = _sigmoid(gc)
        da = dmc * (gc * sg)
        da_ref[...] = da.astype(BF16)
        dat_ref[...] = da.T.astype(BF16)
        dg_ref[...] = (dmc * oc * (sg * (1.0 + gc * (1.0 - sg)))).astype(BF16)
        rr = lax.broadcasted_iota(jnp.int32, (FOX_W, FC_PAD), 0) // HEAD
        cc = lax.broadcasted_iota(jnp.int32, (FOX_W, FC_PAD), 1)
        dl_ref[...] = _dot_sel(da * oc, (rr == cc).astype(F32)).T[:FOX_HEADS, :]

    def row(wd, j=0):
        return pl.BlockSpec((tm, wd), lambda i: (i, j))

    def full(a, b):
        return pl.BlockSpec((a, b), lambda i: (0, 0))

    return _pc(
        body, name, grid=(n // tm,),
        in_specs=[row(d), row(d), full(1, d), full(d, d), row(FOX_W, C_GC // FOX_W), row(FOX_W)],
        out_specs=[row(d), row(wab), full(1, d), row(FOX_W), pl.BlockSpec((FOX_W, tm), lambda i: (0, i)), row(FOX_W),
                   pl.BlockSpec((None, FOX_HEADS, tm), lambda i: (i // nt, 0, i % nt))],
        out_shape=[_sds((n, d), BF16), _sds((n, wab), F32), _sds((1, d), F32), _sds((n, FOX_W), BF16),
                   _sds((FOX_W, n), BF16), _sds((n, FOX_W), BF16), _sds((nb, FOX_HEADS, t), F32)],
        compiler_params=_params(),
    )(dxo, y, g_post, w_out_t, proj, o_c)


def _w_out_grad(mixt, dy, name):
    d, n = mixt.shape
    rows = d // N_DEV

    def body(a_ref, b_ref, o_ref):
        o_ref[...] = _dot(a_ref[...], b_ref[...]).astype(BF16)

    return _pc(
        body, name, grid=(N_DEV,),
        in_specs=[pl.BlockSpec((rows, n), lambda j: (j, 0)), pl.BlockSpec((n, d), lambda j: (0, 0))],
        out_specs=pl.BlockSpec((None, None, rows, d), lambda j: (j % 2, j // 2, 0, 0)),
        out_shape=_sds((2, N_DEV // 2, rows, d), BF16),
        compiler_params=_params(),
    )(mixt, dy)


def _w_in_grad(ht, pieces, name):
    d, n = ht.shape
    ta, tk = min(512, d), min(512, n)
    nk = n // tk
    arrays = [p for p, _ in pieces]
    widths = [p.shape[1] for p in arrays]
    offs = [sum(widths[:i]) for i in range(len(widths))]
    in_w = MAIN_W + FOX_HEADS
    shard = in_w // N_DEV

    def body(*refs):
        a_ref, p_refs = refs[0], refs[1:1 + len(arrays)]
        o_ref, acc = refs[1 + len(arrays):]
        k = pl.program_id(1)

        @pl.when(k == 0)
        def _():
            acc[...] = jnp.zeros_like(acc)

        a = a_ref[...]
        for pr, off, wd in zip(p_refs, offs, widths):
            for j in range(0, wd, 512):
                jw = min(512, wd - j)
                acc[:, off + j:off + j + jw] += _dot(a, pr[:, j:j + jw])

        @pl.when(k == nk - 1)
        def _():
            for j in range(N_DEV):
                o_ref[j % 2, j // 2] = acc[:, j * shard:(j + 1) * shard].astype(BF16)

    return _pc(
        body, name, grid=(d // ta, nk),
        in_specs=[pl.BlockSpec((ta, tk), lambda i, k: (i, k))] + [pl.BlockSpec((tk, wd), lambda i, k: (k, 0)) for wd in widths],
        out_specs=pl.BlockSpec((2, N_DEV // 2, ta, shard), lambda i, k: (0, 0, i, 0)),
        out_shape=_sds((2, N_DEV // 2, d, shard), BF16),
        scratch_shapes=[pltpu.VMEM((ta, sum(widths)), F32)],
        compiler_params=_params(),
    )(ht, *arrays)


def _hgrn_state_bwd(qs, k, v, b, do, s0, ds1, bmask):
    bl = b[SUB - 1:SUB, :]
    eb, ebl, ekt = jnp.exp(b), jnp.exp(bl), jnp.exp(bl - b)
    qe, ktil = qs * eb, k * ekt
    ds1b, dob = ds1.astype(BF16), do.astype(BF16)
    dv = _dot_nt(ktil.astype(BF16), ds1b)
    dqe = _dot(dob, s0.astype(BF16))
    dktil = _dot(v.astype(BF16), ds1b)
    dbl = jnp.sum(dktil * ktil, axis=0, keepdims=True) + ebl * jnp.sum(s0 * ds1, axis=0, keepdims=True)
    ds0 = ds1 * ebl + _dot_tn(dob, qe.astype(BF16)) * bmask
    return dqe * eb, dktil * ekt, dv, dbl, ds0


def _hgrn_intra_bwd(qs, k, v, do, es, aexp, gexp, dq, dk, dv, put_dq_row):
    dks = [dk[j:j + 8] for j in range(0, SUB, 8)]
    dvs = [dv[j:j + 8] for j in range(0, SUB, 8)]
    for t in range(SUB):
        r = _live_rows(t)
        ge = gexp[t * SUB:t * SUB + r, :] * es[t]
        put_dq_row(t, dq[t:t + 1, :] + jnp.sum(ge * k[:r], axis=0, keepdims=True))
        for j in range(r // 8):
            dks[j] = dks[j] + ge[8 * j:8 * j + 8] * qs[t:t + 1, :]
            dvs[j] = dvs[j] + aexp[t * SUB + 8 * j:t * SUB + 8 * j + 8, :] * do[t:t + 1, :]
    return jnp.concatenate(dks, axis=0), jnp.concatenate(dvs, axis=0)


def _hgrn_bwd(dmix, proj, o_h, s0, gh, lb, ones_b, nb, name):
    n = proj.shape[0]
    t = n // nb
    tt = _hgrn_tile(t)
    nt = t // tt
    ncs = tt // CHUNK
    nsub = CHUNK // SUB
    w = HGRN_W

    def body(dm_ref, q_ref, z_ref, v_ref, ga_ref, oh_ref, s0_ref, gh_ref, lb_ref, ones_ref,
             dp_ref, dgh_ref, dlb_ref, ds_s, ss_s, b_s, qs_s, k_s, do_s, dq_s, dk_s, dv_s, dbl_s):
        first = jnp.logical_and(pl.program_id(0) == 0, pl.program_id(1) == 0)

        @pl.when(first)
        def _():
            dgh_ref[...] = jnp.zeros_like(dgh_ref)
            dlb_ref[...] = jnp.zeros_like(dlb_ref)

        @pl.when(pl.program_id(1) == 0)
        def _():
            ds_s[...] = jnp.zeros_like(ds_s)

        ones_b = ones_ref[...]
        ones_f = ones_b.astype(F32)
        bmask = _block_ones(w, F32)
        lbv, ghv = lb_ref[...], gh_ref[...]
        oh, ga, dm = oh_ref[...], ga_ref[...], dm_ref[...]
        rn = lax.rsqrt(_head_mean(oh * oh, ones_f) + NORM_EPS)
        nh = oh * rn
        sga = _sigmoid(ga)
        dp_ref[:, 3 * w:4 * w] = (dm * nh * ghv * (sga * (1.0 + ga * (1.0 - sga)))).astype(BF16)
        dn = dm * (ga * sga)
        dgh_ref[...] += jnp.sum(dn * nh, axis=0, keepdims=True)
        dn = dn * ghv
        do_s[...] = rn * (dn - nh * _head_mean(dn * nh, ones_f))
        q = q_ref[...]
        sig, sn, f, g, k, sq = _hgrn_gates(q, z_ref[...], lbv)
        qs = q * sq
        b_s[...] = _sel_dot(_sub_tri(tt, True), g)
        qs_s[...] = qs
        k_s[...] = k

        def chunk(cc, carry):
            c = ncs - 1 - cc
            base = pl.multiple_of(c * CHUNK, CHUNK)
            tiles = []
            for u in range(nsub):
                rows = pl.ds(base + u * SUB, SUB)
                tiles.append((qs_s[rows, :], k_s[rows, :], v_ref[rows, :], b_s[rows, :], do_s[rows, :]))
            st = s0_ref[c]
            for u, (qs, k, v, b, do) in enumerate(tiles):
                ss_s[u] = st
                if u < nsub - 1:
                    st = _hgrn_state_step(st, k, v, b, bmask)
            ds = ds_s[...]
            for u in reversed(range(nsub)):
                qs, k, v, b, do = tiles[u]
                _, es, ws = _hgrn_decays(qs, k, b)
                gs = [_pad_rows(do[t:t + 1, :] * v[:_live_rows(t)]) for t in range(SUB)]
                aexp = _dot(jnp.concatenate(ws, axis=0).astype(BF16), ones_b)
                gexp = _dot(jnp.concatenate(gs, axis=0).astype(BF16), ones_b)
                dq, dk, dv, dbl, ds = _hgrn_state_bwd(qs, k, v, b, do, ss_s[u], ds, bmask)

                def put_dq_row(i, row, r0=base + u * SUB):
                    dq_s[pl.ds(r0 + i, 1), :] = row

                dk, dv = _hgrn_intra_bwd(qs, k, v, do, es, aexp, gexp, dq, dk, dv, put_dq_row)
                dk_s[pl.ds(base + u * SUB, SUB), :] = dk
                dv_s[pl.ds(base + u * SUB, SUB), :] = dv
                dbl_s[pl.ds(base + u * SUB, SUB), :] = jnp.broadcast_to(dbl, (SUB, w))
            ds_s[...] = ds
            return carry

        lax.fori_loop(0, ncs, chunk, 0)
        dqs, dk = dq_s[...], dk_s[...]
        dg = _sel_dot(_sub_tri(tt, False), qs * dqs - k * dk) + dbl_s[...]
        dfz = jnp.where(f > TINY, dg / jnp.maximum(f, TINY), 0.0)
        dlb_ref[...] += jnp.sum(dfz * (1.0 - sig) - dk * sn, axis=0, keepdims=True)
        dp_ref[:, 0:w] = (dqs * (sq * (1.0 + q * (1.0 - sq)))).astype(BF16)
        dp_ref[:, w:2 * w] = ((dfz - dk) * (1.0 - lbv) * sig * sn).astype(BF16)
        dp_ref[:, 2 * w:3 * w] = dv_s[...].astype(BF16)

    def rv(b, i):
        return b * nt + (nt - 1 - i)

    def col(j):
        return pl.BlockSpec((tt, w), lambda b, i: (rv(b, i), j))

    def full(a, bb):
        return pl.BlockSpec((a, bb), lambda b, i: (0, 0))

    return _pc(
        body, name, grid=(nb, nt),
        in_specs=[col(0), col(C_QA // w), col(C_FA // w), col(C_IA // w), col(C_GA // w), col(0),
                  pl.BlockSpec((ncs, w, w), lambda b, i: (rv(b, i), 0, 0)), full(1, w), full(1, w), full(w, w)],
        out_specs=[pl.BlockSpec((tt, 4 * w), lambda b, i: (rv(b, i), 0)), full(1, w), full(1, w)],
        out_shape=[_sds((n, 4 * w), BF16), _sds((1, w), F32), _sds((1, w), F32)],
        scratch_shapes=[pltpu.VMEM((w, w), F32), pltpu.VMEM((nsub, w, w), F32)] + [pltpu.VMEM((tt, w), F32)] * 8,
        compiler_params=_params(),
    )(dmix, proj, proj, proj, proj, o_h, s0, gh, lb, ones_b)


def _pool_bwd(dmix, proj, wbd, wbd_t, scale, nb, name):
    n = proj.shape[0]
    t = n // nb
    tt = min(512, t)
    nt = t // tt
    nhb = tt // POOL_HALO
    cu, cg, cm = C_UB // POOL_W, C_GB // POOL_W, HGRN_W // POOL_W

    def body(u_ref, g_ref, h_ref, dm_ref, gn_ref, dmn_ref, w_ref, wt_ref, s_ref, dp_ref, dw_ref, ds_ref):
        i = pl.program_id(1)
        first = jnp.logical_and(pl.program_id(0) == 0, i == 0)

        @pl.when(first)
        def _():
            dw_ref[...] = jnp.zeros_like(dw_ref)
            ds_ref[...] = jnp.zeros_like(ds_ref)

        sc = s_ref[...]
        halo = jnp.where(i == 0, 0.0, h_ref[...])
        pooled, cnt = _pool_mix(u_ref[...], halo, i * tt, tt)
        pb = pooled.astype(BF16)
        pre = _dot(pb, w_ref[...])
        gv, dm = g_ref[...], dm_ref[...]
        sg = _sigmoid(gv)
        silu = gv * sg
        dgb = dm * pre * sc * (sg * (1.0 + gv * (1.0 - sg)))
        ds_ref[...] += jnp.sum(dm * pre * silu, axis=0, keepdims=True)
        dpre = (dm * sc * silu).astype(BF16)
        dw_ref[...] += _dot_tn(pb, dpre)
        dpool = _dot(dpre, wt_ref[...])
        gn = gn_ref[...]
        dpre_n = (dmn_ref[...] * sc * (gn * _sigmoid(gn))).astype(BF16)
        dpool_n = jnp.where(i == nt - 1, 0.0, _dot(dpre_n, wt_ref[...]))
        lane, wl = _pool_lane_windows()
        tpos_n = ((i + 1) * tt + lax.broadcasted_iota(jnp.int32, (POOL_HALO, POOL_W), 0)).astype(F32)
        ext = jnp.concatenate([dpool / cnt, dpool_n / jnp.minimum(tpos_n + 1.0, wl)], axis=0)
        rows = tt + POOL_HALO
        sums, cur, shift = [], ext, 1
        for _ in POOL_WINDOWS:
            cur = cur + pltpu.roll(cur, rows - shift, axis=0)
            sums.append(cur[:tt, :])
            shift *= 2
        du = _pool_select(lane, sums) - dpool
        dp_ref[...] = jnp.concatenate([du, dgb], axis=1).astype(BF16)

    def nxt(b, i):
        return jnp.minimum((b * nt + i + 1) * nhb, n // POOL_HALO - 1)

    return _pc(
        body, name, grid=(nb, nt),
        in_specs=_pool_specs(tt, nt, nhb) + [
            pl.BlockSpec((tt, POOL_W), lambda b, i: (b * nt + i, cm)),
            pl.BlockSpec((POOL_HALO, POOL_W), lambda b, i: (nxt(b, i), cg)),
            pl.BlockSpec((POOL_HALO, POOL_W), lambda b, i: (nxt(b, i), cm)),
            pl.BlockSpec((POOL_W, POOL_W), lambda b, i: (0, 0)), pl.BlockSpec((POOL_W, POOL_W), lambda b, i: (0, 0)),
            pl.BlockSpec((1, POOL_W), lambda b, i: (0, 0))],
        out_specs=[pl.BlockSpec((tt, 2 * POOL_W), lambda b, i: (b * nt + i, 0)),
                   pl.BlockSpec((POOL_W, POOL_W), lambda b, i: (0, 0)), pl.BlockSpec((1, POOL_W), lambda b, i: (0, 0))],
        out_shape=[_sds((n, 2 * POOL_W), BF16), _sds((POOL_W, POOL_W), F32), _sds((1, POOL_W), F32)],
        compiler_params=_params(),
    )(proj, proj, proj, dmix, proj, dmix, wbd, wbd_t, scale)


def _fox_bwd(proj, qt, kt, da, dat, c_col, c_row, lse_row, delta_row, nb, name, rider=None):
    n = proj.shape[0]
    t = n // nb
    tb = min(256, t)
    nq = t // tb
    pw = 2 * HEAD
    qw = 2 if nq % 2 == 0 else 1
    nqs = nq // qw

    def body(q_ref, k_ref, v_ref, da_ref, qt_ref, kt_ref, dat_ref, cc_ref, cr_ref, lse_ref, dl_ref,
             dq_ref, dk_ref, dv_ref, dck_ref, dcq_ref, dq_s, dk_s, dv_s, dck_s, dcq_s):
        step = pl.program_id(1)
        kj, qq = pairs(step)

        @pl.when(step == 0)
        def _():
            dq_s[...] = jnp.zeros_like(dq_s)
            dcq_s[...] = jnp.zeros_like(dcq_s)

        @pl.when(qq == nqs - 1)
        def _():
            dk_s[...] = jnp.zeros_like(dk_s)
            dv_s[...] = jnp.zeros_like(dv_s)
            dck_s[...] = jnp.zeros_like(dck_s)

        def block(masked, sub):
            lo = _lane_lo()
            qi = qq * qw + sub
            qs = slice(sub * tb, (sub + 1) * tb)
            if masked:
                causal = lax.broadcasted_iota(jnp.int32, (tb, tb), 1) >= lax.broadcasted_iota(jnp.int32, (tb, tb), 0)
            dck = dck_s[...]
            for p in range(FOX_HEADS // 2):
                sl = slice(p * pw, (p + 1) * pw)
                qp = q_ref[qs, sl] * FOX_SCALE
                kp = k_ref[:, sl].astype(BF16)
                vp = v_ref[:, sl].astype(BF16)
                dap = da_ref[qs, sl]
                dk, dv = dk_s[:, sl], dv_s[:, sl]
                for h in range(2):
                    hh = 2 * p + h
                    lm = lo if h == 0 else jnp.logical_not(lo)
                    rows = slice(hh * HEAD, (hh + 1) * HEAD)
                    none = jnp.zeros((HEAD, tb), BF16)
                    qm = jnp.where(lm, qp, 0.0).astype(BF16)
                    dam = jnp.where(lm, dap, jnp.zeros_like(dap))
                    qtm = jnp.concatenate([qt_ref[rows, qs], none] if h == 0 else [none, qt_ref[rows, qs]], axis=0)
                    datm = jnp.concatenate([dat_ref[rows, qs], none] if h == 0 else [none, dat_ref[rows, qs]], axis=0)
                    s = _dot(kp, qtm) + (cr_ref[hh:hh + 1, qs] - cc_ref[:, hh:hh + 1])
                    pe = jnp.exp(s - lse_ref[hh:hh + 1, qs])
                    if masked:
                        pe = jnp.where(causal, pe, 0.0)
                    dp = _dot(vp, datm)
                    ds = pe * (dp - dl_ref[hh:hh + 1, qs])
                    dsb = ds.astype(BF16)
                    dv = dv + _dot(pe.astype(BF16), dam)
                    dk = dk + _dot(dsb, qm)
                    dq_s[qi, rows, :] += _dot(kt_ref[rows, :], dsb)
                    dck = dck - _put_col(jnp.zeros_like(dck), hh, jnp.sum(ds, axis=1, keepdims=True))
                    dcq_s[qi, hh:hh + 1, :] += _rows_reduce(ds, jnp.add, jnp.sum)
                dk_s[:, sl] = dk
                dv_s[:, sl] = dv
            dck_s[...] = dck

        for sub in reversed(range(qw)):
            @pl.when(qq * qw + sub > kj)
            def _(sub=sub):
                block(False, sub)

            @pl.when(qq * qw + sub == kj)
            def _(sub=sub):
                block(True, sub)

        @pl.when(qq == kj // qw)
        def _():
            dk_ref[...] = dk_s[...].astype(BF16)
            dv_ref[...] = dv_s[...].astype(BF16)
            dck_ref[...] = dck_s[...]

        @pl.when(step == _tri_steps(nq, qw) - 1)
        def _():
            for j in range(nq):
                dq_ref[j * tb:(j + 1) * tb, :] = (dq_s[j].T * FOX_SCALE).astype(BF16)
                dcq_ref[:, j * tb:(j + 1) * tb] = dcq_s[j]

    def pairs(step):
        a, b = _tri_pair(step, nq, qw)
        return nq - 1 - a, nqs - 1 - b

    def kspec(wd, j=0):
        return pl.BlockSpec((tb, wd), lambda b, st: (b * nq + pairs(st)[0], j))

    def qspec(wd, j=0):
        return pl.BlockSpec((qw * tb, wd), lambda b, st: (b * nqs + pairs(st)[1], j))

    def qrow():
        return pl.BlockSpec((None, FOX_HEADS, qw * tb), lambda b, st: (b, 0, pairs(st)[1]))

    def tspec(which):
        if which == 0:
            return pl.BlockSpec((FOX_W, tb), lambda b, st: (0, b * nq + pairs(st)[0]))
        return pl.BlockSpec((FOX_W, qw * tb), lambda b, st: (0, b * nqs + pairs(st)[1]))

    return _call(
        body, name, (proj, proj, proj, da, qt, kt, dat, c_col, c_row, lse_row, delta_row), rider,
        grid=(nb, _tri_steps(nq, qw)),
        in_specs=[qspec(FOX_W, C_QC // FOX_W), kspec(FOX_W, C_KC // FOX_W), kspec(FOX_W, C_VC // FOX_W), qspec(FOX_W),
                  tspec(1), tspec(0), tspec(1), kspec(FC_PAD), qrow(), qrow(), qrow()],
        out_specs=[pl.BlockSpec((t, FOX_W), lambda b, st: (b, 0)), kspec(FOX_W), kspec(FOX_W), kspec(FC_PAD),
                   pl.BlockSpec((None, FOX_HEADS, t), lambda b, st: (b, 0, 0))],
        out_shape=[_sds((n, FOX_W), BF16), _sds((n, FOX_W), BF16), _sds((n, FOX_W), BF16), _sds((n, FC_PAD), F32),
                   _sds((nb, FOX_HEADS, t), F32)],
        scratch_shapes=[pltpu.VMEM((nq, FOX_W, tb), F32), pltpu.VMEM((tb, FOX_W), F32), pltpu.VMEM((tb, FOX_W), F32),
                        pltpu.VMEM((tb, FC_PAD), F32), pltpu.VMEM((nq, FOX_HEADS, tb), F32)],
        compiler_params=_params(),
    )


def _fox_decay_bwd(dc_q, dc_k, fc, bias, nb, name):
    n = fc.shape[0]
    t = n // nb
    tt = min(256, t)
    nt = t // tt

    def body(dcq_ref, dck_ref, fc_ref, b_ref, dfc_ref, db_ref, carry):
        i = pl.program_id(1)
        first = jnp.logical_and(pl.program_id(0) == 0, i == 0)

        @pl.when(first)
        def _():
            db_ref[...] = jnp.zeros_like(db_ref)

        @pl.when(i == 0)
        def _():
            carry[...] = jnp.zeros_like(carry)

        r = lax.broadcasted_iota(jnp.int32, (tt, tt), 0)
        cc = lax.broadcasted_iota(jnp.int32, (tt, tt), 1)
        dcq = jnp.concatenate([dcq_ref[...], jnp.zeros((FC_PAD - FOX_HEADS, tt), F32)], axis=0).T
        dlf = _dot_hi((r <= cc).astype(F32), dcq + dck_ref[...]) + carry[...]
        carry[...] = dlf[0:1, :]
        dfc = dlf * _sigmoid(-(fc_ref[...] + b_ref[...]))
        dfc_ref[...] = dfc.astype(BF16)
        db_ref[...] += jnp.sum(dfc, axis=0, keepdims=True)

    def row():
        return pl.BlockSpec((tt, FC_PAD), lambda b, i: (b * nt + (nt - 1 - i), 0))

    return _pc(
        body, name, grid=(nb, nt),
        in_specs=[pl.BlockSpec((None, FOX_HEADS, tt), lambda b, i: (b, 0, nt - 1 - i)), row(), row(),
                  pl.BlockSpec((1, FC_PAD), lambda b, i: (0, 0))],
        out_specs=[row(), pl.BlockSpec((1, FC_PAD), lambda b, i: (0, 0))],
        out_shape=[_sds((n, FC_PAD), BF16), _sds((1, FC_PAD), F32)],
        scratch_shapes=[pltpu.VMEM((1, FC_PAD), F32)],
        compiler_params=_params(),
    )(dc_q, dc_k, fc, bias)


def _in_proj_bwd(pieces, w_main_t, w_fc_t, x, g_pre, dxo, name, rider=None):
    n, d = x.shape
    tm = min(512, n)
    widths = [p.shape[1] for p, _ in pieces]
    offs = [o for _, o in pieces]
    np_ = len(pieces)

    def body(*refs):
        p_refs = refs[:np_]
        wt_ref, wf_ref, x_ref, g_ref, dxo_ref, dx_ref, dg_ref = refs[np_:]

        @pl.when(pl.program_id(0) == 0)
        def _():
            dg_ref[...] = jnp.zeros_like(dg_ref)

        dh = _dot(p_refs[-1][...], wf_ref[...])
        for pr, wd, off in zip(p_refs[:-1], widths[:-1], offs[:-1]):
            for j in range(0, wd, 512):
                jw = min(512, wd - j)
                dh = dh + _dot(pr[:, j:j + jw], wt_ref[off + j:off + j + jw, :])
        xv = x_ref[...]
        r = lax.rsqrt(jnp.mean(xv * xv, axis=-1, keepdims=True) + NORM_EPS)
        xh = xv * r
        dg_ref[...] += jnp.sum(dh * xh, axis=0, keepdims=True)
        dx_ref[...] = dxo_ref[...] + _rms_bwd(dh * g_ref[...], xh, r)

    row = pl.BlockSpec((tm, d), lambda i: (i, 0))
    return _call(
        body, name, (*[p for p, _ in pieces], w_main_t, w_fc_t, x, g_pre, dxo), rider, grid=(n // tm,),
        in_specs=[pl.BlockSpec((tm, wd), lambda i: (i, 0)) for wd in widths] + [
            pl.BlockSpec((MAIN_W, d), lambda i: (0, 0)), pl.BlockSpec((FC_PAD, d), lambda i: (0, 0)),
            row, pl.BlockSpec((1, d), lambda i: (0, 0)), row],
        out_specs=[row, pl.BlockSpec((1, d), lambda i: (0, 0))],
        out_shape=[_sds((n, d), F32), _sds((1, d), F32)],
        compiler_params=_params(),
    )


def _lower_bound_table(lower_bounds, name):
    depth, w = lower_bounds.shape

    def body(lb_ref, o_ref):
        v = lb_ref[...]
        e = jnp.exp(v - jnp.max(v, axis=0, keepdims=True))
        p = e / jnp.sum(e, axis=0, keepdims=True)
        acc = jnp.zeros((1, w), F32)
        for l in range(depth):
            acc = acc + p[l:l + 1, :]
            o_ref[l:l + 1, :] = acc - p[0:1, :]

    return _pc(body, name, out_shape=_sds((depth, w), F32))(lower_bounds)


def _lower_bound_bwd(lower_bounds, dlbs, name):
    depth, w = lower_bounds.shape

    def body(lb_ref, d_ref, o_ref):
        v, dl = lb_ref[...], d_ref[...]
        e = jnp.exp(v - jnp.max(v, axis=0, keepdims=True))
        p = e / jnp.sum(e, axis=0, keepdims=True)
        tot = jnp.sum(dl, axis=0, keepdims=True)
        rows, tail = [], tot
        for l in range(depth):
            rows.append(tail - tot if l == 0 else tail)
            tail = tail - dl[l:l + 1, :]
        dp = jnp.concatenate(rows, axis=0)
        o_ref[...] = p * (dp - jnp.sum(p * dp, axis=0, keepdims=True))

    return _pc(body, name, out_shape=_sds((depth, w), F32))(lower_bounds, dlbs)


def _place():
    x, y, c = lax.axis_index("x"), lax.axis_index("y"), lax.axis_index("c")
    return x, y, c


def _gather_weights(*arrays):
    na = len(arrays)

    def body(*refs):
        ins, outs = refs[:na], refs[na:2 * na]
        send_sems, recv_sems, local_sems = refs[2 * na:]
        x, y, c = _place()
        me, sibling = (x, y, c), (x, y, 1 - c)
        chips = [(1 - x, y), (x, 1 - y), (1 - x, 1 - y)]

        def slot(a, px, py, pc):
            return outs[a].at[4 * px + 2 * py + pc]

        def copy(a, k, block, to, own=False):
            return pltpu.make_async_remote_copy(
                src_ref=ins[a] if own else slot(a, *block), dst_ref=slot(a, *block),
                send_sem=send_sems.at[a * 7 + k], recv_sem=recv_sems.at[a * 7 + k],
                device_id=to, device_id_type=MESH)

        mine = [pltpu.make_async_copy(ins[a], slot(a, *me), local_sems.at[a]) for a in range(na)]
        for cp in mine:
            cp.start()
        first = []
        for a in range(na):
            first.append(copy(a, 0, me, sibling, own=True))
            first += [copy(a, 1 + j, me, (*chip, c), own=True) for j, chip in enumerate(chips)]
        for cp in first:
            cp.start()
        passed = []
        for j, chip in enumerate(chips):
            for a in range(na):
                copy(a, 1 + j, (*chip, c), me).wait_recv()
                fw = copy(a, 4 + j, (*chip, c), sibling)
                fw.start()
                passed.append(fw)
        for a in range(na):
            copy(a, 0, sibling, me).wait_recv()
            for j, chip in enumerate(chips):
                copy(a, 4 + j, (*chip, 1 - c), me).wait_recv()
        for cp in first + passed:
            cp.wait_send()
        for cp in mine:
            cp.wait()

    any_spec = pl.BlockSpec(memory_space=pl.ANY)
    return _pc(
        body, "gather_weights",
        in_specs=[any_spec] * na, out_specs=[any_spec] * na,
        out_shape=[_sds((N_DEV,) + a.shape, a.dtype) for a in arrays],
        scratch_shapes=[pltpu.SemaphoreType.DMA((7 * na,)), pltpu.SemaphoreType.DMA((7 * na,)),
                        pltpu.SemaphoreType.DMA((na,))],
    )(*arrays)


def _peer(k):
    x, y, c = _place()
    return (1 - x if k & 4 else x, 1 - y if k & 2 else y, 1 - c if k & 1 else c)


def _remote(src, dst, sems, s, to):
    return pltpu.make_async_remote_copy(src_ref=src, dst_ref=dst, send_sem=sems[0].at[s], recv_sem=sems[1].at[s],
                                        device_id=to, device_id_type=MESH)


def _gather_rider(shards):
    na = len(shards)

    def plan(ins, outs, *sems):
        x, y, c = _place()
        me = 4 * x + 2 * y + c
        locs = [pltpu.make_async_copy(ins[a], outs[a].at[me], sems[2].at[a]) for a in range(na)]
        sends, recvs = [], []
        for k in range(1, N_DEV):
            px, py, pc = _peer(k)
            for a in range(na):
                s = (k - 1) * na + a
                sends.append(_remote(ins[a], outs[a].at[me], sems, s, (px, py, pc)))
                recvs.append(_remote(ins[a], outs[a].at[4 * px + 2 * py + pc], sems, s, (px, py, pc)))
        return sends, recvs, locs

    return _Rider(shards, [_sds((N_DEV,) + a.shape, a.dtype) for a in shards], (N_DEV - 1) * na, na, plan)


def _direct_exchange_rider(blocks):
    na = len(blocks)

    def plan(ins, outs, *sems):
        x, y, c = _place()
        me = 4 * x + 2 * y + c
        locs = [pltpu.make_async_copy(ins[a].at[c, 2 * x + y], outs[a].at[me], sems[2].at[a]) for a in range(na)]
        sends, recvs = [], []
        for k in range(1, N_DEV):
            px, py, pc = _peer(k)
            for a in range(na):
                s = (k - 1) * na + a
                sends.append(_remote(ins[a].at[pc, 2 * px + py], outs[a].at[me], sems, s, (px, py, pc)))
                recvs.append(_remote(ins[a].at[pc, 2 * px + py], outs[a].at[4 * px + 2 * py + pc], sems, s, (px, py, pc)))
        return sends, recvs, locs

    return _Rider(blocks, [_sds((N_DEV,) + a.shape[2:], a.dtype) for a in blocks], (N_DEV - 1) * na, na, plan)


def _swap_rider(halves):
    na = len(halves)

    def plan(ins, outs, *sems):
        x, y, c = _place()
        cps = [_remote(ins[a].at[1 - c], outs[a], sems, a, (x, y, 1 - c)) for a in range(na)]
        return cps, cps, []

    return _Rider(halves, [_sds(a.shape[1:], a.dtype) for a in halves], na, 1, plan)


def _chip_exchange_rider(parts, small=None):
    na = len(parts)
    n_chip = N_DEV // 2

    def plan(ins, outs, *sems):
        x, y, c = _place()
        chip = 2 * x + y
        locs = [pltpu.make_async_copy(ins[a].at[chip], outs[a].at[chip], sems[2].at[a]) for a in range(na)]
        sends, recvs = [], []
        for k in range(1, n_chip):
            px, py, _ = _peer(2 * k)
            for a in range(na):
                s = (k - 1) * na + a
                sends.append(_remote(ins[a].at[2 * px + py], outs[a].at[chip], sems, s, (px, py, c)))
                recvs.append(_remote(ins[a].at[2 * px + py], outs[a].at[2 * px + py], sems, s, (px, py, c)))
        if small is not None:
            me = 2 * chip + c
            locs.append(pltpu.make_async_copy(ins[na], outs[na].at[me], sems[2].at[na]))
            for k in range(1, N_DEV):
                px, py, pc = _peer(k)
                s = (n_chip - 1) * na + k - 1
                sends.append(_remote(ins[na], outs[na].at[me], sems, s, (px, py, pc)))
                recvs.append(_remote(ins[na], outs[na].at[4 * px + 2 * py + pc], sems, s, (px, py, pc)))
        return sends, recvs, locs

    extra = [] if small is None else [small]
    shapes = [_sds(a.shape, a.dtype) for a in parts] + [_sds((N_DEV,) + s.shape, s.dtype) for s in extra]
    n_sems = (n_chip - 1) * na + (N_DEV - 1) * len(extra)
    return _Rider(list(parts) + extra, shapes, n_sems, na + len(extra), plan)


def _pair_add(halves, other, core, name):
    _, nch, r, c = halves.shape

    def body(c_ref, h_ref, o_ref, p_ref):
        p_ref[...] = (h_ref[...].astype(F32) + o_ref[...].astype(F32)).astype(BF16)

    blk = pl.BlockSpec((None, r, c), lambda j, c_ref: (j, 0, 0))
    return _pc(
        body, name,
        grid_spec=pltpu.PrefetchScalarGridSpec(
            num_scalar_prefetch=1, grid=(nch,),
            in_specs=[pl.BlockSpec((None, None, r, c), lambda j, c_ref: (c_ref[0], j, 0, 0)), blk], out_specs=blk),
        out_shape=_sds((nch, r, c), BF16),
        compiler_params=_params(),
    )(core, halves, other)


def _sum_adamw(parts, w, m, v, name, rider=None):
    nl, r, c = w.shape
    tr = 256 if r % 256 == 0 else r

    def body(*refs):
        p_refs = refs[:nl]
        w_ref, m_ref, v_ref, g_ref, d_ref, mo_ref, vo_ref = refs[nl:]
        for l in range(nl):
            @pl.when(pl.program_id(0) == l)
            def _(p_ref=p_refs[l]):
                g = p_ref[0].astype(F32)
                for j in range(1, p_ref.shape[0]):
                    g = g + p_ref[j].astype(F32)
                mn = ADAM_B1 * m_ref[...] + (1.0 - ADAM_B1) * g
                vn = ADAM_B2 * v_ref[...] + (1.0 - ADAM_B2) * (g * g)
                m_hat = mn / (1.0 - ADAM_B1 ** ADAM_STEP)
                v_hat = vn / (1.0 - ADAM_B2 ** ADAM_STEP)
                g_ref[...] = g
                d_ref[...] = -ADAM_LR * (m_hat / (jnp.sqrt(v_hat) + ADAM_EPS) + ADAM_WD * w_ref[...])
                mo_ref[...] = mn
                vo_ref[...] = vn

    def part_spec(l, k):
        return pl.BlockSpec((k, tr, c), lambda li, i: (0, jnp.where(li == l, i, 0), 0))

    row = pl.BlockSpec((None, tr, c), lambda li, i: (li, i, 0))
    return _call(
        body, name, (*parts, w, m, v), rider, grid=(nl, r // tr),
        in_specs=[part_spec(l, p.shape[0]) for l, p in enumerate(parts)] + [row, row, row],
        out_specs=[row] * 4,
        out_shape=[_sds((nl, r, c), F32)] * 4,
        compiler_params=_params(),
    )


SMALL = ("lower_bounds", "pre_norm_g", "hgrn_norm_g", "pool_w", "pool_scale", "post_norm_g", "fox_f_bias")
SMALL_LANES = 128


def _small_size(tree):
    return sum(tree[k].size for k in SMALL)


def _pack_small(tree, extra=None):
    flat = jnp.concatenate([tree[k].reshape(-1) for k in SMALL] + ([] if extra is None else [extra.reshape(1)]))
    rows = -(-(_small_size(tree) + 1) // (8 * SMALL_LANES)) * 8
    return jnp.pad(flat, (0, rows * SMALL_LANES - flat.shape[0])).reshape(rows, SMALL_LANES)


def _unpack_small(packed, like):
    out, off = {}, 0
    for k in SMALL:
        size = like[k].size
        assert off % SMALL_LANES == 0
        rows = packed[off // SMALL_LANES:-(-(off + size) // SMALL_LANES)]
        out[k] = rows.reshape(-1)[:size].reshape(like[k].shape)
        off += size
    return out


def _block_diag(pw):
    g = pw.shape[0]
    eye = jnp.eye(g, dtype=pw.dtype)
    return (eye[:, None, :, None] * pw[:, :, None, :]).reshape(g * HEAD, g * HEAD)


def _assemble_w_in(g_in, name):
    _, d, shard = g_in.shape
    tr = min(256, d)
    wide = MAIN_W + FC_PAD

    def body(g_ref, wm_ref, wf_ref, wmt_ref, wft_ref, row_s):
        row_s[:, MAIN_W:] = jnp.zeros((tr, FC_PAD), F32)
        for j in range(N_DEV):
            row_s[:, j * shard:(j + 1) * shard] = g_ref[j].astype(F32)
        wm_ref[...] = row_s[:, :MAIN_W].astype(BF16)
        wf_ref[...] = row_s[:, MAIN_W:].astype(BF16)
        for j in range(0, MAIN_W, 512):
            wmt_ref[j:j + 512, :] = row_s[:, j:j + 512].T.astype(BF16)
        wft_ref[...] = row_s[:, MAIN_W:].T.astype(BF16)

    return _pc(
        body, name, grid=(d // tr,),
        in_specs=[pl.BlockSpec((N_DEV, tr, shard), lambda i: (0, i, 0))],
        out_specs=[pl.BlockSpec((tr, MAIN_W), lambda i: (i, 0)), pl.BlockSpec((tr, FC_PAD), lambda i: (i, 0)),
                   pl.BlockSpec((MAIN_W, tr), lambda i: (0, i)), pl.BlockSpec((FC_PAD, tr), lambda i: (0, i))],
        out_shape=[_sds((d, MAIN_W), BF16), _sds((d, FC_PAD), BF16), _sds((MAIN_W, d), BF16), _sds((FC_PAD, d), BF16)],
        scratch_shapes=[pltpu.VMEM((tr, wide), F32)],
        compiler_params=_params(),
    )(g_in)


def _w_out_parts(g_out):
    full_out = g_out.reshape(N_DEV * g_out.shape[1], g_out.shape[2])
    return full_out, full_out.T


def _layer_fwd(l, x, lbs, weights, lw, nb, rider_h=None, rider_c=None, target=None):
    n = x.shape[0]
    t = n // nb
    w_main, w_fc, _, _, w_out, _ = lw
    bias = jnp.pad(weights["fox_f_bias"][l:l + 1], ((0, 0), (0, FC_PAD - FOX_HEADS)))
    wbd = _block_diag(weights["pool_w"][l]).astype(BF16)
    proj, fc, ht, qt, kt = _in_proj_fwd(x, weights["pre_norm_g"][l:l + 1], w_main, w_fc, f"in_proj_fwd_{l}")
    c_col, c_row = _fox_decay_fwd(fc, bias, nb, f"fox_decay_fwd_{l}")
    (o_h, s0), rode_h = _hgrn_fwd(proj, lbs[l:l + 1], _block_ones(HGRN_W, BF16), nb, f"hgrn_fwd_{l}", rider_h)
    o_b = _pool_fwd(proj, wbd, weights["pool_scale"][l:l + 1], nb, f"pool_fwd_{l}")
    (o_c, lse), rode_c = _fox_fwd(proj, kt, c_col, c_row, nb, f"fox_fwd_{l}", rider_c)
    if w_out is None:
        lw = tuple(lw[:4]) + _w_out_parts(rode_h[0])
        w_out, rode_h = lw[4], rode_h[1:]
    x_next, mixt, y = _merge_fwd(x, proj, o_h, o_b, o_c, weights["hgrn_norm_g"][l:l + 1], w_out,
                                 weights["post_norm_g"][l:l + 1], f"merge_fwd_{l}", target)
    return x_next, (x, proj, fc, ht, qt, kt, c_col, c_row, o_h, s0, o_c, lse, mixt, y, bias, wbd), lw, (rode_h, rode_c)


def _layer_bwd(l, dx, saved, lbs, weights, lw, nb, rider=None):
    _, proj, fc, ht, qt, kt, c_col, c_row, o_h, s0, o_c, lse, mixt, y, bias, wbd = saved
    n = proj.shape[0]
    t = n // nb
    w_out_t = lw[5]
    g = {}
    dy, dmix, dgp, da, dat, d_gc, delta = _merge_bwd(dx, y, weights["post_norm_g"][l:l + 1], w_out_t, proj, o_c,
                                                nb, f"merge_bwd_{l}")
    g["post_norm_g"] = dgp[0]
    g["w_out"] = _w_out_grad(mixt, dy, f"w_out_grad_{l}")
    d_a, dgh, dlb = _hgrn_bwd(dmix, proj, o_h, s0, weights["hgrn_norm_g"][l:l + 1], lbs[l:l + 1],
                              _block_ones(HGRN_W, BF16), nb, f"hgrn_bwd_{l}")
    g["hgrn_norm_g"], g["lbs"] = dgh[0], dlb[0]
    d_b, dwbd, dps = _pool_bwd(dmix, proj, wbd, wbd.T, weights["pool_scale"][l:l + 1], nb, f"pool_bwd_{l}")
    g["pool_w"] = jnp.stack([dwbd[j * HEAD:(j + 1) * HEAD, j * HEAD:(j + 1) * HEAD] for j in range(len(POOL_WINDOWS))])
    g["pool_scale"] = dps[0]
    (d_qc, d_kc, d_vc, dc_k, dc_q), rode = _fox_bwd(proj, qt, kt, da, dat, c_col, c_row, lse, delta, nb,
                                                    f"fox_bwd_{l}", rider)
    d_fc, dbias = _fox_decay_bwd(dc_q, dc_k, fc, bias, nb, f"fox_decay_bwd_{l}")
    g["fox_f_bias"] = dbias[0, :FOX_HEADS]
    pieces = [(d_a, C_QA), (d_b, C_UB), (d_qc, C_QC), (d_kc, C_KC), (d_vc, C_VC), (d_gc, C_GC), (d_fc, None)]
    g["w_in"] = _w_in_grad(ht, pieces, f"w_in_grad_{l}")
    return g, pieces, rode


def _layer_bwd_input(l, dx, pieces, saved, weights, lw, rider=None):
    (dxi, dgpre), rode = _in_proj_bwd(pieces, lw[2], lw[3], saved[0], weights["pre_norm_g"][l:l + 1], dx,
                                      f"in_proj_bwd_{l}", rider)
    return dxi, dgpre[0], rode


def kernel(x, lower_bounds, pre_norm_g, w_in, hgrn_norm_g, fox_f_bias, pool_w, pool_scale, w_out, post_norm_g, loss_target, m_lower_bounds, m_pre_norm_g, m_w_in, m_hgrn_norm_g, m_fox_f_bias, m_pool_w, m_pool_scale, m_w_out, m_post_norm_g, v_lower_bounds, v_pre_norm_g, v_w_in, v_hgrn_norm_g, v_fox_f_bias, v_pool_w, v_pool_scale, v_w_out, v_post_norm_g):
    weights = dict(lower_bounds=lower_bounds, pre_norm_g=pre_norm_g, hgrn_norm_g=hgrn_norm_g, fox_f_bias=fox_f_bias,
                   pool_w=pool_w, pool_scale=pool_scale, post_norm_g=post_norm_g)
    mom_m = dict(lower_bounds=m_lower_bounds, pre_norm_g=m_pre_norm_g, hgrn_norm_g=m_hgrn_norm_g, fox_f_bias=m_fox_f_bias,
                 pool_w=m_pool_w, pool_scale=m_pool_scale, post_norm_g=m_post_norm_g)
    mom_v = dict(lower_bounds=v_lower_bounds, pre_norm_g=v_pre_norm_g, hgrn_norm_g=v_hgrn_norm_g, fox_f_bias=v_fox_f_bias,
                 pool_w=v_pool_w, pool_scale=v_pool_scale, post_norm_g=v_post_norm_g)
    depth = w_in.shape[0]
    nb, t, d = x.shape
    n = nb * t
    core = lax.axis_index("c").astype(jnp.int32).reshape(1)
    shards = [(w_in[l].astype(BF16), w_out[l].astype(BF16)) for l in range(depth)]
    lbs = _lower_bound_table(lower_bounds, "lower_bound_table")

    (g_in,) = _gather_weights(shards[0][0])
    coming = tuple(_assemble_w_in(g_in, "assemble_w_in_0")) + (None, None)
    xl, saved, lw = x.reshape(n, d), [], []
    for l in range(depth):
        last = l + 1 == depth
        ride_h = ([shards[l][1]] if coming[4] is None else []) + ([] if last else [shards[l + 1][1]])
        xl, sv, lw_l, (rode_h, rode_c) = _layer_fwd(
            l, xl, lbs, weights, coming, nb, _gather_rider(ride_h) if ride_h else None,
            None if last else _gather_rider([shards[l + 1][0]]), loss_target.reshape(n, d) if last else None)
        saved.append(sv)
        lw.append(lw_l)
        if not last:
            coming = tuple(_assemble_w_in(rode_c[0], f"assemble_w_in_{l + 1}")) + _w_out_parts(rode_h[0])
    dx, sq = xl
    loss_here = 0.5 * jnp.sum(sq) / d

    grads, recv, pending = [None] * depth, [None] * depth, None
    for l in reversed(range(depth)):
        g, pieces, rode = _layer_bwd(l, dx, saved[l], lbs, weights, lw[l], nb, pending)
        if rode is not None:
            recv[l + 1] = rode
        blocks = (g["w_in"], g["w_out"])
        if l > 0:
            pending = _direct_exchange_rider(blocks)
            dx, g["pre_norm_g"], _ = _layer_bwd_input(l, dx, pieces, saved[l], weights, lw[l])
        else:
            other = _run_rider(_swap_rider(blocks), "grad_swap")
            summed = [_pair_add(hv, ot, core, f"grad_pair_add_{i}") for i, (hv, ot) in enumerate(zip(blocks, other))]
            dx, g["pre_norm_g"], recv[l] = _layer_bwd_input(l, dx, pieces, saved[l], weights, lw[l],
                                                            _chip_exchange_rider(summed))
        grads[l] = g
    small = {k: jnp.stack([grads[l][k] for l in range(depth)]) for k in SMALL if k != "lower_bounds"}
    small["lower_bounds"] = _lower_bound_bwd(lower_bounds, jnp.stack([grads[l]["lbs"] for l in range(depth)]),
                                             "lower_bound_bwd")
    (r_small,) = _run_rider(_gather_rider([_pack_small(small, loss_here)]), "small_grads_gather")

    res_in, _ = _sum_adamw([recv[l][0] for l in range(depth)], w_in, m_w_in, v_w_in, "adamw_w_in")
    res_out, _ = _sum_adamw([recv[l][1] for l in range(depth)], w_out, m_w_out, v_w_out, "adamw_w_out")
    res_small, _ = _sum_adamw([r_small], _pack_small(weights)[None], _pack_small(mom_m)[None], _pack_small(mom_v)[None],
                              "adamw_small")
    loss = res_small[0][0][_small_size(weights) // SMALL_LANES, _small_size(weights) % SMALL_LANES]

    names = ("lower_bounds", "pre_norm_g", "w_in", "hgrn_norm_g", "fox_f_bias", "pool_w", "pool_scale", "w_out", "post_norm_g")
    outs = [loss, dx.reshape(nb, t, d)]
    for i in range(4):
        full = dict(_unpack_small(res_small[i][0], weights), w_in=res_in[i], w_out=res_out[i])
        outs += [full[k] for k in names]
    return tuple(outs)
```

```python
import functools

import jax
import jax.numpy as jnp
from jax import lax
from jax.experimental import pallas as pl
from jax.experimental.pallas import tpu as pltpu

F32, BF16 = jnp.float32, jnp.bfloat16
HI = lax.Precision.HIGHEST
MESH = pl.DeviceIdType.MESH
N_DEV = 8

NORM_EPS = 1e-6
MASK_VALUE = -1e30
TINY = 1e-30
CHUNK = 64
SUB = 16
HGRN_W, POOL_W, FOX_W = 256, 256, 512
HEAD = 64
FOX_HEADS = 8
POOL_WINDOWS = (2, 4, 8, 16)
POOL_HALO = 16
MAIN_W = 3584
FC_PAD = 128
C_QA, C_FA, C_IA, C_GA, C_UB, C_GB, C_QC, C_KC, C_VC, C_GC = 0, 256, 512, 768, 1024, 1280, 1536, 2048, 2560, 3072
FOX_SCALE = HEAD ** -0.5

ADAM_LR, ADAM_B1, ADAM_B2, ADAM_EPS, ADAM_WD, ADAM_STEP = 0.001, 0.9, 0.999, 1e-08, 0.01, 10

VMEM_LIMIT = 56 * 1024 * 1024


def _pc(fn, name, **kw):
    return pl.pallas_call(fn, name=name, **kw)


def _params(**kw):
    return pltpu.CompilerParams(vmem_limit_bytes=VMEM_LIMIT, **kw)


class _Rider:
    def __init__(self, inputs, out_shapes, n_sems, n_local, plan):
        self.inputs, self.out_shapes, self.n_sems, self.n_local, self.plan = list(inputs), list(out_shapes), n_sems, n_local, plan

    def start(self, ins, outs, *sems):
        sends, _, locs = self.plan(ins, outs, *sems)
        for cp in locs + sends:
            cp.start()

    def wait(self, ins, outs, *sems):
        sends, recvs, locs = self.plan(ins, outs, *sems)
        for cp in recvs:
            cp.wait_recv()
        for cp in sends:
            cp.wait_send()
        for cp in locs:
            cp.wait()

    def sem_shapes(self):
        return [pltpu.SemaphoreType.DMA((self.n_sems,)), pltpu.SemaphoreType.DMA((self.n_sems,)),
                pltpu.SemaphoreType.DMA((self.n_local,))]


def _call(body, name, args, rider=None, *, grid, in_specs, out_specs, out_shape, scratch_shapes=(), **kw):
    if rider is None:
        res = _pc(body, name, grid=grid, in_specs=in_specs, out_specs=out_specs, out_shape=out_shape,
                  scratch_shapes=list(scratch_shapes), **kw)(*args)
        return res, None
    n_in, n_out, n_scr = len(in_specs), len(out_specs), len(scratch_shapes)
    n_rin, n_rout = len(rider.inputs), len(rider.out_shapes)

    def ridden(*refs):
        ins, refs = refs[:n_in], refs[n_in:]
        rins, refs = refs[:n_rin], refs[n_rin:]
        outs, refs = refs[:n_out], refs[n_out:]
        routs, refs = refs[:n_rout], refs[n_rout:]
        scr, sems = refs[:n_scr], refs[n_scr:]
        first = functools.reduce(jnp.logical_and, [pl.program_id(a) == 0 for a in range(len(grid))])
        last = functools.reduce(jnp.logical_and, [pl.program_id(a) == g - 1 for a, g in enumerate(grid)])

        @pl.when(first)
        def _():
            rider.start(rins, routs, *sems)

        body(*ins, *outs, *scr)

        @pl.when(last)
        def _():
            rider.wait(rins, routs, *sems)

    any_spec = pl.BlockSpec(memory_space=pl.ANY)
    res = _pc(ridden, name, grid=grid, in_specs=list(in_specs) + [any_spec] * n_rin,
              out_specs=list(out_specs) + [any_spec] * n_rout, out_shape=list(out_shape) + rider.out_shapes,
              scratch_shapes=list(scratch_shapes) + rider.sem_shapes(), **kw)(*args, *rider.inputs)
    return res[:n_out], res[n_out:]


def _run_rider(rider, name):
    n_rin = len(rider.inputs)

    def body(*refs):
        ins, outs, sems = refs[:n_rin], refs[n_rin:n_rin + len(rider.out_shapes)], refs[n_rin + len(rider.out_shapes):]
        rider.start(ins, outs, *sems)
        rider.wait(ins, outs, *sems)

    any_spec = pl.BlockSpec(memory_space=pl.ANY)
    return _pc(body, name, in_specs=[any_spec] * n_rin, out_specs=[any_spec] * len(rider.out_shapes),
               out_shape=rider.out_shapes, scratch_shapes=rider.sem_shapes())(*rider.inputs)


def _dot(a, b):
    return jnp.dot(a, b, preferred_element_type=F32)


def _dot_nt(a, b):
    return lax.dot_general(a, b, (((1,), (1,)), ((), ())), preferred_element_type=F32)


def _dot_tn(a, b):
    return lax.dot_general(a, b, (((0,), (0,)), ((), ())), preferred_element_type=F32)


def _dot_hi(a, b):
    return jnp.dot(a, b, precision=HI, preferred_element_type=F32)


def _split2(x):
    hi = x.astype(BF16)
    return hi, (x - hi.astype(F32)).astype(BF16)


def _sel_dot(sel, x):
    hi, lo = _split2(x)
    sb = sel.astype(BF16)
    return _dot(sb, hi) + _dot(sb, lo)


def _dot_sel(x, sel):
    hi, lo = _split2(x)
    sb = sel.astype(BF16)
    return _dot(hi, sb) + _dot(lo, sb)


def _sigmoid(x):
    return 1.0 / (1.0 + jnp.exp(-x))


def _block_ones(n, dtype):
    r = lax.broadcasted_iota(jnp.int32, (n, n), 0) // HEAD
    c = lax.broadcasted_iota(jnp.int32, (n, n), 1) // HEAD
    return (r == c).astype(dtype)


def _sds(shape, dtype):
    return jax.ShapeDtypeStruct(shape, dtype)


def _in_proj_fwd(x, g_pre, w_main, w_fc, name):
    n, d = x.shape
    tm = min(512, n)

    def body(x_ref, g_ref, w_ref, wf_ref, proj_ref, fc_ref, ht_ref, qt_ref, kt_ref):
        xv = x_ref[...]
        r = lax.rsqrt(jnp.mean(xv * xv, axis=-1, keepdims=True) + NORM_EPS)
        hf = xv * r * g_ref[...]
        hb = hf.astype(BF16)
        ht_ref[...] = hf.T.astype(BF16)
        for j in range(0, MAIN_W, FOX_W):
            res = _dot(hb, w_ref[:, j:j + FOX_W])
            proj_ref[:, j:j + FOX_W] = res
            if j == C_QC:
                qt_ref[...] = (res * FOX_SCALE).T.astype(BF16)
            if j == C_KC:
                kt_ref[...] = res.T.astype(BF16)
        fc_ref[...] = _dot(hb, wf_ref[...])

    def cols(rows):
        return pl.BlockSpec((rows, tm), lambda i: (0, i))

    return _pc(
        body, name, grid=(n // tm,),
        in_specs=[pl.BlockSpec((tm, d), lambda i: (i, 0)), pl.BlockSpec((1, d), lambda i: (0, 0)),
                  pl.BlockSpec((d, MAIN_W), lambda i: (0, 0)), pl.BlockSpec((d, FC_PAD), lambda i: (0, 0))],
        out_specs=[pl.BlockSpec((tm, MAIN_W), lambda i: (i, 0)), pl.BlockSpec((tm, FC_PAD), lambda i: (i, 0)),
                   cols(d), cols(FOX_W), cols(FOX_W)],
        out_shape=[_sds((n, MAIN_W), F32), _sds((n, FC_PAD), F32), _sds((d, n), BF16), _sds((FOX_W, n), BF16),
                   _sds((FOX_W, n), BF16)],
        compiler_params=_params(),
    )(x, g_pre, w_main, w_fc)


def _fox_decay_fwd(fc, bias, nb, name):
    n = fc.shape[0]
    t = n // nb
    tt = min(256, t)
    nt = t // tt

    def body(fc_ref, b_ref, c_ref, cr_ref, carry):
        i = pl.program_id(1)

        @pl.when(i == 0)
        def _():
            carry[...] = jnp.zeros_like(carry)

        xv = fc_ref[...] + b_ref[...]
        lf = jnp.minimum(xv, 0.0) - jnp.log(1.0 + jnp.exp(-jnp.abs(xv)))
        r = lax.broadcasted_iota(jnp.int32, (tt, tt), 0)
        cc = lax.broadcasted_iota(jnp.int32, (tt, tt), 1)
        cs = _dot_hi((r >= cc).astype(F32), lf) + carry[...]
        c_ref[...] = cs
        cr_ref[...] = cs.T[:FOX_HEADS, :]
        carry[...] = cs[tt - 1:tt, :]

    return _pc(
        body, name, grid=(nb, nt),
        in_specs=[pl.BlockSpec((tt, FC_PAD), lambda b, i: (b * nt + i, 0)), pl.BlockSpec((1, FC_PAD), lambda b, i: (0, 0))],
        out_specs=[pl.BlockSpec((tt, FC_PAD), lambda b, i: (b * nt + i, 0)),
                   pl.BlockSpec((None, FOX_HEADS, tt), lambda b, i: (b, 0, i))],
        out_shape=[_sds((n, FC_PAD), F32), _sds((nb, FOX_HEADS, t), F32)],
        scratch_shapes=[pltpu.VMEM((1, FC_PAD), F32)],
        compiler_params=_params(),
    )(fc, bias)


def _hgrn_gates(q, z, lb):
    sig = _sigmoid(z)
    sn = _sigmoid(-z)
    f = lb + (1.0 - lb) * sig
    g = jnp.log(jnp.maximum(f, TINY))
    k = (1.0 - lb) * sn
    sq = _sigmoid(q)
    return sig, sn, f, g, k, sq


def _sub_tri(n, lower):
    r = lax.broadcasted_iota(jnp.int32, (n, n), 0)
    c = lax.broadcasted_iota(jnp.int32, (n, n), 1)
    tri = (r >= c) if lower else (r <= c)
    return jnp.logical_and(r // SUB == c // SUB, tri).astype(F32)


def _live_rows(t):
    return 8 * (t // 8 + 1)


def _pad_rows(x):
    return x if x.shape[0] == SUB else jnp.concatenate([x, jnp.zeros((SUB - x.shape[0], x.shape[1]), x.dtype)], axis=0)


def _hgrn_decays(qs, k, b):
    srow = lax.broadcasted_iota(jnp.int32, (SUB, HGRN_W), 0)
    es, ws = [], []
    for t in range(SUB):
        r = _live_rows(t)
        e = jnp.where(srow[:r] <= t, jnp.exp(b[t:t + 1, :] - b[:r]), 0.0)
        es.append(e)
        ws.append(_pad_rows(e * (qs[t:t + 1, :] * k[:r])))
    return srow, es, ws


def _hgrn_state_step(st, k, v, b, bmask):
    bl = b[SUB - 1:SUB, :]
    ktil = k * jnp.exp(bl - b)
    return st * jnp.exp(bl) + _dot_tn(v.astype(BF16), ktil.astype(BF16)) * bmask


def _hgrn_tile(t):
    return min(256, t)


def _hgrn_fwd(proj, lb, ones_b, nb, name, rider=None):
    n = proj.shape[0]
    t = n // nb
    tt = _hgrn_tile(t)
    nt = t // tt
    ncs = tt // CHUNK
    w = HGRN_W

    def body(q_ref, z_ref, v_ref, lb_ref, ones_ref, o_ref, s0_ref, st_s, b_s, qs_s, k_s):
        @pl.when(pl.program_id(1) == 0)
        def _():
            st_s[...] = jnp.zeros_like(st_s)

        q = q_ref[...]
        _, _, _, g, k, sq = _hgrn_gates(q, z_ref[...], lb_ref[...])
        b_s[...] = _sel_dot(_sub_tri(tt, True), g)
        qs_s[...] = q * sq
        k_s[...] = k
        bmask = _block_ones(w, F32)
        ones_b = ones_ref[...]

        def chunk(c, carry):
            st = st_s[...]
            s0_ref[c] = st
            base = pl.multiple_of(c * CHUNK, CHUNK)
            tiles = []
            for u in range(CHUNK // SUB):
                rows = pl.ds(base + u * SUB, SUB)
                tiles.append((qs_s[rows, :], k_s[rows, :], v_ref[rows, :], b_s[rows, :]))
            aexps = []
            for qs, k, v, b in tiles:
                _, _, ws = _hgrn_decays(qs, k, b)
                aexps.append(_dot(jnp.concatenate(ws, axis=0).astype(BF16), ones_b))
            inters = []
            for qs, k, v, b in tiles:
                inters.append(_dot_nt((qs * jnp.exp(b)).astype(BF16), st.astype(BF16)))
                st = _hgrn_state_step(st, k, v, b, bmask)
            st_s[...] = st
            for u, ((qs, k, v, b), aexp, o) in enumerate(zip(tiles, aexps, inters)):
                for t in range(SUB):
                    r = _live_rows(t)
                    row = o[t:t + 1, :] + jnp.sum(aexp[t * SUB:t * SUB + r, :] * v[:r], axis=0, keepdims=True)
                    o_ref[pl.ds(base + u * SUB + t, 1), :] = row
            return carry

        lax.fori_loop(0, ncs, chunk, 0)

    def col(j):
        return pl.BlockSpec((tt, w), lambda b, i: (b * nt + i, j))

    return _call(
        body, name, (proj, proj, proj, lb, ones_b), rider, grid=(nb, nt),
        in_specs=[col(C_QA // w), col(C_FA // w), col(C_IA // w), pl.BlockSpec((1, w), lambda b, i: (0, 0)),
                  pl.BlockSpec((w, w), lambda b, i: (0, 0))],
        out_specs=[pl.BlockSpec((tt, w), lambda b, i: (b * nt + i, 0)),
                   pl.BlockSpec((ncs, w, w), lambda b, i: (b * nt + i, 0, 0))],
        out_shape=[_sds((n, w), F32), _sds((n // CHUNK, w, w), F32)],
        scratch_shapes=[pltpu.VMEM((w, w), F32)] + [pltpu.VMEM((tt, w), F32)] * 3,
        compiler_params=_params(),
    )


def _pool_lane_windows():
    lane = lax.broadcasted_iota(jnp.int32, (1, POOL_W), 1) // HEAD
    wl = jnp.zeros((1, POOL_W), F32)
    for gi, win in enumerate(POOL_WINDOWS):
        wl = jnp.where(lane == gi, float(win), wl)
    return lane, wl


def _pool_select(lane, parts):
    out = parts[-1]
    for gi in range(len(parts) - 2, -1, -1):
        out = jnp.where(lane == gi, parts[gi], out)
    return out


def _pool_mix(u, halo, t0, tt):
    lane, wl = _pool_lane_windows()
    ext = jnp.concatenate([halo, u], axis=0)
    sums, cur, shift = [], ext, 1
    for _ in POOL_WINDOWS:
        cur = cur + pltpu.roll(cur, shift, axis=0)
        sums.append(cur[POOL_HALO:, :])
        shift *= 2
    tpos = (t0 + lax.broadcasted_iota(jnp.int32, (tt, POOL_W), 0)).astype(F32)
    cnt = jnp.minimum(tpos + 1.0, wl)
    return _pool_select(lane, sums) / cnt - u, cnt


def _pool_specs(tt, nt, nhb):
    cu, cg = C_UB // POOL_W, C_GB // POOL_W
    return [pl.BlockSpec((tt, POOL_W), lambda b, i: (b * nt + i, cu)),
            pl.BlockSpec((tt, POOL_W), lambda b, i: (b * nt + i, cg)),
            pl.BlockSpec((POOL_HALO, POOL_W), lambda b, i: (jnp.maximum((b * nt + i) * nhb - 1, 0), cu))]


def _pool_fwd(proj, wbd, scale, nb, name):
    n = proj.shape[0]
    t = n // nb
    tt = min(512, t)
    nt = t // tt
    nhb = tt // POOL_HALO

    def body(u_ref, g_ref, h_ref, w_ref, s_ref, o_ref):
        i = pl.program_id(1)
        halo = jnp.where(i == 0, 0.0, h_ref[...])
        pooled, _ = _pool_mix(u_ref[...], halo, i * tt, tt)
        gv = g_ref[...]
        o_ref[...] = _dot(pooled.astype(BF16), w_ref[...]) * s_ref[...] * (gv * _sigmoid(gv))

    return _pc(
        body, name, grid=(nb, nt),
        in_specs=_pool_specs(tt, nt, nhb) + [pl.BlockSpec((POOL_W, POOL_W), lambda b, i: (0, 0)),
                                             pl.BlockSpec((1, POOL_W), lambda b, i: (0, 0))],
        out_specs=pl.BlockSpec((tt, POOL_W), lambda b, i: (b * nt + i, 0)),
        out_shape=_sds((n, POOL_W), F32),
        compiler_params=_params(),
    )(proj, proj, proj, wbd, scale)


def _rows_reduce(x, op, final):
    while x.shape[0] > 8 and x.shape[0] % 16 == 0:
        half = x.shape[0] // 2
        x = op(x[:half], x[half:])
    return final(x, axis=0, keepdims=True)


def _tri_pair(step, n, group=1):
    counts = [a // group + 1 for a in range(n)]
    firsts = [sum(counts[:a]) for a in range(1, n)]
    a = sum([(step >= f).astype(jnp.int32) for f in firsts], jnp.int32(0))
    first = sum([jnp.where(step >= f, c, 0) for f, c in zip(firsts, counts)], jnp.int32(0))
    return a, step - first


def _tri_steps(n, group=1):
    return sum(a // group + 1 for a in range(n))


def _lane_lo():
    return lax.broadcasted_iota(jnp.int32, (1, 2 * HEAD), 1) < HEAD


def _put_col(tile, hh, colv):
    lane = lax.broadcasted_iota(jnp.int32, tile.shape, 1)
    return jnp.where(lane == hh, colv, tile)


def _fox_fwd(proj, kt, c_col, c_row, nb, name, rider=None):
    n = proj.shape[0]
    t = n // nb
    tb = min(256, t)
    nq = t // tb
    pw = 2 * HEAD
    kw = 2 if nq % 2 == 0 else 1
    nk = nq // kw

    def body(q_ref, kt_ref, v_ref, cc_ref, cr_ref, o_ref, lse_ref, m_s, acc_s, cq_s):
        qi, kk = _tri_pair(pl.program_id(1), nq, kw)

        @pl.when(kk == 0)
        def _():
            m_s[...] = jnp.full_like(m_s, -jnp.inf)
            acc_s[...] = jnp.zeros_like(acc_s)
            for hh in range(FOX_HEADS):
                cq_s[hh] = jnp.broadcast_to(cc_ref[:, hh:hh + 1], (tb, pw))

        def block(masked, sub):
            lo = _lane_lo()
            keys = slice(sub * tb, (sub + 1) * tb)
            if masked:
                causal = lax.broadcasted_iota(jnp.int32, (tb, tb), 0) >= lax.broadcasted_iota(jnp.int32, (tb, tb), 1)
            def lanes(hh):
                return lo if hh % 2 == 0 else jnp.logical_not(lo)

            def scores(hh):
                sl = slice((hh // 2) * pw, (hh // 2 + 1) * pw)
                return _dot(jnp.where(lanes(hh), q_ref[:, sl] * FOX_SCALE, 0.0).astype(BF16), kt_ref[sl, keys])

            ahead = scores(0)
            for hh in range(FOX_HEADS):
                s = ahead
                if hh + 1 < FOX_HEADS:
                    ahead = scores(hh + 1)
                s = s + (jnp.tile(cq_s[hh], (1, tb // pw)) - cr_ref[hh:hh + 1, keys])
                if masked:
                    s = jnp.where(causal, s, MASK_VALUE)
                m_prev = m_s[hh]
                m_new = jnp.maximum(m_prev, jnp.max(s, axis=1, keepdims=True))
                alpha = jnp.exp(m_prev - m_new)
                pe = jnp.exp(s - jnp.tile(m_new, (1, tb // pw)))
                m_s[hh] = m_new
                vf = v_ref[keys, (hh // 2) * pw:(hh // 2 + 1) * pw]
                acc_s[hh] = alpha * acc_s[hh] + _dot(pe.astype(BF16), jnp.where(lanes(hh), vf, 1.0).astype(BF16))

        def finish():
            lo = _lane_lo()
            m_all, l_all = jnp.zeros((tb, FC_PAD), F32), jnp.ones((tb, FC_PAD), F32)
            for p in range(FOX_HEADS // 2):
                a0, a1 = acc_s[2 * p], acc_s[2 * p + 1]
                both = pltpu.roll(jnp.where(lo, a1, a0), HEAD, axis=1)
                o_ref[:, p * pw:(p + 1) * pw] = jnp.where(lo, a0, a1) / both
                m_all = _put_col(_put_col(m_all, 2 * p, m_s[2 * p]), 2 * p + 1, m_s[2 * p + 1])
                l_all = _put_col(_put_col(l_all, 2 * p, both), 2 * p + 1, a1)
            lse_ref[...] = (m_all + jnp.log(l_all)).T[:FOX_HEADS, :]

        for sub in range(kw):
            @pl.when(kk * kw + sub < qi)
            def _(sub=sub):
                block(False, sub)

            @pl.when(kk * kw + sub == qi)
            def _(sub=sub):
                block(True, sub)
                finish()

    def qspec(wd, j):
        return pl.BlockSpec((tb, wd), lambda b, st: (b * nq + _tri_pair(st, nq, kw)[0], j))

    def kspec(j):
        return pl.BlockSpec((kw * tb, FOX_W), lambda b, st: (b * nk + _tri_pair(st, nq, kw)[1], j))

    return _call(
        body, name, (proj, kt, proj, c_col, c_row), rider, grid=(nb, _tri_steps(nq, kw)),
        in_specs=[qspec(FOX_W, C_QC // FOX_W),
                  pl.BlockSpec((FOX_W, kw * tb), lambda b, st: (0, b * nk + _tri_pair(st, nq, kw)[1])),
                  kspec(C_VC // FOX_W), qspec(FC_PAD, 0),
                  pl.BlockSpec((None, FOX_HEADS, kw * tb), lambda b, st: (b, 0, _tri_pair(st, nq, kw)[1]))],
        out_specs=[qspec(FOX_W, 0),
                   pl.BlockSpec((None, FOX_HEADS, tb), lambda b, st: (b, 0, _tri_pair(st, nq, kw)[0]))],
        out_shape=[_sds((n, FOX_W), F32), _sds((nb, FOX_HEADS, t), F32)],
        scratch_shapes=[pltpu.VMEM((FOX_HEADS, tb, pw), F32), pltpu.VMEM((FOX_HEADS, tb, pw), F32),
                        pltpu.VMEM((FOX_HEADS, tb, pw), F32)],
        compiler_params=_params(),
    )


def _head_mean(x, ones_f):
    return _dot_sel(x, ones_f) * (1.0 / HEAD)


def _merge_fwd(x, proj, o_h, o_b, o_c, gh, w_out, g_post, name, target=None):
    n, d = x.shape
    tm = min(512, n)

    def body(*refs):
        x_ref, ga_ref, gc_ref, oh_ref, ob_ref, oc_ref, gh_ref, w_ref, gp_ref = refs[:9]
        if target is None:
            xo_ref, mixt_ref, y_ref = refs[9:]
        else:
            t_ref, dx_ref, sq_ref, mixt_ref, y_ref = refs[9:]
        oh = oh_ref[...]
        ones_f = _block_ones(HGRN_W, F32)
        na = oh * lax.rsqrt(_head_mean(oh * oh, ones_f) + NORM_EPS) * gh_ref[...]
        ga, gc = ga_ref[...], gc_ref[...]
        mixed = jnp.concatenate([na * (ga * _sigmoid(ga)), ob_ref[...], oc_ref[...] * (gc * _sigmoid(gc))], axis=1)
        mixt_ref[...] = mixed.T.astype(BF16)
        y = _dot(mixed.astype(BF16), w_ref[...])
        y_ref[...] = y
        xn = x_ref[...] + y * lax.rsqrt(jnp.mean(y * y, axis=-1, keepdims=True) + NORM_EPS) * gp_ref[...]
        if target is None:
            xo_ref[...] = xn
        else:
            @pl.when(pl.program_id(0) == 0)
            def _():
                sq_ref[...] = jnp.zeros_like(sq_ref)

            e = xn - t_ref[...]
            dx_ref[...] = e * (1.0 / d)
            sq_ref[...] += jnp.sum(e * e, axis=0, keepdims=True)

    def row(wd, j=0):
        return pl.BlockSpec((tm, wd), lambda i: (i, j))

    def full(a, b):
        return pl.BlockSpec((a, b), lambda i: (0, 0))

    head = [] if target is None else [target]
    res = _pc(
        body, name, grid=(n // tm,),
        in_specs=[row(d), row(HGRN_W, C_GA // HGRN_W), row(FOX_W, C_GC // FOX_W), row(HGRN_W), row(POOL_W), row(FOX_W),
                  full(1, HGRN_W), full(d, d), full(1, d)] + [row(d)] * len(head),
        out_specs=[row(d)] + [full(1, d)] * len(head) + [pl.BlockSpec((d, tm), lambda i: (0, i)), row(d)],
        out_shape=[_sds((n, d), F32)] + [_sds((1, d), F32)] * len(head) + [_sds((d, n), BF16), _sds((n, d), F32)],
        compiler_params=_params(),
    )(x, proj, proj, o_h, o_b, o_c, gh, w_out, g_post, *head)
    return (res[0], res[1], res[2]) if target is None else ((res[0], res[1]), res[2], res[3])


def _rms_bwd(dy_scaled, xhat, r):
    return r * (dy_scaled - xhat * jnp.mean(dy_scaled * xhat, axis=-1, keepdims=True))


def _merge_bwd(dxo, y, g_post, w_out_t, proj, o_c, nb, name):
    n, d = y.shape
    t = n // nb
    tm = min(512, t)
    nt = t // tm
    wab = HGRN_W + POOL_W

    def body(dx_ref, y_ref, gp_ref, wt_ref, gc_ref, oc_ref, dy_ref, dm_ref, dgp_ref, da_ref, dat_ref, dg_ref, dl_ref):
        @pl.when(pl.program_id(0) == 0)
        def _():
            dgp_ref[...] = jnp.zeros_like(dgp_ref)

        yv, dxv = y_ref[...], dx_ref[...]
        r = lax.rsqrt(jnp.mean(yv * yv, axis=-1, keepdims=True) + NORM_EPS)
        yh = yv * r
        dgp_ref[...] += jnp.sum(dxv * yh, axis=0, keepdims=True)
        dyb = _rms_bwd(dxv * gp_ref[...], yh, r).astype(BF16)
        dy_ref[...] = dyb
        dm_ref[...] = _dot(dyb, wt_ref[:, :wab])
        dmc = _dot(dyb, wt_ref[:, wab:])
        gc, oc = gc_ref[...], oc_ref[...]
        sg = _sigmoid(gc)
        da = dmc * (gc * sg)
        da_ref[...] = da.astype(BF16)
        dat_ref[...] = da.T.astype(BF16)
        dg_ref[...] = (dmc * oc * (sg * (1.0 + gc * (1.0 - sg)))).astype(BF16)
        rr = lax.broadcasted_iota(jnp.int32, (FOX_W, FC_PAD), 0) // HEAD
        cc = lax.broadcasted_iota(jnp.int32, (FOX_W, FC_PAD), 1)
        dl_ref[...] = _dot_sel(da * oc, (rr == cc).astype(F32)).T[:FOX_HEADS, :]

    def row(wd, j=0):
        return pl.BlockSpec((tm, wd), lambda i: (i, j))

    def full(a, b):
        return pl.BlockSpec((a, b), lambda i: (0, 0))

    return _pc(
        body, name, grid=(n // tm,),
        in_specs=[row(d), row(d), full(1, d), full(d, d), row(FOX_W, C_GC // FOX_W), row(FOX_W)],
        out_specs=[row(d), row(wab), full(1, d), row(FOX_W), pl.BlockSpec((FOX_W, tm), lambda i: (0, i)), row(FOX_W),
                   pl.BlockSpec((None, FOX_HEADS, tm), lambda i: (i // nt, 0, i % nt))],
        out_shape=[_sds((n, d), BF16), _sds((n, wab), F32), _sds((1, d), F32), _sds((n, FOX_W), BF16),
                   _sds((FOX_W, n), BF16), _sds((n, FOX_W), BF16), _sds((nb, FOX_HEADS, t), F32)],
        compiler_params=_params(),
    )(dxo, y, g_post, w_out_t, proj, o_c)


def _w_out_grad(mixt, dy, name):
    d, n = mixt.shape
    rows = d // N_DEV

    def body(a_ref, b_ref, o_ref):
        o_ref[...] = _dot(a_ref[...], b_ref[...]).astype(BF16)

    return _pc(
        body, name, grid=(N_DEV,),
        in_specs=[pl.BlockSpec((rows, n), lambda j: (j, 0)), pl.BlockSpec((n, d), lambda j: (0, 0))],
        out_specs=pl.BlockSpec((None, None, rows, d), lambda j: (j % 2, j // 2, 0, 0)),
        out_shape=_sds((2, N_DEV // 2, rows, d), BF16),
        compiler_params=_params(),
    )(mixt, dy)


def _w_in_grad(ht, pieces, name):
    d, n = ht.shape
    ta, tk = min(512, d), min(512, n)
    nk = n // tk
    arrays = [p for p, _ in pieces]
    widths = [p.shape[1] for p in arrays]
    offs = [sum(widths[:i]) for i in range(len(widths))]
    in_w = MAIN_W + FOX_HEADS
    shard = in_w // N_DEV

    def body(*refs):
        a_ref, p_refs = refs[0], refs[1:1 + len(arrays)]
        o_ref, acc = refs[1 + len(arrays):]
        k = pl.program_id(1)

        @pl.when(k == 0)
        def _():
            acc[...] = jnp.zeros_like(acc)

        a = a_ref[...]
        for pr, off, wd in zip(p_refs, offs, widths):
            for j in range(0, wd, 512):
                jw = min(512, wd - j)
                acc[:, off + j:off + j + jw] += _dot(a, pr[:, j:j + jw])

        @pl.when(k == nk - 1)
        def _():
            for j in range(N_DEV):
                o_ref[j % 2, j // 2] = acc[:, j * shard:(j + 1) * shard].astype(BF16)

    return _pc(
        body, name, grid=(d // ta, nk),
        in_specs=[pl.BlockSpec((ta, tk), lambda i, k: (i, k))] + [pl.BlockSpec((tk, wd), lambda i, k: (k, 0)) for wd in widths],
        out_specs=pl.BlockSpec((2, N_DEV // 2, ta, shard), lambda i, k: (0, 0, i, 0)),
        out_shape=_sds((2, N_DEV // 2, d, shard), BF16),
        scratch_shapes=[pltpu.VMEM((ta, sum(widths)), F32)],
        compiler_params=_params(),
    )(ht, *arrays)


def _hgrn_state_bwd(qs, k, v, b, do, s0, ds1, bmask):
    bl = b[SUB - 1:SUB, :]
    eb, ebl, ekt = jnp.exp(b), jnp.exp(bl), jnp.exp(bl - b)
    qe, ktil = qs * eb, k * ekt
    ds1b, dob = ds1.astype(BF16), do.astype(BF16)
    dv = _dot_nt(ktil.astype(BF16), ds1b)
    dqe = _dot(dob, s0.astype(BF16))
    dktil = _dot(v.astype(BF16), ds1b)
    dbl = jnp.sum(dktil * ktil, axis=0, keepdims=True) + ebl * jnp.sum(s0 * ds1, axis=0, keepdims=True)
    ds0 = ds1 * ebl + _dot_tn(dob, qe.astype(BF16)) * bmask
    return dqe * eb, dktil * ekt, dv, dbl, ds0


def _hgrn_intra_bwd(qs, k, v, do, es, aexp, gexp, dq, dk, dv, put_dq_row):
    dks = [dk[j:j + 8] for j in range(0, SUB, 8)]
    dvs = [dv[j:j + 8] for j in range(0, SUB, 8)]
    for t in range(SUB):
        r = _live_rows(t)
        ge = gexp[t * SUB:t * SUB + r, :] * es[t]
        put_dq_row(t, dq[t:t + 1, :] + jnp.sum(ge * k[:r], axis=0, keepdims=True))
        for j in range(r // 8):
            dks[j] = dks[j] + ge[8 * j:8 * j + 8] * qs[t:t + 1, :]
            dvs[j] = dvs[j] + aexp[t * SUB + 8 * j:t * SUB + 8 * j + 8, :] * do[t:t + 1, :]
    return jnp.concatenate(dks, axis=0), jnp.concatenate(dvs, axis=0)


def _hgrn_bwd(dmix, proj, o_h, s0, gh, lb, ones_b, nb, name, rider=None):
    n = proj.shape[0]
    t = n // nb
    tt = _hgrn_tile(t)
    nt = t // tt
    ncs = tt // CHUNK
    nsub = CHUNK // SUB
    w = HGRN_W

    def body(dm_ref, q_ref, z_ref, v_ref, ga_ref, oh_ref, s0_ref, gh_ref, lb_ref, ones_ref,
             dp_ref, dgh_ref, dlb_ref, ds_s, ss_s, b_s, qs_s, k_s, do_s, dq_s, dk_s, dv_s, dbl_s):
        first = jnp.logical_and(pl.program_id(0) == 0, pl.program_id(1) == 0)

        @pl.when(first)
        def _():
            dgh_ref[...] = jnp.zeros_like(dgh_ref)
            dlb_ref[...] = jnp.zeros_like(dlb_ref)

        @pl.when(pl.program_id(1) == 0)
        def _():
            ds_s[...] = jnp.zeros_like(ds_s)

        ones_b = ones_ref[...]
        ones_f = ones_b.astype(F32)
        bmask = _block_ones(w, F32)
        lbv, ghv = lb_ref[...], gh_ref[...]
        oh, ga, dm = oh_ref[...], ga_ref[...], dm_ref[...]
        rn = lax.rsqrt(_head_mean(oh * oh, ones_f) + NORM_EPS)
        nh = oh * rn
        sga = _sigmoid(ga)
        dp_ref[:, 3 * w:4 * w] = (dm * nh * ghv * (sga * (1.0 + ga * (1.0 - sga)))).astype(BF16)
        dn = dm * (ga * sga)
        dgh_ref[...] += jnp.sum(dn * nh, axis=0, keepdims=True)
        dn = dn * ghv
        do_s[...] = rn * (dn - nh * _head_mean(dn * nh, ones_f))
        q = q_ref[...]
        sig, sn, f, g, k, sq = _hgrn_gates(q, z_ref[...], lbv)
        qs = q * sq
        b_s[...] = _sel_dot(_sub_tri(tt, True), g)
        qs_s[...] = qs
        k_s[...] = k

        def chunk(cc, carry):
            c = ncs - 1 - cc
            base = pl.multiple_of(c * CHUNK, CHUNK)
            tiles = []
            for u in range(nsub):
                rows = pl.ds(base + u * SUB, SUB)
                tiles.append((qs_s[rows, :], k_s[rows, :], v_ref[rows, :], b_s[rows, :], do_s[rows, :]))
            st = s0_ref[c]
            for u, (qs, k, v, b, do) in enumerate(tiles):
                ss_s[u] = st
                if u < nsub - 1:
                    st = _hgrn_state_step(st, k, v, b, bmask)
            ds = ds_s[...]
            for u in reversed(range(nsub)):
                qs, k, v, b, do = tiles[u]
                _, es, ws = _hgrn_decays(qs, k, b)
                gs = [_pad_rows(do[t:t + 1, :] * v[:_live_rows(t)]) for t in range(SUB)]
                aexp = _dot(jnp.concatenate(ws, axis=0).astype(BF16), ones_b)
                gexp = _dot(jnp.concatenate(gs, axis=0).astype(BF16), ones_b)
                dq, dk, dv, dbl, ds = _hgrn_state_bwd(qs, k, v, b, do, ss_s[u], ds, bmask)

                def put_dq_row(i, row, r0=base + u * SUB):
                    dq_s[pl.ds(r0 + i, 1), :] = row

                dk, dv = _hgrn_intra_bwd(qs, k, v, do, es, aexp, gexp, dq, dk, dv, put_dq_row)
                dk_s[pl.ds(base + u * SUB, SUB), :] = dk
                dv_s[pl.ds(base + u * SUB, SUB), :] = dv
                dbl_s[pl.ds(base + u * SUB, SUB), :] = jnp.broadcast_to(dbl, (SUB, w))
            ds_s[...] = ds
            return carry

        lax.fori_loop(0, ncs, chunk, 0)
        dqs, dk = dq_s[...], dk_s[...]
        dg = _sel_dot(_sub_tri(tt, False), qs * dqs - k * dk) + dbl_s[...]
        dfz = jnp.where(f > TINY, dg / jnp.maximum(f, TINY), 0.0)
        dlb_ref[...] += jnp.sum(dfz * (1.0 - sig) - dk * sn, axis=0, keepdims=True)
        dp_ref[:, 0:w] = (dqs * (sq * (1.0 + q * (1.0 - sq)))).astype(BF16)
        dp_ref[:, w:2 * w] = ((dfz - dk) * (1.0 - lbv) * sig * sn).astype(BF16)
        dp_ref[:, 2 * w:3 * w] = dv_s[...].astype(BF16)

    def rv(b, i):
        return b * nt + (nt - 1 - i)

    def col(j):
        return pl.BlockSpec((tt, w), lambda b, i: (rv(b, i), j))

    def full(a, bb):
        return pl.BlockSpec((a, bb), lambda b, i: (0, 0))

    return _call(
        body, name, (dmix, proj, proj, proj, proj, o_h, s0, gh, lb, ones_b), rider, grid=(nb, nt),
        in_specs=[col(0), col(C_QA // w), col(C_FA // w), col(C_IA // w), col(C_GA // w), col(0),
                  pl.BlockSpec((ncs, w, w), lambda b, i: (rv(b, i), 0, 0)), full(1, w), full(1, w), full(w, w)],
        out_specs=[pl.BlockSpec((tt, 4 * w), lambda b, i: (rv(b, i), 0)), full(1, w), full(1, w)],
        out_shape=[_sds((n, 4 * w), BF16), _sds((1, w), F32), _sds((1, w), F32)],
        scratch_shapes=[pltpu.VMEM((w, w), F32), pltpu.VMEM((nsub, w, w), F32)] + [pltpu.VMEM((tt, w), F32)] * 8,
        compiler_params=_params(),
    )


def _pool_bwd(dmix, proj, wbd, wbd_t, scale, nb, name):
    n = proj.shape[0]
    t = n // nb
    tt = min(512, t)
    nt = t // tt
    nhb = tt // POOL_HALO
    cu, cg, cm = C_UB // POOL_W, C_GB // POOL_W, HGRN_W // POOL_W

    def body(u_ref, g_ref, h_ref, dm_ref, gn_ref, dmn_ref, w_ref, wt_ref, s_ref, dp_ref, dw_ref, ds_ref):
        i = pl.program_id(1)
        first = jnp.logical_and(pl.program_id(0) == 0, i == 0)

        @pl.when(first)
        def _():
            dw_ref[...] = jnp.zeros_like(dw_ref)
            ds_ref[...] = jnp.zeros_like(ds_ref)

        sc = s_ref[...]
        halo = jnp.where(i == 0, 0.0, h_ref[...])
        pooled, cnt = _pool_mix(u_ref[...], halo, i * tt, tt)
        pb = pooled.astype(BF16)
        pre = _dot(pb, w_ref[...])
        gv, dm = g_ref[...], dm_ref[...]
        sg = _sigmoid(gv)
        silu = gv * sg
        dgb = dm * pre * sc * (sg * (1.0 + gv * (1.0 - sg)))
        ds_ref[...] += jnp.sum(dm * pre * silu, axis=0, keepdims=True)
        dpre = (dm * sc * silu).astype(BF16)
        dw_ref[...] += _dot_tn(pb, dpre)
        dpool = _dot(dpre, wt_ref[...])
        gn = gn_ref[...]
        dpre_n = (dmn_ref[...] * sc * (gn * _sigmoid(gn))).astype(BF16)
        dpool_n = jnp.where(i == nt - 1, 0.0, _dot(dpre_n, wt_ref[...]))
        lane, wl = _pool_lane_windows()
        tpos_n = ((i + 1) * tt + lax.broadcasted_iota(jnp.int32, (POOL_HALO, POOL_W), 0)).astype(F32)
        ext = jnp.concatenate([dpool / cnt, dpool_n / jnp.minimum(tpos_n + 1.0, wl)], axis=0)
        rows = tt + POOL_HALO
        sums, cur, shift = [], ext, 1
        for _ in POOL_WINDOWS:
            cur = cur + pltpu.roll(cur, rows - shift, axis=0)
            sums.append(cur[:tt, :])
            shift *= 2
        du = _pool_select(lane, sums) - dpool
        dp_ref[...] = jnp.concatenate([du, dgb], axis=1).astype(BF16)

    def nxt(b, i):
        return jnp.minimum((b * nt + i + 1) * nhb, n // POOL_HALO - 1)

    return _pc(
        body, name, grid=(nb, nt),
        in_specs=_pool_specs(tt, nt, nhb) + [
            pl.BlockSpec((tt, POOL_W), lambda b, i: (b * nt + i, cm)),
            pl.BlockSpec((POOL_HALO, POOL_W), lambda b, i: (nxt(b, i), cg)),
            pl.BlockSpec((POOL_HALO, POOL_W), lambda b, i: (nxt(b, i), cm)),
            pl.BlockSpec((POOL_W, POOL_W), lambda b, i: (0, 0)), pl.BlockSpec((POOL_W, POOL_W), lambda b, i: (0, 0)),
            pl.BlockSpec((1, POOL_W), lambda b, i: (0, 0))],
        out_specs=[pl.BlockSpec((tt, 2 * POOL_W), lambda b, i: (b * nt + i, 0)),
                   pl.BlockSpec((POOL_W, POOL_W), lambda b, i: (0, 0)), pl.BlockSpec((1, POOL_W), lambda b, i: (0, 0))],
        out_shape=[_sds((n, 2 * POOL_W), BF16), _sds((POOL_W, POOL_W), F32), _sds((1, POOL_W), F32)],
        compiler_params=_params(),
    )(proj, proj, proj, dmix, proj, dmix, wbd, wbd_t, scale)


def _fox_bwd(proj, qt, kt, da, dat, c_col, c_row, lse_row, delta_row, nb, name, rider=None):
    n = proj.shape[0]
    t = n // nb
    tb = min(256, t)
    nq = t // tb
    pw = 2 * HEAD
    qw = 2 if nq % 2 == 0 else 1
    nqs = nq // qw

    def body(q_ref, k_ref, v_ref, da_ref, qt_ref, kt_ref, dat_ref, cc_ref, cr_ref, lse_ref, dl_ref,
             dq_ref, dk_ref, dv_ref, dck_ref, dcq_ref, dq_s, dk_s, dv_s, dck_s, dcq_s):
        step = pl.program_id(1)
        kj, qq = pairs(step)

        @pl.when(step == 0)
        def _():
            dq_s[...] = jnp.zeros_like(dq_s)
            dcq_s[...] = jnp.zeros_like(dcq_s)

        @pl.when(qq == nqs - 1)
        def _():
            dk_s[...] = jnp.zeros_like(dk_s)
            dv_s[...] = jnp.zeros_like(dv_s)
            dck_s[...] = jnp.zeros_like(dck_s)

        def block(masked, sub):
            lo = _lane_lo()
            qi = qq * qw + sub
            qs = slice(sub * tb, (sub + 1) * tb)
            if masked:
                causal = lax.broadcasted_iota(jnp.int32, (tb, tb), 1) >= lax.broadcasted_iota(jnp.int32, (tb, tb), 0)
            dck = dck_s[...]
            for p in range(FOX_HEADS // 2):
                sl = slice(p * pw, (p + 1) * pw)
                qp = q_ref[qs, sl] * FOX_SCALE
                kp = k_ref[:, sl].astype(BF16)
                vp = v_ref[:, sl].astype(BF16)
                dap = da_ref[qs, sl]
                dk, dv = dk_s[:, sl], dv_s[:, sl]
                for h in range(2):
                    hh = 2 * p + h
                    lm = lo if h == 0 else jnp.logical_not(lo)
                    rows = slice(hh * HEAD, (hh + 1) * HEAD)
                    none = jnp.zeros((HEAD, tb), BF16)
                    qm = jnp.where(lm, qp, 0.0).astype(BF16)
                    dam = jnp.where(lm, dap, jnp.zeros_like(dap))
                    qtm = jnp.concatenate([qt_ref[rows, qs], none] if h == 0 else [none, qt_ref[rows, qs]], axis=0)
                    datm = jnp.concatenate([dat_ref[rows, qs], none] if h == 0 else [none, dat_ref[rows, qs]], axis=0)
                    s = _dot(kp, qtm) + (cr_ref[hh:hh + 1, qs] - cc_ref[:, hh:hh + 1])
                    pe = jnp.exp(s - lse_ref[hh:hh + 1, qs])
                    if masked:
                        pe = jnp.where(causal, pe, 0.0)
                    dp = _dot(vp, datm)
                    ds = pe * (dp - dl_ref[hh:hh + 1, qs])
                    dsb = ds.astype(BF16)
                    dv = dv + _dot(pe.astype(BF16), dam)
                    dk = dk + _dot(dsb, qm)
                    dq_s[qi, rows, :] += _dot(kt_ref[rows, :], dsb)
                    dck = dck - _put_col(jnp.zeros_like(dck), hh, jnp.sum(ds, axis=1, keepdims=True))
                    dcq_s[qi, hh:hh + 1, :] += _rows_reduce(ds, jnp.add, jnp.sum)
                dk_s[:, sl] = dk
                dv_s[:, sl] = dv
            dck_s[...] = dck

        for sub in reversed(range(qw)):
            @pl.when(qq * qw + sub > kj)
            def _(sub=sub):
                block(False, sub)

            @pl.when(qq * qw + sub == kj)
            def _(sub=sub):
                block(True, sub)

        @pl.when(qq == kj // qw)
        def _():
            dk_ref[...] = dk_s[...].astype(BF16)
            dv_ref[...] = dv_s[...].astype(BF16)
            dck_ref[...] = dck_s[...]

        @pl.when(step == _tri_steps(nq, qw) - 1)
        def _():
            for j in range(nq):
                dq_ref[j * tb:(j + 1) * tb, :] = (dq_s[j].T * FOX_SCALE).astype(BF16)
                dcq_ref[:, j * tb:(j + 1) * tb] = dcq_s[j]

    def pairs(step):
        a, b = _tri_pair(step, nq, qw)
        return nq - 1 - a, nqs - 1 - b

    def kspec(wd, j=0):
        return pl.BlockSpec((tb, wd), lambda b, st: (b * nq + pairs(st)[0], j))

    def qspec(wd, j=0):
        return pl.BlockSpec((qw * tb, wd), lambda b, st: (b * nqs + pairs(st)[1], j))

    def qrow():
        return pl.BlockSpec((None, FOX_HEADS, qw * tb), lambda b, st: (b, 0, pairs(st)[1]))

    def tspec(which):
        if which == 0:
            return pl.BlockSpec((FOX_W, tb), lambda b, st: (0, b * nq + pairs(st)[0]))
        return pl.BlockSpec((FOX_W, qw * tb), lambda b, st: (0, b * nqs + pairs(st)[1]))

    return _call(
        body, name, (proj, proj, proj, da, qt, kt, dat, c_col, c_row, lse_row, delta_row), rider,
        grid=(nb, _tri_steps(nq, qw)),
        in_specs=[qspec(FOX_W, C_QC // FOX_W), kspec(FOX_W, C_KC // FOX_W), kspec(FOX_W, C_VC // FOX_W), qspec(FOX_W),
                  tspec(1), tspec(0), tspec(1), kspec(FC_PAD), qrow(), qrow(), qrow()],
        out_specs=[pl.BlockSpec((t, FOX_W), lambda b, st: (b, 0)), kspec(FOX_W), kspec(FOX_W), kspec(FC_PAD),
                   pl.BlockSpec((None, FOX_HEADS, t), lambda b, st: (b, 0, 0))],
        out_shape=[_sds((n, FOX_W), BF16), _sds((n, FOX_W), BF16), _sds((n, FOX_W), BF16), _sds((n, FC_PAD), F32),
                   _sds((nb, FOX_HEADS, t), F32)],
        scratch_shapes=[pltpu.VMEM((nq, FOX_W, tb), F32), pltpu.VMEM((tb, FOX_W), F32), pltpu.VMEM((tb, FOX_W), F32),
                        pltpu.VMEM((tb, FC_PAD), F32), pltpu.VMEM((nq, FOX_HEADS, tb), F32)],
        compiler_params=_params(),
    )


def _fox_decay_bwd(dc_q, dc_k, fc, bias, nb, name):
    n = fc.shape[0]
    t = n // nb
    tt = min(256, t)
    nt = t // tt

    def body(dcq_ref, dck_ref, fc_ref, b_ref, dfc_ref, db_ref, carry):
        i = pl.program_id(1)
        first = jnp.logical_and(pl.program_id(0) == 0, i == 0)

        @pl.when(first)
        def _():
            db_ref[...] = jnp.zeros_like(db_ref)

        @pl.when(i == 0)
        def _():
            carry[...] = jnp.zeros_like(carry)

        r = lax.broadcasted_iota(jnp.int32, (tt, tt), 0)
        cc = lax.broadcasted_iota(jnp.int32, (tt, tt), 1)
        dcq = jnp.concatenate([dcq_ref[...], jnp.zeros((FC_PAD - FOX_HEADS, tt), F32)], axis=0).T
        dlf = _dot_hi((r <= cc).astype(F32), dcq + dck_ref[...]) + carry[...]
        carry[...] = dlf[0:1, :]
        dfc = dlf * _sigmoid(-(fc_ref[...] + b_ref[...]))
        dfc_ref[...] = dfc.astype(BF16)
        db_ref[...] += jnp.sum(dfc, axis=0, keepdims=True)

    def row():
        return pl.BlockSpec((tt, FC_PAD), lambda b, i: (b * nt + (nt - 1 - i), 0))

    return _pc(
        body, name, grid=(nb, nt),
        in_specs=[pl.BlockSpec((None, FOX_HEADS, tt), lambda b, i: (b, 0, nt - 1 - i)), row(), row(),
                  pl.BlockSpec((1, FC_PAD), lambda b, i: (0, 0))],
        out_specs=[row(), pl.BlockSpec((1, FC_PAD), lambda b, i: (0, 0))],
        out_shape=[_sds((n, FC_PAD), BF16), _sds((1, FC_PAD), F32)],
        scratch_shapes=[pltpu.VMEM((1, FC_PAD), F32)],
        compiler_params=_params(),
    )(dc_q, dc_k, fc, bias)


def _in_proj_bwd(pieces, w_main_t, w_fc_t, x, g_pre, dxo, name, rider=None):
    n, d = x.shape
    tm = min(512, n)
    widths = [p.shape[1] for p, _ in pieces]
    offs = [o for _, o in pieces]
    np_ = len(pieces)

    def body(*refs):
        p_refs = refs[:np_]
        wt_ref, wf_ref, x_ref, g_ref, dxo_ref, dx_ref, dg_ref = refs[np_:]

        @pl.when(pl.program_id(0) == 0)
        def _():
            dg_ref[...] = jnp.zeros_like(dg_ref)

        dh = _dot(p_refs[-1][...], wf_ref[...])
        for pr, wd, off in zip(p_refs[:-1], widths[:-1], offs[:-1]):
            for j in range(0, wd, 512):
                jw = min(512, wd - j)
                dh = dh + _dot(pr[:, j:j + jw], wt_ref[off + j:off + j + jw, :])
        xv = x_ref[...]
        r = lax.rsqrt(jnp.mean(xv * xv, axis=-1, keepdims=True) + NORM_EPS)
        xh = xv * r
        dg_ref[...] += jnp.sum(dh * xh, axis=0, keepdims=True)
        dx_ref[...] = dxo_ref[...] + _rms_bwd(dh * g_ref[...], xh, r)

    row = pl.BlockSpec((tm, d), lambda i: (i, 0))
    return _call(
        body, name, (*[p for p, _ in pieces], w_main_t, w_fc_t, x, g_pre, dxo), rider, grid=(n // tm,),
        in_specs=[pl.BlockSpec((tm, wd), lambda i: (i, 0)) for wd in widths] + [
            pl.BlockSpec((MAIN_W, d), lambda i: (0, 0)), pl.BlockSpec((FC_PAD, d), lambda i: (0, 0)),
            row, pl.BlockSpec((1, d), lambda i: (0, 0)), row],
        out_specs=[row, pl.BlockSpec((1, d), lambda i: (0, 0))],
        out_shape=[_sds((n, d), F32), _sds((1, d), F32)],
        compiler_params=_params(),
    )


def _lower_bound_table(lower_bounds, name):
    depth, w = lower_bounds.shape

    def body(lb_ref, o_ref):
        v = lb_ref[...]
        e = jnp.exp(v - jnp.max(v, axis=0, keepdims=True))
        p = e / jnp.sum(e, axis=0, keepdims=True)
        acc = jnp.zeros((1, w), F32)
        for l in range(depth):
            acc = acc + p[l:l + 1, :]
            o_ref[l:l + 1, :] = acc - p[0:1, :]

    return _pc(body, name, out_shape=_sds((depth, w), F32))(lower_bounds)


def _lower_bound_bwd(lower_bounds, dlbs, name):
    depth, w = lower_bounds.shape

    def body(lb_ref, d_ref, o_ref):
        v, dl = lb_ref[...], d_ref[...]
        e = jnp.exp(v - jnp.max(v, axis=0, keepdims=True))
        p = e / jnp.sum(e, axis=0, keepdims=True)
        tot = jnp.sum(dl, axis=0, keepdims=True)
        rows, tail = [], tot
        for l in range(depth):
            rows.append(tail - tot if l == 0 else tail)
            tail = tail - dl[l:l + 1, :]
        dp = jnp.concatenate(rows, axis=0)
        o_ref[...] = p * (dp - jnp.sum(p * dp, axis=0, keepdims=True))

    return _pc(body, name, out_shape=_sds((depth, w), F32))(lower_bounds, dlbs)


def _place():
    x, y, c = lax.axis_index("x"), lax.axis_index("y"), lax.axis_index("c")
    return x, y, c


def _gather_weights(*arrays):
    na = len(arrays)

    def body(*refs):
        ins, outs = refs[:na], refs[na:2 * na]
        send_sems, recv_sems, local_sems = refs[2 * na:]
        x, y, c = _place()
        me, sibling = (x, y, c), (x, y, 1 - c)
        chips = [(1 - x, y), (x, 1 - y), (1 - x, 1 - y)]

        def slot(a, px, py, pc):
            return outs[a].at[4 * px + 2 * py + pc]

        def copy(a, k, block, to, own=False):
            return pltpu.make_async_remote_copy(
                src_ref=ins[a] if own else slot(a, *block), dst_ref=slot(a, *block),
                send_sem=send_sems.at[a * 7 + k], recv_sem=recv_sems.at[a * 7 + k],
                device_id=to, device_id_type=MESH)

        mine = [pltpu.make_async_copy(ins[a], slot(a, *me), local_sems.at[a]) for a in range(na)]
        for cp in mine:
            cp.start()
        first = []
        for a in range(na):
            first.append(copy(a, 0, me, sibling, own=True))
            first += [copy(a, 1 + j, me, (*chip, c), own=True) for j, chip in enumerate(chips)]
        for cp in first:
            cp.start()
        passed = []
        for j, chip in enumerate(chips):
            for a in range(na):
                copy(a, 1 + j, (*chip, c), me).wait_recv()
                fw = copy(a, 4 + j, (*chip, c), sibling)
                fw.start()
                passed.append(fw)
        for a in range(na):
            copy(a, 0, sibling, me).wait_recv()
            for j, chip in enumerate(chips):
                copy(a, 4 + j, (*chip, 1 - c), me).wait_recv()
        for cp in first + passed:
            cp.wait_send()
        for cp in mine:
            cp.wait()

    any_spec = pl.BlockSpec(memory_space=pl.ANY)
    return _pc(
        body, "gather_weights",
        in_specs=[any_spec] * na, out_specs=[any_spec] * na,
        out_shape=[_sds((N_DEV,) + a.shape, a.dtype) for a in arrays],
        scratch_shapes=[pltpu.SemaphoreType.DMA((7 * na,)), pltpu.SemaphoreType.DMA((7 * na,)),
                        pltpu.SemaphoreType.DMA((na,))],
    )(*arrays)


def _peer(k):
    x, y, c = _place()
    return (1 - x if k & 4 else x, 1 - y if k & 2 else y, 1 - c if k & 1 else c)


def _remote(src, dst, sems, s, to):
    return pltpu.make_async_remote_copy(src_ref=src, dst_ref=dst, send_sem=sems[0].at[s], recv_sem=sems[1].at[s],
                                        device_id=to, device_id_type=MESH)


def _gather_rider(shards):
    na = len(shards)

    def plan(ins, outs, *sems):
        x, y, c = _place()
        me = 4 * x + 2 * y + c
        locs = [pltpu.make_async_copy(ins[a], outs[a].at[me], sems[2].at[a]) for a in range(na)]
        sends, recvs = [], []
        for k in range(1, N_DEV):
            px, py, pc = _peer(k)
            for a in range(na):
                s = (k - 1) * na + a
                sends.append(_remote(ins[a], outs[a].at[me], sems, s, (px, py, pc)))
                recvs.append(_remote(ins[a], outs[a].at[4 * px + 2 * py + pc], sems, s, (px, py, pc)))
        return sends, recvs, locs

    return _Rider(shards, [_sds((N_DEV,) + a.shape, a.dtype) for a in shards], (N_DEV - 1) * na, na, plan)


def _direct_exchange_rider(blocks):
    na = len(blocks)

    def plan(ins, outs, *sems):
        x, y, c = _place()
        me = 4 * x + 2 * y + c
        locs = [pltpu.make_async_copy(ins[a].at[c, 2 * x + y], outs[a].at[me], sems[2].at[a]) for a in range(na)]
        sends, recvs = [], []
        for k in range(1, N_DEV):
            px, py, pc = _peer(k)
            for a in range(na):
                s = (k - 1) * na + a
                sends.append(_remote(ins[a].at[pc, 2 * px + py], outs[a].at[me], sems, s, (px, py, pc)))
                recvs.append(_remote(ins[a].at[pc, 2 * px + py], outs[a].at[4 * px + 2 * py + pc], sems, s, (px, py, pc)))
        return sends, recvs, locs

    return _Rider(blocks, [_sds((N_DEV,) + a.shape[2:], a.dtype) for a in blocks], (N_DEV - 1) * na, na, plan)


def _swap_rider(halves):
    na = len(halves)

    def plan(ins, outs, *sems):
        x, y, c = _place()
        cps = [_remote(ins[a].at[1 - c], outs[a], sems, a, (x, y, 1 - c)) for a in range(na)]
        return cps, cps, []

    return _Rider(halves, [_sds(a.shape[1:], a.dtype) for a in halves], na, 1, plan)


def _chip_exchange_rider(parts, small=None):
    na = len(parts)
    n_chip = N_DEV // 2

    def plan(ins, outs, *sems):
        x, y, c = _place()
        chip = 2 * x + y
        locs = [pltpu.make_async_copy(ins[a].at[chip], outs[a].at[chip], sems[2].at[a]) for a in range(na)]
        sends, recvs = [], []
        for k in range(1, n_chip):
            px, py, _ = _peer(2 * k)
            for a in range(na):
                s = (k - 1) * na + a
                sends.append(_remote(ins[a].at[2 * px + py], outs[a].at[chip], sems, s, (px, py, c)))
                recvs.append(_remote(ins[a].at[2 * px + py], outs[a].at[2 * px + py], sems, s, (px, py, c)))
        if small is not None:
            me = 2 * chip + c
            locs.append(pltpu.make_async_copy(ins[na], outs[na].at[me], sems[2].at[na]))
            for k in range(1, N_DEV):
                px, py, pc = _peer(k)
                s = (n_chip - 1) * na + k - 1
                sends.append(_remote(ins[na], outs[na].at[me], sems, s, (px, py, pc)))
                recvs.append(_remote(ins[na], outs[na].at[4 * px + 2 * py + pc], sems, s, (px, py, pc)))
        return sends, recvs, locs

    extra = [] if small is None else [small]
    shapes = [_sds(a.shape, a.dtype) for a in parts] + [_sds((N_DEV,) + s.shape, s.dtype) for s in extra]
    n_sems = (n_chip - 1) * na + (N_DEV - 1) * len(extra)
    return _Rider(list(parts) + extra, shapes, n_sems, na + len(extra), plan)


def _pair_add(halves, other, core, name):
    _, nch, r, c = halves.shape

    def body(c_ref, h_ref, o_ref, p_ref):
        p_ref[...] = (h_ref[...].astype(F32) + o_ref[...].astype(F32)).astype(BF16)

    blk = pl.BlockSpec((None, r, c), lambda j, c_ref: (j, 0, 0))
    return _pc(
        body, name,
        grid_spec=pltpu.PrefetchScalarGridSpec(
            num_scalar_prefetch=1, grid=(nch,),
            in_specs=[pl.BlockSpec((None, None, r, c), lambda j, c_ref: (c_ref[0], j, 0, 0)), blk], out_specs=blk),
        out_shape=_sds((nch, r, c), BF16),
        compiler_params=_params(),
    )(core, halves, other)


def _sum_adamw(parts, w, m, v, name, rider=None):
    nl, r, c = w.shape
    tr = 256 if r % 256 == 0 else r

    def body(*refs):
        p_refs = refs[:nl]
        w_ref, m_ref, v_ref, g_ref, d_ref, mo_ref, vo_ref = refs[nl:]
        for l in range(nl):
            @pl.when(pl.program_id(0) == l)
            def _(p_ref=p_refs[l]):
                g = p_ref[0].astype(F32)
                for j in range(1, p_ref.shape[0]):
                    g = g + p_ref[j].astype(F32)
                mn = ADAM_B1 * m_ref[...] + (1.0 - ADAM_B1) * g
                vn = ADAM_B2 * v_ref[...] + (1.0 - ADAM_B2) * (g * g)
                m_hat = mn / (1.0 - ADAM_B1 ** ADAM_STEP)
                v_hat = vn / (1.0 - ADAM_B2 ** ADAM_STEP)
                g_ref[...] = g
                d_ref[...] = -ADAM_LR * (m_hat / (jnp.sqrt(v_hat) + ADAM_EPS) + ADAM_WD * w_ref[...])
                mo_ref[...] = mn
                vo_ref[...] = vn

    def part_spec(l, k):
        return pl.BlockSpec((k, tr, c), lambda li, i: (0, jnp.where(li == l, i, 0), 0))

    row = pl.BlockSpec((None, tr, c), lambda li, i: (li, i, 0))
    return _call(
        body, name, (*parts, w, m, v), rider, grid=(nl, r // tr),
        in_specs=[part_spec(l, p.shape[0]) for l, p in enumerate(parts)] + [row, row, row],
        out_specs=[row] * 4,
        out_shape=[_sds((nl, r, c), F32)] * 4,
        compiler_params=_params(),
    )


SMALL = ("lower_bounds", "pre_norm_g", "hgrn_norm_g", "pool_w", "pool_scale", "post_norm_g", "fox_f_bias")
SMALL_LANES = 128


def _small_size(tree):
    return sum(tree[k].size for k in SMALL)


def _pack_small(tree, extra=None):
    flat = jnp.concatenate([tree[k].reshape(-1) for k in SMALL] + ([] if extra is None else [extra.reshape(1)]))
    rows = -(-(_small_size(tree) + 1) // (8 * SMALL_LANES)) * 8
    return jnp.pad(flat, (0, rows * SMALL_LANES - flat.shape[0])).reshape(rows, SMALL_LANES)


def _unpack_small(packed, like):
    out, off = {}, 0
    for k in SMALL:
        size = like[k].size
        assert off % SMALL_LANES == 0
        rows = packed[off // SMALL_LANES:-(-(off + size) // SMALL_LANES)]
        out[k] = rows.reshape(-1)[:size].reshape(like[k].shape)
        off += size
    return out


def _block_diag(pw):
    g = pw.shape[0]
    eye = jnp.eye(g, dtype=pw.dtype)
    return (eye[:, None, :, None] * pw[:, :, None, :]).reshape(g * HEAD, g * HEAD)


def _assemble_w_in(g_in, name):
    _, d, shard = g_in.shape
    tr = min(256, d)
    wide = MAIN_W + FC_PAD

    def body(g_ref, wm_ref, wf_ref, wmt_ref, wft_ref, row_s):
        row_s[:, MAIN_W:] = jnp.zeros((tr, FC_PAD), F32)
        for j in range(N_DEV):
            row_s[:, j * shard:(j + 1) * shard] = g_ref[j].astype(F32)
        wm_ref[...] = row_s[:, :MAIN_W].astype(BF16)
        wf_ref[...] = row_s[:, MAIN_W:].astype(BF16)
        for j in range(0, MAIN_W, 512):
            wmt_ref[j:j + 512, :] = row_s[:, j:j + 512].T.astype(BF16)
        wft_ref[...] = row_s[:, MAIN_W:].T.astype(BF16)

    return _pc(
        body, name, grid=(d // tr,),
        in_specs=[pl.BlockSpec((N_DEV, tr, shard), lambda i: (0, i, 0))],
        out_specs=[pl.BlockSpec((tr, MAIN_W), lambda i: (i, 0)), pl.BlockSpec((tr, FC_PAD), lambda i: (i, 0)),
                   pl.BlockSpec((MAIN_W, tr), lambda i: (0, i)), pl.BlockSpec((FC_PAD, tr), lambda i: (0, i))],
        out_shape=[_sds((d, MAIN_W), BF16), _sds((d, FC_PAD), BF16), _sds((MAIN_W, d), BF16), _sds((FC_PAD, d), BF16)],
        scratch_shapes=[pltpu.VMEM((tr, wide), F32)],
        compiler_params=_params(),
    )(g_in)


def _w_out_parts(g_out):
    full_out = g_out.reshape(N_DEV * g_out.shape[1], g_out.shape[2])
    return full_out, full_out.T


def _layer_fwd(l, x, lbs, weights, lw, nb, rider_h=None, rider_c=None, target=None):
    n = x.shape[0]
    t = n // nb
    w_main, w_fc, _, _, w_out, _ = lw
    bias = jnp.pad(weights["fox_f_bias"][l:l + 1], ((0, 0), (0, FC_PAD - FOX_HEADS)))
    wbd = _block_diag(weights["pool_w"][l]).astype(BF16)
    proj, fc, ht, qt, kt = _in_proj_fwd(x, weights["pre_norm_g"][l:l + 1], w_main, w_fc, f"in_proj_fwd_{l}")
    c_col, c_row = _fox_decay_fwd(fc, bias, nb, f"fox_decay_fwd_{l}")
    (o_h, s0), rode_h = _hgrn_fwd(proj, lbs[l:l + 1], _block_ones(HGRN_W, BF16), nb, f"hgrn_fwd_{l}", rider_h)
    o_b = _pool_fwd(proj, wbd, weights["pool_scale"][l:l + 1], nb, f"pool_fwd_{l}")
    (o_c, lse), rode_c = _fox_fwd(proj, kt, c_col, c_row, nb, f"fox_fwd_{l}", rider_c)
    if w_out is None:
        lw = tuple(lw[:4]) + _w_out_parts(rode_h[0])
        w_out, rode_h = lw[4], rode_h[1:]
    x_next, mixt, y = _merge_fwd(x, proj, o_h, o_b, o_c, weights["hgrn_norm_g"][l:l + 1], w_out,
                                 weights["post_norm_g"][l:l + 1], f"merge_fwd_{l}", target)
    return x_next, (x, proj, fc, ht, qt, kt, c_col, c_row, o_h, s0, o_c, lse, mixt, y, bias, wbd), lw, (rode_h, rode_c)


def _layer_bwd(l, dx, saved, lbs, weights, lw, nb, rider=None, send_w_out=None):
    _, proj, fc, ht, qt, kt, c_col, c_row, o_h, s0, o_c, lse, mixt, y, bias, wbd = saved
    w_out_t = lw[5]
    g = {}
    dy, dmix, dgp, da, dat, d_gc, delta = _merge_bwd(dx, y, weights["post_norm_g"][l:l + 1], w_out_t, proj, o_c,
                                                nb, f"merge_bwd_{l}")
    g["post_norm_g"] = dgp[0]
    g["w_out"] = _w_out_grad(mixt, dy, f"w_out_grad_{l}")
    (d_a, dgh, dlb), arrived = _hgrn_bwd(dmix, proj, o_h, s0, weights["hgrn_norm_g"][l:l + 1], lbs[l:l + 1],
                                         _block_ones(HGRN_W, BF16), nb, f"hgrn_bwd_{l}",
                                         None if send_w_out is None else send_w_out([g["w_out"]]))
    if arrived is not None:
        g["w_out_received"] = arrived[0]
    g["hgrn_norm_g"], g["lbs"] = dgh[0], dlb[0]
    d_b, dwbd, dps = _pool_bwd(dmix, proj, wbd, wbd.T, weights["pool_scale"][l:l + 1], nb, f"pool_bwd_{l}")
    g["pool_w"] = jnp.stack([dwbd[j * HEAD:(j + 1) * HEAD, j * HEAD:(j + 1) * HEAD] for j in range(len(POOL_WINDOWS))])
    g["pool_scale"] = dps[0]
    (d_qc, d_kc, d_vc, dc_k, dc_q), rode = _fox_bwd(proj, qt, kt, da, dat, c_col, c_row, lse, delta, nb,
                                                    f"fox_bwd_{l}", rider)
    d_fc, dbias = _fox_decay_bwd(dc_q, dc_k, fc, bias, nb, f"fox_decay_bwd_{l}")
    g["fox_f_bias"] = dbias[0, :FOX_HEADS]
    pieces = [(d_a, C_QA), (d_b, C_UB), (d_qc, C_QC), (d_kc, C_KC), (d_vc, C_VC), (d_gc, C_GC), (d_fc, None)]
    g["w_in"] = _w_in_grad(ht, pieces, f"w_in_grad_{l}")
    return g, pieces, rode


def _layer_bwd_input(l, dx, pieces, saved, weights, lw, rider=None):
    (dxi, dgpre), rode = _in_proj_bwd(pieces, lw[2], lw[3], saved[0], weights["pre_norm_g"][l:l + 1], dx,
                                      f"in_proj_bwd_{l}", rider)
    return dxi, dgpre[0], rode


def kernel(x, lower_bounds, pre_norm_g, w_in, hgrn_norm_g, fox_f_bias, pool_w, pool_scale, w_out, post_norm_g, loss_target, m_lower_bounds, m_pre_norm_g, m_w_in, m_hgrn_norm_g, m_fox_f_bias, m_pool_w, m_pool_scale, m_w_out, m_post_norm_g, v_lower_bounds, v_pre_norm_g, v_w_in, v_hgrn_norm_g, v_fox_f_bias, v_pool_w, v_pool_scale, v_w_out, v_post_norm_g):
    weights = dict(lower_bounds=lower_bounds, pre_norm_g=pre_norm_g, hgrn_norm_g=hgrn_norm_g, fox_f_bias=fox_f_bias,
                   pool_w=pool_w, pool_scale=pool_scale, post_norm_g=post_norm_g)
    mom_m = dict(lower_bounds=m_lower_bounds, pre_norm_g=m_pre_norm_g, hgrn_norm_g=m_hgrn_norm_g, fox_f_bias=m_fox_f_bias,
                 pool_w=m_pool_w, pool_scale=m_pool_scale, post_norm_g=m_post_norm_g)
    mom_v = dict(lower_bounds=v_lower_bounds, pre_norm_g=v_pre_norm_g, hgrn_norm_g=v_hgrn_norm_g, fox_f_bias=v_fox_f_bias,
                 pool_w=v_pool_w, pool_scale=v_pool_scale, post_norm_g=v_post_norm_g)
    depth = w_in.shape[0]
    nb, t, d = x.shape
    n = nb * t
    core = lax.axis_index("c").astype(jnp.int32).reshape(1)
    shards = [(w_in[l].astype(BF16), w_out[l].astype(BF16)) for l in range(depth)]
    lbs = _lower_bound_table(lower_bounds, "lower_bound_table")

    (g_in,) = _gather_weights(shards[0][0])
    coming = tuple(_assemble_w_in(g_in, "assemble_w_in_0")) + (None, None)
    xl, saved, lw = x.reshape(n, d), [], []
    for l in range(depth):
        last = l + 1 == depth
        ride_h = ([shards[l][1]] if coming[4] is None else []) + ([] if last else [shards[l + 1][1]])
        xl, sv, lw_l, (rode_h, rode_c) = _layer_fwd(
            l, xl, lbs, weights, coming, nb, _gather_rider(ride_h) if ride_h else None,
            None if last else _gather_rider([shards[l + 1][0]]), loss_target.reshape(n, d) if last else None)
        saved.append(sv)
        lw.append(lw_l)
        if not last:
            coming = tuple(_assemble_w_in(rode_c[0], f"assemble_w_in_{l + 1}")) + _w_out_parts(rode_h[0])
    dx, sq = xl
    loss_here = 0.5 * jnp.sum(sq) / d

    grads, recv_in, pending = [None] * depth, [None] * depth, None
    for l in reversed(range(depth)):
        g, pieces, rode = _layer_bwd(l, dx, saved[l], lbs, weights, lw[l], nb, pending, _direct_exchange_rider)
        if rode is not None:
            recv_in[l + 1] = rode[0]
        if l > 0:
            pending = _direct_exchange_rider([g["w_in"]])
            dx, g["pre_norm_g"], _ = _layer_bwd_input(l, dx, pieces, saved[l], weights, lw[l])
        else:
            (other,) = _run_rider(_swap_rider([g["w_in"]]), "grad_swap")
            summed = _pair_add(g["w_in"], other, core, "grad_pair_add")
            dx, g["pre_norm_g"], (recv_in[l],) = _layer_bwd_input(l, dx, pieces, saved[l], weights, lw[l],
                                                                  _chip_exchange_rider([summed]))
        grads[l] = g
    small = {k: jnp.stack([grads[l][k] for l in range(depth)]) for k in SMALL if k != "lower_bounds"}
    small["lower_bounds"] = _lower_bound_bwd(lower_bounds, jnp.stack([grads[l]["lbs"] for l in range(depth)]),
                                             "lower_bound_bwd")
    (r_small,) = _run_rider(_gather_rider([_pack_small(small, loss_here)]), "small_grads_gather")

    res_in, _ = _sum_adamw(recv_in, w_in, m_w_in, v_w_in, "adamw_w_in")
    res_out, _ = _sum_adamw([grads[l]["w_out_received"] for l in range(depth)], w_out, m_w_out, v_w_out, "adamw_w_out")
    res_small, _ = _sum_adamw([r_small], _pack_small(weights)[None], _pack_small(mom_m)[None], _pack_small(mom_v)[None],
                              "adamw_small")
    loss = res_small[0][0][_small_size(weights) // SMALL_LANES, _small_size(weights) % SMALL_LANES]

    names = ("lower_bounds", "pre_norm_g", "w_in", "hgrn_norm_g", "fox_f_bias", "pool_w", "pool_scale", "w_out", "post_norm_g")
    outs = [loss, dx.reshape(nb, t, d)]
    for i in range(4):
        full = dict(_unpack_small(res_small[i][0], weights), w_in=res_in[i], w_out=res_out[i])
        outs += [full[k] for k in names]
    return tuple(outs)
```

```python
import functools

import jax
import jax.numpy as jnp
from jax import lax
from jax.experimental import pallas as pl
from jax.experimental.pallas import tpu as pltpu

F32, BF16 = jnp.float32, jnp.bfloat16
MESH = pl.DeviceIdType.MESH
N_DEV = 8

NORM_EPS = 1e-6
MASK_VALUE = -1e30
TINY = 1e-30
CHUNK = 64
SUB = 16
HGRN_W, POOL_W, FOX_W = 256, 256, 512
HEAD = 64
FOX_HEADS = 8
POOL_WINDOWS = (2, 4, 8, 16)
POOL_HALO = 16
MAIN_W = 3584
FC_PAD = 128
C_QA, C_FA, C_IA, C_GA, C_UB, C_GB, C_QC, C_KC, C_VC, C_GC = 0, 256, 512, 768, 1024, 1280, 1536, 2048, 2560, 3072
FOX_SCALE = HEAD ** -0.5

ADAM_LR, ADAM_B1, ADAM_B2, ADAM_EPS, ADAM_WD, ADAM_STEP = 0.001, 0.9, 0.999, 1e-08, 0.01, 10

VMEM_LIMIT = 56 * 1024 * 1024


def _pc(fn, name, **kw):
    return pl.pallas_call(fn, name=name, **kw)


def _params(**kw):
    return pltpu.CompilerParams(vmem_limit_bytes=VMEM_LIMIT, **kw)


class _Rider:
    def __init__(self, inputs, out_shapes, n_sems, n_local, plan):
        self.inputs, self.out_shapes, self.n_sems, self.n_local, self.plan = list(inputs), list(out_shapes), n_sems, n_local, plan

    def start(self, ins, outs, *sems):
        sends, _, locs = self.plan(ins, outs, *sems)
        for cp in locs + sends:
            cp.start()

    def wait(self, ins, outs, *sems):
        sends, recvs, locs = self.plan(ins, outs, *sems)
        for cp in recvs:
            cp.wait_recv()
        for cp in sends:
            cp.wait_send()
        for cp in locs:
            cp.wait()

    def sem_shapes(self):
        return [pltpu.SemaphoreType.DMA((self.n_sems,)), pltpu.SemaphoreType.DMA((self.n_sems,)),
                pltpu.SemaphoreType.DMA((self.n_local,))]


def _call(body, name, args, rider=None, *, grid, in_specs, out_specs, out_shape, scratch_shapes=(), **kw):
    if rider is None:
        res = _pc(body, name, grid=grid, in_specs=in_specs, out_specs=out_specs, out_shape=out_shape,
                  scratch_shapes=list(scratch_shapes), **kw)(*args)
        return res, None
    n_in, n_out, n_scr = len(in_specs), len(out_specs), len(scratch_shapes)
    n_rin, n_rout = len(rider.inputs), len(rider.out_shapes)

    def ridden(*refs):
        ins, refs = refs[:n_in], refs[n_in:]
        rins, refs = refs[:n_rin], refs[n_rin:]
        outs, refs = refs[:n_out], refs[n_out:]
        routs, refs = refs[:n_rout], refs[n_rout:]
        scr, sems = refs[:n_scr], refs[n_scr:]
        first = functools.reduce(jnp.logical_and, [pl.program_id(a) == 0 for a in range(len(grid))])
        last = functools.reduce(jnp.logical_and, [pl.program_id(a) == g - 1 for a, g in enumerate(grid)])

        @pl.when(first)
        def _():
            rider.start(rins, routs, *sems)

        body(*ins, *outs, *scr)

        @pl.when(last)
        def _():
            rider.wait(rins, routs, *sems)

    any_spec = pl.BlockSpec(memory_space=pl.ANY)
    res = _pc(ridden, name, grid=grid, in_specs=list(in_specs) + [any_spec] * n_rin,
              out_specs=list(out_specs) + [any_spec] * n_rout, out_shape=list(out_shape) + rider.out_shapes,
              scratch_shapes=list(scratch_shapes) + rider.sem_shapes(), **kw)(*args, *rider.inputs)
    return res[:n_out], res[n_out:]


def _run_rider(rider, name):
    n_rin = len(rider.inputs)

    def body(*refs):
        ins, outs, sems = refs[:n_rin], refs[n_rin:n_rin + len(rider.out_shapes)], refs[n_rin + len(rider.out_shapes):]
        rider.start(ins, outs, *sems)
        rider.wait(ins, outs, *sems)

    any_spec = pl.BlockSpec(memory_space=pl.ANY)
    return _pc(body, name, in_specs=[any_spec] * n_rin, out_specs=[any_spec] * len(rider.out_shapes),
               out_shape=rider.out_shapes, scratch_shapes=rider.sem_shapes())(*rider.inputs)


def _dot(a, b):
    return jnp.dot(a, b, preferred_element_type=F32)


def _dot_nt(a, b):
    return lax.dot_general(a, b, (((1,), (1,)), ((), ())), preferred_element_type=F32)


def _dot_tn(a, b):
    return lax.dot_general(a, b, (((0,), (0,)), ((), ())), preferred_element_type=F32)


def _sel_dot_exact(sel, x):
    hi = x.astype(BF16)
    rest = x - hi.astype(F32)
    mid = rest.astype(BF16)
    lo = (rest - mid.astype(F32)).astype(BF16)
    sb = sel.astype(BF16)
    return _dot(sb, hi) + _dot(sb, mid) + _dot(sb, lo)


def _split2(x):
    hi = x.astype(BF16)
    return hi, (x - hi.astype(F32)).astype(BF16)


def _sel_dot(sel, x):
    hi, lo = _split2(x)
    sb = sel.astype(BF16)
    return _dot(sb, hi) + _dot(sb, lo)


def _dot_sel(x, sel):
    hi, lo = _split2(x)
    sb = sel.astype(BF16)
    return _dot(hi, sb) + _dot(lo, sb)


def _sigmoid(x):
    return 1.0 / (1.0 + jnp.exp(-x))


def _block_ones(n, dtype):
    r = lax.broadcasted_iota(jnp.int32, (n, n), 0) // HEAD
    c = lax.broadcasted_iota(jnp.int32, (n, n), 1) // HEAD
    return (r == c).astype(dtype)


def _sds(shape, dtype):
    return jax.ShapeDtypeStruct(shape, dtype)


def _in_proj_fwd(x, g_pre, w_main, w_fc, name):
    n, d = x.shape
    tm = min(512, n)

    def body(x_ref, g_ref, w_ref, wf_ref, proj_ref, fc_ref, ht_ref, qt_ref, kt_ref):
        xv = x_ref[...]
        r = lax.rsqrt(jnp.mean(xv * xv, axis=-1, keepdims=True) + NORM_EPS)
        hf = xv * r * g_ref[...]
        hb = hf.astype(BF16)
        ht_ref[...] = hf.T.astype(BF16)
        for j in range(0, MAIN_W, FOX_W):
            res = _dot(hb, w_ref[:, j:j + FOX_W])
            proj_ref[:, j:j + FOX_W] = res
            if j == C_QC:
                qt_ref[...] = (res * FOX_SCALE).T.astype(BF16)
            if j == C_KC:
                kt_ref[...] = res.T.astype(BF16)
        fc_ref[...] = _dot(hb, wf_ref[...])

    def cols(rows):
        return pl.BlockSpec((rows, tm), lambda i: (0, i))

    return _pc(
        body, name, grid=(n // tm,),
        in_specs=[pl.BlockSpec((tm, d), lambda i: (i, 0)), pl.BlockSpec((1, d), lambda i: (0, 0)),
                  pl.BlockSpec((d, MAIN_W), lambda i: (0, 0)), pl.BlockSpec((d, FC_PAD), lambda i: (0, 0))],
        out_specs=[pl.BlockSpec((tm, MAIN_W), lambda i: (i, 0)), pl.BlockSpec((tm, FC_PAD), lambda i: (i, 0)),
                   cols(d), cols(FOX_W), cols(FOX_W)],
        out_shape=[_sds((n, MAIN_W), F32), _sds((n, FC_PAD), F32), _sds((d, n), BF16), _sds((FOX_W, n), BF16),
                   _sds((FOX_W, n), BF16)],
        compiler_params=_params(),
    )(x, g_pre, w_main, w_fc)


def _fox_decay_fwd(fc, bias, nb, name):
    n = fc.shape[0]
    t = n // nb
    tt = min(256, t)
    nt = t // tt

    def body(fc_ref, b_ref, c_ref, cr_ref, carry):
        i = pl.program_id(1)

        @pl.when(i == 0)
        def _():
            carry[...] = jnp.zeros_like(carry)

        xv = fc_ref[...] + b_ref[...]
        lf = jnp.minimum(xv, 0.0) - jnp.log(1.0 + jnp.exp(-jnp.abs(xv)))
        r = lax.broadcasted_iota(jnp.int32, (tt, tt), 0)
        cc = lax.broadcasted_iota(jnp.int32, (tt, tt), 1)
        cs = _sel_dot_exact(r >= cc, lf) + carry[...]
        c_ref[...] = cs
        cr_ref[...] = cs.T[:FOX_HEADS, :]
        carry[...] = cs[tt - 1:tt, :]

    return _pc(
        body, name, grid=(nb, nt),
        in_specs=[pl.BlockSpec((tt, FC_PAD), lambda b, i: (b * nt + i, 0)), pl.BlockSpec((1, FC_PAD), lambda b, i: (0, 0))],
        out_specs=[pl.BlockSpec((tt, FC_PAD), lambda b, i: (b * nt + i, 0)),
                   pl.BlockSpec((None, FOX_HEADS, tt), lambda b, i: (b, 0, i))],
        out_shape=[_sds((n, FC_PAD), F32), _sds((nb, FOX_HEADS, t), F32)],
        scratch_shapes=[pltpu.VMEM((1, FC_PAD), F32)],
        compiler_params=_params(),
    )(fc, bias)


def _hgrn_gates(q, z, lb):
    sig = _sigmoid(z)
    sn = _sigmoid(-z)
    f = lb + (1.0 - lb) * sig
    g = jnp.log(jnp.maximum(f, TINY))
    k = (1.0 - lb) * sn
    sq = _sigmoid(q)
    return sig, sn, f, g, k, sq


def _sub_tri(n, lower):
    r = lax.broadcasted_iota(jnp.int32, (n, n), 0)
    c = lax.broadcasted_iota(jnp.int32, (n, n), 1)
    tri = (r >= c) if lower else (r <= c)
    return jnp.logical_and(r // SUB == c // SUB, tri).astype(F32)


def _live_rows(t):
    return 8 * (t // 8 + 1)


def _pad_rows(x):
    return x if x.shape[0] == SUB else jnp.concatenate([x, jnp.zeros((SUB - x.shape[0], x.shape[1]), x.dtype)], axis=0)


def _hgrn_decays(qs, k, b):
    srow = lax.broadcasted_iota(jnp.int32, (SUB, HGRN_W), 0)
    es, ws = [], []
    for t in range(SUB):
        r = _live_rows(t)
        e = jnp.where(srow[:r] <= t, jnp.exp(b[t:t + 1, :] - b[:r]), 0.0)
        es.append(e)
        ws.append(_pad_rows(e * (qs[t:t + 1, :] * k[:r])))
    return srow, es, ws


def _hgrn_state_step(st, k, v, b, bmask):
    bl = b[SUB - 1:SUB, :]
    ktil = k * jnp.exp(bl - b)
    return st * jnp.exp(bl) + _dot_tn(v.astype(BF16), ktil.astype(BF16)) * bmask


def _hgrn_tile(t):
    return min(256, t)


def _hgrn_fwd(proj, lb, ones_b, nb, name, rider=None):
    n = proj.shape[0]
    t = n // nb
    tt = _hgrn_tile(t)
    nt = t // tt
    ncs = tt // CHUNK
    w = HGRN_W

    def body(q_ref, z_ref, v_ref, lb_ref, ones_ref, o_ref, s0_ref, st_s, b_s, qs_s, k_s):
        @pl.when(pl.program_id(1) == 0)
        def _():
            st_s[...] = jnp.zeros_like(st_s)

        q = q_ref[...]
        _, _, _, g, k, sq = _hgrn_gates(q, z_ref[...], lb_ref[...])
        b_s[...] = _sel_dot(_sub_tri(tt, True), g)
        qs_s[...] = q * sq
        k_s[...] = k
        bmask = _block_ones(w, F32)
        ones_b = ones_ref[...]

        def chunk(c, carry):
            st = st_s[...]
            s0_ref[c] = st
            base = pl.multiple_of(c * CHUNK, CHUNK)
            tiles = []
            for u in range(CHUNK // SUB):
                rows = pl.ds(base + u * SUB, SUB)
                tiles.append((qs_s[rows, :], k_s[rows, :], v_ref[rows, :], b_s[rows, :]))
            aexps = []
            for qs, k, v, b in tiles:
                _, _, ws = _hgrn_decays(qs, k, b)
                aexps.append(_dot(jnp.concatenate(ws, axis=0).astype(BF16), ones_b))
            inters = []
            for qs, k, v, b in tiles:
                inters.append(_dot_nt((qs * jnp.exp(b)).astype(BF16), st.astype(BF16)))
                st = _hgrn_state_step(st, k, v, b, bmask)
            st_s[...] = st
            for u, ((qs, k, v, b), aexp, o) in enumerate(zip(tiles, aexps, inters)):
                for t in range(SUB):
                    r = _live_rows(t)
                    row = o[t:t + 1, :] + jnp.sum(aexp[t * SUB:t * SUB + r, :] * v[:r], axis=0, keepdims=True)
                    o_ref[pl.ds(base + u * SUB + t, 1), :] = row
            return carry

        lax.fori_loop(0, ncs, chunk, 0)

    def col(j):
        return pl.BlockSpec((tt, w), lambda b, i: (b * nt + i, j))

    return _call(
        body, name, (proj, proj, proj, lb, ones_b), rider, grid=(nb, nt),
        in_specs=[col(C_QA // w), col(C_FA // w), col(C_IA // w), pl.BlockSpec((1, w), lambda b, i: (0, 0)),
                  pl.BlockSpec((w, w), lambda b, i: (0, 0))],
        out_specs=[pl.BlockSpec((tt, w), lambda b, i: (b * nt + i, 0)),
                   pl.BlockSpec((ncs, w, w), lambda b, i: (b * nt + i, 0, 0))],
        out_shape=[_sds((n, w), F32), _sds((n // CHUNK, w, w), F32)],
        scratch_shapes=[pltpu.VMEM((w, w), F32)] + [pltpu.VMEM((tt, w), F32)] * 3,
        compiler_params=_params(),
    )


def _pool_lane_windows():
    lane = lax.broadcasted_iota(jnp.int32, (1, POOL_W), 1) // HEAD
    wl = jnp.zeros((1, POOL_W), F32)
    for gi, win in enumerate(POOL_WINDOWS):
        wl = jnp.where(lane == gi, float(win), wl)
    return lane, wl


def _pool_select(lane, parts):
    out = parts[-1]
    for gi in range(len(parts) - 2, -1, -1):
        out = jnp.where(lane == gi, parts[gi], out)
    return out


def _pool_mix(u, halo, t0, tt):
    lane, wl = _pool_lane_windows()
    ext = jnp.concatenate([halo, u], axis=0)
    sums, cur, shift = [], ext, 1
    for _ in POOL_WINDOWS:
        cur = cur + pltpu.roll(cur, shift, axis=0)
        sums.append(cur[POOL_HALO:, :])
        shift *= 2
    tpos = (t0 + lax.broadcasted_iota(jnp.int32, (tt, POOL_W), 0)).astype(F32)
    cnt = jnp.minimum(tpos + 1.0, wl)
    return _pool_select(lane, sums) / cnt - u, cnt


def _pool_specs(tt, nt, nhb):
    cu, cg = C_UB // POOL_W, C_GB // POOL_W
    return [pl.BlockSpec((tt, POOL_W), lambda b, i: (b * nt + i, cu)),
            pl.BlockSpec((tt, POOL_W), lambda b, i: (b * nt + i, cg)),
            pl.BlockSpec((POOL_HALO, POOL_W), lambda b, i: (jnp.maximum((b * nt + i) * nhb - 1, 0), cu))]


def _pool_fwd(proj, wbd, scale, nb, name):
    n = proj.shape[0]
    t = n // nb
    tt = min(512, t)
    nt = t // tt
    nhb = tt // POOL_HALO

    def body(u_ref, g_ref, h_ref, w_ref, s_ref, o_ref):
        i = pl.program_id(1)
        halo = jnp.where(i == 0, 0.0, h_ref[...])
        pooled, _ = _pool_mix(u_ref[...], halo, i * tt, tt)
        gv = g_ref[...]
        o_ref[...] = _dot(pooled.astype(BF16), w_ref[...]) * s_ref[...] * (gv * _sigmoid(gv))

    return _pc(
        body, name, grid=(nb, nt),
        in_specs=_pool_specs(tt, nt, nhb) + [pl.BlockSpec((POOL_W, POOL_W), lambda b, i: (0, 0)),
                                             pl.BlockSpec((1, POOL_W), lambda b, i: (0, 0))],
        out_specs=pl.BlockSpec((tt, POOL_W), lambda b, i: (b * nt + i, 0)),
        out_shape=_sds((n, POOL_W), F32),
        compiler_params=_params(),
    )(proj, proj, proj, wbd, scale)


def _rows_reduce(x, op, final):
    while x.shape[0] > 8 and x.shape[0] % 16 == 0:
        half = x.shape[0] // 2
        x = op(x[:half], x[half:])
    return final(x, axis=0, keepdims=True)


def _tri_pair(step, n, group=1):
    counts = [a // group + 1 for a in range(n)]
    firsts = [sum(counts[:a]) for a in range(1, n)]
    a = sum([(step >= f).astype(jnp.int32) for f in firsts], jnp.int32(0))
    first = sum([jnp.where(step >= f, c, 0) for f, c in zip(firsts, counts)], jnp.int32(0))
    return a, step - first


def _tri_steps(n, group=1):
    return sum(a // group + 1 for a in range(n))


def _lane_lo():
    return lax.broadcasted_iota(jnp.int32, (1, 2 * HEAD), 1) < HEAD


def _put_col(tile, hh, colv):
    lane = lax.broadcasted_iota(jnp.int32, tile.shape, 1)
    return jnp.where(lane == hh, colv, tile)


def _fox_fwd(proj, kt, c_col, c_row, nb, name, rider=None):
    n = proj.shape[0]
    t = n // nb
    tb = min(256, t)
    nq = t // tb
    pw = 2 * HEAD
    kw = 2 if nq % 2 == 0 else 1
    nk = nq // kw

    def body(q_ref, kt_ref, v_ref, cc_ref, cr_ref, o_ref, lse_ref, m_s, acc_s, cq_s):
        qi, kk = _tri_pair(pl.program_id(1), nq, kw)

        @pl.when(kk == 0)
        def _():
            m_s[...] = jnp.full_like(m_s, -jnp.inf)
            acc_s[...] = jnp.zeros_like(acc_s)
            for hh in range(FOX_HEADS):
                cq_s[hh] = jnp.broadcast_to(cc_ref[:, hh:hh + 1], (tb, pw))

        def block(masked, sub):
            lo = _lane_lo()
            keys = slice(sub * tb, (sub + 1) * tb)
            if masked:
                causal = lax.broadcasted_iota(jnp.int32, (tb, tb), 0) >= lax.broadcasted_iota(jnp.int32, (tb, tb), 1)
            def lanes(hh):
                return lo if hh % 2 == 0 else jnp.logical_not(lo)

            def scores(hh):
                sl = slice((hh // 2) * pw, (hh // 2 + 1) * pw)
                return _dot(jnp.where(lanes(hh), q_ref[:, sl] * FOX_SCALE, 0.0).astype(BF16), kt_ref[sl, keys])

            ahead = scores(0)
            for hh in range(FOX_HEADS):
                s = ahead
                if hh + 1 < FOX_HEADS:
                    ahead = scores(hh + 1)
                s = s + (jnp.tile(cq_s[hh], (1, tb // pw)) - cr_ref[hh:hh + 1, keys])
                if masked:
                    s = jnp.where(causal, s, MASK_VALUE)
                m_prev = m_s[hh]
                m_new = jnp.maximum(m_prev, jnp.max(s, axis=1, keepdims=True))
                alpha = jnp.exp(m_prev - m_new)
                pe = jnp.exp(s - jnp.tile(m_new, (1, tb // pw)))
                m_s[hh] = m_new
                vf = v_ref[keys, (hh // 2) * pw:(hh // 2 + 1) * pw]
                acc_s[hh] = alpha * acc_s[hh] + _dot(pe.astype(BF16), jnp.where(lanes(hh), vf, 1.0).astype(BF16))

        def finish():
            lo = _lane_lo()
            m_all, l_all = jnp.zeros((tb, FC_PAD), F32), jnp.ones((tb, FC_PAD), F32)
            for p in range(FOX_HEADS // 2):
                a0, a1 = acc_s[2 * p], acc_s[2 * p + 1]
                both = pltpu.roll(jnp.where(lo, a1, a0), HEAD, axis=1)
                o_ref[:, p * pw:(p + 1) * pw] = jnp.where(lo, a0, a1) / both
                m_all = _put_col(_put_col(m_all, 2 * p, m_s[2 * p]), 2 * p + 1, m_s[2 * p + 1])
                l_all = _put_col(_put_col(l_all, 2 * p, both), 2 * p + 1, a1)
            lse_ref[...] = (m_all + jnp.log(l_all)).T[:FOX_HEADS, :]

        for sub in range(kw):
            @pl.when(kk * kw + sub < qi)
            def _(sub=sub):
                block(False, sub)

            @pl.when(kk * kw + sub == qi)
            def _(sub=sub):
                block(True, sub)
                finish()

    def qspec(wd, j):
        return pl.BlockSpec((tb, wd), lambda b, st: (b * nq + _tri_pair(st, nq, kw)[0], j))

    def kspec(j):
        return pl.BlockSpec((kw * tb, FOX_W), lambda b, st: (b * nk + _tri_pair(st, nq, kw)[1], j))

    return _call(
        body, name, (proj, kt, proj, c_col, c_row), rider, grid=(nb, _tri_steps(nq, kw)),
        in_specs=[qspec(FOX_W, C_QC // FOX_W),
                  pl.BlockSpec((FOX_W, kw * tb), lambda b, st: (0, b * nk + _tri_pair(st, nq, kw)[1])),
                  kspec(C_VC // FOX_W), qspec(FC_PAD, 0),
                  pl.BlockSpec((None, FOX_HEADS, kw * tb), lambda b, st: (b, 0, _tri_pair(st, nq, kw)[1]))],
        out_specs=[qspec(FOX_W, 0),
                   pl.BlockSpec((None, FOX_HEADS, tb), lambda b, st: (b, 0, _tri_pair(st, nq, kw)[0]))],
        out_shape=[_sds((n, FOX_W), F32), _sds((nb, FOX_HEADS, t), F32)],
        scratch_shapes=[pltpu.VMEM((FOX_HEADS, tb, pw), F32), pltpu.VMEM((FOX_HEADS, tb, pw), F32),
                        pltpu.VMEM((FOX_HEADS, tb, pw), F32)],
        compiler_params=_params(),
    )


def _head_mean(x, ones_f):
    return _dot_sel(x, ones_f) * (1.0 / HEAD)


def _merge_fwd(x, proj, o_h, o_b, o_c, gh, w_out, g_post, name, target=None):
    n, d = x.shape
    tm = min(512, n)

    def body(*refs):
        x_ref, ga_ref, gc_ref, oh_ref, ob_ref, oc_ref, gh_ref, w_ref, gp_ref = refs[:9]
        if target is None:
            xo_ref, mixt_ref, y_ref = refs[9:]
        else:
            t_ref, dx_ref, sq_ref, mixt_ref, y_ref = refs[9:]
        oh = oh_ref[...]
        ones_f = _block_ones(HGRN_W, F32)
        na = oh * lax.rsqrt(_head_mean(oh * oh, ones_f) + NORM_EPS) * gh_ref[...]
        ga, gc = ga_ref[...], gc_ref[...]
        mixed = jnp.concatenate([na * (ga * _sigmoid(ga)), ob_ref[...], oc_ref[...] * (gc * _sigmoid(gc))], axis=1)
        mixt_ref[...] = mixed.T.astype(BF16)
        y = _dot(mixed.astype(BF16), w_ref[...])
        y_ref[...] = y
        xn = x_ref[...] + y * lax.rsqrt(jnp.mean(y * y, axis=-1, keepdims=True) + NORM_EPS) * gp_ref[...]
        if target is None:
            xo_ref[...] = xn
        else:
            @pl.when(pl.program_id(0) == 0)
            def _():
                sq_ref[...] = jnp.zeros_like(sq_ref)

            e = xn - t_ref[...]
            dx_ref[...] = e * (1.0 / d)
            sq_ref[...] += jnp.sum(e * e, axis=0, keepdims=True)

    def row(wd, j=0):
        return pl.BlockSpec((tm, wd), lambda i: (i, j))

    def full(a, b):
        return pl.BlockSpec((a, b), lambda i: (0, 0))

    head = [] if target is None else [target]
    res = _pc(
        body, name, grid=(n // tm,),
        in_specs=[row(d), row(HGRN_W, C_GA // HGRN_W), row(FOX_W, C_GC // FOX_W), row(HGRN_W), row(POOL_W), row(FOX_W),
                  full(1, HGRN_W), full(d, d), full(1, d)] + [row(d)] * len(head),
        out_specs=[row(d)] + [full(1, d)] * len(head) + [pl.BlockSpec((d, tm), lambda i: (0, i)), row(d)],
        out_shape=[_sds((n, d), F32)] + [_sds((1, d), F32)] * len(head) + [_sds((d, n), BF16), _sds((n, d), F32)],
        compiler_params=_params(),
    )(x, proj, proj, o_h, o_b, o_c, gh, w_out, g_post, *head)
    return (res[0], res[1], res[2]) if target is None else ((res[0], res[1]), res[2], res[3])


def _rms_bwd(dy_scaled, xhat, r):
    return r * (dy_scaled - xhat * jnp.mean(dy_scaled * xhat, axis=-1, keepdims=True))


def _merge_bwd(dxo, y, g_post, w_out_t, proj, o_c, nb, name):
    n, d = y.shape
    t = n // nb
    tm = min(512, t)
    nt = t // tm
    wab = HGRN_W + POOL_W

    def body(dx_ref, y_ref, gp_ref, wt_ref, gc_ref, oc_ref, dy_ref, dm_ref, dgp_ref, da_ref, dat_ref, dg_ref, dl_ref):
        @pl.when(pl.program_id(0) == 0)
        def _():
            dgp_ref[...] = jnp.zeros_like(dgp_ref)

        yv, dxv = y_ref[...], dx_ref[...]
        r = lax.rsqrt(jnp.mean(yv * yv, axis=-1, keepdims=True) + NORM_EPS)
        yh = yv * r
        dgp_ref[...] += jnp.sum(dxv * yh, axis=0, keepdims=True)
        dyb = _rms_bwd(dxv * gp_ref[...], yh, r).astype(BF16)
        dy_ref[...] = dyb
        dm_ref[...] = _dot(dyb, wt_ref[:, :wab])
        dmc = _dot(dyb, wt_ref[:, wab:])
        gc, oc = gc_ref[...], oc_ref[...]
        sg = _sigmoid(gc)
        da = dmc * (gc * sg)
        da_ref[...] = da.astype(BF16)
        dat_ref[...] = da.T.astype(BF16)
        dg_ref[...] = (dmc * oc * (sg * (1.0 + gc * (1.0 - sg)))).astype(BF16)
        rr = lax.broadcasted_iota(jnp.int32, (FOX_W, FC_PAD), 0) // HEAD
        cc = lax.broadcasted_iota(jnp.int32, (FOX_W, FC_PAD), 1)
        dl_ref[...] = _dot_sel(da * oc, (rr == cc).astype(F32)).T[:FOX_HEADS, :]

    def row(wd, j=0):
        return pl.BlockSpec((tm, wd), lambda i: (i, j))

    def full(a, b):
        return pl.BlockSpec((a, b), lambda i: (0, 0))

    return _pc(
        body, name, grid=(n // tm,),
        in_specs=[row(d), row(d), full(1, d), full(d, d), row(FOX_W, C_GC // FOX_W), row(FOX_W)],
        out_specs=[row(d), row(wab), full(1, d), row(FOX_W), pl.BlockSpec((FOX_W, tm), lambda i: (0, i)), row(FOX_W),
                   pl.BlockSpec((None, FOX_HEADS, tm), lambda i: (i // nt, 0, i % nt))],
        out_shape=[_sds((n, d), BF16), _sds((n, wab), F32), _sds((1, d), F32), _sds((n, FOX_W), BF16),
                   _sds((FOX_W, n), BF16), _sds((n, FOX_W), BF16), _sds((nb, FOX_HEADS, t), F32)],
        compiler_params=_params(),
    )(dxo, y, g_post, w_out_t, proj, o_c)


def _w_out_grad(mixt, dy, name):
    d, n = mixt.shape
    rows = d // N_DEV

    def body(a_ref, b_ref, o_ref):
        o_ref[...] = _dot(a_ref[...], b_ref[...]).astype(BF16)

    return _pc(
        body, name, grid=(N_DEV,),
        in_specs=[pl.BlockSpec((rows, n), lambda j: (j, 0)), pl.BlockSpec((n, d), lambda j: (0, 0))],
        out_specs=pl.BlockSpec((None, None, rows, d), lambda j: (j % 2, j // 2, 0, 0)),
        out_shape=_sds((2, N_DEV // 2, rows, d), BF16),
        compiler_params=_params(),
    )(mixt, dy)


def _w_in_grad(ht, pieces, name):
    d, n = ht.shape
    ta, tk = min(512, d), min(512, n)
    nk = n // tk
    arrays = [p for p, _ in pieces]
    widths = [p.shape[1] for p in arrays]
    offs = [sum(widths[:i]) for i in range(len(widths))]
    in_w = MAIN_W + FOX_HEADS
    shard = in_w // N_DEV

    def body(*refs):
        a_ref, p_refs = refs[0], refs[1:1 + len(arrays)]
        o_ref, acc = refs[1 + len(arrays):]
        k = pl.program_id(1)

        @pl.when(k == 0)
        def _():
            acc[...] = jnp.zeros_like(acc)

        a = a_ref[...]
        for pr, off, wd in zip(p_refs, offs, widths):
            for j in range(0, wd, 512):
                jw = min(512, wd - j)
                acc[:, off + j:off + j + jw] += _dot(a, pr[:, j:j + jw])

        @pl.when(k == nk - 1)
        def _():
            for j in range(N_DEV):
                o_ref[j % 2, j // 2] = acc[:, j * shard:(j + 1) * shard].astype(BF16)

    return _pc(
        body, name, grid=(d // ta, nk),
        in_specs=[pl.BlockSpec((ta, tk), lambda i, k: (i, k))] + [pl.BlockSpec((tk, wd), lambda i, k: (k, 0)) for wd in widths],
        out_specs=pl.BlockSpec((2, N_DEV // 2, ta, shard), lambda i, k: (0, 0, i, 0)),
        out_shape=_sds((2, N_DEV // 2, d, shard), BF16),
        scratch_shapes=[pltpu.VMEM((ta, sum(widths)), F32)],
        compiler_params=_params(),
    )(ht, *arrays)


def _hgrn_state_bwd(qs, k, v, b, do, s0, ds1, bmask):
    bl = b[SUB - 1:SUB, :]
    eb, ebl, ekt = jnp.exp(b), jnp.exp(bl), jnp.exp(bl - b)
    qe, ktil = qs * eb, k * ekt
    ds1b, dob = ds1.astype(BF16), do.astype(BF16)
    dv = _dot_nt(ktil.astype(BF16), ds1b)
    dqe = _dot(dob, s0.astype(BF16))
    dktil = _dot(v.astype(BF16), ds1b)
    dbl = jnp.sum(dktil * ktil, axis=0, keepdims=True) + ebl * jnp.sum(s0 * ds1, axis=0, keepdims=True)
    ds0 = ds1 * ebl + _dot_tn(dob, qe.astype(BF16)) * bmask
    return dqe * eb, dktil * ekt, dv, dbl, ds0


def _hgrn_intra_bwd(qs, k, v, do, es, aexp, gexp, dq, dk, dv, put_dq_row):
    dks = [dk[j:j + 8] for j in range(0, SUB, 8)]
    dvs = [dv[j:j + 8] for j in range(0, SUB, 8)]
    for t in range(SUB):
        r = _live_rows(t)
        ge = gexp[t * SUB:t * SUB + r, :] * es[t]
        put_dq_row(t, dq[t:t + 1, :] + jnp.sum(ge * k[:r], axis=0, keepdims=True))
        for j in range(r // 8):
            dks[j] = dks[j] + ge[8 * j:8 * j + 8] * qs[t:t + 1, :]
            dvs[j] = dvs[j] + aexp[t * SUB + 8 * j:t * SUB + 8 * j + 8, :] * do[t:t + 1, :]
    return jnp.concatenate(dks, axis=0), jnp.concatenate(dvs, axis=0)


def _hgrn_bwd(dmix, proj, o_h, s0, gh, lb, ones_b, nb, name, rider=None):
    n = proj.shape[0]
    t = n // nb
    tt = _hgrn_tile(t)
    nt = t // tt
    ncs = tt // CHUNK
    nsub = CHUNK // SUB
    w = HGRN_W

    def body(dm_ref, q_ref, z_ref, v_ref, ga_ref, oh_ref, s0_ref, gh_ref, lb_ref, ones_ref,
             dp_ref, dgh_ref, dlb_ref, ds_s, ss_s, b_s, qs_s, k_s, do_s, dq_s, dk_s, dv_s, dbl_s):
        first = jnp.logical_and(pl.program_id(0) == 0, pl.program_id(1) == 0)

        @pl.when(first)
        def _():
            dgh_ref[...] = jnp.zeros_like(dgh_ref)
            dlb_ref[...] = jnp.zeros_like(dlb_ref)

        @pl.when(pl.program_id(1) == 0)
        def _():
            ds_s[...] = jnp.zeros_like(ds_s)

        ones_b = ones_ref[...]
        ones_f = ones_b.astype(F32)
        bmask = _block_ones(w, F32)
        lbv, ghv = lb_ref[...], gh_ref[...]
        oh, ga, dm = oh_ref[...], ga_ref[...], dm_ref[...]
        rn = lax.rsqrt(_head_mean(oh * oh, ones_f) + NORM_EPS)
        nh = oh * rn
        sga = _sigmoid(ga)
        dp_ref[:, 3 * w:4 * w] = (dm * nh * ghv * (sga * (1.0 + ga * (1.0 - sga)))).astype(BF16)
        dn = dm * (ga * sga)
        dgh_ref[...] += jnp.sum(dn * nh, axis=0, keepdims=True)
        dn = dn * ghv
        do_s[...] = rn * (dn - nh * _head_mean(dn * nh, ones_f))
        q = q_ref[...]
        sig, sn, f, g, k, sq = _hgrn_gates(q, z_ref[...], lbv)
        qs = q * sq
        b_s[...] = _sel_dot(_sub_tri(tt, True), g)
        qs_s[...] = qs
        k_s[...] = k

        def chunk(cc, carry):
            c = ncs - 1 - cc
            base = pl.multiple_of(c * CHUNK, CHUNK)
            tiles = []
            for u in range(nsub):
                rows = pl.ds(base + u * SUB, SUB)
                tiles.append((qs_s[rows, :], k_s[rows, :], v_ref[rows, :], b_s[rows, :], do_s[rows, :]))
            st = s0_ref[c]
            for u, (qs, k, v, b, do) in enumerate(tiles):
                ss_s[u] = st
                if u < nsub - 1:
                    st = _hgrn_state_step(st, k, v, b, bmask)
            ds = ds_s[...]
            for u in reversed(range(nsub)):
                qs, k, v, b, do = tiles[u]
                _, es, ws = _hgrn_decays(qs, k, b)
                gs = [_pad_rows(do[t:t + 1, :] * v[:_live_rows(t)]) for t in range(SUB)]
                aexp = _dot(jnp.concatenate(ws, axis=0).astype(BF16), ones_b)
                gexp = _dot(jnp.concatenate(gs, axis=0).astype(BF16), ones_b)
                dq, dk, dv, dbl, ds = _hgrn_state_bwd(qs, k, v, b, do, ss_s[u], ds, bmask)

                def put_dq_row(i, row, r0=base + u * SUB):
                    dq_s[pl.ds(r0 + i, 1), :] = row

                dk, dv = _hgrn_intra_bwd(qs, k, v, do, es, aexp, gexp, dq, dk, dv, put_dq_row)
                dk_s[pl.ds(base + u * SUB, SUB), :] = dk
                dv_s[pl.ds(base + u * SUB, SUB), :] = dv
                dbl_s[pl.ds(base + u * SUB, SUB), :] = jnp.broadcast_to(dbl, (SUB, w))
            ds_s[...] = ds
            return carry

        lax.fori_loop(0, ncs, chunk, 0)
        dqs, dk = dq_s[...], dk_s[...]
        dg = _sel_dot(_sub_tri(tt, False), qs * dqs - k * dk) + dbl_s[...]
        dfz = jnp.where(f > TINY, dg / jnp.maximum(f, TINY), 0.0)
        dlb_ref[...] += jnp.sum(dfz * (1.0 - sig) - dk * sn, axis=0, keepdims=True)
        dp_ref[:, 0:w] = (dqs * (sq * (1.0 + q * (1.0 - sq)))).astype(BF16)
        dp_ref[:, w:2 * w] = ((dfz - dk) * (1.0 - lbv) * sig * sn).astype(BF16)
        dp_ref[:, 2 * w:3 * w] = dv_s[...].astype(BF16)

    def rv(b, i):
        return b * nt + (nt - 1 - i)

    def col(j):
        return pl.BlockSpec((tt, w), lambda b, i: (rv(b, i), j))

    def full(a, bb):
        return pl.BlockSpec((a, bb), lambda b, i: (0, 0))

    return _call(
        body, name, (dmix, proj, proj, proj, proj, o_h, s0, gh, lb, ones_b), rider, grid=(nb, nt),
        in_specs=[col(0), col(C_QA // w), col(C_FA // w), col(C_IA // w), col(C_GA // w), col(0),
                  pl.BlockSpec((ncs, w, w), lambda b, i: (rv(b, i), 0, 0)), full(1, w), full(1, w), full(w, w)],
        out_specs=[pl.BlockSpec((tt, 4 * w), lambda b, i: (rv(b, i), 0)), full(1, w), full(1, w)],
        out_shape=[_sds((n, 4 * w), BF16), _sds((1, w), F32), _sds((1, w), F32)],
        scratch_shapes=[pltpu.VMEM((w, w), F32), pltpu.VMEM((nsub, w, w), F32)] + [pltpu.VMEM((tt, w), F32)] * 8,
        compiler_params=_params(),
    )


def _pool_bwd(dmix, proj, wbd, wbd_t, scale, nb, name):
    n = proj.shape[0]
    t = n // nb
    tt = min(512, t)
    nt = t // tt
    nhb = tt // POOL_HALO
    cu, cg, cm = C_UB // POOL_W, C_GB // POOL_W, HGRN_W // POOL_W

    def body(u_ref, g_ref, h_ref, dm_ref, gn_ref, dmn_ref, w_ref, wt_ref, s_ref, dp_ref, dw_ref, ds_ref):
        i = pl.program_id(1)
        first = jnp.logical_and(pl.program_id(0) == 0, i == 0)

        @pl.when(first)
        def _():
            dw_ref[...] = jnp.zeros_like(dw_ref)
            ds_ref[...] = jnp.zeros_like(ds_ref)

        sc = s_ref[...]
        halo = jnp.where(i == 0, 0.0, h_ref[...])
        pooled, cnt = _pool_mix(u_ref[...], halo, i * tt, tt)
        pb = pooled.astype(BF16)
        pre = _dot(pb, w_ref[...])
        gv, dm = g_ref[...], dm_ref[...]
        sg = _sigmoid(gv)
        silu = gv * sg
        dgb = dm * pre * sc * (sg * (1.0 + gv * (1.0 - sg)))
        ds_ref[...] += jnp.sum(dm * pre * silu, axis=0, keepdims=True)
        dpre = (dm * sc * silu).astype(BF16)
        dw_ref[...] += _dot_tn(pb, dpre)
        dpool = _dot(dpre, wt_ref[...])
        gn = gn_ref[...]
        dpre_n = (dmn_ref[...] * sc * (gn * _sigmoid(gn))).astype(BF16)
        dpool_n = jnp.where(i == nt - 1, 0.0, _dot(dpre_n, wt_ref[...]))
        lane, wl = _pool_lane_windows()
        tpos_n = ((i + 1) * tt + lax.broadcasted_iota(jnp.int32, (POOL_HALO, POOL_W), 0)).astype(F32)
        ext = jnp.concatenate([dpool / cnt, dpool_n / jnp.minimum(tpos_n + 1.0, wl)], axis=0)
        rows = tt + POOL_HALO
        sums, cur, shift = [], ext, 1
        for _ in POOL_WINDOWS:
            cur = cur + pltpu.roll(cur, rows - shift, axis=0)
            sums.append(cur[:tt, :])
            shift *= 2
        du = _pool_select(lane, sums) - dpool
        dp_ref[...] = jnp.concatenate([du, dgb], axis=1).astype(BF16)

    def nxt(b, i):
        return jnp.minimum((b * nt + i + 1) * nhb, n // POOL_HALO - 1)

    return _pc(
        body, name, grid=(nb, nt),
        in_specs=_pool_specs(tt, nt, nhb) + [
            pl.BlockSpec((tt, POOL_W), lambda b, i: (b * nt + i, cm)),
            pl.BlockSpec((POOL_HALO, POOL_W), lambda b, i: (nxt(b, i), cg)),
            pl.BlockSpec((POOL_HALO, POOL_W), lambda b, i: (nxt(b, i), cm)),
            pl.BlockSpec((POOL_W, POOL_W), lambda b, i: (0, 0)), pl.BlockSpec((POOL_W, POOL_W), lambda b, i: (0, 0)),
            pl.BlockSpec((1, POOL_W), lambda b, i: (0, 0))],
        out_specs=[pl.BlockSpec((tt, 2 * POOL_W), lambda b, i: (b * nt + i, 0)),
                   pl.BlockSpec((POOL_W, POOL_W), lambda b, i: (0, 0)), pl.BlockSpec((1, POOL_W), lambda b, i: (0, 0))],
        out_shape=[_sds((n, 2 * POOL_W), BF16), _sds((POOL_W, POOL_W), F32), _sds((1, POOL_W), F32)],
        compiler_params=_params(),
    )(proj, proj, proj, dmix, proj, dmix, wbd, wbd_t, scale)


def _fox_bwd(proj, qt, kt, da, dat, c_col, c_row, lse_row, delta_row, nb, name, rider=None):
    n = proj.shape[0]
    t = n // nb
    tb = min(256, t)
    nq = t // tb
    pw = 2 * HEAD
    qw = 2 if nq % 2 == 0 else 1
    nqs = nq // qw

    def body(q_ref, k_ref, v_ref, da_ref, qt_ref, kt_ref, dat_ref, cc_ref, cr_ref, lse_ref, dl_ref,
             dq_ref, dk_ref, dv_ref, dck_ref, dcq_ref, dq_s, dk_s, dv_s, dck_s, dcq_s):
        step = pl.program_id(1)
        kj, qq = pairs(step)

        @pl.when(step == 0)
        def _():
            dq_s[...] = jnp.zeros_like(dq_s)
            dcq_s[...] = jnp.zeros_like(dcq_s)

        @pl.when(qq == nqs - 1)
        def _():
            dk_s[...] = jnp.zeros_like(dk_s)
            dv_s[...] = jnp.zeros_like(dv_s)
            dck_s[...] = jnp.zeros_like(dck_s)

        def block(masked, sub):
            lo = _lane_lo()
            qi = qq * qw + sub
            qs = slice(sub * tb, (sub + 1) * tb)
            if masked:
                causal = lax.broadcasted_iota(jnp.int32, (tb, tb), 1) >= lax.broadcasted_iota(jnp.int32, (tb, tb), 0)
            dck = dck_s[...]
            for p in range(FOX_HEADS // 2):
                sl = slice(p * pw, (p + 1) * pw)
                qp = q_ref[qs, sl] * FOX_SCALE
                kp = k_ref[:, sl].astype(BF16)
                vp = v_ref[:, sl].astype(BF16)
                dap = da_ref[qs, sl]
                dk, dv = dk_s[:, sl], dv_s[:, sl]
                for h in range(2):
                    hh = 2 * p + h
                    lm = lo if h == 0 else jnp.logical_not(lo)
                    rows = slice(hh * HEAD, (hh + 1) * HEAD)
                    none = jnp.zeros((HEAD, tb), BF16)
                    qm = jnp.where(lm, qp, 0.0).astype(BF16)
                    dam = jnp.where(lm, dap, jnp.zeros_like(dap))
                    qtm = jnp.concatenate([qt_ref[rows, qs], none] if h == 0 else [none, qt_ref[rows, qs]], axis=0)
                    datm = jnp.concatenate([dat_ref[rows, qs], none] if h == 0 else [none, dat_ref[rows, qs]], axis=0)
                    s = _dot(kp, qtm) + (cr_ref[hh:hh + 1, qs] - cc_ref[:, hh:hh + 1])
                    pe = jnp.exp(s - lse_ref[hh:hh + 1, qs])
                    if masked:
                        pe = jnp.where(causal, pe, 0.0)
                    dp = _dot(vp, datm)
                    ds = pe * (dp - dl_ref[hh:hh + 1, qs])
                    dsb = ds.astype(BF16)
                    dv = dv + _dot(pe.astype(BF16), dam)
                    dk = dk + _dot(dsb, qm)
                    dq_s[qi, rows, :] += _dot(kt_ref[rows, :], dsb)
                    dck = dck - _put_col(jnp.zeros_like(dck), hh, jnp.sum(ds, axis=1, keepdims=True))
                    dcq_s[qi, hh:hh + 1, :] += _rows_reduce(ds, jnp.add, jnp.sum)
                dk_s[:, sl] = dk
                dv_s[:, sl] = dv
            dck_s[...] = dck

        for sub in reversed(range(qw)):
            @pl.when(qq * qw + sub > kj)
            def _(sub=sub):
                block(False, sub)

            @pl.when(qq * qw + sub == kj)
            def _(sub=sub):
                block(True, sub)

        @pl.when(qq == kj // qw)
        def _():
            dk_ref[...] = dk_s[...].astype(BF16)
            dv_ref[...] = dv_s[...].astype(BF16)
            dck_ref[...] = dck_s[...]

        @pl.when(step == _tri_steps(nq, qw) - 1)
        def _():
            for j in range(nq):
                dq_ref[j * tb:(j + 1) * tb, :] = (dq_s[j].T * FOX_SCALE).astype(BF16)
                dcq_ref[:, j * tb:(j + 1) * tb] = dcq_s[j]

    def pairs(step):
        a, b = _tri_pair(step, nq, qw)
        return nq - 1 - a, nqs - 1 - b

    def kspec(wd, j=0):
        return pl.BlockSpec((tb, wd), lambda b, st: (b * nq + pairs(st)[0], j))

    def qspec(wd, j=0):
        return pl.BlockSpec((qw * tb, wd), lambda b, st: (b * nqs + pairs(st)[1], j))

    def qrow():
        return pl.BlockSpec((None, FOX_HEADS, qw * tb), lambda b, st: (b, 0, pairs(st)[1]))

    def tspec(which):
        if which == 0:
            return pl.BlockSpec((FOX_W, tb), lambda b, st: (0, b * nq + pairs(st)[0]))
        return pl.BlockSpec((FOX_W, qw * tb), lambda b, st: (0, b * nqs + pairs(st)[1]))

    return _call(
        body, name, (proj, proj, proj, da, qt, kt, dat, c_col, c_row, lse_row, delta_row), rider,
        grid=(nb, _tri_steps(nq, qw)),
        in_specs=[qspec(FOX_W, C_QC // FOX_W), kspec(FOX_W, C_KC // FOX_W), kspec(FOX_W, C_VC // FOX_W), qspec(FOX_W),
                  tspec(1), tspec(0), tspec(1), kspec(FC_PAD), qrow(), qrow(), qrow()],
        out_specs=[pl.BlockSpec((t, FOX_W), lambda b, st: (b, 0)), kspec(FOX_W), kspec(FOX_W), kspec(FC_PAD),
                   pl.BlockSpec((None, FOX_HEADS, t), lambda b, st: (b, 0, 0))],
        out_shape=[_sds((n, FOX_W), BF16), _sds((n, FOX_W), BF16), _sds((n, FOX_W), BF16), _sds((n, FC_PAD), F32),
                   _sds((nb, FOX_HEADS, t), F32)],
        scratch_shapes=[pltpu.VMEM((nq, FOX_W, tb), F32), pltpu.VMEM((tb, FOX_W), F32), pltpu.VMEM((tb, FOX_W), F32),
                        pltpu.VMEM((tb, FC_PAD), F32), pltpu.VMEM((nq, FOX_HEADS, tb), F32)],
        compiler_params=_params(),
    )


def _fox_decay_bwd(dc_q, dc_k, fc, bias, nb, name):
    n = fc.shape[0]
    t = n // nb
    tt = min(256, t)
    nt = t // tt

    def body(dcq_ref, dck_ref, fc_ref, b_ref, dfc_ref, db_ref, carry):
        i = pl.program_id(1)
        first = jnp.logical_and(pl.program_id(0) == 0, i == 0)

        @pl.when(first)
        def _():
            db_ref[...] = jnp.zeros_like(db_ref)

        @pl.when(i == 0)
        def _():
            carry[...] = jnp.zeros_like(carry)

        r = lax.broadcasted_iota(jnp.int32, (tt, tt), 0)
        cc = lax.broadcasted_iota(jnp.int32, (tt, tt), 1)
        dcq = jnp.concatenate([dcq_ref[...], jnp.zeros((FC_PAD - FOX_HEADS, tt), F32)], axis=0).T
        dlf = _sel_dot_exact(r <= cc, dcq + dck_ref[...]) + carry[...]
        carry[...] = dlf[0:1, :]
        dfc = dlf * _sigmoid(-(fc_ref[...] + b_ref[...]))
        dfc_ref[...] = dfc.astype(BF16)
        db_ref[...] += jnp.sum(dfc, axis=0, keepdims=True)

    def row():
        return pl.BlockSpec((tt, FC_PAD), lambda b, i: (b * nt + (nt - 1 - i), 0))

    return _pc(
        body, name, grid=(nb, nt),
        in_specs=[pl.BlockSpec((None, FOX_HEADS, tt), lambda b, i: (b, 0, nt - 1 - i)), row(), row(),
                  pl.BlockSpec((1, FC_PAD), lambda b, i: (0, 0))],
        out_specs=[row(), pl.BlockSpec((1, FC_PAD), lambda b, i: (0, 0))],
        out_shape=[_sds((n, FC_PAD), BF16), _sds((1, FC_PAD), F32)],
        scratch_shapes=[pltpu.VMEM((1, FC_PAD), F32)],
        compiler_params=_params(),
    )(dc_q, dc_k, fc, bias)


def _in_proj_bwd(pieces, w_main_t, w_fc_t, x, g_pre, dxo, name, rider=None):
    n, d = x.shape
    tm = min(512, n)
    widths = [p.shape[1] for p, _ in pieces]
    offs = [o for _, o in pieces]
    np_ = len(pieces)

    def body(*refs):
        p_refs = refs[:np_]
        wt_ref, wf_ref, x_ref, g_ref, dxo_ref, dx_ref, dg_ref = refs[np_:]

        @pl.when(pl.program_id(0) == 0)
        def _():
            dg_ref[...] = jnp.zeros_like(dg_ref)

        dh = _dot(p_refs[-1][...], wf_ref[...])
        for pr, wd, off in zip(p_refs[:-1], widths[:-1], offs[:-1]):
            for j in range(0, wd, 512):
                jw = min(512, wd - j)
                dh = dh + _dot(pr[:, j:j + jw], wt_ref[off + j:off + j + jw, :])
        xv = x_ref[...]
        r = lax.rsqrt(jnp.mean(xv * xv, axis=-1, keepdims=True) + NORM_EPS)
        xh = xv * r
        dg_ref[...] += jnp.sum(dh * xh, axis=0, keepdims=True)
        dx_ref[...] = dxo_ref[...] + _rms_bwd(dh * g_ref[...], xh, r)

    row = pl.BlockSpec((tm, d), lambda i: (i, 0))
    return _call(
        body, name, (*[p for p, _ in pieces], w_main_t, w_fc_t, x, g_pre, dxo), rider, grid=(n // tm,),
        in_specs=[pl.BlockSpec((tm, wd), lambda i: (i, 0)) for wd in widths] + [
            pl.BlockSpec((MAIN_W, d), lambda i: (0, 0)), pl.BlockSpec((FC_PAD, d), lambda i: (0, 0)),
            row, pl.BlockSpec((1, d), lambda i: (0, 0)), row],
        out_specs=[row, pl.BlockSpec((1, d), lambda i: (0, 0))],
        out_shape=[_sds((n, d), F32), _sds((1, d), F32)],
        compiler_params=_params(),
    )


def _lower_bound_table(lower_bounds, name):
    depth, w = lower_bounds.shape

    def body(lb_ref, o_ref):
        v = lb_ref[...]
        e = jnp.exp(v - jnp.max(v, axis=0, keepdims=True))
        p = e / jnp.sum(e, axis=0, keepdims=True)
        acc = jnp.zeros((1, w), F32)
        for l in range(depth):
            acc = acc + p[l:l + 1, :]
            o_ref[l:l + 1, :] = acc - p[0:1, :]

    return _pc(body, name, out_shape=_sds((depth, w), F32))(lower_bounds)


def _lower_bound_bwd(lower_bounds, dlbs, name):
    depth, w = lower_bounds.shape

    def body(lb_ref, d_ref, o_ref):
        v, dl = lb_ref[...], d_ref[...]
        e = jnp.exp(v - jnp.max(v, axis=0, keepdims=True))
        p = e / jnp.sum(e, axis=0, keepdims=True)
        tot = jnp.sum(dl, axis=0, keepdims=True)
        rows, tail = [], tot
        for l in range(depth):
            rows.append(tail - tot if l == 0 else tail)
            tail = tail - dl[l:l + 1, :]
        dp = jnp.concatenate(rows, axis=0)
        o_ref[...] = p * (dp - jnp.sum(p * dp, axis=0, keepdims=True))

    return _pc(body, name, out_shape=_sds((depth, w), F32))(lower_bounds, dlbs)


def _place():
    x, y, c = lax.axis_index("x"), lax.axis_index("y"), lax.axis_index("c")
    return x, y, c


def _gather_weights(*arrays):
    na = len(arrays)

    def body(*refs):
        ins, outs = refs[:na], refs[na:2 * na]
        send_sems, recv_sems, local_sems = refs[2 * na:]
        x, y, c = _place()
        me, sibling = (x, y, c), (x, y, 1 - c)
        chips = [(1 - x, y), (x, 1 - y), (1 - x, 1 - y)]

        def slot(a, px, py, pc):
            return outs[a].at[4 * px + 2 * py + pc]

        def copy(a, k, block, to, own=False):
            return pltpu.make_async_remote_copy(
                src_ref=ins[a] if own else slot(a, *block), dst_ref=slot(a, *block),
                send_sem=send_sems.at[a * 7 + k], recv_sem=recv_sems.at[a * 7 + k],
                device_id=to, device_id_type=MESH)

        mine = [pltpu.make_async_copy(ins[a], slot(a, *me), local_sems.at[a]) for a in range(na)]
        for cp in mine:
            cp.start()
        first = []
        for a in range(na):
            first.append(copy(a, 0, me, sibling, own=True))
            first += [copy(a, 1 + j, me, (*chip, c), own=True) for j, chip in enumerate(chips)]
        for cp in first:
            cp.start()
        passed = []
        for j, chip in enumerate(chips):
            for a in range(na):
                copy(a, 1 + j, (*chip, c), me).wait_recv()
                fw = copy(a, 4 + j, (*chip, c), sibling)
                fw.start()
                passed.append(fw)
        for a in range(na):
            copy(a, 0, sibling, me).wait_recv()
            for j, chip in enumerate(chips):
                copy(a, 4 + j, (*chip, 1 - c), me).wait_recv()
        for cp in first + passed:
            cp.wait_send()
        for cp in mine:
            cp.wait()

    any_spec = pl.BlockSpec(memory_space=pl.ANY)
    return _pc(
        body, "gather_weights",
        in_specs=[any_spec] * na, out_specs=[any_spec] * na,
        out_shape=[_sds((N_DEV,) + a.shape, a.dtype) for a in arrays],
        scratch_shapes=[pltpu.SemaphoreType.DMA((7 * na,)), pltpu.SemaphoreType.DMA((7 * na,)),
                        pltpu.SemaphoreType.DMA((na,))],
    )(*arrays)


def _peer(k):
    x, y, c = _place()
    return (1 - x if k & 4 else x, 1 - y if k & 2 else y, 1 - c if k & 1 else c)


def _remote(src, dst, sems, s, to):
    return pltpu.make_async_remote_copy(src_ref=src, dst_ref=dst, send_sem=sems[0].at[s], recv_sem=sems[1].at[s],
                                        device_id=to, device_id_type=MESH)


def _gather_rider(shards):
    na = len(shards)

    def plan(ins, outs, *sems):
        x, y, c = _place()
        me = 4 * x + 2 * y + c
        locs = [pltpu.make_async_copy(ins[a], outs[a].at[me], sems[2].at[a]) for a in range(na)]
        sends, recvs = [], []
        for k in range(1, N_DEV):
            px, py, pc = _peer(k)
            for a in range(na):
                s = (k - 1) * na + a
                sends.append(_remote(ins[a], outs[a].at[me], sems, s, (px, py, pc)))
                recvs.append(_remote(ins[a], outs[a].at[4 * px + 2 * py + pc], sems, s, (px, py, pc)))
        return sends, recvs, locs

    return _Rider(shards, [_sds((N_DEV,) + a.shape, a.dtype) for a in shards], (N_DEV - 1) * na, na, plan)


def _direct_exchange_rider(blocks):
    na = len(blocks)

    def plan(ins, outs, *sems):
        x, y, c = _place()
        me = 4 * x + 2 * y + c
        locs = [pltpu.make_async_copy(ins[a].at[c, 2 * x + y], outs[a].at[me], sems[2].at[a]) for a in range(na)]
        sends, recvs = [], []
        for k in range(1, N_DEV):
            px, py, pc = _peer(k)
            for a in range(na):
                s = (k - 1) * na + a
                sends.append(_remote(ins[a].at[pc, 2 * px + py], outs[a].at[me], sems, s, (px, py, pc)))
                recvs.append(_remote(ins[a].at[pc, 2 * px + py], outs[a].at[4 * px + 2 * py + pc], sems, s, (px, py, pc)))
        return sends, recvs, locs

    return _Rider(blocks, [_sds((N_DEV,) + a.shape[2:], a.dtype) for a in blocks], (N_DEV - 1) * na, na, plan)


def _swap_rider(halves):
    na = len(halves)

    def plan(ins, outs, *sems):
        x, y, c = _place()
        cps = [_remote(ins[a].at[1 - c], outs[a], sems, a, (x, y, 1 - c)) for a in range(na)]
        return cps, cps, []

    return _Rider(halves, [_sds(a.shape[1:], a.dtype) for a in halves], na, 1, plan)


def _chip_exchange_rider(parts, small=None):
    na = len(parts)
    n_chip = N_DEV // 2

    def plan(ins, outs, *sems):
        x, y, c = _place()
        chip = 2 * x + y
        locs = [pltpu.make_async_copy(ins[a].at[chip], outs[a].at[chip], sems[2].at[a]) for a in range(na)]
        sends, recvs = [], []
        for k in range(1, n_chip):
            px, py, _ = _peer(2 * k)
            for a in range(na):
                s = (k - 1) * na + a
                sends.append(_remote(ins[a].at[2 * px + py], outs[a].at[chip], sems, s, (px, py, c)))
                recvs.append(_remote(ins[a].at[2 * px + py], outs[a].at[2 * px + py], sems, s, (px, py, c)))
        if small is not None:
            me = 2 * chip + c
            locs.append(pltpu.make_async_copy(ins[na], outs[na].at[me], sems[2].at[na]))
            for k in range(1, N_DEV):
                px, py, pc = _peer(k)
                s = (n_chip - 1) * na + k - 1
                sends.append(_remote(ins[na], outs[na].at[me], sems, s, (px, py, pc)))
                recvs.append(_remote(ins[na], outs[na].at[4 * px + 2 * py + pc], sems, s, (px, py, pc)))
        return sends, recvs, locs

    extra = [] if small is None else [small]
    shapes = [_sds(a.shape, a.dtype) for a in parts] + [_sds((N_DEV,) + s.shape, s.dtype) for s in extra]
    n_sems = (n_chip - 1) * na + (N_DEV - 1) * len(extra)
    return _Rider(list(parts) + extra, shapes, n_sems, na + len(extra), plan)


def _pair_add(halves, other, core, name):
    _, nch, r, c = halves.shape

    def body(c_ref, h_ref, o_ref, p_ref):
        p_ref[...] = (h_ref[...].astype(F32) + o_ref[...].astype(F32)).astype(BF16)

    blk = pl.BlockSpec((None, r, c), lambda j, c_ref: (j, 0, 0))
    return _pc(
        body, name,
        grid_spec=pltpu.PrefetchScalarGridSpec(
            num_scalar_prefetch=1, grid=(nch,),
            in_specs=[pl.BlockSpec((None, None, r, c), lambda j, c_ref: (c_ref[0], j, 0, 0)), blk], out_specs=blk),
        out_shape=_sds((nch, r, c), BF16),
        compiler_params=_params(),
    )(core, halves, other)


def _sum_adamw(parts, w, m, v, name, rider=None):
    nl, r, c = w.shape
    tr = 256 if r % 256 == 0 else r

    def body(*refs):
        p_refs = refs[:nl]
        w_ref, m_ref, v_ref, g_ref, d_ref, mo_ref, vo_ref = refs[nl:]
        for l in range(nl):
            @pl.when(pl.program_id(0) == l)
            def _(p_ref=p_refs[l]):
                g = p_ref[0].astype(F32)
                for j in range(1, p_ref.shape[0]):
                    g = g + p_ref[j].astype(F32)
                mn = ADAM_B1 * m_ref[...] + (1.0 - ADAM_B1) * g
                vn = ADAM_B2 * v_ref[...] + (1.0 - ADAM_B2) * (g * g)
                m_hat = mn / (1.0 - ADAM_B1 ** ADAM_STEP)
                v_hat = vn / (1.0 - ADAM_B2 ** ADAM_STEP)
                g_ref[...] = g
                d_ref[...] = -ADAM_LR * (m_hat / (jnp.sqrt(v_hat) + ADAM_EPS) + ADAM_WD * w_ref[...])
                mo_ref[...] = mn
                vo_ref[...] = vn

    def part_spec(l, k):
        return pl.BlockSpec((k, tr, c), lambda li, i: (0, jnp.where(li == l, i, 0), 0))

    row = pl.BlockSpec((None, tr, c), lambda li, i: (li, i, 0))
    return _call(
        body, name, (*parts, w, m, v), rider, grid=(nl, r // tr),
        in_specs=[part_spec(l, p.shape[0]) for l, p in enumerate(parts)] + [row, row, row],
        out_specs=[row] * 4,
        out_shape=[_sds((nl, r, c), F32)] * 4,
        compiler_params=_params(),
    )


SMALL = ("lower_bounds", "pre_norm_g", "hgrn_norm_g", "pool_w", "pool_scale", "post_norm_g", "fox_f_bias")
SMALL_LANES = 128


def _small_size(tree):
    return sum(tree[k].size for k in SMALL)


def _pack_small(tree, extra=None):
    flat = jnp.concatenate([tree[k].reshape(-1) for k in SMALL] + ([] if extra is None else [extra.reshape(1)]))
    rows = -(-(_small_size(tree) + 1) // (8 * SMALL_LANES)) * 8
    return jnp.pad(flat, (0, rows * SMALL_LANES - flat.shape[0])).reshape(rows, SMALL_LANES)


def _unpack_small(packed, like):
    out, off = {}, 0
    for k in SMALL:
        size = like[k].size
        assert off % SMALL_LANES == 0
        rows = packed[off // SMALL_LANES:-(-(off + size) // SMALL_LANES)]
        out[k] = rows.reshape(-1)[:size].reshape(like[k].shape)
        off += size
    return out


def _block_diag(pw):
    g = pw.shape[0]
    eye = jnp.eye(g, dtype=pw.dtype)
    return (eye[:, None, :, None] * pw[:, :, None, :]).reshape(g * HEAD, g * HEAD)


def _assemble_w_in(g_in, name):
    _, d, shard = g_in.shape
    tr = min(256, d)
    wide = MAIN_W + FC_PAD

    def body(g_ref, wm_ref, wf_ref, wmt_ref, wft_ref, row_s):
        row_s[:, MAIN_W:] = jnp.zeros((tr, FC_PAD), F32)
        for j in range(N_DEV):
            row_s[:, j * shard:(j + 1) * shard] = g_ref[j].astype(F32)
        wm_ref[...] = row_s[:, :MAIN_W].astype(BF16)
        wf_ref[...] = row_s[:, MAIN_W:].astype(BF16)
        for j in range(0, MAIN_W, 512):
            wmt_ref[j:j + 512, :] = row_s[:, j:j + 512].T.astype(BF16)
        wft_ref[...] = row_s[:, MAIN_W:].T.astype(BF16)

    return _pc(
        body, name, grid=(d // tr,),
        in_specs=[pl.BlockSpec((N_DEV, tr, shard), lambda i: (0, i, 0))],
        out_specs=[pl.BlockSpec((tr, MAIN_W), lambda i: (i, 0)), pl.BlockSpec((tr, FC_PAD), lambda i: (i, 0)),
                   pl.BlockSpec((MAIN_W, tr), lambda i: (0, i)), pl.BlockSpec((FC_PAD, tr), lambda i: (0, i))],
        out_shape=[_sds((d, MAIN_W), BF16), _sds((d, FC_PAD), BF16), _sds((MAIN_W, d), BF16), _sds((FC_PAD, d), BF16)],
        scratch_shapes=[pltpu.VMEM((tr, wide), F32)],
        compiler_params=_params(),
    )(g_in)


def _w_out_parts(g_out):
    full_out = g_out.reshape(N_DEV * g_out.shape[1], g_out.shape[2])
    return full_out, full_out.T


def _layer_fwd(l, x, lbs, weights, lw, nb, rider_h=None, rider_c=None, target=None):
    n = x.shape[0]
    t = n // nb
    w_main, w_fc, _, _, w_out, _ = lw
    bias = jnp.pad(weights["fox_f_bias"][l:l + 1], ((0, 0), (0, FC_PAD - FOX_HEADS)))
    wbd = _block_diag(weights["pool_w"][l]).astype(BF16)
    proj, fc, ht, qt, kt = _in_proj_fwd(x, weights["pre_norm_g"][l:l + 1], w_main, w_fc, f"in_proj_fwd_{l}")
    c_col, c_row = _fox_decay_fwd(fc, bias, nb, f"fox_decay_fwd_{l}")
    (o_h, s0), rode_h = _hgrn_fwd(proj, lbs[l:l + 1], _block_ones(HGRN_W, BF16), nb, f"hgrn_fwd_{l}", rider_h)
    o_b = _pool_fwd(proj, wbd, weights["pool_scale"][l:l + 1], nb, f"pool_fwd_{l}")
    (o_c, lse), rode_c = _fox_fwd(proj, kt, c_col, c_row, nb, f"fox_fwd_{l}", rider_c)
    if w_out is None:
        lw = tuple(lw[:4]) + _w_out_parts(rode_h[0])
        w_out, rode_h = lw[4], rode_h[1:]
    x_next, mixt, y = _merge_fwd(x, proj, o_h, o_b, o_c, weights["hgrn_norm_g"][l:l + 1], w_out,
                                 weights["post_norm_g"][l:l + 1], f"merge_fwd_{l}", target)
    return x_next, (x, proj, fc, ht, qt, kt, c_col, c_row, o_h, s0, o_c, lse, mixt, y, bias, wbd), lw, (rode_h, rode_c)


def _layer_bwd(l, dx, saved, lbs, weights, lw, nb, rider=None, send_w_out=None):
    _, proj, fc, ht, qt, kt, c_col, c_row, o_h, s0, o_c, lse, mixt, y, bias, wbd = saved
    w_out_t = lw[5]
    g = {}
    dy, dmix, dgp, da, dat, d_gc, delta = _merge_bwd(dx, y, weights["post_norm_g"][l:l + 1], w_out_t, proj, o_c,
                                                nb, f"merge_bwd_{l}")
    g["post_norm_g"] = dgp[0]
    g["w_out"] = _w_out_grad(mixt, dy, f"w_out_grad_{l}")
    (d_a, dgh, dlb), arrived = _hgrn_bwd(dmix, proj, o_h, s0, weights["hgrn_norm_g"][l:l + 1], lbs[l:l + 1],
                                         _block_ones(HGRN_W, BF16), nb, f"hgrn_bwd_{l}",
                                         None if send_w_out is None else send_w_out([g["w_out"]]))
    if arrived is not None:
        g["w_out_received"] = arrived[0]
    g["hgrn_norm_g"], g["lbs"] = dgh[0], dlb[0]
    d_b, dwbd, dps = _pool_bwd(dmix, proj, wbd, wbd.T, weights["pool_scale"][l:l + 1], nb, f"pool_bwd_{l}")
    g["pool_w"] = jnp.stack([dwbd[j * HEAD:(j + 1) * HEAD, j * HEAD:(j + 1) * HEAD] for j in range(len(POOL_WINDOWS))])
    g["pool_scale"] = dps[0]
    (d_qc, d_kc, d_vc, dc_k, dc_q), rode = _fox_bwd(proj, qt, kt, da, dat, c_col, c_row, lse, delta, nb,
                                                    f"fox_bwd_{l}", rider)
    d_fc, dbias = _fox_decay_bwd(dc_q, dc_k, fc, bias, nb, f"fox_decay_bwd_{l}")
    g["fox_f_bias"] = dbias[0, :FOX_HEADS]
    pieces = [(d_a, C_QA), (d_b, C_UB), (d_qc, C_QC), (d_kc, C_KC), (d_vc, C_VC), (d_gc, C_GC), (d_fc, None)]
    g["w_in"] = _w_in_grad(ht, pieces, f"w_in_grad_{l}")
    return g, pieces, rode


def _layer_bwd_input(l, dx, pieces, saved, weights, lw, rider=None):
    (dxi, dgpre), rode = _in_proj_bwd(pieces, lw[2], lw[3], saved[0], weights["pre_norm_g"][l:l + 1], dx,
                                      f"in_proj_bwd_{l}", rider)
    return dxi, dgpre[0], rode


def kernel(x, lower_bounds, pre_norm_g, w_in, hgrn_norm_g, fox_f_bias, pool_w, pool_scale, w_out, post_norm_g, loss_target, m_lower_bounds, m_pre_norm_g, m_w_in, m_hgrn_norm_g, m_fox_f_bias, m_pool_w, m_pool_scale, m_w_out, m_post_norm_g, v_lower_bounds, v_pre_norm_g, v_w_in, v_hgrn_norm_g, v_fox_f_bias, v_pool_w, v_pool_scale, v_w_out, v_post_norm_g):
    weights = dict(lower_bounds=lower_bounds, pre_norm_g=pre_norm_g, hgrn_norm_g=hgrn_norm_g, fox_f_bias=fox_f_bias,
                   pool_w=pool_w, pool_scale=pool_scale, post_norm_g=post_norm_g)
    mom_m = dict(lower_bounds=m_lower_bounds, pre_norm_g=m_pre_norm_g, hgrn_norm_g=m_hgrn_norm_g, fox_f_bias=m_fox_f_bias,
                 pool_w=m_pool_w, pool_scale=m_pool_scale, post_norm_g=m_post_norm_g)
    mom_v = dict(lower_bounds=v_lower_bounds, pre_norm_g=v_pre_norm_g, hgrn_norm_g=v_hgrn_norm_g, fox_f_bias=v_fox_f_bias,
                 pool_w=v_pool_w, pool_scale=v_pool_scale, post_norm_g=v_post_norm_g)
    depth = w_in.shape[0]
    nb, t, d = x.shape
    n = nb * t
    core = lax.axis_index("c").astype(jnp.int32).reshape(1)
    shards = [(w_in[l].astype(BF16), w_out[l].astype(BF16)) for l in range(depth)]
    lbs = _lower_bound_table(lower_bounds, "lower_bound_table")

    (g_in,) = _gather_weights(shards[0][0])
    coming = tuple(_assemble_w_in(g_in, "assemble_w_in_0")) + (None, None)
    xl, saved, lw = x.reshape(n, d), [], []
    for l in range(depth):
        last = l + 1 == depth
        ride_h = ([shards[l][1]] if coming[4] is None else []) + ([] if last else [shards[l + 1][1]])
        xl, sv, lw_l, (rode_h, rode_c) = _layer_fwd(
            l, xl, lbs, weights, coming, nb, _gather_rider(ride_h) if ride_h else None,
            None if last else _gather_rider([shards[l + 1][0]]), loss_target.reshape(n, d) if last else None)
        saved.append(sv)
        lw.append(lw_l)
        if not last:
            coming = tuple(_assemble_w_in(rode_c[0], f"assemble_w_in_{l + 1}")) + _w_out_parts(rode_h[0])
    dx, sq = xl
    loss_here = 0.5 * jnp.sum(sq) / d

    grads, recv_in, pending = [None] * depth, [None] * depth, None
    for l in reversed(range(depth)):
        g, pieces, rode = _layer_bwd(l, dx, saved[l], lbs, weights, lw[l], nb, pending, _direct_exchange_rider)
        if rode is not None:
            recv_in[l + 1] = rode[0]
        if l > 0:
            pending = _direct_exchange_rider([g["w_in"]])
            dx, g["pre_norm_g"], _ = _layer_bwd_input(l, dx, pieces, saved[l], weights, lw[l])
        else:
            (other,) = _run_rider(_swap_rider([g["w_in"]]), "grad_swap")
            summed = _pair_add(g["w_in"], other, core, "grad_pair_add")
            dx, g["pre_norm_g"], (recv_in[l],) = _layer_bwd_input(l, dx, pieces, saved[l], weights, lw[l],
                                                                  _chip_exchange_rider([summed]))
        grads[l] = g
    small = {k: jnp.stack([grads[l][k] for l in range(depth)]) for k in SMALL if k != "lower_bounds"}
    small["lower_bounds"] = _lower_bound_bwd(lower_bounds, jnp.stack([grads[l]["lbs"] for l in range(depth)]),
                                             "lower_bound_bwd")
    (r_small,) = _run_rider(_gather_rider([_pack_small(small, loss_here)]), "small_grads_gather")

    res_in, _ = _sum_adamw(recv_in, w_in, m_w_in, v_w_in, "adamw_w_in")
    res_out, _ = _sum_adamw([grads[l]["w_out_received"] for l in range(depth)], w_out, m_w_out, v_w_out, "adamw_w_out")
    res_small, _ = _sum_adamw([r_small], _pack_small(weights)[None], _pack_small(mom_m)[None], _pack_small(mom_v)[None],
                              "adamw_small")
    loss = res_small[0][0][_small_size(weights) // SMALL_LANES, _small_size(weights) % SMALL_LANES]

    names = ("lower_bounds", "pre_norm_g", "w_in", "hgrn_norm_g", "fox_f_bias", "pool_w", "pool_scale", "w_out", "post_norm_g")
    outs = [loss, dx.reshape(nb, t, d)]
    for i in range(4):
        full = dict(_unpack_small(res_small[i][0], weights), w_in=res_in[i], w_out=res_out[i])
        outs += [full[k] for k in names]
    return tuple(outs)
```

```python
import functools

import jax
import jax.numpy as jnp
from jax import lax
from jax.experimental import pallas as pl
from jax.experimental.pallas import tpu as pltpu

F32, BF16 = jnp.float32, jnp.bfloat16
MESH = pl.DeviceIdType.MESH
N_DEV = 8

NORM_EPS = 1e-6
MASK_VALUE = -1e30
TINY = 1e-30
CHUNK = 64
SUB = 16
HGRN_W, POOL_W, FOX_W = 256, 256, 512
HEAD = 64
FOX_HEADS = 8
POOL_WINDOWS = (2, 4, 8, 16)
POOL_HALO = 16
MAIN_W = 3584
FC_PAD = 128
C_QA, C_FA, C_IA, C_GA, C_UB, C_GB, C_QC, C_KC, C_VC, C_GC = 0, 256, 512, 768, 1024, 1280, 1536, 2048, 2560, 3072
FOX_SCALE = HEAD ** -0.5

ADAM_LR, ADAM_B1, ADAM_B2, ADAM_EPS, ADAM_WD, ADAM_STEP = 0.001, 0.9, 0.999, 1e-08, 0.01, 10

VMEM_LIMIT = 56 * 1024 * 1024


def _pc(fn, name, **kw):
    return pl.pallas_call(fn, name=name, **kw)


def _params(**kw):
    return pltpu.CompilerParams(vmem_limit_bytes=VMEM_LIMIT, **kw)


class _Rider:
    def __init__(self, inputs, out_shapes, n_sems, n_local, plan):
        self.inputs, self.out_shapes, self.n_sems, self.n_local, self.plan = list(inputs), list(out_shapes), n_sems, n_local, plan

    def start(self, ins, outs, *sems):
        sends, _, locs = self.plan(ins, outs, *sems)
        for cp in locs + sends:
            cp.start()

    def wait(self, ins, outs, *sems):
        sends, recvs, locs = self.plan(ins, outs, *sems)
        for cp in recvs:
            cp.wait_recv()
        for cp in sends:
            cp.wait_send()
        for cp in locs:
            cp.wait()

    def sem_shapes(self):
        return [pltpu.SemaphoreType.DMA((self.n_sems,)), pltpu.SemaphoreType.DMA((self.n_sems,)),
                pltpu.SemaphoreType.DMA((self.n_local,))]


def _call(body, name, args, rider=None, *, grid, in_specs, out_specs, out_shape, scratch_shapes=(), **kw):
    if rider is None:
        res = _pc(body, name, grid=grid, in_specs=in_specs, out_specs=out_specs, out_shape=out_shape,
                  scratch_shapes=list(scratch_shapes), **kw)(*args)
        return res, None
    n_in, n_out, n_scr = len(in_specs), len(out_specs), len(scratch_shapes)
    n_rin, n_rout = len(rider.inputs), len(rider.out_shapes)

    def ridden(*refs):
        ins, refs = refs[:n_in], refs[n_in:]
        rins, refs = refs[:n_rin], refs[n_rin:]
        outs, refs = refs[:n_out], refs[n_out:]
        routs, refs = refs[:n_rout], refs[n_rout:]
        scr, sems = refs[:n_scr], refs[n_scr:]
        first = functools.reduce(jnp.logical_and, [pl.program_id(a) == 0 for a in range(len(grid))])
        last = functools.reduce(jnp.logical_and, [pl.program_id(a) == g - 1 for a, g in enumerate(grid)])

        @pl.when(first)
        def _():
            rider.start(rins, routs, *sems)

        body(*ins, *outs, *scr)

        @pl.when(last)
        def _():
            rider.wait(rins, routs, *sems)

    any_spec = pl.BlockSpec(memory_space=pl.ANY)
    res = _pc(ridden, name, grid=grid, in_specs=list(in_specs) + [any_spec] * n_rin,
              out_specs=list(out_specs) + [any_spec] * n_rout, out_shape=list(out_shape) + rider.out_shapes,
              scratch_shapes=list(scratch_shapes) + rider.sem_shapes(), **kw)(*args, *rider.inputs)
    return res[:n_out], res[n_out:]


def _run_rider(rider, name):
    n_rin = len(rider.inputs)

    def body(*refs):
        ins, outs, sems = refs[:n_rin], refs[n_rin:n_rin + len(rider.out_shapes)], refs[n_rin + len(rider.out_shapes):]
        rider.start(ins, outs, *sems)
        rider.wait(ins, outs, *sems)

    any_spec = pl.BlockSpec(memory_space=pl.ANY)
    return _pc(body, name, in_specs=[any_spec] * n_rin, out_specs=[any_spec] * len(rider.out_shapes),
               out_shape=rider.out_shapes, scratch_shapes=rider.sem_shapes())(*rider.inputs)


def _dot(a, b):
    return jnp.dot(a, b, preferred_element_type=F32)


def _dot_nt(a, b):
    return lax.dot_general(a, b, (((1,), (1,)), ((), ())), preferred_element_type=F32)


def _dot_tn(a, b):
    return lax.dot_general(a, b, (((0,), (0,)), ((), ())), preferred_element_type=F32)


def _sel_dot_exact(sel, x):
    hi = x.astype(BF16)
    rest = x - hi.astype(F32)
    mid = rest.astype(BF16)
    lo = (rest - mid.astype(F32)).astype(BF16)
    sb = sel.astype(BF16)
    return _dot(sb, hi) + _dot(sb, mid) + _dot(sb, lo)


def _split2(x):
    hi = x.astype(BF16)
    return hi, (x - hi.astype(F32)).astype(BF16)


def _sel_dot(sel, x):
    hi, lo = _split2(x)
    sb = sel.astype(BF16)
    return _dot(sb, hi) + _dot(sb, lo)


def _dot_sel(x, sel):
    hi, lo = _split2(x)
    sb = sel.astype(BF16)
    return _dot(hi, sb) + _dot(lo, sb)


def _sigmoid(x):
    return 1.0 / (1.0 + jnp.exp(-x))


def _block_ones(n, dtype):
    r = lax.broadcasted_iota(jnp.int32, (n, n), 0) // HEAD
    c = lax.broadcasted_iota(jnp.int32, (n, n), 1) // HEAD
    return (r == c).astype(dtype)


def _sds(shape, dtype):
    return jax.ShapeDtypeStruct(shape, dtype)


def _in_proj_fwd(x, g_pre, w_main, w_fc, name):
    n, d = x.shape
    tm = min(512, n)

    def body(x_ref, g_ref, w_ref, wf_ref, proj_ref, fc_ref, ht_ref, qt_ref, kt_ref):
        xv = x_ref[...]
        r = lax.rsqrt(jnp.mean(xv * xv, axis=-1, keepdims=True) + NORM_EPS)
        hf = xv * r * g_ref[...]
        hb = hf.astype(BF16)
        ht_ref[...] = hf.T.astype(BF16)
        for j in range(0, MAIN_W, FOX_W):
            res = _dot(hb, w_ref[:, j:j + FOX_W])
            proj_ref[:, j:j + FOX_W] = res
            if j == C_QC:
                qt_ref[...] = (res * FOX_SCALE).T.astype(BF16)
            if j == C_KC:
                kt_ref[...] = res.T.astype(BF16)
        fc_ref[...] = _dot(hb, wf_ref[...])

    def cols(rows):
        return pl.BlockSpec((rows, tm), lambda i: (0, i))

    return _pc(
        body, name, grid=(n // tm,),
        in_specs=[pl.BlockSpec((tm, d), lambda i: (i, 0)), pl.BlockSpec((1, d), lambda i: (0, 0)),
                  pl.BlockSpec((d, MAIN_W), lambda i: (0, 0)), pl.BlockSpec((d, FC_PAD), lambda i: (0, 0))],
        out_specs=[pl.BlockSpec((tm, MAIN_W), lambda i: (i, 0)), pl.BlockSpec((tm, FC_PAD), lambda i: (i, 0)),
                   cols(d), cols(FOX_W), cols(FOX_W)],
        out_shape=[_sds((n, MAIN_W), F32), _sds((n, FC_PAD), F32), _sds((d, n), BF16), _sds((FOX_W, n), BF16),
                   _sds((FOX_W, n), BF16)],
        compiler_params=_params(),
    )(x, g_pre, w_main, w_fc)


def _fox_decay_fwd(fc, bias, nb, name):
    n = fc.shape[0]
    t = n // nb
    tt = min(256, t)
    nt = t // tt

    def body(fc_ref, b_ref, c_ref, cr_ref):
        r = lax.broadcasted_iota(jnp.int32, (tt, tt), 0)
        cc = lax.broadcasted_iota(jnp.int32, (tt, tt), 1)
        carry = jnp.zeros((1, FC_PAD), F32)
        for i in range(nt):
            rows = slice(i * tt, (i + 1) * tt)
            xv = fc_ref[rows, :] + b_ref[...]
            lf = jnp.minimum(xv, 0.0) - jnp.log(1.0 + jnp.exp(-jnp.abs(xv)))
            cs = _sel_dot_exact(r >= cc, lf) + carry
            c_ref[rows, :] = cs
            cr_ref[:, rows] = cs.T[:FOX_HEADS, :]
            carry = cs[tt - 1:tt, :]

    return _pc(
        body, name, grid=(nb,),
        in_specs=[pl.BlockSpec((t, FC_PAD), lambda b: (b, 0)), pl.BlockSpec((1, FC_PAD), lambda b: (0, 0))],
        out_specs=[pl.BlockSpec((t, FC_PAD), lambda b: (b, 0)), pl.BlockSpec((None, FOX_HEADS, t), lambda b: (b, 0, 0))],
        out_shape=[_sds((n, FC_PAD), F32), _sds((nb, FOX_HEADS, t), F32)],
        compiler_params=_params(),
    )(fc, bias)


def _hgrn_gates(q, z, lb):
    sig = _sigmoid(z)
    sn = _sigmoid(-z)
    f = lb + (1.0 - lb) * sig
    g = jnp.log(jnp.maximum(f, TINY))
    k = (1.0 - lb) * sn
    sq = _sigmoid(q)
    return sig, sn, f, g, k, sq


def _sub_tri(n, lower):
    r = lax.broadcasted_iota(jnp.int32, (n, n), 0)
    c = lax.broadcasted_iota(jnp.int32, (n, n), 1)
    tri = (r >= c) if lower else (r <= c)
    return jnp.logical_and(r // SUB == c // SUB, tri).astype(F32)


def _live_rows(t):
    return 8 * (t // 8 + 1)


def _pad_rows(x):
    return x if x.shape[0] == SUB else jnp.concatenate([x, jnp.zeros((SUB - x.shape[0], x.shape[1]), x.dtype)], axis=0)


def _hgrn_decays(qs, k, b):
    srow = lax.broadcasted_iota(jnp.int32, (SUB, HGRN_W), 0)
    es, ws = [], []
    for t in range(SUB):
        r = _live_rows(t)
        e = jnp.where(srow[:r] <= t, jnp.exp(b[t:t + 1, :] - b[:r]), 0.0)
        es.append(e)
        ws.append(_pad_rows(e * (qs[t:t + 1, :] * k[:r])))
    return srow, es, ws


def _hgrn_state_step(st, k, v, b, bmask):
    bl = b[SUB - 1:SUB, :]
    ktil = k * jnp.exp(bl - b)
    return st * jnp.exp(bl) + _dot_tn(v.astype(BF16), ktil.astype(BF16)) * bmask


def _hgrn_tile(t):
    return min(256, t)


def _hgrn_fwd(proj, lb, ones_b, nb, name, rider=None):
    n = proj.shape[0]
    t = n // nb
    tt = _hgrn_tile(t)
    nt = t // tt
    ncs = tt // CHUNK
    w = HGRN_W

    def body(q_ref, z_ref, v_ref, lb_ref, ones_ref, o_ref, s0_ref, st_s, b_s, qs_s, k_s):
        @pl.when(pl.program_id(1) == 0)
        def _():
            st_s[...] = jnp.zeros_like(st_s)

        q = q_ref[...]
        _, _, _, g, k, sq = _hgrn_gates(q, z_ref[...], lb_ref[...])
        b_s[...] = _sel_dot(_sub_tri(tt, True), g)
        qs_s[...] = q * sq
        k_s[...] = k
        bmask = _block_ones(w, F32)
        ones_b = ones_ref[...]

        def chunk(c, carry):
            st = st_s[...]
            s0_ref[c] = st
            base = pl.multiple_of(c * CHUNK, CHUNK)
            tiles = []
            for u in range(CHUNK // SUB):
                rows = pl.ds(base + u * SUB, SUB)
                tiles.append((qs_s[rows, :], k_s[rows, :], v_ref[rows, :], b_s[rows, :]))
            aexps = []
            for qs, k, v, b in tiles:
                _, _, ws = _hgrn_decays(qs, k, b)
                aexps.append(_dot(jnp.concatenate(ws, axis=0).astype(BF16), ones_b))
            inters = []
            for qs, k, v, b in tiles:
                inters.append(_dot_nt((qs * jnp.exp(b)).astype(BF16), st.astype(BF16)))
                st = _hgrn_state_step(st, k, v, b, bmask)
            st_s[...] = st
            for u, ((qs, k, v, b), aexp, o) in enumerate(zip(tiles, aexps, inters)):
                for t in range(SUB):
                    r = _live_rows(t)
                    row = o[t:t + 1, :] + jnp.sum(aexp[t * SUB:t * SUB + r, :] * v[:r], axis=0, keepdims=True)
                    o_ref[pl.ds(base + u * SUB + t, 1), :] = row
            return carry

        lax.fori_loop(0, ncs, chunk, 0)

    def col(j):
        return pl.BlockSpec((tt, w), lambda b, i: (b * nt + i, j))

    return _call(
        body, name, (proj, proj, proj, lb, ones_b), rider, grid=(nb, nt),
        in_specs=[col(C_QA // w), col(C_FA // w), col(C_IA // w), pl.BlockSpec((1, w), lambda b, i: (0, 0)),
                  pl.BlockSpec((w, w), lambda b, i: (0, 0))],
        out_specs=[pl.BlockSpec((tt, w), lambda b, i: (b * nt + i, 0)),
                   pl.BlockSpec((ncs, w, w), lambda b, i: (b * nt + i, 0, 0))],
        out_shape=[_sds((n, w), F32), _sds((n // CHUNK, w, w), F32)],
        scratch_shapes=[pltpu.VMEM((w, w), F32)] + [pltpu.VMEM((tt, w), F32)] * 3,
        compiler_params=_params(),
    )


def _pool_lane_windows():
    lane = lax.broadcasted_iota(jnp.int32, (1, POOL_W), 1) // HEAD
    wl = jnp.zeros((1, POOL_W), F32)
    for gi, win in enumerate(POOL_WINDOWS):
        wl = jnp.where(lane == gi, float(win), wl)
    return lane, wl


def _pool_select(lane, parts):
    out = parts[-1]
    for gi in range(len(parts) - 2, -1, -1):
        out = jnp.where(lane == gi, parts[gi], out)
    return out


def _pool_mix(u, halo, t0, tt):
    lane, wl = _pool_lane_windows()
    ext = jnp.concatenate([halo, u], axis=0)
    sums, cur, shift = [], ext, 1
    for _ in POOL_WINDOWS:
        cur = cur + pltpu.roll(cur, shift, axis=0)
        sums.append(cur[POOL_HALO:, :])
        shift *= 2
    tpos = (t0 + lax.broadcasted_iota(jnp.int32, (tt, POOL_W), 0)).astype(F32)
    cnt = jnp.minimum(tpos + 1.0, wl)
    return _pool_select(lane, sums) / cnt - u, cnt


def _pool_specs(tt, nt, nhb):
    cu, cg = C_UB // POOL_W, C_GB // POOL_W
    return [pl.BlockSpec((tt, POOL_W), lambda b, i: (b * nt + i, cu)),
            pl.BlockSpec((tt, POOL_W), lambda b, i: (b * nt + i, cg)),
            pl.BlockSpec((POOL_HALO, POOL_W), lambda b, i: (jnp.maximum((b * nt + i) * nhb - 1, 0), cu))]


def _pool_fwd(proj, wbd, scale, nb, name):
    n = proj.shape[0]
    t = n // nb
    tt = min(512, t)
    nt = t // tt
    nhb = tt // POOL_HALO

    def body(u_ref, g_ref, h_ref, w_ref, s_ref, o_ref):
        i = pl.program_id(1)
        halo = jnp.where(i == 0, 0.0, h_ref[...])
        pooled, _ = _pool_mix(u_ref[...], halo, i * tt, tt)
        gv = g_ref[...]
        o_ref[...] = _dot(pooled.astype(BF16), w_ref[...]) * s_ref[...] * (gv * _sigmoid(gv))

    return _pc(
        body, name, grid=(nb, nt),
        in_specs=_pool_specs(tt, nt, nhb) + [pl.BlockSpec((POOL_W, POOL_W), lambda b, i: (0, 0)),
                                             pl.BlockSpec((1, POOL_W), lambda b, i: (0, 0))],
        out_specs=pl.BlockSpec((tt, POOL_W), lambda b, i: (b * nt + i, 0)),
        out_shape=_sds((n, POOL_W), F32),
        compiler_params=_params(),
    )(proj, proj, proj, wbd, scale)


def _rows_reduce(x, op, final):
    while x.shape[0] > 8 and x.shape[0] % 16 == 0:
        half = x.shape[0] // 2
        x = op(x[:half], x[half:])
    return final(x, axis=0, keepdims=True)


def _tri_pair(step, n, group=1):
    counts = [a // group + 1 for a in range(n)]
    firsts = [sum(counts[:a]) for a in range(1, n)]
    a = sum([(step >= f).astype(jnp.int32) for f in firsts], jnp.int32(0))
    first = sum([jnp.where(step >= f, c, 0) for f, c in zip(firsts, counts)], jnp.int32(0))
    return a, step - first


def _tri_steps(n, group=1):
    return sum(a // group + 1 for a in range(n))


def _lane_lo():
    return lax.broadcasted_iota(jnp.int32, (1, 2 * HEAD), 1) < HEAD


def _put_col(tile, hh, colv):
    lane = lax.broadcasted_iota(jnp.int32, tile.shape, 1)
    return jnp.where(lane == hh, colv, tile)


def _fox_fwd(proj, kt, c_col, c_row, nb, name, rider=None):
    n = proj.shape[0]
    t = n // nb
    tb = min(256, t)
    nq = t // tb
    pw = 2 * HEAD
    kw = 2 if nq % 2 == 0 else 1
    nk = nq // kw

    def body(q_ref, kt_ref, v_ref, cc_ref, cr_ref, o_ref, lse_ref, m_s, acc_s, cq_s):
        qi, kk = _tri_pair(pl.program_id(1), nq, kw)

        @pl.when(kk == 0)
        def _():
            m_s[...] = jnp.full_like(m_s, -jnp.inf)
            acc_s[...] = jnp.zeros_like(acc_s)
            for hh in range(FOX_HEADS):
                cq_s[hh] = jnp.broadcast_to(cc_ref[:, hh:hh + 1], (tb, pw))

        def block(masked, sub):
            lo = _lane_lo()
            keys = slice(sub * tb, (sub + 1) * tb)
            if masked:
                causal = lax.broadcasted_iota(jnp.int32, (tb, tb), 0) >= lax.broadcasted_iota(jnp.int32, (tb, tb), 1)
            def lanes(hh):
                return lo if hh % 2 == 0 else jnp.logical_not(lo)

            def scores(hh):
                sl = slice((hh // 2) * pw, (hh // 2 + 1) * pw)
                return _dot(jnp.where(lanes(hh), q_ref[:, sl] * FOX_SCALE, 0.0).astype(BF16), kt_ref[sl, keys])

            ahead = scores(0)
            for hh in range(FOX_HEADS):
                s = ahead
                if hh + 1 < FOX_HEADS:
                    ahead = scores(hh + 1)
                s = s + (jnp.tile(cq_s[hh], (1, tb // pw)) - cr_ref[hh:hh + 1, keys])
                if masked:
                    s = jnp.where(causal, s, MASK_VALUE)
                m_prev = m_s[hh]
                m_new = jnp.maximum(m_prev, jnp.max(s, axis=1, keepdims=True))
                alpha = jnp.exp(m_prev - m_new)
                pe = jnp.exp(s - jnp.tile(m_new, (1, tb // pw)))
                m_s[hh] = m_new
                vf = v_ref[keys, (hh // 2) * pw:(hh // 2 + 1) * pw]
                acc_s[hh] = alpha * acc_s[hh] + _dot(pe.astype(BF16), jnp.where(lanes(hh), vf, 1.0).astype(BF16))

        def finish():
            lo = _lane_lo()
            m_all, l_all = jnp.zeros((tb, FC_PAD), F32), jnp.ones((tb, FC_PAD), F32)
            for p in range(FOX_HEADS // 2):
                a0, a1 = acc_s[2 * p], acc_s[2 * p + 1]
                both = pltpu.roll(jnp.where(lo, a1, a0), HEAD, axis=1)
                o_ref[:, p * pw:(p + 1) * pw] = jnp.where(lo, a0, a1) / both
                m_all = _put_col(_put_col(m_all, 2 * p, m_s[2 * p]), 2 * p + 1, m_s[2 * p + 1])
                l_all = _put_col(_put_col(l_all, 2 * p, both), 2 * p + 1, a1)
            lse_ref[...] = (m_all + jnp.log(l_all)).T[:FOX_HEADS, :]

        for sub in range(kw):
            @pl.when(kk * kw + sub < qi)
            def _(sub=sub):
                block(False, sub)

            @pl.when(kk * kw + sub == qi)
            def _(sub=sub):
                block(True, sub)
                finish()

    def qspec(wd, j):
        return pl.BlockSpec((tb, wd), lambda b, st: (b * nq + _tri_pair(st, nq, kw)[0], j))

    def kspec(j):
        return pl.BlockSpec((kw * tb, FOX_W), lambda b, st: (b * nk + _tri_pair(st, nq, kw)[1], j))

    return _call(
        body, name, (proj, kt, proj, c_col, c_row), rider, grid=(nb, _tri_steps(nq, kw)),
        in_specs=[qspec(FOX_W, C_QC // FOX_W),
                  pl.BlockSpec((FOX_W, kw * tb), lambda b, st: (0, b * nk + _tri_pair(st, nq, kw)[1])),
                  kspec(C_VC // FOX_W), qspec(FC_PAD, 0),
                  pl.BlockSpec((None, FOX_HEADS, kw * tb), lambda b, st: (b, 0, _tri_pair(st, nq, kw)[1]))],
        out_specs=[qspec(FOX_W, 0),
                   pl.BlockSpec((None, FOX_HEADS, tb), lambda b, st: (b, 0, _tri_pair(st, nq, kw)[0]))],
        out_shape=[_sds((n, FOX_W), F32), _sds((nb, FOX_HEADS, t), F32)],
        scratch_shapes=[pltpu.VMEM((FOX_HEADS, tb, pw), F32), pltpu.VMEM((FOX_HEADS, tb, pw), F32),
                        pltpu.VMEM((FOX_HEADS, tb, pw), F32)],
        compiler_params=_params(),
    )


def _head_mean(x, ones_f):
    return _dot_sel(x, ones_f) * (1.0 / HEAD)


def _merge_fwd(x, proj, o_h, o_b, o_c, gh, w_out, g_post, name, target=None):
    n, d = x.shape
    tm = min(512, n)

    def body(*refs):
        x_ref, ga_ref, gc_ref, oh_ref, ob_ref, oc_ref, gh_ref, w_ref, gp_ref = refs[:9]
        if target is None:
            xo_ref, mixt_ref, y_ref = refs[9:]
        else:
            t_ref, dx_ref, sq_ref, mixt_ref, y_ref = refs[9:]
        oh = oh_ref[...]
        ones_f = _block_ones(HGRN_W, F32)
        na = oh * lax.rsqrt(_head_mean(oh * oh, ones_f) + NORM_EPS) * gh_ref[...]
        ga, gc = ga_ref[...], gc_ref[...]
        mixed = jnp.concatenate([na * (ga * _sigmoid(ga)), ob_ref[...], oc_ref[...] * (gc * _sigmoid(gc))], axis=1)
        mixt_ref[...] = mixed.T.astype(BF16)
        y = _dot(mixed.astype(BF16), w_ref[...])
        y_ref[...] = y
        xn = x_ref[...] + y * lax.rsqrt(jnp.mean(y * y, axis=-1, keepdims=True) + NORM_EPS) * gp_ref[...]
        if target is None:
            xo_ref[...] = xn
        else:
            @pl.when(pl.program_id(0) == 0)
            def _():
                sq_ref[...] = jnp.zeros_like(sq_ref)

            e = xn - t_ref[...]
            dx_ref[...] = e * (1.0 / d)
            sq_ref[...] += jnp.sum(e * e, axis=0, keepdims=True)

    def row(wd, j=0):
        return pl.BlockSpec((tm, wd), lambda i: (i, j))

    def full(a, b):
        return pl.BlockSpec((a, b), lambda i: (0, 0))

    head = [] if target is None else [target]
    res = _pc(
        body, name, grid=(n // tm,),
        in_specs=[row(d), row(HGRN_W, C_GA // HGRN_W), row(FOX_W, C_GC // FOX_W), row(HGRN_W), row(POOL_W), row(FOX_W),
                  full(1, HGRN_W), full(d, d), full(1, d)] + [row(d)] * len(head),
        out_specs=[row(d)] + [full(1, d)] * len(head) + [pl.BlockSpec((d, tm), lambda i: (0, i)), row(d)],
        out_shape=[_sds((n, d), F32)] + [_sds((1, d), F32)] * len(head) + [_sds((d, n), BF16), _sds((n, d), F32)],
        compiler_params=_params(),
    )(x, proj, proj, o_h, o_b, o_c, gh, w_out, g_post, *head)
    return (res[0], res[1], res[2]) if target is None else ((res[0], res[1]), res[2], res[3])


def _rms_bwd(dy_scaled, xhat, r):
    return r * (dy_scaled - xhat * jnp.mean(dy_scaled * xhat, axis=-1, keepdims=True))


def _merge_bwd(dxo, y, g_post, w_out_t, proj, o_c, nb, name):
    n, d = y.shape
    t = n // nb
    tm = min(512, t)
    nt = t // tm
    wab = HGRN_W + POOL_W

    def body(dx_ref, y_ref, gp_ref, wt_ref, gc_ref, oc_ref, dy_ref, dm_ref, dgp_ref, da_ref, dat_ref, dg_ref, dl_ref):
        @pl.when(pl.program_id(0) == 0)
        def _():
            dgp_ref[...] = jnp.zeros_like(dgp_ref)

        yv, dxv = y_ref[...], dx_ref[...]
        r = lax.rsqrt(jnp.mean(yv * yv, axis=-1, keepdims=True) + NORM_EPS)
        yh = yv * r
        dgp_ref[...] += jnp.sum(dxv * yh, axis=0, keepdims=True)
        dyb = _rms_bwd(dxv * gp_ref[...], yh, r).astype(BF16)
        dy_ref[...] = dyb
        dm_ref[...] = _dot(dyb, wt_ref[:, :wab])
        dmc = _dot(dyb, wt_ref[:, wab:])
        gc, oc = gc_ref[...], oc_ref[...]
        sg = _sigmoid(gc)
        da = dmc * (gc * sg)
        da_ref[...] = da.astype(BF16)
        dat_ref[...] = da.T.astype(BF16)
        dg_ref[...] = (dmc * oc * (sg * (1.0 + gc * (1.0 - sg)))).astype(BF16)
        rr = lax.broadcasted_iota(jnp.int32, (FOX_W, FC_PAD), 0) // HEAD
        cc = lax.broadcasted_iota(jnp.int32, (FOX_W, FC_PAD), 1)
        dl_ref[...] = _dot_sel(da * oc, (rr == cc).astype(F32)).T[:FOX_HEADS, :]

    def row(wd, j=0):
        return pl.BlockSpec((tm, wd), lambda i: (i, j))

    def full(a, b):
        return pl.BlockSpec((a, b), lambda i: (0, 0))

    return _pc(
        body, name, grid=(n // tm,),
        in_specs=[row(d), row(d), full(1, d), full(d, d), row(FOX_W, C_GC // FOX_W), row(FOX_W)],
        out_specs=[row(d), row(wab), full(1, d), row(FOX_W), pl.BlockSpec((FOX_W, tm), lambda i: (0, i)), row(FOX_W),
                   pl.BlockSpec((None, FOX_HEADS, tm), lambda i: (i // nt, 0, i % nt))],
        out_shape=[_sds((n, d), BF16), _sds((n, wab), F32), _sds((1, d), F32), _sds((n, FOX_W), BF16),
                   _sds((FOX_W, n), BF16), _sds((n, FOX_W), BF16), _sds((nb, FOX_HEADS, t), F32)],
        compiler_params=_params(),
    )(dxo, y, g_post, w_out_t, proj, o_c)


def _w_out_grad(mixt, dy, name):
    d, n = mixt.shape
    rows = d // N_DEV

    def body(a_ref, b_ref, o_ref):
        o_ref[...] = _dot(a_ref[...], b_ref[...]).astype(BF16)

    return _pc(
        body, name, grid=(N_DEV,),
        in_specs=[pl.BlockSpec((rows, n), lambda j: (j, 0)), pl.BlockSpec((n, d), lambda j: (0, 0))],
        out_specs=pl.BlockSpec((None, None, rows, d), lambda j: (j % 2, j // 2, 0, 0)),
        out_shape=_sds((2, N_DEV // 2, rows, d), BF16),
        compiler_params=_params(),
    )(mixt, dy)


def _w_in_grad(ht, pieces, name):
    d, n = ht.shape
    ta, tk = min(512, d), min(512, n)
    nk = n // tk
    arrays = [p for p, _ in pieces]
    widths = [p.shape[1] for p in arrays]
    offs = [sum(widths[:i]) for i in range(len(widths))]
    in_w = MAIN_W + FOX_HEADS
    shard = in_w // N_DEV

    def body(*refs):
        a_ref, p_refs = refs[0], refs[1:1 + len(arrays)]
        o_ref, acc = refs[1 + len(arrays):]
        k = pl.program_id(1)

        @pl.when(k == 0)
        def _():
            acc[...] = jnp.zeros_like(acc)

        a = a_ref[...]
        for pr, off, wd in zip(p_refs, offs, widths):
            for j in range(0, wd, 512):
                jw = min(512, wd - j)
                acc[:, off + j:off + j + jw] += _dot(a, pr[:, j:j + jw])

        @pl.when(k == nk - 1)
        def _():
            for j in range(N_DEV):
                o_ref[j % 2, j // 2] = acc[:, j * shard:(j + 1) * shard].astype(BF16)

    return _pc(
        body, name, grid=(d // ta, nk),
        in_specs=[pl.BlockSpec((ta, tk), lambda i, k: (i, k))] + [pl.BlockSpec((tk, wd), lambda i, k: (k, 0)) for wd in widths],
        out_specs=pl.BlockSpec((2, N_DEV // 2, ta, shard), lambda i, k: (0, 0, i, 0)),
        out_shape=_sds((2, N_DEV // 2, d, shard), BF16),
        scratch_shapes=[pltpu.VMEM((ta, sum(widths)), F32)],
        compiler_params=_params(),
    )(ht, *arrays)


def _hgrn_state_bwd(qs, k, v, b, do, s0, ds1, bmask):
    bl = b[SUB - 1:SUB, :]
    eb, ebl, ekt = jnp.exp(b), jnp.exp(bl), jnp.exp(bl - b)
    qe, ktil = qs * eb, k * ekt
    ds1b, dob = ds1.astype(BF16), do.astype(BF16)
    dv = _dot_nt(ktil.astype(BF16), ds1b)
    dqe = _dot(dob, s0.astype(BF16))
    dktil = _dot(v.astype(BF16), ds1b)
    dbl = jnp.sum(dktil * ktil, axis=0, keepdims=True) + ebl * jnp.sum(s0 * ds1, axis=0, keepdims=True)
    ds0 = ds1 * ebl + _dot_tn(dob, qe.astype(BF16)) * bmask
    return dqe * eb, dktil * ekt, dv, dbl, ds0


def _hgrn_intra_bwd(qs, k, v, do, es, aexp, gexp, dq, dk, dv, put_dq_row):
    dks = [dk[j:j + 8] for j in range(0, SUB, 8)]
    dvs = [dv[j:j + 8] for j in range(0, SUB, 8)]
    for t in range(SUB):
        r = _live_rows(t)
        ge = gexp[t * SUB:t * SUB + r, :] * es[t]
        put_dq_row(t, dq[t:t + 1, :] + jnp.sum(ge * k[:r], axis=0, keepdims=True))
        for j in range(r // 8):
            dks[j] = dks[j] + ge[8 * j:8 * j + 8] * qs[t:t + 1, :]
            dvs[j] = dvs[j] + aexp[t * SUB + 8 * j:t * SUB + 8 * j + 8, :] * do[t:t + 1, :]
    return jnp.concatenate(dks, axis=0), jnp.concatenate(dvs, axis=0)


def _hgrn_bwd(dmix, proj, o_h, s0, gh, lb, ones_b, nb, name, rider=None):
    n = proj.shape[0]
    t = n // nb
    tt = _hgrn_tile(t)
    nt = t // tt
    ncs = tt // CHUNK
    nsub = CHUNK // SUB
    w = HGRN_W

    def body(dm_ref, q_ref, z_ref, v_ref, ga_ref, oh_ref, s0_ref, gh_ref, lb_ref, ones_ref,
             dp_ref, dgh_ref, dlb_ref, ds_s, ss_s, b_s, qs_s, k_s, do_s, dq_s, dk_s, dv_s, dbl_s):
        first = jnp.logical_and(pl.program_id(0) == 0, pl.program_id(1) == 0)

        @pl.when(first)
        def _():
            dgh_ref[...] = jnp.zeros_like(dgh_ref)
            dlb_ref[...] = jnp.zeros_like(dlb_ref)

        @pl.when(pl.program_id(1) == 0)
        def _():
            ds_s[...] = jnp.zeros_like(ds_s)

        ones_b = ones_ref[...]
        ones_f = ones_b.astype(F32)
        bmask = _block_ones(w, F32)
        lbv, ghv = lb_ref[...], gh_ref[...]
        oh, ga, dm = oh_ref[...], ga_ref[...], dm_ref[...]
        rn = lax.rsqrt(_head_mean(oh * oh, ones_f) + NORM_EPS)
        nh = oh * rn
        sga = _sigmoid(ga)
        dp_ref[:, 3 * w:4 * w] = (dm * nh * ghv * (sga * (1.0 + ga * (1.0 - sga)))).astype(BF16)
        dn = dm * (ga * sga)
        dgh_ref[...] += jnp.sum(dn * nh, axis=0, keepdims=True)
        dn = dn * ghv
        do_s[...] = rn * (dn - nh * _head_mean(dn * nh, ones_f))
        q = q_ref[...]
        sig, sn, f, g, k, sq = _hgrn_gates(q, z_ref[...], lbv)
        qs = q * sq
        b_s[...] = _sel_dot(_sub_tri(tt, True), g)
        qs_s[...] = qs
        k_s[...] = k

        def chunk(cc, carry):
            c = ncs - 1 - cc
            base = pl.multiple_of(c * CHUNK, CHUNK)
            tiles = []
            for u in range(nsub):
                rows = pl.ds(base + u * SUB, SUB)
                tiles.append((qs_s[rows, :], k_s[rows, :], v_ref[rows, :], b_s[rows, :], do_s[rows, :]))
            st = s0_ref[c]
            for u, (qs, k, v, b, do) in enumerate(tiles):
                ss_s[u] = st
                if u < nsub - 1:
                    st = _hgrn_state_step(st, k, v, b, bmask)
            ds = ds_s[...]
            for u in reversed(range(nsub)):
                qs, k, v, b, do = tiles[u]
                _, es, ws = _hgrn_decays(qs, k, b)
                gs = [_pad_rows(do[t:t + 1, :] * v[:_live_rows(t)]) for t in range(SUB)]
                aexp = _dot(jnp.concatenate(ws, axis=0).astype(BF16), ones_b)
                gexp = _dot(jnp.concatenate(gs, axis=0).astype(BF16), ones_b)
                dq, dk, dv, dbl, ds = _hgrn_state_bwd(qs, k, v, b, do, ss_s[u], ds, bmask)

                def put_dq_row(i, row, r0=base + u * SUB):
                    dq_s[pl.ds(r0 + i, 1), :] = row

                dk, dv = _hgrn_intra_bwd(qs, k, v, do, es, aexp, gexp, dq, dk, dv, put_dq_row)
                dk_s[pl.ds(base + u * SUB, SUB), :] = dk
                dv_s[pl.ds(base + u * SUB, SUB), :] = dv
                dbl_s[pl.ds(base + u * SUB, SUB), :] = jnp.broadcast_to(dbl, (SUB, w))
            ds_s[...] = ds
            return carry

        lax.fori_loop(0, ncs, chunk, 0)
        dqs, dk = dq_s[...], dk_s[...]
        dg = _sel_dot(_sub_tri(tt, False), qs * dqs - k * dk) + dbl_s[...]
        dfz = jnp.where(f > TINY, dg / jnp.maximum(f, TINY), 0.0)
        dlb_ref[...] += jnp.sum(dfz * (1.0 - sig) - dk * sn, axis=0, keepdims=True)
        dp_ref[:, 0:w] = (dqs * (sq * (1.0 + q * (1.0 - sq)))).astype(BF16)
        dp_ref[:, w:2 * w] = ((dfz - dk) * (1.0 - lbv) * sig * sn).astype(BF16)
        dp_ref[:, 2 * w:3 * w] = dv_s[...].astype(BF16)

    def rv(b, i):
        return b * nt + (nt - 1 - i)

    def col(j):
        return pl.BlockSpec((tt, w), lambda b, i: (rv(b, i), j))

    def full(a, bb):
        return pl.BlockSpec((a, bb), lambda b, i: (0, 0))

    return _call(
        body, name, (dmix, proj, proj, proj, proj, o_h, s0, gh, lb, ones_b), rider, grid=(nb, nt),
        in_specs=[col(0), col(C_QA // w), col(C_FA // w), col(C_IA // w), col(C_GA // w), col(0),
                  pl.BlockSpec((ncs, w, w), lambda b, i: (rv(b, i), 0, 0)), full(1, w), full(1, w), full(w, w)],
        out_specs=[pl.BlockSpec((tt, 4 * w), lambda b, i: (rv(b, i), 0)), full(1, w), full(1, w)],
        out_shape=[_sds((n, 4 * w), BF16), _sds((1, w), F32), _sds((1, w), F32)],
        scratch_shapes=[pltpu.VMEM((w, w), F32), pltpu.VMEM((nsub, w, w), F32)] + [pltpu.VMEM((tt, w), F32)] * 8,
        compiler_params=_params(),
    )


def _pool_bwd(dmix, proj, wbd, wbd_t, scale, nb, name):
    n = proj.shape[0]
    t = n // nb
    tt = min(512, t)
    nt = t // tt
    nhb = tt // POOL_HALO
    cu, cg, cm = C_UB // POOL_W, C_GB // POOL_W, HGRN_W // POOL_W

    def body(u_ref, g_ref, h_ref, dm_ref, gn_ref, dmn_ref, w_ref, wt_ref, s_ref, dp_ref, dw_ref, ds_ref):
        i = pl.program_id(1)
        first = jnp.logical_and(pl.program_id(0) == 0, i == 0)

        @pl.when(first)
        def _():
            dw_ref[...] = jnp.zeros_like(dw_ref)
            ds_ref[...] = jnp.zeros_like(ds_ref)

        sc = s_ref[...]
        halo = jnp.where(i == 0, 0.0, h_ref[...])
        pooled, cnt = _pool_mix(u_ref[...], halo, i * tt, tt)
        pb = pooled.astype(BF16)
        pre = _dot(pb, w_ref[...])
        gv, dm = g_ref[...], dm_ref[...]
        sg = _sigmoid(gv)
        silu = gv * sg
        dgb = dm * pre * sc * (sg * (1.0 + gv * (1.0 - sg)))
        ds_ref[...] += jnp.sum(dm * pre * silu, axis=0, keepdims=True)
        dpre = (dm * sc * silu).astype(BF16)
        dw_ref[...] += _dot_tn(pb, dpre)
        dpool = _dot(dpre, wt_ref[...])
        gn = gn_ref[...]
        dpre_n = (dmn_ref[...] * sc * (gn * _sigmoid(gn))).astype(BF16)
        dpool_n = jnp.where(i == nt - 1, 0.0, _dot(dpre_n, wt_ref[...]))
        lane, wl = _pool_lane_windows()
        tpos_n = ((i + 1) * tt + lax.broadcasted_iota(jnp.int32, (POOL_HALO, POOL_W), 0)).astype(F32)
        ext = jnp.concatenate([dpool / cnt, dpool_n / jnp.minimum(tpos_n + 1.0, wl)], axis=0)
        rows = tt + POOL_HALO
        sums, cur, shift = [], ext, 1
        for _ in POOL_WINDOWS:
            cur = cur + pltpu.roll(cur, rows - shift, axis=0)
            sums.append(cur[:tt, :])
            shift *= 2
        du = _pool_select(lane, sums) - dpool
        dp_ref[...] = jnp.concatenate([du, dgb], axis=1).astype(BF16)

    def nxt(b, i):
        return jnp.minimum((b * nt + i + 1) * nhb, n // POOL_HALO - 1)

    return _pc(
        body, name, grid=(nb, nt),
        in_specs=_pool_specs(tt, nt, nhb) + [
            pl.BlockSpec((tt, POOL_W), lambda b, i: (b * nt + i, cm)),
            pl.BlockSpec((POOL_HALO, POOL_W), lambda b, i: (nxt(b, i), cg)),
            pl.BlockSpec((POOL_HALO, POOL_W), lambda b, i: (nxt(b, i), cm)),
            pl.BlockSpec((POOL_W, POOL_W), lambda b, i: (0, 0)), pl.BlockSpec((POOL_W, POOL_W), lambda b, i: (0, 0)),
            pl.BlockSpec((1, POOL_W), lambda b, i: (0, 0))],
        out_specs=[pl.BlockSpec((tt, 2 * POOL_W), lambda b, i: (b * nt + i, 0)),
                   pl.BlockSpec((POOL_W, POOL_W), lambda b, i: (0, 0)), pl.BlockSpec((1, POOL_W), lambda b, i: (0, 0))],
        out_shape=[_sds((n, 2 * POOL_W), BF16), _sds((POOL_W, POOL_W), F32), _sds((1, POOL_W), F32)],
        compiler_params=_params(),
    )(proj, proj, proj, dmix, proj, dmix, wbd, wbd_t, scale)


def _fox_bwd(proj, qt, kt, da, dat, c_col, c_row, lse_row, delta_row, nb, name, rider=None):
    n = proj.shape[0]
    t = n // nb
    tb = min(256, t)
    nq = t // tb
    pw = 2 * HEAD
    qw = 2 if nq % 2 == 0 else 1
    nqs = nq // qw

    def body(q_ref, k_ref, v_ref, da_ref, qt_ref, kt_ref, dat_ref, cc_ref, cr_ref, lse_ref, dl_ref,
             dq_ref, dk_ref, dv_ref, dck_ref, dcq_ref, dq_s, dk_s, dv_s, dck_s, dcq_s):
        step = pl.program_id(1)
        kj, qq = pairs(step)

        @pl.when(step == 0)
        def _():
            dq_s[...] = jnp.zeros_like(dq_s)
            dcq_s[...] = jnp.zeros_like(dcq_s)

        @pl.when(qq == nqs - 1)
        def _():
            dk_s[...] = jnp.zeros_like(dk_s)
            dv_s[...] = jnp.zeros_like(dv_s)
            dck_s[...] = jnp.zeros_like(dck_s)

        def block(masked, sub):
            lo = _lane_lo()
            qi = qq * qw + sub
            qs = slice(sub * tb, (sub + 1) * tb)
            if masked:
                causal = lax.broadcasted_iota(jnp.int32, (tb, tb), 1) >= lax.broadcasted_iota(jnp.int32, (tb, tb), 0)
            dck = dck_s[...]
            for p in range(FOX_HEADS // 2):
                sl = slice(p * pw, (p + 1) * pw)
                qp = q_ref[qs, sl] * FOX_SCALE
                kp = k_ref[:, sl].astype(BF16)
                vp = v_ref[:, sl].astype(BF16)
                dap = da_ref[qs, sl]
                dk, dv = dk_s[:, sl], dv_s[:, sl]
                for h in range(2):
                    hh = 2 * p + h
                    lm = lo if h == 0 else jnp.logical_not(lo)
                    rows = slice(hh * HEAD, (hh + 1) * HEAD)
                    none = jnp.zeros((HEAD, tb), BF16)
                    qm = jnp.where(lm, qp, 0.0).astype(BF16)
                    dam = jnp.where(lm, dap, jnp.zeros_like(dap))
                    qtm = jnp.concatenate([qt_ref[rows, qs], none] if h == 0 else [none, qt_ref[rows, qs]], axis=0)
                    datm = jnp.concatenate([dat_ref[rows, qs], none] if h == 0 else [none, dat_ref[rows, qs]], axis=0)
                    s = _dot(kp, qtm) + (cr_ref[hh:hh + 1, qs] - cc_ref[:, hh:hh + 1])
                    pe = jnp.exp(s - lse_ref[hh:hh + 1, qs])
                    if masked:
                        pe = jnp.where(causal, pe, 0.0)
                    dp = _dot(vp, datm)
                    ds = pe * (dp - dl_ref[hh:hh + 1, qs])
                    dsb = ds.astype(BF16)
                    dv = dv + _dot(pe.astype(BF16), dam)
                    dk = dk + _dot(dsb, qm)
                    dq_s[qi, rows, :] += _dot(kt_ref[rows, :], dsb)
                    dck = dck - _put_col(jnp.zeros_like(dck), hh, jnp.sum(ds, axis=1, keepdims=True))
                    dcq_s[qi, hh:hh + 1, :] += _rows_reduce(ds, jnp.add, jnp.sum)
                dk_s[:, sl] = dk
                dv_s[:, sl] = dv
            dck_s[...] = dck

        for sub in reversed(range(qw)):
            @pl.when(qq * qw + sub > kj)
            def _(sub=sub):
                block(False, sub)

            @pl.when(qq * qw + sub == kj)
            def _(sub=sub):
                block(True, sub)

        @pl.when(qq == kj // qw)
        def _():
            dk_ref[...] = dk_s[...].astype(BF16)
            dv_ref[...] = dv_s[...].astype(BF16)
            dck_ref[...] = dck_s[...]

        @pl.when(step == _tri_steps(nq, qw) - 1)
        def _():
            for j in range(nq):
                dq_ref[j * tb:(j + 1) * tb, :] = (dq_s[j].T * FOX_SCALE).astype(BF16)
                dcq_ref[:, j * tb:(j + 1) * tb] = dcq_s[j]

    def pairs(step):
        a, b = _tri_pair(step, nq, qw)
        return nq - 1 - a, nqs - 1 - b

    def kspec(wd, j=0):
        return pl.BlockSpec((tb, wd), lambda b, st: (b * nq + pairs(st)[0], j))

    def qspec(wd, j=0):
        return pl.BlockSpec((qw * tb, wd), lambda b, st: (b * nqs + pairs(st)[1], j))

    def qrow():
        return pl.BlockSpec((None, FOX_HEADS, qw * tb), lambda b, st: (b, 0, pairs(st)[1]))

    def tspec(which):
        if which == 0:
            return pl.BlockSpec((FOX_W, tb), lambda b, st: (0, b * nq + pairs(st)[0]))
        return pl.BlockSpec((FOX_W, qw * tb), lambda b, st: (0, b * nqs + pairs(st)[1]))

    return _call(
        body, name, (proj, proj, proj, da, qt, kt, dat, c_col, c_row, lse_row, delta_row), rider,
        grid=(nb, _tri_steps(nq, qw)),
        in_specs=[qspec(FOX_W, C_QC // FOX_W), kspec(FOX_W, C_KC // FOX_W), kspec(FOX_W, C_VC // FOX_W), qspec(FOX_W),
                  tspec(1), tspec(0), tspec(1), kspec(FC_PAD), qrow(), qrow(), qrow()],
        out_specs=[pl.BlockSpec((t, FOX_W), lambda b, st: (b, 0)), kspec(FOX_W), kspec(FOX_W), kspec(FC_PAD),
                   pl.BlockSpec((None, FOX_HEADS, t), lambda b, st: (b, 0, 0))],
        out_shape=[_sds((n, FOX_W), BF16), _sds((n, FOX_W), BF16), _sds((n, FOX_W), BF16), _sds((n, FC_PAD), F32),
                   _sds((nb, FOX_HEADS, t), F32)],
        scratch_shapes=[pltpu.VMEM((nq, FOX_W, tb), F32), pltpu.VMEM((tb, FOX_W), F32), pltpu.VMEM((tb, FOX_W), F32),
                        pltpu.VMEM((tb, FC_PAD), F32), pltpu.VMEM((nq, FOX_HEADS, tb), F32)],
        compiler_params=_params(),
    )


def _fox_decay_bwd(dc_q, dc_k, fc, bias, nb, name):
    n = fc.shape[0]
    t = n // nb
    tt = min(256, t)
    nt = t // tt

    def body(dcq_ref, dck_ref, fc_ref, b_ref, dfc_ref, db_ref):
        @pl.when(pl.program_id(0) == 0)
        def _():
            db_ref[...] = jnp.zeros_like(db_ref)

        r = lax.broadcasted_iota(jnp.int32, (tt, tt), 0)
        cc = lax.broadcasted_iota(jnp.int32, (tt, tt), 1)
        carry = jnp.zeros((1, FC_PAD), F32)
        db = jnp.zeros((1, FC_PAD), F32)
        for i in reversed(range(nt)):
            rows = slice(i * tt, (i + 1) * tt)
            dcq = jnp.concatenate([dcq_ref[:, rows], jnp.zeros((FC_PAD - FOX_HEADS, tt), F32)], axis=0).T
            dlf = _sel_dot_exact(r <= cc, dcq + dck_ref[rows, :]) + carry
            carry = dlf[0:1, :]
            dfc = dlf * _sigmoid(-(fc_ref[rows, :] + b_ref[...]))
            dfc_ref[rows, :] = dfc.astype(BF16)
            db = db + jnp.sum(dfc, axis=0, keepdims=True)
        db_ref[...] += db

    def row():
        return pl.BlockSpec((t, FC_PAD), lambda b: (b, 0))

    return _pc(
        body, name, grid=(nb,),
        in_specs=[pl.BlockSpec((None, FOX_HEADS, t), lambda b: (b, 0, 0)), row(), row(),
                  pl.BlockSpec((1, FC_PAD), lambda b: (0, 0))],
        out_specs=[row(), pl.BlockSpec((1, FC_PAD), lambda b: (0, 0))],
        out_shape=[_sds((n, FC_PAD), BF16), _sds((1, FC_PAD), F32)],
        compiler_params=_params(),
    )(dc_q, dc_k, fc, bias)


def _in_proj_bwd(pieces, w_main_t, w_fc_t, x, g_pre, dxo, name, rider=None):
    n, d = x.shape
    tm = min(512, n)
    widths = [p.shape[1] for p, _ in pieces]
    offs = [o for _, o in pieces]
    np_ = len(pieces)

    def body(*refs):
        p_refs = refs[:np_]
        wt_ref, wf_ref, x_ref, g_ref, dxo_ref, dx_ref, dg_ref = refs[np_:]

        @pl.when(pl.program_id(0) == 0)
        def _():
            dg_ref[...] = jnp.zeros_like(dg_ref)

        dh = _dot(p_refs[-1][...], wf_ref[...])
        for pr, wd, off in zip(p_refs[:-1], widths[:-1], offs[:-1]):
            for j in range(0, wd, 512):
                jw = min(512, wd - j)
                dh = dh + _dot(pr[:, j:j + jw], wt_ref[off + j:off + j + jw, :])
        xv = x_ref[...]
        r = lax.rsqrt(jnp.mean(xv * xv, axis=-1, keepdims=True) + NORM_EPS)
        xh = xv * r
        dg_ref[...] += jnp.sum(dh * xh, axis=0, keepdims=True)
        dx_ref[...] = dxo_ref[...] + _rms_bwd(dh * g_ref[...], xh, r)

    row = pl.BlockSpec((tm, d), lambda i: (i, 0))
    return _call(
        body, name, (*[p for p, _ in pieces], w_main_t, w_fc_t, x, g_pre, dxo), rider, grid=(n // tm,),
        in_specs=[pl.BlockSpec((tm, wd), lambda i: (i, 0)) for wd in widths] + [
            pl.BlockSpec((MAIN_W, d), lambda i: (0, 0)), pl.BlockSpec((FC_PAD, d), lambda i: (0, 0)),
            row, pl.BlockSpec((1, d), lambda i: (0, 0)), row],
        out_specs=[row, pl.BlockSpec((1, d), lambda i: (0, 0))],
        out_shape=[_sds((n, d), F32), _sds((1, d), F32)],
        compiler_params=_params(),
    )


def _lower_bound_table(lower_bounds, name):
    depth, w = lower_bounds.shape

    def body(lb_ref, o_ref):
        v = lb_ref[...]
        e = jnp.exp(v - jnp.max(v, axis=0, keepdims=True))
        p = e / jnp.sum(e, axis=0, keepdims=True)
        acc = jnp.zeros((1, w), F32)
        for l in range(depth):
            acc = acc + p[l:l + 1, :]
            o_ref[l:l + 1, :] = acc - p[0:1, :]

    return _pc(body, name, out_shape=_sds((depth, w), F32))(lower_bounds)


def _lower_bound_bwd(lower_bounds, dlbs, name):
    depth, w = lower_bounds.shape

    def body(lb_ref, d_ref, o_ref):
        v, dl = lb_ref[...], d_ref[...]
        e = jnp.exp(v - jnp.max(v, axis=0, keepdims=True))
        p = e / jnp.sum(e, axis=0, keepdims=True)
        tot = jnp.sum(dl, axis=0, keepdims=True)
        rows, tail = [], tot
        for l in range(depth):
            rows.append(tail - tot if l == 0 else tail)
            tail = tail - dl[l:l + 1, :]
        dp = jnp.concatenate(rows, axis=0)
        o_ref[...] = p * (dp - jnp.sum(p * dp, axis=0, keepdims=True))

    return _pc(body, name, out_shape=_sds((depth, w), F32))(lower_bounds, dlbs)


def _place():
    x, y, c = lax.axis_index("x"), lax.axis_index("y"), lax.axis_index("c")
    return x, y, c


def _gather_weights(*arrays):
    na = len(arrays)

    def body(*refs):
        ins, outs = refs[:na], refs[na:2 * na]
        send_sems, recv_sems, local_sems = refs[2 * na:]
        x, y, c = _place()
        me, sibling = (x, y, c), (x, y, 1 - c)
        chips = [(1 - x, y), (x, 1 - y), (1 - x, 1 - y)]

        def slot(a, px, py, pc):
            return outs[a].at[4 * px + 2 * py + pc]

        def copy(a, k, block, to, own=False):
            return pltpu.make_async_remote_copy(
                src_ref=ins[a] if own else slot(a, *block), dst_ref=slot(a, *block),
                send_sem=send_sems.at[a * 7 + k], recv_sem=recv_sems.at[a * 7 + k],
                device_id=to, device_id_type=MESH)

        mine = [pltpu.make_async_copy(ins[a], slot(a, *me), local_sems.at[a]) for a in range(na)]
        for cp in mine:
            cp.start()
        first = []
        for a in range(na):
            first.append(copy(a, 0, me, sibling, own=True))
            first += [copy(a, 1 + j, me, (*chip, c), own=True) for j, chip in enumerate(chips)]
        for cp in first:
            cp.start()
        passed = []
        for j, chip in enumerate(chips):
            for a in range(na):
                copy(a, 1 + j, (*chip, c), me).wait_recv()
                fw = copy(a, 4 + j, (*chip, c), sibling)
                fw.start()
                passed.append(fw)
        for a in range(na):
            copy(a, 0, sibling, me).wait_recv()
            for j, chip in enumerate(chips):
                copy(a, 4 + j, (*chip, 1 - c), me).wait_recv()
        for cp in first + passed:
            cp.wait_send()
        for cp in mine:
            cp.wait()

    any_spec = pl.BlockSpec(memory_space=pl.ANY)
    return _pc(
        body, "gather_weights",
        in_specs=[any_spec] * na, out_specs=[any_spec] * na,
        out_shape=[_sds((N_DEV,) + a.shape, a.dtype) for a in arrays],
        scratch_shapes=[pltpu.SemaphoreType.DMA((7 * na,)), pltpu.SemaphoreType.DMA((7 * na,)),
                        pltpu.SemaphoreType.DMA((na,))],
    )(*arrays)


def _peer(k):
    x, y, c = _place()
    return (1 - x if k & 4 else x, 1 - y if k & 2 else y, 1 - c if k & 1 else c)


def _remote(src, dst, sems, s, to):
    return pltpu.make_async_remote_copy(src_ref=src, dst_ref=dst, send_sem=sems[0].at[s], recv_sem=sems[1].at[s],
                                        device_id=to, device_id_type=MESH)


def _gather_rider(shards):
    na = len(shards)

    def plan(ins, outs, *sems):
        x, y, c = _place()
        me = 4 * x + 2 * y + c
        locs = [pltpu.make_async_copy(ins[a], outs[a].at[me], sems[2].at[a]) for a in range(na)]
        sends, recvs = [], []
        for k in range(1, N_DEV):
            px, py, pc = _peer(k)
            for a in range(na):
                s = (k - 1) * na + a
                sends.append(_remote(ins[a], outs[a].at[me], sems, s, (px, py, pc)))
                recvs.append(_remote(ins[a], outs[a].at[4 * px + 2 * py + pc], sems, s, (px, py, pc)))
        return sends, recvs, locs

    return _Rider(shards, [_sds((N_DEV,) + a.shape, a.dtype) for a in shards], (N_DEV - 1) * na, na, plan)


def _direct_exchange_rider(blocks):
    na = len(blocks)

    def plan(ins, outs, *sems):
        x, y, c = _place()
        me = 4 * x + 2 * y + c
        locs = [pltpu.make_async_copy(ins[a].at[c, 2 * x + y], outs[a].at[me], sems[2].at[a]) for a in range(na)]
        sends, recvs = [], []
        for k in range(1, N_DEV):
            px, py, pc = _peer(k)
            for a in range(na):
                s = (k - 1) * na + a
                sends.append(_remote(ins[a].at[pc, 2 * px + py], outs[a].at[me], sems, s, (px, py, pc)))
                recvs.append(_remote(ins[a].at[pc, 2 * px + py], outs[a].at[4 * px + 2 * py + pc], sems, s, (px, py, pc)))
        return sends, recvs, locs

    return _Rider(blocks, [_sds((N_DEV,) + a.shape[2:], a.dtype) for a in blocks], (N_DEV - 1) * na, na, plan)


def _swap_rider(halves):
    na = len(halves)

    def plan(ins, outs, *sems):
        x, y, c = _place()
        cps = [_remote(ins[a].at[1 - c], outs[a], sems, a, (x, y, 1 - c)) for a in range(na)]
        return cps, cps, []

    return _Rider(halves, [_sds(a.shape[1:], a.dtype) for a in halves], na, 1, plan)


def _chip_exchange_rider(parts, small=None):
    na = len(parts)
    n_chip = N_DEV // 2

    def plan(ins, outs, *sems):
        x, y, c = _place()
        chip = 2 * x + y
        locs = [pltpu.make_async_copy(ins[a].at[chip], outs[a].at[chip], sems[2].at[a]) for a in range(na)]
        sends, recvs = [], []
        for k in range(1, n_chip):
            px, py, _ = _peer(2 * k)
            for a in range(na):
                s = (k - 1) * na + a
                sends.append(_remote(ins[a].at[2 * px + py], outs[a].at[chip], sems, s, (px, py, c)))
                recvs.append(_remote(ins[a].at[2 * px + py], outs[a].at[2 * px + py], sems, s, (px, py, c)))
        if small is not None:
            me = 2 * chip + c
            locs.append(pltpu.make_async_copy(ins[na], outs[na].at[me], sems[2].at[na]))
            for k in range(1, N_DEV):
                px, py, pc = _peer(k)
                s = (n_chip - 1) * na + k - 1
                sends.append(_remote(ins[na], outs[na].at[me], sems, s, (px, py, pc)))
                recvs.append(_remote(ins[na], outs[na].at[4 * px + 2 * py + pc], sems, s, (px, py, pc)))
        return sends, recvs, locs

    extra = [] if small is None else [small]
    shapes = [_sds(a.shape, a.dtype) for a in parts] + [_sds((N_DEV,) + s.shape, s.dtype) for s in extra]
    n_sems = (n_chip - 1) * na + (N_DEV - 1) * len(extra)
    return _Rider(list(parts) + extra, shapes, n_sems, na + len(extra), plan)


def _pair_add(halves, other, core, name):
    _, nch, r, c = halves.shape

    def body(c_ref, h_ref, o_ref, p_ref):
        p_ref[...] = (h_ref[...].astype(F32) + o_ref[...].astype(F32)).astype(BF16)

    blk = pl.BlockSpec((None, r, c), lambda j, c_ref: (j, 0, 0))
    return _pc(
        body, name,
        grid_spec=pltpu.PrefetchScalarGridSpec(
            num_scalar_prefetch=1, grid=(nch,),
            in_specs=[pl.BlockSpec((None, None, r, c), lambda j, c_ref: (c_ref[0], j, 0, 0)), blk], out_specs=blk),
        out_shape=_sds((nch, r, c), BF16),
        compiler_params=_params(),
    )(core, halves, other)


def _sum_adamw(parts, w, m, v, name, rider=None):
    nl, r, c = w.shape
    tr = 256 if r % 256 == 0 else r

    def body(*refs):
        p_refs = refs[:nl]
        w_ref, m_ref, v_ref, g_ref, d_ref, mo_ref, vo_ref = refs[nl:]
        for l in range(nl):
            @pl.when(pl.program_id(0) == l)
            def _(p_ref=p_refs[l]):
                g = p_ref[0].astype(F32)
                for j in range(1, p_ref.shape[0]):
                    g = g + p_ref[j].astype(F32)
                mn = ADAM_B1 * m_ref[...] + (1.0 - ADAM_B1) * g
                vn = ADAM_B2 * v_ref[...] + (1.0 - ADAM_B2) * (g * g)
                m_hat = mn / (1.0 - ADAM_B1 ** ADAM_STEP)
                v_hat = vn / (1.0 - ADAM_B2 ** ADAM_STEP)
                g_ref[...] = g
                d_ref[...] = -ADAM_LR * (m_hat / (jnp.sqrt(v_hat) + ADAM_EPS) + ADAM_WD * w_ref[...])
                mo_ref[...] = mn
                vo_ref[...] = vn

    def part_spec(l, k):
        return pl.BlockSpec((k, tr, c), lambda li, i: (0, jnp.where(li == l, i, 0), 0))

    row = pl.BlockSpec((None, tr, c), lambda li, i: (li, i, 0))
    return _call(
        body, name, (*parts, w, m, v), rider, grid=(nl, r // tr),
        in_specs=[part_spec(l, p.shape[0]) for l, p in enumerate(parts)] + [row, row, row],
        out_specs=[row] * 4,
        out_shape=[_sds((nl, r, c), F32)] * 4,
        compiler_params=_params(),
    )


SMALL = ("lower_bounds", "pre_norm_g", "hgrn_norm_g", "pool_w", "pool_scale", "post_norm_g", "fox_f_bias")
SMALL_LANES = 128


def _small_size(tree):
    return sum(tree[k].size for k in SMALL)


def _pack_small(tree, extra=None):
    flat = jnp.concatenate([tree[k].reshape(-1) for k in SMALL] + ([] if extra is None else [extra.reshape(1)]))
    rows = -(-(_small_size(tree) + 1) // (8 * SMALL_LANES)) * 8
    return jnp.pad(flat, (0, rows * SMALL_LANES - flat.shape[0])).reshape(rows, SMALL_LANES)


def _unpack_small(packed, like):
    out, off = {}, 0
    for k in SMALL:
        size = like[k].size
        assert off % SMALL_LANES == 0
        rows = packed[off // SMALL_LANES:-(-(off + size) // SMALL_LANES)]
        out[k] = rows.reshape(-1)[:size].reshape(like[k].shape)
        off += size
    return out


def _block_diag(pw):
    g = pw.shape[0]
    eye = jnp.eye(g, dtype=pw.dtype)
    return (eye[:, None, :, None] * pw[:, :, None, :]).reshape(g * HEAD, g * HEAD)


def _assemble_w_in(g_in, name):
    _, d, shard = g_in.shape
    tr = min(256, d)
    wide = MAIN_W + FC_PAD

    def body(g_ref, wm_ref, wf_ref, wmt_ref, wft_ref, row_s):
        row_s[:, MAIN_W:] = jnp.zeros((tr, FC_PAD), F32)
        for j in range(N_DEV):
            row_s[:, j * shard:(j + 1) * shard] = g_ref[j].astype(F32)
        wm_ref[...] = row_s[:, :MAIN_W].astype(BF16)
        wf_ref[...] = row_s[:, MAIN_W:].astype(BF16)
        for j in range(0, MAIN_W, 512):
            wmt_ref[j:j + 512, :] = row_s[:, j:j + 512].T.astype(BF16)
        wft_ref[...] = row_s[:, MAIN_W:].T.astype(BF16)

    return _pc(
        body, name, grid=(d // tr,),
        in_specs=[pl.BlockSpec((N_DEV, tr, shard), lambda i: (0, i, 0))],
        out_specs=[pl.BlockSpec((tr, MAIN_W), lambda i: (i, 0)), pl.BlockSpec((tr, FC_PAD), lambda i: (i, 0)),
                   pl.BlockSpec((MAIN_W, tr), lambda i: (0, i)), pl.BlockSpec((FC_PAD, tr), lambda i: (0, i))],
        out_shape=[_sds((d, MAIN_W), BF16), _sds((d, FC_PAD), BF16), _sds((MAIN_W, d), BF16), _sds((FC_PAD, d), BF16)],
        scratch_shapes=[pltpu.VMEM((tr, wide), F32)],
        compiler_params=_params(),
    )(g_in)


def _w_out_parts(g_out):
    full_out = g_out.reshape(N_DEV * g_out.shape[1], g_out.shape[2])
    return full_out, full_out.T


def _layer_fwd(l, x, lbs, weights, lw, nb, rider_h=None, rider_c=None, target=None):
    n = x.shape[0]
    t = n // nb
    w_main, w_fc, _, _, w_out, _ = lw
    bias = jnp.pad(weights["fox_f_bias"][l:l + 1], ((0, 0), (0, FC_PAD - FOX_HEADS)))
    wbd = _block_diag(weights["pool_w"][l]).astype(BF16)
    proj, fc, ht, qt, kt = _in_proj_fwd(x, weights["pre_norm_g"][l:l + 1], w_main, w_fc, f"in_proj_fwd_{l}")
    c_col, c_row = _fox_decay_fwd(fc, bias, nb, f"fox_decay_fwd_{l}")
    (o_h, s0), rode_h = _hgrn_fwd(proj, lbs[l:l + 1], _block_ones(HGRN_W, BF16), nb, f"hgrn_fwd_{l}", rider_h)
    o_b = _pool_fwd(proj, wbd, weights["pool_scale"][l:l + 1], nb, f"pool_fwd_{l}")
    (o_c, lse), rode_c = _fox_fwd(proj, kt, c_col, c_row, nb, f"fox_fwd_{l}", rider_c)
    if w_out is None:
        lw = tuple(lw[:4]) + _w_out_parts(rode_h[0])
        w_out, rode_h = lw[4], rode_h[1:]
    x_next, mixt, y = _merge_fwd(x, proj, o_h, o_b, o_c, weights["hgrn_norm_g"][l:l + 1], w_out,
                                 weights["post_norm_g"][l:l + 1], f"merge_fwd_{l}", target)
    return x_next, (x, proj, fc, ht, qt, kt, c_col, c_row, o_h, s0, o_c, lse, mixt, y, bias, wbd), lw, (rode_h, rode_c)


def _layer_bwd(l, dx, saved, lbs, weights, lw, nb, rider=None, send_w_out=None):
    _, proj, fc, ht, qt, kt, c_col, c_row, o_h, s0, o_c, lse, mixt, y, bias, wbd = saved
    w_out_t = lw[5]
    g = {}
    dy, dmix, dgp, da, dat, d_gc, delta = _merge_bwd(dx, y, weights["post_norm_g"][l:l + 1], w_out_t, proj, o_c,
                                                nb, f"merge_bwd_{l}")
    g["post_norm_g"] = dgp[0]
    g["w_out"] = _w_out_grad(mixt, dy, f"w_out_grad_{l}")
    (d_a, dgh, dlb), arrived = _hgrn_bwd(dmix, proj, o_h, s0, weights["hgrn_norm_g"][l:l + 1], lbs[l:l + 1],
                                         _block_ones(HGRN_W, BF16), nb, f"hgrn_bwd_{l}",
                                         None if send_w_out is None else send_w_out([g["w_out"]]))
    if arrived is not None:
        g["w_out_received"] = arrived[0]
    g["hgrn_norm_g"], g["lbs"] = dgh[0], dlb[0]
    d_b, dwbd, dps = _pool_bwd(dmix, proj, wbd, wbd.T, weights["pool_scale"][l:l + 1], nb, f"pool_bwd_{l}")
    g["pool_w"] = jnp.stack([dwbd[j * HEAD:(j + 1) * HEAD, j * HEAD:(j + 1) * HEAD] for j in range(len(POOL_WINDOWS))])
    g["pool_scale"] = dps[0]
    (d_qc, d_kc, d_vc, dc_k, dc_q), rode = _fox_bwd(proj, qt, kt, da, dat, c_col, c_row, lse, delta, nb,
                                                    f"fox_bwd_{l}", rider)
    d_fc, dbias = _fox_decay_bwd(dc_q, dc_k, fc, bias, nb, f"fox_decay_bwd_{l}")
    g["fox_f_bias"] = dbias[0, :FOX_HEADS]
    pieces = [(d_a, C_QA), (d_b, C_UB), (d_qc, C_QC), (d_kc, C_KC), (d_vc, C_VC), (d_gc, C_GC), (d_fc, None)]
    g["w_in"] = _w_in_grad(ht, pieces, f"w_in_grad_{l}")
    return g, pieces, rode


def _layer_bwd_input(l, dx, pieces, saved, weights, lw, rider=None):
    (dxi, dgpre), rode = _in_proj_bwd(pieces, lw[2], lw[3], saved[0], weights["pre_norm_g"][l:l + 1], dx,
                                      f"in_proj_bwd_{l}", rider)
    return dxi, dgpre[0], rode


def kernel(x, lower_bounds, pre_norm_g, w_in, hgrn_norm_g, fox_f_bias, pool_w, pool_scale, w_out, post_norm_g, loss_target, m_lower_bounds, m_pre_norm_g, m_w_in, m_hgrn_norm_g, m_fox_f_bias, m_pool_w, m_pool_scale, m_w_out, m_post_norm_g, v_lower_bounds, v_pre_norm_g, v_w_in, v_hgrn_norm_g, v_fox_f_bias, v_pool_w, v_pool_scale, v_w_out, v_post_norm_g):
    weights = dict(lower_bounds=lower_bounds, pre_norm_g=pre_norm_g, hgrn_norm_g=hgrn_norm_g, fox_f_bias=fox_f_bias,
                   pool_w=pool_w, pool_scale=pool_scale, post_norm_g=post_norm_g)
    mom_m = dict(lower_bounds=m_lower_bounds, pre_norm_g=m_pre_norm_g, hgrn_norm_g=m_hgrn_norm_g, fox_f_bias=m_fox_f_bias,
                 pool_w=m_pool_w, pool_scale=m_pool_scale, post_norm_g=m_post_norm_g)
    mom_v = dict(lower_bounds=v_lower_bounds, pre_norm_g=v_pre_norm_g, hgrn_norm_g=v_hgrn_norm_g, fox_f_bias=v_fox_f_bias,
                 pool_w=v_pool_w, pool_scale=v_pool_scale, post_norm_g=v_post_norm_g)
    depth = w_in.shape[0]
    nb, t, d = x.shape
    n = nb * t
    core = lax.axis_index("c").astype(jnp.int32).reshape(1)
    shards = [(w_in[l].astype(BF16), w_out[l].astype(BF16)) for l in range(depth)]
    lbs = _lower_bound_table(lower_bounds, "lower_bound_table")

    (g_in,) = _gather_weights(shards[0][0])
    coming = tuple(_assemble_w_in(g_in, "assemble_w_in_0")) + (None, None)
    xl, saved, lw = x.reshape(n, d), [], []
    for l in range(depth):
        last = l + 1 == depth
        ride_h = ([shards[l][1]] if coming[4] is None else []) + ([] if last else [shards[l + 1][1]])
        xl, sv, lw_l, (rode_h, rode_c) = _layer_fwd(
            l, xl, lbs, weights, coming, nb, _gather_rider(ride_h) if ride_h else None,
            None if last else _gather_rider([shards[l + 1][0]]), loss_target.reshape(n, d) if last else None)
        saved.append(sv)
        lw.append(lw_l)
        if not last:
            coming = tuple(_assemble_w_in(rode_c[0], f"assemble_w_in_{l + 1}")) + _w_out_parts(rode_h[0])
    dx, sq = xl
    loss_here = 0.5 * jnp.sum(sq) / d

    grads, recv_in, pending = [None] * depth, [None] * depth, None
    for l in reversed(range(depth)):
        g, pieces, rode = _layer_bwd(l, dx, saved[l], lbs, weights, lw[l], nb, pending, _direct_exchange_rider)
        if rode is not None:
            recv_in[l + 1] = rode[0]
        if l > 0:
            pending = _direct_exchange_rider([g["w_in"]])
            dx, g["pre_norm_g"], _ = _layer_bwd_input(l, dx, pieces, saved[l], weights, lw[l])
        else:
            (other,) = _run_rider(_swap_rider([g["w_in"]]), "grad_swap")
            summed = _pair_add(g["w_in"], other, core, "grad_pair_add")
            dx, g["pre_norm_g"], (recv_in[l],) = _layer_bwd_input(l, dx, pieces, saved[l], weights, lw[l],
                                                                  _chip_exchange_rider([summed]))
        grads[l] = g
    small = {k: jnp.stack([grads[l][k] for l in range(depth)]) for k in SMALL if k != "lower_bounds"}
    small["lower_bounds"] = _lower_bound_bwd(lower_bounds, jnp.stack([grads[l]["lbs"] for l in range(depth)]),
                                             "lower_bound_bwd")
    (r_small,) = _run_rider(_gather_rider([_pack_small(small, loss_here)]), "small_grads_gather")

    res_in, _ = _sum_adamw(recv_in, w_in, m_w_in, v_w_in, "adamw_w_in")
    res_out, _ = _sum_adamw([grads[l]["w_out_received"] for l in range(depth)], w_out, m_w_out, v_w_out, "adamw_w_out")
    res_small, _ = _sum_adamw([r_small], _pack_small(weights)[None], _pack_small(mom_m)[None], _pack_small(mom_v)[None],
                              "adamw_small")
    loss = res_small[0][0][_small_size(weights) // SMALL_LANES, _small_size(weights) % SMALL_LANES]

    names = ("lower_bounds", "pre_norm_g", "w_in", "hgrn_norm_g", "fox_f_bias", "pool_w", "pool_scale", "w_out", "post_norm_g")
    outs = [loss, dx.reshape(nb, t, d)]
    for i in range(4):
        full = dict(_unpack_small(res_small[i][0], weights), w_in=res_in[i], w_out=res_out[i])
        outs += [full[k] for k in names]
    return tuple(outs)
```

```python
import functools

import jax
import jax.numpy as jnp
from jax import lax
from jax.experimental import pallas as pl
from jax.experimental.pallas import tpu as pltpu

F32, BF16 = jnp.float32, jnp.bfloat16
MESH = pl.DeviceIdType.MESH
N_DEV = 8

NORM_EPS = 1e-6
MASK_VALUE = -1e30
TINY = 1e-30
CHUNK = 64
SUB = 16
HGRN_W, POOL_W, FOX_W = 256, 256, 512
HEAD = 64
FOX_HEADS = 8
POOL_WINDOWS = (2, 4, 8, 16)
POOL_HALO = 16
MAIN_W = 3584
FC_PAD = 128
C_QA, C_FA, C_IA, C_GA, C_UB, C_GB, C_QC, C_KC, C_VC, C_GC = 0, 256, 512, 768, 1024, 1280, 1536, 2048, 2560, 3072
FOX_SCALE = HEAD ** -0.5

ADAM_LR, ADAM_B1, ADAM_B2, ADAM_EPS, ADAM_WD, ADAM_STEP = 0.001, 0.9, 0.999, 1e-08, 0.01, 10

VMEM_LIMIT = 56 * 1024 * 1024


def _pc(fn, name, **kw):
    return pl.pallas_call(fn, name=name, **kw)


def _params(**kw):
    return pltpu.CompilerParams(vmem_limit_bytes=VMEM_LIMIT, **kw)


class _Rider:
    def __init__(self, inputs, out_shapes, n_sems, n_local, plan):
        self.inputs, self.out_shapes, self.n_sems, self.n_local, self.plan = list(inputs), list(out_shapes), n_sems, n_local, plan

    def start(self, ins, outs, *sems):
        sends, _, locs = self.plan(ins, outs, *sems)
        for cp in locs + sends:
            cp.start()

    def wait(self, ins, outs, *sems):
        sends, recvs, locs = self.plan(ins, outs, *sems)
        for cp in recvs:
            cp.wait_recv()
        for cp in sends:
            cp.wait_send()
        for cp in locs:
            cp.wait()

    def sem_shapes(self):
        return [pltpu.SemaphoreType.DMA((self.n_sems,)), pltpu.SemaphoreType.DMA((self.n_sems,)),
                pltpu.SemaphoreType.DMA((self.n_local,))]


def _call(body, name, args, rider=None, *, grid, in_specs, out_specs, out_shape, scratch_shapes=(), **kw):
    if rider is None:
        res = _pc(body, name, grid=grid, in_specs=in_specs, out_specs=out_specs, out_shape=out_shape,
                  scratch_shapes=list(scratch_shapes), **kw)(*args)
        return res, None
    n_in, n_out, n_scr = len(in_specs), len(out_specs), len(scratch_shapes)
    n_rin, n_rout = len(rider.inputs), len(rider.out_shapes)

    def ridden(*refs):
        ins, refs = refs[:n_in], refs[n_in:]
        rins, refs = refs[:n_rin], refs[n_rin:]
        outs, refs = refs[:n_out], refs[n_out:]
        routs, refs = refs[:n_rout], refs[n_rout:]
        scr, sems = refs[:n_scr], refs[n_scr:]
        first = functools.reduce(jnp.logical_and, [pl.program_id(a) == 0 for a in range(len(grid))])
        last = functools.reduce(jnp.logical_and, [pl.program_id(a) == g - 1 for a, g in enumerate(grid)])

        @pl.when(first)
        def _():
            rider.start(rins, routs, *sems)

        body(*ins, *outs, *scr)

        @pl.when(last)
        def _():
            rider.wait(rins, routs, *sems)

    any_spec = pl.BlockSpec(memory_space=pl.ANY)
    res = _pc(ridden, name, grid=grid, in_specs=list(in_specs) + [any_spec] * n_rin,
              out_specs=list(out_specs) + [any_spec] * n_rout, out_shape=list(out_shape) + rider.out_shapes,
              scratch_shapes=list(scratch_shapes) + rider.sem_shapes(), **kw)(*args, *rider.inputs)
    return res[:n_out], res[n_out:]


def _run_rider(rider, name):
    n_rin = len(rider.inputs)

    def body(*refs):
        ins, outs, sems = refs[:n_rin], refs[n_rin:n_rin + len(rider.out_shapes)], refs[n_rin + len(rider.out_shapes):]
        rider.start(ins, outs, *sems)
        rider.wait(ins, outs, *sems)

    any_spec = pl.BlockSpec(memory_space=pl.ANY)
    return _pc(body, name, in_specs=[any_spec] * n_rin, out_specs=[any_spec] * len(rider.out_shapes),
               out_shape=rider.out_shapes, scratch_shapes=rider.sem_shapes())(*rider.inputs)


def _dot(a, b):
    return jnp.dot(a, b, preferred_element_type=F32)


def _dot_nt(a, b):
    return lax.dot_general(a, b, (((1,), (1,)), ((), ())), preferred_element_type=F32)


def _dot_tn(a, b):
    return lax.dot_general(a, b, (((0,), (0,)), ((), ())), preferred_element_type=F32)


def _sel_dot_exact(sel, x):
    hi = x.astype(BF16)
    rest = x - hi.astype(F32)
    mid = rest.astype(BF16)
    lo = (rest - mid.astype(F32)).astype(BF16)
    sb = sel.astype(BF16)
    return _dot(sb, hi) + _dot(sb, mid) + _dot(sb, lo)


def _split2(x):
    hi = x.astype(BF16)
    return hi, (x - hi.astype(F32)).astype(BF16)


def _sel_dot(sel, x):
    hi, lo = _split2(x)
    sb = sel.astype(BF16)
    return _dot(sb, hi) + _dot(sb, lo)


def _dot_sel(x, sel):
    hi, lo = _split2(x)
    sb = sel.astype(BF16)
    return _dot(hi, sb) + _dot(lo, sb)


def _sigmoid(x):
    return 1.0 / (1.0 + jnp.exp(-x))


def _block_ones(n, dtype):
    r = lax.broadcasted_iota(jnp.int32, (n, n), 0) // HEAD
    c = lax.broadcasted_iota(jnp.int32, (n, n), 1) // HEAD
    return (r == c).astype(dtype)


def _sds(shape, dtype):
    return jax.ShapeDtypeStruct(shape, dtype)


def _in_proj_fwd(x, g_pre, w_main, w_fc, name):
    n, d = x.shape
    tm = min(512, n)

    def body(x_ref, g_ref, w_ref, wf_ref, proj_ref, fc_ref, ht_ref, qt_ref, kt_ref):
        xv = x_ref[...]
        r = lax.rsqrt(jnp.mean(xv * xv, axis=-1, keepdims=True) + NORM_EPS)
        hf = xv * r * g_ref[...]
        hb = hf.astype(BF16)
        ht_ref[...] = hf.T.astype(BF16)
        for j in range(0, MAIN_W, FOX_W):
            res = _dot(hb, w_ref[:, j:j + FOX_W])
            proj_ref[:, j:j + FOX_W] = res
            if j == C_QC:
                qt_ref[...] = (res * FOX_SCALE).T.astype(BF16)
            if j == C_KC:
                kt_ref[...] = res.T.astype(BF16)
        fc_ref[...] = _dot(hb, wf_ref[...])

    def cols(rows):
        return pl.BlockSpec((rows, tm), lambda i: (0, i))

    return _pc(
        body, name, grid=(n // tm,),
        in_specs=[pl.BlockSpec((tm, d), lambda i: (i, 0)), pl.BlockSpec((1, d), lambda i: (0, 0)),
                  pl.BlockSpec((d, MAIN_W), lambda i: (0, 0)), pl.BlockSpec((d, FC_PAD), lambda i: (0, 0))],
        out_specs=[pl.BlockSpec((tm, MAIN_W), lambda i: (i, 0)), pl.BlockSpec((tm, FC_PAD), lambda i: (i, 0)),
                   cols(d), cols(FOX_W), cols(FOX_W)],
        out_shape=[_sds((n, MAIN_W), F32), _sds((n, FC_PAD), F32), _sds((d, n), BF16), _sds((FOX_W, n), BF16),
                   _sds((FOX_W, n), BF16)],
        compiler_params=_params(),
    )(x, g_pre, w_main, w_fc)


def _fox_decay_fwd(fc, bias, nb, name):
    n = fc.shape[0]
    t = n // nb
    tt = min(256, t)
    nt = t // tt

    def body(fc_ref, b_ref, c_ref, cr_ref):
        r = lax.broadcasted_iota(jnp.int32, (tt, tt), 0)
        cc = lax.broadcasted_iota(jnp.int32, (tt, tt), 1)
        carry = jnp.zeros((1, FC_PAD), F32)
        for i in range(nt):
            rows = slice(i * tt, (i + 1) * tt)
            xv = fc_ref[rows, :] + b_ref[...]
            lf = jnp.minimum(xv, 0.0) - jnp.log(1.0 + jnp.exp(-jnp.abs(xv)))
            cs = _sel_dot_exact(r >= cc, lf) + carry
            c_ref[rows, :] = cs
            cr_ref[:, rows] = cs.T[:FOX_HEADS, :]
            carry = cs[tt - 1:tt, :]

    return _pc(
        body, name, grid=(nb,),
        in_specs=[pl.BlockSpec((t, FC_PAD), lambda b: (b, 0)), pl.BlockSpec((1, FC_PAD), lambda b: (0, 0))],
        out_specs=[pl.BlockSpec((t, FC_PAD), lambda b: (b, 0)), pl.BlockSpec((None, FOX_HEADS, t), lambda b: (b, 0, 0))],
        out_shape=[_sds((n, FC_PAD), F32), _sds((nb, FOX_HEADS, t), F32)],
        compiler_params=_params(),
    )(fc, bias)


def _hgrn_gates(q, z, lb):
    sig = _sigmoid(z)
    sn = _sigmoid(-z)
    f = lb + (1.0 - lb) * sig
    g = jnp.log(jnp.maximum(f, TINY))
    k = (1.0 - lb) * sn
    sq = _sigmoid(q)
    return sig, sn, f, g, k, sq


def _sub_tri(n, lower):
    r = lax.broadcasted_iota(jnp.int32, (n, n), 0)
    c = lax.broadcasted_iota(jnp.int32, (n, n), 1)
    tri = (r >= c) if lower else (r <= c)
    return jnp.logical_and(r // SUB == c // SUB, tri).astype(F32)


def _live_rows(t):
    return 8 * (t // 8 + 1)


def _pad_rows(x):
    return x if x.shape[0] == SUB else jnp.concatenate([x, jnp.zeros((SUB - x.shape[0], x.shape[1]), x.dtype)], axis=0)


def _hgrn_decays(qs, k, b):
    srow = lax.broadcasted_iota(jnp.int32, (SUB, HGRN_W), 0)
    es, ws = [], []
    for t in range(SUB):
        r = _live_rows(t)
        e = jnp.where(srow[:r] <= t, jnp.exp(b[t:t + 1, :] - b[:r]), 0.0)
        es.append(e)
        ws.append(_pad_rows(e * (qs[t:t + 1, :] * k[:r])))
    return srow, es, ws


def _hgrn_state_step(st, k, v, b, bmask):
    bl = b[SUB - 1:SUB, :]
    ktil = k * jnp.exp(bl - b)
    return st * jnp.exp(bl) + _dot_tn(v.astype(BF16), ktil.astype(BF16)) * bmask


def _hgrn_tile(t):
    return min(256, t)


def _hgrn_fwd(proj, lb, ones_b, nb, name, rider=None):
    n = proj.shape[0]
    t = n // nb
    tt = _hgrn_tile(t)
    nt = t // tt
    ncs = tt // CHUNK
    w = HGRN_W

    def body(q_ref, z_ref, v_ref, lb_ref, ones_ref, o_ref, s0_ref, st_s, b_s, qs_s, k_s):
        @pl.when(pl.program_id(0) == 0)
        def _():
            st_s[...] = jnp.zeros_like(st_s)

        for e in range(nb):
            q = q_ref[e]
            _, _, _, g, k, sq = _hgrn_gates(q, z_ref[e], lb_ref[...])
            b_s[e] = _sel_dot(_sub_tri(tt, True), g)
            qs_s[e] = q * sq
            k_s[e] = k
        bmask = _block_ones(w, F32)
        ones_b = ones_ref[...]
        nsub = CHUNK // SUB

        def chunk(c, carry):
            sts = [st_s[e] for e in range(nb)]
            for e in range(nb):
                s0_ref[e, c] = sts[e]
            base = pl.multiple_of(c * CHUNK, CHUNK)
            tiles = [[(qs_s[e, pl.ds(base + u * SUB, SUB), :], k_s[e, pl.ds(base + u * SUB, SUB), :],
                       v_ref[e, pl.ds(base + u * SUB, SUB), :], b_s[e, pl.ds(base + u * SUB, SUB), :])
                      for u in range(nsub)] for e in range(nb)]
            aexps = [[] for _ in range(nb)]
            for e in range(nb):
                for qs, k, v, b in tiles[e]:
                    _, _, ws = _hgrn_decays(qs, k, b)
                    aexps[e].append(_dot(jnp.concatenate(ws, axis=0).astype(BF16), ones_b))
            inters = [[] for _ in range(nb)]
            for u in range(nsub):
                for e in range(nb):
                    qs, k, v, b = tiles[e][u]
                    inters[e].append(_dot_nt((qs * jnp.exp(b)).astype(BF16), sts[e].astype(BF16)))
                    sts[e] = _hgrn_state_step(sts[e], k, v, b, bmask)
            for e in range(nb):
                st_s[e] = sts[e]
            for e in range(nb):
                for u, ((qs, k, v, b), aexp, o) in enumerate(zip(tiles[e], aexps[e], inters[e])):
                    for t in range(SUB):
                        r = _live_rows(t)
                        row = o[t:t + 1, :] + jnp.sum(aexp[t * SUB:t * SUB + r, :] * v[:r], axis=0, keepdims=True)
                        o_ref[e, pl.ds(base + u * SUB + t, 1), :] = row
            return carry

        lax.fori_loop(0, ncs, chunk, 0)

    def col(j):
        return pl.BlockSpec((nb, tt, w), lambda i: (0, i, j))

    proj3 = proj.reshape(nb, t, proj.shape[1])
    (o, s0), rode = _call(
        body, name, (proj3, proj3, proj3, lb, ones_b), rider, grid=(nt,),
        in_specs=[col(C_QA // w), col(C_FA // w), col(C_IA // w), pl.BlockSpec((1, w), lambda i: (0, 0)),
                  pl.BlockSpec((w, w), lambda i: (0, 0))],
        out_specs=[pl.BlockSpec((nb, tt, w), lambda i: (0, i, 0)),
                   pl.BlockSpec((nb, ncs, w, w), lambda i: (0, i, 0, 0))],
        out_shape=[_sds((nb, t, w), F32), _sds((nb, t // CHUNK, w, w), F32)],
        scratch_shapes=[pltpu.VMEM((nb, w, w), F32)] + [pltpu.VMEM((nb, tt, w), F32)] * 3,
        compiler_params=_params(),
    )
    return (o.reshape(n, w), s0.reshape(n // CHUNK, w, w)), rode


def _pool_lane_windows():
    lane = lax.broadcasted_iota(jnp.int32, (1, POOL_W), 1) // HEAD
    wl = jnp.zeros((1, POOL_W), F32)
    for gi, win in enumerate(POOL_WINDOWS):
        wl = jnp.where(lane == gi, float(win), wl)
    return lane, wl


def _pool_select(lane, parts):
    out = parts[-1]
    for gi in range(len(parts) - 2, -1, -1):
        out = jnp.where(lane == gi, parts[gi], out)
    return out


def _pool_mix(u, halo, t0, tt):
    lane, wl = _pool_lane_windows()
    ext = jnp.concatenate([halo, u], axis=0)
    sums, cur, shift = [], ext, 1
    for _ in POOL_WINDOWS:
        cur = cur + pltpu.roll(cur, shift, axis=0)
        sums.append(cur[POOL_HALO:, :])
        shift *= 2
    tpos = (t0 + lax.broadcasted_iota(jnp.int32, (tt, POOL_W), 0)).astype(F32)
    cnt = jnp.minimum(tpos + 1.0, wl)
    return _pool_select(lane, sums) / cnt - u, cnt


def _pool_specs(tt, nt, nhb):
    cu, cg = C_UB // POOL_W, C_GB // POOL_W
    return [pl.BlockSpec((tt, POOL_W), lambda b, i: (b * nt + i, cu)),
            pl.BlockSpec((tt, POOL_W), lambda b, i: (b * nt + i, cg)),
            pl.BlockSpec((POOL_HALO, POOL_W), lambda b, i: (jnp.maximum((b * nt + i) * nhb - 1, 0), cu))]


def _pool_fwd(proj, wbd, scale, nb, name):
    n = proj.shape[0]
    t = n // nb
    tt = min(512, t)
    nt = t // tt
    nhb = tt // POOL_HALO

    def body(u_ref, g_ref, h_ref, w_ref, s_ref, o_ref):
        i = pl.program_id(1)
        halo = jnp.where(i == 0, 0.0, h_ref[...])
        pooled, _ = _pool_mix(u_ref[...], halo, i * tt, tt)
        gv = g_ref[...]
        o_ref[...] = _dot(pooled.astype(BF16), w_ref[...]) * s_ref[...] * (gv * _sigmoid(gv))

    return _pc(
        body, name, grid=(nb, nt),
        in_specs=_pool_specs(tt, nt, nhb) + [pl.BlockSpec((POOL_W, POOL_W), lambda b, i: (0, 0)),
                                             pl.BlockSpec((1, POOL_W), lambda b, i: (0, 0))],
        out_specs=pl.BlockSpec((tt, POOL_W), lambda b, i: (b * nt + i, 0)),
        out_shape=_sds((n, POOL_W), F32),
        compiler_params=_params(),
    )(proj, proj, proj, wbd, scale)


def _rows_reduce(x, op, final):
    while x.shape[0] > 8 and x.shape[0] % 16 == 0:
        half = x.shape[0] // 2
        x = op(x[:half], x[half:])
    return final(x, axis=0, keepdims=True)


def _tri_pair(step, n, group=1):
    counts = [a // group + 1 for a in range(n)]
    firsts = [sum(counts[:a]) for a in range(1, n)]
    a = sum([(step >= f).astype(jnp.int32) for f in firsts], jnp.int32(0))
    first = sum([jnp.where(step >= f, c, 0) for f, c in zip(firsts, counts)], jnp.int32(0))
    return a, step - first


def _tri_steps(n, group=1):
    return sum(a // group + 1 for a in range(n))


def _lane_lo():
    return lax.broadcasted_iota(jnp.int32, (1, 2 * HEAD), 1) < HEAD


def _put_col(tile, hh, colv):
    lane = lax.broadcasted_iota(jnp.int32, tile.shape, 1)
    return jnp.where(lane == hh, colv, tile)


def _fox_fwd(proj, kt, c_col, c_row, nb, name, rider=None):
    n = proj.shape[0]
    t = n // nb
    tb = min(256, t)
    nq = t // tb
    pw = 2 * HEAD
    kw = 2 if nq % 2 == 0 else 1
    nk = nq // kw

    def body(q_ref, kt_ref, v_ref, cc_ref, cr_ref, o_ref, lse_ref, m_s, acc_s, cq_s):
        qi, kk = _tri_pair(pl.program_id(1), nq, kw)

        @pl.when(kk == 0)
        def _():
            m_s[...] = jnp.full_like(m_s, -jnp.inf)
            acc_s[...] = jnp.zeros_like(acc_s)
            for hh in range(FOX_HEADS):
                cq_s[hh] = jnp.broadcast_to(cc_ref[:, hh:hh + 1], (tb, pw))

        def block(masked, sub):
            lo = _lane_lo()
            keys = slice(sub * tb, (sub + 1) * tb)
            if masked:
                causal = lax.broadcasted_iota(jnp.int32, (tb, tb), 0) >= lax.broadcasted_iota(jnp.int32, (tb, tb), 1)
            def lanes(hh):
                return lo if hh % 2 == 0 else jnp.logical_not(lo)

            def scores(hh):
                sl = slice((hh // 2) * pw, (hh // 2 + 1) * pw)
                return _dot(jnp.where(lanes(hh), q_ref[:, sl] * FOX_SCALE, 0.0).astype(BF16), kt_ref[sl, keys])

            ahead = scores(0)
            for hh in range(FOX_HEADS):
                s = ahead
                if hh + 1 < FOX_HEADS:
                    ahead = scores(hh + 1)
                s = s + (jnp.tile(cq_s[hh], (1, tb // pw)) - cr_ref[hh:hh + 1, keys])
                if masked:
                    s = jnp.where(causal, s, MASK_VALUE)
                m_prev = m_s[hh]
                m_new = jnp.maximum(m_prev, jnp.max(s, axis=1, keepdims=True))
                alpha = jnp.exp(m_prev - m_new)
                pe = jnp.exp(s - jnp.tile(m_new, (1, tb // pw)))
                m_s[hh] = m_new
                vf = v_ref[keys, (hh // 2) * pw:(hh // 2 + 1) * pw]
                acc_s[hh] = alpha * acc_s[hh] + _dot(pe.astype(BF16), jnp.where(lanes(hh), vf, 1.0).astype(BF16))

        def finish():
            lo = _lane_lo()
            m_all, l_all = jnp.zeros((tb, FC_PAD), F32), jnp.ones((tb, FC_PAD), F32)
            for p in range(FOX_HEADS // 2):
                a0, a1 = acc_s[2 * p], acc_s[2 * p + 1]
                both = pltpu.roll(jnp.where(lo, a1, a0), HEAD, axis=1)
                o_ref[:, p * pw:(p + 1) * pw] = jnp.where(lo, a0, a1) / both
                m_all = _put_col(_put_col(m_all, 2 * p, m_s[2 * p]), 2 * p + 1, m_s[2 * p + 1])
                l_all = _put_col(_put_col(l_all, 2 * p, both), 2 * p + 1, a1)
            lse_ref[...] = (m_all + jnp.log(l_all)).T[:FOX_HEADS, :]

        for sub in range(kw):
            @pl.when(kk * kw + sub < qi)
            def _(sub=sub):
                block(False, sub)

            @pl.when(kk * kw + sub == qi)
            def _(sub=sub):
                block(True, sub)
                finish()

    def qspec(wd, j):
        return pl.BlockSpec((tb, wd), lambda b, st: (b * nq + _tri_pair(st, nq, kw)[0], j))

    def kspec(j):
        return pl.BlockSpec((kw * tb, FOX_W), lambda b, st: (b * nk + _tri_pair(st, nq, kw)[1], j))

    return _call(
        body, name, (proj, kt, proj, c_col, c_row), rider, grid=(nb, _tri_steps(nq, kw)),
        in_specs=[qspec(FOX_W, C_QC // FOX_W),
                  pl.BlockSpec((FOX_W, kw * tb), lambda b, st: (0, b * nk + _tri_pair(st, nq, kw)[1])),
                  kspec(C_VC // FOX_W), qspec(FC_PAD, 0),
                  pl.BlockSpec((None, FOX_HEADS, kw * tb), lambda b, st: (b, 0, _tri_pair(st, nq, kw)[1]))],
        out_specs=[qspec(FOX_W, 0),
                   pl.BlockSpec((None, FOX_HEADS, tb), lambda b, st: (b, 0, _tri_pair(st, nq, kw)[0]))],
        out_shape=[_sds((n, FOX_W), F32), _sds((nb, FOX_HEADS, t), F32)],
        scratch_shapes=[pltpu.VMEM((FOX_HEADS, tb, pw), F32), pltpu.VMEM((FOX_HEADS, tb, pw), F32),
                        pltpu.VMEM((FOX_HEADS, tb, pw), F32)],
        compiler_params=_params(),
    )


def _head_mean(x, ones_f):
    return _dot_sel(x, ones_f) * (1.0 / HEAD)


def _merge_fwd(x, proj, o_h, o_b, o_c, gh, w_out, g_post, name, target=None):
    n, d = x.shape
    tm = min(512, n)

    def body(*refs):
        x_ref, ga_ref, gc_ref, oh_ref, ob_ref, oc_ref, gh_ref, w_ref, gp_ref = refs[:9]
        if target is None:
            xo_ref, mixt_ref, y_ref = refs[9:]
        else:
            t_ref, dx_ref, sq_ref, mixt_ref, y_ref = refs[9:]
        oh = oh_ref[...]
        ones_f = _block_ones(HGRN_W, F32)
        na = oh * lax.rsqrt(_head_mean(oh * oh, ones_f) + NORM_EPS) * gh_ref[...]
        ga, gc = ga_ref[...], gc_ref[...]
        mixed = jnp.concatenate([na * (ga * _sigmoid(ga)), ob_ref[...], oc_ref[...] * (gc * _sigmoid(gc))], axis=1)
        mixt_ref[...] = mixed.T.astype(BF16)
        y = _dot(mixed.astype(BF16), w_ref[...])
        y_ref[...] = y
        xn = x_ref[...] + y * lax.rsqrt(jnp.mean(y * y, axis=-1, keepdims=True) + NORM_EPS) * gp_ref[...]
        if target is None:
            xo_ref[...] = xn
        else:
            @pl.when(pl.program_id(0) == 0)
            def _():
                sq_ref[...] = jnp.zeros_like(sq_ref)

            e = xn - t_ref[...]
            dx_ref[...] = e * (1.0 / d)
            sq_ref[...] += jnp.sum(e * e, axis=0, keepdims=True)

    def row(wd, j=0):
        return pl.BlockSpec((tm, wd), lambda i: (i, j))

    def full(a, b):
        return pl.BlockSpec((a, b), lambda i: (0, 0))

    head = [] if target is None else [target]
    res = _pc(
        body, name, grid=(n // tm,),
        in_specs=[row(d), row(HGRN_W, C_GA // HGRN_W), row(FOX_W, C_GC // FOX_W), row(HGRN_W), row(POOL_W), row(FOX_W),
                  full(1, HGRN_W), full(d, d), full(1, d)] + [row(d)] * len(head),
        out_specs=[row(d)] + [full(1, d)] * len(head) + [pl.BlockSpec((d, tm), lambda i: (0, i)), row(d)],
        out_shape=[_sds((n, d), F32)] + [_sds((1, d), F32)] * len(head) + [_sds((d, n), BF16), _sds((n, d), F32)],
        compiler_params=_params(),
    )(x, proj, proj, o_h, o_b, o_c, gh, w_out, g_post, *head)
    return (res[0], res[1], res[2]) if target is None else ((res[0], res[1]), res[2], res[3])


def _rms_bwd(dy_scaled, xhat, r):
    return r * (dy_scaled - xhat * jnp.mean(dy_scaled * xhat, axis=-1, keepdims=True))


def _merge_bwd(dxo, y, g_post, w_out_t, proj, o_c, nb, name):
    n, d = y.shape
    t = n // nb
    tm = min(512, t)
    nt = t // tm
    wab = HGRN_W + POOL_W

    def body(dx_ref, y_ref, gp_ref, wt_ref, gc_ref, oc_ref, dy_ref, dm_ref, dgp_ref, da_ref, dat_ref, dg_ref, dl_ref):
        @pl.when(pl.program_id(0) == 0)
        def _():
            dgp_ref[...] = jnp.zeros_like(dgp_ref)

        yv, dxv = y_ref[...], dx_ref[...]
        r = lax.rsqrt(jnp.mean(yv * yv, axis=-1, keepdims=True) + NORM_EPS)
        yh = yv * r
        dgp_ref[...] += jnp.sum(dxv * yh, axis=0, keepdims=True)
        dyb = _rms_bwd(dxv * gp_ref[...], yh, r).astype(BF16)
        dy_ref[...] = dyb
        dm_ref[...] = _dot(dyb, wt_ref[:, :wab])
        dmc = _dot(dyb, wt_ref[:, wab:])
        gc, oc = gc_ref[...], oc_ref[...]
        sg = _sigmoid(gc)
        da = dmc * (gc * sg)
        da_ref[...] = da.astype(BF16)
        dat_ref[...] = da.T.astype(BF16)
        dg_ref[...] = (dmc * oc * (sg * (1.0 + gc * (1.0 - sg)))).astype(BF16)
        rr = lax.broadcasted_iota(jnp.int32, (FOX_W, FC_PAD), 0) // HEAD
        cc = lax.broadcasted_iota(jnp.int32, (FOX_W, FC_PAD), 1)
        dl_ref[...] = _dot_sel(da * oc, (rr == cc).astype(F32)).T[:FOX_HEADS, :]

    def row(wd, j=0):
        return pl.BlockSpec((tm, wd), lambda i: (i, j))

    def full(a, b):
        return pl.BlockSpec((a, b), lambda i: (0, 0))

    return _pc(
        body, name, grid=(n // tm,),
        in_specs=[row(d), row(d), full(1, d), full(d, d), row(FOX_W, C_GC // FOX_W), row(FOX_W)],
        out_specs=[row(d), row(wab), full(1, d), row(FOX_W), pl.BlockSpec((FOX_W, tm), lambda i: (0, i)), row(FOX_W),
                   pl.BlockSpec((None, FOX_HEADS, tm), lambda i: (i // nt, 0, i % nt))],
        out_shape=[_sds((n, d), BF16), _sds((n, wab), F32), _sds((1, d), F32), _sds((n, FOX_W), BF16),
                   _sds((FOX_W, n), BF16), _sds((n, FOX_W), BF16), _sds((nb, FOX_HEADS, t), F32)],
        compiler_params=_params(),
    )(dxo, y, g_post, w_out_t, proj, o_c)


def _w_out_grad(mixt, dy, name):
    d, n = mixt.shape
    rows = d // N_DEV

    def body(a_ref, b_ref, o_ref):
        o_ref[...] = _dot(a_ref[...], b_ref[...]).astype(BF16)

    return _pc(
        body, name, grid=(N_DEV,),
        in_specs=[pl.BlockSpec((rows, n), lambda j: (j, 0)), pl.BlockSpec((n, d), lambda j: (0, 0))],
        out_specs=pl.BlockSpec((None, None, rows, d), lambda j: (j % 2, j // 2, 0, 0)),
        out_shape=_sds((2, N_DEV // 2, rows, d), BF16),
        compiler_params=_params(),
    )(mixt, dy)


def _w_in_grad(ht, pieces, name):
    d, n = ht.shape
    ta, tk = min(512, d), min(512, n)
    nk = n // tk
    arrays = [p for p, _ in pieces]
    widths = [p.shape[1] for p in arrays]
    offs = [sum(widths[:i]) for i in range(len(widths))]
    in_w = MAIN_W + FOX_HEADS
    shard = in_w // N_DEV

    def body(*refs):
        a_ref, p_refs = refs[0], refs[1:1 + len(arrays)]
        o_ref, acc = refs[1 + len(arrays):]
        k = pl.program_id(1)

        @pl.when(k == 0)
        def _():
            acc[...] = jnp.zeros_like(acc)

        a = a_ref[...]
        for pr, off, wd in zip(p_refs, offs, widths):
            for j in range(0, wd, 512):
                jw = min(512, wd - j)
                acc[:, off + j:off + j + jw] += _dot(a, pr[:, j:j + jw])

        @pl.when(k == nk - 1)
        def _():
            for j in range(N_DEV):
                o_ref[j % 2, j // 2] = acc[:, j * shard:(j + 1) * shard].astype(BF16)

    return _pc(
        body, name, grid=(d // ta, nk),
        in_specs=[pl.BlockSpec((ta, tk), lambda i, k: (i, k))] + [pl.BlockSpec((tk, wd), lambda i, k: (k, 0)) for wd in widths],
        out_specs=pl.BlockSpec((2, N_DEV // 2, ta, shard), lambda i, k: (0, 0, i, 0)),
        out_shape=_sds((2, N_DEV // 2, d, shard), BF16),
        scratch_shapes=[pltpu.VMEM((ta, sum(widths)), F32)],
        compiler_params=_params(),
    )(ht, *arrays)


def _hgrn_state_bwd(qs, k, v, b, do, s0, ds1, bmask):
    bl = b[SUB - 1:SUB, :]
    eb, ebl, ekt = jnp.exp(b), jnp.exp(bl), jnp.exp(bl - b)
    qe, ktil = qs * eb, k * ekt
    ds1b, dob = ds1.astype(BF16), do.astype(BF16)
    dv = _dot_nt(ktil.astype(BF16), ds1b)
    dqe = _dot(dob, s0.astype(BF16))
    dktil = _dot(v.astype(BF16), ds1b)
    dbl = jnp.sum(dktil * ktil, axis=0, keepdims=True) + ebl * jnp.sum(s0 * ds1, axis=0, keepdims=True)
    ds0 = ds1 * ebl + _dot_tn(dob, qe.astype(BF16)) * bmask
    return dqe * eb, dktil * ekt, dv, dbl, ds0


def _hgrn_intra_bwd(qs, k, v, do, es, aexp, gexp, dq, dk, dv, put_dq_row):
    dks = [dk[j:j + 8] for j in range(0, SUB, 8)]
    dvs = [dv[j:j + 8] for j in range(0, SUB, 8)]
    for t in range(SUB):
        r = _live_rows(t)
        ge = gexp[t * SUB:t * SUB + r, :] * es[t]
        put_dq_row(t, dq[t:t + 1, :] + jnp.sum(ge * k[:r], axis=0, keepdims=True))
        for j in range(r // 8):
            dks[j] = dks[j] + ge[8 * j:8 * j + 8] * qs[t:t + 1, :]
            dvs[j] = dvs[j] + aexp[t * SUB + 8 * j:t * SUB + 8 * j + 8, :] * do[t:t + 1, :]
    return jnp.concatenate(dks, axis=0), jnp.concatenate(dvs, axis=0)


def _hgrn_bwd(dmix, proj, o_h, s0, gh, lb, ones_b, nb, name, rider=None):
    n = proj.shape[0]
    t = n // nb
    tt = _hgrn_tile(t)
    nt = t // tt
    ncs = tt // CHUNK
    nsub = CHUNK // SUB
    w = HGRN_W

    def body(dm_ref, q_ref, z_ref, v_ref, ga_ref, oh_ref, s0_ref, gh_ref, lb_ref, ones_ref,
             dp_ref, dgh_ref, dlb_ref, ds_s, ss_s, b_s, qs_s, k_s, do_s, dq_s, dk_s, dv_s, dbl_s):
        first = jnp.logical_and(pl.program_id(0) == 0, pl.program_id(1) == 0)

        @pl.when(first)
        def _():
            dgh_ref[...] = jnp.zeros_like(dgh_ref)
            dlb_ref[...] = jnp.zeros_like(dlb_ref)

        @pl.when(pl.program_id(1) == 0)
        def _():
            ds_s[...] = jnp.zeros_like(ds_s)

        ones_b = ones_ref[...]
        ones_f = ones_b.astype(F32)
        bmask = _block_ones(w, F32)
        lbv, ghv = lb_ref[...], gh_ref[...]
        oh, ga, dm = oh_ref[...], ga_ref[...], dm_ref[...]
        rn = lax.rsqrt(_head_mean(oh * oh, ones_f) + NORM_EPS)
        nh = oh * rn
        sga = _sigmoid(ga)
        dp_ref[:, 3 * w:4 * w] = (dm * nh * ghv * (sga * (1.0 + ga * (1.0 - sga)))).astype(BF16)
        dn = dm * (ga * sga)
        dgh_ref[...] += jnp.sum(dn * nh, axis=0, keepdims=True)
        dn = dn * ghv
        do_s[...] = rn * (dn - nh * _head_mean(dn * nh, ones_f))
        q = q_ref[...]
        sig, sn, f, g, k, sq = _hgrn_gates(q, z_ref[...], lbv)
        qs = q * sq
        b_s[...] = _sel_dot(_sub_tri(tt, True), g)
        qs_s[...] = qs
        k_s[...] = k

        def chunk(cc, carry):
            c = ncs - 1 - cc
            base = pl.multiple_of(c * CHUNK, CHUNK)
            tiles = []
            for u in range(nsub):
                rows = pl.ds(base + u * SUB, SUB)
                tiles.append((qs_s[rows, :], k_s[rows, :], v_ref[rows, :], b_s[rows, :], do_s[rows, :]))
            st = s0_ref[c]
            for u, (qs, k, v, b, do) in enumerate(tiles):
                ss_s[u] = st
                if u < nsub - 1:
                    st = _hgrn_state_step(st, k, v, b, bmask)
            ds = ds_s[...]
            for u in reversed(range(nsub)):
                qs, k, v, b, do = tiles[u]
                _, es, ws = _hgrn_decays(qs, k, b)
                gs = [_pad_rows(do[t:t + 1, :] * v[:_live_rows(t)]) for t in range(SUB)]
                aexp = _dot(jnp.concatenate(ws, axis=0).astype(BF16), ones_b)
                gexp = _dot(jnp.concatenate(gs, axis=0).astype(BF16), ones_b)
                dq, dk, dv, dbl, ds = _hgrn_state_bwd(qs, k, v, b, do, ss_s[u], ds, bmask)

                def put_dq_row(i, row, r0=base + u * SUB):
                    dq_s[pl.ds(r0 + i, 1), :] = row

                dk, dv = _hgrn_intra_bwd(qs, k, v, do, es, aexp, gexp, dq, dk, dv, put_dq_row)
                dk_s[pl.ds(base + u * SUB, SUB), :] = dk
                dv_s[pl.ds(base + u * SUB, SUB), :] = dv
                dbl_s[pl.ds(base + u * SUB, SUB), :] = jnp.broadcast_to(dbl, (SUB, w))
            ds_s[...] = ds
            return carry

        lax.fori_loop(0, ncs, chunk, 0)
        dqs, dk = dq_s[...], dk_s[...]
        dg = _sel_dot(_sub_tri(tt, False), qs * dqs - k * dk) + dbl_s[...]
        dfz = jnp.where(f > TINY, dg / jnp.maximum(f, TINY), 0.0)
        dlb_ref[...] += jnp.sum(dfz * (1.0 - sig) - dk * sn, axis=0, keepdims=True)
        dp_ref[:, 0:w] = (dqs * (sq * (1.0 + q * (1.0 - sq)))).astype(BF16)
        dp_ref[:, w:2 * w] = ((dfz - dk) * (1.0 - lbv) * sig * sn).astype(BF16)
        dp_ref[:, 2 * w:3 * w] = dv_s[...].astype(BF16)

    def rv(b, i):
        return b * nt + (nt - 1 - i)

    def col(j):
        return pl.BlockSpec((tt, w), lambda b, i: (rv(b, i), j))

    def full(a, bb):
        return pl.BlockSpec((a, bb), lambda b, i: (0, 0))

    return _call(
        body, name, (dmix, proj, proj, proj, proj, o_h, s0, gh, lb, ones_b), rider, grid=(nb, nt),
        in_specs=[col(0), col(C_QA // w), col(C_FA // w), col(C_IA // w), col(C_GA // w), col(0),
                  pl.BlockSpec((ncs, w, w), lambda b, i: (rv(b, i), 0, 0)), full(1, w), full(1, w), full(w, w)],
        out_specs=[pl.BlockSpec((tt, 4 * w), lambda b, i: (rv(b, i), 0)), full(1, w), full(1, w)],
        out_shape=[_sds((n, 4 * w), BF16), _sds((1, w), F32), _sds((1, w), F32)],
        scratch_shapes=[pltpu.VMEM((w, w), F32), pltpu.VMEM((nsub, w, w), F32)] + [pltpu.VMEM((tt, w), F32)] * 8,
        compiler_params=_params(),
    )


def _pool_bwd(dmix, proj, wbd, wbd_t, scale, nb, name):
    n = proj.shape[0]
    t = n // nb
    tt = min(512, t)
    nt = t // tt
    nhb = tt // POOL_HALO
    cu, cg, cm = C_UB // POOL_W, C_GB // POOL_W, HGRN_W // POOL_W

    def body(u_ref, g_ref, h_ref, dm_ref, gn_ref, dmn_ref, w_ref, wt_ref, s_ref, dp_ref, dw_ref, ds_ref):
        i = pl.program_id(1)
        first = jnp.logical_and(pl.program_id(0) == 0, i == 0)

        @pl.when(first)
        def _():
            dw_ref[...] = jnp.zeros_like(dw_ref)
            ds_ref[...] = jnp.zeros_like(ds_ref)

        sc = s_ref[...]
        halo = jnp.where(i == 0, 0.0, h_ref[...])
        pooled, cnt = _pool_mix(u_ref[...], halo, i * tt, tt)
        pb = pooled.astype(BF16)
        pre = _dot(pb, w_ref[...])
        gv, dm = g_ref[...], dm_ref[...]
        sg = _sigmoid(gv)
        silu = gv * sg
        dgb = dm * pre * sc * (sg * (1.0 + gv * (1.0 - sg)))
        ds_ref[...] += jnp.sum(dm * pre * silu, axis=0, keepdims=True)
        dpre = (dm * sc * silu).astype(BF16)
        dw_ref[...] += _dot_tn(pb, dpre)
        dpool = _dot(dpre, wt_ref[...])
        gn = gn_ref[...]
        dpre_n = (dmn_ref[...] * sc * (gn * _sigmoid(gn))).astype(BF16)
        dpool_n = jnp.where(i == nt - 1, 0.0, _dot(dpre_n, wt_ref[...]))
        lane, wl = _pool_lane_windows()
        tpos_n = ((i + 1) * tt + lax.broadcasted_iota(jnp.int32, (POOL_HALO, POOL_W), 0)).astype(F32)
        ext = jnp.concatenate([dpool / cnt, dpool_n / jnp.minimum(tpos_n + 1.0, wl)], axis=0)
        rows = tt + POOL_HALO
        sums, cur, shift = [], ext, 1
        for _ in POOL_WINDOWS:
            cur = cur + pltpu.roll(cur, rows - shift, axis=0)
            sums.append(cur[:tt, :])
            shift *= 2
        du = _pool_select(lane, sums) - dpool
        dp_ref[...] = jnp.concatenate([du, dgb], axis=1).astype(BF16)

    def nxt(b, i):
        return jnp.minimum((b * nt + i + 1) * nhb, n // POOL_HALO - 1)

    return _pc(
        body, name, grid=(nb, nt),
        in_specs=_pool_specs(tt, nt, nhb) + [
            pl.BlockSpec((tt, POOL_W), lambda b, i: (b * nt + i, cm)),
            pl.BlockSpec((POOL_HALO, POOL_W), lambda b, i: (nxt(b, i), cg)),
            pl.BlockSpec((POOL_HALO, POOL_W), lambda b, i: (nxt(b, i), cm)),
            pl.BlockSpec((POOL_W, POOL_W), lambda b, i: (0, 0)), pl.BlockSpec((POOL_W, POOL_W), lambda b, i: (0, 0)),
            pl.BlockSpec((1, POOL_W), lambda b, i: (0, 0))],
        out_specs=[pl.BlockSpec((tt, 2 * POOL_W), lambda b, i: (b * nt + i, 0)),
                   pl.BlockSpec((POOL_W, POOL_W), lambda b, i: (0, 0)), pl.BlockSpec((1, POOL_W), lambda b, i: (0, 0))],
        out_shape=[_sds((n, 2 * POOL_W), BF16), _sds((POOL_W, POOL_W), F32), _sds((1, POOL_W), F32)],
        compiler_params=_params(),
    )(proj, proj, proj, dmix, proj, dmix, wbd, wbd_t, scale)


def _fox_bwd(proj, qt, kt, da, dat, c_col, c_row, lse_row, delta_row, nb, name, rider=None):
    n = proj.shape[0]
    t = n // nb
    tb = min(256, t)
    nq = t // tb
    pw = 2 * HEAD
    qw = 2 if nq % 2 == 0 else 1
    nqs = nq // qw

    def body(q_ref, k_ref, v_ref, da_ref, qt_ref, kt_ref, dat_ref, cc_ref, cr_ref, lse_ref, dl_ref,
             dq_ref, dk_ref, dv_ref, dck_ref, dcq_ref, dq_s, dk_s, dv_s, dck_s, dcq_s):
        step = pl.program_id(1)
        kj, qq = pairs(step)

        @pl.when(step == 0)
        def _():
            dq_s[...] = jnp.zeros_like(dq_s)
            dcq_s[...] = jnp.zeros_like(dcq_s)

        @pl.when(qq == nqs - 1)
        def _():
            dk_s[...] = jnp.zeros_like(dk_s)
            dv_s[...] = jnp.zeros_like(dv_s)
            dck_s[...] = jnp.zeros_like(dck_s)

        def block(masked, sub):
            lo = _lane_lo()
            qi = qq * qw + sub
            qs = slice(sub * tb, (sub + 1) * tb)
            if masked:
                causal = lax.broadcasted_iota(jnp.int32, (tb, tb), 1) >= lax.broadcasted_iota(jnp.int32, (tb, tb), 0)
            dck = dck_s[...]
            for p in range(FOX_HEADS // 2):
                sl = slice(p * pw, (p + 1) * pw)
                qp = q_ref[qs, sl] * FOX_SCALE
                kp = k_ref[:, sl].astype(BF16)
                vp = v_ref[:, sl].astype(BF16)
                dap = da_ref[qs, sl]
                dk, dv = dk_s[:, sl], dv_s[:, sl]
                for h in range(2):
                    hh = 2 * p + h
                    lm = lo if h == 0 else jnp.logical_not(lo)
                    rows = slice(hh * HEAD, (hh + 1) * HEAD)
                    none = jnp.zeros((HEAD, tb), BF16)
                    qm = jnp.where(lm, qp, 0.0).astype(BF16)
                    dam = jnp.where(lm, dap, jnp.zeros_like(dap))
                    qtm = jnp.concatenate([qt_ref[rows, qs], none] if h == 0 else [none, qt_ref[rows, qs]], axis=0)
                    datm = jnp.concatenate([dat_ref[rows, qs], none] if h == 0 else [none, dat_ref[rows, qs]], axis=0)
                    s = _dot(kp, qtm) + (cr_ref[hh:hh + 1, qs] - cc_ref[:, hh:hh + 1])
                    pe = jnp.exp(s - lse_ref[hh:hh + 1, qs])
                    if masked:
                        pe = jnp.where(causal, pe, 0.0)
                    dp = _dot(vp, datm)
                    ds = pe * (dp - dl_ref[hh:hh + 1, qs])
                    dsb = ds.astype(BF16)
                    dv = dv + _dot(pe.astype(BF16), dam)
                    dk = dk + _dot(dsb, qm)
                    dq_s[qi, rows, :] += _dot(kt_ref[rows, :], dsb)
                    dck = dck - _put_col(jnp.zeros_like(dck), hh, jnp.sum(ds, axis=1, keepdims=True))
                    dcq_s[qi, hh:hh + 1, :] += _rows_reduce(ds, jnp.add, jnp.sum)
                dk_s[:, sl] = dk
                dv_s[:, sl] = dv
            dck_s[...] = dck

        for sub in reversed(range(qw)):
            @pl.when(qq * qw + sub > kj)
            def _(sub=sub):
                block(False, sub)

            @pl.when(qq * qw + sub == kj)
            def _(sub=sub):
                block(True, sub)

        @pl.when(qq == kj // qw)
        def _():
            dk_ref[...] = dk_s[...].astype(BF16)
            dv_ref[...] = dv_s[...].astype(BF16)
            dck_ref[...] = dck_s[...]

        @pl.when(step == _tri_steps(nq, qw) - 1)
        def _():
            for j in range(nq):
                dq_ref[j * tb:(j + 1) * tb, :] = (dq_s[j].T * FOX_SCALE).astype(BF16)
                dcq_ref[:, j * tb:(j + 1) * tb] = dcq_s[j]

    def pairs(step):
        a, b = _tri_pair(step, nq, qw)
        return nq - 1 - a, nqs - 1 - b

    def kspec(wd, j=0):
        return pl.BlockSpec((tb, wd), lambda b, st: (b * nq + pairs(st)[0], j))

    def qspec(wd, j=0):
        return pl.BlockSpec((qw * tb, wd), lambda b, st: (b * nqs + pairs(st)[1], j))

    def qrow():
        return pl.BlockSpec((None, FOX_HEADS, qw * tb), lambda b, st: (b, 0, pairs(st)[1]))

    def tspec(which):
        if which == 0:
            return pl.BlockSpec((FOX_W, tb), lambda b, st: (0, b * nq + pairs(st)[0]))
        return pl.BlockSpec((FOX_W, qw * tb), lambda b, st: (0, b * nqs + pairs(st)[1]))

    return _call(
        body, name, (proj, proj, proj, da, qt, kt, dat, c_col, c_row, lse_row, delta_row), rider,
        grid=(nb, _tri_steps(nq, qw)),
        in_specs=[qspec(FOX_W, C_QC // FOX_W), kspec(FOX_W, C_KC // FOX_W), kspec(FOX_W, C_VC // FOX_W), qspec(FOX_W),
                  tspec(1), tspec(0), tspec(1), kspec(FC_PAD), qrow(), qrow(), qrow()],
        out_specs=[pl.BlockSpec((t, FOX_W), lambda b, st: (b, 0)), kspec(FOX_W), kspec(FOX_W), kspec(FC_PAD),
                   pl.BlockSpec((None, FOX_HEADS, t), lambda b, st: (b, 0, 0))],
        out_shape=[_sds((n, FOX_W), BF16), _sds((n, FOX_W), BF16), _sds((n, FOX_W), BF16), _sds((n, FC_PAD), F32),
                   _sds((nb, FOX_HEADS, t), F32)],
        scratch_shapes=[pltpu.VMEM((nq, FOX_W, tb), F32), pltpu.VMEM((tb, FOX_W), F32), pltpu.VMEM((tb, FOX_W), F32),
                        pltpu.VMEM((tb, FC_PAD), F32), pltpu.VMEM((nq, FOX_HEADS, tb), F32)],
        compiler_params=_params(),
    )


def _fox_decay_bwd(dc_q, dc_k, fc, bias, nb, name):
    n = fc.shape[0]
    t = n // nb
    tt = min(256, t)
    nt = t // tt

    def body(dcq_ref, dck_ref, fc_ref, b_ref, dfc_ref, db_ref):
        @pl.when(pl.program_id(0) == 0)
        def _():
            db_ref[...] = jnp.zeros_like(db_ref)

        r = lax.broadcasted_iota(jnp.int32, (tt, tt), 0)
        cc = lax.broadcasted_iota(jnp.int32, (tt, tt), 1)
        carry = jnp.zeros((1, FC_PAD), F32)
        db = jnp.zeros((1, FC_PAD), F32)
        for i in reversed(range(nt)):
            rows = slice(i * tt, (i + 1) * tt)
            dcq = jnp.concatenate([dcq_ref[:, rows], jnp.zeros((FC_PAD - FOX_HEADS, tt), F32)], axis=0).T
            dlf = _sel_dot_exact(r <= cc, dcq + dck_ref[rows, :]) + carry
            carry = dlf[0:1, :]
            dfc = dlf * _sigmoid(-(fc_ref[rows, :] + b_ref[...]))
            dfc_ref[rows, :] = dfc.astype(BF16)
            db = db + jnp.sum(dfc, axis=0, keepdims=True)
        db_ref[...] += db

    def row():
        return pl.BlockSpec((t, FC_PAD), lambda b: (b, 0))

    return _pc(
        body, name, grid=(nb,),
        in_specs=[pl.BlockSpec((None, FOX_HEADS, t), lambda b: (b, 0, 0)), row(), row(),
                  pl.BlockSpec((1, FC_PAD), lambda b: (0, 0))],
        out_specs=[row(), pl.BlockSpec((1, FC_PAD), lambda b: (0, 0))],
        out_shape=[_sds((n, FC_PAD), BF16), _sds((1, FC_PAD), F32)],
        compiler_params=_params(),
    )(dc_q, dc_k, fc, bias)


def _in_proj_bwd(pieces, w_main_t, w_fc_t, x, g_pre, dxo, name, rider=None):
    n, d = x.shape
    tm = min(512, n)
    widths = [p.shape[1] for p, _ in pieces]
    offs = [o for _, o in pieces]
    np_ = len(pieces)

    def body(*refs):
        p_refs = refs[:np_]
        wt_ref, wf_ref, x_ref, g_ref, dxo_ref, dx_ref, dg_ref = refs[np_:]

        @pl.when(pl.program_id(0) == 0)
        def _():
            dg_ref[...] = jnp.zeros_like(dg_ref)

        dh = _dot(p_refs[-1][...], wf_ref[...])
        for pr, wd, off in zip(p_refs[:-1], widths[:-1], offs[:-1]):
            for j in range(0, wd, 512):
                jw = min(512, wd - j)
                dh = dh + _dot(pr[:, j:j + jw], wt_ref[off + j:off + j + jw, :])
        xv = x_ref[...]
        r = lax.rsqrt(jnp.mean(xv * xv, axis=-1, keepdims=True) + NORM_EPS)
        xh = xv * r
        dg_ref[...] += jnp.sum(dh * xh, axis=0, keepdims=True)
        dx_ref[...] = dxo_ref[...] + _rms_bwd(dh * g_ref[...], xh, r)

    row = pl.BlockSpec((tm, d), lambda i: (i, 0))
    return _call(
        body, name, (*[p for p, _ in pieces], w_main_t, w_fc_t, x, g_pre, dxo), rider, grid=(n // tm,),
        in_specs=[pl.BlockSpec((tm, wd), lambda i: (i, 0)) for wd in widths] + [
            pl.BlockSpec((MAIN_W, d), lambda i: (0, 0)), pl.BlockSpec((FC_PAD, d), lambda i: (0, 0)),
            row, pl.BlockSpec((1, d), lambda i: (0, 0)), row],
        out_specs=[row, pl.BlockSpec((1, d), lambda i: (0, 0))],
        out_shape=[_sds((n, d), F32), _sds((1, d), F32)],
        compiler_params=_params(),
    )


def _lower_bound_table(lower_bounds, name):
    depth, w = lower_bounds.shape

    def body(lb_ref, o_ref):
        v = lb_ref[...]
        e = jnp.exp(v - jnp.max(v, axis=0, keepdims=True))
        p = e / jnp.sum(e, axis=0, keepdims=True)
        acc = jnp.zeros((1, w), F32)
        for l in range(depth):
            acc = acc + p[l:l + 1, :]
            o_ref[l:l + 1, :] = acc - p[0:1, :]

    return _pc(body, name, out_shape=_sds((depth, w), F32))(lower_bounds)


def _lower_bound_bwd(lower_bounds, dlbs, name):
    depth, w = lower_bounds.shape

    def body(lb_ref, d_ref, o_ref):
        v, dl = lb_ref[...], d_ref[...]
        e = jnp.exp(v - jnp.max(v, axis=0, keepdims=True))
        p = e / jnp.sum(e, axis=0, keepdims=True)
        tot = jnp.sum(dl, axis=0, keepdims=True)
        rows, tail = [], tot
        for l in range(depth):
            rows.append(tail - tot if l == 0 else tail)
            tail = tail - dl[l:l + 1, :]
        dp = jnp.concatenate(rows, axis=0)
        o_ref[...] = p * (dp - jnp.sum(p * dp, axis=0, keepdims=True))

    return _pc(body, name, out_shape=_sds((depth, w), F32))(lower_bounds, dlbs)


def _place():
    x, y, c = lax.axis_index("x"), lax.axis_index("y"), lax.axis_index("c")
    return x, y, c


def _gather_weights(*arrays):
    na = len(arrays)

    def body(*refs):
        ins, outs = refs[:na], refs[na:2 * na]
        send_sems, recv_sems, local_sems = refs[2 * na:]
        x, y, c = _place()
        me, sibling = (x, y, c), (x, y, 1 - c)
        chips = [(1 - x, y), (x, 1 - y), (1 - x, 1 - y)]

        def slot(a, px, py, pc):
            return outs[a].at[4 * px + 2 * py + pc]

        def copy(a, k, block, to, own=False):
            return pltpu.make_async_remote_copy(
                src_ref=ins[a] if own else slot(a, *block), dst_ref=slot(a, *block),
                send_sem=send_sems.at[a * 7 + k], recv_sem=recv_sems.at[a * 7 + k],
                device_id=to, device_id_type=MESH)

        mine = [pltpu.make_async_copy(ins[a], slot(a, *me), local_sems.at[a]) for a in range(na)]
        for cp in mine:
            cp.start()
        first = []
        for a in range(na):
            first.append(copy(a, 0, me, sibling, own=True))
            first += [copy(a, 1 + j, me, (*chip, c), own=True) for j, chip in enumerate(chips)]
        for cp in first:
            cp.start()
        passed = []
        for j, chip in enumerate(chips):
            for a in range(na):
                copy(a, 1 + j, (*chip, c), me).wait_recv()
                fw = copy(a, 4 + j, (*chip, c), sibling)
                fw.start()
                passed.append(fw)
        for a in range(na):
            copy(a, 0, sibling, me).wait_recv()
            for j, chip in enumerate(chips):
                copy(a, 4 + j, (*chip, 1 - c), me).wait_recv()
        for cp in first + passed:
            cp.wait_send()
        for cp in mine:
            cp.wait()

    any_spec = pl.BlockSpec(memory_space=pl.ANY)
    return _pc(
        body, "gather_weights",
        in_specs=[any_spec] * na, out_specs=[any_spec] * na,
        out_shape=[_sds((N_DEV,) + a.shape, a.dtype) for a in arrays],
        scratch_shapes=[pltpu.SemaphoreType.DMA((7 * na,)), pltpu.SemaphoreType.DMA((7 * na,)),
                        pltpu.SemaphoreType.DMA((na,))],
    )(*arrays)


def _peer(k):
    x, y, c = _place()
    return (1 - x if k & 4 else x, 1 - y if k & 2 else y, 1 - c if k & 1 else c)


def _remote(src, dst, sems, s, to):
    return pltpu.make_async_remote_copy(src_ref=src, dst_ref=dst, send_sem=sems[0].at[s], recv_sem=sems[1].at[s],
                                        device_id=to, device_id_type=MESH)


def _gather_rider(shards):
    na = len(shards)

    def plan(ins, outs, *sems):
        x, y, c = _place()
        me = 4 * x + 2 * y + c
        locs = [pltpu.make_async_copy(ins[a], outs[a].at[me], sems[2].at[a]) for a in range(na)]
        sends, recvs = [], []
        for k in range(1, N_DEV):
            px, py, pc = _peer(k)
            for a in range(na):
                s = (k - 1) * na + a
                sends.append(_remote(ins[a], outs[a].at[me], sems, s, (px, py, pc)))
                recvs.append(_remote(ins[a], outs[a].at[4 * px + 2 * py + pc], sems, s, (px, py, pc)))
        return sends, recvs, locs

    return _Rider(shards, [_sds((N_DEV,) + a.shape, a.dtype) for a in shards], (N_DEV - 1) * na, na, plan)


def _direct_exchange_rider(blocks):
    na = len(blocks)

    def plan(ins, outs, *sems):
        x, y, c = _place()
        me = 4 * x + 2 * y + c
        locs = [pltpu.make_async_copy(ins[a].at[c, 2 * x + y], outs[a].at[me], sems[2].at[a]) for a in range(na)]
        sends, recvs = [], []
        for k in range(1, N_DEV):
            px, py, pc = _peer(k)
            for a in range(na):
                s = (k - 1) * na + a
                sends.append(_remote(ins[a].at[pc, 2 * px + py], outs[a].at[me], sems, s, (px, py, pc)))
                recvs.append(_remote(ins[a].at[pc, 2 * px + py], outs[a].at[4 * px + 2 * py + pc], sems, s, (px, py, pc)))
        return sends, recvs, locs

    return _Rider(blocks, [_sds((N_DEV,) + a.shape[2:], a.dtype) for a in blocks], (N_DEV - 1) * na, na, plan)


def _swap_rider(halves):
    na = len(halves)

    def plan(ins, outs, *sems):
        x, y, c = _place()
        cps = [_remote(ins[a].at[1 - c], outs[a], sems, a, (x, y, 1 - c)) for a in range(na)]
        return cps, cps, []

    return _Rider(halves, [_sds(a.shape[1:], a.dtype) for a in halves], na, 1, plan)


def _chip_exchange_rider(parts, small=None):
    na = len(parts)
    n_chip = N_DEV // 2

    def plan(ins, outs, *sems):
        x, y, c = _place()
        chip = 2 * x + y
        locs = [pltpu.make_async_copy(ins[a].at[chip], outs[a].at[chip], sems[2].at[a]) for a in range(na)]
        sends, recvs = [], []
        for k in range(1, n_chip):
            px, py, _ = _peer(2 * k)
            for a in range(na):
                s = (k - 1) * na + a
                sends.append(_remote(ins[a].at[2 * px + py], outs[a].at[chip], sems, s, (px, py, c)))
                recvs.append(_remote(ins[a].at[2 * px + py], outs[a].at[2 * px + py], sems, s, (px, py, c)))
        if small is not None:
            me = 2 * chip + c
            locs.append(pltpu.make_async_copy(ins[na], outs[na].at[me], sems[2].at[na]))
            for k in range(1, N_DEV):
                px, py, pc = _peer(k)
                s = (n_chip - 1) * na + k - 1
                sends.append(_remote(ins[na], outs[na].at[me], sems, s, (px, py, pc)))
                recvs.append(_remote(ins[na], outs[na].at[4 * px + 2 * py + pc], sems, s, (px, py, pc)))
        return sends, recvs, locs

    extra = [] if small is None else [small]
    shapes = [_sds(a.shape, a.dtype) for a in parts] + [_sds((N_DEV,) + s.shape, s.dtype) for s in extra]
    n_sems = (n_chip - 1) * na + (N_DEV - 1) * len(extra)
    return _Rider(list(parts) + extra, shapes, n_sems, na + len(extra), plan)


def _pair_add(halves, other, core, name):
    _, nch, r, c = halves.shape

    def body(c_ref, h_ref, o_ref, p_ref):
        p_ref[...] = (h_ref[...].astype(F32) + o_ref[...].astype(F32)).astype(BF16)

    blk = pl.BlockSpec((None, r, c), lambda j, c_ref: (j, 0, 0))
    return _pc(
        body, name,
        grid_spec=pltpu.PrefetchScalarGridSpec(
            num_scalar_prefetch=1, grid=(nch,),
            in_specs=[pl.BlockSpec((None, None, r, c), lambda j, c_ref: (c_ref[0], j, 0, 0)), blk], out_specs=blk),
        out_shape=_sds((nch, r, c), BF16),
        compiler_params=_params(),
    )(core, halves, other)


def _sum_adamw(parts, w, m, v, name, rider=None):
    nl, r, c = w.shape
    tr = 256 if r % 256 == 0 else r

    def body(*refs):
        p_refs = refs[:nl]
        w_ref, m_ref, v_ref, g_ref, d_ref, mo_ref, vo_ref = refs[nl:]
        for l in range(nl):
            @pl.when(pl.program_id(0) == l)
            def _(p_ref=p_refs[l]):
                g = p_ref[0].astype(F32)
                for j in range(1, p_ref.shape[0]):
                    g = g + p_ref[j].astype(F32)
                mn = ADAM_B1 * m_ref[...] + (1.0 - ADAM_B1) * g
                vn = ADAM_B2 * v_ref[...] + (1.0 - ADAM_B2) * (g * g)
                m_hat = mn / (1.0 - ADAM_B1 ** ADAM_STEP)
                v_hat = vn / (1.0 - ADAM_B2 ** ADAM_STEP)
                g_ref[...] = g
                d_ref[...] = -ADAM_LR * (m_hat / (jnp.sqrt(v_hat) + ADAM_EPS) + ADAM_WD * w_ref[...])
                mo_ref[...] = mn
                vo_ref[...] = vn

    def part_spec(l, k):
        return pl.BlockSpec((k, tr, c), lambda li, i: (0, jnp.where(li == l, i, 0), 0))

    row = pl.BlockSpec((None, tr, c), lambda li, i: (li, i, 0))
    return _call(
        body, name, (*parts, w, m, v), rider, grid=(nl, r // tr),
        in_specs=[part_spec(l, p.shape[0]) for l, p in enumerate(parts)] + [row, row, row],
        out_specs=[row] * 4,
        out_shape=[_sds((nl, r, c), F32)] * 4,
        compiler_params=_params(),
    )


SMALL = ("lower_bounds", "pre_norm_g", "hgrn_norm_g", "pool_w", "pool_scale", "post_norm_g", "fox_f_bias")
SMALL_LANES = 128


def _small_size(tree):
    return sum(tree[k].size for k in SMALL)


def _pack_small(tree, extra=None):
    flat = jnp.concatenate([tree[k].reshape(-1) for k in SMALL] + ([] if extra is None else [extra.reshape(1)]))
    rows = -(-(_small_size(tree) + 1) // (8 * SMALL_LANES)) * 8
    return jnp.pad(flat, (0, rows * SMALL_LANES - flat.shape[0])).reshape(rows, SMALL_LANES)


def _unpack_small(packed, like):
    out, off = {}, 0
    for k in SMALL:
        size = like[k].size
        assert off % SMALL_LANES == 0
        rows = packed[off // SMALL_LANES:-(-(off + size) // SMALL_LANES)]
        out[k] = rows.reshape(-1)[:size].reshape(like[k].shape)
        off += size
    return out


def _block_diag(pw):
    g = pw.shape[0]
    eye = jnp.eye(g, dtype=pw.dtype)
    return (eye[:, None, :, None] * pw[:, :, None, :]).reshape(g * HEAD, g * HEAD)


def _assemble_w_in(g_in, name):
    _, d, shard = g_in.shape
    tr = min(256, d)
    wide = MAIN_W + FC_PAD

    def body(g_ref, wm_ref, wf_ref, wmt_ref, wft_ref, row_s):
        row_s[:, MAIN_W:] = jnp.zeros((tr, FC_PAD), F32)
        for j in range(N_DEV):
            row_s[:, j * shard:(j + 1) * shard] = g_ref[j].astype(F32)
        wm_ref[...] = row_s[:, :MAIN_W].astype(BF16)
        wf_ref[...] = row_s[:, MAIN_W:].astype(BF16)
        for j in range(0, MAIN_W, 512):
            wmt_ref[j:j + 512, :] = row_s[:, j:j + 512].T.astype(BF16)
        wft_ref[...] = row_s[:, MAIN_W:].T.astype(BF16)

    return _pc(
        body, name, grid=(d // tr,),
        in_specs=[pl.BlockSpec((N_DEV, tr, shard), lambda i: (0, i, 0))],
        out_specs=[pl.BlockSpec((tr, MAIN_W), lambda i: (i, 0)), pl.BlockSpec((tr, FC_PAD), lambda i: (i, 0)),
                   pl.BlockSpec((MAIN_W, tr), lambda i: (0, i)), pl.BlockSpec((FC_PAD, tr), lambda i: (0, i))],
        out_shape=[_sds((d, MAIN_W), BF16), _sds((d, FC_PAD), BF16), _sds((MAIN_W, d), BF16), _sds((FC_PAD, d), BF16)],
        scratch_shapes=[pltpu.VMEM((tr, wide), F32)],
        compiler_params=_params(),
    )(g_in)


def _w_out_parts(g_out):
    full_out = g_out.reshape(N_DEV * g_out.shape[1], g_out.shape[2])
    return full_out, full_out.T


def _layer_fwd(l, x, lbs, weights, lw, nb, rider_h=None, rider_c=None, target=None):
    n = x.shape[0]
    t = n // nb
    w_main, w_fc, _, _, w_out, _ = lw
    bias = jnp.pad(weights["fox_f_bias"][l:l + 1], ((0, 0), (0, FC_PAD - FOX_HEADS)))
    wbd = _block_diag(weights["pool_w"][l]).astype(BF16)
    proj, fc, ht, qt, kt = _in_proj_fwd(x, weights["pre_norm_g"][l:l + 1], w_main, w_fc, f"in_proj_fwd_{l}")
    c_col, c_row = _fox_decay_fwd(fc, bias, nb, f"fox_decay_fwd_{l}")
    (o_h, s0), rode_h = _hgrn_fwd(proj, lbs[l:l + 1], _block_ones(HGRN_W, BF16), nb, f"hgrn_fwd_{l}", rider_h)
    o_b = _pool_fwd(proj, wbd, weights["pool_scale"][l:l + 1], nb, f"pool_fwd_{l}")
    (o_c, lse), rode_c = _fox_fwd(proj, kt, c_col, c_row, nb, f"fox_fwd_{l}", rider_c)
    if w_out is None:
        lw = tuple(lw[:4]) + _w_out_parts(rode_h[0])
        w_out, rode_h = lw[4], rode_h[1:]
    x_next, mixt, y = _merge_fwd(x, proj, o_h, o_b, o_c, weights["hgrn_norm_g"][l:l + 1], w_out,
                                 weights["post_norm_g"][l:l + 1], f"merge_fwd_{l}", target)
    return x_next, (x, proj, fc, ht, qt, kt, c_col, c_row, o_h, s0, o_c, lse, mixt, y, bias, wbd), lw, (rode_h, rode_c)


def _layer_bwd(l, dx, saved, lbs, weights, lw, nb, rider=None, send_w_out=None):
    _, proj, fc, ht, qt, kt, c_col, c_row, o_h, s0, o_c, lse, mixt, y, bias, wbd = saved
    w_out_t = lw[5]
    g = {}
    dy, dmix, dgp, da, dat, d_gc, delta = _merge_bwd(dx, y, weights["post_norm_g"][l:l + 1], w_out_t, proj, o_c,
                                                nb, f"merge_bwd_{l}")
    g["post_norm_g"] = dgp[0]
    g["w_out"] = _w_out_grad(mixt, dy, f"w_out_grad_{l}")
    (d_a, dgh, dlb), arrived = _hgrn_bwd(dmix, proj, o_h, s0, weights["hgrn_norm_g"][l:l + 1], lbs[l:l + 1],
                                         _block_ones(HGRN_W, BF16), nb, f"hgrn_bwd_{l}",
                                         None if send_w_out is None else send_w_out([g["w_out"]]))
    if arrived is not None:
        g["w_out_received"] = arrived[0]
    g["hgrn_norm_g"], g["lbs"] = dgh[0], dlb[0]
    d_b, dwbd, dps = _pool_bwd(dmix, proj, wbd, wbd.T, weights["pool_scale"][l:l + 1], nb, f"pool_bwd_{l}")
    g["pool_w"] = jnp.stack([dwbd[j * HEAD:(j + 1) * HEAD, j * HEAD:(j + 1) * HEAD] for j in range(len(POOL_WINDOWS))])
    g["pool_scale"] = dps[0]
    (d_qc, d_kc, d_vc, dc_k, dc_q), rode = _fox_bwd(proj, qt, kt, da, dat, c_col, c_row, lse, delta, nb,
                                                    f"fox_bwd_{l}", rider)
    d_fc, dbias = _fox_decay_bwd(dc_q, dc_k, fc, bias, nb, f"fox_decay_bwd_{l}")
    g["fox_f_bias"] = dbias[0, :FOX_HEADS]
    pieces = [(d_a, C_QA), (d_b, C_UB), (d_qc, C_QC), (d_kc, C_KC), (d_vc, C_VC), (d_gc, C_GC), (d_fc, None)]
    g["w_in"] = _w_in_grad(ht, pieces, f"w_in_grad_{l}")
    return g, pieces, rode


def _layer_bwd_input(l, dx, pieces, saved, weights, lw, rider=None):
    (dxi, dgpre), rode = _in_proj_bwd(pieces, lw[2], lw[3], saved[0], weights["pre_norm_g"][l:l + 1], dx,
                                      f"in_proj_bwd_{l}", rider)
    return dxi, dgpre[0], rode


def kernel(x, lower_bounds, pre_norm_g, w_in, hgrn_norm_g, fox_f_bias, pool_w, pool_scale, w_out, post_norm_g, loss_target, m_lower_bounds, m_pre_norm_g, m_w_in, m_hgrn_norm_g, m_fox_f_bias, m_pool_w, m_pool_scale, m_w_out, m_post_norm_g, v_lower_bounds, v_pre_norm_g, v_w_in, v_hgrn_norm_g, v_fox_f_bias, v_pool_w, v_pool_scale, v_w_out, v_post_norm_g):
    weights = dict(lower_bounds=lower_bounds, pre_norm_g=pre_norm_g, hgrn_norm_g=hgrn_norm_g, fox_f_bias=fox_f_bias,
                   pool_w=pool_w, pool_scale=pool_scale, post_norm_g=post_norm_g)
    mom_m = dict(lower_bounds=m_lower_bounds, pre_norm_g=m_pre_norm_g, hgrn_norm_g=m_hgrn_norm_g, fox_f_bias=m_fox_f_bias,
                 pool_w=m_pool_w, pool_scale=m_pool_scale, post_norm_g=m_post_norm_g)
    mom_v = dict(lower_bounds=v_lower_bounds, pre_norm_g=v_pre_norm_g, hgrn_norm_g=v_hgrn_norm_g, fox_f_bias=v_fox_f_bias,
                 pool_w=v_pool_w, pool_scale=v_pool_scale, post_norm_g=v_post_norm_g)
    depth = w_in.shape[0]
    nb, t, d = x.shape
    n = nb * t
    core = lax.axis_index("c").astype(jnp.int32).reshape(1)
    shards = [(w_in[l].astype(BF16), w_out[l].astype(BF16)) for l in range(depth)]
    lbs = _lower_bound_table(lower_bounds, "lower_bound_table")

    (g_in,) = _gather_weights(shards[0][0])
    coming = tuple(_assemble_w_in(g_in, "assemble_w_in_0")) + (None, None)
    xl, saved, lw = x.reshape(n, d), [], []
    for l in range(depth):
        last = l + 1 == depth
        ride_h = ([shards[l][1]] if coming[4] is None else []) + ([] if last else [shards[l + 1][1]])
        xl, sv, lw_l, (rode_h, rode_c) = _layer_fwd(
            l, xl, lbs, weights, coming, nb, _gather_rider(ride_h) if ride_h else None,
            None if last else _gather_rider([shards[l + 1][0]]), loss_target.reshape(n, d) if last else None)
        saved.append(sv)
        lw.append(lw_l)
        if not last:
            coming = tuple(_assemble_w_in(rode_c[0], f"assemble_w_in_{l + 1}")) + _w_out_parts(rode_h[0])
    dx, sq = xl
    loss_here = 0.5 * jnp.sum(sq) / d

    grads, recv_in, pending = [None] * depth, [None] * depth, None
    for l in reversed(range(depth)):
        g, pieces, rode = _layer_bwd(l, dx, saved[l], lbs, weights, lw[l], nb, pending, _direct_exchange_rider)
        if rode is not None:
            recv_in[l + 1] = rode[0]
        if l > 0:
            pending = _direct_exchange_rider([g["w_in"]])
            dx, g["pre_norm_g"], _ = _layer_bwd_input(l, dx, pieces, saved[l], weights, lw[l])
        else:
            (other,) = _run_rider(_swap_rider([g["w_in"]]), "grad_swap")
            summed = _pair_add(g["w_in"], other, core, "grad_pair_add")
            dx, g["pre_norm_g"], (recv_in[l],) = _layer_bwd_input(l, dx, pieces, saved[l], weights, lw[l],
                                                                  _chip_exchange_rider([summed]))
        grads[l] = g
    small = {k: jnp.stack([grads[l][k] for l in range(depth)]) for k in SMALL if k != "lower_bounds"}
    small["lower_bounds"] = _lower_bound_bwd(lower_bounds, jnp.stack([grads[l]["lbs"] for l in range(depth)]),
                                             "lower_bound_bwd")
    (r_small,) = _run_rider(_gather_rider([_pack_small(small, loss_here)]), "small_grads_gather")

    res_in, _ = _sum_adamw(recv_in, w_in, m_w_in, v_w_in, "adamw_w_in")
    res_out, _ = _sum_adamw([grads[l]["w_out_received"] for l in range(depth)], w_out, m_w_out, v_w_out, "adamw_w_out")
    res_small, _ = _sum_adamw([r_small], _pack_small(weights)[None], _pack_small(mom_m)[None], _pack_small(mom_v)[None],
                              "adamw_small")
    loss = res_small[0][0][_small_size(weights) // SMALL_LANES, _small_size(weights) % SMALL_LANES]

    names = ("lower_bounds", "pre_norm_g", "w_in", "hgrn_norm_g", "fox_f_bias", "pool_w", "pool_scale", "w_out", "post_norm_g")
    outs = [loss, dx.reshape(nb, t, d)]
    for i in range(4):
        full = dict(_unpack_small(res_small[i][0], weights), w_in=res_in[i], w_out=res_out[i])
        outs += [full[k] for k in names]
    return tuple(outs)
```

```python
import functools

import jax
import jax.numpy as jnp
from jax import lax
from jax.experimental import pallas as pl
from jax.experimental.pallas import tpu as pltpu

F32, BF16 = jnp.float32, jnp.bfloat16
MESH = pl.DeviceIdType.MESH
N_DEV = 8

NORM_EPS = 1e-6
MASK_VALUE = -1e30
TINY = 1e-30
CHUNK = 64
SUB = 16
HGRN_W, POOL_W, FOX_W = 256, 256, 512
HEAD = 64
FOX_HEADS = 8
POOL_WINDOWS = (2, 4, 8, 16)
POOL_HALO = 16
MAIN_W = 3584
FC_PAD = 128
C_QA, C_FA, C_IA, C_GA, C_UB, C_GB, C_QC, C_KC, C_VC, C_GC = 0, 256, 512, 768, 1024, 1280, 1536, 2048, 2560, 3072
FOX_SCALE = HEAD ** -0.5

ADAM_LR, ADAM_B1, ADAM_B2, ADAM_EPS, ADAM_WD, ADAM_STEP = 0.001, 0.9, 0.999, 1e-08, 0.01, 10

VMEM_LIMIT = 56 * 1024 * 1024


def _pc(fn, name, **kw):
    return pl.pallas_call(fn, name=name, **kw)


def _params(**kw):
    return pltpu.CompilerParams(vmem_limit_bytes=VMEM_LIMIT, **kw)


class _Rider:
    def __init__(self, inputs, out_shapes, n_sems, n_local, plan):
        self.inputs, self.out_shapes, self.n_sems, self.n_local, self.plan = list(inputs), list(out_shapes), n_sems, n_local, plan

    def start(self, ins, outs, *sems):
        sends, _, locs = self.plan(ins, outs, *sems)
        for cp in locs + sends:
            cp.start()

    def wait(self, ins, outs, *sems):
        sends, recvs, locs = self.plan(ins, outs, *sems)
        for cp in recvs:
            cp.wait_recv()
        for cp in sends:
            cp.wait_send()
        for cp in locs:
            cp.wait()

    def sem_shapes(self):
        return [pltpu.SemaphoreType.DMA((self.n_sems,)), pltpu.SemaphoreType.DMA((self.n_sems,)),
                pltpu.SemaphoreType.DMA((self.n_local,))]


def _call(body, name, args, rider=None, *, grid, in_specs, out_specs, out_shape, scratch_shapes=(), **kw):
    if rider is None:
        res = _pc(body, name, grid=grid, in_specs=in_specs, out_specs=out_specs, out_shape=out_shape,
                  scratch_shapes=list(scratch_shapes), **kw)(*args)
        return res, None
    n_in, n_out, n_scr = len(in_specs), len(out_specs), len(scratch_shapes)
    n_rin, n_rout = len(rider.inputs), len(rider.out_shapes)

    def ridden(*refs):
        ins, refs = refs[:n_in], refs[n_in:]
        rins, refs = refs[:n_rin], refs[n_rin:]
        outs, refs = refs[:n_out], refs[n_out:]
        routs, refs = refs[:n_rout], refs[n_rout:]
        scr, sems = refs[:n_scr], refs[n_scr:]
        first = functools.reduce(jnp.logical_and, [pl.program_id(a) == 0 for a in range(len(grid))])
        last = functools.reduce(jnp.logical_and, [pl.program_id(a) == g - 1 for a, g in enumerate(grid)])

        @pl.when(first)
        def _():
            rider.start(rins, routs, *sems)

        body(*ins, *outs, *scr)

        @pl.when(last)
        def _():
            rider.wait(rins, routs, *sems)

    any_spec = pl.BlockSpec(memory_space=pl.ANY)
    res = _pc(ridden, name, grid=grid, in_specs=list(in_specs) + [any_spec] * n_rin,
              out_specs=list(out_specs) + [any_spec] * n_rout, out_shape=list(out_shape) + rider.out_shapes,
              scratch_shapes=list(scratch_shapes) + rider.sem_shapes(), **kw)(*args, *rider.inputs)
    return res[:n_out], res[n_out:]


def _run_rider(rider, name):
    n_rin = len(rider.inputs)

    def body(*refs):
        ins, outs, sems = refs[:n_rin], refs[n_rin:n_rin + len(rider.out_shapes)], refs[n_rin + len(rider.out_shapes):]
        rider.start(ins, outs, *sems)
        rider.wait(ins, outs, *sems)

    any_spec = pl.BlockSpec(memory_space=pl.ANY)
    return _pc(body, name, in_specs=[any_spec] * n_rin, out_specs=[any_spec] * len(rider.out_shapes),
               out_shape=rider.out_shapes, scratch_shapes=rider.sem_shapes())(*rider.inputs)


def _dot(a, b):
    return jnp.dot(a, b, preferred_element_type=F32)


def _dot_nt(a, b):
    return lax.dot_general(a, b, (((1,), (1,)), ((), ())), preferred_element_type=F32)


def _dot_tn(a, b):
    return lax.dot_general(a, b, (((0,), (0,)), ((), ())), preferred_element_type=F32)


def _sel_dot_exact(sel, x):
    hi = x.astype(BF16)
    rest = x - hi.astype(F32)
    mid = rest.astype(BF16)
    lo = (rest - mid.astype(F32)).astype(BF16)
    sb = sel.astype(BF16)
    return _dot(sb, hi) + _dot(sb, mid) + _dot(sb, lo)


def _split2(x):
    hi = x.astype(BF16)
    return hi, (x - hi.astype(F32)).astype(BF16)


def _sel_dot(sel, x):
    hi, lo = _split2(x)
    sb = sel.astype(BF16)
    return _dot(sb, hi) + _dot(sb, lo)


def _dot_sel(x, sel):
    hi, lo = _split2(x)
    sb = sel.astype(BF16)
    return _dot(hi, sb) + _dot(lo, sb)


def _sigmoid(x):
    return 1.0 / (1.0 + jnp.exp(-x))


def _block_ones(n, dtype):
    r = lax.broadcasted_iota(jnp.int32, (n, n), 0) // HEAD
    c = lax.broadcasted_iota(jnp.int32, (n, n), 1) // HEAD
    return (r == c).astype(dtype)


def _sds(shape, dtype):
    return jax.ShapeDtypeStruct(shape, dtype)


def _in_proj_fwd(x, g_pre, w_main, w_fc, name):
    n, d = x.shape
    tm = min(512, n)

    def body(x_ref, g_ref, w_ref, wf_ref, proj_ref, fc_ref, ht_ref, qt_ref, kt_ref):
        xv = x_ref[...]
        r = lax.rsqrt(jnp.mean(xv * xv, axis=-1, keepdims=True) + NORM_EPS)
        hf = xv * r * g_ref[...]
        hb = hf.astype(BF16)
        ht_ref[...] = hf.T.astype(BF16)
        for j in range(0, MAIN_W, FOX_W):
            res = _dot(hb, w_ref[:, j:j + FOX_W])
            proj_ref[:, j:j + FOX_W] = res
            if j == C_QC:
                qt_ref[...] = (res * FOX_SCALE).T.astype(BF16)
            if j == C_KC:
                kt_ref[...] = res.T.astype(BF16)
        fc_ref[...] = _dot(hb, wf_ref[...])

    def cols(rows):
        return pl.BlockSpec((rows, tm), lambda i: (0, i))

    return _pc(
        body, name, grid=(n // tm,),
        in_specs=[pl.BlockSpec((tm, d), lambda i: (i, 0)), pl.BlockSpec((1, d), lambda i: (0, 0)),
                  pl.BlockSpec((d, MAIN_W), lambda i: (0, 0)), pl.BlockSpec((d, FC_PAD), lambda i: (0, 0))],
        out_specs=[pl.BlockSpec((tm, MAIN_W), lambda i: (i, 0)), pl.BlockSpec((tm, FC_PAD), lambda i: (i, 0)),
                   cols(d), cols(FOX_W), cols(FOX_W)],
        out_shape=[_sds((n, MAIN_W), F32), _sds((n, FC_PAD), F32), _sds((d, n), BF16), _sds((FOX_W, n), BF16),
                   _sds((FOX_W, n), BF16)],
        compiler_params=_params(),
    )(x, g_pre, w_main, w_fc)


def _fox_decay_fwd(fc, bias, nb, name):
    n = fc.shape[0]
    t = n // nb
    tt = min(256, t)
    nt = t // tt

    def body(fc_ref, b_ref, c_ref, cr_ref):
        r = lax.broadcasted_iota(jnp.int32, (tt, tt), 0)
        cc = lax.broadcasted_iota(jnp.int32, (tt, tt), 1)
        carry = jnp.zeros((1, FC_PAD), F32)
        for i in range(nt):
            rows = slice(i * tt, (i + 1) * tt)
            xv = fc_ref[rows, :] + b_ref[...]
            lf = jnp.minimum(xv, 0.0) - jnp.log(1.0 + jnp.exp(-jnp.abs(xv)))
            cs = _sel_dot_exact(r >= cc, lf) + carry
            c_ref[rows, :] = cs
            cr_ref[:, rows] = cs.T[:FOX_HEADS, :]
            carry = cs[tt - 1:tt, :]

    return _pc(
        body, name, grid=(nb,),
        in_specs=[pl.BlockSpec((t, FC_PAD), lambda b: (b, 0)), pl.BlockSpec((1, FC_PAD), lambda b: (0, 0))],
        out_specs=[pl.BlockSpec((t, FC_PAD), lambda b: (b, 0)), pl.BlockSpec((None, FOX_HEADS, t), lambda b: (b, 0, 0))],
        out_shape=[_sds((n, FC_PAD), F32), _sds((nb, FOX_HEADS, t), F32)],
        compiler_params=_params(),
    )(fc, bias)


def _hgrn_gates(q, z, lb):
    sig = _sigmoid(z)
    sn = _sigmoid(-z)
    f = lb + (1.0 - lb) * sig
    g = jnp.log(jnp.maximum(f, TINY))
    k = (1.0 - lb) * sn
    sq = _sigmoid(q)
    return sig, sn, f, g, k, sq


def _sub_tri(n, lower):
    r = lax.broadcasted_iota(jnp.int32, (n, n), 0)
    c = lax.broadcasted_iota(jnp.int32, (n, n), 1)
    tri = (r >= c) if lower else (r <= c)
    return jnp.logical_and(r // SUB == c // SUB, tri).astype(F32)


def _live_rows(t):
    return 8 * (t // 8 + 1)


def _pad_rows(x):
    return x if x.shape[0] == SUB else jnp.concatenate([x, jnp.zeros((SUB - x.shape[0], x.shape[1]), x.dtype)], axis=0)


def _hgrn_decays(qs, k, b):
    srow = lax.broadcasted_iota(jnp.int32, (SUB, HGRN_W), 0)
    es, ws = [], []
    for t in range(SUB):
        r = _live_rows(t)
        e = jnp.where(srow[:r] <= t, jnp.exp(b[t:t + 1, :] - b[:r]), 0.0)
        es.append(e)
        ws.append(_pad_rows(e * (qs[t:t + 1, :] * k[:r])))
    return srow, es, ws


def _hgrn_state_step(st, k, v, b, bmask):
    bl = b[SUB - 1:SUB, :]
    ktil = k * jnp.exp(bl - b)
    return st * jnp.exp(bl) + _dot_tn(v.astype(BF16), ktil.astype(BF16)) * bmask


def _hgrn_tile(t):
    return min(256, t)


def _hgrn_fwd(proj, lb, ones_b, nb, name, rider=None):
    n = proj.shape[0]
    t = n // nb
    tt = _hgrn_tile(t)
    nt = t // tt
    ncs = tt // CHUNK
    w = HGRN_W

    def body(q_ref, z_ref, v_ref, lb_ref, ones_ref, o_ref, s0_ref, st_s, b_s, qs_s, k_s):
        @pl.when(pl.program_id(0) == 0)
        def _():
            st_s[...] = jnp.zeros_like(st_s)

        for e in range(nb):
            q = q_ref[e]
            _, _, _, g, k, sq = _hgrn_gates(q, z_ref[e], lb_ref[...])
            b_s[e] = _sel_dot(_sub_tri(tt, True), g)
            qs_s[e] = q * sq
            k_s[e] = k
        bmask = _block_ones(w, F32)
        ones_b = ones_ref[...]
        nsub = CHUNK // SUB

        def chunk(c, carry):
            sts = [st_s[e] for e in range(nb)]
            for e in range(nb):
                s0_ref[e, c] = sts[e]
            base = pl.multiple_of(c * CHUNK, CHUNK)
            tiles = [[(qs_s[e, pl.ds(base + u * SUB, SUB), :], k_s[e, pl.ds(base + u * SUB, SUB), :],
                       v_ref[e, pl.ds(base + u * SUB, SUB), :], b_s[e, pl.ds(base + u * SUB, SUB), :])
                      for u in range(nsub)] for e in range(nb)]
            aexps = [[] for _ in range(nb)]
            for e in range(nb):
                for qs, k, v, b in tiles[e]:
                    _, _, ws = _hgrn_decays(qs, k, b)
                    aexps[e].append(_dot(jnp.concatenate(ws, axis=0).astype(BF16), ones_b))
            inters = [[] for _ in range(nb)]
            for u in range(nsub):
                for e in range(nb):
                    qs, k, v, b = tiles[e][u]
                    inters[e].append(_dot_nt((qs * jnp.exp(b)).astype(BF16), sts[e].astype(BF16)))
                    sts[e] = _hgrn_state_step(sts[e], k, v, b, bmask)
            for e in range(nb):
                st_s[e] = sts[e]
            for e in range(nb):
                for u, ((qs, k, v, b), aexp, o) in enumerate(zip(tiles[e], aexps[e], inters[e])):
                    for t in range(SUB):
                        r = _live_rows(t)
                        row = o[t:t + 1, :] + jnp.sum(aexp[t * SUB:t * SUB + r, :] * v[:r], axis=0, keepdims=True)
                        o_ref[e, pl.ds(base + u * SUB + t, 1), :] = row
            return carry

        lax.fori_loop(0, ncs, chunk, 0)

    def col(j):
        return pl.BlockSpec((nb, tt, w), lambda i: (0, i, j))

    proj3 = proj.reshape(nb, t, proj.shape[1])
    (o, s0), rode = _call(
        body, name, (proj3, proj3, proj3, lb, ones_b), rider, grid=(nt,),
        in_specs=[col(C_QA // w), col(C_FA // w), col(C_IA // w), pl.BlockSpec((1, w), lambda i: (0, 0)),
                  pl.BlockSpec((w, w), lambda i: (0, 0))],
        out_specs=[pl.BlockSpec((nb, tt, w), lambda i: (0, i, 0)),
                   pl.BlockSpec((nb, ncs, w, w), lambda i: (0, i, 0, 0))],
        out_shape=[_sds((nb, t, w), F32), _sds((nb, t // CHUNK, w, w), F32)],
        scratch_shapes=[pltpu.VMEM((nb, w, w), F32)] + [pltpu.VMEM((nb, tt, w), F32)] * 3,
        compiler_params=_params(),
    )
    return (o.reshape(n, w), s0.reshape(n // CHUNK, w, w)), rode


def _pool_lane_windows():
    lane = lax.broadcasted_iota(jnp.int32, (1, POOL_W), 1) // HEAD
    wl = jnp.zeros((1, POOL_W), F32)
    for gi, win in enumerate(POOL_WINDOWS):
        wl = jnp.where(lane == gi, float(win), wl)
    return lane, wl


def _pool_select(lane, parts):
    out = parts[-1]
    for gi in range(len(parts) - 2, -1, -1):
        out = jnp.where(lane == gi, parts[gi], out)
    return out


def _pool_mix(u, halo, t0, tt):
    lane, wl = _pool_lane_windows()
    ext = jnp.concatenate([halo, u], axis=0)
    sums, cur, shift = [], ext, 1
    for _ in POOL_WINDOWS:
        cur = cur + pltpu.roll(cur, shift, axis=0)
        sums.append(cur[POOL_HALO:, :])
        shift *= 2
    tpos = (t0 + lax.broadcasted_iota(jnp.int32, (tt, POOL_W), 0)).astype(F32)
    cnt = jnp.minimum(tpos + 1.0, wl)
    return _pool_select(lane, sums) / cnt - u, cnt


def _pool_specs(tt, nt, nhb):
    cu, cg = C_UB // POOL_W, C_GB // POOL_W
    return [pl.BlockSpec((tt, POOL_W), lambda b, i: (b * nt + i, cu)),
            pl.BlockSpec((tt, POOL_W), lambda b, i: (b * nt + i, cg)),
            pl.BlockSpec((POOL_HALO, POOL_W), lambda b, i: (jnp.maximum((b * nt + i) * nhb - 1, 0), cu))]


def _pool_fwd(proj, wbd, scale, nb, name):
    n = proj.shape[0]
    t = n // nb
    tt = min(512, t)
    nt = t // tt
    nhb = tt // POOL_HALO

    def body(u_ref, g_ref, h_ref, w_ref, s_ref, o_ref):
        i = pl.program_id(1)
        halo = jnp.where(i == 0, 0.0, h_ref[...])
        pooled, _ = _pool_mix(u_ref[...], halo, i * tt, tt)
        gv = g_ref[...]
        o_ref[...] = _dot(pooled.astype(BF16), w_ref[...]) * s_ref[...] * (gv * _sigmoid(gv))

    return _pc(
        body, name, grid=(nb, nt),
        in_specs=_pool_specs(tt, nt, nhb) + [pl.BlockSpec((POOL_W, POOL_W), lambda b, i: (0, 0)),
                                             pl.BlockSpec((1, POOL_W), lambda b, i: (0, 0))],
        out_specs=pl.BlockSpec((tt, POOL_W), lambda b, i: (b * nt + i, 0)),
        out_shape=_sds((n, POOL_W), F32),
        compiler_params=_params(),
    )(proj, proj, proj, wbd, scale)


def _rows_reduce(x, op, final):
    while x.shape[0] > 8 and x.shape[0] % 16 == 0:
        half = x.shape[0] // 2
        x = op(x[:half], x[half:])
    return final(x, axis=0, keepdims=True)


def _tri_pair(step, n, group=1):
    counts = [a // group + 1 for a in range(n)]
    firsts = [sum(counts[:a]) for a in range(1, n)]
    a = sum([(step >= f).astype(jnp.int32) for f in firsts], jnp.int32(0))
    first = sum([jnp.where(step >= f, c, 0) for f, c in zip(firsts, counts)], jnp.int32(0))
    return a, step - first


def _tri_steps(n, group=1):
    return sum(a // group + 1 for a in range(n))


def _lane_lo():
    return lax.broadcasted_iota(jnp.int32, (1, 2 * HEAD), 1) < HEAD


def _put_col(tile, hh, colv):
    lane = lax.broadcasted_iota(jnp.int32, tile.shape, 1)
    return jnp.where(lane == hh, colv, tile)


def _fox_fwd(proj, kt, c_col, c_row, nb, name, rider=None):
    n = proj.shape[0]
    t = n // nb
    tb = min(256, t)
    nq = t // tb
    pw = 2 * HEAD
    kw = 2 if nq % 2 == 0 else 1
    nk = nq // kw

    def body(q_ref, kt_ref, v_ref, cc_ref, cr_ref, o_ref, lse_ref, m_s, acc_s, cq_s):
        qi, kk = _tri_pair(pl.program_id(1), nq, kw)

        @pl.when(kk == 0)
        def _():
            m_s[...] = jnp.full_like(m_s, -jnp.inf)
            acc_s[...] = jnp.zeros_like(acc_s)
            for hh in range(FOX_HEADS):
                cq_s[hh] = jnp.broadcast_to(cc_ref[:, hh:hh + 1], (tb, pw))

        def block(masked, sub):
            lo = _lane_lo()
            keys = slice(sub * tb, (sub + 1) * tb)
            if masked:
                causal = lax.broadcasted_iota(jnp.int32, (tb, tb), 0) >= lax.broadcasted_iota(jnp.int32, (tb, tb), 1)
            def lanes(hh):
                return lo if hh % 2 == 0 else jnp.logical_not(lo)

            def scores(hh):
                sl = slice((hh // 2) * pw, (hh // 2 + 1) * pw)
                return _dot(jnp.where(lanes(hh), q_ref[:, sl] * FOX_SCALE, 0.0).astype(BF16), kt_ref[sl, keys])

            ahead = scores(0)
            for hh in range(FOX_HEADS):
                s = ahead
                if hh + 1 < FOX_HEADS:
                    ahead = scores(hh + 1)
                s = s + (jnp.tile(cq_s[hh], (1, tb // pw)) - cr_ref[hh:hh + 1, keys])
                if masked:
                    s = jnp.where(causal, s, MASK_VALUE)
                m_prev = m_s[hh]
                m_new = jnp.maximum(m_prev, jnp.max(s, axis=1, keepdims=True))
                alpha = jnp.exp(m_prev - m_new)
                pe = jnp.exp(s - jnp.tile(m_new, (1, tb // pw)))
                m_s[hh] = m_new
                vf = v_ref[keys, (hh // 2) * pw:(hh // 2 + 1) * pw]
                acc_s[hh] = alpha * acc_s[hh] + _dot(pe.astype(BF16), jnp.where(lanes(hh), vf, 1.0).astype(BF16))

        def finish():
            lo = _lane_lo()
            m_all, l_all = jnp.zeros((tb, FC_PAD), F32), jnp.ones((tb, FC_PAD), F32)
            for p in range(FOX_HEADS // 2):
                a0, a1 = acc_s[2 * p], acc_s[2 * p + 1]
                both = pltpu.roll(jnp.where(lo, a1, a0), HEAD, axis=1)
                o_ref[:, p * pw:(p + 1) * pw] = jnp.where(lo, a0, a1) / both
                m_all = _put_col(_put_col(m_all, 2 * p, m_s[2 * p]), 2 * p + 1, m_s[2 * p + 1])
                l_all = _put_col(_put_col(l_all, 2 * p, both), 2 * p + 1, a1)
            lse_ref[...] = (m_all + jnp.log(l_all)).T[:FOX_HEADS, :]

        for sub in range(kw):
            @pl.when(kk * kw + sub < qi)
            def _(sub=sub):
                block(False, sub)

            @pl.when(kk * kw + sub == qi)
            def _(sub=sub):
                block(True, sub)
                finish()

    def qspec(wd, j):
        return pl.BlockSpec((tb, wd), lambda b, st: (b * nq + _tri_pair(st, nq, kw)[0], j))

    def kspec(j):
        return pl.BlockSpec((kw * tb, FOX_W), lambda b, st: (b * nk + _tri_pair(st, nq, kw)[1], j))

    return _call(
        body, name, (proj, kt, proj, c_col, c_row), rider, grid=(nb, _tri_steps(nq, kw)),
        in_specs=[qspec(FOX_W, C_QC // FOX_W),
                  pl.BlockSpec((FOX_W, kw * tb), lambda b, st: (0, b * nk + _tri_pair(st, nq, kw)[1])),
                  kspec(C_VC // FOX_W), qspec(FC_PAD, 0),
                  pl.BlockSpec((None, FOX_HEADS, kw * tb), lambda b, st: (b, 0, _tri_pair(st, nq, kw)[1]))],
        out_specs=[qspec(FOX_W, 0),
                   pl.BlockSpec((None, FOX_HEADS, tb), lambda b, st: (b, 0, _tri_pair(st, nq, kw)[0]))],
        out_shape=[_sds((n, FOX_W), F32), _sds((nb, FOX_HEADS, t), F32)],
        scratch_shapes=[pltpu.VMEM((FOX_HEADS, tb, pw), F32), pltpu.VMEM((FOX_HEADS, tb, pw), F32),
                        pltpu.VMEM((FOX_HEADS, tb, pw), F32)],
        compiler_params=_params(),
    )


def _head_mean(x, ones_f):
    return _dot_sel(x, ones_f) * (1.0 / HEAD)


def _merge_fwd(x, proj, o_h, o_b, o_c, gh, w_out, g_post, name, target=None):
    n, d = x.shape
    tm = min(512, n)

    def body(*refs):
        x_ref, ga_ref, gc_ref, oh_ref, ob_ref, oc_ref, gh_ref, w_ref, gp_ref = refs[:9]
        if target is None:
            xo_ref, mixt_ref, y_ref = refs[9:]
        else:
            t_ref, dx_ref, sq_ref, mixt_ref, y_ref = refs[9:]
        oh = oh_ref[...]
        ones_f = _block_ones(HGRN_W, F32)
        na = oh * lax.rsqrt(_head_mean(oh * oh, ones_f) + NORM_EPS) * gh_ref[...]
        ga, gc = ga_ref[...], gc_ref[...]
        mixed = jnp.concatenate([na * (ga * _sigmoid(ga)), ob_ref[...], oc_ref[...] * (gc * _sigmoid(gc))], axis=1)
        mixt_ref[...] = mixed.T.astype(BF16)
        y = _dot(mixed.astype(BF16), w_ref[...])
        y_ref[...] = y
        xn = x_ref[...] + y * lax.rsqrt(jnp.mean(y * y, axis=-1, keepdims=True) + NORM_EPS) * gp_ref[...]
        if target is None:
            xo_ref[...] = xn
        else:
            @pl.when(pl.program_id(0) == 0)
            def _():
                sq_ref[...] = jnp.zeros_like(sq_ref)

            e = xn - t_ref[...]
            dx_ref[...] = e * (1.0 / d)
            sq_ref[...] += jnp.sum(e * e, axis=0, keepdims=True)

    def row(wd, j=0):
        return pl.BlockSpec((tm, wd), lambda i: (i, j))

    def full(a, b):
        return pl.BlockSpec((a, b), lambda i: (0, 0))

    head = [] if target is None else [target]
    res = _pc(
        body, name, grid=(n // tm,),
        in_specs=[row(d), row(HGRN_W, C_GA // HGRN_W), row(FOX_W, C_GC // FOX_W), row(HGRN_W), row(POOL_W), row(FOX_W),
                  full(1, HGRN_W), full(d, d), full(1, d)] + [row(d)] * len(head),
        out_specs=[row(d)] + [full(1, d)] * len(head) + [pl.BlockSpec((d, tm), lambda i: (0, i)), row(d)],
        out_shape=[_sds((n, d), F32)] + [_sds((1, d), F32)] * len(head) + [_sds((d, n), BF16), _sds((n, d), F32)],
        compiler_params=_params(),
    )(x, proj, proj, o_h, o_b, o_c, gh, w_out, g_post, *head)
    return (res[0], res[1], res[2]) if target is None else ((res[0], res[1]), res[2], res[3])


def _rms_bwd(dy_scaled, xhat, r):
    return r * (dy_scaled - xhat * jnp.mean(dy_scaled * xhat, axis=-1, keepdims=True))


def _merge_bwd(dxo, y, g_post, w_out_t, proj, o_c, nb, name):
    n, d = y.shape
    t = n // nb
    tm = min(512, t)
    nt = t // tm
    wab = HGRN_W + POOL_W

    def body(dx_ref, y_ref, gp_ref, wt_ref, gc_ref, oc_ref, dy_ref, dm_ref, dgp_ref, da_ref, dat_ref, dg_ref, dl_ref):
        @pl.when(pl.program_id(0) == 0)
        def _():
            dgp_ref[...] = jnp.zeros_like(dgp_ref)

        yv, dxv = y_ref[...], dx_ref[...]
        r = lax.rsqrt(jnp.mean(yv * yv, axis=-1, keepdims=True) + NORM_EPS)
        yh = yv * r
        dgp_ref[...] += jnp.sum(dxv * yh, axis=0, keepdims=True)
        dyb = _rms_bwd(dxv * gp_ref[...], yh, r).astype(BF16)
        dy_ref[...] = dyb
        dm_ref[...] = _dot(dyb, wt_ref[:, :wab])
        dmc = _dot(dyb, wt_ref[:, wab:])
        gc, oc = gc_ref[...], oc_ref[...]
        sg = _sigmoid(gc)
        da = dmc * (gc * sg)
        da_ref[...] = da.astype(BF16)
        dat_ref[...] = da.T.astype(BF16)
        dg_ref[...] = (dmc * oc * (sg * (1.0 + gc * (1.0 - sg)))).astype(BF16)
        rr = lax.broadcasted_iota(jnp.int32, (FOX_W, FC_PAD), 0) // HEAD
        cc = lax.broadcasted_iota(jnp.int32, (FOX_W, FC_PAD), 1)
        dl_ref[...] = _dot_sel(da * oc, (rr == cc).astype(F32)).T[:FOX_HEADS, :]

    def row(wd, j=0):
        return pl.BlockSpec((tm, wd), lambda i: (i, j))

    def full(a, b):
        return pl.BlockSpec((a, b), lambda i: (0, 0))

    return _pc(
        body, name, grid=(n // tm,),
        in_specs=[row(d), row(d), full(1, d), full(d, d), row(FOX_W, C_GC // FOX_W), row(FOX_W)],
        out_specs=[row(d), row(wab), full(1, d), row(FOX_W), pl.BlockSpec((FOX_W, tm), lambda i: (0, i)), row(FOX_W),
                   pl.BlockSpec((None, FOX_HEADS, tm), lambda i: (i // nt, 0, i % nt))],
        out_shape=[_sds((n, d), BF16), _sds((n, wab), F32), _sds((1, d), F32), _sds((n, FOX_W), BF16),
                   _sds((FOX_W, n), BF16), _sds((n, FOX_W), BF16), _sds((nb, FOX_HEADS, t), F32)],
        compiler_params=_params(),
    )(dxo, y, g_post, w_out_t, proj, o_c)


def _w_out_grad(mixt, dy, name):
    d, n = mixt.shape
    rows = d // N_DEV

    def body(a_ref, b_ref, o_ref):
        o_ref[...] = _dot(a_ref[...], b_ref[...]).astype(BF16)

    return _pc(
        body, name, grid=(N_DEV,),
        in_specs=[pl.BlockSpec((rows, n), lambda j: (j, 0)), pl.BlockSpec((n, d), lambda j: (0, 0))],
        out_specs=pl.BlockSpec((None, None, rows, d), lambda j: (j % 2, j // 2, 0, 0)),
        out_shape=_sds((2, N_DEV // 2, rows, d), BF16),
        compiler_params=_params(),
    )(mixt, dy)


def _w_in_grad(ht, pieces, name):
    d, n = ht.shape
    ta, tk = min(512, d), min(512, n)
    nk = n // tk
    arrays = [p for p, _ in pieces]
    widths = [p.shape[1] for p in arrays]
    offs = [sum(widths[:i]) for i in range(len(widths))]
    in_w = MAIN_W + FOX_HEADS
    shard = in_w // N_DEV

    def body(*refs):
        a_ref, p_refs = refs[0], refs[1:1 + len(arrays)]
        o_ref, acc = refs[1 + len(arrays):]
        k = pl.program_id(1)

        @pl.when(k == 0)
        def _():
            acc[...] = jnp.zeros_like(acc)

        a = a_ref[...]
        for pr, off, wd in zip(p_refs, offs, widths):
            for j in range(0, wd, 512):
                jw = min(512, wd - j)
                acc[:, off + j:off + j + jw] += _dot(a, pr[:, j:j + jw])

        @pl.when(k == nk - 1)
        def _():
            for j in range(N_DEV):
                o_ref[j % 2, j // 2] = acc[:, j * shard:(j + 1) * shard].astype(BF16)

    return _pc(
        body, name, grid=(d // ta, nk),
        in_specs=[pl.BlockSpec((ta, tk), lambda i, k: (i, k))] + [pl.BlockSpec((tk, wd), lambda i, k: (k, 0)) for wd in widths],
        out_specs=pl.BlockSpec((2, N_DEV // 2, ta, shard), lambda i, k: (0, 0, i, 0)),
        out_shape=_sds((2, N_DEV // 2, d, shard), BF16),
        scratch_shapes=[pltpu.VMEM((ta, sum(widths)), F32)],
        compiler_params=_params(),
    )(ht, *arrays)


def _hgrn_state_bwd(qs, k, v, b, do, s0, ds1, bmask):
    bl = b[SUB - 1:SUB, :]
    eb, ebl, ekt = jnp.exp(b), jnp.exp(bl), jnp.exp(bl - b)
    qe, ktil = qs * eb, k * ekt
    ds1b, dob = ds1.astype(BF16), do.astype(BF16)
    dv = _dot_nt(ktil.astype(BF16), ds1b)
    dqe = _dot(dob, s0.astype(BF16))
    dktil = _dot(v.astype(BF16), ds1b)
    dbl = jnp.sum(dktil * ktil, axis=0, keepdims=True) + ebl * jnp.sum(s0 * ds1, axis=0, keepdims=True)
    ds0 = ds1 * ebl + _dot_tn(dob, qe.astype(BF16)) * bmask
    return dqe * eb, dktil * ekt, dv, dbl, ds0


def _hgrn_intra_bwd(qs, k, v, do, es, aexp, gexp, dq, dk, dv, put_dq_row):
    dks = [dk[j:j + 8] for j in range(0, SUB, 8)]
    dvs = [dv[j:j + 8] for j in range(0, SUB, 8)]
    for t in range(SUB):
        r = _live_rows(t)
        ge = gexp[t * SUB:t * SUB + r, :] * es[t]
        put_dq_row(t, dq[t:t + 1, :] + jnp.sum(ge * k[:r], axis=0, keepdims=True))
        for j in range(r // 8):
            dks[j] = dks[j] + ge[8 * j:8 * j + 8] * qs[t:t + 1, :]
            dvs[j] = dvs[j] + aexp[t * SUB + 8 * j:t * SUB + 8 * j + 8, :] * do[t:t + 1, :]
    return jnp.concatenate(dks, axis=0), jnp.concatenate(dvs, axis=0)


def _hgrn_bwd(dmix, proj, o_h, s0, gh, lb, ones_b, nb, name, rider=None):
    n = proj.shape[0]
    t = n // nb
    tt = _hgrn_tile(t)
    nt = t // tt
    ncs = tt // CHUNK
    nsub = CHUNK // SUB
    w = HGRN_W

    def body(dm_ref, q_ref, z_ref, v_ref, ga_ref, oh_ref, s0_ref, gh_ref, lb_ref, ones_ref,
             dp_ref, dgh_ref, dlb_ref, ds_s, ss_s, b_s, qs_s, k_s, do_s, dq_s, dk_s, dv_s, dbl_s):
        @pl.when(pl.program_id(0) == 0)
        def _():
            dgh_ref[...] = jnp.zeros_like(dgh_ref)
            dlb_ref[...] = jnp.zeros_like(dlb_ref)
            ds_s[...] = jnp.zeros_like(ds_s)

        ones_b = ones_ref[...]
        ones_f = ones_b.astype(F32)
        bmask = _block_ones(w, F32)
        lbv, ghv = lb_ref[...], gh_ref[...]
        kept = []
        for e in range(nb):
            oh, ga, dm = oh_ref[e], ga_ref[e], dm_ref[e]
            rn = lax.rsqrt(_head_mean(oh * oh, ones_f) + NORM_EPS)
            nh = oh * rn
            sga = _sigmoid(ga)
            dp_ref[e, :, 3 * w:4 * w] = (dm * nh * ghv * (sga * (1.0 + ga * (1.0 - sga)))).astype(BF16)
            dn = dm * (ga * sga)
            dgh_ref[...] += jnp.sum(dn * nh, axis=0, keepdims=True)
            dn = dn * ghv
            do_s[e] = rn * (dn - nh * _head_mean(dn * nh, ones_f))
            q = q_ref[e]
            sig, sn, f, g, k, sq = _hgrn_gates(q, z_ref[e], lbv)
            qs = q * sq
            b_s[e] = _sel_dot(_sub_tri(tt, True), g)
            qs_s[e] = qs
            k_s[e] = k
            kept.append((q, sig, sn, f, k, sq, qs))

        def chunk(cc, carry):
            c = ncs - 1 - cc
            base = pl.multiple_of(c * CHUNK, CHUNK)
            tiles = [[(qs_s[e, pl.ds(base + u * SUB, SUB), :], k_s[e, pl.ds(base + u * SUB, SUB), :],
                       v_ref[e, pl.ds(base + u * SUB, SUB), :], b_s[e, pl.ds(base + u * SUB, SUB), :],
                       do_s[e, pl.ds(base + u * SUB, SUB), :]) for u in range(nsub)] for e in range(nb)]
            sts = [s0_ref[e, c] for e in range(nb)]
            for u in range(nsub):
                for e in range(nb):
                    qs, k, v, b, do = tiles[e][u]
                    ss_s[e, u] = sts[e]
                    if u < nsub - 1:
                        sts[e] = _hgrn_state_step(sts[e], k, v, b, bmask)
            dss = [ds_s[e] for e in range(nb)]
            for u in reversed(range(nsub)):
                for e in range(nb):
                    qs, k, v, b, do = tiles[e][u]
                    _, es, ws = _hgrn_decays(qs, k, b)
                    gs = [_pad_rows(do[t:t + 1, :] * v[:_live_rows(t)]) for t in range(SUB)]
                    aexp = _dot(jnp.concatenate(ws, axis=0).astype(BF16), ones_b)
                    gexp = _dot(jnp.concatenate(gs, axis=0).astype(BF16), ones_b)
                    dq, dk, dv, dbl, dss[e] = _hgrn_state_bwd(qs, k, v, b, do, ss_s[e, u], dss[e], bmask)

                    def put_dq_row(i, row, e=e, r0=base + u * SUB):
                        dq_s[e, pl.ds(r0 + i, 1), :] = row

                    dk, dv = _hgrn_intra_bwd(qs, k, v, do, es, aexp, gexp, dq, dk, dv, put_dq_row)
                    dk_s[e, pl.ds(base + u * SUB, SUB), :] = dk
                    dv_s[e, pl.ds(base + u * SUB, SUB), :] = dv
                    dbl_s[e, pl.ds(base + u * SUB, SUB), :] = jnp.broadcast_to(dbl, (SUB, w))
            for e in range(nb):
                ds_s[e] = dss[e]
            return carry

        lax.fori_loop(0, ncs, chunk, 0)
        for e, (q, sig, sn, f, k, sq, qs) in enumerate(kept):
            dqs, dk = dq_s[e], dk_s[e]
            dg = _sel_dot(_sub_tri(tt, False), qs * dqs - k * dk) + dbl_s[e]
            dfz = jnp.where(f > TINY, dg / jnp.maximum(f, TINY), 0.0)
            dlb_ref[...] += jnp.sum(dfz * (1.0 - sig) - dk * sn, axis=0, keepdims=True)
            dp_ref[e, :, 0:w] = (dqs * (sq * (1.0 + q * (1.0 - sq)))).astype(BF16)
            dp_ref[e, :, w:2 * w] = ((dfz - dk) * (1.0 - lbv) * sig * sn).astype(BF16)
            dp_ref[e, :, 2 * w:3 * w] = dv_s[e].astype(BF16)

    def col(j):
        return pl.BlockSpec((nb, tt, w), lambda i: (0, nt - 1 - i, j))

    def full(a, bb):
        return pl.BlockSpec((a, bb), lambda i: (0, 0))

    proj3 = proj.reshape(nb, t, proj.shape[1])
    (dp, dgh, dlb), rode = _call(
        body, name, (dmix.reshape(nb, t, dmix.shape[1]), proj3, proj3, proj3, proj3, o_h.reshape(nb, t, w),
                     s0.reshape(nb, t // CHUNK, w, w), gh, lb, ones_b), rider, grid=(nt,),
        in_specs=[col(0), col(C_QA // w), col(C_FA // w), col(C_IA // w), col(C_GA // w), col(0),
                  pl.BlockSpec((nb, ncs, w, w), lambda i: (0, nt - 1 - i, 0, 0)), full(1, w), full(1, w), full(w, w)],
        out_specs=[pl.BlockSpec((nb, tt, 4 * w), lambda i: (0, nt - 1 - i, 0)), full(1, w), full(1, w)],
        out_shape=[_sds((nb, t, 4 * w), BF16), _sds((1, w), F32), _sds((1, w), F32)],
        scratch_shapes=[pltpu.VMEM((nb, w, w), F32), pltpu.VMEM((nb, nsub, w, w), F32)]
        + [pltpu.VMEM((nb, tt, w), F32)] * 8,
        compiler_params=_params(),
    )
    return (dp.reshape(n, 4 * w), dgh, dlb), rode


def _pool_bwd(dmix, proj, wbd, wbd_t, scale, nb, name):
    n = proj.shape[0]
    t = n // nb
    tt = min(512, t)
    nt = t // tt
    nhb = tt // POOL_HALO
    cu, cg, cm = C_UB // POOL_W, C_GB // POOL_W, HGRN_W // POOL_W

    def body(u_ref, g_ref, h_ref, dm_ref, gn_ref, dmn_ref, w_ref, wt_ref, s_ref, dp_ref, dw_ref, ds_ref):
        i = pl.program_id(1)
        first = jnp.logical_and(pl.program_id(0) == 0, i == 0)

        @pl.when(first)
        def _():
            dw_ref[...] = jnp.zeros_like(dw_ref)
            ds_ref[...] = jnp.zeros_like(ds_ref)

        sc = s_ref[...]
        halo = jnp.where(i == 0, 0.0, h_ref[...])
        pooled, cnt = _pool_mix(u_ref[...], halo, i * tt, tt)
        pb = pooled.astype(BF16)
        pre = _dot(pb, w_ref[...])
        gv, dm = g_ref[...], dm_ref[...]
        sg = _sigmoid(gv)
        silu = gv * sg
        dgb = dm * pre * sc * (sg * (1.0 + gv * (1.0 - sg)))
        ds_ref[...] += jnp.sum(dm * pre * silu, axis=0, keepdims=True)
        dpre = (dm * sc * silu).astype(BF16)
        dw_ref[...] += _dot_tn(pb, dpre)
        dpool = _dot(dpre, wt_ref[...])
        gn = gn_ref[...]
        dpre_n = (dmn_ref[...] * sc * (gn * _sigmoid(gn))).astype(BF16)
        dpool_n = jnp.where(i == nt - 1, 0.0, _dot(dpre_n, wt_ref[...]))
        lane, wl = _pool_lane_windows()
        tpos_n = ((i + 1) * tt + lax.broadcasted_iota(jnp.int32, (POOL_HALO, POOL_W), 0)).astype(F32)
        ext = jnp.concatenate([dpool / cnt, dpool_n / jnp.minimum(tpos_n + 1.0, wl)], axis=0)
        rows = tt + POOL_HALO
        sums, cur, shift = [], ext, 1
        for _ in POOL_WINDOWS:
            cur = cur + pltpu.roll(cur, rows - shift, axis=0)
            sums.append(cur[:tt, :])
            shift *= 2
        du = _pool_select(lane, sums) - dpool
        dp_ref[...] = jnp.concatenate([du, dgb], axis=1).astype(BF16)

    def nxt(b, i):
        return jnp.minimum((b * nt + i + 1) * nhb, n // POOL_HALO - 1)

    return _pc(
        body, name, grid=(nb, nt),
        in_specs=_pool_specs(tt, nt, nhb) + [
            pl.BlockSpec((tt, POOL_W), lambda b, i: (b * nt + i, cm)),
            pl.BlockSpec((POOL_HALO, POOL_W), lambda b, i: (nxt(b, i), cg)),
            pl.BlockSpec((POOL_HALO, POOL_W), lambda b, i: (nxt(b, i), cm)),
            pl.BlockSpec((POOL_W, POOL_W), lambda b, i: (0, 0)), pl.BlockSpec((POOL_W, POOL_W), lambda b, i: (0, 0)),
            pl.BlockSpec((1, POOL_W), lambda b, i: (0, 0))],
        out_specs=[pl.BlockSpec((tt, 2 * POOL_W), lambda b, i: (b * nt + i, 0)),
                   pl.BlockSpec((POOL_W, POOL_W), lambda b, i: (0, 0)), pl.BlockSpec((1, POOL_W), lambda b, i: (0, 0))],
        out_shape=[_sds((n, 2 * POOL_W), BF16), _sds((POOL_W, POOL_W), F32), _sds((1, POOL_W), F32)],
        compiler_params=_params(),
    )(proj, proj, proj, dmix, proj, dmix, wbd, wbd_t, scale)


def _fox_bwd(proj, qt, kt, da, dat, c_col, c_row, lse_row, delta_row, nb, name, rider=None):
    n = proj.shape[0]
    t = n // nb
    tb = min(256, t)
    nq = t // tb
    pw = 2 * HEAD
    qw = 2 if nq % 2 == 0 else 1
    nqs = nq // qw

    def body(q_ref, k_ref, v_ref, da_ref, qt_ref, kt_ref, dat_ref, cc_ref, cr_ref, lse_ref, dl_ref,
             dq_ref, dk_ref, dv_ref, dck_ref, dcq_ref, dq_s, dk_s, dv_s, dck_s, dcq_s):
        step = pl.program_id(1)
        kj, qq = pairs(step)

        @pl.when(step == 0)
        def _():
            dq_s[...] = jnp.zeros_like(dq_s)
            dcq_s[...] = jnp.zeros_like(dcq_s)

        @pl.when(qq == nqs - 1)
        def _():
            dk_s[...] = jnp.zeros_like(dk_s)
            dv_s[...] = jnp.zeros_like(dv_s)
            dck_s[...] = jnp.zeros_like(dck_s)

        def block(masked, sub):
            lo = _lane_lo()
            qi = qq * qw + sub
            qs = slice(sub * tb, (sub + 1) * tb)
            if masked:
                causal = lax.broadcasted_iota(jnp.int32, (tb, tb), 1) >= lax.broadcasted_iota(jnp.int32, (tb, tb), 0)
            dck = dck_s[...]
            for p in range(FOX_HEADS // 2):
                sl = slice(p * pw, (p + 1) * pw)
                qp = q_ref[qs, sl] * FOX_SCALE
                kp = k_ref[:, sl].astype(BF16)
                vp = v_ref[:, sl].astype(BF16)
                dap = da_ref[qs, sl]
                dk, dv = dk_s[:, sl], dv_s[:, sl]
                for h in range(2):
                    hh = 2 * p + h
                    lm = lo if h == 0 else jnp.logical_not(lo)
                    rows = slice(hh * HEAD, (hh + 1) * HEAD)
                    none = jnp.zeros((HEAD, tb), BF16)
                    qm = jnp.where(lm, qp, 0.0).astype(BF16)
                    dam = jnp.where(lm, dap, jnp.zeros_like(dap))
                    qtm = jnp.concatenate([qt_ref[rows, qs], none] if h == 0 else [none, qt_ref[rows, qs]], axis=0)
                    datm = jnp.concatenate([dat_ref[rows, qs], none] if h == 0 else [none, dat_ref[rows, qs]], axis=0)
                    s = _dot(kp, qtm) + (cr_ref[hh:hh + 1, qs] - cc_ref[:, hh:hh + 1])
                    pe = jnp.exp(s - lse_ref[hh:hh + 1, qs])
                    if masked:
                        pe = jnp.where(causal, pe, 0.0)
                    dp = _dot(vp, datm)
                    ds = pe * (dp - dl_ref[hh:hh + 1, qs])
                    dsb = ds.astype(BF16)
                    dv = dv + _dot(pe.astype(BF16), dam)
                    dk = dk + _dot(dsb, qm)
                    dq_s[qi, rows, :] += _dot(kt_ref[rows, :], dsb)
                    dck = dck - _put_col(jnp.zeros_like(dck), hh, jnp.sum(ds, axis=1, keepdims=True))
                    dcq_s[qi, hh:hh + 1, :] += _rows_reduce(ds, jnp.add, jnp.sum)
                dk_s[:, sl] = dk
                dv_s[:, sl] = dv
            dck_s[...] = dck

        for sub in reversed(range(qw)):
            @pl.when(qq * qw + sub > kj)
            def _(sub=sub):
                block(False, sub)

            @pl.when(qq * qw + sub == kj)
            def _(sub=sub):
                block(True, sub)

        @pl.when(qq == kj // qw)
        def _():
            dk_ref[...] = dk_s[...].astype(BF16)
            dv_ref[...] = dv_s[...].astype(BF16)
            dck_ref[...] = dck_s[...]

        @pl.when(step == _tri_steps(nq, qw) - 1)
        def _():
            for j in range(nq):
                dq_ref[j * tb:(j + 1) * tb, :] = (dq_s[j].T * FOX_SCALE).astype(BF16)
                dcq_ref[:, j * tb:(j + 1) * tb] = dcq_s[j]

    def pairs(step):
        a, b = _tri_pair(step, nq, qw)
        return nq - 1 - a, nqs - 1 - b

    def kspec(wd, j=0):
        return pl.BlockSpec((tb, wd), lambda b, st: (b * nq + pairs(st)[0], j))

    def qspec(wd, j=0):
        return pl.BlockSpec((qw * tb, wd), lambda b, st: (b * nqs + pairs(st)[1], j))

    def qrow():
        return pl.BlockSpec((None, FOX_HEADS, qw * tb), lambda b, st: (b, 0, pairs(st)[1]))

    def tspec(which):
        if which == 0:
            return pl.BlockSpec((FOX_W, tb), lambda b, st: (0, b * nq + pairs(st)[0]))
        return pl.BlockSpec((FOX_W, qw * tb), lambda b, st: (0, b * nqs + pairs(st)[1]))

    return _call(
        body, name, (proj, proj, proj, da, qt, kt, dat, c_col, c_row, lse_row, delta_row), rider,
        grid=(nb, _tri_steps(nq, qw)),
        in_specs=[qspec(FOX_W, C_QC // FOX_W), kspec(FOX_W, C_KC // FOX_W), kspec(FOX_W, C_VC // FOX_W), qspec(FOX_W),
                  tspec(1), tspec(0), tspec(1), kspec(FC_PAD), qrow(), qrow(), qrow()],
        out_specs=[pl.BlockSpec((t, FOX_W), lambda b, st: (b, 0)), kspec(FOX_W), kspec(FOX_W), kspec(FC_PAD),
                   pl.BlockSpec((None, FOX_HEADS, t), lambda b, st: (b, 0, 0))],
        out_shape=[_sds((n, FOX_W), BF16), _sds((n, FOX_W), BF16), _sds((n, FOX_W), BF16), _sds((n, FC_PAD), F32),
                   _sds((nb, FOX_HEADS, t), F32)],
        scratch_shapes=[pltpu.VMEM((nq, FOX_W, tb), F32), pltpu.VMEM((tb, FOX_W), F32), pltpu.VMEM((tb, FOX_W), F32),
                        pltpu.VMEM((tb, FC_PAD), F32), pltpu.VMEM((nq, FOX_HEADS, tb), F32)],
        compiler_params=_params(),
    )


def _fox_decay_bwd(dc_q, dc_k, fc, bias, nb, name):
    n = fc.shape[0]
    t = n // nb
    tt = min(256, t)
    nt = t // tt

    def body(dcq_ref, dck_ref, fc_ref, b_ref, dfc_ref, db_ref):
        @pl.when(pl.program_id(0) == 0)
        def _():
            db_ref[...] = jnp.zeros_like(db_ref)

        r = lax.broadcasted_iota(jnp.int32, (tt, tt), 0)
        cc = lax.broadcasted_iota(jnp.int32, (tt, tt), 1)
        carry = jnp.zeros((1, FC_PAD), F32)
        db = jnp.zeros((1, FC_PAD), F32)
        for i in reversed(range(nt)):
            rows = slice(i * tt, (i + 1) * tt)
            dcq = jnp.concatenate([dcq_ref[:, rows], jnp.zeros((FC_PAD - FOX_HEADS, tt), F32)], axis=0).T
            dlf = _sel_dot_exact(r <= cc, dcq + dck_ref[rows, :]) + carry
            carry = dlf[0:1, :]
            dfc = dlf * _sigmoid(-(fc_ref[rows, :] + b_ref[...]))
            dfc_ref[rows, :] = dfc.astype(BF16)
            db = db + jnp.sum(dfc, axis=0, keepdims=True)
        db_ref[...] += db

    def row():
        return pl.BlockSpec((t, FC_PAD), lambda b: (b, 0))

    return _pc(
        body, name, grid=(nb,),
        in_specs=[pl.BlockSpec((None, FOX_HEADS, t), lambda b: (b, 0, 0)), row(), row(),
                  pl.BlockSpec((1, FC_PAD), lambda b: (0, 0))],
        out_specs=[row(), pl.BlockSpec((1, FC_PAD), lambda b: (0, 0))],
        out_shape=[_sds((n, FC_PAD), BF16), _sds((1, FC_PAD), F32)],
        compiler_params=_params(),
    )(dc_q, dc_k, fc, bias)


def _in_proj_bwd(pieces, w_main_t, w_fc_t, x, g_pre, dxo, name, rider=None):
    n, d = x.shape
    tm = min(512, n)
    widths = [p.shape[1] for p, _ in pieces]
    offs = [o for _, o in pieces]
    np_ = len(pieces)

    def body(*refs):
        p_refs = refs[:np_]
        wt_ref, wf_ref, x_ref, g_ref, dxo_ref, dx_ref, dg_ref = refs[np_:]

        @pl.when(pl.program_id(0) == 0)
        def _():
            dg_ref[...] = jnp.zeros_like(dg_ref)

        dh = _dot(p_refs[-1][...], wf_ref[...])
        for pr, wd, off in zip(p_refs[:-1], widths[:-1], offs[:-1]):
            for j in range(0, wd, 512):
                jw = min(512, wd - j)
                dh = dh + _dot(pr[:, j:j + jw], wt_ref[off + j:off + j + jw, :])
        xv = x_ref[...]
        r = lax.rsqrt(jnp.mean(xv * xv, axis=-1, keepdims=True) + NORM_EPS)
        xh = xv * r
        dg_ref[...] += jnp.sum(dh * xh, axis=0, keepdims=True)
        dx_ref[...] = dxo_ref[...] + _rms_bwd(dh * g_ref[...], xh, r)

    row = pl.BlockSpec((tm, d), lambda i: (i, 0))
    return _call(
        body, name, (*[p for p, _ in pieces], w_main_t, w_fc_t, x, g_pre, dxo), rider, grid=(n // tm,),
        in_specs=[pl.BlockSpec((tm, wd), lambda i: (i, 0)) for wd in widths] + [
            pl.BlockSpec((MAIN_W, d), lambda i: (0, 0)), pl.BlockSpec((FC_PAD, d), lambda i: (0, 0)),
            row, pl.BlockSpec((1, d), lambda i: (0, 0)), row],
        out_specs=[row, pl.BlockSpec((1, d), lambda i: (0, 0))],
        out_shape=[_sds((n, d), F32), _sds((1, d), F32)],
        compiler_params=_params(),
    )


def _lower_bound_table(lower_bounds, name):
    depth, w = lower_bounds.shape

    def body(lb_ref, o_ref):
        v = lb_ref[...]
        e = jnp.exp(v - jnp.max(v, axis=0, keepdims=True))
        p = e / jnp.sum(e, axis=0, keepdims=True)
        acc = jnp.zeros((1, w), F32)
        for l in range(depth):
            acc = acc + p[l:l + 1, :]
            o_ref[l:l + 1, :] = acc - p[0:1, :]

    return _pc(body, name, out_shape=_sds((depth, w), F32))(lower_bounds)


def _lower_bound_bwd(lower_bounds, dlbs, name):
    depth, w = lower_bounds.shape

    def body(lb_ref, d_ref, o_ref):
        v, dl = lb_ref[...], d_ref[...]
        e = jnp.exp(v - jnp.max(v, axis=0, keepdims=True))
        p = e / jnp.sum(e, axis=0, keepdims=True)
        tot = jnp.sum(dl, axis=0, keepdims=True)
        rows, tail = [], tot
        for l in range(depth):
            rows.append(tail - tot if l == 0 else tail)
            tail = tail - dl[l:l + 1, :]
        dp = jnp.concatenate(rows, axis=0)
        o_ref[...] = p * (dp - jnp.sum(p * dp, axis=0, keepdims=True))

    return _pc(body, name, out_shape=_sds((depth, w), F32))(lower_bounds, dlbs)


def _place():
    x, y, c = lax.axis_index("x"), lax.axis_index("y"), lax.axis_index("c")
    return x, y, c


def _gather_weights(*arrays):
    na = len(arrays)

    def body(*refs):
        ins, outs = refs[:na], refs[na:2 * na]
        send_sems, recv_sems, local_sems = refs[2 * na:]
        x, y, c = _place()
        me, sibling = (x, y, c), (x, y, 1 - c)
        chips = [(1 - x, y), (x, 1 - y), (1 - x, 1 - y)]

        def slot(a, px, py, pc):
            return outs[a].at[4 * px + 2 * py + pc]

        def copy(a, k, block, to, own=False):
            return pltpu.make_async_remote_copy(
                src_ref=ins[a] if own else slot(a, *block), dst_ref=slot(a, *block),
                send_sem=send_sems.at[a * 7 + k], recv_sem=recv_sems.at[a * 7 + k],
                device_id=to, device_id_type=MESH)

        mine = [pltpu.make_async_copy(ins[a], slot(a, *me), local_sems.at[a]) for a in range(na)]
        for cp in mine:
            cp.start()
        first = []
        for a in range(na):
            first.append(copy(a, 0, me, sibling, own=True))
            first += [copy(a, 1 + j, me, (*chip, c), own=True) for j, chip in enumerate(chips)]
        for cp in first:
            cp.start()
        passed = []
        for j, chip in enumerate(chips):
            for a in range(na):
                copy(a, 1 + j, (*chip, c), me).wait_recv()
                fw = copy(a, 4 + j, (*chip, c), sibling)
                fw.start()
                passed.append(fw)
        for a in range(na):
            copy(a, 0, sibling, me).wait_recv()
            for j, chip in enumerate(chips):
                copy(a, 4 + j, (*chip, 1 - c), me).wait_recv()
        for cp in first + passed:
            cp.wait_send()
        for cp in mine:
            cp.wait()

    any_spec = pl.BlockSpec(memory_space=pl.ANY)
    return _pc(
        body, "gather_weights",
        in_specs=[any_spec] * na, out_specs=[any_spec] * na,
        out_shape=[_sds((N_DEV,) + a.shape, a.dtype) for a in arrays],
        scratch_shapes=[pltpu.SemaphoreType.DMA((7 * na,)), pltpu.SemaphoreType.DMA((7 * na,)),
                        pltpu.SemaphoreType.DMA((na,))],
    )(*arrays)


def _peer(k):
    x, y, c = _place()
    return (1 - x if k & 4 else x, 1 - y if k & 2 else y, 1 - c if k & 1 else c)


def _remote(src, dst, sems, s, to):
    return pltpu.make_async_remote_copy(src_ref=src, dst_ref=dst, send_sem=sems[0].at[s], recv_sem=sems[1].at[s],
                                        device_id=to, device_id_type=MESH)


def _gather_rider(shards):
    na = len(shards)

    def plan(ins, outs, *sems):
        x, y, c = _place()
        me = 4 * x + 2 * y + c
        locs = [pltpu.make_async_copy(ins[a], outs[a].at[me], sems[2].at[a]) for a in range(na)]
        sends, recvs = [], []
        for k in range(1, N_DEV):
            px, py, pc = _peer(k)
            for a in range(na):
                s = (k - 1) * na + a
                sends.append(_remote(ins[a], outs[a].at[me], sems, s, (px, py, pc)))
                recvs.append(_remote(ins[a], outs[a].at[4 * px + 2 * py + pc], sems, s, (px, py, pc)))
        return sends, recvs, locs

    return _Rider(shards, [_sds((N_DEV,) + a.shape, a.dtype) for a in shards], (N_DEV - 1) * na, na, plan)


def _direct_exchange_rider(blocks):
    na = len(blocks)

    def plan(ins, outs, *sems):
        x, y, c = _place()
        me = 4 * x + 2 * y + c
        locs = [pltpu.make_async_copy(ins[a].at[c, 2 * x + y], outs[a].at[me], sems[2].at[a]) for a in range(na)]
        sends, recvs = [], []
        for k in range(1, N_DEV):
            px, py, pc = _peer(k)
            for a in range(na):
                s = (k - 1) * na + a
                sends.append(_remote(ins[a].at[pc, 2 * px + py], outs[a].at[me], sems, s, (px, py, pc)))
                recvs.append(_remote(ins[a].at[pc, 2 * px + py], outs[a].at[4 * px + 2 * py + pc], sems, s, (px, py, pc)))
        return sends, recvs, locs

    return _Rider(blocks, [_sds((N_DEV,) + a.shape[2:], a.dtype) for a in blocks], (N_DEV - 1) * na, na, plan)


def _swap_rider(halves):
    na = len(halves)

    def plan(ins, outs, *sems):
        x, y, c = _place()
        cps = [_remote(ins[a].at[1 - c], outs[a], sems, a, (x, y, 1 - c)) for a in range(na)]
        return cps, cps, []

    return _Rider(halves, [_sds(a.shape[1:], a.dtype) for a in halves], na, 1, plan)


def _chip_exchange_rider(parts, small=None):
    na = len(parts)
    n_chip = N_DEV // 2

    def plan(ins, outs, *sems):
        x, y, c = _place()
        chip = 2 * x + y
        locs = [pltpu.make_async_copy(ins[a].at[chip], outs[a].at[chip], sems[2].at[a]) for a in range(na)]
        sends, recvs = [], []
        for k in range(1, n_chip):
            px, py, _ = _peer(2 * k)
            for a in range(na):
                s = (k - 1) * na + a
                sends.append(_remote(ins[a].at[2 * px + py], outs[a].at[chip], sems, s, (px, py, c)))
                recvs.append(_remote(ins[a].at[2 * px + py], outs[a].at[2 * px + py], sems, s, (px, py, c)))
        if small is not None:
            me = 2 * chip + c
            locs.append(pltpu.make_async_copy(ins[na], outs[na].at[me], sems[2].at[na]))
            for k in range(1, N_DEV):
                px, py, pc = _peer(k)
                s = (n_chip - 1) * na + k - 1
                sends.append(_remote(ins[na], outs[na].at[me], sems, s, (px, py, pc)))
                recvs.append(_remote(ins[na], outs[na].at[4 * px + 2 * py + pc], sems, s, (px, py, pc)))
        return sends, recvs, locs

    extra = [] if small is None else [small]
    shapes = [_sds(a.shape, a.dtype) for a in parts] + [_sds((N_DEV,) + s.shape, s.dtype) for s in extra]
    n_sems = (n_chip - 1) * na + (N_DEV - 1) * len(extra)
    return _Rider(list(parts) + extra, shapes, n_sems, na + len(extra), plan)


def _pair_add(halves, other, core, name):
    _, nch, r, c = halves.shape

    def body(c_ref, h_ref, o_ref, p_ref):
        p_ref[...] = (h_ref[...].astype(F32) + o_ref[...].astype(F32)).astype(BF16)

    blk = pl.BlockSpec((None, r, c), lambda j, c_ref: (j, 0, 0))
    return _pc(
        body, name,
        grid_spec=pltpu.PrefetchScalarGridSpec(
            num_scalar_prefetch=1, grid=(nch,),
            in_specs=[pl.BlockSpec((None, None, r, c), lambda j, c_ref: (c_ref[0], j, 0, 0)), blk], out_specs=blk),
        out_shape=_sds((nch, r, c), BF16),
        compiler_params=_params(),
    )(core, halves, other)


def _sum_adamw(parts, w, m, v, name, rider=None):
    nl, r, c = w.shape
    tr = 256 if r % 256 == 0 else r

    def body(*refs):
        p_refs = refs[:nl]
        w_ref, m_ref, v_ref, g_ref, d_ref, mo_ref, vo_ref = refs[nl:]
        for l in range(nl):
            @pl.when(pl.program_id(0) == l)
            def _(p_ref=p_refs[l]):
                g = p_ref[0].astype(F32)
                for j in range(1, p_ref.shape[0]):
                    g = g + p_ref[j].astype(F32)
                mn = ADAM_B1 * m_ref[...] + (1.0 - ADAM_B1) * g
                vn = ADAM_B2 * v_ref[...] + (1.0 - ADAM_B2) * (g * g)
                m_hat = mn / (1.0 - ADAM_B1 ** ADAM_STEP)
                v_hat = vn / (1.0 - ADAM_B2 ** ADAM_STEP)
                g_ref[...] = g
                d_ref[...] = -ADAM_LR * (m_hat / (jnp.sqrt(v_hat) + ADAM_EPS) + ADAM_WD * w_ref[...])
                mo_ref[...] = mn
                vo_ref[...] = vn

    def part_spec(l, k):
        return pl.BlockSpec((k, tr, c), lambda li, i: (0, jnp.where(li == l, i, 0), 0))

    row = pl.BlockSpec((None, tr, c), lambda li, i: (li, i, 0))
    return _call(
        body, name, (*parts, w, m, v), rider, grid=(nl, r // tr),
        in_specs=[part_spec(l, p.shape[0]) for l, p in enumerate(parts)] + [row, row, row],
        out_specs=[row] * 4,
        out_shape=[_sds((nl, r, c), F32)] * 4,
        compiler_params=_params(),
    )


SMALL = ("lower_bounds", "pre_norm_g", "hgrn_norm_g", "pool_w", "pool_scale", "post_norm_g", "fox_f_bias")
SMALL_LANES = 128


def _small_size(tree):
    return sum(tree[k].size for k in SMALL)


def _pack_small(tree, extra=None):
    flat = jnp.concatenate([tree[k].reshape(-1) for k in SMALL] + ([] if extra is None else [extra.reshape(1)]))
    rows = -(-(_small_size(tree) + 1) // (8 * SMALL_LANES)) * 8
    return jnp.pad(flat, (0, rows * SMALL_LANES - flat.shape[0])).reshape(rows, SMALL_LANES)


def _unpack_small(packed, like):
    out, off = {}, 0
    for k in SMALL:
        size = like[k].size
        assert off % SMALL_LANES == 0
        rows = packed[off // SMALL_LANES:-(-(off + size) // SMALL_LANES)]
        out[k] = rows.reshape(-1)[:size].reshape(like[k].shape)
        off += size
    return out


def _block_diag(pw):
    g = pw.shape[0]
    eye = jnp.eye(g, dtype=pw.dtype)
    return (eye[:, None, :, None] * pw[:, :, None, :]).reshape(g * HEAD, g * HEAD)


def _assemble_w_in(g_in, name):
    _, d, shard = g_in.shape
    tr = min(256, d)
    wide = MAIN_W + FC_PAD

    def body(g_ref, wm_ref, wf_ref, wmt_ref, wft_ref, row_s):
        row_s[:, MAIN_W:] = jnp.zeros((tr, FC_PAD), F32)
        for j in range(N_DEV):
            row_s[:, j * shard:(j + 1) * shard] = g_ref[j].astype(F32)
        wm_ref[...] = row_s[:, :MAIN_W].astype(BF16)
        wf_ref[...] = row_s[:, MAIN_W:].astype(BF16)
        for j in range(0, MAIN_W, 512):
            wmt_ref[j:j + 512, :] = row_s[:, j:j + 512].T.astype(BF16)
        wft_ref[...] = row_s[:, MAIN_W:].T.astype(BF16)

    return _pc(
        body, name, grid=(d // tr,),
        in_specs=[pl.BlockSpec((N_DEV, tr, shard), lambda i: (0, i, 0))],
        out_specs=[pl.BlockSpec((tr, MAIN_W), lambda i: (i, 0)), pl.BlockSpec((tr, FC_PAD), lambda i: (i, 0)),
                   pl.BlockSpec((MAIN_W, tr), lambda i: (0, i)), pl.BlockSpec((FC_PAD, tr), lambda i: (0, i))],
        out_shape=[_sds((d, MAIN_W), BF16), _sds((d, FC_PAD), BF16), _sds((MAIN_W, d), BF16), _sds((FC_PAD, d), BF16)],
        scratch_shapes=[pltpu.VMEM((tr, wide), F32)],
        compiler_params=_params(),
    )(g_in)


def _w_out_parts(g_out):
    full_out = g_out.reshape(N_DEV * g_out.shape[1], g_out.shape[2])
    return full_out, full_out.T


def _layer_fwd(l, x, lbs, weights, lw, nb, rider_h=None, rider_c=None, target=None):
    n = x.shape[0]
    t = n // nb
    w_main, w_fc, _, _, w_out, _ = lw
    bias = jnp.pad(weights["fox_f_bias"][l:l + 1], ((0, 0), (0, FC_PAD - FOX_HEADS)))
    wbd = _block_diag(weights["pool_w"][l]).astype(BF16)
    proj, fc, ht, qt, kt = _in_proj_fwd(x, weights["pre_norm_g"][l:l + 1], w_main, w_fc, f"in_proj_fwd_{l}")
    c_col, c_row = _fox_decay_fwd(fc, bias, nb, f"fox_decay_fwd_{l}")
    (o_h, s0), rode_h = _hgrn_fwd(proj, lbs[l:l + 1], _block_ones(HGRN_W, BF16), nb, f"hgrn_fwd_{l}", rider_h)
    o_b = _pool_fwd(proj, wbd, weights["pool_scale"][l:l + 1], nb, f"pool_fwd_{l}")
    (o_c, lse), rode_c = _fox_fwd(proj, kt, c_col, c_row, nb, f"fox_fwd_{l}", rider_c)
    if w_out is None:
        lw = tuple(lw[:4]) + _w_out_parts(rode_h[0])
        w_out, rode_h = lw[4], rode_h[1:]
    x_next, mixt, y = _merge_fwd(x, proj, o_h, o_b, o_c, weights["hgrn_norm_g"][l:l + 1], w_out,
                                 weights["post_norm_g"][l:l + 1], f"merge_fwd_{l}", target)
    return x_next, (x, proj, fc, ht, qt, kt, c_col, c_row, o_h, s0, o_c, lse, mixt, y, bias, wbd), lw, (rode_h, rode_c)


def _layer_bwd(l, dx, saved, lbs, weights, lw, nb, rider=None, send_w_out=None):
    _, proj, fc, ht, qt, kt, c_col, c_row, o_h, s0, o_c, lse, mixt, y, bias, wbd = saved
    w_out_t = lw[5]
    g = {}
    dy, dmix, dgp, da, dat, d_gc, delta = _merge_bwd(dx, y, weights["post_norm_g"][l:l + 1], w_out_t, proj, o_c,
                                                nb, f"merge_bwd_{l}")
    g["post_norm_g"] = dgp[0]
    g["w_out"] = _w_out_grad(mixt, dy, f"w_out_grad_{l}")
    (d_a, dgh, dlb), arrived = _hgrn_bwd(dmix, proj, o_h, s0, weights["hgrn_norm_g"][l:l + 1], lbs[l:l + 1],
                                         _block_ones(HGRN_W, BF16), nb, f"hgrn_bwd_{l}",
                                         None if send_w_out is None else send_w_out([g["w_out"]]))
    if arrived is not None:
        g["w_out_received"] = arrived[0]
    g["hgrn_norm_g"], g["lbs"] = dgh[0], dlb[0]
    d_b, dwbd, dps = _pool_bwd(dmix, proj, wbd, wbd.T, weights["pool_scale"][l:l + 1], nb, f"pool_bwd_{l}")
    g["pool_w"] = jnp.stack([dwbd[j * HEAD:(j + 1) * HEAD, j * HEAD:(j + 1) * HEAD] for j in range(len(POOL_WINDOWS))])
    g["pool_scale"] = dps[0]
    (d_qc, d_kc, d_vc, dc_k, dc_q), rode = _fox_bwd(proj, qt, kt, da, dat, c_col, c_row, lse, delta, nb,
                                                    f"fox_bwd_{l}", rider)
    d_fc, dbias = _fox_decay_bwd(dc_q, dc_k, fc, bias, nb, f"fox_decay_bwd_{l}")
    g["fox_f_bias"] = dbias[0, :FOX_HEADS]
    pieces = [(d_a, C_QA), (d_b, C_UB), (d_qc, C_QC), (d_kc, C_KC), (d_vc, C_VC), (d_gc, C_GC), (d_fc, None)]
    g["w_in"] = _w_in_grad(ht, pieces, f"w_in_grad_{l}")
    return g, pieces, rode


def _layer_bwd_input(l, dx, pieces, saved, weights, lw, rider=None):
    (dxi, dgpre), rode = _in_proj_bwd(pieces, lw[2], lw[3], saved[0], weights["pre_norm_g"][l:l + 1], dx,
                                      f"in_proj_bwd_{l}", rider)
    return dxi, dgpre[0], rode


def kernel(x, lower_bounds, pre_norm_g, w_in, hgrn_norm_g, fox_f_bias, pool_w, pool_scale, w_out, post_norm_g, loss_target, m_lower_bounds, m_pre_norm_g, m_w_in, m_hgrn_norm_g, m_fox_f_bias, m_pool_w, m_pool_scale, m_w_out, m_post_norm_g, v_lower_bounds, v_pre_norm_g, v_w_in, v_hgrn_norm_g, v_fox_f_bias, v_pool_w, v_pool_scale, v_w_out, v_post_norm_g):
    weights = dict(lower_bounds=lower_bounds, pre_norm_g=pre_norm_g, hgrn_norm_g=hgrn_norm_g, fox_f_bias=fox_f_bias,
                   pool_w=pool_w, pool_scale=pool_scale, post_norm_g=post_norm_g)
    mom_m = dict(lower_bounds=m_lower_bounds, pre_norm_g=m_pre_norm_g, hgrn_norm_g=m_hgrn_norm_g, fox_f_bias=m_fox_f_bias,
                 pool_w=m_pool_w, pool_scale=m_pool_scale, post_norm_g=m_post_norm_g)
    mom_v = dict(lower_bounds=v_lower_bounds, pre_norm_g=v_pre_norm_g, hgrn_norm_g=v_hgrn_norm_g, fox_f_bias=v_fox_f_bias,
                 pool_w=v_pool_w, pool_scale=v_pool_scale, post_norm_g=v_post_norm_g)
    depth = w_in.shape[0]
    nb, t, d = x.shape
    n = nb * t
    core = lax.axis_index("c").astype(jnp.int32).reshape(1)
    shards = [(w_in[l].astype(BF16), w_out[l].astype(BF16)) for l in range(depth)]
    lbs = _lower_bound_table(lower_bounds, "lower_bound_table")

    (g_in,) = _gather_weights(shards[0][0])
    coming = tuple(_assemble_w_in(g_in, "assemble_w_in_0")) + (None, None)
    xl, saved, lw = x.reshape(n, d), [], []
    for l in range(depth):
        last = l + 1 == depth
        ride_h = ([shards[l][1]] if coming[4] is None else []) + ([] if last else [shards[l + 1][1]])
        xl, sv, lw_l, (rode_h, rode_c) = _layer_fwd(
            l, xl, lbs, weights, coming, nb, _gather_rider(ride_h) if ride_h else None,
            None if last else _gather_rider([shards[l + 1][0]]), loss_target.reshape(n, d) if last else None)
        saved.append(sv)
        lw.append(lw_l)
        if not last:
            coming = tuple(_assemble_w_in(rode_c[0], f"assemble_w_in_{l + 1}")) + _w_out_parts(rode_h[0])
    dx, sq = xl
    loss_here = 0.5 * jnp.sum(sq) / d

    grads, recv_in, pending = [None] * depth, [None] * depth, None
    for l in reversed(range(depth)):
        g, pieces, rode = _layer_bwd(l, dx, saved[l], lbs, weights, lw[l], nb, pending, _direct_exchange_rider)
        if rode is not None:
            recv_in[l + 1] = rode[0]
        if l > 0:
            pending = _direct_exchange_rider([g["w_in"]])
            dx, g["pre_norm_g"], _ = _layer_bwd_input(l, dx, pieces, saved[l], weights, lw[l])
        else:
            (other,) = _run_rider(_swap_rider([g["w_in"]]), "grad_swap")
            summed = _pair_add(g["w_in"], other, core, "grad_pair_add")
            dx, g["pre_norm_g"], (recv_in[l],) = _layer_bwd_input(l, dx, pieces, saved[l], weights, lw[l],
                                                                  _chip_exchange_rider([summed]))
        grads[l] = g
    small = {k: jnp.stack([grads[l][k] for l in range(depth)]) for k in SMALL if k != "lower_bounds"}
    small["lower_bounds"] = _lower_bound_bwd(lower_bounds, jnp.stack([grads[l]["lbs"] for l in range(depth)]),
                                             "lower_bound_bwd")
    (r_small,) = _run_rider(_gather_rider([_pack_small(small, loss_here)]), "small_grads_gather")

    res_in, _ = _sum_adamw(recv_in, w_in, m_w_in, v_w_in, "adamw_w_in")
    res_out, _ = _sum_adamw([grads[l]["w_out_received"] for l in range(depth)], w_out, m_w_out, v_w_out, "adamw_w_out")
    res_small, _ = _sum_adamw([r_small], _pack_small(weights)[None], _pack_small(mom_m)[None], _pack_small(mom_v)[None],
                              "adamw_small")
    loss = res_small[0][0][_small_size(weights) // SMALL_LANES, _small_size(weights) % SMALL_LANES]

    names = ("lower_bounds", "pre_norm_g", "w_in", "hgrn_norm_g", "fox_f_bias", "pool_w", "pool_scale", "w_out", "post_norm_g")
    outs = [loss, dx.reshape(nb, t, d)]
    for i in range(4):
        full = dict(_unpack_small(res_small[i][0], weights), w_in=res_in[i], w_out=res_out[i])
        outs += [full[k] for k in names]
    return tuple(outs)
```

```python
import functools

import jax
import jax.numpy as jnp
from jax import lax
from jax.experimental import pallas as pl
from jax.experimental.pallas import tpu as pltpu

F32, BF16 = jnp.float32, jnp.bfloat16
MESH = pl.DeviceIdType.MESH
N_DEV = 8

NORM_EPS = 1e-6
MASK_VALUE = -1e30
TINY = 1e-30
CHUNK = 64
SUB = 16
HGRN_W, POOL_W, FOX_W = 256, 256, 512
HEAD = 64
FOX_HEADS = 8
POOL_WINDOWS = (2, 4, 8, 16)
POOL_HALO = 16
MAIN_W = 3584
FC_PAD = 128
C_QA, C_FA, C_IA, C_GA, C_UB, C_GB, C_QC, C_KC, C_VC, C_GC = 0, 256, 512, 768, 1024, 1280, 1536, 2048, 2560, 3072
FOX_SCALE = HEAD ** -0.5

ADAM_LR, ADAM_B1, ADAM_B2, ADAM_EPS, ADAM_WD, ADAM_STEP = 0.001, 0.9, 0.999, 1e-08, 0.01, 10

VMEM_LIMIT = 56 * 1024 * 1024


def _pc(fn, name, **kw):
    return pl.pallas_call(fn, name=name, **kw)


def _params(**kw):
    return pltpu.CompilerParams(vmem_limit_bytes=VMEM_LIMIT, **kw)


class _Rider:
    def __init__(self, inputs, out_shapes, n_sems, n_local, plan):
        self.inputs, self.out_shapes, self.n_sems, self.n_local, self.plan = list(inputs), list(out_shapes), n_sems, n_local, plan

    def start(self, ins, outs, *sems):
        sends, _, locs = self.plan(ins, outs, *sems)
        for cp in locs + sends:
            cp.start()

    def wait(self, ins, outs, *sems):
        sends, recvs, locs = self.plan(ins, outs, *sems)
        for cp in recvs:
            cp.wait_recv()
        for cp in sends:
            cp.wait_send()
        for cp in locs:
            cp.wait()

    def sem_shapes(self):
        return [pltpu.SemaphoreType.DMA((self.n_sems,)), pltpu.SemaphoreType.DMA((self.n_sems,)),
                pltpu.SemaphoreType.DMA((self.n_local,))]


def _call(body, name, args, rider=None, *, grid, in_specs, out_specs, out_shape, scratch_shapes=(), **kw):
    if rider is None:
        res = _pc(body, name, grid=grid, in_specs=in_specs, out_specs=out_specs, out_shape=out_shape,
                  scratch_shapes=list(scratch_shapes), **kw)(*args)
        return res, None
    n_in, n_out, n_scr = len(in_specs), len(out_specs), len(scratch_shapes)
    n_rin, n_rout = len(rider.inputs), len(rider.out_shapes)

    def ridden(*refs):
        ins, refs = refs[:n_in], refs[n_in:]
        rins, refs = refs[:n_rin], refs[n_rin:]
        outs, refs = refs[:n_out], refs[n_out:]
        routs, refs = refs[:n_rout], refs[n_rout:]
        scr, sems = refs[:n_scr], refs[n_scr:]
        first = functools.reduce(jnp.logical_and, [pl.program_id(a) == 0 for a in range(len(grid))])
        last = functools.reduce(jnp.logical_and, [pl.program_id(a) == g - 1 for a, g in enumerate(grid)])

        @pl.when(first)
        def _():
            rider.start(rins, routs, *sems)

        body(*ins, *outs, *scr)

        @pl.when(last)
        def _():
            rider.wait(rins, routs, *sems)

    any_spec = pl.BlockSpec(memory_space=pl.ANY)
    res = _pc(ridden, name, grid=grid, in_specs=list(in_specs) + [any_spec] * n_rin,
              out_specs=list(out_specs) + [any_spec] * n_rout, out_shape=list(out_shape) + rider.out_shapes,
              scratch_shapes=list(scratch_shapes) + rider.sem_shapes(), **kw)(*args, *rider.inputs)
    return res[:n_out], res[n_out:]


def _run_rider(rider, name):
    n_rin = len(rider.inputs)

    def body(*refs):
        ins, outs, sems = refs[:n_rin], refs[n_rin:n_rin + len(rider.out_shapes)], refs[n_rin + len(rider.out_shapes):]
        rider.start(ins, outs, *sems)
        rider.wait(ins, outs, *sems)

    any_spec = pl.BlockSpec(memory_space=pl.ANY)
    return _pc(body, name, in_specs=[any_spec] * n_rin, out_specs=[any_spec] * len(rider.out_shapes),
               out_shape=rider.out_shapes, scratch_shapes=rider.sem_shapes())(*rider.inputs)


def _dot(a, b):
    return jnp.dot(a, b, preferred_element_type=F32)


def _dot_nt(a, b):
    return lax.dot_general(a, b, (((1,), (1,)), ((), ())), preferred_element_type=F32)


def _dot_tn(a, b):
    return lax.dot_general(a, b, (((0,), (0,)), ((), ())), preferred_element_type=F32)


def _sel_dot_exact(sel, x):
    hi = x.astype(BF16)
    rest = x - hi.astype(F32)
    mid = rest.astype(BF16)
    lo = (rest - mid.astype(F32)).astype(BF16)
    sb = sel.astype(BF16)
    return _dot(sb, hi) + _dot(sb, mid) + _dot(sb, lo)


def _split2(x):
    hi = x.astype(BF16)
    return hi, (x - hi.astype(F32)).astype(BF16)


def _sel_dot(sel, x):
    hi, lo = _split2(x)
    sb = sel.astype(BF16)
    return _dot(sb, hi) + _dot(sb, lo)


def _dot_sel(x, sel):
    hi, lo = _split2(x)
    sb = sel.astype(BF16)
    return _dot(hi, sb) + _dot(lo, sb)


def _sigmoid(x):
    return 1.0 / (1.0 + jnp.exp(-x))


def _block_ones(n, dtype):
    r = lax.broadcasted_iota(jnp.int32, (n, n), 0) // HEAD
    c = lax.broadcasted_iota(jnp.int32, (n, n), 1) // HEAD
    return (r == c).astype(dtype)


def _sds(shape, dtype):
    return jax.ShapeDtypeStruct(shape, dtype)


def _in_proj_fwd(x, g_pre, w_main, w_fc, name):
    n, d = x.shape
    tm = min(512, n)

    def body(x_ref, g_ref, w_ref, wf_ref, proj_ref, fc_ref, ht_ref, qt_ref, kt_ref):
        xv = x_ref[...]
        r = lax.rsqrt(jnp.mean(xv * xv, axis=-1, keepdims=True) + NORM_EPS)
        hf = xv * r * g_ref[...]
        hb = hf.astype(BF16)
        ht_ref[...] = hf.T.astype(BF16)
        for j in range(0, MAIN_W, FOX_W):
            res = _dot(hb, w_ref[:, j:j + FOX_W])
            proj_ref[:, j:j + FOX_W] = res
            if j == C_QC:
                qt_ref[...] = (res * FOX_SCALE).T.astype(BF16)
            if j == C_KC:
                kt_ref[...] = res.T.astype(BF16)
        fc_ref[...] = _dot(hb, wf_ref[...])

    def cols(rows):
        return pl.BlockSpec((rows, tm), lambda i: (0, i))

    return _pc(
        body, name, grid=(n // tm,),
        in_specs=[pl.BlockSpec((tm, d), lambda i: (i, 0)), pl.BlockSpec((1, d), lambda i: (0, 0)),
                  pl.BlockSpec((d, MAIN_W), lambda i: (0, 0)), pl.BlockSpec((d, FC_PAD), lambda i: (0, 0))],
        out_specs=[pl.BlockSpec((tm, MAIN_W), lambda i: (i, 0)), pl.BlockSpec((tm, FC_PAD), lambda i: (i, 0)),
                   cols(d), cols(FOX_W), cols(FOX_W)],
        out_shape=[_sds((n, MAIN_W), F32), _sds((n, FC_PAD), F32), _sds((d, n), BF16), _sds((FOX_W, n), BF16),
                   _sds((FOX_W, n), BF16)],
        compiler_params=_params(),
    )(x, g_pre, w_main, w_fc)


def _fox_decay_fwd(fc, bias, nb, name):
    n = fc.shape[0]
    t = n // nb
    tt = min(256, t)
    nt = t // tt

    def body(fc_ref, b_ref, c_ref, cr_ref):
        r = lax.broadcasted_iota(jnp.int32, (tt, tt), 0)
        cc = lax.broadcasted_iota(jnp.int32, (tt, tt), 1)
        carry = jnp.zeros((1, FC_PAD), F32)
        for i in range(nt):
            rows = slice(i * tt, (i + 1) * tt)
            xv = fc_ref[rows, :] + b_ref[...]
            lf = jnp.minimum(xv, 0.0) - jnp.log(1.0 + jnp.exp(-jnp.abs(xv)))
            cs = _sel_dot_exact(r >= cc, lf) + carry
            c_ref[rows, :] = cs
            cr_ref[:, rows] = cs.T[:FOX_HEADS, :]
            carry = cs[tt - 1:tt, :]

    return _pc(
        body, name, grid=(nb,),
        in_specs=[pl.BlockSpec((t, FC_PAD), lambda b: (b, 0)), pl.BlockSpec((1, FC_PAD), lambda b: (0, 0))],
        out_specs=[pl.BlockSpec((t, FC_PAD), lambda b: (b, 0)), pl.BlockSpec((None, FOX_HEADS, t), lambda b: (b, 0, 0))],
        out_shape=[_sds((n, FC_PAD), F32), _sds((nb, FOX_HEADS, t), F32)],
        compiler_params=_params(),
    )(fc, bias)


def _hgrn_gates(q, z, lb):
    sig = _sigmoid(z)
    sn = _sigmoid(-z)
    f = lb + (1.0 - lb) * sig
    g = jnp.log(jnp.maximum(f, TINY))
    k = (1.0 - lb) * sn
    sq = _sigmoid(q)
    return sig, sn, f, g, k, sq


def _sub_tri(n, lower):
    r = lax.broadcasted_iota(jnp.int32, (n, n), 0)
    c = lax.broadcasted_iota(jnp.int32, (n, n), 1)
    tri = (r >= c) if lower else (r <= c)
    return jnp.logical_and(r // SUB == c // SUB, tri).astype(F32)


def _live_rows(t):
    return 8 * (t // 8 + 1)


def _pad_rows(x):
    return x if x.shape[0] == SUB else jnp.concatenate([x, jnp.zeros((SUB - x.shape[0], x.shape[1]), x.dtype)], axis=0)


def _hgrn_decays(qs, k, b):
    srow = lax.broadcasted_iota(jnp.int32, (SUB, HGRN_W), 0)
    es, ws = [], []
    for t in range(SUB):
        r = _live_rows(t)
        e = jnp.where(srow[:r] <= t, jnp.exp(b[t:t + 1, :] - b[:r]), 0.0)
        es.append(e)
        ws.append(_pad_rows(e * (qs[t:t + 1, :] * k[:r])))
    return srow, es, ws


def _hgrn_state_step(st, k, v, b, bmask):
    bl = b[SUB - 1:SUB, :]
    ktil = k * jnp.exp(bl - b)
    return st * jnp.exp(bl) + _dot_tn(v.astype(BF16), ktil.astype(BF16)) * bmask


def _hgrn_tile(t):
    return min(256, t)


def _hgrn_fwd(proj, lb, ones_b, nb, name, rider=None):
    n = proj.shape[0]
    t = n // nb
    tt = _hgrn_tile(t)
    nt = t // tt
    ncs = tt // CHUNK
    w = HGRN_W

    def body(q_ref, z_ref, v_ref, lb_ref, ones_ref, o_ref, s0_ref, st_s, b_s, qs_s, k_s):
        @pl.when(pl.program_id(0) == 0)
        def _():
            st_s[...] = jnp.zeros_like(st_s)

        for e in range(nb):
            q = q_ref[e]
            _, _, _, g, k, sq = _hgrn_gates(q, z_ref[e], lb_ref[...])
            b_s[e] = _sel_dot(_sub_tri(tt, True), g)
            qs_s[e] = q * sq
            k_s[e] = k
        bmask = _block_ones(w, F32)
        ones_b = ones_ref[...]
        nsub = CHUNK // SUB

        def chunk(c, carry):
            sts = [st_s[e] for e in range(nb)]
            for e in range(nb):
                s0_ref[e, c] = sts[e]
            base = pl.multiple_of(c * CHUNK, CHUNK)
            tiles = [[(qs_s[e, pl.ds(base + u * SUB, SUB), :], k_s[e, pl.ds(base + u * SUB, SUB), :],
                       v_ref[e, pl.ds(base + u * SUB, SUB), :], b_s[e, pl.ds(base + u * SUB, SUB), :])
                      for u in range(nsub)] for e in range(nb)]
            aexps = [[] for _ in range(nb)]
            for e in range(nb):
                for qs, k, v, b in tiles[e]:
                    _, _, ws = _hgrn_decays(qs, k, b)
                    aexps[e].append(_dot(jnp.concatenate(ws, axis=0).astype(BF16), ones_b))
            inters = [[] for _ in range(nb)]
            for u in range(nsub):
                for e in range(nb):
                    qs, k, v, b = tiles[e][u]
                    inters[e].append(_dot_nt((qs * jnp.exp(b)).astype(BF16), sts[e].astype(BF16)))
                    sts[e] = _hgrn_state_step(sts[e], k, v, b, bmask)
            for e in range(nb):
                st_s[e] = sts[e]
            for e in range(nb):
                for u, ((qs, k, v, b), aexp, o) in enumerate(zip(tiles[e], aexps[e], inters[e])):
                    for t in range(SUB):
                        r = _live_rows(t)
                        row = o[t:t + 1, :] + jnp.sum(aexp[t * SUB:t * SUB + r, :] * v[:r], axis=0, keepdims=True)
                        o_ref[e, pl.ds(base + u * SUB + t, 1), :] = row
            return carry

        lax.fori_loop(0, ncs, chunk, 0)

    def col(j):
        return pl.BlockSpec((nb, tt, w), lambda i: (0, i, j))

    proj3 = proj.reshape(nb, t, proj.shape[1])
    (o, s0), rode = _call(
        body, name, (proj3, proj3, proj3, lb, ones_b), rider, grid=(nt,),
        in_specs=[col(C_QA // w), col(C_FA // w), col(C_IA // w), pl.BlockSpec((1, w), lambda i: (0, 0)),
                  pl.BlockSpec((w, w), lambda i: (0, 0))],
        out_specs=[pl.BlockSpec((nb, tt, w), lambda i: (0, i, 0)),
                   pl.BlockSpec((nb, ncs, w, w), lambda i: (0, i, 0, 0))],
        out_shape=[_sds((nb, t, w), F32), _sds((nb, t // CHUNK, w, w), F32)],
        scratch_shapes=[pltpu.VMEM((nb, w, w), F32)] + [pltpu.VMEM((nb, tt, w), F32)] * 3,
        compiler_params=_params(),
    )
    return (o.reshape(n, w), s0.reshape(n // CHUNK, w, w)), rode


def _pool_lane_windows():
    lane = lax.broadcasted_iota(jnp.int32, (1, POOL_W), 1) // HEAD
    wl = jnp.zeros((1, POOL_W), F32)
    for gi, win in enumerate(POOL_WINDOWS):
        wl = jnp.where(lane == gi, float(win), wl)
    return lane, wl


def _pool_select(lane, parts):
    out = parts[-1]
    for gi in range(len(parts) - 2, -1, -1):
        out = jnp.where(lane == gi, parts[gi], out)
    return out


def _pool_mix(u, halo, t0, tt):
    lane, wl = _pool_lane_windows()
    ext = jnp.concatenate([halo, u], axis=0)
    sums, cur, shift = [], ext, 1
    for _ in POOL_WINDOWS:
        cur = cur + pltpu.roll(cur, shift, axis=0)
        sums.append(cur[POOL_HALO:, :])
        shift *= 2
    tpos = (t0 + lax.broadcasted_iota(jnp.int32, (tt, POOL_W), 0)).astype(F32)
    cnt = jnp.minimum(tpos + 1.0, wl)
    return _pool_select(lane, sums) / cnt - u, cnt


def _pool_specs(tt, nt, nhb):
    cu, cg = C_UB // POOL_W, C_GB // POOL_W
    return [pl.BlockSpec((tt, POOL_W), lambda b, i: (b * nt + i, cu)),
            pl.BlockSpec((tt, POOL_W), lambda b, i: (b * nt + i, cg)),
            pl.BlockSpec((POOL_HALO, POOL_W), lambda b, i: (jnp.maximum((b * nt + i) * nhb - 1, 0), cu))]


def _rows_reduce(x, op, final):
    while x.shape[0] > 8 and x.shape[0] % 16 == 0:
        half = x.shape[0] // 2
        x = op(x[:half], x[half:])
    return final(x, axis=0, keepdims=True)


def _tri_pair(step, n, group=1):
    counts = [a // group + 1 for a in range(n)]
    firsts = [sum(counts[:a]) for a in range(1, n)]
    a = sum([(step >= f).astype(jnp.int32) for f in firsts], jnp.int32(0))
    first = sum([jnp.where(step >= f, c, 0) for f, c in zip(firsts, counts)], jnp.int32(0))
    return a, step - first


def _tri_steps(n, group=1):
    return sum(a // group + 1 for a in range(n))


def _lane_lo():
    return lax.broadcasted_iota(jnp.int32, (1, 2 * HEAD), 1) < HEAD


def _put_col(tile, hh, colv):
    lane = lax.broadcasted_iota(jnp.int32, tile.shape, 1)
    return jnp.where(lane == hh, colv, tile)


def _fox_fwd(proj, kt, c_col, c_row, nb, name, rider=None):
    n = proj.shape[0]
    t = n // nb
    tb = min(256, t)
    nq = t // tb
    pw = 2 * HEAD
    kw = 2 if nq % 2 == 0 else 1
    nk = nq // kw

    def body(q_ref, kt_ref, v_ref, cc_ref, cr_ref, o_ref, lse_ref, m_s, acc_s, cq_s):
        qi, kk = _tri_pair(pl.program_id(1), nq, kw)

        @pl.when(kk == 0)
        def _():
            m_s[...] = jnp.full_like(m_s, -jnp.inf)
            acc_s[...] = jnp.zeros_like(acc_s)
            for hh in range(FOX_HEADS):
                cq_s[hh] = jnp.broadcast_to(cc_ref[:, hh:hh + 1], (tb, pw))

        def block(masked, sub):
            lo = _lane_lo()
            keys = slice(sub * tb, (sub + 1) * tb)
            if masked:
                causal = lax.broadcasted_iota(jnp.int32, (tb, tb), 0) >= lax.broadcasted_iota(jnp.int32, (tb, tb), 1)
            def lanes(hh):
                return lo if hh % 2 == 0 else jnp.logical_not(lo)

            def scores(hh):
                sl = slice((hh // 2) * pw, (hh // 2 + 1) * pw)
                return _dot(jnp.where(lanes(hh), q_ref[:, sl] * FOX_SCALE, 0.0).astype(BF16), kt_ref[sl, keys])

            ahead = scores(0)
            for hh in range(FOX_HEADS):
                s = ahead
                if hh + 1 < FOX_HEADS:
                    ahead = scores(hh + 1)
                s = s + (jnp.tile(cq_s[hh], (1, tb // pw)) - cr_ref[hh:hh + 1, keys])
                if masked:
                    s = jnp.where(causal, s, MASK_VALUE)
                m_prev = m_s[hh]
                m_new = jnp.maximum(m_prev, jnp.max(s, axis=1, keepdims=True))
                alpha = jnp.exp(m_prev - m_new)
                pe = jnp.exp(s - jnp.tile(m_new, (1, tb // pw)))
                m_s[hh] = m_new
                vf = v_ref[keys, (hh // 2) * pw:(hh // 2 + 1) * pw]
                acc_s[hh] = alpha * acc_s[hh] + _dot(pe.astype(BF16), jnp.where(lanes(hh), vf, 1.0).astype(BF16))

        def finish():
            lo = _lane_lo()
            m_all, l_all = jnp.zeros((tb, FC_PAD), F32), jnp.ones((tb, FC_PAD), F32)
            for p in range(FOX_HEADS // 2):
                a0, a1 = acc_s[2 * p], acc_s[2 * p + 1]
                both = pltpu.roll(jnp.where(lo, a1, a0), HEAD, axis=1)
                o_ref[:, p * pw:(p + 1) * pw] = jnp.where(lo, a0, a1) / both
                m_all = _put_col(_put_col(m_all, 2 * p, m_s[2 * p]), 2 * p + 1, m_s[2 * p + 1])
                l_all = _put_col(_put_col(l_all, 2 * p, both), 2 * p + 1, a1)
            lse_ref[...] = (m_all + jnp.log(l_all)).T[:FOX_HEADS, :]

        for sub in range(kw):
            @pl.when(kk * kw + sub < qi)
            def _(sub=sub):
                block(False, sub)

            @pl.when(kk * kw + sub == qi)
            def _(sub=sub):
                block(True, sub)
                finish()

    def qspec(wd, j):
        return pl.BlockSpec((tb, wd), lambda b, st: (b * nq + _tri_pair(st, nq, kw)[0], j))

    def kspec(j):
        return pl.BlockSpec((kw * tb, FOX_W), lambda b, st: (b * nk + _tri_pair(st, nq, kw)[1], j))

    return _call(
        body, name, (proj, kt, proj, c_col, c_row), rider, grid=(nb, _tri_steps(nq, kw)),
        in_specs=[qspec(FOX_W, C_QC // FOX_W),
                  pl.BlockSpec((FOX_W, kw * tb), lambda b, st: (0, b * nk + _tri_pair(st, nq, kw)[1])),
                  kspec(C_VC // FOX_W), qspec(FC_PAD, 0),
                  pl.BlockSpec((None, FOX_HEADS, kw * tb), lambda b, st: (b, 0, _tri_pair(st, nq, kw)[1]))],
        out_specs=[qspec(FOX_W, 0),
                   pl.BlockSpec((None, FOX_HEADS, tb), lambda b, st: (b, 0, _tri_pair(st, nq, kw)[0]))],
        out_shape=[_sds((n, FOX_W), F32), _sds((nb, FOX_HEADS, t), F32)],
        scratch_shapes=[pltpu.VMEM((FOX_HEADS, tb, pw), F32), pltpu.VMEM((FOX_HEADS, tb, pw), F32),
                        pltpu.VMEM((FOX_HEADS, tb, pw), F32)],
        compiler_params=_params(),
    )


def _head_mean(x, ones_f):
    return _dot_sel(x, ones_f) * (1.0 / HEAD)


def _merge_fwd(x, proj, o_h, o_c, gh, wbd, scale, w_out, g_post, nb, name, target=None):
    n, d = x.shape
    t = n // nb
    tm = min(512, t)
    nt = t // tm
    nhb = tm // POOL_HALO

    def body(*refs):
        x_ref, ga_ref, gc_ref, oh_ref, u_ref, gb_ref, h_ref, oc_ref, gh_ref, wb_ref, s_ref, w_ref, gp_ref = refs[:13]
        if target is None:
            xo_ref, mixt_ref, y_ref = refs[13:]
        else:
            t_ref, dx_ref, sq_ref, mixt_ref, y_ref = refs[13:]
        oh = oh_ref[...]
        ones_f = _block_ones(HGRN_W, F32)
        na = oh * lax.rsqrt(_head_mean(oh * oh, ones_f) + NORM_EPS) * gh_ref[...]
        ga, gb, gc = ga_ref[...], gb_ref[...], gc_ref[...]
        ti = pl.program_id(0) % nt
        pooled, _ = _pool_mix(u_ref[...], jnp.where(ti == 0, 0.0, h_ref[...]), ti * tm, tm)
        ob = _dot(pooled.astype(BF16), wb_ref[...]) * s_ref[...] * (gb * _sigmoid(gb))
        mixed = jnp.concatenate([na * (ga * _sigmoid(ga)), ob, oc_ref[...] * (gc * _sigmoid(gc))], axis=1)
        mixt_ref[...] = mixed.T.astype(BF16)
        y = _dot(mixed.astype(BF16), w_ref[...])
        y_ref[...] = y
        xn = x_ref[...] + y * lax.rsqrt(jnp.mean(y * y, axis=-1, keepdims=True) + NORM_EPS) * gp_ref[...]
        if target is None:
            xo_ref[...] = xn
        else:
            @pl.when(pl.program_id(0) == 0)
            def _():
                sq_ref[...] = jnp.zeros_like(sq_ref)

            e = xn - t_ref[...]
            dx_ref[...] = e * (1.0 / d)
            sq_ref[...] += jnp.sum(e * e, axis=0, keepdims=True)

    def row(wd, j=0):
        return pl.BlockSpec((tm, wd), lambda i: (i, j))

    def full(a, b):
        return pl.BlockSpec((a, b), lambda i: (0, 0))

    head = [] if target is None else [target]
    res = _pc(
        body, name, grid=(n // tm,),
        in_specs=[row(d), row(HGRN_W, C_GA // HGRN_W), row(FOX_W, C_GC // FOX_W), row(HGRN_W),
                  row(POOL_W, C_UB // POOL_W), row(POOL_W, C_GB // POOL_W),
                  pl.BlockSpec((POOL_HALO, POOL_W), lambda i: (jnp.maximum(i * nhb - 1, 0), C_UB // POOL_W)), row(FOX_W),
                  full(1, HGRN_W), full(POOL_W, POOL_W), full(1, POOL_W), full(d, d), full(1, d)] + [row(d)] * len(head),
        out_specs=[row(d)] + [full(1, d)] * len(head) + [pl.BlockSpec((d, tm), lambda i: (0, i)), row(d)],
        out_shape=[_sds((n, d), F32)] + [_sds((1, d), F32)] * len(head) + [_sds((d, n), BF16), _sds((n, d), F32)],
        compiler_params=_params(),
    )(x, proj, proj, o_h, proj, proj, proj, o_c, gh, wbd, scale, w_out, g_post, *head)
    return (res[0], res[1], res[2]) if target is None else ((res[0], res[1]), res[2], res[3])


def _rms_bwd(dy_scaled, xhat, r):
    return r * (dy_scaled - xhat * jnp.mean(dy_scaled * xhat, axis=-1, keepdims=True))


def _merge_bwd(dxo, y, g_post, w_out_t, proj, o_c, nb, name):
    n, d = y.shape
    t = n // nb
    tm = min(512, t)
    nt = t // tm
    wab = HGRN_W + POOL_W

    def body(dx_ref, y_ref, gp_ref, wt_ref, gc_ref, oc_ref, dy_ref, dm_ref, dgp_ref, da_ref, dat_ref, dg_ref, dl_ref):
        @pl.when(pl.program_id(0) == 0)
        def _():
            dgp_ref[...] = jnp.zeros_like(dgp_ref)

        yv, dxv = y_ref[...], dx_ref[...]
        r = lax.rsqrt(jnp.mean(yv * yv, axis=-1, keepdims=True) + NORM_EPS)
        yh = yv * r
        dgp_ref[...] += jnp.sum(dxv * yh, axis=0, keepdims=True)
        dyb = _rms_bwd(dxv * gp_ref[...], yh, r).astype(BF16)
        dy_ref[...] = dyb
        dm_ref[...] = _dot(dyb, wt_ref[:, :wab])
        dmc = _dot(dyb, wt_ref[:, wab:])
        gc, oc = gc_ref[...], oc_ref[...]
        sg = _sigmoid(gc)
        da = dmc * (gc * sg)
        da_ref[...] = da.astype(BF16)
        dat_ref[...] = da.T.astype(BF16)
        dg_ref[...] = (dmc * oc * (sg * (1.0 + gc * (1.0 - sg)))).astype(BF16)
        rr = lax.broadcasted_iota(jnp.int32, (FOX_W, FC_PAD), 0) // HEAD
        cc = lax.broadcasted_iota(jnp.int32, (FOX_W, FC_PAD), 1)
        dl_ref[...] = _dot_sel(da * oc, (rr == cc).astype(F32)).T[:FOX_HEADS, :]

    def row(wd, j=0):
        return pl.BlockSpec((tm, wd), lambda i: (i, j))

    def full(a, b):
        return pl.BlockSpec((a, b), lambda i: (0, 0))

    return _pc(
        body, name, grid=(n // tm,),
        in_specs=[row(d), row(d), full(1, d), full(d, d), row(FOX_W, C_GC // FOX_W), row(FOX_W)],
        out_specs=[row(d), row(wab), full(1, d), row(FOX_W), pl.BlockSpec((FOX_W, tm), lambda i: (0, i)), row(FOX_W),
                   pl.BlockSpec((None, FOX_HEADS, tm), lambda i: (i // nt, 0, i % nt))],
        out_shape=[_sds((n, d), BF16), _sds((n, wab), F32), _sds((1, d), F32), _sds((n, FOX_W), BF16),
                   _sds((FOX_W, n), BF16), _sds((n, FOX_W), BF16), _sds((nb, FOX_HEADS, t), F32)],
        compiler_params=_params(),
    )(dxo, y, g_post, w_out_t, proj, o_c)


def _w_out_grad(mixt, dy, name):
    d, n = mixt.shape
    rows = d // N_DEV

    def body(a_ref, b_ref, o_ref):
        o_ref[...] = _dot(a_ref[...], b_ref[...]).astype(BF16)

    return _pc(
        body, name, grid=(N_DEV,),
        in_specs=[pl.BlockSpec((rows, n), lambda j: (j, 0)), pl.BlockSpec((n, d), lambda j: (0, 0))],
        out_specs=pl.BlockSpec((None, None, rows, d), lambda j: (j % 2, j // 2, 0, 0)),
        out_shape=_sds((2, N_DEV // 2, rows, d), BF16),
        compiler_params=_params(),
    )(mixt, dy)


def _w_in_grad(ht, pieces, name):
    d, n = ht.shape
    ta, tk = min(512, d), min(512, n)
    nk = n // tk
    arrays = [p for p, _ in pieces]
    widths = [p.shape[1] for p in arrays]
    offs = [sum(widths[:i]) for i in range(len(widths))]
    in_w = MAIN_W + FOX_HEADS
    shard = in_w // N_DEV

    def body(*refs):
        a_ref, p_refs = refs[0], refs[1:1 + len(arrays)]
        o_ref, acc = refs[1 + len(arrays):]
        k = pl.program_id(1)

        @pl.when(k == 0)
        def _():
            acc[...] = jnp.zeros_like(acc)

        a = a_ref[...]
        for pr, off, wd in zip(p_refs, offs, widths):
            for j in range(0, wd, 512):
                jw = min(512, wd - j)
                acc[:, off + j:off + j + jw] += _dot(a, pr[:, j:j + jw])

        @pl.when(k == nk - 1)
        def _():
            for j in range(N_DEV):
                o_ref[j % 2, j // 2] = acc[:, j * shard:(j + 1) * shard].astype(BF16)

    return _pc(
        body, name, grid=(d // ta, nk),
        in_specs=[pl.BlockSpec((ta, tk), lambda i, k: (i, k))] + [pl.BlockSpec((tk, wd), lambda i, k: (k, 0)) for wd in widths],
        out_specs=pl.BlockSpec((2, N_DEV // 2, ta, shard), lambda i, k: (0, 0, i, 0)),
        out_shape=_sds((2, N_DEV // 2, d, shard), BF16),
        scratch_shapes=[pltpu.VMEM((ta, sum(widths)), F32)],
        compiler_params=_params(),
    )(ht, *arrays)


def _hgrn_state_bwd(qs, k, v, b, do, s0, ds1, bmask):
    bl = b[SUB - 1:SUB, :]
    eb, ebl, ekt = jnp.exp(b), jnp.exp(bl), jnp.exp(bl - b)
    qe, ktil = qs * eb, k * ekt
    ds1b, dob = ds1.astype(BF16), do.astype(BF16)
    dv = _dot_nt(ktil.astype(BF16), ds1b)
    dqe = _dot(dob, s0.astype(BF16))
    dktil = _dot(v.astype(BF16), ds1b)
    dbl = jnp.sum(dktil * ktil, axis=0, keepdims=True) + ebl * jnp.sum(s0 * ds1, axis=0, keepdims=True)
    ds0 = ds1 * ebl + _dot_tn(dob, qe.astype(BF16)) * bmask
    return dqe * eb, dktil * ekt, dv, dbl, ds0


def _hgrn_intra_bwd(qs, k, v, do, es, aexp, gexp, dq, dk, dv, put_dq_row):
    dks = [dk[j:j + 8] for j in range(0, SUB, 8)]
    dvs = [dv[j:j + 8] for j in range(0, SUB, 8)]
    for t in range(SUB):
        r = _live_rows(t)
        ge = gexp[t * SUB:t * SUB + r, :] * es[t]
        put_dq_row(t, dq[t:t + 1, :] + jnp.sum(ge * k[:r], axis=0, keepdims=True))
        for j in range(r // 8):
            dks[j] = dks[j] + ge[8 * j:8 * j + 8] * qs[t:t + 1, :]
            dvs[j] = dvs[j] + aexp[t * SUB + 8 * j:t * SUB + 8 * j + 8, :] * do[t:t + 1, :]
    return jnp.concatenate(dks, axis=0), jnp.concatenate(dvs, axis=0)


def _hgrn_bwd(dmix, proj, o_h, s0, gh, lb, ones_b, nb, name, rider=None):
    n = proj.shape[0]
    t = n // nb
    tt = _hgrn_tile(t)
    nt = t // tt
    ncs = tt // CHUNK
    nsub = CHUNK // SUB
    w = HGRN_W

    def body(dm_ref, q_ref, z_ref, v_ref, ga_ref, oh_ref, s0_ref, gh_ref, lb_ref, ones_ref,
             dp_ref, dgh_ref, dlb_ref, ds_s, ss_s, b_s, qs_s, k_s, do_s, dq_s, dk_s, dv_s, dbl_s):
        @pl.when(pl.program_id(0) == 0)
        def _():
            dgh_ref[...] = jnp.zeros_like(dgh_ref)
            dlb_ref[...] = jnp.zeros_like(dlb_ref)
            ds_s[...] = jnp.zeros_like(ds_s)

        ones_b = ones_ref[...]
        ones_f = ones_b.astype(F32)
        bmask = _block_ones(w, F32)
        lbv, ghv = lb_ref[...], gh_ref[...]
        kept = []
        for e in range(nb):
            oh, ga, dm = oh_ref[e], ga_ref[e], dm_ref[e]
            rn = lax.rsqrt(_head_mean(oh * oh, ones_f) + NORM_EPS)
            nh = oh * rn
            sga = _sigmoid(ga)
            dp_ref[e, :, 3 * w:4 * w] = (dm * nh * ghv * (sga * (1.0 + ga * (1.0 - sga)))).astype(BF16)
            dn = dm * (ga * sga)
            dgh_ref[...] += jnp.sum(dn * nh, axis=0, keepdims=True)
            dn = dn * ghv
            do_s[e] = rn * (dn - nh * _head_mean(dn * nh, ones_f))
            q = q_ref[e]
            sig, sn, f, g, k, sq = _hgrn_gates(q, z_ref[e], lbv)
            qs = q * sq
            b_s[e] = _sel_dot(_sub_tri(tt, True), g)
            qs_s[e] = qs
            k_s[e] = k
            kept.append((q, sig, sn, f, k, sq, qs))

        def chunk(cc, carry):
            c = ncs - 1 - cc
            base = pl.multiple_of(c * CHUNK, CHUNK)
            tiles = [[(qs_s[e, pl.ds(base + u * SUB, SUB), :], k_s[e, pl.ds(base + u * SUB, SUB), :],
                       v_ref[e, pl.ds(base + u * SUB, SUB), :], b_s[e, pl.ds(base + u * SUB, SUB), :],
                       do_s[e, pl.ds(base + u * SUB, SUB), :]) for u in range(nsub)] for e in range(nb)]
            sts = [s0_ref[e, c] for e in range(nb)]
            for u in range(nsub):
                for e in range(nb):
                    qs, k, v, b, do = tiles[e][u]
                    ss_s[e, u] = sts[e]
                    if u < nsub - 1:
                        sts[e] = _hgrn_state_step(sts[e], k, v, b, bmask)
            dss = [ds_s[e] for e in range(nb)]
            for u in reversed(range(nsub)):
                for e in range(nb):
                    qs, k, v, b, do = tiles[e][u]
                    _, es, ws = _hgrn_decays(qs, k, b)
                    gs = [_pad_rows(do[t:t + 1, :] * v[:_live_rows(t)]) for t in range(SUB)]
                    aexp = _dot(jnp.concatenate(ws, axis=0).astype(BF16), ones_b)
                    gexp = _dot(jnp.concatenate(gs, axis=0).astype(BF16), ones_b)
                    dq, dk, dv, dbl, dss[e] = _hgrn_state_bwd(qs, k, v, b, do, ss_s[e, u], dss[e], bmask)

                    def put_dq_row(i, row, e=e, r0=base + u * SUB):
                        dq_s[e, pl.ds(r0 + i, 1), :] = row

                    dk, dv = _hgrn_intra_bwd(qs, k, v, do, es, aexp, gexp, dq, dk, dv, put_dq_row)
                    dk_s[e, pl.ds(base + u * SUB, SUB), :] = dk
                    dv_s[e, pl.ds(base + u * SUB, SUB), :] = dv
                    dbl_s[e, pl.ds(base + u * SUB, SUB), :] = jnp.broadcast_to(dbl, (SUB, w))
            for e in range(nb):
                ds_s[e] = dss[e]
            return carry

        lax.fori_loop(0, ncs, chunk, 0)
        for e, (q, sig, sn, f, k, sq, qs) in enumerate(kept):
            dqs, dk = dq_s[e], dk_s[e]
            dg = _sel_dot(_sub_tri(tt, False), qs * dqs - k * dk) + dbl_s[e]
            dfz = jnp.where(f > TINY, dg / jnp.maximum(f, TINY), 0.0)
            dlb_ref[...] += jnp.sum(dfz * (1.0 - sig) - dk * sn, axis=0, keepdims=True)
            dp_ref[e, :, 0:w] = (dqs * (sq * (1.0 + q * (1.0 - sq)))).astype(BF16)
            dp_ref[e, :, w:2 * w] = ((dfz - dk) * (1.0 - lbv) * sig * sn).astype(BF16)
            dp_ref[e, :, 2 * w:3 * w] = dv_s[e].astype(BF16)

    def col(j):
        return pl.BlockSpec((nb, tt, w), lambda i: (0, nt - 1 - i, j))

    def full(a, bb):
        return pl.BlockSpec((a, bb), lambda i: (0, 0))

    proj3 = proj.reshape(nb, t, proj.shape[1])
    (dp, dgh, dlb), rode = _call(
        body, name, (dmix.reshape(nb, t, dmix.shape[1]), proj3, proj3, proj3, proj3, o_h.reshape(nb, t, w),
                     s0.reshape(nb, t // CHUNK, w, w), gh, lb, ones_b), rider, grid=(nt,),
        in_specs=[col(0), col(C_QA // w), col(C_FA // w), col(C_IA // w), col(C_GA // w), col(0),
                  pl.BlockSpec((nb, ncs, w, w), lambda i: (0, nt - 1 - i, 0, 0)), full(1, w), full(1, w), full(w, w)],
        out_specs=[pl.BlockSpec((nb, tt, 4 * w), lambda i: (0, nt - 1 - i, 0)), full(1, w), full(1, w)],
        out_shape=[_sds((nb, t, 4 * w), BF16), _sds((1, w), F32), _sds((1, w), F32)],
        scratch_shapes=[pltpu.VMEM((nb, w, w), F32), pltpu.VMEM((nb, nsub, w, w), F32)]
        + [pltpu.VMEM((nb, tt, w), F32)] * 8,
        compiler_params=_params(),
    )
    return (dp.reshape(n, 4 * w), dgh, dlb), rode


def _pool_bwd(dmix, proj, wbd, wbd_t, scale, nb, name):
    n = proj.shape[0]
    t = n // nb
    tt = min(512, t)
    nt = t // tt
    nhb = tt // POOL_HALO
    cu, cg, cm = C_UB // POOL_W, C_GB // POOL_W, HGRN_W // POOL_W

    def body(u_ref, g_ref, h_ref, dm_ref, gn_ref, dmn_ref, w_ref, wt_ref, s_ref, dp_ref, dw_ref, ds_ref):
        i = pl.program_id(1)
        first = jnp.logical_and(pl.program_id(0) == 0, i == 0)

        @pl.when(first)
        def _():
            dw_ref[...] = jnp.zeros_like(dw_ref)
            ds_ref[...] = jnp.zeros_like(ds_ref)

        sc = s_ref[...]
        halo = jnp.where(i == 0, 0.0, h_ref[...])
        pooled, cnt = _pool_mix(u_ref[...], halo, i * tt, tt)
        pb = pooled.astype(BF16)
        pre = _dot(pb, w_ref[...])
        gv, dm = g_ref[...], dm_ref[...]
        sg = _sigmoid(gv)
        silu = gv * sg
        dgb = dm * pre * sc * (sg * (1.0 + gv * (1.0 - sg)))
        ds_ref[...] += jnp.sum(dm * pre * silu, axis=0, keepdims=True)
        dpre = (dm * sc * silu).astype(BF16)
        dw_ref[...] += _dot_tn(pb, dpre)
        dpool = _dot(dpre, wt_ref[...])
        gn = gn_ref[...]
        dpre_n = (dmn_ref[...] * sc * (gn * _sigmoid(gn))).astype(BF16)
        dpool_n = jnp.where(i == nt - 1, 0.0, _dot(dpre_n, wt_ref[...]))
        lane, wl = _pool_lane_windows()
        tpos_n = ((i + 1) * tt + lax.broadcasted_iota(jnp.int32, (POOL_HALO, POOL_W), 0)).astype(F32)
        ext = jnp.concatenate([dpool / cnt, dpool_n / jnp.minimum(tpos_n + 1.0, wl)], axis=0)
        rows = tt + POOL_HALO
        sums, cur, shift = [], ext, 1
        for _ in POOL_WINDOWS:
            cur = cur + pltpu.roll(cur, rows - shift, axis=0)
            sums.append(cur[:tt, :])
            shift *= 2
        du = _pool_select(lane, sums) - dpool
        dp_ref[...] = jnp.concatenate([du, dgb], axis=1).astype(BF16)

    def nxt(b, i):
        return jnp.minimum((b * nt + i + 1) * nhb, n // POOL_HALO - 1)

    return _pc(
        body, name, grid=(nb, nt),
        in_specs=_pool_specs(tt, nt, nhb) + [
            pl.BlockSpec((tt, POOL_W), lambda b, i: (b * nt + i, cm)),
            pl.BlockSpec((POOL_HALO, POOL_W), lambda b, i: (nxt(b, i), cg)),
            pl.BlockSpec((POOL_HALO, POOL_W), lambda b, i: (nxt(b, i), cm)),
            pl.BlockSpec((POOL_W, POOL_W), lambda b, i: (0, 0)), pl.BlockSpec((POOL_W, POOL_W), lambda b, i: (0, 0)),
            pl.BlockSpec((1, POOL_W), lambda b, i: (0, 0))],
        out_specs=[pl.BlockSpec((tt, 2 * POOL_W), lambda b, i: (b * nt + i, 0)),
                   pl.BlockSpec((POOL_W, POOL_W), lambda b, i: (0, 0)), pl.BlockSpec((1, POOL_W), lambda b, i: (0, 0))],
        out_shape=[_sds((n, 2 * POOL_W), BF16), _sds((POOL_W, POOL_W), F32), _sds((1, POOL_W), F32)],
        compiler_params=_params(),
    )(proj, proj, proj, dmix, proj, dmix, wbd, wbd_t, scale)


def _fox_bwd(proj, qt, kt, da, dat, c_col, c_row, lse_row, delta_row, nb, name, rider=None):
    n = proj.shape[0]
    t = n // nb
    tb = min(256, t)
    nq = t // tb
    pw = 2 * HEAD
    qw = 2 if nq % 2 == 0 else 1
    nqs = nq // qw

    def body(q_ref, k_ref, v_ref, da_ref, qt_ref, kt_ref, dat_ref, cc_ref, cr_ref, lse_ref, dl_ref,
             dq_ref, dk_ref, dv_ref, dck_ref, dcq_ref, dq_s, dk_s, dv_s, dck_s, dcq_s):
        step = pl.program_id(1)
        kj, qq = pairs(step)

        @pl.when(step == 0)
        def _():
            dq_s[...] = jnp.zeros_like(dq_s)
            dcq_s[...] = jnp.zeros_like(dcq_s)

        @pl.when(qq == nqs - 1)
        def _():
            dk_s[...] = jnp.zeros_like(dk_s)
            dv_s[...] = jnp.zeros_like(dv_s)
            dck_s[...] = jnp.zeros_like(dck_s)

        def block(masked, sub):
            lo = _lane_lo()
            qi = qq * qw + sub
            qs = slice(sub * tb, (sub + 1) * tb)
            if masked:
                causal = lax.broadcasted_iota(jnp.int32, (tb, tb), 1) >= lax.broadcasted_iota(jnp.int32, (tb, tb), 0)
            dck = dck_s[...]
            for p in range(FOX_HEADS // 2):
                sl = slice(p * pw, (p + 1) * pw)
                qp = q_ref[qs, sl] * FOX_SCALE
                kp = k_ref[:, sl].astype(BF16)
                vp = v_ref[:, sl].astype(BF16)
                dap = da_ref[qs, sl]
                dk, dv = dk_s[:, sl], dv_s[:, sl]
                for h in range(2):
                    hh = 2 * p + h
                    lm = lo if h == 0 else jnp.logical_not(lo)
                    rows = slice(hh * HEAD, (hh + 1) * HEAD)
                    none = jnp.zeros((HEAD, tb), BF16)
                    qm = jnp.where(lm, qp, 0.0).astype(BF16)
                    dam = jnp.where(lm, dap, jnp.zeros_like(dap))
                    qtm = jnp.concatenate([qt_ref[rows, qs], none] if h == 0 else [none, qt_ref[rows, qs]], axis=0)
                    datm = jnp.concatenate([dat_ref[rows, qs], none] if h == 0 else [none, dat_ref[rows, qs]], axis=0)
                    s = _dot(kp, qtm) + (cr_ref[hh:hh + 1, qs] - cc_ref[:, hh:hh + 1])
                    pe = jnp.exp(s - lse_ref[hh:hh + 1, qs])
                    if masked:
                        pe = jnp.where(causal, pe, 0.0)
                    dp = _dot(vp, datm)
                    ds = pe * (dp - dl_ref[hh:hh + 1, qs])
                    dsb = ds.astype(BF16)
                    dv = dv + _dot(pe.astype(BF16), dam)
                    dk = dk + _dot(dsb, qm)
                    dq_s[qi, rows, :] += _dot(kt_ref[rows, :], dsb)
                    dck = dck - _put_col(jnp.zeros_like(dck), hh, jnp.sum(ds, axis=1, keepdims=True))
                    dcq_s[qi, hh:hh + 1, :] += _rows_reduce(ds, jnp.add, jnp.sum)
                dk_s[:, sl] = dk
                dv_s[:, sl] = dv
            dck_s[...] = dck

        for sub in reversed(range(qw)):
            @pl.when(qq * qw + sub > kj)
            def _(sub=sub):
                block(False, sub)

            @pl.when(qq * qw + sub == kj)
            def _(sub=sub):
                block(True, sub)

        @pl.when(qq == kj // qw)
        def _():
            dk_ref[...] = dk_s[...].astype(BF16)
            dv_ref[...] = dv_s[...].astype(BF16)
            dck_ref[...] = dck_s[...]

        @pl.when(step == _tri_steps(nq, qw) - 1)
        def _():
            for j in range(nq):
                dq_ref[j * tb:(j + 1) * tb, :] = (dq_s[j].T * FOX_SCALE).astype(BF16)
                dcq_ref[:, j * tb:(j + 1) * tb] = dcq_s[j]

    def pairs(step):
        a, b = _tri_pair(step, nq, qw)
        return nq - 1 - a, nqs - 1 - b

    def kspec(wd, j=0):
        return pl.BlockSpec((tb, wd), lambda b, st: (b * nq + pairs(st)[0], j))

    def qspec(wd, j=0):
        return pl.BlockSpec((qw * tb, wd), lambda b, st: (b * nqs + pairs(st)[1], j))

    def qrow():
        return pl.BlockSpec((None, FOX_HEADS, qw * tb), lambda b, st: (b, 0, pairs(st)[1]))

    def tspec(which):
        if which == 0:
            return pl.BlockSpec((FOX_W, tb), lambda b, st: (0, b * nq + pairs(st)[0]))
        return pl.BlockSpec((FOX_W, qw * tb), lambda b, st: (0, b * nqs + pairs(st)[1]))

    return _call(
        body, name, (proj, proj, proj, da, qt, kt, dat, c_col, c_row, lse_row, delta_row), rider,
        grid=(nb, _tri_steps(nq, qw)),
        in_specs=[qspec(FOX_W, C_QC // FOX_W), kspec(FOX_W, C_KC // FOX_W), kspec(FOX_W, C_VC // FOX_W), qspec(FOX_W),
                  tspec(1), tspec(0), tspec(1), kspec(FC_PAD), qrow(), qrow(), qrow()],
        out_specs=[pl.BlockSpec((t, FOX_W), lambda b, st: (b, 0)), kspec(FOX_W), kspec(FOX_W), kspec(FC_PAD),
                   pl.BlockSpec((None, FOX_HEADS, t), lambda b, st: (b, 0, 0))],
        out_shape=[_sds((n, FOX_W), BF16), _sds((n, FOX_W), BF16), _sds((n, FOX_W), BF16), _sds((n, FC_PAD), F32),
                   _sds((nb, FOX_HEADS, t), F32)],
        scratch_shapes=[pltpu.VMEM((nq, FOX_W, tb), F32), pltpu.VMEM((tb, FOX_W), F32), pltpu.VMEM((tb, FOX_W), F32),
                        pltpu.VMEM((tb, FC_PAD), F32), pltpu.VMEM((nq, FOX_HEADS, tb), F32)],
        compiler_params=_params(),
    )


def _fox_decay_bwd(dc_q, dc_k, fc, bias, nb, name):
    n = fc.shape[0]
    t = n // nb
    tt = min(256, t)
    nt = t // tt

    def body(dcq_ref, dck_ref, fc_ref, b_ref, dfc_ref, db_ref):
        @pl.when(pl.program_id(0) == 0)
        def _():
            db_ref[...] = jnp.zeros_like(db_ref)

        r = lax.broadcasted_iota(jnp.int32, (tt, tt), 0)
        cc = lax.broadcasted_iota(jnp.int32, (tt, tt), 1)
        carry = jnp.zeros((1, FC_PAD), F32)
        db = jnp.zeros((1, FC_PAD), F32)
        for i in reversed(range(nt)):
            rows = slice(i * tt, (i + 1) * tt)
            dcq = jnp.concatenate([dcq_ref[:, rows], jnp.zeros((FC_PAD - FOX_HEADS, tt), F32)], axis=0).T
            dlf = _sel_dot_exact(r <= cc, dcq + dck_ref[rows, :]) + carry
            carry = dlf[0:1, :]
            dfc = dlf * _sigmoid(-(fc_ref[rows, :] + b_ref[...]))
            dfc_ref[rows, :] = dfc.astype(BF16)
            db = db + jnp.sum(dfc, axis=0, keepdims=True)
        db_ref[...] += db

    def row():
        return pl.BlockSpec((t, FC_PAD), lambda b: (b, 0))

    return _pc(
        body, name, grid=(nb,),
        in_specs=[pl.BlockSpec((None, FOX_HEADS, t), lambda b: (b, 0, 0)), row(), row(),
                  pl.BlockSpec((1, FC_PAD), lambda b: (0, 0))],
        out_specs=[row(), pl.BlockSpec((1, FC_PAD), lambda b: (0, 0))],
        out_shape=[_sds((n, FC_PAD), BF16), _sds((1, FC_PAD), F32)],
        compiler_params=_params(),
    )(dc_q, dc_k, fc, bias)


def _in_proj_bwd(pieces, w_main_t, w_fc_t, x, g_pre, dxo, name, rider=None):
    n, d = x.shape
    tm = min(512, n)
    widths = [p.shape[1] for p, _ in pieces]
    offs = [o for _, o in pieces]
    np_ = len(pieces)

    def body(*refs):
        p_refs = refs[:np_]
        wt_ref, wf_ref, x_ref, g_ref, dxo_ref, dx_ref, dg_ref = refs[np_:]

        @pl.when(pl.program_id(0) == 0)
        def _():
            dg_ref[...] = jnp.zeros_like(dg_ref)

        dh = _dot(p_refs[-1][...], wf_ref[...])
        for pr, wd, off in zip(p_refs[:-1], widths[:-1], offs[:-1]):
            for j in range(0, wd, 512):
                jw = min(512, wd - j)
                dh = dh + _dot(pr[:, j:j + jw], wt_ref[off + j:off + j + jw, :])
        xv = x_ref[...]
        r = lax.rsqrt(jnp.mean(xv * xv, axis=-1, keepdims=True) + NORM_EPS)
        xh = xv * r
        dg_ref[...] += jnp.sum(dh * xh, axis=0, keepdims=True)
        dx_ref[...] = dxo_ref[...] + _rms_bwd(dh * g_ref[...], xh, r)

    row = pl.BlockSpec((tm, d), lambda i: (i, 0))
    return _call(
        body, name, (*[p for p, _ in pieces], w_main_t, w_fc_t, x, g_pre, dxo), rider, grid=(n // tm,),
        in_specs=[pl.BlockSpec((tm, wd), lambda i: (i, 0)) for wd in widths] + [
            pl.BlockSpec((MAIN_W, d), lambda i: (0, 0)), pl.BlockSpec((FC_PAD, d), lambda i: (0, 0)),
            row, pl.BlockSpec((1, d), lambda i: (0, 0)), row],
        out_specs=[row, pl.BlockSpec((1, d), lambda i: (0, 0))],
        out_shape=[_sds((n, d), F32), _sds((1, d), F32)],
        compiler_params=_params(),
    )


def _lower_bound_table(lower_bounds, name):
    depth, w = lower_bounds.shape

    def body(lb_ref, o_ref):
        v = lb_ref[...]
        e = jnp.exp(v - jnp.max(v, axis=0, keepdims=True))
        p = e / jnp.sum(e, axis=0, keepdims=True)
        acc = jnp.zeros((1, w), F32)
        for l in range(depth):
            acc = acc + p[l:l + 1, :]
            o_ref[l:l + 1, :] = acc - p[0:1, :]

    return _pc(body, name, out_shape=_sds((depth, w), F32))(lower_bounds)


def _lower_bound_bwd(lower_bounds, dlbs, name):
    depth, w = lower_bounds.shape

    def body(lb_ref, d_ref, o_ref):
        v, dl = lb_ref[...], d_ref[...]
        e = jnp.exp(v - jnp.max(v, axis=0, keepdims=True))
        p = e / jnp.sum(e, axis=0, keepdims=True)
        tot = jnp.sum(dl, axis=0, keepdims=True)
        rows, tail = [], tot
        for l in range(depth):
            rows.append(tail - tot if l == 0 else tail)
            tail = tail - dl[l:l + 1, :]
        dp = jnp.concatenate(rows, axis=0)
        o_ref[...] = p * (dp - jnp.sum(p * dp, axis=0, keepdims=True))

    return _pc(body, name, out_shape=_sds((depth, w), F32))(lower_bounds, dlbs)


def _place():
    x, y, c = lax.axis_index("x"), lax.axis_index("y"), lax.axis_index("c")
    return x, y, c


def _gather_weights(*arrays):
    na = len(arrays)

    def body(*refs):
        ins, outs = refs[:na], refs[na:2 * na]
        send_sems, recv_sems, local_sems = refs[2 * na:]
        x, y, c = _place()
        me, sibling = (x, y, c), (x, y, 1 - c)
        chips = [(1 - x, y), (x, 1 - y), (1 - x, 1 - y)]

        def slot(a, px, py, pc):
            return outs[a].at[4 * px + 2 * py + pc]

        def copy(a, k, block, to, own=False):
            return pltpu.make_async_remote_copy(
                src_ref=ins[a] if own else slot(a, *block), dst_ref=slot(a, *block),
                send_sem=send_sems.at[a * 7 + k], recv_sem=recv_sems.at[a * 7 + k],
                device_id=to, device_id_type=MESH)

        mine = [pltpu.make_async_copy(ins[a], slot(a, *me), local_sems.at[a]) for a in range(na)]
        for cp in mine:
            cp.start()
        first = []
        for a in range(na):
            first.append(copy(a, 0, me, sibling, own=True))
            first += [copy(a, 1 + j, me, (*chip, c), own=True) for j, chip in enumerate(chips)]
        for cp in first:
            cp.start()
        passed = []
        for j, chip in enumerate(chips):
            for a in range(na):
                copy(a, 1 + j, (*chip, c), me).wait_recv()
                fw = copy(a, 4 + j, (*chip, c), sibling)
                fw.start()
                passed.append(fw)
        for a in range(na):
            copy(a, 0, sibling, me).wait_recv()
            for j, chip in enumerate(chips):
                copy(a, 4 + j, (*chip, 1 - c), me).wait_recv()
        for cp in first + passed:
            cp.wait_send()
        for cp in mine:
            cp.wait()

    any_spec = pl.BlockSpec(memory_space=pl.ANY)
    return _pc(
        body, "gather_weights",
        in_specs=[any_spec] * na, out_specs=[any_spec] * na,
        out_shape=[_sds((N_DEV,) + a.shape, a.dtype) for a in arrays],
        scratch_shapes=[pltpu.SemaphoreType.DMA((7 * na,)), pltpu.SemaphoreType.DMA((7 * na,)),
                        pltpu.SemaphoreType.DMA((na,))],
    )(*arrays)


def _peer(k):
    x, y, c = _place()
    return (1 - x if k & 4 else x, 1 - y if k & 2 else y, 1 - c if k & 1 else c)


def _remote(src, dst, sems, s, to):
    return pltpu.make_async_remote_copy(src_ref=src, dst_ref=dst, send_sem=sems[0].at[s], recv_sem=sems[1].at[s],
                                        device_id=to, device_id_type=MESH)


def _gather_rider(shards):
    na = len(shards)

    def plan(ins, outs, *sems):
        x, y, c = _place()
        me = 4 * x + 2 * y + c
        locs = [pltpu.make_async_copy(ins[a], outs[a].at[me], sems[2].at[a]) for a in range(na)]
        sends, recvs = [], []
        for k in range(1, N_DEV):
            px, py, pc = _peer(k)
            for a in range(na):
                s = (k - 1) * na + a
                sends.append(_remote(ins[a], outs[a].at[me], sems, s, (px, py, pc)))
                recvs.append(_remote(ins[a], outs[a].at[4 * px + 2 * py + pc], sems, s, (px, py, pc)))
        return sends, recvs, locs

    return _Rider(shards, [_sds((N_DEV,) + a.shape, a.dtype) for a in shards], (N_DEV - 1) * na, na, plan)


def _direct_exchange_rider(blocks):
    na = len(blocks)

    def plan(ins, outs, *sems):
        x, y, c = _place()
        me = 4 * x + 2 * y + c
        locs = [pltpu.make_async_copy(ins[a].at[c, 2 * x + y], outs[a].at[me], sems[2].at[a]) for a in range(na)]
        sends, recvs = [], []
        for k in range(1, N_DEV):
            px, py, pc = _peer(k)
            for a in range(na):
                s = (k - 1) * na + a
                sends.append(_remote(ins[a].at[pc, 2 * px + py], outs[a].at[me], sems, s, (px, py, pc)))
                recvs.append(_remote(ins[a].at[pc, 2 * px + py], outs[a].at[4 * px + 2 * py + pc], sems, s, (px, py, pc)))
        return sends, recvs, locs

    return _Rider(blocks, [_sds((N_DEV,) + a.shape[2:], a.dtype) for a in blocks], (N_DEV - 1) * na, na, plan)


def _swap_rider(halves):
    na = len(halves)

    def plan(ins, outs, *sems):
        x, y, c = _place()
        cps = [_remote(ins[a].at[1 - c], outs[a], sems, a, (x, y, 1 - c)) for a in range(na)]
        return cps, cps, []

    return _Rider(halves, [_sds(a.shape[1:], a.dtype) for a in halves], na, 1, plan)


def _chip_exchange_rider(parts, small=None):
    na = len(parts)
    n_chip = N_DEV // 2

    def plan(ins, outs, *sems):
        x, y, c = _place()
        chip = 2 * x + y
        locs = [pltpu.make_async_copy(ins[a].at[chip], outs[a].at[chip], sems[2].at[a]) for a in range(na)]
        sends, recvs = [], []
        for k in range(1, n_chip):
            px, py, _ = _peer(2 * k)
            for a in range(na):
                s = (k - 1) * na + a
                sends.append(_remote(ins[a].at[2 * px + py], outs[a].at[chip], sems, s, (px, py, c)))
                recvs.append(_remote(ins[a].at[2 * px + py], outs[a].at[2 * px + py], sems, s, (px, py, c)))
        if small is not None:
            me = 2 * chip + c
            locs.append(pltpu.make_async_copy(ins[na], outs[na].at[me], sems[2].at[na]))
            for k in range(1, N_DEV):
                px, py, pc = _peer(k)
                s = (n_chip - 1) * na + k - 1
                sends.append(_remote(ins[na], outs[na].at[me], sems, s, (px, py, pc)))
                recvs.append(_remote(ins[na], outs[na].at[4 * px + 2 * py + pc], sems, s, (px, py, pc)))
        return sends, recvs, locs

    extra = [] if small is None else [small]
    shapes = [_sds(a.shape, a.dtype) for a in parts] + [_sds((N_DEV,) + s.shape, s.dtype) for s in extra]
    n_sems = (n_chip - 1) * na + (N_DEV - 1) * len(extra)
    return _Rider(list(parts) + extra, shapes, n_sems, na + len(extra), plan)


def _pair_add(halves, other, core, name):
    _, nch, r, c = halves.shape

    def body(c_ref, h_ref, o_ref, p_ref):
        p_ref[...] = (h_ref[...].astype(F32) + o_ref[...].astype(F32)).astype(BF16)

    blk = pl.BlockSpec((None, r, c), lambda j, c_ref: (j, 0, 0))
    return _pc(
        body, name,
        grid_spec=pltpu.PrefetchScalarGridSpec(
            num_scalar_prefetch=1, grid=(nch,),
            in_specs=[pl.BlockSpec((None, None, r, c), lambda j, c_ref: (c_ref[0], j, 0, 0)), blk], out_specs=blk),
        out_shape=_sds((nch, r, c), BF16),
        compiler_params=_params(),
    )(core, halves, other)


def _sum_adamw(parts, w, m, v, name, rider=None):
    nl, r, c = w.shape
    tr = 256 if r % 256 == 0 else r

    def body(*refs):
        p_refs = refs[:nl]
        w_ref, m_ref, v_ref, g_ref, d_ref, mo_ref, vo_ref = refs[nl:]
        for l in range(nl):
            @pl.when(pl.program_id(0) == l)
            def _(p_ref=p_refs[l]):
                g = p_ref[0].astype(F32)
                for j in range(1, p_ref.shape[0]):
                    g = g + p_ref[j].astype(F32)
                mn = ADAM_B1 * m_ref[...] + (1.0 - ADAM_B1) * g
                vn = ADAM_B2 * v_ref[...] + (1.0 - ADAM_B2) * (g * g)
                m_hat = mn / (1.0 - ADAM_B1 ** ADAM_STEP)
                v_hat = vn / (1.0 - ADAM_B2 ** ADAM_STEP)
                g_ref[...] = g
                d_ref[...] = -ADAM_LR * (m_hat / (jnp.sqrt(v_hat) + ADAM_EPS) + ADAM_WD * w_ref[...])
                mo_ref[...] = mn
                vo_ref[...] = vn

    def part_spec(l, k):
        return pl.BlockSpec((k, tr, c), lambda li, i: (0, jnp.where(li == l, i, 0), 0))

    row = pl.BlockSpec((None, tr, c), lambda li, i: (li, i, 0))
    return _call(
        body, name, (*parts, w, m, v), rider, grid=(nl, r // tr),
        in_specs=[part_spec(l, p.shape[0]) for l, p in enumerate(parts)] + [row, row, row],
        out_specs=[row] * 4,
        out_shape=[_sds((nl, r, c), F32)] * 4,
        compiler_params=_params(),
    )


SMALL = ("lower_bounds", "pre_norm_g", "hgrn_norm_g", "pool_w", "pool_scale", "post_norm_g", "fox_f_bias")
SMALL_LANES = 128


def _small_size(tree):
    return sum(tree[k].size for k in SMALL)


def _pack_small(tree, extra=None):
    flat = jnp.concatenate([tree[k].reshape(-1) for k in SMALL] + ([] if extra is None else [extra.reshape(1)]))
    rows = -(-(_small_size(tree) + 1) // (8 * SMALL_LANES)) * 8
    return jnp.pad(flat, (0, rows * SMALL_LANES - flat.shape[0])).reshape(rows, SMALL_LANES)


def _unpack_small(packed, like):
    out, off = {}, 0
    for k in SMALL:
        size = like[k].size
        assert off % SMALL_LANES == 0
        rows = packed[off // SMALL_LANES:-(-(off + size) // SMALL_LANES)]
        out[k] = rows.reshape(-1)[:size].reshape(like[k].shape)
        off += size
    return out


def _block_diag(pw):
    g = pw.shape[0]
    eye = jnp.eye(g, dtype=pw.dtype)
    return (eye[:, None, :, None] * pw[:, :, None, :]).reshape(g * HEAD, g * HEAD)


def _assemble_w_in(g_in, name):
    _, d, shard = g_in.shape
    tr = min(256, d)
    wide = MAIN_W + FC_PAD

    def body(g_ref, wm_ref, wf_ref, wmt_ref, wft_ref, row_s):
        row_s[:, MAIN_W:] = jnp.zeros((tr, FC_PAD), F32)
        for j in range(N_DEV):
            row_s[:, j * shard:(j + 1) * shard] = g_ref[j].astype(F32)
        wm_ref[...] = row_s[:, :MAIN_W].astype(BF16)
        wf_ref[...] = row_s[:, MAIN_W:].astype(BF16)
        for j in range(0, MAIN_W, 512):
            wmt_ref[j:j + 512, :] = row_s[:, j:j + 512].T.astype(BF16)
        wft_ref[...] = row_s[:, MAIN_W:].T.astype(BF16)

    return _pc(
        body, name, grid=(d // tr,),
        in_specs=[pl.BlockSpec((N_DEV, tr, shard), lambda i: (0, i, 0))],
        out_specs=[pl.BlockSpec((tr, MAIN_W), lambda i: (i, 0)), pl.BlockSpec((tr, FC_PAD), lambda i: (i, 0)),
                   pl.BlockSpec((MAIN_W, tr), lambda i: (0, i)), pl.BlockSpec((FC_PAD, tr), lambda i: (0, i))],
        out_shape=[_sds((d, MAIN_W), BF16), _sds((d, FC_PAD), BF16), _sds((MAIN_W, d), BF16), _sds((FC_PAD, d), BF16)],
        scratch_shapes=[pltpu.VMEM((tr, wide), F32)],
        compiler_params=_params(),
    )(g_in)


def _w_out_parts(g_out):
    full_out = g_out.reshape(N_DEV * g_out.shape[1], g_out.shape[2])
    return full_out, full_out.T


def _layer_fwd(l, x, lbs, weights, lw, nb, rider_h=None, rider_c=None, target=None):
    w_main, w_fc, _, _, w_out, _ = lw
    bias = jnp.pad(weights["fox_f_bias"][l:l + 1], ((0, 0), (0, FC_PAD - FOX_HEADS)))
    wbd = _block_diag(weights["pool_w"][l]).astype(BF16)
    proj, fc, ht, qt, kt = _in_proj_fwd(x, weights["pre_norm_g"][l:l + 1], w_main, w_fc, f"in_proj_fwd_{l}")
    c_col, c_row = _fox_decay_fwd(fc, bias, nb, f"fox_decay_fwd_{l}")
    (o_h, s0), rode_h = _hgrn_fwd(proj, lbs[l:l + 1], _block_ones(HGRN_W, BF16), nb, f"hgrn_fwd_{l}", rider_h)
    (o_c, lse), rode_c = _fox_fwd(proj, kt, c_col, c_row, nb, f"fox_fwd_{l}", rider_c)
    if w_out is None:
        lw = tuple(lw[:4]) + _w_out_parts(rode_h[0])
        w_out, rode_h = lw[4], rode_h[1:]
    x_next, mixt, y = _merge_fwd(x, proj, o_h, o_c, weights["hgrn_norm_g"][l:l + 1], wbd, weights["pool_scale"][l:l + 1],
                                 w_out, weights["post_norm_g"][l:l + 1], nb, f"merge_fwd_{l}", target)
    return x_next, (x, proj, fc, ht, qt, kt, c_col, c_row, o_h, s0, o_c, lse, mixt, y, bias, wbd), lw, (rode_h, rode_c)


def _layer_bwd(l, dx, saved, lbs, weights, lw, nb, rider=None, send_w_out=None):
    _, proj, fc, ht, qt, kt, c_col, c_row, o_h, s0, o_c, lse, mixt, y, bias, wbd = saved
    w_out_t = lw[5]
    g = {}
    dy, dmix, dgp, da, dat, d_gc, delta = _merge_bwd(dx, y, weights["post_norm_g"][l:l + 1], w_out_t, proj, o_c,
                                                nb, f"merge_bwd_{l}")
    g["post_norm_g"] = dgp[0]
    g["w_out"] = _w_out_grad(mixt, dy, f"w_out_grad_{l}")
    (d_a, dgh, dlb), arrived = _hgrn_bwd(dmix, proj, o_h, s0, weights["hgrn_norm_g"][l:l + 1], lbs[l:l + 1],
                                         _block_ones(HGRN_W, BF16), nb, f"hgrn_bwd_{l}",
                                         None if send_w_out is None else send_w_out([g["w_out"]]))
    if arrived is not None:
        g["w_out_received"] = arrived[0]
    g["hgrn_norm_g"], g["lbs"] = dgh[0], dlb[0]
    d_b, dwbd, dps = _pool_bwd(dmix, proj, wbd, wbd.T, weights["pool_scale"][l:l + 1], nb, f"pool_bwd_{l}")
    g["pool_w"] = jnp.stack([dwbd[j * HEAD:(j + 1) * HEAD, j * HEAD:(j + 1) * HEAD] for j in range(len(POOL_WINDOWS))])
    g["pool_scale"] = dps[0]
    (d_qc, d_kc, d_vc, dc_k, dc_q), rode = _fox_bwd(proj, qt, kt, da, dat, c_col, c_row, lse, delta, nb,
                                                    f"fox_bwd_{l}", rider)
    d_fc, dbias = _fox_decay_bwd(dc_q, dc_k, fc, bias, nb, f"fox_decay_bwd_{l}")
    g["fox_f_bias"] = dbias[0, :FOX_HEADS]
    pieces = [(d_a, C_QA), (d_b, C_UB), (d_qc, C_QC), (d_kc, C_KC), (d_vc, C_VC), (d_gc, C_GC), (d_fc, None)]
    g["w_in"] = _w_in_grad(ht, pieces, f"w_in_grad_{l}")
    return g, pieces, rode


def _layer_bwd_input(l, dx, pieces, saved, weights, lw, rider=None):
    (dxi, dgpre), rode = _in_proj_bwd(pieces, lw[2], lw[3], saved[0], weights["pre_norm_g"][l:l + 1], dx,
                                      f"in_proj_bwd_{l}", rider)
    return dxi, dgpre[0], rode


def kernel(x, lower_bounds, pre_norm_g, w_in, hgrn_norm_g, fox_f_bias, pool_w, pool_scale, w_out, post_norm_g, loss_target, m_lower_bounds, m_pre_norm_g, m_w_in, m_hgrn_norm_g, m_fox_f_bias, m_pool_w, m_pool_scale, m_w_out, m_post_norm_g, v_lower_bounds, v_pre_norm_g, v_w_in, v_hgrn_norm_g, v_fox_f_bias, v_pool_w, v_pool_scale, v_w_out, v_post_norm_g):
    weights = dict(lower_bounds=lower_bounds, pre_norm_g=pre_norm_g, hgrn_norm_g=hgrn_norm_g, fox_f_bias=fox_f_bias,
                   pool_w=pool_w, pool_scale=pool_scale, post_norm_g=post_norm_g)
    mom_m = dict(lower_bounds=m_lower_bounds, pre_norm_g=m_pre_norm_g, hgrn_norm_g=m_hgrn_norm_g, fox_f_bias=m_fox_f_bias,
                 pool_w=m_pool_w, pool_scale=m_pool_scale, post_norm_g=m_post_norm_g)
    mom_v = dict(lower_bounds=v_lower_bounds, pre_norm_g=v_pre_norm_g, hgrn_norm_g=v_hgrn_norm_g, fox_f_bias=v_fox_f_bias,
                 pool_w=v_pool_w, pool_scale=v_pool_scale, post_norm_g=v_post_norm_g)
    depth = w_in.shape[0]
    nb, t, d = x.shape
    n = nb * t
    core = lax.axis_index("c").astype(jnp.int32).reshape(1)
    shards = [(w_in[l].astype(BF16), w_out[l].astype(BF16)) for l in range(depth)]
    lbs = _lower_bound_table(lower_bounds, "lower_bound_table")

    (g_in,) = _gather_weights(shards[0][0])
    coming = tuple(_assemble_w_in(g_in, "assemble_w_in_0")) + (None, None)
    xl, saved, lw = x.reshape(n, d), [], []
    for l in range(depth):
        last = l + 1 == depth
        ride_h = ([shards[l][1]] if coming[4] is None else []) + ([] if last else [shards[l + 1][1]])
        xl, sv, lw_l, (rode_h, rode_c) = _layer_fwd(
            l, xl, lbs, weights, coming, nb, _gather_rider(ride_h) if ride_h else None,
            None if last else _gather_rider([shards[l + 1][0]]), loss_target.reshape(n, d) if last else None)
        saved.append(sv)
        lw.append(lw_l)
        if not last:
            coming = tuple(_assemble_w_in(rode_c[0], f"assemble_w_in_{l + 1}")) + _w_out_parts(rode_h[0])
    dx, sq = xl
    loss_here = 0.5 * jnp.sum(sq) / d

    grads, recv_in, pending = [None] * depth, [None] * depth, None
    for l in reversed(range(depth)):
        g, pieces, rode = _layer_bwd(l, dx, saved[l], lbs, weights, lw[l], nb, pending, _direct_exchange_rider)
        if rode is not None:
            recv_in[l + 1] = rode[0]
        if l > 0:
            pending = _direct_exchange_rider([g["w_in"]])
            dx, g["pre_norm_g"], _ = _layer_bwd_input(l, dx, pieces, saved[l], weights, lw[l])
        else:
            (other,) = _run_rider(_swap_rider([g["w_in"]]), "grad_swap")
            summed = _pair_add(g["w_in"], other, core, "grad_pair_add")
            dx, g["pre_norm_g"], (recv_in[l],) = _layer_bwd_input(l, dx, pieces, saved[l], weights, lw[l],
                                                                  _chip_exchange_rider([summed]))
        grads[l] = g
    small = {k: jnp.stack([grads[l][k] for l in range(depth)]) for k in SMALL if k != "lower_bounds"}
    small["lower_bounds"] = _lower_bound_bwd(lower_bounds, jnp.stack([grads[l]["lbs"] for l in range(depth)]),
                                             "lower_bound_bwd")
    (r_small,) = _run_rider(_gather_rider([_pack_small(small, loss_here)]), "small_grads_gather")

    res_in, _ = _sum_adamw(recv_in, w_in, m_w_in, v_w_in, "adamw_w_in")
    res_out, _ = _sum_adamw([grads[l]["w_out_received"] for l in range(depth)], w_out, m_w_out, v_w_out, "adamw_w_out")
    res_small, _ = _sum_adamw([r_small], _pack_small(weights)[None], _pack_small(mom_m)[None], _pack_small(mom_v)[None],
                              "adamw_small")
    loss = res_small[0][0][_small_size(weights) // SMALL_LANES, _small_size(weights) % SMALL_LANES]

    names = ("lower_bounds", "pre_norm_g", "w_in", "hgrn_norm_g", "fox_f_bias", "pool_w", "pool_scale", "w_out", "post_norm_g")
    outs = [loss, dx.reshape(nb, t, d)]
    for i in range(4):
        full = dict(_unpack_small(res_small[i][0], weights), w_in=res_in[i], w_out=res_out[i])
        outs += [full[k] for k in names]
    return tuple(outs)
```

```python
import functools

import jax
import jax.numpy as jnp
from jax import lax
from jax.experimental import pallas as pl
from jax.experimental.pallas import tpu as pltpu

F32, BF16 = jnp.float32, jnp.bfloat16
MESH = pl.DeviceIdType.MESH
N_DEV = 8

NORM_EPS = 1e-6
MASK_VALUE = -1e30
TINY = 1e-30
CHUNK = 64
SUB = 16
HGRN_W, POOL_W, FOX_W = 256, 256, 512
HEAD = 64
FOX_HEADS = 8
POOL_WINDOWS = (2, 4, 8, 16)
POOL_HALO = 16
MAIN_W = 3584
FC_PAD = 128
C_QA, C_FA, C_IA, C_GA, C_UB, C_GB, C_QC, C_KC, C_VC, C_GC = 0, 256, 512, 768, 1024, 1280, 1536, 2048, 2560, 3072
FOX_SCALE = HEAD ** -0.5

ADAM_LR, ADAM_B1, ADAM_B2, ADAM_EPS, ADAM_WD, ADAM_STEP = 0.001, 0.9, 0.999, 1e-08, 0.01, 10

VMEM_LIMIT = 56 * 1024 * 1024


def _pc(fn, name, **kw):
    return pl.pallas_call(fn, name=name, **kw)


def _params(**kw):
    return pltpu.CompilerParams(vmem_limit_bytes=VMEM_LIMIT, **kw)


class _Rider:
    def __init__(self, inputs, out_shapes, n_sems, n_local, plan):
        self.inputs, self.out_shapes, self.n_sems, self.n_local, self.plan = list(inputs), list(out_shapes), n_sems, n_local, plan

    def start(self, ins, outs, *sems):
        sends, _, locs = self.plan(ins, outs, *sems)
        for cp in locs + sends:
            cp.start()

    def wait(self, ins, outs, *sems):
        sends, recvs, locs = self.plan(ins, outs, *sems)
        for cp in recvs:
            cp.wait_recv()
        for cp in sends:
            cp.wait_send()
        for cp in locs:
            cp.wait()

    def sem_shapes(self):
        return [pltpu.SemaphoreType.DMA((self.n_sems,)), pltpu.SemaphoreType.DMA((self.n_sems,)),
                pltpu.SemaphoreType.DMA((self.n_local,))]


def _call(body, name, args, rider=None, *, grid, in_specs, out_specs, out_shape, scratch_shapes=(), **kw):
    if rider is None:
        res = _pc(body, name, grid=grid, in_specs=in_specs, out_specs=out_specs, out_shape=out_shape,
                  scratch_shapes=list(scratch_shapes), **kw)(*args)
        return res, None
    n_in, n_out, n_scr = len(in_specs), len(out_specs), len(scratch_shapes)
    n_rin, n_rout = len(rider.inputs), len(rider.out_shapes)

    def ridden(*refs):
        ins, refs = refs[:n_in], refs[n_in:]
        rins, refs = refs[:n_rin], refs[n_rin:]
        outs, refs = refs[:n_out], refs[n_out:]
        routs, refs = refs[:n_rout], refs[n_rout:]
        scr, sems = refs[:n_scr], refs[n_scr:]
        first = functools.reduce(jnp.logical_and, [pl.program_id(a) == 0 for a in range(len(grid))])
        last = functools.reduce(jnp.logical_and, [pl.program_id(a) == g - 1 for a, g in enumerate(grid)])

        @pl.when(first)
        def _():
            rider.start(rins, routs, *sems)

        body(*ins, *outs, *scr)

        @pl.when(last)
        def _():
            rider.wait(rins, routs, *sems)

    any_spec = pl.BlockSpec(memory_space=pl.ANY)
    res = _pc(ridden, name, grid=grid, in_specs=list(in_specs) + [any_spec] * n_rin,
              out_specs=list(out_specs) + [any_spec] * n_rout, out_shape=list(out_shape) + rider.out_shapes,
              scratch_shapes=list(scratch_shapes) + rider.sem_shapes(), **kw)(*args, *rider.inputs)
    return res[:n_out], res[n_out:]


def _run_rider(rider, name):
    n_rin = len(rider.inputs)

    def body(*refs):
        ins, outs, sems = refs[:n_rin], refs[n_rin:n_rin + len(rider.out_shapes)], refs[n_rin + len(rider.out_shapes):]
        rider.start(ins, outs, *sems)
        rider.wait(ins, outs, *sems)

    any_spec = pl.BlockSpec(memory_space=pl.ANY)
    return _pc(body, name, in_specs=[any_spec] * n_rin, out_specs=[any_spec] * len(rider.out_shapes),
               out_shape=rider.out_shapes, scratch_shapes=rider.sem_shapes())(*rider.inputs)


def _dot(a, b):
    return jnp.dot(a, b, preferred_element_type=F32)


def _dot_nt(a, b):
    return lax.dot_general(a, b, (((1,), (1,)), ((), ())), preferred_element_type=F32)


def _dot_tn(a, b):
    return lax.dot_general(a, b, (((0,), (0,)), ((), ())), preferred_element_type=F32)


def _sel_dot_exact(sel, x):
    hi = x.astype(BF16)
    rest = x - hi.astype(F32)
    mid = rest.astype(BF16)
    lo = (rest - mid.astype(F32)).astype(BF16)
    sb = sel.astype(BF16)
    return _dot(sb, hi) + _dot(sb, mid) + _dot(sb, lo)


def _split2(x):
    hi = x.astype(BF16)
    return hi, (x - hi.astype(F32)).astype(BF16)


def _sel_dot(sel, x):
    hi, lo = _split2(x)
    sb = sel.astype(BF16)
    return _dot(sb, hi) + _dot(sb, lo)


def _dot_sel(x, sel):
    hi, lo = _split2(x)
    sb = sel.astype(BF16)
    return _dot(hi, sb) + _dot(lo, sb)


def _sigmoid(x):
    return 1.0 / (1.0 + jnp.exp(-x))


def _block_ones(n, dtype):
    r = lax.broadcasted_iota(jnp.int32, (n, n), 0) // HEAD
    c = lax.broadcasted_iota(jnp.int32, (n, n), 1) // HEAD
    return (r == c).astype(dtype)


def _sds(shape, dtype):
    return jax.ShapeDtypeStruct(shape, dtype)


def _in_proj_fwd(x, g_pre, w_main, w_fc, name):
    n, d = x.shape
    tm = min(512, n)

    def body(x_ref, g_ref, w_ref, wf_ref, proj_ref, fc_ref, ht_ref, qt_ref, kt_ref):
        xv = x_ref[...]
        r = lax.rsqrt(jnp.mean(xv * xv, axis=-1, keepdims=True) + NORM_EPS)
        hf = xv * r * g_ref[...]
        hb = hf.astype(BF16)
        ht_ref[...] = hf.T.astype(BF16)
        for j in range(0, MAIN_W, FOX_W):
            res = _dot(hb, w_ref[:, j:j + FOX_W])
            proj_ref[:, j:j + FOX_W] = res
            if j == C_QC:
                qt_ref[...] = (res * FOX_SCALE).T.astype(BF16)
            if j == C_KC:
                kt_ref[...] = res.T.astype(BF16)
        fc_ref[...] = _dot(hb, wf_ref[...])

    def cols(rows):
        return pl.BlockSpec((rows, tm), lambda i: (0, i))

    return _pc(
        body, name, grid=(n // tm,),
        in_specs=[pl.BlockSpec((tm, d), lambda i: (i, 0)), pl.BlockSpec((1, d), lambda i: (0, 0)),
                  pl.BlockSpec((d, MAIN_W), lambda i: (0, 0)), pl.BlockSpec((d, FC_PAD), lambda i: (0, 0))],
        out_specs=[pl.BlockSpec((tm, MAIN_W), lambda i: (i, 0)), pl.BlockSpec((tm, FC_PAD), lambda i: (i, 0)),
                   cols(d), cols(FOX_W), cols(FOX_W)],
        out_shape=[_sds((n, MAIN_W), F32), _sds((n, FC_PAD), F32), _sds((d, n), BF16), _sds((FOX_W, n), BF16),
                   _sds((FOX_W, n), BF16)],
        compiler_params=_params(),
    )(x, g_pre, w_main, w_fc)


def _fox_decay_fwd(fc, bias, nb, name):
    n = fc.shape[0]
    t = n // nb
    tt = min(256, t)
    nt = t // tt

    def body(fc_ref, b_ref, c_ref, cr_ref):
        r = lax.broadcasted_iota(jnp.int32, (tt, tt), 0)
        cc = lax.broadcasted_iota(jnp.int32, (tt, tt), 1)
        carry = jnp.zeros((1, FC_PAD), F32)
        for i in range(nt):
            rows = slice(i * tt, (i + 1) * tt)
            xv = fc_ref[rows, :] + b_ref[...]
            lf = jnp.minimum(xv, 0.0) - jnp.log(1.0 + jnp.exp(-jnp.abs(xv)))
            cs = _sel_dot_exact(r >= cc, lf) + carry
            c_ref[rows, :] = cs
            cr_ref[:, rows] = cs.T[:FOX_HEADS, :]
            carry = cs[tt - 1:tt, :]

    return _pc(
        body, name, grid=(nb,),
        in_specs=[pl.BlockSpec((t, FC_PAD), lambda b: (b, 0)), pl.BlockSpec((1, FC_PAD), lambda b: (0, 0))],
        out_specs=[pl.BlockSpec((t, FC_PAD), lambda b: (b, 0)), pl.BlockSpec((None, FOX_HEADS, t), lambda b: (b, 0, 0))],
        out_shape=[_sds((n, FC_PAD), F32), _sds((nb, FOX_HEADS, t), F32)],
        compiler_params=_params(),
    )(fc, bias)


def _hgrn_gates(q, z, lb):
    sig = _sigmoid(z)
    sn = _sigmoid(-z)
    f = lb + (1.0 - lb) * sig
    g = jnp.log(jnp.maximum(f, TINY))
    k = (1.0 - lb) * sn
    sq = _sigmoid(q)
    return sig, sn, f, g, k, sq


def _sub_tri(n, lower):
    r = lax.broadcasted_iota(jnp.int32, (n, n), 0)
    c = lax.broadcasted_iota(jnp.int32, (n, n), 1)
    tri = (r >= c) if lower else (r <= c)
    return jnp.logical_and(r // SUB == c // SUB, tri).astype(F32)


def _live_rows(t):
    return 8 * (t // 8 + 1)


def _pad_rows(x):
    return x if x.shape[0] == SUB else jnp.concatenate([x, jnp.zeros((SUB - x.shape[0], x.shape[1]), x.dtype)], axis=0)


def _hgrn_decays(qs, k, b):
    srow = lax.broadcasted_iota(jnp.int32, (SUB, HGRN_W), 0)
    es, ws = [], []
    for t in range(SUB):
        r = _live_rows(t)
        e = jnp.where(srow[:r] <= t, jnp.exp(b[t:t + 1, :] - b[:r]), 0.0)
        es.append(e)
        ws.append(_pad_rows(e * (qs[t:t + 1, :] * k[:r])))
    return srow, es, ws


def _hgrn_state_step(st, k, v, b, bmask):
    bl = b[SUB - 1:SUB, :]
    ktil = k * jnp.exp(bl - b)
    return st * jnp.exp(bl) + _dot_tn(v.astype(BF16), ktil.astype(BF16)) * bmask


def _hgrn_tile(t):
    return min(256, t)


def _hgrn_fwd(proj, lb, ones_b, nb, name, rider=None):
    n = proj.shape[0]
    t = n // nb
    tt = _hgrn_tile(t)
    nt = t // tt
    ncs = tt // CHUNK
    w = HGRN_W

    def body(q_ref, z_ref, v_ref, lb_ref, ones_ref, o_ref, s0_ref, st_s, b_s, qs_s, k_s):
        @pl.when(pl.program_id(0) == 0)
        def _():
            st_s[...] = jnp.zeros_like(st_s)

        for e in range(nb):
            q = q_ref[e]
            _, _, _, g, k, sq = _hgrn_gates(q, z_ref[e], lb_ref[...])
            b_s[e] = _sel_dot(_sub_tri(tt, True), g)
            qs_s[e] = q * sq
            k_s[e] = k
        bmask = _block_ones(w, F32)
        ones_b = ones_ref[...]
        nsub = CHUNK // SUB

        def chunk(c, carry):
            sts = [st_s[e] for e in range(nb)]
            for e in range(nb):
                s0_ref[e, c] = sts[e]
            base = pl.multiple_of(c * CHUNK, CHUNK)
            tiles = [[(qs_s[e, pl.ds(base + u * SUB, SUB), :], k_s[e, pl.ds(base + u * SUB, SUB), :],
                       v_ref[e, pl.ds(base + u * SUB, SUB), :], b_s[e, pl.ds(base + u * SUB, SUB), :])
                      for u in range(nsub)] for e in range(nb)]
            aexps = [[] for _ in range(nb)]
            for e in range(nb):
                for qs, k, v, b in tiles[e]:
                    _, _, ws = _hgrn_decays(qs, k, b)
                    aexps[e].append(_dot(jnp.concatenate(ws, axis=0).astype(BF16), ones_b))
            inters = [[] for _ in range(nb)]
            for u in range(nsub):
                for e in range(nb):
                    qs, k, v, b = tiles[e][u]
                    inters[e].append(_dot_nt((qs * jnp.exp(b)).astype(BF16), sts[e].astype(BF16)))
                    sts[e] = _hgrn_state_step(sts[e], k, v, b, bmask)
            for e in range(nb):
                st_s[e] = sts[e]
            for e in range(nb):
                for u, ((qs, k, v, b), aexp, o) in enumerate(zip(tiles[e], aexps[e], inters[e])):
                    for t in range(SUB):
                        r = _live_rows(t)
                        row = o[t:t + 1, :] + jnp.sum(aexp[t * SUB:t * SUB + r, :] * v[:r], axis=0, keepdims=True)
                        o_ref[e, pl.ds(base + u * SUB + t, 1), :] = row
            return carry

        lax.fori_loop(0, ncs, chunk, 0)

    def col(j):
        return pl.BlockSpec((nb, tt, w), lambda i: (0, i, j))

    proj3 = proj.reshape(nb, t, proj.shape[1])
    (o, s0), rode = _call(
        body, name, (proj3, proj3, proj3, lb, ones_b), rider, grid=(nt,),
        in_specs=[col(C_QA // w), col(C_FA // w), col(C_IA // w), pl.BlockSpec((1, w), lambda i: (0, 0)),
                  pl.BlockSpec((w, w), lambda i: (0, 0))],
        out_specs=[pl.BlockSpec((nb, tt, w), lambda i: (0, i, 0)),
                   pl.BlockSpec((nb, ncs, w, w), lambda i: (0, i, 0, 0))],
        out_shape=[_sds((nb, t, w), F32), _sds((nb, t // CHUNK, w, w), F32)],
        scratch_shapes=[pltpu.VMEM((nb, w, w), F32)] + [pltpu.VMEM((nb, tt, w), F32)] * 3,
        compiler_params=_params(),
    )
    return (o.reshape(n, w), s0.reshape(n // CHUNK, w, w)), rode


def _pool_lane_windows():
    lane = lax.broadcasted_iota(jnp.int32, (1, POOL_W), 1) // HEAD
    wl = jnp.zeros((1, POOL_W), F32)
    for gi, win in enumerate(POOL_WINDOWS):
        wl = jnp.where(lane == gi, float(win), wl)
    return lane, wl


def _pool_select(lane, parts):
    out = parts[-1]
    for gi in range(len(parts) - 2, -1, -1):
        out = jnp.where(lane == gi, parts[gi], out)
    return out


def _pool_mix(u, halo, t0, tt):
    lane, wl = _pool_lane_windows()
    ext = jnp.concatenate([halo, u], axis=0)
    sums, cur, shift = [], ext, 1
    for _ in POOL_WINDOWS:
        cur = cur + pltpu.roll(cur, shift, axis=0)
        sums.append(cur[POOL_HALO:, :])
        shift *= 2
    tpos = (t0 + lax.broadcasted_iota(jnp.int32, (tt, POOL_W), 0)).astype(F32)
    cnt = jnp.minimum(tpos + 1.0, wl)
    return _pool_select(lane, sums) / cnt - u, cnt


def _pool_specs(tt, nt, nhb):
    cu, cg = C_UB // POOL_W, C_GB // POOL_W
    return [pl.BlockSpec((tt, POOL_W), lambda b, i: (b * nt + i, cu)),
            pl.BlockSpec((tt, POOL_W), lambda b, i: (b * nt + i, cg)),
            pl.BlockSpec((POOL_HALO, POOL_W), lambda b, i: (jnp.maximum((b * nt + i) * nhb - 1, 0), cu))]


def _rows_reduce(x, op, final):
    while x.shape[0] > 8 and x.shape[0] % 16 == 0:
        half = x.shape[0] // 2
        x = op(x[:half], x[half:])
    return final(x, axis=0, keepdims=True)


def _tri_pair(step, n, group=1):
    counts = [a // group + 1 for a in range(n)]
    firsts = [sum(counts[:a]) for a in range(1, n)]
    a = sum([(step >= f).astype(jnp.int32) for f in firsts], jnp.int32(0))
    first = sum([jnp.where(step >= f, c, 0) for f, c in zip(firsts, counts)], jnp.int32(0))
    return a, step - first


def _tri_steps(n, group=1):
    return sum(a // group + 1 for a in range(n))


def _lane_lo():
    return lax.broadcasted_iota(jnp.int32, (1, 2 * HEAD), 1) < HEAD


def _put_col(tile, hh, colv):
    lane = lax.broadcasted_iota(jnp.int32, tile.shape, 1)
    return jnp.where(lane == hh, colv, tile)


def _fox_fwd(proj, kt, c_col, c_row, nb, name, rider=None):
    n = proj.shape[0]
    t = n // nb
    tb = min(256, t)
    nq = t // tb
    pw = 2 * HEAD
    kw = 2 if nq % 2 == 0 else 1
    nk = nq // kw

    def body(q_ref, kt_ref, v_ref, cc_ref, cr_ref, o_ref, lse_ref, m_s, acc_s, cq_s):
        qi, kk = _tri_pair(pl.program_id(1), nq, kw)

        @pl.when(kk == 0)
        def _():
            m_s[...] = jnp.full_like(m_s, -jnp.inf)
            acc_s[...] = jnp.zeros_like(acc_s)
            for hh in range(FOX_HEADS):
                cq_s[hh] = jnp.broadcast_to(cc_ref[:, hh:hh + 1], (tb, pw))

        def block(masked, sub):
            lo = _lane_lo()
            keys = slice(sub * tb, (sub + 1) * tb)
            if masked:
                causal = lax.broadcasted_iota(jnp.int32, (tb, tb), 0) >= lax.broadcasted_iota(jnp.int32, (tb, tb), 1)
            def lanes(hh):
                return lo if hh % 2 == 0 else jnp.logical_not(lo)

            def scores(hh):
                sl = slice((hh // 2) * pw, (hh // 2 + 1) * pw)
                return _dot(jnp.where(lanes(hh), q_ref[:, sl] * FOX_SCALE, 0.0).astype(BF16), kt_ref[sl, keys])

            ahead = scores(0)
            for hh in range(FOX_HEADS):
                s = ahead
                if hh + 1 < FOX_HEADS:
                    ahead = scores(hh + 1)
                s = s + (jnp.tile(cq_s[hh], (1, tb // pw)) - cr_ref[hh:hh + 1, keys])
                if masked:
                    s = jnp.where(causal, s, MASK_VALUE)
                m_prev = m_s[hh]
                m_new = jnp.maximum(m_prev, jnp.max(s, axis=1, keepdims=True))
                alpha = jnp.exp(m_prev - m_new)
                pe = jnp.exp(s - jnp.tile(m_new, (1, tb // pw)))
                m_s[hh] = m_new
                vf = v_ref[keys, (hh // 2) * pw:(hh // 2 + 1) * pw]
                acc_s[hh] = alpha * acc_s[hh] + _dot(pe.astype(BF16), jnp.where(lanes(hh), vf, 1.0).astype(BF16))

        def finish():
            lo = _lane_lo()
            m_all, l_all = jnp.zeros((tb, FC_PAD), F32), jnp.ones((tb, FC_PAD), F32)
            for p in range(FOX_HEADS // 2):
                a0, a1 = acc_s[2 * p], acc_s[2 * p + 1]
                both = pltpu.roll(jnp.where(lo, a1, a0), HEAD, axis=1)
                o_ref[:, p * pw:(p + 1) * pw] = jnp.where(lo, a0, a1) / both
                m_all = _put_col(_put_col(m_all, 2 * p, m_s[2 * p]), 2 * p + 1, m_s[2 * p + 1])
                l_all = _put_col(_put_col(l_all, 2 * p, both), 2 * p + 1, a1)
            lse_ref[...] = (m_all + jnp.log(l_all)).T[:FOX_HEADS, :]

        for sub in range(kw):
            @pl.when(kk * kw + sub < qi)
            def _(sub=sub):
                block(False, sub)

            @pl.when(kk * kw + sub == qi)
            def _(sub=sub):
                block(True, sub)
                finish()

    def qspec(wd, j):
        return pl.BlockSpec((tb, wd), lambda b, st: (b * nq + _tri_pair(st, nq, kw)[0], j))

    def kspec(j):
        return pl.BlockSpec((kw * tb, FOX_W), lambda b, st: (b * nk + _tri_pair(st, nq, kw)[1], j))

    return _call(
        body, name, (proj, kt, proj, c_col, c_row), rider, grid=(nb, _tri_steps(nq, kw)),
        in_specs=[qspec(FOX_W, C_QC // FOX_W),
                  pl.BlockSpec((FOX_W, kw * tb), lambda b, st: (0, b * nk + _tri_pair(st, nq, kw)[1])),
                  kspec(C_VC // FOX_W), qspec(FC_PAD, 0),
                  pl.BlockSpec((None, FOX_HEADS, kw * tb), lambda b, st: (b, 0, _tri_pair(st, nq, kw)[1]))],
        out_specs=[qspec(FOX_W, 0),
                   pl.BlockSpec((None, FOX_HEADS, tb), lambda b, st: (b, 0, _tri_pair(st, nq, kw)[0]))],
        out_shape=[_sds((n, FOX_W), F32), _sds((nb, FOX_HEADS, t), F32)],
        scratch_shapes=[pltpu.VMEM((FOX_HEADS, tb, pw), F32), pltpu.VMEM((FOX_HEADS, tb, pw), F32),
                        pltpu.VMEM((FOX_HEADS, tb, pw), F32)],
        compiler_params=_params(),
    )


def _head_mean(x, ones_f):
    return _dot_sel(x, ones_f) * (1.0 / HEAD)


def _merge_fwd(x, proj, o_h, o_c, gh, wbd, scale, w_out, g_post, nb, name, target=None):
    n, d = x.shape
    t = n // nb
    tm = min(512, t)
    nt = t // tm
    nhb = tm // POOL_HALO

    def body(*refs):
        x_ref, ga_ref, gc_ref, oh_ref, u_ref, gb_ref, h_ref, oc_ref, gh_ref, wb_ref, s_ref, w_ref, gp_ref = refs[:13]
        if target is None:
            xo_ref, mixt_ref, y_ref = refs[13:]
        else:
            t_ref, dx_ref, sq_ref, mixt_ref, y_ref = refs[13:]
        oh = oh_ref[...]
        ones_f = _block_ones(HGRN_W, F32)
        na = oh * lax.rsqrt(_head_mean(oh * oh, ones_f) + NORM_EPS) * gh_ref[...]
        ga, gb, gc = ga_ref[...], gb_ref[...], gc_ref[...]
        ti = pl.program_id(0) % nt
        pooled, _ = _pool_mix(u_ref[...], jnp.where(ti == 0, 0.0, h_ref[...]), ti * tm, tm)
        ob = _dot(pooled.astype(BF16), wb_ref[...]) * s_ref[...] * (gb * _sigmoid(gb))
        mixed = jnp.concatenate([na * (ga * _sigmoid(ga)), ob, oc_ref[...] * (gc * _sigmoid(gc))], axis=1)
        mixt_ref[...] = mixed.T.astype(BF16)
        y = _dot(mixed.astype(BF16), w_ref[...])
        y_ref[...] = y
        xn = x_ref[...] + y * lax.rsqrt(jnp.mean(y * y, axis=-1, keepdims=True) + NORM_EPS) * gp_ref[...]
        if target is None:
            xo_ref[...] = xn
        else:
            @pl.when(pl.program_id(0) == 0)
            def _():
                sq_ref[...] = jnp.zeros_like(sq_ref)

            e = xn - t_ref[...]
            dx_ref[...] = e * (1.0 / d)
            sq_ref[...] += jnp.sum(e * e, axis=0, keepdims=True)

    def row(wd, j=0):
        return pl.BlockSpec((tm, wd), lambda i: (i, j))

    def full(a, b):
        return pl.BlockSpec((a, b), lambda i: (0, 0))

    head = [] if target is None else [target]
    res = _pc(
        body, name, grid=(n // tm,),
        in_specs=[row(d), row(HGRN_W, C_GA // HGRN_W), row(FOX_W, C_GC // FOX_W), row(HGRN_W),
                  row(POOL_W, C_UB // POOL_W), row(POOL_W, C_GB // POOL_W),
                  pl.BlockSpec((POOL_HALO, POOL_W), lambda i: (jnp.maximum(i * nhb - 1, 0), C_UB // POOL_W)), row(FOX_W),
                  full(1, HGRN_W), full(POOL_W, POOL_W), full(1, POOL_W), full(d, d), full(1, d)] + [row(d)] * len(head),
        out_specs=[row(d)] + [full(1, d)] * len(head) + [pl.BlockSpec((d, tm), lambda i: (0, i)), row(d)],
        out_shape=[_sds((n, d), F32)] + [_sds((1, d), F32)] * len(head) + [_sds((d, n), BF16), _sds((n, d), F32)],
        compiler_params=_params(),
    )(x, proj, proj, o_h, proj, proj, proj, o_c, gh, wbd, scale, w_out, g_post, *head)
    return (res[0], res[1], res[2]) if target is None else ((res[0], res[1]), res[2], res[3])


def _rms_bwd(dy_scaled, xhat, r):
    return r * (dy_scaled - xhat * jnp.mean(dy_scaled * xhat, axis=-1, keepdims=True))


def _merge_bwd(dxo, y, g_post, w_out_t, proj, o_c, mixt, nb, name):
    n, d = y.shape
    t = n // nb
    tm = min(512, t)
    nt = t // tm
    wab = HGRN_W + POOL_W
    rows = d // N_DEV

    def body(dx_ref, y_ref, gp_ref, wt_ref, gc_ref, oc_ref, mt_ref,
             gw_ref, dm_ref, dgp_ref, da_ref, dat_ref, dg_ref, dl_ref, gw_s):
        @pl.when(pl.program_id(0) == 0)
        def _():
            dgp_ref[...] = jnp.zeros_like(dgp_ref)
            gw_s[...] = jnp.zeros_like(gw_s)

        yv, dxv = y_ref[...], dx_ref[...]
        r = lax.rsqrt(jnp.mean(yv * yv, axis=-1, keepdims=True) + NORM_EPS)
        yh = yv * r
        dgp_ref[...] += jnp.sum(dxv * yh, axis=0, keepdims=True)
        dyb = _rms_bwd(dxv * gp_ref[...], yh, r).astype(BF16)
        gw_s[...] += _dot(mt_ref[...], dyb)

        @pl.when(pl.program_id(0) == n // tm - 1)
        def _():
            for j in range(N_DEV):
                gw_ref[j % 2, j // 2] = gw_s[j * rows:(j + 1) * rows, :].astype(BF16)

        dm_ref[...] = _dot(dyb, wt_ref[:, :wab])
        dmc = _dot(dyb, wt_ref[:, wab:])
        gc, oc = gc_ref[...], oc_ref[...]
        sg = _sigmoid(gc)
        da = dmc * (gc * sg)
        da_ref[...] = da.astype(BF16)
        dat_ref[...] = da.T.astype(BF16)
        dg_ref[...] = (dmc * oc * (sg * (1.0 + gc * (1.0 - sg)))).astype(BF16)
        rr = lax.broadcasted_iota(jnp.int32, (FOX_W, FC_PAD), 0) // HEAD
        cc = lax.broadcasted_iota(jnp.int32, (FOX_W, FC_PAD), 1)
        dl_ref[...] = _dot_sel(da * oc, (rr == cc).astype(F32)).T[:FOX_HEADS, :]

    def row(wd, j=0):
        return pl.BlockSpec((tm, wd), lambda i: (i, j))

    def full(a, b):
        return pl.BlockSpec((a, b), lambda i: (0, 0))

    return _pc(
        body, name, grid=(n // tm,),
        in_specs=[row(d), row(d), full(1, d), full(d, d), row(FOX_W, C_GC // FOX_W), row(FOX_W),
                  pl.BlockSpec((d, tm), lambda i: (0, i))],
        out_specs=[pl.BlockSpec((2, N_DEV // 2, rows, d), lambda i: (0, 0, 0, 0)), row(wab), full(1, d), row(FOX_W),
                   pl.BlockSpec((FOX_W, tm), lambda i: (0, i)), row(FOX_W),
                   pl.BlockSpec((None, FOX_HEADS, tm), lambda i: (i // nt, 0, i % nt))],
        out_shape=[_sds((2, N_DEV // 2, rows, d), BF16), _sds((n, wab), F32), _sds((1, d), F32), _sds((n, FOX_W), BF16),
                   _sds((FOX_W, n), BF16), _sds((n, FOX_W), BF16), _sds((nb, FOX_HEADS, t), F32)],
        scratch_shapes=[pltpu.VMEM((d, d), F32)],
        compiler_params=_params(),
    )(dxo, y, g_post, w_out_t, proj, o_c, mixt)


def _w_in_grad(ht, pieces, name):
    d, n = ht.shape
    ta, tk = min(512, d), min(512, n)
    nk = n // tk
    arrays = [p for p, _ in pieces]
    widths = [p.shape[1] for p in arrays]
    offs = [sum(widths[:i]) for i in range(len(widths))]
    in_w = MAIN_W + FOX_HEADS
    shard = in_w // N_DEV

    def body(*refs):
        a_ref, p_refs = refs[0], refs[1:1 + len(arrays)]
        o_ref, acc = refs[1 + len(arrays):]
        k = pl.program_id(1)

        @pl.when(k == 0)
        def _():
            acc[...] = jnp.zeros_like(acc)

        a = a_ref[...]
        for pr, off, wd in zip(p_refs, offs, widths):
            for j in range(0, wd, 512):
                jw = min(512, wd - j)
                acc[:, off + j:off + j + jw] += _dot(a, pr[:, j:j + jw])

        @pl.when(k == nk - 1)
        def _():
            for j in range(N_DEV):
                o_ref[j % 2, j // 2] = acc[:, j * shard:(j + 1) * shard].astype(BF16)

    return _pc(
        body, name, grid=(d // ta, nk),
        in_specs=[pl.BlockSpec((ta, tk), lambda i, k: (i, k))] + [pl.BlockSpec((tk, wd), lambda i, k: (k, 0)) for wd in widths],
        out_specs=pl.BlockSpec((2, N_DEV // 2, ta, shard), lambda i, k: (0, 0, i, 0)),
        out_shape=_sds((2, N_DEV // 2, d, shard), BF16),
        scratch_shapes=[pltpu.VMEM((ta, sum(widths)), F32)],
        compiler_params=_params(),
    )(ht, *arrays)


def _hgrn_state_bwd(qs, k, v, b, do, s0, ds1, bmask):
    bl = b[SUB - 1:SUB, :]
    eb, ebl, ekt = jnp.exp(b), jnp.exp(bl), jnp.exp(bl - b)
    qe, ktil = qs * eb, k * ekt
    ds1b, dob = ds1.astype(BF16), do.astype(BF16)
    dv = _dot_nt(ktil.astype(BF16), ds1b)
    dqe = _dot(dob, s0.astype(BF16))
    dktil = _dot(v.astype(BF16), ds1b)
    dbl = jnp.sum(dktil * ktil, axis=0, keepdims=True) + ebl * jnp.sum(s0 * ds1, axis=0, keepdims=True)
    ds0 = ds1 * ebl + _dot_tn(dob, qe.astype(BF16)) * bmask
    return dqe * eb, dktil * ekt, dv, dbl, ds0


def _hgrn_intra_bwd(qs, k, v, do, es, aexp, gexp, dq, dk, dv, put_dq_row):
    dks = [dk[j:j + 8] for j in range(0, SUB, 8)]
    dvs = [dv[j:j + 8] for j in range(0, SUB, 8)]
    for t in range(SUB):
        r = _live_rows(t)
        ge = gexp[t * SUB:t * SUB + r, :] * es[t]
        put_dq_row(t, dq[t:t + 1, :] + jnp.sum(ge * k[:r], axis=0, keepdims=True))
        for j in range(r // 8):
            dks[j] = dks[j] + ge[8 * j:8 * j + 8] * qs[t:t + 1, :]
            dvs[j] = dvs[j] + aexp[t * SUB + 8 * j:t * SUB + 8 * j + 8, :] * do[t:t + 1, :]
    return jnp.concatenate(dks, axis=0), jnp.concatenate(dvs, axis=0)


def _hgrn_bwd(dmix, proj, o_h, s0, gh, lb, ones_b, nb, name, rider=None):
    n = proj.shape[0]
    t = n // nb
    tt = _hgrn_tile(t)
    nt = t // tt
    ncs = tt // CHUNK
    nsub = CHUNK // SUB
    w = HGRN_W

    def body(dm_ref, q_ref, z_ref, v_ref, ga_ref, oh_ref, s0_ref, gh_ref, lb_ref, ones_ref,
             dp_ref, dgh_ref, dlb_ref, ds_s, ss_s, b_s, qs_s, k_s, do_s, dq_s, dk_s, dv_s, dbl_s):
        @pl.when(pl.program_id(0) == 0)
        def _():
            dgh_ref[...] = jnp.zeros_like(dgh_ref)
            dlb_ref[...] = jnp.zeros_like(dlb_ref)
            ds_s[...] = jnp.zeros_like(ds_s)

        ones_b = ones_ref[...]
        ones_f = ones_b.astype(F32)
        bmask = _block_ones(w, F32)
        lbv, ghv = lb_ref[...], gh_ref[...]
        kept = []
        for e in range(nb):
            oh, ga, dm = oh_ref[e], ga_ref[e], dm_ref[e]
            rn = lax.rsqrt(_head_mean(oh * oh, ones_f) + NORM_EPS)
            nh = oh * rn
            sga = _sigmoid(ga)
            dp_ref[e, :, 3 * w:4 * w] = (dm * nh * ghv * (sga * (1.0 + ga * (1.0 - sga)))).astype(BF16)
            dn = dm * (ga * sga)
            dgh_ref[...] += jnp.sum(dn * nh, axis=0, keepdims=True)
            dn = dn * ghv
            do_s[e] = rn * (dn - nh * _head_mean(dn * nh, ones_f))
            q = q_ref[e]
            sig, sn, f, g, k, sq = _hgrn_gates(q, z_ref[e], lbv)
            qs = q * sq
            b_s[e] = _sel_dot(_sub_tri(tt, True), g)
            qs_s[e] = qs
            k_s[e] = k
            kept.append((q, sig, sn, f, k, sq, qs))

        def chunk(cc, carry):
            c = ncs - 1 - cc
            base = pl.multiple_of(c * CHUNK, CHUNK)
            tiles = [[(qs_s[e, pl.ds(base + u * SUB, SUB), :], k_s[e, pl.ds(base + u * SUB, SUB), :],
                       v_ref[e, pl.ds(base + u * SUB, SUB), :], b_s[e, pl.ds(base + u * SUB, SUB), :],
                       do_s[e, pl.ds(base + u * SUB, SUB), :]) for u in range(nsub)] for e in range(nb)]
            sts = [s0_ref[e, c] for e in range(nb)]
            for u in range(nsub):
                for e in range(nb):
                    qs, k, v, b, do = tiles[e][u]
                    ss_s[e, u] = sts[e]
                    if u < nsub - 1:
                        sts[e] = _hgrn_state_step(sts[e], k, v, b, bmask)
            dss = [ds_s[e] for e in range(nb)]
            for u in reversed(range(nsub)):
                for e in range(nb):
                    qs, k, v, b, do = tiles[e][u]
                    _, es, ws = _hgrn_decays(qs, k, b)
                    gs = [_pad_rows(do[t:t + 1, :] * v[:_live_rows(t)]) for t in range(SUB)]
                    aexp = _dot(jnp.concatenate(ws, axis=0).astype(BF16), ones_b)
                    gexp = _dot(jnp.concatenate(gs, axis=0).astype(BF16), ones_b)
                    dq, dk, dv, dbl, dss[e] = _hgrn_state_bwd(qs, k, v, b, do, ss_s[e, u], dss[e], bmask)

                    def put_dq_row(i, row, e=e, r0=base + u * SUB):
                        dq_s[e, pl.ds(r0 + i, 1), :] = row

                    dk, dv = _hgrn_intra_bwd(qs, k, v, do, es, aexp, gexp, dq, dk, dv, put_dq_row)
                    dk_s[e, pl.ds(base + u * SUB, SUB), :] = dk
                    dv_s[e, pl.ds(base + u * SUB, SUB), :] = dv
                    dbl_s[e, pl.ds(base + u * SUB, SUB), :] = jnp.broadcast_to(dbl, (SUB, w))
            for e in range(nb):
                ds_s[e] = dss[e]
            return carry

        lax.fori_loop(0, ncs, chunk, 0)
        for e, (q, sig, sn, f, k, sq, qs) in enumerate(kept):
            dqs, dk = dq_s[e], dk_s[e]
            dg = _sel_dot(_sub_tri(tt, False), qs * dqs - k * dk) + dbl_s[e]
            dfz = jnp.where(f > TINY, dg / jnp.maximum(f, TINY), 0.0)
            dlb_ref[...] += jnp.sum(dfz * (1.0 - sig) - dk * sn, axis=0, keepdims=True)
            dp_ref[e, :, 0:w] = (dqs * (sq * (1.0 + q * (1.0 - sq)))).astype(BF16)
            dp_ref[e, :, w:2 * w] = ((dfz - dk) * (1.0 - lbv) * sig * sn).astype(BF16)
            dp_ref[e, :, 2 * w:3 * w] = dv_s[e].astype(BF16)

    def col(j):
        return pl.BlockSpec((nb, tt, w), lambda i: (0, nt - 1 - i, j))

    def full(a, bb):
        return pl.BlockSpec((a, bb), lambda i: (0, 0))

    proj3 = proj.reshape(nb, t, proj.shape[1])
    (dp, dgh, dlb), rode = _call(
        body, name, (dmix.reshape(nb, t, dmix.shape[1]), proj3, proj3, proj3, proj3, o_h.reshape(nb, t, w),
                     s0.reshape(nb, t // CHUNK, w, w), gh, lb, ones_b), rider, grid=(nt,),
        in_specs=[col(0), col(C_QA // w), col(C_FA // w), col(C_IA // w), col(C_GA // w), col(0),
                  pl.BlockSpec((nb, ncs, w, w), lambda i: (0, nt - 1 - i, 0, 0)), full(1, w), full(1, w), full(w, w)],
        out_specs=[pl.BlockSpec((nb, tt, 4 * w), lambda i: (0, nt - 1 - i, 0)), full(1, w), full(1, w)],
        out_shape=[_sds((nb, t, 4 * w), BF16), _sds((1, w), F32), _sds((1, w), F32)],
        scratch_shapes=[pltpu.VMEM((nb, w, w), F32), pltpu.VMEM((nb, nsub, w, w), F32)]
        + [pltpu.VMEM((nb, tt, w), F32)] * 8,
        compiler_params=_params(),
    )
    return (dp.reshape(n, 4 * w), dgh, dlb), rode


def _pool_bwd(dmix, proj, wbd, wbd_t, scale, nb, name):
    n = proj.shape[0]
    t = n // nb
    tt = min(512, t)
    nt = t // tt
    nhb = tt // POOL_HALO
    cu, cg, cm = C_UB // POOL_W, C_GB // POOL_W, HGRN_W // POOL_W

    def body(u_ref, g_ref, h_ref, dm_ref, gn_ref, dmn_ref, w_ref, wt_ref, s_ref, dp_ref, dw_ref, ds_ref):
        i = pl.program_id(1)
        first = jnp.logical_and(pl.program_id(0) == 0, i == 0)

        @pl.when(first)
        def _():
            dw_ref[...] = jnp.zeros_like(dw_ref)
            ds_ref[...] = jnp.zeros_like(ds_ref)

        sc = s_ref[...]
        halo = jnp.where(i == 0, 0.0, h_ref[...])
        pooled, cnt = _pool_mix(u_ref[...], halo, i * tt, tt)
        pb = pooled.astype(BF16)
        pre = _dot(pb, w_ref[...])
        gv, dm = g_ref[...], dm_ref[...]
        sg = _sigmoid(gv)
        silu = gv * sg
        dgb = dm * pre * sc * (sg * (1.0 + gv * (1.0 - sg)))
        ds_ref[...] += jnp.sum(dm * pre * silu, axis=0, keepdims=True)
        dpre = (dm * sc * silu).astype(BF16)
        dw_ref[...] += _dot_tn(pb, dpre)
        dpool = _dot(dpre, wt_ref[...])
        gn = gn_ref[...]
        dpre_n = (dmn_ref[...] * sc * (gn * _sigmoid(gn))).astype(BF16)
        dpool_n = jnp.where(i == nt - 1, 0.0, _dot(dpre_n, wt_ref[...]))
        lane, wl = _pool_lane_windows()
        tpos_n = ((i + 1) * tt + lax.broadcasted_iota(jnp.int32, (POOL_HALO, POOL_W), 0)).astype(F32)
        ext = jnp.concatenate([dpool / cnt, dpool_n / jnp.minimum(tpos_n + 1.0, wl)], axis=0)
        rows = tt + POOL_HALO
        sums, cur, shift = [], ext, 1
        for _ in POOL_WINDOWS:
            cur = cur + pltpu.roll(cur, rows - shift, axis=0)
            sums.append(cur[:tt, :])
            shift *= 2
        du = _pool_select(lane, sums) - dpool
        dp_ref[...] = jnp.concatenate([du, dgb], axis=1).astype(BF16)

    def nxt(b, i):
        return jnp.minimum((b * nt + i + 1) * nhb, n // POOL_HALO - 1)

    return _pc(
        body, name, grid=(nb, nt),
        in_specs=_pool_specs(tt, nt, nhb) + [
            pl.BlockSpec((tt, POOL_W), lambda b, i: (b * nt + i, cm)),
            pl.BlockSpec((POOL_HALO, POOL_W), lambda b, i: (nxt(b, i), cg)),
            pl.BlockSpec((POOL_HALO, POOL_W), lambda b, i: (nxt(b, i), cm)),
            pl.BlockSpec((POOL_W, POOL_W), lambda b, i: (0, 0)), pl.BlockSpec((POOL_W, POOL_W), lambda b, i: (0, 0)),
            pl.BlockSpec((1, POOL_W), lambda b, i: (0, 0))],
        out_specs=[pl.BlockSpec((tt, 2 * POOL_W), lambda b, i: (b * nt + i, 0)),
                   pl.BlockSpec((POOL_W, POOL_W), lambda b, i: (0, 0)), pl.BlockSpec((1, POOL_W), lambda b, i: (0, 0))],
        out_shape=[_sds((n, 2 * POOL_W), BF16), _sds((POOL_W, POOL_W), F32), _sds((1, POOL_W), F32)],
        compiler_params=_params(),
    )(proj, proj, proj, dmix, proj, dmix, wbd, wbd_t, scale)


def _fox_bwd(proj, qt, kt, da, dat, c_col, c_row, lse_row, delta_row, nb, name, rider=None):
    n = proj.shape[0]
    t = n // nb
    tb = min(256, t)
    nq = t // tb
    pw = 2 * HEAD
    qw = 2 if nq % 2 == 0 else 1
    nqs = nq // qw

    def body(q_ref, k_ref, v_ref, da_ref, qt_ref, kt_ref, dat_ref, cc_ref, cr_ref, lse_ref, dl_ref,
             dq_ref, dk_ref, dv_ref, dck_ref, dcq_ref, dq_s, dk_s, dv_s, dck_s, dcq_s):
        step = pl.program_id(1)
        kj, qq = pairs(step)

        @pl.when(step == 0)
        def _():
            dq_s[...] = jnp.zeros_like(dq_s)
            dcq_s[...] = jnp.zeros_like(dcq_s)

        @pl.when(qq == nqs - 1)
        def _():
            dk_s[...] = jnp.zeros_like(dk_s)
            dv_s[...] = jnp.zeros_like(dv_s)
            dck_s[...] = jnp.zeros_like(dck_s)

        def block(masked, sub):
            lo = _lane_lo()
            qi = qq * qw + sub
            qs = slice(sub * tb, (sub + 1) * tb)
            if masked:
                causal = lax.broadcasted_iota(jnp.int32, (tb, tb), 1) >= lax.broadcasted_iota(jnp.int32, (tb, tb), 0)
            dck = dck_s[...]
            for p in range(FOX_HEADS // 2):
                sl = slice(p * pw, (p + 1) * pw)
                qp = q_ref[qs, sl] * FOX_SCALE
                kp = k_ref[:, sl].astype(BF16)
                vp = v_ref[:, sl].astype(BF16)
                dap = da_ref[qs, sl]
                dk, dv = dk_s[:, sl], dv_s[:, sl]
                for h in range(2):
                    hh = 2 * p + h
                    lm = lo if h == 0 else jnp.logical_not(lo)
                    rows = slice(hh * HEAD, (hh + 1) * HEAD)
                    none = jnp.zeros((HEAD, tb), BF16)
                    qm = jnp.where(lm, qp, 0.0).astype(BF16)
                    dam = jnp.where(lm, dap, jnp.zeros_like(dap))
                    qtm = jnp.concatenate([qt_ref[rows, qs], none] if h == 0 else [none, qt_ref[rows, qs]], axis=0)
                    datm = jnp.concatenate([dat_ref[rows, qs], none] if h == 0 else [none, dat_ref[rows, qs]], axis=0)
                    s = _dot(kp, qtm) + (cr_ref[hh:hh + 1, qs] - cc_ref[:, hh:hh + 1])
                    pe = jnp.exp(s - lse_ref[hh:hh + 1, qs])
                    if masked:
                        pe = jnp.where(causal, pe, 0.0)
                    dp = _dot(vp, datm)
                    ds = pe * (dp - dl_ref[hh:hh + 1, qs])
                    dsb = ds.astype(BF16)
                    dv = dv + _dot(pe.astype(BF16), dam)
                    dk = dk + _dot(dsb, qm)
                    dq_s[qi, rows, :] += _dot(kt_ref[rows, :], dsb)
                    dck = dck - _put_col(jnp.zeros_like(dck), hh, jnp.sum(ds, axis=1, keepdims=True))
                    dcq_s[qi, hh:hh + 1, :] += _rows_reduce(ds, jnp.add, jnp.sum)
                dk_s[:, sl] = dk
                dv_s[:, sl] = dv
            dck_s[...] = dck

        for sub in reversed(range(qw)):
            @pl.when(qq * qw + sub > kj)
            def _(sub=sub):
                block(False, sub)

            @pl.when(qq * qw + sub == kj)
            def _(sub=sub):
                block(True, sub)

        @pl.when(qq == kj // qw)
        def _():
            dk_ref[...] = dk_s[...].astype(BF16)
            dv_ref[...] = dv_s[...].astype(BF16)
            dck_ref[...] = dck_s[...]

        @pl.when(step == _tri_steps(nq, qw) - 1)
        def _():
            for j in range(nq):
                dq_ref[j * tb:(j + 1) * tb, :] = (dq_s[j].T * FOX_SCALE).astype(BF16)
                dcq_ref[:, j * tb:(j + 1) * tb] = dcq_s[j]

    def pairs(step):
        a, b = _tri_pair(step, nq, qw)
        return nq - 1 - a, nqs - 1 - b

    def kspec(wd, j=0):
        return pl.BlockSpec((tb, wd), lambda b, st: (b * nq + pairs(st)[0], j))

    def qspec(wd, j=0):
        return pl.BlockSpec((qw * tb, wd), lambda b, st: (b * nqs + pairs(st)[1], j))

    def qrow():
        return pl.BlockSpec((None, FOX_HEADS, qw * tb), lambda b, st: (b, 0, pairs(st)[1]))

    def tspec(which):
        if which == 0:
            return pl.BlockSpec((FOX_W, tb), lambda b, st: (0, b * nq + pairs(st)[0]))
        return pl.BlockSpec((FOX_W, qw * tb), lambda b, st: (0, b * nqs + pairs(st)[1]))

    return _call(
        body, name, (proj, proj, proj, da, qt, kt, dat, c_col, c_row, lse_row, delta_row), rider,
        grid=(nb, _tri_steps(nq, qw)),
        in_specs=[qspec(FOX_W, C_QC // FOX_W), kspec(FOX_W, C_KC // FOX_W), kspec(FOX_W, C_VC // FOX_W), qspec(FOX_W),
                  tspec(1), tspec(0), tspec(1), kspec(FC_PAD), qrow(), qrow(), qrow()],
        out_specs=[pl.BlockSpec((t, FOX_W), lambda b, st: (b, 0)), kspec(FOX_W), kspec(FOX_W), kspec(FC_PAD),
                   pl.BlockSpec((None, FOX_HEADS, t), lambda b, st: (b, 0, 0))],
        out_shape=[_sds((n, FOX_W), BF16), _sds((n, FOX_W), BF16), _sds((n, FOX_W), BF16), _sds((n, FC_PAD), F32),
                   _sds((nb, FOX_HEADS, t), F32)],
        scratch_shapes=[pltpu.VMEM((nq, FOX_W, tb), F32), pltpu.VMEM((tb, FOX_W), F32), pltpu.VMEM((tb, FOX_W), F32),
                        pltpu.VMEM((tb, FC_PAD), F32), pltpu.VMEM((nq, FOX_HEADS, tb), F32)],
        compiler_params=_params(),
    )


def _fox_decay_bwd(dc_q, dc_k, fc, bias, nb, name):
    n = fc.shape[0]
    t = n // nb
    tt = min(256, t)
    nt = t // tt

    def body(dcq_ref, dck_ref, fc_ref, b_ref, dfc_ref, db_ref):
        @pl.when(pl.program_id(0) == 0)
        def _():
            db_ref[...] = jnp.zeros_like(db_ref)

        r = lax.broadcasted_iota(jnp.int32, (tt, tt), 0)
        cc = lax.broadcasted_iota(jnp.int32, (tt, tt), 1)
        carry = jnp.zeros((1, FC_PAD), F32)
        db = jnp.zeros((1, FC_PAD), F32)
        for i in reversed(range(nt)):
            rows = slice(i * tt, (i + 1) * tt)
            dcq = jnp.concatenate([dcq_ref[:, rows], jnp.zeros((FC_PAD - FOX_HEADS, tt), F32)], axis=0).T
            dlf = _sel_dot_exact(r <= cc, dcq + dck_ref[rows, :]) + carry
            carry = dlf[0:1, :]
            dfc = dlf * _sigmoid(-(fc_ref[rows, :] + b_ref[...]))
            dfc_ref[rows, :] = dfc.astype(BF16)
            db = db + jnp.sum(dfc, axis=0, keepdims=True)
        db_ref[...] += db

    def row():
        return pl.BlockSpec((t, FC_PAD), lambda b: (b, 0))

    return _pc(
        body, name, grid=(nb,),
        in_specs=[pl.BlockSpec((None, FOX_HEADS, t), lambda b: (b, 0, 0)), row(), row(),
                  pl.BlockSpec((1, FC_PAD), lambda b: (0, 0))],
        out_specs=[row(), pl.BlockSpec((1, FC_PAD), lambda b: (0, 0))],
        out_shape=[_sds((n, FC_PAD), BF16), _sds((1, FC_PAD), F32)],
        compiler_params=_params(),
    )(dc_q, dc_k, fc, bias)


def _in_proj_bwd(pieces, w_main_t, w_fc_t, x, g_pre, dxo, name, rider=None):
    n, d = x.shape
    tm = min(512, n)
    widths = [p.shape[1] for p, _ in pieces]
    offs = [o for _, o in pieces]
    np_ = len(pieces)

    def body(*refs):
        p_refs = refs[:np_]
        wt_ref, wf_ref, x_ref, g_ref, dxo_ref, dx_ref, dg_ref = refs[np_:]

        @pl.when(pl.program_id(0) == 0)
        def _():
            dg_ref[...] = jnp.zeros_like(dg_ref)

        dh = _dot(p_refs[-1][...], wf_ref[...])
        for pr, wd, off in zip(p_refs[:-1], widths[:-1], offs[:-1]):
            for j in range(0, wd, 512):
                jw = min(512, wd - j)
                dh = dh + _dot(pr[:, j:j + jw], wt_ref[off + j:off + j + jw, :])
        xv = x_ref[...]
        r = lax.rsqrt(jnp.mean(xv * xv, axis=-1, keepdims=True) + NORM_EPS)
        xh = xv * r
        dg_ref[...] += jnp.sum(dh * xh, axis=0, keepdims=True)
        dx_ref[...] = dxo_ref[...] + _rms_bwd(dh * g_ref[...], xh, r)

    row = pl.BlockSpec((tm, d), lambda i: (i, 0))
    return _call(
        body, name, (*[p for p, _ in pieces], w_main_t, w_fc_t, x, g_pre, dxo), rider, grid=(n // tm,),
        in_specs=[pl.BlockSpec((tm, wd), lambda i: (i, 0)) for wd in widths] + [
            pl.BlockSpec((MAIN_W, d), lambda i: (0, 0)), pl.BlockSpec((FC_PAD, d), lambda i: (0, 0)),
            row, pl.BlockSpec((1, d), lambda i: (0, 0)), row],
        out_specs=[row, pl.BlockSpec((1, d), lambda i: (0, 0))],
        out_shape=[_sds((n, d), F32), _sds((1, d), F32)],
        compiler_params=_params(),
    )


def _lower_bound_table(lower_bounds, name):
    depth, w = lower_bounds.shape

    def body(lb_ref, o_ref):
        v = lb_ref[...]
        e = jnp.exp(v - jnp.max(v, axis=0, keepdims=True))
        p = e / jnp.sum(e, axis=0, keepdims=True)
        acc = jnp.zeros((1, w), F32)
        for l in range(depth):
            acc = acc + p[l:l + 1, :]
            o_ref[l:l + 1, :] = acc - p[0:1, :]

    return _pc(body, name, out_shape=_sds((depth, w), F32))(lower_bounds)


def _lower_bound_bwd(lower_bounds, dlbs, name):
    depth, w = lower_bounds.shape

    def body(lb_ref, d_ref, o_ref):
        v, dl = lb_ref[...], d_ref[...]
        e = jnp.exp(v - jnp.max(v, axis=0, keepdims=True))
        p = e / jnp.sum(e, axis=0, keepdims=True)
        tot = jnp.sum(dl, axis=0, keepdims=True)
        rows, tail = [], tot
        for l in range(depth):
            rows.append(tail - tot if l == 0 else tail)
            tail = tail - dl[l:l + 1, :]
        dp = jnp.concatenate(rows, axis=0)
        o_ref[...] = p * (dp - jnp.sum(p * dp, axis=0, keepdims=True))

    return _pc(body, name, out_shape=_sds((depth, w), F32))(lower_bounds, dlbs)


def _place():
    x, y, c = lax.axis_index("x"), lax.axis_index("y"), lax.axis_index("c")
    return x, y, c


def _gather_weights(*arrays):
    na = len(arrays)

    def body(*refs):
        ins, outs = refs[:na], refs[na:2 * na]
        send_sems, recv_sems, local_sems = refs[2 * na:]
        x, y, c = _place()
        me, sibling = (x, y, c), (x, y, 1 - c)
        chips = [(1 - x, y), (x, 1 - y), (1 - x, 1 - y)]

        def slot(a, px, py, pc):
            return outs[a].at[4 * px + 2 * py + pc]

        def copy(a, k, block, to, own=False):
            return pltpu.make_async_remote_copy(
                src_ref=ins[a] if own else slot(a, *block), dst_ref=slot(a, *block),
                send_sem=send_sems.at[a * 7 + k], recv_sem=recv_sems.at[a * 7 + k],
                device_id=to, device_id_type=MESH)

        mine = [pltpu.make_async_copy(ins[a], slot(a, *me), local_sems.at[a]) for a in range(na)]
        for cp in mine:
            cp.start()
        first = []
        for a in range(na):
            first.append(copy(a, 0, me, sibling, own=True))
            first += [copy(a, 1 + j, me, (*chip, c), own=True) for j, chip in enumerate(chips)]
        for cp in first:
            cp.start()
        passed = []
        for j, chip in enumerate(chips):
            for a in range(na):
                copy(a, 1 + j, (*chip, c), me).wait_recv()
                fw = copy(a, 4 + j, (*chip, c), sibling)
                fw.start()
                passed.append(fw)
        for a in range(na):
            copy(a, 0, sibling, me).wait_recv()
            for j, chip in enumerate(chips):
                copy(a, 4 + j, (*chip, 1 - c), me).wait_recv()
        for cp in first + passed:
            cp.wait_send()
        for cp in mine:
            cp.wait()

    any_spec = pl.BlockSpec(memory_space=pl.ANY)
    return _pc(
        body, "gather_weights",
        in_specs=[any_spec] * na, out_specs=[any_spec] * na,
        out_shape=[_sds((N_DEV,) + a.shape, a.dtype) for a in arrays],
        scratch_shapes=[pltpu.SemaphoreType.DMA((7 * na,)), pltpu.SemaphoreType.DMA((7 * na,)),
                        pltpu.SemaphoreType.DMA((na,))],
    )(*arrays)


def _peer(k):
    x, y, c = _place()
    return (1 - x if k & 4 else x, 1 - y if k & 2 else y, 1 - c if k & 1 else c)


def _remote(src, dst, sems, s, to):
    return pltpu.make_async_remote_copy(src_ref=src, dst_ref=dst, send_sem=sems[0].at[s], recv_sem=sems[1].at[s],
                                        device_id=to, device_id_type=MESH)


def _gather_rider(shards):
    na = len(shards)

    def plan(ins, outs, *sems):
        x, y, c = _place()
        me = 4 * x + 2 * y + c
        locs = [pltpu.make_async_copy(ins[a], outs[a].at[me], sems[2].at[a]) for a in range(na)]
        sends, recvs = [], []
        for k in range(1, N_DEV):
            px, py, pc = _peer(k)
            for a in range(na):
                s = (k - 1) * na + a
                sends.append(_remote(ins[a], outs[a].at[me], sems, s, (px, py, pc)))
                recvs.append(_remote(ins[a], outs[a].at[4 * px + 2 * py + pc], sems, s, (px, py, pc)))
        return sends, recvs, locs

    return _Rider(shards, [_sds((N_DEV,) + a.shape, a.dtype) for a in shards], (N_DEV - 1) * na, na, plan)


def _direct_exchange_rider(blocks):
    na = len(blocks)

    def plan(ins, outs, *sems):
        x, y, c = _place()
        me = 4 * x + 2 * y + c
        locs = [pltpu.make_async_copy(ins[a].at[c, 2 * x + y], outs[a].at[me], sems[2].at[a]) for a in range(na)]
        sends, recvs = [], []
        for k in range(1, N_DEV):
            px, py, pc = _peer(k)
            for a in range(na):
                s = (k - 1) * na + a
                sends.append(_remote(ins[a].at[pc, 2 * px + py], outs[a].at[me], sems, s, (px, py, pc)))
                recvs.append(_remote(ins[a].at[pc, 2 * px + py], outs[a].at[4 * px + 2 * py + pc], sems, s, (px, py, pc)))
        return sends, recvs, locs

    return _Rider(blocks, [_sds((N_DEV,) + a.shape[2:], a.dtype) for a in blocks], (N_DEV - 1) * na, na, plan)


def _swap_rider(halves):
    na = len(halves)

    def plan(ins, outs, *sems):
        x, y, c = _place()
        cps = [_remote(ins[a].at[1 - c], outs[a], sems, a, (x, y, 1 - c)) for a in range(na)]
        return cps, cps, []

    return _Rider(halves, [_sds(a.shape[1:], a.dtype) for a in halves], na, 1, plan)


def _chip_exchange_rider(parts, small=None):
    na = len(parts)
    n_chip = N_DEV // 2

    def plan(ins, outs, *sems):
        x, y, c = _place()
        chip = 2 * x + y
        locs = [pltpu.make_async_copy(ins[a].at[chip], outs[a].at[chip], sems[2].at[a]) for a in range(na)]
        sends, recvs = [], []
        for k in range(1, n_chip):
            px, py, _ = _peer(2 * k)
            for a in range(na):
                s = (k - 1) * na + a
                sends.append(_remote(ins[a].at[2 * px + py], outs[a].at[chip], sems, s, (px, py, c)))
                recvs.append(_remote(ins[a].at[2 * px + py], outs[a].at[2 * px + py], sems, s, (px, py, c)))
        if small is not None:
            me = 2 * chip + c
            locs.append(pltpu.make_async_copy(ins[na], outs[na].at[me], sems[2].at[na]))
            for k in range(1, N_DEV):
                px, py, pc = _peer(k)
                s = (n_chip - 1) * na + k - 1
                sends.append(_remote(ins[na], outs[na].at[me], sems, s, (px, py, pc)))
                recvs.append(_remote(ins[na], outs[na].at[4 * px + 2 * py + pc], sems, s, (px, py, pc)))
        return sends, recvs, locs

    extra = [] if small is None else [small]
    shapes = [_sds(a.shape, a.dtype) for a in parts] + [_sds((N_DEV,) + s.shape, s.dtype) for s in extra]
    n_sems = (n_chip - 1) * na + (N_DEV - 1) * len(extra)
    return _Rider(list(parts) + extra, shapes, n_sems, na + len(extra), plan)


def _pair_add(halves, other, core, name):
    _, nch, r, c = halves.shape

    def body(c_ref, h_ref, o_ref, p_ref):
        p_ref[...] = (h_ref[...].astype(F32) + o_ref[...].astype(F32)).astype(BF16)

    blk = pl.BlockSpec((None, r, c), lambda j, c_ref: (j, 0, 0))
    return _pc(
        body, name,
        grid_spec=pltpu.PrefetchScalarGridSpec(
            num_scalar_prefetch=1, grid=(nch,),
            in_specs=[pl.BlockSpec((None, None, r, c), lambda j, c_ref: (c_ref[0], j, 0, 0)), blk], out_specs=blk),
        out_shape=_sds((nch, r, c), BF16),
        compiler_params=_params(),
    )(core, halves, other)


def _sum_adamw(parts, w, m, v, name, rider=None):
    nl, r, c = w.shape
    tr = 256 if r % 256 == 0 else r

    def body(*refs):
        p_refs = refs[:nl]
        w_ref, m_ref, v_ref, g_ref, d_ref, mo_ref, vo_ref = refs[nl:]
        for l in range(nl):
            @pl.when(pl.program_id(0) == l)
            def _(p_ref=p_refs[l]):
                g = p_ref[0].astype(F32)
                for j in range(1, p_ref.shape[0]):
                    g = g + p_ref[j].astype(F32)
                mn = ADAM_B1 * m_ref[...] + (1.0 - ADAM_B1) * g
                vn = ADAM_B2 * v_ref[...] + (1.0 - ADAM_B2) * (g * g)
                m_hat = mn / (1.0 - ADAM_B1 ** ADAM_STEP)
                v_hat = vn / (1.0 - ADAM_B2 ** ADAM_STEP)
                g_ref[...] = g
                d_ref[...] = -ADAM_LR * (m_hat / (jnp.sqrt(v_hat) + ADAM_EPS) + ADAM_WD * w_ref[...])
                mo_ref[...] = mn
                vo_ref[...] = vn

    def part_spec(l, k):
        return pl.BlockSpec((k, tr, c), lambda li, i: (0, jnp.where(li == l, i, 0), 0))

    row = pl.BlockSpec((None, tr, c), lambda li, i: (li, i, 0))
    return _call(
        body, name, (*parts, w, m, v), rider, grid=(nl, r // tr),
        in_specs=[part_spec(l, p.shape[0]) for l, p in enumerate(parts)] + [row, row, row],
        out_specs=[row] * 4,
        out_shape=[_sds((nl, r, c), F32)] * 4,
        compiler_params=_params(),
    )


SMALL = ("lower_bounds", "pre_norm_g", "hgrn_norm_g", "pool_w", "pool_scale", "post_norm_g", "fox_f_bias")
SMALL_LANES = 128


def _small_size(tree):
    return sum(tree[k].size for k in SMALL)


def _pack_small(tree, extra=None):
    flat = jnp.concatenate([tree[k].reshape(-1) for k in SMALL] + ([] if extra is None else [extra.reshape(1)]))
    rows = -(-(_small_size(tree) + 1) // (8 * SMALL_LANES)) * 8
    return jnp.pad(flat, (0, rows * SMALL_LANES - flat.shape[0])).reshape(rows, SMALL_LANES)


def _unpack_small(packed, like):
    out, off = {}, 0
    for k in SMALL:
        size = like[k].size
        assert off % SMALL_LANES == 0
        rows = packed[off // SMALL_LANES:-(-(off + size) // SMALL_LANES)]
        out[k] = rows.reshape(-1)[:size].reshape(like[k].shape)
        off += size
    return out


def _block_diag(pw):
    g = pw.shape[0]
    eye = jnp.eye(g, dtype=pw.dtype)
    return (eye[:, None, :, None] * pw[:, :, None, :]).reshape(g * HEAD, g * HEAD)


def _assemble_w_in(g_in, name):
    _, d, shard = g_in.shape
    tr = min(256, d)
    wide = MAIN_W + FC_PAD

    def body(g_ref, wm_ref, wf_ref, wmt_ref, wft_ref, row_s):
        row_s[:, MAIN_W:] = jnp.zeros((tr, FC_PAD), F32)
        for j in range(N_DEV):
            row_s[:, j * shard:(j + 1) * shard] = g_ref[j].astype(F32)
        wm_ref[...] = row_s[:, :MAIN_W].astype(BF16)
        wf_ref[...] = row_s[:, MAIN_W:].astype(BF16)
        for j in range(0, MAIN_W, 512):
            wmt_ref[j:j + 512, :] = row_s[:, j:j + 512].T.astype(BF16)
        wft_ref[...] = row_s[:, MAIN_W:].T.astype(BF16)

    return _pc(
        body, name, grid=(d // tr,),
        in_specs=[pl.BlockSpec((N_DEV, tr, shard), lambda i: (0, i, 0))],
        out_specs=[pl.BlockSpec((tr, MAIN_W), lambda i: (i, 0)), pl.BlockSpec((tr, FC_PAD), lambda i: (i, 0)),
                   pl.BlockSpec((MAIN_W, tr), lambda i: (0, i)), pl.BlockSpec((FC_PAD, tr), lambda i: (0, i))],
        out_shape=[_sds((d, MAIN_W), BF16), _sds((d, FC_PAD), BF16), _sds((MAIN_W, d), BF16), _sds((FC_PAD, d), BF16)],
        scratch_shapes=[pltpu.VMEM((tr, wide), F32)],
        compiler_params=_params(),
    )(g_in)


def _w_out_parts(g_out):
    full_out = g_out.reshape(N_DEV * g_out.shape[1], g_out.shape[2])
    return full_out, full_out.T


def _layer_fwd(l, x, lbs, weights, lw, nb, rider_h=None, rider_c=None, target=None):
    w_main, w_fc, _, _, w_out, _ = lw
    bias = jnp.pad(weights["fox_f_bias"][l:l + 1], ((0, 0), (0, FC_PAD - FOX_HEADS)))
    wbd = _block_diag(weights["pool_w"][l]).astype(BF16)
    proj, fc, ht, qt, kt = _in_proj_fwd(x, weights["pre_norm_g"][l:l + 1], w_main, w_fc, f"in_proj_fwd_{l}")
    c_col, c_row = _fox_decay_fwd(fc, bias, nb, f"fox_decay_fwd_{l}")
    (o_h, s0), rode_h = _hgrn_fwd(proj, lbs[l:l + 1], _block_ones(HGRN_W, BF16), nb, f"hgrn_fwd_{l}", rider_h)
    (o_c, lse), rode_c = _fox_fwd(proj, kt, c_col, c_row, nb, f"fox_fwd_{l}", rider_c)
    if w_out is None:
        lw = tuple(lw[:4]) + _w_out_parts(rode_h[0])
        w_out, rode_h = lw[4], rode_h[1:]
    x_next, mixt, y = _merge_fwd(x, proj, o_h, o_c, weights["hgrn_norm_g"][l:l + 1], wbd, weights["pool_scale"][l:l + 1],
                                 w_out, weights["post_norm_g"][l:l + 1], nb, f"merge_fwd_{l}", target)
    return x_next, (x, proj, fc, ht, qt, kt, c_col, c_row, o_h, s0, o_c, lse, mixt, y, bias, wbd), lw, (rode_h, rode_c)


def _layer_bwd(l, dx, saved, lbs, weights, lw, nb, rider=None, send_w_out=None):
    _, proj, fc, ht, qt, kt, c_col, c_row, o_h, s0, o_c, lse, mixt, y, bias, wbd = saved
    w_out_t = lw[5]
    g = {}
    g["w_out"], dmix, dgp, da, dat, d_gc, delta = _merge_bwd(dx, y, weights["post_norm_g"][l:l + 1], w_out_t, proj, o_c,
                                                         mixt, nb, f"merge_bwd_{l}")
    g["post_norm_g"] = dgp[0]
    (d_a, dgh, dlb), arrived = _hgrn_bwd(dmix, proj, o_h, s0, weights["hgrn_norm_g"][l:l + 1], lbs[l:l + 1],
                                         _block_ones(HGRN_W, BF16), nb, f"hgrn_bwd_{l}",
                                         None if send_w_out is None else send_w_out([g["w_out"]]))
    if arrived is not None:
        g["w_out_received"] = arrived[0]
    g["hgrn_norm_g"], g["lbs"] = dgh[0], dlb[0]
    d_b, dwbd, dps = _pool_bwd(dmix, proj, wbd, wbd.T, weights["pool_scale"][l:l + 1], nb, f"pool_bwd_{l}")
    g["pool_w"] = jnp.stack([dwbd[j * HEAD:(j + 1) * HEAD, j * HEAD:(j + 1) * HEAD] for j in range(len(POOL_WINDOWS))])
    g["pool_scale"] = dps[0]
    (d_qc, d_kc, d_vc, dc_k, dc_q), rode = _fox_bwd(proj, qt, kt, da, dat, c_col, c_row, lse, delta, nb,
                                                    f"fox_bwd_{l}", rider)
    d_fc, dbias = _fox_decay_bwd(dc_q, dc_k, fc, bias, nb, f"fox_decay_bwd_{l}")
    g["fox_f_bias"] = dbias[0, :FOX_HEADS]
    pieces = [(d_a, C_QA), (d_b, C_UB), (d_qc, C_QC), (d_kc, C_KC), (d_vc, C_VC), (d_gc, C_GC), (d_fc, None)]
    g["w_in"] = _w_in_grad(ht, pieces, f"w_in_grad_{l}")
    return g, pieces, rode


def _layer_bwd_input(l, dx, pieces, saved, weights, lw, rider=None):
    (dxi, dgpre), rode = _in_proj_bwd(pieces, lw[2], lw[3], saved[0], weights["pre_norm_g"][l:l + 1], dx,
                                      f"in_proj_bwd_{l}", rider)
    return dxi, dgpre[0], rode


def kernel(x, lower_bounds, pre_norm_g, w_in, hgrn_norm_g, fox_f_bias, pool_w, pool_scale, w_out, post_norm_g, loss_target, m_lower_bounds, m_pre_norm_g, m_w_in, m_hgrn_norm_g, m_fox_f_bias, m_pool_w, m_pool_scale, m_w_out, m_post_norm_g, v_lower_bounds, v_pre_norm_g, v_w_in, v_hgrn_norm_g, v_fox_f_bias, v_pool_w, v_pool_scale, v_w_out, v_post_norm_g):
    weights = dict(lower_bounds=lower_bounds, pre_norm_g=pre_norm_g, hgrn_norm_g=hgrn_norm_g, fox_f_bias=fox_f_bias,
                   pool_w=pool_w, pool_scale=pool_scale, post_norm_g=post_norm_g)
    mom_m = dict(lower_bounds=m_lower_bounds, pre_norm_g=m_pre_norm_g, hgrn_norm_g=m_hgrn_norm_g, fox_f_bias=m_fox_f_bias,
                 pool_w=m_pool_w, pool_scale=m_pool_scale, post_norm_g=m_post_norm_g)
    mom_v = dict(lower_bounds=v_lower_bounds, pre_norm_g=v_pre_norm_g, hgrn_norm_g=v_hgrn_norm_g, fox_f_bias=v_fox_f_bias,
                 pool_w=v_pool_w, pool_scale=v_pool_scale, post_norm_g=v_post_norm_g)
    depth = w_in.shape[0]
    nb, t, d = x.shape
    n = nb * t
    core = lax.axis_index("c").astype(jnp.int32).reshape(1)
    shards = [(w_in[l].astype(BF16), w_out[l].astype(BF16)) for l in range(depth)]
    lbs = _lower_bound_table(lower_bounds, "lower_bound_table")

    (g_in,) = _gather_weights(shards[0][0])
    coming = tuple(_assemble_w_in(g_in, "assemble_w_in_0")) + (None, None)
    xl, saved, lw = x.reshape(n, d), [], []
    for l in range(depth):
        last = l + 1 == depth
        ride_h = ([shards[l][1]] if coming[4] is None else []) + ([] if last else [shards[l + 1][1]])
        xl, sv, lw_l, (rode_h, rode_c) = _layer_fwd(
            l, xl, lbs, weights, coming, nb, _gather_rider(ride_h) if ride_h else None,
            None if last else _gather_rider([shards[l + 1][0]]), loss_target.reshape(n, d) if last else None)
        saved.append(sv)
        lw.append(lw_l)
        if not last:
            coming = tuple(_assemble_w_in(rode_c[0], f"assemble_w_in_{l + 1}")) + _w_out_parts(rode_h[0])
    dx, sq = xl
    loss_here = 0.5 * jnp.sum(sq) / d

    grads, recv_in, pending = [None] * depth, [None] * depth, None
    for l in reversed(range(depth)):
        g, pieces, rode = _layer_bwd(l, dx, saved[l], lbs, weights, lw[l], nb, pending, _direct_exchange_rider)
        if rode is not None:
            recv_in[l + 1] = rode[0]
        if l > 0:
            pending = _direct_exchange_rider([g["w_in"]])
            dx, g["pre_norm_g"], _ = _layer_bwd_input(l, dx, pieces, saved[l], weights, lw[l])
        else:
            (other,) = _run_rider(_swap_rider([g["w_in"]]), "grad_swap")
            summed = _pair_add(g["w_in"], other, core, "grad_pair_add")
            dx, g["pre_norm_g"], (recv_in[l],) = _layer_bwd_input(l, dx, pieces, saved[l], weights, lw[l],
                                                                  _chip_exchange_rider([summed]))
        grads[l] = g
    small = {k: jnp.stack([grads[l][k] for l in range(depth)]) for k in SMALL if k != "lower_bounds"}
    small["lower_bounds"] = _lower_bound_bwd(lower_bounds, jnp.stack([grads[l]["lbs"] for l in range(depth)]),
                                             "lower_bound_bwd")
    (r_small,) = _run_rider(_gather_rider([_pack_small(small, loss_here)]), "small_grads_gather")

    res_in, _ = _sum_adamw(recv_in, w_in, m_w_in, v_w_in, "adamw_w_in")
    res_out, _ = _sum_adamw([grads[l]["w_out_received"] for l in range(depth)], w_out, m_w_out, v_w_out, "adamw_w_out")
    res_small, _ = _sum_adamw([r_small], _pack_small(weights)[None], _pack_small(mom_m)[None], _pack_small(mom_v)[None],
                              "adamw_small")
    loss = res_small[0][0][_small_size(weights) // SMALL_LANES, _small_size(weights) % SMALL_LANES]

    names = ("lower_bounds", "pre_norm_g", "w_in", "hgrn_norm_g", "fox_f_bias", "pool_w", "pool_scale", "w_out", "post_norm_g")
    outs = [loss, dx.reshape(nb, t, d)]
    for i in range(4):
        full = dict(_unpack_small(res_small[i][0], weights), w_in=res_in[i], w_out=res_out[i])
        outs += [full[k] for k in names]
    return tuple(outs)
```

```python
import functools

import jax
import jax.numpy as jnp
from jax import lax
from jax.experimental import pallas as pl
from jax.experimental.pallas import tpu as pltpu

F32, BF16 = jnp.float32, jnp.bfloat16
MESH = pl.DeviceIdType.MESH
N_DEV = 8

NORM_EPS = 1e-6
MASK_VALUE = -1e30
TINY = 1e-30
CHUNK = 64
SUB = 16
HGRN_W, POOL_W, FOX_W = 256, 256, 512
HEAD = 64
FOX_HEADS = 8
POOL_WINDOWS = (2, 4, 8, 16)
POOL_HALO = 16
MAIN_W = 3584
FC_PAD = 128
C_QA, C_FA, C_IA, C_GA, C_UB, C_GB, C_QC, C_KC, C_VC, C_GC = 0, 256, 512, 768, 1024, 1280, 1536, 2048, 2560, 3072
FOX_SCALE = HEAD ** -0.5

ADAM_LR, ADAM_B1, ADAM_B2, ADAM_EPS, ADAM_WD, ADAM_STEP = 0.001, 0.9, 0.999, 1e-08, 0.01, 10

VMEM_LIMIT = 56 * 1024 * 1024


def _pc(fn, name, **kw):
    return pl.pallas_call(fn, name=name, **kw)


def _params(**kw):
    return pltpu.CompilerParams(vmem_limit_bytes=VMEM_LIMIT, **kw)


class _Rider:
    def __init__(self, inputs, out_shapes, n_sems, n_local, plan):
        self.inputs, self.out_shapes, self.n_sems, self.n_local, self.plan = list(inputs), list(out_shapes), n_sems, n_local, plan

    def start(self, ins, outs, *sems):
        sends, _, locs = self.plan(ins, outs, *sems)
        for cp in locs + sends:
            cp.start()

    def wait(self, ins, outs, *sems):
        sends, recvs, locs = self.plan(ins, outs, *sems)
        for cp in recvs:
            cp.wait_recv()
        for cp in sends:
            cp.wait_send()
        for cp in locs:
            cp.wait()

    def sem_shapes(self):
        return [pltpu.SemaphoreType.DMA((self.n_sems,)), pltpu.SemaphoreType.DMA((self.n_sems,)),
                pltpu.SemaphoreType.DMA((self.n_local,))]


def _call(body, name, args, rider=None, *, grid, in_specs, out_specs, out_shape, scratch_shapes=(), **kw):
    if rider is None:
        res = _pc(body, name, grid=grid, in_specs=in_specs, out_specs=out_specs, out_shape=out_shape,
                  scratch_shapes=list(scratch_shapes), **kw)(*args)
        return res, None
    n_in, n_out, n_scr = len(in_specs), len(out_specs), len(scratch_shapes)
    n_rin, n_rout = len(rider.inputs), len(rider.out_shapes)

    def ridden(*refs):
        ins, refs = refs[:n_in], refs[n_in:]
        rins, refs = refs[:n_rin], refs[n_rin:]
        outs, refs = refs[:n_out], refs[n_out:]
        routs, refs = refs[:n_rout], refs[n_rout:]
        scr, sems = refs[:n_scr], refs[n_scr:]
        first = functools.reduce(jnp.logical_and, [pl.program_id(a) == 0 for a in range(len(grid))])
        last = functools.reduce(jnp.logical_and, [pl.program_id(a) == g - 1 for a, g in enumerate(grid)])

        @pl.when(first)
        def _():
            rider.start(rins, routs, *sems)

        body(*ins, *outs, *scr)

        @pl.when(last)
        def _():
            rider.wait(rins, routs, *sems)

    any_spec = pl.BlockSpec(memory_space=pl.ANY)
    res = _pc(ridden, name, grid=grid, in_specs=list(in_specs) + [any_spec] * n_rin,
              out_specs=list(out_specs) + [any_spec] * n_rout, out_shape=list(out_shape) + rider.out_shapes,
              scratch_shapes=list(scratch_shapes) + rider.sem_shapes(), **kw)(*args, *rider.inputs)
    return res[:n_out], res[n_out:]


def _run_rider(rider, name):
    n_rin = len(rider.inputs)

    def body(*refs):
        ins, outs, sems = refs[:n_rin], refs[n_rin:n_rin + len(rider.out_shapes)], refs[n_rin + len(rider.out_shapes):]
        rider.start(ins, outs, *sems)
        rider.wait(ins, outs, *sems)

    any_spec = pl.BlockSpec(memory_space=pl.ANY)
    return _pc(body, name, in_specs=[any_spec] * n_rin, out_specs=[any_spec] * len(rider.out_shapes),
               out_shape=rider.out_shapes, scratch_shapes=rider.sem_shapes())(*rider.inputs)


def _dot(a, b):
    return jnp.dot(a, b, preferred_element_type=F32)


def _dot_nt(a, b):
    return lax.dot_general(a, b, (((1,), (1,)), ((), ())), preferred_element_type=F32)


def _dot_tn(a, b):
    return lax.dot_general(a, b, (((0,), (0,)), ((), ())), preferred_element_type=F32)


def _sel_dot_exact(sel, x):
    hi = x.astype(BF16)
    rest = x - hi.astype(F32)
    mid = rest.astype(BF16)
    lo = (rest - mid.astype(F32)).astype(BF16)
    sb = sel.astype(BF16)
    return _dot(sb, hi) + _dot(sb, mid) + _dot(sb, lo)


def _split2(x):
    hi = x.astype(BF16)
    return hi, (x - hi.astype(F32)).astype(BF16)


def _sel_dot(sel, x):
    hi, lo = _split2(x)
    sb = sel.astype(BF16)
    return _dot(sb, hi) + _dot(sb, lo)


def _dot_sel(x, sel):
    hi, lo = _split2(x)
    sb = sel.astype(BF16)
    return _dot(hi, sb) + _dot(lo, sb)


def _sigmoid(x):
    return 1.0 / (1.0 + jnp.exp(-x))


def _block_ones(n, dtype):
    r = lax.broadcasted_iota(jnp.int32, (n, n), 0) // HEAD
    c = lax.broadcasted_iota(jnp.int32, (n, n), 1) // HEAD
    return (r == c).astype(dtype)


def _sds(shape, dtype):
    return jax.ShapeDtypeStruct(shape, dtype)


def _in_proj_fwd(x, g_pre, w_main, w_fc, name):
    n, d = x.shape
    tm = min(512, n)

    def body(x_ref, g_ref, w_ref, wf_ref, proj_ref, fc_ref, ht_ref, qt_ref, kt_ref):
        xv = x_ref[...]
        r = lax.rsqrt(jnp.mean(xv * xv, axis=-1, keepdims=True) + NORM_EPS)
        hf = xv * r * g_ref[...]
        hb = hf.astype(BF16)
        ht_ref[...] = hf.T.astype(BF16)
        for j in range(0, MAIN_W, FOX_W):
            res = _dot(hb, w_ref[:, j:j + FOX_W])
            proj_ref[:, j:j + FOX_W] = res
            if j == C_QC:
                qt_ref[...] = (res * FOX_SCALE).T.astype(BF16)
            if j == C_KC:
                kt_ref[...] = res.T.astype(BF16)
        fc_ref[...] = _dot(hb, wf_ref[...])

    def cols(rows):
        return pl.BlockSpec((rows, tm), lambda i: (0, i))

    return _pc(
        body, name, grid=(n // tm,),
        in_specs=[pl.BlockSpec((tm, d), lambda i: (i, 0)), pl.BlockSpec((1, d), lambda i: (0, 0)),
                  pl.BlockSpec((d, MAIN_W), lambda i: (0, 0)), pl.BlockSpec((d, FC_PAD), lambda i: (0, 0))],
        out_specs=[pl.BlockSpec((tm, MAIN_W), lambda i: (i, 0)), pl.BlockSpec((tm, FC_PAD), lambda i: (i, 0)),
                   cols(d), cols(FOX_W), cols(FOX_W)],
        out_shape=[_sds((n, MAIN_W), F32), _sds((n, FC_PAD), F32), _sds((d, n), BF16), _sds((FOX_W, n), BF16),
                   _sds((FOX_W, n), BF16)],
        compiler_params=_params(),
    )(x, g_pre, w_main, w_fc)


def _fox_decay_fwd(fc, bias, nb, name):
    n = fc.shape[0]
    t = n // nb
    tt = min(256, t)
    nt = t // tt

    def body(fc_ref, b_ref, c_ref, cr_ref):
        r = lax.broadcasted_iota(jnp.int32, (tt, tt), 0)
        cc = lax.broadcasted_iota(jnp.int32, (tt, tt), 1)
        carry = jnp.zeros((1, FC_PAD), F32)
        for i in range(nt):
            rows = slice(i * tt, (i + 1) * tt)
            xv = fc_ref[rows, :] + b_ref[...]
            lf = jnp.minimum(xv, 0.0) - jnp.log(1.0 + jnp.exp(-jnp.abs(xv)))
            cs = _sel_dot_exact(r >= cc, lf) + carry
            c_ref[rows, :] = cs
            cr_ref[:, rows] = cs.T[:FOX_HEADS, :]
            carry = cs[tt - 1:tt, :]

    return _pc(
        body, name, grid=(nb,),
        in_specs=[pl.BlockSpec((t, FC_PAD), lambda b: (b, 0)), pl.BlockSpec((1, FC_PAD), lambda b: (0, 0))],
        out_specs=[pl.BlockSpec((t, FC_PAD), lambda b: (b, 0)), pl.BlockSpec((None, FOX_HEADS, t), lambda b: (b, 0, 0))],
        out_shape=[_sds((n, FC_PAD), F32), _sds((nb, FOX_HEADS, t), F32)],
        compiler_params=_params(),
    )(fc, bias)


def _hgrn_gates(q, z, lb):
    sig = _sigmoid(z)
    sn = _sigmoid(-z)
    f = lb + (1.0 - lb) * sig
    g = jnp.log(jnp.maximum(f, TINY))
    k = (1.0 - lb) * sn
    sq = _sigmoid(q)
    return sig, sn, f, g, k, sq


def _sub_tri(n, lower):
    r = lax.broadcasted_iota(jnp.int32, (n, n), 0)
    c = lax.broadcasted_iota(jnp.int32, (n, n), 1)
    tri = (r >= c) if lower else (r <= c)
    return jnp.logical_and(r // SUB == c // SUB, tri).astype(F32)


def _live_rows(t):
    return 8 * (t // 8 + 1)


def _pad_rows(x):
    return x if x.shape[0] == SUB else jnp.concatenate([x, jnp.zeros((SUB - x.shape[0], x.shape[1]), x.dtype)], axis=0)


def _hgrn_decays(qs, k, b):
    srow = lax.broadcasted_iota(jnp.int32, (SUB, HGRN_W), 0)
    es, ws = [], []
    for t in range(SUB):
        r = _live_rows(t)
        e = jnp.where(srow[:r] <= t, jnp.exp(b[t:t + 1, :] - b[:r]), 0.0)
        es.append(e)
        ws.append(_pad_rows(e * (qs[t:t + 1, :] * k[:r])))
    return srow, es, ws


def _hgrn_state_step(st, k, v, b, bmask):
    bl = b[SUB - 1:SUB, :]
    ktil = k * jnp.exp(bl - b)
    return st * jnp.exp(bl) + _dot_tn(v.astype(BF16), ktil.astype(BF16)) * bmask


def _hgrn_tile(t):
    return min(256, t)


def _hgrn_fwd(proj, lb, ones_b, nb, name, rider=None):
    n = proj.shape[0]
    t = n // nb
    tt = _hgrn_tile(t)
    nt = t // tt
    ncs = tt // CHUNK
    w = HGRN_W

    def body(q_ref, z_ref, v_ref, lb_ref, ones_ref, o_ref, s0_ref, st_s, b_s, qs_s, k_s):
        @pl.when(pl.program_id(0) == 0)
        def _():
            st_s[...] = jnp.zeros_like(st_s)

        for e in range(nb):
            q = q_ref[e]
            _, _, _, g, k, sq = _hgrn_gates(q, z_ref[e], lb_ref[...])
            b_s[e] = _sel_dot(_sub_tri(tt, True), g)
            qs_s[e] = q * sq
            k_s[e] = k
        bmask = _block_ones(w, F32)
        ones_b = ones_ref[...]
        nsub = CHUNK // SUB

        def chunk(c, carry):
            sts = [st_s[e] for e in range(nb)]
            for e in range(nb):
                s0_ref[e, c] = sts[e]
            base = pl.multiple_of(c * CHUNK, CHUNK)
            tiles = [[(qs_s[e, pl.ds(base + u * SUB, SUB), :], k_s[e, pl.ds(base + u * SUB, SUB), :],
                       v_ref[e, pl.ds(base + u * SUB, SUB), :], b_s[e, pl.ds(base + u * SUB, SUB), :])
                      for u in range(nsub)] for e in range(nb)]
            aexps = [[] for _ in range(nb)]
            for e in range(nb):
                for qs, k, v, b in tiles[e]:
                    _, _, ws = _hgrn_decays(qs, k, b)
                    aexps[e].append(_dot(jnp.concatenate(ws, axis=0).astype(BF16), ones_b))
            inters = [[] for _ in range(nb)]
            for u in range(nsub):
                for e in range(nb):
                    qs, k, v, b = tiles[e][u]
                    inters[e].append(_dot_nt((qs * jnp.exp(b)).astype(BF16), sts[e].astype(BF16)))
                    sts[e] = _hgrn_state_step(sts[e], k, v, b, bmask)
            for e in range(nb):
                st_s[e] = sts[e]
            for e in range(nb):
                for u, ((qs, k, v, b), aexp, o) in enumerate(zip(tiles[e], aexps[e], inters[e])):
                    for t in range(SUB):
                        r = _live_rows(t)
                        row = o[t:t + 1, :] + jnp.sum(aexp[t * SUB:t * SUB + r, :] * v[:r], axis=0, keepdims=True)
                        o_ref[e, pl.ds(base + u * SUB + t, 1), :] = row
            return carry

        lax.fori_loop(0, ncs, chunk, 0)

    def col(j):
        return pl.BlockSpec((nb, tt, w), lambda i: (0, i, j))

    proj3 = proj.reshape(nb, t, proj.shape[1])
    (o, s0), rode = _call(
        body, name, (proj3, proj3, proj3, lb, ones_b), rider, grid=(nt,),
        in_specs=[col(C_QA // w), col(C_FA // w), col(C_IA // w), pl.BlockSpec((1, w), lambda i: (0, 0)),
                  pl.BlockSpec((w, w), lambda i: (0, 0))],
        out_specs=[pl.BlockSpec((nb, tt, w), lambda i: (0, i, 0)),
                   pl.BlockSpec((nb, ncs, w, w), lambda i: (0, i, 0, 0))],
        out_shape=[_sds((nb, t, w), F32), _sds((nb, t // CHUNK, w, w), F32)],
        scratch_shapes=[pltpu.VMEM((nb, w, w), F32)] + [pltpu.VMEM((nb, tt, w), F32)] * 3,
        compiler_params=_params(),
    )
    return (o.reshape(n, w), s0.reshape(n // CHUNK, w, w)), rode


def _pool_lane_windows():
    lane = lax.broadcasted_iota(jnp.int32, (1, POOL_W), 1) // HEAD
    wl = jnp.zeros((1, POOL_W), F32)
    for gi, win in enumerate(POOL_WINDOWS):
        wl = jnp.where(lane == gi, float(win), wl)
    return lane, wl


def _pool_select(lane, parts):
    out = parts[-1]
    for gi in range(len(parts) - 2, -1, -1):
        out = jnp.where(lane == gi, parts[gi], out)
    return out


def _pool_mix(u, halo, t0, tt):
    lane, wl = _pool_lane_windows()
    ext = jnp.concatenate([halo, u], axis=0)
    sums, cur, shift = [], ext, 1
    for _ in POOL_WINDOWS:
        cur = cur + pltpu.roll(cur, shift, axis=0)
        sums.append(cur[POOL_HALO:, :])
        shift *= 2
    tpos = (t0 + lax.broadcasted_iota(jnp.int32, (tt, POOL_W), 0)).astype(F32)
    cnt = jnp.minimum(tpos + 1.0, wl)
    return _pool_select(lane, sums) / cnt - u, cnt


def _rows_reduce(x, op, final):
    while x.shape[0] > 8 and x.shape[0] % 16 == 0:
        half = x.shape[0] // 2
        x = op(x[:half], x[half:])
    return final(x, axis=0, keepdims=True)


def _tri_pair(step, n, group=1):
    counts = [a // group + 1 for a in range(n)]
    firsts = [sum(counts[:a]) for a in range(1, n)]
    a = sum([(step >= f).astype(jnp.int32) for f in firsts], jnp.int32(0))
    first = sum([jnp.where(step >= f, c, 0) for f, c in zip(firsts, counts)], jnp.int32(0))
    return a, step - first


def _tri_steps(n, group=1):
    return sum(a // group + 1 for a in range(n))


def _lane_lo():
    return lax.broadcasted_iota(jnp.int32, (1, 2 * HEAD), 1) < HEAD


def _put_col(tile, hh, colv):
    lane = lax.broadcasted_iota(jnp.int32, tile.shape, 1)
    return jnp.where(lane == hh, colv, tile)


def _fox_fwd(proj, kt, c_col, c_row, nb, name, rider=None):
    n = proj.shape[0]
    t = n // nb
    tb = min(256, t)
    nq = t // tb
    pw = 2 * HEAD
    kw = 2 if nq % 2 == 0 else 1
    nk = nq // kw

    def body(q_ref, kt_ref, v_ref, cc_ref, cr_ref, o_ref, lse_ref, m_s, acc_s, cq_s):
        qi, kk = _tri_pair(pl.program_id(1), nq, kw)

        @pl.when(kk == 0)
        def _():
            m_s[...] = jnp.full_like(m_s, -jnp.inf)
            acc_s[...] = jnp.zeros_like(acc_s)
            for hh in range(FOX_HEADS):
                cq_s[hh] = jnp.broadcast_to(cc_ref[:, hh:hh + 1], (tb, pw))

        def block(masked, sub):
            lo = _lane_lo()
            keys = slice(sub * tb, (sub + 1) * tb)
            if masked:
                causal = lax.broadcasted_iota(jnp.int32, (tb, tb), 0) >= lax.broadcasted_iota(jnp.int32, (tb, tb), 1)
            def lanes(hh):
                return lo if hh % 2 == 0 else jnp.logical_not(lo)

            def scores(hh):
                sl = slice((hh // 2) * pw, (hh // 2 + 1) * pw)
                return _dot(jnp.where(lanes(hh), q_ref[:, sl] * FOX_SCALE, 0.0).astype(BF16), kt_ref[sl, keys])

            ahead = scores(0)
            for hh in range(FOX_HEADS):
                s = ahead
                if hh + 1 < FOX_HEADS:
                    ahead = scores(hh + 1)
                s = s + (jnp.tile(cq_s[hh], (1, tb // pw)) - cr_ref[hh:hh + 1, keys])
                if masked:
                    s = jnp.where(causal, s, MASK_VALUE)
                m_prev = m_s[hh]
                m_new = jnp.maximum(m_prev, jnp.max(s, axis=1, keepdims=True))
                alpha = jnp.exp(m_prev - m_new)
                pe = jnp.exp(s - jnp.tile(m_new, (1, tb // pw)))
                m_s[hh] = m_new
                vf = v_ref[keys, (hh // 2) * pw:(hh // 2 + 1) * pw]
                acc_s[hh] = alpha * acc_s[hh] + _dot(pe.astype(BF16), jnp.where(lanes(hh), vf, 1.0).astype(BF16))

        def finish():
            lo = _lane_lo()
            m_all, l_all = jnp.zeros((tb, FC_PAD), F32), jnp.ones((tb, FC_PAD), F32)
            for p in range(FOX_HEADS // 2):
                a0, a1 = acc_s[2 * p], acc_s[2 * p + 1]
                both = pltpu.roll(jnp.where(lo, a1, a0), HEAD, axis=1)
                o_ref[:, p * pw:(p + 1) * pw] = jnp.where(lo, a0, a1) / both
                m_all = _put_col(_put_col(m_all, 2 * p, m_s[2 * p]), 2 * p + 1, m_s[2 * p + 1])
                l_all = _put_col(_put_col(l_all, 2 * p, both), 2 * p + 1, a1)
            lse_ref[...] = (m_all + jnp.log(l_all)).T[:FOX_HEADS, :]

        for sub in range(kw):
            @pl.when(kk * kw + sub < qi)
            def _(sub=sub):
                block(False, sub)

            @pl.when(kk * kw + sub == qi)
            def _(sub=sub):
                block(True, sub)
                finish()

    def qspec(wd, j):
        return pl.BlockSpec((tb, wd), lambda b, st: (b * nq + _tri_pair(st, nq, kw)[0], j))

    def kspec(j):
        return pl.BlockSpec((kw * tb, FOX_W), lambda b, st: (b * nk + _tri_pair(st, nq, kw)[1], j))

    return _call(
        body, name, (proj, kt, proj, c_col, c_row), rider, grid=(nb, _tri_steps(nq, kw)),
        in_specs=[qspec(FOX_W, C_QC // FOX_W),
                  pl.BlockSpec((FOX_W, kw * tb), lambda b, st: (0, b * nk + _tri_pair(st, nq, kw)[1])),
                  kspec(C_VC // FOX_W), qspec(FC_PAD, 0),
                  pl.BlockSpec((None, FOX_HEADS, kw * tb), lambda b, st: (b, 0, _tri_pair(st, nq, kw)[1]))],
        out_specs=[qspec(FOX_W, 0),
                   pl.BlockSpec((None, FOX_HEADS, tb), lambda b, st: (b, 0, _tri_pair(st, nq, kw)[0]))],
        out_shape=[_sds((n, FOX_W), F32), _sds((nb, FOX_HEADS, t), F32)],
        scratch_shapes=[pltpu.VMEM((FOX_HEADS, tb, pw), F32), pltpu.VMEM((FOX_HEADS, tb, pw), F32),
                        pltpu.VMEM((FOX_HEADS, tb, pw), F32)],
        compiler_params=_params(),
    )


def _head_mean(x, ones_f):
    return _dot_sel(x, ones_f) * (1.0 / HEAD)


def _merge_fwd(x, proj, o_h, o_c, gh, wbd, scale, w_out, g_post, nb, name, target=None):
    n, d = x.shape
    t = n // nb
    tm = min(512, t)
    nt = t // tm
    nhb = tm // POOL_HALO

    def body(*refs):
        x_ref, ga_ref, gc_ref, oh_ref, u_ref, gb_ref, h_ref, oc_ref, gh_ref, wb_ref, s_ref, w_ref, gp_ref = refs[:13]
        if target is None:
            xo_ref, mixt_ref, y_ref = refs[13:]
        else:
            t_ref, dx_ref, sq_ref, mixt_ref, y_ref = refs[13:]
        oh = oh_ref[...]
        ones_f = _block_ones(HGRN_W, F32)
        na = oh * lax.rsqrt(_head_mean(oh * oh, ones_f) + NORM_EPS) * gh_ref[...]
        ga, gb, gc = ga_ref[...], gb_ref[...], gc_ref[...]
        ti = pl.program_id(0) % nt
        pooled, _ = _pool_mix(u_ref[...], jnp.where(ti == 0, 0.0, h_ref[...]), ti * tm, tm)
        ob = _dot(pooled.astype(BF16), wb_ref[...]) * s_ref[...] * (gb * _sigmoid(gb))
        mixed = jnp.concatenate([na * (ga * _sigmoid(ga)), ob, oc_ref[...] * (gc * _sigmoid(gc))], axis=1)
        mixt_ref[...] = mixed.T.astype(BF16)
        y = _dot(mixed.astype(BF16), w_ref[...])
        y_ref[...] = y
        xn = x_ref[...] + y * lax.rsqrt(jnp.mean(y * y, axis=-1, keepdims=True) + NORM_EPS) * gp_ref[...]
        if target is None:
            xo_ref[...] = xn
        else:
            @pl.when(pl.program_id(0) == 0)
            def _():
                sq_ref[...] = jnp.zeros_like(sq_ref)

            e = xn - t_ref[...]
            dx_ref[...] = e * (1.0 / d)
            sq_ref[...] += jnp.sum(e * e, axis=0, keepdims=True)

    def row(wd, j=0):
        return pl.BlockSpec((tm, wd), lambda i: (i, j))

    def full(a, b):
        return pl.BlockSpec((a, b), lambda i: (0, 0))

    head = [] if target is None else [target]
    res = _pc(
        body, name, grid=(n // tm,),
        in_specs=[row(d), row(HGRN_W, C_GA // HGRN_W), row(FOX_W, C_GC // FOX_W), row(HGRN_W),
                  row(POOL_W, C_UB // POOL_W), row(POOL_W, C_GB // POOL_W),
                  pl.BlockSpec((POOL_HALO, POOL_W), lambda i: (jnp.maximum(i * nhb - 1, 0), C_UB // POOL_W)), row(FOX_W),
                  full(1, HGRN_W), full(POOL_W, POOL_W), full(1, POOL_W), full(d, d), full(1, d)] + [row(d)] * len(head),
        out_specs=[row(d)] + [full(1, d)] * len(head) + [pl.BlockSpec((d, tm), lambda i: (0, i)), row(d)],
        out_shape=[_sds((n, d), F32)] + [_sds((1, d), F32)] * len(head) + [_sds((d, n), BF16), _sds((n, d), F32)],
        compiler_params=_params(),
    )(x, proj, proj, o_h, proj, proj, proj, o_c, gh, wbd, scale, w_out, g_post, *head)
    return (res[0], res[1], res[2]) if target is None else ((res[0], res[1]), res[2], res[3])


def _rms_bwd(dy_scaled, xhat, r):
    return r * (dy_scaled - xhat * jnp.mean(dy_scaled * xhat, axis=-1, keepdims=True))


def _merge_bwd(dxo, y, g_post, w_out_t, proj, o_c, mixt, nb, name):
    n, d = y.shape
    t = n // nb
    tm = min(512, t)
    nt = t // tm
    wab = HGRN_W + POOL_W
    rows = d // N_DEV

    def body(dx_ref, y_ref, gp_ref, wt_ref, gc_ref, oc_ref, mt_ref,
             gw_ref, dm_ref, dgp_ref, da_ref, dat_ref, dg_ref, dl_ref, gw_s):
        @pl.when(pl.program_id(0) == 0)
        def _():
            dgp_ref[...] = jnp.zeros_like(dgp_ref)
            gw_s[...] = jnp.zeros_like(gw_s)

        yv, dxv = y_ref[...], dx_ref[...]
        r = lax.rsqrt(jnp.mean(yv * yv, axis=-1, keepdims=True) + NORM_EPS)
        yh = yv * r
        dgp_ref[...] += jnp.sum(dxv * yh, axis=0, keepdims=True)
        dyb = _rms_bwd(dxv * gp_ref[...], yh, r).astype(BF16)
        gw_s[...] += _dot(mt_ref[...], dyb)

        @pl.when(pl.program_id(0) == n // tm - 1)
        def _():
            for j in range(N_DEV):
                gw_ref[j % 2, j // 2] = gw_s[j * rows:(j + 1) * rows, :].astype(BF16)

        dm_ref[...] = _dot(dyb, wt_ref[:, :wab])
        dmc = _dot(dyb, wt_ref[:, wab:])
        gc, oc = gc_ref[...], oc_ref[...]
        sg = _sigmoid(gc)
        da = dmc * (gc * sg)
        da_ref[...] = da.astype(BF16)
        dat_ref[...] = da.T.astype(BF16)
        dg_ref[...] = (dmc * oc * (sg * (1.0 + gc * (1.0 - sg)))).astype(BF16)
        rr = lax.broadcasted_iota(jnp.int32, (FOX_W, FC_PAD), 0) // HEAD
        cc = lax.broadcasted_iota(jnp.int32, (FOX_W, FC_PAD), 1)
        dl_ref[...] = _dot_sel(da * oc, (rr == cc).astype(F32)).T[:FOX_HEADS, :]

    def row(wd, j=0):
        return pl.BlockSpec((tm, wd), lambda i: (i, j))

    def full(a, b):
        return pl.BlockSpec((a, b), lambda i: (0, 0))

    return _pc(
        body, name, grid=(n // tm,),
        in_specs=[row(d), row(d), full(1, d), full(d, d), row(FOX_W, C_GC // FOX_W), row(FOX_W),
                  pl.BlockSpec((d, tm), lambda i: (0, i))],
        out_specs=[pl.BlockSpec((2, N_DEV // 2, rows, d), lambda i: (0, 0, 0, 0)), row(wab), full(1, d), row(FOX_W),
                   pl.BlockSpec((FOX_W, tm), lambda i: (0, i)), row(FOX_W),
                   pl.BlockSpec((None, FOX_HEADS, tm), lambda i: (i // nt, 0, i % nt))],
        out_shape=[_sds((2, N_DEV // 2, rows, d), BF16), _sds((n, wab), F32), _sds((1, d), F32), _sds((n, FOX_W), BF16),
                   _sds((FOX_W, n), BF16), _sds((n, FOX_W), BF16), _sds((nb, FOX_HEADS, t), F32)],
        scratch_shapes=[pltpu.VMEM((d, d), F32)],
        compiler_params=_params(),
    )(dxo, y, g_post, w_out_t, proj, o_c, mixt)


def _w_in_grad(ht, pieces, dmix, proj, wbd, wbd_t, scale, nb, name):
    d, n = ht.shape
    t = n // nb
    ta, tk = d, min(512, t)
    nk, nt, nhb = n // tk, t // tk, tk // POOL_HALO
    given = [p for p, _ in pieces if p is not None]
    widths = [2 * POOL_W if p is None else p.shape[1] for p, _ in pieces]
    offs = [sum(widths[:i]) for i in range(len(widths))]
    in_w = MAIN_W + FOX_HEADS
    shard = in_w // N_DEV
    cu, cg, cm = C_UB // POOL_W, C_GB // POOL_W, HGRN_W // POOL_W

    def body(*refs):
        a_ref, p_refs = refs[0], list(refs[1:1 + len(given)])
        pool_refs = refs[1 + len(given):10 + len(given)]
        o_ref, db_ref, dw_ref, ds_ref, acc = refs[10 + len(given):]
        k = pl.program_id(1)

        @pl.when(k == 0)
        def _():
            dw_ref[...] = jnp.zeros_like(dw_ref)
            ds_ref[...] = jnp.zeros_like(ds_ref)
            acc[...] = jnp.zeros_like(acc)

        db, dw, dscale = _pool_bwd_tile(*pool_refs, k % nt, nt, tk)
        dw_ref[...] += dw
        ds_ref[...] += dscale
        db_ref[...] = db
        a = a_ref[...]
        for (p, _), off, wd in zip(pieces, offs, widths):
            pr = db if p is None else p_refs.pop(0)
            for j in range(0, wd, 512):
                jw = min(512, wd - j)
                acc[:, off + j:off + j + jw] += _dot(a, pr[:, j:j + jw])

        @pl.when(k == nk - 1)
        def _():
            for j in range(N_DEV):
                o_ref[j % 2, j // 2] = acc[:, j * shard:(j + 1) * shard].astype(BF16)

    def after(k):
        return jnp.minimum((k + 1) * nhb, n // POOL_HALO - 1)

    def tile(j):
        return pl.BlockSpec((tk, POOL_W), lambda i, k: (k, j))

    def full(r, c):
        return pl.BlockSpec((r, c), lambda i, k: (0, 0))

    pool_specs = [tile(cu), tile(cg),
                  pl.BlockSpec((POOL_HALO, POOL_W), lambda i, k: (jnp.maximum(k * nhb - 1, 0), cu)), tile(cm),
                  pl.BlockSpec((POOL_HALO, POOL_W), lambda i, k: (after(k), cg)),
                  pl.BlockSpec((POOL_HALO, POOL_W), lambda i, k: (after(k), cm)),
                  full(POOL_W, POOL_W), full(POOL_W, POOL_W), full(1, POOL_W)]
    return _pc(
        body, name, grid=(d // ta, nk),
        in_specs=[pl.BlockSpec((ta, tk), lambda i, k: (i, k))]
        + [pl.BlockSpec((tk, p.shape[1]), lambda i, k: (k, 0)) for p in given] + pool_specs,
        out_specs=[pl.BlockSpec((2, N_DEV // 2, ta, shard), lambda i, k: (0, 0, i, 0)),
                   pl.BlockSpec((tk, 2 * POOL_W), lambda i, k: (k, 0)), full(POOL_W, POOL_W), full(1, POOL_W)],
        out_shape=[_sds((2, N_DEV // 2, d, shard), BF16), _sds((n, 2 * POOL_W), BF16), _sds((POOL_W, POOL_W), F32),
                   _sds((1, POOL_W), F32)],
        scratch_shapes=[pltpu.VMEM((ta, sum(widths)), F32)],
        compiler_params=_params(),
    )(ht, *given, proj, proj, proj, dmix, proj, dmix, wbd, wbd_t, scale)


def _hgrn_state_bwd(qs, k, v, b, do, s0, ds1, bmask):
    bl = b[SUB - 1:SUB, :]
    eb, ebl, ekt = jnp.exp(b), jnp.exp(bl), jnp.exp(bl - b)
    qe, ktil = qs * eb, k * ekt
    ds1b, dob = ds1.astype(BF16), do.astype(BF16)
    dv = _dot_nt(ktil.astype(BF16), ds1b)
    dqe = _dot(dob, s0.astype(BF16))
    dktil = _dot(v.astype(BF16), ds1b)
    dbl = jnp.sum(dktil * ktil, axis=0, keepdims=True) + ebl * jnp.sum(s0 * ds1, axis=0, keepdims=True)
    ds0 = ds1 * ebl + _dot_tn(dob, qe.astype(BF16)) * bmask
    return dqe * eb, dktil * ekt, dv, dbl, ds0


def _hgrn_intra_bwd(qs, k, v, do, es, aexp, gexp, dq, dk, dv, put_dq_row):
    dks = [dk[j:j + 8] for j in range(0, SUB, 8)]
    dvs = [dv[j:j + 8] for j in range(0, SUB, 8)]
    for t in range(SUB):
        r = _live_rows(t)
        ge = gexp[t * SUB:t * SUB + r, :] * es[t]
        put_dq_row(t, dq[t:t + 1, :] + jnp.sum(ge * k[:r], axis=0, keepdims=True))
        for j in range(r // 8):
            dks[j] = dks[j] + ge[8 * j:8 * j + 8] * qs[t:t + 1, :]
            dvs[j] = dvs[j] + aexp[t * SUB + 8 * j:t * SUB + 8 * j + 8, :] * do[t:t + 1, :]
    return jnp.concatenate(dks, axis=0), jnp.concatenate(dvs, axis=0)


def _hgrn_bwd(dmix, proj, o_h, s0, gh, lb, ones_b, nb, name, rider=None):
    n = proj.shape[0]
    t = n // nb
    tt = _hgrn_tile(t)
    nt = t // tt
    ncs = tt // CHUNK
    nsub = CHUNK // SUB
    w = HGRN_W

    def body(dm_ref, q_ref, z_ref, v_ref, ga_ref, oh_ref, s0_ref, gh_ref, lb_ref, ones_ref,
             dp_ref, dgh_ref, dlb_ref, ds_s, ss_s, b_s, qs_s, k_s, do_s, dq_s, dk_s, dv_s, dbl_s):
        @pl.when(pl.program_id(0) == 0)
        def _():
            dgh_ref[...] = jnp.zeros_like(dgh_ref)
            dlb_ref[...] = jnp.zeros_like(dlb_ref)
            ds_s[...] = jnp.zeros_like(ds_s)

        ones_b = ones_ref[...]
        ones_f = ones_b.astype(F32)
        bmask = _block_ones(w, F32)
        lbv, ghv = lb_ref[...], gh_ref[...]
        kept = []
        for e in range(nb):
            oh, ga, dm = oh_ref[e], ga_ref[e], dm_ref[e]
            rn = lax.rsqrt(_head_mean(oh * oh, ones_f) + NORM_EPS)
            nh = oh * rn
            sga = _sigmoid(ga)
            dp_ref[e, :, 3 * w:4 * w] = (dm * nh * ghv * (sga * (1.0 + ga * (1.0 - sga)))).astype(BF16)
            dn = dm * (ga * sga)
            dgh_ref[...] += jnp.sum(dn * nh, axis=0, keepdims=True)
            dn = dn * ghv
            do_s[e] = rn * (dn - nh * _head_mean(dn * nh, ones_f))
            q = q_ref[e]
            sig, sn, f, g, k, sq = _hgrn_gates(q, z_ref[e], lbv)
            qs = q * sq
            b_s[e] = _sel_dot(_sub_tri(tt, True), g)
            qs_s[e] = qs
            k_s[e] = k
            kept.append((q, sig, sn, f, k, sq, qs))

        def chunk(cc, carry):
            c = ncs - 1 - cc
            base = pl.multiple_of(c * CHUNK, CHUNK)
            tiles = [[(qs_s[e, pl.ds(base + u * SUB, SUB), :], k_s[e, pl.ds(base + u * SUB, SUB), :],
                       v_ref[e, pl.ds(base + u * SUB, SUB), :], b_s[e, pl.ds(base + u * SUB, SUB), :],
                       do_s[e, pl.ds(base + u * SUB, SUB), :]) for u in range(nsub)] for e in range(nb)]
            sts = [s0_ref[e, c] for e in range(nb)]
            for u in range(nsub):
                for e in range(nb):
                    qs, k, v, b, do = tiles[e][u]
                    ss_s[e, u] = sts[e]
                    if u < nsub - 1:
                        sts[e] = _hgrn_state_step(sts[e], k, v, b, bmask)
            dss = [ds_s[e] for e in range(nb)]
            for u in reversed(range(nsub)):
                for e in range(nb):
                    qs, k, v, b, do = tiles[e][u]
                    _, es, ws = _hgrn_decays(qs, k, b)
                    gs = [_pad_rows(do[t:t + 1, :] * v[:_live_rows(t)]) for t in range(SUB)]
                    aexp = _dot(jnp.concatenate(ws, axis=0).astype(BF16), ones_b)
                    gexp = _dot(jnp.concatenate(gs, axis=0).astype(BF16), ones_b)
                    dq, dk, dv, dbl, dss[e] = _hgrn_state_bwd(qs, k, v, b, do, ss_s[e, u], dss[e], bmask)

                    def put_dq_row(i, row, e=e, r0=base + u * SUB):
                        dq_s[e, pl.ds(r0 + i, 1), :] = row

                    dk, dv = _hgrn_intra_bwd(qs, k, v, do, es, aexp, gexp, dq, dk, dv, put_dq_row)
                    dk_s[e, pl.ds(base + u * SUB, SUB), :] = dk
                    dv_s[e, pl.ds(base + u * SUB, SUB), :] = dv
                    dbl_s[e, pl.ds(base + u * SUB, SUB), :] = jnp.broadcast_to(dbl, (SUB, w))
            for e in range(nb):
                ds_s[e] = dss[e]
            return carry

        lax.fori_loop(0, ncs, chunk, 0)
        for e, (q, sig, sn, f, k, sq, qs) in enumerate(kept):
            dqs, dk = dq_s[e], dk_s[e]
            dg = _sel_dot(_sub_tri(tt, False), qs * dqs - k * dk) + dbl_s[e]
            dfz = jnp.where(f > TINY, dg / jnp.maximum(f, TINY), 0.0)
            dlb_ref[...] += jnp.sum(dfz * (1.0 - sig) - dk * sn, axis=0, keepdims=True)
            dp_ref[e, :, 0:w] = (dqs * (sq * (1.0 + q * (1.0 - sq)))).astype(BF16)
            dp_ref[e, :, w:2 * w] = ((dfz - dk) * (1.0 - lbv) * sig * sn).astype(BF16)
            dp_ref[e, :, 2 * w:3 * w] = dv_s[e].astype(BF16)

    def col(j):
        return pl.BlockSpec((nb, tt, w), lambda i: (0, nt - 1 - i, j))

    def full(a, bb):
        return pl.BlockSpec((a, bb), lambda i: (0, 0))

    proj3 = proj.reshape(nb, t, proj.shape[1])
    (dp, dgh, dlb), rode = _call(
        body, name, (dmix.reshape(nb, t, dmix.shape[1]), proj3, proj3, proj3, proj3, o_h.reshape(nb, t, w),
                     s0.reshape(nb, t // CHUNK, w, w), gh, lb, ones_b), rider, grid=(nt,),
        in_specs=[col(0), col(C_QA // w), col(C_FA // w), col(C_IA // w), col(C_GA // w), col(0),
                  pl.BlockSpec((nb, ncs, w, w), lambda i: (0, nt - 1 - i, 0, 0)), full(1, w), full(1, w), full(w, w)],
        out_specs=[pl.BlockSpec((nb, tt, 4 * w), lambda i: (0, nt - 1 - i, 0)), full(1, w), full(1, w)],
        out_shape=[_sds((nb, t, 4 * w), BF16), _sds((1, w), F32), _sds((1, w), F32)],
        scratch_shapes=[pltpu.VMEM((nb, w, w), F32), pltpu.VMEM((nb, nsub, w, w), F32)]
        + [pltpu.VMEM((nb, tt, w), F32)] * 8,
        compiler_params=_params(),
    )
    return (dp.reshape(n, 4 * w), dgh, dlb), rode


def _pool_bwd_tile(u_ref, g_ref, h_ref, dm_ref, gn_ref, dmn_ref, w_ref, wt_ref, s_ref, i, nt, tt):
    sc = s_ref[...]
    halo = jnp.where(i == 0, 0.0, h_ref[...])
    pooled, cnt = _pool_mix(u_ref[...], halo, i * tt, tt)
    pb = pooled.astype(BF16)
    pre = _dot(pb, w_ref[...])
    gv, dm = g_ref[...], dm_ref[...]
    sg = _sigmoid(gv)
    silu = gv * sg
    dgb = dm * pre * sc * (sg * (1.0 + gv * (1.0 - sg)))
    dscale = jnp.sum(dm * pre * silu, axis=0, keepdims=True)
    dpre = (dm * sc * silu).astype(BF16)
    dw = _dot_tn(pb, dpre)
    dpool = _dot(dpre, wt_ref[...])
    gn = gn_ref[...]
    dpre_n = (dmn_ref[...] * sc * (gn * _sigmoid(gn))).astype(BF16)
    dpool_n = jnp.where(i == nt - 1, 0.0, _dot(dpre_n, wt_ref[...]))
    lane, wl = _pool_lane_windows()
    tpos_n = ((i + 1) * tt + lax.broadcasted_iota(jnp.int32, (POOL_HALO, POOL_W), 0)).astype(F32)
    ext = jnp.concatenate([dpool / cnt, dpool_n / jnp.minimum(tpos_n + 1.0, wl)], axis=0)
    rows = tt + POOL_HALO
    sums, cur, shift = [], ext, 1
    for _ in POOL_WINDOWS:
        cur = cur + pltpu.roll(cur, rows - shift, axis=0)
        sums.append(cur[:tt, :])
        shift *= 2
    du = _pool_select(lane, sums) - dpool
    return jnp.concatenate([du, dgb], axis=1).astype(BF16), dw, dscale


def _fox_bwd(proj, qt, kt, da, dat, c_col, c_row, lse_row, delta_row, nb, name, rider=None):
    n = proj.shape[0]
    t = n // nb
    tb = min(256, t)
    nq = t // tb
    pw = 2 * HEAD
    qw = 2 if nq % 2 == 0 else 1
    nqs = nq // qw

    def body(q_ref, k_ref, v_ref, da_ref, qt_ref, kt_ref, dat_ref, cc_ref, cr_ref, lse_ref, dl_ref,
             dq_ref, dk_ref, dv_ref, dck_ref, dcq_ref, dq_s, dk_s, dv_s, dck_s, dcq_s):
        step = pl.program_id(1)
        kj, qq = pairs(step)

        @pl.when(step == 0)
        def _():
            dq_s[...] = jnp.zeros_like(dq_s)
            dcq_s[...] = jnp.zeros_like(dcq_s)

        @pl.when(qq == nqs - 1)
        def _():
            dk_s[...] = jnp.zeros_like(dk_s)
            dv_s[...] = jnp.zeros_like(dv_s)
            dck_s[...] = jnp.zeros_like(dck_s)

        def block(masked, sub):
            lo = _lane_lo()
            qi = qq * qw + sub
            qs = slice(sub * tb, (sub + 1) * tb)
            if masked:
                causal = lax.broadcasted_iota(jnp.int32, (tb, tb), 1) >= lax.broadcasted_iota(jnp.int32, (tb, tb), 0)
            dck = dck_s[...]
            for p in range(FOX_HEADS // 2):
                sl = slice(p * pw, (p + 1) * pw)
                qp = q_ref[qs, sl] * FOX_SCALE
                kp = k_ref[:, sl].astype(BF16)
                vp = v_ref[:, sl].astype(BF16)
                dap = da_ref[qs, sl]
                dk, dv = dk_s[:, sl], dv_s[:, sl]
                for h in range(2):
                    hh = 2 * p + h
                    lm = lo if h == 0 else jnp.logical_not(lo)
                    rows = slice(hh * HEAD, (hh + 1) * HEAD)
                    none = jnp.zeros((HEAD, tb), BF16)
                    qm = jnp.where(lm, qp, 0.0).astype(BF16)
                    dam = jnp.where(lm, dap, jnp.zeros_like(dap))
                    qtm = jnp.concatenate([qt_ref[rows, qs], none] if h == 0 else [none, qt_ref[rows, qs]], axis=0)
                    datm = jnp.concatenate([dat_ref[rows, qs], none] if h == 0 else [none, dat_ref[rows, qs]], axis=0)
                    s = _dot(kp, qtm) + (cr_ref[hh:hh + 1, qs] - cc_ref[:, hh:hh + 1])
                    pe = jnp.exp(s - lse_ref[hh:hh + 1, qs])
                    if masked:
                        pe = jnp.where(causal, pe, 0.0)
                    dp = _dot(vp, datm)
                    ds = pe * (dp - dl_ref[hh:hh + 1, qs])
                    dsb = ds.astype(BF16)
                    dv = dv + _dot(pe.astype(BF16), dam)
                    dk = dk + _dot(dsb, qm)
                    dq_s[qi, rows, :] += _dot(kt_ref[rows, :], dsb)
                    dck = dck - _put_col(jnp.zeros_like(dck), hh, jnp.sum(ds, axis=1, keepdims=True))
                    dcq_s[qi, hh:hh + 1, :] += _rows_reduce(ds, jnp.add, jnp.sum)
                dk_s[:, sl] = dk
                dv_s[:, sl] = dv
            dck_s[...] = dck

        for sub in reversed(range(qw)):
            @pl.when(qq * qw + sub > kj)
            def _(sub=sub):
                block(False, sub)

            @pl.when(qq * qw + sub == kj)
            def _(sub=sub):
                block(True, sub)

        @pl.when(qq == kj // qw)
        def _():
            dk_ref[...] = dk_s[...].astype(BF16)
            dv_ref[...] = dv_s[...].astype(BF16)
            dck_ref[...] = dck_s[...]

        @pl.when(step == _tri_steps(nq, qw) - 1)
        def _():
            for j in range(nq):
                dq_ref[j * tb:(j + 1) * tb, :] = (dq_s[j].T * FOX_SCALE).astype(BF16)
                dcq_ref[:, j * tb:(j + 1) * tb] = dcq_s[j]

    def pairs(step):
        a, b = _tri_pair(step, nq, qw)
        return nq - 1 - a, nqs - 1 - b

    def kspec(wd, j=0):
        return pl.BlockSpec((tb, wd), lambda b, st: (b * nq + pairs(st)[0], j))

    def qspec(wd, j=0):
        return pl.BlockSpec((qw * tb, wd), lambda b, st: (b * nqs + pairs(st)[1], j))

    def qrow():
        return pl.BlockSpec((None, FOX_HEADS, qw * tb), lambda b, st: (b, 0, pairs(st)[1]))

    def tspec(which):
        if which == 0:
            return pl.BlockSpec((FOX_W, tb), lambda b, st: (0, b * nq + pairs(st)[0]))
        return pl.BlockSpec((FOX_W, qw * tb), lambda b, st: (0, b * nqs + pairs(st)[1]))

    return _call(
        body, name, (proj, proj, proj, da, qt, kt, dat, c_col, c_row, lse_row, delta_row), rider,
        grid=(nb, _tri_steps(nq, qw)),
        in_specs=[qspec(FOX_W, C_QC // FOX_W), kspec(FOX_W, C_KC // FOX_W), kspec(FOX_W, C_VC // FOX_W), qspec(FOX_W),
                  tspec(1), tspec(0), tspec(1), kspec(FC_PAD), qrow(), qrow(), qrow()],
        out_specs=[pl.BlockSpec((t, FOX_W), lambda b, st: (b, 0)), kspec(FOX_W), kspec(FOX_W), kspec(FC_PAD),
                   pl.BlockSpec((None, FOX_HEADS, t), lambda b, st: (b, 0, 0))],
        out_shape=[_sds((n, FOX_W), BF16), _sds((n, FOX_W), BF16), _sds((n, FOX_W), BF16), _sds((n, FC_PAD), F32),
                   _sds((nb, FOX_HEADS, t), F32)],
        scratch_shapes=[pltpu.VMEM((nq, FOX_W, tb), F32), pltpu.VMEM((tb, FOX_W), F32), pltpu.VMEM((tb, FOX_W), F32),
                        pltpu.VMEM((tb, FC_PAD), F32), pltpu.VMEM((nq, FOX_HEADS, tb), F32)],
        compiler_params=_params(),
    )


def _fox_decay_bwd(dc_q, dc_k, fc, bias, nb, name):
    n = fc.shape[0]
    t = n // nb
    tt = min(256, t)
    nt = t // tt

    def body(dcq_ref, dck_ref, fc_ref, b_ref, dfc_ref, db_ref):
        @pl.when(pl.program_id(0) == 0)
        def _():
            db_ref[...] = jnp.zeros_like(db_ref)

        r = lax.broadcasted_iota(jnp.int32, (tt, tt), 0)
        cc = lax.broadcasted_iota(jnp.int32, (tt, tt), 1)
        carry = jnp.zeros((1, FC_PAD), F32)
        db = jnp.zeros((1, FC_PAD), F32)
        for i in reversed(range(nt)):
            rows = slice(i * tt, (i + 1) * tt)
            dcq = jnp.concatenate([dcq_ref[:, rows], jnp.zeros((FC_PAD - FOX_HEADS, tt), F32)], axis=0).T
            dlf = _sel_dot_exact(r <= cc, dcq + dck_ref[rows, :]) + carry
            carry = dlf[0:1, :]
            dfc = dlf * _sigmoid(-(fc_ref[rows, :] + b_ref[...]))
            dfc_ref[rows, :] = dfc.astype(BF16)
            db = db + jnp.sum(dfc, axis=0, keepdims=True)
        db_ref[...] += db

    def row():
        return pl.BlockSpec((t, FC_PAD), lambda b: (b, 0))

    return _pc(
        body, name, grid=(nb,),
        in_specs=[pl.BlockSpec((None, FOX_HEADS, t), lambda b: (b, 0, 0)), row(), row(),
                  pl.BlockSpec((1, FC_PAD), lambda b: (0, 0))],
        out_specs=[row(), pl.BlockSpec((1, FC_PAD), lambda b: (0, 0))],
        out_shape=[_sds((n, FC_PAD), BF16), _sds((1, FC_PAD), F32)],
        compiler_params=_params(),
    )(dc_q, dc_k, fc, bias)


def _in_proj_bwd(pieces, w_main_t, w_fc_t, x, g_pre, dxo, name, rider=None):
    n, d = x.shape
    tm = min(512, n)
    widths = [p.shape[1] for p, _ in pieces]
    offs = [o for _, o in pieces]
    np_ = len(pieces)

    def body(*refs):
        p_refs = refs[:np_]
        wt_ref, wf_ref, x_ref, g_ref, dxo_ref, dx_ref, dg_ref = refs[np_:]

        @pl.when(pl.program_id(0) == 0)
        def _():
            dg_ref[...] = jnp.zeros_like(dg_ref)

        dh = _dot(p_refs[-1][...], wf_ref[...])
        for pr, wd, off in zip(p_refs[:-1], widths[:-1], offs[:-1]):
            for j in range(0, wd, 512):
                jw = min(512, wd - j)
                dh = dh + _dot(pr[:, j:j + jw], wt_ref[off + j:off + j + jw, :])
        xv = x_ref[...]
        r = lax.rsqrt(jnp.mean(xv * xv, axis=-1, keepdims=True) + NORM_EPS)
        xh = xv * r
        dg_ref[...] += jnp.sum(dh * xh, axis=0, keepdims=True)
        dx_ref[...] = dxo_ref[...] + _rms_bwd(dh * g_ref[...], xh, r)

    row = pl.BlockSpec((tm, d), lambda i: (i, 0))
    return _call(
        body, name, (*[p for p, _ in pieces], w_main_t, w_fc_t, x, g_pre, dxo), rider, grid=(n // tm,),
        in_specs=[pl.BlockSpec((tm, wd), lambda i: (i, 0)) for wd in widths] + [
            pl.BlockSpec((MAIN_W, d), lambda i: (0, 0)), pl.BlockSpec((FC_PAD, d), lambda i: (0, 0)),
            row, pl.BlockSpec((1, d), lambda i: (0, 0)), row],
        out_specs=[row, pl.BlockSpec((1, d), lambda i: (0, 0))],
        out_shape=[_sds((n, d), F32), _sds((1, d), F32)],
        compiler_params=_params(),
    )


def _lower_bound_table(lower_bounds, name):
    depth, w = lower_bounds.shape

    def body(lb_ref, o_ref):
        v = lb_ref[...]
        e = jnp.exp(v - jnp.max(v, axis=0, keepdims=True))
        p = e / jnp.sum(e, axis=0, keepdims=True)
        acc = jnp.zeros((1, w), F32)
        for l in range(depth):
            acc = acc + p[l:l + 1, :]
            o_ref[l:l + 1, :] = acc - p[0:1, :]

    return _pc(body, name, out_shape=_sds((depth, w), F32))(lower_bounds)


def _lower_bound_bwd(lower_bounds, dlbs, name):
    depth, w = lower_bounds.shape

    def body(lb_ref, d_ref, o_ref):
        v, dl = lb_ref[...], d_ref[...]
        e = jnp.exp(v - jnp.max(v, axis=0, keepdims=True))
        p = e / jnp.sum(e, axis=0, keepdims=True)
        tot = jnp.sum(dl, axis=0, keepdims=True)
        rows, tail = [], tot
        for l in range(depth):
            rows.append(tail - tot if l == 0 else tail)
            tail = tail - dl[l:l + 1, :]
        dp = jnp.concatenate(rows, axis=0)
        o_ref[...] = p * (dp - jnp.sum(p * dp, axis=0, keepdims=True))

    return _pc(body, name, out_shape=_sds((depth, w), F32))(lower_bounds, dlbs)


def _place():
    x, y, c = lax.axis_index("x"), lax.axis_index("y"), lax.axis_index("c")
    return x, y, c


def _gather_weights(*arrays):
    na = len(arrays)

    def body(*refs):
        ins, outs = refs[:na], refs[na:2 * na]
        send_sems, recv_sems, local_sems = refs[2 * na:]
        x, y, c = _place()
        me, sibling = (x, y, c), (x, y, 1 - c)
        chips = [(1 - x, y), (x, 1 - y), (1 - x, 1 - y)]

        def slot(a, px, py, pc):
            return outs[a].at[4 * px + 2 * py + pc]

        def copy(a, k, block, to, own=False):
            return pltpu.make_async_remote_copy(
                src_ref=ins[a] if own else slot(a, *block), dst_ref=slot(a, *block),
                send_sem=send_sems.at[a * 7 + k], recv_sem=recv_sems.at[a * 7 + k],
                device_id=to, device_id_type=MESH)

        mine = [pltpu.make_async_copy(ins[a], slot(a, *me), local_sems.at[a]) for a in range(na)]
        for cp in mine:
            cp.start()
        first = []
        for a in range(na):
            first.append(copy(a, 0, me, sibling, own=True))
            first += [copy(a, 1 + j, me, (*chip, c), own=True) for j, chip in enumerate(chips)]
        for cp in first:
            cp.start()
        passed = []
        for j, chip in enumerate(chips):
            for a in range(na):
                copy(a, 1 + j, (*chip, c), me).wait_recv()
                fw = copy(a, 4 + j, (*chip, c), sibling)
                fw.start()
                passed.append(fw)
        for a in range(na):
            copy(a, 0, sibling, me).wait_recv()
            for j, chip in enumerate(chips):
                copy(a, 4 + j, (*chip, 1 - c), me).wait_recv()
        for cp in first + passed:
            cp.wait_send()
        for cp in mine:
            cp.wait()

    any_spec = pl.BlockSpec(memory_space=pl.ANY)
    return _pc(
        body, "gather_weights",
        in_specs=[any_spec] * na, out_specs=[any_spec] * na,
        out_shape=[_sds((N_DEV,) + a.shape, a.dtype) for a in arrays],
        scratch_shapes=[pltpu.SemaphoreType.DMA((7 * na,)), pltpu.SemaphoreType.DMA((7 * na,)),
                        pltpu.SemaphoreType.DMA((na,))],
    )(*arrays)


def _peer(k):
    x, y, c = _place()
    return (1 - x if k & 4 else x, 1 - y if k & 2 else y, 1 - c if k & 1 else c)


def _remote(src, dst, sems, s, to):
    return pltpu.make_async_remote_copy(src_ref=src, dst_ref=dst, send_sem=sems[0].at[s], recv_sem=sems[1].at[s],
                                        device_id=to, device_id_type=MESH)


def _gather_rider(shards):
    na = len(shards)

    def plan(ins, outs, *sems):
        x, y, c = _place()
        me = 4 * x + 2 * y + c
        locs = [pltpu.make_async_copy(ins[a], outs[a].at[me], sems[2].at[a]) for a in range(na)]
        sends, recvs = [], []
        for k in range(1, N_DEV):
            px, py, pc = _peer(k)
            for a in range(na):
                s = (k - 1) * na + a
                sends.append(_remote(ins[a], outs[a].at[me], sems, s, (px, py, pc)))
                recvs.append(_remote(ins[a], outs[a].at[4 * px + 2 * py + pc], sems, s, (px, py, pc)))
        return sends, recvs, locs

    return _Rider(shards, [_sds((N_DEV,) + a.shape, a.dtype) for a in shards], (N_DEV - 1) * na, na, plan)


def _direct_exchange_rider(blocks):
    na = len(blocks)

    def plan(ins, outs, *sems):
        x, y, c = _place()
        me = 4 * x + 2 * y + c
        locs = [pltpu.make_async_copy(ins[a].at[c, 2 * x + y], outs[a].at[me], sems[2].at[a]) for a in range(na)]
        sends, recvs = [], []
        for k in range(1, N_DEV):
            px, py, pc = _peer(k)
            for a in range(na):
                s = (k - 1) * na + a
                sends.append(_remote(ins[a].at[pc, 2 * px + py], outs[a].at[me], sems, s, (px, py, pc)))
                recvs.append(_remote(ins[a].at[pc, 2 * px + py], outs[a].at[4 * px + 2 * py + pc], sems, s, (px, py, pc)))
        return sends, recvs, locs

    return _Rider(blocks, [_sds((N_DEV,) + a.shape[2:], a.dtype) for a in blocks], (N_DEV - 1) * na, na, plan)


def _swap_rider(halves):
    na = len(halves)

    def plan(ins, outs, *sems):
        x, y, c = _place()
        cps = [_remote(ins[a].at[1 - c], outs[a], sems, a, (x, y, 1 - c)) for a in range(na)]
        return cps, cps, []

    return _Rider(halves, [_sds(a.shape[1:], a.dtype) for a in halves], na, 1, plan)


def _chip_exchange_rider(parts, small=None):
    na = len(parts)
    n_chip = N_DEV // 2

    def plan(ins, outs, *sems):
        x, y, c = _place()
        chip = 2 * x + y
        locs = [pltpu.make_async_copy(ins[a].at[chip], outs[a].at[chip], sems[2].at[a]) for a in range(na)]
        sends, recvs = [], []
        for k in range(1, n_chip):
            px, py, _ = _peer(2 * k)
            for a in range(na):
                s = (k - 1) * na + a
                sends.append(_remote(ins[a].at[2 * px + py], outs[a].at[chip], sems, s, (px, py, c)))
                recvs.append(_remote(ins[a].at[2 * px + py], outs[a].at[2 * px + py], sems, s, (px, py, c)))
        if small is not None:
            me = 2 * chip + c
            locs.append(pltpu.make_async_copy(ins[na], outs[na].at[me], sems[2].at[na]))
            for k in range(1, N_DEV):
                px, py, pc = _peer(k)
                s = (n_chip - 1) * na + k - 1
                sends.append(_remote(ins[na], outs[na].at[me], sems, s, (px, py, pc)))
                recvs.append(_remote(ins[na], outs[na].at[4 * px + 2 * py + pc], sems, s, (px, py, pc)))
        return sends, recvs, locs

    extra = [] if small is None else [small]
    shapes = [_sds(a.shape, a.dtype) for a in parts] + [_sds((N_DEV,) + s.shape, s.dtype) for s in extra]
    n_sems = (n_chip - 1) * na + (N_DEV - 1) * len(extra)
    return _Rider(list(parts) + extra, shapes, n_sems, na + len(extra), plan)


def _pair_add(halves, other, core, name):
    _, nch, r, c = halves.shape

    def body(c_ref, h_ref, o_ref, p_ref):
        p_ref[...] = (h_ref[...].astype(F32) + o_ref[...].astype(F32)).astype(BF16)

    blk = pl.BlockSpec((None, r, c), lambda j, c_ref: (j, 0, 0))
    return _pc(
        body, name,
        grid_spec=pltpu.PrefetchScalarGridSpec(
            num_scalar_prefetch=1, grid=(nch,),
            in_specs=[pl.BlockSpec((None, None, r, c), lambda j, c_ref: (c_ref[0], j, 0, 0)), blk], out_specs=blk),
        out_shape=_sds((nch, r, c), BF16),
        compiler_params=_params(),
    )(core, halves, other)


def _sum_adamw(parts, w, m, v, name, rider=None):
    nl, r, c = w.shape
    tr = 256 if r % 256 == 0 else r

    def body(*refs):
        p_refs = refs[:nl]
        w_ref, m_ref, v_ref, g_ref, d_ref, mo_ref, vo_ref = refs[nl:]
        for l in range(nl):
            @pl.when(pl.program_id(0) == l)
            def _(p_ref=p_refs[l]):
                g = p_ref[0].astype(F32)
                for j in range(1, p_ref.shape[0]):
                    g = g + p_ref[j].astype(F32)
                mn = ADAM_B1 * m_ref[...] + (1.0 - ADAM_B1) * g
                vn = ADAM_B2 * v_ref[...] + (1.0 - ADAM_B2) * (g * g)
                m_hat = mn / (1.0 - ADAM_B1 ** ADAM_STEP)
                v_hat = vn / (1.0 - ADAM_B2 ** ADAM_STEP)
                g_ref[...] = g
                d_ref[...] = -ADAM_LR * (m_hat / (jnp.sqrt(v_hat) + ADAM_EPS) + ADAM_WD * w_ref[...])
                mo_ref[...] = mn
                vo_ref[...] = vn

    def part_spec(l, k):
        return pl.BlockSpec((k, tr, c), lambda li, i: (0, jnp.where(li == l, i, 0), 0))

    row = pl.BlockSpec((None, tr, c), lambda li, i: (li, i, 0))
    return _call(
        body, name, (*parts, w, m, v), rider, grid=(nl, r // tr),
        in_specs=[part_spec(l, p.shape[0]) for l, p in enumerate(parts)] + [row, row, row],
        out_specs=[row] * 4,
        out_shape=[_sds((nl, r, c), F32)] * 4,
        compiler_params=_params(),
    )


SMALL = ("lower_bounds", "pre_norm_g", "hgrn_norm_g", "pool_w", "pool_scale", "post_norm_g", "fox_f_bias")
SMALL_LANES = 128


def _small_size(tree):
    return sum(tree[k].size for k in SMALL)


def _pack_small(tree, extra=None):
    flat = jnp.concatenate([tree[k].reshape(-1) for k in SMALL] + ([] if extra is None else [extra.reshape(1)]))
    rows = -(-(_small_size(tree) + 1) // (8 * SMALL_LANES)) * 8
    return jnp.pad(flat, (0, rows * SMALL_LANES - flat.shape[0])).reshape(rows, SMALL_LANES)


def _unpack_small(packed, like):
    out, off = {}, 0
    for k in SMALL:
        size = like[k].size
        assert off % SMALL_LANES == 0
        rows = packed[off // SMALL_LANES:-(-(off + size) // SMALL_LANES)]
        out[k] = rows.reshape(-1)[:size].reshape(like[k].shape)
        off += size
    return out


def _block_diag(pw):
    g = pw.shape[0]
    eye = jnp.eye(g, dtype=pw.dtype)
    return (eye[:, None, :, None] * pw[:, :, None, :]).reshape(g * HEAD, g * HEAD)


def _assemble_w_in(g_in, name):
    _, d, shard = g_in.shape
    tr = min(256, d)
    wide = MAIN_W + FC_PAD

    def body(g_ref, wm_ref, wf_ref, wmt_ref, wft_ref, row_s):
        row_s[:, MAIN_W:] = jnp.zeros((tr, FC_PAD), F32)
        for j in range(N_DEV):
            row_s[:, j * shard:(j + 1) * shard] = g_ref[j].astype(F32)
        wm_ref[...] = row_s[:, :MAIN_W].astype(BF16)
        wf_ref[...] = row_s[:, MAIN_W:].astype(BF16)
        for j in range(0, MAIN_W, 512):
            wmt_ref[j:j + 512, :] = row_s[:, j:j + 512].T.astype(BF16)
        wft_ref[...] = row_s[:, MAIN_W:].T.astype(BF16)

    return _pc(
        body, name, grid=(d // tr,),
        in_specs=[pl.BlockSpec((N_DEV, tr, shard), lambda i: (0, i, 0))],
        out_specs=[pl.BlockSpec((tr, MAIN_W), lambda i: (i, 0)), pl.BlockSpec((tr, FC_PAD), lambda i: (i, 0)),
                   pl.BlockSpec((MAIN_W, tr), lambda i: (0, i)), pl.BlockSpec((FC_PAD, tr), lambda i: (0, i))],
        out_shape=[_sds((d, MAIN_W), BF16), _sds((d, FC_PAD), BF16), _sds((MAIN_W, d), BF16), _sds((FC_PAD, d), BF16)],
        scratch_shapes=[pltpu.VMEM((tr, wide), F32)],
        compiler_params=_params(),
    )(g_in)


def _w_out_parts(g_out):
    full_out = g_out.reshape(N_DEV * g_out.shape[1], g_out.shape[2])
    return full_out, full_out.T


def _layer_fwd(l, x, lbs, weights, lw, nb, rider_h=None, rider_c=None, target=None):
    w_main, w_fc, _, _, w_out, _ = lw
    bias = jnp.pad(weights["fox_f_bias"][l:l + 1], ((0, 0), (0, FC_PAD - FOX_HEADS)))
    wbd = _block_diag(weights["pool_w"][l]).astype(BF16)
    proj, fc, ht, qt, kt = _in_proj_fwd(x, weights["pre_norm_g"][l:l + 1], w_main, w_fc, f"in_proj_fwd_{l}")
    c_col, c_row = _fox_decay_fwd(fc, bias, nb, f"fox_decay_fwd_{l}")
    (o_h, s0), rode_h = _hgrn_fwd(proj, lbs[l:l + 1], _block_ones(HGRN_W, BF16), nb, f"hgrn_fwd_{l}", rider_h)
    (o_c, lse), rode_c = _fox_fwd(proj, kt, c_col, c_row, nb, f"fox_fwd_{l}", rider_c)
    if w_out is None:
        lw = tuple(lw[:4]) + _w_out_parts(rode_h[0])
        w_out, rode_h = lw[4], rode_h[1:]
    x_next, mixt, y = _merge_fwd(x, proj, o_h, o_c, weights["hgrn_norm_g"][l:l + 1], wbd, weights["pool_scale"][l:l + 1],
                                 w_out, weights["post_norm_g"][l:l + 1], nb, f"merge_fwd_{l}", target)
    return x_next, (x, proj, fc, ht, qt, kt, c_col, c_row, o_h, s0, o_c, lse, mixt, y, bias, wbd), lw, (rode_h, rode_c)


def _layer_bwd(l, dx, saved, lbs, weights, lw, nb, rider=None, send_w_out=None):
    _, proj, fc, ht, qt, kt, c_col, c_row, o_h, s0, o_c, lse, mixt, y, bias, wbd = saved
    w_out_t = lw[5]
    g = {}
    g["w_out"], dmix, dgp, da, dat, d_gc, delta = _merge_bwd(dx, y, weights["post_norm_g"][l:l + 1], w_out_t, proj, o_c,
                                                         mixt, nb, f"merge_bwd_{l}")
    g["post_norm_g"] = dgp[0]
    (d_a, dgh, dlb), arrived = _hgrn_bwd(dmix, proj, o_h, s0, weights["hgrn_norm_g"][l:l + 1], lbs[l:l + 1],
                                         _block_ones(HGRN_W, BF16), nb, f"hgrn_bwd_{l}",
                                         None if send_w_out is None else send_w_out([g["w_out"]]))
    if arrived is not None:
        g["w_out_received"] = arrived[0]
    g["hgrn_norm_g"], g["lbs"] = dgh[0], dlb[0]
    (d_qc, d_kc, d_vc, dc_k, dc_q), rode = _fox_bwd(proj, qt, kt, da, dat, c_col, c_row, lse, delta, nb,
                                                    f"fox_bwd_{l}", rider)
    d_fc, dbias = _fox_decay_bwd(dc_q, dc_k, fc, bias, nb, f"fox_decay_bwd_{l}")
    g["fox_f_bias"] = dbias[0, :FOX_HEADS]
    pieces = [(d_a, C_QA), (None, C_UB), (d_qc, C_QC), (d_kc, C_KC), (d_vc, C_VC), (d_gc, C_GC), (d_fc, None)]
    g["w_in"], d_b, dwbd, dps = _w_in_grad(ht, pieces, dmix, proj, wbd, wbd.T, weights["pool_scale"][l:l + 1], nb,
                                           f"w_in_grad_{l}")
    pieces[1] = (d_b, C_UB)
    g["pool_w"] = jnp.stack([dwbd[j * HEAD:(j + 1) * HEAD, j * HEAD:(j + 1) * HEAD] for j in range(len(POOL_WINDOWS))])
    g["pool_scale"] = dps[0]
    return g, pieces, rode


def _layer_bwd_input(l, dx, pieces, saved, weights, lw, rider=None):
    (dxi, dgpre), rode = _in_proj_bwd(pieces, lw[2], lw[3], saved[0], weights["pre_norm_g"][l:l + 1], dx,
                                      f"in_proj_bwd_{l}", rider)
    return dxi, dgpre[0], rode


def kernel(x, lower_bounds, pre_norm_g, w_in, hgrn_norm_g, fox_f_bias, pool_w, pool_scale, w_out, post_norm_g, loss_target, m_lower_bounds, m_pre_norm_g, m_w_in, m_hgrn_norm_g, m_fox_f_bias, m_pool_w, m_pool_scale, m_w_out, m_post_norm_g, v_lower_bounds, v_pre_norm_g, v_w_in, v_hgrn_norm_g, v_fox_f_bias, v_pool_w, v_pool_scale, v_w_out, v_post_norm_g):
    weights = dict(lower_bounds=lower_bounds, pre_norm_g=pre_norm_g, hgrn_norm_g=hgrn_norm_g, fox_f_bias=fox_f_bias,
                   pool_w=pool_w, pool_scale=pool_scale, post_norm_g=post_norm_g)
    mom_m = dict(lower_bounds=m_lower_bounds, pre_norm_g=m_pre_norm_g, hgrn_norm_g=m_hgrn_norm_g, fox_f_bias=m_fox_f_bias,
                 pool_w=m_pool_w, pool_scale=m_pool_scale, post_norm_g=m_post_norm_g)
    mom_v = dict(lower_bounds=v_lower_bounds, pre_norm_g=v_pre_norm_g, hgrn_norm_g=v_hgrn_norm_g, fox_f_bias=v_fox_f_bias,
                 pool_w=v_pool_w, pool_scale=v_pool_scale, post_norm_g=v_post_norm_g)
    depth = w_in.shape[0]
    nb, t, d = x.shape
    n = nb * t
    core = lax.axis_index("c").astype(jnp.int32).reshape(1)
    shards = [(w_in[l].astype(BF16), w_out[l].astype(BF16)) for l in range(depth)]
    lbs = _lower_bound_table(lower_bounds, "lower_bound_table")

    (g_in,) = _gather_weights(shards[0][0])
    coming = tuple(_assemble_w_in(g_in, "assemble_w_in_0")) + (None, None)
    xl, saved, lw = x.reshape(n, d), [], []
    for l in range(depth):
        last = l + 1 == depth
        ride_h = ([shards[l][1]] if coming[4] is None else []) + ([] if last else [shards[l + 1][1]])
        xl, sv, lw_l, (rode_h, rode_c) = _layer_fwd(
            l, xl, lbs, weights, coming, nb, _gather_rider(ride_h) if ride_h else None,
            None if last else _gather_rider([shards[l + 1][0]]), loss_target.reshape(n, d) if last else None)
        saved.append(sv)
        lw.append(lw_l)
        if not last:
            coming = tuple(_assemble_w_in(rode_c[0], f"assemble_w_in_{l + 1}")) + _w_out_parts(rode_h[0])
    dx, sq = xl
    loss_here = 0.5 * jnp.sum(sq) / d

    grads, recv_in, pending = [None] * depth, [None] * depth, None
    for l in reversed(range(depth)):
        g, pieces, rode = _layer_bwd(l, dx, saved[l], lbs, weights, lw[l], nb, pending, _direct_exchange_rider)
        if rode is not None:
            recv_in[l + 1] = rode[0]
        if l > 0:
            pending = _direct_exchange_rider([g["w_in"]])
            dx, g["pre_norm_g"], _ = _layer_bwd_input(l, dx, pieces, saved[l], weights, lw[l])
        else:
            (other,) = _run_rider(_swap_rider([g["w_in"]]), "grad_swap")
            summed = _pair_add(g["w_in"], other, core, "grad_pair_add")
            dx, g["pre_norm_g"], (recv_in[l],) = _layer_bwd_input(l, dx, pieces, saved[l], weights, lw[l],
                                                                  _chip_exchange_rider([summed]))
        grads[l] = g
    small = {k: jnp.stack([grads[l][k] for l in range(depth)]) for k in SMALL if k != "lower_bounds"}
    small["lower_bounds"] = _lower_bound_bwd(lower_bounds, jnp.stack([grads[l]["lbs"] for l in range(depth)]),
                                             "lower_bound_bwd")
    (r_small,) = _run_rider(_gather_rider([_pack_small(small, loss_here)]), "small_grads_gather")

    res_in, _ = _sum_adamw(recv_in, w_in, m_w_in, v_w_in, "adamw_w_in")
    res_out, _ = _sum_adamw([grads[l]["w_out_received"] for l in range(depth)], w_out, m_w_out, v_w_out, "adamw_w_out")
    res_small, _ = _sum_adamw([r_small], _pack_small(weights)[None], _pack_small(mom_m)[None], _pack_small(mom_v)[None],
                              "adamw_small")
    loss = res_small[0][0][_small_size(weights) // SMALL_LANES, _small_size(weights) % SMALL_LANES]

    names = ("lower_bounds", "pre_norm_g", "w_in", "hgrn_norm_g", "fox_f_bias", "pool_w", "pool_scale", "w_out", "post_norm_g")
    outs = [loss, dx.reshape(nb, t, d)]
    for i in range(4):
        full = dict(_unpack_small(res_small[i][0], weights), w_in=res_in[i], w_out=res_out[i])
        outs += [full[k] for k in names]
    return tuple(outs)
```

```python
import functools

import jax
import jax.numpy as jnp
from jax import lax
from jax.experimental import pallas as pl
from jax.experimental.pallas import tpu as pltpu

F32, BF16 = jnp.float32, jnp.bfloat16
MESH = pl.DeviceIdType.MESH
N_DEV = 8

NORM_EPS = 1e-6
MASK_VALUE = -1e30
TINY = 1e-30
CHUNK = 64
SUB = 16
HGRN_W, POOL_W, FOX_W = 256, 256, 512
HEAD = 64
FOX_HEADS = 8
POOL_WINDOWS = (2, 4, 8, 16)
POOL_HALO = 16
MAIN_W = 3584
FC_PAD = 128
C_QA, C_FA, C_IA, C_GA, C_UB, C_GB, C_QC, C_KC, C_VC, C_GC = 0, 256, 512, 768, 1024, 1280, 1536, 2048, 2560, 3072
FOX_SCALE = HEAD ** -0.5

ADAM_LR, ADAM_B1, ADAM_B2, ADAM_EPS, ADAM_WD, ADAM_STEP = 0.001, 0.9, 0.999, 1e-08, 0.01, 10

VMEM_LIMIT = 56 * 1024 * 1024


def _pc(fn, name, **kw):
    return pl.pallas_call(fn, name=name, **kw)


def _params(**kw):
    return pltpu.CompilerParams(vmem_limit_bytes=VMEM_LIMIT, **kw)


class _Rider:
    def __init__(self, inputs, out_shapes, n_sems, n_local, plan):
        self.inputs, self.out_shapes, self.n_sems, self.n_local, self.plan = list(inputs), list(out_shapes), n_sems, n_local, plan

    def start(self, ins, outs, *sems):
        sends, _, locs = self.plan(ins, outs, *sems)
        for cp in locs + sends:
            cp.start()

    def wait(self, ins, outs, *sems):
        sends, recvs, locs = self.plan(ins, outs, *sems)
        for cp in recvs:
            cp.wait_recv()
        for cp in sends:
            cp.wait_send()
        for cp in locs:
            cp.wait()

    def sem_shapes(self):
        return [pltpu.SemaphoreType.DMA((self.n_sems,)), pltpu.SemaphoreType.DMA((self.n_sems,)),
                pltpu.SemaphoreType.DMA((self.n_local,))]


def _call(body, name, args, rider=None, *, grid, in_specs, out_specs, out_shape, scratch_shapes=(), **kw):
    if rider is None:
        res = _pc(body, name, grid=grid, in_specs=in_specs, out_specs=out_specs, out_shape=out_shape,
                  scratch_shapes=list(scratch_shapes), **kw)(*args)
        return res, None
    n_in, n_out, n_scr = len(in_specs), len(out_specs), len(scratch_shapes)
    n_rin, n_rout = len(rider.inputs), len(rider.out_shapes)

    def ridden(*refs):
        ins, refs = refs[:n_in], refs[n_in:]
        rins, refs = refs[:n_rin], refs[n_rin:]
        outs, refs = refs[:n_out], refs[n_out:]
        routs, refs = refs[:n_rout], refs[n_rout:]
        scr, sems = refs[:n_scr], refs[n_scr:]
        first = functools.reduce(jnp.logical_and, [pl.program_id(a) == 0 for a in range(len(grid))])
        last = functools.reduce(jnp.logical_and, [pl.program_id(a) == g - 1 for a, g in enumerate(grid)])

        @pl.when(first)
        def _():
            rider.start(rins, routs, *sems)

        body(*ins, *outs, *scr)

        @pl.when(last)
        def _():
            rider.wait(rins, routs, *sems)

    any_spec = pl.BlockSpec(memory_space=pl.ANY)
    res = _pc(ridden, name, grid=grid, in_specs=list(in_specs) + [any_spec] * n_rin,
              out_specs=list(out_specs) + [any_spec] * n_rout, out_shape=list(out_shape) + rider.out_shapes,
              scratch_shapes=list(scratch_shapes) + rider.sem_shapes(), **kw)(*args, *rider.inputs)
    return res[:n_out], res[n_out:]


def _run_rider(rider, name):
    n_rin = len(rider.inputs)

    def body(*refs):
        ins, outs, sems = refs[:n_rin], refs[n_rin:n_rin + len(rider.out_shapes)], refs[n_rin + len(rider.out_shapes):]
        rider.start(ins, outs, *sems)
        rider.wait(ins, outs, *sems)

    any_spec = pl.BlockSpec(memory_space=pl.ANY)
    return _pc(body, name, in_specs=[any_spec] * n_rin, out_specs=[any_spec] * len(rider.out_shapes),
               out_shape=rider.out_shapes, scratch_shapes=rider.sem_shapes())(*rider.inputs)


def _dot(a, b):
    return jnp.dot(a, b, preferred_element_type=F32)


def _dot_nt(a, b):
    return lax.dot_general(a, b, (((1,), (1,)), ((), ())), preferred_element_type=F32)


def _dot_tn(a, b):
    return lax.dot_general(a, b, (((0,), (0,)), ((), ())), preferred_element_type=F32)


def _sel_dot_exact(sel, x):
    hi = x.astype(BF16)
    rest = x - hi.astype(F32)
    mid = rest.astype(BF16)
    lo = (rest - mid.astype(F32)).astype(BF16)
    sb = sel.astype(BF16)
    return _dot(sb, hi) + _dot(sb, mid) + _dot(sb, lo)


def _split2(x):
    hi = x.astype(BF16)
    return hi, (x - hi.astype(F32)).astype(BF16)


def _sel_dot(sel, x):
    hi, lo = _split2(x)
    sb = sel.astype(BF16)
    return _dot(sb, hi) + _dot(sb, lo)


def _dot_sel(x, sel):
    hi, lo = _split2(x)
    sb = sel.astype(BF16)
    return _dot(hi, sb) + _dot(lo, sb)


def _sigmoid(x):
    return 1.0 / (1.0 + jnp.exp(-x))


def _block_ones(n, dtype):
    r = lax.broadcasted_iota(jnp.int32, (n, n), 0) // HEAD
    c = lax.broadcasted_iota(jnp.int32, (n, n), 1) // HEAD
    return (r == c).astype(dtype)


def _sds(shape, dtype):
    return jax.ShapeDtypeStruct(shape, dtype)


def _in_proj_fwd(x, g_pre, w_main, w_fc, name):
    n, d = x.shape
    tm = min(512, n)

    def body(x_ref, g_ref, w_ref, wf_ref, proj_ref, fc_ref, ht_ref, qt_ref, kt_ref):
        xv = x_ref[...]
        r = lax.rsqrt(jnp.mean(xv * xv, axis=-1, keepdims=True) + NORM_EPS)
        hf = xv * r * g_ref[...]
        hb = hf.astype(BF16)
        ht_ref[...] = hf.T.astype(BF16)
        for j in range(0, MAIN_W, FOX_W):
            res = _dot(hb, w_ref[:, j:j + FOX_W])
            proj_ref[:, j:j + FOX_W] = res
            if j == C_QC:
                qt_ref[...] = (res * FOX_SCALE).T.astype(BF16)
            if j == C_KC:
                kt_ref[...] = res.T.astype(BF16)
        fc_ref[...] = _dot(hb, wf_ref[...])

    def cols(rows):
        return pl.BlockSpec((rows, tm), lambda i: (0, i))

    return _pc(
        body, name, grid=(n // tm,),
        in_specs=[pl.BlockSpec((tm, d), lambda i: (i, 0)), pl.BlockSpec((1, d), lambda i: (0, 0)),
                  pl.BlockSpec((d, MAIN_W), lambda i: (0, 0)), pl.BlockSpec((d, FC_PAD), lambda i: (0, 0))],
        out_specs=[pl.BlockSpec((tm, MAIN_W), lambda i: (i, 0)), pl.BlockSpec((tm, FC_PAD), lambda i: (i, 0)),
                   cols(d), cols(FOX_W), cols(FOX_W)],
        out_shape=[_sds((n, MAIN_W), F32), _sds((n, FC_PAD), F32), _sds((d, n), BF16), _sds((FOX_W, n), BF16),
                   _sds((FOX_W, n), BF16)],
        compiler_params=_params(),
    )(x, g_pre, w_main, w_fc)


def _fox_decay_fwd(fc, bias, nb, name):
    n = fc.shape[0]
    t = n // nb
    tt = min(256, t)
    nt = t // tt

    def body(fc_ref, b_ref, c_ref, cr_ref):
        r = lax.broadcasted_iota(jnp.int32, (tt, tt), 0)
        cc = lax.broadcasted_iota(jnp.int32, (tt, tt), 1)
        carry = jnp.zeros((1, FC_PAD), F32)
        for i in range(nt):
            rows = slice(i * tt, (i + 1) * tt)
            xv = fc_ref[rows, :] + b_ref[...]
            lf = jnp.minimum(xv, 0.0) - jnp.log(1.0 + jnp.exp(-jnp.abs(xv)))
            cs = _sel_dot_exact(r >= cc, lf) + carry
            c_ref[rows, :] = cs
            cr_ref[:, rows] = cs.T[:FOX_HEADS, :]
            carry = cs[tt - 1:tt, :]

    return _pc(
        body, name, grid=(nb,),
        in_specs=[pl.BlockSpec((t, FC_PAD), lambda b: (b, 0)), pl.BlockSpec((1, FC_PAD), lambda b: (0, 0))],
        out_specs=[pl.BlockSpec((t, FC_PAD), lambda b: (b, 0)), pl.BlockSpec((None, FOX_HEADS, t), lambda b: (b, 0, 0))],
        out_shape=[_sds((n, FC_PAD), F32), _sds((nb, FOX_HEADS, t), F32)],
        compiler_params=_params(),
    )(fc, bias)


def _hgrn_gates(q, z, lb):
    sig = _sigmoid(z)
    sn = _sigmoid(-z)
    f = lb + (1.0 - lb) * sig
    g = jnp.log(jnp.maximum(f, TINY))
    k = (1.0 - lb) * sn
    sq = _sigmoid(q)
    return sig, sn, f, g, k, sq


def _sub_tri(n, lower):
    r = lax.broadcasted_iota(jnp.int32, (n, n), 0)
    c = lax.broadcasted_iota(jnp.int32, (n, n), 1)
    tri = (r >= c) if lower else (r <= c)
    return jnp.logical_and(r // SUB == c // SUB, tri).astype(F32)


def _live_rows(t):
    return 8 * (t // 8 + 1)


def _pad_rows(x):
    return x if x.shape[0] == SUB else jnp.concatenate([x, jnp.zeros((SUB - x.shape[0], x.shape[1]), x.dtype)], axis=0)


def _hgrn_decays(qs, k, b):
    srow = lax.broadcasted_iota(jnp.int32, (SUB, HGRN_W), 0)
    es, ws = [], []
    for t in range(SUB):
        r = _live_rows(t)
        e = jnp.where(srow[:r] <= t, jnp.exp(b[t:t + 1, :] - b[:r]), 0.0)
        es.append(e)
        ws.append(_pad_rows(e * (qs[t:t + 1, :] * k[:r])))
    return srow, es, ws


def _hgrn_state_step(st, k, v, b, bmask):
    bl = b[SUB - 1:SUB, :]
    ktil = k * jnp.exp(bl - b)
    return st * jnp.exp(bl) + _dot_tn(v.astype(BF16), ktil.astype(BF16)) * bmask


def _hgrn_tile(t):
    return min(256, t)


def _hgrn_fwd(proj, lb, ones_b, nb, name, rider=None):
    n = proj.shape[0]
    t = n // nb
    tt = _hgrn_tile(t)
    nt = t // tt
    ncs = tt // CHUNK
    w = HGRN_W

    def body(q_ref, z_ref, v_ref, lb_ref, ones_ref, o_ref, s0_ref, st_s, b_s, qs_s, k_s):
        @pl.when(pl.program_id(0) == 0)
        def _():
            st_s[...] = jnp.zeros_like(st_s)

        for e in range(nb):
            q = q_ref[e]
            _, _, _, g, k, sq = _hgrn_gates(q, z_ref[e], lb_ref[...])
            b_s[e] = _sel_dot(_sub_tri(tt, True), g)
            qs_s[e] = q * sq
            k_s[e] = k
        bmask = _block_ones(w, F32)
        ones_b = ones_ref[...]
        nsub = CHUNK // SUB

        def chunk(c, carry):
            sts = [st_s[e] for e in range(nb)]
            for e in range(nb):
                s0_ref[e, c] = sts[e]
            base = pl.multiple_of(c * CHUNK, CHUNK)
            tiles = [[(qs_s[e, pl.ds(base + u * SUB, SUB), :], k_s[e, pl.ds(base + u * SUB, SUB), :],
                       v_ref[e, pl.ds(base + u * SUB, SUB), :], b_s[e, pl.ds(base + u * SUB, SUB), :])
                      for u in range(nsub)] for e in range(nb)]
            aexps = [[] for _ in range(nb)]
            for e in range(nb):
                for qs, k, v, b in tiles[e]:
                    _, _, ws = _hgrn_decays(qs, k, b)
                    aexps[e].append(_dot(jnp.concatenate(ws, axis=0).astype(BF16), ones_b))
            inters = [[] for _ in range(nb)]
            for u in range(nsub):
                for e in range(nb):
                    qs, k, v, b = tiles[e][u]
                    inters[e].append(_dot_nt((qs * jnp.exp(b)).astype(BF16), sts[e].astype(BF16)))
                    sts[e] = _hgrn_state_step(sts[e], k, v, b, bmask)
            for e in range(nb):
                st_s[e] = sts[e]
            for e in range(nb):
                for u, ((qs, k, v, b), aexp, o) in enumerate(zip(tiles[e], aexps[e], inters[e])):
                    for t in range(SUB):
                        r = _live_rows(t)
                        row = o[t:t + 1, :] + jnp.sum(aexp[t * SUB:t * SUB + r, :] * v[:r], axis=0, keepdims=True)
                        o_ref[e, pl.ds(base + u * SUB + t, 1), :] = row
            return carry

        lax.fori_loop(0, ncs, chunk, 0)

    def col(j):
        return pl.BlockSpec((nb, tt, w), lambda i: (0, i, j))

    proj3 = proj.reshape(nb, t, proj.shape[1])
    (o, s0), rode = _call(
        body, name, (proj3, proj3, proj3, lb, ones_b), rider, grid=(nt,),
        in_specs=[col(C_QA // w), col(C_FA // w), col(C_IA // w), pl.BlockSpec((1, w), lambda i: (0, 0)),
                  pl.BlockSpec((w, w), lambda i: (0, 0))],
        out_specs=[pl.BlockSpec((nb, tt, w), lambda i: (0, i, 0)),
                   pl.BlockSpec((nb, ncs, w, w), lambda i: (0, i, 0, 0))],
        out_shape=[_sds((nb, t, w), F32), _sds((nb, t // CHUNK, w, w), F32)],
        scratch_shapes=[pltpu.VMEM((nb, w, w), F32)] + [pltpu.VMEM((nb, tt, w), F32)] * 3,
        compiler_params=_params(),
    )
    return (o.reshape(n, w), s0.reshape(n // CHUNK, w, w)), rode


def _pool_lane_windows():
    lane = lax.broadcasted_iota(jnp.int32, (1, POOL_W), 1) // HEAD
    wl = jnp.zeros((1, POOL_W), F32)
    for gi, win in enumerate(POOL_WINDOWS):
        wl = jnp.where(lane == gi, float(win), wl)
    return lane, wl


def _pool_select(lane, parts):
    out = parts[-1]
    for gi in range(len(parts) - 2, -1, -1):
        out = jnp.where(lane == gi, parts[gi], out)
    return out


def _pool_mix(u, halo, t0, tt):
    lane, wl = _pool_lane_windows()
    ext = jnp.concatenate([halo, u], axis=0)
    sums, cur, shift = [], ext, 1
    for _ in POOL_WINDOWS:
        cur = cur + pltpu.roll(cur, shift, axis=0)
        sums.append(cur[POOL_HALO:, :])
        shift *= 2
    tpos = (t0 + lax.broadcasted_iota(jnp.int32, (tt, POOL_W), 0)).astype(F32)
    cnt = jnp.minimum(tpos + 1.0, wl)
    return _pool_select(lane, sums) / cnt - u, cnt


def _rows_reduce(x, op, final):
    while x.shape[0] > 8 and x.shape[0] % 16 == 0:
        half = x.shape[0] // 2
        x = op(x[:half], x[half:])
    return final(x, axis=0, keepdims=True)


def _tri_pair(step, n, group=1):
    counts = [a // group + 1 for a in range(n)]
    firsts = [sum(counts[:a]) for a in range(1, n)]
    a = sum([(step >= f).astype(jnp.int32) for f in firsts], jnp.int32(0))
    first = sum([jnp.where(step >= f, c, 0) for f, c in zip(firsts, counts)], jnp.int32(0))
    return a, step - first


def _tri_steps(n, group=1):
    return sum(a // group + 1 for a in range(n))


def _lane_lo():
    return lax.broadcasted_iota(jnp.int32, (1, 2 * HEAD), 1) < HEAD


def _put_col(tile, hh, colv):
    lane = lax.broadcasted_iota(jnp.int32, tile.shape, 1)
    return jnp.where(lane == hh, colv, tile)


def _fox_fwd(proj, kt, c_col, c_row, nb, name, rider=None):
    n = proj.shape[0]
    t = n // nb
    tb = min(256, t)
    nq = t // tb
    pw = 2 * HEAD
    kw = 2 if nq % 2 == 0 else 1
    nk = nq // kw

    def body(*refs):
        q_ref, kt_refs, (v_ref, cc_ref, cr_ref, o_ref, lse_ref, m_s, acc_s, cq_s) = refs[0], refs[1:1 + nb], refs[1 + nb:]
        qi, kk = _tri_pair(pl.program_id(0), nq, kw)

        @pl.when(kk == 0)
        def _():
            m_s[...] = jnp.full_like(m_s, -jnp.inf)
            acc_s[...] = jnp.zeros_like(acc_s)
            for e in range(nb):
                for hh in range(FOX_HEADS):
                    cq_s[e, hh] = jnp.broadcast_to(cc_ref[e, :, hh:hh + 1], (tb, pw))

        def block(masked, sub):
            lo = _lane_lo()
            keys = slice(sub * tb, (sub + 1) * tb)
            if masked:
                causal = lax.broadcasted_iota(jnp.int32, (tb, tb), 0) >= lax.broadcasted_iota(jnp.int32, (tb, tb), 1)
            def lanes(hh):
                return lo if hh % 2 == 0 else jnp.logical_not(lo)

            def scores(hh, e):
                sl = slice((hh // 2) * pw, (hh // 2 + 1) * pw)
                return _dot(jnp.where(lanes(hh), q_ref[e, :, sl] * FOX_SCALE, 0.0).astype(BF16), kt_refs[e][sl, keys])

            order = ([(hh, e) for e in range(nb) for hh in range(FOX_HEADS)] if masked else
                     [(hh, e) for hh in range(FOX_HEADS) for e in range(nb)])
            ahead = scores(*order[0])
            for j, (hh, e) in enumerate(order):
                s = ahead
                if j + 1 < len(order):
                    ahead = scores(*order[j + 1])
                s = s + (jnp.tile(cq_s[e, hh], (1, tb // pw)) - cr_ref[e, hh:hh + 1, keys])
                if masked:
                    s = jnp.where(causal, s, MASK_VALUE)
                m_prev = m_s[e, hh]
                m_new = jnp.maximum(m_prev, jnp.max(s, axis=1, keepdims=True))
                alpha = jnp.exp(m_prev - m_new)
                pe = jnp.exp(s - jnp.tile(m_new, (1, tb // pw)))
                m_s[e, hh] = m_new
                vf = v_ref[e, keys, (hh // 2) * pw:(hh // 2 + 1) * pw]
                acc_s[e, hh] = alpha * acc_s[e, hh] + _dot(pe.astype(BF16), jnp.where(lanes(hh), vf, 1.0).astype(BF16))

        def finish():
            lo = _lane_lo()
            for e in range(nb):
                m_all, l_all = jnp.zeros((tb, FC_PAD), F32), jnp.ones((tb, FC_PAD), F32)
                for p in range(FOX_HEADS // 2):
                    a0, a1 = acc_s[e, 2 * p], acc_s[e, 2 * p + 1]
                    both = pltpu.roll(jnp.where(lo, a1, a0), HEAD, axis=1)
                    o_ref[e, :, p * pw:(p + 1) * pw] = jnp.where(lo, a0, a1) / both
                    m_all = _put_col(_put_col(m_all, 2 * p, m_s[e, 2 * p]), 2 * p + 1, m_s[e, 2 * p + 1])
                    l_all = _put_col(_put_col(l_all, 2 * p, both), 2 * p + 1, a1)
                lse_ref[e] = (m_all + jnp.log(l_all)).T[:FOX_HEADS, :]

        for sub in range(kw):
            @pl.when(kk * kw + sub < qi)
            def _(sub=sub):
                block(False, sub)

            @pl.when(kk * kw + sub == qi)
            def _(sub=sub):
                block(True, sub)
                finish()

    def qspec(wd, j):
        return pl.BlockSpec((nb, tb, wd), lambda st: (0, _tri_pair(st, nq, kw)[0], j))

    def ktspec(e):
        return pl.BlockSpec((FOX_W, kw * tb), lambda st: (0, e * nk + _tri_pair(st, nq, kw)[1]))

    proj3 = proj.reshape(nb, t, proj.shape[1])
    (o, lse), rode = _call(
        body, name, (proj3, *[kt] * nb, proj3, c_col.reshape(nb, t, FC_PAD), c_row), rider, grid=(_tri_steps(nq, kw),),
        in_specs=[qspec(FOX_W, C_QC // FOX_W)] + [ktspec(e) for e in range(nb)] + [
            pl.BlockSpec((nb, kw * tb, FOX_W), lambda st: (0, _tri_pair(st, nq, kw)[1], C_VC // FOX_W)), qspec(FC_PAD, 0),
            pl.BlockSpec((nb, FOX_HEADS, kw * tb), lambda st: (0, 0, _tri_pair(st, nq, kw)[1]))],
        out_specs=[qspec(FOX_W, 0), pl.BlockSpec((nb, FOX_HEADS, tb), lambda st: (0, 0, _tri_pair(st, nq, kw)[0]))],
        out_shape=[_sds((nb, t, FOX_W), F32), _sds((nb, FOX_HEADS, t), F32)],
        scratch_shapes=[pltpu.VMEM((nb, FOX_HEADS, tb, pw), F32)] * 3,
        compiler_params=_params(),
    )
    return (o.reshape(n, FOX_W), lse), rode


def _head_mean(x, ones_f):
    return _dot_sel(x, ones_f) * (1.0 / HEAD)


def _merge_fwd(x, proj, o_h, o_c, gh, wbd, scale, w_out, g_post, nb, name, target=None):
    n, d = x.shape
    t = n // nb
    tm = min(512, t)
    nt = t // tm
    nhb = tm // POOL_HALO

    def body(*refs):
        x_ref, ga_ref, gc_ref, oh_ref, u_ref, gb_ref, h_ref, oc_ref, gh_ref, wb_ref, s_ref, w_ref, gp_ref = refs[:13]
        if target is None:
            xo_ref, mixt_ref, y_ref = refs[13:]
        else:
            t_ref, dx_ref, sq_ref, mixt_ref, y_ref = refs[13:]
        oh = oh_ref[...]
        ones_f = _block_ones(HGRN_W, F32)
        na = oh * lax.rsqrt(_head_mean(oh * oh, ones_f) + NORM_EPS) * gh_ref[...]
        ga, gb, gc = ga_ref[...], gb_ref[...], gc_ref[...]
        ti = pl.program_id(0) % nt
        pooled, _ = _pool_mix(u_ref[...], jnp.where(ti == 0, 0.0, h_ref[...]), ti * tm, tm)
        ob = _dot(pooled.astype(BF16), wb_ref[...]) * s_ref[...] * (gb * _sigmoid(gb))
        mixed = jnp.concatenate([na * (ga * _sigmoid(ga)), ob, oc_ref[...] * (gc * _sigmoid(gc))], axis=1)
        mixt_ref[...] = mixed.T.astype(BF16)
        y = _dot(mixed.astype(BF16), w_ref[...])
        y_ref[...] = y
        xn = x_ref[...] + y * lax.rsqrt(jnp.mean(y * y, axis=-1, keepdims=True) + NORM_EPS) * gp_ref[...]
        if target is None:
            xo_ref[...] = xn
        else:
            @pl.when(pl.program_id(0) == 0)
            def _():
                sq_ref[...] = jnp.zeros_like(sq_ref)

            e = xn - t_ref[...]
            dx_ref[...] = e * (1.0 / d)
            sq_ref[...] += jnp.sum(e * e, axis=0, keepdims=True)

    def row(wd, j=0):
        return pl.BlockSpec((tm, wd), lambda i: (i, j))

    def full(a, b):
        return pl.BlockSpec((a, b), lambda i: (0, 0))

    head = [] if target is None else [target]
    res = _pc(
        body, name, grid=(n // tm,),
        in_specs=[row(d), row(HGRN_W, C_GA // HGRN_W), row(FOX_W, C_GC // FOX_W), row(HGRN_W),
                  row(POOL_W, C_UB // POOL_W), row(POOL_W, C_GB // POOL_W),
                  pl.BlockSpec((POOL_HALO, POOL_W), lambda i: (jnp.maximum(i * nhb - 1, 0), C_UB // POOL_W)), row(FOX_W),
                  full(1, HGRN_W), full(POOL_W, POOL_W), full(1, POOL_W), full(d, d), full(1, d)] + [row(d)] * len(head),
        out_specs=[row(d)] + [full(1, d)] * len(head) + [pl.BlockSpec((d, tm), lambda i: (0, i)), row(d)],
        out_shape=[_sds((n, d), F32)] + [_sds((1, d), F32)] * len(head) + [_sds((d, n), BF16), _sds((n, d), F32)],
        compiler_params=_params(),
    )(x, proj, proj, o_h, proj, proj, proj, o_c, gh, wbd, scale, w_out, g_post, *head)
    return (res[0], res[1], res[2]) if target is None else ((res[0], res[1]), res[2], res[3])


def _rms_bwd(dy_scaled, xhat, r):
    return r * (dy_scaled - xhat * jnp.mean(dy_scaled * xhat, axis=-1, keepdims=True))


def _merge_bwd(dxo, y, g_post, w_out_t, proj, o_c, mixt, nb, name):
    n, d = y.shape
    t = n // nb
    tm = min(512, t)
    nt = t // tm
    wab = HGRN_W + POOL_W
    rows = d // N_DEV

    def body(dx_ref, y_ref, gp_ref, wt_ref, gc_ref, oc_ref, mt_ref,
             gw_ref, dm_ref, dgp_ref, da_ref, dat_ref, dg_ref, dl_ref, gw_s):
        @pl.when(pl.program_id(0) == 0)
        def _():
            dgp_ref[...] = jnp.zeros_like(dgp_ref)
            gw_s[...] = jnp.zeros_like(gw_s)

        yv, dxv = y_ref[...], dx_ref[...]
        r = lax.rsqrt(jnp.mean(yv * yv, axis=-1, keepdims=True) + NORM_EPS)
        yh = yv * r
        dgp_ref[...] += jnp.sum(dxv * yh, axis=0, keepdims=True)
        dyb = _rms_bwd(dxv * gp_ref[...], yh, r).astype(BF16)
        gw_s[...] += _dot(mt_ref[...], dyb)

        @pl.when(pl.program_id(0) == n // tm - 1)
        def _():
            for j in range(N_DEV):
                gw_ref[j % 2, j // 2] = gw_s[j * rows:(j + 1) * rows, :].astype(BF16)

        dm_ref[...] = _dot(dyb, wt_ref[:, :wab])
        dmc = _dot(dyb, wt_ref[:, wab:])
        gc, oc = gc_ref[...], oc_ref[...]
        sg = _sigmoid(gc)
        da = dmc * (gc * sg)
        da_ref[...] = da.astype(BF16)
        dat_ref[...] = da.T.astype(BF16)
        dg_ref[...] = (dmc * oc * (sg * (1.0 + gc * (1.0 - sg)))).astype(BF16)
        rr = lax.broadcasted_iota(jnp.int32, (FOX_W, FC_PAD), 0) // HEAD
        cc = lax.broadcasted_iota(jnp.int32, (FOX_W, FC_PAD), 1)
        dl_ref[...] = _dot_sel(da * oc, (rr == cc).astype(F32)).T[:FOX_HEADS, :]

    def row(wd, j=0):
        return pl.BlockSpec((tm, wd), lambda i: (i, j))

    def full(a, b):
        return pl.BlockSpec((a, b), lambda i: (0, 0))

    return _pc(
        body, name, grid=(n // tm,),
        in_specs=[row(d), row(d), full(1, d), full(d, d), row(FOX_W, C_GC // FOX_W), row(FOX_W),
                  pl.BlockSpec((d, tm), lambda i: (0, i))],
        out_specs=[pl.BlockSpec((2, N_DEV // 2, rows, d), lambda i: (0, 0, 0, 0)), row(wab), full(1, d), row(FOX_W),
                   pl.BlockSpec((FOX_W, tm), lambda i: (0, i)), row(FOX_W),
                   pl.BlockSpec((None, FOX_HEADS, tm), lambda i: (i // nt, 0, i % nt))],
        out_shape=[_sds((2, N_DEV // 2, rows, d), BF16), _sds((n, wab), F32), _sds((1, d), F32), _sds((n, FOX_W), BF16),
                   _sds((FOX_W, n), BF16), _sds((n, FOX_W), BF16), _sds((nb, FOX_HEADS, t), F32)],
        scratch_shapes=[pltpu.VMEM((d, d), F32)],
        compiler_params=_params(),
    )(dxo, y, g_post, w_out_t, proj, o_c, mixt)


def _w_in_grad(ht, pieces, dmix, proj, wbd, wbd_t, scale, nb, name):
    d, n = ht.shape
    t = n // nb
    ta, tk = d, min(512, t)
    nk, nt, nhb = n // tk, t // tk, tk // POOL_HALO
    given = [p for p, _ in pieces if p is not None]
    widths = [2 * POOL_W if p is None else p.shape[1] for p, _ in pieces]
    offs = [sum(widths[:i]) for i in range(len(widths))]
    in_w = MAIN_W + FOX_HEADS
    shard = in_w // N_DEV
    cu, cg, cm = C_UB // POOL_W, C_GB // POOL_W, HGRN_W // POOL_W

    def body(*refs):
        a_ref, p_refs = refs[0], list(refs[1:1 + len(given)])
        pool_refs = refs[1 + len(given):10 + len(given)]
        o_ref, db_ref, dw_ref, ds_ref, acc = refs[10 + len(given):]
        k = pl.program_id(1)

        @pl.when(k == 0)
        def _():
            dw_ref[...] = jnp.zeros_like(dw_ref)
            ds_ref[...] = jnp.zeros_like(ds_ref)
            acc[...] = jnp.zeros_like(acc)

        db, dw, dscale = _pool_bwd_tile(*pool_refs, k % nt, nt, tk)
        dw_ref[...] += dw
        ds_ref[...] += dscale
        db_ref[...] = db
        a = a_ref[...]
        for (p, _), off, wd in zip(pieces, offs, widths):
            pr = db if p is None else p_refs.pop(0)
            for j in range(0, wd, 512):
                jw = min(512, wd - j)
                acc[:, off + j:off + j + jw] += _dot(a, pr[:, j:j + jw])

        @pl.when(k == nk - 1)
        def _():
            for j in range(N_DEV):
                o_ref[j % 2, j // 2] = acc[:, j * shard:(j + 1) * shard].astype(BF16)

    def after(k):
        return jnp.minimum((k + 1) * nhb, n // POOL_HALO - 1)

    def tile(j):
        return pl.BlockSpec((tk, POOL_W), lambda i, k: (k, j))

    def full(r, c):
        return pl.BlockSpec((r, c), lambda i, k: (0, 0))

    pool_specs = [tile(cu), tile(cg),
                  pl.BlockSpec((POOL_HALO, POOL_W), lambda i, k: (jnp.maximum(k * nhb - 1, 0), cu)), tile(cm),
                  pl.BlockSpec((POOL_HALO, POOL_W), lambda i, k: (after(k), cg)),
                  pl.BlockSpec((POOL_HALO, POOL_W), lambda i, k: (after(k), cm)),
                  full(POOL_W, POOL_W), full(POOL_W, POOL_W), full(1, POOL_W)]
    return _pc(
        body, name, grid=(d // ta, nk),
        in_specs=[pl.BlockSpec((ta, tk), lambda i, k: (i, k))]
        + [pl.BlockSpec((tk, p.shape[1]), lambda i, k: (k, 0)) for p in given] + pool_specs,
        out_specs=[pl.BlockSpec((2, N_DEV // 2, ta, shard), lambda i, k: (0, 0, i, 0)),
                   pl.BlockSpec((tk, 2 * POOL_W), lambda i, k: (k, 0)), full(POOL_W, POOL_W), full(1, POOL_W)],
        out_shape=[_sds((2, N_DEV // 2, d, shard), BF16), _sds((n, 2 * POOL_W), BF16), _sds((POOL_W, POOL_W), F32),
                   _sds((1, POOL_W), F32)],
        scratch_shapes=[pltpu.VMEM((ta, sum(widths)), F32)],
        compiler_params=_params(),
    )(ht, *given, proj, proj, proj, dmix, proj, dmix, wbd, wbd_t, scale)


def _hgrn_state_bwd(qs, k, v, b, do, s0, ds1, bmask):
    bl = b[SUB - 1:SUB, :]
    eb, ebl, ekt = jnp.exp(b), jnp.exp(bl), jnp.exp(bl - b)
    qe, ktil = qs * eb, k * ekt
    ds1b, dob = ds1.astype(BF16), do.astype(BF16)
    dv = _dot_nt(ktil.astype(BF16), ds1b)
    dqe = _dot(dob, s0.astype(BF16))
    dktil = _dot(v.astype(BF16), ds1b)
    dbl = jnp.sum(dktil * ktil, axis=0, keepdims=True) + ebl * jnp.sum(s0 * ds1, axis=0, keepdims=True)
    ds0 = ds1 * ebl + _dot_tn(dob, qe.astype(BF16)) * bmask
    return dqe * eb, dktil * ekt, dv, dbl, ds0


def _hgrn_intra_bwd(qs, k, v, do, es, aexp, gexp, dq, dk, dv, put_dq_row):
    dks = [dk[j:j + 8] for j in range(0, SUB, 8)]
    dvs = [dv[j:j + 8] for j in range(0, SUB, 8)]
    for t in range(SUB):
        r = _live_rows(t)
        ge = gexp[t * SUB:t * SUB + r, :] * es[t]
        put_dq_row(t, dq[t:t + 1, :] + jnp.sum(ge * k[:r], axis=0, keepdims=True))
        for j in range(r // 8):
            dks[j] = dks[j] + ge[8 * j:8 * j + 8] * qs[t:t + 1, :]
            dvs[j] = dvs[j] + aexp[t * SUB + 8 * j:t * SUB + 8 * j + 8, :] * do[t:t + 1, :]
    return jnp.concatenate(dks, axis=0), jnp.concatenate(dvs, axis=0)


def _hgrn_bwd(dmix, proj, o_h, s0, gh, lb, ones_b, nb, name, rider=None):
    n = proj.shape[0]
    t = n // nb
    tt = _hgrn_tile(t)
    nt = t // tt
    ncs = tt // CHUNK
    nsub = CHUNK // SUB
    w = HGRN_W

    def body(dm_ref, q_ref, z_ref, v_ref, ga_ref, oh_ref, s0_ref, gh_ref, lb_ref, ones_ref,
             dp_ref, dgh_ref, dlb_ref, ds_s, ss_s, b_s, qs_s, k_s, do_s, dq_s, dk_s, dv_s, dbl_s):
        @pl.when(pl.program_id(0) == 0)
        def _():
            dgh_ref[...] = jnp.zeros_like(dgh_ref)
            dlb_ref[...] = jnp.zeros_like(dlb_ref)
            ds_s[...] = jnp.zeros_like(ds_s)

        ones_b = ones_ref[...]
        ones_f = ones_b.astype(F32)
        bmask = _block_ones(w, F32)
        lbv, ghv = lb_ref[...], gh_ref[...]
        kept = []
        for e in range(nb):
            oh, ga, dm = oh_ref[e], ga_ref[e], dm_ref[e]
            rn = lax.rsqrt(_head_mean(oh * oh, ones_f) + NORM_EPS)
            nh = oh * rn
            sga = _sigmoid(ga)
            dp_ref[e, :, 3 * w:4 * w] = (dm * nh * ghv * (sga * (1.0 + ga * (1.0 - sga)))).astype(BF16)
            dn = dm * (ga * sga)
            dgh_ref[...] += jnp.sum(dn * nh, axis=0, keepdims=True)
            dn = dn * ghv
            do_s[e] = rn * (dn - nh * _head_mean(dn * nh, ones_f))
            q = q_ref[e]
            sig, sn, f, g, k, sq = _hgrn_gates(q, z_ref[e], lbv)
            qs = q * sq
            b_s[e] = _sel_dot(_sub_tri(tt, True), g)
            qs_s[e] = qs
            k_s[e] = k
            kept.append((q, sig, sn, f, k, sq, qs))

        def chunk(cc, carry):
            c = ncs - 1 - cc
            base = pl.multiple_of(c * CHUNK, CHUNK)
            tiles = [[(qs_s[e, pl.ds(base + u * SUB, SUB), :], k_s[e, pl.ds(base + u * SUB, SUB), :],
                       v_ref[e, pl.ds(base + u * SUB, SUB), :], b_s[e, pl.ds(base + u * SUB, SUB), :],
                       do_s[e, pl.ds(base + u * SUB, SUB), :]) for u in range(nsub)] for e in range(nb)]
            sts = [s0_ref[e, c] for e in range(nb)]
            for u in range(nsub):
                for e in range(nb):
                    qs, k, v, b, do = tiles[e][u]
                    ss_s[e, u] = sts[e]
                    if u < nsub - 1:
                        sts[e] = _hgrn_state_step(sts[e], k, v, b, bmask)
            dss = [ds_s[e] for e in range(nb)]
            for u in reversed(range(nsub)):
                for e in range(nb):
                    qs, k, v, b, do = tiles[e][u]
                    _, es, ws = _hgrn_decays(qs, k, b)
                    gs = [_pad_rows(do[t:t + 1, :] * v[:_live_rows(t)]) for t in range(SUB)]
                    aexp = _dot(jnp.concatenate(ws, axis=0).astype(BF16), ones_b)
                    gexp = _dot(jnp.concatenate(gs, axis=0).astype(BF16), ones_b)
                    dq, dk, dv, dbl, dss[e] = _hgrn_state_bwd(qs, k, v, b, do, ss_s[e, u], dss[e], bmask)

                    def put_dq_row(i, row, e=e, r0=base + u * SUB):
                        dq_s[e, pl.ds(r0 + i, 1), :] = row

                    dk, dv = _hgrn_intra_bwd(qs, k, v, do, es, aexp, gexp, dq, dk, dv, put_dq_row)
                    dk_s[e, pl.ds(base + u * SUB, SUB), :] = dk
                    dv_s[e, pl.ds(base + u * SUB, SUB), :] = dv
                    dbl_s[e, pl.ds(base + u * SUB, SUB), :] = jnp.broadcast_to(dbl, (SUB, w))
            for e in range(nb):
                ds_s[e] = dss[e]
            return carry

        lax.fori_loop(0, ncs, chunk, 0)
        for e, (q, sig, sn, f, k, sq, qs) in enumerate(kept):
            dqs, dk = dq_s[e], dk_s[e]
            dg = _sel_dot(_sub_tri(tt, False), qs * dqs - k * dk) + dbl_s[e]
            dfz = jnp.where(f > TINY, dg / jnp.maximum(f, TINY), 0.0)
            dlb_ref[...] += jnp.sum(dfz * (1.0 - sig) - dk * sn, axis=0, keepdims=True)
            dp_ref[e, :, 0:w] = (dqs * (sq * (1.0 + q * (1.0 - sq)))).astype(BF16)
            dp_ref[e, :, w:2 * w] = ((dfz - dk) * (1.0 - lbv) * sig * sn).astype(BF16)
            dp_ref[e, :, 2 * w:3 * w] = dv_s[e].astype(BF16)

    def col(j):
        return pl.BlockSpec((nb, tt, w), lambda i: (0, nt - 1 - i, j))

    def full(a, bb):
        return pl.BlockSpec((a, bb), lambda i: (0, 0))

    proj3 = proj.reshape(nb, t, proj.shape[1])
    (dp, dgh, dlb), rode = _call(
        body, name, (dmix.reshape(nb, t, dmix.shape[1]), proj3, proj3, proj3, proj3, o_h.reshape(nb, t, w),
                     s0.reshape(nb, t // CHUNK, w, w), gh, lb, ones_b), rider, grid=(nt,),
        in_specs=[col(0), col(C_QA // w), col(C_FA // w), col(C_IA // w), col(C_GA // w), col(0),
                  pl.BlockSpec((nb, ncs, w, w), lambda i: (0, nt - 1 - i, 0, 0)), full(1, w), full(1, w), full(w, w)],
        out_specs=[pl.BlockSpec((nb, tt, 4 * w), lambda i: (0, nt - 1 - i, 0)), full(1, w), full(1, w)],
        out_shape=[_sds((nb, t, 4 * w), BF16), _sds((1, w), F32), _sds((1, w), F32)],
        scratch_shapes=[pltpu.VMEM((nb, w, w), F32), pltpu.VMEM((nb, nsub, w, w), F32)]
        + [pltpu.VMEM((nb, tt, w), F32)] * 8,
        compiler_params=_params(),
    )
    return (dp.reshape(n, 4 * w), dgh, dlb), rode


def _pool_bwd_tile(u_ref, g_ref, h_ref, dm_ref, gn_ref, dmn_ref, w_ref, wt_ref, s_ref, i, nt, tt):
    sc = s_ref[...]
    halo = jnp.where(i == 0, 0.0, h_ref[...])
    pooled, cnt = _pool_mix(u_ref[...], halo, i * tt, tt)
    pb = pooled.astype(BF16)
    pre = _dot(pb, w_ref[...])
    gv, dm = g_ref[...], dm_ref[...]
    sg = _sigmoid(gv)
    silu = gv * sg
    dgb = dm * pre * sc * (sg * (1.0 + gv * (1.0 - sg)))
    dscale = jnp.sum(dm * pre * silu, axis=0, keepdims=True)
    dpre = (dm * sc * silu).astype(BF16)
    dw = _dot_tn(pb, dpre)
    dpool = _dot(dpre, wt_ref[...])
    gn = gn_ref[...]
    dpre_n = (dmn_ref[...] * sc * (gn * _sigmoid(gn))).astype(BF16)
    dpool_n = jnp.where(i == nt - 1, 0.0, _dot(dpre_n, wt_ref[...]))
    lane, wl = _pool_lane_windows()
    tpos_n = ((i + 1) * tt + lax.broadcasted_iota(jnp.int32, (POOL_HALO, POOL_W), 0)).astype(F32)
    ext = jnp.concatenate([dpool / cnt, dpool_n / jnp.minimum(tpos_n + 1.0, wl)], axis=0)
    rows = tt + POOL_HALO
    sums, cur, shift = [], ext, 1
    for _ in POOL_WINDOWS:
        cur = cur + pltpu.roll(cur, rows - shift, axis=0)
        sums.append(cur[:tt, :])
        shift *= 2
    du = _pool_select(lane, sums) - dpool
    return jnp.concatenate([du, dgb], axis=1).astype(BF16), dw, dscale


def _fox_bwd(proj, qt, kt, da, dat, c_col, c_row, lse_row, delta_row, nb, name, rider=None):
    n = proj.shape[0]
    t = n // nb
    tb = min(256, t)
    nq = t // tb
    pw = 2 * HEAD
    qw = 2 if nq % 2 == 0 else 1
    nqs = nq // qw

    def body(q_ref, k_ref, v_ref, da_ref, qt_ref, kt_ref, dat_ref, cc_ref, cr_ref, lse_ref, dl_ref,
             dq_ref, dk_ref, dv_ref, dck_ref, dcq_ref, dq_s, dk_s, dv_s, dck_s, dcq_s):
        step = pl.program_id(1)
        kj, qq = pairs(step)

        @pl.when(step == 0)
        def _():
            dq_s[...] = jnp.zeros_like(dq_s)
            dcq_s[...] = jnp.zeros_like(dcq_s)

        @pl.when(qq == nqs - 1)
        def _():
            dk_s[...] = jnp.zeros_like(dk_s)
            dv_s[...] = jnp.zeros_like(dv_s)
            dck_s[...] = jnp.zeros_like(dck_s)

        def block(masked, sub):
            lo = _lane_lo()
            qi = qq * qw + sub
            qs = slice(sub * tb, (sub + 1) * tb)
            if masked:
                causal = lax.broadcasted_iota(jnp.int32, (tb, tb), 1) >= lax.broadcasted_iota(jnp.int32, (tb, tb), 0)
            dck = dck_s[...]
            for p in range(FOX_HEADS // 2):
                sl = slice(p * pw, (p + 1) * pw)
                qp = q_ref[qs, sl] * FOX_SCALE
                kp = k_ref[:, sl].astype(BF16)
                vp = v_ref[:, sl].astype(BF16)
                dap = da_ref[qs, sl]
                dk, dv = dk_s[:, sl], dv_s[:, sl]
                for h in range(2):
                    hh = 2 * p + h
                    lm = lo if h == 0 else jnp.logical_not(lo)
                    rows = slice(hh * HEAD, (hh + 1) * HEAD)
                    none = jnp.zeros((HEAD, tb), BF16)
                    qm = jnp.where(lm, qp, 0.0).astype(BF16)
                    dam = jnp.where(lm, dap, jnp.zeros_like(dap))
                    qtm = jnp.concatenate([qt_ref[rows, qs], none] if h == 0 else [none, qt_ref[rows, qs]], axis=0)
                    datm = jnp.concatenate([dat_ref[rows, qs], none] if h == 0 else [none, dat_ref[rows, qs]], axis=0)
                    s = _dot(kp, qtm) + (cr_ref[hh:hh + 1, qs] - cc_ref[:, hh:hh + 1])
                    pe = jnp.exp(s - lse_ref[hh:hh + 1, qs])
                    if masked:
                        pe = jnp.where(causal, pe, 0.0)
                    dp = _dot(vp, datm)
                    ds = pe * (dp - dl_ref[hh:hh + 1, qs])
                    dsb = ds.astype(BF16)
                    dv = dv + _dot(pe.astype(BF16), dam)
                    dk = dk + _dot(dsb, qm)
                    dq_s[qi, rows, :] += _dot(kt_ref[rows, :], dsb)
                    dck = dck - _put_col(jnp.zeros_like(dck), hh, jnp.sum(ds, axis=1, keepdims=True))
                    dcq_s[qi, hh:hh + 1, :] += _rows_reduce(ds, jnp.add, jnp.sum)
                dk_s[:, sl] = dk
                dv_s[:, sl] = dv
            dck_s[...] = dck

        for sub in reversed(range(qw)):
            @pl.when(qq * qw + sub > kj)
            def _(sub=sub):
                block(False, sub)

            @pl.when(qq * qw + sub == kj)
            def _(sub=sub):
                block(True, sub)

        @pl.when(qq == kj // qw)
        def _():
            dk_ref[...] = dk_s[...].astype(BF16)
            dv_ref[...] = dv_s[...].astype(BF16)
            dck_ref[...] = dck_s[...]

        @pl.when(step == _tri_steps(nq, qw) - 1)
        def _():
            for j in range(nq):
                dq_ref[j * tb:(j + 1) * tb, :] = (dq_s[j].T * FOX_SCALE).astype(BF16)
                dcq_ref[:, j * tb:(j + 1) * tb] = dcq_s[j]

    def pairs(step):
        a, b = _tri_pair(step, nq, qw)
        return nq - 1 - a, nqs - 1 - b

    def kspec(wd, j=0):
        return pl.BlockSpec((tb, wd), lambda b, st: (b * nq + pairs(st)[0], j))

    def qspec(wd, j=0):
        return pl.BlockSpec((qw * tb, wd), lambda b, st: (b * nqs + pairs(st)[1], j))

    def qrow():
        return pl.BlockSpec((None, FOX_HEADS, qw * tb), lambda b, st: (b, 0, pairs(st)[1]))

    def tspec(which):
        if which == 0:
            return pl.BlockSpec((FOX_W, tb), lambda b, st: (0, b * nq + pairs(st)[0]))
        return pl.BlockSpec((FOX_W, qw * tb), lambda b, st: (0, b * nqs + pairs(st)[1]))

    return _call(
        body, name, (proj, proj, proj, da, qt, kt, dat, c_col, c_row, lse_row, delta_row), rider,
        grid=(nb, _tri_steps(nq, qw)),
        in_specs=[qspec(FOX_W, C_QC // FOX_W), kspec(FOX_W, C_KC // FOX_W), kspec(FOX_W, C_VC // FOX_W), qspec(FOX_W),
                  tspec(1), tspec(0), tspec(1), kspec(FC_PAD), qrow(), qrow(), qrow()],
        out_specs=[pl.BlockSpec((t, FOX_W), lambda b, st: (b, 0)), kspec(FOX_W), kspec(FOX_W), kspec(FC_PAD),
                   pl.BlockSpec((None, FOX_HEADS, t), lambda b, st: (b, 0, 0))],
        out_shape=[_sds((n, FOX_W), BF16), _sds((n, FOX_W), BF16), _sds((n, FOX_W), BF16), _sds((n, FC_PAD), F32),
                   _sds((nb, FOX_HEADS, t), F32)],
        scratch_shapes=[pltpu.VMEM((nq, FOX_W, tb), F32), pltpu.VMEM((tb, FOX_W), F32), pltpu.VMEM((tb, FOX_W), F32),
                        pltpu.VMEM((tb, FC_PAD), F32), pltpu.VMEM((nq, FOX_HEADS, tb), F32)],
        compiler_params=_params(),
    )


def _fox_decay_bwd(dc_q, dc_k, fc, bias, nb, name):
    n = fc.shape[0]
    t = n // nb
    tt = min(256, t)
    nt = t // tt

    def body(dcq_ref, dck_ref, fc_ref, b_ref, dfc_ref, db_ref):
        @pl.when(pl.program_id(0) == 0)
        def _():
            db_ref[...] = jnp.zeros_like(db_ref)

        r = lax.broadcasted_iota(jnp.int32, (tt, tt), 0)
        cc = lax.broadcasted_iota(jnp.int32, (tt, tt), 1)
        carry = jnp.zeros((1, FC_PAD), F32)
        db = jnp.zeros((1, FC_PAD), F32)
        for i in reversed(range(nt)):
            rows = slice(i * tt, (i + 1) * tt)
            dcq = jnp.concatenate([dcq_ref[:, rows], jnp.zeros((FC_PAD - FOX_HEADS, tt), F32)], axis=0).T
            dlf = _sel_dot_exact(r <= cc, dcq + dck_ref[rows, :]) + carry
            carry = dlf[0:1, :]
            dfc = dlf * _sigmoid(-(fc_ref[rows, :] + b_ref[...]))
            dfc_ref[rows, :] = dfc.astype(BF16)
            db = db + jnp.sum(dfc, axis=0, keepdims=True)
        db_ref[...] += db

    def row():
        return pl.BlockSpec((t, FC_PAD), lambda b: (b, 0))

    return _pc(
        body, name, grid=(nb,),
        in_specs=[pl.BlockSpec((None, FOX_HEADS, t), lambda b: (b, 0, 0)), row(), row(),
                  pl.BlockSpec((1, FC_PAD), lambda b: (0, 0))],
        out_specs=[row(), pl.BlockSpec((1, FC_PAD), lambda b: (0, 0))],
        out_shape=[_sds((n, FC_PAD), BF16), _sds((1, FC_PAD), F32)],
        compiler_params=_params(),
    )(dc_q, dc_k, fc, bias)


def _in_proj_bwd(pieces, w_main_t, w_fc_t, x, g_pre, dxo, name, rider=None):
    n, d = x.shape
    tm = min(512, n)
    widths = [p.shape[1] for p, _ in pieces]
    offs = [o for _, o in pieces]
    np_ = len(pieces)

    def body(*refs):
        p_refs = refs[:np_]
        wt_ref, wf_ref, x_ref, g_ref, dxo_ref, dx_ref, dg_ref = refs[np_:]

        @pl.when(pl.program_id(0) == 0)
        def _():
            dg_ref[...] = jnp.zeros_like(dg_ref)

        dh = _dot(p_refs[-1][...], wf_ref[...])
        for pr, wd, off in zip(p_refs[:-1], widths[:-1], offs[:-1]):
            for j in range(0, wd, 512):
                jw = min(512, wd - j)
                dh = dh + _dot(pr[:, j:j + jw], wt_ref[off + j:off + j + jw, :])
        xv = x_ref[...]
        r = lax.rsqrt(jnp.mean(xv * xv, axis=-1, keepdims=True) + NORM_EPS)
        xh = xv * r
        dg_ref[...] += jnp.sum(dh * xh, axis=0, keepdims=True)
        dx_ref[...] = dxo_ref[...] + _rms_bwd(dh * g_ref[...], xh, r)

    row = pl.BlockSpec((tm, d), lambda i: (i, 0))
    return _call(
        body, name, (*[p for p, _ in pieces], w_main_t, w_fc_t, x, g_pre, dxo), rider, grid=(n // tm,),
        in_specs=[pl.BlockSpec((tm, wd), lambda i: (i, 0)) for wd in widths] + [
            pl.BlockSpec((MAIN_W, d), lambda i: (0, 0)), pl.BlockSpec((FC_PAD, d), lambda i: (0, 0)),
            row, pl.BlockSpec((1, d), lambda i: (0, 0)), row],
        out_specs=[row, pl.BlockSpec((1, d), lambda i: (0, 0))],
        out_shape=[_sds((n, d), F32), _sds((1, d), F32)],
        compiler_params=_params(),
    )


def _lower_bound_table(lower_bounds, name):
    depth, w = lower_bounds.shape

    def body(lb_ref, o_ref):
        v = lb_ref[...]
        e = jnp.exp(v - jnp.max(v, axis=0, keepdims=True))
        p = e / jnp.sum(e, axis=0, keepdims=True)
        acc = jnp.zeros((1, w), F32)
        for l in range(depth):
            acc = acc + p[l:l + 1, :]
            o_ref[l:l + 1, :] = acc - p[0:1, :]

    return _pc(body, name, out_shape=_sds((depth, w), F32))(lower_bounds)


def _lower_bound_bwd(lower_bounds, dlbs, name):
    depth, w = lower_bounds.shape

    def body(lb_ref, d_ref, o_ref):
        v, dl = lb_ref[...], d_ref[...]
        e = jnp.exp(v - jnp.max(v, axis=0, keepdims=True))
        p = e / jnp.sum(e, axis=0, keepdims=True)
        tot = jnp.sum(dl, axis=0, keepdims=True)
        rows, tail = [], tot
        for l in range(depth):
            rows.append(tail - tot if l == 0 else tail)
            tail = tail - dl[l:l + 1, :]
        dp = jnp.concatenate(rows, axis=0)
        o_ref[...] = p * (dp - jnp.sum(p * dp, axis=0, keepdims=True))

    return _pc(body, name, out_shape=_sds((depth, w), F32))(lower_bounds, dlbs)


def _place():
    x, y, c = lax.axis_index("x"), lax.axis_index("y"), lax.axis_index("c")
    return x, y, c


def _gather_weights(*arrays):
    na = len(arrays)

    def body(*refs):
        ins, outs = refs[:na], refs[na:2 * na]
        send_sems, recv_sems, local_sems = refs[2 * na:]
        x, y, c = _place()
        me, sibling = (x, y, c), (x, y, 1 - c)
        chips = [(1 - x, y), (x, 1 - y), (1 - x, 1 - y)]

        def slot(a, px, py, pc):
            return outs[a].at[4 * px + 2 * py + pc]

        def copy(a, k, block, to, own=False):
            return pltpu.make_async_remote_copy(
                src_ref=ins[a] if own else slot(a, *block), dst_ref=slot(a, *block),
                send_sem=send_sems.at[a * 7 + k], recv_sem=recv_sems.at[a * 7 + k],
                device_id=to, device_id_type=MESH)

        mine = [pltpu.make_async_copy(ins[a], slot(a, *me), local_sems.at[a]) for a in range(na)]
        for cp in mine:
            cp.start()
        first = []
        for a in range(na):
            first.append(copy(a, 0, me, sibling, own=True))
            first += [copy(a, 1 + j, me, (*chip, c), own=True) for j, chip in enumerate(chips)]
        for cp in first:
            cp.start()
        passed = []
        for j, chip in enumerate(chips):
            for a in range(na):
                copy(a, 1 + j, (*chip, c), me).wait_recv()
                fw = copy(a, 4 + j, (*chip, c), sibling)
                fw.start()
                passed.append(fw)
        for a in range(na):
            copy(a, 0, sibling, me).wait_recv()
            for j, chip in enumerate(chips):
                copy(a, 4 + j, (*chip, 1 - c), me).wait_recv()
        for cp in first + passed:
            cp.wait_send()
        for cp in mine:
            cp.wait()

    any_spec = pl.BlockSpec(memory_space=pl.ANY)
    return _pc(
        body, "gather_weights",
        in_specs=[any_spec] * na, out_specs=[any_spec] * na,
        out_shape=[_sds((N_DEV,) + a.shape, a.dtype) for a in arrays],
        scratch_shapes=[pltpu.SemaphoreType.DMA((7 * na,)), pltpu.SemaphoreType.DMA((7 * na,)),
                        pltpu.SemaphoreType.DMA((na,))],
    )(*arrays)


def _peer(k):
    x, y, c = _place()
    return (1 - x if k & 4 else x, 1 - y if k & 2 else y, 1 - c if k & 1 else c)


def _remote(src, dst, sems, s, to):
    return pltpu.make_async_remote_copy(src_ref=src, dst_ref=dst, send_sem=sems[0].at[s], recv_sem=sems[1].at[s],
                                        device_id=to, device_id_type=MESH)


def _gather_rider(shards):
    na = len(shards)

    def plan(ins, outs, *sems):
        x, y, c = _place()
        me = 4 * x + 2 * y + c
        locs = [pltpu.make_async_copy(ins[a], outs[a].at[me], sems[2].at[a]) for a in range(na)]
        sends, recvs = [], []
        for k in range(1, N_DEV):
            px, py, pc = _peer(k)
            for a in range(na):
                s = (k - 1) * na + a
                sends.append(_remote(ins[a], outs[a].at[me], sems, s, (px, py, pc)))
                recvs.append(_remote(ins[a], outs[a].at[4 * px + 2 * py + pc], sems, s, (px, py, pc)))
        return sends, recvs, locs

    return _Rider(shards, [_sds((N_DEV,) + a.shape, a.dtype) for a in shards], (N_DEV - 1) * na, na, plan)


def _direct_exchange_rider(blocks):
    na = len(blocks)

    def plan(ins, outs, *sems):
        x, y, c = _place()
        me = 4 * x + 2 * y + c
        locs = [pltpu.make_async_copy(ins[a].at[c, 2 * x + y], outs[a].at[me], sems[2].at[a]) for a in range(na)]
        sends, recvs = [], []
        for k in range(1, N_DEV):
            px, py, pc = _peer(k)
            for a in range(na):
                s = (k - 1) * na + a
                sends.append(_remote(ins[a].at[pc, 2 * px + py], outs[a].at[me], sems, s, (px, py, pc)))
                recvs.append(_remote(ins[a].at[pc, 2 * px + py], outs[a].at[4 * px + 2 * py + pc], sems, s, (px, py, pc)))
        return sends, recvs, locs

    return _Rider(blocks, [_sds((N_DEV,) + a.shape[2:], a.dtype) for a in blocks], (N_DEV - 1) * na, na, plan)


def _swap_rider(halves):
    na = len(halves)

    def plan(ins, outs, *sems):
        x, y, c = _place()
        cps = [_remote(ins[a].at[1 - c], outs[a], sems, a, (x, y, 1 - c)) for a in range(na)]
        return cps, cps, []

    return _Rider(halves, [_sds(a.shape[1:], a.dtype) for a in halves], na, 1, plan)


def _chip_exchange_rider(parts, small=None):
    na = len(parts)
    n_chip = N_DEV // 2

    def plan(ins, outs, *sems):
        x, y, c = _place()
        chip = 2 * x + y
        locs = [pltpu.make_async_copy(ins[a].at[chip], outs[a].at[chip], sems[2].at[a]) for a in range(na)]
        sends, recvs = [], []
        for k in range(1, n_chip):
            px, py, _ = _peer(2 * k)
            for a in range(na):
                s = (k - 1) * na + a
                sends.append(_remote(ins[a].at[2 * px + py], outs[a].at[chip], sems, s, (px, py, c)))
                recvs.append(_remote(ins[a].at[2 * px + py], outs[a].at[2 * px + py], sems, s, (px, py, c)))
        if small is not None:
            me = 2 * chip + c
            locs.append(pltpu.make_async_copy(ins[na], outs[na].at[me], sems[2].at[na]))
            for k in range(1, N_DEV):
                px, py, pc = _peer(k)
                s = (n_chip - 1) * na + k - 1
                sends.append(_remote(ins[na], outs[na].at[me], sems, s, (px, py, pc)))
                recvs.append(_remote(ins[na], outs[na].at[4 * px + 2 * py + pc], sems, s, (px, py, pc)))
        return sends, recvs, locs

    extra = [] if small is None else [small]
    shapes = [_sds(a.shape, a.dtype) for a in parts] + [_sds((N_DEV,) + s.shape, s.dtype) for s in extra]
    n_sems = (n_chip - 1) * na + (N_DEV - 1) * len(extra)
    return _Rider(list(parts) + extra, shapes, n_sems, na + len(extra), plan)


def _pair_add(halves, other, core, name):
    _, nch, r, c = halves.shape

    def body(c_ref, h_ref, o_ref, p_ref):
        p_ref[...] = (h_ref[...].astype(F32) + o_ref[...].astype(F32)).astype(BF16)

    blk = pl.BlockSpec((None, r, c), lambda j, c_ref: (j, 0, 0))
    return _pc(
        body, name,
        grid_spec=pltpu.PrefetchScalarGridSpec(
            num_scalar_prefetch=1, grid=(nch,),
            in_specs=[pl.BlockSpec((None, None, r, c), lambda j, c_ref: (c_ref[0], j, 0, 0)), blk], out_specs=blk),
        out_shape=_sds((nch, r, c), BF16),
        compiler_params=_params(),
    )(core, halves, other)


def _sum_adamw(parts, w, m, v, name, rider=None):
    nl, r, c = w.shape
    tr = 256 if r % 256 == 0 else r

    def body(*refs):
        p_refs = refs[:nl]
        w_ref, m_ref, v_ref, g_ref, d_ref, mo_ref, vo_ref = refs[nl:]
        for l in range(nl):
            @pl.when(pl.program_id(0) == l)
            def _(p_ref=p_refs[l]):
                g = p_ref[0].astype(F32)
                for j in range(1, p_ref.shape[0]):
                    g = g + p_ref[j].astype(F32)
                mn = ADAM_B1 * m_ref[...] + (1.0 - ADAM_B1) * g
                vn = ADAM_B2 * v_ref[...] + (1.0 - ADAM_B2) * (g * g)
                m_hat = mn / (1.0 - ADAM_B1 ** ADAM_STEP)
                v_hat = vn / (1.0 - ADAM_B2 ** ADAM_STEP)
                g_ref[...] = g
                d_ref[...] = -ADAM_LR * (m_hat / (jnp.sqrt(v_hat) + ADAM_EPS) + ADAM_WD * w_ref[...])
                mo_ref[...] = mn
                vo_ref[...] = vn

    def part_spec(l, k):
        return pl.BlockSpec((k, tr, c), lambda li, i: (0, jnp.where(li == l, i, 0), 0))

    row = pl.BlockSpec((None, tr, c), lambda li, i: (li, i, 0))
    return _call(
        body, name, (*parts, w, m, v), rider, grid=(nl, r // tr),
        in_specs=[part_spec(l, p.shape[0]) for l, p in enumerate(parts)] + [row, row, row],
        out_specs=[row] * 4,
        out_shape=[_sds((nl, r, c), F32)] * 4,
        compiler_params=_params(),
    )


SMALL = ("lower_bounds", "pre_norm_g", "hgrn_norm_g", "pool_w", "pool_scale", "post_norm_g", "fox_f_bias")
SMALL_LANES = 128


def _small_size(tree):
    return sum(tree[k].size for k in SMALL)


def _pack_small(tree, extra=None):
    flat = jnp.concatenate([tree[k].reshape(-1) for k in SMALL] + ([] if extra is None else [extra.reshape(1)]))
    rows = -(-(_small_size(tree) + 1) // (8 * SMALL_LANES)) * 8
    return jnp.pad(flat, (0, rows * SMALL_LANES - flat.shape[0])).reshape(rows, SMALL_LANES)


def _unpack_small(packed, like):
    out, off = {}, 0
    for k in SMALL:
        size = like[k].size
        assert off % SMALL_LANES == 0
        rows = packed[off // SMALL_LANES:-(-(off + size) // SMALL_LANES)]
        out[k] = rows.reshape(-1)[:size].reshape(like[k].shape)
        off += size
    return out


def _block_diag(pw):
    g = pw.shape[0]
    eye = jnp.eye(g, dtype=pw.dtype)
    return (eye[:, None, :, None] * pw[:, :, None, :]).reshape(g * HEAD, g * HEAD)


def _assemble_w_in(g_in, name):
    _, d, shard = g_in.shape
    tr = min(256, d)
    wide = MAIN_W + FC_PAD

    def body(g_ref, wm_ref, wf_ref, wmt_ref, wft_ref, row_s):
        row_s[:, MAIN_W:] = jnp.zeros((tr, FC_PAD), F32)
        for j in range(N_DEV):
            row_s[:, j * shard:(j + 1) * shard] = g_ref[j].astype(F32)
        wm_ref[...] = row_s[:, :MAIN_W].astype(BF16)
        wf_ref[...] = row_s[:, MAIN_W:].astype(BF16)
        for j in range(0, MAIN_W, 512):
            wmt_ref[j:j + 512, :] = row_s[:, j:j + 512].T.astype(BF16)
        wft_ref[...] = row_s[:, MAIN_W:].T.astype(BF16)

    return _pc(
        body, name, grid=(d // tr,),
        in_specs=[pl.BlockSpec((N_DEV, tr, shard), lambda i: (0, i, 0))],
        out_specs=[pl.BlockSpec((tr, MAIN_W), lambda i: (i, 0)), pl.BlockSpec((tr, FC_PAD), lambda i: (i, 0)),
                   pl.BlockSpec((MAIN_W, tr), lambda i: (0, i)), pl.BlockSpec((FC_PAD, tr), lambda i: (0, i))],
        out_shape=[_sds((d, MAIN_W), BF16), _sds((d, FC_PAD), BF16), _sds((MAIN_W, d), BF16), _sds((FC_PAD, d), BF16)],
        scratch_shapes=[pltpu.VMEM((tr, wide), F32)],
        compiler_params=_params(),
    )(g_in)


def _w_out_parts(g_out):
    full_out = g_out.reshape(N_DEV * g_out.shape[1], g_out.shape[2])
    return full_out, full_out.T


def _layer_fwd(l, x, lbs, weights, lw, nb, rider_h=None, rider_c=None, target=None):
    w_main, w_fc, _, _, w_out, _ = lw
    bias = jnp.pad(weights["fox_f_bias"][l:l + 1], ((0, 0), (0, FC_PAD - FOX_HEADS)))
    wbd = _block_diag(weights["pool_w"][l]).astype(BF16)
    proj, fc, ht, qt, kt = _in_proj_fwd(x, weights["pre_norm_g"][l:l + 1], w_main, w_fc, f"in_proj_fwd_{l}")
    c_col, c_row = _fox_decay_fwd(fc, bias, nb, f"fox_decay_fwd_{l}")
    (o_h, s0), rode_h = _hgrn_fwd(proj, lbs[l:l + 1], _block_ones(HGRN_W, BF16), nb, f"hgrn_fwd_{l}", rider_h)
    (o_c, lse), rode_c = _fox_fwd(proj, kt, c_col, c_row, nb, f"fox_fwd_{l}", rider_c)
    if w_out is None:
        lw = tuple(lw[:4]) + _w_out_parts(rode_h[0])
        w_out, rode_h = lw[4], rode_h[1:]
    x_next, mixt, y = _merge_fwd(x, proj, o_h, o_c, weights["hgrn_norm_g"][l:l + 1], wbd, weights["pool_scale"][l:l + 1],
                                 w_out, weights["post_norm_g"][l:l + 1], nb, f"merge_fwd_{l}", target)
    return x_next, (x, proj, fc, ht, qt, kt, c_col, c_row, o_h, s0, o_c, lse, mixt, y, bias, wbd), lw, (rode_h, rode_c)


def _layer_bwd(l, dx, saved, lbs, weights, lw, nb, rider=None, send_w_out=None):
    _, proj, fc, ht, qt, kt, c_col, c_row, o_h, s0, o_c, lse, mixt, y, bias, wbd = saved
    w_out_t = lw[5]
    g = {}
    g["w_out"], dmix, dgp, da, dat, d_gc, delta = _merge_bwd(dx, y, weights["post_norm_g"][l:l + 1], w_out_t, proj, o_c,
                                                         mixt, nb, f"merge_bwd_{l}")
    g["post_norm_g"] = dgp[0]
    (d_a, dgh, dlb), arrived = _hgrn_bwd(dmix, proj, o_h, s0, weights["hgrn_norm_g"][l:l + 1], lbs[l:l + 1],
                                         _block_ones(HGRN_W, BF16), nb, f"hgrn_bwd_{l}",
                                         None if send_w_out is None else send_w_out([g["w_out"]]))
    if arrived is not None:
        g["w_out_received"] = arrived[0]
    g["hgrn_norm_g"], g["lbs"] = dgh[0], dlb[0]
    (d_qc, d_kc, d_vc, dc_k, dc_q), rode = _fox_bwd(proj, qt, kt, da, dat, c_col, c_row, lse, delta, nb,
                                                    f"fox_bwd_{l}", rider)
    d_fc, dbias = _fox_decay_bwd(dc_q, dc_k, fc, bias, nb, f"fox_decay_bwd_{l}")
    g["fox_f_bias"] = dbias[0, :FOX_HEADS]
    pieces = [(d_a, C_QA), (None, C_UB), (d_qc, C_QC), (d_kc, C_KC), (d_vc, C_VC), (d_gc, C_GC), (d_fc, None)]
    g["w_in"], d_b, dwbd, dps = _w_in_grad(ht, pieces, dmix, proj, wbd, wbd.T, weights["pool_scale"][l:l + 1], nb,
                                           f"w_in_grad_{l}")
    pieces[1] = (d_b, C_UB)
    g["pool_w"] = jnp.stack([dwbd[j * HEAD:(j + 1) * HEAD, j * HEAD:(j + 1) * HEAD] for j in range(len(POOL_WINDOWS))])
    g["pool_scale"] = dps[0]
    return g, pieces, rode


def _layer_bwd_input(l, dx, pieces, saved, weights, lw, rider=None):
    (dxi, dgpre), rode = _in_proj_bwd(pieces, lw[2], lw[3], saved[0], weights["pre_norm_g"][l:l + 1], dx,
                                      f"in_proj_bwd_{l}", rider)
    return dxi, dgpre[0], rode


def kernel(x, lower_bounds, pre_norm_g, w_in, hgrn_norm_g, fox_f_bias, pool_w, pool_scale, w_out, post_norm_g, loss_target, m_lower_bounds, m_pre_norm_g, m_w_in, m_hgrn_norm_g, m_fox_f_bias, m_pool_w, m_pool_scale, m_w_out, m_post_norm_g, v_lower_bounds, v_pre_norm_g, v_w_in, v_hgrn_norm_g, v_fox_f_bias, v_pool_w, v_pool_scale, v_w_out, v_post_norm_g):
    weights = dict(lower_bounds=lower_bounds, pre_norm_g=pre_norm_g, hgrn_norm_g=hgrn_norm_g, fox_f_bias=fox_f_bias,
                   pool_w=pool_w, pool_scale=pool_scale, post_norm_g=post_norm_g)
    mom_m = dict(lower_bounds=m_lower_bounds, pre_norm_g=m_pre_norm_g, hgrn_norm_g=m_hgrn_norm_g, fox_f_bias=m_fox_f_bias,
                 pool_w=m_pool_w, pool_scale=m_pool_scale, post_norm_g=m_post_norm_g)
    mom_v = dict(lower_bounds=v_lower_bounds, pre_norm_g=v_pre_norm_g, hgrn_norm_g=v_hgrn_norm_g, fox_f_bias=v_fox_f_bias,
                 pool_w=v_pool_w, pool_scale=v_pool_scale, post_norm_g=v_post_norm_g)
    depth = w_in.shape[0]
    nb, t, d = x.shape
    n = nb * t
    core = lax.axis_index("c").astype(jnp.int32).reshape(1)
    shards = [(w_in[l].astype(BF16), w_out[l].astype(BF16)) for l in range(depth)]
    lbs = _lower_bound_table(lower_bounds, "lower_bound_table")

    (g_in,) = _gather_weights(shards[0][0])
    coming = tuple(_assemble_w_in(g_in, "assemble_w_in_0")) + (None, None)
    xl, saved, lw = x.reshape(n, d), [], []
    for l in range(depth):
        last = l + 1 == depth
        ride_h = ([shards[l][1]] if coming[4] is None else []) + ([] if last else [shards[l + 1][1]])
        xl, sv, lw_l, (rode_h, rode_c) = _layer_fwd(
            l, xl, lbs, weights, coming, nb, _gather_rider(ride_h) if ride_h else None,
            None if last else _gather_rider([shards[l + 1][0]]), loss_target.reshape(n, d) if last else None)
        saved.append(sv)
        lw.append(lw_l)
        if not last:
            coming = tuple(_assemble_w_in(rode_c[0], f"assemble_w_in_{l + 1}")) + _w_out_parts(rode_h[0])
    dx, sq = xl
    loss_here = 0.5 * jnp.sum(sq) / d

    grads, recv_in, pending = [None] * depth, [None] * depth, None
    for l in reversed(range(depth)):
        g, pieces, rode = _layer_bwd(l, dx, saved[l], lbs, weights, lw[l], nb, pending, _direct_exchange_rider)
        if rode is not None:
            recv_in[l + 1] = rode[0]
        if l > 0:
            pending = _direct_exchange_rider([g["w_in"]])
            dx, g["pre_norm_g"], _ = _layer_bwd_input(l, dx, pieces, saved[l], weights, lw[l])
        else:
            (other,) = _run_rider(_swap_rider([g["w_in"]]), "grad_swap")
            summed = _pair_add(g["w_in"], other, core, "grad_pair_add")
            dx, g["pre_norm_g"], (recv_in[l],) = _layer_bwd_input(l, dx, pieces, saved[l], weights, lw[l],
                                                                  _chip_exchange_rider([summed]))
        grads[l] = g
    small = {k: jnp.stack([grads[l][k] for l in range(depth)]) for k in SMALL if k != "lower_bounds"}
    small["lower_bounds"] = _lower_bound_bwd(lower_bounds, jnp.stack([grads[l]["lbs"] for l in range(depth)]),
                                             "lower_bound_bwd")
    (r_small,) = _run_rider(_gather_rider([_pack_small(small, loss_here)]), "small_grads_gather")

    res_in, _ = _sum_adamw(recv_in, w_in, m_w_in, v_w_in, "adamw_w_in")
    res_out, _ = _sum_adamw([grads[l]["w_out_received"] for l in range(depth)], w_out, m_w_out, v_w_out, "adamw_w_out")
    res_small, _ = _sum_adamw([r_small], _pack_small(weights)[None], _pack_small(mom_m)[None], _pack_small(mom_v)[None],
                              "adamw_small")
    loss = res_small[0][0][_small_size(weights) // SMALL_LANES, _small_size(weights) % SMALL_LANES]

    names = ("lower_bounds", "pre_norm_g", "w_in", "hgrn_norm_g", "fox_f_bias", "pool_w", "pool_scale", "w_out", "post_norm_g")
    outs = [loss, dx.reshape(nb, t, d)]
    for i in range(4):
        full = dict(_unpack_small(res_small[i][0], weights), w_in=res_in[i], w_out=res_out[i])
        outs += [full[k] for k in names]
    return tuple(outs)
```

```python
import functools

import jax
import jax.numpy as jnp
from jax import lax
from jax.experimental import pallas as pl
from jax.experimental.pallas import tpu as pltpu

F32, BF16 = jnp.float32, jnp.bfloat16
MESH = pl.DeviceIdType.MESH
N_DEV = 8

NORM_EPS = 1e-6
MASK_VALUE = -1e30
TINY = 1e-30
CHUNK = 64
SUB = 16
HGRN_W, POOL_W, FOX_W = 256, 256, 512
HEAD = 64
FOX_HEADS = 8
POOL_WINDOWS = (2, 4, 8, 16)
POOL_HALO = 16
MAIN_W = 3584
FC_PAD = 128
C_QA, C_FA, C_IA, C_GA, C_UB, C_GB, C_QC, C_KC, C_VC, C_GC = 0, 256, 512, 768, 1024, 1280, 1536, 2048, 2560, 3072
FOX_SCALE = HEAD ** -0.5

ADAM_LR, ADAM_B1, ADAM_B2, ADAM_EPS, ADAM_WD, ADAM_STEP = 0.001, 0.9, 0.999, 1e-08, 0.01, 10

VMEM_LIMIT = 56 * 1024 * 1024


def _pc(fn, name, **kw):
    return pl.pallas_call(fn, name=name, **kw)


def _params(**kw):
    return pltpu.CompilerParams(vmem_limit_bytes=VMEM_LIMIT, **kw)


class _Rider:
    def __init__(self, inputs, out_shapes, n_sems, n_local, plan):
        self.inputs, self.out_shapes, self.n_sems, self.n_local, self.plan = list(inputs), list(out_shapes), n_sems, n_local, plan

    def start(self, ins, outs, *sems):
        sends, _, locs = self.plan(ins, outs, *sems)
        for cp in locs + sends:
            cp.start()

    def wait(self, ins, outs, *sems):
        sends, recvs, locs = self.plan(ins, outs, *sems)
        for cp in recvs:
            cp.wait_recv()
        for cp in sends:
            cp.wait_send()
        for cp in locs:
            cp.wait()

    def sem_shapes(self):
        return [pltpu.SemaphoreType.DMA((self.n_sems,)), pltpu.SemaphoreType.DMA((self.n_sems,)),
                pltpu.SemaphoreType.DMA((self.n_local,))]


def _call(body, name, args, rider=None, *, grid, in_specs, out_specs, out_shape, scratch_shapes=(), **kw):
    if rider is None:
        res = _pc(body, name, grid=grid, in_specs=in_specs, out_specs=out_specs, out_shape=out_shape,
                  scratch_shapes=list(scratch_shapes), **kw)(*args)
        return res, None
    n_in, n_out, n_scr = len(in_specs), len(out_specs), len(scratch_shapes)
    n_rin, n_rout = len(rider.inputs), len(rider.out_shapes)

    def ridden(*refs):
        ins, refs = refs[:n_in], refs[n_in:]
        rins, refs = refs[:n_rin], refs[n_rin:]
        outs, refs = refs[:n_out], refs[n_out:]
        routs, refs = refs[:n_rout], refs[n_rout:]
        scr, sems = refs[:n_scr], refs[n_scr:]
        first = functools.reduce(jnp.logical_and, [pl.program_id(a) == 0 for a in range(len(grid))])
        last = functools.reduce(jnp.logical_and, [pl.program_id(a) == g - 1 for a, g in enumerate(grid)])

        @pl.when(first)
        def _():
            rider.start(rins, routs, *sems)

        body(*ins, *outs, *scr)

        @pl.when(last)
        def _():
            rider.wait(rins, routs, *sems)

    any_spec = pl.BlockSpec(memory_space=pl.ANY)
    res = _pc(ridden, name, grid=grid, in_specs=list(in_specs) + [any_spec] * n_rin,
              out_specs=list(out_specs) + [any_spec] * n_rout, out_shape=list(out_shape) + rider.out_shapes,
              scratch_shapes=list(scratch_shapes) + rider.sem_shapes(), **kw)(*args, *rider.inputs)
    return res[:n_out], res[n_out:]


def _run_rider(rider, name):
    n_rin = len(rider.inputs)

    def body(*refs):
        ins, outs, sems = refs[:n_rin], refs[n_rin:n_rin + len(rider.out_shapes)], refs[n_rin + len(rider.out_shapes):]
        rider.start(ins, outs, *sems)
        rider.wait(ins, outs, *sems)

    any_spec = pl.BlockSpec(memory_space=pl.ANY)
    return _pc(body, name, in_specs=[any_spec] * n_rin, out_specs=[any_spec] * len(rider.out_shapes),
               out_shape=rider.out_shapes, scratch_shapes=rider.sem_shapes())(*rider.inputs)


def _dot(a, b):
    return jnp.dot(a, b, preferred_element_type=F32)


def _dot_nt(a, b):
    return lax.dot_general(a, b, (((1,), (1,)), ((), ())), preferred_element_type=F32)


def _dot_tn(a, b):
    return lax.dot_general(a, b, (((0,), (0,)), ((), ())), preferred_element_type=F32)


def _sel_dot_exact(sel, x):
    hi = x.astype(BF16)
    rest = x - hi.astype(F32)
    mid = rest.astype(BF16)
    lo = (rest - mid.astype(F32)).astype(BF16)
    sb = sel.astype(BF16)
    return _dot(sb, hi) + _dot(sb, mid) + _dot(sb, lo)


def _split2(x):
    hi = x.astype(BF16)
    return hi, (x - hi.astype(F32)).astype(BF16)


def _sel_dot(sel, x):
    hi, lo = _split2(x)
    sb = sel.astype(BF16)
    return _dot(sb, hi) + _dot(sb, lo)


def _dot_sel(x, sel):
    hi, lo = _split2(x)
    sb = sel.astype(BF16)
    return _dot(hi, sb) + _dot(lo, sb)


def _sigmoid(x):
    return 1.0 / (1.0 + jnp.exp(-x))


def _block_ones(n, dtype):
    r = lax.broadcasted_iota(jnp.int32, (n, n), 0) // HEAD
    c = lax.broadcasted_iota(jnp.int32, (n, n), 1) // HEAD
    return (r == c).astype(dtype)


def _sds(shape, dtype):
    return jax.ShapeDtypeStruct(shape, dtype)


def _in_proj_fwd(x, g_pre, w_main, w_fc, name):
    n, d = x.shape
    tm = min(512, n)

    def body(x_ref, g_ref, w_ref, wf_ref, proj_ref, fc_ref, ht_ref, qt_ref, kt_ref):
        xv = x_ref[...]
        r = lax.rsqrt(jnp.mean(xv * xv, axis=-1, keepdims=True) + NORM_EPS)
        hf = xv * r * g_ref[...]
        hb = hf.astype(BF16)
        ht_ref[...] = hf.T.astype(BF16)
        for j in range(0, MAIN_W, FOX_W):
            res = _dot(hb, w_ref[:, j:j + FOX_W])
            proj_ref[:, j:j + FOX_W] = res
            if j == C_QC:
                qt_ref[...] = (res * FOX_SCALE).T.astype(BF16)
            if j == C_KC:
                kt_ref[...] = res.T.astype(BF16)
        fc_ref[...] = _dot(hb, wf_ref[...])

    def cols(rows):
        return pl.BlockSpec((rows, tm), lambda i: (0, i))

    return _pc(
        body, name, grid=(n // tm,),
        in_specs=[pl.BlockSpec((tm, d), lambda i: (i, 0)), pl.BlockSpec((1, d), lambda i: (0, 0)),
                  pl.BlockSpec((d, MAIN_W), lambda i: (0, 0)), pl.BlockSpec((d, FC_PAD), lambda i: (0, 0))],
        out_specs=[pl.BlockSpec((tm, MAIN_W), lambda i: (i, 0)), pl.BlockSpec((tm, FC_PAD), lambda i: (i, 0)),
                   cols(d), cols(FOX_W), cols(FOX_W)],
        out_shape=[_sds((n, MAIN_W), F32), _sds((n, FC_PAD), F32), _sds((d, n), BF16), _sds((FOX_W, n), BF16),
                   _sds((FOX_W, n), BF16)],
        compiler_params=_params(),
    )(x, g_pre, w_main, w_fc)


def _fox_decay_fwd(fc, bias, nb, name):
    n = fc.shape[0]
    t = n // nb
    tt = min(256, t)
    nt = t // tt

    def body(fc_ref, b_ref, c_ref, cr_ref):
        r = lax.broadcasted_iota(jnp.int32, (tt, tt), 0)
        cc = lax.broadcasted_iota(jnp.int32, (tt, tt), 1)
        carry = jnp.zeros((1, FC_PAD), F32)
        for i in range(nt):
            rows = slice(i * tt, (i + 1) * tt)
            xv = fc_ref[rows, :] + b_ref[...]
            lf = jnp.minimum(xv, 0.0) - jnp.log(1.0 + jnp.exp(-jnp.abs(xv)))
            cs = _sel_dot_exact(r >= cc, lf) + carry
            c_ref[rows, :] = cs
            cr_ref[:, rows] = cs.T[:FOX_HEADS, :]
            carry = cs[tt - 1:tt, :]

    return _pc(
        body, name, grid=(nb,),
        in_specs=[pl.BlockSpec((t, FC_PAD), lambda b: (b, 0)), pl.BlockSpec((1, FC_PAD), lambda b: (0, 0))],
        out_specs=[pl.BlockSpec((t, FC_PAD), lambda b: (b, 0)), pl.BlockSpec((None, FOX_HEADS, t), lambda b: (b, 0, 0))],
        out_shape=[_sds((n, FC_PAD), F32), _sds((nb, FOX_HEADS, t), F32)],
        compiler_params=_params(),
    )(fc, bias)


def _hgrn_gates(q, z, lb):
    sig = _sigmoid(z)
    sn = _sigmoid(-z)
    f = lb + (1.0 - lb) * sig
    g = jnp.log(jnp.maximum(f, TINY))
    k = (1.0 - lb) * sn
    sq = _sigmoid(q)
    return sig, sn, f, g, k, sq


def _sub_tri(n, lower):
    r = lax.broadcasted_iota(jnp.int32, (n, n), 0)
    c = lax.broadcasted_iota(jnp.int32, (n, n), 1)
    tri = (r >= c) if lower else (r <= c)
    return jnp.logical_and(r // SUB == c // SUB, tri).astype(F32)


def _live_rows(t):
    return 8 * (t // 8 + 1)


def _pad_rows(x):
    return x if x.shape[0] == SUB else jnp.concatenate([x, jnp.zeros((SUB - x.shape[0], x.shape[1]), x.dtype)], axis=0)


def _hgrn_decays(qs, k, b):
    srow = lax.broadcasted_iota(jnp.int32, (SUB, HGRN_W), 0)
    es, ws = [], []
    for t in range(SUB):
        r = _live_rows(t)
        e = jnp.where(srow[:r] <= t, jnp.exp(b[t:t + 1, :] - b[:r]), 0.0)
        es.append(e)
        ws.append(_pad_rows(e * (qs[t:t + 1, :] * k[:r])))
    return srow, es, ws


def _hgrn_state_step(st, k, v, b, bmask):
    bl = b[SUB - 1:SUB, :]
    ktil = k * jnp.exp(bl - b)
    return st * jnp.exp(bl) + _dot_tn(v.astype(BF16), ktil.astype(BF16)) * bmask


def _hgrn_tile(t):
    return min(256, t)


def _hgrn_fwd(proj, lb, ones_b, nb, name, rider=None):
    n = proj.shape[0]
    t = n // nb
    tt = _hgrn_tile(t)
    nt = t // tt
    ncs = tt // CHUNK
    w = HGRN_W

    def body(q_ref, z_ref, v_ref, lb_ref, ones_ref, o_ref, s0_ref, st_s, b_s, qs_s, k_s):
        @pl.when(pl.program_id(0) == 0)
        def _():
            st_s[...] = jnp.zeros_like(st_s)

        for e in range(nb):
            q = q_ref[e]
            _, _, _, g, k, sq = _hgrn_gates(q, z_ref[e], lb_ref[...])
            b_s[e] = _sel_dot(_sub_tri(tt, True), g)
            qs_s[e] = q * sq
            k_s[e] = k
        bmask = _block_ones(w, F32)
        ones_b = ones_ref[...]
        nsub = CHUNK // SUB

        def chunk(c, carry):
            sts = [st_s[e] for e in range(nb)]
            for e in range(nb):
                s0_ref[e, c] = sts[e]
            base = pl.multiple_of(c * CHUNK, CHUNK)
            tiles = [[(qs_s[e, pl.ds(base + u * SUB, SUB), :], k_s[e, pl.ds(base + u * SUB, SUB), :],
                       v_ref[e, pl.ds(base + u * SUB, SUB), :], b_s[e, pl.ds(base + u * SUB, SUB), :])
                      for u in range(nsub)] for e in range(nb)]
            aexps = [[] for _ in range(nb)]
            for e in range(nb):
                for qs, k, v, b in tiles[e]:
                    _, _, ws = _hgrn_decays(qs, k, b)
                    aexps[e].append(_dot(jnp.concatenate(ws, axis=0).astype(BF16), ones_b))
            inters = [[] for _ in range(nb)]
            for u in range(nsub):
                for e in range(nb):
                    qs, k, v, b = tiles[e][u]
                    inters[e].append(_dot_nt((qs * jnp.exp(b)).astype(BF16), sts[e].astype(BF16)))
                    sts[e] = _hgrn_state_step(sts[e], k, v, b, bmask)
            for e in range(nb):
                st_s[e] = sts[e]
            for e in range(nb):
                for u, ((qs, k, v, b), aexp, o) in enumerate(zip(tiles[e], aexps[e], inters[e])):
                    for t in range(SUB):
                        r = _live_rows(t)
                        row = o[t:t + 1, :] + jnp.sum(aexp[t * SUB:t * SUB + r, :] * v[:r], axis=0, keepdims=True)
                        o_ref[e, pl.ds(base + u * SUB + t, 1), :] = row
            return carry

        lax.fori_loop(0, ncs, chunk, 0)

    def col(j):
        return pl.BlockSpec((nb, tt, w), lambda i: (0, i, j))

    proj3 = proj.reshape(nb, t, proj.shape[1])
    (o, s0), rode = _call(
        body, name, (proj3, proj3, proj3, lb, ones_b), rider, grid=(nt,),
        in_specs=[col(C_QA // w), col(C_FA // w), col(C_IA // w), pl.BlockSpec((1, w), lambda i: (0, 0)),
                  pl.BlockSpec((w, w), lambda i: (0, 0))],
        out_specs=[pl.BlockSpec((nb, tt, w), lambda i: (0, i, 0)),
                   pl.BlockSpec((nb, ncs, w, w), lambda i: (0, i, 0, 0))],
        out_shape=[_sds((nb, t, w), F32), _sds((nb, t // CHUNK, w, w), F32)],
        scratch_shapes=[pltpu.VMEM((nb, w, w), F32)] + [pltpu.VMEM((nb, tt, w), F32)] * 3,
        compiler_params=_params(),
    )
    return (o.reshape(n, w), s0.reshape(n // CHUNK, w, w)), rode


def _pool_lane_windows():
    lane = lax.broadcasted_iota(jnp.int32, (1, POOL_W), 1) // HEAD
    wl = jnp.zeros((1, POOL_W), F32)
    for gi, win in enumerate(POOL_WINDOWS):
        wl = jnp.where(lane == gi, float(win), wl)
    return lane, wl


def _pool_select(lane, parts):
    out = parts[-1]
    for gi in range(len(parts) - 2, -1, -1):
        out = jnp.where(lane == gi, parts[gi], out)
    return out


def _pool_mix(u, halo, t0, tt):
    lane, wl = _pool_lane_windows()
    ext = jnp.concatenate([halo, u], axis=0)
    sums, cur, shift = [], ext, 1
    for _ in POOL_WINDOWS:
        cur = cur + pltpu.roll(cur, shift, axis=0)
        sums.append(cur[POOL_HALO:, :])
        shift *= 2
    tpos = (t0 + lax.broadcasted_iota(jnp.int32, (tt, POOL_W), 0)).astype(F32)
    cnt = jnp.minimum(tpos + 1.0, wl)
    return _pool_select(lane, sums) / cnt - u, cnt


def _rows_reduce(x, op, final):
    while x.shape[0] > 8 and x.shape[0] % 16 == 0:
        half = x.shape[0] // 2
        x = op(x[:half], x[half:])
    return final(x, axis=0, keepdims=True)


def _tri_pair(step, n, group=1):
    counts = [a // group + 1 for a in range(n)]
    firsts = [sum(counts[:a]) for a in range(1, n)]
    a = sum([(step >= f).astype(jnp.int32) for f in firsts], jnp.int32(0))
    first = sum([jnp.where(step >= f, c, 0) for f, c in zip(firsts, counts)], jnp.int32(0))
    return a, step - first


def _tri_steps(n, group=1):
    return sum(a // group + 1 for a in range(n))


def _lane_lo():
    return lax.broadcasted_iota(jnp.int32, (1, 2 * HEAD), 1) < HEAD


def _put_col(tile, hh, colv):
    lane = lax.broadcasted_iota(jnp.int32, tile.shape, 1)
    return jnp.where(lane == hh, colv, tile)


def _fox_fwd(proj, kt, c_col, c_row, nb, name, rider=None):
    n = proj.shape[0]
    t = n // nb
    tb = min(256, t)
    nq = t // tb
    pw = 2 * HEAD
    kw = 2 if nq % 2 == 0 else 1
    nk = nq // kw

    def body(*refs):
        q_ref, kt_refs, (v_ref, cc_ref, cr_ref, o_ref, lse_ref, m_s, acc_s, cq_s) = refs[0], refs[1:1 + nb], refs[1 + nb:]
        qi, kk = _tri_pair(pl.program_id(0), nq, kw)

        @pl.when(kk == 0)
        def _():
            m_s[...] = jnp.full_like(m_s, -jnp.inf)
            acc_s[...] = jnp.zeros_like(acc_s)
            for e in range(nb):
                for hh in range(FOX_HEADS):
                    cq_s[e, hh] = jnp.broadcast_to(cc_ref[e, :, hh:hh + 1], (tb, pw))

        def block(masked, sub):
            lo = _lane_lo()
            keys = slice(sub * tb, (sub + 1) * tb)
            if masked:
                causal = lax.broadcasted_iota(jnp.int32, (tb, tb), 0) >= lax.broadcasted_iota(jnp.int32, (tb, tb), 1)
            def lanes(hh):
                return lo if hh % 2 == 0 else jnp.logical_not(lo)

            def scores(hh, e):
                sl = slice((hh // 2) * pw, (hh // 2 + 1) * pw)
                return _dot(jnp.where(lanes(hh), q_ref[e, :, sl] * FOX_SCALE, 0.0).astype(BF16), kt_refs[e][sl, keys])

            order = ([(hh, e) for e in range(nb) for hh in range(FOX_HEADS)] if masked else
                     [(hh, e) for hh in range(FOX_HEADS) for e in range(nb)])
            ahead = scores(*order[0])
            for j, (hh, e) in enumerate(order):
                s = ahead
                if j + 1 < len(order):
                    ahead = scores(*order[j + 1])
                s = s + (jnp.tile(cq_s[e, hh], (1, tb // pw)) - cr_ref[e, hh:hh + 1, keys])
                if masked:
                    s = jnp.where(causal, s, MASK_VALUE)
                m_prev = m_s[e, hh]
                m_new = jnp.maximum(m_prev, jnp.max(s, axis=1, keepdims=True))
                alpha = jnp.exp(m_prev - m_new)
                pe = jnp.exp(s - jnp.tile(m_new, (1, tb // pw)))
                m_s[e, hh] = m_new
                vf = v_ref[e, keys, (hh // 2) * pw:(hh // 2 + 1) * pw]
                acc_s[e, hh] = alpha * acc_s[e, hh] + _dot(pe.astype(BF16), jnp.where(lanes(hh), vf, 1.0).astype(BF16))

        def finish():
            lo = _lane_lo()
            for e in range(nb):
                m_all, l_all = jnp.zeros((tb, FC_PAD), F32), jnp.ones((tb, FC_PAD), F32)
                for p in range(FOX_HEADS // 2):
                    a0, a1 = acc_s[e, 2 * p], acc_s[e, 2 * p + 1]
                    both = pltpu.roll(jnp.where(lo, a1, a0), HEAD, axis=1)
                    o_ref[e, :, p * pw:(p + 1) * pw] = jnp.where(lo, a0, a1) / both
                    m_all = _put_col(_put_col(m_all, 2 * p, m_s[e, 2 * p]), 2 * p + 1, m_s[e, 2 * p + 1])
                    l_all = _put_col(_put_col(l_all, 2 * p, both), 2 * p + 1, a1)
                lse_ref[e] = (m_all + jnp.log(l_all)).T[:FOX_HEADS, :]

        for sub in range(kw):
            @pl.when(kk * kw + sub < qi)
            def _(sub=sub):
                block(False, sub)

            @pl.when(kk * kw + sub == qi)
            def _(sub=sub):
                block(True, sub)

        @pl.when(kk == qi // kw)
        def _():
            finish()

    def qspec(wd, j):
        return pl.BlockSpec((nb, tb, wd), lambda st: (0, _tri_pair(st, nq, kw)[0], j))

    def ktspec(e):
        return pl.BlockSpec((FOX_W, kw * tb), lambda st: (0, e * nk + _tri_pair(st, nq, kw)[1]))

    proj3 = proj.reshape(nb, t, proj.shape[1])
    (o, lse), rode = _call(
        body, name, (proj3, *[kt] * nb, proj3, c_col.reshape(nb, t, FC_PAD), c_row), rider, grid=(_tri_steps(nq, kw),),
        in_specs=[qspec(FOX_W, C_QC // FOX_W)] + [ktspec(e) for e in range(nb)] + [
            pl.BlockSpec((nb, kw * tb, FOX_W), lambda st: (0, _tri_pair(st, nq, kw)[1], C_VC // FOX_W)), qspec(FC_PAD, 0),
            pl.BlockSpec((nb, FOX_HEADS, kw * tb), lambda st: (0, 0, _tri_pair(st, nq, kw)[1]))],
        out_specs=[qspec(FOX_W, 0), pl.BlockSpec((nb, FOX_HEADS, tb), lambda st: (0, 0, _tri_pair(st, nq, kw)[0]))],
        out_shape=[_sds((nb, t, FOX_W), F32), _sds((nb, FOX_HEADS, t), F32)],
        scratch_shapes=[pltpu.VMEM((nb, FOX_HEADS, tb, pw), F32)] * 3,
        compiler_params=_params(),
    )
    return (o.reshape(n, FOX_W), lse), rode


def _head_mean(x, ones_f):
    return _dot_sel(x, ones_f) * (1.0 / HEAD)


def _merge_fwd(x, proj, o_h, o_c, gh, wbd, scale, w_out, g_post, nb, name, target=None):
    n, d = x.shape
    t = n // nb
    tm = min(512, t)
    nt = t // tm
    nhb = tm // POOL_HALO

    def body(*refs):
        x_ref, ga_ref, gc_ref, oh_ref, u_ref, gb_ref, h_ref, oc_ref, gh_ref, wb_ref, s_ref, w_ref, gp_ref = refs[:13]
        if target is None:
            xo_ref, mixt_ref, y_ref = refs[13:]
        else:
            t_ref, dx_ref, sq_ref, mixt_ref, y_ref = refs[13:]
        oh = oh_ref[...]
        ones_f = _block_ones(HGRN_W, F32)
        na = oh * lax.rsqrt(_head_mean(oh * oh, ones_f) + NORM_EPS) * gh_ref[...]
        ga, gb, gc = ga_ref[...], gb_ref[...], gc_ref[...]
        ti = pl.program_id(0) % nt
        pooled, _ = _pool_mix(u_ref[...], jnp.where(ti == 0, 0.0, h_ref[...]), ti * tm, tm)
        ob = _dot(pooled.astype(BF16), wb_ref[...]) * s_ref[...] * (gb * _sigmoid(gb))
        mixed = jnp.concatenate([na * (ga * _sigmoid(ga)), ob, oc_ref[...] * (gc * _sigmoid(gc))], axis=1)
        mixt_ref[...] = mixed.T.astype(BF16)
        y = _dot(mixed.astype(BF16), w_ref[...])
        y_ref[...] = y
        xn = x_ref[...] + y * lax.rsqrt(jnp.mean(y * y, axis=-1, keepdims=True) + NORM_EPS) * gp_ref[...]
        if target is None:
            xo_ref[...] = xn
        else:
            @pl.when(pl.program_id(0) == 0)
            def _():
                sq_ref[...] = jnp.zeros_like(sq_ref)

            e = xn - t_ref[...]
            dx_ref[...] = e * (1.0 / d)
            sq_ref[...] += jnp.sum(e * e, axis=0, keepdims=True)

    def row(wd, j=0):
        return pl.BlockSpec((tm, wd), lambda i: (i, j))

    def full(a, b):
        return pl.BlockSpec((a, b), lambda i: (0, 0))

    head = [] if target is None else [target]
    res = _pc(
        body, name, grid=(n // tm,),
        in_specs=[row(d), row(HGRN_W, C_GA // HGRN_W), row(FOX_W, C_GC // FOX_W), row(HGRN_W),
                  row(POOL_W, C_UB // POOL_W), row(POOL_W, C_GB // POOL_W),
                  pl.BlockSpec((POOL_HALO, POOL_W), lambda i: (jnp.maximum(i * nhb - 1, 0), C_UB // POOL_W)), row(FOX_W),
                  full(1, HGRN_W), full(POOL_W, POOL_W), full(1, POOL_W), full(d, d), full(1, d)] + [row(d)] * len(head),
        out_specs=[row(d)] + [full(1, d)] * len(head) + [pl.BlockSpec((d, tm), lambda i: (0, i)), row(d)],
        out_shape=[_sds((n, d), F32)] + [_sds((1, d), F32)] * len(head) + [_sds((d, n), BF16), _sds((n, d), F32)],
        compiler_params=_params(),
    )(x, proj, proj, o_h, proj, proj, proj, o_c, gh, wbd, scale, w_out, g_post, *head)
    return (res[0], res[1], res[2]) if target is None else ((res[0], res[1]), res[2], res[3])


def _rms_bwd(dy_scaled, xhat, r):
    return r * (dy_scaled - xhat * jnp.mean(dy_scaled * xhat, axis=-1, keepdims=True))


def _merge_bwd(dxo, y, g_post, w_out_t, proj, o_c, mixt, nb, name):
    n, d = y.shape
    t = n // nb
    tm = min(512, t)
    nt = t // tm
    wab = HGRN_W + POOL_W
    rows = d // N_DEV

    def body(dx_ref, y_ref, gp_ref, wt_ref, gc_ref, oc_ref, mt_ref,
             gw_ref, dm_ref, dgp_ref, da_ref, dat_ref, dg_ref, dl_ref, gw_s):
        @pl.when(pl.program_id(0) == 0)
        def _():
            dgp_ref[...] = jnp.zeros_like(dgp_ref)
            gw_s[...] = jnp.zeros_like(gw_s)

        yv, dxv = y_ref[...], dx_ref[...]
        r = lax.rsqrt(jnp.mean(yv * yv, axis=-1, keepdims=True) + NORM_EPS)
        yh = yv * r
        dgp_ref[...] += jnp.sum(dxv * yh, axis=0, keepdims=True)
        dyb = _rms_bwd(dxv * gp_ref[...], yh, r).astype(BF16)
        gw_s[...] += _dot(mt_ref[...], dyb)

        @pl.when(pl.program_id(0) == n // tm - 1)
        def _():
            for j in range(N_DEV):
                gw_ref[j % 2, j // 2] = gw_s[j * rows:(j + 1) * rows, :].astype(BF16)

        dm_ref[...] = _dot(dyb, wt_ref[:, :wab])
        dmc = _dot(dyb, wt_ref[:, wab:])
        gc, oc = gc_ref[...], oc_ref[...]
        sg = _sigmoid(gc)
        da = dmc * (gc * sg)
        da_ref[...] = da.astype(BF16)
        dat_ref[...] = da.T.astype(BF16)
        dg_ref[...] = (dmc * oc * (sg * (1.0 + gc * (1.0 - sg)))).astype(BF16)
        rr = lax.broadcasted_iota(jnp.int32, (FOX_W, FC_PAD), 0) // HEAD
        cc = lax.broadcasted_iota(jnp.int32, (FOX_W, FC_PAD), 1)
        dl_ref[...] = _dot_sel(da * oc, (rr == cc).astype(F32)).T[:FOX_HEADS, :]

    def row(wd, j=0):
        return pl.BlockSpec((tm, wd), lambda i: (i, j))

    def full(a, b):
        return pl.BlockSpec((a, b), lambda i: (0, 0))

    return _pc(
        body, name, grid=(n // tm,),
        in_specs=[row(d), row(d), full(1, d), full(d, d), row(FOX_W, C_GC // FOX_W), row(FOX_W),
                  pl.BlockSpec((d, tm), lambda i: (0, i))],
        out_specs=[pl.BlockSpec((2, N_DEV // 2, rows, d), lambda i: (0, 0, 0, 0)), row(wab), full(1, d), row(FOX_W),
                   pl.BlockSpec((FOX_W, tm), lambda i: (0, i)), row(FOX_W),
                   pl.BlockSpec((None, FOX_HEADS, tm), lambda i: (i // nt, 0, i % nt))],
        out_shape=[_sds((2, N_DEV // 2, rows, d), BF16), _sds((n, wab), F32), _sds((1, d), F32), _sds((n, FOX_W), BF16),
                   _sds((FOX_W, n), BF16), _sds((n, FOX_W), BF16), _sds((nb, FOX_HEADS, t), F32)],
        scratch_shapes=[pltpu.VMEM((d, d), F32)],
        compiler_params=_params(),
    )(dxo, y, g_post, w_out_t, proj, o_c, mixt)


def _w_in_grad(ht, pieces, dmix, proj, wbd, wbd_t, scale, nb, name):
    d, n = ht.shape
    t = n // nb
    ta, tk = d, min(512, t)
    nk, nt, nhb = n // tk, t // tk, tk // POOL_HALO
    given = [p for p, _ in pieces if p is not None]
    widths = [2 * POOL_W if p is None else p.shape[1] for p, _ in pieces]
    offs = [sum(widths[:i]) for i in range(len(widths))]
    in_w = MAIN_W + FOX_HEADS
    shard = in_w // N_DEV
    cu, cg, cm = C_UB // POOL_W, C_GB // POOL_W, HGRN_W // POOL_W

    def body(*refs):
        a_ref, p_refs = refs[0], list(refs[1:1 + len(given)])
        pool_refs = refs[1 + len(given):10 + len(given)]
        o_ref, db_ref, dw_ref, ds_ref, acc = refs[10 + len(given):]
        k = pl.program_id(1)

        @pl.when(k == 0)
        def _():
            dw_ref[...] = jnp.zeros_like(dw_ref)
            ds_ref[...] = jnp.zeros_like(ds_ref)
            acc[...] = jnp.zeros_like(acc)

        db, dw, dscale = _pool_bwd_tile(*pool_refs, k % nt, nt, tk)
        dw_ref[...] += dw
        ds_ref[...] += dscale
        db_ref[...] = db
        a = a_ref[...]
        for (p, _), off, wd in zip(pieces, offs, widths):
            pr = db if p is None else p_refs.pop(0)
            for j in range(0, wd, 512):
                jw = min(512, wd - j)
                acc[:, off + j:off + j + jw] += _dot(a, pr[:, j:j + jw])

        @pl.when(k == nk - 1)
        def _():
            for j in range(N_DEV):
                o_ref[j % 2, j // 2] = acc[:, j * shard:(j + 1) * shard].astype(BF16)

    def after(k):
        return jnp.minimum((k + 1) * nhb, n // POOL_HALO - 1)

    def tile(j):
        return pl.BlockSpec((tk, POOL_W), lambda i, k: (k, j))

    def full(r, c):
        return pl.BlockSpec((r, c), lambda i, k: (0, 0))

    pool_specs = [tile(cu), tile(cg),
                  pl.BlockSpec((POOL_HALO, POOL_W), lambda i, k: (jnp.maximum(k * nhb - 1, 0), cu)), tile(cm),
                  pl.BlockSpec((POOL_HALO, POOL_W), lambda i, k: (after(k), cg)),
                  pl.BlockSpec((POOL_HALO, POOL_W), lambda i, k: (after(k), cm)),
                  full(POOL_W, POOL_W), full(POOL_W, POOL_W), full(1, POOL_W)]
    return _pc(
        body, name, grid=(d // ta, nk),
        in_specs=[pl.BlockSpec((ta, tk), lambda i, k: (i, k))]
        + [pl.BlockSpec((tk, p.shape[1]), lambda i, k: (k, 0)) for p in given] + pool_specs,
        out_specs=[pl.BlockSpec((2, N_DEV // 2, ta, shard), lambda i, k: (0, 0, i, 0)),
                   pl.BlockSpec((tk, 2 * POOL_W), lambda i, k: (k, 0)), full(POOL_W, POOL_W), full(1, POOL_W)],
        out_shape=[_sds((2, N_DEV // 2, d, shard), BF16), _sds((n, 2 * POOL_W), BF16), _sds((POOL_W, POOL_W), F32),
                   _sds((1, POOL_W), F32)],
        scratch_shapes=[pltpu.VMEM((ta, sum(widths)), F32)],
        compiler_params=_params(),
    )(ht, *given, proj, proj, proj, dmix, proj, dmix, wbd, wbd_t, scale)


def _hgrn_state_bwd(qs, k, v, b, do, s0, ds1, bmask):
    bl = b[SUB - 1:SUB, :]
    eb, ebl, ekt = jnp.exp(b), jnp.exp(bl), jnp.exp(bl - b)
    qe, ktil = qs * eb, k * ekt
    ds1b, dob = ds1.astype(BF16), do.astype(BF16)
    dv = _dot_nt(ktil.astype(BF16), ds1b)
    dqe = _dot(dob, s0.astype(BF16))
    dktil = _dot(v.astype(BF16), ds1b)
    dbl = jnp.sum(dktil * ktil, axis=0, keepdims=True) + ebl * jnp.sum(s0 * ds1, axis=0, keepdims=True)
    ds0 = ds1 * ebl + _dot_tn(dob, qe.astype(BF16)) * bmask
    return dqe * eb, dktil * ekt, dv, dbl, ds0


def _hgrn_intra_bwd(qs, k, v, do, es, aexp, gexp, dq, dk, dv, put_dq_row):
    dks = [dk[j:j + 8] for j in range(0, SUB, 8)]
    dvs = [dv[j:j + 8] for j in range(0, SUB, 8)]
    for t in range(SUB):
        r = _live_rows(t)
        ge = gexp[t * SUB:t * SUB + r, :] * es[t]
        put_dq_row(t, dq[t:t + 1, :] + jnp.sum(ge * k[:r], axis=0, keepdims=True))
        for j in range(r // 8):
            dks[j] = dks[j] + ge[8 * j:8 * j + 8] * qs[t:t + 1, :]
            dvs[j] = dvs[j] + aexp[t * SUB + 8 * j:t * SUB + 8 * j + 8, :] * do[t:t + 1, :]
    return jnp.concatenate(dks, axis=0), jnp.concatenate(dvs, axis=0)


def _hgrn_bwd(dmix, proj, o_h, s0, gh, lb, ones_b, nb, name, rider=None):
    n = proj.shape[0]
    t = n // nb
    tt = _hgrn_tile(t)
    nt = t // tt
    ncs = tt // CHUNK
    nsub = CHUNK // SUB
    w = HGRN_W

    def body(dm_ref, q_ref, z_ref, v_ref, ga_ref, oh_ref, s0_ref, gh_ref, lb_ref, ones_ref,
             dp_ref, dgh_ref, dlb_ref, ds_s, ss_s, b_s, qs_s, k_s, do_s, dq_s, dk_s, dv_s, dbl_s):
        @pl.when(pl.program_id(0) == 0)
        def _():
            dgh_ref[...] = jnp.zeros_like(dgh_ref)
            dlb_ref[...] = jnp.zeros_like(dlb_ref)
            ds_s[...] = jnp.zeros_like(ds_s)

        ones_b = ones_ref[...]
        ones_f = ones_b.astype(F32)
        bmask = _block_ones(w, F32)
        lbv, ghv = lb_ref[...], gh_ref[...]
        kept = []
        for e in range(nb):
            oh, ga, dm = oh_ref[e], ga_ref[e], dm_ref[e]
            rn = lax.rsqrt(_head_mean(oh * oh, ones_f) + NORM_EPS)
            nh = oh * rn
            sga = _sigmoid(ga)
            dp_ref[e, :, 3 * w:4 * w] = (dm * nh * ghv * (sga * (1.0 + ga * (1.0 - sga)))).astype(BF16)
            dn = dm * (ga * sga)
            dgh_ref[...] += jnp.sum(dn * nh, axis=0, keepdims=True)
            dn = dn * ghv
            do_s[e] = rn * (dn - nh * _head_mean(dn * nh, ones_f))
            q = q_ref[e]
            sig, sn, f, g, k, sq = _hgrn_gates(q, z_ref[e], lbv)
            qs = q * sq
            b_s[e] = _sel_dot(_sub_tri(tt, True), g)
            qs_s[e] = qs
            k_s[e] = k
            kept.append((q, sig, sn, f, k, sq, qs))

        def chunk(cc, carry):
            c = ncs - 1 - cc
            base = pl.multiple_of(c * CHUNK, CHUNK)
            tiles = [[(qs_s[e, pl.ds(base + u * SUB, SUB), :], k_s[e, pl.ds(base + u * SUB, SUB), :],
                       v_ref[e, pl.ds(base + u * SUB, SUB), :], b_s[e, pl.ds(base + u * SUB, SUB), :],
                       do_s[e, pl.ds(base + u * SUB, SUB), :]) for u in range(nsub)] for e in range(nb)]
            sts = [s0_ref[e, c] for e in range(nb)]
            for u in range(nsub):
                for e in range(nb):
                    qs, k, v, b, do = tiles[e][u]
                    ss_s[e, u] = sts[e]
                    if u < nsub - 1:
                        sts[e] = _hgrn_state_step(sts[e], k, v, b, bmask)
            dss = [ds_s[e] for e in range(nb)]
            for u in reversed(range(nsub)):
                for e in range(nb):
                    qs, k, v, b, do = tiles[e][u]
                    _, es, ws = _hgrn_decays(qs, k, b)
                    gs = [_pad_rows(do[t:t + 1, :] * v[:_live_rows(t)]) for t in range(SUB)]
                    aexp = _dot(jnp.concatenate(ws, axis=0).astype(BF16), ones_b)
                    gexp = _dot(jnp.concatenate(gs, axis=0).astype(BF16), ones_b)
                    dq, dk, dv, dbl, dss[e] = _hgrn_state_bwd(qs, k, v, b, do, ss_s[e, u], dss[e], bmask)

                    def put_dq_row(i, row, e=e, r0=base + u * SUB):
                        dq_s[e, pl.ds(r0 + i, 1), :] = row

                    dk, dv = _hgrn_intra_bwd(qs, k, v, do, es, aexp, gexp, dq, dk, dv, put_dq_row)
                    dk_s[e, pl.ds(base + u * SUB, SUB), :] = dk
                    dv_s[e, pl.ds(base + u * SUB, SUB), :] = dv
                    dbl_s[e, pl.ds(base + u * SUB, SUB), :] = jnp.broadcast_to(dbl, (SUB, w))
            for e in range(nb):
                ds_s[e] = dss[e]
            return carry

        lax.fori_loop(0, ncs, chunk, 0)
        for e, (q, sig, sn, f, k, sq, qs) in enumerate(kept):
            dqs, dk = dq_s[e], dk_s[e]
            dg = _sel_dot(_sub_tri(tt, False), qs * dqs - k * dk) + dbl_s[e]
            dfz = jnp.where(f > TINY, dg / jnp.maximum(f, TINY), 0.0)
            dlb_ref[...] += jnp.sum(dfz * (1.0 - sig) - dk * sn, axis=0, keepdims=True)
            dp_ref[e, :, 0:w] = (dqs * (sq * (1.0 + q * (1.0 - sq)))).astype(BF16)
            dp_ref[e, :, w:2 * w] = ((dfz - dk) * (1.0 - lbv) * sig * sn).astype(BF16)
            dp_ref[e, :, 2 * w:3 * w] = dv_s[e].astype(BF16)

    def col(j):
        return pl.BlockSpec((nb, tt, w), lambda i: (0, nt - 1 - i, j))

    def full(a, bb):
        return pl.BlockSpec((a, bb), lambda i: (0, 0))

    proj3 = proj.reshape(nb, t, proj.shape[1])
    (dp, dgh, dlb), rode = _call(
        body, name, (dmix.reshape(nb, t, dmix.shape[1]), proj3, proj3, proj3, proj3, o_h.reshape(nb, t, w),
                     s0.reshape(nb, t // CHUNK, w, w), gh, lb, ones_b), rider, grid=(nt,),
        in_specs=[col(0), col(C_QA // w), col(C_FA // w), col(C_IA // w), col(C_GA // w), col(0),
                  pl.BlockSpec((nb, ncs, w, w), lambda i: (0, nt - 1 - i, 0, 0)), full(1, w), full(1, w), full(w, w)],
        out_specs=[pl.BlockSpec((nb, tt, 4 * w), lambda i: (0, nt - 1 - i, 0)), full(1, w), full(1, w)],
        out_shape=[_sds((nb, t, 4 * w), BF16), _sds((1, w), F32), _sds((1, w), F32)],
        scratch_shapes=[pltpu.VMEM((nb, w, w), F32), pltpu.VMEM((nb, nsub, w, w), F32)]
        + [pltpu.VMEM((nb, tt, w), F32)] * 8,
        compiler_params=_params(),
    )
    return (dp.reshape(n, 4 * w), dgh, dlb), rode


def _pool_bwd_tile(u_ref, g_ref, h_ref, dm_ref, gn_ref, dmn_ref, w_ref, wt_ref, s_ref, i, nt, tt):
    sc = s_ref[...]
    halo = jnp.where(i == 0, 0.0, h_ref[...])
    pooled, cnt = _pool_mix(u_ref[...], halo, i * tt, tt)
    pb = pooled.astype(BF16)
    pre = _dot(pb, w_ref[...])
    gv, dm = g_ref[...], dm_ref[...]
    sg = _sigmoid(gv)
    silu = gv * sg
    dgb = dm * pre * sc * (sg * (1.0 + gv * (1.0 - sg)))
    dscale = jnp.sum(dm * pre * silu, axis=0, keepdims=True)
    dpre = (dm * sc * silu).astype(BF16)
    dw = _dot_tn(pb, dpre)
    dpool = _dot(dpre, wt_ref[...])
    gn = gn_ref[...]
    dpre_n = (dmn_ref[...] * sc * (gn * _sigmoid(gn))).astype(BF16)
    dpool_n = jnp.where(i == nt - 1, 0.0, _dot(dpre_n, wt_ref[...]))
    lane, wl = _pool_lane_windows()
    tpos_n = ((i + 1) * tt + lax.broadcasted_iota(jnp.int32, (POOL_HALO, POOL_W), 0)).astype(F32)
    ext = jnp.concatenate([dpool / cnt, dpool_n / jnp.minimum(tpos_n + 1.0, wl)], axis=0)
    rows = tt + POOL_HALO
    sums, cur, shift = [], ext, 1
    for _ in POOL_WINDOWS:
        cur = cur + pltpu.roll(cur, rows - shift, axis=0)
        sums.append(cur[:tt, :])
        shift *= 2
    du = _pool_select(lane, sums) - dpool
    return jnp.concatenate([du, dgb], axis=1).astype(BF16), dw, dscale


def _fox_bwd(proj, qt, kt, da, dat, c_col, c_row, lse_row, delta_row, nb, name, rider=None):
    n = proj.shape[0]
    t = n // nb
    tb = min(256, t)
    nq = t // tb
    pw = 2 * HEAD
    qw = 2 if nq % 2 == 0 else 1
    nqs = nq // qw

    def body(q_ref, k_ref, v_ref, da_ref, qt_ref, kt_ref, dat_ref, cc_ref, cr_ref, lse_ref, dl_ref,
             dq_ref, dk_ref, dv_ref, dck_ref, dcq_ref, dq_s, dk_s, dv_s, dck_s, dcq_s):
        step = pl.program_id(1)
        kj, qq = pairs(step)

        @pl.when(step == 0)
        def _():
            dq_s[...] = jnp.zeros_like(dq_s)
            dcq_s[...] = jnp.zeros_like(dcq_s)

        @pl.when(qq == nqs - 1)
        def _():
            dk_s[...] = jnp.zeros_like(dk_s)
            dv_s[...] = jnp.zeros_like(dv_s)
            dck_s[...] = jnp.zeros_like(dck_s)

        def block(masked, sub):
            lo = _lane_lo()
            qi = qq * qw + sub
            qs = slice(sub * tb, (sub + 1) * tb)
            if masked:
                causal = lax.broadcasted_iota(jnp.int32, (tb, tb), 1) >= lax.broadcasted_iota(jnp.int32, (tb, tb), 0)
            dck = dck_s[...]
            for p in range(FOX_HEADS // 2):
                sl = slice(p * pw, (p + 1) * pw)
                qp = q_ref[qs, sl] * FOX_SCALE
                kp = k_ref[:, sl].astype(BF16)
                vp = v_ref[:, sl].astype(BF16)
                dap = da_ref[qs, sl]
                dk, dv = dk_s[:, sl], dv_s[:, sl]
                for h in range(2):
                    hh = 2 * p + h
                    lm = lo if h == 0 else jnp.logical_not(lo)
                    rows = slice(hh * HEAD, (hh + 1) * HEAD)
                    none = jnp.zeros((HEAD, tb), BF16)
                    qm = jnp.where(lm, qp, 0.0).astype(BF16)
                    dam = jnp.where(lm, dap, jnp.zeros_like(dap))
                    qtm = jnp.concatenate([qt_ref[rows, qs], none] if h == 0 else [none, qt_ref[rows, qs]], axis=0)
                    datm = jnp.concatenate([dat_ref[rows, qs], none] if h == 0 else [none, dat_ref[rows, qs]], axis=0)
                    s = _dot(kp, qtm) + (cr_ref[hh:hh + 1, qs] - cc_ref[:, hh:hh + 1])
                    pe = jnp.exp(s - lse_ref[hh:hh + 1, qs])
                    if masked:
                        pe = jnp.where(causal, pe, 0.0)
                    dp = _dot(vp, datm)
                    ds = pe * (dp - dl_ref[hh:hh + 1, qs])
                    dsb = ds.astype(BF16)
                    dv = dv + _dot(pe.astype(BF16), dam)
                    dk = dk + _dot(dsb, qm)
                    dq_s[qi, rows, :] += _dot(kt_ref[rows, :], dsb)
                    dck = dck - _put_col(jnp.zeros_like(dck), hh, jnp.sum(ds, axis=1, keepdims=True))
                    dcq_s[qi, hh:hh + 1, :] += _rows_reduce(ds, jnp.add, jnp.sum)
                dk_s[:, sl] = dk
                dv_s[:, sl] = dv
            dck_s[...] = dck

        for sub in reversed(range(qw)):
            @pl.when(qq * qw + sub > kj)
            def _(sub=sub):
                block(False, sub)

            @pl.when(qq * qw + sub == kj)
            def _(sub=sub):
                block(True, sub)

        @pl.when(qq == kj // qw)
        def _():
            dk_ref[...] = dk_s[...].astype(BF16)
            dv_ref[...] = dv_s[...].astype(BF16)
            dck_ref[...] = dck_s[...]

        @pl.when(step == _tri_steps(nq, qw) - 1)
        def _():
            for j in range(nq):
                dq_ref[j * tb:(j + 1) * tb, :] = (dq_s[j].T * FOX_SCALE).astype(BF16)
                dcq_ref[:, j * tb:(j + 1) * tb] = dcq_s[j]

    def pairs(step):
        a, b = _tri_pair(step, nq, qw)
        return nq - 1 - a, nqs - 1 - b

    def kspec(wd, j=0):
        return pl.BlockSpec((tb, wd), lambda b, st: (b * nq + pairs(st)[0], j))

    def qspec(wd, j=0):
        return pl.BlockSpec((qw * tb, wd), lambda b, st: (b * nqs + pairs(st)[1], j))

    def qrow():
        return pl.BlockSpec((None, FOX_HEADS, qw * tb), lambda b, st: (b, 0, pairs(st)[1]))

    def tspec(which):
        if which == 0:
            return pl.BlockSpec((FOX_W, tb), lambda b, st: (0, b * nq + pairs(st)[0]))
        return pl.BlockSpec((FOX_W, qw * tb), lambda b, st: (0, b * nqs + pairs(st)[1]))

    return _call(
        body, name, (proj, proj, proj, da, qt, kt, dat, c_col, c_row, lse_row, delta_row), rider,
        grid=(nb, _tri_steps(nq, qw)),
        in_specs=[qspec(FOX_W, C_QC // FOX_W), kspec(FOX_W, C_KC // FOX_W), kspec(FOX_W, C_VC // FOX_W), qspec(FOX_W),
                  tspec(1), tspec(0), tspec(1), kspec(FC_PAD), qrow(), qrow(), qrow()],
        out_specs=[pl.BlockSpec((t, FOX_W), lambda b, st: (b, 0)), kspec(FOX_W), kspec(FOX_W), kspec(FC_PAD),
                   pl.BlockSpec((None, FOX_HEADS, t), lambda b, st: (b, 0, 0))],
        out_shape=[_sds((n, FOX_W), BF16), _sds((n, FOX_W), BF16), _sds((n, FOX_W), BF16), _sds((n, FC_PAD), F32),
                   _sds((nb, FOX_HEADS, t), F32)],
        scratch_shapes=[pltpu.VMEM((nq, FOX_W, tb), F32), pltpu.VMEM((tb, FOX_W), F32), pltpu.VMEM((tb, FOX_W), F32),
                        pltpu.VMEM((tb, FC_PAD), F32), pltpu.VMEM((nq, FOX_HEADS, tb), F32)],
        compiler_params=_params(),
    )


def _fox_decay_bwd(dc_q, dc_k, fc, bias, nb, name):
    n = fc.shape[0]
    t = n // nb
    tt = min(256, t)
    nt = t // tt

    def body(dcq_ref, dck_ref, fc_ref, b_ref, dfc_ref, db_ref):
        @pl.when(pl.program_id(0) == 0)
        def _():
            db_ref[...] = jnp.zeros_like(db_ref)

        r = lax.broadcasted_iota(jnp.int32, (tt, tt), 0)
        cc = lax.broadcasted_iota(jnp.int32, (tt, tt), 1)
        carry = jnp.zeros((1, FC_PAD), F32)
        db = jnp.zeros((1, FC_PAD), F32)
        for i in reversed(range(nt)):
            rows = slice(i * tt, (i + 1) * tt)
            dcq = jnp.concatenate([dcq_ref[:, rows], jnp.zeros((FC_PAD - FOX_HEADS, tt), F32)], axis=0).T
            dlf = _sel_dot_exact(r <= cc, dcq + dck_ref[rows, :]) + carry
            carry = dlf[0:1, :]
            dfc = dlf * _sigmoid(-(fc_ref[rows, :] + b_ref[...]))
            dfc_ref[rows, :] = dfc.astype(BF16)
            db = db + jnp.sum(dfc, axis=0, keepdims=True)
        db_ref[...] += db

    def row():
        return pl.BlockSpec((t, FC_PAD), lambda b: (b, 0))

    return _pc(
        body, name, grid=(nb,),
        in_specs=[pl.BlockSpec((None, FOX_HEADS, t), lambda b: (b, 0, 0)), row(), row(),
                  pl.BlockSpec((1, FC_PAD), lambda b: (0, 0))],
        out_specs=[row(), pl.BlockSpec((1, FC_PAD), lambda b: (0, 0))],
        out_shape=[_sds((n, FC_PAD), BF16), _sds((1, FC_PAD), F32)],
        compiler_params=_params(),
    )(dc_q, dc_k, fc, bias)


def _in_proj_bwd(pieces, w_main_t, w_fc_t, x, g_pre, dxo, name, rider=None):
    n, d = x.shape
    tm = min(512, n)
    widths = [p.shape[1] for p, _ in pieces]
    offs = [o for _, o in pieces]
    np_ = len(pieces)

    def body(*refs):
        p_refs = refs[:np_]
        wt_ref, wf_ref, x_ref, g_ref, dxo_ref, dx_ref, dg_ref = refs[np_:]

        @pl.when(pl.program_id(0) == 0)
        def _():
            dg_ref[...] = jnp.zeros_like(dg_ref)

        dh = _dot(p_refs[-1][...], wf_ref[...])
        for pr, wd, off in zip(p_refs[:-1], widths[:-1], offs[:-1]):
            for j in range(0, wd, 512):
                jw = min(512, wd - j)
                dh = dh + _dot(pr[:, j:j + jw], wt_ref[off + j:off + j + jw, :])
        xv = x_ref[...]
        r = lax.rsqrt(jnp.mean(xv * xv, axis=-1, keepdims=True) + NORM_EPS)
        xh = xv * r
        dg_ref[...] += jnp.sum(dh * xh, axis=0, keepdims=True)
        dx_ref[...] = dxo_ref[...] + _rms_bwd(dh * g_ref[...], xh, r)

    row = pl.BlockSpec((tm, d), lambda i: (i, 0))
    return _call(
        body, name, (*[p for p, _ in pieces], w_main_t, w_fc_t, x, g_pre, dxo), rider, grid=(n // tm,),
        in_specs=[pl.BlockSpec((tm, wd), lambda i: (i, 0)) for wd in widths] + [
            pl.BlockSpec((MAIN_W, d), lambda i: (0, 0)), pl.BlockSpec((FC_PAD, d), lambda i: (0, 0)),
            row, pl.BlockSpec((1, d), lambda i: (0, 0)), row],
        out_specs=[row, pl.BlockSpec((1, d), lambda i: (0, 0))],
        out_shape=[_sds((n, d), F32), _sds((1, d), F32)],
        compiler_params=_params(),
    )


def _lower_bound_table(lower_bounds, name):
    depth, w = lower_bounds.shape

    def body(lb_ref, o_ref):
        v = lb_ref[...]
        e = jnp.exp(v - jnp.max(v, axis=0, keepdims=True))
        p = e / jnp.sum(e, axis=0, keepdims=True)
        acc = jnp.zeros((1, w), F32)
        for l in range(depth):
            acc = acc + p[l:l + 1, :]
            o_ref[l:l + 1, :] = acc - p[0:1, :]

    return _pc(body, name, out_shape=_sds((depth, w), F32))(lower_bounds)


def _lower_bound_bwd(lower_bounds, dlbs, name):
    depth, w = lower_bounds.shape

    def body(lb_ref, d_ref, o_ref):
        v, dl = lb_ref[...], d_ref[...]
        e = jnp.exp(v - jnp.max(v, axis=0, keepdims=True))
        p = e / jnp.sum(e, axis=0, keepdims=True)
        tot = jnp.sum(dl, axis=0, keepdims=True)
        rows, tail = [], tot
        for l in range(depth):
            rows.append(tail - tot if l == 0 else tail)
            tail = tail - dl[l:l + 1, :]
        dp = jnp.concatenate(rows, axis=0)
        o_ref[...] = p * (dp - jnp.sum(p * dp, axis=0, keepdims=True))

    return _pc(body, name, out_shape=_sds((depth, w), F32))(lower_bounds, dlbs)


def _place():
    x, y, c = lax.axis_index("x"), lax.axis_index("y"), lax.axis_index("c")
    return x, y, c


def _gather_weights(*arrays):
    na = len(arrays)

    def body(*refs):
        ins, outs = refs[:na], refs[na:2 * na]
        send_sems, recv_sems, local_sems = refs[2 * na:]
        x, y, c = _place()
        me, sibling = (x, y, c), (x, y, 1 - c)
        chips = [(1 - x, y), (x, 1 - y), (1 - x, 1 - y)]

        def slot(a, px, py, pc):
            return outs[a].at[4 * px + 2 * py + pc]

        def copy(a, k, block, to, own=False):
            return pltpu.make_async_remote_copy(
                src_ref=ins[a] if own else slot(a, *block), dst_ref=slot(a, *block),
                send_sem=send_sems.at[a * 7 + k], recv_sem=recv_sems.at[a * 7 + k],
                device_id=to, device_id_type=MESH)

        mine = [pltpu.make_async_copy(ins[a], slot(a, *me), local_sems.at[a]) for a in range(na)]
        for cp in mine:
            cp.start()
        first = []
        for a in range(na):
            first.append(copy(a, 0, me, sibling, own=True))
            first += [copy(a, 1 + j, me, (*chip, c), own=True) for j, chip in enumerate(chips)]
        for cp in first:
            cp.start()
        passed = []
        for j, chip in enumerate(chips):
            for a in range(na):
                copy(a, 1 + j, (*chip, c), me).wait_recv()
                fw = copy(a, 4 + j, (*chip, c), sibling)
                fw.start()
                passed.append(fw)
        for a in range(na):
            copy(a, 0, sibling, me).wait_recv()
            for j, chip in enumerate(chips):
                copy(a, 4 + j, (*chip, 1 - c), me).wait_recv()
        for cp in first + passed:
            cp.wait_send()
        for cp in mine:
            cp.wait()

    any_spec = pl.BlockSpec(memory_space=pl.ANY)
    return _pc(
        body, "gather_weights",
        in_specs=[any_spec] * na, out_specs=[any_spec] * na,
        out_shape=[_sds((N_DEV,) + a.shape, a.dtype) for a in arrays],
        scratch_shapes=[pltpu.SemaphoreType.DMA((7 * na,)), pltpu.SemaphoreType.DMA((7 * na,)),
                        pltpu.SemaphoreType.DMA((na,))],
    )(*arrays)


def _peer(k):
    x, y, c = _place()
    return (1 - x if k & 4 else x, 1 - y if k & 2 else y, 1 - c if k & 1 else c)


def _remote(src, dst, sems, s, to):
    return pltpu.make_async_remote_copy(src_ref=src, dst_ref=dst, send_sem=sems[0].at[s], recv_sem=sems[1].at[s],
                                        device_id=to, device_id_type=MESH)


def _gather_rider(shards):
    na = len(shards)

    def plan(ins, outs, *sems):
        x, y, c = _place()
        me = 4 * x + 2 * y + c
        locs = [pltpu.make_async_copy(ins[a], outs[a].at[me], sems[2].at[a]) for a in range(na)]
        sends, recvs = [], []
        for k in range(1, N_DEV):
            px, py, pc = _peer(k)
            for a in range(na):
                s = (k - 1) * na + a
                sends.append(_remote(ins[a], outs[a].at[me], sems, s, (px, py, pc)))
                recvs.append(_remote(ins[a], outs[a].at[4 * px + 2 * py + pc], sems, s, (px, py, pc)))
        return sends, recvs, locs

    return _Rider(shards, [_sds((N_DEV,) + a.shape, a.dtype) for a in shards], (N_DEV - 1) * na, na, plan)


def _direct_exchange_rider(blocks):
    na = len(blocks)

    def plan(ins, outs, *sems):
        x, y, c = _place()
        me = 4 * x + 2 * y + c
        locs = [pltpu.make_async_copy(ins[a].at[c, 2 * x + y], outs[a].at[me], sems[2].at[a]) for a in range(na)]
        sends, recvs = [], []
        for k in range(1, N_DEV):
            px, py, pc = _peer(k)
            for a in range(na):
                s = (k - 1) * na + a
                sends.append(_remote(ins[a].at[pc, 2 * px + py], outs[a].at[me], sems, s, (px, py, pc)))
                recvs.append(_remote(ins[a].at[pc, 2 * px + py], outs[a].at[4 * px + 2 * py + pc], sems, s, (px, py, pc)))
        return sends, recvs, locs

    return _Rider(blocks, [_sds((N_DEV,) + a.shape[2:], a.dtype) for a in blocks], (N_DEV - 1) * na, na, plan)


def _swap_rider(halves):
    na = len(halves)

    def plan(ins, outs, *sems):
        x, y, c = _place()
        cps = [_remote(ins[a].at[1 - c], outs[a], sems, a, (x, y, 1 - c)) for a in range(na)]
        return cps, cps, []

    return _Rider(halves, [_sds(a.shape[1:], a.dtype) for a in halves], na, 1, plan)


def _chip_exchange_rider(parts, small=None):
    na = len(parts)
    n_chip = N_DEV // 2

    def plan(ins, outs, *sems):
        x, y, c = _place()
        chip = 2 * x + y
        locs = [pltpu.make_async_copy(ins[a].at[chip], outs[a].at[chip], sems[2].at[a]) for a in range(na)]
        sends, recvs = [], []
        for k in range(1, n_chip):
            px, py, _ = _peer(2 * k)
            for a in range(na):
                s = (k - 1) * na + a
                sends.append(_remote(ins[a].at[2 * px + py], outs[a].at[chip], sems, s, (px, py, c)))
                recvs.append(_remote(ins[a].at[2 * px + py], outs[a].at[2 * px + py], sems, s, (px, py, c)))
        if small is not None:
            me = 2 * chip + c
            locs.append(pltpu.make_async_copy(ins[na], outs[na].at[me], sems[2].at[na]))
            for k in range(1, N_DEV):
                px, py, pc = _peer(k)
                s = (n_chip - 1) * na + k - 1
                sends.append(_remote(ins[na], outs[na].at[me], sems, s, (px, py, pc)))
                recvs.append(_remote(ins[na], outs[na].at[4 * px + 2 * py + pc], sems, s, (px, py, pc)))
        return sends, recvs, locs

    extra = [] if small is None else [small]
    shapes = [_sds(a.shape, a.dtype) for a in parts] + [_sds((N_DEV,) + s.shape, s.dtype) for s in extra]
    n_sems = (n_chip - 1) * na + (N_DEV - 1) * len(extra)
    return _Rider(list(parts) + extra, shapes, n_sems, na + len(extra), plan)


def _pair_add(halves, other, core, name):
    _, nch, r, c = halves.shape

    def body(c_ref, h_ref, o_ref, p_ref):
        p_ref[...] = (h_ref[...].astype(F32) + o_ref[...].astype(F32)).astype(BF16)

    blk = pl.BlockSpec((None, r, c), lambda j, c_ref: (j, 0, 0))
    return _pc(
        body, name,
        grid_spec=pltpu.PrefetchScalarGridSpec(
            num_scalar_prefetch=1, grid=(nch,),
            in_specs=[pl.BlockSpec((None, None, r, c), lambda j, c_ref: (c_ref[0], j, 0, 0)), blk], out_specs=blk),
        out_shape=_sds((nch, r, c), BF16),
        compiler_params=_params(),
    )(core, halves, other)


def _sum_adamw(parts, w, m, v, name, rider=None):
    nl, r, c = w.shape
    tr = 256 if r % 256 == 0 else r

    def body(*refs):
        p_refs = refs[:nl]
        w_ref, m_ref, v_ref, g_ref, d_ref, mo_ref, vo_ref = refs[nl:]
        for l in range(nl):
            @pl.when(pl.program_id(0) == l)
            def _(p_ref=p_refs[l]):
                g = p_ref[0].astype(F32)
                for j in range(1, p_ref.shape[0]):
                    g = g + p_ref[j].astype(F32)
                mn = ADAM_B1 * m_ref[...] + (1.0 - ADAM_B1) * g
                vn = ADAM_B2 * v_ref[...] + (1.0 - ADAM_B2) * (g * g)
                m_hat = mn / (1.0 - ADAM_B1 ** ADAM_STEP)
                v_hat = vn / (1.0 - ADAM_B2 ** ADAM_STEP)
                g_ref[...] = g
                d_ref[...] = -ADAM_LR * (m_hat / (jnp.sqrt(v_hat) + ADAM_EPS) + ADAM_WD * w_ref[...])
                mo_ref[...] = mn
                vo_ref[...] = vn

    def part_spec(l, k):
        return pl.BlockSpec((k, tr, c), lambda li, i: (0, jnp.where(li == l, i, 0), 0))

    row = pl.BlockSpec((None, tr, c), lambda li, i: (li, i, 0))
    return _call(
        body, name, (*parts, w, m, v), rider, grid=(nl, r // tr),
        in_specs=[part_spec(l, p.shape[0]) for l, p in enumerate(parts)] + [row, row, row],
        out_specs=[row] * 4,
        out_shape=[_sds((nl, r, c), F32)] * 4,
        compiler_params=_params(),
    )


SMALL = ("lower_bounds", "pre_norm_g", "hgrn_norm_g", "pool_w", "pool_scale", "post_norm_g", "fox_f_bias")
SMALL_LANES = 128


def _small_size(tree):
    return sum(tree[k].size for k in SMALL)


def _pack_small(tree, extra=None):
    flat = jnp.concatenate([tree[k].reshape(-1) for k in SMALL] + ([] if extra is None else [extra.reshape(1)]))
    rows = -(-(_small_size(tree) + 1) // (8 * SMALL_LANES)) * 8
    return jnp.pad(flat, (0, rows * SMALL_LANES - flat.shape[0])).reshape(rows, SMALL_LANES)


def _unpack_small(packed, like):
    out, off = {}, 0
    for k in SMALL:
        size = like[k].size
        assert off % SMALL_LANES == 0
        rows = packed[off // SMALL_LANES:-(-(off + size) // SMALL_LANES)]
        out[k] = rows.reshape(-1)[:size].reshape(like[k].shape)
        off += size
    return out


def _block_diag(pw):
    g = pw.shape[0]
    eye = jnp.eye(g, dtype=pw.dtype)
    return (eye[:, None, :, None] * pw[:, :, None, :]).reshape(g * HEAD, g * HEAD)


def _assemble_w_in(g_in, name):
    _, d, shard = g_in.shape
    tr = min(256, d)
    wide = MAIN_W + FC_PAD

    def body(g_ref, wm_ref, wf_ref, wmt_ref, wft_ref, row_s):
        row_s[:, MAIN_W:] = jnp.zeros((tr, FC_PAD), F32)
        for j in range(N_DEV):
            row_s[:, j * shard:(j + 1) * shard] = g_ref[j].astype(F32)
        wm_ref[...] = row_s[:, :MAIN_W].astype(BF16)
        wf_ref[...] = row_s[:, MAIN_W:].astype(BF16)
        for j in range(0, MAIN_W, 512):
            wmt_ref[j:j + 512, :] = row_s[:, j:j + 512].T.astype(BF16)
        wft_ref[...] = row_s[:, MAIN_W:].T.astype(BF16)

    return _pc(
        body, name, grid=(d // tr,),
        in_specs=[pl.BlockSpec((N_DEV, tr, shard), lambda i: (0, i, 0))],
        out_specs=[pl.BlockSpec((tr, MAIN_W), lambda i: (i, 0)), pl.BlockSpec((tr, FC_PAD), lambda i: (i, 0)),
                   pl.BlockSpec((MAIN_W, tr), lambda i: (0, i)), pl.BlockSpec((FC_PAD, tr), lambda i: (0, i))],
        out_shape=[_sds((d, MAIN_W), BF16), _sds((d, FC_PAD), BF16), _sds((MAIN_W, d), BF16), _sds((FC_PAD, d), BF16)],
        scratch_shapes=[pltpu.VMEM((tr, wide), F32)],
        compiler_params=_params(),
    )(g_in)


def _w_out_parts(g_out):
    full_out = g_out.reshape(N_DEV * g_out.shape[1], g_out.shape[2])
    return full_out, full_out.T


def _layer_fwd(l, x, lbs, weights, lw, nb, rider_h=None, rider_c=None, target=None):
    w_main, w_fc, _, _, w_out, _ = lw
    bias = jnp.pad(weights["fox_f_bias"][l:l + 1], ((0, 0), (0, FC_PAD - FOX_HEADS)))
    wbd = _block_diag(weights["pool_w"][l]).astype(BF16)
    proj, fc, ht, qt, kt = _in_proj_fwd(x, weights["pre_norm_g"][l:l + 1], w_main, w_fc, f"in_proj_fwd_{l}")
    c_col, c_row = _fox_decay_fwd(fc, bias, nb, f"fox_decay_fwd_{l}")
    (o_h, s0), rode_h = _hgrn_fwd(proj, lbs[l:l + 1], _block_ones(HGRN_W, BF16), nb, f"hgrn_fwd_{l}", rider_h)
    (o_c, lse), rode_c = _fox_fwd(proj, kt, c_col, c_row, nb, f"fox_fwd_{l}", rider_c)
    if w_out is None:
        lw = tuple(lw[:4]) + _w_out_parts(rode_h[0])
        w_out, rode_h = lw[4], rode_h[1:]
    x_next, mixt, y = _merge_fwd(x, proj, o_h, o_c, weights["hgrn_norm_g"][l:l + 1], wbd, weights["pool_scale"][l:l + 1],
                                 w_out, weights["post_norm_g"][l:l + 1], nb, f"merge_fwd_{l}", target)
    return x_next, (x, proj, fc, ht, qt, kt, c_col, c_row, o_h, s0, o_c, lse, mixt, y, bias, wbd), lw, (rode_h, rode_c)


def _layer_bwd(l, dx, saved, lbs, weights, lw, nb, rider=None, send_w_out=None):
    _, proj, fc, ht, qt, kt, c_col, c_row, o_h, s0, o_c, lse, mixt, y, bias, wbd = saved
    w_out_t = lw[5]
    g = {}
    g["w_out"], dmix, dgp, da, dat, d_gc, delta = _merge_bwd(dx, y, weights["post_norm_g"][l:l + 1], w_out_t, proj, o_c,
                                                         mixt, nb, f"merge_bwd_{l}")
    g["post_norm_g"] = dgp[0]
    (d_a, dgh, dlb), arrived = _hgrn_bwd(dmix, proj, o_h, s0, weights["hgrn_norm_g"][l:l + 1], lbs[l:l + 1],
                                         _block_ones(HGRN_W, BF16), nb, f"hgrn_bwd_{l}",
                                         None if send_w_out is None else send_w_out([g["w_out"]]))
    if arrived is not None:
        g["w_out_received"] = arrived[0]
    g["hgrn_norm_g"], g["lbs"] = dgh[0], dlb[0]
    (d_qc, d_kc, d_vc, dc_k, dc_q), rode = _fox_bwd(proj, qt, kt, da, dat, c_col, c_row, lse, delta, nb,
                                                    f"fox_bwd_{l}", rider)
    d_fc, dbias = _fox_decay_bwd(dc_q, dc_k, fc, bias, nb, f"fox_decay_bwd_{l}")
    g["fox_f_bias"] = dbias[0, :FOX_HEADS]
    pieces = [(d_a, C_QA), (None, C_UB), (d_qc, C_QC), (d_kc, C_KC), (d_vc, C_VC), (d_gc, C_GC), (d_fc, None)]
    g["w_in"], d_b, dwbd, dps = _w_in_grad(ht, pieces, dmix, proj, wbd, wbd.T, weights["pool_scale"][l:l + 1], nb,
                                           f"w_in_grad_{l}")
    pieces[1] = (d_b, C_UB)
    g["pool_w"] = jnp.stack([dwbd[j * HEAD:(j + 1) * HEAD, j * HEAD:(j + 1) * HEAD] for j in range(len(POOL_WINDOWS))])
    g["pool_scale"] = dps[0]
    return g, pieces, rode


def _layer_bwd_input(l, dx, pieces, saved, weights, lw, rider=None):
    (dxi, dgpre), rode = _in_proj_bwd(pieces, lw[2], lw[3], saved[0], weights["pre_norm_g"][l:l + 1], dx,
                                      f"in_proj_bwd_{l}", rider)
    return dxi, dgpre[0], rode


def kernel(x, lower_bounds, pre_norm_g, w_in, hgrn_norm_g, fox_f_bias, pool_w, pool_scale, w_out, post_norm_g, loss_target, m_lower_bounds, m_pre_norm_g, m_w_in, m_hgrn_norm_g, m_fox_f_bias, m_pool_w, m_pool_scale, m_w_out, m_post_norm_g, v_lower_bounds, v_pre_norm_g, v_w_in, v_hgrn_norm_g, v_fox_f_bias, v_pool_w, v_pool_scale, v_w_out, v_post_norm_g):
    weights = dict(lower_bounds=lower_bounds, pre_norm_g=pre_norm_g, hgrn_norm_g=hgrn_norm_g, fox_f_bias=fox_f_bias,
                   pool_w=pool_w, pool_scale=pool_scale, post_norm_g=post_norm_g)
    mom_m = dict(lower_bounds=m_lower_bounds, pre_norm_g=m_pre_norm_g, hgrn_norm_g=m_hgrn_norm_g, fox_f_bias=m_fox_f_bias,
                 pool_w=m_pool_w, pool_scale=m_pool_scale, post_norm_g=m_post_norm_g)
    mom_v = dict(lower_bounds=v_lower_bounds, pre_norm_g=v_pre_norm_g, hgrn_norm_g=v_hgrn_norm_g, fox_f_bias=v_fox_f_bias,
                 pool_w=v_pool_w, pool_scale=v_pool_scale, post_norm_g=v_post_norm_g)
    depth = w_in.shape[0]
    nb, t, d = x.shape
    n = nb * t
    core = lax.axis_index("c").astype(jnp.int32).reshape(1)
    shards = [(w_in[l].astype(BF16), w_out[l].astype(BF16)) for l in range(depth)]
    lbs = _lower_bound_table(lower_bounds, "lower_bound_table")

    (g_in,) = _gather_weights(shards[0][0])
    coming = tuple(_assemble_w_in(g_in, "assemble_w_in_0")) + (None, None)
    xl, saved, lw = x.reshape(n, d), [], []
    for l in range(depth):
        last = l + 1 == depth
        ride_h = ([shards[l][1]] if coming[4] is None else []) + ([] if last else [shards[l + 1][1]])
        xl, sv, lw_l, (rode_h, rode_c) = _layer_fwd(
            l, xl, lbs, weights, coming, nb, _gather_rider(ride_h) if ride_h else None,
            None if last else _gather_rider([shards[l + 1][0]]), loss_target.reshape(n, d) if last else None)
        saved.append(sv)
        lw.append(lw_l)
        if not last:
            coming = tuple(_assemble_w_in(rode_c[0], f"assemble_w_in_{l + 1}")) + _w_out_parts(rode_h[0])
    dx, sq = xl
    loss_here = 0.5 * jnp.sum(sq) / d

    grads, recv_in, pending = [None] * depth, [None] * depth, None
    for l in reversed(range(depth)):
        g, pieces, rode = _layer_bwd(l, dx, saved[l], lbs, weights, lw[l], nb, pending, _direct_exchange_rider)
        if rode is not None:
            recv_in[l + 1] = rode[0]
        if l > 0:
            pending = _direct_exchange_rider([g["w_in"]])
            dx, g["pre_norm_g"], _ = _layer_bwd_input(l, dx, pieces, saved[l], weights, lw[l])
        else:
            (other,) = _run_rider(_swap_rider([g["w_in"]]), "grad_swap")
            summed = _pair_add(g["w_in"], other, core, "grad_pair_add")
            dx, g["pre_norm_g"], (recv_in[l],) = _layer_bwd_input(l, dx, pieces, saved[l], weights, lw[l],
                                                                  _chip_exchange_rider([summed]))
        grads[l] = g
    small = {k: jnp.stack([grads[l][k] for l in range(depth)]) for k in SMALL if k != "lower_bounds"}
    small["lower_bounds"] = _lower_bound_bwd(lower_bounds, jnp.stack([grads[l]["lbs"] for l in range(depth)]),
                                             "lower_bound_bwd")
    (r_small,) = _run_rider(_gather_rider([_pack_small(small, loss_here)]), "small_grads_gather")

    res_in, _ = _sum_adamw(recv_in, w_in, m_w_in, v_w_in, "adamw_w_in")
    res_out, _ = _sum_adamw([grads[l]["w_out_received"] for l in range(depth)], w_out, m_w_out, v_w_out, "adamw_w_out")
    res_small, _ = _sum_adamw([r_small], _pack_small(weights)[None], _pack_small(mom_m)[None], _pack_small(mom_v)[None],
                              "adamw_small")
    loss = res_small[0][0][_small_size(weights) // SMALL_LANES, _small_size(weights) % SMALL_LANES]

    names = ("lower_bounds", "pre_norm_g", "w_in", "hgrn_norm_g", "fox_f_bias", "pool_w", "pool_scale", "w_out", "post_norm_g")
    outs = [loss, dx.reshape(nb, t, d)]
    for i in range(4):
        full = dict(_unpack_small(res_small[i][0], weights), w_in=res_in[i], w_out=res_out[i])
        outs += [full[k] for k in names]
    return tuple(outs)
```

```python
import functools

import jax
import jax.numpy as jnp
from jax import lax
from jax.experimental import pallas as pl
from jax.experimental.pallas import tpu as pltpu

F32, BF16 = jnp.float32, jnp.bfloat16
MESH = pl.DeviceIdType.MESH
N_DEV = 8

NORM_EPS = 1e-6
MASK_VALUE = -1e30
TINY = 1e-30
CHUNK = 64
SUB = 16
HGRN_W, POOL_W, FOX_W = 256, 256, 512
HEAD = 64
FOX_HEADS = 8
POOL_WINDOWS = (2, 4, 8, 16)
POOL_HALO = 16
MAIN_W = 3584
FC_PAD = 128
C_QA, C_FA, C_IA, C_GA, C_UB, C_GB, C_QC, C_KC, C_VC, C_GC = 0, 256, 512, 768, 1024, 1280, 1536, 2048, 2560, 3072
FOX_SCALE = HEAD ** -0.5

ADAM_LR, ADAM_B1, ADAM_B2, ADAM_EPS, ADAM_WD, ADAM_STEP = 0.001, 0.9, 0.999, 1e-08, 0.01, 10

VMEM_LIMIT = 56 * 1024 * 1024


def _pc(fn, name, **kw):
    return pl.pallas_call(fn, name=name, **kw)


def _params(**kw):
    return pltpu.CompilerParams(vmem_limit_bytes=VMEM_LIMIT, **kw)


class _Rider:
    def __init__(self, inputs, out_shapes, n_sems, n_local, plan):
        self.inputs, self.out_shapes, self.n_sems, self.n_local, self.plan = list(inputs), list(out_shapes), n_sems, n_local, plan

    def start(self, ins, outs, *sems):
        sends, _, locs = self.plan(ins, outs, *sems)
        for cp in locs + sends:
            cp.start()

    def wait(self, ins, outs, *sems):
        sends, recvs, locs = self.plan(ins, outs, *sems)
        for cp in recvs:
            cp.wait_recv()
        for cp in sends:
            cp.wait_send()
        for cp in locs:
            cp.wait()

    def sem_shapes(self):
        return [pltpu.SemaphoreType.DMA((self.n_sems,)), pltpu.SemaphoreType.DMA((self.n_sems,)),
                pltpu.SemaphoreType.DMA((self.n_local,))]


def _call(body, name, args, rider=None, *, grid, in_specs, out_specs, out_shape, scratch_shapes=(), **kw):
    if rider is None:
        res = _pc(body, name, grid=grid, in_specs=in_specs, out_specs=out_specs, out_shape=out_shape,
                  scratch_shapes=list(scratch_shapes), **kw)(*args)
        return res, None
    n_in, n_out, n_scr = len(in_specs), len(out_specs), len(scratch_shapes)
    n_rin, n_rout = len(rider.inputs), len(rider.out_shapes)

    def ridden(*refs):
        ins, refs = refs[:n_in], refs[n_in:]
        rins, refs = refs[:n_rin], refs[n_rin:]
        outs, refs = refs[:n_out], refs[n_out:]
        routs, refs = refs[:n_rout], refs[n_rout:]
        scr, sems = refs[:n_scr], refs[n_scr:]
        first = functools.reduce(jnp.logical_and, [pl.program_id(a) == 0 for a in range(len(grid))])
        last = functools.reduce(jnp.logical_and, [pl.program_id(a) == g - 1 for a, g in enumerate(grid)])

        @pl.when(first)
        def _():
            rider.start(rins, routs, *sems)

        body(*ins, *outs, *scr)

        @pl.when(last)
        def _():
            rider.wait(rins, routs, *sems)

    any_spec = pl.BlockSpec(memory_space=pl.ANY)
    res = _pc(ridden, name, grid=grid, in_specs=list(in_specs) + [any_spec] * n_rin,
              out_specs=list(out_specs) + [any_spec] * n_rout, out_shape=list(out_shape) + rider.out_shapes,
              scratch_shapes=list(scratch_shapes) + rider.sem_shapes(), **kw)(*args, *rider.inputs)
    return res[:n_out], res[n_out:]


def _run_rider(rider, name):
    n_rin = len(rider.inputs)

    def body(*refs):
        ins, outs, sems = refs[:n_rin], refs[n_rin:n_rin + len(rider.out_shapes)], refs[n_rin + len(rider.out_shapes):]
        rider.start(ins, outs, *sems)
        rider.wait(ins, outs, *sems)

    any_spec = pl.BlockSpec(memory_space=pl.ANY)
    return _pc(body, name, in_specs=[any_spec] * n_rin, out_specs=[any_spec] * len(rider.out_shapes),
               out_shape=rider.out_shapes, scratch_shapes=rider.sem_shapes())(*rider.inputs)


def _dot(a, b):
    return jnp.dot(a, b, preferred_element_type=F32)


def _dot_nt(a, b):
    return lax.dot_general(a, b, (((1,), (1,)), ((), ())), preferred_element_type=F32)


def _dot_tn(a, b):
    return lax.dot_general(a, b, (((0,), (0,)), ((), ())), preferred_element_type=F32)


def _sel_dot_exact(sel, x):
    hi = x.astype(BF16)
    rest = x - hi.astype(F32)
    mid = rest.astype(BF16)
    lo = (rest - mid.astype(F32)).astype(BF16)
    sb = sel.astype(BF16)
    return _dot(sb, hi) + _dot(sb, mid) + _dot(sb, lo)


def _split2(x):
    hi = x.astype(BF16)
    return hi, (x - hi.astype(F32)).astype(BF16)


def _sel_dot(sel, x):
    hi, lo = _split2(x)
    sb = sel.astype(BF16)
    return _dot(sb, hi) + _dot(sb, lo)


def _dot_sel(x, sel):
    hi, lo = _split2(x)
    sb = sel.astype(BF16)
    return _dot(hi, sb) + _dot(lo, sb)


def _sigmoid(x):
    return 1.0 / (1.0 + jnp.exp(-x))


def _block_ones(n, dtype):
    r = lax.broadcasted_iota(jnp.int32, (n, n), 0) // HEAD
    c = lax.broadcasted_iota(jnp.int32, (n, n), 1) // HEAD
    return (r == c).astype(dtype)


def _sds(shape, dtype):
    return jax.ShapeDtypeStruct(shape, dtype)


def _in_proj_fwd(x, g_pre, w_main, w_fc, name, rider=None):
    n, d = x.shape
    tm = min(512, n)

    def body(x_ref, g_ref, w_ref, wf_ref, proj_ref, fc_ref, ht_ref, qt_ref, kt_ref):
        xv = x_ref[...]
        r = lax.rsqrt(jnp.mean(xv * xv, axis=-1, keepdims=True) + NORM_EPS)
        hf = xv * r * g_ref[...]
        hb = hf.astype(BF16)
        ht_ref[...] = hf.T.astype(BF16)
        for j in range(0, MAIN_W, FOX_W):
            res = _dot(hb, w_ref[:, j:j + FOX_W])
            proj_ref[:, j:j + FOX_W] = res
            if j == C_QC:
                qt_ref[...] = (res * FOX_SCALE).T.astype(BF16)
            if j == C_KC:
                kt_ref[...] = res.T.astype(BF16)
        fc_ref[...] = _dot(hb, wf_ref[...])

    def cols(rows):
        return pl.BlockSpec((rows, tm), lambda i: (0, i))

    return _call(
        body, name, (x, g_pre, w_main, w_fc), rider, grid=(n // tm,),
        in_specs=[pl.BlockSpec((tm, d), lambda i: (i, 0)), pl.BlockSpec((1, d), lambda i: (0, 0)),
                  pl.BlockSpec((d, MAIN_W), lambda i: (0, 0)), pl.BlockSpec((d, FC_PAD), lambda i: (0, 0))],
        out_specs=[pl.BlockSpec((tm, MAIN_W), lambda i: (i, 0)), pl.BlockSpec((tm, FC_PAD), lambda i: (i, 0)),
                   cols(d), cols(FOX_W), cols(FOX_W)],
        out_shape=[_sds((n, MAIN_W), F32), _sds((n, FC_PAD), F32), _sds((d, n), BF16), _sds((FOX_W, n), BF16),
                   _sds((FOX_W, n), BF16)],
        compiler_params=_params(),
    )


def _fox_decay_fwd(fc, bias, nb, name):
    n = fc.shape[0]
    t = n // nb
    tt = min(256, t)
    nt = t // tt

    def body(fc_ref, b_ref, c_ref, cr_ref):
        r = lax.broadcasted_iota(jnp.int32, (tt, tt), 0)
        cc = lax.broadcasted_iota(jnp.int32, (tt, tt), 1)
        carry = jnp.zeros((1, FC_PAD), F32)
        for i in range(nt):
            rows = slice(i * tt, (i + 1) * tt)
            xv = fc_ref[rows, :] + b_ref[...]
            lf = jnp.minimum(xv, 0.0) - jnp.log(1.0 + jnp.exp(-jnp.abs(xv)))
            cs = _sel_dot_exact(r >= cc, lf) + carry
            c_ref[rows, :] = cs
            cr_ref[:, rows] = cs.T[:FOX_HEADS, :]
            carry = cs[tt - 1:tt, :]

    return _pc(
        body, name, grid=(nb,),
        in_specs=[pl.BlockSpec((t, FC_PAD), lambda b: (b, 0)), pl.BlockSpec((1, FC_PAD), lambda b: (0, 0))],
        out_specs=[pl.BlockSpec((t, FC_PAD), lambda b: (b, 0)), pl.BlockSpec((None, FOX_HEADS, t), lambda b: (b, 0, 0))],
        out_shape=[_sds((n, FC_PAD), F32), _sds((nb, FOX_HEADS, t), F32)],
        compiler_params=_params(),
    )(fc, bias)


def _hgrn_gates(q, z, lb):
    sig = _sigmoid(z)
    sn = _sigmoid(-z)
    f = lb + (1.0 - lb) * sig
    g = jnp.log(jnp.maximum(f, TINY))
    k = (1.0 - lb) * sn
    sq = _sigmoid(q)
    return sig, sn, f, g, k, sq


def _sub_tri(n, lower):
    r = lax.broadcasted_iota(jnp.int32, (n, n), 0)
    c = lax.broadcasted_iota(jnp.int32, (n, n), 1)
    tri = (r >= c) if lower else (r <= c)
    return jnp.logical_and(r // SUB == c // SUB, tri).astype(F32)


def _live_rows(t):
    return 8 * (t // 8 + 1)


def _pad_rows(x):
    return x if x.shape[0] == SUB else jnp.concatenate([x, jnp.zeros((SUB - x.shape[0], x.shape[1]), x.dtype)], axis=0)


def _hgrn_decays(qs, k, b):
    srow = lax.broadcasted_iota(jnp.int32, (SUB, HGRN_W), 0)
    es, ws = [], []
    for t in range(SUB):
        r = _live_rows(t)
        e = jnp.where(srow[:r] <= t, jnp.exp(b[t:t + 1, :] - b[:r]), 0.0)
        es.append(e)
        ws.append(_pad_rows(e * (qs[t:t + 1, :] * k[:r])))
    return srow, es, ws


def _hgrn_state_step(st, k, v, b, bmask):
    bl = b[SUB - 1:SUB, :]
    ktil = k * jnp.exp(bl - b)
    return st * jnp.exp(bl) + _dot_tn(v.astype(BF16), ktil.astype(BF16)) * bmask


def _hgrn_tile(t):
    return min(256, t)


def _hgrn_fwd(proj, lb, ones_b, nb, name, rider=None):
    n = proj.shape[0]
    t = n // nb
    tt = _hgrn_tile(t)
    nt = t // tt
    ncs = tt // CHUNK
    w = HGRN_W

    def body(q_ref, z_ref, v_ref, lb_ref, ones_ref, o_ref, s0_ref, st_s, b_s, qs_s, k_s):
        @pl.when(pl.program_id(0) == 0)
        def _():
            st_s[...] = jnp.zeros_like(st_s)

        for e in range(nb):
            q = q_ref[e]
            _, _, _, g, k, sq = _hgrn_gates(q, z_ref[e], lb_ref[...])
            b_s[e] = _sel_dot(_sub_tri(tt, True), g)
            qs_s[e] = q * sq
            k_s[e] = k
        bmask = _block_ones(w, F32)
        ones_b = ones_ref[...]
        nsub = CHUNK // SUB

        def chunk(c, carry):
            sts = [st_s[e] for e in range(nb)]
            for e in range(nb):
                s0_ref[e, c] = sts[e]
            base = pl.multiple_of(c * CHUNK, CHUNK)
            tiles = [[(qs_s[e, pl.ds(base + u * SUB, SUB), :], k_s[e, pl.ds(base + u * SUB, SUB), :],
                       v_ref[e, pl.ds(base + u * SUB, SUB), :], b_s[e, pl.ds(base + u * SUB, SUB), :])
                      for u in range(nsub)] for e in range(nb)]
            aexps = [[] for _ in range(nb)]
            for e in range(nb):
                for qs, k, v, b in tiles[e]:
                    _, _, ws = _hgrn_decays(qs, k, b)
                    aexps[e].append(_dot(jnp.concatenate(ws, axis=0).astype(BF16), ones_b))
            inters = [[] for _ in range(nb)]
            for u in range(nsub):
                for e in range(nb):
                    qs, k, v, b = tiles[e][u]
                    inters[e].append(_dot_nt((qs * jnp.exp(b)).astype(BF16), sts[e].astype(BF16)))
                    sts[e] = _hgrn_state_step(sts[e], k, v, b, bmask)
            for e in range(nb):
                st_s[e] = sts[e]
            for e in range(nb):
                for u, ((qs, k, v, b), aexp, o) in enumerate(zip(tiles[e], aexps[e], inters[e])):
                    for t in range(SUB):
                        r = _live_rows(t)
                        row = o[t:t + 1, :] + jnp.sum(aexp[t * SUB:t * SUB + r, :] * v[:r], axis=0, keepdims=True)
                        o_ref[e, pl.ds(base + u * SUB + t, 1), :] = row
            return carry

        lax.fori_loop(0, ncs, chunk, 0)

    def col(j):
        return pl.BlockSpec((nb, tt, w), lambda i: (0, i, j))

    proj3 = proj.reshape(nb, t, proj.shape[1])
    (o, s0), rode = _call(
        body, name, (proj3, proj3, proj3, lb, ones_b), rider, grid=(nt,),
        in_specs=[col(C_QA // w), col(C_FA // w), col(C_IA // w), pl.BlockSpec((1, w), lambda i: (0, 0)),
                  pl.BlockSpec((w, w), lambda i: (0, 0))],
        out_specs=[pl.BlockSpec((nb, tt, w), lambda i: (0, i, 0)),
                   pl.BlockSpec((nb, ncs, w, w), lambda i: (0, i, 0, 0))],
        out_shape=[_sds((nb, t, w), F32), _sds((nb, t // CHUNK, w, w), F32)],
        scratch_shapes=[pltpu.VMEM((nb, w, w), F32)] + [pltpu.VMEM((nb, tt, w), F32)] * 3,
        compiler_params=_params(),
    )
    return (o.reshape(n, w), s0.reshape(n // CHUNK, w, w)), rode


def _pool_lane_windows():
    lane = lax.broadcasted_iota(jnp.int32, (1, POOL_W), 1) // HEAD
    wl = jnp.zeros((1, POOL_W), F32)
    for gi, win in enumerate(POOL_WINDOWS):
        wl = jnp.where(lane == gi, float(win), wl)
    return lane, wl


def _pool_select(lane, parts):
    out = parts[-1]
    for gi in range(len(parts) - 2, -1, -1):
        out = jnp.where(lane == gi, parts[gi], out)
    return out


def _pool_mix(u, halo, t0, tt):
    lane, wl = _pool_lane_windows()
    ext = jnp.concatenate([halo, u], axis=0)
    sums, cur, shift = [], ext, 1
    for _ in POOL_WINDOWS:
        cur = cur + pltpu.roll(cur, shift, axis=0)
        sums.append(cur[POOL_HALO:, :])
        shift *= 2
    tpos = (t0 + lax.broadcasted_iota(jnp.int32, (tt, POOL_W), 0)).astype(F32)
    cnt = jnp.minimum(tpos + 1.0, wl)
    return _pool_select(lane, sums) / cnt - u, cnt


def _rows_reduce(x, op, final):
    while x.shape[0] > 8 and x.shape[0] % 16 == 0:
        half = x.shape[0] // 2
        x = op(x[:half], x[half:])
    return final(x, axis=0, keepdims=True)


def _tri_pair(step, n, group=1):
    counts = [a // group + 1 for a in range(n)]
    firsts = [sum(counts[:a]) for a in range(1, n)]
    a = sum([(step >= f).astype(jnp.int32) for f in firsts], jnp.int32(0))
    first = sum([jnp.where(step >= f, c, 0) for f, c in zip(firsts, counts)], jnp.int32(0))
    return a, step - first


def _tri_steps(n, group=1):
    return sum(a // group + 1 for a in range(n))


def _lane_lo():
    return lax.broadcasted_iota(jnp.int32, (1, 2 * HEAD), 1) < HEAD


def _put_col(tile, hh, colv):
    lane = lax.broadcasted_iota(jnp.int32, tile.shape, 1)
    return jnp.where(lane == hh, colv, tile)


def _fox_fwd(proj, kt, c_col, c_row, nb, name, rider=None):
    n = proj.shape[0]
    t = n // nb
    tb = min(256, t)
    nq = t // tb
    pw = 2 * HEAD
    kw = 2 if nq % 2 == 0 else 1
    nk = nq // kw

    def body(*refs):
        q_ref, kt_refs, (v_ref, cc_ref, cr_ref, o_ref, lse_ref, m_s, acc_s, cq_s) = refs[0], refs[1:1 + nb], refs[1 + nb:]
        qi, kk = _tri_pair(pl.program_id(0), nq, kw)

        @pl.when(kk == 0)
        def _():
            m_s[...] = jnp.full_like(m_s, -jnp.inf)
            acc_s[...] = jnp.zeros_like(acc_s)
            for e in range(nb):
                for hh in range(FOX_HEADS):
                    cq_s[e, hh] = jnp.broadcast_to(cc_ref[e, :, hh:hh + 1], (tb, pw))

        def block(masked, sub):
            lo = _lane_lo()
            keys = slice(sub * tb, (sub + 1) * tb)
            if masked:
                causal = lax.broadcasted_iota(jnp.int32, (tb, tb), 0) >= lax.broadcasted_iota(jnp.int32, (tb, tb), 1)
            def lanes(hh):
                return lo if hh % 2 == 0 else jnp.logical_not(lo)

            def scores(hh, e):
                sl = slice((hh // 2) * pw, (hh // 2 + 1) * pw)
                return _dot(jnp.where(lanes(hh), q_ref[e, :, sl] * FOX_SCALE, 0.0).astype(BF16), kt_refs[e][sl, keys])

            order = ([(hh, e) for e in range(nb) for hh in range(FOX_HEADS)] if masked else
                     [(hh, e) for hh in range(FOX_HEADS) for e in range(nb)])
            ahead = scores(*order[0])
            for j, (hh, e) in enumerate(order):
                s = ahead
                if j + 1 < len(order):
                    ahead = scores(*order[j + 1])
                s = s + (jnp.tile(cq_s[e, hh], (1, tb // pw)) - cr_ref[e, hh:hh + 1, keys])
                if masked:
                    s = jnp.where(causal, s, MASK_VALUE)
                m_prev = m_s[e, hh]
                m_new = jnp.maximum(m_prev, jnp.max(s, axis=1, keepdims=True))
                alpha = jnp.exp(m_prev - m_new)
                pe = jnp.exp(s - jnp.tile(m_new, (1, tb // pw)))
                m_s[e, hh] = m_new
                vf = v_ref[e, keys, (hh // 2) * pw:(hh // 2 + 1) * pw]
                acc_s[e, hh] = alpha * acc_s[e, hh] + _dot(pe.astype(BF16), jnp.where(lanes(hh), vf, 1.0).astype(BF16))

        def finish():
            lo = _lane_lo()
            for e in range(nb):
                m_all, l_all = jnp.zeros((tb, FC_PAD), F32), jnp.ones((tb, FC_PAD), F32)
                for p in range(FOX_HEADS // 2):
                    a0, a1 = acc_s[e, 2 * p], acc_s[e, 2 * p + 1]
                    both = pltpu.roll(jnp.where(lo, a1, a0), HEAD, axis=1)
                    o_ref[e, :, p * pw:(p + 1) * pw] = jnp.where(lo, a0, a1) / both
                    m_all = _put_col(_put_col(m_all, 2 * p, m_s[e, 2 * p]), 2 * p + 1, m_s[e, 2 * p + 1])
                    l_all = _put_col(_put_col(l_all, 2 * p, both), 2 * p + 1, a1)
                lse_ref[e] = (m_all + jnp.log(l_all)).T[:FOX_HEADS, :]

        for sub in range(kw):
            @pl.when(kk * kw + sub < qi)
            def _(sub=sub):
                block(False, sub)

            @pl.when(kk * kw + sub == qi)
            def _(sub=sub):
                block(True, sub)

        @pl.when(kk == qi // kw)
        def _():
            finish()

    def qspec(wd, j):
        return pl.BlockSpec((nb, tb, wd), lambda st: (0, _tri_pair(st, nq, kw)[0], j))

    def ktspec(e):
        return pl.BlockSpec((FOX_W, kw * tb), lambda st: (0, e * nk + _tri_pair(st, nq, kw)[1]))

    proj3 = proj.reshape(nb, t, proj.shape[1])
    (o, lse), rode = _call(
        body, name, (proj3, *[kt] * nb, proj3, c_col.reshape(nb, t, FC_PAD), c_row), rider, grid=(_tri_steps(nq, kw),),
        in_specs=[qspec(FOX_W, C_QC // FOX_W)] + [ktspec(e) for e in range(nb)] + [
            pl.BlockSpec((nb, kw * tb, FOX_W), lambda st: (0, _tri_pair(st, nq, kw)[1], C_VC // FOX_W)), qspec(FC_PAD, 0),
            pl.BlockSpec((nb, FOX_HEADS, kw * tb), lambda st: (0, 0, _tri_pair(st, nq, kw)[1]))],
        out_specs=[qspec(FOX_W, 0), pl.BlockSpec((nb, FOX_HEADS, tb), lambda st: (0, 0, _tri_pair(st, nq, kw)[0]))],
        out_shape=[_sds((nb, t, FOX_W), F32), _sds((nb, FOX_HEADS, t), F32)],
        scratch_shapes=[pltpu.VMEM((nb, FOX_HEADS, tb, pw), F32)] * 3,
        compiler_params=_params(),
    )
    return (o.reshape(n, FOX_W), lse), rode


def _head_mean(x, ones_f):
    return _dot_sel(x, ones_f) * (1.0 / HEAD)


def _merge_fwd(x, proj, o_h, o_c, gh, wbd, scale, w_out, g_post, nb, name, target=None):
    n, d = x.shape
    t = n // nb
    tm = min(512, t)
    nt = t // tm
    nhb = tm // POOL_HALO

    def body(*refs):
        x_ref, ga_ref, gc_ref, oh_ref, u_ref, gb_ref, h_ref, oc_ref, gh_ref, wb_ref, s_ref, w_ref, gp_ref = refs[:13]
        if target is None:
            xo_ref, mixt_ref, y_ref = refs[13:]
        else:
            t_ref, dx_ref, sq_ref, mixt_ref, y_ref = refs[13:]
        oh = oh_ref[...]
        ones_f = _block_ones(HGRN_W, F32)
        na = oh * lax.rsqrt(_head_mean(oh * oh, ones_f) + NORM_EPS) * gh_ref[...]
        ga, gb, gc = ga_ref[...], gb_ref[...], gc_ref[...]
        ti = pl.program_id(0) % nt
        pooled, _ = _pool_mix(u_ref[...], jnp.where(ti == 0, 0.0, h_ref[...]), ti * tm, tm)
        ob = _dot(pooled.astype(BF16), wb_ref[...]) * s_ref[...] * (gb * _sigmoid(gb))
        mixed = jnp.concatenate([na * (ga * _sigmoid(ga)), ob, oc_ref[...] * (gc * _sigmoid(gc))], axis=1)
        mixt_ref[...] = mixed.T.astype(BF16)
        y = _dot(mixed.astype(BF16), w_ref[...])
        y_ref[...] = y
        xn = x_ref[...] + y * lax.rsqrt(jnp.mean(y * y, axis=-1, keepdims=True) + NORM_EPS) * gp_ref[...]
        if target is None:
            xo_ref[...] = xn
        else:
            @pl.when(pl.program_id(0) == 0)
            def _():
                sq_ref[...] = jnp.zeros_like(sq_ref)

            e = xn - t_ref[...]
            dx_ref[...] = e * (1.0 / d)
            sq_ref[...] += jnp.sum(e * e, axis=0, keepdims=True)

    def row(wd, j=0):
        return pl.BlockSpec((tm, wd), lambda i: (i, j))

    def full(a, b):
        return pl.BlockSpec((a, b), lambda i: (0, 0))

    head = [] if target is None else [target]
    res = _pc(
        body, name, grid=(n // tm,),
        in_specs=[row(d), row(HGRN_W, C_GA // HGRN_W), row(FOX_W, C_GC // FOX_W), row(HGRN_W),
                  row(POOL_W, C_UB // POOL_W), row(POOL_W, C_GB // POOL_W),
                  pl.BlockSpec((POOL_HALO, POOL_W), lambda i: (jnp.maximum(i * nhb - 1, 0), C_UB // POOL_W)), row(FOX_W),
                  full(1, HGRN_W), full(POOL_W, POOL_W), full(1, POOL_W), full(d, d), full(1, d)] + [row(d)] * len(head),
        out_specs=[row(d)] + [full(1, d)] * len(head) + [pl.BlockSpec((d, tm), lambda i: (0, i)), row(d)],
        out_shape=[_sds((n, d), F32)] + [_sds((1, d), F32)] * len(head) + [_sds((d, n), BF16), _sds((n, d), F32)],
        compiler_params=_params(),
    )(x, proj, proj, o_h, proj, proj, proj, o_c, gh, wbd, scale, w_out, g_post, *head)
    return (res[0], res[1], res[2]) if target is None else ((res[0], res[1]), res[2], res[3])


def _rms_bwd(dy_scaled, xhat, r):
    return r * (dy_scaled - xhat * jnp.mean(dy_scaled * xhat, axis=-1, keepdims=True))


def _merge_bwd(dxo, y, g_post, w_out_t, proj, o_c, mixt, nb, name):
    n, d = y.shape
    t = n // nb
    tm = min(512, t)
    nt = t // tm
    wab = HGRN_W + POOL_W
    rows = d // N_DEV

    def body(dx_ref, y_ref, gp_ref, wt_ref, gc_ref, oc_ref, mt_ref,
             gw_ref, dm_ref, dgp_ref, da_ref, dat_ref, dg_ref, dl_ref, gw_s):
        @pl.when(pl.program_id(0) == 0)
        def _():
            dgp_ref[...] = jnp.zeros_like(dgp_ref)
            gw_s[...] = jnp.zeros_like(gw_s)

        yv, dxv = y_ref[...], dx_ref[...]
        r = lax.rsqrt(jnp.mean(yv * yv, axis=-1, keepdims=True) + NORM_EPS)
        yh = yv * r
        dgp_ref[...] += jnp.sum(dxv * yh, axis=0, keepdims=True)
        dyb = _rms_bwd(dxv * gp_ref[...], yh, r).astype(BF16)
        gw_s[...] += _dot(mt_ref[...], dyb)

        @pl.when(pl.program_id(0) == n // tm - 1)
        def _():
            for j in range(N_DEV):
                gw_ref[j % 2, j // 2] = gw_s[j * rows:(j + 1) * rows, :].astype(BF16)

        dm_ref[...] = _dot(dyb, wt_ref[:, :wab])
        dmc = _dot(dyb, wt_ref[:, wab:])
        gc, oc = gc_ref[...], oc_ref[...]
        sg = _sigmoid(gc)
        da = dmc * (gc * sg)
        da_ref[...] = da.astype(BF16)
        dat_ref[...] = da.T.astype(BF16)
        dg_ref[...] = (dmc * oc * (sg * (1.0 + gc * (1.0 - sg)))).astype(BF16)
        rr = lax.broadcasted_iota(jnp.int32, (FOX_W, FC_PAD), 0) // HEAD
        cc = lax.broadcasted_iota(jnp.int32, (FOX_W, FC_PAD), 1)
        dl_ref[...] = _dot_sel(da * oc, (rr == cc).astype(F32)).T[:FOX_HEADS, :]

    def row(wd, j=0):
        return pl.BlockSpec((tm, wd), lambda i: (i, j))

    def full(a, b):
        return pl.BlockSpec((a, b), lambda i: (0, 0))

    return _pc(
        body, name, grid=(n // tm,),
        in_specs=[row(d), row(d), full(1, d), full(d, d), row(FOX_W, C_GC // FOX_W), row(FOX_W),
                  pl.BlockSpec((d, tm), lambda i: (0, i))],
        out_specs=[pl.BlockSpec((2, N_DEV // 2, rows, d), lambda i: (0, 0, 0, 0)), row(wab), full(1, d), row(FOX_W),
                   pl.BlockSpec((FOX_W, tm), lambda i: (0, i)), row(FOX_W),
                   pl.BlockSpec((None, FOX_HEADS, tm), lambda i: (i // nt, 0, i % nt))],
        out_shape=[_sds((2, N_DEV // 2, rows, d), BF16), _sds((n, wab), F32), _sds((1, d), F32), _sds((n, FOX_W), BF16),
                   _sds((FOX_W, n), BF16), _sds((n, FOX_W), BF16), _sds((nb, FOX_HEADS, t), F32)],
        scratch_shapes=[pltpu.VMEM((d, d), F32)],
        compiler_params=_params(),
    )(dxo, y, g_post, w_out_t, proj, o_c, mixt)


def _w_in_grad(ht, pieces, dmix, proj, wbd, wbd_t, scale, nb, name):
    d, n = ht.shape
    t = n // nb
    ta, tk = d, min(512, t)
    nk, nt, nhb = n // tk, t // tk, tk // POOL_HALO
    given = [p for p, _ in pieces if p is not None]
    widths = [2 * POOL_W if p is None else p.shape[1] for p, _ in pieces]
    offs = [sum(widths[:i]) for i in range(len(widths))]
    in_w = MAIN_W + FOX_HEADS
    shard = in_w // N_DEV
    cu, cg, cm = C_UB // POOL_W, C_GB // POOL_W, HGRN_W // POOL_W

    def body(*refs):
        a_ref, p_refs = refs[0], list(refs[1:1 + len(given)])
        pool_refs = refs[1 + len(given):10 + len(given)]
        o_ref, db_ref, dw_ref, ds_ref, acc = refs[10 + len(given):]
        k = pl.program_id(1)

        @pl.when(k == 0)
        def _():
            dw_ref[...] = jnp.zeros_like(dw_ref)
            ds_ref[...] = jnp.zeros_like(ds_ref)
            acc[...] = jnp.zeros_like(acc)

        db, dw, dscale = _pool_bwd_tile(*pool_refs, k % nt, nt, tk)
        dw_ref[...] += dw
        ds_ref[...] += dscale
        db_ref[...] = db
        a = a_ref[...]
        for (p, _), off, wd in zip(pieces, offs, widths):
            pr = db if p is None else p_refs.pop(0)
            for j in range(0, wd, 512):
                jw = min(512, wd - j)
                acc[:, off + j:off + j + jw] += _dot(a, pr[:, j:j + jw])

        @pl.when(k == nk - 1)
        def _():
            for j in range(N_DEV):
                o_ref[j % 2, j // 2] = acc[:, j * shard:(j + 1) * shard].astype(BF16)

    def after(k):
        return jnp.minimum((k + 1) * nhb, n // POOL_HALO - 1)

    def tile(j):
        return pl.BlockSpec((tk, POOL_W), lambda i, k: (k, j))

    def full(r, c):
        return pl.BlockSpec((r, c), lambda i, k: (0, 0))

    pool_specs = [tile(cu), tile(cg),
                  pl.BlockSpec((POOL_HALO, POOL_W), lambda i, k: (jnp.maximum(k * nhb - 1, 0), cu)), tile(cm),
                  pl.BlockSpec((POOL_HALO, POOL_W), lambda i, k: (after(k), cg)),
                  pl.BlockSpec((POOL_HALO, POOL_W), lambda i, k: (after(k), cm)),
                  full(POOL_W, POOL_W), full(POOL_W, POOL_W), full(1, POOL_W)]
    return _pc(
        body, name, grid=(d // ta, nk),
        in_specs=[pl.BlockSpec((ta, tk), lambda i, k: (i, k))]
        + [pl.BlockSpec((tk, p.shape[1]), lambda i, k: (k, 0)) for p in given] + pool_specs,
        out_specs=[pl.BlockSpec((2, N_DEV // 2, ta, shard), lambda i, k: (0, 0, i, 0)),
                   pl.BlockSpec((tk, 2 * POOL_W), lambda i, k: (k, 0)), full(POOL_W, POOL_W), full(1, POOL_W)],
        out_shape=[_sds((2, N_DEV // 2, d, shard), BF16), _sds((n, 2 * POOL_W), BF16), _sds((POOL_W, POOL_W), F32),
                   _sds((1, POOL_W), F32)],
        scratch_shapes=[pltpu.VMEM((ta, sum(widths)), F32)],
        compiler_params=_params(),
    )(ht, *given, proj, proj, proj, dmix, proj, dmix, wbd, wbd_t, scale)


def _hgrn_state_bwd(qs, k, v, b, do, s0, ds1, bmask):
    bl = b[SUB - 1:SUB, :]
    eb, ebl, ekt = jnp.exp(b), jnp.exp(bl), jnp.exp(bl - b)
    qe, ktil = qs * eb, k * ekt
    ds1b, dob = ds1.astype(BF16), do.astype(BF16)
    dv = _dot_nt(ktil.astype(BF16), ds1b)
    dqe = _dot(dob, s0.astype(BF16))
    dktil = _dot(v.astype(BF16), ds1b)
    dbl = jnp.sum(dktil * ktil, axis=0, keepdims=True) + ebl * jnp.sum(s0 * ds1, axis=0, keepdims=True)
    ds0 = ds1 * ebl + _dot_tn(dob, qe.astype(BF16)) * bmask
    return dqe * eb, dktil * ekt, dv, dbl, ds0


def _hgrn_intra_bwd(qs, k, v, do, es, aexp, gexp, dq, dk, dv, put_dq_row):
    dks = [dk[j:j + 8] for j in range(0, SUB, 8)]
    dvs = [dv[j:j + 8] for j in range(0, SUB, 8)]
    for t in range(SUB):
        r = _live_rows(t)
        ge = gexp[t * SUB:t * SUB + r, :] * es[t]
        put_dq_row(t, dq[t:t + 1, :] + jnp.sum(ge * k[:r], axis=0, keepdims=True))
        for j in range(r // 8):
            dks[j] = dks[j] + ge[8 * j:8 * j + 8] * qs[t:t + 1, :]
            dvs[j] = dvs[j] + aexp[t * SUB + 8 * j:t * SUB + 8 * j + 8, :] * do[t:t + 1, :]
    return jnp.concatenate(dks, axis=0), jnp.concatenate(dvs, axis=0)


def _hgrn_bwd(dmix, proj, o_h, s0, gh, lb, ones_b, nb, name, rider=None):
    n = proj.shape[0]
    t = n // nb
    tt = _hgrn_tile(t)
    nt = t // tt
    ncs = tt // CHUNK
    nsub = CHUNK // SUB
    w = HGRN_W

    def body(dm_ref, q_ref, z_ref, v_ref, ga_ref, oh_ref, s0_ref, gh_ref, lb_ref, ones_ref,
             dp_ref, dgh_ref, dlb_ref, ds_s, ss_s, b_s, qs_s, k_s, do_s, dq_s, dk_s, dv_s, dbl_s):
        @pl.when(pl.program_id(0) == 0)
        def _():
            dgh_ref[...] = jnp.zeros_like(dgh_ref)
            dlb_ref[...] = jnp.zeros_like(dlb_ref)
            ds_s[...] = jnp.zeros_like(ds_s)

        ones_b = ones_ref[...]
        ones_f = ones_b.astype(F32)
        bmask = _block_ones(w, F32)
        lbv, ghv = lb_ref[...], gh_ref[...]
        kept = []
        for e in range(nb):
            oh, ga, dm = oh_ref[e], ga_ref[e], dm_ref[e]
            rn = lax.rsqrt(_head_mean(oh * oh, ones_f) + NORM_EPS)
            nh = oh * rn
            sga = _sigmoid(ga)
            dp_ref[e, :, 3 * w:4 * w] = (dm * nh * ghv * (sga * (1.0 + ga * (1.0 - sga)))).astype(BF16)
            dn = dm * (ga * sga)
            dgh_ref[...] += jnp.sum(dn * nh, axis=0, keepdims=True)
            dn = dn * ghv
            do_s[e] = rn * (dn - nh * _head_mean(dn * nh, ones_f))
            q = q_ref[e]
            sig, sn, f, g, k, sq = _hgrn_gates(q, z_ref[e], lbv)
            qs = q * sq
            b_s[e] = _sel_dot(_sub_tri(tt, True), g)
            qs_s[e] = qs
            k_s[e] = k
            kept.append((q, sig, sn, f, k, sq, qs))

        def chunk(cc, carry):
            c = ncs - 1 - cc
            base = pl.multiple_of(c * CHUNK, CHUNK)
            tiles = [[(qs_s[e, pl.ds(base + u * SUB, SUB), :], k_s[e, pl.ds(base + u * SUB, SUB), :],
                       v_ref[e, pl.ds(base + u * SUB, SUB), :], b_s[e, pl.ds(base + u * SUB, SUB), :],
                       do_s[e, pl.ds(base + u * SUB, SUB), :]) for u in range(nsub)] for e in range(nb)]
            sts = [s0_ref[e, c] for e in range(nb)]
            for u in range(nsub):
                for e in range(nb):
                    qs, k, v, b, do = tiles[e][u]
                    ss_s[e, u] = sts[e]
                    if u < nsub - 1:
                        sts[e] = _hgrn_state_step(sts[e], k, v, b, bmask)
            dss = [ds_s[e] for e in range(nb)]
            for u in reversed(range(nsub)):
                for e in range(nb):
                    qs, k, v, b, do = tiles[e][u]
                    _, es, ws = _hgrn_decays(qs, k, b)
                    gs = [_pad_rows(do[t:t + 1, :] * v[:_live_rows(t)]) for t in range(SUB)]
                    aexp = _dot(jnp.concatenate(ws, axis=0).astype(BF16), ones_b)
                    gexp = _dot(jnp.concatenate(gs, axis=0).astype(BF16), ones_b)
                    dq, dk, dv, dbl, dss[e] = _hgrn_state_bwd(qs, k, v, b, do, ss_s[e, u], dss[e], bmask)

                    def put_dq_row(i, row, e=e, r0=base + u * SUB):
                        dq_s[e, pl.ds(r0 + i, 1), :] = row

                    dk, dv = _hgrn_intra_bwd(qs, k, v, do, es, aexp, gexp, dq, dk, dv, put_dq_row)
                    dk_s[e, pl.ds(base + u * SUB, SUB), :] = dk
                    dv_s[e, pl.ds(base + u * SUB, SUB), :] = dv
                    dbl_s[e, pl.ds(base + u * SUB, SUB), :] = jnp.broadcast_to(dbl, (SUB, w))
            for e in range(nb):
                ds_s[e] = dss[e]
            return carry

        lax.fori_loop(0, ncs, chunk, 0)
        for e, (q, sig, sn, f, k, sq, qs) in enumerate(kept):
            dqs, dk = dq_s[e], dk_s[e]
            dg = _sel_dot(_sub_tri(tt, False), qs * dqs - k * dk) + dbl_s[e]
            dfz = jnp.where(f > TINY, dg / jnp.maximum(f, TINY), 0.0)
            dlb_ref[...] += jnp.sum(dfz * (1.0 - sig) - dk * sn, axis=0, keepdims=True)
            dp_ref[e, :, 0:w] = (dqs * (sq * (1.0 + q * (1.0 - sq)))).astype(BF16)
            dp_ref[e, :, w:2 * w] = ((dfz - dk) * (1.0 - lbv) * sig * sn).astype(BF16)
            dp_ref[e, :, 2 * w:3 * w] = dv_s[e].astype(BF16)

    def col(j):
        return pl.BlockSpec((nb, tt, w), lambda i: (0, nt - 1 - i, j))

    def full(a, bb):
        return pl.BlockSpec((a, bb), lambda i: (0, 0))

    proj3 = proj.reshape(nb, t, proj.shape[1])
    (dp, dgh, dlb), rode = _call(
        body, name, (dmix.reshape(nb, t, dmix.shape[1]), proj3, proj3, proj3, proj3, o_h.reshape(nb, t, w),
                     s0.reshape(nb, t // CHUNK, w, w), gh, lb, ones_b), rider, grid=(nt,),
        in_specs=[col(0), col(C_QA // w), col(C_FA // w), col(C_IA // w), col(C_GA // w), col(0),
                  pl.BlockSpec((nb, ncs, w, w), lambda i: (0, nt - 1 - i, 0, 0)), full(1, w), full(1, w), full(w, w)],
        out_specs=[pl.BlockSpec((nb, tt, 4 * w), lambda i: (0, nt - 1 - i, 0)), full(1, w), full(1, w)],
        out_shape=[_sds((nb, t, 4 * w), BF16), _sds((1, w), F32), _sds((1, w), F32)],
        scratch_shapes=[pltpu.VMEM((nb, w, w), F32), pltpu.VMEM((nb, nsub, w, w), F32)]
        + [pltpu.VMEM((nb, tt, w), F32)] * 8,
        compiler_params=_params(),
    )
    return (dp.reshape(n, 4 * w), dgh, dlb), rode


def _pool_bwd_tile(u_ref, g_ref, h_ref, dm_ref, gn_ref, dmn_ref, w_ref, wt_ref, s_ref, i, nt, tt):
    sc = s_ref[...]
    halo = jnp.where(i == 0, 0.0, h_ref[...])
    pooled, cnt = _pool_mix(u_ref[...], halo, i * tt, tt)
    pb = pooled.astype(BF16)
    pre = _dot(pb, w_ref[...])
    gv, dm = g_ref[...], dm_ref[...]
    sg = _sigmoid(gv)
    silu = gv * sg
    dgb = dm * pre * sc * (sg * (1.0 + gv * (1.0 - sg)))
    dscale = jnp.sum(dm * pre * silu, axis=0, keepdims=True)
    dpre = (dm * sc * silu).astype(BF16)
    dw = _dot_tn(pb, dpre)
    dpool = _dot(dpre, wt_ref[...])
    gn = gn_ref[...]
    dpre_n = (dmn_ref[...] * sc * (gn * _sigmoid(gn))).astype(BF16)
    dpool_n = jnp.where(i == nt - 1, 0.0, _dot(dpre_n, wt_ref[...]))
    lane, wl = _pool_lane_windows()
    tpos_n = ((i + 1) * tt + lax.broadcasted_iota(jnp.int32, (POOL_HALO, POOL_W), 0)).astype(F32)
    ext = jnp.concatenate([dpool / cnt, dpool_n / jnp.minimum(tpos_n + 1.0, wl)], axis=0)
    rows = tt + POOL_HALO
    sums, cur, shift = [], ext, 1
    for _ in POOL_WINDOWS:
        cur = cur + pltpu.roll(cur, rows - shift, axis=0)
        sums.append(cur[:tt, :])
        shift *= 2
    du = _pool_select(lane, sums) - dpool
    return jnp.concatenate([du, dgb], axis=1).astype(BF16), dw, dscale


def _fox_bwd(proj, qt, kt, da, dat, c_col, c_row, lse_row, delta_row, nb, name, rider=None):
    n = proj.shape[0]
    t = n // nb
    tb = min(256, t)
    nq = t // tb
    pw = 2 * HEAD
    qw = 2 if nq % 2 == 0 else 1
    nqs = nq // qw

    def body(q_ref, k_ref, v_ref, da_ref, qt_ref, kt_ref, dat_ref, cc_ref, cr_ref, lse_ref, dl_ref,
             dq_ref, dk_ref, dv_ref, dck_ref, dcq_ref, dq_s, dk_s, dv_s, dck_s, dcq_s):
        step = pl.program_id(1)
        kj, qq = pairs(step)

        @pl.when(step == 0)
        def _():
            dq_s[...] = jnp.zeros_like(dq_s)
            dcq_s[...] = jnp.zeros_like(dcq_s)

        @pl.when(qq == nqs - 1)
        def _():
            dk_s[...] = jnp.zeros_like(dk_s)
            dv_s[...] = jnp.zeros_like(dv_s)
            dck_s[...] = jnp.zeros_like(dck_s)

        def block(masked, sub):
            lo = _lane_lo()
            qi = qq * qw + sub
            qs = slice(sub * tb, (sub + 1) * tb)
            if masked:
                causal = lax.broadcasted_iota(jnp.int32, (tb, tb), 1) >= lax.broadcasted_iota(jnp.int32, (tb, tb), 0)
            dck = dck_s[...]
            for p in range(FOX_HEADS // 2):
                sl = slice(p * pw, (p + 1) * pw)
                qp = q_ref[qs, sl] * FOX_SCALE
                kp = k_ref[:, sl].astype(BF16)
                vp = v_ref[:, sl].astype(BF16)
                dap = da_ref[qs, sl]
                dk, dv = dk_s[:, sl], dv_s[:, sl]
                for h in range(2):
                    hh = 2 * p + h
                    lm = lo if h == 0 else jnp.logical_not(lo)
                    rows = slice(hh * HEAD, (hh + 1) * HEAD)
                    none = jnp.zeros((HEAD, tb), BF16)
                    qm = jnp.where(lm, qp, 0.0).astype(BF16)
                    dam = jnp.where(lm, dap, jnp.zeros_like(dap))
                    qtm = jnp.concatenate([qt_ref[rows, qs], none] if h == 0 else [none, qt_ref[rows, qs]], axis=0)
                    datm = jnp.concatenate([dat_ref[rows, qs], none] if h == 0 else [none, dat_ref[rows, qs]], axis=0)
                    s = _dot(kp, qtm) + (cr_ref[hh:hh + 1, qs] - cc_ref[:, hh:hh + 1])
                    pe = jnp.exp(s - lse_ref[hh:hh + 1, qs])
                    if masked:
                        pe = jnp.where(causal, pe, 0.0)
                    dp = _dot(vp, datm)
                    ds = pe * (dp - dl_ref[hh:hh + 1, qs])
                    dsb = ds.astype(BF16)
                    dv = dv + _dot(pe.astype(BF16), dam)
                    dk = dk + _dot(dsb, qm)
                    dq_s[qi, rows, :] += _dot(kt_ref[rows, :], dsb)
                    dck = dck - _put_col(jnp.zeros_like(dck), hh, jnp.sum(ds, axis=1, keepdims=True))
                    dcq_s[qi, hh:hh + 1, :] += _rows_reduce(ds, jnp.add, jnp.sum)
                dk_s[:, sl] = dk
                dv_s[:, sl] = dv
            dck_s[...] = dck

        for sub in reversed(range(qw)):
            @pl.when(qq * qw + sub > kj)
            def _(sub=sub):
                block(False, sub)

            @pl.when(qq * qw + sub == kj)
            def _(sub=sub):
                block(True, sub)

        @pl.when(qq == kj // qw)
        def _():
            dk_ref[...] = dk_s[...].astype(BF16)
            dv_ref[...] = dv_s[...].astype(BF16)
            dck_ref[...] = dck_s[...]

        @pl.when(step == _tri_steps(nq, qw) - 1)
        def _():
            for j in range(nq):
                dq_ref[j * tb:(j + 1) * tb, :] = (dq_s[j].T * FOX_SCALE).astype(BF16)
                dcq_ref[:, j * tb:(j + 1) * tb] = dcq_s[j]

    def pairs(step):
        a, b = _tri_pair(step, nq, qw)
        return nq - 1 - a, nqs - 1 - b

    def kspec(wd, j=0):
        return pl.BlockSpec((tb, wd), lambda b, st: (b * nq + pairs(st)[0], j))

    def qspec(wd, j=0):
        return pl.BlockSpec((qw * tb, wd), lambda b, st: (b * nqs + pairs(st)[1], j))

    def qrow():
        return pl.BlockSpec((None, FOX_HEADS, qw * tb), lambda b, st: (b, 0, pairs(st)[1]))

    def tspec(which):
        if which == 0:
            return pl.BlockSpec((FOX_W, tb), lambda b, st: (0, b * nq + pairs(st)[0]))
        return pl.BlockSpec((FOX_W, qw * tb), lambda b, st: (0, b * nqs + pairs(st)[1]))

    return _call(
        body, name, (proj, proj, proj, da, qt, kt, dat, c_col, c_row, lse_row, delta_row), rider,
        grid=(nb, _tri_steps(nq, qw)),
        in_specs=[qspec(FOX_W, C_QC // FOX_W), kspec(FOX_W, C_KC // FOX_W), kspec(FOX_W, C_VC // FOX_W), qspec(FOX_W),
                  tspec(1), tspec(0), tspec(1), kspec(FC_PAD), qrow(), qrow(), qrow()],
        out_specs=[pl.BlockSpec((t, FOX_W), lambda b, st: (b, 0)), kspec(FOX_W), kspec(FOX_W), kspec(FC_PAD),
                   pl.BlockSpec((None, FOX_HEADS, t), lambda b, st: (b, 0, 0))],
        out_shape=[_sds((n, FOX_W), BF16), _sds((n, FOX_W), BF16), _sds((n, FOX_W), BF16), _sds((n, FC_PAD), F32),
                   _sds((nb, FOX_HEADS, t), F32)],
        scratch_shapes=[pltpu.VMEM((nq, FOX_W, tb), F32), pltpu.VMEM((tb, FOX_W), F32), pltpu.VMEM((tb, FOX_W), F32),
                        pltpu.VMEM((tb, FC_PAD), F32), pltpu.VMEM((nq, FOX_HEADS, tb), F32)],
        compiler_params=_params(),
    )


def _fox_decay_bwd(dc_q, dc_k, fc, bias, nb, name):
    n = fc.shape[0]
    t = n // nb
    tt = min(256, t)
    nt = t // tt

    def body(dcq_ref, dck_ref, fc_ref, b_ref, dfc_ref, db_ref):
        @pl.when(pl.program_id(0) == 0)
        def _():
            db_ref[...] = jnp.zeros_like(db_ref)

        r = lax.broadcasted_iota(jnp.int32, (tt, tt), 0)
        cc = lax.broadcasted_iota(jnp.int32, (tt, tt), 1)
        carry = jnp.zeros((1, FC_PAD), F32)
        db = jnp.zeros((1, FC_PAD), F32)
        for i in reversed(range(nt)):
            rows = slice(i * tt, (i + 1) * tt)
            dcq = jnp.concatenate([dcq_ref[:, rows], jnp.zeros((FC_PAD - FOX_HEADS, tt), F32)], axis=0).T
            dlf = _sel_dot_exact(r <= cc, dcq + dck_ref[rows, :]) + carry
            carry = dlf[0:1, :]
            dfc = dlf * _sigmoid(-(fc_ref[rows, :] + b_ref[...]))
            dfc_ref[rows, :] = dfc.astype(BF16)
            db = db + jnp.sum(dfc, axis=0, keepdims=True)
        db_ref[...] += db

    def row():
        return pl.BlockSpec((t, FC_PAD), lambda b: (b, 0))

    return _pc(
        body, name, grid=(nb,),
        in_specs=[pl.BlockSpec((None, FOX_HEADS, t), lambda b: (b, 0, 0)), row(), row(),
                  pl.BlockSpec((1, FC_PAD), lambda b: (0, 0))],
        out_specs=[row(), pl.BlockSpec((1, FC_PAD), lambda b: (0, 0))],
        out_shape=[_sds((n, FC_PAD), BF16), _sds((1, FC_PAD), F32)],
        compiler_params=_params(),
    )(dc_q, dc_k, fc, bias)


def _in_proj_bwd(pieces, w_main_t, w_fc_t, x, g_pre, dxo, name, rider=None):
    n, d = x.shape
    tm = min(512, n)
    widths = [p.shape[1] for p, _ in pieces]
    offs = [o for _, o in pieces]
    np_ = len(pieces)

    def body(*refs):
        p_refs = refs[:np_]
        wt_ref, wf_ref, x_ref, g_ref, dxo_ref, dx_ref, dg_ref = refs[np_:]

        @pl.when(pl.program_id(0) == 0)
        def _():
            dg_ref[...] = jnp.zeros_like(dg_ref)

        dh = _dot(p_refs[-1][...], wf_ref[...])
        for pr, wd, off in zip(p_refs[:-1], widths[:-1], offs[:-1]):
            for j in range(0, wd, 512):
                jw = min(512, wd - j)
                dh = dh + _dot(pr[:, j:j + jw], wt_ref[off + j:off + j + jw, :])
        xv = x_ref[...]
        r = lax.rsqrt(jnp.mean(xv * xv, axis=-1, keepdims=True) + NORM_EPS)
        xh = xv * r
        dg_ref[...] += jnp.sum(dh * xh, axis=0, keepdims=True)
        dx_ref[...] = dxo_ref[...] + _rms_bwd(dh * g_ref[...], xh, r)

    row = pl.BlockSpec((tm, d), lambda i: (i, 0))
    return _call(
        body, name, (*[p for p, _ in pieces], w_main_t, w_fc_t, x, g_pre, dxo), rider, grid=(n // tm,),
        in_specs=[pl.BlockSpec((tm, wd), lambda i: (i, 0)) for wd in widths] + [
            pl.BlockSpec((MAIN_W, d), lambda i: (0, 0)), pl.BlockSpec((FC_PAD, d), lambda i: (0, 0)),
            row, pl.BlockSpec((1, d), lambda i: (0, 0)), row],
        out_specs=[row, pl.BlockSpec((1, d), lambda i: (0, 0))],
        out_shape=[_sds((n, d), F32), _sds((1, d), F32)],
        compiler_params=_params(),
    )


def _lower_bound_table(lower_bounds, name):
    depth, w = lower_bounds.shape

    def body(lb_ref, o_ref):
        v = lb_ref[...]
        e = jnp.exp(v - jnp.max(v, axis=0, keepdims=True))
        p = e / jnp.sum(e, axis=0, keepdims=True)
        acc = jnp.zeros((1, w), F32)
        for l in range(depth):
            acc = acc + p[l:l + 1, :]
            o_ref[l:l + 1, :] = acc - p[0:1, :]

    return _pc(body, name, out_shape=_sds((depth, w), F32))(lower_bounds)


def _lower_bound_bwd(lower_bounds, dlbs, name):
    depth, w = lower_bounds.shape

    def body(lb_ref, d_ref, o_ref):
        v, dl = lb_ref[...], d_ref[...]
        e = jnp.exp(v - jnp.max(v, axis=0, keepdims=True))
        p = e / jnp.sum(e, axis=0, keepdims=True)
        tot = jnp.sum(dl, axis=0, keepdims=True)
        rows, tail = [], tot
        for l in range(depth):
            rows.append(tail - tot if l == 0 else tail)
            tail = tail - dl[l:l + 1, :]
        dp = jnp.concatenate(rows, axis=0)
        o_ref[...] = p * (dp - jnp.sum(p * dp, axis=0, keepdims=True))

    return _pc(body, name, out_shape=_sds((depth, w), F32))(lower_bounds, dlbs)


def _place():
    x, y, c = lax.axis_index("x"), lax.axis_index("y"), lax.axis_index("c")
    return x, y, c


def _gather_weights(*arrays):
    na = len(arrays)

    def body(*refs):
        ins, outs = refs[:na], refs[na:2 * na]
        send_sems, recv_sems, local_sems = refs[2 * na:]
        x, y, c = _place()
        me, sibling = (x, y, c), (x, y, 1 - c)
        chips = [(1 - x, y), (x, 1 - y), (1 - x, 1 - y)]

        def slot(a, px, py, pc):
            return outs[a].at[4 * px + 2 * py + pc]

        def copy(a, k, block, to, own=False):
            return pltpu.make_async_remote_copy(
                src_ref=ins[a] if own else slot(a, *block), dst_ref=slot(a, *block),
                send_sem=send_sems.at[a * 7 + k], recv_sem=recv_sems.at[a * 7 + k],
                device_id=to, device_id_type=MESH)

        mine = [pltpu.make_async_copy(ins[a], slot(a, *me), local_sems.at[a]) for a in range(na)]
        for cp in mine:
            cp.start()
        first = []
        for a in range(na):
            first.append(copy(a, 0, me, sibling, own=True))
            first += [copy(a, 1 + j, me, (*chip, c), own=True) for j, chip in enumerate(chips)]
        for cp in first:
            cp.start()
        passed = []
        for j, chip in enumerate(chips):
            for a in range(na):
                copy(a, 1 + j, (*chip, c), me).wait_recv()
                fw = copy(a, 4 + j, (*chip, c), sibling)
                fw.start()
                passed.append(fw)
        for a in range(na):
            copy(a, 0, sibling, me).wait_recv()
            for j, chip in enumerate(chips):
                copy(a, 4 + j, (*chip, 1 - c), me).wait_recv()
        for cp in first + passed:
            cp.wait_send()
        for cp in mine:
            cp.wait()

    any_spec = pl.BlockSpec(memory_space=pl.ANY)
    return _pc(
        body, "gather_weights",
        in_specs=[any_spec] * na, out_specs=[any_spec] * na,
        out_shape=[_sds((N_DEV,) + a.shape, a.dtype) for a in arrays],
        scratch_shapes=[pltpu.SemaphoreType.DMA((7 * na,)), pltpu.SemaphoreType.DMA((7 * na,)),
                        pltpu.SemaphoreType.DMA((na,))],
    )(*arrays)


def _peer(k):
    x, y, c = _place()
    return (1 - x if k & 4 else x, 1 - y if k & 2 else y, 1 - c if k & 1 else c)


def _remote(src, dst, sems, s, to):
    return pltpu.make_async_remote_copy(src_ref=src, dst_ref=dst, send_sem=sems[0].at[s], recv_sem=sems[1].at[s],
                                        device_id=to, device_id_type=MESH)


def _gather_rider(shards):
    na = len(shards)

    def plan(ins, outs, *sems):
        x, y, c = _place()
        me = 4 * x + 2 * y + c
        locs = [pltpu.make_async_copy(ins[a], outs[a].at[me], sems[2].at[a]) for a in range(na)]
        sends, recvs = [], []
        for k in range(1, N_DEV):
            px, py, pc = _peer(k)
            for a in range(na):
                s = (k - 1) * na + a
                sends.append(_remote(ins[a], outs[a].at[me], sems, s, (px, py, pc)))
                recvs.append(_remote(ins[a], outs[a].at[4 * px + 2 * py + pc], sems, s, (px, py, pc)))
        return sends, recvs, locs

    return _Rider(shards, [_sds((N_DEV,) + a.shape, a.dtype) for a in shards], (N_DEV - 1) * na, na, plan)


def _direct_exchange_rider(blocks):
    na = len(blocks)

    def plan(ins, outs, *sems):
        x, y, c = _place()
        me = 4 * x + 2 * y + c
        locs = [pltpu.make_async_copy(ins[a].at[c, 2 * x + y], outs[a].at[me], sems[2].at[a]) for a in range(na)]
        sends, recvs = [], []
        for k in range(1, N_DEV):
            px, py, pc = _peer(k)
            for a in range(na):
                s = (k - 1) * na + a
                sends.append(_remote(ins[a].at[pc, 2 * px + py], outs[a].at[me], sems, s, (px, py, pc)))
                recvs.append(_remote(ins[a].at[pc, 2 * px + py], outs[a].at[4 * px + 2 * py + pc], sems, s, (px, py, pc)))
        return sends, recvs, locs

    return _Rider(blocks, [_sds((N_DEV,) + a.shape[2:], a.dtype) for a in blocks], (N_DEV - 1) * na, na, plan)


def _swap_rider(halves):
    na = len(halves)

    def plan(ins, outs, *sems):
        x, y, c = _place()
        cps = [_remote(ins[a].at[1 - c], outs[a], sems, a, (x, y, 1 - c)) for a in range(na)]
        return cps, cps, []

    return _Rider(halves, [_sds(a.shape[1:], a.dtype) for a in halves], na, 1, plan)


def _chip_exchange_rider(parts, small=None):
    na = len(parts)
    n_chip = N_DEV // 2

    def plan(ins, outs, *sems):
        x, y, c = _place()
        chip = 2 * x + y
        locs = [pltpu.make_async_copy(ins[a].at[chip], outs[a].at[chip], sems[2].at[a]) for a in range(na)]
        sends, recvs = [], []
        for k in range(1, n_chip):
            px, py, _ = _peer(2 * k)
            for a in range(na):
                s = (k - 1) * na + a
                sends.append(_remote(ins[a].at[2 * px + py], outs[a].at[chip], sems, s, (px, py, c)))
                recvs.append(_remote(ins[a].at[2 * px + py], outs[a].at[2 * px + py], sems, s, (px, py, c)))
        if small is not None:
            me = 2 * chip + c
            locs.append(pltpu.make_async_copy(ins[na], outs[na].at[me], sems[2].at[na]))
            for k in range(1, N_DEV):
                px, py, pc = _peer(k)
                s = (n_chip - 1) * na + k - 1
                sends.append(_remote(ins[na], outs[na].at[me], sems, s, (px, py, pc)))
                recvs.append(_remote(ins[na], outs[na].at[4 * px + 2 * py + pc], sems, s, (px, py, pc)))
        return sends, recvs, locs

    extra = [] if small is None else [small]
    shapes = [_sds(a.shape, a.dtype) for a in parts] + [_sds((N_DEV,) + s.shape, s.dtype) for s in extra]
    n_sems = (n_chip - 1) * na + (N_DEV - 1) * len(extra)
    return _Rider(list(parts) + extra, shapes, n_sems, na + len(extra), plan)


def _pair_add(halves, other, core, name):
    _, nch, r, c = halves.shape

    def body(c_ref, h_ref, o_ref, p_ref):
        p_ref[...] = (h_ref[...].astype(F32) + o_ref[...].astype(F32)).astype(BF16)

    blk = pl.BlockSpec((None, r, c), lambda j, c_ref: (j, 0, 0))
    return _pc(
        body, name,
        grid_spec=pltpu.PrefetchScalarGridSpec(
            num_scalar_prefetch=1, grid=(nch,),
            in_specs=[pl.BlockSpec((None, None, r, c), lambda j, c_ref: (c_ref[0], j, 0, 0)), blk], out_specs=blk),
        out_shape=_sds((nch, r, c), BF16),
        compiler_params=_params(),
    )(core, halves, other)


def _sum_adamw(parts, w, m, v, name, rider=None):
    nl, r, c = w.shape
    tr = 256 if r % 256 == 0 else r

    def body(*refs):
        p_refs = refs[:nl]
        w_ref, m_ref, v_ref, g_ref, d_ref, mo_ref, vo_ref = refs[nl:]
        for l in range(nl):
            @pl.when(pl.program_id(0) == l)
            def _(p_ref=p_refs[l]):
                g = p_ref[0].astype(F32)
                for j in range(1, p_ref.shape[0]):
                    g = g + p_ref[j].astype(F32)
                mn = ADAM_B1 * m_ref[...] + (1.0 - ADAM_B1) * g
                vn = ADAM_B2 * v_ref[...] + (1.0 - ADAM_B2) * (g * g)
                m_hat = mn / (1.0 - ADAM_B1 ** ADAM_STEP)
                v_hat = vn / (1.0 - ADAM_B2 ** ADAM_STEP)
                g_ref[...] = g
                d_ref[...] = -ADAM_LR * (m_hat / (jnp.sqrt(v_hat) + ADAM_EPS) + ADAM_WD * w_ref[...])
                mo_ref[...] = mn
                vo_ref[...] = vn

    def part_spec(l, k):
        return pl.BlockSpec((k, tr, c), lambda li, i: (0, jnp.where(li == l, i, 0), 0))

    row = pl.BlockSpec((None, tr, c), lambda li, i: (li, i, 0))
    return _call(
        body, name, (*parts, w, m, v), rider, grid=(nl, r // tr),
        in_specs=[part_spec(l, p.shape[0]) for l, p in enumerate(parts)] + [row, row, row],
        out_specs=[row] * 4,
        out_shape=[_sds((nl, r, c), F32)] * 4,
        compiler_params=_params(),
    )


SMALL = ("lower_bounds", "pre_norm_g", "hgrn_norm_g", "pool_w", "pool_scale", "post_norm_g", "fox_f_bias")
SMALL_LANES = 128


def _small_size(tree):
    return sum(tree[k].size for k in SMALL)


def _pack_small(tree, extra=None):
    flat = jnp.concatenate([tree[k].reshape(-1) for k in SMALL] + ([] if extra is None else [extra.reshape(1)]))
    rows = -(-(_small_size(tree) + 1) // (8 * SMALL_LANES)) * 8
    return jnp.pad(flat, (0, rows * SMALL_LANES - flat.shape[0])).reshape(rows, SMALL_LANES)


def _unpack_small(packed, like):
    out, off = {}, 0
    for k in SMALL:
        size = like[k].size
        assert off % SMALL_LANES == 0
        rows = packed[off // SMALL_LANES:-(-(off + size) // SMALL_LANES)]
        out[k] = rows.reshape(-1)[:size].reshape(like[k].shape)
        off += size
    return out


def _block_diag(pw):
    g = pw.shape[0]
    eye = jnp.eye(g, dtype=pw.dtype)
    return (eye[:, None, :, None] * pw[:, :, None, :]).reshape(g * HEAD, g * HEAD)


def _assemble_w_in(g_in, name):
    parts = g_in if isinstance(g_in, (list, tuple)) else [g_in]
    _, rows, shard = parts[0].shape
    d = rows * len(parts)
    tr = min(256, rows)
    per = rows // tr
    wide = MAIN_W + FC_PAD

    def body(*refs):
        g_refs, (wm_ref, wf_ref, wmt_ref, wft_ref, row_s) = refs[:len(parts)], refs[len(parts):]
        row_s[:, MAIN_W:] = jnp.zeros((tr, FC_PAD), F32)
        for p, g_ref in enumerate(g_refs):
            @pl.when(pl.program_id(0) // per == p)
            def _(g_ref=g_ref):
                for j in range(N_DEV):
                    row_s[:, j * shard:(j + 1) * shard] = g_ref[j].astype(F32)
        wm_ref[...] = row_s[:, :MAIN_W].astype(BF16)
        wf_ref[...] = row_s[:, MAIN_W:].astype(BF16)
        for j in range(0, MAIN_W, 512):
            wmt_ref[j:j + 512, :] = row_s[:, j:j + 512].T.astype(BF16)
        wft_ref[...] = row_s[:, MAIN_W:].T.astype(BF16)

    return _pc(
        body, name, grid=(d // tr,),
        in_specs=[pl.BlockSpec((N_DEV, tr, shard), lambda i, p=p: (0, jnp.clip(i - p * per, 0, per - 1), 0))
                  for p in range(len(parts))],
        out_specs=[pl.BlockSpec((tr, MAIN_W), lambda i: (i, 0)), pl.BlockSpec((tr, FC_PAD), lambda i: (i, 0)),
                   pl.BlockSpec((MAIN_W, tr), lambda i: (0, i)), pl.BlockSpec((FC_PAD, tr), lambda i: (0, i))],
        out_shape=[_sds((d, MAIN_W), BF16), _sds((d, FC_PAD), BF16), _sds((MAIN_W, d), BF16), _sds((FC_PAD, d), BF16)],
        scratch_shapes=[pltpu.VMEM((tr, wide), F32)],
        compiler_params=_params(),
    )(*parts)


def _w_out_parts(g_out):
    full_out = g_out.reshape(N_DEV * g_out.shape[1], g_out.shape[2])
    return full_out, full_out.T


def _layer_fwd(l, x, lbs, weights, lw, nb, rider_h=None, rider_c=None, target=None, rider_p=None):
    w_main, w_fc, _, _, w_out, _ = lw
    bias = jnp.pad(weights["fox_f_bias"][l:l + 1], ((0, 0), (0, FC_PAD - FOX_HEADS)))
    wbd = _block_diag(weights["pool_w"][l]).astype(BF16)
    (proj, fc, ht, qt, kt), rode_p = _in_proj_fwd(x, weights["pre_norm_g"][l:l + 1], w_main, w_fc, f"in_proj_fwd_{l}",
                                                  rider_p)
    c_col, c_row = _fox_decay_fwd(fc, bias, nb, f"fox_decay_fwd_{l}")
    (o_h, s0), rode_h = _hgrn_fwd(proj, lbs[l:l + 1], _block_ones(HGRN_W, BF16), nb, f"hgrn_fwd_{l}", rider_h)
    (o_c, lse), rode_c = _fox_fwd(proj, kt, c_col, c_row, nb, f"fox_fwd_{l}", rider_c)
    if w_out is None:
        lw = tuple(lw[:4]) + _w_out_parts(rode_p[0])
        w_out = lw[4]
    x_next, mixt, y = _merge_fwd(x, proj, o_h, o_c, weights["hgrn_norm_g"][l:l + 1], wbd, weights["pool_scale"][l:l + 1],
                                 w_out, weights["post_norm_g"][l:l + 1], nb, f"merge_fwd_{l}", target)
    return x_next, (x, proj, fc, ht, qt, kt, c_col, c_row, o_h, s0, o_c, lse, mixt, y, bias, wbd), lw, (rode_h, rode_c)


def _layer_bwd(l, dx, saved, lbs, weights, lw, nb, rider=None, send_w_out=None):
    _, proj, fc, ht, qt, kt, c_col, c_row, o_h, s0, o_c, lse, mixt, y, bias, wbd = saved
    w_out_t = lw[5]
    g = {}
    g["w_out"], dmix, dgp, da, dat, d_gc, delta = _merge_bwd(dx, y, weights["post_norm_g"][l:l + 1], w_out_t, proj, o_c,
                                                         mixt, nb, f"merge_bwd_{l}")
    g["post_norm_g"] = dgp[0]
    (d_a, dgh, dlb), arrived = _hgrn_bwd(dmix, proj, o_h, s0, weights["hgrn_norm_g"][l:l + 1], lbs[l:l + 1],
                                         _block_ones(HGRN_W, BF16), nb, f"hgrn_bwd_{l}",
                                         None if send_w_out is None else send_w_out([g["w_out"]]))
    if arrived is not None:
        g["w_out_received"] = arrived[0]
    g["hgrn_norm_g"], g["lbs"] = dgh[0], dlb[0]
    (d_qc, d_kc, d_vc, dc_k, dc_q), rode = _fox_bwd(proj, qt, kt, da, dat, c_col, c_row, lse, delta, nb,
                                                    f"fox_bwd_{l}", rider)
    d_fc, dbias = _fox_decay_bwd(dc_q, dc_k, fc, bias, nb, f"fox_decay_bwd_{l}")
    g["fox_f_bias"] = dbias[0, :FOX_HEADS]
    pieces = [(d_a, C_QA), (None, C_UB), (d_qc, C_QC), (d_kc, C_KC), (d_vc, C_VC), (d_gc, C_GC), (d_fc, None)]
    g["w_in"], d_b, dwbd, dps = _w_in_grad(ht, pieces, dmix, proj, wbd, wbd.T, weights["pool_scale"][l:l + 1], nb,
                                           f"w_in_grad_{l}")
    pieces[1] = (d_b, C_UB)
    g["pool_w"] = jnp.stack([dwbd[j * HEAD:(j + 1) * HEAD, j * HEAD:(j + 1) * HEAD] for j in range(len(POOL_WINDOWS))])
    g["pool_scale"] = dps[0]
    return g, pieces, rode


def _layer_bwd_input(l, dx, pieces, saved, weights, lw, rider=None):
    (dxi, dgpre), rode = _in_proj_bwd(pieces, lw[2], lw[3], saved[0], weights["pre_norm_g"][l:l + 1], dx,
                                      f"in_proj_bwd_{l}", rider)
    return dxi, dgpre[0], rode


def kernel(x, lower_bounds, pre_norm_g, w_in, hgrn_norm_g, fox_f_bias, pool_w, pool_scale, w_out, post_norm_g, loss_target, m_lower_bounds, m_pre_norm_g, m_w_in, m_hgrn_norm_g, m_fox_f_bias, m_pool_w, m_pool_scale, m_w_out, m_post_norm_g, v_lower_bounds, v_pre_norm_g, v_w_in, v_hgrn_norm_g, v_fox_f_bias, v_pool_w, v_pool_scale, v_w_out, v_post_norm_g):
    weights = dict(lower_bounds=lower_bounds, pre_norm_g=pre_norm_g, hgrn_norm_g=hgrn_norm_g, fox_f_bias=fox_f_bias,
                   pool_w=pool_w, pool_scale=pool_scale, post_norm_g=post_norm_g)
    mom_m = dict(lower_bounds=m_lower_bounds, pre_norm_g=m_pre_norm_g, hgrn_norm_g=m_hgrn_norm_g, fox_f_bias=m_fox_f_bias,
                 pool_w=m_pool_w, pool_scale=m_pool_scale, post_norm_g=m_post_norm_g)
    mom_v = dict(lower_bounds=v_lower_bounds, pre_norm_g=v_pre_norm_g, hgrn_norm_g=v_hgrn_norm_g, fox_f_bias=v_fox_f_bias,
                 pool_w=v_pool_w, pool_scale=v_pool_scale, post_norm_g=v_post_norm_g)
    depth = w_in.shape[0]
    nb, t, d = x.shape
    n = nb * t
    core = lax.axis_index("c").astype(jnp.int32).reshape(1)
    shards = [(w_in[l].astype(BF16), w_out[l].astype(BF16)) for l in range(depth)]
    lbs = _lower_bound_table(lower_bounds, "lower_bound_table")

    (g_in,) = _gather_weights(shards[0][0])
    coming = tuple(_assemble_w_in(g_in, "assemble_w_in_0")) + (None, None)
    xl, saved, lw = x.reshape(n, d), [], []
    half = d // 2
    for l in range(depth):
        last = l + 1 == depth
        xl, sv, lw_l, (rode_h, rode_c) = _layer_fwd(
            l, xl, lbs, weights, coming, nb, None if last else _gather_rider([shards[l + 1][0][:half]]),
            None if last else _gather_rider([shards[l + 1][0][half:], shards[l + 1][1]]),
            loss_target.reshape(n, d) if last else None,
            rider_p=_gather_rider([shards[l][1]]) if coming[4] is None else None)
        saved.append(sv)
        lw.append(lw_l)
        if not last:
            coming = (tuple(_assemble_w_in([rode_h[0], rode_c[0]], f"assemble_w_in_{l + 1}"))
                      + _w_out_parts(rode_c[1]))
    dx, sq = xl
    loss_here = 0.5 * jnp.sum(sq) / d

    grads, recv_in, pending = [None] * depth, [None] * depth, None
    for l in reversed(range(depth)):
        g, pieces, rode = _layer_bwd(l, dx, saved[l], lbs, weights, lw[l], nb, pending, _direct_exchange_rider)
        if rode is not None:
            recv_in[l + 1] = rode[0]
        if l > 0:
            pending = _direct_exchange_rider([g["w_in"]])
            dx, g["pre_norm_g"], _ = _layer_bwd_input(l, dx, pieces, saved[l], weights, lw[l])
        else:
            (other,) = _run_rider(_swap_rider([g["w_in"]]), "grad_swap")
            summed = _pair_add(g["w_in"], other, core, "grad_pair_add")
            dx, g["pre_norm_g"], (recv_in[l],) = _layer_bwd_input(l, dx, pieces, saved[l], weights, lw[l],
                                                                  _chip_exchange_rider([summed]))
        grads[l] = g
    small = {k: jnp.stack([grads[l][k] for l in range(depth)]) for k in SMALL if k != "lower_bounds"}
    small["lower_bounds"] = _lower_bound_bwd(lower_bounds, jnp.stack([grads[l]["lbs"] for l in range(depth)]),
                                             "lower_bound_bwd")
    (r_small,) = _run_rider(_gather_rider([_pack_small(small, loss_here)]), "small_grads_gather")

    res_in, _ = _sum_adamw(recv_in, w_in, m_w_in, v_w_in, "adamw_w_in")
    res_out, _ = _sum_adamw([grads[l]["w_out_received"] for l in range(depth)], w_out, m_w_out, v_w_out, "adamw_w_out")
    res_small, _ = _sum_adamw([r_small], _pack_small(weights)[None], _pack_small(mom_m)[None], _pack_small(mom_v)[None],
                              "adamw_small")
    loss = res_small[0][0][_small_size(weights) // SMALL_LANES, _small_size(weights) % SMALL_LANES]

    names = ("lower_bounds", "pre_norm_g", "w_in", "hgrn_norm_g", "fox_f_bias", "pool_w", "pool_scale", "w_out", "post_norm_g")
    outs = [loss, dx.reshape(nb, t, d)]
    for i in range(4):
        full = dict(_unpack_small(res_small[i][0], weights), w_in=res_in[i], w_out=res_out[i])
        outs += [full[k] for k in names]
    return tuple(outs)
```

```python
import functools

import jax
import jax.numpy as jnp
from jax import lax
from jax.experimental import pallas as pl
from jax.experimental.pallas import tpu as pltpu

F32, BF16 = jnp.float32, jnp.bfloat16
MESH = pl.DeviceIdType.MESH
N_DEV = 8

NORM_EPS = 1e-6
MASK_VALUE = -1e30
TINY = 1e-30
CHUNK = 64
SUB = 16
HGRN_W, POOL_W, FOX_W = 256, 256, 512
HEAD = 64
FOX_HEADS = 8
POOL_WINDOWS = (2, 4, 8, 16)
POOL_HALO = 16
MAIN_W = 3584
FC_PAD = 128
C_QA, C_FA, C_IA, C_GA, C_UB, C_GB, C_QC, C_KC, C_VC, C_GC = 0, 256, 512, 768, 1024, 1280, 1536, 2048, 2560, 3072
FOX_SCALE = HEAD ** -0.5

ADAM_LR, ADAM_B1, ADAM_B2, ADAM_EPS, ADAM_WD, ADAM_STEP = 0.001, 0.9, 0.999, 1e-08, 0.01, 10

VMEM_LIMIT = 56 * 1024 * 1024


def _pc(fn, name, **kw):
    return pl.pallas_call(fn, name=name, **kw)


def _params(**kw):
    return pltpu.CompilerParams(vmem_limit_bytes=VMEM_LIMIT, **kw)


class _Rider:
    def __init__(self, inputs, out_shapes, n_sems, n_local, plan):
        self.inputs, self.out_shapes, self.n_sems, self.n_local, self.plan = list(inputs), list(out_shapes), n_sems, n_local, plan

    def start(self, ins, outs, *sems):
        sends, _, locs = self.plan(ins, outs, *sems)
        for cp in locs + sends:
            cp.start()

    def wait(self, ins, outs, *sems):
        sends, recvs, locs = self.plan(ins, outs, *sems)
        for cp in recvs:
            cp.wait_recv()
        for cp in sends:
            cp.wait_send()
        for cp in locs:
            cp.wait()

    def sem_shapes(self):
        return [pltpu.SemaphoreType.DMA((self.n_sems,)), pltpu.SemaphoreType.DMA((self.n_sems,)),
                pltpu.SemaphoreType.DMA((self.n_local,))]


def _call(body, name, args, rider=None, *, grid, in_specs, out_specs, out_shape, scratch_shapes=(), **kw):
    if rider is None:
        res = _pc(body, name, grid=grid, in_specs=in_specs, out_specs=out_specs, out_shape=out_shape,
                  scratch_shapes=list(scratch_shapes), **kw)(*args)
        return res, None
    n_in, n_out, n_scr = len(in_specs), len(out_specs), len(scratch_shapes)
    n_rin, n_rout = len(rider.inputs), len(rider.out_shapes)

    def ridden(*refs):
        ins, refs = refs[:n_in], refs[n_in:]
        rins, refs = refs[:n_rin], refs[n_rin:]
        outs, refs = refs[:n_out], refs[n_out:]
        routs, refs = refs[:n_rout], refs[n_rout:]
        scr, sems = refs[:n_scr], refs[n_scr:]
        first = functools.reduce(jnp.logical_and, [pl.program_id(a) == 0 for a in range(len(grid))])
        last = functools.reduce(jnp.logical_and, [pl.program_id(a) == g - 1 for a, g in enumerate(grid)])

        @pl.when(first)
        def _():
            rider.start(rins, routs, *sems)

        body(*ins, *outs, *scr)

        @pl.when(last)
        def _():
            rider.wait(rins, routs, *sems)

    any_spec = pl.BlockSpec(memory_space=pl.ANY)
    res = _pc(ridden, name, grid=grid, in_specs=list(in_specs) + [any_spec] * n_rin,
              out_specs=list(out_specs) + [any_spec] * n_rout, out_shape=list(out_shape) + rider.out_shapes,
              scratch_shapes=list(scratch_shapes) + rider.sem_shapes(), **kw)(*args, *rider.inputs)
    return res[:n_out], res[n_out:]


def _run_rider(rider, name):
    n_rin = len(rider.inputs)

    def body(*refs):
        ins, outs, sems = refs[:n_rin], refs[n_rin:n_rin + len(rider.out_shapes)], refs[n_rin + len(rider.out_shapes):]
        rider.start(ins, outs, *sems)
        rider.wait(ins, outs, *sems)

    any_spec = pl.BlockSpec(memory_space=pl.ANY)
    return _pc(body, name, in_specs=[any_spec] * n_rin, out_specs=[any_spec] * len(rider.out_shapes),
               out_shape=rider.out_shapes, scratch_shapes=rider.sem_shapes())(*rider.inputs)


def _dot(a, b):
    return jnp.dot(a, b, preferred_element_type=F32)


def _dot_nt(a, b):
    return lax.dot_general(a, b, (((1,), (1,)), ((), ())), preferred_element_type=F32)


def _dot_tn(a, b):
    return lax.dot_general(a, b, (((0,), (0,)), ((), ())), preferred_element_type=F32)


def _sel_dot_exact(sel, x):
    hi = x.astype(BF16)
    rest = x - hi.astype(F32)
    mid = rest.astype(BF16)
    lo = (rest - mid.astype(F32)).astype(BF16)
    sb = sel.astype(BF16)
    return _dot(sb, hi) + _dot(sb, mid) + _dot(sb, lo)


def _split2(x):
    hi = x.astype(BF16)
    return hi, (x - hi.astype(F32)).astype(BF16)


def _sel_dot(sel, x):
    hi, lo = _split2(x)
    sb = sel.astype(BF16)
    return _dot(sb, hi) + _dot(sb, lo)


def _dot_sel(x, sel):
    hi, lo = _split2(x)
    sb = sel.astype(BF16)
    return _dot(hi, sb) + _dot(lo, sb)


def _sigmoid(x):
    return 1.0 / (1.0 + jnp.exp(-x))


def _block_ones(n, dtype):
    r = lax.broadcasted_iota(jnp.int32, (n, n), 0) // HEAD
    c = lax.broadcasted_iota(jnp.int32, (n, n), 1) // HEAD
    return (r == c).astype(dtype)


def _sds(shape, dtype):
    return jax.ShapeDtypeStruct(shape, dtype)


def _in_proj_fwd(x, g_pre, w_main, w_fc, name, rider=None):
    n, d = x.shape
    tm = min(512, n)

    def body(x_ref, g_ref, w_ref, wf_ref, proj_ref, fc_ref, ht_ref, qt_ref, kt_ref):
        xv = x_ref[...]
        r = lax.rsqrt(jnp.mean(xv * xv, axis=-1, keepdims=True) + NORM_EPS)
        hf = xv * r * g_ref[...]
        hb = hf.astype(BF16)
        ht_ref[...] = hf.T.astype(BF16)
        for j in range(0, MAIN_W, FOX_W):
            res = _dot(hb, w_ref[:, j:j + FOX_W])
            proj_ref[:, j:j + FOX_W] = res
            if j == C_QC:
                qt_ref[...] = (res * FOX_SCALE).T.astype(BF16)
            if j == C_KC:
                kt_ref[...] = res.T.astype(BF16)
        fc_ref[...] = _dot(hb, wf_ref[...])

    def cols(rows):
        return pl.BlockSpec((rows, tm), lambda i: (0, i))

    return _call(
        body, name, (x, g_pre, w_main, w_fc), rider, grid=(n // tm,),
        in_specs=[pl.BlockSpec((tm, d), lambda i: (i, 0)), pl.BlockSpec((1, d), lambda i: (0, 0)),
                  pl.BlockSpec((d, MAIN_W), lambda i: (0, 0)), pl.BlockSpec((d, FC_PAD), lambda i: (0, 0))],
        out_specs=[pl.BlockSpec((tm, MAIN_W), lambda i: (i, 0)), pl.BlockSpec((tm, FC_PAD), lambda i: (i, 0)),
                   cols(d), cols(FOX_W), cols(FOX_W)],
        out_shape=[_sds((n, MAIN_W), F32), _sds((n, FC_PAD), F32), _sds((d, n), BF16), _sds((FOX_W, n), BF16),
                   _sds((FOX_W, n), BF16)],
        compiler_params=_params(),
    )


def _fox_decay_fwd(fc, bias, nb, name):
    n = fc.shape[0]
    t = n // nb
    tt = min(256, t)
    nt = t // tt

    def body(fc_ref, b_ref, c_ref, cr_ref):
        r = lax.broadcasted_iota(jnp.int32, (tt, tt), 0)
        cc = lax.broadcasted_iota(jnp.int32, (tt, tt), 1)
        carry = jnp.zeros((1, FC_PAD), F32)
        for i in range(nt):
            rows = slice(i * tt, (i + 1) * tt)
            xv = fc_ref[rows, :] + b_ref[...]
            lf = jnp.minimum(xv, 0.0) - jnp.log(1.0 + jnp.exp(-jnp.abs(xv)))
            cs = _sel_dot_exact(r >= cc, lf) + carry
            c_ref[rows, :] = cs
            cr_ref[:, rows] = cs.T[:FOX_HEADS, :]
            carry = cs[tt - 1:tt, :]

    return _pc(
        body, name, grid=(nb,),
        in_specs=[pl.BlockSpec((t, FC_PAD), lambda b: (b, 0)), pl.BlockSpec((1, FC_PAD), lambda b: (0, 0))],
        out_specs=[pl.BlockSpec((t, FC_PAD), lambda b: (b, 0)), pl.BlockSpec((None, FOX_HEADS, t), lambda b: (b, 0, 0))],
        out_shape=[_sds((n, FC_PAD), F32), _sds((nb, FOX_HEADS, t), F32)],
        compiler_params=_params(),
    )(fc, bias)


def _hgrn_gates(q, z, lb):
    sig = _sigmoid(z)
    sn = _sigmoid(-z)
    f = lb + (1.0 - lb) * sig
    g = jnp.log(jnp.maximum(f, TINY))
    k = (1.0 - lb) * sn
    sq = _sigmoid(q)
    return sig, sn, f, g, k, sq


def _sub_tri(n, lower):
    r = lax.broadcasted_iota(jnp.int32, (n, n), 0)
    c = lax.broadcasted_iota(jnp.int32, (n, n), 1)
    tri = (r >= c) if lower else (r <= c)
    return jnp.logical_and(r // SUB == c // SUB, tri).astype(F32)


def _live_rows(t):
    return 8 * (t // 8 + 1)


def _pad_rows(x):
    return x if x.shape[0] == SUB else jnp.concatenate([x, jnp.zeros((SUB - x.shape[0], x.shape[1]), x.dtype)], axis=0)


def _hgrn_decays(qs, k, b):
    srow = lax.broadcasted_iota(jnp.int32, (SUB, HGRN_W), 0)
    es, ws = [], []
    for t in range(SUB):
        r = _live_rows(t)
        e = jnp.where(srow[:r] <= t, jnp.exp(b[t:t + 1, :] - b[:r]), 0.0)
        es.append(e)
        ws.append(_pad_rows(e * (qs[t:t + 1, :] * k[:r])))
    return srow, es, ws


def _hgrn_state_step(st, k, v, b, bmask):
    bl = b[SUB - 1:SUB, :]
    ktil = k * jnp.exp(bl - b)
    return st * jnp.exp(bl) + _dot_tn(v.astype(BF16), ktil.astype(BF16)) * bmask


def _hgrn_tile(t):
    return min(256, t)


def _hgrn_fwd(proj, lb, ones_b, nb, name, rider=None):
    n = proj.shape[0]
    t = n // nb
    tt = _hgrn_tile(t)
    nt = t // tt
    ncs = tt // CHUNK
    w = HGRN_W

    def body(q_ref, z_ref, v_ref, lb_ref, ones_ref, o_ref, s0_ref, st_s, b_s, qs_s, k_s):
        @pl.when(pl.program_id(0) == 0)
        def _():
            st_s[...] = jnp.zeros_like(st_s)

        for e in range(nb):
            q = q_ref[e]
            _, _, _, g, k, sq = _hgrn_gates(q, z_ref[e], lb_ref[...])
            b_s[e] = _sel_dot(_sub_tri(tt, True), g)
            qs_s[e] = q * sq
            k_s[e] = k
        bmask = _block_ones(w, F32)
        ones_b = ones_ref[...]
        nsub = CHUNK // SUB

        def chunk(c, carry):
            sts = [st_s[e] for e in range(nb)]
            for e in range(nb):
                s0_ref[e, c] = sts[e]
            base = pl.multiple_of(c * CHUNK, CHUNK)
            tiles = [[(qs_s[e, pl.ds(base + u * SUB, SUB), :], k_s[e, pl.ds(base + u * SUB, SUB), :],
                       v_ref[e, pl.ds(base + u * SUB, SUB), :], b_s[e, pl.ds(base + u * SUB, SUB), :])
                      for u in range(nsub)] for e in range(nb)]
            aexps = [[] for _ in range(nb)]
            for e in range(nb):
                for qs, k, v, b in tiles[e]:
                    _, _, ws = _hgrn_decays(qs, k, b)
                    aexps[e].append(_dot(jnp.concatenate(ws, axis=0).astype(BF16), ones_b))
            inters = [[] for _ in range(nb)]
            for u in range(nsub):
                for e in range(nb):
                    qs, k, v, b = tiles[e][u]
                    inters[e].append(_dot_nt((qs * jnp.exp(b)).astype(BF16), sts[e].astype(BF16)))
                    sts[e] = _hgrn_state_step(sts[e], k, v, b, bmask)
            for e in range(nb):
                st_s[e] = sts[e]
            for e in range(nb):
                for u, ((qs, k, v, b), aexp, o) in enumerate(zip(tiles[e], aexps[e], inters[e])):
                    for t in range(SUB):
                        r = _live_rows(t)
                        row = o[t:t + 1, :] + jnp.sum(aexp[t * SUB:t * SUB + r, :] * v[:r], axis=0, keepdims=True)
                        o_ref[e, pl.ds(base + u * SUB + t, 1), :] = row
            return carry

        lax.fori_loop(0, ncs, chunk, 0)

    def col(j):
        return pl.BlockSpec((nb, tt, w), lambda i: (0, i, j))

    proj3 = proj.reshape(nb, t, proj.shape[1])
    (o, s0), rode = _call(
        body, name, (proj3, proj3, proj3, lb, ones_b), rider, grid=(nt,),
        in_specs=[col(C_QA // w), col(C_FA // w), col(C_IA // w), pl.BlockSpec((1, w), lambda i: (0, 0)),
                  pl.BlockSpec((w, w), lambda i: (0, 0))],
        out_specs=[pl.BlockSpec((nb, tt, w), lambda i: (0, i, 0)),
                   pl.BlockSpec((nb, ncs, w, w), lambda i: (0, i, 0, 0))],
        out_shape=[_sds((nb, t, w), F32), _sds((nb, t // CHUNK, w, w), F32)],
        scratch_shapes=[pltpu.VMEM((nb, w, w), F32)] + [pltpu.VMEM((nb, tt, w), F32)] * 3,
        compiler_params=_params(),
    )
    return (o.reshape(n, w), s0.reshape(n // CHUNK, w, w)), rode


def _pool_lane_windows():
    lane = lax.broadcasted_iota(jnp.int32, (1, POOL_W), 1) // HEAD
    wl = jnp.zeros((1, POOL_W), F32)
    for gi, win in enumerate(POOL_WINDOWS):
        wl = jnp.where(lane == gi, float(win), wl)
    return lane, wl


def _pool_select(lane, parts):
    out = parts[-1]
    for gi in range(len(parts) - 2, -1, -1):
        out = jnp.where(lane == gi, parts[gi], out)
    return out


def _pool_mix(u, halo, t0, tt):
    lane, wl = _pool_lane_windows()
    ext = jnp.concatenate([halo, u], axis=0)
    sums, cur, shift = [], ext, 1
    for _ in POOL_WINDOWS:
        cur = cur + pltpu.roll(cur, shift, axis=0)
        sums.append(cur[POOL_HALO:, :])
        shift *= 2
    tpos = (t0 + lax.broadcasted_iota(jnp.int32, (tt, POOL_W), 0)).astype(F32)
    cnt = jnp.minimum(tpos + 1.0, wl)
    return _pool_select(lane, sums) / cnt - u, cnt


def _rows_reduce(x, op, final):
    while x.shape[0] > 8 and x.shape[0] % 16 == 0:
        half = x.shape[0] // 2
        x = op(x[:half], x[half:])
    return final(x, axis=0, keepdims=True)


def _tri_pair(step, n, group=1):
    counts = [a // group + 1 for a in range(n)]
    firsts = [sum(counts[:a]) for a in range(1, n)]
    a = sum([(step >= f).astype(jnp.int32) for f in firsts], jnp.int32(0))
    first = sum([jnp.where(step >= f, c, 0) for f, c in zip(firsts, counts)], jnp.int32(0))
    return a, step - first


def _tri_steps(n, group=1):
    return sum(a // group + 1 for a in range(n))


def _lane_lo():
    return lax.broadcasted_iota(jnp.int32, (1, 2 * HEAD), 1) < HEAD


def _put_col(tile, hh, colv):
    lane = lax.broadcasted_iota(jnp.int32, tile.shape, 1)
    return jnp.where(lane == hh, colv, tile)


def _fox_fwd(proj, kt, c_col, c_row, nb, name, rider=None):
    n = proj.shape[0]
    t = n // nb
    tb = min(256, t)
    nq = t // tb
    pw = 2 * HEAD
    kw = 2 if nq % 2 == 0 else 1
    nk = nq // kw

    def body(*refs):
        q_ref, kt_refs, (v_ref, cc_ref, cr_ref, o_ref, lse_ref, m_s, acc_s, cq_s) = refs[0], refs[1:1 + nb], refs[1 + nb:]
        qi, kk = _tri_pair(pl.program_id(0), nq, kw)

        @pl.when(kk == 0)
        def _():
            m_s[...] = jnp.full_like(m_s, -jnp.inf)
            acc_s[...] = jnp.zeros_like(acc_s)
            for e in range(nb):
                for hh in range(FOX_HEADS):
                    cq_s[e, hh] = jnp.broadcast_to(cc_ref[e, :, hh:hh + 1], (tb, pw))

        def block(masked, sub):
            lo = _lane_lo()
            keys = slice(sub * tb, (sub + 1) * tb)
            if masked:
                causal = lax.broadcasted_iota(jnp.int32, (tb, tb), 0) >= lax.broadcasted_iota(jnp.int32, (tb, tb), 1)
            def lanes(hh):
                return lo if hh % 2 == 0 else jnp.logical_not(lo)

            def scores(hh, e):
                sl = slice((hh // 2) * pw, (hh // 2 + 1) * pw)
                return _dot(jnp.where(lanes(hh), q_ref[e, :, sl] * FOX_SCALE, 0.0).astype(BF16), kt_refs[e][sl, keys])

            order = ([(hh, e) for e in range(nb) for hh in range(FOX_HEADS)] if masked else
                     [(hh, e) for hh in range(FOX_HEADS) for e in range(nb)])
            ahead = scores(*order[0])
            for j, (hh, e) in enumerate(order):
                s = ahead
                if j + 1 < len(order):
                    ahead = scores(*order[j + 1])
                s = s + (jnp.tile(cq_s[e, hh], (1, tb // pw)) - cr_ref[e, hh:hh + 1, keys])
                if masked:
                    s = jnp.where(causal, s, MASK_VALUE)
                m_prev = m_s[e, hh]
                m_new = jnp.maximum(m_prev, jnp.max(s, axis=1, keepdims=True))
                alpha = jnp.exp(m_prev - m_new)
                pe = jnp.exp(s - jnp.tile(m_new, (1, tb // pw)))
                m_s[e, hh] = m_new
                vf = v_ref[e, keys, (hh // 2) * pw:(hh // 2 + 1) * pw]
                acc_s[e, hh] = alpha * acc_s[e, hh] + _dot(pe.astype(BF16), jnp.where(lanes(hh), vf, 1.0).astype(BF16))

        def finish():
            lo = _lane_lo()
            for e in range(nb):
                m_all, l_all = jnp.zeros((tb, FC_PAD), F32), jnp.ones((tb, FC_PAD), F32)
                for p in range(FOX_HEADS // 2):
                    a0, a1 = acc_s[e, 2 * p], acc_s[e, 2 * p + 1]
                    both = pltpu.roll(jnp.where(lo, a1, a0), HEAD, axis=1)
                    o_ref[e, :, p * pw:(p + 1) * pw] = jnp.where(lo, a0, a1) / both
                    m_all = _put_col(_put_col(m_all, 2 * p, m_s[e, 2 * p]), 2 * p + 1, m_s[e, 2 * p + 1])
                    l_all = _put_col(_put_col(l_all, 2 * p, both), 2 * p + 1, a1)
                lse_ref[e] = (m_all + jnp.log(l_all)).T[:FOX_HEADS, :]

        for sub in range(kw):
            @pl.when(kk * kw + sub < qi)
            def _(sub=sub):
                block(False, sub)

            @pl.when(kk * kw + sub == qi)
            def _(sub=sub):
                block(True, sub)

        @pl.when(kk == qi // kw)
        def _():
            finish()

    def qspec(wd, j):
        return pl.BlockSpec((nb, tb, wd), lambda st: (0, _tri_pair(st, nq, kw)[0], j))

    def ktspec(e):
        return pl.BlockSpec((FOX_W, kw * tb), lambda st: (0, e * nk + _tri_pair(st, nq, kw)[1]))

    proj3 = proj.reshape(nb, t, proj.shape[1])
    (o, lse), rode = _call(
        body, name, (proj3, *[kt] * nb, proj3, c_col.reshape(nb, t, FC_PAD), c_row), rider, grid=(_tri_steps(nq, kw),),
        in_specs=[qspec(FOX_W, C_QC // FOX_W)] + [ktspec(e) for e in range(nb)] + [
            pl.BlockSpec((nb, kw * tb, FOX_W), lambda st: (0, _tri_pair(st, nq, kw)[1], C_VC // FOX_W)), qspec(FC_PAD, 0),
            pl.BlockSpec((nb, FOX_HEADS, kw * tb), lambda st: (0, 0, _tri_pair(st, nq, kw)[1]))],
        out_specs=[qspec(FOX_W, 0), pl.BlockSpec((nb, FOX_HEADS, tb), lambda st: (0, 0, _tri_pair(st, nq, kw)[0]))],
        out_shape=[_sds((nb, t, FOX_W), F32), _sds((nb, FOX_HEADS, t), F32)],
        scratch_shapes=[pltpu.VMEM((nb, FOX_HEADS, tb, pw), F32)] * 3,
        compiler_params=_params(),
    )
    return (o.reshape(n, FOX_W), lse), rode


def _head_mean(x, ones_f):
    return _dot_sel(x, ones_f) * (1.0 / HEAD)


def _merge_fwd(x, proj, o_h, o_c, gh, wbd, scale, w_out, g_post, nb, name, target=None):
    n, d = x.shape
    t = n // nb
    tm = min(512, t)
    nt = t // tm
    nhb = tm // POOL_HALO

    def body(*refs):
        x_ref, ga_ref, gc_ref, oh_ref, u_ref, gb_ref, h_ref, oc_ref, gh_ref, wb_ref, s_ref, w_ref, gp_ref = refs[:13]
        if target is None:
            xo_ref, mixt_ref, y_ref = refs[13:]
        else:
            t_ref, dx_ref, sq_ref, mixt_ref, y_ref = refs[13:]
        oh = oh_ref[...]
        ones_f = _block_ones(HGRN_W, F32)
        na = oh * lax.rsqrt(_head_mean(oh * oh, ones_f) + NORM_EPS) * gh_ref[...]
        ga, gb, gc = ga_ref[...], gb_ref[...], gc_ref[...]
        ti = pl.program_id(0) % nt
        pooled, _ = _pool_mix(u_ref[...], jnp.where(ti == 0, 0.0, h_ref[...]), ti * tm, tm)
        ob = _dot(pooled.astype(BF16), wb_ref[...]) * s_ref[...] * (gb * _sigmoid(gb))
        mixed = jnp.concatenate([na * (ga * _sigmoid(ga)), ob, oc_ref[...] * (gc * _sigmoid(gc))], axis=1)
        mixt_ref[...] = mixed.T.astype(BF16)
        y = _dot(mixed.astype(BF16), w_ref[...])
        y_ref[...] = y
        xn = x_ref[...] + y * lax.rsqrt(jnp.mean(y * y, axis=-1, keepdims=True) + NORM_EPS) * gp_ref[...]
        if target is None:
            xo_ref[...] = xn
        else:
            @pl.when(pl.program_id(0) == 0)
            def _():
                sq_ref[...] = jnp.zeros_like(sq_ref)

            e = xn - t_ref[...]
            dx_ref[...] = e * (1.0 / d)
            sq_ref[...] += jnp.sum(e * e, axis=0, keepdims=True)

    def row(wd, j=0):
        return pl.BlockSpec((tm, wd), lambda i: (i, j))

    def full(a, b):
        return pl.BlockSpec((a, b), lambda i: (0, 0))

    head = [] if target is None else [target]
    res = _pc(
        body, name, grid=(n // tm,),
        in_specs=[row(d), row(HGRN_W, C_GA // HGRN_W), row(FOX_W, C_GC // FOX_W), row(HGRN_W),
                  row(POOL_W, C_UB // POOL_W), row(POOL_W, C_GB // POOL_W),
                  pl.BlockSpec((POOL_HALO, POOL_W), lambda i: (jnp.maximum(i * nhb - 1, 0), C_UB // POOL_W)), row(FOX_W),
                  full(1, HGRN_W), full(POOL_W, POOL_W), full(1, POOL_W), full(d, d), full(1, d)] + [row(d)] * len(head),
        out_specs=[row(d)] + [full(1, d)] * len(head) + [pl.BlockSpec((d, tm), lambda i: (0, i)), row(d)],
        out_shape=[_sds((n, d), F32)] + [_sds((1, d), F32)] * len(head) + [_sds((d, n), BF16), _sds((n, d), F32)],
        compiler_params=_params(),
    )(x, proj, proj, o_h, proj, proj, proj, o_c, gh, wbd, scale, w_out, g_post, *head)
    return (res[0], res[1], res[2]) if target is None else ((res[0], res[1]), res[2], res[3])


def _rms_bwd(dy_scaled, xhat, r):
    return r * (dy_scaled - xhat * jnp.mean(dy_scaled * xhat, axis=-1, keepdims=True))


def _merge_bwd(dxo, y, g_post, w_out_t, proj, o_c, mixt, nb, name):
    n, d = y.shape
    t = n // nb
    tm = min(512, t)
    nt = t // tm
    wab = HGRN_W + POOL_W
    rows = d // N_DEV

    def body(dx_ref, y_ref, gp_ref, wt_ref, gc_ref, oc_ref, mt_ref,
             gw_ref, dm_ref, dgp_ref, da_ref, dat_ref, dg_ref, dl_ref, gw_s):
        @pl.when(pl.program_id(0) == 0)
        def _():
            dgp_ref[...] = jnp.zeros_like(dgp_ref)
            gw_s[...] = jnp.zeros_like(gw_s)

        yv, dxv = y_ref[...], dx_ref[...]
        r = lax.rsqrt(jnp.mean(yv * yv, axis=-1, keepdims=True) + NORM_EPS)
        yh = yv * r
        dgp_ref[...] += jnp.sum(dxv * yh, axis=0, keepdims=True)
        dyb = _rms_bwd(dxv * gp_ref[...], yh, r).astype(BF16)
        gw_s[...] += _dot(mt_ref[...], dyb)

        @pl.when(pl.program_id(0) == n // tm - 1)
        def _():
            for j in range(N_DEV):
                gw_ref[j % 2, j // 2] = gw_s[j * rows:(j + 1) * rows, :].astype(BF16)

        dm_ref[...] = _dot(dyb, wt_ref[:, :wab])
        dmc = _dot(dyb, wt_ref[:, wab:])
        gc, oc = gc_ref[...], oc_ref[...]
        sg = _sigmoid(gc)
        da = dmc * (gc * sg)
        da_ref[...] = da.astype(BF16)
        dat_ref[...] = da.T.astype(BF16)
        dg_ref[...] = (dmc * oc * (sg * (1.0 + gc * (1.0 - sg)))).astype(BF16)
        rr = lax.broadcasted_iota(jnp.int32, (FOX_W, FC_PAD), 0) // HEAD
        cc = lax.broadcasted_iota(jnp.int32, (FOX_W, FC_PAD), 1)
        dl_ref[...] = _dot_sel(da * oc, (rr == cc).astype(F32)).T[:FOX_HEADS, :]

    def row(wd, j=0):
        return pl.BlockSpec((tm, wd), lambda i: (i, j))

    def full(a, b):
        return pl.BlockSpec((a, b), lambda i: (0, 0))

    return _pc(
        body, name, grid=(n // tm,),
        in_specs=[row(d), row(d), full(1, d), full(d, d), row(FOX_W, C_GC // FOX_W), row(FOX_W),
                  pl.BlockSpec((d, tm), lambda i: (0, i))],
        out_specs=[pl.BlockSpec((2, N_DEV // 2, rows, d), lambda i: (0, 0, 0, 0)), row(wab), full(1, d), row(FOX_W),
                   pl.BlockSpec((FOX_W, tm), lambda i: (0, i)), row(FOX_W),
                   pl.BlockSpec((None, FOX_HEADS, tm), lambda i: (i // nt, 0, i % nt))],
        out_shape=[_sds((2, N_DEV // 2, rows, d), BF16), _sds((n, wab), F32), _sds((1, d), F32), _sds((n, FOX_W), BF16),
                   _sds((FOX_W, n), BF16), _sds((n, FOX_W), BF16), _sds((nb, FOX_HEADS, t), F32)],
        scratch_shapes=[pltpu.VMEM((d, d), F32)],
        compiler_params=_params(),
    )(dxo, y, g_post, w_out_t, proj, o_c, mixt)


def _w_in_grad(ht, pieces, dmix, proj, wbd, wbd_t, scale, nb, name):
    d, n = ht.shape
    t = n // nb
    ta, tk = d, min(512, t)
    nk, nt, nhb = n // tk, t // tk, tk // POOL_HALO
    given = [p for p, _ in pieces if p is not None]
    widths = [2 * POOL_W if p is None else p.shape[1] for p, _ in pieces]
    offs = [sum(widths[:i]) for i in range(len(widths))]
    in_w = MAIN_W + FOX_HEADS
    shard = in_w // N_DEV
    cu, cg, cm = C_UB // POOL_W, C_GB // POOL_W, HGRN_W // POOL_W

    def body(*refs):
        a_ref, p_refs = refs[0], list(refs[1:1 + len(given)])
        pool_refs = refs[1 + len(given):10 + len(given)]
        o_ref, db_ref, dw_ref, ds_ref, acc = refs[10 + len(given):]
        k = pl.program_id(1)

        @pl.when(k == 0)
        def _():
            dw_ref[...] = jnp.zeros_like(dw_ref)
            ds_ref[...] = jnp.zeros_like(ds_ref)
            acc[...] = jnp.zeros_like(acc)

        db, dw, dscale = _pool_bwd_tile(*pool_refs, k % nt, nt, tk)
        dw_ref[...] += dw
        ds_ref[...] += dscale
        db_ref[...] = db
        a = a_ref[...]
        for (p, _), off, wd in zip(pieces, offs, widths):
            pr = db if p is None else p_refs.pop(0)
            for j in range(0, wd, 512):
                jw = min(512, wd - j)
                acc[:, off + j:off + j + jw] += _dot(a, pr[:, j:j + jw])

        @pl.when(k == nk - 1)
        def _():
            for j in range(N_DEV):
                o_ref[j % 2, j // 2] = acc[:, j * shard:(j + 1) * shard].astype(BF16)

    def after(k):
        return jnp.minimum((k + 1) * nhb, n // POOL_HALO - 1)

    def tile(j):
        return pl.BlockSpec((tk, POOL_W), lambda i, k: (k, j))

    def full(r, c):
        return pl.BlockSpec((r, c), lambda i, k: (0, 0))

    pool_specs = [tile(cu), tile(cg),
                  pl.BlockSpec((POOL_HALO, POOL_W), lambda i, k: (jnp.maximum(k * nhb - 1, 0), cu)), tile(cm),
                  pl.BlockSpec((POOL_HALO, POOL_W), lambda i, k: (after(k), cg)),
                  pl.BlockSpec((POOL_HALO, POOL_W), lambda i, k: (after(k), cm)),
                  full(POOL_W, POOL_W), full(POOL_W, POOL_W), full(1, POOL_W)]
    return _pc(
        body, name, grid=(d // ta, nk),
        in_specs=[pl.BlockSpec((ta, tk), lambda i, k: (i, k))]
        + [pl.BlockSpec((tk, p.shape[1]), lambda i, k: (k, 0)) for p in given] + pool_specs,
        out_specs=[pl.BlockSpec((2, N_DEV // 2, ta, shard), lambda i, k: (0, 0, i, 0)),
                   pl.BlockSpec((tk, 2 * POOL_W), lambda i, k: (k, 0)), full(POOL_W, POOL_W), full(1, POOL_W)],
        out_shape=[_sds((2, N_DEV // 2, d, shard), BF16), _sds((n, 2 * POOL_W), BF16), _sds((POOL_W, POOL_W), F32),
                   _sds((1, POOL_W), F32)],
        scratch_shapes=[pltpu.VMEM((ta, sum(widths)), F32)],
        compiler_params=_params(),
    )(ht, *given, proj, proj, proj, dmix, proj, dmix, wbd, wbd_t, scale)


def _hgrn_state_bwd(qs, k, v, b, do, s0, ds1, bmask):
    bl = b[SUB - 1:SUB, :]
    eb, ebl, ekt = jnp.exp(b), jnp.exp(bl), jnp.exp(bl - b)
    qe, ktil = qs * eb, k * ekt
    ds1b, dob = ds1.astype(BF16), do.astype(BF16)
    dv = _dot_nt(ktil.astype(BF16), ds1b)
    dqe = _dot(dob, s0.astype(BF16))
    dktil = _dot(v.astype(BF16), ds1b)
    dbl = jnp.sum(dktil * ktil, axis=0, keepdims=True) + ebl * jnp.sum(s0 * ds1, axis=0, keepdims=True)
    ds0 = ds1 * ebl + _dot_tn(dob, qe.astype(BF16)) * bmask
    return dqe * eb, dktil * ekt, dv, dbl, ds0


def _hgrn_intra_bwd(qs, k, v, do, es, aexp, gexp, dq, dk, dv, put_dq_row):
    dks = [dk[j:j + 8] for j in range(0, SUB, 8)]
    dvs = [dv[j:j + 8] for j in range(0, SUB, 8)]
    for t in range(SUB):
        r = _live_rows(t)
        ge = gexp[t * SUB:t * SUB + r, :] * es[t]
        put_dq_row(t, dq[t:t + 1, :] + jnp.sum(ge * k[:r], axis=0, keepdims=True))
        for j in range(r // 8):
            dks[j] = dks[j] + ge[8 * j:8 * j + 8] * qs[t:t + 1, :]
            dvs[j] = dvs[j] + aexp[t * SUB + 8 * j:t * SUB + 8 * j + 8, :] * do[t:t + 1, :]
    return jnp.concatenate(dks, axis=0), jnp.concatenate(dvs, axis=0)


def _hgrn_bwd(dmix, proj, o_h, s0, gh, lb, ones_b, nb, name, rider=None):
    n = proj.shape[0]
    t = n // nb
    tt = _hgrn_tile(t)
    nt = t // tt
    ncs = tt // CHUNK
    nsub = CHUNK // SUB
    w = HGRN_W

    def body(dm_ref, q_ref, z_ref, v_ref, ga_ref, oh_ref, s0_ref, gh_ref, lb_ref, ones_ref,
             dp_ref, dgh_ref, dlb_ref, ds_s, ss_s, b_s, qs_s, k_s, do_s, dq_s, dk_s, dv_s, dbl_s):
        @pl.when(pl.program_id(0) == 0)
        def _():
            dgh_ref[...] = jnp.zeros_like(dgh_ref)
            dlb_ref[...] = jnp.zeros_like(dlb_ref)
            ds_s[...] = jnp.zeros_like(ds_s)

        ones_b = ones_ref[...]
        ones_f = ones_b.astype(F32)
        bmask = _block_ones(w, F32)
        lbv, ghv = lb_ref[...], gh_ref[...]
        kept = []
        for e in range(nb):
            oh, ga, dm = oh_ref[e], ga_ref[e], dm_ref[e]
            rn = lax.rsqrt(_head_mean(oh * oh, ones_f) + NORM_EPS)
            nh = oh * rn
            sga = _sigmoid(ga)
            dp_ref[e, :, 3 * w:4 * w] = (dm * nh * ghv * (sga * (1.0 + ga * (1.0 - sga)))).astype(BF16)
            dn = dm * (ga * sga)
            dgh_ref[...] += jnp.sum(dn * nh, axis=0, keepdims=True)
            dn = dn * ghv
            do_s[e] = rn * (dn - nh * _head_mean(dn * nh, ones_f))
            q = q_ref[e]
            sig, sn, f, g, k, sq = _hgrn_gates(q, z_ref[e], lbv)
            qs = q * sq
            b_s[e] = _sel_dot(_sub_tri(tt, True), g)
            qs_s[e] = qs
            k_s[e] = k
            kept.append((q, sig, sn, f, k, sq, qs))

        def chunk(cc, carry):
            c = ncs - 1 - cc
            base = pl.multiple_of(c * CHUNK, CHUNK)
            tiles = [[(qs_s[e, pl.ds(base + u * SUB, SUB), :], k_s[e, pl.ds(base + u * SUB, SUB), :],
                       v_ref[e, pl.ds(base + u * SUB, SUB), :], b_s[e, pl.ds(base + u * SUB, SUB), :],
                       do_s[e, pl.ds(base + u * SUB, SUB), :]) for u in range(nsub)] for e in range(nb)]
            sts = [s0_ref[e, c] for e in range(nb)]
            for u in range(nsub):
                for e in range(nb):
                    qs, k, v, b, do = tiles[e][u]
                    ss_s[e, u] = sts[e]
                    if u < nsub - 1:
                        sts[e] = _hgrn_state_step(sts[e], k, v, b, bmask)
            dss = [ds_s[e] for e in range(nb)]
            for u in reversed(range(nsub)):
                for e in range(nb):
                    qs, k, v, b, do = tiles[e][u]
                    _, es, ws = _hgrn_decays(qs, k, b)
                    gs = [_pad_rows(do[t:t + 1, :] * v[:_live_rows(t)]) for t in range(SUB)]
                    aexp = _dot(jnp.concatenate(ws, axis=0).astype(BF16), ones_b)
                    gexp = _dot(jnp.concatenate(gs, axis=0).astype(BF16), ones_b)
                    dq, dk, dv, dbl, dss[e] = _hgrn_state_bwd(qs, k, v, b, do, ss_s[e, u], dss[e], bmask)

                    def put_dq_row(i, row, e=e, r0=base + u * SUB):
                        dq_s[e, pl.ds(r0 + i, 1), :] = row

                    dk, dv = _hgrn_intra_bwd(qs, k, v, do, es, aexp, gexp, dq, dk, dv, put_dq_row)
                    dk_s[e, pl.ds(base + u * SUB, SUB), :] = dk
                    dv_s[e, pl.ds(base + u * SUB, SUB), :] = dv
                    dbl_s[e, pl.ds(base + u * SUB, SUB), :] = jnp.broadcast_to(dbl, (SUB, w))
            for e in range(nb):
                ds_s[e] = dss[e]
            return carry

        lax.fori_loop(0, ncs, chunk, 0)
        for e, (q, sig, sn, f, k, sq, qs) in enumerate(kept):
            dqs, dk = dq_s[e], dk_s[e]
            dg = _sel_dot(_sub_tri(tt, False), qs * dqs - k * dk) + dbl_s[e]
            dfz = jnp.where(f > TINY, dg / jnp.maximum(f, TINY), 0.0)
            dlb_ref[...] += jnp.sum(dfz * (1.0 - sig) - dk * sn, axis=0, keepdims=True)
            dp_ref[e, :, 0:w] = (dqs * (sq * (1.0 + q * (1.0 - sq)))).astype(BF16)
            dp_ref[e, :, w:2 * w] = ((dfz - dk) * (1.0 - lbv) * sig * sn).astype(BF16)
            dp_ref[e, :, 2 * w:3 * w] = dv_s[e].astype(BF16)

    def col(j):
        return pl.BlockSpec((nb, tt, w), lambda i: (0, nt - 1 - i, j))

    def full(a, bb):
        return pl.BlockSpec((a, bb), lambda i: (0, 0))

    proj3 = proj.reshape(nb, t, proj.shape[1])
    (dp, dgh, dlb), rode = _call(
        body, name, (dmix.reshape(nb, t, dmix.shape[1]), proj3, proj3, proj3, proj3, o_h.reshape(nb, t, w),
                     s0.reshape(nb, t // CHUNK, w, w), gh, lb, ones_b), rider, grid=(nt,),
        in_specs=[col(0), col(C_QA // w), col(C_FA // w), col(C_IA // w), col(C_GA // w), col(0),
                  pl.BlockSpec((nb, ncs, w, w), lambda i: (0, nt - 1 - i, 0, 0)), full(1, w), full(1, w), full(w, w)],
        out_specs=[pl.BlockSpec((nb, tt, 4 * w), lambda i: (0, nt - 1 - i, 0)), full(1, w), full(1, w)],
        out_shape=[_sds((nb, t, 4 * w), BF16), _sds((1, w), F32), _sds((1, w), F32)],
        scratch_shapes=[pltpu.VMEM((nb, w, w), F32), pltpu.VMEM((nb, nsub, w, w), F32)]
        + [pltpu.VMEM((nb, tt, w), F32)] * 8,
        compiler_params=_params(),
    )
    return (dp.reshape(n, 4 * w), dgh, dlb), rode


def _pool_bwd_tile(u_ref, g_ref, h_ref, dm_ref, gn_ref, dmn_ref, w_ref, wt_ref, s_ref, i, nt, tt):
    sc = s_ref[...]
    halo = jnp.where(i == 0, 0.0, h_ref[...])
    pooled, cnt = _pool_mix(u_ref[...], halo, i * tt, tt)
    pb = pooled.astype(BF16)
    pre = _dot(pb, w_ref[...])
    gv, dm = g_ref[...], dm_ref[...]
    sg = _sigmoid(gv)
    silu = gv * sg
    dgb = dm * pre * sc * (sg * (1.0 + gv * (1.0 - sg)))
    dscale = jnp.sum(dm * pre * silu, axis=0, keepdims=True)
    dpre = (dm * sc * silu).astype(BF16)
    dw = _dot_tn(pb, dpre)
    dpool = _dot(dpre, wt_ref[...])
    gn = gn_ref[...]
    dpre_n = (dmn_ref[...] * sc * (gn * _sigmoid(gn))).astype(BF16)
    dpool_n = jnp.where(i == nt - 1, 0.0, _dot(dpre_n, wt_ref[...]))
    lane, wl = _pool_lane_windows()
    tpos_n = ((i + 1) * tt + lax.broadcasted_iota(jnp.int32, (POOL_HALO, POOL_W), 0)).astype(F32)
    ext = jnp.concatenate([dpool / cnt, dpool_n / jnp.minimum(tpos_n + 1.0, wl)], axis=0)
    rows = tt + POOL_HALO
    sums, cur, shift = [], ext, 1
    for _ in POOL_WINDOWS:
        cur = cur + pltpu.roll(cur, rows - shift, axis=0)
        sums.append(cur[:tt, :])
        shift *= 2
    du = _pool_select(lane, sums) - dpool
    return jnp.concatenate([du, dgb], axis=1).astype(BF16), dw, dscale


def _fox_bwd(proj, qt, kt, da, dat, c_col, c_row, lse_row, delta_row, nb, name, rider=None):
    n = proj.shape[0]
    t = n // nb
    tb = min(256, t)
    nq = t // tb
    pw = 2 * HEAD
    qw = 2 if nq % 2 == 0 else 1
    nqs = nq // qw

    def body(q_ref, k_ref, v_ref, da_ref, qt_ref, kt_ref, dat_ref, cc_ref, cr_ref, lse_ref, dl_ref,
             dq_ref, dk_ref, dv_ref, dck_ref, dcq_ref, dq_s, dk_s, dv_s, dck_s, dcq_s):
        step = pl.program_id(1)
        kj, qq = pairs(step)

        @pl.when(step == 0)
        def _():
            dq_s[...] = jnp.zeros_like(dq_s)
            dcq_s[...] = jnp.zeros_like(dcq_s)

        @pl.when(qq == nqs - 1)
        def _():
            dk_s[...] = jnp.zeros_like(dk_s)
            dv_s[...] = jnp.zeros_like(dv_s)
            dck_s[...] = jnp.zeros_like(dck_s)

        def block(masked, sub):
            lo = _lane_lo()
            qi = qq * qw + sub
            qs = slice(sub * tb, (sub + 1) * tb)
            if masked:
                causal = lax.broadcasted_iota(jnp.int32, (tb, tb), 1) >= lax.broadcasted_iota(jnp.int32, (tb, tb), 0)
            dck = dck_s[...]
            for p in range(FOX_HEADS // 2):
                sl = slice(p * pw, (p + 1) * pw)
                qp = q_ref[qs, sl] * FOX_SCALE
                kp = k_ref[:, sl].astype(BF16)
                vp = v_ref[:, sl].astype(BF16)
                dap = da_ref[qs, sl]
                dk, dv = dk_s[:, sl], dv_s[:, sl]
                for h in range(2):
                    hh = 2 * p + h
                    lm = lo if h == 0 else jnp.logical_not(lo)
                    rows = slice(hh * HEAD, (hh + 1) * HEAD)
                    none = jnp.zeros((HEAD, tb), BF16)
                    qm = jnp.where(lm, qp, 0.0).astype(BF16)
                    dam = jnp.where(lm, dap, jnp.zeros_like(dap))
                    qtm = jnp.concatenate([qt_ref[rows, qs], none] if h == 0 else [none, qt_ref[rows, qs]], axis=0)
                    datm = jnp.concatenate([dat_ref[rows, qs], none] if h == 0 else [none, dat_ref[rows, qs]], axis=0)
                    s = _dot(kp, qtm) + (cr_ref[hh:hh + 1, qs] - cc_ref[:, hh:hh + 1])
                    pe = jnp.exp(s - lse_ref[hh:hh + 1, qs])
                    if masked:
                        pe = jnp.where(causal, pe, 0.0)
                    dp = _dot(vp, datm)
                    ds = pe * (dp - dl_ref[hh:hh + 1, qs])
                    dsb = ds.astype(BF16)
                    dv = dv + _dot(pe.astype(BF16), dam)
                    dk = dk + _dot(dsb, qm)
                    dq_s[qi, rows, :] += _dot(kt_ref[rows, :], dsb)
                    dck = dck - _put_col(jnp.zeros_like(dck), hh, jnp.sum(ds, axis=1, keepdims=True))
                    dcq_s[qi, hh:hh + 1, :] += _rows_reduce(ds, jnp.add, jnp.sum)
                dk_s[:, sl] = dk
                dv_s[:, sl] = dv
            dck_s[...] = dck

        for sub in reversed(range(qw)):
            @pl.when(qq * qw + sub > kj)
            def _(sub=sub):
                block(False, sub)

            @pl.when(qq * qw + sub == kj)
            def _(sub=sub):
                block(True, sub)

        @pl.when(qq == kj // qw)
        def _():
            dk_ref[...] = dk_s[...].astype(BF16)
            dv_ref[...] = dv_s[...].astype(BF16)
            dck_ref[...] = dck_s[...]

        @pl.when(step == _tri_steps(nq, qw) - 1)
        def _():
            for j in range(nq):
                dq_ref[j * tb:(j + 1) * tb, :] = (dq_s[j].T * FOX_SCALE).astype(BF16)
                dcq_ref[:, j * tb:(j + 1) * tb] = dcq_s[j]

    def pairs(step):
        a, b = _tri_pair(step, nq, qw)
        return nq - 1 - a, nqs - 1 - b

    def kspec(wd, j=0):
        return pl.BlockSpec((tb, wd), lambda b, st: (b * nq + pairs(st)[0], j))

    def qspec(wd, j=0):
        return pl.BlockSpec((qw * tb, wd), lambda b, st: (b * nqs + pairs(st)[1], j))

    def qrow():
        return pl.BlockSpec((None, FOX_HEADS, qw * tb), lambda b, st: (b, 0, pairs(st)[1]))

    def tspec(which):
        if which == 0:
            return pl.BlockSpec((FOX_W, tb), lambda b, st: (0, b * nq + pairs(st)[0]))
        return pl.BlockSpec((FOX_W, qw * tb), lambda b, st: (0, b * nqs + pairs(st)[1]))

    return _call(
        body, name, (proj, proj, proj, da, qt, kt, dat, c_col, c_row, lse_row, delta_row), rider,
        grid=(nb, _tri_steps(nq, qw)),
        in_specs=[qspec(FOX_W, C_QC // FOX_W), kspec(FOX_W, C_KC // FOX_W), kspec(FOX_W, C_VC // FOX_W), qspec(FOX_W),
                  tspec(1), tspec(0), tspec(1), kspec(FC_PAD), qrow(), qrow(), qrow()],
        out_specs=[pl.BlockSpec((t, FOX_W), lambda b, st: (b, 0)), kspec(FOX_W), kspec(FOX_W), kspec(FC_PAD),
                   pl.BlockSpec((None, FOX_HEADS, t), lambda b, st: (b, 0, 0))],
        out_shape=[_sds((n, FOX_W), BF16), _sds((n, FOX_W), BF16), _sds((n, FOX_W), BF16), _sds((n, FC_PAD), F32),
                   _sds((nb, FOX_HEADS, t), F32)],
        scratch_shapes=[pltpu.VMEM((nq, FOX_W, tb), F32), pltpu.VMEM((tb, FOX_W), F32), pltpu.VMEM((tb, FOX_W), F32),
                        pltpu.VMEM((tb, FC_PAD), F32), pltpu.VMEM((nq, FOX_HEADS, tb), F32)],
        compiler_params=_params(),
    )


def _fox_decay_bwd(dc_q, dc_k, fc, bias, nb, name):
    n = fc.shape[0]
    t = n // nb
    tt = min(256, t)
    nt = t // tt

    def body(dcq_ref, dck_ref, fc_ref, b_ref, dfc_ref, db_ref):
        @pl.when(pl.program_id(0) == 0)
        def _():
            db_ref[...] = jnp.zeros_like(db_ref)

        r = lax.broadcasted_iota(jnp.int32, (tt, tt), 0)
        cc = lax.broadcasted_iota(jnp.int32, (tt, tt), 1)
        carry = jnp.zeros((1, FC_PAD), F32)
        db = jnp.zeros((1, FC_PAD), F32)
        for i in reversed(range(nt)):
            rows = slice(i * tt, (i + 1) * tt)
            dcq = jnp.concatenate([dcq_ref[:, rows], jnp.zeros((FC_PAD - FOX_HEADS, tt), F32)], axis=0).T
            dlf = _sel_dot_exact(r <= cc, dcq + dck_ref[rows, :]) + carry
            carry = dlf[0:1, :]
            dfc = dlf * _sigmoid(-(fc_ref[rows, :] + b_ref[...]))
            dfc_ref[rows, :] = dfc.astype(BF16)
            db = db + jnp.sum(dfc, axis=0, keepdims=True)
        db_ref[...] += db

    def row():
        return pl.BlockSpec((t, FC_PAD), lambda b: (b, 0))

    return _pc(
        body, name, grid=(nb,),
        in_specs=[pl.BlockSpec((None, FOX_HEADS, t), lambda b: (b, 0, 0)), row(), row(),
                  pl.BlockSpec((1, FC_PAD), lambda b: (0, 0))],
        out_specs=[row(), pl.BlockSpec((1, FC_PAD), lambda b: (0, 0))],
        out_shape=[_sds((n, FC_PAD), BF16), _sds((1, FC_PAD), F32)],
        compiler_params=_params(),
    )(dc_q, dc_k, fc, bias)


def _in_proj_bwd(pieces, w_main_t, w_fc_t, x, g_pre, dxo, name, rider=None):
    n, d = x.shape
    tm = min(512, n)
    widths = [p.shape[1] for p, _ in pieces]
    offs = [o for _, o in pieces]
    np_ = len(pieces)

    def body(*refs):
        p_refs = refs[:np_]
        wt_ref, wf_ref, x_ref, g_ref, dxo_ref, dx_ref, dg_ref = refs[np_:]

        @pl.when(pl.program_id(0) == 0)
        def _():
            dg_ref[...] = jnp.zeros_like(dg_ref)

        dh = _dot(p_refs[-1][...], wf_ref[...])
        for pr, wd, off in zip(p_refs[:-1], widths[:-1], offs[:-1]):
            for j in range(0, wd, 512):
                jw = min(512, wd - j)
                dh = dh + _dot(pr[:, j:j + jw], wt_ref[off + j:off + j + jw, :])
        xv = x_ref[...]
        r = lax.rsqrt(jnp.mean(xv * xv, axis=-1, keepdims=True) + NORM_EPS)
        xh = xv * r
        dg_ref[...] += jnp.sum(dh * xh, axis=0, keepdims=True)
        dx_ref[...] = dxo_ref[...] + _rms_bwd(dh * g_ref[...], xh, r)

    row = pl.BlockSpec((tm, d), lambda i: (i, 0))
    return _call(
        body, name, (*[p for p, _ in pieces], w_main_t, w_fc_t, x, g_pre, dxo), rider, grid=(n // tm,),
        in_specs=[pl.BlockSpec((tm, wd), lambda i: (i, 0)) for wd in widths] + [
            pl.BlockSpec((MAIN_W, d), lambda i: (0, 0)), pl.BlockSpec((FC_PAD, d), lambda i: (0, 0)),
            row, pl.BlockSpec((1, d), lambda i: (0, 0)), row],
        out_specs=[row, pl.BlockSpec((1, d), lambda i: (0, 0))],
        out_shape=[_sds((n, d), F32), _sds((1, d), F32)],
        compiler_params=_params(),
    )


def _lower_bound_table(lower_bounds, name):
    depth, w = lower_bounds.shape

    def body(lb_ref, o_ref):
        v = lb_ref[...]
        e = jnp.exp(v - jnp.max(v, axis=0, keepdims=True))
        p = e / jnp.sum(e, axis=0, keepdims=True)
        acc = jnp.zeros((1, w), F32)
        for l in range(depth):
            acc = acc + p[l:l + 1, :]
            o_ref[l:l + 1, :] = acc - p[0:1, :]

    return _pc(body, name, out_shape=_sds((depth, w), F32))(lower_bounds)


def _lower_bound_bwd(lower_bounds, dlbs, name):
    depth, w = lower_bounds.shape

    def body(lb_ref, d_ref, o_ref):
        v, dl = lb_ref[...], d_ref[...]
        e = jnp.exp(v - jnp.max(v, axis=0, keepdims=True))
        p = e / jnp.sum(e, axis=0, keepdims=True)
        tot = jnp.sum(dl, axis=0, keepdims=True)
        rows, tail = [], tot
        for l in range(depth):
            rows.append(tail - tot if l == 0 else tail)
            tail = tail - dl[l:l + 1, :]
        dp = jnp.concatenate(rows, axis=0)
        o_ref[...] = p * (dp - jnp.sum(p * dp, axis=0, keepdims=True))

    return _pc(body, name, out_shape=_sds((depth, w), F32))(lower_bounds, dlbs)


def _place():
    x, y, c = lax.axis_index("x"), lax.axis_index("y"), lax.axis_index("c")
    return x, y, c


def _gather_weights(*arrays):
    na = len(arrays)

    def body(*refs):
        ins, outs = refs[:na], refs[na:2 * na]
        send_sems, recv_sems, local_sems = refs[2 * na:]
        x, y, c = _place()
        me, sibling = (x, y, c), (x, y, 1 - c)
        chips = [(1 - x, y), (x, 1 - y), (1 - x, 1 - y)]

        def slot(a, px, py, pc):
            return outs[a].at[4 * px + 2 * py + pc]

        def copy(a, k, block, to, own=False):
            return pltpu.make_async_remote_copy(
                src_ref=ins[a] if own else slot(a, *block), dst_ref=slot(a, *block),
                send_sem=send_sems.at[a * 7 + k], recv_sem=recv_sems.at[a * 7 + k],
                device_id=to, device_id_type=MESH)

        mine = [pltpu.make_async_copy(ins[a], slot(a, *me), local_sems.at[a]) for a in range(na)]
        for cp in mine:
            cp.start()
        first = []
        for a in range(na):
            first.append(copy(a, 0, me, sibling, own=True))
            first += [copy(a, 1 + j, me, (*chip, c), own=True) for j, chip in enumerate(chips)]
        for cp in first:
            cp.start()
        passed = []
        for j, chip in enumerate(chips):
            for a in range(na):
                copy(a, 1 + j, (*chip, c), me).wait_recv()
                fw = copy(a, 4 + j, (*chip, c), sibling)
                fw.start()
                passed.append(fw)
        for a in range(na):
            copy(a, 0, sibling, me).wait_recv()
            for j, chip in enumerate(chips):
                copy(a, 4 + j, (*chip, 1 - c), me).wait_recv()
        for cp in first + passed:
            cp.wait_send()
        for cp in mine:
            cp.wait()

    any_spec = pl.BlockSpec(memory_space=pl.ANY)
    return _pc(
        body, "gather_weights",
        in_specs=[any_spec] * na, out_specs=[any_spec] * na,
        out_shape=[_sds((N_DEV,) + a.shape, a.dtype) for a in arrays],
        scratch_shapes=[pltpu.SemaphoreType.DMA((7 * na,)), pltpu.SemaphoreType.DMA((7 * na,)),
                        pltpu.SemaphoreType.DMA((na,))],
    )(*arrays)


def _peer(k):
    x, y, c = _place()
    return (1 - x if k & 4 else x, 1 - y if k & 2 else y, 1 - c if k & 1 else c)


def _remote(src, dst, sems, s, to):
    return pltpu.make_async_remote_copy(src_ref=src, dst_ref=dst, send_sem=sems[0].at[s], recv_sem=sems[1].at[s],
                                        device_id=to, device_id_type=MESH)


def _gather_rider(shards):
    na = len(shards)

    def plan(ins, outs, *sems):
        x, y, c = _place()
        me = 4 * x + 2 * y + c
        locs = [pltpu.make_async_copy(ins[a], outs[a].at[me], sems[2].at[a]) for a in range(na)]
        sends, recvs = [], []
        for k in range(1, N_DEV):
            px, py, pc = _peer(k)
            for a in range(na):
                s = (k - 1) * na + a
                sends.append(_remote(ins[a], outs[a].at[me], sems, s, (px, py, pc)))
                recvs.append(_remote(ins[a], outs[a].at[4 * px + 2 * py + pc], sems, s, (px, py, pc)))
        return sends, recvs, locs

    return _Rider(shards, [_sds((N_DEV,) + a.shape, a.dtype) for a in shards], (N_DEV - 1) * na, na, plan)


def _direct_exchange_rider(blocks):
    na = len(blocks)

    def plan(ins, outs, *sems):
        x, y, c = _place()
        me = 4 * x + 2 * y + c
        locs = [pltpu.make_async_copy(ins[a].at[c, 2 * x + y], outs[a].at[me], sems[2].at[a]) for a in range(na)]
        sends, recvs = [], []
        for k in range(1, N_DEV):
            px, py, pc = _peer(k)
            for a in range(na):
                s = (k - 1) * na + a
                sends.append(_remote(ins[a].at[pc, 2 * px + py], outs[a].at[me], sems, s, (px, py, pc)))
                recvs.append(_remote(ins[a].at[pc, 2 * px + py], outs[a].at[4 * px + 2 * py + pc], sems, s, (px, py, pc)))
        return sends, recvs, locs

    return _Rider(blocks, [_sds((N_DEV,) + a.shape[2:], a.dtype) for a in blocks], (N_DEV - 1) * na, na, plan)


def _swap_rider(halves):
    na = len(halves)

    def plan(ins, outs, *sems):
        x, y, c = _place()
        cps = [_remote(ins[a].at[1 - c], outs[a], sems, a, (x, y, 1 - c)) for a in range(na)]
        return cps, cps, []

    return _Rider(halves, [_sds(a.shape[1:], a.dtype) for a in halves], na, 1, plan)


def _chip_exchange_rider(parts, small=None):
    na = len(parts)
    n_chip = N_DEV // 2

    def plan(ins, outs, *sems):
        x, y, c = _place()
        chip = 2 * x + y
        locs = [pltpu.make_async_copy(ins[a].at[chip], outs[a].at[chip], sems[2].at[a]) for a in range(na)]
        sends, recvs = [], []
        for k in range(1, n_chip):
            px, py, _ = _peer(2 * k)
            for a in range(na):
                s = (k - 1) * na + a
                sends.append(_remote(ins[a].at[2 * px + py], outs[a].at[chip], sems, s, (px, py, c)))
                recvs.append(_remote(ins[a].at[2 * px + py], outs[a].at[2 * px + py], sems, s, (px, py, c)))
        if small is not None:
            me = 2 * chip + c
            locs.append(pltpu.make_async_copy(ins[na], outs[na].at[me], sems[2].at[na]))
            for k in range(1, N_DEV):
                px, py, pc = _peer(k)
                s = (n_chip - 1) * na + k - 1
                sends.append(_remote(ins[na], outs[na].at[me], sems, s, (px, py, pc)))
                recvs.append(_remote(ins[na], outs[na].at[4 * px + 2 * py + pc], sems, s, (px, py, pc)))
        return sends, recvs, locs

    extra = [] if small is None else [small]
    shapes = [_sds(a.shape, a.dtype) for a in parts] + [_sds((N_DEV,) + s.shape, s.dtype) for s in extra]
    n_sems = (n_chip - 1) * na + (N_DEV - 1) * len(extra)
    return _Rider(list(parts) + extra, shapes, n_sems, na + len(extra), plan)


def _pair_add(halves, other, core, name):
    _, nch, r, c = halves.shape

    def body(c_ref, h_ref, o_ref, p_ref):
        p_ref[...] = (h_ref[...].astype(F32) + o_ref[...].astype(F32)).astype(BF16)

    blk = pl.BlockSpec((None, r, c), lambda j, c_ref: (j, 0, 0))
    return _pc(
        body, name,
        grid_spec=pltpu.PrefetchScalarGridSpec(
            num_scalar_prefetch=1, grid=(nch,),
            in_specs=[pl.BlockSpec((None, None, r, c), lambda j, c_ref: (c_ref[0], j, 0, 0)), blk], out_specs=blk),
        out_shape=_sds((nch, r, c), BF16),
        compiler_params=_params(),
    )(core, halves, other)


def _sum_adamw(parts, w, m, v, name, rider=None):
    nl, r, c = w.shape
    tr = 256 if r % 256 == 0 else r

    def body(*refs):
        p_refs = refs[:nl]
        w_ref, m_ref, v_ref, g_ref, d_ref, mo_ref, vo_ref = refs[nl:]
        for l in range(nl):
            @pl.when(pl.program_id(0) == l)
            def _(p_ref=p_refs[l]):
                g = p_ref[0].astype(F32)
                for j in range(1, p_ref.shape[0]):
                    g = g + p_ref[j].astype(F32)
                mn = ADAM_B1 * m_ref[...] + (1.0 - ADAM_B1) * g
                vn = ADAM_B2 * v_ref[...] + (1.0 - ADAM_B2) * (g * g)
                m_hat = mn / (1.0 - ADAM_B1 ** ADAM_STEP)
                v_hat = vn / (1.0 - ADAM_B2 ** ADAM_STEP)
                g_ref[...] = g
                d_ref[...] = -ADAM_LR * (m_hat / (jnp.sqrt(v_hat) + ADAM_EPS) + ADAM_WD * w_ref[...])
                mo_ref[...] = mn
                vo_ref[...] = vn

    def part_spec(l, k):
        return pl.BlockSpec((k, tr, c), lambda li, i: (0, jnp.where(li == l, i, 0), 0))

    row = pl.BlockSpec((None, tr, c), lambda li, i: (li, i, 0))
    return _call(
        body, name, (*parts, w, m, v), rider, grid=(nl, r // tr),
        in_specs=[part_spec(l, p.shape[0]) for l, p in enumerate(parts)] + [row, row, row],
        out_specs=[row] * 4,
        out_shape=[_sds((nl, r, c), F32)] * 4,
        compiler_params=_params(),
    )


SMALL = ("lower_bounds", "pre_norm_g", "hgrn_norm_g", "pool_w", "pool_scale", "post_norm_g", "fox_f_bias")
SMALL_LANES = 128


def _small_size(tree):
    return sum(tree[k].size for k in SMALL)


def _pack_small(tree, extra=None):
    flat = jnp.concatenate([tree[k].reshape(-1) for k in SMALL] + ([] if extra is None else [extra.reshape(1)]))
    rows = -(-(_small_size(tree) + 1) // (8 * SMALL_LANES)) * 8
    return jnp.pad(flat, (0, rows * SMALL_LANES - flat.shape[0])).reshape(rows, SMALL_LANES)


def _unpack_small(packed, like):
    out, off = {}, 0
    for k in SMALL:
        size = like[k].size
        assert off % SMALL_LANES == 0
        rows = packed[off // SMALL_LANES:-(-(off + size) // SMALL_LANES)]
        out[k] = rows.reshape(-1)[:size].reshape(like[k].shape)
        off += size
    return out


def _block_diag(pw):
    g = pw.shape[0]
    eye = jnp.eye(g, dtype=pw.dtype)
    return (eye[:, None, :, None] * pw[:, :, None, :]).reshape(g * HEAD, g * HEAD)


def _assemble_w_in(g_in, name):
    parts = g_in if isinstance(g_in, (list, tuple)) else [g_in]
    _, rows, shard = parts[0].shape
    d = rows * len(parts)
    tr = min(256, rows)
    per = rows // tr
    wide = MAIN_W + FC_PAD

    def body(*refs):
        g_refs, (wm_ref, wf_ref, wmt_ref, wft_ref, row_s) = refs[:len(parts)], refs[len(parts):]
        row_s[:, MAIN_W:] = jnp.zeros((tr, FC_PAD), F32)
        for p, g_ref in enumerate(g_refs):
            @pl.when(pl.program_id(0) // per == p)
            def _(g_ref=g_ref):
                for j in range(N_DEV):
                    row_s[:, j * shard:(j + 1) * shard] = g_ref[j].astype(F32)
        wm_ref[...] = row_s[:, :MAIN_W].astype(BF16)
        wf_ref[...] = row_s[:, MAIN_W:].astype(BF16)
        for j in range(0, MAIN_W, 512):
            wmt_ref[j:j + 512, :] = row_s[:, j:j + 512].T.astype(BF16)
        wft_ref[...] = row_s[:, MAIN_W:].T.astype(BF16)

    return _pc(
        body, name, grid=(d // tr,),
        in_specs=[pl.BlockSpec((N_DEV, tr, shard), lambda i, p=p: (0, jnp.clip(i - p * per, 0, per - 1), 0))
                  for p in range(len(parts))],
        out_specs=[pl.BlockSpec((tr, MAIN_W), lambda i: (i, 0)), pl.BlockSpec((tr, FC_PAD), lambda i: (i, 0)),
                   pl.BlockSpec((MAIN_W, tr), lambda i: (0, i)), pl.BlockSpec((FC_PAD, tr), lambda i: (0, i))],
        out_shape=[_sds((d, MAIN_W), BF16), _sds((d, FC_PAD), BF16), _sds((MAIN_W, d), BF16), _sds((FC_PAD, d), BF16)],
        scratch_shapes=[pltpu.VMEM((tr, wide), F32)],
        compiler_params=_params(),
    )(*parts)


def _w_out_parts(g_out):
    full_out = g_out.reshape(N_DEV * g_out.shape[1], g_out.shape[2])
    return full_out, full_out.T


def _layer_fwd(l, x, lbs, weights, lw, nb, rider_h=None, rider_c=None, target=None, rider_p=None):
    w_main, w_fc, _, _, w_out, _ = lw
    bias = jnp.pad(weights["fox_f_bias"][l:l + 1], ((0, 0), (0, FC_PAD - FOX_HEADS)))
    wbd = _block_diag(weights["pool_w"][l]).astype(BF16)
    (proj, fc, ht, qt, kt), rode_p = _in_proj_fwd(x, weights["pre_norm_g"][l:l + 1], w_main, w_fc, f"in_proj_fwd_{l}",
                                                  rider_p)
    c_col, c_row = _fox_decay_fwd(fc, bias, nb, f"fox_decay_fwd_{l}")
    (o_h, s0), rode_h = _hgrn_fwd(proj, lbs[l:l + 1], _block_ones(HGRN_W, BF16), nb, f"hgrn_fwd_{l}", rider_h)
    (o_c, lse), rode_c = _fox_fwd(proj, kt, c_col, c_row, nb, f"fox_fwd_{l}", rider_c)
    if w_out is None:
        lw = tuple(lw[:4]) + _w_out_parts(rode_p[0])
        w_out = lw[4]
    x_next, mixt, y = _merge_fwd(x, proj, o_h, o_c, weights["hgrn_norm_g"][l:l + 1], wbd, weights["pool_scale"][l:l + 1],
                                 w_out, weights["post_norm_g"][l:l + 1], nb, f"merge_fwd_{l}", target)
    return x_next, (x, proj, fc, ht, qt, kt, c_col, c_row, o_h, s0, o_c, lse, mixt, y, bias, wbd), lw, (rode_h, rode_c)


def _layer_bwd(l, dx, saved, lbs, weights, lw, nb, rider=None, send_w_out=None):
    _, proj, fc, ht, qt, kt, c_col, c_row, o_h, s0, o_c, lse, mixt, y, bias, wbd = saved
    w_out_t = lw[5]
    g = {}
    g["w_out"], dmix, dgp, da, dat, d_gc, delta = _merge_bwd(dx, y, weights["post_norm_g"][l:l + 1], w_out_t, proj, o_c,
                                                         mixt, nb, f"merge_bwd_{l}")
    g["post_norm_g"] = dgp[0]
    (d_a, dgh, dlb), arrived = _hgrn_bwd(dmix, proj, o_h, s0, weights["hgrn_norm_g"][l:l + 1], lbs[l:l + 1],
                                         _block_ones(HGRN_W, BF16), nb, f"hgrn_bwd_{l}",
                                         None if send_w_out is None else send_w_out([g["w_out"]]))
    if arrived is not None:
        g["w_out_received"] = arrived[0]
    g["hgrn_norm_g"], g["lbs"] = dgh[0], dlb[0]
    (d_qc, d_kc, d_vc, dc_k, dc_q), rode = _fox_bwd(proj, qt, kt, da, dat, c_col, c_row, lse, delta, nb,
                                                    f"fox_bwd_{l}", rider)
    d_fc, dbias = _fox_decay_bwd(dc_q, dc_k, fc, bias, nb, f"fox_decay_bwd_{l}")
    g["fox_f_bias"] = dbias[0, :FOX_HEADS]
    pieces = [(d_a, C_QA), (None, C_UB), (d_qc, C_QC), (d_kc, C_KC), (d_vc, C_VC), (d_gc, C_GC), (d_fc, None)]
    g["w_in"], d_b, dwbd, dps = _w_in_grad(ht, pieces, dmix, proj, wbd, wbd.T, weights["pool_scale"][l:l + 1], nb,
                                           f"w_in_grad_{l}")
    pieces[1] = (d_b, C_UB)
    g["pool_w"] = jnp.stack([dwbd[j * HEAD:(j + 1) * HEAD, j * HEAD:(j + 1) * HEAD] for j in range(len(POOL_WINDOWS))])
    g["pool_scale"] = dps[0]
    return g, pieces, rode


def _layer_bwd_input(l, dx, pieces, saved, weights, lw, rider=None):
    (dxi, dgpre), rode = _in_proj_bwd(pieces, lw[2], lw[3], saved[0], weights["pre_norm_g"][l:l + 1], dx,
                                      f"in_proj_bwd_{l}", rider)
    return dxi, dgpre[0], rode


def kernel(x, lower_bounds, pre_norm_g, w_in, hgrn_norm_g, fox_f_bias, pool_w, pool_scale, w_out, post_norm_g, loss_target, m_lower_bounds, m_pre_norm_g, m_w_in, m_hgrn_norm_g, m_fox_f_bias, m_pool_w, m_pool_scale, m_w_out, m_post_norm_g, v_lower_bounds, v_pre_norm_g, v_w_in, v_hgrn_norm_g, v_fox_f_bias, v_pool_w, v_pool_scale, v_w_out, v_post_norm_g):
    weights = dict(lower_bounds=lower_bounds, pre_norm_g=pre_norm_g, hgrn_norm_g=hgrn_norm_g, fox_f_bias=fox_f_bias,
                   pool_w=pool_w, pool_scale=pool_scale, post_norm_g=post_norm_g)
    mom_m = dict(lower_bounds=m_lower_bounds, pre_norm_g=m_pre_norm_g, hgrn_norm_g=m_hgrn_norm_g, fox_f_bias=m_fox_f_bias,
                 pool_w=m_pool_w, pool_scale=m_pool_scale, post_norm_g=m_post_norm_g)
    mom_v = dict(lower_bounds=v_lower_bounds, pre_norm_g=v_pre_norm_g, hgrn_norm_g=v_hgrn_norm_g, fox_f_bias=v_fox_f_bias,
                 pool_w=v_pool_w, pool_scale=v_pool_scale, post_norm_g=v_post_norm_g)
    depth = w_in.shape[0]
    nb, t, d = x.shape
    n = nb * t
    core = lax.axis_index("c").astype(jnp.int32).reshape(1)
    shards = [(w_in[l].astype(BF16), w_out[l].astype(BF16)) for l in range(depth)]
    lbs = _lower_bound_table(lower_bounds, "lower_bound_table")

    (g_in,) = _gather_weights(shards[0][0])
    coming = tuple(_assemble_w_in(g_in, "assemble_w_in_0")) + (None, None)
    xl, saved, lw = x.reshape(n, d), [], []
    half = d // 2
    for l in range(depth):
        last = l + 1 == depth
        xl, sv, lw_l, (rode_h, rode_c) = _layer_fwd(
            l, xl, lbs, weights, coming, nb, None if last else _gather_rider([shards[l + 1][0][:half]]),
            None if last else _gather_rider([shards[l + 1][0][half:], shards[l + 1][1]]),
            loss_target.reshape(n, d) if last else None,
            rider_p=_gather_rider([shards[l][1]]) if coming[4] is None else None)
        saved.append(sv)
        lw.append(lw_l)
        if not last:
            coming = (tuple(_assemble_w_in([rode_h[0], rode_c[0]], f"assemble_w_in_{l + 1}"))
                      + _w_out_parts(rode_c[1]))
    dx, sq = xl
    loss_here = 0.5 * jnp.sum(sq) / d

    grads, recv_in, pending = [None] * depth, [None] * depth, None
    for l in reversed(range(depth)):
        g, pieces, rode = _layer_bwd(l, dx, saved[l], lbs, weights, lw[l], nb, pending,
                                     None if l > 0 else _direct_exchange_rider)
        if rode is not None:
            recv_in[l + 1], grads[l + 1]["w_out_received"] = rode
        if l > 0:
            pending = _direct_exchange_rider([g["w_in"], g["w_out"]])
            dx, g["pre_norm_g"], _ = _layer_bwd_input(l, dx, pieces, saved[l], weights, lw[l])
        else:
            (other,) = _run_rider(_swap_rider([g["w_in"]]), "grad_swap")
            summed = _pair_add(g["w_in"], other, core, "grad_pair_add")
            dx, g["pre_norm_g"], (recv_in[l],) = _layer_bwd_input(l, dx, pieces, saved[l], weights, lw[l],
                                                                  _chip_exchange_rider([summed]))
        grads[l] = g
    small = {k: jnp.stack([grads[l][k] for l in range(depth)]) for k in SMALL if k != "lower_bounds"}
    small["lower_bounds"] = _lower_bound_bwd(lower_bounds, jnp.stack([grads[l]["lbs"] for l in range(depth)]),
                                             "lower_bound_bwd")
    (r_small,) = _run_rider(_gather_rider([_pack_small(small, loss_here)]), "small_grads_gather")

    res_in, _ = _sum_adamw(recv_in, w_in, m_w_in, v_w_in, "adamw_w_in")
    res_out, _ = _sum_adamw([grads[l]["w_out_received"] for l in range(depth)], w_out, m_w_out, v_w_out, "adamw_w_out")
    res_small, _ = _sum_adamw([r_small], _pack_small(weights)[None], _pack_small(mom_m)[None], _pack_small(mom_v)[None],
                              "adamw_small")
    loss = res_small[0][0][_small_size(weights) // SMALL_LANES, _small_size(weights) % SMALL_LANES]

    names = ("lower_bounds", "pre_norm_g", "w_in", "hgrn_norm_g", "fox_f_bias", "pool_w", "pool_scale", "w_out", "post_norm_g")
    outs = [loss, dx.reshape(nb, t, d)]
    for i in range(4):
        full = dict(_unpack_small(res_small[i][0], weights), w_in=res_in[i], w_out=res_out[i])
        outs += [full[k] for k in names]
    return tuple(outs)
```

```python
import functools

import jax
import jax.numpy as jnp
from jax import lax
from jax.experimental import pallas as pl
from jax.experimental.pallas import tpu as pltpu

F32, BF16 = jnp.float32, jnp.bfloat16
MESH = pl.DeviceIdType.MESH
N_DEV = 8

NORM_EPS = 1e-6
MASK_VALUE = -1e30
TINY = 1e-30
CHUNK = 64
SUB = 16
HGRN_W, POOL_W, FOX_W = 256, 256, 512
HEAD = 64
FOX_HEADS = 8
POOL_WINDOWS = (2, 4, 8, 16)
POOL_HALO = 16
MAIN_W = 3584
FC_PAD = 128
C_QA, C_FA, C_IA, C_GA, C_UB, C_GB, C_QC, C_KC, C_VC, C_GC = 0, 256, 512, 768, 1024, 1280, 1536, 2048, 2560, 3072
FOX_SCALE = HEAD ** -0.5

ADAM_LR, ADAM_B1, ADAM_B2, ADAM_EPS, ADAM_WD, ADAM_STEP = 0.001, 0.9, 0.999, 1e-08, 0.01, 10

VMEM_LIMIT = 56 * 1024 * 1024


def _pc(fn, name, **kw):
    return pl.pallas_call(fn, name=name, **kw)


def _params(**kw):
    return pltpu.CompilerParams(vmem_limit_bytes=VMEM_LIMIT, **kw)


class _Rider:
    def __init__(self, inputs, out_shapes, n_sems, n_local, plan):
        self.inputs, self.out_shapes, self.n_sems, self.n_local, self.plan = list(inputs), list(out_shapes), n_sems, n_local, plan

    def start(self, ins, outs, *sems):
        sends, _, locs = self.plan(ins, outs, *sems)
        for cp in locs + sends:
            cp.start()

    def wait(self, ins, outs, *sems):
        sends, recvs, locs = self.plan(ins, outs, *sems)
        for cp in recvs:
            cp.wait_recv()
        for cp in sends:
            cp.wait_send()
        for cp in locs:
            cp.wait()

    def sem_shapes(self):
        return [pltpu.SemaphoreType.DMA((self.n_sems,)), pltpu.SemaphoreType.DMA((self.n_sems,)),
                pltpu.SemaphoreType.DMA((self.n_local,))]


def _call(body, name, args, rider=None, *, grid, in_specs, out_specs, out_shape, scratch_shapes=(), **kw):
    if rider is None:
        res = _pc(body, name, grid=grid, in_specs=in_specs, out_specs=out_specs, out_shape=out_shape,
                  scratch_shapes=list(scratch_shapes), **kw)(*args)
        return res, None
    n_in, n_out, n_scr = len(in_specs), len(out_specs), len(scratch_shapes)
    n_rin, n_rout = len(rider.inputs), len(rider.out_shapes)

    def ridden(*refs):
        ins, refs = refs[:n_in], refs[n_in:]
        rins, refs = refs[:n_rin], refs[n_rin:]
        outs, refs = refs[:n_out], refs[n_out:]
        routs, refs = refs[:n_rout], refs[n_rout:]
        scr, sems = refs[:n_scr], refs[n_scr:]
        first = functools.reduce(jnp.logical_and, [pl.program_id(a) == 0 for a in range(len(grid))])
        last = functools.reduce(jnp.logical_and, [pl.program_id(a) == g - 1 for a, g in enumerate(grid)])

        @pl.when(first)
        def _():
            rider.start(rins, routs, *sems)

        body(*ins, *outs, *scr)

        @pl.when(last)
        def _():
            rider.wait(rins, routs, *sems)

    any_spec = pl.BlockSpec(memory_space=pl.ANY)
    res = _pc(ridden, name, grid=grid, in_specs=list(in_specs) + [any_spec] * n_rin,
              out_specs=list(out_specs) + [any_spec] * n_rout, out_shape=list(out_shape) + rider.out_shapes,
              scratch_shapes=list(scratch_shapes) + rider.sem_shapes(), **kw)(*args, *rider.inputs)
    return res[:n_out], res[n_out:]


def _run_rider(rider, name):
    n_rin = len(rider.inputs)

    def body(*refs):
        ins, outs, sems = refs[:n_rin], refs[n_rin:n_rin + len(rider.out_shapes)], refs[n_rin + len(rider.out_shapes):]
        rider.start(ins, outs, *sems)
        rider.wait(ins, outs, *sems)

    any_spec = pl.BlockSpec(memory_space=pl.ANY)
    return _pc(body, name, in_specs=[any_spec] * n_rin, out_specs=[any_spec] * len(rider.out_shapes),
               out_shape=rider.out_shapes, scratch_shapes=rider.sem_shapes())(*rider.inputs)


def _dot(a, b):
    return jnp.dot(a, b, preferred_element_type=F32)


def _dot_nt(a, b):
    return lax.dot_general(a, b, (((1,), (1,)), ((), ())), preferred_element_type=F32)


def _dot_tn(a, b):
    return lax.dot_general(a, b, (((0,), (0,)), ((), ())), preferred_element_type=F32)


def _sel_dot_exact(sel, x):
    hi = x.astype(BF16)
    rest = x - hi.astype(F32)
    mid = rest.astype(BF16)
    lo = (rest - mid.astype(F32)).astype(BF16)
    sb = sel.astype(BF16)
    return _dot(sb, hi) + _dot(sb, mid) + _dot(sb, lo)


def _split2(x):
    hi = x.astype(BF16)
    return hi, (x - hi.astype(F32)).astype(BF16)


def _sel_dot(sel, x):
    hi, lo = _split2(x)
    sb = sel.astype(BF16)
    return _dot(sb, hi) + _dot(sb, lo)


def _dot_sel(x, sel):
    hi, lo = _split2(x)
    sb = sel.astype(BF16)
    return _dot(hi, sb) + _dot(lo, sb)


def _sigmoid(x):
    return 1.0 / (1.0 + jnp.exp(-x))


def _block_ones(n, dtype):
    r = lax.broadcasted_iota(jnp.int32, (n, n), 0) // HEAD
    c = lax.broadcasted_iota(jnp.int32, (n, n), 1) // HEAD
    return (r == c).astype(dtype)


def _sds(shape, dtype):
    return jax.ShapeDtypeStruct(shape, dtype)


def _in_proj_fwd(x, g_pre, w_main, w_fc, name, rider=None):
    n, d = x.shape
    tm = min(512, n)

    def body(x_ref, g_ref, w_ref, wf_ref, proj_ref, fc_ref, ht_ref, qt_ref, kt_ref):
        xv = x_ref[...]
        r = lax.rsqrt(jnp.mean(xv * xv, axis=-1, keepdims=True) + NORM_EPS)
        hf = xv * r * g_ref[...]
        hb = hf.astype(BF16)
        ht_ref[...] = hf.T.astype(BF16)
        for j in range(0, MAIN_W, FOX_W):
            res = _dot(hb, w_ref[:, j:j + FOX_W])
            proj_ref[:, j:j + FOX_W] = res
            if j == C_QC:
                qt_ref[...] = (res * FOX_SCALE).T.astype(BF16)
            if j == C_KC:
                kt_ref[...] = res.T.astype(BF16)
        fc_ref[...] = _dot(hb, wf_ref[...])

    def cols(rows):
        return pl.BlockSpec((rows, tm), lambda i: (0, i))

    return _call(
        body, name, (x, g_pre, w_main, w_fc), rider, grid=(n // tm,),
        in_specs=[pl.BlockSpec((tm, d), lambda i: (i, 0)), pl.BlockSpec((1, d), lambda i: (0, 0)),
                  pl.BlockSpec((d, MAIN_W), lambda i: (0, 0)), pl.BlockSpec((d, FC_PAD), lambda i: (0, 0))],
        out_specs=[pl.BlockSpec((tm, MAIN_W), lambda i: (i, 0)), pl.BlockSpec((tm, FC_PAD), lambda i: (i, 0)),
                   cols(d), cols(FOX_W), cols(FOX_W)],
        out_shape=[_sds((n, MAIN_W), F32), _sds((n, FC_PAD), F32), _sds((d, n), BF16), _sds((FOX_W, n), BF16),
                   _sds((FOX_W, n), BF16)],
        compiler_params=_params(),
    )


def _fox_decay_fwd(fc, bias, nb, name):
    n = fc.shape[0]
    t = n // nb
    tt = min(256, t)
    nt = t // tt

    def body(fc_ref, b_ref, c_ref, cr_ref):
        r = lax.broadcasted_iota(jnp.int32, (tt, tt), 0)
        cc = lax.broadcasted_iota(jnp.int32, (tt, tt), 1)
        carry = jnp.zeros((1, FC_PAD), F32)
        for i in range(nt):
            rows = slice(i * tt, (i + 1) * tt)
            xv = fc_ref[rows, :] + b_ref[...]
            lf = jnp.minimum(xv, 0.0) - jnp.log(1.0 + jnp.exp(-jnp.abs(xv)))
            cs = _sel_dot_exact(r >= cc, lf) + carry
            c_ref[rows, :] = cs
            cr_ref[:, rows] = cs.T[:FOX_HEADS, :]
            carry = cs[tt - 1:tt, :]

    return _pc(
        body, name, grid=(nb,),
        in_specs=[pl.BlockSpec((t, FC_PAD), lambda b: (b, 0)), pl.BlockSpec((1, FC_PAD), lambda b: (0, 0))],
        out_specs=[pl.BlockSpec((t, FC_PAD), lambda b: (b, 0)), pl.BlockSpec((None, FOX_HEADS, t), lambda b: (b, 0, 0))],
        out_shape=[_sds((n, FC_PAD), F32), _sds((nb, FOX_HEADS, t), F32)],
        compiler_params=_params(),
    )(fc, bias)


def _hgrn_gates(q, z, lb):
    sig = _sigmoid(z)
    sn = _sigmoid(-z)
    f = lb + (1.0 - lb) * sig
    g = jnp.log(jnp.maximum(f, TINY))
    k = (1.0 - lb) * sn
    sq = _sigmoid(q)
    return sig, sn, f, g, k, sq


def _sub_tri(n, lower):
    r = lax.broadcasted_iota(jnp.int32, (n, n), 0)
    c = lax.broadcasted_iota(jnp.int32, (n, n), 1)
    tri = (r >= c) if lower else (r <= c)
    return jnp.logical_and(r // SUB == c // SUB, tri).astype(F32)


def _live_rows(t):
    return 8 * (t // 8 + 1)


def _pad_rows(x):
    return x if x.shape[0] == SUB else jnp.concatenate([x, jnp.zeros((SUB - x.shape[0], x.shape[1]), x.dtype)], axis=0)


def _hgrn_decays(qs, k, b):
    srow = lax.broadcasted_iota(jnp.int32, (SUB, HGRN_W), 0)
    es, ws = [], []
    for t in range(SUB):
        r = _live_rows(t)
        e = jnp.where(srow[:r] <= t, jnp.exp(b[t:t + 1, :] - b[:r]), 0.0)
        es.append(e)
        ws.append(_pad_rows(e * (qs[t:t + 1, :] * k[:r])))
    return srow, es, ws


def _hgrn_state_step(st, k, v, b, bmask):
    bl = b[SUB - 1:SUB, :]
    ktil = k * jnp.exp(bl - b)
    return st * jnp.exp(bl) + _dot_tn(v.astype(BF16), ktil.astype(BF16)) * bmask


def _hgrn_tile(t):
    return min(256, t)


def _hgrn_fwd(proj, lb, ones_b, nb, name, rider=None):
    n = proj.shape[0]
    t = n // nb
    tt = _hgrn_tile(t)
    nt = t // tt
    ncs = tt // CHUNK
    w = HGRN_W

    def body(q_ref, z_ref, v_ref, lb_ref, ones_ref, o_ref, s0_ref, st_s, b_s, qs_s, k_s):
        @pl.when(pl.program_id(0) == 0)
        def _():
            st_s[...] = jnp.zeros_like(st_s)

        for e in range(nb):
            q = q_ref[e]
            _, _, _, g, k, sq = _hgrn_gates(q, z_ref[e], lb_ref[...])
            b_s[e] = _sel_dot(_sub_tri(tt, True), g)
            qs_s[e] = q * sq
            k_s[e] = k
        bmask = _block_ones(w, F32)
        ones_b = ones_ref[...]
        nsub = CHUNK // SUB

        def chunk(c, carry):
            sts = [st_s[e] for e in range(nb)]
            for e in range(nb):
                s0_ref[e, c] = sts[e]
            base = pl.multiple_of(c * CHUNK, CHUNK)
            tiles = [[(qs_s[e, pl.ds(base + u * SUB, SUB), :], k_s[e, pl.ds(base + u * SUB, SUB), :],
                       v_ref[e, pl.ds(base + u * SUB, SUB), :], b_s[e, pl.ds(base + u * SUB, SUB), :])
                      for u in range(nsub)] for e in range(nb)]
            aexps = [[] for _ in range(nb)]
            for e in range(nb):
                for qs, k, v, b in tiles[e]:
                    _, _, ws = _hgrn_decays(qs, k, b)
                    aexps[e].append(_dot(jnp.concatenate(ws, axis=0).astype(BF16), ones_b))
            inters = [[] for _ in range(nb)]
            for u in range(nsub):
                for e in range(nb):
                    qs, k, v, b = tiles[e][u]
                    inters[e].append(_dot_nt((qs * jnp.exp(b)).astype(BF16), sts[e].astype(BF16)))
                    sts[e] = _hgrn_state_step(sts[e], k, v, b, bmask)
            for e in range(nb):
                st_s[e] = sts[e]
            for e in range(nb):
                for u, ((qs, k, v, b), aexp, o) in enumerate(zip(tiles[e], aexps[e], inters[e])):
                    for t in range(SUB):
                        r = _live_rows(t)
                        row = o[t:t + 1, :] + jnp.sum(aexp[t * SUB:t * SUB + r, :] * v[:r], axis=0, keepdims=True)
                        o_ref[e, pl.ds(base + u * SUB + t, 1), :] = row
            return carry

        lax.fori_loop(0, ncs, chunk, 0)

    def col(j):
        return pl.BlockSpec((nb, tt, w), lambda i: (0, i, j))

    proj3 = proj.reshape(nb, t, proj.shape[1])
    (o, s0), rode = _call(
        body, name, (proj3, proj3, proj3, lb, ones_b), rider, grid=(nt,),
        in_specs=[col(C_QA // w), col(C_FA // w), col(C_IA // w), pl.BlockSpec((1, w), lambda i: (0, 0)),
                  pl.BlockSpec((w, w), lambda i: (0, 0))],
        out_specs=[pl.BlockSpec((nb, tt, w), lambda i: (0, i, 0)),
                   pl.BlockSpec((nb, ncs, w, w), lambda i: (0, i, 0, 0))],
        out_shape=[_sds((nb, t, w), F32), _sds((nb, t // CHUNK, w, w), F32)],
        scratch_shapes=[pltpu.VMEM((nb, w, w), F32)] + [pltpu.VMEM((nb, tt, w), F32)] * 3,
        compiler_params=_params(),
    )
    return (o.reshape(n, w), s0.reshape(n // CHUNK, w, w)), rode


def _pool_lane_windows():
    lane = lax.broadcasted_iota(jnp.int32, (1, POOL_W), 1) // HEAD
    wl = jnp.zeros((1, POOL_W), F32)
    for gi, win in enumerate(POOL_WINDOWS):
        wl = jnp.where(lane == gi, float(win), wl)
    return lane, wl


def _pool_select(lane, parts):
    out = parts[-1]
    for gi in range(len(parts) - 2, -1, -1):
        out = jnp.where(lane == gi, parts[gi], out)
    return out


def _pool_mix(u, halo, t0, tt):
    lane, wl = _pool_lane_windows()
    ext = jnp.concatenate([halo, u], axis=0)
    sums, cur, shift = [], ext, 1
    for _ in POOL_WINDOWS:
        cur = cur + pltpu.roll(cur, shift, axis=0)
        sums.append(cur[POOL_HALO:, :])
        shift *= 2
    tpos = (t0 + lax.broadcasted_iota(jnp.int32, (tt, POOL_W), 0)).astype(F32)
    cnt = jnp.minimum(tpos + 1.0, wl)
    return _pool_select(lane, sums) / cnt - u, cnt


def _rows_reduce(x, op, final):
    while x.shape[0] > 8 and x.shape[0] % 16 == 0:
        half = x.shape[0] // 2
        x = op(x[:half], x[half:])
    return final(x, axis=0, keepdims=True)


def _tri_pair(step, n, group=1):
    counts = [a // group + 1 for a in range(n)]
    firsts = [sum(counts[:a]) for a in range(1, n)]
    a = sum([(step >= f).astype(jnp.int32) for f in firsts], jnp.int32(0))
    first = sum([jnp.where(step >= f, c, 0) for f, c in zip(firsts, counts)], jnp.int32(0))
    return a, step - first


def _tri_steps(n, group=1):
    return sum(a // group + 1 for a in range(n))


def _lane_lo():
    return lax.broadcasted_iota(jnp.int32, (1, 2 * HEAD), 1) < HEAD


def _put_col(tile, hh, colv):
    lane = lax.broadcasted_iota(jnp.int32, tile.shape, 1)
    return jnp.where(lane == hh, colv, tile)


def _fox_fwd(proj, kt, c_col, c_row, nb, name, rider=None):
    n = proj.shape[0]
    t = n // nb
    tb = min(256, t)
    nq = t // tb
    pw = 2 * HEAD
    kw = 2 if nq % 2 == 0 else 1
    nk = nq // kw

    def body(*refs):
        q_ref, kt_refs, (v_ref, cc_ref, cr_ref, o_ref, lse_ref, m_s, acc_s, cq_s) = refs[0], refs[1:1 + nb], refs[1 + nb:]
        qi, kk = _tri_pair(pl.program_id(0), nq, kw)

        @pl.when(kk == 0)
        def _():
            m_s[...] = jnp.full_like(m_s, -jnp.inf)
            acc_s[...] = jnp.zeros_like(acc_s)
            for e in range(nb):
                for hh in range(FOX_HEADS):
                    cq_s[e, hh] = jnp.broadcast_to(cc_ref[e, :, hh:hh + 1], (tb, pw))

        def block(masked, sub):
            lo = _lane_lo()
            keys = slice(sub * tb, (sub + 1) * tb)
            if masked:
                causal = lax.broadcasted_iota(jnp.int32, (tb, tb), 0) >= lax.broadcasted_iota(jnp.int32, (tb, tb), 1)
            def lanes(hh):
                return lo if hh % 2 == 0 else jnp.logical_not(lo)

            def scores(hh, e):
                sl = slice((hh // 2) * pw, (hh // 2 + 1) * pw)
                return _dot(jnp.where(lanes(hh), q_ref[e, :, sl] * FOX_SCALE, 0.0).astype(BF16), kt_refs[e][sl, keys])

            order = ([(hh, e) for e in range(nb) for hh in range(FOX_HEADS)] if masked else
                     [(hh, e) for hh in range(FOX_HEADS) for e in range(nb)])
            ahead = scores(*order[0])
            for j, (hh, e) in enumerate(order):
                s = ahead
                if j + 1 < len(order):
                    ahead = scores(*order[j + 1])
                s = s + (jnp.tile(cq_s[e, hh], (1, tb // pw)) - cr_ref[e, hh:hh + 1, keys])
                if masked:
                    s = jnp.where(causal, s, MASK_VALUE)
                m_prev = m_s[e, hh]
                m_new = jnp.maximum(m_prev, jnp.max(s, axis=1, keepdims=True))
                alpha = jnp.exp(m_prev - m_new)
                pe = jnp.exp(s - jnp.tile(m_new, (1, tb // pw)))
                m_s[e, hh] = m_new
                vf = v_ref[e, keys, (hh // 2) * pw:(hh // 2 + 1) * pw]
                acc_s[e, hh] = alpha * acc_s[e, hh] + _dot(pe.astype(BF16), jnp.where(lanes(hh), vf, 1.0).astype(BF16))

        def finish():
            lo = _lane_lo()
            for e in range(nb):
                m_all, l_all = jnp.zeros((tb, FC_PAD), F32), jnp.ones((tb, FC_PAD), F32)
                for p in range(FOX_HEADS // 2):
                    a0, a1 = acc_s[e, 2 * p], acc_s[e, 2 * p + 1]
                    both = pltpu.roll(jnp.where(lo, a1, a0), HEAD, axis=1)
                    o_ref[e, :, p * pw:(p + 1) * pw] = jnp.where(lo, a0, a1) / both
                    m_all = _put_col(_put_col(m_all, 2 * p, m_s[e, 2 * p]), 2 * p + 1, m_s[e, 2 * p + 1])
                    l_all = _put_col(_put_col(l_all, 2 * p, both), 2 * p + 1, a1)
                lse_ref[e] = (m_all + jnp.log(l_all)).T[:FOX_HEADS, :]

        for sub in range(kw):
            @pl.when(kk * kw + sub < qi)
            def _(sub=sub):
                block(False, sub)

            @pl.when(kk * kw + sub == qi)
            def _(sub=sub):
                block(True, sub)

        @pl.when(kk == qi // kw)
        def _():
            finish()

    def qspec(wd, j):
        return pl.BlockSpec((nb, tb, wd), lambda st: (0, _tri_pair(st, nq, kw)[0], j))

    def ktspec(e):
        return pl.BlockSpec((FOX_W, kw * tb), lambda st: (0, e * nk + _tri_pair(st, nq, kw)[1]))

    proj3 = proj.reshape(nb, t, proj.shape[1])
    (o, lse), rode = _call(
        body, name, (proj3, *[kt] * nb, proj3, c_col.reshape(nb, t, FC_PAD), c_row), rider, grid=(_tri_steps(nq, kw),),
        in_specs=[qspec(FOX_W, C_QC // FOX_W)] + [ktspec(e) for e in range(nb)] + [
            pl.BlockSpec((nb, kw * tb, FOX_W), lambda st: (0, _tri_pair(st, nq, kw)[1], C_VC // FOX_W)), qspec(FC_PAD, 0),
            pl.BlockSpec((nb, FOX_HEADS, kw * tb), lambda st: (0, 0, _tri_pair(st, nq, kw)[1]))],
        out_specs=[qspec(FOX_W, 0), pl.BlockSpec((nb, FOX_HEADS, tb), lambda st: (0, 0, _tri_pair(st, nq, kw)[0]))],
        out_shape=[_sds((nb, t, FOX_W), F32), _sds((nb, FOX_HEADS, t), F32)],
        scratch_shapes=[pltpu.VMEM((nb, FOX_HEADS, tb, pw), F32)] * 3,
        compiler_params=_params(),
    )
    return (o.reshape(n, FOX_W), lse), rode


def _head_mean(x, ones_f):
    return _dot_sel(x, ones_f) * (1.0 / HEAD)


def _merge_fwd(x, proj, o_h, o_c, gh, wbd, scale, w_out, g_post, nb, name, target=None):
    n, d = x.shape
    t = n // nb
    tm = min(512, t)
    nt = t // tm
    nhb = tm // POOL_HALO

    def body(*refs):
        x_ref, ga_ref, gc_ref, oh_ref, u_ref, gb_ref, h_ref, oc_ref, gh_ref, wb_ref, s_ref, w_ref, gp_ref = refs[:13]
        if target is None:
            xo_ref, mixt_ref, y_ref = refs[13:]
        else:
            t_ref, dx_ref, sq_ref, mixt_ref, y_ref = refs[13:]
        oh = oh_ref[...]
        ones_f = _block_ones(HGRN_W, F32)
        na = oh * lax.rsqrt(_head_mean(oh * oh, ones_f) + NORM_EPS) * gh_ref[...]
        ga, gb, gc = ga_ref[...], gb_ref[...], gc_ref[...]
        ti = pl.program_id(0) % nt
        pooled, _ = _pool_mix(u_ref[...], jnp.where(ti == 0, 0.0, h_ref[...]), ti * tm, tm)
        ob = _dot(pooled.astype(BF16), wb_ref[...]) * s_ref[...] * (gb * _sigmoid(gb))
        mixed = jnp.concatenate([na * (ga * _sigmoid(ga)), ob, oc_ref[...] * (gc * _sigmoid(gc))], axis=1)
        mixt_ref[...] = mixed.T.astype(BF16)
        y = _dot(mixed.astype(BF16), w_ref[...])
        y_ref[...] = y
        xn = x_ref[...] + y * lax.rsqrt(jnp.mean(y * y, axis=-1, keepdims=True) + NORM_EPS) * gp_ref[...]
        if target is None:
            xo_ref[...] = xn
        else:
            @pl.when(pl.program_id(0) == 0)
            def _():
                sq_ref[...] = jnp.zeros_like(sq_ref)

            e = xn - t_ref[...]
            dx_ref[...] = e * (1.0 / d)
            sq_ref[...] += jnp.sum(e * e, axis=0, keepdims=True)

    def row(wd, j=0):
        return pl.BlockSpec((tm, wd), lambda i: (i, j))

    def full(a, b):
        return pl.BlockSpec((a, b), lambda i: (0, 0))

    head = [] if target is None else [target]
    res = _pc(
        body, name, grid=(n // tm,),
        in_specs=[row(d), row(HGRN_W, C_GA // HGRN_W), row(FOX_W, C_GC // FOX_W), row(HGRN_W),
                  row(POOL_W, C_UB // POOL_W), row(POOL_W, C_GB // POOL_W),
                  pl.BlockSpec((POOL_HALO, POOL_W), lambda i: (jnp.maximum(i * nhb - 1, 0), C_UB // POOL_W)), row(FOX_W),
                  full(1, HGRN_W), full(POOL_W, POOL_W), full(1, POOL_W), full(d, d), full(1, d)] + [row(d)] * len(head),
        out_specs=[row(d)] + [full(1, d)] * len(head) + [pl.BlockSpec((d, tm), lambda i: (0, i)), row(d)],
        out_shape=[_sds((n, d), F32)] + [_sds((1, d), F32)] * len(head) + [_sds((d, n), BF16), _sds((n, d), F32)],
        compiler_params=_params(),
    )(x, proj, proj, o_h, proj, proj, proj, o_c, gh, wbd, scale, w_out, g_post, *head)
    return (res[0], res[1], res[2]) if target is None else ((res[0], res[1]), res[2], res[3])


def _rms_bwd(dy_scaled, xhat, r):
    return r * (dy_scaled - xhat * jnp.mean(dy_scaled * xhat, axis=-1, keepdims=True))


def _merge_bwd(dxo, y, g_post, w_out_t, proj, o_c, mixt, nb, name):
    n, d = y.shape
    t = n // nb
    tm = min(512, t)
    nt = t // tm
    wab = HGRN_W + POOL_W
    rows = d // N_DEV

    def body(dx_ref, y_ref, gp_ref, wt_ref, gc_ref, oc_ref, mt_ref,
             gw_ref, dm_ref, dgp_ref, da_ref, dat_ref, dg_ref, dl_ref, gw_s):
        @pl.when(pl.program_id(0) == 0)
        def _():
            dgp_ref[...] = jnp.zeros_like(dgp_ref)
            gw_s[...] = jnp.zeros_like(gw_s)

        yv, dxv = y_ref[...], dx_ref[...]
        r = lax.rsqrt(jnp.mean(yv * yv, axis=-1, keepdims=True) + NORM_EPS)
        yh = yv * r
        dgp_ref[...] += jnp.sum(dxv * yh, axis=0, keepdims=True)
        dyb = _rms_bwd(dxv * gp_ref[...], yh, r).astype(BF16)
        gw_s[...] += _dot(mt_ref[...], dyb)

        @pl.when(pl.program_id(0) == n // tm - 1)
        def _():
            for j in range(N_DEV):
                gw_ref[j % 2, j // 2] = gw_s[j * rows:(j + 1) * rows, :].astype(BF16)

        dm_ref[...] = _dot(dyb, wt_ref[:, :wab])
        dmc = _dot(dyb, wt_ref[:, wab:])
        gc, oc = gc_ref[...], oc_ref[...]
        sg = _sigmoid(gc)
        da = dmc * (gc * sg)
        da_ref[...] = da.astype(BF16)
        dat_ref[...] = da.T.astype(BF16)
        dg_ref[...] = (dmc * oc * (sg * (1.0 + gc * (1.0 - sg)))).astype(BF16)
        rr = lax.broadcasted_iota(jnp.int32, (FOX_W, FC_PAD), 0) // HEAD
        cc = lax.broadcasted_iota(jnp.int32, (FOX_W, FC_PAD), 1)
        dl_ref[...] = _dot_sel(da * oc, (rr == cc).astype(F32)).T[:FOX_HEADS, :]

    def row(wd, j=0):
        return pl.BlockSpec((tm, wd), lambda i: (i, j))

    def full(a, b):
        return pl.BlockSpec((a, b), lambda i: (0, 0))

    return _pc(
        body, name, grid=(n // tm,),
        in_specs=[row(d), row(d), full(1, d), full(d, d), row(FOX_W, C_GC // FOX_W), row(FOX_W),
                  pl.BlockSpec((d, tm), lambda i: (0, i))],
        out_specs=[pl.BlockSpec((2, N_DEV // 2, rows, d), lambda i: (0, 0, 0, 0)), row(wab), full(1, d), row(FOX_W),
                   pl.BlockSpec((FOX_W, tm), lambda i: (0, i)), row(FOX_W),
                   pl.BlockSpec((None, FOX_HEADS, tm), lambda i: (i // nt, 0, i % nt))],
        out_shape=[_sds((2, N_DEV // 2, rows, d), BF16), _sds((n, wab), F32), _sds((1, d), F32), _sds((n, FOX_W), BF16),
                   _sds((FOX_W, n), BF16), _sds((n, FOX_W), BF16), _sds((nb, FOX_HEADS, t), F32)],
        scratch_shapes=[pltpu.VMEM((d, d), F32)],
        compiler_params=_params(),
    )(dxo, y, g_post, w_out_t, proj, o_c, mixt)


def _w_in_grad(ht, pieces, dmix, proj, wbd, wbd_t, scale, nb, name):
    d, n = ht.shape
    t = n // nb
    ta, tk = d, min(512, t)
    nk, nt, nhb = n // tk, t // tk, tk // POOL_HALO
    given = [p for p, _ in pieces if p is not None]
    widths = [2 * POOL_W if p is None else p.shape[1] for p, _ in pieces]
    offs = [sum(widths[:i]) for i in range(len(widths))]
    in_w = MAIN_W + FOX_HEADS
    shard = in_w // N_DEV
    cu, cg, cm = C_UB // POOL_W, C_GB // POOL_W, HGRN_W // POOL_W

    def body(*refs):
        a_ref, p_refs = refs[0], list(refs[1:1 + len(given)])
        pool_refs = refs[1 + len(given):10 + len(given)]
        o_ref, db_ref, dw_ref, ds_ref, acc = refs[10 + len(given):]
        k = pl.program_id(1)

        @pl.when(k == 0)
        def _():
            dw_ref[...] = jnp.zeros_like(dw_ref)
            ds_ref[...] = jnp.zeros_like(ds_ref)
            acc[...] = jnp.zeros_like(acc)

        db, dw, dscale = _pool_bwd_tile(*pool_refs, k % nt, nt, tk)
        dw_ref[...] += dw
        ds_ref[...] += dscale
        db_ref[...] = db
        a = a_ref[...]
        for (p, _), off, wd in zip(pieces, offs, widths):
            pr = db if p is None else p_refs.pop(0)
            for j in range(0, wd, 512):
                jw = min(512, wd - j)
                acc[:, off + j:off + j + jw] += _dot(a, pr[:, j:j + jw])

        @pl.when(k == nk - 1)
        def _():
            for j in range(N_DEV):
                o_ref[j % 2, j // 2] = acc[:, j * shard:(j + 1) * shard].astype(BF16)

    def after(k):
        return jnp.minimum((k + 1) * nhb, n // POOL_HALO - 1)

    def tile(j):
        return pl.BlockSpec((tk, POOL_W), lambda i, k: (k, j))

    def full(r, c):
        return pl.BlockSpec((r, c), lambda i, k: (0, 0))

    pool_specs = [tile(cu), tile(cg),
                  pl.BlockSpec((POOL_HALO, POOL_W), lambda i, k: (jnp.maximum(k * nhb - 1, 0), cu)), tile(cm),
                  pl.BlockSpec((POOL_HALO, POOL_W), lambda i, k: (after(k), cg)),
                  pl.BlockSpec((POOL_HALO, POOL_W), lambda i, k: (after(k), cm)),
                  full(POOL_W, POOL_W), full(POOL_W, POOL_W), full(1, POOL_W)]
    return _pc(
        body, name, grid=(d // ta, nk),
        in_specs=[pl.BlockSpec((ta, tk), lambda i, k: (i, k))]
        + [pl.BlockSpec((tk, p.shape[1]), lambda i, k: (k, 0)) for p in given] + pool_specs,
        out_specs=[pl.BlockSpec((2, N_DEV // 2, ta, shard), lambda i, k: (0, 0, i, 0)),
                   pl.BlockSpec((tk, 2 * POOL_W), lambda i, k: (k, 0)), full(POOL_W, POOL_W), full(1, POOL_W)],
        out_shape=[_sds((2, N_DEV // 2, d, shard), BF16), _sds((n, 2 * POOL_W), BF16), _sds((POOL_W, POOL_W), F32),
                   _sds((1, POOL_W), F32)],
        scratch_shapes=[pltpu.VMEM((ta, sum(widths)), F32)],
        compiler_params=_params(),
    )(ht, *given, proj, proj, proj, dmix, proj, dmix, wbd, wbd_t, scale)


def _hgrn_state_bwd(qs, k, v, b, do, s0, ds1, bmask):
    bl = b[SUB - 1:SUB, :]
    eb, ebl, ekt = jnp.exp(b), jnp.exp(bl), jnp.exp(bl - b)
    qe, ktil = qs * eb, k * ekt
    ds1b, dob = ds1.astype(BF16), do.astype(BF16)
    dv = _dot_nt(ktil.astype(BF16), ds1b)
    dqe = _dot(dob, s0.astype(BF16))
    dktil = _dot(v.astype(BF16), ds1b)
    dbl = jnp.sum(dktil * ktil, axis=0, keepdims=True) + ebl * jnp.sum(s0 * ds1, axis=0, keepdims=True)
    ds0 = ds1 * ebl + _dot_tn(dob, qe.astype(BF16)) * bmask
    return dqe * eb, dktil * ekt, dv, dbl, ds0


def _hgrn_intra_bwd(qs, k, v, do, es, aexp, gexp, dq, dk, dv, put_dq_row):
    dks = [dk[j:j + 8] for j in range(0, SUB, 8)]
    dvs = [dv[j:j + 8] for j in range(0, SUB, 8)]
    for t in range(SUB):
        r = _live_rows(t)
        ge = gexp[t * SUB:t * SUB + r, :] * es[t]
        put_dq_row(t, dq[t:t + 1, :] + jnp.sum(ge * k[:r], axis=0, keepdims=True))
        for j in range(r // 8):
            dks[j] = dks[j] + ge[8 * j:8 * j + 8] * qs[t:t + 1, :]
            dvs[j] = dvs[j] + aexp[t * SUB + 8 * j:t * SUB + 8 * j + 8, :] * do[t:t + 1, :]
    return jnp.concatenate(dks, axis=0), jnp.concatenate(dvs, axis=0)


def _hgrn_bwd(dmix, proj, o_h, s0, gh, lb, ones_b, nb, name, rider=None):
    n = proj.shape[0]
    t = n // nb
    tt = _hgrn_tile(t)
    nt = t // tt
    ncs = tt // CHUNK
    nsub = CHUNK // SUB
    w = HGRN_W

    def body(dm_ref, q_ref, z_ref, v_ref, ga_ref, oh_ref, s0_ref, gh_ref, lb_ref, ones_ref,
             dp_ref, dgh_ref, dlb_ref, ds_s, ss_s, b_s, qs_s, k_s, do_s, dq_s, dk_s, dv_s, dbl_s):
        @pl.when(pl.program_id(0) == 0)
        def _():
            dgh_ref[...] = jnp.zeros_like(dgh_ref)
            dlb_ref[...] = jnp.zeros_like(dlb_ref)
            ds_s[...] = jnp.zeros_like(ds_s)

        ones_b = ones_ref[...]
        ones_f = ones_b.astype(F32)
        bmask = _block_ones(w, F32)
        lbv, ghv = lb_ref[...], gh_ref[...]
        kept = []
        for e in range(nb):
            oh, ga, dm = oh_ref[e], ga_ref[e], dm_ref[e]
            rn = lax.rsqrt(_head_mean(oh * oh, ones_f) + NORM_EPS)
            nh = oh * rn
            sga = _sigmoid(ga)
            dp_ref[e, :, 3 * w:4 * w] = (dm * nh * ghv * (sga * (1.0 + ga * (1.0 - sga)))).astype(BF16)
            dn = dm * (ga * sga)
            dgh_ref[...] += jnp.sum(dn * nh, axis=0, keepdims=True)
            dn = dn * ghv
            do_s[e] = rn * (dn - nh * _head_mean(dn * nh, ones_f))
            q = q_ref[e]
            sig, sn, f, g, k, sq = _hgrn_gates(q, z_ref[e], lbv)
            qs = q * sq
            b_s[e] = _sel_dot(_sub_tri(tt, True), g)
            qs_s[e] = qs
            k_s[e] = k
            kept.append((q, sig, sn, f, k, sq, qs))

        def chunk(cc, carry):
            c = ncs - 1 - cc
            base = pl.multiple_of(c * CHUNK, CHUNK)
            tiles = [[(qs_s[e, pl.ds(base + u * SUB, SUB), :], k_s[e, pl.ds(base + u * SUB, SUB), :],
                       v_ref[e, pl.ds(base + u * SUB, SUB), :], b_s[e, pl.ds(base + u * SUB, SUB), :],
                       do_s[e, pl.ds(base + u * SUB, SUB), :]) for u in range(nsub)] for e in range(nb)]
            sts = [s0_ref[e, c] for e in range(nb)]
            for u in range(nsub):
                for e in range(nb):
                    qs, k, v, b, do = tiles[e][u]
                    ss_s[e, u] = sts[e]
                    if u < nsub - 1:
                        sts[e] = _hgrn_state_step(sts[e], k, v, b, bmask)
            dss = [ds_s[e] for e in range(nb)]
            for u in reversed(range(nsub)):
                for e in range(nb):
                    qs, k, v, b, do = tiles[e][u]
                    _, es, ws = _hgrn_decays(qs, k, b)
                    gs = [_pad_rows(do[t:t + 1, :] * v[:_live_rows(t)]) for t in range(SUB)]
                    aexp = _dot(jnp.concatenate(ws, axis=0).astype(BF16), ones_b)
                    gexp = _dot(jnp.concatenate(gs, axis=0).astype(BF16), ones_b)
                    dq, dk, dv, dbl, dss[e] = _hgrn_state_bwd(qs, k, v, b, do, ss_s[e, u], dss[e], bmask)

                    def put_dq_row(i, row, e=e, r0=base + u * SUB):
                        dq_s[e, pl.ds(r0 + i, 1), :] = row

                    dk, dv = _hgrn_intra_bwd(qs, k, v, do, es, aexp, gexp, dq, dk, dv, put_dq_row)
                    dk_s[e, pl.ds(base + u * SUB, SUB), :] = dk
                    dv_s[e, pl.ds(base + u * SUB, SUB), :] = dv
                    dbl_s[e, pl.ds(base + u * SUB, SUB), :] = jnp.broadcast_to(dbl, (SUB, w))
            for e in range(nb):
                ds_s[e] = dss[e]
            return carry

        lax.fori_loop(0, ncs, chunk, 0)
        for e, (q, sig, sn, f, k, sq, qs) in enumerate(kept):
            dqs, dk = dq_s[e], dk_s[e]
            dg = _sel_dot(_sub_tri(tt, False), qs * dqs - k * dk) + dbl_s[e]
            dfz = jnp.where(f > TINY, dg / jnp.maximum(f, TINY), 0.0)
            dlb_ref[...] += jnp.sum(dfz * (1.0 - sig) - dk * sn, axis=0, keepdims=True)
            dp_ref[e, :, 0:w] = (dqs * (sq * (1.0 + q * (1.0 - sq)))).astype(BF16)
            dp_ref[e, :, w:2 * w] = ((dfz - dk) * (1.0 - lbv) * sig * sn).astype(BF16)
            dp_ref[e, :, 2 * w:3 * w] = dv_s[e].astype(BF16)

    def col(j):
        return pl.BlockSpec((nb, tt, w), lambda i: (0, nt - 1 - i, j))

    def full(a, bb):
        return pl.BlockSpec((a, bb), lambda i: (0, 0))

    proj3 = proj.reshape(nb, t, proj.shape[1])
    (dp, dgh, dlb), rode = _call(
        body, name, (dmix.reshape(nb, t, dmix.shape[1]), proj3, proj3, proj3, proj3, o_h.reshape(nb, t, w),
                     s0.reshape(nb, t // CHUNK, w, w), gh, lb, ones_b), rider, grid=(nt,),
        in_specs=[col(0), col(C_QA // w), col(C_FA // w), col(C_IA // w), col(C_GA // w), col(0),
                  pl.BlockSpec((nb, ncs, w, w), lambda i: (0, nt - 1 - i, 0, 0)), full(1, w), full(1, w), full(w, w)],
        out_specs=[pl.BlockSpec((nb, tt, 4 * w), lambda i: (0, nt - 1 - i, 0)), full(1, w), full(1, w)],
        out_shape=[_sds((nb, t, 4 * w), BF16), _sds((1, w), F32), _sds((1, w), F32)],
        scratch_shapes=[pltpu.VMEM((nb, w, w), F32), pltpu.VMEM((nb, nsub, w, w), F32)]
        + [pltpu.VMEM((nb, tt, w), F32)] * 8,
        compiler_params=_params(),
    )
    return (dp.reshape(n, 4 * w), dgh, dlb), rode


def _pool_bwd_tile(u_ref, g_ref, h_ref, dm_ref, gn_ref, dmn_ref, w_ref, wt_ref, s_ref, i, nt, tt):
    sc = s_ref[...]
    halo = jnp.where(i == 0, 0.0, h_ref[...])
    pooled, cnt = _pool_mix(u_ref[...], halo, i * tt, tt)
    pb = pooled.astype(BF16)
    pre = _dot(pb, w_ref[...])
    gv, dm = g_ref[...], dm_ref[...]
    sg = _sigmoid(gv)
    silu = gv * sg
    dgb = dm * pre * sc * (sg * (1.0 + gv * (1.0 - sg)))
    dscale = jnp.sum(dm * pre * silu, axis=0, keepdims=True)
    dpre = (dm * sc * silu).astype(BF16)
    dw = _dot_tn(pb, dpre)
    dpool = _dot(dpre, wt_ref[...])
    gn = gn_ref[...]
    dpre_n = (dmn_ref[...] * sc * (gn * _sigmoid(gn))).astype(BF16)
    dpool_n = jnp.where(i == nt - 1, 0.0, _dot(dpre_n, wt_ref[...]))
    lane, wl = _pool_lane_windows()
    tpos_n = ((i + 1) * tt + lax.broadcasted_iota(jnp.int32, (POOL_HALO, POOL_W), 0)).astype(F32)
    ext = jnp.concatenate([dpool / cnt, dpool_n / jnp.minimum(tpos_n + 1.0, wl)], axis=0)
    rows = tt + POOL_HALO
    sums, cur, shift = [], ext, 1
    for _ in POOL_WINDOWS:
        cur = cur + pltpu.roll(cur, rows - shift, axis=0)
        sums.append(cur[:tt, :])
        shift *= 2
    du = _pool_select(lane, sums) - dpool
    return jnp.concatenate([du, dgb], axis=1).astype(BF16), dw, dscale


def _fox_bwd(proj, qt, kt, da, dat, c_col, c_row, lse_row, delta_row, nb, name, rider=None):
    n = proj.shape[0]
    t = n // nb
    tb = min(256, t)
    nq = t // tb
    pw = 2 * HEAD
    qw = 2 if nq % 2 == 0 else 1
    nqs = nq // qw

    def body(q_ref, k_ref, v_ref, da_ref, qt_ref, kt_ref, dat_ref, cc_ref, cr_ref, lse_ref, dl_ref,
             dq_ref, dk_ref, dv_ref, dck_ref, dcq_ref, dq_s, dk_s, dv_s, dck_s, dcq_s):
        step = pl.program_id(1)
        kj, qq = pairs(step)

        @pl.when(step == 0)
        def _():
            dq_s[...] = jnp.zeros_like(dq_s)
            dcq_s[...] = jnp.zeros_like(dcq_s)

        @pl.when(qq == nqs - 1)
        def _():
            dk_s[...] = jnp.zeros_like(dk_s)
            dv_s[...] = jnp.zeros_like(dv_s)
            dck_s[...] = jnp.zeros_like(dck_s)

        def block(masked, sub):
            lo = _lane_lo()
            qi = qq * qw + sub
            qs = slice(sub * tb, (sub + 1) * tb)
            if masked:
                causal = lax.broadcasted_iota(jnp.int32, (tb, tb), 1) >= lax.broadcasted_iota(jnp.int32, (tb, tb), 0)
            dck = dck_s[...]
            for p in range(FOX_HEADS // 2):
                sl = slice(p * pw, (p + 1) * pw)
                qp = q_ref[qs, sl] * FOX_SCALE
                kp = k_ref[:, sl].astype(BF16)
                vp = v_ref[:, sl].astype(BF16)
                dap = da_ref[qs, sl]
                dk, dv = dk_s[:, sl], dv_s[:, sl]
                for h in range(2):
                    hh = 2 * p + h
                    lm = lo if h == 0 else jnp.logical_not(lo)
                    rows = slice(hh * HEAD, (hh + 1) * HEAD)
                    none = jnp.zeros((HEAD, tb), BF16)
                    qm = jnp.where(lm, qp, 0.0).astype(BF16)
                    dam = jnp.where(lm, dap, jnp.zeros_like(dap))
                    qtm = jnp.concatenate([qt_ref[rows, qs], none] if h == 0 else [none, qt_ref[rows, qs]], axis=0)
                    datm = jnp.concatenate([dat_ref[rows, qs], none] if h == 0 else [none, dat_ref[rows, qs]], axis=0)
                    s = _dot(kp, qtm) + (cr_ref[hh:hh + 1, qs] - cc_ref[:, hh:hh + 1])
                    pe = jnp.exp(s - lse_ref[hh:hh + 1, qs])
                    if masked:
                        pe = jnp.where(causal, pe, 0.0)
                    dp = _dot(vp, datm)
                    ds = pe * (dp - dl_ref[hh:hh + 1, qs])
                    dsb = ds.astype(BF16)
                    dv = dv + _dot(pe.astype(BF16), dam)
                    dk = dk + _dot(dsb, qm)
                    dq_s[qi, rows, :] += _dot(kt_ref[rows, :], dsb)
                    dck = dck - _put_col(jnp.zeros_like(dck), hh, jnp.sum(ds, axis=1, keepdims=True))
                    dcq_s[qi, hh:hh + 1, :] += _rows_reduce(ds, jnp.add, jnp.sum)
                dk_s[:, sl] = dk
                dv_s[:, sl] = dv
            dck_s[...] = dck

        for sub in reversed(range(qw)):
            @pl.when(qq * qw + sub > kj)
            def _(sub=sub):
                block(False, sub)

            @pl.when(qq * qw + sub == kj)
            def _(sub=sub):
                block(True, sub)

        @pl.when(qq == kj // qw)
        def _():
            dk_ref[...] = dk_s[...].astype(BF16)
            dv_ref[...] = dv_s[...].astype(BF16)
            dck_ref[...] = dck_s[...]

        @pl.when(step == _tri_steps(nq, qw) - 1)
        def _():
            for j in range(nq):
                dq_ref[j * tb:(j + 1) * tb, :] = (dq_s[j].T * FOX_SCALE).astype(BF16)
                dcq_ref[:, j * tb:(j + 1) * tb] = dcq_s[j]

    def pairs(step):
        a, b = _tri_pair(step, nq, qw)
        return nq - 1 - a, nqs - 1 - b

    def kspec(wd, j=0):
        return pl.BlockSpec((tb, wd), lambda b, st: (b * nq + pairs(st)[0], j))

    def qspec(wd, j=0):
        return pl.BlockSpec((qw * tb, wd), lambda b, st: (b * nqs + pairs(st)[1], j))

    def qrow():
        return pl.BlockSpec((None, FOX_HEADS, qw * tb), lambda b, st: (b, 0, pairs(st)[1]))

    def tspec(which):
        if which == 0:
            return pl.BlockSpec((FOX_W, tb), lambda b, st: (0, b * nq + pairs(st)[0]))
        return pl.BlockSpec((FOX_W, qw * tb), lambda b, st: (0, b * nqs + pairs(st)[1]))

    return _call(
        body, name, (proj, proj, proj, da, qt, kt, dat, c_col, c_row, lse_row, delta_row), rider,
        grid=(nb, _tri_steps(nq, qw)),
        in_specs=[qspec(FOX_W, C_QC // FOX_W), kspec(FOX_W, C_KC // FOX_W), kspec(FOX_W, C_VC // FOX_W), qspec(FOX_W),
                  tspec(1), tspec(0), tspec(1), kspec(FC_PAD), qrow(), qrow(), qrow()],
        out_specs=[pl.BlockSpec((t, FOX_W), lambda b, st: (b, 0)), kspec(FOX_W), kspec(FOX_W), kspec(FC_PAD),
                   pl.BlockSpec((None, FOX_HEADS, t), lambda b, st: (b, 0, 0))],
        out_shape=[_sds((n, FOX_W), BF16), _sds((n, FOX_W), BF16), _sds((n, FOX_W), BF16), _sds((n, FC_PAD), F32),
                   _sds((nb, FOX_HEADS, t), F32)],
        scratch_shapes=[pltpu.VMEM((nq, FOX_W, tb), F32), pltpu.VMEM((tb, FOX_W), F32), pltpu.VMEM((tb, FOX_W), F32),
                        pltpu.VMEM((tb, FC_PAD), F32), pltpu.VMEM((nq, FOX_HEADS, tb), F32)],
        compiler_params=_params(),
    )


def _fox_decay_bwd(dc_q, dc_k, fc, bias, nb, name):
    n = fc.shape[0]
    t = n // nb
    tt = min(256, t)
    nt = t // tt

    def body(dcq_ref, dck_ref, fc_ref, b_ref, dfc_ref, db_ref):
        @pl.when(pl.program_id(0) == 0)
        def _():
            db_ref[...] = jnp.zeros_like(db_ref)

        r = lax.broadcasted_iota(jnp.int32, (tt, tt), 0)
        cc = lax.broadcasted_iota(jnp.int32, (tt, tt), 1)
        carry = jnp.zeros((1, FC_PAD), F32)
        db = jnp.zeros((1, FC_PAD), F32)
        for i in reversed(range(nt)):
            rows = slice(i * tt, (i + 1) * tt)
            dcq = jnp.concatenate([dcq_ref[:, rows], jnp.zeros((FC_PAD - FOX_HEADS, tt), F32)], axis=0).T
            dlf = _sel_dot_exact(r <= cc, dcq + dck_ref[rows, :]) + carry
            carry = dlf[0:1, :]
            dfc = dlf * _sigmoid(-(fc_ref[rows, :] + b_ref[...]))
            dfc_ref[rows, :] = dfc.astype(BF16)
            db = db + jnp.sum(dfc, axis=0, keepdims=True)
        db_ref[...] += db

    def row():
        return pl.BlockSpec((t, FC_PAD), lambda b: (b, 0))

    return _pc(
        body, name, grid=(nb,),
        in_specs=[pl.BlockSpec((None, FOX_HEADS, t), lambda b: (b, 0, 0)), row(), row(),
                  pl.BlockSpec((1, FC_PAD), lambda b: (0, 0))],
        out_specs=[row(), pl.BlockSpec((1, FC_PAD), lambda b: (0, 0))],
        out_shape=[_sds((n, FC_PAD), BF16), _sds((1, FC_PAD), F32)],
        compiler_params=_params(),
    )(dc_q, dc_k, fc, bias)


def _in_proj_bwd(pieces, w_main_t, w_fc_t, x, g_pre, dxo, name, rider=None):
    n, d = x.shape
    tm = min(512, n)
    widths = [p.shape[1] for p, _ in pieces]
    offs = [o for _, o in pieces]
    np_ = len(pieces)

    def body(*refs):
        p_refs = refs[:np_]
        wt_ref, wf_ref, x_ref, g_ref, dxo_ref, dx_ref, dg_ref = refs[np_:]

        @pl.when(pl.program_id(0) == 0)
        def _():
            dg_ref[...] = jnp.zeros_like(dg_ref)

        dh = _dot(p_refs[-1][...], wf_ref[...])
        for pr, wd, off in zip(p_refs[:-1], widths[:-1], offs[:-1]):
            for j in range(0, wd, 512):
                jw = min(512, wd - j)
                dh = dh + _dot(pr[:, j:j + jw], wt_ref[off + j:off + j + jw, :])
        xv = x_ref[...]
        r = lax.rsqrt(jnp.mean(xv * xv, axis=-1, keepdims=True) + NORM_EPS)
        xh = xv * r
        dg_ref[...] += jnp.sum(dh * xh, axis=0, keepdims=True)
        dx_ref[...] = dxo_ref[...] + _rms_bwd(dh * g_ref[...], xh, r)

    row = pl.BlockSpec((tm, d), lambda i: (i, 0))
    return _call(
        body, name, (*[p for p, _ in pieces], w_main_t, w_fc_t, x, g_pre, dxo), rider, grid=(n // tm,),
        in_specs=[pl.BlockSpec((tm, wd), lambda i: (i, 0)) for wd in widths] + [
            pl.BlockSpec((MAIN_W, d), lambda i: (0, 0)), pl.BlockSpec((FC_PAD, d), lambda i: (0, 0)),
            row, pl.BlockSpec((1, d), lambda i: (0, 0)), row],
        out_specs=[row, pl.BlockSpec((1, d), lambda i: (0, 0))],
        out_shape=[_sds((n, d), F32), _sds((1, d), F32)],
        compiler_params=_params(),
    )


def _lower_bound_table(lower_bounds, name):
    depth, w = lower_bounds.shape

    def body(lb_ref, o_ref):
        v = lb_ref[...]
        e = jnp.exp(v - jnp.max(v, axis=0, keepdims=True))
        p = e / jnp.sum(e, axis=0, keepdims=True)
        acc = jnp.zeros((1, w), F32)
        for l in range(depth):
            acc = acc + p[l:l + 1, :]
            o_ref[l:l + 1, :] = acc - p[0:1, :]

    return _pc(body, name, out_shape=_sds((depth, w), F32))(lower_bounds)


def _lower_bound_bwd(lower_bounds, dlbs, name):
    depth, w = lower_bounds.shape

    def body(lb_ref, d_ref, o_ref):
        v, dl = lb_ref[...], d_ref[...]
        e = jnp.exp(v - jnp.max(v, axis=0, keepdims=True))
        p = e / jnp.sum(e, axis=0, keepdims=True)
        tot = jnp.sum(dl, axis=0, keepdims=True)
        rows, tail = [], tot
        for l in range(depth):
            rows.append(tail - tot if l == 0 else tail)
            tail = tail - dl[l:l + 1, :]
        dp = jnp.concatenate(rows, axis=0)
        o_ref[...] = p * (dp - jnp.sum(p * dp, axis=0, keepdims=True))

    return _pc(body, name, out_shape=_sds((depth, w), F32))(lower_bounds, dlbs)


def _place():
    x, y, c = lax.axis_index("x"), lax.axis_index("y"), lax.axis_index("c")
    return x, y, c


def _gather_weights(*arrays):
    na = len(arrays)

    def body(*refs):
        ins, outs = refs[:na], refs[na:2 * na]
        send_sems, recv_sems, local_sems = refs[2 * na:]
        x, y, c = _place()
        me, sibling = (x, y, c), (x, y, 1 - c)
        chips = [(1 - x, y), (x, 1 - y), (1 - x, 1 - y)]

        def slot(a, px, py, pc):
            return outs[a].at[4 * px + 2 * py + pc]

        def copy(a, k, block, to, own=False):
            return pltpu.make_async_remote_copy(
                src_ref=ins[a] if own else slot(a, *block), dst_ref=slot(a, *block),
                send_sem=send_sems.at[a * 7 + k], recv_sem=recv_sems.at[a * 7 + k],
                device_id=to, device_id_type=MESH)

        mine = [pltpu.make_async_copy(ins[a], slot(a, *me), local_sems.at[a]) for a in range(na)]
        for cp in mine:
            cp.start()
        first = []
        for a in range(na):
            first.append(copy(a, 0, me, sibling, own=True))
            first += [copy(a, 1 + j, me, (*chip, c), own=True) for j, chip in enumerate(chips)]
        for cp in first:
            cp.start()
        passed = []
        for j, chip in enumerate(chips):
            for a in range(na):
                copy(a, 1 + j, (*chip, c), me).wait_recv()
                fw = copy(a, 4 + j, (*chip, c), sibling)
                fw.start()
                passed.append(fw)
        for a in range(na):
            copy(a, 0, sibling, me).wait_recv()
            for j, chip in enumerate(chips):
                copy(a, 4 + j, (*chip, 1 - c), me).wait_recv()
        for cp in first + passed:
            cp.wait_send()
        for cp in mine:
            cp.wait()

    any_spec = pl.BlockSpec(memory_space=pl.ANY)
    return _pc(
        body, "gather_weights",
        in_specs=[any_spec] * na, out_specs=[any_spec] * na,
        out_shape=[_sds((N_DEV,) + a.shape, a.dtype) for a in arrays],
        scratch_shapes=[pltpu.SemaphoreType.DMA((7 * na,)), pltpu.SemaphoreType.DMA((7 * na,)),
                        pltpu.SemaphoreType.DMA((na,))],
    )(*arrays)


def _peer(k):
    x, y, c = _place()
    return (1 - x if k & 4 else x, 1 - y if k & 2 else y, 1 - c if k & 1 else c)


def _remote(src, dst, sems, s, to):
    return pltpu.make_async_remote_copy(src_ref=src, dst_ref=dst, send_sem=sems[0].at[s], recv_sem=sems[1].at[s],
                                        device_id=to, device_id_type=MESH)


def _gather_rider(shards):
    na = len(shards)

    def plan(ins, outs, *sems):
        x, y, c = _place()
        me = 4 * x + 2 * y + c
        locs = [pltpu.make_async_copy(ins[a], outs[a].at[me], sems[2].at[a]) for a in range(na)]
        sends, recvs = [], []
        for k in range(1, N_DEV):
            px, py, pc = _peer(k)
            for a in range(na):
                s = (k - 1) * na + a
                sends.append(_remote(ins[a], outs[a].at[me], sems, s, (px, py, pc)))
                recvs.append(_remote(ins[a], outs[a].at[4 * px + 2 * py + pc], sems, s, (px, py, pc)))
        return sends, recvs, locs

    return _Rider(shards, [_sds((N_DEV,) + a.shape, a.dtype) for a in shards], (N_DEV - 1) * na, na, plan)


def _direct_exchange_rider(blocks):
    na = len(blocks)

    def plan(ins, outs, *sems):
        x, y, c = _place()
        me = 4 * x + 2 * y + c
        locs = [pltpu.make_async_copy(ins[a].at[c, 2 * x + y], outs[a].at[me], sems[2].at[a]) for a in range(na)]
        sends, recvs = [], []
        for k in range(1, N_DEV):
            px, py, pc = _peer(k)
            for a in range(na):
                s = (k - 1) * na + a
                sends.append(_remote(ins[a].at[pc, 2 * px + py], outs[a].at[me], sems, s, (px, py, pc)))
                recvs.append(_remote(ins[a].at[pc, 2 * px + py], outs[a].at[4 * px + 2 * py + pc], sems, s, (px, py, pc)))
        return sends, recvs, locs

    return _Rider(blocks, [_sds((N_DEV,) + a.shape[2:], a.dtype) for a in blocks], (N_DEV - 1) * na, na, plan)


def _swap_rider(halves):
    na = len(halves)

    def plan(ins, outs, *sems):
        x, y, c = _place()
        cps = [_remote(ins[a].at[1 - c], outs[a], sems, a, (x, y, 1 - c)) for a in range(na)]
        return cps, cps, []

    return _Rider(halves, [_sds(a.shape[1:], a.dtype) for a in halves], na, 1, plan)


def _chip_exchange_rider(parts, small=None):
    na = len(parts)
    n_chip = N_DEV // 2

    def plan(ins, outs, *sems):
        x, y, c = _place()
        chip = 2 * x + y
        locs = [pltpu.make_async_copy(ins[a].at[chip], outs[a].at[chip], sems[2].at[a]) for a in range(na)]
        sends, recvs = [], []
        for k in range(1, n_chip):
            px, py, _ = _peer(2 * k)
            for a in range(na):
                s = (k - 1) * na + a
                sends.append(_remote(ins[a].at[2 * px + py], outs[a].at[chip], sems, s, (px, py, c)))
                recvs.append(_remote(ins[a].at[2 * px + py], outs[a].at[2 * px + py], sems, s, (px, py, c)))
        if small is not None:
            me = 2 * chip + c
            locs.append(pltpu.make_async_copy(ins[na], outs[na].at[me], sems[2].at[na]))
            for k in range(1, N_DEV):
                px, py, pc = _peer(k)
                s = (n_chip - 1) * na + k - 1
                sends.append(_remote(ins[na], outs[na].at[me], sems, s, (px, py, pc)))
                recvs.append(_remote(ins[na], outs[na].at[4 * px + 2 * py + pc], sems, s, (px, py, pc)))
        return sends, recvs, locs

    extra = [] if small is None else [small]
    shapes = [_sds(a.shape, a.dtype) for a in parts] + [_sds((N_DEV,) + s.shape, s.dtype) for s in extra]
    n_sems = (n_chip - 1) * na + (N_DEV - 1) * len(extra)
    return _Rider(list(parts) + extra, shapes, n_sems, na + len(extra), plan)


def _pair_add(halves, other, core, name):
    _, nch, r, c = halves.shape

    def body(c_ref, h_ref, o_ref, p_ref):
        p_ref[...] = (h_ref[...].astype(F32) + o_ref[...].astype(F32)).astype(BF16)

    blk = pl.BlockSpec((None, r, c), lambda j, c_ref: (j, 0, 0))
    return _pc(
        body, name,
        grid_spec=pltpu.PrefetchScalarGridSpec(
            num_scalar_prefetch=1, grid=(nch,),
            in_specs=[pl.BlockSpec((None, None, r, c), lambda j, c_ref: (c_ref[0], j, 0, 0)), blk], out_specs=blk),
        out_shape=_sds((nch, r, c), BF16),
        compiler_params=_params(),
    )(core, halves, other)


def _sum_adamw(parts, w, m, v, name, rider=None):
    nl, r, c = w.shape
    tr = 256 if r % 256 == 0 else r

    def body(*refs):
        p_refs = refs[:nl]
        w_ref, m_ref, v_ref, g_ref, d_ref, mo_ref, vo_ref = refs[nl:]
        for l in range(nl):
            @pl.when(pl.program_id(0) == l)
            def _(p_ref=p_refs[l]):
                g = p_ref[0].astype(F32)
                for j in range(1, p_ref.shape[0]):
                    g = g + p_ref[j].astype(F32)
                mn = ADAM_B1 * m_ref[...] + (1.0 - ADAM_B1) * g
                vn = ADAM_B2 * v_ref[...] + (1.0 - ADAM_B2) * (g * g)
                m_hat = mn / (1.0 - ADAM_B1 ** ADAM_STEP)
                v_hat = vn / (1.0 - ADAM_B2 ** ADAM_STEP)
                g_ref[...] = g
                d_ref[...] = -ADAM_LR * (m_hat / (jnp.sqrt(v_hat) + ADAM_EPS) + ADAM_WD * w_ref[...])
                mo_ref[...] = mn
                vo_ref[...] = vn

    def part_spec(l, k):
        return pl.BlockSpec((k, tr, c), lambda li, i: (0, jnp.where(li == l, i, 0), 0))

    row = pl.BlockSpec((None, tr, c), lambda li, i: (li, i, 0))
    return _call(
        body, name, (*parts, w, m, v), rider, grid=(nl, r // tr),
        in_specs=[part_spec(l, p.shape[0]) for l, p in enumerate(parts)] + [row, row, row],
        out_specs=[row] * 4,
        out_shape=[_sds((nl, r, c), F32)] * 4,
        compiler_params=_params(),
    )


SMALL = ("lower_bounds", "pre_norm_g", "hgrn_norm_g", "pool_w", "pool_scale", "post_norm_g", "fox_f_bias")
SMALL_LANES = 128


def _small_size(tree):
    return sum(tree[k].size for k in SMALL)


def _pack_small(tree, extra=None):
    flat = jnp.concatenate([tree[k].reshape(-1) for k in SMALL] + ([] if extra is None else [extra.reshape(1)]))
    rows = -(-(_small_size(tree) + 1) // (8 * SMALL_LANES)) * 8
    return jnp.pad(flat, (0, rows * SMALL_LANES - flat.shape[0])).reshape(rows, SMALL_LANES)


def _unpack_small(packed, like):
    out, off = {}, 0
    for k in SMALL:
        size = like[k].size
        assert off % SMALL_LANES == 0
        rows = packed[off // SMALL_LANES:-(-(off + size) // SMALL_LANES)]
        out[k] = rows.reshape(-1)[:size].reshape(like[k].shape)
        off += size
    return out


def _block_diag(pw):
    g = pw.shape[0]
    eye = jnp.eye(g, dtype=pw.dtype)
    return (eye[:, None, :, None] * pw[:, :, None, :]).reshape(g * HEAD, g * HEAD)


def _assemble_w_in(g_in, name):
    parts = g_in if isinstance(g_in, (list, tuple)) else [g_in]
    _, rows, shard = parts[0].shape
    d = rows * len(parts)
    tr = min(256, rows)
    per = rows // tr
    wide = MAIN_W + FC_PAD

    def body(*refs):
        g_refs, (wm_ref, wf_ref, wmt_ref, wft_ref, row_s) = refs[:len(parts)], refs[len(parts):]
        row_s[:, MAIN_W:] = jnp.zeros((tr, FC_PAD), F32)
        for p, g_ref in enumerate(g_refs):
            @pl.when(pl.program_id(0) // per == p)
            def _(g_ref=g_ref):
                for j in range(N_DEV):
                    row_s[:, j * shard:(j + 1) * shard] = g_ref[j].astype(F32)
        wm_ref[...] = row_s[:, :MAIN_W].astype(BF16)
        wf_ref[...] = row_s[:, MAIN_W:].astype(BF16)
        for j in range(0, MAIN_W, 512):
            wmt_ref[j:j + 512, :] = row_s[:, j:j + 512].T.astype(BF16)
        wft_ref[...] = row_s[:, MAIN_W:].T.astype(BF16)

    return _pc(
        body, name, grid=(d // tr,),
        in_specs=[pl.BlockSpec((N_DEV, tr, shard), lambda i, p=p: (0, jnp.clip(i - p * per, 0, per - 1), 0))
                  for p in range(len(parts))],
        out_specs=[pl.BlockSpec((tr, MAIN_W), lambda i: (i, 0)), pl.BlockSpec((tr, FC_PAD), lambda i: (i, 0)),
                   pl.BlockSpec((MAIN_W, tr), lambda i: (0, i)), pl.BlockSpec((FC_PAD, tr), lambda i: (0, i))],
        out_shape=[_sds((d, MAIN_W), BF16), _sds((d, FC_PAD), BF16), _sds((MAIN_W, d), BF16), _sds((FC_PAD, d), BF16)],
        scratch_shapes=[pltpu.VMEM((tr, wide), F32)],
        compiler_params=_params(),
    )(*parts)


def _w_out_parts(g_out):
    full_out = g_out.reshape(N_DEV * g_out.shape[1], g_out.shape[2])
    return full_out, full_out.T


def _layer_fwd(l, x, lbs, weights, lw, nb, rider_h=None, rider_c=None, target=None, rider_p=None):
    w_main, w_fc, _, _, w_out, _ = lw
    bias = jnp.pad(weights["fox_f_bias"][l:l + 1], ((0, 0), (0, FC_PAD - FOX_HEADS)))
    wbd = _block_diag(weights["pool_w"][l]).astype(BF16)
    (proj, fc, ht, qt, kt), rode_p = _in_proj_fwd(x, weights["pre_norm_g"][l:l + 1], w_main, w_fc, f"in_proj_fwd_{l}",
                                                  rider_p)
    c_col, c_row = _fox_decay_fwd(fc, bias, nb, f"fox_decay_fwd_{l}")
    (o_h, s0), rode_h = _hgrn_fwd(proj, lbs[l:l + 1], _block_ones(HGRN_W, BF16), nb, f"hgrn_fwd_{l}", rider_h)
    (o_c, lse), rode_c = _fox_fwd(proj, kt, c_col, c_row, nb, f"fox_fwd_{l}", rider_c)
    if w_out is None:
        lw = tuple(lw[:4]) + _w_out_parts(rode_p[0])
        w_out = lw[4]
    x_next, mixt, y = _merge_fwd(x, proj, o_h, o_c, weights["hgrn_norm_g"][l:l + 1], wbd, weights["pool_scale"][l:l + 1],
                                 w_out, weights["post_norm_g"][l:l + 1], nb, f"merge_fwd_{l}", target)
    return x_next, (x, proj, fc, ht, qt, kt, c_col, c_row, o_h, s0, o_c, lse, mixt, y, bias, wbd), lw, (rode_h, rode_c)


def _layer_bwd(l, dx, saved, lbs, weights, lw, nb, rider=None, send_w_out=None):
    _, proj, fc, ht, qt, kt, c_col, c_row, o_h, s0, o_c, lse, mixt, y, bias, wbd = saved
    w_out_t = lw[5]
    g = {}
    g["w_out"], dmix, dgp, da, dat, d_gc, delta = _merge_bwd(dx, y, weights["post_norm_g"][l:l + 1], w_out_t, proj, o_c,
                                                         mixt, nb, f"merge_bwd_{l}")
    g["post_norm_g"] = dgp[0]
    (d_a, dgh, dlb), arrived = _hgrn_bwd(dmix, proj, o_h, s0, weights["hgrn_norm_g"][l:l + 1], lbs[l:l + 1],
                                         _block_ones(HGRN_W, BF16), nb, f"hgrn_bwd_{l}",
                                         None if send_w_out is None else send_w_out([g["w_out"]]))
    if arrived is not None:
        g["w_out_received"] = arrived[0]
    g["hgrn_norm_g"], g["lbs"] = dgh[0], dlb[0]
    if callable(rider):
        rider = rider([g["w_out"]])
    (d_qc, d_kc, d_vc, dc_k, dc_q), rode = _fox_bwd(proj, qt, kt, da, dat, c_col, c_row, lse, delta, nb,
                                                    f"fox_bwd_{l}", rider)
    d_fc, dbias = _fox_decay_bwd(dc_q, dc_k, fc, bias, nb, f"fox_decay_bwd_{l}")
    g["fox_f_bias"] = dbias[0, :FOX_HEADS]
    pieces = [(d_a, C_QA), (None, C_UB), (d_qc, C_QC), (d_kc, C_KC), (d_vc, C_VC), (d_gc, C_GC), (d_fc, None)]
    g["w_in"], d_b, dwbd, dps = _w_in_grad(ht, pieces, dmix, proj, wbd, wbd.T, weights["pool_scale"][l:l + 1], nb,
                                           f"w_in_grad_{l}")
    pieces[1] = (d_b, C_UB)
    g["pool_w"] = jnp.stack([dwbd[j * HEAD:(j + 1) * HEAD, j * HEAD:(j + 1) * HEAD] for j in range(len(POOL_WINDOWS))])
    g["pool_scale"] = dps[0]
    return g, pieces, rode


def _layer_bwd_input(l, dx, pieces, saved, weights, lw, rider=None):
    (dxi, dgpre), rode = _in_proj_bwd(pieces, lw[2], lw[3], saved[0], weights["pre_norm_g"][l:l + 1], dx,
                                      f"in_proj_bwd_{l}", rider)
    return dxi, dgpre[0], rode


def kernel(x, lower_bounds, pre_norm_g, w_in, hgrn_norm_g, fox_f_bias, pool_w, pool_scale, w_out, post_norm_g, loss_target, m_lower_bounds, m_pre_norm_g, m_w_in, m_hgrn_norm_g, m_fox_f_bias, m_pool_w, m_pool_scale, m_w_out, m_post_norm_g, v_lower_bounds, v_pre_norm_g, v_w_in, v_hgrn_norm_g, v_fox_f_bias, v_pool_w, v_pool_scale, v_w_out, v_post_norm_g):
    weights = dict(lower_bounds=lower_bounds, pre_norm_g=pre_norm_g, hgrn_norm_g=hgrn_norm_g, fox_f_bias=fox_f_bias,
                   pool_w=pool_w, pool_scale=pool_scale, post_norm_g=post_norm_g)
    mom_m = dict(lower_bounds=m_lower_bounds, pre_norm_g=m_pre_norm_g, hgrn_norm_g=m_hgrn_norm_g, fox_f_bias=m_fox_f_bias,
                 pool_w=m_pool_w, pool_scale=m_pool_scale, post_norm_g=m_post_norm_g)
    mom_v = dict(lower_bounds=v_lower_bounds, pre_norm_g=v_pre_norm_g, hgrn_norm_g=v_hgrn_norm_g, fox_f_bias=v_fox_f_bias,
                 pool_w=v_pool_w, pool_scale=v_pool_scale, post_norm_g=v_post_norm_g)
    depth = w_in.shape[0]
    nb, t, d = x.shape
    n = nb * t
    core = lax.axis_index("c").astype(jnp.int32).reshape(1)
    shards = [(w_in[l].astype(BF16), w_out[l].astype(BF16)) for l in range(depth)]
    lbs = _lower_bound_table(lower_bounds, "lower_bound_table")

    (g_in,) = _gather_weights(shards[0][0])
    coming = tuple(_assemble_w_in(g_in, "assemble_w_in_0")) + (None, None)
    xl, saved, lw = x.reshape(n, d), [], []
    half = d // 2
    for l in range(depth):
        last = l + 1 == depth
        xl, sv, lw_l, (rode_h, rode_c) = _layer_fwd(
            l, xl, lbs, weights, coming, nb, None if last else _gather_rider([shards[l + 1][0][:half]]),
            None if last else _gather_rider([shards[l + 1][0][half:], shards[l + 1][1]]),
            loss_target.reshape(n, d) if last else None,
            rider_p=_gather_rider([shards[l][1]]) if coming[4] is None else None)
        saved.append(sv)
        lw.append(lw_l)
        if not last:
            coming = (tuple(_assemble_w_in([rode_h[0], rode_c[0]], f"assemble_w_in_{l + 1}"))
                      + _w_out_parts(rode_c[1]))
    dx, sq = xl
    loss_here = 0.5 * jnp.sum(sq) / d

    grads, recv_in, pending = [None] * depth, [None] * depth, None
    for l in reversed(range(depth)):
        g, pieces, rode = _layer_bwd(l, dx, saved[l], lbs, weights, lw[l], nb, pending,
                                     _direct_exchange_rider if pending is None and l == 0 else None)
        if rode is not None:
            recv_in[l + 1], grads[l + 1]["w_out_received"], g["w_out_received"] = rode
        if l > 0:
            pending = functools.partial(lambda own, more: _direct_exchange_rider(own + more), [g["w_in"], g["w_out"]])
            dx, g["pre_norm_g"], _ = _layer_bwd_input(l, dx, pieces, saved[l], weights, lw[l])
        else:
            (other,) = _run_rider(_swap_rider([g["w_in"]]), "grad_swap")
            summed = _pair_add(g["w_in"], other, core, "grad_pair_add")
            dx, g["pre_norm_g"], (recv_in[l],) = _layer_bwd_input(l, dx, pieces, saved[l], weights, lw[l],
                                                                  _chip_exchange_rider([summed]))
        grads[l] = g
    small = {k: jnp.stack([grads[l][k] for l in range(depth)]) for k in SMALL if k != "lower_bounds"}
    small["lower_bounds"] = _lower_bound_bwd(lower_bounds, jnp.stack([grads[l]["lbs"] for l in range(depth)]),
                                             "lower_bound_bwd")
    (r_small,) = _run_rider(_gather_rider([_pack_small(small, loss_here)]), "small_grads_gather")

    res_in, _ = _sum_adamw(recv_in, w_in, m_w_in, v_w_in, "adamw_w_in")
    res_out, _ = _sum_adamw([grads[l]["w_out_received"] for l in range(depth)], w_out, m_w_out, v_w_out, "adamw_w_out")
    res_small, _ = _sum_adamw([r_small], _pack_small(weights)[None], _pack_small(mom_m)[None], _pack_small(mom_v)[None],
                              "adamw_small")
    loss = res_small[0][0][_small_size(weights) // SMALL_LANES, _small_size(weights) % SMALL_LANES]

    names = ("lower_bounds", "pre_norm_g", "w_in", "hgrn_norm_g", "fox_f_bias", "pool_w", "pool_scale", "w_out", "post_norm_g")
    outs = [loss, dx.reshape(nb, t, d)]
    for i in range(4):
        full = dict(_unpack_small(res_small[i][0], weights), w_in=res_in[i], w_out=res_out[i])
        outs += [full[k] for k in names]
    return tuple(outs)
```

```python
import functools

import jax
import jax.numpy as jnp
from jax import lax
from jax.experimental import pallas as pl
from jax.experimental.pallas import tpu as pltpu

F32, BF16 = jnp.float32, jnp.bfloat16
MESH = pl.DeviceIdType.MESH
N_DEV = 8

NORM_EPS = 1e-6
MASK_VALUE = -1e30
TINY = 1e-30
CHUNK = 64
SUB = 16
HGRN_W, POOL_W, FOX_W = 256, 256, 512
HEAD = 64
FOX_HEADS = 8
POOL_WINDOWS = (2, 4, 8, 16)
POOL_HALO = 16
MAIN_W = 3584
FC_PAD = 128
C_QA, C_FA, C_IA, C_GA, C_UB, C_GB, C_QC, C_KC, C_VC, C_GC = 0, 256, 512, 768, 1024, 1280, 1536, 2048, 2560, 3072
FOX_SCALE = HEAD ** -0.5

ADAM_LR, ADAM_B1, ADAM_B2, ADAM_EPS, ADAM_WD, ADAM_STEP = 0.001, 0.9, 0.999, 1e-08, 0.01, 10

VMEM_LIMIT = 56 * 1024 * 1024


def _pc(fn, name, **kw):
    return pl.pallas_call(fn, name=name, **kw)


def _params(**kw):
    return pltpu.CompilerParams(vmem_limit_bytes=VMEM_LIMIT, **kw)


class _Rider:
    def __init__(self, inputs, out_shapes, n_sems, n_local, plan):
        self.inputs, self.out_shapes, self.n_sems, self.n_local, self.plan = list(inputs), list(out_shapes), n_sems, n_local, plan

    def start(self, ins, outs, *sems):
        sends, _, locs = self.plan(ins, outs, *sems)
        for cp in locs + sends:
            cp.start()

    def wait(self, ins, outs, *sems):
        sends, recvs, locs = self.plan(ins, outs, *sems)
        for cp in recvs:
            cp.wait_recv()
        for cp in sends:
            cp.wait_send()
        for cp in locs:
            cp.wait()

    def sem_shapes(self):
        return [pltpu.SemaphoreType.DMA((self.n_sems,)), pltpu.SemaphoreType.DMA((self.n_sems,)),
                pltpu.SemaphoreType.DMA((self.n_local,))]


def _call(body, name, args, rider=None, *, grid, in_specs, out_specs, out_shape, scratch_shapes=(), **kw):
    if rider is None:
        res = _pc(body, name, grid=grid, in_specs=in_specs, out_specs=out_specs, out_shape=out_shape,
                  scratch_shapes=list(scratch_shapes), **kw)(*args)
        return res, None
    n_in, n_out, n_scr = len(in_specs), len(out_specs), len(scratch_shapes)
    n_rin, n_rout = len(rider.inputs), len(rider.out_shapes)

    def ridden(*refs):
        ins, refs = refs[:n_in], refs[n_in:]
        rins, refs = refs[:n_rin], refs[n_rin:]
        outs, refs = refs[:n_out], refs[n_out:]
        routs, refs = refs[:n_rout], refs[n_rout:]
        scr, sems = refs[:n_scr], refs[n_scr:]
        first = functools.reduce(jnp.logical_and, [pl.program_id(a) == 0 for a in range(len(grid))])
        last = functools.reduce(jnp.logical_and, [pl.program_id(a) == g - 1 for a, g in enumerate(grid)])

        @pl.when(first)
        def _():
            rider.start(rins, routs, *sems)

        body(*ins, *outs, *scr)

        @pl.when(last)
        def _():
            rider.wait(rins, routs, *sems)

    any_spec = pl.BlockSpec(memory_space=pl.ANY)
    res = _pc(ridden, name, grid=grid, in_specs=list(in_specs) + [any_spec] * n_rin,
              out_specs=list(out_specs) + [any_spec] * n_rout, out_shape=list(out_shape) + rider.out_shapes,
              scratch_shapes=list(scratch_shapes) + rider.sem_shapes(), **kw)(*args, *rider.inputs)
    return res[:n_out], res[n_out:]


def _run_rider(rider, name):
    n_rin = len(rider.inputs)

    def body(*refs):
        ins, outs, sems = refs[:n_rin], refs[n_rin:n_rin + len(rider.out_shapes)], refs[n_rin + len(rider.out_shapes):]
        rider.start(ins, outs, *sems)
        rider.wait(ins, outs, *sems)

    any_spec = pl.BlockSpec(memory_space=pl.ANY)
    return _pc(body, name, in_specs=[any_spec] * n_rin, out_specs=[any_spec] * len(rider.out_shapes),
               out_shape=rider.out_shapes, scratch_shapes=rider.sem_shapes())(*rider.inputs)


def _dot(a, b):
    return jnp.dot(a, b, preferred_element_type=F32)


def _dot_nt(a, b):
    return lax.dot_general(a, b, (((1,), (1,)), ((), ())), preferred_element_type=F32)


def _dot_tn(a, b):
    return lax.dot_general(a, b, (((0,), (0,)), ((), ())), preferred_element_type=F32)


def _sel_dot_exact(sel, x):
    hi = x.astype(BF16)
    rest = x - hi.astype(F32)
    mid = rest.astype(BF16)
    lo = (rest - mid.astype(F32)).astype(BF16)
    sb = sel.astype(BF16)
    return _dot(sb, hi) + _dot(sb, mid) + _dot(sb, lo)


def _split2(x):
    hi = x.astype(BF16)
    return hi, (x - hi.astype(F32)).astype(BF16)


def _sel_dot(sel, x):
    hi, lo = _split2(x)
    sb = sel.astype(BF16)
    return _dot(sb, hi) + _dot(sb, lo)


def _dot_sel(x, sel):
    hi, lo = _split2(x)
    sb = sel.astype(BF16)
    return _dot(hi, sb) + _dot(lo, sb)


def _sigmoid(x):
    return 1.0 / (1.0 + jnp.exp(-x))


def _block_ones(n, dtype):
    r = lax.broadcasted_iota(jnp.int32, (n, n), 0) // HEAD
    c = lax.broadcasted_iota(jnp.int32, (n, n), 1) // HEAD
    return (r == c).astype(dtype)


def _sds(shape, dtype):
    return jax.ShapeDtypeStruct(shape, dtype)


def _in_proj_fwd(x, g_pre, w_main, w_fc, bias, nb, name, rider=None):
    n, d = x.shape
    t = n // nb
    tm = min(512, t)
    nt = t // tm
    tt = min(256, tm)

    def body(x_ref, g_ref, w_ref, wf_ref, b_ref, proj_ref, fc_ref, ht_ref, qt_ref, kt_ref, c_ref, cr_ref, carry_s):
        xv = x_ref[...]
        r = lax.rsqrt(jnp.mean(xv * xv, axis=-1, keepdims=True) + NORM_EPS)
        hf = xv * r * g_ref[...]
        hb = hf.astype(BF16)
        ht_ref[...] = hf.T.astype(BF16)
        for j in range(0, MAIN_W, FOX_W):
            res = _dot(hb, w_ref[:, j:j + FOX_W])
            proj_ref[:, j:j + FOX_W] = res
            if j == C_QC:
                qt_ref[...] = (res * FOX_SCALE).T.astype(BF16)
            if j == C_KC:
                kt_ref[...] = res.T.astype(BF16)
        fc = _dot(hb, wf_ref[...])
        fc_ref[...] = fc
        xb = fc + b_ref[...]
        lf = jnp.minimum(xb, 0.0) - jnp.log(1.0 + jnp.exp(-jnp.abs(xb)))
        r = lax.broadcasted_iota(jnp.int32, (tt, tt), 0)
        cc = lax.broadcasted_iota(jnp.int32, (tt, tt), 1)
        carry = jnp.where(pl.program_id(0) % nt == 0, 0.0, carry_s[...])
        for j in range(tm // tt):
            rows = slice(j * tt, (j + 1) * tt)
            cs = _sel_dot_exact(r >= cc, lf[rows]) + carry
            c_ref[rows, :] = cs
            cr_ref[:, rows] = cs.T[:FOX_HEADS, :]
            carry = cs[tt - 1:tt, :]
        carry_s[...] = carry

    def cols(rows):
        return pl.BlockSpec((rows, tm), lambda i: (0, i))

    return _call(
        body, name, (x, g_pre, w_main, w_fc, bias), rider, grid=(n // tm,),
        in_specs=[pl.BlockSpec((tm, d), lambda i: (i, 0)), pl.BlockSpec((1, d), lambda i: (0, 0)),
                  pl.BlockSpec((d, MAIN_W), lambda i: (0, 0)), pl.BlockSpec((d, FC_PAD), lambda i: (0, 0)),
                  pl.BlockSpec((1, FC_PAD), lambda i: (0, 0))],
        out_specs=[pl.BlockSpec((tm, MAIN_W), lambda i: (i, 0)), pl.BlockSpec((tm, FC_PAD), lambda i: (i, 0)),
                   cols(d), cols(FOX_W), cols(FOX_W), pl.BlockSpec((tm, FC_PAD), lambda i: (i, 0)),
                   pl.BlockSpec((None, FOX_HEADS, tm), lambda i: (i // nt, 0, i % nt))],
        out_shape=[_sds((n, MAIN_W), F32), _sds((n, FC_PAD), F32), _sds((d, n), BF16), _sds((FOX_W, n), BF16),
                   _sds((FOX_W, n), BF16), _sds((n, FC_PAD), F32), _sds((nb, FOX_HEADS, t), F32)],
        scratch_shapes=[pltpu.VMEM((1, FC_PAD), F32)],
        compiler_params=_params(),
    )


def _hgrn_gates(q, z, lb):
    sig = _sigmoid(z)
    sn = _sigmoid(-z)
    f = lb + (1.0 - lb) * sig
    g = jnp.log(jnp.maximum(f, TINY))
    k = (1.0 - lb) * sn
    sq = _sigmoid(q)
    return sig, sn, f, g, k, sq


def _sub_tri(n, lower):
    r = lax.broadcasted_iota(jnp.int32, (n, n), 0)
    c = lax.broadcasted_iota(jnp.int32, (n, n), 1)
    tri = (r >= c) if lower else (r <= c)
    return jnp.logical_and(r // SUB == c // SUB, tri).astype(F32)


def _live_rows(t):
    return 8 * (t // 8 + 1)


def _pad_rows(x):
    return x if x.shape[0] == SUB else jnp.concatenate([x, jnp.zeros((SUB - x.shape[0], x.shape[1]), x.dtype)], axis=0)


def _hgrn_decays(qs, k, b):
    srow = lax.broadcasted_iota(jnp.int32, (SUB, HGRN_W), 0)
    es, ws = [], []
    for t in range(SUB):
        r = _live_rows(t)
        e = jnp.where(srow[:r] <= t, jnp.exp(b[t:t + 1, :] - b[:r]), 0.0)
        es.append(e)
        ws.append(_pad_rows(e * (qs[t:t + 1, :] * k[:r])))
    return srow, es, ws


def _hgrn_state_step(st, k, v, b, bmask):
    bl = b[SUB - 1:SUB, :]
    ktil = k * jnp.exp(bl - b)
    return st * jnp.exp(bl) + _dot_tn(v.astype(BF16), ktil.astype(BF16)) * bmask


def _hgrn_tile(t):
    return min(256, t)


def _hgrn_fwd(proj, lb, ones_b, nb, name, rider=None):
    n = proj.shape[0]
    t = n // nb
    tt = _hgrn_tile(t)
    nt = t // tt
    ncs = tt // CHUNK
    w = HGRN_W

    def body(q_ref, z_ref, v_ref, lb_ref, ones_ref, o_ref, s0_ref, st_s, b_s, qs_s, k_s):
        @pl.when(pl.program_id(0) == 0)
        def _():
            st_s[...] = jnp.zeros_like(st_s)

        for e in range(nb):
            q = q_ref[e]
            _, _, _, g, k, sq = _hgrn_gates(q, z_ref[e], lb_ref[...])
            b_s[e] = _sel_dot(_sub_tri(tt, True), g)
            qs_s[e] = q * sq
            k_s[e] = k
        bmask = _block_ones(w, F32)
        ones_b = ones_ref[...]
        nsub = CHUNK // SUB

        def chunk(c, carry):
            sts = [st_s[e] for e in range(nb)]
            for e in range(nb):
                s0_ref[e, c] = sts[e]
            base = pl.multiple_of(c * CHUNK, CHUNK)
            tiles = [[(qs_s[e, pl.ds(base + u * SUB, SUB), :], k_s[e, pl.ds(base + u * SUB, SUB), :],
                       v_ref[e, pl.ds(base + u * SUB, SUB), :], b_s[e, pl.ds(base + u * SUB, SUB), :])
                      for u in range(nsub)] for e in range(nb)]
            aexps = [[] for _ in range(nb)]
            for e in range(nb):
                for qs, k, v, b in tiles[e]:
                    _, _, ws = _hgrn_decays(qs, k, b)
                    aexps[e].append(_dot(jnp.concatenate(ws, axis=0).astype(BF16), ones_b))
            inters = [[] for _ in range(nb)]
            for u in range(nsub):
                for e in range(nb):
                    qs, k, v, b = tiles[e][u]
                    inters[e].append(_dot_nt((qs * jnp.exp(b)).astype(BF16), sts[e].astype(BF16)))
                    sts[e] = _hgrn_state_step(sts[e], k, v, b, bmask)
            for e in range(nb):
                st_s[e] = sts[e]
            for e in range(nb):
                for u, ((qs, k, v, b), aexp, o) in enumerate(zip(tiles[e], aexps[e], inters[e])):
                    for t in range(SUB):
                        r = _live_rows(t)
                        row = o[t:t + 1, :] + jnp.sum(aexp[t * SUB:t * SUB + r, :] * v[:r], axis=0, keepdims=True)
                        o_ref[e, pl.ds(base + u * SUB + t, 1), :] = row
            return carry

        lax.fori_loop(0, ncs, chunk, 0)

    def col(j):
        return pl.BlockSpec((nb, tt, w), lambda i: (0, i, j))

    proj3 = proj.reshape(nb, t, proj.shape[1])
    (o, s0), rode = _call(
        body, name, (proj3, proj3, proj3, lb, ones_b), rider, grid=(nt,),
        in_specs=[col(C_QA // w), col(C_FA // w), col(C_IA // w), pl.BlockSpec((1, w), lambda i: (0, 0)),
                  pl.BlockSpec((w, w), lambda i: (0, 0))],
        out_specs=[pl.BlockSpec((nb, tt, w), lambda i: (0, i, 0)),
                   pl.BlockSpec((nb, ncs, w, w), lambda i: (0, i, 0, 0))],
        out_shape=[_sds((nb, t, w), F32), _sds((nb, t // CHUNK, w, w), F32)],
        scratch_shapes=[pltpu.VMEM((nb, w, w), F32)] + [pltpu.VMEM((nb, tt, w), F32)] * 3,
        compiler_params=_params(),
    )
    return (o.reshape(n, w), s0.reshape(n // CHUNK, w, w)), rode


def _pool_lane_windows():
    lane = lax.broadcasted_iota(jnp.int32, (1, POOL_W), 1) // HEAD
    wl = jnp.zeros((1, POOL_W), F32)
    for gi, win in enumerate(POOL_WINDOWS):
        wl = jnp.where(lane == gi, float(win), wl)
    return lane, wl


def _pool_select(lane, parts):
    out = parts[-1]
    for gi in range(len(parts) - 2, -1, -1):
        out = jnp.where(lane == gi, parts[gi], out)
    return out


def _pool_mix(u, halo, t0, tt):
    lane, wl = _pool_lane_windows()
    ext = jnp.concatenate([halo, u], axis=0)
    sums, cur, shift = [], ext, 1
    for _ in POOL_WINDOWS:
        cur = cur + pltpu.roll(cur, shift, axis=0)
        sums.append(cur[POOL_HALO:, :])
        shift *= 2
    tpos = (t0 + lax.broadcasted_iota(jnp.int32, (tt, POOL_W), 0)).astype(F32)
    cnt = jnp.minimum(tpos + 1.0, wl)
    return _pool_select(lane, sums) / cnt - u, cnt


def _rows_reduce(x, op, final):
    while x.shape[0] > 8 and x.shape[0] % 16 == 0:
        half = x.shape[0] // 2
        x = op(x[:half], x[half:])
    return final(x, axis=0, keepdims=True)


def _tri_pair(step, n, group=1):
    counts = [a // group + 1 for a in range(n)]
    firsts = [sum(counts[:a]) for a in range(1, n)]
    a = sum([(step >= f).astype(jnp.int32) for f in firsts], jnp.int32(0))
    first = sum([jnp.where(step >= f, c, 0) for f, c in zip(firsts, counts)], jnp.int32(0))
    return a, step - first


def _tri_steps(n, group=1):
    return sum(a // group + 1 for a in range(n))


def _lane_lo():
    return lax.broadcasted_iota(jnp.int32, (1, 2 * HEAD), 1) < HEAD


def _put_col(tile, hh, colv):
    lane = lax.broadcasted_iota(jnp.int32, tile.shape, 1)
    return jnp.where(lane == hh, colv, tile)


def _fox_fwd(proj, kt, c_col, c_row, nb, name, rider=None):
    n = proj.shape[0]
    t = n // nb
    tb = min(256, t)
    nq = t // tb
    pw = 2 * HEAD
    kw = 2 if nq % 2 == 0 else 1
    nk = nq // kw

    def body(*refs):
        q_ref, kt_refs, (v_ref, cc_ref, cr_ref, o_ref, lse_ref, m_s, acc_s, cq_s) = refs[0], refs[1:1 + nb], refs[1 + nb:]
        qi, kk = _tri_pair(pl.program_id(0), nq, kw)

        @pl.when(kk == 0)
        def _():
            m_s[...] = jnp.full_like(m_s, -jnp.inf)
            acc_s[...] = jnp.zeros_like(acc_s)
            for e in range(nb):
                for hh in range(FOX_HEADS):
                    cq_s[e, hh] = jnp.broadcast_to(cc_ref[e, :, hh:hh + 1], (tb, pw))

        def block(masked, sub):
            lo = _lane_lo()
            keys = slice(sub * tb, (sub + 1) * tb)
            if masked:
                causal = lax.broadcasted_iota(jnp.int32, (tb, tb), 0) >= lax.broadcasted_iota(jnp.int32, (tb, tb), 1)
            def lanes(hh):
                return lo if hh % 2 == 0 else jnp.logical_not(lo)

            def scores(hh, e):
                sl = slice((hh // 2) * pw, (hh // 2 + 1) * pw)
                return _dot(jnp.where(lanes(hh), q_ref[e, :, sl] * FOX_SCALE, 0.0).astype(BF16), kt_refs[e][sl, keys])

            order = ([(hh, e) for e in range(nb) for hh in range(FOX_HEADS)] if masked else
                     [(hh, e) for hh in range(FOX_HEADS) for e in range(nb)])
            ahead = scores(*order[0])
            for j, (hh, e) in enumerate(order):
                s = ahead
                if j + 1 < len(order):
                    ahead = scores(*order[j + 1])
                s = s + (jnp.tile(cq_s[e, hh], (1, tb // pw)) - cr_ref[e, hh:hh + 1, keys])
                if masked:
                    s = jnp.where(causal, s, MASK_VALUE)
                m_prev = m_s[e, hh]
                m_new = jnp.maximum(m_prev, jnp.max(s, axis=1, keepdims=True))
                alpha = jnp.exp(m_prev - m_new)
                pe = jnp.exp(s - jnp.tile(m_new, (1, tb // pw)))
                m_s[e, hh] = m_new
                vf = v_ref[e, keys, (hh // 2) * pw:(hh // 2 + 1) * pw]
                acc_s[e, hh] = alpha * acc_s[e, hh] + _dot(pe.astype(BF16), jnp.where(lanes(hh), vf, 1.0).astype(BF16))

        def finish():
            lo = _lane_lo()
            for e in range(nb):
                m_all, l_all = jnp.zeros((tb, FC_PAD), F32), jnp.ones((tb, FC_PAD), F32)
                for p in range(FOX_HEADS // 2):
                    a0, a1 = acc_s[e, 2 * p], acc_s[e, 2 * p + 1]
                    both = pltpu.roll(jnp.where(lo, a1, a0), HEAD, axis=1)
                    o_ref[e, :, p * pw:(p + 1) * pw] = jnp.where(lo, a0, a1) / both
                    m_all = _put_col(_put_col(m_all, 2 * p, m_s[e, 2 * p]), 2 * p + 1, m_s[e, 2 * p + 1])
                    l_all = _put_col(_put_col(l_all, 2 * p, both), 2 * p + 1, a1)
                lse_ref[e] = (m_all + jnp.log(l_all)).T[:FOX_HEADS, :]

        for sub in range(kw):
            @pl.when(kk * kw + sub < qi)
            def _(sub=sub):
                block(False, sub)

            @pl.when(kk * kw + sub == qi)
            def _(sub=sub):
                block(True, sub)

        @pl.when(kk == qi // kw)
        def _():
            finish()

    def qspec(wd, j):
        return pl.BlockSpec((nb, tb, wd), lambda st: (0, _tri_pair(st, nq, kw)[0], j))

    def ktspec(e):
        return pl.BlockSpec((FOX_W, kw * tb), lambda st: (0, e * nk + _tri_pair(st, nq, kw)[1]))

    proj3 = proj.reshape(nb, t, proj.shape[1])
    (o, lse), rode = _call(
        body, name, (proj3, *[kt] * nb, proj3, c_col.reshape(nb, t, FC_PAD), c_row), rider, grid=(_tri_steps(nq, kw),),
        in_specs=[qspec(FOX_W, C_QC // FOX_W)] + [ktspec(e) for e in range(nb)] + [
            pl.BlockSpec((nb, kw * tb, FOX_W), lambda st: (0, _tri_pair(st, nq, kw)[1], C_VC // FOX_W)), qspec(FC_PAD, 0),
            pl.BlockSpec((nb, FOX_HEADS, kw * tb), lambda st: (0, 0, _tri_pair(st, nq, kw)[1]))],
        out_specs=[qspec(FOX_W, 0), pl.BlockSpec((nb, FOX_HEADS, tb), lambda st: (0, 0, _tri_pair(st, nq, kw)[0]))],
        out_shape=[_sds((nb, t, FOX_W), F32), _sds((nb, FOX_HEADS, t), F32)],
        scratch_shapes=[pltpu.VMEM((nb, FOX_HEADS, tb, pw), F32)] * 3,
        compiler_params=_params(),
    )
    return (o.reshape(n, FOX_W), lse), rode


def _head_mean(x, ones_f):
    return _dot_sel(x, ones_f) * (1.0 / HEAD)


def _merge_fwd(x, proj, o_h, o_c, gh, wbd, scale, w_out, g_post, nb, name, target=None):
    n, d = x.shape
    t = n // nb
    tm = min(512, t)
    nt = t // tm
    nhb = tm // POOL_HALO

    def body(*refs):
        x_ref, ga_ref, gc_ref, oh_ref, u_ref, gb_ref, h_ref, oc_ref, gh_ref, wb_ref, s_ref, w_ref, gp_ref = refs[:13]
        if target is None:
            xo_ref, mixt_ref, y_ref = refs[13:]
        else:
            t_ref, dx_ref, sq_ref, mixt_ref, y_ref = refs[13:]
        oh = oh_ref[...]
        ones_f = _block_ones(HGRN_W, F32)
        na = oh * lax.rsqrt(_head_mean(oh * oh, ones_f) + NORM_EPS) * gh_ref[...]
        ga, gb, gc = ga_ref[...], gb_ref[...], gc_ref[...]
        ti = pl.program_id(0) % nt
        pooled, _ = _pool_mix(u_ref[...], jnp.where(ti == 0, 0.0, h_ref[...]), ti * tm, tm)
        ob = _dot(pooled.astype(BF16), wb_ref[...]) * s_ref[...] * (gb * _sigmoid(gb))
        mixed = jnp.concatenate([na * (ga * _sigmoid(ga)), ob, oc_ref[...] * (gc * _sigmoid(gc))], axis=1)
        mixt_ref[...] = mixed.T.astype(BF16)
        y = _dot(mixed.astype(BF16), w_ref[...])
        y_ref[...] = y
        xn = x_ref[...] + y * lax.rsqrt(jnp.mean(y * y, axis=-1, keepdims=True) + NORM_EPS) * gp_ref[...]
        if target is None:
            xo_ref[...] = xn
        else:
            @pl.when(pl.program_id(0) == 0)
            def _():
                sq_ref[...] = jnp.zeros_like(sq_ref)

            e = xn - t_ref[...]
            dx_ref[...] = e * (1.0 / d)
            sq_ref[...] += jnp.sum(e * e, axis=0, keepdims=True)

    def row(wd, j=0):
        return pl.BlockSpec((tm, wd), lambda i: (i, j))

    def full(a, b):
        return pl.BlockSpec((a, b), lambda i: (0, 0))

    head = [] if target is None else [target]
    res = _pc(
        body, name, grid=(n // tm,),
        in_specs=[row(d), row(HGRN_W, C_GA // HGRN_W), row(FOX_W, C_GC // FOX_W), row(HGRN_W),
                  row(POOL_W, C_UB // POOL_W), row(POOL_W, C_GB // POOL_W),
                  pl.BlockSpec((POOL_HALO, POOL_W), lambda i: (jnp.maximum(i * nhb - 1, 0), C_UB // POOL_W)), row(FOX_W),
                  full(1, HGRN_W), full(POOL_W, POOL_W), full(1, POOL_W), full(d, d), full(1, d)] + [row(d)] * len(head),
        out_specs=[row(d)] + [full(1, d)] * len(head) + [pl.BlockSpec((d, tm), lambda i: (0, i)), row(d)],
        out_shape=[_sds((n, d), F32)] + [_sds((1, d), F32)] * len(head) + [_sds((d, n), BF16), _sds((n, d), F32)],
        compiler_params=_params(),
    )(x, proj, proj, o_h, proj, proj, proj, o_c, gh, wbd, scale, w_out, g_post, *head)
    return (res[0], res[1], res[2]) if target is None else ((res[0], res[1]), res[2], res[3])


def _rms_bwd(dy_scaled, xhat, r):
    return r * (dy_scaled - xhat * jnp.mean(dy_scaled * xhat, axis=-1, keepdims=True))


def _merge_bwd(dxo, y, g_post, w_out_t, proj, o_c, mixt, nb, name):
    n, d = y.shape
    t = n // nb
    tm = min(512, t)
    nt = t // tm
    wab = HGRN_W + POOL_W
    rows = d // N_DEV

    def body(dx_ref, y_ref, gp_ref, wt_ref, gc_ref, oc_ref, mt_ref,
             gw_ref, dm_ref, dgp_ref, da_ref, dat_ref, dg_ref, dl_ref, gw_s):
        @pl.when(pl.program_id(0) == 0)
        def _():
            dgp_ref[...] = jnp.zeros_like(dgp_ref)
            gw_s[...] = jnp.zeros_like(gw_s)

        yv, dxv = y_ref[...], dx_ref[...]
        r = lax.rsqrt(jnp.mean(yv * yv, axis=-1, keepdims=True) + NORM_EPS)
        yh = yv * r
        dgp_ref[...] += jnp.sum(dxv * yh, axis=0, keepdims=True)
        dyb = _rms_bwd(dxv * gp_ref[...], yh, r).astype(BF16)
        gw_s[...] += _dot(mt_ref[...], dyb)

        @pl.when(pl.program_id(0) == n // tm - 1)
        def _():
            for j in range(N_DEV):
                gw_ref[j % 2, j // 2] = gw_s[j * rows:(j + 1) * rows, :].astype(BF16)

        dm_ref[...] = _dot(dyb, wt_ref[:, :wab])
        dmc = _dot(dyb, wt_ref[:, wab:])
        gc, oc = gc_ref[...], oc_ref[...]
        sg = _sigmoid(gc)
        da = dmc * (gc * sg)
        da_ref[...] = da.astype(BF16)
        dat_ref[...] = da.T.astype(BF16)
        dg_ref[...] = (dmc * oc * (sg * (1.0 + gc * (1.0 - sg)))).astype(BF16)
        rr = lax.broadcasted_iota(jnp.int32, (FOX_W, FC_PAD), 0) // HEAD
        cc = lax.broadcasted_iota(jnp.int32, (FOX_W, FC_PAD), 1)
        dl_ref[...] = _dot_sel(da * oc, (rr == cc).astype(F32)).T[:FOX_HEADS, :]

    def row(wd, j=0):
        return pl.BlockSpec((tm, wd), lambda i: (i, j))

    def full(a, b):
        return pl.BlockSpec((a, b), lambda i: (0, 0))

    return _pc(
        body, name, grid=(n // tm,),
        in_specs=[row(d), row(d), full(1, d), full(d, d), row(FOX_W, C_GC // FOX_W), row(FOX_W),
                  pl.BlockSpec((d, tm), lambda i: (0, i))],
        out_specs=[pl.BlockSpec((2, N_DEV // 2, rows, d), lambda i: (0, 0, 0, 0)), row(wab), full(1, d), row(FOX_W),
                   pl.BlockSpec((FOX_W, tm), lambda i: (0, i)), row(FOX_W),
                   pl.BlockSpec((None, FOX_HEADS, tm), lambda i: (i // nt, 0, i % nt))],
        out_shape=[_sds((2, N_DEV // 2, rows, d), BF16), _sds((n, wab), F32), _sds((1, d), F32), _sds((n, FOX_W), BF16),
                   _sds((FOX_W, n), BF16), _sds((n, FOX_W), BF16), _sds((nb, FOX_HEADS, t), F32)],
        scratch_shapes=[pltpu.VMEM((d, d), F32)],
        compiler_params=_params(),
    )(dxo, y, g_post, w_out_t, proj, o_c, mixt)


def _w_in_grad(ht, pieces, dmix, proj, wbd, wbd_t, scale, nb, name):
    d, n = ht.shape
    t = n // nb
    ta, tk = d, min(512, t)
    nk, nt, nhb = n // tk, t // tk, tk // POOL_HALO
    given = [p for p, _ in pieces if p is not None]
    widths = [2 * POOL_W if p is None else p.shape[1] for p, _ in pieces]
    offs = [sum(widths[:i]) for i in range(len(widths))]
    in_w = MAIN_W + FOX_HEADS
    shard = in_w // N_DEV
    cu, cg, cm = C_UB // POOL_W, C_GB // POOL_W, HGRN_W // POOL_W

    def body(*refs):
        a_ref, p_refs = refs[0], list(refs[1:1 + len(given)])
        pool_refs = refs[1 + len(given):10 + len(given)]
        o_ref, db_ref, dw_ref, ds_ref, acc = refs[10 + len(given):]
        k = pl.program_id(1)

        @pl.when(k == 0)
        def _():
            dw_ref[...] = jnp.zeros_like(dw_ref)
            ds_ref[...] = jnp.zeros_like(ds_ref)
            acc[...] = jnp.zeros_like(acc)

        db, dw, dscale = _pool_bwd_tile(*pool_refs, k % nt, nt, tk)
        dw_ref[...] += dw
        ds_ref[...] += dscale
        db_ref[...] = db
        a = a_ref[...]
        for (p, _), off, wd in zip(pieces, offs, widths):
            pr = db if p is None else p_refs.pop(0)
            for j in range(0, wd, 512):
                jw = min(512, wd - j)
                acc[:, off + j:off + j + jw] += _dot(a, pr[:, j:j + jw])

        @pl.when(k == nk - 1)
        def _():
            for j in range(N_DEV):
                o_ref[j % 2, j // 2] = acc[:, j * shard:(j + 1) * shard].astype(BF16)

    def after(k):
        return jnp.minimum((k + 1) * nhb, n // POOL_HALO - 1)

    def tile(j):
        return pl.BlockSpec((tk, POOL_W), lambda i, k: (k, j))

    def full(r, c):
        return pl.BlockSpec((r, c), lambda i, k: (0, 0))

    pool_specs = [tile(cu), tile(cg),
                  pl.BlockSpec((POOL_HALO, POOL_W), lambda i, k: (jnp.maximum(k * nhb - 1, 0), cu)), tile(cm),
                  pl.BlockSpec((POOL_HALO, POOL_W), lambda i, k: (after(k), cg)),
                  pl.BlockSpec((POOL_HALO, POOL_W), lambda i, k: (after(k), cm)),
                  full(POOL_W, POOL_W), full(POOL_W, POOL_W), full(1, POOL_W)]
    return _pc(
        body, name, grid=(d // ta, nk),
        in_specs=[pl.BlockSpec((ta, tk), lambda i, k: (i, k))]
        + [pl.BlockSpec((tk, p.shape[1]), lambda i, k: (k, 0)) for p in given] + pool_specs,
        out_specs=[pl.BlockSpec((2, N_DEV // 2, ta, shard), lambda i, k: (0, 0, i, 0)),
                   pl.BlockSpec((tk, 2 * POOL_W), lambda i, k: (k, 0)), full(POOL_W, POOL_W), full(1, POOL_W)],
        out_shape=[_sds((2, N_DEV // 2, d, shard), BF16), _sds((n, 2 * POOL_W), BF16), _sds((POOL_W, POOL_W), F32),
                   _sds((1, POOL_W), F32)],
        scratch_shapes=[pltpu.VMEM((ta, sum(widths)), F32)],
        compiler_params=_params(),
    )(ht, *given, proj, proj, proj, dmix, proj, dmix, wbd, wbd_t, scale)


def _hgrn_state_bwd(qs, k, v, b, do, s0, ds1, bmask):
    bl = b[SUB - 1:SUB, :]
    eb, ebl, ekt = jnp.exp(b), jnp.exp(bl), jnp.exp(bl - b)
    qe, ktil = qs * eb, k * ekt
    ds1b, dob = ds1.astype(BF16), do.astype(BF16)
    dv = _dot_nt(ktil.astype(BF16), ds1b)
    dqe = _dot(dob, s0.astype(BF16))
    dktil = _dot(v.astype(BF16), ds1b)
    dbl = jnp.sum(dktil * ktil, axis=0, keepdims=True) + ebl * jnp.sum(s0 * ds1, axis=0, keepdims=True)
    ds0 = ds1 * ebl + _dot_tn(dob, qe.astype(BF16)) * bmask
    return dqe * eb, dktil * ekt, dv, dbl, ds0


def _hgrn_intra_bwd(qs, k, v, do, es, aexp, gexp, dq, dk, dv, put_dq_row):
    dks = [dk[j:j + 8] for j in range(0, SUB, 8)]
    dvs = [dv[j:j + 8] for j in range(0, SUB, 8)]
    for t in range(SUB):
        r = _live_rows(t)
        ge = gexp[t * SUB:t * SUB + r, :] * es[t]
        put_dq_row(t, dq[t:t + 1, :] + jnp.sum(ge * k[:r], axis=0, keepdims=True))
        for j in range(r // 8):
            dks[j] = dks[j] + ge[8 * j:8 * j + 8] * qs[t:t + 1, :]
            dvs[j] = dvs[j] + aexp[t * SUB + 8 * j:t * SUB + 8 * j + 8, :] * do[t:t + 1, :]
    return jnp.concatenate(dks, axis=0), jnp.concatenate(dvs, axis=0)


def _hgrn_bwd(dmix, proj, o_h, s0, gh, lb, ones_b, nb, name, rider=None):
    n = proj.shape[0]
    t = n // nb
    tt = _hgrn_tile(t)
    nt = t // tt
    ncs = tt // CHUNK
    nsub = CHUNK // SUB
    w = HGRN_W

    def body(dm_ref, q_ref, z_ref, v_ref, ga_ref, oh_ref, s0_ref, gh_ref, lb_ref, ones_ref,
             dp_ref, dgh_ref, dlb_ref, ds_s, ss_s, b_s, qs_s, k_s, do_s, dq_s, dk_s, dv_s, dbl_s):
        @pl.when(pl.program_id(0) == 0)
        def _():
            dgh_ref[...] = jnp.zeros_like(dgh_ref)
            dlb_ref[...] = jnp.zeros_like(dlb_ref)
            ds_s[...] = jnp.zeros_like(ds_s)

        ones_b = ones_ref[...]
        ones_f = ones_b.astype(F32)
        bmask = _block_ones(w, F32)
        lbv, ghv = lb_ref[...], gh_ref[...]
        kept = []
        for e in range(nb):
            oh, ga, dm = oh_ref[e], ga_ref[e], dm_ref[e]
            rn = lax.rsqrt(_head_mean(oh * oh, ones_f) + NORM_EPS)
            nh = oh * rn
            sga = _sigmoid(ga)
            dp_ref[e, :, 3 * w:4 * w] = (dm * nh * ghv * (sga * (1.0 + ga * (1.0 - sga)))).astype(BF16)
            dn = dm * (ga * sga)
            dgh_ref[...] += jnp.sum(dn * nh, axis=0, keepdims=True)
            dn = dn * ghv
            do_s[e] = rn * (dn - nh * _head_mean(dn * nh, ones_f))
            q = q_ref[e]
            sig, sn, f, g, k, sq = _hgrn_gates(q, z_ref[e], lbv)
            qs = q * sq
            b_s[e] = _sel_dot(_sub_tri(tt, True), g)
            qs_s[e] = qs
            k_s[e] = k
            kept.append((q, sig, sn, f, k, sq, qs))

        def chunk(cc, carry):
            c = ncs - 1 - cc
            base = pl.multiple_of(c * CHUNK, CHUNK)
            tiles = [[(qs_s[e, pl.ds(base + u * SUB, SUB), :], k_s[e, pl.ds(base + u * SUB, SUB), :],
                       v_ref[e, pl.ds(base + u * SUB, SUB), :], b_s[e, pl.ds(base + u * SUB, SUB), :],
                       do_s[e, pl.ds(base + u * SUB, SUB), :]) for u in range(nsub)] for e in range(nb)]
            sts = [s0_ref[e, c] for e in range(nb)]
            for u in range(nsub):
                for e in range(nb):
                    qs, k, v, b, do = tiles[e][u]
                    ss_s[e, u] = sts[e]
                    if u < nsub - 1:
                        sts[e] = _hgrn_state_step(sts[e], k, v, b, bmask)
            dss = [ds_s[e] for e in range(nb)]
            for u in reversed(range(nsub)):
                for e in range(nb):
                    qs, k, v, b, do = tiles[e][u]
                    _, es, ws = _hgrn_decays(qs, k, b)
                    gs = [_pad_rows(do[t:t + 1, :] * v[:_live_rows(t)]) for t in range(SUB)]
                    aexp = _dot(jnp.concatenate(ws, axis=0).astype(BF16), ones_b)
                    gexp = _dot(jnp.concatenate(gs, axis=0).astype(BF16), ones_b)
                    dq, dk, dv, dbl, dss[e] = _hgrn_state_bwd(qs, k, v, b, do, ss_s[e, u], dss[e], bmask)

                    def put_dq_row(i, row, e=e, r0=base + u * SUB):
                        dq_s[e, pl.ds(r0 + i, 1), :] = row

                    dk, dv = _hgrn_intra_bwd(qs, k, v, do, es, aexp, gexp, dq, dk, dv, put_dq_row)
                    dk_s[e, pl.ds(base + u * SUB, SUB), :] = dk
                    dv_s[e, pl.ds(base + u * SUB, SUB), :] = dv
                    dbl_s[e, pl.ds(base + u * SUB, SUB), :] = jnp.broadcast_to(dbl, (SUB, w))
            for e in range(nb):
                ds_s[e] = dss[e]
            return carry

        lax.fori_loop(0, ncs, chunk, 0)
        for e, (q, sig, sn, f, k, sq, qs) in enumerate(kept):
            dqs, dk = dq_s[e], dk_s[e]
            dg = _sel_dot(_sub_tri(tt, False), qs * dqs - k * dk) + dbl_s[e]
            dfz = jnp.where(f > TINY, dg / jnp.maximum(f, TINY), 0.0)
            dlb_ref[...] += jnp.sum(dfz * (1.0 - sig) - dk * sn, axis=0, keepdims=True)
            dp_ref[e, :, 0:w] = (dqs * (sq * (1.0 + q * (1.0 - sq)))).astype(BF16)
            dp_ref[e, :, w:2 * w] = ((dfz - dk) * (1.0 - lbv) * sig * sn).astype(BF16)
            dp_ref[e, :, 2 * w:3 * w] = dv_s[e].astype(BF16)

    def col(j):
        return pl.BlockSpec((nb, tt, w), lambda i: (0, nt - 1 - i, j))

    def full(a, bb):
        return pl.BlockSpec((a, bb), lambda i: (0, 0))

    proj3 = proj.reshape(nb, t, proj.shape[1])
    (dp, dgh, dlb), rode = _call(
        body, name, (dmix.reshape(nb, t, dmix.shape[1]), proj3, proj3, proj3, proj3, o_h.reshape(nb, t, w),
                     s0.reshape(nb, t // CHUNK, w, w), gh, lb, ones_b), rider, grid=(nt,),
        in_specs=[col(0), col(C_QA // w), col(C_FA // w), col(C_IA // w), col(C_GA // w), col(0),
                  pl.BlockSpec((nb, ncs, w, w), lambda i: (0, nt - 1 - i, 0, 0)), full(1, w), full(1, w), full(w, w)],
        out_specs=[pl.BlockSpec((nb, tt, 4 * w), lambda i: (0, nt - 1 - i, 0)), full(1, w), full(1, w)],
        out_shape=[_sds((nb, t, 4 * w), BF16), _sds((1, w), F32), _sds((1, w), F32)],
        scratch_shapes=[pltpu.VMEM((nb, w, w), F32), pltpu.VMEM((nb, nsub, w, w), F32)]
        + [pltpu.VMEM((nb, tt, w), F32)] * 8,
        compiler_params=_params(),
    )
    return (dp.reshape(n, 4 * w), dgh, dlb), rode


def _pool_bwd_tile(u_ref, g_ref, h_ref, dm_ref, gn_ref, dmn_ref, w_ref, wt_ref, s_ref, i, nt, tt):
    sc = s_ref[...]
    halo = jnp.where(i == 0, 0.0, h_ref[...])
    pooled, cnt = _pool_mix(u_ref[...], halo, i * tt, tt)
    pb = pooled.astype(BF16)
    pre = _dot(pb, w_ref[...])
    gv, dm = g_ref[...], dm_ref[...]
    sg = _sigmoid(gv)
    silu = gv * sg
    dgb = dm * pre * sc * (sg * (1.0 + gv * (1.0 - sg)))
    dscale = jnp.sum(dm * pre * silu, axis=0, keepdims=True)
    dpre = (dm * sc * silu).astype(BF16)
    dw = _dot_tn(pb, dpre)
    dpool = _dot(dpre, wt_ref[...])
    gn = gn_ref[...]
    dpre_n = (dmn_ref[...] * sc * (gn * _sigmoid(gn))).astype(BF16)
    dpool_n = jnp.where(i == nt - 1, 0.0, _dot(dpre_n, wt_ref[...]))
    lane, wl = _pool_lane_windows()
    tpos_n = ((i + 1) * tt + lax.broadcasted_iota(jnp.int32, (POOL_HALO, POOL_W), 0)).astype(F32)
    ext = jnp.concatenate([dpool / cnt, dpool_n / jnp.minimum(tpos_n + 1.0, wl)], axis=0)
    rows = tt + POOL_HALO
    sums, cur, shift = [], ext, 1
    for _ in POOL_WINDOWS:
        cur = cur + pltpu.roll(cur, rows - shift, axis=0)
        sums.append(cur[:tt, :])
        shift *= 2
    du = _pool_select(lane, sums) - dpool
    return jnp.concatenate([du, dgb], axis=1).astype(BF16), dw, dscale


def _fox_bwd(proj, qt, kt, da, dat, c_col, c_row, lse_row, delta_row, nb, name, rider=None):
    n = proj.shape[0]
    t = n // nb
    tb = min(256, t)
    nq = t // tb
    pw = 2 * HEAD
    qw = 2 if nq % 2 == 0 else 1
    nqs = nq // qw

    def body(q_ref, k_ref, v_ref, da_ref, qt_ref, kt_ref, dat_ref, cc_ref, cr_ref, lse_ref, dl_ref,
             dq_ref, dk_ref, dv_ref, dck_ref, dcq_ref, dq_s, dk_s, dv_s, dck_s, dcq_s):
        step = pl.program_id(1)
        kj, qq = pairs(step)

        @pl.when(step == 0)
        def _():
            dq_s[...] = jnp.zeros_like(dq_s)
            dcq_s[...] = jnp.zeros_like(dcq_s)

        @pl.when(qq == nqs - 1)
        def _():
            dk_s[...] = jnp.zeros_like(dk_s)
            dv_s[...] = jnp.zeros_like(dv_s)
            dck_s[...] = jnp.zeros_like(dck_s)

        def block(masked, sub):
            lo = _lane_lo()
            qi = qq * qw + sub
            qs = slice(sub * tb, (sub + 1) * tb)
            if masked:
                causal = lax.broadcasted_iota(jnp.int32, (tb, tb), 1) >= lax.broadcasted_iota(jnp.int32, (tb, tb), 0)
            dck = dck_s[...]
            for p in range(FOX_HEADS // 2):
                sl = slice(p * pw, (p + 1) * pw)
                qp = q_ref[qs, sl] * FOX_SCALE
                kp = k_ref[:, sl].astype(BF16)
                vp = v_ref[:, sl].astype(BF16)
                dap = da_ref[qs, sl]
                dk, dv = dk_s[:, sl], dv_s[:, sl]
                for h in range(2):
                    hh = 2 * p + h
                    lm = lo if h == 0 else jnp.logical_not(lo)
                    rows = slice(hh * HEAD, (hh + 1) * HEAD)
                    none = jnp.zeros((HEAD, tb), BF16)
                    qm = jnp.where(lm, qp, 0.0).astype(BF16)
                    dam = jnp.where(lm, dap, jnp.zeros_like(dap))
                    qtm = jnp.concatenate([qt_ref[rows, qs], none] if h == 0 else [none, qt_ref[rows, qs]], axis=0)
                    datm = jnp.concatenate([dat_ref[rows, qs], none] if h == 0 else [none, dat_ref[rows, qs]], axis=0)
                    s = _dot(kp, qtm) + (cr_ref[hh:hh + 1, qs] - cc_ref[:, hh:hh + 1])
                    pe = jnp.exp(s - lse_ref[hh:hh + 1, qs])
                    if masked:
                        pe = jnp.where(causal, pe, 0.0)
                    dp = _dot(vp, datm)
                    ds = pe * (dp - dl_ref[hh:hh + 1, qs])
                    dsb = ds.astype(BF16)
                    dv = dv + _dot(pe.astype(BF16), dam)
                    dk = dk + _dot(dsb, qm)
                    dq_s[qi, rows, :] += _dot(kt_ref[rows, :], dsb)
                    dck = dck - _put_col(jnp.zeros_like(dck), hh, jnp.sum(ds, axis=1, keepdims=True))
                    dcq_s[qi, hh:hh + 1, :] += _rows_reduce(ds, jnp.add, jnp.sum)
                dk_s[:, sl] = dk
                dv_s[:, sl] = dv
            dck_s[...] = dck

        for sub in reversed(range(qw)):
            @pl.when(qq * qw + sub > kj)
            def _(sub=sub):
                block(False, sub)

            @pl.when(qq * qw + sub == kj)
            def _(sub=sub):
                block(True, sub)

        @pl.when(qq == kj // qw)
        def _():
            dk_ref[...] = dk_s[...].astype(BF16)
            dv_ref[...] = dv_s[...].astype(BF16)
            dck_ref[...] = dck_s[...]

        @pl.when(step == _tri_steps(nq, qw) - 1)
        def _():
            for j in range(nq):
                dq_ref[j * tb:(j + 1) * tb, :] = (dq_s[j].T * FOX_SCALE).astype(BF16)
                dcq_ref[:, j * tb:(j + 1) * tb] = dcq_s[j]

    def pairs(step):
        a, b = _tri_pair(step, nq, qw)
        return nq - 1 - a, nqs - 1 - b

    def kspec(wd, j=0):
        return pl.BlockSpec((tb, wd), lambda b, st: (b * nq + pairs(st)[0], j))

    def qspec(wd, j=0):
        return pl.BlockSpec((qw * tb, wd), lambda b, st: (b * nqs + pairs(st)[1], j))

    def qrow():
        return pl.BlockSpec((None, FOX_HEADS, qw * tb), lambda b, st: (b, 0, pairs(st)[1]))

    def tspec(which):
        if which == 0:
            return pl.BlockSpec((FOX_W, tb), lambda b, st: (0, b * nq + pairs(st)[0]))
        return pl.BlockSpec((FOX_W, qw * tb), lambda b, st: (0, b * nqs + pairs(st)[1]))

    return _call(
        body, name, (proj, proj, proj, da, qt, kt, dat, c_col, c_row, lse_row, delta_row), rider,
        grid=(nb, _tri_steps(nq, qw)),
        in_specs=[qspec(FOX_W, C_QC // FOX_W), kspec(FOX_W, C_KC // FOX_W), kspec(FOX_W, C_VC // FOX_W), qspec(FOX_W),
                  tspec(1), tspec(0), tspec(1), kspec(FC_PAD), qrow(), qrow(), qrow()],
        out_specs=[pl.BlockSpec((t, FOX_W), lambda b, st: (b, 0)), kspec(FOX_W), kspec(FOX_W), kspec(FC_PAD),
                   pl.BlockSpec((None, FOX_HEADS, t), lambda b, st: (b, 0, 0))],
        out_shape=[_sds((n, FOX_W), BF16), _sds((n, FOX_W), BF16), _sds((n, FOX_W), BF16), _sds((n, FC_PAD), F32),
                   _sds((nb, FOX_HEADS, t), F32)],
        scratch_shapes=[pltpu.VMEM((nq, FOX_W, tb), F32), pltpu.VMEM((tb, FOX_W), F32), pltpu.VMEM((tb, FOX_W), F32),
                        pltpu.VMEM((tb, FC_PAD), F32), pltpu.VMEM((nq, FOX_HEADS, tb), F32)],
        compiler_params=_params(),
    )


def _fox_decay_bwd(dc_q, dc_k, fc, bias, nb, name):
    n = fc.shape[0]
    t = n // nb
    tt = min(256, t)
    nt = t // tt

    def body(dcq_ref, dck_ref, fc_ref, b_ref, dfc_ref, db_ref):
        @pl.when(pl.program_id(0) == 0)
        def _():
            db_ref[...] = jnp.zeros_like(db_ref)

        r = lax.broadcasted_iota(jnp.int32, (tt, tt), 0)
        cc = lax.broadcasted_iota(jnp.int32, (tt, tt), 1)
        carry = jnp.zeros((1, FC_PAD), F32)
        db = jnp.zeros((1, FC_PAD), F32)
        for i in reversed(range(nt)):
            rows = slice(i * tt, (i + 1) * tt)
            dcq = jnp.concatenate([dcq_ref[:, rows], jnp.zeros((FC_PAD - FOX_HEADS, tt), F32)], axis=0).T
            dlf = _sel_dot_exact(r <= cc, dcq + dck_ref[rows, :]) + carry
            carry = dlf[0:1, :]
            dfc = dlf * _sigmoid(-(fc_ref[rows, :] + b_ref[...]))
            dfc_ref[rows, :] = dfc.astype(BF16)
            db = db + jnp.sum(dfc, axis=0, keepdims=True)
        db_ref[...] += db

    def row():
        return pl.BlockSpec((t, FC_PAD), lambda b: (b, 0))

    return _pc(
        body, name, grid=(nb,),
        in_specs=[pl.BlockSpec((None, FOX_HEADS, t), lambda b: (b, 0, 0)), row(), row(),
                  pl.BlockSpec((1, FC_PAD), lambda b: (0, 0))],
        out_specs=[row(), pl.BlockSpec((1, FC_PAD), lambda b: (0, 0))],
        out_shape=[_sds((n, FC_PAD), BF16), _sds((1, FC_PAD), F32)],
        compiler_params=_params(),
    )(dc_q, dc_k, fc, bias)


def _in_proj_bwd(pieces, w_main_t, w_fc_t, x, g_pre, dxo, name, rider=None):
    n, d = x.shape
    tm = min(512, n)
    widths = [p.shape[1] for p, _ in pieces]
    offs = [o for _, o in pieces]
    np_ = len(pieces)

    def body(*refs):
        p_refs = refs[:np_]
        wt_ref, wf_ref, x_ref, g_ref, dxo_ref, dx_ref, dg_ref = refs[np_:]

        @pl.when(pl.program_id(0) == 0)
        def _():
            dg_ref[...] = jnp.zeros_like(dg_ref)

        dh = _dot(p_refs[-1][...], wf_ref[...])
        for pr, wd, off in zip(p_refs[:-1], widths[:-1], offs[:-1]):
            for j in range(0, wd, 512):
                jw = min(512, wd - j)
                dh = dh + _dot(pr[:, j:j + jw], wt_ref[off + j:off + j + jw, :])
        xv = x_ref[...]
        r = lax.rsqrt(jnp.mean(xv * xv, axis=-1, keepdims=True) + NORM_EPS)
        xh = xv * r
        dg_ref[...] += jnp.sum(dh * xh, axis=0, keepdims=True)
        dx_ref[...] = dxo_ref[...] + _rms_bwd(dh * g_ref[...], xh, r)

    row = pl.BlockSpec((tm, d), lambda i: (i, 0))
    return _call(
        body, name, (*[p for p, _ in pieces], w_main_t, w_fc_t, x, g_pre, dxo), rider, grid=(n // tm,),
        in_specs=[pl.BlockSpec((tm, wd), lambda i: (i, 0)) for wd in widths] + [
            pl.BlockSpec((MAIN_W, d), lambda i: (0, 0)), pl.BlockSpec((FC_PAD, d), lambda i: (0, 0)),
            row, pl.BlockSpec((1, d), lambda i: (0, 0)), row],
        out_specs=[row, pl.BlockSpec((1, d), lambda i: (0, 0))],
        out_shape=[_sds((n, d), F32), _sds((1, d), F32)],
        compiler_params=_params(),
    )


def _lower_bound_table(lower_bounds, name):
    depth, w = lower_bounds.shape

    def body(lb_ref, o_ref):
        v = lb_ref[...]
        e = jnp.exp(v - jnp.max(v, axis=0, keepdims=True))
        p = e / jnp.sum(e, axis=0, keepdims=True)
        acc = jnp.zeros((1, w), F32)
        for l in range(depth):
            acc = acc + p[l:l + 1, :]
            o_ref[l:l + 1, :] = acc - p[0:1, :]

    return _pc(body, name, out_shape=_sds((depth, w), F32))(lower_bounds)


def _lower_bound_bwd(lower_bounds, dlbs, name):
    depth, w = lower_bounds.shape

    def body(lb_ref, d_ref, o_ref):
        v, dl = lb_ref[...], d_ref[...]
        e = jnp.exp(v - jnp.max(v, axis=0, keepdims=True))
        p = e / jnp.sum(e, axis=0, keepdims=True)
        tot = jnp.sum(dl, axis=0, keepdims=True)
        rows, tail = [], tot
        for l in range(depth):
            rows.append(tail - tot if l == 0 else tail)
            tail = tail - dl[l:l + 1, :]
        dp = jnp.concatenate(rows, axis=0)
        o_ref[...] = p * (dp - jnp.sum(p * dp, axis=0, keepdims=True))

    return _pc(body, name, out_shape=_sds((depth, w), F32))(lower_bounds, dlbs)


def _place():
    x, y, c = lax.axis_index("x"), lax.axis_index("y"), lax.axis_index("c")
    return x, y, c


def _gather_weights(*arrays):
    na = len(arrays)

    def body(*refs):
        ins, outs = refs[:na], refs[na:2 * na]
        send_sems, recv_sems, local_sems = refs[2 * na:]
        x, y, c = _place()
        me, sibling = (x, y, c), (x, y, 1 - c)
        chips = [(1 - x, y), (x, 1 - y), (1 - x, 1 - y)]

        def slot(a, px, py, pc):
            return outs[a].at[4 * px + 2 * py + pc]

        def copy(a, k, block, to, own=False):
            return pltpu.make_async_remote_copy(
                src_ref=ins[a] if own else slot(a, *block), dst_ref=slot(a, *block),
                send_sem=send_sems.at[a * 7 + k], recv_sem=recv_sems.at[a * 7 + k],
                device_id=to, device_id_type=MESH)

        mine = [pltpu.make_async_copy(ins[a], slot(a, *me), local_sems.at[a]) for a in range(na)]
        for cp in mine:
            cp.start()
        first = []
        for a in range(na):
            first.append(copy(a, 0, me, sibling, own=True))
            first += [copy(a, 1 + j, me, (*chip, c), own=True) for j, chip in enumerate(chips)]
        for cp in first:
            cp.start()
        passed = []
        for j, chip in enumerate(chips):
            for a in range(na):
                copy(a, 1 + j, (*chip, c), me).wait_recv()
                fw = copy(a, 4 + j, (*chip, c), sibling)
                fw.start()
                passed.append(fw)
        for a in range(na):
            copy(a, 0, sibling, me).wait_recv()
            for j, chip in enumerate(chips):
                copy(a, 4 + j, (*chip, 1 - c), me).wait_recv()
        for cp in first + passed:
            cp.wait_send()
        for cp in mine:
            cp.wait()

    any_spec = pl.BlockSpec(memory_space=pl.ANY)
    return _pc(
        body, "gather_weights",
        in_specs=[any_spec] * na, out_specs=[any_spec] * na,
        out_shape=[_sds((N_DEV,) + a.shape, a.dtype) for a in arrays],
        scratch_shapes=[pltpu.SemaphoreType.DMA((7 * na,)), pltpu.SemaphoreType.DMA((7 * na,)),
                        pltpu.SemaphoreType.DMA((na,))],
    )(*arrays)


def _peer(k):
    x, y, c = _place()
    return (1 - x if k & 4 else x, 1 - y if k & 2 else y, 1 - c if k & 1 else c)


def _remote(src, dst, sems, s, to):
    return pltpu.make_async_remote_copy(src_ref=src, dst_ref=dst, send_sem=sems[0].at[s], recv_sem=sems[1].at[s],
                                        device_id=to, device_id_type=MESH)


def _gather_rider(shards):
    na = len(shards)

    def plan(ins, outs, *sems):
        x, y, c = _place()
        me = 4 * x + 2 * y + c
        locs = [pltpu.make_async_copy(ins[a], outs[a].at[me], sems[2].at[a]) for a in range(na)]
        sends, recvs = [], []
        for k in range(1, N_DEV):
            px, py, pc = _peer(k)
            for a in range(na):
                s = (k - 1) * na + a
                sends.append(_remote(ins[a], outs[a].at[me], sems, s, (px, py, pc)))
                recvs.append(_remote(ins[a], outs[a].at[4 * px + 2 * py + pc], sems, s, (px, py, pc)))
        return sends, recvs, locs

    return _Rider(shards, [_sds((N_DEV,) + a.shape, a.dtype) for a in shards], (N_DEV - 1) * na, na, plan)


def _direct_exchange_rider(blocks):
    na = len(blocks)

    def plan(ins, outs, *sems):
        x, y, c = _place()
        me = 4 * x + 2 * y + c
        locs = [pltpu.make_async_copy(ins[a].at[c, 2 * x + y], outs[a].at[me], sems[2].at[a]) for a in range(na)]
        sends, recvs = [], []
        for k in range(1, N_DEV):
            px, py, pc = _peer(k)
            for a in range(na):
                s = (k - 1) * na + a
                sends.append(_remote(ins[a].at[pc, 2 * px + py], outs[a].at[me], sems, s, (px, py, pc)))
                recvs.append(_remote(ins[a].at[pc, 2 * px + py], outs[a].at[4 * px + 2 * py + pc], sems, s, (px, py, pc)))
        return sends, recvs, locs

    return _Rider(blocks, [_sds((N_DEV,) + a.shape[2:], a.dtype) for a in blocks], (N_DEV - 1) * na, na, plan)


def _swap_rider(halves):
    na = len(halves)

    def plan(ins, outs, *sems):
        x, y, c = _place()
        cps = [_remote(ins[a].at[1 - c], outs[a], sems, a, (x, y, 1 - c)) for a in range(na)]
        return cps, cps, []

    return _Rider(halves, [_sds(a.shape[1:], a.dtype) for a in halves], na, 1, plan)


def _chip_exchange_rider(parts, small=None):
    na = len(parts)
    n_chip = N_DEV // 2

    def plan(ins, outs, *sems):
        x, y, c = _place()
        chip = 2 * x + y
        locs = [pltpu.make_async_copy(ins[a].at[chip], outs[a].at[chip], sems[2].at[a]) for a in range(na)]
        sends, recvs = [], []
        for k in range(1, n_chip):
            px, py, _ = _peer(2 * k)
            for a in range(na):
                s = (k - 1) * na + a
                sends.append(_remote(ins[a].at[2 * px + py], outs[a].at[chip], sems, s, (px, py, c)))
                recvs.append(_remote(ins[a].at[2 * px + py], outs[a].at[2 * px + py], sems, s, (px, py, c)))
        if small is not None:
            me = 2 * chip + c
            locs.append(pltpu.make_async_copy(ins[na], outs[na].at[me], sems[2].at[na]))
            for k in range(1, N_DEV):
                px, py, pc = _peer(k)
                s = (n_chip - 1) * na + k - 1
                sends.append(_remote(ins[na], outs[na].at[me], sems, s, (px, py, pc)))
                recvs.append(_remote(ins[na], outs[na].at[4 * px + 2 * py + pc], sems, s, (px, py, pc)))
        return sends, recvs, locs

    extra = [] if small is None else [small]
    shapes = [_sds(a.shape, a.dtype) for a in parts] + [_sds((N_DEV,) + s.shape, s.dtype) for s in extra]
    n_sems = (n_chip - 1) * na + (N_DEV - 1) * len(extra)
    return _Rider(list(parts) + extra, shapes, n_sems, na + len(extra), plan)


def _pair_add(halves, other, core, name):
    _, nch, r, c = halves.shape

    def body(c_ref, h_ref, o_ref, p_ref):
        p_ref[...] = (h_ref[...].astype(F32) + o_ref[...].astype(F32)).astype(BF16)

    blk = pl.BlockSpec((None, r, c), lambda j, c_ref: (j, 0, 0))
    return _pc(
        body, name,
        grid_spec=pltpu.PrefetchScalarGridSpec(
            num_scalar_prefetch=1, grid=(nch,),
            in_specs=[pl.BlockSpec((None, None, r, c), lambda j, c_ref: (c_ref[0], j, 0, 0)), blk], out_specs=blk),
        out_shape=_sds((nch, r, c), BF16),
        compiler_params=_params(),
    )(core, halves, other)


def _sum_adamw(parts, w, m, v, name, rider=None):
    nl, r, c = w.shape
    tr = 256 if r % 256 == 0 else r

    def body(*refs):
        p_refs = refs[:nl]
        w_ref, m_ref, v_ref, g_ref, d_ref, mo_ref, vo_ref = refs[nl:]
        for l in range(nl):
            @pl.when(pl.program_id(0) == l)
            def _(p_ref=p_refs[l]):
                g = p_ref[0].astype(F32)
                for j in range(1, p_ref.shape[0]):
                    g = g + p_ref[j].astype(F32)
                mn = ADAM_B1 * m_ref[...] + (1.0 - ADAM_B1) * g
                vn = ADAM_B2 * v_ref[...] + (1.0 - ADAM_B2) * (g * g)
                m_hat = mn / (1.0 - ADAM_B1 ** ADAM_STEP)
                v_hat = vn / (1.0 - ADAM_B2 ** ADAM_STEP)
                g_ref[...] = g
                d_ref[...] = -ADAM_LR * (m_hat / (jnp.sqrt(v_hat) + ADAM_EPS) + ADAM_WD * w_ref[...])
                mo_ref[...] = mn
                vo_ref[...] = vn

    def part_spec(l, k):
        return pl.BlockSpec((k, tr, c), lambda li, i: (0, jnp.where(li == l, i, 0), 0))

    row = pl.BlockSpec((None, tr, c), lambda li, i: (li, i, 0))
    return _call(
        body, name, (*parts, w, m, v), rider, grid=(nl, r // tr),
        in_specs=[part_spec(l, p.shape[0]) for l, p in enumerate(parts)] + [row, row, row],
        out_specs=[row] * 4,
        out_shape=[_sds((nl, r, c), F32)] * 4,
        compiler_params=_params(),
    )


SMALL = ("lower_bounds", "pre_norm_g", "hgrn_norm_g", "pool_w", "pool_scale", "post_norm_g", "fox_f_bias")
SMALL_LANES = 128


def _small_size(tree):
    return sum(tree[k].size for k in SMALL)


def _pack_small(tree, extra=None):
    flat = jnp.concatenate([tree[k].reshape(-1) for k in SMALL] + ([] if extra is None else [extra.reshape(1)]))
    rows = -(-(_small_size(tree) + 1) // (8 * SMALL_LANES)) * 8
    return jnp.pad(flat, (0, rows * SMALL_LANES - flat.shape[0])).reshape(rows, SMALL_LANES)


def _unpack_small(packed, like):
    out, off = {}, 0
    for k in SMALL:
        size = like[k].size
        assert off % SMALL_LANES == 0
        rows = packed[off // SMALL_LANES:-(-(off + size) // SMALL_LANES)]
        out[k] = rows.reshape(-1)[:size].reshape(like[k].shape)
        off += size
    return out


def _block_diag(pw):
    g = pw.shape[0]
    eye = jnp.eye(g, dtype=pw.dtype)
    return (eye[:, None, :, None] * pw[:, :, None, :]).reshape(g * HEAD, g * HEAD)


def _assemble_w_in(g_in, name):
    parts = g_in if isinstance(g_in, (list, tuple)) else [g_in]
    _, rows, shard = parts[0].shape
    d = rows * len(parts)
    tr = min(256, rows)
    per = rows // tr
    wide = MAIN_W + FC_PAD

    def body(*refs):
        g_refs, (wm_ref, wf_ref, wmt_ref, wft_ref, row_s) = refs[:len(parts)], refs[len(parts):]
        row_s[:, MAIN_W:] = jnp.zeros((tr, FC_PAD), F32)
        for p, g_ref in enumerate(g_refs):
            @pl.when(pl.program_id(0) // per == p)
            def _(g_ref=g_ref):
                for j in range(N_DEV):
                    row_s[:, j * shard:(j + 1) * shard] = g_ref[j].astype(F32)
        wm_ref[...] = row_s[:, :MAIN_W].astype(BF16)
        wf_ref[...] = row_s[:, MAIN_W:].astype(BF16)
        for j in range(0, MAIN_W, 512):
            wmt_ref[j:j + 512, :] = row_s[:, j:j + 512].T.astype(BF16)
        wft_ref[...] = row_s[:, MAIN_W:].T.astype(BF16)

    return _pc(
        body, name, grid=(d // tr,),
        in_specs=[pl.BlockSpec((N_DEV, tr, shard), lambda i, p=p: (0, jnp.clip(i - p * per, 0, per - 1), 0))
                  for p in range(len(parts))],
        out_specs=[pl.BlockSpec((tr, MAIN_W), lambda i: (i, 0)), pl.BlockSpec((tr, FC_PAD), lambda i: (i, 0)),
                   pl.BlockSpec((MAIN_W, tr), lambda i: (0, i)), pl.BlockSpec((FC_PAD, tr), lambda i: (0, i))],
        out_shape=[_sds((d, MAIN_W), BF16), _sds((d, FC_PAD), BF16), _sds((MAIN_W, d), BF16), _sds((FC_PAD, d), BF16)],
        scratch_shapes=[pltpu.VMEM((tr, wide), F32)],
        compiler_params=_params(),
    )(*parts)


def _w_out_parts(g_out):
    full_out = g_out.reshape(N_DEV * g_out.shape[1], g_out.shape[2])
    return full_out, full_out.T


def _layer_fwd(l, x, lbs, weights, lw, nb, rider_h=None, rider_c=None, target=None, rider_p=None):
    w_main, w_fc, _, _, w_out, _ = lw
    bias = jnp.pad(weights["fox_f_bias"][l:l + 1], ((0, 0), (0, FC_PAD - FOX_HEADS)))
    wbd = _block_diag(weights["pool_w"][l]).astype(BF16)
    (proj, fc, ht, qt, kt, c_col, c_row), rode_p = _in_proj_fwd(x, weights["pre_norm_g"][l:l + 1], w_main, w_fc, bias, nb,
                                                                f"in_proj_fwd_{l}", rider_p)
    (o_h, s0), rode_h = _hgrn_fwd(proj, lbs[l:l + 1], _block_ones(HGRN_W, BF16), nb, f"hgrn_fwd_{l}", rider_h)
    (o_c, lse), rode_c = _fox_fwd(proj, kt, c_col, c_row, nb, f"fox_fwd_{l}", rider_c)
    if w_out is None:
        lw = tuple(lw[:4]) + _w_out_parts(rode_p[0])
        w_out = lw[4]
    x_next, mixt, y = _merge_fwd(x, proj, o_h, o_c, weights["hgrn_norm_g"][l:l + 1], wbd, weights["pool_scale"][l:l + 1],
                                 w_out, weights["post_norm_g"][l:l + 1], nb, f"merge_fwd_{l}", target)
    return x_next, (x, proj, fc, ht, qt, kt, c_col, c_row, o_h, s0, o_c, lse, mixt, y, bias, wbd), lw, (rode_h, rode_c)


def _layer_bwd(l, dx, saved, lbs, weights, lw, nb, rider=None, send_w_out=None):
    _, proj, fc, ht, qt, kt, c_col, c_row, o_h, s0, o_c, lse, mixt, y, bias, wbd = saved
    w_out_t = lw[5]
    g = {}
    g["w_out"], dmix, dgp, da, dat, d_gc, delta = _merge_bwd(dx, y, weights["post_norm_g"][l:l + 1], w_out_t, proj, o_c,
                                                         mixt, nb, f"merge_bwd_{l}")
    g["post_norm_g"] = dgp[0]
    (d_a, dgh, dlb), arrived = _hgrn_bwd(dmix, proj, o_h, s0, weights["hgrn_norm_g"][l:l + 1], lbs[l:l + 1],
                                         _block_ones(HGRN_W, BF16), nb, f"hgrn_bwd_{l}",
                                         None if send_w_out is None else send_w_out([g["w_out"]]))
    if arrived is not None:
        g["w_out_received"] = arrived[0]
    g["hgrn_norm_g"], g["lbs"] = dgh[0], dlb[0]
    if callable(rider):
        rider = rider([g["w_out"]])
    (d_qc, d_kc, d_vc, dc_k, dc_q), rode = _fox_bwd(proj, qt, kt, da, dat, c_col, c_row, lse, delta, nb,
                                                    f"fox_bwd_{l}", rider)
    d_fc, dbias = _fox_decay_bwd(dc_q, dc_k, fc, bias, nb, f"fox_decay_bwd_{l}")
    g["fox_f_bias"] = dbias[0, :FOX_HEADS]
    pieces = [(d_a, C_QA), (None, C_UB), (d_qc, C_QC), (d_kc, C_KC), (d_vc, C_VC), (d_gc, C_GC), (d_fc, None)]
    g["w_in"], d_b, dwbd, dps = _w_in_grad(ht, pieces, dmix, proj, wbd, wbd.T, weights["pool_scale"][l:l + 1], nb,
                                           f"w_in_grad_{l}")
    pieces[1] = (d_b, C_UB)
    g["pool_w"] = jnp.stack([dwbd[j * HEAD:(j + 1) * HEAD, j * HEAD:(j + 1) * HEAD] for j in range(len(POOL_WINDOWS))])
    g["pool_scale"] = dps[0]
    return g, pieces, rode


def _layer_bwd_input(l, dx, pieces, saved, weights, lw, rider=None):
    (dxi, dgpre), rode = _in_proj_bwd(pieces, lw[2], lw[3], saved[0], weights["pre_norm_g"][l:l + 1], dx,
                                      f"in_proj_bwd_{l}", rider)
    return dxi, dgpre[0], rode


def kernel(x, lower_bounds, pre_norm_g, w_in, hgrn_norm_g, fox_f_bias, pool_w, pool_scale, w_out, post_norm_g, loss_target, m_lower_bounds, m_pre_norm_g, m_w_in, m_hgrn_norm_g, m_fox_f_bias, m_pool_w, m_pool_scale, m_w_out, m_post_norm_g, v_lower_bounds, v_pre_norm_g, v_w_in, v_hgrn_norm_g, v_fox_f_bias, v_pool_w, v_pool_scale, v_w_out, v_post_norm_g):
    weights = dict(lower_bounds=lower_bounds, pre_norm_g=pre_norm_g, hgrn_norm_g=hgrn_norm_g, fox_f_bias=fox_f_bias,
                   pool_w=pool_w, pool_scale=pool_scale, post_norm_g=post_norm_g)
    mom_m = dict(lower_bounds=m_lower_bounds, pre_norm_g=m_pre_norm_g, hgrn_norm_g=m_hgrn_norm_g, fox_f_bias=m_fox_f_bias,
                 pool_w=m_pool_w, pool_scale=m_pool_scale, post_norm_g=m_post_norm_g)
    mom_v = dict(lower_bounds=v_lower_bounds, pre_norm_g=v_pre_norm_g, hgrn_norm_g=v_hgrn_norm_g, fox_f_bias=v_fox_f_bias,
                 pool_w=v_pool_w, pool_scale=v_pool_scale, post_norm_g=v_post_norm_g)
    depth = w_in.shape[0]
    nb, t, d = x.shape
    n = nb * t
    core = lax.axis_index("c").astype(jnp.int32).reshape(1)
    shards = [(w_in[l].astype(BF16), w_out[l].astype(BF16)) for l in range(depth)]
    lbs = _lower_bound_table(lower_bounds, "lower_bound_table")

    (g_in,) = _gather_weights(shards[0][0])
    coming = tuple(_assemble_w_in(g_in, "assemble_w_in_0")) + (None, None)
    xl, saved, lw = x.reshape(n, d), [], []
    half = d // 2
    for l in range(depth):
        last = l + 1 == depth
        xl, sv, lw_l, (rode_h, rode_c) = _layer_fwd(
            l, xl, lbs, weights, coming, nb, None if last else _gather_rider([shards[l + 1][0][:half]]),
            None if last else _gather_rider([shards[l + 1][0][half:], shards[l + 1][1]]),
            loss_target.reshape(n, d) if last else None,
            rider_p=_gather_rider([shards[l][1]]) if coming[4] is None else None)
        saved.append(sv)
        lw.append(lw_l)
        if not last:
            coming = (tuple(_assemble_w_in([rode_h[0], rode_c[0]], f"assemble_w_in_{l + 1}"))
                      + _w_out_parts(rode_c[1]))
    dx, sq = xl
    loss_here = 0.5 * jnp.sum(sq) / d

    grads, recv_in, pending = [None] * depth, [None] * depth, None
    for l in reversed(range(depth)):
        g, pieces, rode = _layer_bwd(l, dx, saved[l], lbs, weights, lw[l], nb, pending,
                                     _direct_exchange_rider if pending is None and l == 0 else None)
        if rode is not None:
            recv_in[l + 1], grads[l + 1]["w_out_received"], g["w_out_received"] = rode
        if l > 0:
            pending = functools.partial(lambda own, more: _direct_exchange_rider(own + more), [g["w_in"], g["w_out"]])
            dx, g["pre_norm_g"], _ = _layer_bwd_input(l, dx, pieces, saved[l], weights, lw[l])
        else:
            (other,) = _run_rider(_swap_rider([g["w_in"]]), "grad_swap")
            summed = _pair_add(g["w_in"], other, core, "grad_pair_add")
            dx, g["pre_norm_g"], (recv_in[l],) = _layer_bwd_input(l, dx, pieces, saved[l], weights, lw[l],
                                                                  _chip_exchange_rider([summed]))
        grads[l] = g
    small = {k: jnp.stack([grads[l][k] for l in range(depth)]) for k in SMALL if k != "lower_bounds"}
    small["lower_bounds"] = _lower_bound_bwd(lower_bounds, jnp.stack([grads[l]["lbs"] for l in range(depth)]),
                                             "lower_bound_bwd")
    (r_small,) = _run_rider(_gather_rider([_pack_small(small, loss_here)]), "small_grads_gather")

    res_in, _ = _sum_adamw(recv_in, w_in, m_w_in, v_w_in, "adamw_w_in")
    res_out, _ = _sum_adamw([grads[l]["w_out_received"] for l in range(depth)], w_out, m_w_out, v_w_out, "adamw_w_out")
    res_small, _ = _sum_adamw([r_small], _pack_small(weights)[None], _pack_small(mom_m)[None], _pack_small(mom_v)[None],
                              "adamw_small")
    loss = res_small[0][0][_small_size(weights) // SMALL_LANES, _small_size(weights) % SMALL_LANES]

    names = ("lower_bounds", "pre_norm_g", "w_in", "hgrn_norm_g", "fox_f_bias", "pool_w", "pool_scale", "w_out", "post_norm_g")
    outs = [loss, dx.reshape(nb, t, d)]
    for i in range(4):
        full = dict(_unpack_small(res_small[i][0], weights), w_in=res_in[i], w_out=res_out[i])
        outs += [full[k] for k in names]
    return tuple(outs)
```
